```python
import jax, jax.numpy as jnp
from jax import lax
import numpy as np

D_MODEL = 1024
BATCH = 8
SEQ = 8192
DEPTH = 2

ATT_HEADS = 16
ATT_KV_HEADS = 2
ATT_HEAD_DIM = 64
WINDOW = 128
ATT_BLOCK = 128
SSD_EXPAND = 2
SSD_D_INNER = SSD_EXPAND * D_MODEL
SSD_HEAD_DIM = 64
SSD_HEADS = SSD_D_INNER // SSD_HEAD_DIM
SSD_GROUPS = 4
SSD_STATE = 128
SSD_CONV = 4
SSD_CHUNK = 128
FFN_HIDDEN = -(-8 * D_MODEL // (3 * 256)) * 256

LN_EPS = 1e-5
RMS_EPS = 1e-5
DEEPNORM_ALPHA = (2 * DEPTH) ** 0.25
DEEPNORM_BETA = (8 * DEPTH) ** -0.25

Q_DIM = ATT_HEADS * ATT_HEAD_DIM
KV_DIM = ATT_KV_HEADS * ATT_HEAD_DIM
BC_DIM = SSD_GROUPS * SSD_STATE
CONV_DIM = SSD_D_INNER + 2 * BC_DIM
IN_SIZES = (Q_DIM, KV_DIM, KV_DIM, SSD_D_INNER, SSD_D_INNER, BC_DIM, BC_DIM, SSD_HEADS, 2 * D_MODEL)
IN_DIM = sum(IN_SIZES)

kernel_name = 'hybrid_ssd_swa_sink_alibi_deepnorm'


def _split(t, sizes):
    offs = np.cumsum(sizes)[:-1].tolist()
    return jnp.split(t, offs, axis=-1)


def layer_norm(x, g, b):
    xf = x.astype(jnp.float32)
    mu = jnp.mean(xf, axis=-1, keepdims=True)
    var = jnp.mean(jnp.square(xf - mu), axis=-1, keepdims=True)
    return ((xf - mu) * lax.rsqrt(var + LN_EPS) * g + b).astype(x.dtype)


def grouped_rms_norm(y, w):
    yg = y.reshape(*y.shape[:-1], SSD_GROUPS, -1)
    yg = yg * lax.rsqrt(jnp.mean(jnp.square(yg), axis=-1, keepdims=True) + RMS_EPS)
    return yg.reshape(y.shape) * w


def causal_depthwise_conv(u, w, b):
    c = u.shape[-1]
    out = lax.conv_general_dilated(
        u, w[:, None, :].astype(u.dtype), window_strides=(1,),
        padding=[(SSD_CONV - 1, 0)], dimension_numbers=('NWC', 'WIO', 'NWC'),
        feature_group_count=c)
    return out + b


def segsum(a):
    t = a.shape[-1]
    cs = jnp.cumsum(a, axis=-1)
    diff = cs[..., :, None] - cs[..., None, :]
    mask = jnp.tril(jnp.ones((t, t), dtype=bool))
    return jnp.where(mask, diff, -jnp.inf)


def ssd_chunked_scan(xh, dt, a, b_ssm, c_ssm):
    bsz, seqlen, nh, hp = xh.shape
    ng, ns = b_ssm.shape[2], b_ssm.shape[3]
    ne = nh // ng
    nc = seqlen // SSD_CHUNK
    T = SSD_CHUNK
    X = (xh.astype(jnp.float32) * dt[..., None]).reshape(bsz, nc, T, ng, ne, hp)
    dA = jnp.moveaxis((dt * a).reshape(bsz, nc, T, ng, ne), 2, -1)
    a_cum = jnp.cumsum(dA, axis=-1)
    Bc = b_ssm.astype(jnp.float32).reshape(bsz, nc, T, ng, ns)
    Cc = c_ssm.astype(jnp.float32).reshape(bsz, nc, T, ng, ns)
    Lmat = jnp.exp(segsum(dA))
    CB = jnp.einsum('bclgn,bcsgn->bcgls', Cc, Bc)
    y_diag = jnp.einsum('bcgels,bcsgep->bclgep', Lmat * CB[:, :, :, None], X)
    decay_states = jnp.exp(a_cum[..., -1:] - a_cum)
    states = jnp.einsum('bclgn,bcgel,bclgep->bcgepn', Bc, decay_states, X)
    chunk_decay = jnp.exp(a_cum[..., -1])

    def step(h, inp):
        s_c, d_c = inp
        return h * d_c[..., None, None] + s_c, h

    h0 = jnp.zeros((bsz, ng, ne, hp, ns), jnp.float32)
    _, h_in = lax.scan(step, h0, (jnp.moveaxis(states, 1, 0), jnp.moveaxis(chunk_decay, 1, 0)))
    h_in = jnp.moveaxis(h_in, 0, 1)
    y_off = jnp.einsum('bclgn,bcgepn,bcgel->bclgep', Cc, h_in, jnp.exp(a_cum))
    return (y_diag + y_off).reshape(bsz, seqlen, nh, hp)


def alibi_slopes(n_heads):
    return jnp.exp2(-8.0 * jnp.arange(1, n_heads + 1, dtype=jnp.float32) / n_heads)


def sliding_window_sink_attention(q, k, v, sinks):
    bsz, seqlen, nh, hd = q.shape
    nkv = k.shape[2]
    ng = nh // nkv
    nb = seqlen // ATT_BLOCK
    qb = (q * (hd ** -0.5)).reshape(bsz, nb, ATT_BLOCK, nkv, ng, hd)
    pad = ((0, 0), (ATT_BLOCK, 0), (0, 0), (0, 0))
    kb = jnp.pad(k, pad).reshape(bsz, nb + 1, ATT_BLOCK, nkv, hd)
    vb = jnp.pad(v, pad).reshape(bsz, nb + 1, ATT_BLOCK, nkv, hd)
    k_band = jnp.concatenate([kb[:, :-1], kb[:, 1:]], axis=2)
    v_band = jnp.concatenate([vb[:, :-1], vb[:, 1:]], axis=2)
    scores = jnp.einsum('bnqkgd,bnskd->bnkgqs', qb, k_band).astype(jnp.float32)
    qi = jnp.arange(ATT_BLOCK)
    kj = jnp.arange(2 * ATT_BLOCK)
    rel = qi[:, None] + ATT_BLOCK - kj[None, :]
    key_pos = jnp.arange(nb)[:, None] * ATT_BLOCK - ATT_BLOCK + kj[None, :]
    valid = (rel >= 0)[None] & (rel < WINDOW)[None] & (key_pos >= 0)[:, None, :]
    slopes = alibi_slopes(nh).reshape(nkv, ng)
    bias = -slopes[:, :, None, None] * rel.astype(jnp.float32)
    scores = jnp.where(valid[None, :, None, None], scores + bias, -jnp.inf)
    sink = sinks.astype(jnp.float32).reshape(1, 1, nkv, ng, 1, 1)
    m = jnp.maximum(jnp.max(scores, axis=-1, keepdims=True), sink)
    p = jnp.exp(scores - m)
    p = p / (jnp.sum(p, axis=-1, keepdims=True) + jnp.exp(sink - m))
    out = jnp.einsum('bnkgqs,bnskd->bnqkgd', p.astype(v.dtype), v_band)
    return out.reshape(bsz, seqlen, nh * hd)


def token_mixer(h, w_in, conv_w, conv_b, dt_bias, a_log, d_skip, ssd_norm_w, att_sinks,
                w_ssd_out, w_att_out, w_mix_out):
    bsz, seqlen, _ = h.shape
    proj = h @ w_in
    q, k, v, z, xs, b_ssm, c_ssm, dt_raw, gate_logits = _split(proj, IN_SIZES)
    xbc = jax.nn.silu(causal_depthwise_conv(jnp.concatenate([xs, b_ssm, c_ssm], axis=-1), conv_w, conv_b))
    xs, b_ssm, c_ssm = _split(xbc, (SSD_D_INNER, BC_DIM, BC_DIM))
    xh = xs.reshape(bsz, seqlen, SSD_HEADS, SSD_HEAD_DIM)
    dt = jax.nn.softplus(dt_raw.astype(jnp.float32) + dt_bias)
    a = -jnp.exp(a_log.astype(jnp.float32))
    y = ssd_chunked_scan(xh, dt, a,
                         b_ssm.reshape(bsz, seqlen, SSD_GROUPS, SSD_STATE),
                         c_ssm.reshape(bsz, seqlen, SSD_GROUPS, SSD_STATE))
    y = y + d_skip[:, None] * xh
    y = y.reshape(bsz, seqlen, SSD_D_INNER) * jax.nn.silu(z.astype(jnp.float32))
    y_a = grouped_rms_norm(y, ssd_norm_w).astype(h.dtype) @ w_ssd_out
    att = sliding_window_sink_attention(
        q.reshape(bsz, seqlen, ATT_HEADS, ATT_HEAD_DIM),
        k.reshape(bsz, seqlen, ATT_KV_HEADS, ATT_HEAD_DIM),
        v.reshape(bsz, seqlen, ATT_KV_HEADS, ATT_HEAD_DIM), att_sinks)
    y_b = att @ w_att_out
    g_a, g_b = jnp.split(jax.nn.sigmoid(gate_logits), 2, axis=-1)
    return (g_a * y_a + g_b * y_b) @ w_mix_out


def swiglu_ffn(h, w_gate, w_up, w_down):
    return (jax.nn.silu(h @ w_gate) * (h @ w_up)) @ w_down


def _fwd_setup_inputs(seed: int = 0) -> dict:
    key = jax.random.key(seed)
    ks = jax.random.split(key, 24)
    f32 = jnp.float32

    def nrm(k, shape, scale):
        return jax.random.normal(k, shape, f32) * scale

    x = nrm(ks[0], (BATCH, SEQ, D_MODEL), 1.0)
    ln_in_g = 1.0 + nrm(ks[1], (D_MODEL,), 0.02)
    ln_in_b = nrm(ks[2], (D_MODEL,), 0.02)
    col_scale = jnp.concatenate([
        jnp.ones((Q_DIM + KV_DIM,), f32), jnp.full((KV_DIM,), DEEPNORM_BETA, f32),
        jnp.ones((SSD_D_INNER,), f32), jnp.full((SSD_D_INNER,), DEEPNORM_BETA, f32),
        jnp.ones((2 * BC_DIM + SSD_HEADS + 2 * D_MODEL,), f32)])
    w_in = nrm(ks[3], (DEPTH, D_MODEL, IN_DIM), D_MODEL ** -0.5) * col_scale
    conv_w = nrm(ks[4], (DEPTH, SSD_CONV, CONV_DIM), SSD_CONV ** -0.5)
    conv_b = nrm(ks[5], (DEPTH, CONV_DIM), 0.01)
    dt0 = jnp.exp(jax.random.uniform(ks[6], (DEPTH, SSD_HEADS), f32)
                  * (jnp.log(0.1) - jnp.log(0.001)) + jnp.log(0.001))
    dt_bias = dt0 + jnp.log(-jnp.expm1(-dt0))
    a_log = jnp.log(jax.random.uniform(ks[7], (DEPTH, SSD_HEADS), f32, 1.0, 16.0))
    d_skip = 1.0 + nrm(ks[8], (DEPTH, SSD_HEADS), 0.1)
    ssd_norm_w = 1.0 + nrm(ks[9], (DEPTH, SSD_D_INNER), 0.02)
    att_sinks = nrm(ks[10], (DEPTH, ATT_HEADS), 0.5)
    w_ssd_out = nrm(ks[11], (DEPTH, SSD_D_INNER, D_MODEL), SSD_D_INNER ** -0.5 * DEEPNORM_BETA)
    w_att_out = nrm(ks[12], (DEPTH, Q_DIM, D_MODEL), Q_DIM ** -0.5 * DEEPNORM_BETA)
    w_mix_out = nrm(ks[13], (DEPTH, D_MODEL, D_MODEL), D_MODEL ** -0.5 * DEEPNORM_BETA)
    ln_mix_g = 1.0 + nrm(ks[14], (DEPTH, D_MODEL), 0.02)
    ln_mix_b = nrm(ks[15], (DEPTH, D_MODEL), 0.02)
    w_ffn_gate = nrm(ks[16], (DEPTH, D_MODEL, FFN_HIDDEN), D_MODEL ** -0.5 * DEEPNORM_BETA)
    w_ffn_up = nrm(ks[17], (DEPTH, D_MODEL, FFN_HIDDEN), D_MODEL ** -0.5 * DEEPNORM_BETA)
    w_ffn_down = nrm(ks[18], (DEPTH, FFN_HIDDEN, D_MODEL), FFN_HIDDEN ** -0.5 * DEEPNORM_BETA)
    ln_ffn_g = 1.0 + nrm(ks[19], (DEPTH, D_MODEL), 0.02)
    ln_ffn_b = nrm(ks[20], (DEPTH, D_MODEL), 0.02)
    return {'x': x, 'ln_in_g': ln_in_g, 'ln_in_b': ln_in_b, 'w_in': w_in,
            'conv_w': conv_w, 'conv_b': conv_b, 'dt_bias': dt_bias, 'a_log': a_log,
            'd_skip': d_skip, 'ssd_norm_w': ssd_norm_w, 'att_sinks': att_sinks,
            'w_ssd_out': w_ssd_out, 'w_att_out': w_att_out, 'w_mix_out': w_mix_out,
            'ln_mix_g': ln_mix_g, 'ln_mix_b': ln_mix_b, 'w_ffn_gate': w_ffn_gate,
            'w_ffn_up': w_ffn_up, 'w_ffn_down': w_ffn_down,
            'ln_ffn_g': ln_ffn_g, 'ln_ffn_b': ln_ffn_b}


def _fwd_reference(x, ln_in_g, ln_in_b, w_in, conv_w, conv_b, dt_bias, a_log, d_skip, ssd_norm_w,
              att_sinks, w_ssd_out, w_att_out, w_mix_out, ln_mix_g, ln_mix_b,
              w_ffn_gate, w_ffn_up, w_ffn_down, ln_ffn_g, ln_ffn_b):
    h = layer_norm(x, ln_in_g, ln_in_b)
    for l in range(DEPTH):
        mix = token_mixer(h, w_in[l], conv_w[l], conv_b[l], dt_bias[l], a_log[l], d_skip[l],
                          ssd_norm_w[l], att_sinks[l], w_ssd_out[l], w_att_out[l], w_mix_out[l])
        h = layer_norm(DEEPNORM_ALPHA * h + mix, ln_mix_g[l], ln_mix_b[l])
        ffn = swiglu_ffn(h, w_ffn_gate[l], w_ffn_up[l], w_ffn_down[l])
        h = layer_norm(DEEPNORM_ALPHA * h + ffn, ln_ffn_g[l], ln_ffn_b[l])
    return h


import jax as _jax
import jax.numpy as _jnp

TWIN_FORMAT = 'train_step'
FWD_PARAMS = ['x', 'ln_in_g', 'ln_in_b', 'w_in', 'conv_w', 'conv_b', 'dt_bias', 'a_log', 'd_skip', 'ssd_norm_w', 'att_sinks', 'w_ssd_out', 'w_att_out', 'w_mix_out', 'ln_mix_g', 'ln_mix_b', 'w_ffn_gate', 'w_ffn_up', 'w_ffn_down', 'ln_ffn_g', 'ln_ffn_b']
TWIN_WEIGHTS = ['ln_in_g', 'ln_in_b', 'w_in', 'conv_w', 'conv_b', 'dt_bias', 'a_log', 'd_skip', 'ssd_norm_w', 'att_sinks', 'w_ssd_out', 'w_att_out', 'w_mix_out', 'ln_mix_g', 'ln_mix_b', 'w_ffn_gate', 'w_ffn_up', 'w_ffn_down', 'ln_ffn_g', 'ln_ffn_b']
TWIN_DIFF_INPUT = 'x'
TWIN_INPUTS = ['x', 'ln_in_g', 'ln_in_b', 'w_in', 'conv_w', 'conv_b', 'dt_bias', 'a_log', 'd_skip', 'ssd_norm_w', 'att_sinks', 'w_ssd_out', 'w_att_out', 'w_mix_out', 'ln_mix_g', 'ln_mix_b', 'w_ffn_gate', 'w_ffn_up', 'w_ffn_down', 'ln_ffn_g', 'ln_ffn_b', 'loss_target', 'm_ln_in_g', 'm_ln_in_b', 'm_w_in', 'm_conv_w', 'm_conv_b', 'm_dt_bias', 'm_a_log', 'm_d_skip', 'm_ssd_norm_w', 'm_att_sinks', 'm_w_ssd_out', 'm_w_att_out', 'm_w_mix_out', 'm_ln_mix_g', 'm_ln_mix_b', 'm_w_ffn_gate', 'm_w_ffn_up', 'm_w_ffn_down', 'm_ln_ffn_g', 'm_ln_ffn_b', 'v_ln_in_g', 'v_ln_in_b', 'v_w_in', 'v_conv_w', 'v_conv_b', 'v_dt_bias', 'v_a_log', 'v_d_skip', 'v_ssd_norm_w', 'v_att_sinks', 'v_w_ssd_out', 'v_w_att_out', 'v_w_mix_out', 'v_ln_mix_g', 'v_ln_mix_b', 'v_w_ffn_gate', 'v_w_ffn_up', 'v_w_ffn_down', 'v_ln_ffn_g', 'v_ln_ffn_b']
TWIN_OUTPUTS = ['loss', 'grad_x', 'grad_ln_in_g', 'grad_ln_in_b', 'grad_w_in', 'grad_conv_w', 'grad_conv_b', 'grad_dt_bias', 'grad_a_log', 'grad_d_skip', 'grad_ssd_norm_w', 'grad_att_sinks', 'grad_w_ssd_out', 'grad_w_att_out', 'grad_w_mix_out', 'grad_ln_mix_g', 'grad_ln_mix_b', 'grad_w_ffn_gate', 'grad_w_ffn_up', 'grad_w_ffn_down', 'grad_ln_ffn_g', 'grad_ln_ffn_b', 'delta_ln_in_g', 'delta_ln_in_b', 'delta_w_in', 'delta_conv_w', 'delta_conv_b', 'delta_dt_bias', 'delta_a_log', 'delta_d_skip', 'delta_ssd_norm_w', 'delta_att_sinks', 'delta_w_ssd_out', 'delta_w_att_out', 'delta_w_mix_out', 'delta_ln_mix_g', 'delta_ln_mix_b', 'delta_w_ffn_gate', 'delta_w_ffn_up', 'delta_w_ffn_down', 'delta_ln_ffn_g', 'delta_ln_ffn_b', 'new_m_ln_in_g', 'new_m_ln_in_b', 'new_m_w_in', 'new_m_conv_w', 'new_m_conv_b', 'new_m_dt_bias', 'new_m_a_log', 'new_m_d_skip', 'new_m_ssd_norm_w', 'new_m_att_sinks', 'new_m_w_ssd_out', 'new_m_w_att_out', 'new_m_w_mix_out', 'new_m_ln_mix_g', 'new_m_ln_mix_b', 'new_m_w_ffn_gate', 'new_m_w_ffn_up', 'new_m_w_ffn_down', 'new_m_ln_ffn_g', 'new_m_ln_ffn_b', 'new_v_ln_in_g', 'new_v_ln_in_b', 'new_v_w_in', 'new_v_conv_w', 'new_v_conv_b', 'new_v_dt_bias', 'new_v_a_log', 'new_v_d_skip', 'new_v_ssd_norm_w', 'new_v_att_sinks', 'new_v_w_ssd_out', 'new_v_w_att_out', 'new_v_w_mix_out', 'new_v_ln_mix_g', 'new_v_ln_mix_b', 'new_v_w_ffn_gate', 'new_v_w_ffn_up', 'new_v_w_ffn_down', 'new_v_ln_ffn_g', 'new_v_ln_ffn_b']
TWIN_LEAF_KINDS = {'loss': 'loss', 'grad_x': 'grad_x', 'grad_ln_in_g': 'grad_w', 'grad_ln_in_b': 'grad_w', 'grad_w_in': 'grad_w', 'grad_conv_w': 'grad_w', 'grad_conv_b': 'grad_w', 'grad_dt_bias': 'grad_w', 'grad_a_log': 'grad_w', 'grad_d_skip': 'grad_w', 'grad_ssd_norm_w': 'grad_w', 'grad_att_sinks': 'grad_w', 'grad_w_ssd_out': 'grad_w', 'grad_w_att_out': 'grad_w', 'grad_w_mix_out': 'grad_w', 'grad_ln_mix_g': 'grad_w', 'grad_ln_mix_b': 'grad_w', 'grad_w_ffn_gate': 'grad_w', 'grad_w_ffn_up': 'grad_w', 'grad_w_ffn_down': 'grad_w', 'grad_ln_ffn_g': 'grad_w', 'grad_ln_ffn_b': 'grad_w', 'delta_ln_in_g': 'delta_w', 'delta_ln_in_b': 'delta_w', 'delta_w_in': 'delta_w', 'delta_conv_w': 'delta_w', 'delta_conv_b': 'delta_w', 'delta_dt_bias': 'delta_w', 'delta_a_log': 'delta_w', 'delta_d_skip': 'delta_w', 'delta_ssd_norm_w': 'delta_w', 'delta_att_sinks': 'delta_w', 'delta_w_ssd_out': 'delta_w', 'delta_w_att_out': 'delta_w', 'delta_w_mix_out': 'delta_w', 'delta_ln_mix_g': 'delta_w', 'delta_ln_mix_b': 'delta_w', 'delta_w_ffn_gate': 'delta_w', 'delta_w_ffn_up': 'delta_w', 'delta_w_ffn_down': 'delta_w', 'delta_ln_ffn_g': 'delta_w', 'delta_ln_ffn_b': 'delta_w', 'new_m_ln_in_g': 'new_m', 'new_m_ln_in_b': 'new_m', 'new_m_w_in': 'new_m', 'new_m_conv_w': 'new_m', 'new_m_conv_b': 'new_m', 'new_m_dt_bias': 'new_m', 'new_m_a_log': 'new_m', 'new_m_d_skip': 'new_m', 'new_m_ssd_norm_w': 'new_m', 'new_m_att_sinks': 'new_m', 'new_m_w_ssd_out': 'new_m', 'new_m_w_att_out': 'new_m', 'new_m_w_mix_out': 'new_m', 'new_m_ln_mix_g': 'new_m', 'new_m_ln_mix_b': 'new_m', 'new_m_w_ffn_gate': 'new_m', 'new_m_w_ffn_up': 'new_m', 'new_m_w_ffn_down': 'new_m', 'new_m_ln_ffn_g': 'new_m', 'new_m_ln_ffn_b': 'new_m', 'new_v_ln_in_g': 'new_v', 'new_v_ln_in_b': 'new_v', 'new_v_w_in': 'new_v', 'new_v_conv_w': 'new_v', 'new_v_conv_b': 'new_v', 'new_v_dt_bias': 'new_v', 'new_v_a_log': 'new_v', 'new_v_d_skip': 'new_v', 'new_v_ssd_norm_w': 'new_v', 'new_v_att_sinks': 'new_v', 'new_v_w_ssd_out': 'new_v', 'new_v_w_att_out': 'new_v', 'new_v_w_mix_out': 'new_v', 'new_v_ln_mix_g': 'new_v', 'new_v_ln_mix_b': 'new_v', 'new_v_w_ffn_gate': 'new_v', 'new_v_w_ffn_up': 'new_v', 'new_v_w_ffn_down': 'new_v', 'new_v_ln_ffn_g': 'new_v', 'new_v_ln_ffn_b': 'new_v'}


def _forward(args):
    return _fwd_reference(*[args[k] for k in FWD_PARAMS])


def _output_shape():
    def fwd():
        inp = _fwd_setup_inputs(0)
        return _fwd_reference(*[inp[k] for k in FWD_PARAMS])
    out = _jax.eval_shape(fwd)
    return out.shape, out.dtype

N_MICROBATCH = 1
ADAM_LR = 0.001
ADAM_B1 = 0.9
ADAM_B2 = 0.999
ADAM_EPS = 1e-08
ADAM_WD = 0.01
ADAM_STEP = 10
PER_EXAMPLE_BATCH_AXIS = {'x': 0, 'loss_target': 0}
SHARED_INPUTS = []
_WEIGHT_DTYPES = {'ln_in_g': _jnp.float32, 'ln_in_b': _jnp.float32, 'w_in': _jnp.float32, 'conv_w': _jnp.float32, 'conv_b': _jnp.float32, 'dt_bias': _jnp.float32, 'a_log': _jnp.float32, 'd_skip': _jnp.float32, 'ssd_norm_w': _jnp.float32, 'att_sinks': _jnp.float32, 'w_ssd_out': _jnp.float32, 'w_att_out': _jnp.float32, 'w_mix_out': _jnp.float32, 'ln_mix_g': _jnp.float32, 'ln_mix_b': _jnp.float32, 'w_ffn_gate': _jnp.float32, 'w_ffn_up': _jnp.float32, 'w_ffn_down': _jnp.float32, 'ln_ffn_g': _jnp.float32, 'ln_ffn_b': _jnp.float32}
MOMENT_SCALE = {'ln_in_g': 2.137010e+00, 'ln_in_b': 1.032919e+00, 'w_in': 1.980297e-02, 'conv_w': 1.527500e-02, 'conv_b': 5.715786e-02, 'dt_bias': 5.470289e-02, 'a_log': 5.039520e-02, 'd_skip': 9.607400e-02, 'ssd_norm_w': 1.742331e-02, 'att_sinks': 5.979847e-03, 'w_ssd_out': 5.048295e-02, 'w_att_out': 7.591588e-03, 'w_mix_out': 5.133494e-02, 'ln_mix_g': 2.185426e+00, 'ln_mix_b': 1.058568e+00, 'w_ffn_gate': 1.467145e-02, 'w_ffn_up': 1.444114e-02, 'w_ffn_down': 2.392340e-02, 'ln_ffn_g': 4.541940e+01, 'ln_ffn_b': 2.106434e+00}


def _to_microbatches(a, axis):
    t = _jnp.moveaxis(a, axis, 0)
    t = t.reshape((N_MICROBATCH, t.shape[0] // N_MICROBATCH) + t.shape[1:])
    return _jnp.moveaxis(t, 1, axis + 1)


def setup_inputs(seed: int = 0) -> dict:
    inp = _fwd_setup_inputs(seed)
    key = _jax.random.fold_in(_jax.random.key(seed), 7919)
    shape, _ = _output_shape()
    out = dict(inp)
    out["loss_target"] = _jax.random.normal(_jax.random.fold_in(key, 0), shape, _jnp.float32)
    for i, name in enumerate(TWIN_WEIGHTS):
        w = inp[name].astype(_jnp.float32)
        if MOMENT_SCALE is None:
            s = _jnp.sqrt(_jnp.mean(_jnp.square(w)) + 1e-30)
        else:
            s = MOMENT_SCALE[name]
        km, kv = _jax.random.split(_jax.random.fold_in(key, i + 1))
        out[name] = w
        out["m_" + name] = s * _jax.random.normal(km, w.shape, _jnp.float32)
        out["v_" + name] = (s * s) * _jax.random.uniform(kv, w.shape, _jnp.float32, 0.5, 1.5)
    if N_MICROBATCH > 1:
        for name, axis in PER_EXAMPLE_BATCH_AXIS.items():
            out[name] = _to_microbatches(out[name], axis)
    return {'x': out['x'], 'ln_in_g': out['ln_in_g'], 'ln_in_b': out['ln_in_b'], 'w_in': out['w_in'], 'conv_w': out['conv_w'], 'conv_b': out['conv_b'], 'dt_bias': out['dt_bias'], 'a_log': out['a_log'], 'd_skip': out['d_skip'], 'ssd_norm_w': out['ssd_norm_w'], 'att_sinks': out['att_sinks'], 'w_ssd_out': out['w_ssd_out'], 'w_att_out': out['w_att_out'], 'w_mix_out': out['w_mix_out'], 'ln_mix_g': out['ln_mix_g'], 'ln_mix_b': out['ln_mix_b'], 'w_ffn_gate': out['w_ffn_gate'], 'w_ffn_up': out['w_ffn_up'], 'w_ffn_down': out['w_ffn_down'], 'ln_ffn_g': out['ln_ffn_g'], 'ln_ffn_b': out['ln_ffn_b'], 'loss_target': out['loss_target'], 'm_ln_in_g': out['m_ln_in_g'], 'm_ln_in_b': out['m_ln_in_b'], 'm_w_in': out['m_w_in'], 'm_conv_w': out['m_conv_w'], 'm_conv_b': out['m_conv_b'], 'm_dt_bias': out['m_dt_bias'], 'm_a_log': out['m_a_log'], 'm_d_skip': out['m_d_skip'], 'm_ssd_norm_w': out['m_ssd_norm_w'], 'm_att_sinks': out['m_att_sinks'], 'm_w_ssd_out': out['m_w_ssd_out'], 'm_w_att_out': out['m_w_att_out'], 'm_w_mix_out': out['m_w_mix_out'], 'm_ln_mix_g': out['m_ln_mix_g'], 'm_ln_mix_b': out['m_ln_mix_b'], 'm_w_ffn_gate': out['m_w_ffn_gate'], 'm_w_ffn_up': out['m_w_ffn_up'], 'm_w_ffn_down': out['m_w_ffn_down'], 'm_ln_ffn_g': out['m_ln_ffn_g'], 'm_ln_ffn_b': out['m_ln_ffn_b'], 'v_ln_in_g': out['v_ln_in_g'], 'v_ln_in_b': out['v_ln_in_b'], 'v_w_in': out['v_w_in'], 'v_conv_w': out['v_conv_w'], 'v_conv_b': out['v_conv_b'], 'v_dt_bias': out['v_dt_bias'], 'v_a_log': out['v_a_log'], 'v_d_skip': out['v_d_skip'], 'v_ssd_norm_w': out['v_ssd_norm_w'], 'v_att_sinks': out['v_att_sinks'], 'v_w_ssd_out': out['v_w_ssd_out'], 'v_w_att_out': out['v_w_att_out'], 'v_w_mix_out': out['v_w_mix_out'], 'v_ln_mix_g': out['v_ln_mix_g'], 'v_ln_mix_b': out['v_ln_mix_b'], 'v_w_ffn_gate': out['v_w_ffn_gate'], 'v_w_ffn_up': out['v_w_ffn_up'], 'v_w_ffn_down': out['v_w_ffn_down'], 'v_ln_ffn_g': out['v_ln_ffn_g'], 'v_ln_ffn_b': out['v_ln_ffn_b']}


def _loss(weights, diff, rest, loss_target):
    with _jax.named_scope("forward"):
        args = {**rest, TWIN_DIFF_INPUT: diff, **{k: w.astype(_WEIGHT_DTYPES[k]) for k, w in weights.items()}}
        y = _forward(args)
    with _jax.named_scope("loss_head"):
        err = _jnp.square(y.astype(_jnp.float32) - loss_target)
        return 0.5 * _jnp.sum(_jnp.mean(err, axis=-1)) if err.ndim else 0.5 * err


def _adamw(w, g, m, v):
    m = ADAM_B1 * m + (1.0 - ADAM_B1) * g
    v = ADAM_B2 * v + (1.0 - ADAM_B2) * _jnp.square(g)
    m_hat = m / (1.0 - ADAM_B1 ** ADAM_STEP)
    v_hat = v / (1.0 - ADAM_B2 ** ADAM_STEP)
    delta = -ADAM_LR * (m_hat / (_jnp.sqrt(v_hat) + ADAM_EPS) + ADAM_WD * w)
    return delta, m, v


def reference(x, ln_in_g, ln_in_b, w_in, conv_w, conv_b, dt_bias, a_log, d_skip, ssd_norm_w, att_sinks, w_ssd_out, w_att_out, w_mix_out, ln_mix_g, ln_mix_b, w_ffn_gate, w_ffn_up, w_ffn_down, ln_ffn_g, ln_ffn_b, loss_target, m_ln_in_g, m_ln_in_b, m_w_in, m_conv_w, m_conv_b, m_dt_bias, m_a_log, m_d_skip, m_ssd_norm_w, m_att_sinks, m_w_ssd_out, m_w_att_out, m_w_mix_out, m_ln_mix_g, m_ln_mix_b, m_w_ffn_gate, m_w_ffn_up, m_w_ffn_down, m_ln_ffn_g, m_ln_ffn_b, v_ln_in_g, v_ln_in_b, v_w_in, v_conv_w, v_conv_b, v_dt_bias, v_a_log, v_d_skip, v_ssd_norm_w, v_att_sinks, v_w_ssd_out, v_w_att_out, v_w_mix_out, v_ln_mix_g, v_ln_mix_b, v_w_ffn_gate, v_w_ffn_up, v_w_ffn_down, v_ln_ffn_g, v_ln_ffn_b):
    given = dict(x=x, ln_in_g=ln_in_g, ln_in_b=ln_in_b, w_in=w_in, conv_w=conv_w, conv_b=conv_b, dt_bias=dt_bias, a_log=a_log, d_skip=d_skip, ssd_norm_w=ssd_norm_w, att_sinks=att_sinks, w_ssd_out=w_ssd_out, w_att_out=w_att_out, w_mix_out=w_mix_out, ln_mix_g=ln_mix_g, ln_mix_b=ln_mix_b, w_ffn_gate=w_ffn_gate, w_ffn_up=w_ffn_up, w_ffn_down=w_ffn_down, ln_ffn_g=ln_ffn_g, ln_ffn_b=ln_ffn_b, loss_target=loss_target, m_ln_in_g=m_ln_in_g, m_ln_in_b=m_ln_in_b, m_w_in=m_w_in, m_conv_w=m_conv_w, m_conv_b=m_conv_b, m_dt_bias=m_dt_bias, m_a_log=m_a_log, m_d_skip=m_d_skip, m_ssd_norm_w=m_ssd_norm_w, m_att_sinks=m_att_sinks, m_w_ssd_out=m_w_ssd_out, m_w_att_out=m_w_att_out, m_w_mix_out=m_w_mix_out, m_ln_mix_g=m_ln_mix_g, m_ln_mix_b=m_ln_mix_b, m_w_ffn_gate=m_w_ffn_gate, m_w_ffn_up=m_w_ffn_up, m_w_ffn_down=m_w_ffn_down, m_ln_ffn_g=m_ln_ffn_g, m_ln_ffn_b=m_ln_ffn_b, v_ln_in_g=v_ln_in_g, v_ln_in_b=v_ln_in_b, v_w_in=v_w_in, v_conv_w=v_conv_w, v_conv_b=v_conv_b, v_dt_bias=v_dt_bias, v_a_log=v_a_log, v_d_skip=v_d_skip, v_ssd_norm_w=v_ssd_norm_w, v_att_sinks=v_att_sinks, v_w_ssd_out=v_w_ssd_out, v_w_att_out=v_w_att_out, v_w_mix_out=v_w_mix_out, v_ln_mix_g=v_ln_mix_g, v_ln_mix_b=v_ln_mix_b, v_w_ffn_gate=v_w_ffn_gate, v_w_ffn_up=v_w_ffn_up, v_w_ffn_down=v_w_ffn_down, v_ln_ffn_g=v_ln_ffn_g, v_ln_ffn_b=v_ln_ffn_b)
    weights = {n: given[n] for n in TWIN_WEIGHTS}
    shared = {n: given[n] for n in SHARED_INPUTS}
    per_example = {n: given[n] for n in ['x']}
    grad_fn = _jax.value_and_grad(_loss, argnums=(0, 1))

    def one_microbatch(ex, loss_target):
        ex = dict(ex)
        diff = ex.pop(TWIN_DIFF_INPUT)
        return grad_fn(weights, diff, {**shared, **ex}, loss_target)

    if N_MICROBATCH == 1:
        loss, (grad_w, grad_x) = one_microbatch(per_example, given["loss_target"])
    else:
        def body(carry, xs):
            loss_sum, grad_sum = carry
            l_k, (gw_k, gx_k) = one_microbatch(xs[0], xs[1])
            with _jax.named_scope("update"):
                return (loss_sum + l_k, _jax.tree.map(_jnp.add, grad_sum, gw_k)), gx_k

        init = (_jnp.zeros((), _jnp.float32), _jax.tree.map(_jnp.zeros_like, weights))
        (loss, grad_w), grad_x = _jax.lax.scan(body, init, (per_example, given["loss_target"]))
    with _jax.named_scope("update"):
        delta_w, new_m, new_v = {}, {}, {}
        for n in TWIN_WEIGHTS:
            delta_w[n], new_m[n], new_v[n] = _adamw(weights[n], grad_w[n], given["m_" + n], given["v_" + n])
    return (loss, grad_x, *[grad_w[n] for n in TWIN_WEIGHTS], *[delta_w[n] for n in TWIN_WEIGHTS],
            *[new_m[n] for n in TWIN_WEIGHTS], *[new_v[n] for n in TWIN_WEIGHTS])
```

```python
import functools
import math

import jax
import jax.numpy as jnp
from jax import lax
from jax.experimental import pallas as pl
from jax.experimental.pallas import tpu as pltpu

F32 = jnp.float32
BF16 = jnp.bfloat16

D_MODEL = 1024
DEPTH = 2
N_DEV = 8
ATT_HEADS = 16
ATT_KV_HEADS = 2
ATT_HEAD_DIM = 64
ATT_BLOCK = 128
SSD_D_INNER = 2048
SSD_HEADS = 32
SSD_GROUPS = 4
SSD_STATE = 128
SSD_CHUNK = 128
FFN_HIDDEN = 2816
LN_EPS = 1e-5
RMS_EPS = 1e-5
ALPHA = (2 * DEPTH) ** 0.25
Q_DIM = 1024
KV_DIM = 128
BC_DIM = 512
IN_DIM = 8480
IN_SHARD = IN_DIM // N_DEV
DT_PAD = 512

ADAM_LR = 0.001
ADAM_B1 = 0.9
ADAM_B2 = 0.999
ADAM_EPS = 1e-08
ADAM_WD = 0.01
ADAM_STEP = 10

LANE = 128
VMEM_LIMIT = 48 * 1024 * 1024
PACK_W = 1024
NEG = -1e30

_NN = (((1,), (0,)), ((), ()))
_NT = (((1,), (1,)), ((), ()))
_TN = (((0,), (0,)), ((), ()))
MESH_ID = pl.DeviceIdType.MESH


def _dot(a, b, dims=_NN):
    return lax.dot_general(a, b, dims, preferred_element_type=F32)


def _dot_hi(a, b):
    return lax.dot_general(a, b, _NN, preferred_element_type=F32, precision=lax.Precision.HIGHEST)


def _sig(x):
    return 1.0 / (1.0 + jnp.exp(-x))


def _softplus(x):
    return jnp.maximum(x, 0.0) + jnp.log(1.0 + jnp.exp(-jnp.abs(x)))


def _cparams(*sem):
    return pltpu.CompilerParams(dimension_semantics=sem, vmem_limit_bytes=VMEM_LIMIT)


def _pick(n, cap):
    if n <= cap:
        return n
    best = None
    for t in range(LANE, cap + 1, LANE):
        if n % t == 0:
            best = t
    assert best is not None, (n, cap)
    return best


def _tile(n):
    if n <= 1024 or n % 1024 == 0:
        return min(n, 1024)
    return _pick(n, 1408)


def _rows(n):
    return min(512, n)


def _mm(a, b, mode, name, add=None, add_scale=1.0, out_dtype=F32):
    if mode == "nn":
        m, k = a.shape
        n = b.shape[1]
    elif mode == "nt":
        m, k = a.shape
        n = b.shape[0]
    else:
        k, m = a.shape
        n = b.shape[1]
    tm = _tile(m)
    tn = _tile(n)
    tk = _tile(k)
    nk = k // tk
    has_add = add is not None
    dims = {"nn": _NN, "nt": _NT, "tn": _TN}[mode]

    def body(*refs):
        if has_add:
            a_ref, b_ref, add_ref, o_ref, acc_ref = refs
        else:
            a_ref, b_ref, o_ref, acc_ref = refs
        kk = pl.program_id(2)

        @pl.when(kk == 0)
        def _():
            if has_add:
                acc_ref[...] = add_scale * add_ref[...].astype(F32)
            else:
                acc_ref[...] = jnp.zeros_like(acc_ref)

        acc_ref[...] += _dot(a_ref[...].astype(BF16), b_ref[...].astype(BF16), dims)

        @pl.when(kk == nk - 1)
        def _():
            o_ref[...] = acc_ref[...].astype(o_ref.dtype)

    if mode == "nn":
        a_spec = pl.BlockSpec((tm, tk), lambda i, j, kk: (i, kk))
        b_spec = pl.BlockSpec((tk, tn), lambda i, j, kk: (kk, j))
    elif mode == "nt":
        a_spec = pl.BlockSpec((tm, tk), lambda i, j, kk: (i, kk))
        b_spec = pl.BlockSpec((tn, tk), lambda i, j, kk: (j, kk))
    else:
        a_spec = pl.BlockSpec((tk, tm), lambda i, j, kk: (kk, i))
        b_spec = pl.BlockSpec((tk, tn), lambda i, j, kk: (kk, j))
    o_spec = pl.BlockSpec((tm, tn), lambda i, j, kk: (i, j))
    in_specs = [a_spec, b_spec] + ([o_spec] if has_add else [])
    args = (a, b) + ((add,) if has_add else ())
    return pl.pallas_call(
        body, name=name, grid=(m // tm, n // tn, nk),
        in_specs=in_specs, out_specs=o_spec,
        out_shape=jax.ShapeDtypeStruct((m, n), out_dtype),
        scratch_shapes=[pltpu.VMEM((tm, tn), F32)],
        compiler_params=_cparams("parallel", "parallel", "arbitrary"),
    )(*args)


def _vec_spec(width):
    return pl.BlockSpec((1, width), lambda i: (0, 0))


def _ln_fwd(a, b, gamma, beta, alpha, name):
    n_rows, dm = a.shape
    has_b = b is not None

    def body(*refs):
        if has_b:
            a_ref, b_ref, g_ref, be_ref, o_ref = refs
            u = alpha * a_ref[...] + b_ref[...]
        else:
            a_ref, g_ref, be_ref, o_ref = refs
            u = a_ref[...]
        mu = jnp.mean(u, axis=-1, keepdims=True)
        d = u - mu
        var = jnp.mean(d * d, axis=-1, keepdims=True)
        o_ref[...] = d * lax.rsqrt(var + LN_EPS) * g_ref[...] + be_ref[...]

    row = pl.BlockSpec((_rows(n_rows),dm), lambda i: (i, 0))
    in_specs = [row] + ([row] if has_b else []) + [_vec_spec(dm), _vec_spec(dm)]
    args = (a,) + ((b,) if has_b else ()) + (gamma.reshape(1, dm), beta.reshape(1, dm))
    return pl.pallas_call(
        body, name=name, grid=(n_rows // _rows(n_rows),), in_specs=in_specs, out_specs=row,
        out_shape=jax.ShapeDtypeStruct((n_rows, dm), F32),
        compiler_params=_cparams("parallel"),
    )(*args)


def _ln_bwd(a, b, gamma, dy, alpha, name):
    n_rows, dm = a.shape
    has_b = b is not None

    def body(*refs):
        if has_b:
            a_ref, b_ref, g_ref, dy_ref, du_ref, acc_ref = refs
            u = alpha * a_ref[...] + b_ref[...]
        else:
            a_ref, g_ref, dy_ref, du_ref, acc_ref = refs
            u = a_ref[...]

        @pl.when(pl.program_id(0) == 0)
        def _():
            acc_ref[...] = jnp.zeros_like(acc_ref)

        mu = jnp.mean(u, axis=-1, keepdims=True)
        d = u - mu
        var = jnp.mean(d * d, axis=-1, keepdims=True)
        rstd = lax.rsqrt(var + LN_EPS)
        xhat = d * rstd
        dyv = dy_ref[...]
        acc_ref[0:1, :] += jnp.sum(dyv * xhat, axis=0, keepdims=True)
        acc_ref[1:2, :] += jnp.sum(dyv, axis=0, keepdims=True)
        dxh = dyv * g_ref[...]
        m1 = jnp.mean(dxh, axis=-1, keepdims=True)
        m2 = jnp.mean(dxh * xhat, axis=-1, keepdims=True)
        du_ref[...] = rstd * (dxh - m1 - xhat * m2)

    row = pl.BlockSpec((_rows(n_rows),dm), lambda i: (i, 0))
    in_specs = [row] + ([row] if has_b else []) + [_vec_spec(dm), row]
    args = (a,) + ((b,) if has_b else ()) + (gamma.reshape(1, dm), dy)
    return pl.pallas_call(
        body, name=name, grid=(n_rows // _rows(n_rows),), in_specs=in_specs,
        out_specs=(row, pl.BlockSpec((8, dm), lambda i: (0, 0))),
        out_shape=(jax.ShapeDtypeStruct((n_rows, dm), F32), jax.ShapeDtypeStruct((8, dm), F32)),
        compiler_params=_cparams("arbitrary"),
    )(*args)


def _loss_fwd_bwd(y, target, name):
    n_rows, dm = y.shape

    def body(y_ref, t_ref, acc_ref, dy_ref):
        @pl.when(pl.program_id(0) == 0)
        def _():
            acc_ref[...] = jnp.zeros_like(acc_ref)

        d = y_ref[...] - t_ref[...]
        acc_ref[...] += jnp.sum(d * d)
        dy_ref[...] = d * (1.0 / dm)

    row = pl.BlockSpec((_rows(n_rows),dm), lambda i: (i, 0))
    return pl.pallas_call(
        body, name=name, grid=(n_rows // _rows(n_rows),), in_specs=[row, row],
        out_specs=(pl.BlockSpec((8, LANE), lambda i: (0, 0)), row),
        out_shape=(jax.ShapeDtypeStruct((8, LANE), F32), jax.ShapeDtypeStruct((n_rows, dm), F32)),
        compiler_params=_cparams("arbitrary"),
    )(y, target)


def _swiglu_fwd(g, u, name):
    n_rows, w = g.shape
    tw = _pick(w, 1408)

    def body(g_ref, u_ref, o_ref):
        gv = g_ref[...]
        o_ref[...] = (gv * _sig(gv) * u_ref[...]).astype(BF16)

    blk = pl.BlockSpec((_rows(n_rows),tw), lambda i, j: (i, j))
    return pl.pallas_call(
        body, name=name, grid=(n_rows // _rows(n_rows), w // tw), in_specs=[blk, blk], out_specs=blk,
        out_shape=jax.ShapeDtypeStruct((n_rows, w), BF16),
        compiler_params=_cparams("parallel", "parallel"),
    )(g, u)


def _swiglu_bwd(g, u, dact, name):
    n_rows, w = g.shape
    tw = _pick(w, 1408)

    def body(g_ref, u_ref, da_ref, dg_ref, du_ref):
        gv = g_ref[...]
        s = _sig(gv)
        da = da_ref[...]
        dg_ref[...] = (da * u_ref[...] * (s * (1.0 + gv * (1.0 - s)))).astype(BF16)
        du_ref[...] = (da * gv * s).astype(BF16)

    blk = pl.BlockSpec((_rows(n_rows),tw), lambda i, j: (i, j))
    return pl.pallas_call(
        body, name=name, grid=(n_rows // _rows(n_rows), w // tw), in_specs=[blk, blk, blk], out_specs=(blk, blk),
        out_shape=(jax.ShapeDtypeStruct((n_rows, w), BF16), jax.ShapeDtypeStruct((n_rows, w), BF16)),
        compiler_params=_cparams("parallel", "parallel"),
    )(g, u, dact)


def _merge_fwd(gl, ya, yb, name):
    n_rows, dm = ya.shape

    def body(gl_ref, ya_ref, yb_ref, o_ref):
        ga = _sig(gl_ref[:, :dm])
        gb = _sig(gl_ref[:, dm:])
        o_ref[...] = (ga * ya_ref[...] + gb * yb_ref[...]).astype(BF16)

    row = pl.BlockSpec((_rows(n_rows),dm), lambda i: (i, 0))
    row2 = pl.BlockSpec((_rows(n_rows),2 * dm), lambda i: (i, 0))
    return pl.pallas_call(
        body, name=name, grid=(n_rows // _rows(n_rows),), in_specs=[row2, row, row], out_specs=row,
        out_shape=jax.ShapeDtypeStruct((n_rows, dm), BF16),
        compiler_params=_cparams("parallel"),
    )(gl, ya, yb)


def _merge_bwd(gl, ya, yb, dmerged, name):
    n_rows, dm = ya.shape

    def body(gl_ref, ya_ref, yb_ref, dm_ref, dya_ref, dyb_ref, dgl_ref):
        ga = _sig(gl_ref[:, :dm])
        gb = _sig(gl_ref[:, dm:])
        dmv = dm_ref[...]
        dya_ref[...] = (dmv * ga).astype(BF16)
        dyb_ref[...] = (dmv * gb).astype(BF16)
        dgl_ref[:, :dm] = (dmv * ya_ref[...] * ga * (1.0 - ga)).astype(BF16)
        dgl_ref[:, dm:] = (dmv * yb_ref[...] * gb * (1.0 - gb)).astype(BF16)

    row = pl.BlockSpec((_rows(n_rows),dm), lambda i: (i, 0))
    row2 = pl.BlockSpec((_rows(n_rows),2 * dm), lambda i: (i, 0))
    return pl.pallas_call(
        body, name=name, grid=(n_rows // _rows(n_rows),), in_specs=[row2, row, row, row], out_specs=(row, row, row2),
        out_shape=(jax.ShapeDtypeStruct((n_rows, dm), BF16), jax.ShapeDtypeStruct((n_rows, dm), BF16),
                   jax.ShapeDtypeStruct((n_rows, 2 * dm), BF16)),
        compiler_params=_cparams("parallel"),
    )(gl, ya, yb, dmerged)


CONV_TAPS = 4
CONV_COLS = 512
HALO = 8


def _shift_down(cur, prev8, s, row8):
    r = pltpu.roll(cur, s, axis=0)
    top = jnp.where(row8 < s, pltpu.roll(prev8, s, axis=0), r[0:HALO])
    return jnp.concatenate([top, r[HALO:]], axis=0)


def _shift_up(cur, next8, s, row8):
    n = cur.shape[0]
    r = pltpu.roll(cur, n - s, axis=0)
    bot = jnp.where(row8 >= HALO - s, pltpu.roll(next8, HALO - s, axis=0), r[n - HALO:])
    return jnp.concatenate([r[:n - HALO], bot], axis=0)


def _conv_pre(u_ref, prev_ref, w_ref, b_ref, li):
    cur = u_ref[...]
    prev8 = jnp.where(li == 0, 0.0, prev_ref[...])
    row8 = lax.broadcasted_iota(jnp.int32, prev8.shape, 0)
    shifted = [cur] + [_shift_down(cur, prev8, s, row8) for s in range(1, CONV_TAPS)]
    acc = b_ref[...] + shifted[0] * w_ref[CONV_TAPS - 1:CONV_TAPS, :]
    for s in range(1, CONV_TAPS):
        acc = acc + shifted[s] * w_ref[CONV_TAPS - 1 - s:CONV_TAPS - s, :]
    return acc, shifted


def _conv_specs(n_rows, tl):
    cur = pl.BlockSpec((tl, CONV_COLS), lambda cj, li: (li, cj))
    prev = pl.BlockSpec((HALO, CONV_COLS), lambda cj, li: (jnp.maximum(li * (tl // HALO) - 1, 0), cj))
    nxt = pl.BlockSpec((HALO, CONV_COLS),
                       lambda cj, li: (jnp.minimum((li + 1) * (tl // HALO), n_rows // HALO - 1), cj))
    par = pl.BlockSpec((8, CONV_COLS), lambda cj, li: (0, cj))
    return cur, prev, nxt, par


def _conv_fwd(u, w8, b8, name):
    n_rows, c = u.shape
    tl = _rows(n_rows)
    cur, prev, _, par = _conv_specs(n_rows, tl)

    def body(u_ref, prev_ref, w_ref, b_ref, o_ref):
        acc, _ = _conv_pre(u_ref, prev_ref, w_ref, b_ref[0:1, :], pl.program_id(1))
        o_ref[...] = acc * _sig(acc)

    return pl.pallas_call(
        body, name=name, grid=(c // CONV_COLS, n_rows // tl), in_specs=[cur, prev, par, par], out_specs=cur,
        out_shape=jax.ShapeDtypeStruct((n_rows, c), F32),
        compiler_params=_cparams("parallel", "parallel"),
    )(u, u, w8, b8)


def _conv_bwd_pre(u, w8, b8, dout, name):
    n_rows, c = u.shape
    tl = _rows(n_rows)
    cur, prev, _, par = _conv_specs(n_rows, tl)

    def body(u_ref, prev_ref, w_ref, b_ref, do_ref, dc_ref, acc_ref):
        @pl.when(pl.program_id(1) == 0)
        def _():
            acc_ref[...] = jnp.zeros_like(acc_ref)

        acc, shifted = _conv_pre(u_ref, prev_ref, w_ref, b_ref[0:1, :], pl.program_id(1))
        sg = _sig(acc)
        dc = do_ref[...] * (sg * (1.0 + acc * (1.0 - sg)))
        dc_ref[...] = dc
        for k in range(CONV_TAPS):
            acc_ref[k:k + 1, :] += jnp.sum(dc * shifted[CONV_TAPS - 1 - k], axis=0, keepdims=True)
        acc_ref[CONV_TAPS:CONV_TAPS + 1, :] += jnp.sum(dc, axis=0, keepdims=True)

    return pl.pallas_call(
        body, name=name, grid=(c // CONV_COLS, n_rows // tl), in_specs=[cur, prev, par, par, cur],
        out_specs=(cur, par),
        out_shape=(jax.ShapeDtypeStruct((n_rows, c), F32), jax.ShapeDtypeStruct((8, c), F32)),
        compiler_params=_cparams("parallel", "arbitrary"),
    )(u, u, w8, b8, dout)


def _conv_bwd_in(dc, w8, name):
    n_rows, c = dc.shape
    tl = _rows(n_rows)
    cur, _, nxt, par = _conv_specs(n_rows, tl)
    n_l = n_rows // tl

    def body(dc_ref, next_ref, w_ref, o_ref):
        cur_v = dc_ref[...]
        next8 = jnp.where(pl.program_id(1) == n_l - 1, 0.0, next_ref[...])
        row8 = lax.broadcasted_iota(jnp.int32, next8.shape, 0)
        acc = cur_v * w_ref[CONV_TAPS - 1:CONV_TAPS, :]
        for s in range(1, CONV_TAPS):
            acc = acc + _shift_up(cur_v, next8, s, row8) * w_ref[CONV_TAPS - 1 - s:CONV_TAPS - s, :]
        o_ref[...] = acc.astype(BF16)

    return pl.pallas_call(
        body, name=name, grid=(c // CONV_COLS, n_l), in_specs=[cur, nxt, par], out_specs=cur,
        out_shape=jax.ShapeDtypeStruct((n_rows, c), BF16),
        compiler_params=_cparams("parallel", "parallel"),
    )(dc, dc, w8)


NORM_GROUP = SSD_D_INNER // SSD_GROUPS


def _gnorm_fwd(y, z, w, name):
    n_rows, c = y.shape

    def body(y_ref, z_ref, w_ref, o_ref):
        zv = z_ref[...]
        yg = y_ref[...] * (zv * _sig(zv))
        r = lax.rsqrt(jnp.mean(yg * yg, axis=-1, keepdims=True) + RMS_EPS)
        o_ref[...] = (yg * r * w_ref[...]).astype(BF16)

    blk = pl.BlockSpec((_rows(n_rows),NORM_GROUP), lambda i, j: (i, j))
    wspec = pl.BlockSpec((1, NORM_GROUP), lambda i, j: (0, j))
    return pl.pallas_call(
        body, name=name, grid=(n_rows // _rows(n_rows), c // NORM_GROUP), in_specs=[blk, blk, wspec], out_specs=blk,
        out_shape=jax.ShapeDtypeStruct((n_rows, c), BF16),
        compiler_params=_cparams("parallel", "parallel"),
    )(y, z, w.reshape(1, c))


def _gnorm_bwd(y, z, w, dyn, name):
    n_rows, c = y.shape

    def body(y_ref, z_ref, w_ref, dn_ref, dy_ref, dz_ref, acc_ref):
        @pl.when(pl.program_id(1) == 0)
        def _():
            acc_ref[...] = jnp.zeros_like(acc_ref)

        zv = z_ref[...]
        yv = y_ref[...]
        sz = _sig(zv)
        silu = zv * sz
        yg = yv * silu
        r = lax.rsqrt(jnp.mean(yg * yg, axis=-1, keepdims=True) + RMS_EPS)
        nrm = yg * r
        dn = dn_ref[...]
        acc_ref[0:1, :] += jnp.sum(dn * nrm, axis=0, keepdims=True)
        dnw = dn * w_ref[...]
        dyg = r * (dnw - nrm * jnp.mean(dnw * nrm, axis=-1, keepdims=True))
        dy_ref[...] = dyg * silu
        dz_ref[...] = (dyg * yv * (sz * (1.0 + zv * (1.0 - sz)))).astype(BF16)

    blk = pl.BlockSpec((_rows(n_rows),NORM_GROUP), lambda j, i: (i, j))
    wspec = pl.BlockSpec((1, NORM_GROUP), lambda j, i: (0, j))
    aspec = pl.BlockSpec((8, NORM_GROUP), lambda j, i: (0, j))
    return pl.pallas_call(
        body, name=name, grid=(c // NORM_GROUP, n_rows // _rows(n_rows)), in_specs=[blk, blk, wspec, blk],
        out_specs=(blk, blk, aspec),
        out_shape=(jax.ShapeDtypeStruct((n_rows, c), F32), jax.ShapeDtypeStruct((n_rows, c), BF16),
                   jax.ShapeDtypeStruct((8, c), F32)),
        compiler_params=_cparams("parallel", "arbitrary"),
    )(y, z, w.reshape(1, c), dyn)


ATT_SCALE = ATT_HEAD_DIM ** -0.5
ATT_SLOPES = [2.0 ** (-8.0 * (h + 1) / ATT_HEADS) for h in range(ATT_HEADS)]
Q_PER_KV = ATT_HEADS // ATT_KV_HEADS


def _dup_half(t, g, lo):
    tr = pltpu.roll(t, ATT_HEAD_DIM, axis=1)
    return jnp.where(lo, t, tr) if g == 0 else jnp.where(lo, tr, t)


def _att_band(kv_ref, kvp_ref, n):
    cur = kv_ref[...]
    prev = jnp.where(n == 0, 0.0, kvp_ref[...])
    lo = lax.broadcasted_iota(jnp.int32, (ATT_BLOCK, LANE), 1) < ATT_HEAD_DIM
    bands = []
    for g in range(ATT_KV_HEADS):
        kb = jnp.concatenate([_dup_half(prev[:, :LANE], g, lo), _dup_half(cur[:, :LANE], g, lo)], axis=0)
        vb = jnp.concatenate([_dup_half(prev[:, LANE:], g, lo), _dup_half(cur[:, LANE:], g, lo)], axis=0)
        bands.append((kb.astype(BF16), vb.astype(BF16)))
    return bands


def _att_probs(qh, kb, sink, h, n):
    s = _dot(qh, kb, _NT)
    qi = lax.broadcasted_iota(jnp.int32, s.shape, 0)
    kj = lax.broadcasted_iota(jnp.int32, s.shape, 1)
    rel = qi + ATT_BLOCK - kj
    valid = (rel >= 0) & (rel < ATT_BLOCK) & ((kj >= ATT_BLOCK) | (n > 0))
    s = jnp.where(valid, s - ATT_SLOPES[h] * rel.astype(F32), NEG)
    m = jnp.maximum(jnp.max(s, axis=-1, keepdims=True), sink)
    p = jnp.exp(s - m)
    es = jnp.exp(sink - m)
    inv = 1.0 / (jnp.sum(p, axis=-1, keepdims=True) + es)
    return p * inv, es * inv


def _att_fwd(q, kv, sinks8, name):
    n_rows = q.shape[0]
    nb = n_rows // ATT_BLOCK

    def body(q_ref, kv_ref, kvp_ref, s_ref, o_ref):
        n = pl.program_id(0)
        bands = _att_band(kv_ref, kvp_ref, n)
        lo = lax.broadcasted_iota(jnp.int32, (ATT_BLOCK, LANE), 1) < ATT_HEAD_DIM
        for j in range(ATT_HEADS // 2):
            qp = q_ref[:, j * LANE:(j + 1) * LANE] * ATT_SCALE
            outs = []
            for half in range(2):
                h = 2 * j + half
                kb, vb = bands[h // Q_PER_KV]
                msk = lo if half == 0 else jnp.logical_not(lo)
                qh = jnp.where(msk, qp, 0.0).astype(BF16)
                p, _ = _att_probs(qh, kb, s_ref[0:1, h:h + 1], h, n)
                outs.append(_dot(p.astype(BF16), vb))
            o_ref[:, j * LANE:(j + 1) * LANE] = jnp.where(lo, outs[0], outs[1]).astype(BF16)

    return pl.pallas_call(
        body, name=name, grid=(nb,),
        in_specs=[pl.BlockSpec((ATT_BLOCK, Q_DIM), lambda n: (n, 0)),
                  pl.BlockSpec((ATT_BLOCK, 2 * LANE), lambda n: (n, 0)),
                  pl.BlockSpec((ATT_BLOCK, 2 * LANE), lambda n: (jnp.maximum(n - 1, 0), 0)),
                  pl.BlockSpec((8, LANE), lambda n: (0, 0))],
        out_specs=pl.BlockSpec((ATT_BLOCK, Q_DIM), lambda n: (n, 0)),
        out_shape=jax.ShapeDtypeStruct((n_rows, Q_DIM), BF16),
        compiler_params=_cparams("parallel"),
    )(q, kv, kv, sinks8)


def _att_bwd(q, kv, sinks8, dout, name):
    n_rows = q.shape[0]
    nb = n_rows // ATT_BLOCK

    def body(q_ref, kv_ref, kvp_ref, s_ref, do_ref, dq_ref, dkv_ref, acc_ref, carry_ref):
        n = pl.program_id(0)

        @pl.when(n == 0)
        def _():
            acc_ref[...] = jnp.zeros_like(acc_ref)
            carry_ref[...] = jnp.zeros_like(carry_ref)

        @pl.when(n == nb)
        def _():
            dkv_ref[...] = carry_ref[...].astype(BF16)

        @pl.when(n < nb)
        def _():
            bands = _att_band(kv_ref, kvp_ref, n)
            lo = lax.broadcasted_iota(jnp.int32, (ATT_BLOCK, LANE), 1) < ATT_HEAD_DIM
            lane1 = lax.broadcasted_iota(jnp.int32, (1, LANE), 1)
            dk_acc = [jnp.zeros((2 * ATT_BLOCK, LANE), F32) for _ in range(ATT_KV_HEADS)]
            dv_acc = [jnp.zeros((2 * ATT_BLOCK, LANE), F32) for _ in range(ATT_KV_HEADS)]
            dsink = jnp.zeros((1, LANE), F32)
            for j in range(ATT_HEADS // 2):
                qp = q_ref[:, j * LANE:(j + 1) * LANE] * ATT_SCALE
                dop = do_ref[:, j * LANE:(j + 1) * LANE].astype(F32)
                dqs = []
                for half in range(2):
                    h = 2 * j + half
                    g = h // Q_PER_KV
                    kb, vb = bands[g]
                    msk = lo if half == 0 else jnp.logical_not(lo)
                    qh = jnp.where(msk, qp, 0.0).astype(BF16)
                    doh = jnp.where(msk, dop, 0.0).astype(BF16)
                    p, ps = _att_probs(qh, kb, s_ref[0:1, h:h + 1], h, n)
                    dp = _dot(doh, vb, _NT)
                    delta = jnp.sum(p * dp, axis=-1, keepdims=True)
                    ds = p * (dp - delta)
                    dsink = jnp.where(lane1 == h, -jnp.sum(ps * delta), dsink)
                    ds_b = ds.astype(BF16)
                    dqs.append(_dot(ds_b, kb) * ATT_SCALE)
                    dk_acc[g] = dk_acc[g] + _dot(ds.T.astype(BF16), qh)
                    dv_acc[g] = dv_acc[g] + _dot(p.T.astype(BF16), doh)
                dq_ref[:, j * LANE:(j + 1) * LANE] = jnp.where(lo, dqs[0], dqs[1]).astype(BF16)
            acc_ref[0:1, :] += dsink
            lo2 = lax.broadcasted_iota(jnp.int32, (2 * ATT_BLOCK, LANE), 1) < ATT_HEAD_DIM
            folded = []
            for acc in (dk_acc, dv_acc):
                t0 = acc[0] + pltpu.roll(acc[0], ATT_HEAD_DIM, axis=1)
                t1 = acc[1] + pltpu.roll(acc[1], ATT_HEAD_DIM, axis=1)
                folded.append(jnp.where(lo2, t0, t1))
            band = jnp.concatenate(folded, axis=1)
            dkv_ref[...] = (carry_ref[...] + band[:ATT_BLOCK]).astype(BF16)
            carry_ref[...] = band[ATT_BLOCK:]

    def qmap(n):
        return (jnp.minimum(n, nb - 1), 0)

    return pl.pallas_call(
        body, name=name, grid=(nb + 1,),
        in_specs=[pl.BlockSpec((ATT_BLOCK, Q_DIM), qmap),
                  pl.BlockSpec((ATT_BLOCK, 2 * LANE), qmap),
                  pl.BlockSpec((ATT_BLOCK, 2 * LANE), lambda n: (jnp.maximum(jnp.minimum(n, nb - 1) - 1, 0), 0)),
                  pl.BlockSpec((8, LANE), lambda n: (0, 0)),
                  pl.BlockSpec((ATT_BLOCK, Q_DIM), qmap)],
        out_specs=(pl.BlockSpec((ATT_BLOCK, Q_DIM), qmap),
                   pl.BlockSpec((ATT_BLOCK, 2 * LANE), lambda n: (jnp.maximum(n - 1, 0), 0)),
                   pl.BlockSpec((8, LANE), lambda n: (0, 0))),
        out_shape=(jax.ShapeDtypeStruct((n_rows, Q_DIM), BF16), jax.ShapeDtypeStruct((n_rows, 2 * LANE), BF16),
                   jax.ShapeDtypeStruct((8, LANE), F32)),
        scratch_shapes=[pltpu.VMEM((ATT_BLOCK, 2 * LANE), F32)],
        compiler_params=_cparams("arbitrary"),
    )(q, kv, kv, sinks8, dout)


HEADS_PER_GROUP = SSD_HEADS // SSD_GROUPS
PAIRS_PER_GROUP = HEADS_PER_GROUP // 2
T = SSD_CHUNK


def _ssd_scalars(dtr_ref, par_ref):
    dt = _softplus(dtr_ref[...] + par_ref[0:1, :])
    a = -jnp.exp(par_ref[1:2, :])
    ri = lax.broadcasted_iota(jnp.int32, (T, T), 0)
    ci = lax.broadcasted_iota(jnp.int32, (T, T), 1)
    tril = (ri >= ci).astype(F32)
    cs = _dot_hi(tril, dt * a)
    cst = cs.T
    return dt, a, cs, cst, ri, ci


def _lane_pick(lo, arr, k0):
    return jnp.where(lo, arr[:, k0:k0 + 1], arr[:, k0 + 1:k0 + 2])


def _ssd_fwd(xs, bm, cm, dtr, par, name):
    n_rows = xs.shape[0]
    nc = n_rows // T
    gw = PAIRS_PER_GROUP * LANE

    def body(x_ref, b_ref, c_ref, dtr_ref, par_ref, y_ref, hs_ref, h_ref):
        @pl.when(pl.program_id(1) == 0)
        def _():
            h_ref[...] = jnp.zeros_like(h_ref)

        dt, a, cs, cst, ri, ci = _ssd_scalars(dtr_ref, par_ref)
        tri = ri >= ci
        lo = lax.broadcasted_iota(jnp.int32, (T, LANE), 1) < SSD_CHUNK // 2
        ecs = jnp.exp(cs)
        dect = jnp.exp(cst[:, T - 1:T] - cst)
        etot = jnp.exp(cs[T - 1:T, :])
        bg = b_ref[...]
        cg = c_ref[...]
        bgt = bg.T
        cb = _dot(cg.astype(BF16), bg.astype(BF16), _NT)
        for j in range(PAIRS_PER_GROUP):
            xp = x_ref[:, j * LANE:(j + 1) * LANE]
            xdt = (xp * _lane_pick(lo, dt, 2 * j)).astype(BF16)
            hp = h_ref[j]
            hs_ref[0, 0, j] = hp
            hp_b = hp.astype(BF16)
            ys, ss = [], []
            for half in range(2):
                k = 2 * j + half
                lm = jnp.exp(jnp.where(tri, cs[:, k:k + 1] - cst[k:k + 1, :], NEG))
                yh = _dot((lm * cb).astype(BF16), xdt) + _dot((cg * ecs[:, k:k + 1]).astype(BF16), hp_b)
                ys.append(yh)
                ss.append(_dot((bgt * dect[k:k + 1, :]).astype(BF16), xdt))
            dsk = jnp.where(lo[0:1, :], par_ref[2:3, 2 * j:2 * j + 1], par_ref[2:3, 2 * j + 1:2 * j + 2])
            y_ref[:, j * LANE:(j + 1) * LANE] = jnp.where(lo, ys[0], ys[1]) + dsk * xp
            et = jnp.where(lo[0:1, :], etot[:, 2 * j:2 * j + 1], etot[:, 2 * j + 1:2 * j + 2])
            h_ref[j] = hp * et + jnp.where(lo, ss[0], ss[1])

    return pl.pallas_call(
        body, name=name, grid=(SSD_GROUPS, nc),
        in_specs=[pl.BlockSpec((T, gw), lambda g, c: (c, g)),
                  pl.BlockSpec((T, SSD_STATE), lambda g, c: (c, g)),
                  pl.BlockSpec((T, SSD_STATE), lambda g, c: (c, g)),
                  pl.BlockSpec((T, LANE), lambda g, c: (c, g)),
                  pl.BlockSpec((8, LANE), lambda g, c: (0, g))],
        out_specs=(pl.BlockSpec((T, gw), lambda g, c: (c, g)),
                   pl.BlockSpec((1, 1, PAIRS_PER_GROUP, SSD_STATE, LANE), lambda g, c: (g, c, 0, 0, 0))),
        out_shape=(jax.ShapeDtypeStruct((n_rows, SSD_D_INNER), F32),
                   jax.ShapeDtypeStruct((SSD_GROUPS, nc, PAIRS_PER_GROUP, SSD_STATE, LANE), F32)),
        scratch_shapes=[pltpu.VMEM((PAIRS_PER_GROUP, SSD_STATE, LANE), F32)],
        compiler_params=_cparams("parallel", "arbitrary"),
    )(xs, bm, cm, dtr, par)


def _ssd_bwd(xs, bm, cm, dtr, par, hs, dy, name):
    n_rows = xs.shape[0]
    nc = n_rows // T
    gw = PAIRS_PER_GROUP * LANE

    def body(x_ref, b_ref, c_ref, dtr_ref, par_ref, hs_ref, dy_ref,
             dx_ref, db_ref, dc_ref, ddtr_ref, acc_ref, dh_ref):
        @pl.when(pl.program_id(1) == 0)
        def _():
            dh_ref[...] = jnp.zeros_like(dh_ref)
            acc_ref[...] = jnp.zeros_like(acc_ref)

        dt, a, cs, cst, ri, ci = _ssd_scalars(dtr_ref, par_ref)
        tri = ri >= ci
        trit = ci >= ri
        lane = lax.broadcasted_iota(jnp.int32, (T, LANE), 1)
        lo = lane < SSD_CHUNK // 2
        lane1 = lane[0:1, :]
        ecs = jnp.exp(cs)
        ecst = jnp.exp(cst)
        dec = jnp.exp(cs[T - 1:T, :] - cs)
        etot = jnp.exp(cs[T - 1:T, :])
        bg = b_ref[...]
        cg = c_ref[...]
        bg_b = bg.astype(BF16)
        cg_b = cg.astype(BF16)
        cgt = cg.T
        cb = _dot(cg_b, bg_b, _NT)
        cbt = _dot(bg_b, cg_b, _NT)
        dbg = jnp.zeros((T, SSD_STATE), F32)
        dcg = jnp.zeros((T, SSD_STATE), F32)
        dcs_acc = jnp.zeros((T, LANE), F32)
        ddt_acc = jnp.zeros((T, LANE), F32)
        dsk_acc = jnp.zeros((1, LANE), F32)
        last_row = lax.broadcasted_iota(jnp.int32, (T, 1), 0) == T - 1
        for j in range(PAIRS_PER_GROUP):
            xp = x_ref[:, j * LANE:(j + 1) * LANE]
            dtl = _lane_pick(lo, dt, 2 * j)
            xdt = xp * dtl
            hp = hs_ref[0, 0, j]
            dhn = dh_ref[j]
            dyp = dy_ref[:, j * LANE:(j + 1) * LANE]
            dxdt = jnp.zeros((T, LANE), F32)
            et = jnp.where(lo[0:1, :], etot[:, 2 * j:2 * j + 1], etot[:, 2 * j + 1:2 * j + 2])
            dh_new = dhn * et
            for half in range(2):
                k = 2 * j + half
                msk = lo if half == 0 else jnp.logical_not(lo)
                xh_f = jnp.where(msk, xdt, 0.0)
                dyh_f = jnp.where(msk, dyp, 0.0)
                hh_f = jnp.where(msk, hp, 0.0)
                dhh_f = jnp.where(msk, dhn, 0.0)
                xh, dyh, hh, dhh = (v.astype(BF16) for v in (xh_f, dyh_f, hh_f, dhh_f))
                cs_col = cs[:, k:k + 1]
                cs_row = cst[k:k + 1, :]
                lm = jnp.exp(jnp.where(tri, cs_col - cs_row, NEG))
                lmt = jnp.exp(jnp.where(trit, cs_row - cs_col, NEG))
                dm = _dot(dyh, xh, _NT)
                dmt = _dot(xh, dyh, _NT)
                mm = lm * cb
                mmt = lmt * cbt
                bdh = _dot((bg * dec[:, k:k + 1]).astype(BF16), dhh)
                dxdt = dxdt + _dot(mmt.astype(BF16), dyh) + bdh
                dcg = dcg + _dot((dm * lm).astype(BF16), bg_b) + _dot(dyh, hh, _NT) * ecs[:, k:k + 1]
                dbg = dbg + _dot((dmt * lmt).astype(BF16), cg_b) + _dot(xh, dhh, _NT) * dec[:, k:k + 1]
                dh_new = dh_new + _dot((cgt * ecst[k:k + 1, :]).astype(BF16), dyh)
                yo = _dot((cg * ecs[:, k:k + 1]).astype(BF16), hh)
                e1 = jnp.sum(dm * mm, axis=-1, keepdims=True)
                e2 = jnp.sum(dmt * mmt, axis=-1, keepdims=True)
                e3 = jnp.sum(dyh_f * yo, axis=-1, keepdims=True)
                e4 = jnp.sum(xh_f * bdh, axis=-1, keepdims=True)
                tsum = jnp.sum(e4) + etot[:, k:k + 1] * jnp.sum(hh_f * dhh_f)
                dcs_h = e1 - e2 + e3 - e4 + jnp.where(last_row, tsum, 0.0)
                dcs_acc = jnp.where(lane == k, dcs_h, dcs_acc)
                dsk_acc = jnp.where(lane1 == k, jnp.sum(dyh_f * xp), dsk_acc)
            ddt_lo = jnp.sum(jnp.where(lo, dxdt * xp, 0.0), axis=-1, keepdims=True)
            ddt_hi = jnp.sum(jnp.where(lo, 0.0, dxdt * xp), axis=-1, keepdims=True)
            ddt_acc = jnp.where(lane == 2 * j, ddt_lo, jnp.where(lane == 2 * j + 1, ddt_hi, ddt_acc))
            dsk = jnp.where(lo[0:1, :], par_ref[2:3, 2 * j:2 * j + 1], par_ref[2:3, 2 * j + 1:2 * j + 2])
            dx_ref[:, j * LANE:(j + 1) * LANE] = dxdt * dtl + dsk * dyp
            dh_ref[j] = dh_new
        db_ref[...] = dbg
        dc_ref[...] = dcg
        triu = (ci >= ri).astype(F32)
        dda = _dot_hi(triu, dcs_acc)
        ddt = ddt_acc + dda * a
        ddtr = ddt * _sig(dtr_ref[...] + par_ref[0:1, :])
        ddtr_ref[...] = ddtr.astype(BF16)
        acc_ref[0:1, :] += jnp.sum(ddtr, axis=0, keepdims=True)
        acc_ref[1:2, :] += jnp.sum(dda * dt, axis=0, keepdims=True) * a
        acc_ref[2:3, :] += dsk_acc

    def rev(g, c):
        return (nc - 1 - c, g)

    return pl.pallas_call(
        body, name=name, grid=(SSD_GROUPS, nc),
        in_specs=[pl.BlockSpec((T, gw), rev),
                  pl.BlockSpec((T, SSD_STATE), rev),
                  pl.BlockSpec((T, SSD_STATE), rev),
                  pl.BlockSpec((T, LANE), rev),
                  pl.BlockSpec((8, LANE), lambda g, c: (0, g)),
                  pl.BlockSpec((1, 1, PAIRS_PER_GROUP, SSD_STATE, LANE), lambda g, c: (g, nc - 1 - c, 0, 0, 0)),
                  pl.BlockSpec((T, gw), rev)],
        out_specs=(pl.BlockSpec((T, gw), rev),
                   pl.BlockSpec((T, SSD_STATE), rev),
                   pl.BlockSpec((T, SSD_STATE), rev),
                   pl.BlockSpec((T, LANE), rev),
                   pl.BlockSpec((8, LANE), lambda g, c: (0, g))),
        out_shape=(jax.ShapeDtypeStruct((n_rows, SSD_D_INNER), F32),
                   jax.ShapeDtypeStruct((n_rows, BC_DIM), F32),
                   jax.ShapeDtypeStruct((n_rows, BC_DIM), F32),
                   jax.ShapeDtypeStruct((n_rows, DT_PAD), BF16),
                   jax.ShapeDtypeStruct((8, DT_PAD), F32)),
        scratch_shapes=[pltpu.VMEM((PAIRS_PER_GROUP, SSD_STATE, LANE), F32)],
        compiler_params=_cparams("parallel", "arbitrary"),
    )(xs, bm, cm, dtr, par, hs, dy)


def _adamw(land, w, m, v, name):
    n_slots, r, wd = land.shape
    tr = _pick_rows(r)
    bc1 = 1.0 - ADAM_B1 ** ADAM_STEP
    bc2 = 1.0 - ADAM_B2 ** ADAM_STEP

    def body(l_ref, w_ref, m_ref, v_ref, g_ref, d_ref, nm_ref, nv_ref):
        g = l_ref[0]
        for s in range(1, n_slots):
            g = g + l_ref[s]
        mn = ADAM_B1 * m_ref[...] + (1.0 - ADAM_B1) * g
        vn = ADAM_B2 * v_ref[...] + (1.0 - ADAM_B2) * (g * g)
        mh = mn / bc1
        vh = vn / bc2
        g_ref[...] = g
        nm_ref[...] = mn
        nv_ref[...] = vn
        d_ref[...] = -ADAM_LR * (mh / (jnp.sqrt(vh) + ADAM_EPS) + ADAM_WD * w_ref[...])

    blk = pl.BlockSpec((tr, wd), lambda i: (i, 0))
    lblk = pl.BlockSpec((n_slots, tr, wd), lambda i: (0, i, 0))
    shp = jax.ShapeDtypeStruct((r, wd), F32)
    return pl.pallas_call(
        body, name=name, grid=(r // tr,), in_specs=[lblk, blk, blk, blk], out_specs=(blk, blk, blk, blk),
        out_shape=(shp, shp, shp, shp),
        compiler_params=_cparams("parallel"),
    )(land, w, m, v)


def _pick_rows(r):
    for t in (128, 112, 64, 56, 32, 16, 8):
        if r % t == 0:
            return t
    raise ValueError(r)


def _mesh_pos():
    return lax.axis_index("x"), lax.axis_index("y"), lax.axis_index("c")


def _peer(pos, k):
    x, y, c = pos
    px = 1 - x if (k >> 2) & 1 else x
    py = 1 - y if (k >> 1) & 1 else y
    pc = 1 - c if k & 1 else c
    return px, py, pc


def _flat(pos):
    return 4 * pos[0] + 2 * pos[1] + pos[2]


HBM_SPEC = pl.BlockSpec(memory_space=pl.ANY)


def _all_gather(shard, name):
    r, w = shard.shape

    def body(x_ref, out_ref, send_sems, recv_sems, local_sem):
        pos = _mesh_pos()
        me = _flat(pos)
        local = pltpu.make_async_copy(x_ref, out_ref.at[me], local_sem)
        local.start()
        copies = []
        for k in range(1, N_DEV):
            cp = pltpu.make_async_remote_copy(
                src_ref=x_ref, dst_ref=out_ref.at[me], send_sem=send_sems.at[k - 1], recv_sem=recv_sems.at[k - 1],
                device_id=_peer(pos, k), device_id_type=MESH_ID)
            cp.start()
            copies.append(cp)
        for cp in copies:
            cp.wait()
        local.wait()

    return pl.pallas_call(
        body, name=name, in_specs=[HBM_SPEC], out_specs=HBM_SPEC,
        out_shape=jax.ShapeDtypeStruct((N_DEV, r, w), shard.dtype),
        scratch_shapes=[pltpu.SemaphoreType.DMA((N_DEV - 1,)), pltpu.SemaphoreType.DMA((N_DEV - 1,)),
                        pltpu.SemaphoreType.DMA],
        compiler_params=pltpu.CompilerParams(has_side_effects=True),
    )(shard)


def _scatter_exchange(chunks, name):
    _, r, w = chunks.shape

    def body(g_ref, land_ref, send_sems, recv_sems, local_sem):
        pos = _mesh_pos()
        local = pltpu.make_async_copy(g_ref.at[_flat(pos)], land_ref.at[0], local_sem)
        local.start()
        copies = []
        for k in range(1, N_DEV):
            peer = _peer(pos, k)
            cp = pltpu.make_async_remote_copy(
                src_ref=g_ref.at[_flat(peer)], dst_ref=land_ref.at[k], send_sem=send_sems.at[k - 1],
                recv_sem=recv_sems.at[k - 1], device_id=peer, device_id_type=MESH_ID)
            cp.start()
            copies.append(cp)
        for cp in copies:
            cp.wait()
        local.wait()

    return pl.pallas_call(
        body, name=name, in_specs=[HBM_SPEC], out_specs=HBM_SPEC,
        out_shape=jax.ShapeDtypeStruct(chunks.shape, chunks.dtype),
        scratch_shapes=[pltpu.SemaphoreType.DMA((N_DEV - 1,)), pltpu.SemaphoreType.DMA((N_DEV - 1,)),
                        pltpu.SemaphoreType.DMA],
        compiler_params=pltpu.CompilerParams(has_side_effects=True),
    )(chunks)


def _all_gather_small(x, name):
    r, w = x.shape

    def body(x_ref, out_ref, send_sems, recv_sems):
        pos = _mesh_pos()
        me = _flat(pos)
        copies = []
        for k in range(1, N_DEV):
            cp = pltpu.make_async_remote_copy(
                src_ref=x_ref, dst_ref=out_ref.at[me], send_sem=send_sems.at[k - 1], recv_sem=recv_sems.at[k - 1],
                device_id=_peer(pos, k), device_id_type=MESH_ID)
            cp.start()
            copies.append(cp)
        out_ref[me] = x_ref[...]
        for cp in copies:
            cp.wait()

    vmem = pl.BlockSpec(memory_space=pltpu.VMEM)
    return pl.pallas_call(
        body, name=name, in_specs=[vmem], out_specs=vmem,
        out_shape=jax.ShapeDtypeStruct((N_DEV, r, w), x.dtype),
        scratch_shapes=[pltpu.SemaphoreType.DMA((N_DEV - 1,)), pltpu.SemaphoreType.DMA((N_DEV - 1,))],
        compiler_params=pltpu.CompilerParams(has_side_effects=True),
    )(x)


BIG = ("w_in", "w_ssd_out", "w_att_out", "w_mix_out", "w_ffn_gate", "w_ffn_up", "w_ffn_down")
COL_SHARDED = {"w_in": True, "w_ssd_out": False, "w_att_out": False, "w_mix_out": False,
               "w_ffn_gate": True, "w_ffn_up": True, "w_ffn_down": False}


def _pad_rows(flat, mult):
    n = flat.shape[-1]
    rows = -(-n // PACK_W)
    rows = -(-rows // mult) * mult
    pad = rows * PACK_W - n
    cfg = [(0, 0)] * (flat.ndim - 1) + [(0, pad)]
    return jnp.pad(flat, cfg).reshape(flat.shape[:-1] + (rows, PACK_W))


def _pack_shard_bf16(ws, conv_w):
    parts = [ws[n].astype(BF16).reshape(-1) for n in BIG]
    parts.append(lax.bitcast_convert_type(conv_w, BF16).reshape(-1))
    return _pad_rows(jnp.concatenate(parts), 16)


def _unpack_gathered(buf, shard_shapes, conv_shape):
    flat = buf.reshape(N_DEV, -1)
    off = 0
    full = {}
    for n in BIG:
        shp = shard_shapes[n]
        cnt = math.prod(shp)
        t = flat[:, off:off + cnt].reshape((N_DEV,) + shp)
        off += cnt
        if COL_SHARDED[n]:
            full[n] = jnp.transpose(t, (1, 2, 0, 3)).reshape(shp[0], shp[1], N_DEV * shp[2])
        else:
            full[n] = jnp.transpose(t, (1, 0, 2, 3)).reshape(shp[0], N_DEV * shp[1], shp[2])
    cnt = math.prod(conv_shape)
    t = flat[:, off:off + 2 * cnt].reshape((N_DEV,) + conv_shape + (2,))
    t = lax.bitcast_convert_type(t, F32)
    conv_full = jnp.transpose(t, (1, 2, 0, 3)).reshape(conv_shape[0], conv_shape[1], N_DEV * conv_shape[2])
    return full, conv_full


def _pack_shard_f32(ws, conv_w):
    parts = [ws[n].reshape(-1) for n in BIG] + [conv_w.reshape(-1)]
    return _pad_rows(jnp.concatenate(parts), 8)


def _pack_full_grads(gs, gconv):
    parts = []
    for n in BIG:
        g = gs[n]
        d0, d1, d2 = g.shape
        if COL_SHARDED[n]:
            t = jnp.transpose(g.reshape(d0, d1, N_DEV, d2 // N_DEV), (2, 0, 1, 3))
        else:
            t = jnp.transpose(g.reshape(d0, N_DEV, d1 // N_DEV, d2), (1, 0, 2, 3))
        parts.append(t.reshape(N_DEV, -1))
    d0, d1, d2 = gconv.shape
    parts.append(jnp.transpose(gconv.reshape(d0, d1, N_DEV, d2 // N_DEV), (2, 0, 1, 3)).reshape(N_DEV, -1))
    return _pad_rows(jnp.concatenate(parts, axis=1), 8)


def _unpack_shard_f32(buf, shard_shapes, conv_shape):
    flat = buf.reshape(-1)
    off = 0
    out = {}
    for n in BIG:
        cnt = math.prod(shard_shapes[n])
        out[n] = flat[off:off + cnt].reshape(shard_shapes[n])
        off += cnt
    cnt = math.prod(conv_shape)
    out["conv_w"] = flat[off:off + cnt].reshape(conv_shape)
    return out


SMALL = ("ln_in_g", "ln_in_b", "conv_b", "dt_bias", "a_log", "d_skip", "ssd_norm_w", "att_sinks",
         "ln_mix_g", "ln_mix_b", "ln_ffn_g", "ln_ffn_b")


def _pack_small(vals):
    flat = jnp.concatenate([vals[n].reshape(-1) for n in SMALL])
    n = flat.shape[0]
    rows = -(-n // LANE)
    rows = -(-rows // 8) * 8
    return jnp.pad(flat, (0, rows * LANE - n)).reshape(rows, LANE)


def _unpack_small(buf, shapes):
    flat = buf.reshape(-1)
    off = 0
    out = {}
    for n in SMALL:
        cnt = math.prod(shapes[n])
        out[n] = flat[off:off + cnt].reshape(shapes[n])
        off += cnt
    return out


def _to_group_major(v):
    lead = v.shape[:-1]
    t = v.reshape(lead + (SSD_GROUPS, HEADS_PER_GROUP))
    t = jnp.pad(t, [(0, 0)] * len(lead) + [(0, 0), (0, LANE - HEADS_PER_GROUP)])
    return t.reshape(lead + (DT_PAD,))


def _from_group_major(v):
    lead = v.shape[:-1]
    return v.reshape(lead + (SSD_GROUPS, LANE))[..., :HEADS_PER_GROUP].reshape(lead + (SSD_HEADS,))


def _rows8(v):
    return jnp.pad(v, ((0, 8 - v.shape[0]), (0, 0)))


IN_OFFS = {"q": (0, 1024), "kv": (1024, 1280), "z": (1280, 3328), "xs": (3328, 5376), "b": (5376, 5888),
           "c": (5888, 6400), "dt": (6400, 6432), "gl": (6432, 8480)}
PIECES = ("q", "kv", "z", "xs", "b", "c", "dt", "gl")


def _split_w_in(w_in):
    out = {p: w_in[:, lo:hi] for p, (lo, hi) in IN_OFFS.items()}
    out["dt"] = _to_group_major(out["dt"])
    return out


def _join_dw_in(dws):
    dws = dict(dws)
    dws["dt"] = _from_group_major(dws["dt"])
    return jnp.concatenate([dws[p] for p in PIECES], axis=1)


def _layer_params(l, W, conv_full, sm):
    p = {"w_in": _split_w_in(W["w_in"][l])}
    for n in BIG[1:]:
        p[n] = W[n][l]
    cw = conv_full[l]
    cb = sm["conv_b"][l]
    segs = {"xs": (0, 2048), "b": (2048, 2560), "c": (2560, 3072)}
    p["conv_w8"] = {s: _rows8(cw[:, lo:hi]) for s, (lo, hi) in segs.items()}
    p["conv_b8"] = {s: _rows8(cb[None, lo:hi]) for s, (lo, hi) in segs.items()}
    p["ssd_par"] = _rows8(jnp.stack([_to_group_major(sm["dt_bias"][l]), _to_group_major(sm["a_log"][l]),
                                     _to_group_major(sm["d_skip"][l])]))
    p["norm_w"] = sm["ssd_norm_w"][l]
    p["sinks8"] = _rows8(jnp.pad(sm["att_sinks"][l], (0, LANE - ATT_HEADS))[None])
    for n in ("ln_mix_g", "ln_mix_b", "ln_ffn_g", "ln_ffn_b"):
        p[n] = sm[n][l]
    return p


def _layer_fwd(h0, p, l):
    tag = f"l{l}_"
    a = {"h0": h0}
    for pc in PIECES:
        a[pc] = _mm(h0, p["w_in"][pc], "nn", tag + "proj_" + pc)
    for s in ("xs", "b", "c"):
        a[s + "c"] = _conv_fwd(a[s], p["conv_w8"][s], p["conv_b8"][s], tag + "conv_" + s)
    a["y"], a["hs"] = _ssd_fwd(a["xsc"], a["bc"], a["cc"], a["dt"], p["ssd_par"], tag + "ssd_fwd")
    a["yn"] = _gnorm_fwd(a["y"], a["z"], p["norm_w"], tag + "gnorm")
    a["ya"] = _mm(a["yn"], p["w_ssd_out"], "nn", tag + "ssd_out")
    a["att"] = _att_fwd(a["q"], a["kv"], p["sinks8"], tag + "att_fwd")
    a["yb"] = _mm(a["att"], p["w_att_out"], "nn", tag + "att_out")
    a["merged"] = _merge_fwd(a["gl"], a["ya"], a["yb"], tag + "merge")
    a["mix"] = _mm(a["merged"], p["w_mix_out"], "nn", tag + "mix_out")
    a["h1"] = _ln_fwd(h0, a["mix"], p["ln_mix_g"], p["ln_mix_b"], ALPHA, tag + "ln_mix")
    a["fg"] = _mm(a["h1"], p["w_ffn_gate"], "nn", tag + "ffn_gate")
    a["fu"] = _mm(a["h1"], p["w_ffn_up"], "nn", tag + "ffn_up")
    a["act"] = _swiglu_fwd(a["fg"], a["fu"], tag + "swiglu")
    a["ffn"] = _mm(a["act"], p["w_ffn_down"], "nn", tag + "ffn_down")
    a["h2"] = _ln_fwd(a["h1"], a["ffn"], p["ln_ffn_g"], p["ln_ffn_b"], ALPHA, tag + "ln_ffn")
    return a


def _layer_bwd(a, p, dh2, l):
    tag = f"l{l}_b_"
    gw, gs = {}, {}
    du2, acc = _ln_bwd(a["h1"], a["ffn"], p["ln_ffn_g"], dh2, ALPHA, tag + "ln_ffn")
    gs["ln_ffn_g"], gs["ln_ffn_b"] = acc[0], acc[1]
    gw["w_ffn_down"] = _mm(a["act"], du2, "tn", tag + "dw_down")
    dact = _mm(du2, p["w_ffn_down"], "nt", tag + "dact")
    dfg, dfu = _swiglu_bwd(a["fg"], a["fu"], dact, tag + "swiglu")
    gw["w_ffn_gate"] = _mm(a["h1"], dfg, "tn", tag + "dw_gate")
    gw["w_ffn_up"] = _mm(a["h1"], dfu, "tn", tag + "dw_up")
    dh1 = _mm(dfg, p["w_ffn_gate"], "nt", tag + "dh1_gate", add=du2, add_scale=ALPHA)
    dh1 = _mm(dfu, p["w_ffn_up"], "nt", tag + "dh1_up", add=dh1)
    du1, acc = _ln_bwd(a["h0"], a["mix"], p["ln_mix_g"], dh1, ALPHA, tag + "ln_mix")
    gs["ln_mix_g"], gs["ln_mix_b"] = acc[0], acc[1]
    gw["w_mix_out"] = _mm(a["merged"], du1, "tn", tag + "dw_mix")
    dmerged = _mm(du1, p["w_mix_out"], "nt", tag + "dmerged")
    dya, dyb, dgl = _merge_bwd(a["gl"], a["ya"], a["yb"], dmerged, tag + "merge")
    gw["w_ssd_out"] = _mm(a["yn"], dya, "tn", tag + "dw_ssd")
    gw["w_att_out"] = _mm(a["att"], dyb, "tn", tag + "dw_att")
    dyn = _mm(dya, p["w_ssd_out"], "nt", tag + "dyn")
    datt = _mm(dyb, p["w_att_out"], "nt", tag + "datt", out_dtype=BF16)
    dq, dkv, acc = _att_bwd(a["q"], a["kv"], p["sinks8"], datt, tag + "att")
    gs["att_sinks"] = acc[0, :ATT_HEADS]
    dy, dz, acc = _gnorm_bwd(a["y"], a["z"], p["norm_w"], dyn, tag + "gnorm")
    gs["ssd_norm_w"] = acc[0]
    dxs, dbm, dcm, ddt, acc = _ssd_bwd(a["xsc"], a["bc"], a["cc"], a["dt"], p["ssd_par"], a["hs"], dy,
                                       tag + "ssd")
    gs["dt_bias"], gs["a_log"], gs["d_skip"] = (_from_group_major(acc[i]) for i in range(3))
    dpieces = {"q": dq, "kv": dkv, "z": dz, "dt": ddt, "gl": dgl}
    dconv_w, dconv_b = [], []
    for s, dout in (("xs", dxs), ("b", dbm), ("c", dcm)):
        dc, acc = _conv_bwd_pre(a[s], p["conv_w8"][s], p["conv_b8"][s], dout, tag + "conv_pre_" + s)
        dconv_w.append(acc[:CONV_TAPS])
        dconv_b.append(acc[CONV_TAPS])
        dpieces[s] = _conv_bwd_in(dc, p["conv_w8"][s], tag + "conv_in_" + s)
    gconv = jnp.concatenate(dconv_w, axis=1)
    gs["conv_b"] = jnp.concatenate(dconv_b)
    dws = {}
    dh0 = du1
    scale = ALPHA
    for pc in PIECES:
        dws[pc] = _mm(a["h0"], dpieces[pc], "tn", tag + "dw_in_" + pc)
        dh0 = _mm(dpieces[pc], p["w_in"][pc], "nt", tag + "dh0_" + pc, add=dh0, add_scale=scale)
        scale = 1.0
    gw["w_in"] = _join_dw_in(dws)
    return dh0, gw, gconv, gs


def _step(x, target, w, m, v):
    x2 = x[0]
    t2 = target[0]
    shard_shapes = {n: w[n].shape for n in BIG}
    conv_shape = w["conv_w"].shape

    gathered = _all_gather(_pack_shard_bf16(w, w["conv_w"]), "weights_all_gather")
    W, conv_full = _unpack_gathered(gathered, shard_shapes, conv_shape)

    h = _ln_fwd(x2, None, w["ln_in_g"], w["ln_in_b"], 1.0, "ln_in")
    params, acts = [], []
    for l in range(DEPTH):
        p = _layer_params(l, W, conv_full, w)
        a = _layer_fwd(h, p, l)
        params.append(p)
        acts.append(a)
        h = a["h2"]

    sse, dh = _loss_fwd_bwd(h, t2, "loss")
    loss = lax.psum(0.5 / D_MODEL * sse[0, 0], ("x", "y", "c"))

    gw_layers, gconv_layers, gs_layers = [], [], []
    for l in reversed(range(DEPTH)):
        dh, gw, gconv, gs = _layer_bwd(acts[l], params[l], dh, l)
        gw_layers.insert(0, gw)
        gconv_layers.insert(0, gconv)
        gs_layers.insert(0, gs)
    grad_x2, acc = _ln_bwd(x2, None, w["ln_in_g"], dh, 1.0, "ln_in_b")

    full_g = {n: jnp.stack([gw_layers[l][n] for l in range(DEPTH)]) for n in BIG}
    chunks = _pack_full_grads(full_g, jnp.stack(gconv_layers))
    landing = _scatter_exchange(chunks, "grads_reduce_scatter")
    g_p, d_p, m_p, v_p = _adamw(landing, _pack_shard_f32(w, w["conv_w"]), _pack_shard_f32(m, m["conv_w"]),
                                _pack_shard_f32(v, v["conv_w"]), "adamw_sharded")
    big_out = [_unpack_shard_f32(t, shard_shapes, conv_shape) for t in (g_p, d_p, m_p, v_p)]

    gsm = {"ln_in_g": acc[0], "ln_in_b": acc[1]}
    for n in SMALL[2:]:
        gsm[n] = jnp.stack([gs_layers[l][n] for l in range(DEPTH)])
    small_shapes = {n: w[n].shape for n in SMALL}
    land_s = _all_gather_small(_pack_small(gsm), "small_grads_all_gather")
    g_s, d_s, m_s, v_s = _adamw(land_s, _pack_small(w), _pack_small(m), _pack_small(v), "adamw_small")
    small_out = [_unpack_small(t, small_shapes) for t in (g_s, d_s, m_s, v_s)]

    outs = [{**big_out[i], **small_out[i]} for i in range(4)]
    return loss, grad_x2[None], outs


WEIGHT_NAMES = ("ln_in_g", "ln_in_b", "w_in", "conv_w", "conv_b", "dt_bias", "a_log", "d_skip", "ssd_norm_w",
                "att_sinks", "w_ssd_out", "w_att_out", "w_mix_out", "ln_mix_g", "ln_mix_b", "w_ffn_gate",
                "w_ffn_up", "w_ffn_down", "ln_ffn_g", "ln_ffn_b")


def kernel(x, ln_in_g, ln_in_b, w_in, conv_w, conv_b, dt_bias, a_log, d_skip, ssd_norm_w, att_sinks, w_ssd_out, w_att_out, w_mix_out, ln_mix_g, ln_mix_b, w_ffn_gate, w_ffn_up, w_ffn_down, ln_ffn_g, ln_ffn_b, loss_target, m_ln_in_g, m_ln_in_b, m_w_in, m_conv_w, m_conv_b, m_dt_bias, m_a_log, m_d_skip, m_ssd_norm_w, m_att_sinks, m_w_ssd_out, m_w_att_out, m_w_mix_out, m_ln_mix_g, m_ln_mix_b, m_w_ffn_gate, m_w_ffn_up, m_w_ffn_down, m_ln_ffn_g, m_ln_ffn_b, v_ln_in_g, v_ln_in_b, v_w_in, v_conv_w, v_conv_b, v_dt_bias, v_a_log, v_d_skip, v_ssd_norm_w, v_att_sinks, v_w_ssd_out, v_w_att_out, v_w_mix_out, v_ln_mix_g, v_ln_mix_b, v_w_ffn_gate, v_w_ffn_up, v_w_ffn_down, v_ln_ffn_g, v_ln_ffn_b):
    w = dict(zip(WEIGHT_NAMES, (ln_in_g, ln_in_b, w_in, conv_w, conv_b, dt_bias, a_log, d_skip, ssd_norm_w,
                                att_sinks, w_ssd_out, w_att_out, w_mix_out, ln_mix_g, ln_mix_b, w_ffn_gate,
                                w_ffn_up, w_ffn_down, ln_ffn_g, ln_ffn_b)))
    m = dict(zip(WEIGHT_NAMES, (m_ln_in_g, m_ln_in_b, m_w_in, m_conv_w, m_conv_b, m_dt_bias, m_a_log, m_d_skip,
                                m_ssd_norm_w, m_att_sinks, m_w_ssd_out, m_w_att_out, m_w_mix_out, m_ln_mix_g,
                                m_ln_mix_b, m_w_ffn_gate, m_w_ffn_up, m_w_ffn_down, m_ln_ffn_g, m_ln_ffn_b)))
    v = dict(zip(WEIGHT_NAMES, (v_ln_in_g, v_ln_in_b, v_w_in, v_conv_w, v_conv_b, v_dt_bias, v_a_log, v_d_skip,
                                v_ssd_norm_w, v_att_sinks, v_w_ssd_out, v_w_att_out, v_w_mix_out, v_ln_mix_g,
                                v_ln_mix_b, v_w_ffn_gate, v_w_ffn_up, v_w_ffn_down, v_ln_ffn_g, v_ln_ffn_b)))
    loss, grad_x, outs = _step(x, loss_target, w, m, v)
    result = [loss, grad_x]
    for o in outs:
        result.extend(o[n] for n in WEIGHT_NAMES)
    return tuple(result)
```

```python
import functools
import math

import jax
import jax.numpy as jnp
from jax import lax
from jax.experimental import pallas as pl
from jax.experimental.pallas import tpu as pltpu

F32 = jnp.float32
BF16 = jnp.bfloat16

D_MODEL = 1024
DEPTH = 2
N_DEV = 8
ATT_HEADS = 16
ATT_KV_HEADS = 2
ATT_HEAD_DIM = 64
ATT_BLOCK = 128
SSD_D_INNER = 2048
SSD_HEADS = 32
SSD_GROUPS = 4
SSD_STATE = 128
SSD_CHUNK = 128
FFN_HIDDEN = 2816
LN_EPS = 1e-5
RMS_EPS = 1e-5
ALPHA = (2 * DEPTH) ** 0.25
Q_DIM = 1024
KV_DIM = 128
BC_DIM = 512
IN_DIM = 8480
IN_SHARD = IN_DIM // N_DEV
DT_PAD = 512

ADAM_LR = 0.001
ADAM_B1 = 0.9
ADAM_B2 = 0.999
ADAM_EPS = 1e-08
ADAM_WD = 0.01
ADAM_STEP = 10

LANE = 128
VMEM_LIMIT = 48 * 1024 * 1024
PACK_W = 1024
NEG = -1e30

_NN = (((1,), (0,)), ((), ()))
_NT = (((1,), (1,)), ((), ()))
_TN = (((0,), (0,)), ((), ()))
MESH_ID = pl.DeviceIdType.MESH


def _dot(a, b, dims=_NN):
    return lax.dot_general(a, b, dims, preferred_element_type=F32)


def _dot_hi(a, b):
    return lax.dot_general(a, b, _NN, preferred_element_type=F32, precision=lax.Precision.HIGHEST)


def _sig(x):
    return 1.0 / (1.0 + jnp.exp(-x))


def _softplus(x):
    return jnp.maximum(x, 0.0) + jnp.log(1.0 + jnp.exp(-jnp.abs(x)))


def _cparams(*sem):
    return pltpu.CompilerParams(dimension_semantics=sem, vmem_limit_bytes=VMEM_LIMIT)


def _pick(n, cap):
    if n <= cap:
        return n
    best = None
    for t in range(LANE, cap + 1, LANE):
        if n % t == 0:
            best = t
    assert best is not None, (n, cap)
    return best


def _tile(n):
    if n <= 1024 or n % 1024 == 0:
        return min(n, 1024)
    return _pick(n, 1408)


def _rows(n):
    return min(512, n)


def _mm(a, b, mode, name, add=None, add_scale=1.0, out_dtype=F32):
    if mode == "nn":
        m, k = a.shape
        n = b.shape[1]
    elif mode == "nt":
        m, k = a.shape
        n = b.shape[0]
    else:
        k, m = a.shape
        n = b.shape[1]
    tm = _tile(m)
    tn = _tile(n)
    tk = _tile(k)
    nk = k // tk
    has_add = add is not None
    dims = {"nn": _NN, "nt": _NT, "tn": _TN}[mode]

    def body(*refs):
        if has_add:
            a_ref, b_ref, add_ref, o_ref, acc_ref = refs
        else:
            a_ref, b_ref, o_ref, acc_ref = refs
        kk = pl.program_id(2)

        @pl.when(kk == 0)
        def _():
            if has_add:
                acc_ref[...] = add_scale * add_ref[...].astype(F32)
            else:
                acc_ref[...] = jnp.zeros_like(acc_ref)

        acc_ref[...] += _dot(a_ref[...].astype(BF16), b_ref[...].astype(BF16), dims)

        @pl.when(kk == nk - 1)
        def _():
            o_ref[...] = acc_ref[...].astype(o_ref.dtype)

    if mode == "nn":
        a_spec = pl.BlockSpec((tm, tk), lambda i, j, kk: (i, kk))
        b_spec = pl.BlockSpec((tk, tn), lambda i, j, kk: (kk, j))
    elif mode == "nt":
        a_spec = pl.BlockSpec((tm, tk), lambda i, j, kk: (i, kk))
        b_spec = pl.BlockSpec((tn, tk), lambda i, j, kk: (j, kk))
    else:
        a_spec = pl.BlockSpec((tk, tm), lambda i, j, kk: (kk, i))
        b_spec = pl.BlockSpec((tk, tn), lambda i, j, kk: (kk, j))
    o_spec = pl.BlockSpec((tm, tn), lambda i, j, kk: (i, j))
    in_specs = [a_spec, b_spec] + ([o_spec] if has_add else [])
    args = (a, b) + ((add,) if has_add else ())
    return pl.pallas_call(
        body, name=name, grid=(m // tm, n // tn, nk),
        in_specs=in_specs, out_specs=o_spec,
        out_shape=jax.ShapeDtypeStruct((m, n), out_dtype),
        scratch_shapes=[pltpu.VMEM((tm, tn), F32)],
        compiler_params=_cparams("parallel", "parallel", "arbitrary"),
    )(*args)


def _vec_spec(width):
    return pl.BlockSpec((1, width), lambda i: (0, 0))


def _ln_fwd(a, b, gamma, beta, alpha, name):
    n_rows, dm = a.shape
    has_b = b is not None

    def body(*refs):
        if has_b:
            a_ref, b_ref, g_ref, be_ref, o_ref = refs
            u = alpha * a_ref[...] + b_ref[...]
        else:
            a_ref, g_ref, be_ref, o_ref = refs
            u = a_ref[...]
        mu = jnp.mean(u, axis=-1, keepdims=True)
        d = u - mu
        var = jnp.mean(d * d, axis=-1, keepdims=True)
        o_ref[...] = d * lax.rsqrt(var + LN_EPS) * g_ref[...] + be_ref[...]

    row = pl.BlockSpec((_rows(n_rows),dm), lambda i: (i, 0))
    in_specs = [row] + ([row] if has_b else []) + [_vec_spec(dm), _vec_spec(dm)]
    args = (a,) + ((b,) if has_b else ()) + (gamma.reshape(1, dm), beta.reshape(1, dm))
    return pl.pallas_call(
        body, name=name, grid=(n_rows // _rows(n_rows),), in_specs=in_specs, out_specs=row,
        out_shape=jax.ShapeDtypeStruct((n_rows, dm), F32),
        compiler_params=_cparams("parallel"),
    )(*args)


def _ln_bwd(a, b, gamma, dy, alpha, name):
    n_rows, dm = a.shape
    has_b = b is not None

    def body(*refs):
        if has_b:
            a_ref, b_ref, g_ref, dy_ref, du_ref, acc_ref = refs
            u = alpha * a_ref[...] + b_ref[...]
        else:
            a_ref, g_ref, dy_ref, du_ref, acc_ref = refs
            u = a_ref[...]

        @pl.when(pl.program_id(0) == 0)
        def _():
            acc_ref[...] = jnp.zeros_like(acc_ref)

        mu = jnp.mean(u, axis=-1, keepdims=True)
        d = u - mu
        var = jnp.mean(d * d, axis=-1, keepdims=True)
        rstd = lax.rsqrt(var + LN_EPS)
        xhat = d * rstd
        dyv = dy_ref[...]
        acc_ref[0:1, :] += jnp.sum(dyv * xhat, axis=0, keepdims=True)
        acc_ref[1:2, :] += jnp.sum(dyv, axis=0, keepdims=True)
        dxh = dyv * g_ref[...]
        m1 = jnp.mean(dxh, axis=-1, keepdims=True)
        m2 = jnp.mean(dxh * xhat, axis=-1, keepdims=True)
        du_ref[...] = rstd * (dxh - m1 - xhat * m2)

    row = pl.BlockSpec((_rows(n_rows),dm), lambda i: (i, 0))
    in_specs = [row] + ([row] if has_b else []) + [_vec_spec(dm), row]
    args = (a,) + ((b,) if has_b else ()) + (gamma.reshape(1, dm), dy)
    return pl.pallas_call(
        body, name=name, grid=(n_rows // _rows(n_rows),), in_specs=in_specs,
        out_specs=(row, pl.BlockSpec((8, dm), lambda i: (0, 0))),
        out_shape=(jax.ShapeDtypeStruct((n_rows, dm), F32), jax.ShapeDtypeStruct((8, dm), F32)),
        compiler_params=_cparams("arbitrary"),
    )(*args)


def _loss_fwd_bwd(y, target, name):
    n_rows, dm = y.shape

    def body(y_ref, t_ref, acc_ref, dy_ref):
        @pl.when(pl.program_id(0) == 0)
        def _():
            acc_ref[...] = jnp.zeros_like(acc_ref)

        d = y_ref[...] - t_ref[...]
        acc_ref[...] += jnp.sum(d * d)
        dy_ref[...] = d * (1.0 / dm)

    row = pl.BlockSpec((_rows(n_rows),dm), lambda i: (i, 0))
    return pl.pallas_call(
        body, name=name, grid=(n_rows // _rows(n_rows),), in_specs=[row, row],
        out_specs=(pl.BlockSpec((8, LANE), lambda i: (0, 0)), row),
        out_shape=(jax.ShapeDtypeStruct((8, LANE), F32), jax.ShapeDtypeStruct((n_rows, dm), F32)),
        compiler_params=_cparams("arbitrary"),
    )(y, target)


def _swiglu_fwd(g, u, name):
    n_rows, w = g.shape
    tw = _pick(w, 1408)

    def body(g_ref, u_ref, o_ref):
        gv = g_ref[...]
        o_ref[...] = (gv * _sig(gv) * u_ref[...]).astype(BF16)

    blk = pl.BlockSpec((_rows(n_rows),tw), lambda i, j: (i, j))
    return pl.pallas_call(
        body, name=name, grid=(n_rows // _rows(n_rows), w // tw), in_specs=[blk, blk], out_specs=blk,
        out_shape=jax.ShapeDtypeStruct((n_rows, w), BF16),
        compiler_params=_cparams("parallel", "parallel"),
    )(g, u)


def _swiglu_bwd(g, u, dact, name):
    n_rows, w = g.shape
    tw = _pick(w, 1408)

    def body(g_ref, u_ref, da_ref, dg_ref, du_ref):
        gv = g_ref[...]
        s = _sig(gv)
        da = da_ref[...]
        dg_ref[...] = (da * u_ref[...] * (s * (1.0 + gv * (1.0 - s)))).astype(BF16)
        du_ref[...] = (da * gv * s).astype(BF16)

    blk = pl.BlockSpec((_rows(n_rows),tw), lambda i, j: (i, j))
    return pl.pallas_call(
        body, name=name, grid=(n_rows // _rows(n_rows), w // tw), in_specs=[blk, blk, blk], out_specs=(blk, blk),
        out_shape=(jax.ShapeDtypeStruct((n_rows, w), BF16), jax.ShapeDtypeStruct((n_rows, w), BF16)),
        compiler_params=_cparams("parallel", "parallel"),
    )(g, u, dact)


def _merge_fwd(gl, ya, yb, name):
    n_rows, dm = ya.shape

    def body(gl_ref, ya_ref, yb_ref, o_ref):
        ga = _sig(gl_ref[:, :dm])
        gb = _sig(gl_ref[:, dm:])
        o_ref[...] = (ga * ya_ref[...] + gb * yb_ref[...]).astype(BF16)

    row = pl.BlockSpec((_rows(n_rows),dm), lambda i: (i, 0))
    row2 = pl.BlockSpec((_rows(n_rows),2 * dm), lambda i: (i, 0))
    return pl.pallas_call(
        body, name=name, grid=(n_rows // _rows(n_rows),), in_specs=[row2, row, row], out_specs=row,
        out_shape=jax.ShapeDtypeStruct((n_rows, dm), BF16),
        compiler_params=_cparams("parallel"),
    )(gl, ya, yb)


def _merge_bwd(gl, ya, yb, dmerged, name):
    n_rows, dm = ya.shape

    def body(gl_ref, ya_ref, yb_ref, dm_ref, dya_ref, dyb_ref, dgl_ref):
        ga = _sig(gl_ref[:, :dm])
        gb = _sig(gl_ref[:, dm:])
        dmv = dm_ref[...]
        dya_ref[...] = (dmv * ga).astype(BF16)
        dyb_ref[...] = (dmv * gb).astype(BF16)
        dgl_ref[:, :dm] = (dmv * ya_ref[...] * ga * (1.0 - ga)).astype(BF16)
        dgl_ref[:, dm:] = (dmv * yb_ref[...] * gb * (1.0 - gb)).astype(BF16)

    row = pl.BlockSpec((_rows(n_rows),dm), lambda i: (i, 0))
    row2 = pl.BlockSpec((_rows(n_rows),2 * dm), lambda i: (i, 0))
    return pl.pallas_call(
        body, name=name, grid=(n_rows // _rows(n_rows),), in_specs=[row2, row, row, row], out_specs=(row, row, row2),
        out_shape=(jax.ShapeDtypeStruct((n_rows, dm), BF16), jax.ShapeDtypeStruct((n_rows, dm), BF16),
                   jax.ShapeDtypeStruct((n_rows, 2 * dm), BF16)),
        compiler_params=_cparams("parallel"),
    )(gl, ya, yb, dmerged)


CONV_TAPS = 4
CONV_COLS = 512
HALO = 8


def _shift_down(cur, prev8, s, row8):
    r = pltpu.roll(cur, s, axis=0)
    top = jnp.where(row8 < s, pltpu.roll(prev8, s, axis=0), r[0:HALO])
    return jnp.concatenate([top, r[HALO:]], axis=0)


def _shift_up(cur, next8, s, row8):
    n = cur.shape[0]
    r = pltpu.roll(cur, n - s, axis=0)
    bot = jnp.where(row8 >= HALO - s, pltpu.roll(next8, HALO - s, axis=0), r[n - HALO:])
    return jnp.concatenate([r[:n - HALO], bot], axis=0)


def _conv_pre(u_ref, prev_ref, w_ref, b_ref, li):
    cur = u_ref[...]
    prev8 = jnp.where(li == 0, 0.0, prev_ref[...])
    row8 = lax.broadcasted_iota(jnp.int32, prev8.shape, 0)
    shifted = [cur] + [_shift_down(cur, prev8, s, row8) for s in range(1, CONV_TAPS)]
    acc = b_ref[...] + shifted[0] * w_ref[CONV_TAPS - 1:CONV_TAPS, :]
    for s in range(1, CONV_TAPS):
        acc = acc + shifted[s] * w_ref[CONV_TAPS - 1 - s:CONV_TAPS - s, :]
    return acc, shifted


def _conv_specs(n_rows, tl):
    cur = pl.BlockSpec((tl, CONV_COLS), lambda cj, li: (li, cj))
    prev = pl.BlockSpec((HALO, CONV_COLS), lambda cj, li: (jnp.maximum(li * (tl // HALO) - 1, 0), cj))
    nxt = pl.BlockSpec((HALO, CONV_COLS),
                       lambda cj, li: (jnp.minimum((li + 1) * (tl // HALO), n_rows // HALO - 1), cj))
    par = pl.BlockSpec((8, CONV_COLS), lambda cj, li: (0, cj))
    return cur, prev, nxt, par


def _conv_fwd(u, w8, b8, name):
    n_rows, c = u.shape
    tl = _rows(n_rows)
    cur, prev, _, par = _conv_specs(n_rows, tl)

    def body(u_ref, prev_ref, w_ref, b_ref, o_ref):
        acc, _ = _conv_pre(u_ref, prev_ref, w_ref, b_ref[0:1, :], pl.program_id(1))
        o_ref[...] = acc * _sig(acc)

    return pl.pallas_call(
        body, name=name, grid=(c // CONV_COLS, n_rows // tl), in_specs=[cur, prev, par, par], out_specs=cur,
        out_shape=jax.ShapeDtypeStruct((n_rows, c), F32),
        compiler_params=_cparams("parallel", "parallel"),
    )(u, u, w8, b8)


def _conv_bwd_pre(u, w8, b8, dout, name):
    n_rows, c = u.shape
    tl = _rows(n_rows)
    cur, prev, _, par = _conv_specs(n_rows, tl)

    def body(u_ref, prev_ref, w_ref, b_ref, do_ref, dc_ref, acc_ref):
        @pl.when(pl.program_id(1) == 0)
        def _():
            acc_ref[...] = jnp.zeros_like(acc_ref)

        acc, shifted = _conv_pre(u_ref, prev_ref, w_ref, b_ref[0:1, :], pl.program_id(1))
        sg = _sig(acc)
        dc = do_ref[...] * (sg * (1.0 + acc * (1.0 - sg)))
        dc_ref[...] = dc
        for k in range(CONV_TAPS):
            acc_ref[k:k + 1, :] += jnp.sum(dc * shifted[CONV_TAPS - 1 - k], axis=0, keepdims=True)
        acc_ref[CONV_TAPS:CONV_TAPS + 1, :] += jnp.sum(dc, axis=0, keepdims=True)

    return pl.pallas_call(
        body, name=name, grid=(c // CONV_COLS, n_rows // tl), in_specs=[cur, prev, par, par, cur],
        out_specs=(cur, par),
        out_shape=(jax.ShapeDtypeStruct((n_rows, c), F32), jax.ShapeDtypeStruct((8, c), F32)),
        compiler_params=_cparams("parallel", "arbitrary"),
    )(u, u, w8, b8, dout)


def _conv_bwd_in(dc, w8, name):
    n_rows, c = dc.shape
    tl = _rows(n_rows)
    cur, _, nxt, par = _conv_specs(n_rows, tl)
    n_l = n_rows // tl

    def body(dc_ref, next_ref, w_ref, o_ref):
        cur_v = dc_ref[...]
        next8 = jnp.where(pl.program_id(1) == n_l - 1, 0.0, next_ref[...])
        row8 = lax.broadcasted_iota(jnp.int32, next8.shape, 0)
        acc = cur_v * w_ref[CONV_TAPS - 1:CONV_TAPS, :]
        for s in range(1, CONV_TAPS):
            acc = acc + _shift_up(cur_v, next8, s, row8) * w_ref[CONV_TAPS - 1 - s:CONV_TAPS - s, :]
        o_ref[...] = acc.astype(BF16)

    return pl.pallas_call(
        body, name=name, grid=(c // CONV_COLS, n_l), in_specs=[cur, nxt, par], out_specs=cur,
        out_shape=jax.ShapeDtypeStruct((n_rows, c), BF16),
        compiler_params=_cparams("parallel", "parallel"),
    )(dc, dc, w8)


NORM_GROUP = SSD_D_INNER // SSD_GROUPS


def _gnorm_fwd(y, z, w, name):
    n_rows, c = y.shape

    def body(y_ref, z_ref, w_ref, o_ref):
        zv = z_ref[...]
        yg = y_ref[...] * (zv * _sig(zv))
        r = lax.rsqrt(jnp.mean(yg * yg, axis=-1, keepdims=True) + RMS_EPS)
        o_ref[...] = (yg * r * w_ref[...]).astype(BF16)

    blk = pl.BlockSpec((_rows(n_rows),NORM_GROUP), lambda i, j: (i, j))
    wspec = pl.BlockSpec((1, NORM_GROUP), lambda i, j: (0, j))
    return pl.pallas_call(
        body, name=name, grid=(n_rows // _rows(n_rows), c // NORM_GROUP), in_specs=[blk, blk, wspec], out_specs=blk,
        out_shape=jax.ShapeDtypeStruct((n_rows, c), BF16),
        compiler_params=_cparams("parallel", "parallel"),
    )(y, z, w.reshape(1, c))


def _gnorm_bwd(y, z, w, dyn, name):
    n_rows, c = y.shape

    def body(y_ref, z_ref, w_ref, dn_ref, dy_ref, dz_ref, acc_ref):
        @pl.when(pl.program_id(1) == 0)
        def _():
            acc_ref[...] = jnp.zeros_like(acc_ref)

        zv = z_ref[...]
        yv = y_ref[...]
        sz = _sig(zv)
        silu = zv * sz
        yg = yv * silu
        r = lax.rsqrt(jnp.mean(yg * yg, axis=-1, keepdims=True) + RMS_EPS)
        nrm = yg * r
        dn = dn_ref[...]
        acc_ref[0:1, :] += jnp.sum(dn * nrm, axis=0, keepdims=True)
        dnw = dn * w_ref[...]
        dyg = r * (dnw - nrm * jnp.mean(dnw * nrm, axis=-1, keepdims=True))
        dy_ref[...] = dyg * silu
        dz_ref[...] = (dyg * yv * (sz * (1.0 + zv * (1.0 - sz)))).astype(BF16)

    blk = pl.BlockSpec((_rows(n_rows),NORM_GROUP), lambda j, i: (i, j))
    wspec = pl.BlockSpec((1, NORM_GROUP), lambda j, i: (0, j))
    aspec = pl.BlockSpec((8, NORM_GROUP), lambda j, i: (0, j))
    return pl.pallas_call(
        body, name=name, grid=(c // NORM_GROUP, n_rows // _rows(n_rows)), in_specs=[blk, blk, wspec, blk],
        out_specs=(blk, blk, aspec),
        out_shape=(jax.ShapeDtypeStruct((n_rows, c), F32), jax.ShapeDtypeStruct((n_rows, c), BF16),
                   jax.ShapeDtypeStruct((8, c), F32)),
        compiler_params=_cparams("parallel", "arbitrary"),
    )(y, z, w.reshape(1, c), dyn)


ATT_SCALE = ATT_HEAD_DIM ** -0.5
ATT_SLOPES = [2.0 ** (-8.0 * (h + 1) / ATT_HEADS) for h in range(ATT_HEADS)]
Q_PER_KV = ATT_HEADS // ATT_KV_HEADS


def _dup_half(t, g, lo):
    tr = pltpu.roll(t, ATT_HEAD_DIM, axis=1)
    return jnp.where(lo, t, tr) if g == 0 else jnp.where(lo, tr, t)


def _att_band(kv_ref, kvp_ref, n):
    cur = kv_ref[...]
    prev = jnp.where(n == 0, 0.0, kvp_ref[...])
    lo = lax.broadcasted_iota(jnp.int32, (ATT_BLOCK, LANE), 1) < ATT_HEAD_DIM
    bands = []
    for g in range(ATT_KV_HEADS):
        kb = jnp.concatenate([_dup_half(prev[:, :LANE], g, lo), _dup_half(cur[:, :LANE], g, lo)], axis=0)
        vb = jnp.concatenate([_dup_half(prev[:, LANE:], g, lo), _dup_half(cur[:, LANE:], g, lo)], axis=0)
        bands.append((kb.astype(BF16), vb.astype(BF16)))
    return bands


def _att_probs(qh, kb, sink, h, n):
    s = _dot(qh, kb, _NT)
    qi = lax.broadcasted_iota(jnp.int32, s.shape, 0)
    kj = lax.broadcasted_iota(jnp.int32, s.shape, 1)
    rel = qi + ATT_BLOCK - kj
    valid = (rel >= 0) & (rel < ATT_BLOCK) & ((kj >= ATT_BLOCK) | (n > 0))
    s = jnp.where(valid, s - ATT_SLOPES[h] * rel.astype(F32), NEG)
    m = jnp.maximum(jnp.max(s, axis=-1, keepdims=True), sink)
    p = jnp.exp(s - m)
    es = jnp.exp(sink - m)
    inv = 1.0 / (jnp.sum(p, axis=-1, keepdims=True) + es)
    return p * inv, es * inv


def _att_fwd(q, kv, sinks8, name):
    n_rows = q.shape[0]
    nb = n_rows // ATT_BLOCK

    def body(q_ref, kv_ref, kvp_ref, s_ref, o_ref):
        n = pl.program_id(0)
        bands = _att_band(kv_ref, kvp_ref, n)
        lo = lax.broadcasted_iota(jnp.int32, (ATT_BLOCK, LANE), 1) < ATT_HEAD_DIM
        for j in range(ATT_HEADS // 2):
            qp = q_ref[:, j * LANE:(j + 1) * LANE] * ATT_SCALE
            outs = []
            for half in range(2):
                h = 2 * j + half
                kb, vb = bands[h // Q_PER_KV]
                msk = lo if half == 0 else jnp.logical_not(lo)
                qh = jnp.where(msk, qp, 0.0).astype(BF16)
                p, _ = _att_probs(qh, kb, s_ref[0:1, h:h + 1], h, n)
                outs.append(_dot(p.astype(BF16), vb))
            o_ref[:, j * LANE:(j + 1) * LANE] = jnp.where(lo, outs[0], outs[1]).astype(BF16)

    return pl.pallas_call(
        body, name=name, grid=(nb,),
        in_specs=[pl.BlockSpec((ATT_BLOCK, Q_DIM), lambda n: (n, 0)),
                  pl.BlockSpec((ATT_BLOCK, 2 * LANE), lambda n: (n, 0)),
                  pl.BlockSpec((ATT_BLOCK, 2 * LANE), lambda n: (jnp.maximum(n - 1, 0), 0)),
                  pl.BlockSpec((8, LANE), lambda n: (0, 0))],
        out_specs=pl.BlockSpec((ATT_BLOCK, Q_DIM), lambda n: (n, 0)),
        out_shape=jax.ShapeDtypeStruct((n_rows, Q_DIM), BF16),
        compiler_params=_cparams("parallel"),
    )(q, kv, kv, sinks8)


def _att_bwd(q, kv, sinks8, dout, name):
    n_rows = q.shape[0]
    nb = n_rows // ATT_BLOCK

    def body(q_ref, kv_ref, kvp_ref, s_ref, do_ref, dq_ref, dkv_ref, acc_ref, carry_ref):
        n = pl.program_id(0)

        @pl.when(n == 0)
        def _():
            acc_ref[...] = jnp.zeros_like(acc_ref)
            carry_ref[...] = jnp.zeros_like(carry_ref)

        @pl.when(n == nb)
        def _():
            dkv_ref[...] = carry_ref[...].astype(BF16)

        @pl.when(n < nb)
        def _():
            bands = _att_band(kv_ref, kvp_ref, n)
            lo = lax.broadcasted_iota(jnp.int32, (ATT_BLOCK, LANE), 1) < ATT_HEAD_DIM
            lane1 = lax.broadcasted_iota(jnp.int32, (1, LANE), 1)
            dk_acc = [jnp.zeros((2 * ATT_BLOCK, LANE), F32) for _ in range(ATT_KV_HEADS)]
            dv_acc = [jnp.zeros((2 * ATT_BLOCK, LANE), F32) for _ in range(ATT_KV_HEADS)]
            dsink = jnp.zeros((1, LANE), F32)
            for j in range(ATT_HEADS // 2):
                qp = q_ref[:, j * LANE:(j + 1) * LANE] * ATT_SCALE
                dop = do_ref[:, j * LANE:(j + 1) * LANE].astype(F32)
                dqs = []
                for half in range(2):
                    h = 2 * j + half
                    g = h // Q_PER_KV
                    kb, vb = bands[g]
                    msk = lo if half == 0 else jnp.logical_not(lo)
                    qh = jnp.where(msk, qp, 0.0).astype(BF16)
                    doh = jnp.where(msk, dop, 0.0).astype(BF16)
                    p, ps = _att_probs(qh, kb, s_ref[0:1, h:h + 1], h, n)
                    dp = _dot(doh, vb, _NT)
                    delta = jnp.sum(p * dp, axis=-1, keepdims=True)
                    ds = p * (dp - delta)
                    dsink = jnp.where(lane1 == h, -jnp.sum(ps * delta), dsink)
                    ds_b = ds.astype(BF16)
                    dqs.append(_dot(ds_b, kb) * ATT_SCALE)
                    dk_acc[g] = dk_acc[g] + _dot(ds.T.astype(BF16), qh)
                    dv_acc[g] = dv_acc[g] + _dot(p.T.astype(BF16), doh)
                dq_ref[:, j * LANE:(j + 1) * LANE] = jnp.where(lo, dqs[0], dqs[1]).astype(BF16)
            acc_ref[0:1, :] += dsink
            lo2 = lax.broadcasted_iota(jnp.int32, (2 * ATT_BLOCK, LANE), 1) < ATT_HEAD_DIM
            folded = []
            for acc in (dk_acc, dv_acc):
                t0 = acc[0] + pltpu.roll(acc[0], ATT_HEAD_DIM, axis=1)
                t1 = acc[1] + pltpu.roll(acc[1], ATT_HEAD_DIM, axis=1)
                folded.append(jnp.where(lo2, t0, t1))
            band = jnp.concatenate(folded, axis=1)
            dkv_ref[...] = (carry_ref[...] + band[:ATT_BLOCK]).astype(BF16)
            carry_ref[...] = band[ATT_BLOCK:]

    def qmap(n):
        return (jnp.minimum(n, nb - 1), 0)

    return pl.pallas_call(
        body, name=name, grid=(nb + 1,),
        in_specs=[pl.BlockSpec((ATT_BLOCK, Q_DIM), qmap),
                  pl.BlockSpec((ATT_BLOCK, 2 * LANE), qmap),
                  pl.BlockSpec((ATT_BLOCK, 2 * LANE), lambda n: (jnp.maximum(jnp.minimum(n, nb - 1) - 1, 0), 0)),
                  pl.BlockSpec((8, LANE), lambda n: (0, 0)),
                  pl.BlockSpec((ATT_BLOCK, Q_DIM), qmap)],
        out_specs=(pl.BlockSpec((ATT_BLOCK, Q_DIM), qmap),
                   pl.BlockSpec((ATT_BLOCK, 2 * LANE), lambda n: (jnp.maximum(n - 1, 0), 0)),
                   pl.BlockSpec((8, LANE), lambda n: (0, 0))),
        out_shape=(jax.ShapeDtypeStruct((n_rows, Q_DIM), BF16), jax.ShapeDtypeStruct((n_rows, 2 * LANE), BF16),
                   jax.ShapeDtypeStruct((8, LANE), F32)),
        scratch_shapes=[pltpu.VMEM((ATT_BLOCK, 2 * LANE), F32)],
        compiler_params=_cparams("arbitrary"),
    )(q, kv, kv, sinks8, dout)


HEADS_PER_GROUP = SSD_HEADS // SSD_GROUPS
PAIRS_PER_GROUP = HEADS_PER_GROUP // 2
T = SSD_CHUNK


def _ssd_scalars(dtr_ref, par_ref):
    dt = _softplus(dtr_ref[...] + par_ref[0:1, :])
    a = -jnp.exp(par_ref[1:2, :])
    ri = lax.broadcasted_iota(jnp.int32, (T, T), 0)
    ci = lax.broadcasted_iota(jnp.int32, (T, T), 1)
    tril = (ri >= ci).astype(F32)
    cs = _dot_hi(tril, dt * a)
    cst = cs.T
    return dt, a, cs, cst, ri, ci


def _lane_pick(lo, arr, k0):
    return jnp.where(lo, arr[:, k0:k0 + 1], arr[:, k0 + 1:k0 + 2])


def _ssd_fwd(xs, bm, cm, dtr, par, name):
    n_rows = xs.shape[0]
    nc = n_rows // T
    gw = PAIRS_PER_GROUP * LANE

    def body(x_ref, b_ref, c_ref, dtr_ref, par_ref, y_ref, hs_ref, h_ref):
        @pl.when(pl.program_id(1) == 0)
        def _():
            h_ref[...] = jnp.zeros_like(h_ref)

        dt, a, cs, cst, ri, ci = _ssd_scalars(dtr_ref, par_ref)
        tri = ri >= ci
        lo = lax.broadcasted_iota(jnp.int32, (T, LANE), 1) < SSD_CHUNK // 2
        ecs = jnp.exp(cs)
        dect = jnp.exp(cst[:, T - 1:T] - cst)
        etot = jnp.exp(cs[T - 1:T, :])
        bg = b_ref[...]
        cg = c_ref[...]
        bgt = bg.T
        cb = _dot(cg.astype(BF16), bg.astype(BF16), _NT)
        for j in range(PAIRS_PER_GROUP):
            xp = x_ref[:, j * LANE:(j + 1) * LANE]
            xdt = (xp * _lane_pick(lo, dt, 2 * j)).astype(BF16)
            hp = h_ref[j]
            hs_ref[0, 0, j] = hp
            hp_b = hp.astype(BF16)
            ys, ss = [], []
            for half in range(2):
                k = 2 * j + half
                lm = jnp.exp(jnp.where(tri, cs[:, k:k + 1] - cst[k:k + 1, :], NEG))
                yh = _dot((lm * cb).astype(BF16), xdt) + _dot((cg * ecs[:, k:k + 1]).astype(BF16), hp_b)
                ys.append(yh)
                ss.append(_dot((bgt * dect[k:k + 1, :]).astype(BF16), xdt))
            dsk = jnp.where(lo[0:1, :], par_ref[2:3, 2 * j:2 * j + 1], par_ref[2:3, 2 * j + 1:2 * j + 2])
            y_ref[:, j * LANE:(j + 1) * LANE] = jnp.where(lo, ys[0], ys[1]) + dsk * xp
            et = jnp.where(lo[0:1, :], etot[:, 2 * j:2 * j + 1], etot[:, 2 * j + 1:2 * j + 2])
            h_ref[j] = hp * et + jnp.where(lo, ss[0], ss[1])

    return pl.pallas_call(
        body, name=name, grid=(SSD_GROUPS, nc),
        in_specs=[pl.BlockSpec((T, gw), lambda g, c: (c, g)),
                  pl.BlockSpec((T, SSD_STATE), lambda g, c: (c, g)),
                  pl.BlockSpec((T, SSD_STATE), lambda g, c: (c, g)),
                  pl.BlockSpec((T, LANE), lambda g, c: (c, g)),
                  pl.BlockSpec((8, LANE), lambda g, c: (0, g))],
        out_specs=(pl.BlockSpec((T, gw), lambda g, c: (c, g)),
                   pl.BlockSpec((1, 1, PAIRS_PER_GROUP, SSD_STATE, LANE), lambda g, c: (g, c, 0, 0, 0))),
        out_shape=(jax.ShapeDtypeStruct((n_rows, SSD_D_INNER), F32),
                   jax.ShapeDtypeStruct((SSD_GROUPS, nc, PAIRS_PER_GROUP, SSD_STATE, LANE), F32)),
        scratch_shapes=[pltpu.VMEM((PAIRS_PER_GROUP, SSD_STATE, LANE), F32)],
        compiler_params=_cparams("parallel", "arbitrary"),
    )(xs, bm, cm, dtr, par)


def _ssd_bwd(xs, bm, cm, dtr, par, hs, dy, name):
    n_rows = xs.shape[0]
    nc = n_rows // T
    gw = PAIRS_PER_GROUP * LANE

    def body(x_ref, b_ref, c_ref, dtr_ref, par_ref, hs_ref, dy_ref,
             dx_ref, db_ref, dc_ref, ddtr_ref, acc_ref, dh_ref):
        @pl.when(pl.program_id(1) == 0)
        def _():
            dh_ref[...] = jnp.zeros_like(dh_ref)
            acc_ref[...] = jnp.zeros_like(acc_ref)

        dt, a, cs, cst, ri, ci = _ssd_scalars(dtr_ref, par_ref)
        tri = ri >= ci
        trit = ci >= ri
        lane = lax.broadcasted_iota(jnp.int32, (T, LANE), 1)
        lo = lane < SSD_CHUNK // 2
        lane1 = lane[0:1, :]
        ecs = jnp.exp(cs)
        ecst = jnp.exp(cst)
        dec = jnp.exp(cs[T - 1:T, :] - cs)
        etot = jnp.exp(cs[T - 1:T, :])
        bg = b_ref[...]
        cg = c_ref[...]
        bg_b = bg.astype(BF16)
        cg_b = cg.astype(BF16)
        cgt = cg.T
        cb = _dot(cg_b, bg_b, _NT)
        cbt = _dot(bg_b, cg_b, _NT)
        dbg = jnp.zeros((T, SSD_STATE), F32)
        dcg = jnp.zeros((T, SSD_STATE), F32)
        dcs_acc = jnp.zeros((T, LANE), F32)
        ddt_acc = jnp.zeros((T, LANE), F32)
        dsk_acc = jnp.zeros((1, LANE), F32)
        last_row = lax.broadcasted_iota(jnp.int32, (T, 1), 0) == T - 1
        for j in range(PAIRS_PER_GROUP):
            xp = x_ref[:, j * LANE:(j + 1) * LANE]
            dtl = _lane_pick(lo, dt, 2 * j)
            xdt = xp * dtl
            hp = hs_ref[0, 0, j]
            dhn = dh_ref[j]
            dyp = dy_ref[:, j * LANE:(j + 1) * LANE]
            dxdt = jnp.zeros((T, LANE), F32)
            et = jnp.where(lo[0:1, :], etot[:, 2 * j:2 * j + 1], etot[:, 2 * j + 1:2 * j + 2])
            dh_new = dhn * et
            for half in range(2):
                k = 2 * j + half
                msk = lo if half == 0 else jnp.logical_not(lo)
                xh_f = jnp.where(msk, xdt, 0.0)
                dyh_f = jnp.where(msk, dyp, 0.0)
                hh_f = jnp.where(msk, hp, 0.0)
                dhh_f = jnp.where(msk, dhn, 0.0)
                xh, dyh, hh, dhh = (v.astype(BF16) for v in (xh_f, dyh_f, hh_f, dhh_f))
                cs_col = cs[:, k:k + 1]
                cs_row = cst[k:k + 1, :]
                lm = jnp.exp(jnp.where(tri, cs_col - cs_row, NEG))
                lmt = jnp.exp(jnp.where(trit, cs_row - cs_col, NEG))
                dm = _dot(dyh, xh, _NT)
                dmt = _dot(xh, dyh, _NT)
                mm = lm * cb
                mmt = lmt * cbt
                bdh = _dot((bg * dec[:, k:k + 1]).astype(BF16), dhh)
                dxdt = dxdt + _dot(mmt.astype(BF16), dyh) + bdh
                dcg = dcg + _dot((dm * lm).astype(BF16), bg_b) + _dot(dyh, hh, _NT) * ecs[:, k:k + 1]
                dbg = dbg + _dot((dmt * lmt).astype(BF16), cg_b) + _dot(xh, dhh, _NT) * dec[:, k:k + 1]
                dh_new = dh_new + _dot((cgt * ecst[k:k + 1, :]).astype(BF16), dyh)
                yo = _dot((cg * ecs[:, k:k + 1]).astype(BF16), hh)
                e1 = jnp.sum(dm * mm, axis=-1, keepdims=True)
                e2 = jnp.sum(dmt * mmt, axis=-1, keepdims=True)
                e3 = jnp.sum(dyh_f * yo, axis=-1, keepdims=True)
                e4 = jnp.sum(xh_f * bdh, axis=-1, keepdims=True)
                tsum = jnp.sum(e4) + etot[:, k:k + 1] * jnp.sum(hh_f * dhh_f)
                dcs_h = e1 - e2 + e3 - e4 + jnp.where(last_row, tsum, 0.0)
                dcs_acc = jnp.where(lane == k, dcs_h, dcs_acc)
                dsk_acc = jnp.where(lane1 == k, jnp.sum(dyh_f * xp), dsk_acc)
            ddt_lo = jnp.sum(jnp.where(lo, dxdt * xp, 0.0), axis=-1, keepdims=True)
            ddt_hi = jnp.sum(jnp.where(lo, 0.0, dxdt * xp), axis=-1, keepdims=True)
            ddt_acc = jnp.where(lane == 2 * j, ddt_lo, jnp.where(lane == 2 * j + 1, ddt_hi, ddt_acc))
            dsk = jnp.where(lo[0:1, :], par_ref[2:3, 2 * j:2 * j + 1], par_ref[2:3, 2 * j + 1:2 * j + 2])
            dx_ref[:, j * LANE:(j + 1) * LANE] = dxdt * dtl + dsk * dyp
            dh_ref[j] = dh_new
        db_ref[...] = dbg
        dc_ref[...] = dcg
        triu = (ci >= ri).astype(F32)
        dda = _dot_hi(triu, dcs_acc)
        ddt = ddt_acc + dda * a
        ddtr = ddt * _sig(dtr_ref[...] + par_ref[0:1, :])
        ddtr_ref[...] = ddtr.astype(BF16)
        acc_ref[0:1, :] += jnp.sum(ddtr, axis=0, keepdims=True)
        acc_ref[1:2, :] += jnp.sum(dda * dt, axis=0, keepdims=True) * a
        acc_ref[2:3, :] += dsk_acc

    def rev(g, c):
        return (nc - 1 - c, g)

    return pl.pallas_call(
        body, name=name, grid=(SSD_GROUPS, nc),
        in_specs=[pl.BlockSpec((T, gw), rev),
                  pl.BlockSpec((T, SSD_STATE), rev),
                  pl.BlockSpec((T, SSD_STATE), rev),
                  pl.BlockSpec((T, LANE), rev),
                  pl.BlockSpec((8, LANE), lambda g, c: (0, g)),
                  pl.BlockSpec((1, 1, PAIRS_PER_GROUP, SSD_STATE, LANE), lambda g, c: (g, nc - 1 - c, 0, 0, 0)),
                  pl.BlockSpec((T, gw), rev)],
        out_specs=(pl.BlockSpec((T, gw), rev),
                   pl.BlockSpec((T, SSD_STATE), rev),
                   pl.BlockSpec((T, SSD_STATE), rev),
                   pl.BlockSpec((T, LANE), rev),
                   pl.BlockSpec((8, LANE), lambda g, c: (0, g))),
        out_shape=(jax.ShapeDtypeStruct((n_rows, SSD_D_INNER), F32),
                   jax.ShapeDtypeStruct((n_rows, BC_DIM), F32),
                   jax.ShapeDtypeStruct((n_rows, BC_DIM), F32),
                   jax.ShapeDtypeStruct((n_rows, DT_PAD), BF16),
                   jax.ShapeDtypeStruct((8, DT_PAD), F32)),
        scratch_shapes=[pltpu.VMEM((PAIRS_PER_GROUP, SSD_STATE, LANE), F32)],
        compiler_params=_cparams("parallel", "arbitrary"),
    )(xs, bm, cm, dtr, par, hs, dy)


ADAM_ROWS = 256


def _adamw(land, w, m, v, name):
    n_slots, na, r, wd = land.shape
    tr = r if r <= 2 * ADAM_ROWS else ADAM_ROWS
    bc1 = 1.0 - ADAM_B1 ** ADAM_STEP
    bc2 = 1.0 - ADAM_B2 ** ADAM_STEP

    def body(l_ref, w_ref, m_ref, v_ref, g_ref, d_ref, nm_ref, nv_ref):
        g = l_ref[0].astype(F32)
        for s in range(1, n_slots):
            g = g + l_ref[s].astype(F32)
        mn = ADAM_B1 * m_ref[...] + (1.0 - ADAM_B1) * g
        vn = ADAM_B2 * v_ref[...] + (1.0 - ADAM_B2) * (g * g)
        mh = mn / bc1
        vh = vn / bc2
        g_ref[...] = g
        nm_ref[...] = mn
        nv_ref[...] = vn
        d_ref[...] = -ADAM_LR * (mh / (jnp.sqrt(vh) + ADAM_EPS) + ADAM_WD * w_ref[...])

    blk = pl.BlockSpec((1, tr, wd), lambda i, j: (i, j, 0))
    lblk = pl.BlockSpec((n_slots, 1, tr, wd), lambda i, j: (0, i, j, 0))
    shp = jax.ShapeDtypeStruct((na, r, wd), F32)
    return pl.pallas_call(
        body, name=name, grid=(na, r // tr), in_specs=[lblk, blk, blk, blk], out_specs=(blk, blk, blk, blk),
        out_shape=(shp, shp, shp, shp),
        compiler_params=_cparams("parallel", "parallel"),
    )(land, w, m, v)


def _mesh_pos():
    return lax.axis_index("x"), lax.axis_index("y"), lax.axis_index("c")


def _peer(pos, k):
    x, y, c = pos
    px = 1 - x if (k >> 2) & 1 else x
    py = 1 - y if (k >> 1) & 1 else y
    pc = 1 - c if k & 1 else c
    return px, py, pc


def _flat(pos):
    return 4 * pos[0] + 2 * pos[1] + pos[2]


HBM_SPEC = pl.BlockSpec(memory_space=pl.ANY)


ROW_SHARDED = ("w_ssd_out", "w_att_out", "w_mix_out", "w_ffn_down")
COL_SHARDED = ("w_in", "w_ffn_gate", "w_ffn_up")
GATHERED = ROW_SHARDED + COL_SHARDED + ("conv_w",)
BIG = ROW_SHARDED + COL_SHARDED


def _all_gather_weights(blocks, name):
    n_t = len(GATHERED)
    ins = [blocks[n] for n in GATHERED]

    def out_shape(n, b):
        if n in ROW_SHARDED:
            return jax.ShapeDtypeStruct((b.shape[0], N_DEV * b.shape[1], b.shape[2]), b.dtype)
        return jax.ShapeDtypeStruct((N_DEV,) + b.shape, b.dtype)

    def body(*refs):
        in_refs, out_refs = refs[:n_t], refs[n_t:2 * n_t]
        send_sems, recv_sems, local_sems = refs[2 * n_t:]
        pos = _mesh_pos()
        me = _flat(pos)
        copies = []
        for t, n in enumerate(GATHERED):
            if n in ROW_SHARDED:
                r = in_refs[t].shape[1]
                dst = out_refs[t].at[:, pl.ds(pl.multiple_of(me * r, 16), r), :]
            else:
                dst = out_refs[t].at[me]
            local = pltpu.make_async_copy(in_refs[t], dst, local_sems.at[t])
            local.start()
            copies.append(local)
            for k in range(1, N_DEV):
                s = t * (N_DEV - 1) + k - 1
                cp = pltpu.make_async_remote_copy(
                    src_ref=in_refs[t], dst_ref=dst, send_sem=send_sems.at[s], recv_sem=recv_sems.at[s],
                    device_id=_peer(pos, k), device_id_type=MESH_ID)
                cp.start()
                copies.append(cp)
        for cp in copies:
            cp.wait()

    n_sem = n_t * (N_DEV - 1)
    outs = pl.pallas_call(
        body, name=name, in_specs=[HBM_SPEC] * n_t, out_specs=[HBM_SPEC] * n_t,
        out_shape=[out_shape(n, b) for n, b in zip(GATHERED, ins)],
        scratch_shapes=[pltpu.SemaphoreType.DMA((n_sem,)), pltpu.SemaphoreType.DMA((n_sem,)),
                        pltpu.SemaphoreType.DMA((n_t,))],
        compiler_params=pltpu.CompilerParams(has_side_effects=True),
    )(*ins)
    return dict(zip(GATHERED, outs))


def _exchange_grads(row_grads, chunked, name):
    items = []
    for n in ROW_SHARDED:
        gl = row_grads[n]
        r = gl[0].shape[0] // N_DEV
        items.append((n, list(gl), jax.ShapeDtypeStruct((N_DEV, len(gl), r, gl[0].shape[1]), gl[0].dtype)))
    for n, g in chunked.items():
        items.append((n, [g], jax.ShapeDtypeStruct(g.shape, g.dtype)))
    flat_in = [a for _, arrs, _ in items for a in arrs]
    n_in, n_out = len(flat_in), len(items)
    n_copy = n_in * (N_DEV - 1)

    def body(*refs):
        in_refs, out_refs = refs[:n_in], refs[n_in:n_in + n_out]
        send_sems, recv_sems, local_sems = refs[n_in + n_out:]
        pos = _mesh_pos()
        copies = []
        i = 0
        for t, (n, arrs, _) in enumerate(items):
            for l in range(len(arrs)):
                g_ref = in_refs[i]

                def src(dev, g_ref=g_ref, n=n):
                    if n in ROW_SHARDED:
                        r = g_ref.shape[0] // N_DEV
                        return g_ref.at[pl.ds(pl.multiple_of(dev * r, 16), r), :]
                    return g_ref.at[dev]

                def dst(k, t=t, l=l, n=n):
                    return out_refs[t].at[k, l] if n in ROW_SHARDED else out_refs[t].at[k]

                local = pltpu.make_async_copy(src(_flat(pos)), dst(0), local_sems.at[i])
                local.start()
                copies.append(local)
                for k in range(1, N_DEV):
                    peer = _peer(pos, k)
                    s = i * (N_DEV - 1) + k - 1
                    cp = pltpu.make_async_remote_copy(
                        src_ref=src(_flat(peer)), dst_ref=dst(k), send_sem=send_sems.at[s], recv_sem=recv_sems.at[s],
                        device_id=peer, device_id_type=MESH_ID)
                    cp.start()
                    copies.append(cp)
                i += 1
        for cp in copies:
            cp.wait()

    outs = pl.pallas_call(
        body, name=name, in_specs=[HBM_SPEC] * n_in, out_specs=[HBM_SPEC] * n_out,
        out_shape=[shp for _, _, shp in items],
        scratch_shapes=[pltpu.SemaphoreType.DMA((n_copy,)), pltpu.SemaphoreType.DMA((n_copy,)),
                        pltpu.SemaphoreType.DMA((n_in,))],
        compiler_params=pltpu.CompilerParams(has_side_effects=True),
    )(*flat_in)
    return {n: o for (n, _, _), o in zip(items, outs)}


def _all_gather_small(x, name):
    r, w = x.shape

    def body(x_ref, out_ref, send_sems, recv_sems):
        pos = _mesh_pos()
        me = _flat(pos)
        copies = []
        for k in range(1, N_DEV):
            cp = pltpu.make_async_remote_copy(
                src_ref=x_ref, dst_ref=out_ref.at[me], send_sem=send_sems.at[k - 1], recv_sem=recv_sems.at[k - 1],
                device_id=_peer(pos, k), device_id_type=MESH_ID)
            cp.start()
            copies.append(cp)
        out_ref[me] = x_ref[...]
        for cp in copies:
            cp.wait()

    vmem = pl.BlockSpec(memory_space=pltpu.VMEM)
    return pl.pallas_call(
        body, name=name, in_specs=[vmem], out_specs=vmem,
        out_shape=jax.ShapeDtypeStruct((N_DEV, r, w), x.dtype),
        scratch_shapes=[pltpu.SemaphoreType.DMA((N_DEV - 1,)), pltpu.SemaphoreType.DMA((N_DEV - 1,))],
        compiler_params=pltpu.CompilerParams(has_side_effects=True),
    )(x)


def _cols(g, l, lo, hi):
    c = g.shape[-1]
    parts = []
    for d in range(N_DEV):
        a, b = max(lo, d * c), min(hi, (d + 1) * c)
        if a < b:
            parts.append(g[d, l, :, a - d * c:b - d * c])
    return parts[0] if len(parts) == 1 else jnp.concatenate(parts, axis=1)


def _col_chunks(per_layer):
    c = per_layer[0].shape[-1] // N_DEV
    return jnp.stack([jnp.stack([g[..., d * c:(d + 1) * c] for g in per_layer]) for d in range(N_DEV)])


SMALL = ("ln_in_g", "ln_in_b", "conv_b", "dt_bias", "a_log", "d_skip", "ssd_norm_w", "att_sinks",
         "ln_mix_g", "ln_mix_b", "ln_ffn_g", "ln_ffn_b")


def _pack_small(vals):
    flat = jnp.concatenate([vals[n].reshape(-1) for n in SMALL])
    n = flat.shape[0]
    rows = -(-n // LANE)
    rows = -(-rows // 8) * 8
    return jnp.pad(flat, (0, rows * LANE - n)).reshape(rows, LANE)


def _unpack_small(buf, shapes):
    flat = buf.reshape(-1)
    off = 0
    out = {}
    for n in SMALL:
        cnt = math.prod(shapes[n])
        out[n] = flat[off:off + cnt].reshape(shapes[n])
        off += cnt
    return out


def _to_group_major(v):
    lead = v.shape[:-1]
    t = v.reshape(lead + (SSD_GROUPS, HEADS_PER_GROUP))
    t = jnp.pad(t, [(0, 0)] * len(lead) + [(0, 0), (0, LANE - HEADS_PER_GROUP)])
    return t.reshape(lead + (DT_PAD,))


def _from_group_major(v):
    lead = v.shape[:-1]
    return v.reshape(lead + (SSD_GROUPS, LANE))[..., :HEADS_PER_GROUP].reshape(lead + (SSD_HEADS,))


def _rows8(v):
    return jnp.pad(v, ((0, 8 - v.shape[0]), (0, 0)))


IN_OFFS = {"q": (0, 1024), "kv": (1024, 1280), "z": (1280, 3328), "xs": (3328, 5376), "b": (5376, 5888),
           "c": (5888, 6400), "dt": (6400, 6432), "gl": (6432, 8480)}
PIECES = ("q", "kv", "z", "xs", "b", "c", "dt", "gl")


def _split_w_in(g, l):
    out = {p: _cols(g, l, lo, hi) for p, (lo, hi) in IN_OFFS.items()}
    out["dt"] = _to_group_major(out["dt"])
    return out


def _join_dw_in(dws):
    dws = dict(dws)
    dws["dt"] = _from_group_major(dws["dt"])
    return jnp.concatenate([dws[p] for p in PIECES], axis=1)


def _layer_params(l, W, sm):
    p = {"w_in": _split_w_in(W["w_in"], l)}
    for n in ROW_SHARDED:
        p[n] = W[n][l]
    for n in ("w_ffn_gate", "w_ffn_up"):
        p[n] = _cols(W[n], l, 0, FFN_HIDDEN)
    cw = _cols(W["conv_w"], l, 0, SSD_D_INNER + 2 * BC_DIM)
    cb = sm["conv_b"][l]
    segs = {"xs": (0, 2048), "b": (2048, 2560), "c": (2560, 3072)}
    p["conv_w8"] = {s: _rows8(cw[:, lo:hi]) for s, (lo, hi) in segs.items()}
    p["conv_b8"] = {s: _rows8(cb[None, lo:hi]) for s, (lo, hi) in segs.items()}
    p["ssd_par"] = _rows8(jnp.stack([_to_group_major(sm["dt_bias"][l]), _to_group_major(sm["a_log"][l]),
                                     _to_group_major(sm["d_skip"][l])]))
    p["norm_w"] = sm["ssd_norm_w"][l]
    p["sinks8"] = _rows8(jnp.pad(sm["att_sinks"][l], (0, LANE - ATT_HEADS))[None])
    for n in ("ln_mix_g", "ln_mix_b", "ln_ffn_g", "ln_ffn_b"):
        p[n] = sm[n][l]
    return p


def _layer_fwd(h0, p, l):
    tag = f"l{l}_"
    a = {"h0": h0}
    for pc in PIECES:
        a[pc] = _mm(h0, p["w_in"][pc], "nn", tag + "proj_" + pc)
    for s in ("xs", "b", "c"):
        a[s + "c"] = _conv_fwd(a[s], p["conv_w8"][s], p["conv_b8"][s], tag + "conv_" + s)
    a["y"], a["hs"] = _ssd_fwd(a["xsc"], a["bc"], a["cc"], a["dt"], p["ssd_par"], tag + "ssd_fwd")
    a["yn"] = _gnorm_fwd(a["y"], a["z"], p["norm_w"], tag + "gnorm")
    a["ya"] = _mm(a["yn"], p["w_ssd_out"], "nn", tag + "ssd_out")
    a["att"] = _att_fwd(a["q"], a["kv"], p["sinks8"], tag + "att_fwd")
    a["yb"] = _mm(a["att"], p["w_att_out"], "nn", tag + "att_out")
    a["merged"] = _merge_fwd(a["gl"], a["ya"], a["yb"], tag + "merge")
    a["mix"] = _mm(a["merged"], p["w_mix_out"], "nn", tag + "mix_out")
    a["h1"] = _ln_fwd(h0, a["mix"], p["ln_mix_g"], p["ln_mix_b"], ALPHA, tag + "ln_mix")
    a["fg"] = _mm(a["h1"], p["w_ffn_gate"], "nn", tag + "ffn_gate")
    a["fu"] = _mm(a["h1"], p["w_ffn_up"], "nn", tag + "ffn_up")
    a["act"] = _swiglu_fwd(a["fg"], a["fu"], tag + "swiglu")
    a["ffn"] = _mm(a["act"], p["w_ffn_down"], "nn", tag + "ffn_down")
    a["h2"] = _ln_fwd(a["h1"], a["ffn"], p["ln_ffn_g"], p["ln_ffn_b"], ALPHA, tag + "ln_ffn")
    return a


def _dw(x, dy, name):
    return _mm(x, dy, "tn", name, out_dtype=BF16)


def _layer_bwd(a, p, dh2, l):
    tag = f"l{l}_b_"
    gw, gs = {}, {}
    du2, acc = _ln_bwd(a["h1"], a["ffn"], p["ln_ffn_g"], dh2, ALPHA, tag + "ln_ffn")
    gs["ln_ffn_g"], gs["ln_ffn_b"] = acc[0], acc[1]
    gw["w_ffn_down"] = _dw(a["act"], du2, tag + "dw_down")
    dact = _mm(du2, p["w_ffn_down"], "nt", tag + "dact")
    dfg, dfu = _swiglu_bwd(a["fg"], a["fu"], dact, tag + "swiglu")
    gw["w_ffn_gate"] = _dw(a["h1"], dfg, tag + "dw_gate")
    gw["w_ffn_up"] = _dw(a["h1"], dfu, tag + "dw_up")
    dh1 = _mm(dfg, p["w_ffn_gate"], "nt", tag + "dh1_gate", add=du2, add_scale=ALPHA)
    dh1 = _mm(dfu, p["w_ffn_up"], "nt", tag + "dh1_up", add=dh1)
    du1, acc = _ln_bwd(a["h0"], a["mix"], p["ln_mix_g"], dh1, ALPHA, tag + "ln_mix")
    gs["ln_mix_g"], gs["ln_mix_b"] = acc[0], acc[1]
    gw["w_mix_out"] = _dw(a["merged"], du1, tag + "dw_mix")
    dmerged = _mm(du1, p["w_mix_out"], "nt", tag + "dmerged")
    dya, dyb, dgl = _merge_bwd(a["gl"], a["ya"], a["yb"], dmerged, tag + "merge")
    gw["w_ssd_out"] = _dw(a["yn"], dya, tag + "dw_ssd")
    gw["w_att_out"] = _dw(a["att"], dyb, tag + "dw_att")
    dyn = _mm(dya, p["w_ssd_out"], "nt", tag + "dyn")
    datt = _mm(dyb, p["w_att_out"], "nt", tag + "datt", out_dtype=BF16)
    dq, dkv, acc = _att_bwd(a["q"], a["kv"], p["sinks8"], datt, tag + "att")
    gs["att_sinks"] = acc[0, :ATT_HEADS]
    dy, dz, acc = _gnorm_bwd(a["y"], a["z"], p["norm_w"], dyn, tag + "gnorm")
    gs["ssd_norm_w"] = acc[0]
    dxs, dbm, dcm, ddt, acc = _ssd_bwd(a["xsc"], a["bc"], a["cc"], a["dt"], p["ssd_par"], a["hs"], dy,
                                       tag + "ssd")
    gs["dt_bias"], gs["a_log"], gs["d_skip"] = (_from_group_major(acc[i]) for i in range(3))
    dpieces = {"q": dq, "kv": dkv, "z": dz, "dt": ddt, "gl": dgl}
    dconv_w, dconv_b = [], []
    for s, dout in (("xs", dxs), ("b", dbm), ("c", dcm)):
        dc, acc = _conv_bwd_pre(a[s], p["conv_w8"][s], p["conv_b8"][s], dout, tag + "conv_pre_" + s)
        dconv_w.append(acc[:CONV_TAPS])
        dconv_b.append(acc[CONV_TAPS])
        dpieces[s] = _conv_bwd_in(dc, p["conv_w8"][s], tag + "conv_in_" + s)
    gconv = jnp.concatenate(dconv_w, axis=1)
    gs["conv_b"] = jnp.concatenate(dconv_b)
    dws = {}
    dh0 = du1
    scale = ALPHA
    for pc in PIECES:
        dws[pc] = _dw(a["h0"], dpieces[pc], tag + "dw_in_" + pc)
        dh0 = _mm(dpieces[pc], p["w_in"][pc], "nt", tag + "dh0_" + pc, add=dh0, add_scale=scale)
        scale = 1.0
    gw["w_in"] = _join_dw_in(dws)
    return dh0, gw, gconv, gs


def _step(x, target, w, m, v):
    x2 = x[0]
    t2 = target[0]

    blocks = {n: w[n].astype(BF16) for n in BIG}
    blocks["conv_w"] = w["conv_w"]
    W = _all_gather_weights(blocks, "weights_all_gather")

    h = _ln_fwd(x2, None, w["ln_in_g"], w["ln_in_b"], 1.0, "ln_in")
    params, acts = [], []
    for l in range(DEPTH):
        p = _layer_params(l, W, w)
        a = _layer_fwd(h, p, l)
        params.append(p)
        acts.append(a)
        h = a["h2"]

    sse, dh = _loss_fwd_bwd(h, t2, "loss")
    loss = lax.psum(0.5 / D_MODEL * sse[0, 0], ("x", "y", "c"))

    gw_layers, gconv_layers, gs_layers = [], [], []
    for l in reversed(range(DEPTH)):
        dh, gw, gconv, gs = _layer_bwd(acts[l], params[l], dh, l)
        gw_layers.insert(0, gw)
        gconv_layers.insert(0, gconv)
        gs_layers.insert(0, gs)
    grad_x2, acc = _ln_bwd(x2, None, w["ln_in_g"], dh, 1.0, "ln_in_b")

    row_grads = {n: [gw_layers[l][n] for l in range(DEPTH)] for n in ROW_SHARDED}
    chunked = {n: _col_chunks([gw_layers[l][n] for l in range(DEPTH)]) for n in COL_SHARDED}
    chunked["conv_w"] = _col_chunks(gconv_layers)
    landing = _exchange_grads(row_grads, chunked, "grads_reduce_scatter")
    outs = [{} for _ in range(4)]
    for n in GATHERED:
        for o, t in zip(outs, _adamw(landing[n], w[n], m[n], v[n], "adamw_" + n)):
            o[n] = t

    gsm = {"ln_in_g": acc[0], "ln_in_b": acc[1]}
    for n in SMALL[2:]:
        gsm[n] = jnp.stack([gs_layers[l][n] for l in range(DEPTH)])
    small_shapes = {n: w[n].shape for n in SMALL}
    land_s = _all_gather_small(_pack_small(gsm), "small_grads_all_gather")
    for o, t in zip(outs, _adamw(land_s[:, None], _pack_small(w)[None], _pack_small(m)[None],
                                 _pack_small(v)[None], "adamw_small")):
        o.update(_unpack_small(t[0], small_shapes))
    return loss, grad_x2[None], outs


WEIGHT_NAMES = ("ln_in_g", "ln_in_b", "w_in", "conv_w", "conv_b", "dt_bias", "a_log", "d_skip", "ssd_norm_w",
                "att_sinks", "w_ssd_out", "w_att_out", "w_mix_out", "ln_mix_g", "ln_mix_b", "w_ffn_gate",
                "w_ffn_up", "w_ffn_down", "ln_ffn_g", "ln_ffn_b")


def kernel(x, ln_in_g, ln_in_b, w_in, conv_w, conv_b, dt_bias, a_log, d_skip, ssd_norm_w, att_sinks, w_ssd_out, w_att_out, w_mix_out, ln_mix_g, ln_mix_b, w_ffn_gate, w_ffn_up, w_ffn_down, ln_ffn_g, ln_ffn_b, loss_target, m_ln_in_g, m_ln_in_b, m_w_in, m_conv_w, m_conv_b, m_dt_bias, m_a_log, m_d_skip, m_ssd_norm_w, m_att_sinks, m_w_ssd_out, m_w_att_out, m_w_mix_out, m_ln_mix_g, m_ln_mix_b, m_w_ffn_gate, m_w_ffn_up, m_w_ffn_down, m_ln_ffn_g, m_ln_ffn_b, v_ln_in_g, v_ln_in_b, v_w_in, v_conv_w, v_conv_b, v_dt_bias, v_a_log, v_d_skip, v_ssd_norm_w, v_att_sinks, v_w_ssd_out, v_w_att_out, v_w_mix_out, v_ln_mix_g, v_ln_mix_b, v_w_ffn_gate, v_w_ffn_up, v_w_ffn_down, v_ln_ffn_g, v_ln_ffn_b):
    w = dict(zip(WEIGHT_NAMES, (ln_in_g, ln_in_b, w_in, conv_w, conv_b, dt_bias, a_log, d_skip, ssd_norm_w,
                                att_sinks, w_ssd_out, w_att_out, w_mix_out, ln_mix_g, ln_mix_b, w_ffn_gate,
                                w_ffn_up, w_ffn_down, ln_ffn_g, ln_ffn_b)))
    m = dict(zip(WEIGHT_NAMES, (m_ln_in_g, m_ln_in_b, m_w_in, m_conv_w, m_conv_b, m_dt_bias, m_a_log, m_d_skip,
                                m_ssd_norm_w, m_att_sinks, m_w_ssd_out, m_w_att_out, m_w_mix_out, m_ln_mix_g,
                                m_ln_mix_b, m_w_ffn_gate, m_w_ffn_up, m_w_ffn_down, m_ln_ffn_g, m_ln_ffn_b)))
    v = dict(zip(WEIGHT_NAMES, (v_ln_in_g, v_ln_in_b, v_w_in, v_conv_w, v_conv_b, v_dt_bias, v_a_log, v_d_skip,
                                v_ssd_norm_w, v_att_sinks, v_w_ssd_out, v_w_att_out, v_w_mix_out, v_ln_mix_g,
                                v_ln_mix_b, v_w_ffn_gate, v_w_ffn_up, v_w_ffn_down, v_ln_ffn_g, v_ln_ffn_b)))
    loss, grad_x, outs = _step(x, loss_target, w, m, v)
    result = [loss, grad_x]
    for o in outs:
        result.extend(o[n] for n in WEIGHT_NAMES)
    return tuple(result)
```

```python
import functools
import math

import jax
import jax.numpy as jnp
from jax import lax
from jax.experimental import pallas as pl
from jax.experimental.pallas import tpu as pltpu

F32 = jnp.float32
BF16 = jnp.bfloat16

D_MODEL = 1024
DEPTH = 2
N_DEV = 8
ATT_HEADS = 16
ATT_KV_HEADS = 2
ATT_HEAD_DIM = 64
ATT_BLOCK = 128
SSD_D_INNER = 2048
SSD_HEADS = 32
SSD_GROUPS = 4
SSD_STATE = 128
SSD_CHUNK = 128
FFN_HIDDEN = 2816
LN_EPS = 1e-5
RMS_EPS = 1e-5
ALPHA = (2 * DEPTH) ** 0.25
Q_DIM = 1024
KV_DIM = 128
BC_DIM = 512
IN_DIM = 8480
IN_SHARD = IN_DIM // N_DEV
DT_PAD = 512

ADAM_LR = 0.001
ADAM_B1 = 0.9
ADAM_B2 = 0.999
ADAM_EPS = 1e-08
ADAM_WD = 0.01
ADAM_STEP = 10

LANE = 128
VMEM_LIMIT = 48 * 1024 * 1024
PACK_W = 1024
NEG = -1e30

_NN = (((1,), (0,)), ((), ()))
_NT = (((1,), (1,)), ((), ()))
_TN = (((0,), (0,)), ((), ()))
MESH_ID = pl.DeviceIdType.MESH


def _dot(a, b, dims=_NN):
    return lax.dot_general(a, b, dims, preferred_element_type=F32)


def _dot_hi(a, b):
    return lax.dot_general(a, b, _NN, preferred_element_type=F32, precision=lax.Precision.HIGHEST)


def _sig(x):
    return 1.0 / (1.0 + jnp.exp(-x))


def _softplus(x):
    return jnp.maximum(x, 0.0) + jnp.log(1.0 + jnp.exp(-jnp.abs(x)))


def _cparams(*sem):
    return pltpu.CompilerParams(dimension_semantics=sem, vmem_limit_bytes=VMEM_LIMIT)


def _pick(n, cap):
    if n <= cap:
        return n
    best = None
    for t in range(LANE, cap + 1, LANE):
        if n % t == 0:
            best = t
    assert best is not None, (n, cap)
    return best


def _tile(n):
    if n <= 1024 or n % 1024 == 0:
        return min(n, 1024)
    return _pick(n, 1408)


def _rows(n):
    return min(512, n)


def _mm(a, b, mode, name, add=None, add_scale=1.0, out_dtype=F32, dep=None):
    if mode == "nn":
        m, k = a.shape
        n = b.shape[1]
    elif mode == "nt":
        m, k = a.shape
        n = b.shape[0]
    else:
        k, m = a.shape
        n = b.shape[1]
    tm = _tile(m)
    tn = _tile(n)
    tk = _tile(k)
    nk = k // tk
    has_add = add is not None
    dims = {"nn": _NN, "nt": _NT, "tn": _TN}[mode]

    def body(*refs):
        if dep is not None:
            refs = refs[:-3] + refs[-2:]
        if has_add:
            a_ref, b_ref, add_ref, o_ref, acc_ref = refs
        else:
            a_ref, b_ref, o_ref, acc_ref = refs
        kk = pl.program_id(2)

        @pl.when(kk == 0)
        def _():
            if has_add:
                acc_ref[...] = add_scale * add_ref[...].astype(F32)
            else:
                acc_ref[...] = jnp.zeros_like(acc_ref)

        acc_ref[...] += _dot(a_ref[...].astype(BF16), b_ref[...].astype(BF16), dims)

        @pl.when(kk == nk - 1)
        def _():
            o_ref[...] = acc_ref[...].astype(o_ref.dtype)

    if mode == "nn":
        a_spec = pl.BlockSpec((tm, tk), lambda i, j, kk: (i, kk))
        b_spec = pl.BlockSpec((tk, tn), lambda i, j, kk: (kk, j))
    elif mode == "nt":
        a_spec = pl.BlockSpec((tm, tk), lambda i, j, kk: (i, kk))
        b_spec = pl.BlockSpec((tn, tk), lambda i, j, kk: (j, kk))
    else:
        a_spec = pl.BlockSpec((tk, tm), lambda i, j, kk: (kk, i))
        b_spec = pl.BlockSpec((tk, tn), lambda i, j, kk: (kk, j))
    o_spec = pl.BlockSpec((tm, tn), lambda i, j, kk: (i, j))
    in_specs = [a_spec, b_spec] + ([o_spec] if has_add else [])
    args = (a, b) + ((add,) if has_add else ())
    if dep is not None:
        in_specs.append(pl.BlockSpec((8, LANE), lambda i, j, kk: (0, 0)))
        args += (dep,)
    return pl.pallas_call(
        body, name=name, grid=(m // tm, n // tn, nk),
        in_specs=in_specs, out_specs=o_spec,
        out_shape=jax.ShapeDtypeStruct((m, n), out_dtype),
        scratch_shapes=[pltpu.VMEM((tm, tn), F32)],
        compiler_params=_cparams("parallel", "parallel", "arbitrary"),
    )(*args)


def _vec_spec(width):
    return pl.BlockSpec((1, width), lambda i: (0, 0))


def _ln_fwd(a, b, gamma, beta, alpha, name):
    n_rows, dm = a.shape
    has_b = b is not None

    def body(*refs):
        if has_b:
            a_ref, b_ref, g_ref, be_ref, o_ref = refs
            u = alpha * a_ref[...] + b_ref[...]
        else:
            a_ref, g_ref, be_ref, o_ref = refs
            u = a_ref[...]
        mu = jnp.mean(u, axis=-1, keepdims=True)
        d = u - mu
        var = jnp.mean(d * d, axis=-1, keepdims=True)
        o_ref[...] = d * lax.rsqrt(var + LN_EPS) * g_ref[...] + be_ref[...]

    row = pl.BlockSpec((_rows(n_rows),dm), lambda i: (i, 0))
    in_specs = [row] + ([row] if has_b else []) + [_vec_spec(dm), _vec_spec(dm)]
    args = (a,) + ((b,) if has_b else ()) + (gamma.reshape(1, dm), beta.reshape(1, dm))
    return pl.pallas_call(
        body, name=name, grid=(n_rows // _rows(n_rows),), in_specs=in_specs, out_specs=row,
        out_shape=jax.ShapeDtypeStruct((n_rows, dm), F32),
        compiler_params=_cparams("parallel"),
    )(*args)


def _ln_bwd(a, b, gamma, dy, alpha, name):
    n_rows, dm = a.shape
    has_b = b is not None

    def body(*refs):
        if has_b:
            a_ref, b_ref, g_ref, dy_ref, du_ref, acc_ref = refs
            u = alpha * a_ref[...] + b_ref[...]
        else:
            a_ref, g_ref, dy_ref, du_ref, acc_ref = refs
            u = a_ref[...]

        @pl.when(pl.program_id(0) == 0)
        def _():
            acc_ref[...] = jnp.zeros_like(acc_ref)

        mu = jnp.mean(u, axis=-1, keepdims=True)
        d = u - mu
        var = jnp.mean(d * d, axis=-1, keepdims=True)
        rstd = lax.rsqrt(var + LN_EPS)
        xhat = d * rstd
        dyv = dy_ref[...]
        acc_ref[0:1, :] += jnp.sum(dyv * xhat, axis=0, keepdims=True)
        acc_ref[1:2, :] += jnp.sum(dyv, axis=0, keepdims=True)
        dxh = dyv * g_ref[...]
        m1 = jnp.mean(dxh, axis=-1, keepdims=True)
        m2 = jnp.mean(dxh * xhat, axis=-1, keepdims=True)
        du_ref[...] = rstd * (dxh - m1 - xhat * m2)

    row = pl.BlockSpec((_rows(n_rows),dm), lambda i: (i, 0))
    in_specs = [row] + ([row] if has_b else []) + [_vec_spec(dm), row]
    args = (a,) + ((b,) if has_b else ()) + (gamma.reshape(1, dm), dy)
    return pl.pallas_call(
        body, name=name, grid=(n_rows // _rows(n_rows),), in_specs=in_specs,
        out_specs=(row, pl.BlockSpec((8, dm), lambda i: (0, 0))),
        out_shape=(jax.ShapeDtypeStruct((n_rows, dm), F32), jax.ShapeDtypeStruct((8, dm), F32)),
        compiler_params=_cparams("arbitrary"),
    )(*args)


def _loss_fwd_bwd(y, target, name):
    n_rows, dm = y.shape

    def body(y_ref, t_ref, acc_ref, dy_ref):
        @pl.when(pl.program_id(0) == 0)
        def _():
            acc_ref[...] = jnp.zeros_like(acc_ref)

        d = y_ref[...] - t_ref[...]
        acc_ref[...] += jnp.sum(d * d)
        dy_ref[...] = d * (1.0 / dm)

    row = pl.BlockSpec((_rows(n_rows),dm), lambda i: (i, 0))
    return pl.pallas_call(
        body, name=name, grid=(n_rows // _rows(n_rows),), in_specs=[row, row],
        out_specs=(pl.BlockSpec((8, LANE), lambda i: (0, 0)), row),
        out_shape=(jax.ShapeDtypeStruct((8, LANE), F32), jax.ShapeDtypeStruct((n_rows, dm), F32)),
        compiler_params=_cparams("arbitrary"),
    )(y, target)


def _swiglu_fwd(g, u, name):
    n_rows, w = g.shape
    tw = _pick(w, 1408)

    def body(g_ref, u_ref, o_ref):
        gv = g_ref[...]
        o_ref[...] = (gv * _sig(gv) * u_ref[...]).astype(BF16)

    blk = pl.BlockSpec((_rows(n_rows),tw), lambda i, j: (i, j))
    return pl.pallas_call(
        body, name=name, grid=(n_rows // _rows(n_rows), w // tw), in_specs=[blk, blk], out_specs=blk,
        out_shape=jax.ShapeDtypeStruct((n_rows, w), BF16),
        compiler_params=_cparams("parallel", "parallel"),
    )(g, u)


def _swiglu_bwd(g, u, dact, name):
    n_rows, w = g.shape
    tw = _pick(w, 1408)

    def body(g_ref, u_ref, da_ref, dg_ref, du_ref):
        gv = g_ref[...]
        s = _sig(gv)
        da = da_ref[...]
        dg_ref[...] = (da * u_ref[...] * (s * (1.0 + gv * (1.0 - s)))).astype(BF16)
        du_ref[...] = (da * gv * s).astype(BF16)

    blk = pl.BlockSpec((_rows(n_rows),tw), lambda i, j: (i, j))
    return pl.pallas_call(
        body, name=name, grid=(n_rows // _rows(n_rows), w // tw), in_specs=[blk, blk, blk], out_specs=(blk, blk),
        out_shape=(jax.ShapeDtypeStruct((n_rows, w), BF16), jax.ShapeDtypeStruct((n_rows, w), BF16)),
        compiler_params=_cparams("parallel", "parallel"),
    )(g, u, dact)


def _merge_fwd(gl, ya, yb, name):
    n_rows, dm = ya.shape

    def body(gl_ref, ya_ref, yb_ref, o_ref):
        ga = _sig(gl_ref[:, :dm])
        gb = _sig(gl_ref[:, dm:])
        o_ref[...] = (ga * ya_ref[...] + gb * yb_ref[...]).astype(BF16)

    row = pl.BlockSpec((_rows(n_rows),dm), lambda i: (i, 0))
    row2 = pl.BlockSpec((_rows(n_rows),2 * dm), lambda i: (i, 0))
    return pl.pallas_call(
        body, name=name, grid=(n_rows // _rows(n_rows),), in_specs=[row2, row, row], out_specs=row,
        out_shape=jax.ShapeDtypeStruct((n_rows, dm), BF16),
        compiler_params=_cparams("parallel"),
    )(gl, ya, yb)


def _merge_bwd(gl, ya, yb, dmerged, name):
    n_rows, dm = ya.shape

    def body(gl_ref, ya_ref, yb_ref, dm_ref, dya_ref, dyb_ref, dgl_ref):
        ga = _sig(gl_ref[:, :dm])
        gb = _sig(gl_ref[:, dm:])
        dmv = dm_ref[...]
        dya_ref[...] = (dmv * ga).astype(BF16)
        dyb_ref[...] = (dmv * gb).astype(BF16)
        dgl_ref[:, :dm] = (dmv * ya_ref[...] * ga * (1.0 - ga)).astype(BF16)
        dgl_ref[:, dm:] = (dmv * yb_ref[...] * gb * (1.0 - gb)).astype(BF16)

    row = pl.BlockSpec((_rows(n_rows),dm), lambda i: (i, 0))
    row2 = pl.BlockSpec((_rows(n_rows),2 * dm), lambda i: (i, 0))
    return pl.pallas_call(
        body, name=name, grid=(n_rows // _rows(n_rows),), in_specs=[row2, row, row, row], out_specs=(row, row, row2),
        out_shape=(jax.ShapeDtypeStruct((n_rows, dm), BF16), jax.ShapeDtypeStruct((n_rows, dm), BF16),
                   jax.ShapeDtypeStruct((n_rows, 2 * dm), BF16)),
        compiler_params=_cparams("parallel"),
    )(gl, ya, yb, dmerged)


CONV_TAPS = 4
CONV_COLS = 512
HALO = 8


def _shift_down(cur, prev8, s, row8):
    r = pltpu.roll(cur, s, axis=0)
    top = jnp.where(row8 < s, pltpu.roll(prev8, s, axis=0), r[0:HALO])
    return jnp.concatenate([top, r[HALO:]], axis=0)


def _shift_up(cur, next8, s, row8):
    n = cur.shape[0]
    r = pltpu.roll(cur, n - s, axis=0)
    bot = jnp.where(row8 >= HALO - s, pltpu.roll(next8, HALO - s, axis=0), r[n - HALO:])
    return jnp.concatenate([r[:n - HALO], bot], axis=0)


def _conv_pre(u_ref, prev_ref, w_ref, b_ref, li):
    cur = u_ref[...]
    prev8 = jnp.where(li == 0, 0.0, prev_ref[...])
    row8 = lax.broadcasted_iota(jnp.int32, prev8.shape, 0)
    shifted = [cur] + [_shift_down(cur, prev8, s, row8) for s in range(1, CONV_TAPS)]
    acc = b_ref[...] + shifted[0] * w_ref[CONV_TAPS - 1:CONV_TAPS, :]
    for s in range(1, CONV_TAPS):
        acc = acc + shifted[s] * w_ref[CONV_TAPS - 1 - s:CONV_TAPS - s, :]
    return acc, shifted


def _conv_specs(n_rows, tl):
    cur = pl.BlockSpec((tl, CONV_COLS), lambda cj, li: (li, cj))
    prev = pl.BlockSpec((HALO, CONV_COLS), lambda cj, li: (jnp.maximum(li * (tl // HALO) - 1, 0), cj))
    nxt = pl.BlockSpec((HALO, CONV_COLS),
                       lambda cj, li: (jnp.minimum((li + 1) * (tl // HALO), n_rows // HALO - 1), cj))
    par = pl.BlockSpec((8, CONV_COLS), lambda cj, li: (0, cj))
    return cur, prev, nxt, par


def _conv_fwd(u, w8, b8, name):
    n_rows, c = u.shape
    tl = _rows(n_rows)
    cur, prev, _, par = _conv_specs(n_rows, tl)

    def body(u_ref, prev_ref, w_ref, b_ref, o_ref):
        acc, _ = _conv_pre(u_ref, prev_ref, w_ref, b_ref[0:1, :], pl.program_id(1))
        o_ref[...] = acc * _sig(acc)

    return pl.pallas_call(
        body, name=name, grid=(c // CONV_COLS, n_rows // tl), in_specs=[cur, prev, par, par], out_specs=cur,
        out_shape=jax.ShapeDtypeStruct((n_rows, c), F32),
        compiler_params=_cparams("parallel", "parallel"),
    )(u, u, w8, b8)


def _conv_bwd_pre(u, w8, b8, dout, name):
    n_rows, c = u.shape
    tl = _rows(n_rows)
    cur, prev, _, par = _conv_specs(n_rows, tl)

    def body(u_ref, prev_ref, w_ref, b_ref, do_ref, dc_ref, acc_ref):
        @pl.when(pl.program_id(1) == 0)
        def _():
            acc_ref[...] = jnp.zeros_like(acc_ref)

        acc, shifted = _conv_pre(u_ref, prev_ref, w_ref, b_ref[0:1, :], pl.program_id(1))
        sg = _sig(acc)
        dc = do_ref[...] * (sg * (1.0 + acc * (1.0 - sg)))
        dc_ref[...] = dc
        for k in range(CONV_TAPS):
            acc_ref[k:k + 1, :] += jnp.sum(dc * shifted[CONV_TAPS - 1 - k], axis=0, keepdims=True)
        acc_ref[CONV_TAPS:CONV_TAPS + 1, :] += jnp.sum(dc, axis=0, keepdims=True)

    return pl.pallas_call(
        body, name=name, grid=(c // CONV_COLS, n_rows // tl), in_specs=[cur, prev, par, par, cur],
        out_specs=(cur, par),
        out_shape=(jax.ShapeDtypeStruct((n_rows, c), F32), jax.ShapeDtypeStruct((8, c), F32)),
        compiler_params=_cparams("parallel", "arbitrary"),
    )(u, u, w8, b8, dout)


def _conv_bwd_in(dc, w8, name):
    n_rows, c = dc.shape
    tl = _rows(n_rows)
    cur, _, nxt, par = _conv_specs(n_rows, tl)
    n_l = n_rows // tl

    def body(dc_ref, next_ref, w_ref, o_ref):
        cur_v = dc_ref[...]
        next8 = jnp.where(pl.program_id(1) == n_l - 1, 0.0, next_ref[...])
        row8 = lax.broadcasted_iota(jnp.int32, next8.shape, 0)
        acc = cur_v * w_ref[CONV_TAPS - 1:CONV_TAPS, :]
        for s in range(1, CONV_TAPS):
            acc = acc + _shift_up(cur_v, next8, s, row8) * w_ref[CONV_TAPS - 1 - s:CONV_TAPS - s, :]
        o_ref[...] = acc.astype(BF16)

    return pl.pallas_call(
        body, name=name, grid=(c // CONV_COLS, n_l), in_specs=[cur, nxt, par], out_specs=cur,
        out_shape=jax.ShapeDtypeStruct((n_rows, c), BF16),
        compiler_params=_cparams("parallel", "parallel"),
    )(dc, dc, w8)


NORM_GROUP = SSD_D_INNER // SSD_GROUPS


def _gnorm_fwd(y, z, w, name):
    n_rows, c = y.shape

    def body(y_ref, z_ref, w_ref, o_ref):
        zv = z_ref[...]
        yg = y_ref[...] * (zv * _sig(zv))
        r = lax.rsqrt(jnp.mean(yg * yg, axis=-1, keepdims=True) + RMS_EPS)
        o_ref[...] = (yg * r * w_ref[...]).astype(BF16)

    blk = pl.BlockSpec((_rows(n_rows),NORM_GROUP), lambda i, j: (i, j))
    wspec = pl.BlockSpec((1, NORM_GROUP), lambda i, j: (0, j))
    return pl.pallas_call(
        body, name=name, grid=(n_rows // _rows(n_rows), c // NORM_GROUP), in_specs=[blk, blk, wspec], out_specs=blk,
        out_shape=jax.ShapeDtypeStruct((n_rows, c), BF16),
        compiler_params=_cparams("parallel", "parallel"),
    )(y, z, w.reshape(1, c))


def _gnorm_bwd(y, z, w, dyn, name):
    n_rows, c = y.shape

    def body(y_ref, z_ref, w_ref, dn_ref, dy_ref, dz_ref, acc_ref):
        @pl.when(pl.program_id(1) == 0)
        def _():
            acc_ref[...] = jnp.zeros_like(acc_ref)

        zv = z_ref[...]
        yv = y_ref[...]
        sz = _sig(zv)
        silu = zv * sz
        yg = yv * silu
        r = lax.rsqrt(jnp.mean(yg * yg, axis=-1, keepdims=True) + RMS_EPS)
        nrm = yg * r
        dn = dn_ref[...]
        acc_ref[0:1, :] += jnp.sum(dn * nrm, axis=0, keepdims=True)
        dnw = dn * w_ref[...]
        dyg = r * (dnw - nrm * jnp.mean(dnw * nrm, axis=-1, keepdims=True))
        dy_ref[...] = dyg * silu
        dz_ref[...] = (dyg * yv * (sz * (1.0 + zv * (1.0 - sz)))).astype(BF16)

    blk = pl.BlockSpec((_rows(n_rows),NORM_GROUP), lambda j, i: (i, j))
    wspec = pl.BlockSpec((1, NORM_GROUP), lambda j, i: (0, j))
    aspec = pl.BlockSpec((8, NORM_GROUP), lambda j, i: (0, j))
    return pl.pallas_call(
        body, name=name, grid=(c // NORM_GROUP, n_rows // _rows(n_rows)), in_specs=[blk, blk, wspec, blk],
        out_specs=(blk, blk, aspec),
        out_shape=(jax.ShapeDtypeStruct((n_rows, c), F32), jax.ShapeDtypeStruct((n_rows, c), BF16),
                   jax.ShapeDtypeStruct((8, c), F32)),
        compiler_params=_cparams("parallel", "arbitrary"),
    )(y, z, w.reshape(1, c), dyn)


ATT_SCALE = ATT_HEAD_DIM ** -0.5
ATT_SLOPES = [2.0 ** (-8.0 * (h + 1) / ATT_HEADS) for h in range(ATT_HEADS)]
Q_PER_KV = ATT_HEADS // ATT_KV_HEADS


def _dup_half(t, g, lo):
    tr = pltpu.roll(t, ATT_HEAD_DIM, axis=1)
    return jnp.where(lo, t, tr) if g == 0 else jnp.where(lo, tr, t)


def _att_band(kv_ref, kvp_ref, n):
    cur = kv_ref[...]
    prev = jnp.where(n == 0, 0.0, kvp_ref[...])
    lo = lax.broadcasted_iota(jnp.int32, (ATT_BLOCK, LANE), 1) < ATT_HEAD_DIM
    bands = []
    for g in range(ATT_KV_HEADS):
        kb = jnp.concatenate([_dup_half(prev[:, :LANE], g, lo), _dup_half(cur[:, :LANE], g, lo)], axis=0)
        vb = jnp.concatenate([_dup_half(prev[:, LANE:], g, lo), _dup_half(cur[:, LANE:], g, lo)], axis=0)
        bands.append((kb.astype(BF16), vb.astype(BF16)))
    return bands


def _att_probs(qh, kb, sink, h, n):
    s = _dot(qh, kb, _NT)
    qi = lax.broadcasted_iota(jnp.int32, s.shape, 0)
    kj = lax.broadcasted_iota(jnp.int32, s.shape, 1)
    rel = qi + ATT_BLOCK - kj
    valid = (rel >= 0) & (rel < ATT_BLOCK) & ((kj >= ATT_BLOCK) | (n > 0))
    s = jnp.where(valid, s - ATT_SLOPES[h] * rel.astype(F32), NEG)
    m = jnp.maximum(jnp.max(s, axis=-1, keepdims=True), sink)
    p = jnp.exp(s - m)
    es = jnp.exp(sink - m)
    inv = 1.0 / (jnp.sum(p, axis=-1, keepdims=True) + es)
    return p * inv, es * inv


def _att_fwd(q, kv, sinks8, name):
    n_rows = q.shape[0]
    nb = n_rows // ATT_BLOCK

    def body(q_ref, kv_ref, kvp_ref, s_ref, o_ref):
        n = pl.program_id(0)
        bands = _att_band(kv_ref, kvp_ref, n)
        lo = lax.broadcasted_iota(jnp.int32, (ATT_BLOCK, LANE), 1) < ATT_HEAD_DIM
        for j in range(ATT_HEADS // 2):
            qp = q_ref[:, j * LANE:(j + 1) * LANE] * ATT_SCALE
            outs = []
            for half in range(2):
                h = 2 * j + half
                kb, vb = bands[h // Q_PER_KV]
                msk = lo if half == 0 else jnp.logical_not(lo)
                qh = jnp.where(msk, qp, 0.0).astype(BF16)
                p, _ = _att_probs(qh, kb, s_ref[0:1, h:h + 1], h, n)
                outs.append(_dot(p.astype(BF16), vb))
            o_ref[:, j * LANE:(j + 1) * LANE] = jnp.where(lo, outs[0], outs[1]).astype(BF16)

    return pl.pallas_call(
        body, name=name, grid=(nb,),
        in_specs=[pl.BlockSpec((ATT_BLOCK, Q_DIM), lambda n: (n, 0)),
                  pl.BlockSpec((ATT_BLOCK, 2 * LANE), lambda n: (n, 0)),
                  pl.BlockSpec((ATT_BLOCK, 2 * LANE), lambda n: (jnp.maximum(n - 1, 0), 0)),
                  pl.BlockSpec((8, LANE), lambda n: (0, 0))],
        out_specs=pl.BlockSpec((ATT_BLOCK, Q_DIM), lambda n: (n, 0)),
        out_shape=jax.ShapeDtypeStruct((n_rows, Q_DIM), BF16),
        compiler_params=_cparams("parallel"),
    )(q, kv, kv, sinks8)


def _att_bwd(q, kv, sinks8, dout, name):
    n_rows = q.shape[0]
    nb = n_rows // ATT_BLOCK

    def body(q_ref, kv_ref, kvp_ref, s_ref, do_ref, dq_ref, dkv_ref, acc_ref, carry_ref):
        n = pl.program_id(0)

        @pl.when(n == 0)
        def _():
            acc_ref[...] = jnp.zeros_like(acc_ref)
            carry_ref[...] = jnp.zeros_like(carry_ref)

        @pl.when(n == nb)
        def _():
            dkv_ref[...] = carry_ref[...].astype(BF16)

        @pl.when(n < nb)
        def _():
            bands = _att_band(kv_ref, kvp_ref, n)
            lo = lax.broadcasted_iota(jnp.int32, (ATT_BLOCK, LANE), 1) < ATT_HEAD_DIM
            lane1 = lax.broadcasted_iota(jnp.int32, (1, LANE), 1)
            dk_acc = [jnp.zeros((2 * ATT_BLOCK, LANE), F32) for _ in range(ATT_KV_HEADS)]
            dv_acc = [jnp.zeros((2 * ATT_BLOCK, LANE), F32) for _ in range(ATT_KV_HEADS)]
            dsink = jnp.zeros((1, LANE), F32)
            for j in range(ATT_HEADS // 2):
                qp = q_ref[:, j * LANE:(j + 1) * LANE] * ATT_SCALE
                dop = do_ref[:, j * LANE:(j + 1) * LANE].astype(F32)
                dqs = []
                for half in range(2):
                    h = 2 * j + half
                    g = h // Q_PER_KV
                    kb, vb = bands[g]
                    msk = lo if half == 0 else jnp.logical_not(lo)
                    qh = jnp.where(msk, qp, 0.0).astype(BF16)
                    doh = jnp.where(msk, dop, 0.0).astype(BF16)
                    p, ps = _att_probs(qh, kb, s_ref[0:1, h:h + 1], h, n)
                    dp = _dot(doh, vb, _NT)
                    delta = jnp.sum(p * dp, axis=-1, keepdims=True)
                    ds = p * (dp - delta)
                    dsink = jnp.where(lane1 == h, -jnp.sum(ps * delta), dsink)
                    ds_b = ds.astype(BF16)
                    dqs.append(_dot(ds_b, kb) * ATT_SCALE)
                    dk_acc[g] = dk_acc[g] + _dot(ds.T.astype(BF16), qh)
                    dv_acc[g] = dv_acc[g] + _dot(p.T.astype(BF16), doh)
                dq_ref[:, j * LANE:(j + 1) * LANE] = jnp.where(lo, dqs[0], dqs[1]).astype(BF16)
            acc_ref[0:1, :] += dsink
            lo2 = lax.broadcasted_iota(jnp.int32, (2 * ATT_BLOCK, LANE), 1) < ATT_HEAD_DIM
            folded = []
            for acc in (dk_acc, dv_acc):
                t0 = acc[0] + pltpu.roll(acc[0], ATT_HEAD_DIM, axis=1)
                t1 = acc[1] + pltpu.roll(acc[1], ATT_HEAD_DIM, axis=1)
                folded.append(jnp.where(lo2, t0, t1))
            band = jnp.concatenate(folded, axis=1)
            dkv_ref[...] = (carry_ref[...] + band[:ATT_BLOCK]).astype(BF16)
            carry_ref[...] = band[ATT_BLOCK:]

    def qmap(n):
        return (jnp.minimum(n, nb - 1), 0)

    return pl.pallas_call(
        body, name=name, grid=(nb + 1,),
        in_specs=[pl.BlockSpec((ATT_BLOCK, Q_DIM), qmap),
                  pl.BlockSpec((ATT_BLOCK, 2 * LANE), qmap),
                  pl.BlockSpec((ATT_BLOCK, 2 * LANE), lambda n: (jnp.maximum(jnp.minimum(n, nb - 1) - 1, 0), 0)),
                  pl.BlockSpec((8, LANE), lambda n: (0, 0)),
                  pl.BlockSpec((ATT_BLOCK, Q_DIM), qmap)],
        out_specs=(pl.BlockSpec((ATT_BLOCK, Q_DIM), qmap),
                   pl.BlockSpec((ATT_BLOCK, 2 * LANE), lambda n: (jnp.maximum(n - 1, 0), 0)),
                   pl.BlockSpec((8, LANE), lambda n: (0, 0))),
        out_shape=(jax.ShapeDtypeStruct((n_rows, Q_DIM), BF16), jax.ShapeDtypeStruct((n_rows, 2 * LANE), BF16),
                   jax.ShapeDtypeStruct((8, LANE), F32)),
        scratch_shapes=[pltpu.VMEM((ATT_BLOCK, 2 * LANE), F32)],
        compiler_params=_cparams("arbitrary"),
    )(q, kv, kv, sinks8, dout)


HEADS_PER_GROUP = SSD_HEADS // SSD_GROUPS
PAIRS_PER_GROUP = HEADS_PER_GROUP // 2
T = SSD_CHUNK


def _ssd_scalars(dtr_ref, par_ref):
    dt = _softplus(dtr_ref[...] + par_ref[0:1, :])
    a = -jnp.exp(par_ref[1:2, :])
    ri = lax.broadcasted_iota(jnp.int32, (T, T), 0)
    ci = lax.broadcasted_iota(jnp.int32, (T, T), 1)
    tril = (ri >= ci).astype(F32)
    cs = _dot_hi(tril, dt * a)
    cst = cs.T
    return dt, a, cs, cst, ri, ci


def _lane_pick(lo, arr, k0):
    return jnp.where(lo, arr[:, k0:k0 + 1], arr[:, k0 + 1:k0 + 2])


def _ssd_fwd(xs, bm, cm, dtr, par, name):
    n_rows = xs.shape[0]
    nc = n_rows // T
    gw = PAIRS_PER_GROUP * LANE

    def body(x_ref, b_ref, c_ref, dtr_ref, par_ref, y_ref, hs_ref, h_ref):
        @pl.when(pl.program_id(1) == 0)
        def _():
            h_ref[...] = jnp.zeros_like(h_ref)

        dt, a, cs, cst, ri, ci = _ssd_scalars(dtr_ref, par_ref)
        tri = ri >= ci
        lo = lax.broadcasted_iota(jnp.int32, (T, LANE), 1) < SSD_CHUNK // 2
        ecs = jnp.exp(cs)
        dect = jnp.exp(cst[:, T - 1:T] - cst)
        etot = jnp.exp(cs[T - 1:T, :])
        bg = b_ref[...]
        cg = c_ref[...]
        bgt = bg.T
        cb = _dot(cg.astype(BF16), bg.astype(BF16), _NT)
        for j in range(PAIRS_PER_GROUP):
            xp = x_ref[:, j * LANE:(j + 1) * LANE]
            xdt = (xp * _lane_pick(lo, dt, 2 * j)).astype(BF16)
            hp = h_ref[j]
            hs_ref[0, 0, j] = hp
            hp_b = hp.astype(BF16)
            ys, ss = [], []
            for half in range(2):
                k = 2 * j + half
                lm = jnp.exp(jnp.where(tri, cs[:, k:k + 1] - cst[k:k + 1, :], NEG))
                yh = _dot((lm * cb).astype(BF16), xdt) + _dot((cg * ecs[:, k:k + 1]).astype(BF16), hp_b)
                ys.append(yh)
                ss.append(_dot((bgt * dect[k:k + 1, :]).astype(BF16), xdt))
            dsk = jnp.where(lo[0:1, :], par_ref[2:3, 2 * j:2 * j + 1], par_ref[2:3, 2 * j + 1:2 * j + 2])
            y_ref[:, j * LANE:(j + 1) * LANE] = jnp.where(lo, ys[0], ys[1]) + dsk * xp
            et = jnp.where(lo[0:1, :], etot[:, 2 * j:2 * j + 1], etot[:, 2 * j + 1:2 * j + 2])
            h_ref[j] = hp * et + jnp.where(lo, ss[0], ss[1])

    return pl.pallas_call(
        body, name=name, grid=(SSD_GROUPS, nc),
        in_specs=[pl.BlockSpec((T, gw), lambda g, c: (c, g)),
                  pl.BlockSpec((T, SSD_STATE), lambda g, c: (c, g)),
                  pl.BlockSpec((T, SSD_STATE), lambda g, c: (c, g)),
                  pl.BlockSpec((T, LANE), lambda g, c: (c, g)),
                  pl.BlockSpec((8, LANE), lambda g, c: (0, g))],
        out_specs=(pl.BlockSpec((T, gw), lambda g, c: (c, g)),
                   pl.BlockSpec((1, 1, PAIRS_PER_GROUP, SSD_STATE, LANE), lambda g, c: (g, c, 0, 0, 0))),
        out_shape=(jax.ShapeDtypeStruct((n_rows, SSD_D_INNER), F32),
                   jax.ShapeDtypeStruct((SSD_GROUPS, nc, PAIRS_PER_GROUP, SSD_STATE, LANE), F32)),
        scratch_shapes=[pltpu.VMEM((PAIRS_PER_GROUP, SSD_STATE, LANE), F32)],
        compiler_params=_cparams("parallel", "arbitrary"),
    )(xs, bm, cm, dtr, par)


def _ssd_bwd(xs, bm, cm, dtr, par, hs, dy, name):
    n_rows = xs.shape[0]
    nc = n_rows // T
    gw = PAIRS_PER_GROUP * LANE

    def body(x_ref, b_ref, c_ref, dtr_ref, par_ref, hs_ref, dy_ref,
             dx_ref, db_ref, dc_ref, ddtr_ref, acc_ref, dh_ref):
        @pl.when(pl.program_id(1) == 0)
        def _():
            dh_ref[...] = jnp.zeros_like(dh_ref)
            acc_ref[...] = jnp.zeros_like(acc_ref)

        dt, a, cs, cst, ri, ci = _ssd_scalars(dtr_ref, par_ref)
        tri = ri >= ci
        trit = ci >= ri
        lane = lax.broadcasted_iota(jnp.int32, (T, LANE), 1)
        lo = lane < SSD_CHUNK // 2
        lane1 = lane[0:1, :]
        ecs = jnp.exp(cs)
        ecst = jnp.exp(cst)
        dec = jnp.exp(cs[T - 1:T, :] - cs)
        etot = jnp.exp(cs[T - 1:T, :])
        bg = b_ref[...]
        cg = c_ref[...]
        bg_b = bg.astype(BF16)
        cg_b = cg.astype(BF16)
        cgt = cg.T
        cb = _dot(cg_b, bg_b, _NT)
        cbt = _dot(bg_b, cg_b, _NT)
        dbg = jnp.zeros((T, SSD_STATE), F32)
        dcg = jnp.zeros((T, SSD_STATE), F32)
        dcs_acc = jnp.zeros((T, LANE), F32)
        ddt_acc = jnp.zeros((T, LANE), F32)
        dsk_acc = jnp.zeros((1, LANE), F32)
        last_row = lax.broadcasted_iota(jnp.int32, (T, 1), 0) == T - 1
        for j in range(PAIRS_PER_GROUP):
            xp = x_ref[:, j * LANE:(j + 1) * LANE]
            dtl = _lane_pick(lo, dt, 2 * j)
            xdt = xp * dtl
            hp = hs_ref[0, 0, j]
            dhn = dh_ref[j]
            dyp = dy_ref[:, j * LANE:(j + 1) * LANE]
            dxdt = jnp.zeros((T, LANE), F32)
            et = jnp.where(lo[0:1, :], etot[:, 2 * j:2 * j + 1], etot[:, 2 * j + 1:2 * j + 2])
            dh_new = dhn * et
            for half in range(2):
                k = 2 * j + half
                msk = lo if half == 0 else jnp.logical_not(lo)
                xh_f = jnp.where(msk, xdt, 0.0)
                dyh_f = jnp.where(msk, dyp, 0.0)
                hh_f = jnp.where(msk, hp, 0.0)
                dhh_f = jnp.where(msk, dhn, 0.0)
                xh, dyh, hh, dhh = (v.astype(BF16) for v in (xh_f, dyh_f, hh_f, dhh_f))
                cs_col = cs[:, k:k + 1]
                cs_row = cst[k:k + 1, :]
                lm = jnp.exp(jnp.where(tri, cs_col - cs_row, NEG))
                lmt = jnp.exp(jnp.where(trit, cs_row - cs_col, NEG))
                dm = _dot(dyh, xh, _NT)
                dmt = _dot(xh, dyh, _NT)
                mm = lm * cb
                mmt = lmt * cbt
                bdh = _dot((bg * dec[:, k:k + 1]).astype(BF16), dhh)
                dxdt = dxdt + _dot(mmt.astype(BF16), dyh) + bdh
                dcg = dcg + _dot((dm * lm).astype(BF16), bg_b) + _dot(dyh, hh, _NT) * ecs[:, k:k + 1]
                dbg = dbg + _dot((dmt * lmt).astype(BF16), cg_b) + _dot(xh, dhh, _NT) * dec[:, k:k + 1]
                dh_new = dh_new + _dot((cgt * ecst[k:k + 1, :]).astype(BF16), dyh)
                yo = _dot((cg * ecs[:, k:k + 1]).astype(BF16), hh)
                e1 = jnp.sum(dm * mm, axis=-1, keepdims=True)
                e2 = jnp.sum(dmt * mmt, axis=-1, keepdims=True)
                e3 = jnp.sum(dyh_f * yo, axis=-1, keepdims=True)
                e4 = jnp.sum(xh_f * bdh, axis=-1, keepdims=True)
                tsum = jnp.sum(e4) + etot[:, k:k + 1] * jnp.sum(hh_f * dhh_f)
                dcs_h = e1 - e2 + e3 - e4 + jnp.where(last_row, tsum, 0.0)
                dcs_acc = jnp.where(lane == k, dcs_h, dcs_acc)
                dsk_acc = jnp.where(lane1 == k, jnp.sum(dyh_f * xp), dsk_acc)
            ddt_lo = jnp.sum(jnp.where(lo, dxdt * xp, 0.0), axis=-1, keepdims=True)
            ddt_hi = jnp.sum(jnp.where(lo, 0.0, dxdt * xp), axis=-1, keepdims=True)
            ddt_acc = jnp.where(lane == 2 * j, ddt_lo, jnp.where(lane == 2 * j + 1, ddt_hi, ddt_acc))
            dsk = jnp.where(lo[0:1, :], par_ref[2:3, 2 * j:2 * j + 1], par_ref[2:3, 2 * j + 1:2 * j + 2])
            dx_ref[:, j * LANE:(j + 1) * LANE] = dxdt * dtl + dsk * dyp
            dh_ref[j] = dh_new
        db_ref[...] = dbg
        dc_ref[...] = dcg
        triu = (ci >= ri).astype(F32)
        dda = _dot_hi(triu, dcs_acc)
        ddt = ddt_acc + dda * a
        ddtr = ddt * _sig(dtr_ref[...] + par_ref[0:1, :])
        ddtr_ref[...] = ddtr.astype(BF16)
        acc_ref[0:1, :] += jnp.sum(ddtr, axis=0, keepdims=True)
        acc_ref[1:2, :] += jnp.sum(dda * dt, axis=0, keepdims=True) * a
        acc_ref[2:3, :] += dsk_acc

    def rev(g, c):
        return (nc - 1 - c, g)

    return pl.pallas_call(
        body, name=name, grid=(SSD_GROUPS, nc),
        in_specs=[pl.BlockSpec((T, gw), rev),
                  pl.BlockSpec((T, SSD_STATE), rev),
                  pl.BlockSpec((T, SSD_STATE), rev),
                  pl.BlockSpec((T, LANE), rev),
                  pl.BlockSpec((8, LANE), lambda g, c: (0, g)),
                  pl.BlockSpec((1, 1, PAIRS_PER_GROUP, SSD_STATE, LANE), lambda g, c: (g, nc - 1 - c, 0, 0, 0)),
                  pl.BlockSpec((T, gw), rev)],
        out_specs=(pl.BlockSpec((T, gw), rev),
                   pl.BlockSpec((T, SSD_STATE), rev),
                   pl.BlockSpec((T, SSD_STATE), rev),
                   pl.BlockSpec((T, LANE), rev),
                   pl.BlockSpec((8, LANE), lambda g, c: (0, g))),
        out_shape=(jax.ShapeDtypeStruct((n_rows, SSD_D_INNER), F32),
                   jax.ShapeDtypeStruct((n_rows, BC_DIM), F32),
                   jax.ShapeDtypeStruct((n_rows, BC_DIM), F32),
                   jax.ShapeDtypeStruct((n_rows, DT_PAD), BF16),
                   jax.ShapeDtypeStruct((8, DT_PAD), F32)),
        scratch_shapes=[pltpu.VMEM((PAIRS_PER_GROUP, SSD_STATE, LANE), F32)],
        compiler_params=_cparams("parallel", "arbitrary"),
    )(xs, bm, cm, dtr, par, hs, dy)


ADAM_ROWS = 256


def _adamw(lands, w, m, v, name):
    na = len(lands)
    n_slots, r, wd = lands[0].shape
    tr = r if r <= 2 * ADAM_ROWS else ADAM_ROWS
    nj = r // tr
    bc1 = 1.0 - ADAM_B1 ** ADAM_STEP
    bc2 = 1.0 - ADAM_B2 ** ADAM_STEP

    def body(*refs):
        l_refs = refs[:na]
        w_ref, m_ref, v_ref, g_ref, d_ref, nm_ref, nv_ref = refs[na:]
        for a in range(na):
            @pl.when(pl.program_id(0) == a)
            def _(l_ref=l_refs[a]):
                g = l_ref[0].astype(F32)
                for s in range(1, n_slots):
                    g = g + l_ref[s].astype(F32)
                mn = ADAM_B1 * m_ref[0] + (1.0 - ADAM_B1) * g
                vn = ADAM_B2 * v_ref[0] + (1.0 - ADAM_B2) * (g * g)
                mh = mn / bc1
                vh = vn / bc2
                g_ref[0] = g
                nm_ref[0] = mn
                nv_ref[0] = vn
                d_ref[0] = -ADAM_LR * (mh / (jnp.sqrt(vh) + ADAM_EPS) + ADAM_WD * w_ref[0])

    def land_spec(a):
        return pl.BlockSpec((n_slots, tr, wd),
                            lambda i, j: (0, jnp.where(i == a, j, jnp.where(i < a, 0, nj - 1)), 0))

    blk = pl.BlockSpec((1, tr, wd), lambda i, j: (i, j, 0))
    shp = jax.ShapeDtypeStruct((na, r, wd), F32)
    return pl.pallas_call(
        body, name=name, grid=(na, nj), in_specs=[land_spec(a) for a in range(na)] + [blk, blk, blk],
        out_specs=(blk, blk, blk, blk), out_shape=(shp, shp, shp, shp),
        compiler_params=_cparams("arbitrary", "arbitrary"),
    )(*lands, w, m, v)


def _mesh_pos():
    return lax.axis_index("x"), lax.axis_index("y"), lax.axis_index("c")


def _peer(pos, k):
    x, y, c = pos
    px = 1 - x if (k >> 2) & 1 else x
    py = 1 - y if (k >> 1) & 1 else y
    pc = 1 - c if k & 1 else c
    return px, py, pc


def _flat(pos):
    return 4 * pos[0] + 2 * pos[1] + pos[2]


HBM_SPEC = pl.BlockSpec(memory_space=pl.ANY)


ROW_SHARDED = ("w_ssd_out", "w_att_out", "w_mix_out", "w_ffn_down")
COL_SHARDED = ("w_in", "w_ffn_gate", "w_ffn_up")
GATHERED = ROW_SHARDED + COL_SHARDED + ("conv_w",)
BIG = ROW_SHARDED + COL_SHARDED


SEM_SPEC = pl.BlockSpec(memory_space=pltpu.SEMAPHORE)
TOKEN = jax.ShapeDtypeStruct((8, LANE), F32)
SPLIT_EFFECT = pltpu.SideEffectType.DATAFLOW_SIDE_EFFECTING
GATHER_ROWS = "gather_rows"
GATHER_SLOT = "gather_slot"
SCATTER_ROWS = "scatter_rows"
SCATTER_SLOT = "scatter_slot"


def _land_shape(kind, src):
    if kind == GATHER_ROWS:
        return (N_DEV * src.shape[0],) + src.shape[1:]
    if kind == GATHER_SLOT:
        return (N_DEV,) + src.shape
    if kind == SCATTER_ROWS:
        return (N_DEV, src.shape[0] // N_DEV) + src.shape[1:]
    return src.shape


def _views(kind, src_ref, land_ref, pos, k):
    me = _flat(pos)
    if kind == GATHER_ROWS:
        r = src_ref.shape[0]
        return src_ref, land_ref.at[pl.ds(pl.multiple_of(me * r, 16), r), :]
    if kind == GATHER_SLOT:
        return src_ref, land_ref.at[me]
    dev = _flat(_peer(pos, k))
    if kind == SCATTER_ROWS:
        r = land_ref.shape[1]
        return src_ref.at[pl.ds(pl.multiple_of(dev * r, 16), r), :], land_ref.at[k]
    return src_ref.at[dev], land_ref.at[k]


def _hbm(x):
    return pltpu.with_memory_space_constraint(x, pltpu.HBM)


def _exchange_start(items, after, name):
    kinds = [k for k, _ in items]
    srcs = [_hbm(s) for _, s in items]
    lands = [_hbm(lax.empty(_land_shape(k, s), s.dtype)) for k, s in items]
    n = len(items)
    n_copy = n * (N_DEV - 1)

    def body(*refs):
        src_refs, land_refs = refs[:n], refs[n:2 * n]
        send_sems, recv_sems = refs[2 * n + 1], refs[2 * n + 2]
        token_ref, local_sems = refs[4 * n + 3], refs[4 * n + 4]
        pos = _mesh_pos()
        local = []
        for i, kind in enumerate(kinds):
            s, d = _views(kind, src_refs[i], land_refs[i], pos, 0)
            lc = pltpu.make_async_copy(s, d, local_sems.at[i])
            lc.start()
            local.append(lc)
            for k in range(1, N_DEV):
                s, d = _views(kind, src_refs[i], land_refs[i], pos, k)
                j = i * (N_DEV - 1) + k - 1
                pltpu.make_async_remote_copy(src_ref=s, dst_ref=d, send_sem=send_sems.at[j], recv_sem=recv_sems.at[j],
                                             device_id=_peer(pos, k), device_id_type=MESH_ID).start()
        for lc in local:
            lc.wait()
        token_ref[...] = jnp.zeros_like(token_ref)

    arrs = srcs + lands
    outs = pl.pallas_call(
        body, name=name,
        in_specs=[HBM_SPEC] * (2 * n + 1),
        out_specs=[SEM_SPEC, SEM_SPEC] + [HBM_SPEC] * (2 * n) + [pl.BlockSpec(memory_space=pltpu.VMEM)],
        out_shape=[pltpu.SemaphoreType.DMA((n_copy,)), pltpu.SemaphoreType.DMA((n_copy,))]
        + [pltpu.HBM(a.shape, a.dtype) for a in arrs] + [TOKEN],
        input_output_aliases={i: 2 + i for i in range(2 * n)},
        scratch_shapes=[pltpu.SemaphoreType.DMA((n,))],
        compiler_params=pltpu.CompilerParams(has_side_effects=SPLIT_EFFECT),
    )(*arrs, after)
    return {"kinds": kinds, "send": outs[0], "recv": outs[1], "arrs": outs[2:2 + 2 * n], "token": outs[-1]}


def _exchange_wait(ex, after, name):
    kinds = ex["kinds"]
    n = len(kinds)

    def body(*refs):
        src_refs, land_refs = refs[:n], refs[n:2 * n]
        send_sems, recv_sems = refs[2 * n], refs[2 * n + 1]
        token_ref = refs[-1]
        pos = _mesh_pos()
        for i, kind in enumerate(kinds):
            for k in range(1, N_DEV):
                s, d = _views(kind, src_refs[i], land_refs[i], pos, k)
                j = i * (N_DEV - 1) + k - 1
                cp = pltpu.make_async_remote_copy(src_ref=s, dst_ref=d, send_sem=send_sems.at[j],
                                                  recv_sem=recv_sems.at[j], device_id=_peer(pos, k),
                                                  device_id_type=MESH_ID)
                cp.wait_send()
                cp.wait_recv()
        token_ref[...] = jnp.zeros_like(token_ref)

    outs = pl.pallas_call(
        body, name=name,
        in_specs=[HBM_SPEC] * (2 * n) + [SEM_SPEC, SEM_SPEC, HBM_SPEC],
        out_specs=[HBM_SPEC] * (2 * n) + [pl.BlockSpec(memory_space=pltpu.VMEM)],
        out_shape=[pltpu.HBM(a.shape, a.dtype) for a in ex["arrs"]] + [TOKEN],
        input_output_aliases={i: i for i in range(2 * n)},
        compiler_params=pltpu.CompilerParams(has_side_effects=SPLIT_EFFECT),
    )(*ex["arrs"], ex["send"], ex["recv"], after)
    return list(outs[n:2 * n]), outs[-1]


def _all_gather_small(x, name):
    r, w = x.shape

    def body(x_ref, out_ref, send_sems, recv_sems):
        pos = _mesh_pos()
        me = _flat(pos)
        copies = []
        for k in range(1, N_DEV):
            cp = pltpu.make_async_remote_copy(
                src_ref=x_ref, dst_ref=out_ref.at[me], send_sem=send_sems.at[k - 1], recv_sem=recv_sems.at[k - 1],
                device_id=_peer(pos, k), device_id_type=MESH_ID)
            cp.start()
            copies.append(cp)
        out_ref[me] = x_ref[...]
        for cp in copies:
            cp.wait()

    vmem = pl.BlockSpec(memory_space=pltpu.VMEM)
    return pl.pallas_call(
        body, name=name, in_specs=[vmem], out_specs=vmem,
        out_shape=jax.ShapeDtypeStruct((N_DEV, r, w), x.dtype),
        scratch_shapes=[pltpu.SemaphoreType.DMA((N_DEV - 1,)), pltpu.SemaphoreType.DMA((N_DEV - 1,))],
        compiler_params=pltpu.CompilerParams(has_side_effects=True),
    )(x)


def _cols(g, lo, hi):
    c = g.shape[-1]
    parts = []
    for d in range(N_DEV):
        a, b = max(lo, d * c), min(hi, (d + 1) * c)
        if a < b:
            parts.append(g[d, :, a - d * c:b - d * c])
    return parts[0] if len(parts) == 1 else jnp.concatenate(parts, axis=1)


def _col_chunks(g):
    c = g.shape[-1] // N_DEV
    return jnp.stack([g[:, d * c:(d + 1) * c] for d in range(N_DEV)])


IN_PART = ("w_in", "conv_w")
OUT_PART = ROW_SHARDED + ("w_ffn_gate", "w_ffn_up")


def _gather_items(w, names, l):
    items = []
    for n in names:
        blk = w[n][l] if n == "conv_w" else w[n][l].astype(BF16)
        items.append((GATHER_ROWS if n in ROW_SHARDED else GATHER_SLOT, blk))
    return items


def _scatter_items(grads, names):
    return [(SCATTER_ROWS, grads[n]) if n in ROW_SHARDED else (SCATTER_SLOT, _col_chunks(grads[n]))
            for n in names]


SMALL = ("ln_in_g", "ln_in_b", "conv_b", "dt_bias", "a_log", "d_skip", "ssd_norm_w", "att_sinks",
         "ln_mix_g", "ln_mix_b", "ln_ffn_g", "ln_ffn_b")


def _pack_small(vals):
    flat = jnp.concatenate([vals[n].reshape(-1) for n in SMALL])
    n = flat.shape[0]
    rows = -(-n // LANE)
    rows = -(-rows // 8) * 8
    return jnp.pad(flat, (0, rows * LANE - n)).reshape(rows, LANE)


def _unpack_small(buf, shapes):
    flat = buf.reshape(-1)
    off = 0
    out = {}
    for n in SMALL:
        cnt = math.prod(shapes[n])
        out[n] = flat[off:off + cnt].reshape(shapes[n])
        off += cnt
    return out


def _to_group_major(v):
    lead = v.shape[:-1]
    t = v.reshape(lead + (SSD_GROUPS, HEADS_PER_GROUP))
    t = jnp.pad(t, [(0, 0)] * len(lead) + [(0, 0), (0, LANE - HEADS_PER_GROUP)])
    return t.reshape(lead + (DT_PAD,))


def _from_group_major(v):
    lead = v.shape[:-1]
    return v.reshape(lead + (SSD_GROUPS, LANE))[..., :HEADS_PER_GROUP].reshape(lead + (SSD_HEADS,))


def _rows8(v):
    return jnp.pad(v, ((0, 8 - v.shape[0]), (0, 0)))


IN_OFFS = {"q": (0, 1024), "kv": (1024, 1280), "z": (1280, 3328), "xs": (3328, 5376), "b": (5376, 5888),
           "c": (5888, 6400), "dt": (6400, 6432), "gl": (6432, 8480)}
PIECES = ("q", "kv", "z", "xs", "b", "c", "dt", "gl")


def _split_w_in(g):
    out = {p: _cols(g, lo, hi) for p, (lo, hi) in IN_OFFS.items()}
    out["dt"] = _to_group_major(out["dt"])
    return out


def _join_dw_in(dws):
    dws = dict(dws)
    dws["dt"] = _from_group_major(dws["dt"])
    return jnp.concatenate([dws[p] for p in PIECES], axis=1)


def _params_out(W):
    p = {n: W[n] for n in ROW_SHARDED}
    for n in ("w_ffn_gate", "w_ffn_up"):
        p[n] = _cols(W[n], 0, FFN_HIDDEN)
    return p


def _params_in(l, W, sm):
    p = {"w_in": _split_w_in(W["w_in"])}
    cw = _cols(W["conv_w"], 0, SSD_D_INNER + 2 * BC_DIM)
    cb = sm["conv_b"][l]
    segs = {"xs": (0, 2048), "b": (2048, 2560), "c": (2560, 3072)}
    p["conv_w8"] = {s: _rows8(cw[:, lo:hi]) for s, (lo, hi) in segs.items()}
    p["conv_b8"] = {s: _rows8(cb[None, lo:hi]) for s, (lo, hi) in segs.items()}
    p["ssd_par"] = _rows8(jnp.stack([_to_group_major(sm["dt_bias"][l]), _to_group_major(sm["a_log"][l]),
                                     _to_group_major(sm["d_skip"][l])]))
    p["norm_w"] = sm["ssd_norm_w"][l]
    p["sinks8"] = _rows8(jnp.pad(sm["att_sinks"][l], (0, LANE - ATT_HEADS))[None])
    for n in ("ln_mix_g", "ln_mix_b", "ln_ffn_g", "ln_ffn_b"):
        p[n] = sm[n][l]
    return p


def _fwd_mixers(h0, p, l, dep=None):
    tag = f"l{l}_"
    a = {"h0": h0}
    for pc in PIECES:
        a[pc] = _mm(h0, p["w_in"][pc], "nn", tag + "proj_" + pc, dep=dep)
    for s in ("xs", "b", "c"):
        a[s + "c"] = _conv_fwd(a[s], p["conv_w8"][s], p["conv_b8"][s], tag + "conv_" + s)
    a["y"], a["hs"] = _ssd_fwd(a["xsc"], a["bc"], a["cc"], a["dt"], p["ssd_par"], tag + "ssd_fwd")
    a["yn"] = _gnorm_fwd(a["y"], a["z"], p["norm_w"], tag + "gnorm")
    a["att"] = _att_fwd(a["q"], a["kv"], p["sinks8"], tag + "att_fwd")
    return a


def _fwd_out(a, p, l, dep=None):
    tag = f"l{l}_"
    h0 = a["h0"]
    a["ya"] = _mm(a["yn"], p["w_ssd_out"], "nn", tag + "ssd_out", dep=dep)
    a["yb"] = _mm(a["att"], p["w_att_out"], "nn", tag + "att_out", dep=dep)
    a["merged"] = _merge_fwd(a["gl"], a["ya"], a["yb"], tag + "merge")
    a["mix"] = _mm(a["merged"], p["w_mix_out"], "nn", tag + "mix_out")
    a["h1"] = _ln_fwd(h0, a["mix"], p["ln_mix_g"], p["ln_mix_b"], ALPHA, tag + "ln_mix")
    a["fg"] = _mm(a["h1"], p["w_ffn_gate"], "nn", tag + "ffn_gate")
    a["fu"] = _mm(a["h1"], p["w_ffn_up"], "nn", tag + "ffn_up")
    a["act"] = _swiglu_fwd(a["fg"], a["fu"], tag + "swiglu")
    a["ffn"] = _mm(a["act"], p["w_ffn_down"], "nn", tag + "ffn_down")
    a["h2"] = _ln_fwd(a["h1"], a["ffn"], p["ln_ffn_g"], p["ln_ffn_b"], ALPHA, tag + "ln_ffn")
    return a


def _dw(x, dy, name, dep=None):
    return _mm(x, dy, "tn", name, out_dtype=BF16, dep=dep)


def _bwd_out(a, p, dh2, l, dep=None):
    tag = f"l{l}_b_"
    gw, gs = {}, {}
    du2, acc = _ln_bwd(a["h1"], a["ffn"], p["ln_ffn_g"], dh2, ALPHA, tag + "ln_ffn")
    gs["ln_ffn_g"], gs["ln_ffn_b"] = acc[0], acc[1]
    gw["w_ffn_down"] = _dw(a["act"], du2, tag + "dw_down", dep=dep)
    dact = _mm(du2, p["w_ffn_down"], "nt", tag + "dact", dep=dep)
    dfg, dfu = _swiglu_bwd(a["fg"], a["fu"], dact, tag + "swiglu")
    gw["w_ffn_gate"] = _dw(a["h1"], dfg, tag + "dw_gate")
    gw["w_ffn_up"] = _dw(a["h1"], dfu, tag + "dw_up")
    dh1 = _mm(dfg, p["w_ffn_gate"], "nt", tag + "dh1_gate", add=du2, add_scale=ALPHA)
    dh1 = _mm(dfu, p["w_ffn_up"], "nt", tag + "dh1_up", add=dh1)
    du1, acc = _ln_bwd(a["h0"], a["mix"], p["ln_mix_g"], dh1, ALPHA, tag + "ln_mix")
    gs["ln_mix_g"], gs["ln_mix_b"] = acc[0], acc[1]
    gw["w_mix_out"] = _dw(a["merged"], du1, tag + "dw_mix")
    dmerged = _mm(du1, p["w_mix_out"], "nt", tag + "dmerged")
    dya, dyb, dgl = _merge_bwd(a["gl"], a["ya"], a["yb"], dmerged, tag + "merge")
    gw["w_ssd_out"] = _dw(a["yn"], dya, tag + "dw_ssd")
    gw["w_att_out"] = _dw(a["att"], dyb, tag + "dw_att")
    return {"du1": du1, "dya": dya, "dyb": dyb, "dgl": dgl}, gw, gs


def _bwd_mixers(a, p, carry, l, dep=None):
    tag = f"l{l}_b_"
    gs = {}
    du1, dgl = carry["du1"], carry["dgl"]
    dyn = _mm(carry["dya"], p["w_ssd_out"], "nt", tag + "dyn", dep=dep)
    datt = _mm(carry["dyb"], p["w_att_out"], "nt", tag + "datt", out_dtype=BF16, dep=dep)
    dq, dkv, acc = _att_bwd(a["q"], a["kv"], p["sinks8"], datt, tag + "att")
    gs["att_sinks"] = acc[0, :ATT_HEADS]
    dy, dz, acc = _gnorm_bwd(a["y"], a["z"], p["norm_w"], dyn, tag + "gnorm")
    gs["ssd_norm_w"] = acc[0]
    dxs, dbm, dcm, ddt, acc = _ssd_bwd(a["xsc"], a["bc"], a["cc"], a["dt"], p["ssd_par"], a["hs"], dy,
                                       tag + "ssd")
    gs["dt_bias"], gs["a_log"], gs["d_skip"] = (_from_group_major(acc[i]) for i in range(3))
    dpieces = {"q": dq, "kv": dkv, "z": dz, "dt": ddt, "gl": dgl}
    dconv_w, dconv_b = [], []
    for s, dout in (("xs", dxs), ("b", dbm), ("c", dcm)):
        dc, acc = _conv_bwd_pre(a[s], p["conv_w8"][s], p["conv_b8"][s], dout, tag + "conv_pre_" + s)
        dconv_w.append(acc[:CONV_TAPS])
        dconv_b.append(acc[CONV_TAPS])
        dpieces[s] = _conv_bwd_in(dc, p["conv_w8"][s], tag + "conv_in_" + s)
    gconv = jnp.concatenate(dconv_w, axis=1)
    gs["conv_b"] = jnp.concatenate(dconv_b)
    dws = {}
    dh0 = du1
    scale = ALPHA
    for pc in PIECES:
        dws[pc] = _dw(a["h0"], dpieces[pc], tag + "dw_in_" + pc)
        dh0 = _mm(dpieces[pc], p["w_in"][pc], "nt", tag + "dh0_" + pc, add=dh0, add_scale=scale)
        scale = 1.0
    return dh0, _join_dw_in(dws), gconv, gs


def _step(x, target, w, m, v):
    x2 = x[0]
    t2 = target[0]
    tok = jnp.zeros(TOKEN.shape, TOKEN.dtype)

    ex = _exchange_start(_gather_items(w, IN_PART, 0), tok, "gather_l0_in_start")
    lands, tok = _exchange_wait(ex, ex["token"], "gather_l0_in_wait")
    p0 = _params_in(0, dict(zip(IN_PART, lands)), w)
    ex = _exchange_start(_gather_items(w, OUT_PART, 0) + _gather_items(w, IN_PART, 1), tok,
                         "gather_l0_out_l1_in_start")
    h = _ln_fwd(x2, None, w["ln_in_g"], w["ln_in_b"], 1.0, "ln_in")
    a0 = _fwd_mixers(h, p0, 0, dep=ex["token"])
    lands, tok = _exchange_wait(ex, a0["att"], "gather_l0_out_l1_in_wait")
    p0.update(_params_out(dict(zip(OUT_PART, lands))))
    p1 = _params_in(1, dict(zip(IN_PART, lands[len(OUT_PART):])), w)
    ex = _exchange_start(_gather_items(w, OUT_PART, 1), tok, "gather_l1_out_start")
    a0 = _fwd_out(a0, p0, 0, dep=ex["token"])
    lands, tok = _exchange_wait(ex, a0["h2"], "gather_l1_out_wait")
    p1.update(_params_out(dict(zip(OUT_PART, lands))))
    a1 = _fwd_out(_fwd_mixers(a0["h2"], p1, 1), p1, 1)

    sse, dh = _loss_fwd_bwd(a1["h2"], t2, "loss")
    loss = lax.psum(0.5 / D_MODEL * sse[0, 0], ("x", "y", "c"))

    carry, gw1, gs1 = _bwd_out(a1, p1, dh, 1)
    dh, gw1["w_in"], gw1["conv_w"], gs = _bwd_mixers(a1, p1, carry, 1)
    gs1.update(gs)
    ex1 = _exchange_start(_scatter_items(gw1, GATHERED), tok, "scatter_l1_start")
    carry, gw0, gs0 = _bwd_out(a0, p0, dh, 0, dep=ex1["token"])
    lands, tok = _exchange_wait(ex1, carry["dgl"], "scatter_l1_wait")
    land1 = dict(zip(GATHERED, lands))
    ex0 = _exchange_start(_scatter_items(gw0, OUT_PART), tok, "scatter_l0_out_start")
    dh, gw0["w_in"], gw0["conv_w"], gs = _bwd_mixers(a0, p0, carry, 0, dep=ex0["token"])
    gs0.update(gs)
    lands, tok = _exchange_wait(ex0, dh, "scatter_l0_out_wait")
    land0 = dict(zip(OUT_PART, lands))
    ex0 = _exchange_start(_scatter_items(gw0, IN_PART), tok, "scatter_l0_in_start")
    grad_x2, acc = _ln_bwd(x2, None, w["ln_in_g"], dh, 1.0, "ln_in_b")

    outs = [{} for _ in range(4)]

    def update(names):
        res = None
        for n in names:
            res = _adamw([land0[n], land1[n]], w[n], m[n], v[n], "adamw_" + n)
            for o, t in zip(outs, res):
                o[n] = t
        return res[1]

    update(OUT_PART)
    gsm = {"ln_in_g": acc[0], "ln_in_b": acc[1]}
    for n in SMALL[2:]:
        gsm[n] = jnp.stack([gs0[n], gs1[n]])
    small_shapes = {n: w[n].shape for n in SMALL}
    land_s = _all_gather_small(_pack_small(gsm), "small_grads_all_gather")
    res = _adamw([land_s], _pack_small(w)[None], _pack_small(m)[None], _pack_small(v)[None], "adamw_small")
    for o, t in zip(outs, res):
        o.update(_unpack_small(t[0], small_shapes))
    lands, _ = _exchange_wait(ex0, res[1], "scatter_l0_in_wait")
    land0.update(zip(IN_PART, lands))
    update(IN_PART)
    return loss, grad_x2[None], outs


WEIGHT_NAMES = ("ln_in_g", "ln_in_b", "w_in", "conv_w", "conv_b", "dt_bias", "a_log", "d_skip", "ssd_norm_w",
                "att_sinks", "w_ssd_out", "w_att_out", "w_mix_out", "ln_mix_g", "ln_mix_b", "w_ffn_gate",
                "w_ffn_up", "w_ffn_down", "ln_ffn_g", "ln_ffn_b")


def kernel(x, ln_in_g, ln_in_b, w_in, conv_w, conv_b, dt_bias, a_log, d_skip, ssd_norm_w, att_sinks, w_ssd_out, w_att_out, w_mix_out, ln_mix_g, ln_mix_b, w_ffn_gate, w_ffn_up, w_ffn_down, ln_ffn_g, ln_ffn_b, loss_target, m_ln_in_g, m_ln_in_b, m_w_in, m_conv_w, m_conv_b, m_dt_bias, m_a_log, m_d_skip, m_ssd_norm_w, m_att_sinks, m_w_ssd_out, m_w_att_out, m_w_mix_out, m_ln_mix_g, m_ln_mix_b, m_w_ffn_gate, m_w_ffn_up, m_w_ffn_down, m_ln_ffn_g, m_ln_ffn_b, v_ln_in_g, v_ln_in_b, v_w_in, v_conv_w, v_conv_b, v_dt_bias, v_a_log, v_d_skip, v_ssd_norm_w, v_att_sinks, v_w_ssd_out, v_w_att_out, v_w_mix_out, v_ln_mix_g, v_ln_mix_b, v_w_ffn_gate, v_w_ffn_up, v_w_ffn_down, v_ln_ffn_g, v_ln_ffn_b):
    w = dict(zip(WEIGHT_NAMES, (ln_in_g, ln_in_b, w_in, conv_w, conv_b, dt_bias, a_log, d_skip, ssd_norm_w,
                                att_sinks, w_ssd_out, w_att_out, w_mix_out, ln_mix_g, ln_mix_b, w_ffn_gate,
                                w_ffn_up, w_ffn_down, ln_ffn_g, ln_ffn_b)))
    m = dict(zip(WEIGHT_NAMES, (m_ln_in_g, m_ln_in_b, m_w_in, m_conv_w, m_conv_b, m_dt_bias, m_a_log, m_d_skip,
                                m_ssd_norm_w, m_att_sinks, m_w_ssd_out, m_w_att_out, m_w_mix_out, m_ln_mix_g,
                                m_ln_mix_b, m_w_ffn_gate, m_w_ffn_up, m_w_ffn_down, m_ln_ffn_g, m_ln_ffn_b)))
    v = dict(zip(WEIGHT_NAMES, (v_ln_in_g, v_ln_in_b, v_w_in, v_conv_w, v_conv_b, v_dt_bias, v_a_log, v_d_skip,
                                v_ssd_norm_w, v_att_sinks, v_w_ssd_out, v_w_att_out, v_w_mix_out, v_ln_mix_g,
                                v_ln_mix_b, v_w_ffn_gate, v_w_ffn_up, v_w_ffn_down, v_ln_ffn_g, v_ln_ffn_b)))
    loss, grad_x, outs = _step(x, loss_target, w, m, v)
    result = [loss, grad_x]
    for o in outs:
        result.extend(o[n] for n in WEIGHT_NAMES)
    return tuple(result)
```

```python
import functools
import math

import jax
import jax.numpy as jnp
from jax import lax
from jax.experimental import pallas as pl
from jax.experimental.pallas import tpu as pltpu

F32 = jnp.float32
BF16 = jnp.bfloat16

D_MODEL = 1024
DEPTH = 2
N_DEV = 8
ATT_HEADS = 16
ATT_KV_HEADS = 2
ATT_HEAD_DIM = 64
ATT_BLOCK = 128
SSD_D_INNER = 2048
SSD_HEADS = 32
SSD_GROUPS = 4
SSD_STATE = 128
SSD_CHUNK = 128
FFN_HIDDEN = 2816
LN_EPS = 1e-5
RMS_EPS = 1e-5
ALPHA = (2 * DEPTH) ** 0.25
Q_DIM = 1024
KV_DIM = 128
BC_DIM = 512
IN_DIM = 8480
IN_SHARD = IN_DIM // N_DEV
DT_PAD = 512

ADAM_LR = 0.001
ADAM_B1 = 0.9
ADAM_B2 = 0.999
ADAM_EPS = 1e-08
ADAM_WD = 0.01
ADAM_STEP = 10

LANE = 128
VMEM_LIMIT = 48 * 1024 * 1024
PACK_W = 1024
NEG = -1e30

_NN = (((1,), (0,)), ((), ()))
_NT = (((1,), (1,)), ((), ()))
_TN = (((0,), (0,)), ((), ()))
MESH_ID = pl.DeviceIdType.MESH


def _dot(a, b, dims=_NN):
    return lax.dot_general(a, b, dims, preferred_element_type=F32)


def _dot_hi(a, b):
    return lax.dot_general(a, b, _NN, preferred_element_type=F32, precision=lax.Precision.HIGHEST)


def _sig(x):
    return 1.0 / (1.0 + jnp.exp(-x))


def _softplus(x):
    return jnp.maximum(x, 0.0) + jnp.log(1.0 + jnp.exp(-jnp.abs(x)))


def _cparams(*sem):
    return pltpu.CompilerParams(dimension_semantics=sem, vmem_limit_bytes=VMEM_LIMIT)


def _pick(n, cap):
    if n <= cap:
        return n
    best = None
    for t in range(LANE, cap + 1, LANE):
        if n % t == 0:
            best = t
    assert best is not None, (n, cap)
    return best


def _tile(n):
    if n <= 1024 or n % 1024 == 0:
        return min(n, 1024)
    return _pick(n, 1408)


def _rows(n):
    return min(512, n)


def _mm(a, b, mode, name, add=None, add_scale=1.0, out_dtype=F32, dep=None):
    if mode == "nn":
        m, k = a.shape
        n = b.shape[1]
    elif mode == "nt":
        m, k = a.shape
        n = b.shape[0]
    else:
        k, m = a.shape
        n = b.shape[1]
    tm = _tile(m)
    tn = _tile(n)
    tk = _tile(k)
    nk = k // tk
    has_add = add is not None
    dims = {"nn": _NN, "nt": _NT, "tn": _TN}[mode]

    def body(*refs):
        if dep is not None:
            refs = refs[:-3] + refs[-2:]
        if has_add:
            a_ref, b_ref, add_ref, o_ref, acc_ref = refs
        else:
            a_ref, b_ref, o_ref, acc_ref = refs
        kk = pl.program_id(2)

        @pl.when(kk == 0)
        def _():
            if has_add:
                acc_ref[...] = add_scale * add_ref[...].astype(F32)
            else:
                acc_ref[...] = jnp.zeros_like(acc_ref)

        acc_ref[...] += _dot(a_ref[...].astype(BF16), b_ref[...].astype(BF16), dims)

        @pl.when(kk == nk - 1)
        def _():
            o_ref[...] = acc_ref[...].astype(o_ref.dtype)

    if mode == "nn":
        a_spec = pl.BlockSpec((tm, tk), lambda i, j, kk: (i, kk))
        b_spec = pl.BlockSpec((tk, tn), lambda i, j, kk: (kk, j))
    elif mode == "nt":
        a_spec = pl.BlockSpec((tm, tk), lambda i, j, kk: (i, kk))
        b_spec = pl.BlockSpec((tn, tk), lambda i, j, kk: (j, kk))
    else:
        a_spec = pl.BlockSpec((tk, tm), lambda i, j, kk: (kk, i))
        b_spec = pl.BlockSpec((tk, tn), lambda i, j, kk: (kk, j))
    o_spec = pl.BlockSpec((tm, tn), lambda i, j, kk: (i, j))
    in_specs = [a_spec, b_spec] + ([o_spec] if has_add else [])
    args = (a, b) + ((add,) if has_add else ())
    if dep is not None:
        in_specs.append(pl.BlockSpec((8, LANE), lambda i, j, kk: (0, 0)))
        args += (dep,)
    return pl.pallas_call(
        body, name=name, grid=(m // tm, n // tn, nk),
        in_specs=in_specs, out_specs=o_spec,
        out_shape=jax.ShapeDtypeStruct((m, n), out_dtype),
        scratch_shapes=[pltpu.VMEM((tm, tn), F32)],
        compiler_params=_cparams("parallel", "parallel", "arbitrary"),
    )(*args)


def _vec_spec(width):
    return pl.BlockSpec((1, width), lambda i: (0, 0))


def _ln_fwd(a, b, gamma, beta, alpha, name):
    n_rows, dm = a.shape
    has_b = b is not None

    def body(*refs):
        if has_b:
            a_ref, b_ref, g_ref, be_ref, o_ref = refs
            u = alpha * a_ref[...] + b_ref[...]
        else:
            a_ref, g_ref, be_ref, o_ref = refs
            u = a_ref[...]
        mu = jnp.mean(u, axis=-1, keepdims=True)
        d = u - mu
        var = jnp.mean(d * d, axis=-1, keepdims=True)
        o_ref[...] = d * lax.rsqrt(var + LN_EPS) * g_ref[...] + be_ref[...]

    row = pl.BlockSpec((_rows(n_rows),dm), lambda i: (i, 0))
    in_specs = [row] + ([row] if has_b else []) + [_vec_spec(dm), _vec_spec(dm)]
    args = (a,) + ((b,) if has_b else ()) + (gamma.reshape(1, dm), beta.reshape(1, dm))
    return pl.pallas_call(
        body, name=name, grid=(n_rows // _rows(n_rows),), in_specs=in_specs, out_specs=row,
        out_shape=jax.ShapeDtypeStruct((n_rows, dm), F32),
        compiler_params=_cparams("parallel"),
    )(*args)


def _ln_bwd(a, b, gamma, dy, alpha, name):
    n_rows, dm = a.shape
    has_b = b is not None

    def body(*refs):
        if has_b:
            a_ref, b_ref, g_ref, dy_ref, du_ref, acc_ref = refs
            u = alpha * a_ref[...] + b_ref[...]
        else:
            a_ref, g_ref, dy_ref, du_ref, acc_ref = refs
            u = a_ref[...]

        @pl.when(pl.program_id(0) == 0)
        def _():
            acc_ref[...] = jnp.zeros_like(acc_ref)

        mu = jnp.mean(u, axis=-1, keepdims=True)
        d = u - mu
        var = jnp.mean(d * d, axis=-1, keepdims=True)
        rstd = lax.rsqrt(var + LN_EPS)
        xhat = d * rstd
        dyv = dy_ref[...]
        acc_ref[0:1, :] += jnp.sum(dyv * xhat, axis=0, keepdims=True)
        acc_ref[1:2, :] += jnp.sum(dyv, axis=0, keepdims=True)
        dxh = dyv * g_ref[...]
        m1 = jnp.mean(dxh, axis=-1, keepdims=True)
        m2 = jnp.mean(dxh * xhat, axis=-1, keepdims=True)
        du_ref[...] = rstd * (dxh - m1 - xhat * m2)

    row = pl.BlockSpec((_rows(n_rows),dm), lambda i: (i, 0))
    in_specs = [row] + ([row] if has_b else []) + [_vec_spec(dm), row]
    args = (a,) + ((b,) if has_b else ()) + (gamma.reshape(1, dm), dy)
    return pl.pallas_call(
        body, name=name, grid=(n_rows // _rows(n_rows),), in_specs=in_specs,
        out_specs=(row, pl.BlockSpec((8, dm), lambda i: (0, 0))),
        out_shape=(jax.ShapeDtypeStruct((n_rows, dm), F32), jax.ShapeDtypeStruct((8, dm), F32)),
        compiler_params=_cparams("arbitrary"),
    )(*args)


def _loss_fwd_bwd(y, target, name):
    n_rows, dm = y.shape

    def body(y_ref, t_ref, acc_ref, dy_ref):
        @pl.when(pl.program_id(0) == 0)
        def _():
            acc_ref[...] = jnp.zeros_like(acc_ref)

        d = y_ref[...] - t_ref[...]
        acc_ref[...] += jnp.sum(d * d)
        dy_ref[...] = d * (1.0 / dm)

    row = pl.BlockSpec((_rows(n_rows),dm), lambda i: (i, 0))
    return pl.pallas_call(
        body, name=name, grid=(n_rows // _rows(n_rows),), in_specs=[row, row],
        out_specs=(pl.BlockSpec((8, LANE), lambda i: (0, 0)), row),
        out_shape=(jax.ShapeDtypeStruct((8, LANE), F32), jax.ShapeDtypeStruct((n_rows, dm), F32)),
        compiler_params=_cparams("arbitrary"),
    )(y, target)


def _swiglu_fwd(g, u, name):
    n_rows, w = g.shape
    tw = _pick(w, 1408)

    def body(g_ref, u_ref, o_ref):
        gv = g_ref[...]
        o_ref[...] = (gv * _sig(gv) * u_ref[...]).astype(BF16)

    blk = pl.BlockSpec((_rows(n_rows),tw), lambda i, j: (i, j))
    return pl.pallas_call(
        body, name=name, grid=(n_rows // _rows(n_rows), w // tw), in_specs=[blk, blk], out_specs=blk,
        out_shape=jax.ShapeDtypeStruct((n_rows, w), BF16),
        compiler_params=_cparams("parallel", "parallel"),
    )(g, u)


def _swiglu_bwd(g, u, dact, name):
    n_rows, w = g.shape
    tw = _pick(w, 1408)

    def body(g_ref, u_ref, da_ref, dg_ref, du_ref):
        gv = g_ref[...]
        s = _sig(gv)
        da = da_ref[...]
        dg_ref[...] = (da * u_ref[...] * (s * (1.0 + gv * (1.0 - s)))).astype(BF16)
        du_ref[...] = (da * gv * s).astype(BF16)

    blk = pl.BlockSpec((_rows(n_rows),tw), lambda i, j: (i, j))
    return pl.pallas_call(
        body, name=name, grid=(n_rows // _rows(n_rows), w // tw), in_specs=[blk, blk, blk], out_specs=(blk, blk),
        out_shape=(jax.ShapeDtypeStruct((n_rows, w), BF16), jax.ShapeDtypeStruct((n_rows, w), BF16)),
        compiler_params=_cparams("parallel", "parallel"),
    )(g, u, dact)


def _merge_fwd(gl, ya, yb, name):
    n_rows, dm = ya.shape

    def body(gl_ref, ya_ref, yb_ref, o_ref):
        ga = _sig(gl_ref[:, :dm])
        gb = _sig(gl_ref[:, dm:])
        o_ref[...] = (ga * ya_ref[...] + gb * yb_ref[...]).astype(BF16)

    row = pl.BlockSpec((_rows(n_rows),dm), lambda i: (i, 0))
    row2 = pl.BlockSpec((_rows(n_rows),2 * dm), lambda i: (i, 0))
    return pl.pallas_call(
        body, name=name, grid=(n_rows // _rows(n_rows),), in_specs=[row2, row, row], out_specs=row,
        out_shape=jax.ShapeDtypeStruct((n_rows, dm), BF16),
        compiler_params=_cparams("parallel"),
    )(gl, ya, yb)


def _merge_bwd(gl, ya, yb, dmerged, name):
    n_rows, dm = ya.shape

    def body(gl_ref, ya_ref, yb_ref, dm_ref, dya_ref, dyb_ref, dgl_ref):
        ga = _sig(gl_ref[:, :dm])
        gb = _sig(gl_ref[:, dm:])
        dmv = dm_ref[...]
        dya_ref[...] = (dmv * ga).astype(BF16)
        dyb_ref[...] = (dmv * gb).astype(BF16)
        dgl_ref[:, :dm] = (dmv * ya_ref[...] * ga * (1.0 - ga)).astype(BF16)
        dgl_ref[:, dm:] = (dmv * yb_ref[...] * gb * (1.0 - gb)).astype(BF16)

    row = pl.BlockSpec((_rows(n_rows),dm), lambda i: (i, 0))
    row2 = pl.BlockSpec((_rows(n_rows),2 * dm), lambda i: (i, 0))
    return pl.pallas_call(
        body, name=name, grid=(n_rows // _rows(n_rows),), in_specs=[row2, row, row, row], out_specs=(row, row, row2),
        out_shape=(jax.ShapeDtypeStruct((n_rows, dm), BF16), jax.ShapeDtypeStruct((n_rows, dm), BF16),
                   jax.ShapeDtypeStruct((n_rows, 2 * dm), BF16)),
        compiler_params=_cparams("parallel"),
    )(gl, ya, yb, dmerged)


CONV_TAPS = 4
CONV_COLS = 512
HALO = 8


def _shift_down(cur, prev8, s, row8):
    r = pltpu.roll(cur, s, axis=0)
    top = jnp.where(row8 < s, pltpu.roll(prev8, s, axis=0), r[0:HALO])
    return jnp.concatenate([top, r[HALO:]], axis=0)


def _shift_up(cur, next8, s, row8):
    n = cur.shape[0]
    r = pltpu.roll(cur, n - s, axis=0)
    bot = jnp.where(row8 >= HALO - s, pltpu.roll(next8, HALO - s, axis=0), r[n - HALO:])
    return jnp.concatenate([r[:n - HALO], bot], axis=0)


def _conv_pre(u_ref, prev_ref, w_ref, b_ref, li):
    cur = u_ref[...]
    prev8 = jnp.where(li == 0, 0.0, prev_ref[...])
    row8 = lax.broadcasted_iota(jnp.int32, prev8.shape, 0)
    shifted = [cur] + [_shift_down(cur, prev8, s, row8) for s in range(1, CONV_TAPS)]
    acc = b_ref[...] + shifted[0] * w_ref[CONV_TAPS - 1:CONV_TAPS, :]
    for s in range(1, CONV_TAPS):
        acc = acc + shifted[s] * w_ref[CONV_TAPS - 1 - s:CONV_TAPS - s, :]
    return acc, shifted


def _conv_specs(n_rows, tl):
    cur = pl.BlockSpec((tl, CONV_COLS), lambda cj, li: (li, cj))
    prev = pl.BlockSpec((HALO, CONV_COLS), lambda cj, li: (jnp.maximum(li * (tl // HALO) - 1, 0), cj))
    nxt = pl.BlockSpec((HALO, CONV_COLS),
                       lambda cj, li: (jnp.minimum((li + 1) * (tl // HALO), n_rows // HALO - 1), cj))
    par = pl.BlockSpec((8, CONV_COLS), lambda cj, li: (0, cj))
    return cur, prev, nxt, par


def _conv_fwd(u, w8, b8, name):
    n_rows, c = u.shape
    tl = _rows(n_rows)
    cur, prev, _, par = _conv_specs(n_rows, tl)

    def body(u_ref, prev_ref, w_ref, b_ref, o_ref):
        acc, _ = _conv_pre(u_ref, prev_ref, w_ref, b_ref[0:1, :], pl.program_id(1))
        o_ref[...] = acc * _sig(acc)

    return pl.pallas_call(
        body, name=name, grid=(c // CONV_COLS, n_rows // tl), in_specs=[cur, prev, par, par], out_specs=cur,
        out_shape=jax.ShapeDtypeStruct((n_rows, c), F32),
        compiler_params=_cparams("parallel", "parallel"),
    )(u, u, w8, b8)


def _conv_bwd_pre(u, w8, b8, dout, name):
    n_rows, c = u.shape
    tl = _rows(n_rows)
    cur, prev, _, par = _conv_specs(n_rows, tl)

    def body(u_ref, prev_ref, w_ref, b_ref, do_ref, dc_ref, acc_ref):
        @pl.when(pl.program_id(1) == 0)
        def _():
            acc_ref[...] = jnp.zeros_like(acc_ref)

        acc, shifted = _conv_pre(u_ref, prev_ref, w_ref, b_ref[0:1, :], pl.program_id(1))
        sg = _sig(acc)
        dc = do_ref[...] * (sg * (1.0 + acc * (1.0 - sg)))
        dc_ref[...] = dc
        for k in range(CONV_TAPS):
            acc_ref[k:k + 1, :] += jnp.sum(dc * shifted[CONV_TAPS - 1 - k], axis=0, keepdims=True)
        acc_ref[CONV_TAPS:CONV_TAPS + 1, :] += jnp.sum(dc, axis=0, keepdims=True)

    return pl.pallas_call(
        body, name=name, grid=(c // CONV_COLS, n_rows // tl), in_specs=[cur, prev, par, par, cur],
        out_specs=(cur, par),
        out_shape=(jax.ShapeDtypeStruct((n_rows, c), F32), jax.ShapeDtypeStruct((8, c), F32)),
        compiler_params=_cparams("parallel", "arbitrary"),
    )(u, u, w8, b8, dout)


def _conv_bwd_in(dc, w8, name):
    n_rows, c = dc.shape
    tl = _rows(n_rows)
    cur, _, nxt, par = _conv_specs(n_rows, tl)
    n_l = n_rows // tl

    def body(dc_ref, next_ref, w_ref, o_ref):
        cur_v = dc_ref[...]
        next8 = jnp.where(pl.program_id(1) == n_l - 1, 0.0, next_ref[...])
        row8 = lax.broadcasted_iota(jnp.int32, next8.shape, 0)
        acc = cur_v * w_ref[CONV_TAPS - 1:CONV_TAPS, :]
        for s in range(1, CONV_TAPS):
            acc = acc + _shift_up(cur_v, next8, s, row8) * w_ref[CONV_TAPS - 1 - s:CONV_TAPS - s, :]
        o_ref[...] = acc.astype(BF16)

    return pl.pallas_call(
        body, name=name, grid=(c // CONV_COLS, n_l), in_specs=[cur, nxt, par], out_specs=cur,
        out_shape=jax.ShapeDtypeStruct((n_rows, c), BF16),
        compiler_params=_cparams("parallel", "parallel"),
    )(dc, dc, w8)


NORM_GROUP = SSD_D_INNER // SSD_GROUPS


def _gnorm_fwd(y, z, w, name):
    n_rows, c = y.shape

    def body(y_ref, z_ref, w_ref, o_ref):
        zv = z_ref[...]
        yg = y_ref[...] * (zv * _sig(zv))
        r = lax.rsqrt(jnp.mean(yg * yg, axis=-1, keepdims=True) + RMS_EPS)
        o_ref[...] = (yg * r * w_ref[...]).astype(BF16)

    blk = pl.BlockSpec((_rows(n_rows),NORM_GROUP), lambda i, j: (i, j))
    wspec = pl.BlockSpec((1, NORM_GROUP), lambda i, j: (0, j))
    return pl.pallas_call(
        body, name=name, grid=(n_rows // _rows(n_rows), c // NORM_GROUP), in_specs=[blk, blk, wspec], out_specs=blk,
        out_shape=jax.ShapeDtypeStruct((n_rows, c), BF16),
        compiler_params=_cparams("parallel", "parallel"),
    )(y, z, w.reshape(1, c))


def _gnorm_bwd(y, z, w, dyn, name):
    n_rows, c = y.shape

    def body(y_ref, z_ref, w_ref, dn_ref, dy_ref, dz_ref, acc_ref):
        @pl.when(pl.program_id(1) == 0)
        def _():
            acc_ref[...] = jnp.zeros_like(acc_ref)

        zv = z_ref[...]
        yv = y_ref[...]
        sz = _sig(zv)
        silu = zv * sz
        yg = yv * silu
        r = lax.rsqrt(jnp.mean(yg * yg, axis=-1, keepdims=True) + RMS_EPS)
        nrm = yg * r
        dn = dn_ref[...]
        acc_ref[0:1, :] += jnp.sum(dn * nrm, axis=0, keepdims=True)
        dnw = dn * w_ref[...]
        dyg = r * (dnw - nrm * jnp.mean(dnw * nrm, axis=-1, keepdims=True))
        dy_ref[...] = dyg * silu
        dz_ref[...] = (dyg * yv * (sz * (1.0 + zv * (1.0 - sz)))).astype(BF16)

    blk = pl.BlockSpec((_rows(n_rows),NORM_GROUP), lambda j, i: (i, j))
    wspec = pl.BlockSpec((1, NORM_GROUP), lambda j, i: (0, j))
    aspec = pl.BlockSpec((8, NORM_GROUP), lambda j, i: (0, j))
    return pl.pallas_call(
        body, name=name, grid=(c // NORM_GROUP, n_rows // _rows(n_rows)), in_specs=[blk, blk, wspec, blk],
        out_specs=(blk, blk, aspec),
        out_shape=(jax.ShapeDtypeStruct((n_rows, c), F32), jax.ShapeDtypeStruct((n_rows, c), BF16),
                   jax.ShapeDtypeStruct((8, c), F32)),
        compiler_params=_cparams("parallel", "arbitrary"),
    )(y, z, w.reshape(1, c), dyn)


ATT_SCALE = ATT_HEAD_DIM ** -0.5
ATT_SLOPES = [2.0 ** (-8.0 * (h + 1) / ATT_HEADS) for h in range(ATT_HEADS)]
Q_PER_KV = ATT_HEADS // ATT_KV_HEADS


def _dup_half(t, g, lo):
    tr = pltpu.roll(t, ATT_HEAD_DIM, axis=1)
    return jnp.where(lo, t, tr) if g == 0 else jnp.where(lo, tr, t)


def _att_band(kv_ref, kvp_ref, n):
    cur = kv_ref[...]
    prev = jnp.where(n == 0, 0.0, kvp_ref[...])
    lo = lax.broadcasted_iota(jnp.int32, (ATT_BLOCK, LANE), 1) < ATT_HEAD_DIM
    bands = []
    for g in range(ATT_KV_HEADS):
        kb = jnp.concatenate([_dup_half(prev[:, :LANE], g, lo), _dup_half(cur[:, :LANE], g, lo)], axis=0)
        vb = jnp.concatenate([_dup_half(prev[:, LANE:], g, lo), _dup_half(cur[:, LANE:], g, lo)], axis=0)
        bands.append((kb.astype(BF16), vb.astype(BF16)))
    return bands


def _att_probs(qh, kb, sink, h, n):
    s = _dot(qh, kb, _NT)
    qi = lax.broadcasted_iota(jnp.int32, s.shape, 0)
    kj = lax.broadcasted_iota(jnp.int32, s.shape, 1)
    rel = qi + ATT_BLOCK - kj
    valid = (rel >= 0) & (rel < ATT_BLOCK) & ((kj >= ATT_BLOCK) | (n > 0))
    s = jnp.where(valid, s - ATT_SLOPES[h] * rel.astype(F32), NEG)
    m = jnp.maximum(jnp.max(s, axis=-1, keepdims=True), sink)
    p = jnp.exp(s - m)
    es = jnp.exp(sink - m)
    inv = 1.0 / (jnp.sum(p, axis=-1, keepdims=True) + es)
    return p * inv, es * inv


def _att_fwd(q, kv, sinks8, name):
    n_rows = q.shape[0]
    nb = n_rows // ATT_BLOCK

    def body(q_ref, kv_ref, kvp_ref, s_ref, o_ref):
        n = pl.program_id(0)
        bands = _att_band(kv_ref, kvp_ref, n)
        lo = lax.broadcasted_iota(jnp.int32, (ATT_BLOCK, LANE), 1) < ATT_HEAD_DIM
        for j in range(ATT_HEADS // 2):
            qp = q_ref[:, j * LANE:(j + 1) * LANE] * ATT_SCALE
            outs = []
            for half in range(2):
                h = 2 * j + half
                kb, vb = bands[h // Q_PER_KV]
                msk = lo if half == 0 else jnp.logical_not(lo)
                qh = jnp.where(msk, qp, 0.0).astype(BF16)
                p, _ = _att_probs(qh, kb, s_ref[0:1, h:h + 1], h, n)
                outs.append(_dot(p.astype(BF16), vb))
            o_ref[:, j * LANE:(j + 1) * LANE] = jnp.where(lo, outs[0], outs[1]).astype(BF16)

    return pl.pallas_call(
        body, name=name, grid=(nb,),
        in_specs=[pl.BlockSpec((ATT_BLOCK, Q_DIM), lambda n: (n, 0)),
                  pl.BlockSpec((ATT_BLOCK, 2 * LANE), lambda n: (n, 0)),
                  pl.BlockSpec((ATT_BLOCK, 2 * LANE), lambda n: (jnp.maximum(n - 1, 0), 0)),
                  pl.BlockSpec((8, LANE), lambda n: (0, 0))],
        out_specs=pl.BlockSpec((ATT_BLOCK, Q_DIM), lambda n: (n, 0)),
        out_shape=jax.ShapeDtypeStruct((n_rows, Q_DIM), BF16),
        compiler_params=_cparams("parallel"),
    )(q, kv, kv, sinks8)


def _att_bwd(q, kv, sinks8, dout, name):
    n_rows = q.shape[0]
    nb = n_rows // ATT_BLOCK

    def body(q_ref, kv_ref, kvp_ref, s_ref, do_ref, dq_ref, dkv_ref, acc_ref, carry_ref):
        n = pl.program_id(0)

        @pl.when(n == 0)
        def _():
            acc_ref[...] = jnp.zeros_like(acc_ref)
            carry_ref[...] = jnp.zeros_like(carry_ref)

        @pl.when(n == nb)
        def _():
            dkv_ref[...] = carry_ref[...].astype(BF16)

        @pl.when(n < nb)
        def _():
            bands = _att_band(kv_ref, kvp_ref, n)
            lo = lax.broadcasted_iota(jnp.int32, (ATT_BLOCK, LANE), 1) < ATT_HEAD_DIM
            lane1 = lax.broadcasted_iota(jnp.int32, (1, LANE), 1)
            dk_acc = [jnp.zeros((2 * ATT_BLOCK, LANE), F32) for _ in range(ATT_KV_HEADS)]
            dv_acc = [jnp.zeros((2 * ATT_BLOCK, LANE), F32) for _ in range(ATT_KV_HEADS)]
            dsink = jnp.zeros((1, LANE), F32)
            for j in range(ATT_HEADS // 2):
                qp = q_ref[:, j * LANE:(j + 1) * LANE] * ATT_SCALE
                dop = do_ref[:, j * LANE:(j + 1) * LANE].astype(F32)
                dqs = []
                for half in range(2):
                    h = 2 * j + half
                    g = h // Q_PER_KV
                    kb, vb = bands[g]
                    msk = lo if half == 0 else jnp.logical_not(lo)
                    qh = jnp.where(msk, qp, 0.0).astype(BF16)
                    doh = jnp.where(msk, dop, 0.0).astype(BF16)
                    p, ps = _att_probs(qh, kb, s_ref[0:1, h:h + 1], h, n)
                    dp = _dot(doh, vb, _NT)
                    delta = jnp.sum(p * dp, axis=-1, keepdims=True)
                    ds = p * (dp - delta)
                    dsink = jnp.where(lane1 == h, -jnp.sum(ps * delta), dsink)
                    ds_b = ds.astype(BF16)
                    dqs.append(_dot(ds_b, kb) * ATT_SCALE)
                    dk_acc[g] = dk_acc[g] + _dot(ds.T.astype(BF16), qh)
                    dv_acc[g] = dv_acc[g] + _dot(p.T.astype(BF16), doh)
                dq_ref[:, j * LANE:(j + 1) * LANE] = jnp.where(lo, dqs[0], dqs[1]).astype(BF16)
            acc_ref[0:1, :] += dsink
            lo2 = lax.broadcasted_iota(jnp.int32, (2 * ATT_BLOCK, LANE), 1) < ATT_HEAD_DIM
            folded = []
            for acc in (dk_acc, dv_acc):
                t0 = acc[0] + pltpu.roll(acc[0], ATT_HEAD_DIM, axis=1)
                t1 = acc[1] + pltpu.roll(acc[1], ATT_HEAD_DIM, axis=1)
                folded.append(jnp.where(lo2, t0, t1))
            band = jnp.concatenate(folded, axis=1)
            dkv_ref[...] = (carry_ref[...] + band[:ATT_BLOCK]).astype(BF16)
            carry_ref[...] = band[ATT_BLOCK:]

    def qmap(n):
        return (jnp.minimum(n, nb - 1), 0)

    return pl.pallas_call(
        body, name=name, grid=(nb + 1,),
        in_specs=[pl.BlockSpec((ATT_BLOCK, Q_DIM), qmap),
                  pl.BlockSpec((ATT_BLOCK, 2 * LANE), qmap),
                  pl.BlockSpec((ATT_BLOCK, 2 * LANE), lambda n: (jnp.maximum(jnp.minimum(n, nb - 1) - 1, 0), 0)),
                  pl.BlockSpec((8, LANE), lambda n: (0, 0)),
                  pl.BlockSpec((ATT_BLOCK, Q_DIM), qmap)],
        out_specs=(pl.BlockSpec((ATT_BLOCK, Q_DIM), qmap),
                   pl.BlockSpec((ATT_BLOCK, 2 * LANE), lambda n: (jnp.maximum(n - 1, 0), 0)),
                   pl.BlockSpec((8, LANE), lambda n: (0, 0))),
        out_shape=(jax.ShapeDtypeStruct((n_rows, Q_DIM), BF16), jax.ShapeDtypeStruct((n_rows, 2 * LANE), BF16),
                   jax.ShapeDtypeStruct((8, LANE), F32)),
        scratch_shapes=[pltpu.VMEM((ATT_BLOCK, 2 * LANE), F32)],
        compiler_params=_cparams("arbitrary"),
    )(q, kv, kv, sinks8, dout)


HEADS_PER_GROUP = SSD_HEADS // SSD_GROUPS
PAIRS_PER_GROUP = HEADS_PER_GROUP // 2
T = SSD_CHUNK


def _ssd_scalars(dtr_ref, par_ref):
    dt = _softplus(dtr_ref[...] + par_ref[0:1, :])
    a = -jnp.exp(par_ref[1:2, :])
    ri = lax.broadcasted_iota(jnp.int32, (T, T), 0)
    ci = lax.broadcasted_iota(jnp.int32, (T, T), 1)
    tril = (ri >= ci).astype(F32)
    cs = _dot_hi(tril, dt * a)
    cst = cs.T
    return dt, a, cs, cst, ri, ci


def _lane_pick(lo, arr, k0):
    return jnp.where(lo, arr[:, k0:k0 + 1], arr[:, k0 + 1:k0 + 2])


def _ssd_fwd(xs, bm, cm, dtr, par, name):
    n_rows = xs.shape[0]
    nc = n_rows // T
    gw = PAIRS_PER_GROUP * LANE

    def body(x_ref, b_ref, c_ref, dtr_ref, par_ref, y_ref, hs_ref, h_ref):
        @pl.when(pl.program_id(1) == 0)
        def _():
            h_ref[...] = jnp.zeros_like(h_ref)

        dt, a, cs, cst, ri, ci = _ssd_scalars(dtr_ref, par_ref)
        tri = ri >= ci
        lo = lax.broadcasted_iota(jnp.int32, (T, LANE), 1) < SSD_CHUNK // 2
        ecs = jnp.exp(cs)
        dect = jnp.exp(cst[:, T - 1:T] - cst)
        etot = jnp.exp(cs[T - 1:T, :])
        bg = b_ref[...]
        cg = c_ref[...]
        bgt = bg.T
        cb = _dot(cg.astype(BF16), bg.astype(BF16), _NT)
        for j in range(PAIRS_PER_GROUP):
            xp = x_ref[:, j * LANE:(j + 1) * LANE]
            xdt = (xp * _lane_pick(lo, dt, 2 * j)).astype(BF16)
            hp = h_ref[j]
            hs_ref[0, 0, j] = hp
            hp_b = hp.astype(BF16)
            ys, ss = [], []
            for half in range(2):
                k = 2 * j + half
                lm = jnp.exp(jnp.where(tri, cs[:, k:k + 1] - cst[k:k + 1, :], NEG))
                yh = _dot((lm * cb).astype(BF16), xdt) + _dot((cg * ecs[:, k:k + 1]).astype(BF16), hp_b)
                ys.append(yh)
                ss.append(_dot((bgt * dect[k:k + 1, :]).astype(BF16), xdt))
            dsk = jnp.where(lo[0:1, :], par_ref[2:3, 2 * j:2 * j + 1], par_ref[2:3, 2 * j + 1:2 * j + 2])
            y_ref[:, j * LANE:(j + 1) * LANE] = jnp.where(lo, ys[0], ys[1]) + dsk * xp
            et = jnp.where(lo[0:1, :], etot[:, 2 * j:2 * j + 1], etot[:, 2 * j + 1:2 * j + 2])
            h_ref[j] = hp * et + jnp.where(lo, ss[0], ss[1])

    return pl.pallas_call(
        body, name=name, grid=(SSD_GROUPS, nc),
        in_specs=[pl.BlockSpec((T, gw), lambda g, c: (c, g)),
                  pl.BlockSpec((T, SSD_STATE), lambda g, c: (c, g)),
                  pl.BlockSpec((T, SSD_STATE), lambda g, c: (c, g)),
                  pl.BlockSpec((T, LANE), lambda g, c: (c, g)),
                  pl.BlockSpec((8, LANE), lambda g, c: (0, g))],
        out_specs=(pl.BlockSpec((T, gw), lambda g, c: (c, g)),
                   pl.BlockSpec((1, 1, PAIRS_PER_GROUP, SSD_STATE, LANE), lambda g, c: (g, c, 0, 0, 0))),
        out_shape=(jax.ShapeDtypeStruct((n_rows, SSD_D_INNER), F32),
                   jax.ShapeDtypeStruct((SSD_GROUPS, nc, PAIRS_PER_GROUP, SSD_STATE, LANE), F32)),
        scratch_shapes=[pltpu.VMEM((PAIRS_PER_GROUP, SSD_STATE, LANE), F32)],
        compiler_params=_cparams("parallel", "arbitrary"),
    )(xs, bm, cm, dtr, par)


def _ssd_bwd(xs, bm, cm, dtr, par, hs, dy, name):
    n_rows = xs.shape[0]
    nc = n_rows // T
    gw = PAIRS_PER_GROUP * LANE

    def body(x_ref, b_ref, c_ref, dtr_ref, par_ref, hs_ref, dy_ref,
             dx_ref, db_ref, dc_ref, ddtr_ref, acc_ref, dh_ref):
        @pl.when(pl.program_id(1) == 0)
        def _():
            dh_ref[...] = jnp.zeros_like(dh_ref)
            acc_ref[...] = jnp.zeros_like(acc_ref)

        dt, a, cs, cst, ri, ci = _ssd_scalars(dtr_ref, par_ref)
        tri = ri >= ci
        trit = ci >= ri
        lane = lax.broadcasted_iota(jnp.int32, (T, LANE), 1)
        lo = lane < SSD_CHUNK // 2
        lane1 = lane[0:1, :]
        ecs = jnp.exp(cs)
        ecst = jnp.exp(cst)
        dec = jnp.exp(cs[T - 1:T, :] - cs)
        etot = jnp.exp(cs[T - 1:T, :])
        bg = b_ref[...]
        cg = c_ref[...]
        bg_b = bg.astype(BF16)
        cg_b = cg.astype(BF16)
        cgt = cg.T
        cb = _dot(cg_b, bg_b, _NT)
        cbt = _dot(bg_b, cg_b, _NT)
        dbg = jnp.zeros((T, SSD_STATE), F32)
        dcg = jnp.zeros((T, SSD_STATE), F32)
        dcs_acc = jnp.zeros((T, LANE), F32)
        ddt_acc = jnp.zeros((T, LANE), F32)
        dsk_acc = jnp.zeros((1, LANE), F32)
        last_row = lax.broadcasted_iota(jnp.int32, (T, 1), 0) == T - 1
        for j in range(PAIRS_PER_GROUP):
            xp = x_ref[:, j * LANE:(j + 1) * LANE]
            dtl = _lane_pick(lo, dt, 2 * j)
            xdt = xp * dtl
            hp = hs_ref[0, 0, j]
            dhn = dh_ref[j]
            dyp = dy_ref[:, j * LANE:(j + 1) * LANE]
            dxdt = jnp.zeros((T, LANE), F32)
            et = jnp.where(lo[0:1, :], etot[:, 2 * j:2 * j + 1], etot[:, 2 * j + 1:2 * j + 2])
            dh_new = dhn * et
            for half in range(2):
                k = 2 * j + half
                msk = lo if half == 0 else jnp.logical_not(lo)
                xh_f = jnp.where(msk, xdt, 0.0)
                dyh_f = jnp.where(msk, dyp, 0.0)
                hh_f = jnp.where(msk, hp, 0.0)
                dhh_f = jnp.where(msk, dhn, 0.0)
                xh, dyh, hh, dhh = (v.astype(BF16) for v in (xh_f, dyh_f, hh_f, dhh_f))
                cs_col = cs[:, k:k + 1]
                cs_row = cst[k:k + 1, :]
                lm = jnp.exp(jnp.where(tri, cs_col - cs_row, NEG))
                lmt = jnp.exp(jnp.where(trit, cs_row - cs_col, NEG))
                dm = _dot(dyh, xh, _NT)
                dmt = _dot(xh, dyh, _NT)
                mm = lm * cb
                mmt = lmt * cbt
                bdh = _dot((bg * dec[:, k:k + 1]).astype(BF16), dhh)
                dxdt = dxdt + _dot(mmt.astype(BF16), dyh) + bdh
                dcg = dcg + _dot((dm * lm).astype(BF16), bg_b) + _dot(dyh, hh, _NT) * ecs[:, k:k + 1]
                dbg = dbg + _dot((dmt * lmt).astype(BF16), cg_b) + _dot(xh, dhh, _NT) * dec[:, k:k + 1]
                dh_new = dh_new + _dot((cgt * ecst[k:k + 1, :]).astype(BF16), dyh)
                yo = _dot((cg * ecs[:, k:k + 1]).astype(BF16), hh)
                e1 = jnp.sum(dm * mm, axis=-1, keepdims=True)
                e2 = jnp.sum(dmt * mmt, axis=-1, keepdims=True)
                e3 = jnp.sum(dyh_f * yo, axis=-1, keepdims=True)
                e4 = jnp.sum(xh_f * bdh, axis=-1, keepdims=True)
                tsum = jnp.sum(e4) + etot[:, k:k + 1] * jnp.sum(hh_f * dhh_f)
                dcs_h = e1 - e2 + e3 - e4 + jnp.where(last_row, tsum, 0.0)
                dcs_acc = jnp.where(lane == k, dcs_h, dcs_acc)
                dsk_acc = jnp.where(lane1 == k, jnp.sum(dyh_f * xp), dsk_acc)
            ddt_lo = jnp.sum(jnp.where(lo, dxdt * xp, 0.0), axis=-1, keepdims=True)
            ddt_hi = jnp.sum(jnp.where(lo, 0.0, dxdt * xp), axis=-1, keepdims=True)
            ddt_acc = jnp.where(lane == 2 * j, ddt_lo, jnp.where(lane == 2 * j + 1, ddt_hi, ddt_acc))
            dsk = jnp.where(lo[0:1, :], par_ref[2:3, 2 * j:2 * j + 1], par_ref[2:3, 2 * j + 1:2 * j + 2])
            dx_ref[:, j * LANE:(j + 1) * LANE] = dxdt * dtl + dsk * dyp
            dh_ref[j] = dh_new
        db_ref[...] = dbg
        dc_ref[...] = dcg
        triu = (ci >= ri).astype(F32)
        dda = _dot_hi(triu, dcs_acc)
        ddt = ddt_acc + dda * a
        ddtr = ddt * _sig(dtr_ref[...] + par_ref[0:1, :])
        ddtr_ref[...] = ddtr.astype(BF16)
        acc_ref[0:1, :] += jnp.sum(ddtr, axis=0, keepdims=True)
        acc_ref[1:2, :] += jnp.sum(dda * dt, axis=0, keepdims=True) * a
        acc_ref[2:3, :] += dsk_acc

    def rev(g, c):
        return (nc - 1 - c, g)

    return pl.pallas_call(
        body, name=name, grid=(SSD_GROUPS, nc),
        in_specs=[pl.BlockSpec((T, gw), rev),
                  pl.BlockSpec((T, SSD_STATE), rev),
                  pl.BlockSpec((T, SSD_STATE), rev),
                  pl.BlockSpec((T, LANE), rev),
                  pl.BlockSpec((8, LANE), lambda g, c: (0, g)),
                  pl.BlockSpec((1, 1, PAIRS_PER_GROUP, SSD_STATE, LANE), lambda g, c: (g, nc - 1 - c, 0, 0, 0)),
                  pl.BlockSpec((T, gw), rev)],
        out_specs=(pl.BlockSpec((T, gw), rev),
                   pl.BlockSpec((T, SSD_STATE), rev),
                   pl.BlockSpec((T, SSD_STATE), rev),
                   pl.BlockSpec((T, LANE), rev),
                   pl.BlockSpec((8, LANE), lambda g, c: (0, g))),
        out_shape=(jax.ShapeDtypeStruct((n_rows, SSD_D_INNER), F32),
                   jax.ShapeDtypeStruct((n_rows, BC_DIM), F32),
                   jax.ShapeDtypeStruct((n_rows, BC_DIM), F32),
                   jax.ShapeDtypeStruct((n_rows, DT_PAD), BF16),
                   jax.ShapeDtypeStruct((8, DT_PAD), F32)),
        scratch_shapes=[pltpu.VMEM((PAIRS_PER_GROUP, SSD_STATE, LANE), F32)],
        compiler_params=_cparams("parallel", "arbitrary"),
    )(xs, bm, cm, dtr, par, hs, dy)


ADAM_ROWS = 256


def _adamw(lands, w, m, v, name):
    na = len(lands)
    n_slots, r, wd = lands[0].shape
    tr = r if r <= 2 * ADAM_ROWS else ADAM_ROWS
    nj = r // tr
    bc1 = 1.0 - ADAM_B1 ** ADAM_STEP
    bc2 = 1.0 - ADAM_B2 ** ADAM_STEP

    def body(*refs):
        l_refs = refs[:na]
        w_ref, m_ref, v_ref, g_ref, d_ref, nm_ref, nv_ref = refs[na:]
        for a in range(na):
            @pl.when(pl.program_id(0) == a)
            def _(l_ref=l_refs[a]):
                g = l_ref[0].astype(F32)
                for s in range(1, n_slots):
                    g = g + l_ref[s].astype(F32)
                mn = ADAM_B1 * m_ref[0] + (1.0 - ADAM_B1) * g
                vn = ADAM_B2 * v_ref[0] + (1.0 - ADAM_B2) * (g * g)
                mh = mn / bc1
                vh = vn / bc2
                g_ref[0] = g
                nm_ref[0] = mn
                nv_ref[0] = vn
                d_ref[0] = -ADAM_LR * (mh / (jnp.sqrt(vh) + ADAM_EPS) + ADAM_WD * w_ref[0])

    def land_spec(a):
        return pl.BlockSpec((n_slots, tr, wd),
                            lambda i, j: (0, jnp.where(i == a, j, jnp.where(i < a, 0, nj - 1)), 0))

    blk = pl.BlockSpec((1, tr, wd), lambda i, j: (i, j, 0))
    shp = jax.ShapeDtypeStruct((na, r, wd), F32)
    return pl.pallas_call(
        body, name=name, grid=(na, nj), in_specs=[land_spec(a) for a in range(na)] + [blk, blk, blk],
        out_specs=(blk, blk, blk, blk), out_shape=(shp, shp, shp, shp),
        compiler_params=_cparams("arbitrary", "arbitrary"),
    )(*lands, w, m, v)


def _mesh_pos():
    return lax.axis_index("x"), lax.axis_index("y"), lax.axis_index("c")


def _peer(pos, k):
    x, y, c = pos
    px = 1 - x if (k >> 2) & 1 else x
    py = 1 - y if (k >> 1) & 1 else y
    pc = 1 - c if k & 1 else c
    return px, py, pc


def _flat(pos):
    return 4 * pos[0] + 2 * pos[1] + pos[2]


HBM_SPEC = pl.BlockSpec(memory_space=pl.ANY)


ROW_SHARDED = ("w_ssd_out", "w_att_out", "w_mix_out", "w_ffn_down")
COL_SHARDED = ("w_in", "w_ffn_gate", "w_ffn_up")
GATHERED = ROW_SHARDED + COL_SHARDED + ("conv_w",)
BIG = ROW_SHARDED + COL_SHARDED


SEM_SPEC = pl.BlockSpec(memory_space=pltpu.SEMAPHORE)
TOKEN = jax.ShapeDtypeStruct((8, LANE), F32)
SPLIT_EFFECT = pltpu.SideEffectType.DATAFLOW_SIDE_EFFECTING
GATHER_ROWS = "gather_rows"
GATHER_SLOT = "gather_slot"
SCATTER_ROWS = "scatter_rows"
SCATTER_SLOT = "scatter_slot"


def _land_shape(kind, src):
    if kind == GATHER_ROWS:
        return (N_DEV * src.shape[0],) + src.shape[1:]
    if kind == GATHER_SLOT:
        return (N_DEV,) + src.shape
    if kind == SCATTER_ROWS:
        return (N_DEV, src.shape[0] // N_DEV) + src.shape[1:]
    return src.shape


def _views(kind, src_ref, land_ref, pos, k):
    me = _flat(pos)
    if kind == GATHER_ROWS:
        r = src_ref.shape[0]
        return src_ref, land_ref.at[pl.ds(pl.multiple_of(me * r, 16), r), :]
    if kind == GATHER_SLOT:
        return src_ref, land_ref.at[me]
    dev = _flat(_peer(pos, k))
    if kind == SCATTER_ROWS:
        r = land_ref.shape[1]
        return src_ref.at[pl.ds(pl.multiple_of(dev * r, 16), r), :], land_ref.at[k]
    return src_ref.at[dev], land_ref.at[k]


def _hbm(x):
    return pltpu.with_memory_space_constraint(x, pltpu.HBM)


def _exchange_start(items, after, name):
    kinds = [k for k, _ in items]
    srcs = [_hbm(s) for _, s in items]
    lands = [_hbm(lax.empty(_land_shape(k, s), s.dtype)) for k, s in items]
    n = len(items)
    n_copy = n * (N_DEV - 1)

    def body(*refs):
        src_refs, land_refs = refs[:n], refs[n:2 * n]
        send_sems, recv_sems = refs[2 * n + 1], refs[2 * n + 2]
        token_ref = refs[4 * n + 3]
        pos = _mesh_pos()
        for i, kind in enumerate(kinds):
            for k in range(1, N_DEV):
                s, d = _views(kind, src_refs[i], land_refs[i], pos, k)
                j = i * (N_DEV - 1) + k - 1
                pltpu.make_async_remote_copy(src_ref=s, dst_ref=d, send_sem=send_sems.at[j], recv_sem=recv_sems.at[j],
                                             device_id=_peer(pos, k), device_id_type=MESH_ID).start()
        token_ref[...] = jnp.zeros_like(token_ref)

    arrs = srcs + lands
    outs = pl.pallas_call(
        body, name=name,
        in_specs=[HBM_SPEC] * (2 * n + 1),
        out_specs=[SEM_SPEC, SEM_SPEC] + [HBM_SPEC] * (2 * n) + [pl.BlockSpec(memory_space=pltpu.VMEM)],
        out_shape=[pltpu.SemaphoreType.DMA((n_copy,)), pltpu.SemaphoreType.DMA((n_copy,))]
        + [pltpu.HBM(a.shape, a.dtype) for a in arrs] + [TOKEN],
        input_output_aliases={i: 2 + i for i in range(2 * n)},
        compiler_params=pltpu.CompilerParams(has_side_effects=SPLIT_EFFECT),
    )(*arrs, after)
    return {"kinds": kinds, "send": outs[0], "recv": outs[1], "arrs": outs[2:2 + 2 * n], "token": outs[-1]}


def _exchange_wait(ex, after, name):
    kinds = ex["kinds"]
    n = len(kinds)

    def body(*refs):
        src_refs, land_refs = refs[:n], refs[n:2 * n]
        send_sems, recv_sems = refs[2 * n], refs[2 * n + 1]
        token_ref, local_sems = refs[-2], refs[-1]
        pos = _mesh_pos()
        land_out = refs[3 * n + 3:4 * n + 3]
        local = []
        for i, kind in enumerate(kinds):
            s, d = _views(kind, src_refs[i], land_out[i], pos, 0)
            lc = pltpu.make_async_copy(s, d, local_sems.at[i])
            lc.start()
            local.append(lc)
        for i, kind in enumerate(kinds):
            for k in range(1, N_DEV):
                s, d = _views(kind, src_refs[i], land_refs[i], pos, k)
                j = i * (N_DEV - 1) + k - 1
                cp = pltpu.make_async_remote_copy(src_ref=s, dst_ref=d, send_sem=send_sems.at[j],
                                                  recv_sem=recv_sems.at[j], device_id=_peer(pos, k),
                                                  device_id_type=MESH_ID)
                cp.wait_send()
                cp.wait_recv()
        for lc in local:
            lc.wait()
        token_ref[...] = jnp.zeros_like(token_ref)

    outs = pl.pallas_call(
        body, name=name,
        in_specs=[HBM_SPEC] * (2 * n) + [SEM_SPEC, SEM_SPEC, HBM_SPEC],
        out_specs=[HBM_SPEC] * (2 * n) + [pl.BlockSpec(memory_space=pltpu.VMEM)],
        out_shape=[pltpu.HBM(a.shape, a.dtype) for a in ex["arrs"]] + [TOKEN],
        input_output_aliases={i: i for i in range(2 * n)},
        scratch_shapes=[pltpu.SemaphoreType.DMA((n,))],
        compiler_params=pltpu.CompilerParams(has_side_effects=SPLIT_EFFECT),
    )(*ex["arrs"], ex["send"], ex["recv"], after)
    return list(outs[n:2 * n]), outs[-1]


def _all_gather_small(x, name):
    r, w = x.shape

    def body(x_ref, out_ref, send_sems, recv_sems):
        pos = _mesh_pos()
        me = _flat(pos)
        copies = []
        for k in range(1, N_DEV):
            cp = pltpu.make_async_remote_copy(
                src_ref=x_ref, dst_ref=out_ref.at[me], send_sem=send_sems.at[k - 1], recv_sem=recv_sems.at[k - 1],
                device_id=_peer(pos, k), device_id_type=MESH_ID)
            cp.start()
            copies.append(cp)
        out_ref[me] = x_ref[...]
        for cp in copies:
            cp.wait()

    vmem = pl.BlockSpec(memory_space=pltpu.VMEM)
    return pl.pallas_call(
        body, name=name, in_specs=[vmem], out_specs=vmem,
        out_shape=jax.ShapeDtypeStruct((N_DEV, r, w), x.dtype),
        scratch_shapes=[pltpu.SemaphoreType.DMA((N_DEV - 1,)), pltpu.SemaphoreType.DMA((N_DEV - 1,))],
        compiler_params=pltpu.CompilerParams(has_side_effects=True),
    )(x)


def _cols(g, lo, hi):
    c = g.shape[-1]
    parts = []
    for d in range(N_DEV):
        a, b = max(lo, d * c), min(hi, (d + 1) * c)
        if a < b:
            parts.append(g[d, :, a - d * c:b - d * c])
    return parts[0] if len(parts) == 1 else jnp.concatenate(parts, axis=1)


def _col_chunks(g):
    c = g.shape[-1] // N_DEV
    return jnp.stack([g[:, d * c:(d + 1) * c] for d in range(N_DEV)])


IN_PART = ("w_in", "conv_w")
OUT_PART = ROW_SHARDED + ("w_ffn_gate", "w_ffn_up")


def _gather_items(w, names, l):
    items = []
    for n in names:
        blk = w[n][l] if n == "conv_w" else w[n][l].astype(BF16)
        items.append((GATHER_ROWS if n in ROW_SHARDED else GATHER_SLOT, blk))
    return items


def _scatter_items(grads, names):
    return [(SCATTER_ROWS, grads[n]) if n in ROW_SHARDED else (SCATTER_SLOT, _col_chunks(grads[n]))
            for n in names]


SMALL = ("ln_in_g", "ln_in_b", "conv_b", "dt_bias", "a_log", "d_skip", "ssd_norm_w", "att_sinks",
         "ln_mix_g", "ln_mix_b", "ln_ffn_g", "ln_ffn_b")


def _pack_small(vals):
    flat = jnp.concatenate([vals[n].reshape(-1) for n in SMALL])
    n = flat.shape[0]
    rows = -(-n // LANE)
    rows = -(-rows // 8) * 8
    return jnp.pad(flat, (0, rows * LANE - n)).reshape(rows, LANE)


def _unpack_small(buf, shapes):
    flat = buf.reshape(-1)
    off = 0
    out = {}
    for n in SMALL:
        cnt = math.prod(shapes[n])
        out[n] = flat[off:off + cnt].reshape(shapes[n])
        off += cnt
    return out


def _to_group_major(v):
    lead = v.shape[:-1]
    t = v.reshape(lead + (SSD_GROUPS, HEADS_PER_GROUP))
    t = jnp.pad(t, [(0, 0)] * len(lead) + [(0, 0), (0, LANE - HEADS_PER_GROUP)])
    return t.reshape(lead + (DT_PAD,))


def _from_group_major(v):
    lead = v.shape[:-1]
    return v.reshape(lead + (SSD_GROUPS, LANE))[..., :HEADS_PER_GROUP].reshape(lead + (SSD_HEADS,))


def _rows8(v):
    return jnp.pad(v, ((0, 8 - v.shape[0]), (0, 0)))


IN_OFFS = {"q": (0, 1024), "kv": (1024, 1280), "z": (1280, 3328), "xs": (3328, 5376), "b": (5376, 5888),
           "c": (5888, 6400), "dt": (6400, 6432), "gl": (6432, 8480)}
PIECES = ("q", "kv", "z", "xs", "b", "c", "dt", "gl")


def _split_w_in(g):
    out = {p: _cols(g, lo, hi) for p, (lo, hi) in IN_OFFS.items()}
    out["dt"] = _to_group_major(out["dt"])
    return out


def _join_dw_in(dws):
    dws = dict(dws)
    dws["dt"] = _from_group_major(dws["dt"])
    return jnp.concatenate([dws[p] for p in PIECES], axis=1)


def _params_out(W):
    p = {n: W[n] for n in ROW_SHARDED}
    for n in ("w_ffn_gate", "w_ffn_up"):
        p[n] = _cols(W[n], 0, FFN_HIDDEN)
    return p


def _params_in(l, W, sm):
    p = {"w_in": _split_w_in(W["w_in"])}
    cw = _cols(W["conv_w"], 0, SSD_D_INNER + 2 * BC_DIM)
    cb = sm["conv_b"][l]
    segs = {"xs": (0, 2048), "b": (2048, 2560), "c": (2560, 3072)}
    p["conv_w8"] = {s: _rows8(cw[:, lo:hi]) for s, (lo, hi) in segs.items()}
    p["conv_b8"] = {s: _rows8(cb[None, lo:hi]) for s, (lo, hi) in segs.items()}
    p["ssd_par"] = _rows8(jnp.stack([_to_group_major(sm["dt_bias"][l]), _to_group_major(sm["a_log"][l]),
                                     _to_group_major(sm["d_skip"][l])]))
    p["norm_w"] = sm["ssd_norm_w"][l]
    p["sinks8"] = _rows8(jnp.pad(sm["att_sinks"][l], (0, LANE - ATT_HEADS))[None])
    for n in ("ln_mix_g", "ln_mix_b", "ln_ffn_g", "ln_ffn_b"):
        p[n] = sm[n][l]
    return p


def _fwd_mixers(h0, p, l, dep=None):
    tag = f"l{l}_"
    a = {"h0": h0}
    for pc in PIECES:
        a[pc] = _mm(h0, p["w_in"][pc], "nn", tag + "proj_" + pc, dep=dep)
    for s in ("xs", "b", "c"):
        a[s + "c"] = _conv_fwd(a[s], p["conv_w8"][s], p["conv_b8"][s], tag + "conv_" + s)
    a["y"], a["hs"] = _ssd_fwd(a["xsc"], a["bc"], a["cc"], a["dt"], p["ssd_par"], tag + "ssd_fwd")
    a["yn"] = _gnorm_fwd(a["y"], a["z"], p["norm_w"], tag + "gnorm")
    a["att"] = _att_fwd(a["q"], a["kv"], p["sinks8"], tag + "att_fwd")
    return a


def _fwd_out(a, p, l, dep=None):
    tag = f"l{l}_"
    h0 = a["h0"]
    a["ya"] = _mm(a["yn"], p["w_ssd_out"], "nn", tag + "ssd_out", dep=dep)
    a["yb"] = _mm(a["att"], p["w_att_out"], "nn", tag + "att_out", dep=dep)
    a["merged"] = _merge_fwd(a["gl"], a["ya"], a["yb"], tag + "merge")
    a["mix"] = _mm(a["merged"], p["w_mix_out"], "nn", tag + "mix_out")
    a["h1"] = _ln_fwd(h0, a["mix"], p["ln_mix_g"], p["ln_mix_b"], ALPHA, tag + "ln_mix")
    a["fg"] = _mm(a["h1"], p["w_ffn_gate"], "nn", tag + "ffn_gate")
    a["fu"] = _mm(a["h1"], p["w_ffn_up"], "nn", tag + "ffn_up")
    a["act"] = _swiglu_fwd(a["fg"], a["fu"], tag + "swiglu")
    a["ffn"] = _mm(a["act"], p["w_ffn_down"], "nn", tag + "ffn_down")
    a["h2"] = _ln_fwd(a["h1"], a["ffn"], p["ln_ffn_g"], p["ln_ffn_b"], ALPHA, tag + "ln_ffn")
    return a


def _dw(x, dy, name, dep=None):
    return _mm(x, dy, "tn", name, out_dtype=BF16, dep=dep)


def _bwd_out(a, p, dh2, l, dep=None):
    tag = f"l{l}_b_"
    gw, gs = {}, {}
    du2, acc = _ln_bwd(a["h1"], a["ffn"], p["ln_ffn_g"], dh2, ALPHA, tag + "ln_ffn")
    gs["ln_ffn_g"], gs["ln_ffn_b"] = acc[0], acc[1]
    gw["w_ffn_down"] = _dw(a["act"], du2, tag + "dw_down", dep=dep)
    dact = _mm(du2, p["w_ffn_down"], "nt", tag + "dact", dep=dep)
    dfg, dfu = _swiglu_bwd(a["fg"], a["fu"], dact, tag + "swiglu")
    gw["w_ffn_gate"] = _dw(a["h1"], dfg, tag + "dw_gate")
    gw["w_ffn_up"] = _dw(a["h1"], dfu, tag + "dw_up")
    dh1 = _mm(dfg, p["w_ffn_gate"], "nt", tag + "dh1_gate", add=du2, add_scale=ALPHA)
    dh1 = _mm(dfu, p["w_ffn_up"], "nt", tag + "dh1_up", add=dh1)
    du1, acc = _ln_bwd(a["h0"], a["mix"], p["ln_mix_g"], dh1, ALPHA, tag + "ln_mix")
    gs["ln_mix_g"], gs["ln_mix_b"] = acc[0], acc[1]
    gw["w_mix_out"] = _dw(a["merged"], du1, tag + "dw_mix")
    dmerged = _mm(du1, p["w_mix_out"], "nt", tag + "dmerged")
    dya, dyb, dgl = _merge_bwd(a["gl"], a["ya"], a["yb"], dmerged, tag + "merge")
    gw["w_ssd_out"] = _dw(a["yn"], dya, tag + "dw_ssd")
    gw["w_att_out"] = _dw(a["att"], dyb, tag + "dw_att")
    return {"du1": du1, "dya": dya, "dyb": dyb, "dgl": dgl}, gw, gs


def _bwd_mixers(a, p, carry, l, dep=None):
    tag = f"l{l}_b_"
    gs = {}
    du1, dgl = carry["du1"], carry["dgl"]
    dyn = _mm(carry["dya"], p["w_ssd_out"], "nt", tag + "dyn", dep=dep)
    datt = _mm(carry["dyb"], p["w_att_out"], "nt", tag + "datt", out_dtype=BF16, dep=dep)
    dq, dkv, acc = _att_bwd(a["q"], a["kv"], p["sinks8"], datt, tag + "att")
    gs["att_sinks"] = acc[0, :ATT_HEADS]
    dy, dz, acc = _gnorm_bwd(a["y"], a["z"], p["norm_w"], dyn, tag + "gnorm")
    gs["ssd_norm_w"] = acc[0]
    dxs, dbm, dcm, ddt, acc = _ssd_bwd(a["xsc"], a["bc"], a["cc"], a["dt"], p["ssd_par"], a["hs"], dy,
                                       tag + "ssd")
    gs["dt_bias"], gs["a_log"], gs["d_skip"] = (_from_group_major(acc[i]) for i in range(3))
    dpieces = {"q": dq, "kv": dkv, "z": dz, "dt": ddt, "gl": dgl}
    dconv_w, dconv_b = [], []
    for s, dout in (("xs", dxs), ("b", dbm), ("c", dcm)):
        dc, acc = _conv_bwd_pre(a[s], p["conv_w8"][s], p["conv_b8"][s], dout, tag + "conv_pre_" + s)
        dconv_w.append(acc[:CONV_TAPS])
        dconv_b.append(acc[CONV_TAPS])
        dpieces[s] = _conv_bwd_in(dc, p["conv_w8"][s], tag + "conv_in_" + s)
    gconv = jnp.concatenate(dconv_w, axis=1)
    gs["conv_b"] = jnp.concatenate(dconv_b)
    dws = {}
    dh0 = du1
    scale = ALPHA
    for pc in PIECES:
        dws[pc] = _dw(a["h0"], dpieces[pc], tag + "dw_in_" + pc)
        dh0 = _mm(dpieces[pc], p["w_in"][pc], "nt", tag + "dh0_" + pc, add=dh0, add_scale=scale)
        scale = 1.0
    return dh0, _join_dw_in(dws), gconv, gs


def _step(x, target, w, m, v):
    x2 = x[0]
    t2 = target[0]
    tok = jnp.zeros(TOKEN.shape, TOKEN.dtype)

    ex = _exchange_start(_gather_items(w, IN_PART, 0), tok, "gather_l0_in_start")
    lands, tok = _exchange_wait(ex, ex["token"], "gather_l0_in_wait")
    p0 = _params_in(0, dict(zip(IN_PART, lands)), w)
    ex = _exchange_start(_gather_items(w, OUT_PART, 0) + _gather_items(w, IN_PART, 1), tok,
                         "gather_l0_out_l1_in_start")
    h = _ln_fwd(x2, None, w["ln_in_g"], w["ln_in_b"], 1.0, "ln_in")
    a0 = _fwd_mixers(h, p0, 0, dep=ex["token"])
    lands, tok = _exchange_wait(ex, a0["att"], "gather_l0_out_l1_in_wait")
    p0.update(_params_out(dict(zip(OUT_PART, lands))))
    p1 = _params_in(1, dict(zip(IN_PART, lands[len(OUT_PART):])), w)
    ex = _exchange_start(_gather_items(w, OUT_PART, 1), tok, "gather_l1_out_start")
    a0 = _fwd_out(a0, p0, 0, dep=ex["token"])
    lands, tok = _exchange_wait(ex, a0["h2"], "gather_l1_out_wait")
    p1.update(_params_out(dict(zip(OUT_PART, lands))))
    a1 = _fwd_out(_fwd_mixers(a0["h2"], p1, 1), p1, 1)

    sse, dh = _loss_fwd_bwd(a1["h2"], t2, "loss")
    loss = lax.psum(0.5 / D_MODEL * sse[0, 0], ("x", "y", "c"))

    carry, gw1, gs1 = _bwd_out(a1, p1, dh, 1)
    dh, gw1["w_in"], gw1["conv_w"], gs = _bwd_mixers(a1, p1, carry, 1)
    gs1.update(gs)
    ex1 = _exchange_start(_scatter_items(gw1, GATHERED), tok, "scatter_l1_start")
    carry, gw0, gs0 = _bwd_out(a0, p0, dh, 0, dep=ex1["token"])
    lands, tok = _exchange_wait(ex1, carry["dgl"], "scatter_l1_wait")
    land1 = dict(zip(GATHERED, lands))
    ex0 = _exchange_start(_scatter_items(gw0, OUT_PART), tok, "scatter_l0_out_start")
    dh, gw0["w_in"], gw0["conv_w"], gs = _bwd_mixers(a0, p0, carry, 0, dep=ex0["token"])
    gs0.update(gs)
    lands, tok = _exchange_wait(ex0, dh, "scatter_l0_out_wait")
    land0 = dict(zip(OUT_PART, lands))
    ex0 = _exchange_start(_scatter_items(gw0, IN_PART), tok, "scatter_l0_in_start")
    grad_x2, acc = _ln_bwd(x2, None, w["ln_in_g"], dh, 1.0, "ln_in_b")

    outs = [{} for _ in range(4)]

    def update(names):
        res = None
        for n in names:
            res = _adamw([land0[n], land1[n]], w[n], m[n], v[n], "adamw_" + n)
            for o, t in zip(outs, res):
                o[n] = t
        return res[1]

    update(OUT_PART)
    gsm = {"ln_in_g": acc[0], "ln_in_b": acc[1]}
    for n in SMALL[2:]:
        gsm[n] = jnp.stack([gs0[n], gs1[n]])
    small_shapes = {n: w[n].shape for n in SMALL}
    land_s = _all_gather_small(_pack_small(gsm), "small_grads_all_gather")
    res = _adamw([land_s], _pack_small(w)[None], _pack_small(m)[None], _pack_small(v)[None], "adamw_small")
    for o, t in zip(outs, res):
        o.update(_unpack_small(t[0], small_shapes))
    lands, _ = _exchange_wait(ex0, res[1], "scatter_l0_in_wait")
    land0.update(zip(IN_PART, lands))
    update(IN_PART)
    return loss, grad_x2[None], outs


WEIGHT_NAMES = ("ln_in_g", "ln_in_b", "w_in", "conv_w", "conv_b", "dt_bias", "a_log", "d_skip", "ssd_norm_w",
                "att_sinks", "w_ssd_out", "w_att_out", "w_mix_out", "ln_mix_g", "ln_mix_b", "w_ffn_gate",
                "w_ffn_up", "w_ffn_down", "ln_ffn_g", "ln_ffn_b")


def kernel(x, ln_in_g, ln_in_b, w_in, conv_w, conv_b, dt_bias, a_log, d_skip, ssd_norm_w, att_sinks, w_ssd_out, w_att_out, w_mix_out, ln_mix_g, ln_mix_b, w_ffn_gate, w_ffn_up, w_ffn_down, ln_ffn_g, ln_ffn_b, loss_target, m_ln_in_g, m_ln_in_b, m_w_in, m_conv_w, m_conv_b, m_dt_bias, m_a_log, m_d_skip, m_ssd_norm_w, m_att_sinks, m_w_ssd_out, m_w_att_out, m_w_mix_out, m_ln_mix_g, m_ln_mix_b, m_w_ffn_gate, m_w_ffn_up, m_w_ffn_down, m_ln_ffn_g, m_ln_ffn_b, v_ln_in_g, v_ln_in_b, v_w_in, v_conv_w, v_conv_b, v_dt_bias, v_a_log, v_d_skip, v_ssd_norm_w, v_att_sinks, v_w_ssd_out, v_w_att_out, v_w_mix_out, v_ln_mix_g, v_ln_mix_b, v_w_ffn_gate, v_w_ffn_up, v_w_ffn_down, v_ln_ffn_g, v_ln_ffn_b):
    w = dict(zip(WEIGHT_NAMES, (ln_in_g, ln_in_b, w_in, conv_w, conv_b, dt_bias, a_log, d_skip, ssd_norm_w,
                                att_sinks, w_ssd_out, w_att_out, w_mix_out, ln_mix_g, ln_mix_b, w_ffn_gate,
                                w_ffn_up, w_ffn_down, ln_ffn_g, ln_ffn_b)))
    m = dict(zip(WEIGHT_NAMES, (m_ln_in_g, m_ln_in_b, m_w_in, m_conv_w, m_conv_b, m_dt_bias, m_a_log, m_d_skip,
                                m_ssd_norm_w, m_att_sinks, m_w_ssd_out, m_w_att_out, m_w_mix_out, m_ln_mix_g,
                                m_ln_mix_b, m_w_ffn_gate, m_w_ffn_up, m_w_ffn_down, m_ln_ffn_g, m_ln_ffn_b)))
    v = dict(zip(WEIGHT_NAMES, (v_ln_in_g, v_ln_in_b, v_w_in, v_conv_w, v_conv_b, v_dt_bias, v_a_log, v_d_skip,
                                v_ssd_norm_w, v_att_sinks, v_w_ssd_out, v_w_att_out, v_w_mix_out, v_ln_mix_g,
                                v_ln_mix_b, v_w_ffn_gate, v_w_ffn_up, v_w_ffn_down, v_ln_ffn_g, v_ln_ffn_b)))
    loss, grad_x, outs = _step(x, loss_target, w, m, v)
    result = [loss, grad_x]
    for o in outs:
        result.extend(o[n] for n in WEIGHT_NAMES)
    return tuple(result)
```

```python
import functools
import math

import jax
import jax.numpy as jnp
from jax import lax
from jax.experimental import pallas as pl
from jax.experimental.pallas import tpu as pltpu

F32 = jnp.float32
BF16 = jnp.bfloat16

D_MODEL = 1024
DEPTH = 2
N_DEV = 8
ATT_HEADS = 16
ATT_KV_HEADS = 2
ATT_HEAD_DIM = 64
ATT_BLOCK = 128
SSD_D_INNER = 2048
SSD_HEADS = 32
SSD_GROUPS = 4
SSD_STATE = 128
SSD_CHUNK = 128
FFN_HIDDEN = 2816
LN_EPS = 1e-5
RMS_EPS = 1e-5
ALPHA = (2 * DEPTH) ** 0.25
Q_DIM = 1024
KV_DIM = 128
BC_DIM = 512
IN_DIM = 8480
IN_SHARD = IN_DIM // N_DEV
DT_PAD = 512

ADAM_LR = 0.001
ADAM_B1 = 0.9
ADAM_B2 = 0.999
ADAM_EPS = 1e-08
ADAM_WD = 0.01
ADAM_STEP = 10

LANE = 128
VMEM_LIMIT = 48 * 1024 * 1024
PACK_W = 1024
NEG = -1e30

_NN = (((1,), (0,)), ((), ()))
_NT = (((1,), (1,)), ((), ()))
_TN = (((0,), (0,)), ((), ()))
MESH_ID = pl.DeviceIdType.MESH


def _dot(a, b, dims=_NN):
    return lax.dot_general(a, b, dims, preferred_element_type=F32)


def _dot_hi(a, b):
    return lax.dot_general(a, b, _NN, preferred_element_type=F32, precision=lax.Precision.HIGHEST)


def _sig(x):
    return 1.0 / (1.0 + jnp.exp(-x))


def _softplus(x):
    return jnp.maximum(x, 0.0) + jnp.log(1.0 + jnp.exp(-jnp.abs(x)))


def _cparams(*sem):
    return pltpu.CompilerParams(dimension_semantics=sem, vmem_limit_bytes=VMEM_LIMIT)


def _pick(n, cap):
    if n <= cap:
        return n
    best = None
    for t in range(LANE, cap + 1, LANE):
        if n % t == 0:
            best = t
    assert best is not None, (n, cap)
    return best


def _tile(n):
    if n <= 1024 or n % 1024 == 0:
        return min(n, 1024)
    return _pick(n, 1408)


def _rows(n):
    return min(512, n)


def _mm(a, b, mode, name, add=None, add_scale=1.0, out_dtype=F32, dep=None):
    if mode == "nn":
        m, k = a.shape
        n = b.shape[1]
    elif mode == "nt":
        m, k = a.shape
        n = b.shape[0]
    else:
        k, m = a.shape
        n = b.shape[1]
    tm = _tile(m)
    tn = _tile(n)
    tk = _tile(k)
    nk = k // tk
    has_add = add is not None
    dims = {"nn": _NN, "nt": _NT, "tn": _TN}[mode]

    def body(*refs):
        if dep is not None:
            refs = refs[:-3] + refs[-2:]
        if has_add:
            a_ref, b_ref, add_ref, o_ref, acc_ref = refs
        else:
            a_ref, b_ref, o_ref, acc_ref = refs
        kk = pl.program_id(2)

        @pl.when(kk == 0)
        def _():
            if has_add:
                acc_ref[...] = add_scale * add_ref[...].astype(F32)
            else:
                acc_ref[...] = jnp.zeros_like(acc_ref)

        acc_ref[...] += _dot(a_ref[...].astype(BF16), b_ref[...].astype(BF16), dims)

        @pl.when(kk == nk - 1)
        def _():
            o_ref[...] = acc_ref[...].astype(o_ref.dtype)

    if mode == "nn":
        a_spec = pl.BlockSpec((tm, tk), lambda i, j, kk: (i, kk))
        b_spec = pl.BlockSpec((tk, tn), lambda i, j, kk: (kk, j))
    elif mode == "nt":
        a_spec = pl.BlockSpec((tm, tk), lambda i, j, kk: (i, kk))
        b_spec = pl.BlockSpec((tn, tk), lambda i, j, kk: (j, kk))
    else:
        a_spec = pl.BlockSpec((tk, tm), lambda i, j, kk: (kk, i))
        b_spec = pl.BlockSpec((tk, tn), lambda i, j, kk: (kk, j))
    o_spec = pl.BlockSpec((tm, tn), lambda i, j, kk: (i, j))
    in_specs = [a_spec, b_spec] + ([o_spec] if has_add else [])
    args = (a, b) + ((add,) if has_add else ())
    if dep is not None:
        in_specs.append(pl.BlockSpec((8, LANE), lambda i, j, kk: (0, 0)))
        args += (dep,)
    return pl.pallas_call(
        body, name=name, grid=(m // tm, n // tn, nk),
        in_specs=in_specs, out_specs=o_spec,
        out_shape=jax.ShapeDtypeStruct((m, n), out_dtype),
        scratch_shapes=[pltpu.VMEM((tm, tn), F32)],
        compiler_params=_cparams("parallel", "parallel", "arbitrary"),
    )(*args)


def _vec_spec(width):
    return pl.BlockSpec((1, width), lambda i: (0, 0))


def _ln_fwd(a, b, gamma, beta, alpha, name):
    n_rows, dm = a.shape
    has_b = b is not None

    def body(*refs):
        if has_b:
            a_ref, b_ref, g_ref, be_ref, o_ref = refs
            u = alpha * a_ref[...] + b_ref[...]
        else:
            a_ref, g_ref, be_ref, o_ref = refs
            u = a_ref[...]
        mu = jnp.mean(u, axis=-1, keepdims=True)
        d = u - mu
        var = jnp.mean(d * d, axis=-1, keepdims=True)
        o_ref[...] = d * lax.rsqrt(var + LN_EPS) * g_ref[...] + be_ref[...]

    row = pl.BlockSpec((_rows(n_rows),dm), lambda i: (i, 0))
    in_specs = [row] + ([row] if has_b else []) + [_vec_spec(dm), _vec_spec(dm)]
    args = (a,) + ((b,) if has_b else ()) + (gamma.reshape(1, dm), beta.reshape(1, dm))
    return pl.pallas_call(
        body, name=name, grid=(n_rows // _rows(n_rows),), in_specs=in_specs, out_specs=row,
        out_shape=jax.ShapeDtypeStruct((n_rows, dm), F32),
        compiler_params=_cparams("parallel"),
    )(*args)


def _ln_bwd(a, b, gamma, dy, alpha, name):
    n_rows, dm = a.shape
    has_b = b is not None

    def body(*refs):
        if has_b:
            a_ref, b_ref, g_ref, dy_ref, du_ref, acc_ref = refs
            u = alpha * a_ref[...] + b_ref[...]
        else:
            a_ref, g_ref, dy_ref, du_ref, acc_ref = refs
            u = a_ref[...]

        @pl.when(pl.program_id(0) == 0)
        def _():
            acc_ref[...] = jnp.zeros_like(acc_ref)

        mu = jnp.mean(u, axis=-1, keepdims=True)
        d = u - mu
        var = jnp.mean(d * d, axis=-1, keepdims=True)
        rstd = lax.rsqrt(var + LN_EPS)
        xhat = d * rstd
        dyv = dy_ref[...]
        acc_ref[0:1, :] += jnp.sum(dyv * xhat, axis=0, keepdims=True)
        acc_ref[1:2, :] += jnp.sum(dyv, axis=0, keepdims=True)
        dxh = dyv * g_ref[...]
        m1 = jnp.mean(dxh, axis=-1, keepdims=True)
        m2 = jnp.mean(dxh * xhat, axis=-1, keepdims=True)
        du_ref[...] = rstd * (dxh - m1 - xhat * m2)

    row = pl.BlockSpec((_rows(n_rows),dm), lambda i: (i, 0))
    in_specs = [row] + ([row] if has_b else []) + [_vec_spec(dm), row]
    args = (a,) + ((b,) if has_b else ()) + (gamma.reshape(1, dm), dy)
    return pl.pallas_call(
        body, name=name, grid=(n_rows // _rows(n_rows),), in_specs=in_specs,
        out_specs=(row, pl.BlockSpec((8, dm), lambda i: (0, 0))),
        out_shape=(jax.ShapeDtypeStruct((n_rows, dm), F32), jax.ShapeDtypeStruct((8, dm), F32)),
        compiler_params=_cparams("arbitrary"),
    )(*args)


def _loss_fwd_bwd(y, target, name):
    n_rows, dm = y.shape

    def body(y_ref, t_ref, acc_ref, dy_ref):
        @pl.when(pl.program_id(0) == 0)
        def _():
            acc_ref[...] = jnp.zeros_like(acc_ref)

        d = y_ref[...] - t_ref[...]
        acc_ref[...] += jnp.sum(d * d)
        dy_ref[...] = d * (1.0 / dm)

    row = pl.BlockSpec((_rows(n_rows),dm), lambda i: (i, 0))
    return pl.pallas_call(
        body, name=name, grid=(n_rows // _rows(n_rows),), in_specs=[row, row],
        out_specs=(pl.BlockSpec((8, LANE), lambda i: (0, 0)), row),
        out_shape=(jax.ShapeDtypeStruct((8, LANE), F32), jax.ShapeDtypeStruct((n_rows, dm), F32)),
        compiler_params=_cparams("arbitrary"),
    )(y, target)


def _swiglu_fwd(g, u, name):
    n_rows, w = g.shape
    tw = _pick(w, 1408)

    def body(g_ref, u_ref, o_ref):
        gv = g_ref[...]
        o_ref[...] = (gv * _sig(gv) * u_ref[...]).astype(BF16)

    blk = pl.BlockSpec((_rows(n_rows),tw), lambda i, j: (i, j))
    return pl.pallas_call(
        body, name=name, grid=(n_rows // _rows(n_rows), w // tw), in_specs=[blk, blk], out_specs=blk,
        out_shape=jax.ShapeDtypeStruct((n_rows, w), BF16),
        compiler_params=_cparams("parallel", "parallel"),
    )(g, u)


def _swiglu_bwd(g, u, dact, name):
    n_rows, w = g.shape
    tw = _pick(w, 1408)

    def body(g_ref, u_ref, da_ref, dg_ref, du_ref):
        gv = g_ref[...]
        s = _sig(gv)
        da = da_ref[...]
        dg_ref[...] = (da * u_ref[...] * (s * (1.0 + gv * (1.0 - s)))).astype(BF16)
        du_ref[...] = (da * gv * s).astype(BF16)

    blk = pl.BlockSpec((_rows(n_rows),tw), lambda i, j: (i, j))
    return pl.pallas_call(
        body, name=name, grid=(n_rows // _rows(n_rows), w // tw), in_specs=[blk, blk, blk], out_specs=(blk, blk),
        out_shape=(jax.ShapeDtypeStruct((n_rows, w), BF16), jax.ShapeDtypeStruct((n_rows, w), BF16)),
        compiler_params=_cparams("parallel", "parallel"),
    )(g, u, dact)


def _merge_fwd(gl, ya, yb, name):
    n_rows, dm = ya.shape

    def body(gl_ref, ya_ref, yb_ref, o_ref):
        ga = _sig(gl_ref[:, :dm])
        gb = _sig(gl_ref[:, dm:])
        o_ref[...] = (ga * ya_ref[...] + gb * yb_ref[...]).astype(BF16)

    row = pl.BlockSpec((_rows(n_rows),dm), lambda i: (i, 0))
    row2 = pl.BlockSpec((_rows(n_rows),2 * dm), lambda i: (i, 0))
    return pl.pallas_call(
        body, name=name, grid=(n_rows // _rows(n_rows),), in_specs=[row2, row, row], out_specs=row,
        out_shape=jax.ShapeDtypeStruct((n_rows, dm), BF16),
        compiler_params=_cparams("parallel"),
    )(gl, ya, yb)


def _merge_bwd(gl, ya, yb, dmerged, name):
    n_rows, dm = ya.shape

    def body(gl_ref, ya_ref, yb_ref, dm_ref, dya_ref, dyb_ref, dgl_ref):
        ga = _sig(gl_ref[:, :dm])
        gb = _sig(gl_ref[:, dm:])
        dmv = dm_ref[...]
        dya_ref[...] = (dmv * ga).astype(BF16)
        dyb_ref[...] = (dmv * gb).astype(BF16)
        dgl_ref[:, :dm] = (dmv * ya_ref[...] * ga * (1.0 - ga)).astype(BF16)
        dgl_ref[:, dm:] = (dmv * yb_ref[...] * gb * (1.0 - gb)).astype(BF16)

    row = pl.BlockSpec((_rows(n_rows),dm), lambda i: (i, 0))
    row2 = pl.BlockSpec((_rows(n_rows),2 * dm), lambda i: (i, 0))
    return pl.pallas_call(
        body, name=name, grid=(n_rows // _rows(n_rows),), in_specs=[row2, row, row, row], out_specs=(row, row, row2),
        out_shape=(jax.ShapeDtypeStruct((n_rows, dm), BF16), jax.ShapeDtypeStruct((n_rows, dm), BF16),
                   jax.ShapeDtypeStruct((n_rows, 2 * dm), BF16)),
        compiler_params=_cparams("parallel"),
    )(gl, ya, yb, dmerged)


CONV_TAPS = 4
CONV_COLS = 512
HALO = 8


def _shift_down(cur, prev8, s, row8):
    r = pltpu.roll(cur, s, axis=0)
    top = jnp.where(row8 < s, pltpu.roll(prev8, s, axis=0), r[0:HALO])
    return jnp.concatenate([top, r[HALO:]], axis=0)


def _shift_up(cur, next8, s, row8):
    n = cur.shape[0]
    r = pltpu.roll(cur, n - s, axis=0)
    bot = jnp.where(row8 >= HALO - s, pltpu.roll(next8, HALO - s, axis=0), r[n - HALO:])
    return jnp.concatenate([r[:n - HALO], bot], axis=0)


def _conv_pre(u_ref, prev_ref, w_ref, b_ref, li):
    cur = u_ref[...]
    prev8 = jnp.where(li == 0, 0.0, prev_ref[...])
    row8 = lax.broadcasted_iota(jnp.int32, prev8.shape, 0)
    shifted = [cur] + [_shift_down(cur, prev8, s, row8) for s in range(1, CONV_TAPS)]
    acc = b_ref[...] + shifted[0] * w_ref[CONV_TAPS - 1:CONV_TAPS, :]
    for s in range(1, CONV_TAPS):
        acc = acc + shifted[s] * w_ref[CONV_TAPS - 1 - s:CONV_TAPS - s, :]
    return acc, shifted


def _conv_specs(n_rows, tl):
    cur = pl.BlockSpec((tl, CONV_COLS), lambda cj, li: (li, cj))
    prev = pl.BlockSpec((HALO, CONV_COLS), lambda cj, li: (jnp.maximum(li * (tl // HALO) - 1, 0), cj))
    nxt = pl.BlockSpec((HALO, CONV_COLS),
                       lambda cj, li: (jnp.minimum((li + 1) * (tl // HALO), n_rows // HALO - 1), cj))
    par = pl.BlockSpec((8, CONV_COLS), lambda cj, li: (0, cj))
    return cur, prev, nxt, par


def _conv_fwd(u, w8, b8, name):
    n_rows, c = u.shape
    tl = _rows(n_rows)
    cur, prev, _, par = _conv_specs(n_rows, tl)

    def body(u_ref, prev_ref, w_ref, b_ref, o_ref):
        acc, _ = _conv_pre(u_ref, prev_ref, w_ref, b_ref[0:1, :], pl.program_id(1))
        o_ref[...] = acc * _sig(acc)

    return pl.pallas_call(
        body, name=name, grid=(c // CONV_COLS, n_rows // tl), in_specs=[cur, prev, par, par], out_specs=cur,
        out_shape=jax.ShapeDtypeStruct((n_rows, c), F32),
        compiler_params=_cparams("parallel", "parallel"),
    )(u, u, w8, b8)


def _conv_bwd_pre(u, w8, b8, dout, name):
    n_rows, c = u.shape
    tl = _rows(n_rows)
    cur, prev, _, par = _conv_specs(n_rows, tl)

    def body(u_ref, prev_ref, w_ref, b_ref, do_ref, dc_ref, acc_ref):
        @pl.when(pl.program_id(1) == 0)
        def _():
            acc_ref[...] = jnp.zeros_like(acc_ref)

        acc, shifted = _conv_pre(u_ref, prev_ref, w_ref, b_ref[0:1, :], pl.program_id(1))
        sg = _sig(acc)
        dc = do_ref[...] * (sg * (1.0 + acc * (1.0 - sg)))
        dc_ref[...] = dc
        for k in range(CONV_TAPS):
            acc_ref[k:k + 1, :] += jnp.sum(dc * shifted[CONV_TAPS - 1 - k], axis=0, keepdims=True)
        acc_ref[CONV_TAPS:CONV_TAPS + 1, :] += jnp.sum(dc, axis=0, keepdims=True)

    return pl.pallas_call(
        body, name=name, grid=(c // CONV_COLS, n_rows // tl), in_specs=[cur, prev, par, par, cur],
        out_specs=(cur, par),
        out_shape=(jax.ShapeDtypeStruct((n_rows, c), F32), jax.ShapeDtypeStruct((8, c), F32)),
        compiler_params=_cparams("parallel", "arbitrary"),
    )(u, u, w8, b8, dout)


def _conv_bwd_in(dc, w8, name):
    n_rows, c = dc.shape
    tl = _rows(n_rows)
    cur, _, nxt, par = _conv_specs(n_rows, tl)
    n_l = n_rows // tl

    def body(dc_ref, next_ref, w_ref, o_ref):
        cur_v = dc_ref[...]
        next8 = jnp.where(pl.program_id(1) == n_l - 1, 0.0, next_ref[...])
        row8 = lax.broadcasted_iota(jnp.int32, next8.shape, 0)
        acc = cur_v * w_ref[CONV_TAPS - 1:CONV_TAPS, :]
        for s in range(1, CONV_TAPS):
            acc = acc + _shift_up(cur_v, next8, s, row8) * w_ref[CONV_TAPS - 1 - s:CONV_TAPS - s, :]
        o_ref[...] = acc.astype(BF16)

    return pl.pallas_call(
        body, name=name, grid=(c // CONV_COLS, n_l), in_specs=[cur, nxt, par], out_specs=cur,
        out_shape=jax.ShapeDtypeStruct((n_rows, c), BF16),
        compiler_params=_cparams("parallel", "parallel"),
    )(dc, dc, w8)


NORM_GROUP = SSD_D_INNER // SSD_GROUPS


def _gnorm_fwd(y, z, w, name):
    n_rows, c = y.shape

    def body(y_ref, z_ref, w_ref, o_ref):
        zv = z_ref[...]
        yg = y_ref[...] * (zv * _sig(zv))
        r = lax.rsqrt(jnp.mean(yg * yg, axis=-1, keepdims=True) + RMS_EPS)
        o_ref[...] = (yg * r * w_ref[...]).astype(BF16)

    blk = pl.BlockSpec((_rows(n_rows),NORM_GROUP), lambda i, j: (i, j))
    wspec = pl.BlockSpec((1, NORM_GROUP), lambda i, j: (0, j))
    return pl.pallas_call(
        body, name=name, grid=(n_rows // _rows(n_rows), c // NORM_GROUP), in_specs=[blk, blk, wspec], out_specs=blk,
        out_shape=jax.ShapeDtypeStruct((n_rows, c), BF16),
        compiler_params=_cparams("parallel", "parallel"),
    )(y, z, w.reshape(1, c))


def _gnorm_bwd(y, z, w, dyn, name):
    n_rows, c = y.shape

    def body(y_ref, z_ref, w_ref, dn_ref, dy_ref, dz_ref, acc_ref):
        @pl.when(pl.program_id(1) == 0)
        def _():
            acc_ref[...] = jnp.zeros_like(acc_ref)

        zv = z_ref[...]
        yv = y_ref[...]
        sz = _sig(zv)
        silu = zv * sz
        yg = yv * silu
        r = lax.rsqrt(jnp.mean(yg * yg, axis=-1, keepdims=True) + RMS_EPS)
        nrm = yg * r
        dn = dn_ref[...]
        acc_ref[0:1, :] += jnp.sum(dn * nrm, axis=0, keepdims=True)
        dnw = dn * w_ref[...]
        dyg = r * (dnw - nrm * jnp.mean(dnw * nrm, axis=-1, keepdims=True))
        dy_ref[...] = dyg * silu
        dz_ref[...] = (dyg * yv * (sz * (1.0 + zv * (1.0 - sz)))).astype(BF16)

    blk = pl.BlockSpec((_rows(n_rows),NORM_GROUP), lambda j, i: (i, j))
    wspec = pl.BlockSpec((1, NORM_GROUP), lambda j, i: (0, j))
    aspec = pl.BlockSpec((8, NORM_GROUP), lambda j, i: (0, j))
    return pl.pallas_call(
        body, name=name, grid=(c // NORM_GROUP, n_rows // _rows(n_rows)), in_specs=[blk, blk, wspec, blk],
        out_specs=(blk, blk, aspec),
        out_shape=(jax.ShapeDtypeStruct((n_rows, c), F32), jax.ShapeDtypeStruct((n_rows, c), BF16),
                   jax.ShapeDtypeStruct((8, c), F32)),
        compiler_params=_cparams("parallel", "arbitrary"),
    )(y, z, w.reshape(1, c), dyn)


ATT_SCALE = ATT_HEAD_DIM ** -0.5
ATT_SLOPES = [2.0 ** (-8.0 * (h + 1) / ATT_HEADS) for h in range(ATT_HEADS)]
Q_PER_KV = ATT_HEADS // ATT_KV_HEADS


def _dup_half(t, g, lo):
    tr = pltpu.roll(t, ATT_HEAD_DIM, axis=1)
    return jnp.where(lo, t, tr) if g == 0 else jnp.where(lo, tr, t)


def _att_band(kv_ref, kvp_ref, n):
    cur = kv_ref[...]
    prev = jnp.where(n == 0, 0.0, kvp_ref[...])
    lo = lax.broadcasted_iota(jnp.int32, (ATT_BLOCK, LANE), 1) < ATT_HEAD_DIM
    bands = []
    for g in range(ATT_KV_HEADS):
        kb = jnp.concatenate([_dup_half(prev[:, :LANE], g, lo), _dup_half(cur[:, :LANE], g, lo)], axis=0)
        vb = jnp.concatenate([_dup_half(prev[:, LANE:], g, lo), _dup_half(cur[:, LANE:], g, lo)], axis=0)
        bands.append((kb.astype(BF16), vb.astype(BF16)))
    return bands


def _att_probs(qh, kb, sink, h, n):
    s = _dot(qh, kb, _NT)
    qi = lax.broadcasted_iota(jnp.int32, s.shape, 0)
    kj = lax.broadcasted_iota(jnp.int32, s.shape, 1)
    rel = qi + ATT_BLOCK - kj
    valid = (rel >= 0) & (rel < ATT_BLOCK) & ((kj >= ATT_BLOCK) | (n > 0))
    s = jnp.where(valid, s - ATT_SLOPES[h] * rel.astype(F32), NEG)
    m = jnp.maximum(jnp.max(s, axis=-1, keepdims=True), sink)
    p = jnp.exp(s - m)
    es = jnp.exp(sink - m)
    inv = 1.0 / (jnp.sum(p, axis=-1, keepdims=True) + es)
    return p * inv, es * inv


def _att_fwd(q, kv, sinks8, name):
    n_rows = q.shape[0]
    nb = n_rows // ATT_BLOCK

    def body(q_ref, kv_ref, kvp_ref, s_ref, o_ref):
        n = pl.program_id(0)
        bands = _att_band(kv_ref, kvp_ref, n)
        lo = lax.broadcasted_iota(jnp.int32, (ATT_BLOCK, LANE), 1) < ATT_HEAD_DIM
        for j in range(ATT_HEADS // 2):
            qp = q_ref[:, j * LANE:(j + 1) * LANE] * ATT_SCALE
            outs = []
            for half in range(2):
                h = 2 * j + half
                kb, vb = bands[h // Q_PER_KV]
                msk = lo if half == 0 else jnp.logical_not(lo)
                qh = jnp.where(msk, qp, 0.0).astype(BF16)
                p, _ = _att_probs(qh, kb, s_ref[0:1, h:h + 1], h, n)
                outs.append(_dot(p.astype(BF16), vb))
            o_ref[:, j * LANE:(j + 1) * LANE] = jnp.where(lo, outs[0], outs[1]).astype(BF16)

    return pl.pallas_call(
        body, name=name, grid=(nb,),
        in_specs=[pl.BlockSpec((ATT_BLOCK, Q_DIM), lambda n: (n, 0)),
                  pl.BlockSpec((ATT_BLOCK, 2 * LANE), lambda n: (n, 0)),
                  pl.BlockSpec((ATT_BLOCK, 2 * LANE), lambda n: (jnp.maximum(n - 1, 0), 0)),
                  pl.BlockSpec((8, LANE), lambda n: (0, 0))],
        out_specs=pl.BlockSpec((ATT_BLOCK, Q_DIM), lambda n: (n, 0)),
        out_shape=jax.ShapeDtypeStruct((n_rows, Q_DIM), BF16),
        compiler_params=_cparams("parallel"),
    )(q, kv, kv, sinks8)


def _att_bwd(q, kv, sinks8, dout, name):
    n_rows = q.shape[0]
    nb = n_rows // ATT_BLOCK

    def body(q_ref, kv_ref, kvp_ref, s_ref, do_ref, dq_ref, dkv_ref, acc_ref, carry_ref):
        n = pl.program_id(0)

        @pl.when(n == 0)
        def _():
            acc_ref[...] = jnp.zeros_like(acc_ref)
            carry_ref[...] = jnp.zeros_like(carry_ref)

        @pl.when(n == nb)
        def _():
            dkv_ref[...] = carry_ref[...].astype(BF16)

        @pl.when(n < nb)
        def _():
            bands = _att_band(kv_ref, kvp_ref, n)
            lo = lax.broadcasted_iota(jnp.int32, (ATT_BLOCK, LANE), 1) < ATT_HEAD_DIM
            lane1 = lax.broadcasted_iota(jnp.int32, (1, LANE), 1)
            dk_acc = [jnp.zeros((2 * ATT_BLOCK, LANE), F32) for _ in range(ATT_KV_HEADS)]
            dv_acc = [jnp.zeros((2 * ATT_BLOCK, LANE), F32) for _ in range(ATT_KV_HEADS)]
            dsink = jnp.zeros((1, LANE), F32)
            for j in range(ATT_HEADS // 2):
                qp = q_ref[:, j * LANE:(j + 1) * LANE] * ATT_SCALE
                dop = do_ref[:, j * LANE:(j + 1) * LANE].astype(F32)
                dqs = []
                for half in range(2):
                    h = 2 * j + half
                    g = h // Q_PER_KV
                    kb, vb = bands[g]
                    msk = lo if half == 0 else jnp.logical_not(lo)
                    qh = jnp.where(msk, qp, 0.0).astype(BF16)
                    doh = jnp.where(msk, dop, 0.0).astype(BF16)
                    p, ps = _att_probs(qh, kb, s_ref[0:1, h:h + 1], h, n)
                    dp = _dot(doh, vb, _NT)
                    delta = jnp.sum(p * dp, axis=-1, keepdims=True)
                    ds = p * (dp - delta)
                    dsink = jnp.where(lane1 == h, -jnp.sum(ps * delta), dsink)
                    ds_b = ds.astype(BF16)
                    dqs.append(_dot(ds_b, kb) * ATT_SCALE)
                    dk_acc[g] = dk_acc[g] + _dot(ds.T.astype(BF16), qh)
                    dv_acc[g] = dv_acc[g] + _dot(p.T.astype(BF16), doh)
                dq_ref[:, j * LANE:(j + 1) * LANE] = jnp.where(lo, dqs[0], dqs[1]).astype(BF16)
            acc_ref[0:1, :] += dsink
            lo2 = lax.broadcasted_iota(jnp.int32, (2 * ATT_BLOCK, LANE), 1) < ATT_HEAD_DIM
            folded = []
            for acc in (dk_acc, dv_acc):
                t0 = acc[0] + pltpu.roll(acc[0], ATT_HEAD_DIM, axis=1)
                t1 = acc[1] + pltpu.roll(acc[1], ATT_HEAD_DIM, axis=1)
                folded.append(jnp.where(lo2, t0, t1))
            band = jnp.concatenate(folded, axis=1)
            dkv_ref[...] = (carry_ref[...] + band[:ATT_BLOCK]).astype(BF16)
            carry_ref[...] = band[ATT_BLOCK:]

    def qmap(n):
        return (jnp.minimum(n, nb - 1), 0)

    return pl.pallas_call(
        body, name=name, grid=(nb + 1,),
        in_specs=[pl.BlockSpec((ATT_BLOCK, Q_DIM), qmap),
                  pl.BlockSpec((ATT_BLOCK, 2 * LANE), qmap),
                  pl.BlockSpec((ATT_BLOCK, 2 * LANE), lambda n: (jnp.maximum(jnp.minimum(n, nb - 1) - 1, 0), 0)),
                  pl.BlockSpec((8, LANE), lambda n: (0, 0)),
                  pl.BlockSpec((ATT_BLOCK, Q_DIM), qmap)],
        out_specs=(pl.BlockSpec((ATT_BLOCK, Q_DIM), qmap),
                   pl.BlockSpec((ATT_BLOCK, 2 * LANE), lambda n: (jnp.maximum(n - 1, 0), 0)),
                   pl.BlockSpec((8, LANE), lambda n: (0, 0))),
        out_shape=(jax.ShapeDtypeStruct((n_rows, Q_DIM), BF16), jax.ShapeDtypeStruct((n_rows, 2 * LANE), BF16),
                   jax.ShapeDtypeStruct((8, LANE), F32)),
        scratch_shapes=[pltpu.VMEM((ATT_BLOCK, 2 * LANE), F32)],
        compiler_params=_cparams("arbitrary"),
    )(q, kv, kv, sinks8, dout)


HEADS_PER_GROUP = SSD_HEADS // SSD_GROUPS
PAIRS_PER_GROUP = HEADS_PER_GROUP // 2
T = SSD_CHUNK


def _ssd_scalars(dtr_ref, par_ref):
    dt = _softplus(dtr_ref[...] + par_ref[0:1, :])
    a = -jnp.exp(par_ref[1:2, :])
    ri = lax.broadcasted_iota(jnp.int32, (T, T), 0)
    ci = lax.broadcasted_iota(jnp.int32, (T, T), 1)
    tril = (ri >= ci).astype(F32)
    cs = _dot_hi(tril, dt * a)
    cst = cs.T
    return dt, a, cs, cst, ri, ci


def _lane_pick(lo, arr, k0):
    return jnp.where(lo, arr[:, k0:k0 + 1], arr[:, k0 + 1:k0 + 2])


def _ssd_fwd(xs, bm, cm, dtr, par, name):
    n_rows = xs.shape[0]
    nc = n_rows // T
    gw = PAIRS_PER_GROUP * LANE

    def body(x_ref, b_ref, c_ref, dtr_ref, par_ref, y_ref, hs_ref, h_ref):
        @pl.when(pl.program_id(1) == 0)
        def _():
            h_ref[...] = jnp.zeros_like(h_ref)

        dt, a, cs, cst, ri, ci = _ssd_scalars(dtr_ref, par_ref)
        tri = ri >= ci
        lo = lax.broadcasted_iota(jnp.int32, (T, LANE), 1) < SSD_CHUNK // 2
        ecs = jnp.exp(cs)
        dect = jnp.exp(cst[:, T - 1:T] - cst)
        etot = jnp.exp(cs[T - 1:T, :])
        bg = b_ref[...]
        cg = c_ref[...]
        bgt = bg.T
        cb = _dot(cg.astype(BF16), bg.astype(BF16), _NT)
        for j in range(PAIRS_PER_GROUP):
            xp = x_ref[:, j * LANE:(j + 1) * LANE]
            xdt = (xp * _lane_pick(lo, dt, 2 * j)).astype(BF16)
            hp = h_ref[j]
            hs_ref[0, 0, j] = hp
            hp_b = hp.astype(BF16)
            ys, ss = [], []
            for half in range(2):
                k = 2 * j + half
                lm = jnp.exp(jnp.where(tri, cs[:, k:k + 1] - cst[k:k + 1, :], NEG))
                yh = _dot((lm * cb).astype(BF16), xdt) + _dot((cg * ecs[:, k:k + 1]).astype(BF16), hp_b)
                ys.append(yh)
                ss.append(_dot((bgt * dect[k:k + 1, :]).astype(BF16), xdt))
            dsk = jnp.where(lo[0:1, :], par_ref[2:3, 2 * j:2 * j + 1], par_ref[2:3, 2 * j + 1:2 * j + 2])
            y_ref[:, j * LANE:(j + 1) * LANE] = jnp.where(lo, ys[0], ys[1]) + dsk * xp
            et = jnp.where(lo[0:1, :], etot[:, 2 * j:2 * j + 1], etot[:, 2 * j + 1:2 * j + 2])
            h_ref[j] = hp * et + jnp.where(lo, ss[0], ss[1])

    return pl.pallas_call(
        body, name=name, grid=(SSD_GROUPS, nc),
        in_specs=[pl.BlockSpec((T, gw), lambda g, c: (c, g)),
                  pl.BlockSpec((T, SSD_STATE), lambda g, c: (c, g)),
                  pl.BlockSpec((T, SSD_STATE), lambda g, c: (c, g)),
                  pl.BlockSpec((T, LANE), lambda g, c: (c, g)),
                  pl.BlockSpec((8, LANE), lambda g, c: (0, g))],
        out_specs=(pl.BlockSpec((T, gw), lambda g, c: (c, g)),
                   pl.BlockSpec((1, 1, PAIRS_PER_GROUP, SSD_STATE, LANE), lambda g, c: (g, c, 0, 0, 0))),
        out_shape=(jax.ShapeDtypeStruct((n_rows, SSD_D_INNER), F32),
                   jax.ShapeDtypeStruct((SSD_GROUPS, nc, PAIRS_PER_GROUP, SSD_STATE, LANE), F32)),
        scratch_shapes=[pltpu.VMEM((PAIRS_PER_GROUP, SSD_STATE, LANE), F32)],
        compiler_params=_cparams("parallel", "arbitrary"),
    )(xs, bm, cm, dtr, par)


def _ssd_bwd(xs, bm, cm, dtr, par, hs, dy, name):
    n_rows = xs.shape[0]
    nc = n_rows // T
    gw = PAIRS_PER_GROUP * LANE

    def body(x_ref, b_ref, c_ref, dtr_ref, par_ref, hs_ref, dy_ref,
             dx_ref, db_ref, dc_ref, ddtr_ref, acc_ref, dh_ref):
        @pl.when(pl.program_id(1) == 0)
        def _():
            dh_ref[...] = jnp.zeros_like(dh_ref)
            acc_ref[...] = jnp.zeros_like(acc_ref)

        dt, a, cs, cst, ri, ci = _ssd_scalars(dtr_ref, par_ref)
        tri = ri >= ci
        trit = ci >= ri
        lane = lax.broadcasted_iota(jnp.int32, (T, LANE), 1)
        lo = lane < SSD_CHUNK // 2
        lane1 = lane[0:1, :]
        ecs = jnp.exp(cs)
        ecst = jnp.exp(cst)
        dec = jnp.exp(cs[T - 1:T, :] - cs)
        etot = jnp.exp(cs[T - 1:T, :])
        bg = b_ref[...]
        cg = c_ref[...]
        bg_b = bg.astype(BF16)
        cg_b = cg.astype(BF16)
        cgt = cg.T
        cb = _dot(cg_b, bg_b, _NT)
        cbt = _dot(bg_b, cg_b, _NT)
        dbg = jnp.zeros((T, SSD_STATE), F32)
        dcg = jnp.zeros((T, SSD_STATE), F32)
        dcs_acc = jnp.zeros((T, LANE), F32)
        ddt_acc = jnp.zeros((T, LANE), F32)
        dsk_acc = jnp.zeros((1, LANE), F32)
        last_row = lax.broadcasted_iota(jnp.int32, (T, 1), 0) == T - 1
        for j in range(PAIRS_PER_GROUP):
            xp = x_ref[:, j * LANE:(j + 1) * LANE]
            dtl = _lane_pick(lo, dt, 2 * j)
            xdt = xp * dtl
            hp = hs_ref[0, 0, j]
            dhn = dh_ref[j]
            dyp = dy_ref[:, j * LANE:(j + 1) * LANE]
            dxdt = jnp.zeros((T, LANE), F32)
            et = jnp.where(lo[0:1, :], etot[:, 2 * j:2 * j + 1], etot[:, 2 * j + 1:2 * j + 2])
            dh_new = dhn * et
            for half in range(2):
                k = 2 * j + half
                msk = lo if half == 0 else jnp.logical_not(lo)
                xh_f = jnp.where(msk, xdt, 0.0)
                dyh_f = jnp.where(msk, dyp, 0.0)
                hh_f = jnp.where(msk, hp, 0.0)
                dhh_f = jnp.where(msk, dhn, 0.0)
                xh, dyh, hh, dhh = (v.astype(BF16) for v in (xh_f, dyh_f, hh_f, dhh_f))
                cs_col = cs[:, k:k + 1]
                cs_row = cst[k:k + 1, :]
                lm = jnp.exp(jnp.where(tri, cs_col - cs_row, NEG))
                lmt = jnp.exp(jnp.where(trit, cs_row - cs_col, NEG))
                dm = _dot(dyh, xh, _NT)
                dmt = _dot(xh, dyh, _NT)
                mm = lm * cb
                mmt = lmt * cbt
                bdh = _dot((bg * dec[:, k:k + 1]).astype(BF16), dhh)
                dxdt = dxdt + _dot(mmt.astype(BF16), dyh) + bdh
                dcg = dcg + _dot((dm * lm).astype(BF16), bg_b) + _dot(dyh, hh, _NT) * ecs[:, k:k + 1]
                dbg = dbg + _dot((dmt * lmt).astype(BF16), cg_b) + _dot(xh, dhh, _NT) * dec[:, k:k + 1]
                dh_new = dh_new + _dot((cgt * ecst[k:k + 1, :]).astype(BF16), dyh)
                yo = _dot((cg * ecs[:, k:k + 1]).astype(BF16), hh)
                e1 = jnp.sum(dm * mm, axis=-1, keepdims=True)
                e2 = jnp.sum(dmt * mmt, axis=-1, keepdims=True)
                e3 = jnp.sum(dyh_f * yo, axis=-1, keepdims=True)
                e4 = jnp.sum(xh_f * bdh, axis=-1, keepdims=True)
                tsum = jnp.sum(e4) + etot[:, k:k + 1] * jnp.sum(hh_f * dhh_f)
                dcs_h = e1 - e2 + e3 - e4 + jnp.where(last_row, tsum, 0.0)
                dcs_acc = jnp.where(lane == k, dcs_h, dcs_acc)
                dsk_acc = jnp.where(lane1 == k, jnp.sum(dyh_f * xp), dsk_acc)
            ddt_lo = jnp.sum(jnp.where(lo, dxdt * xp, 0.0), axis=-1, keepdims=True)
            ddt_hi = jnp.sum(jnp.where(lo, 0.0, dxdt * xp), axis=-1, keepdims=True)
            ddt_acc = jnp.where(lane == 2 * j, ddt_lo, jnp.where(lane == 2 * j + 1, ddt_hi, ddt_acc))
            dsk = jnp.where(lo[0:1, :], par_ref[2:3, 2 * j:2 * j + 1], par_ref[2:3, 2 * j + 1:2 * j + 2])
            dx_ref[:, j * LANE:(j + 1) * LANE] = dxdt * dtl + dsk * dyp
            dh_ref[j] = dh_new
        db_ref[...] = dbg
        dc_ref[...] = dcg
        triu = (ci >= ri).astype(F32)
        dda = _dot_hi(triu, dcs_acc)
        ddt = ddt_acc + dda * a
        ddtr = ddt * _sig(dtr_ref[...] + par_ref[0:1, :])
        ddtr_ref[...] = ddtr.astype(BF16)
        acc_ref[0:1, :] += jnp.sum(ddtr, axis=0, keepdims=True)
        acc_ref[1:2, :] += jnp.sum(dda * dt, axis=0, keepdims=True) * a
        acc_ref[2:3, :] += dsk_acc

    def rev(g, c):
        return (nc - 1 - c, g)

    return pl.pallas_call(
        body, name=name, grid=(SSD_GROUPS, nc),
        in_specs=[pl.BlockSpec((T, gw), rev),
                  pl.BlockSpec((T, SSD_STATE), rev),
                  pl.BlockSpec((T, SSD_STATE), rev),
                  pl.BlockSpec((T, LANE), rev),
                  pl.BlockSpec((8, LANE), lambda g, c: (0, g)),
                  pl.BlockSpec((1, 1, PAIRS_PER_GROUP, SSD_STATE, LANE), lambda g, c: (g, nc - 1 - c, 0, 0, 0)),
                  pl.BlockSpec((T, gw), rev)],
        out_specs=(pl.BlockSpec((T, gw), rev),
                   pl.BlockSpec((T, SSD_STATE), rev),
                   pl.BlockSpec((T, SSD_STATE), rev),
                   pl.BlockSpec((T, LANE), rev),
                   pl.BlockSpec((8, LANE), lambda g, c: (0, g))),
        out_shape=(jax.ShapeDtypeStruct((n_rows, SSD_D_INNER), F32),
                   jax.ShapeDtypeStruct((n_rows, BC_DIM), F32),
                   jax.ShapeDtypeStruct((n_rows, BC_DIM), F32),
                   jax.ShapeDtypeStruct((n_rows, DT_PAD), BF16),
                   jax.ShapeDtypeStruct((8, DT_PAD), F32)),
        scratch_shapes=[pltpu.VMEM((PAIRS_PER_GROUP, SSD_STATE, LANE), F32)],
        compiler_params=_cparams("parallel", "arbitrary"),
    )(xs, bm, cm, dtr, par, hs, dy)


ADAM_ROWS = 256


def _adamw(lands, w, m, v, name):
    na = len(lands)
    n_slots, r, wd = lands[0].shape
    tr = r if r <= 2 * ADAM_ROWS else ADAM_ROWS
    nj = r // tr
    bc1 = 1.0 - ADAM_B1 ** ADAM_STEP
    bc2 = 1.0 - ADAM_B2 ** ADAM_STEP

    def body(*refs):
        l_refs = refs[:na]
        w_ref, m_ref, v_ref, g_ref, d_ref, nm_ref, nv_ref = refs[na:]
        for a in range(na):
            @pl.when(pl.program_id(0) == a)
            def _(l_ref=l_refs[a]):
                g = l_ref[0].astype(F32)
                for s in range(1, n_slots):
                    g = g + l_ref[s].astype(F32)
                mn = ADAM_B1 * m_ref[0] + (1.0 - ADAM_B1) * g
                vn = ADAM_B2 * v_ref[0] + (1.0 - ADAM_B2) * (g * g)
                mh = mn / bc1
                vh = vn / bc2
                g_ref[0] = g
                nm_ref[0] = mn
                nv_ref[0] = vn
                d_ref[0] = -ADAM_LR * (mh / (jnp.sqrt(vh) + ADAM_EPS) + ADAM_WD * w_ref[0])

    def land_spec(a):
        return pl.BlockSpec((n_slots, tr, wd),
                            lambda i, j: (0, jnp.where(i == a, j, jnp.where(i < a, 0, nj - 1)), 0))

    blk = pl.BlockSpec((1, tr, wd), lambda i, j: (i, j, 0))
    shp = jax.ShapeDtypeStruct((na, r, wd), F32)
    return pl.pallas_call(
        body, name=name, grid=(na, nj), in_specs=[land_spec(a) for a in range(na)] + [blk, blk, blk],
        out_specs=(blk, blk, blk, blk), out_shape=(shp, shp, shp, shp),
        compiler_params=_cparams("arbitrary", "arbitrary"),
    )(*lands, w, m, v)


def _mesh_pos():
    return lax.axis_index("x"), lax.axis_index("y"), lax.axis_index("c")


def _peer(pos, k):
    x, y, c = pos
    px = 1 - x if (k >> 2) & 1 else x
    py = 1 - y if (k >> 1) & 1 else y
    pc = 1 - c if k & 1 else c
    return px, py, pc


def _flat(pos):
    return 4 * pos[0] + 2 * pos[1] + pos[2]


HBM_SPEC = pl.BlockSpec(memory_space=pl.ANY)


ROW_SHARDED = ("w_ssd_out", "w_att_out", "w_mix_out", "w_ffn_down")
COL_SHARDED = ("w_in", "w_ffn_gate", "w_ffn_up")
GATHERED = ROW_SHARDED + COL_SHARDED + ("conv_w",)
BIG = ROW_SHARDED + COL_SHARDED


SEM_SPEC = pl.BlockSpec(memory_space=pltpu.SEMAPHORE)
TOKEN = jax.ShapeDtypeStruct((8, LANE), F32)
SPLIT_EFFECT = pltpu.SideEffectType.DATAFLOW_SIDE_EFFECTING
GATHER_ROWS = "gather_rows"
GATHER_SLOT = "gather_slot"
SCATTER_ROWS = "scatter_rows"
SCATTER_SLOT = "scatter_slot"


def _land_shape(kind, src):
    if kind == GATHER_ROWS:
        return (N_DEV * src.shape[0],) + src.shape[1:]
    if kind == GATHER_SLOT:
        return (N_DEV,) + src.shape
    if kind == SCATTER_ROWS:
        return (N_DEV, src.shape[0] // N_DEV) + src.shape[1:]
    return src.shape


def _views(kind, src_ref, land_ref, pos, k):
    me = _flat(pos)
    if kind == GATHER_ROWS:
        r = src_ref.shape[0]
        return src_ref, land_ref.at[pl.ds(pl.multiple_of(me * r, 16), r), :]
    if kind == GATHER_SLOT:
        return src_ref, land_ref.at[me]
    dev = _flat(_peer(pos, k))
    if kind == SCATTER_ROWS:
        r = land_ref.shape[1]
        return src_ref.at[pl.ds(pl.multiple_of(dev * r, 16), r), :], land_ref.at[k]
    return src_ref.at[dev], land_ref.at[k]


def _hbm(x):
    return pltpu.with_memory_space_constraint(x, pltpu.HBM)


def _exchange_start(items, after, name):
    kinds = [k for k, _ in items]
    srcs = [_hbm(s) for _, s in items]
    lands = [_hbm(lax.empty(_land_shape(k, s), s.dtype)) for k, s in items]
    n = len(items)
    n_copy = n * (N_DEV - 1)

    def body(*refs):
        src_refs, land_refs = refs[:n], refs[n:2 * n]
        send_sems, recv_sems = refs[2 * n + 1], refs[2 * n + 2]
        token_ref = refs[4 * n + 3]
        pos = _mesh_pos()
        for i, kind in enumerate(kinds):
            for k in range(1, N_DEV):
                s, d = _views(kind, src_refs[i], land_refs[i], pos, k)
                j = i * (N_DEV - 1) + k - 1
                pltpu.make_async_remote_copy(src_ref=s, dst_ref=d, send_sem=send_sems.at[j], recv_sem=recv_sems.at[j],
                                             device_id=_peer(pos, k), device_id_type=MESH_ID).start()
        token_ref[...] = jnp.zeros_like(token_ref)

    arrs = srcs + lands
    outs = pl.pallas_call(
        body, name=name,
        in_specs=[HBM_SPEC] * (2 * n + 1),
        out_specs=[SEM_SPEC, SEM_SPEC] + [HBM_SPEC] * (2 * n) + [pl.BlockSpec(memory_space=pltpu.VMEM)],
        out_shape=[pltpu.SemaphoreType.DMA((n_copy,)), pltpu.SemaphoreType.DMA((n_copy,))]
        + [pltpu.HBM(a.shape, a.dtype) for a in arrs] + [TOKEN],
        input_output_aliases={i: 2 + i for i in range(2 * n)},
        compiler_params=pltpu.CompilerParams(has_side_effects=SPLIT_EFFECT),
    )(*arrs, after)
    return {"kinds": kinds, "send": outs[0], "recv": outs[1], "arrs": outs[2:2 + 2 * n], "token": outs[-1]}


def _exchange_wait(ex, after, name):
    kinds = ex["kinds"]
    n = len(kinds)

    def body(*refs):
        src_refs, land_refs = refs[:n], refs[n:2 * n]
        send_sems, recv_sems = refs[2 * n], refs[2 * n + 1]
        token_ref = refs[-1]
        pos = _mesh_pos()
        for i, kind in enumerate(kinds):
            for k in range(1, N_DEV):
                s, d = _views(kind, src_refs[i], land_refs[i], pos, k)
                j = i * (N_DEV - 1) + k - 1
                cp = pltpu.make_async_remote_copy(src_ref=s, dst_ref=d, send_sem=send_sems.at[j],
                                                  recv_sem=recv_sems.at[j], device_id=_peer(pos, k),
                                                  device_id_type=MESH_ID)
                cp.wait_send()
                cp.wait_recv()
        token_ref[...] = jnp.zeros_like(token_ref)

    outs = pl.pallas_call(
        body, name=name,
        in_specs=[HBM_SPEC] * (2 * n) + [SEM_SPEC, SEM_SPEC, HBM_SPEC],
        out_specs=[HBM_SPEC] * (2 * n) + [pl.BlockSpec(memory_space=pltpu.VMEM)],
        out_shape=[pltpu.HBM(a.shape, a.dtype) for a in ex["arrs"]] + [TOKEN],
        input_output_aliases={i: i for i in range(2 * n)},
        compiler_params=pltpu.CompilerParams(has_side_effects=SPLIT_EFFECT),
    )(*ex["arrs"], ex["send"], ex["recv"], after)
    lands = [_place_own(k, s, d) for k, s, d in zip(kinds, outs[:n], outs[n:2 * n])]
    return lands, outs[-1]


def _place_own(kind, src, land):
    me = _flat(_mesh_pos())
    zeros = (0,) * (src.ndim - 1)
    if kind == GATHER_ROWS:
        return lax.dynamic_update_slice(land, src, (me * src.shape[0],) + zeros)
    if kind == GATHER_SLOT:
        return lax.dynamic_update_slice(land, src[None], (me,) + (0,) * src.ndim)
    if kind == SCATTER_ROWS:
        r = land.shape[1]
        own = lax.dynamic_slice(src, (me * r,) + zeros, (r,) + src.shape[1:])
    else:
        own = lax.dynamic_index_in_dim(src, me, 0, keepdims=False)
    return lax.dynamic_update_slice(land, own[None], (0,) * land.ndim)


def _all_gather_small(x, name):
    r, w = x.shape

    def body(x_ref, out_ref, send_sems, recv_sems):
        pos = _mesh_pos()
        me = _flat(pos)
        copies = []
        for k in range(1, N_DEV):
            cp = pltpu.make_async_remote_copy(
                src_ref=x_ref, dst_ref=out_ref.at[me], send_sem=send_sems.at[k - 1], recv_sem=recv_sems.at[k - 1],
                device_id=_peer(pos, k), device_id_type=MESH_ID)
            cp.start()
            copies.append(cp)
        out_ref[me] = x_ref[...]
        for cp in copies:
            cp.wait()

    vmem = pl.BlockSpec(memory_space=pltpu.VMEM)
    return pl.pallas_call(
        body, name=name, in_specs=[vmem], out_specs=vmem,
        out_shape=jax.ShapeDtypeStruct((N_DEV, r, w), x.dtype),
        scratch_shapes=[pltpu.SemaphoreType.DMA((N_DEV - 1,)), pltpu.SemaphoreType.DMA((N_DEV - 1,))],
        compiler_params=pltpu.CompilerParams(has_side_effects=True),
    )(x)


def _cols(g, lo, hi):
    c = g.shape[-1]
    parts = []
    for d in range(N_DEV):
        a, b = max(lo, d * c), min(hi, (d + 1) * c)
        if a < b:
            parts.append(g[d, :, a - d * c:b - d * c])
    return parts[0] if len(parts) == 1 else jnp.concatenate(parts, axis=1)


def _col_chunks(g):
    c = g.shape[-1] // N_DEV
    return jnp.stack([g[:, d * c:(d + 1) * c] for d in range(N_DEV)])


IN_PART = ("w_in", "conv_w")
OUT_PART = ROW_SHARDED + ("w_ffn_gate", "w_ffn_up")


def _gather_items(w, names, l):
    items = []
    for n in names:
        blk = w[n][l] if n == "conv_w" else w[n][l].astype(BF16)
        items.append((GATHER_ROWS if n in ROW_SHARDED else GATHER_SLOT, blk))
    return items


def _scatter_items(grads, names):
    return [(SCATTER_ROWS, grads[n]) if n in ROW_SHARDED else (SCATTER_SLOT, _col_chunks(grads[n]))
            for n in names]


SMALL = ("ln_in_g", "ln_in_b", "conv_b", "dt_bias", "a_log", "d_skip", "ssd_norm_w", "att_sinks",
         "ln_mix_g", "ln_mix_b", "ln_ffn_g", "ln_ffn_b")


def _pack_small(vals):
    flat = jnp.concatenate([vals[n].reshape(-1) for n in SMALL])
    n = flat.shape[0]
    rows = -(-n // LANE)
    rows = -(-rows // 8) * 8
    return jnp.pad(flat, (0, rows * LANE - n)).reshape(rows, LANE)


def _unpack_small(buf, shapes):
    flat = buf.reshape(-1)
    off = 0
    out = {}
    for n in SMALL:
        cnt = math.prod(shapes[n])
        out[n] = flat[off:off + cnt].reshape(shapes[n])
        off += cnt
    return out


def _to_group_major(v):
    lead = v.shape[:-1]
    t = v.reshape(lead + (SSD_GROUPS, HEADS_PER_GROUP))
    t = jnp.pad(t, [(0, 0)] * len(lead) + [(0, 0), (0, LANE - HEADS_PER_GROUP)])
    return t.reshape(lead + (DT_PAD,))


def _from_group_major(v):
    lead = v.shape[:-1]
    return v.reshape(lead + (SSD_GROUPS, LANE))[..., :HEADS_PER_GROUP].reshape(lead + (SSD_HEADS,))


def _rows8(v):
    return jnp.pad(v, ((0, 8 - v.shape[0]), (0, 0)))


IN_OFFS = {"q": (0, 1024), "kv": (1024, 1280), "z": (1280, 3328), "xs": (3328, 5376), "b": (5376, 5888),
           "c": (5888, 6400), "dt": (6400, 6432), "gl": (6432, 8480)}
PIECES = ("q", "kv", "z", "xs", "b", "c", "dt", "gl")


def _split_w_in(g):
    out = {p: _cols(g, lo, hi) for p, (lo, hi) in IN_OFFS.items()}
    out["dt"] = _to_group_major(out["dt"])
    return out


def _join_dw_in(dws):
    dws = dict(dws)
    dws["dt"] = _from_group_major(dws["dt"])
    return jnp.concatenate([dws[p] for p in PIECES], axis=1)


def _params_out(W):
    p = {n: W[n] for n in ROW_SHARDED}
    for n in ("w_ffn_gate", "w_ffn_up"):
        p[n] = _cols(W[n], 0, FFN_HIDDEN)
    return p


def _params_in(l, W, sm):
    p = {"w_in": _split_w_in(W["w_in"])}
    cw = _cols(W["conv_w"], 0, SSD_D_INNER + 2 * BC_DIM)
    cb = sm["conv_b"][l]
    segs = {"xs": (0, 2048), "b": (2048, 2560), "c": (2560, 3072)}
    p["conv_w8"] = {s: _rows8(cw[:, lo:hi]) for s, (lo, hi) in segs.items()}
    p["conv_b8"] = {s: _rows8(cb[None, lo:hi]) for s, (lo, hi) in segs.items()}
    p["ssd_par"] = _rows8(jnp.stack([_to_group_major(sm["dt_bias"][l]), _to_group_major(sm["a_log"][l]),
                                     _to_group_major(sm["d_skip"][l])]))
    p["norm_w"] = sm["ssd_norm_w"][l]
    p["sinks8"] = _rows8(jnp.pad(sm["att_sinks"][l], (0, LANE - ATT_HEADS))[None])
    for n in ("ln_mix_g", "ln_mix_b", "ln_ffn_g", "ln_ffn_b"):
        p[n] = sm[n][l]
    return p


def _fwd_mixers(h0, p, l, dep=None):
    tag = f"l{l}_"
    a = {"h0": h0}
    for pc in PIECES:
        a[pc] = _mm(h0, p["w_in"][pc], "nn", tag + "proj_" + pc, dep=dep)
    for s in ("xs", "b", "c"):
        a[s + "c"] = _conv_fwd(a[s], p["conv_w8"][s], p["conv_b8"][s], tag + "conv_" + s)
    a["y"], a["hs"] = _ssd_fwd(a["xsc"], a["bc"], a["cc"], a["dt"], p["ssd_par"], tag + "ssd_fwd")
    a["yn"] = _gnorm_fwd(a["y"], a["z"], p["norm_w"], tag + "gnorm")
    a["att"] = _att_fwd(a["q"], a["kv"], p["sinks8"], tag + "att_fwd")
    return a


def _fwd_out(a, p, l, dep=None):
    tag = f"l{l}_"
    h0 = a["h0"]
    a["ya"] = _mm(a["yn"], p["w_ssd_out"], "nn", tag + "ssd_out", dep=dep)
    a["yb"] = _mm(a["att"], p["w_att_out"], "nn", tag + "att_out", dep=dep)
    a["merged"] = _merge_fwd(a["gl"], a["ya"], a["yb"], tag + "merge")
    a["mix"] = _mm(a["merged"], p["w_mix_out"], "nn", tag + "mix_out")
    a["h1"] = _ln_fwd(h0, a["mix"], p["ln_mix_g"], p["ln_mix_b"], ALPHA, tag + "ln_mix")
    a["fg"] = _mm(a["h1"], p["w_ffn_gate"], "nn", tag + "ffn_gate")
    a["fu"] = _mm(a["h1"], p["w_ffn_up"], "nn", tag + "ffn_up")
    a["act"] = _swiglu_fwd(a["fg"], a["fu"], tag + "swiglu")
    a["ffn"] = _mm(a["act"], p["w_ffn_down"], "nn", tag + "ffn_down")
    a["h2"] = _ln_fwd(a["h1"], a["ffn"], p["ln_ffn_g"], p["ln_ffn_b"], ALPHA, tag + "ln_ffn")
    return a


def _dw(x, dy, name, dep=None):
    return _mm(x, dy, "tn", name, out_dtype=BF16, dep=dep)


def _bwd_out(a, p, dh2, l, dep=None):
    tag = f"l{l}_b_"
    gw, gs = {}, {}
    du2, acc = _ln_bwd(a["h1"], a["ffn"], p["ln_ffn_g"], dh2, ALPHA, tag + "ln_ffn")
    gs["ln_ffn_g"], gs["ln_ffn_b"] = acc[0], acc[1]
    gw["w_ffn_down"] = _dw(a["act"], du2, tag + "dw_down", dep=dep)
    dact = _mm(du2, p["w_ffn_down"], "nt", tag + "dact", dep=dep)
    dfg, dfu = _swiglu_bwd(a["fg"], a["fu"], dact, tag + "swiglu")
    gw["w_ffn_gate"] = _dw(a["h1"], dfg, tag + "dw_gate")
    gw["w_ffn_up"] = _dw(a["h1"], dfu, tag + "dw_up")
    dh1 = _mm(dfg, p["w_ffn_gate"], "nt", tag + "dh1_gate", add=du2, add_scale=ALPHA)
    dh1 = _mm(dfu, p["w_ffn_up"], "nt", tag + "dh1_up", add=dh1)
    du1, acc = _ln_bwd(a["h0"], a["mix"], p["ln_mix_g"], dh1, ALPHA, tag + "ln_mix")
    gs["ln_mix_g"], gs["ln_mix_b"] = acc[0], acc[1]
    gw["w_mix_out"] = _dw(a["merged"], du1, tag + "dw_mix")
    dmerged = _mm(du1, p["w_mix_out"], "nt", tag + "dmerged")
    dya, dyb, dgl = _merge_bwd(a["gl"], a["ya"], a["yb"], dmerged, tag + "merge")
    gw["w_ssd_out"] = _dw(a["yn"], dya, tag + "dw_ssd")
    gw["w_att_out"] = _dw(a["att"], dyb, tag + "dw_att")
    return {"du1": du1, "dya": dya, "dyb": dyb, "dgl": dgl}, gw, gs


def _bwd_mixers(a, p, carry, l, dep=None):
    tag = f"l{l}_b_"
    gs = {}
    du1, dgl = carry["du1"], carry["dgl"]
    dyn = _mm(carry["dya"], p["w_ssd_out"], "nt", tag + "dyn", dep=dep)
    datt = _mm(carry["dyb"], p["w_att_out"], "nt", tag + "datt", out_dtype=BF16, dep=dep)
    dq, dkv, acc = _att_bwd(a["q"], a["kv"], p["sinks8"], datt, tag + "att")
    gs["att_sinks"] = acc[0, :ATT_HEADS]
    dy, dz, acc = _gnorm_bwd(a["y"], a["z"], p["norm_w"], dyn, tag + "gnorm")
    gs["ssd_norm_w"] = acc[0]
    dxs, dbm, dcm, ddt, acc = _ssd_bwd(a["xsc"], a["bc"], a["cc"], a["dt"], p["ssd_par"], a["hs"], dy,
                                       tag + "ssd")
    gs["dt_bias"], gs["a_log"], gs["d_skip"] = (_from_group_major(acc[i]) for i in range(3))
    dpieces = {"q": dq, "kv": dkv, "z": dz, "dt": ddt, "gl": dgl}
    dconv_w, dconv_b = [], []
    for s, dout in (("xs", dxs), ("b", dbm), ("c", dcm)):
        dc, acc = _conv_bwd_pre(a[s], p["conv_w8"][s], p["conv_b8"][s], dout, tag + "conv_pre_" + s)
        dconv_w.append(acc[:CONV_TAPS])
        dconv_b.append(acc[CONV_TAPS])
        dpieces[s] = _conv_bwd_in(dc, p["conv_w8"][s], tag + "conv_in_" + s)
    gconv = jnp.concatenate(dconv_w, axis=1)
    gs["conv_b"] = jnp.concatenate(dconv_b)
    dws = {}
    dh0 = du1
    scale = ALPHA
    for pc in PIECES:
        dws[pc] = _dw(a["h0"], dpieces[pc], tag + "dw_in_" + pc)
        dh0 = _mm(dpieces[pc], p["w_in"][pc], "nt", tag + "dh0_" + pc, add=dh0, add_scale=scale)
        scale = 1.0
    return dh0, _join_dw_in(dws), gconv, gs


def _step(x, target, w, m, v):
    x2 = x[0]
    t2 = target[0]
    tok = jnp.zeros(TOKEN.shape, TOKEN.dtype)

    ex = _exchange_start(_gather_items(w, IN_PART, 0), tok, "gather_l0_in_start")
    lands, tok = _exchange_wait(ex, ex["token"], "gather_l0_in_wait")
    p0 = _params_in(0, dict(zip(IN_PART, lands)), w)
    ex = _exchange_start(_gather_items(w, OUT_PART, 0) + _gather_items(w, IN_PART, 1), tok,
                         "gather_l0_out_l1_in_start")
    h = _ln_fwd(x2, None, w["ln_in_g"], w["ln_in_b"], 1.0, "ln_in")
    a0 = _fwd_mixers(h, p0, 0, dep=ex["token"])
    lands, tok = _exchange_wait(ex, a0["att"], "gather_l0_out_l1_in_wait")
    p0.update(_params_out(dict(zip(OUT_PART, lands))))
    p1 = _params_in(1, dict(zip(IN_PART, lands[len(OUT_PART):])), w)
    ex = _exchange_start(_gather_items(w, OUT_PART, 1), tok, "gather_l1_out_start")
    a0 = _fwd_out(a0, p0, 0, dep=ex["token"])
    lands, tok = _exchange_wait(ex, a0["h2"], "gather_l1_out_wait")
    p1.update(_params_out(dict(zip(OUT_PART, lands))))
    a1 = _fwd_out(_fwd_mixers(a0["h2"], p1, 1), p1, 1)

    sse, dh = _loss_fwd_bwd(a1["h2"], t2, "loss")
    loss = lax.psum(0.5 / D_MODEL * sse[0, 0], ("x", "y", "c"))

    carry, gw1, gs1 = _bwd_out(a1, p1, dh, 1)
    dh, gw1["w_in"], gw1["conv_w"], gs = _bwd_mixers(a1, p1, carry, 1)
    gs1.update(gs)
    ex1 = _exchange_start(_scatter_items(gw1, GATHERED), tok, "scatter_l1_start")
    carry, gw0, gs0 = _bwd_out(a0, p0, dh, 0, dep=ex1["token"])
    lands, tok = _exchange_wait(ex1, carry["dgl"], "scatter_l1_wait")
    land1 = dict(zip(GATHERED, lands))
    ex0 = _exchange_start(_scatter_items(gw0, OUT_PART), tok, "scatter_l0_out_start")
    dh, gw0["w_in"], gw0["conv_w"], gs = _bwd_mixers(a0, p0, carry, 0, dep=ex0["token"])
    gs0.update(gs)
    lands, tok = _exchange_wait(ex0, dh, "scatter_l0_out_wait")
    land0 = dict(zip(OUT_PART, lands))
    ex0 = _exchange_start(_scatter_items(gw0, IN_PART), tok, "scatter_l0_in_start")
    grad_x2, acc = _ln_bwd(x2, None, w["ln_in_g"], dh, 1.0, "ln_in_b")

    outs = [{} for _ in range(4)]

    def update(names):
        res = None
        for n in names:
            res = _adamw([land0[n], land1[n]], w[n], m[n], v[n], "adamw_" + n)
            for o, t in zip(outs, res):
                o[n] = t
        return res[1]

    update(OUT_PART)
    gsm = {"ln_in_g": acc[0], "ln_in_b": acc[1]}
    for n in SMALL[2:]:
        gsm[n] = jnp.stack([gs0[n], gs1[n]])
    small_shapes = {n: w[n].shape for n in SMALL}
    land_s = _all_gather_small(_pack_small(gsm), "small_grads_all_gather")
    res = _adamw([land_s], _pack_small(w)[None], _pack_small(m)[None], _pack_small(v)[None], "adamw_small")
    for o, t in zip(outs, res):
        o.update(_unpack_small(t[0], small_shapes))
    lands, _ = _exchange_wait(ex0, res[1], "scatter_l0_in_wait")
    land0.update(zip(IN_PART, lands))
    update(IN_PART)
    return loss, grad_x2[None], outs


WEIGHT_NAMES = ("ln_in_g", "ln_in_b", "w_in", "conv_w", "conv_b", "dt_bias", "a_log", "d_skip", "ssd_norm_w",
                "att_sinks", "w_ssd_out", "w_att_out", "w_mix_out", "ln_mix_g", "ln_mix_b", "w_ffn_gate",
                "w_ffn_up", "w_ffn_down", "ln_ffn_g", "ln_ffn_b")


def kernel(x, ln_in_g, ln_in_b, w_in, conv_w, conv_b, dt_bias, a_log, d_skip, ssd_norm_w, att_sinks, w_ssd_out, w_att_out, w_mix_out, ln_mix_g, ln_mix_b, w_ffn_gate, w_ffn_up, w_ffn_down, ln_ffn_g, ln_ffn_b, loss_target, m_ln_in_g, m_ln_in_b, m_w_in, m_conv_w, m_conv_b, m_dt_bias, m_a_log, m_d_skip, m_ssd_norm_w, m_att_sinks, m_w_ssd_out, m_w_att_out, m_w_mix_out, m_ln_mix_g, m_ln_mix_b, m_w_ffn_gate, m_w_ffn_up, m_w_ffn_down, m_ln_ffn_g, m_ln_ffn_b, v_ln_in_g, v_ln_in_b, v_w_in, v_conv_w, v_conv_b, v_dt_bias, v_a_log, v_d_skip, v_ssd_norm_w, v_att_sinks, v_w_ssd_out, v_w_att_out, v_w_mix_out, v_ln_mix_g, v_ln_mix_b, v_w_ffn_gate, v_w_ffn_up, v_w_ffn_down, v_ln_ffn_g, v_ln_ffn_b):
    w = dict(zip(WEIGHT_NAMES, (ln_in_g, ln_in_b, w_in, conv_w, conv_b, dt_bias, a_log, d_skip, ssd_norm_w,
                                att_sinks, w_ssd_out, w_att_out, w_mix_out, ln_mix_g, ln_mix_b, w_ffn_gate,
                                w_ffn_up, w_ffn_down, ln_ffn_g, ln_ffn_b)))
    m = dict(zip(WEIGHT_NAMES, (m_ln_in_g, m_ln_in_b, m_w_in, m_conv_w, m_conv_b, m_dt_bias, m_a_log, m_d_skip,
                                m_ssd_norm_w, m_att_sinks, m_w_ssd_out, m_w_att_out, m_w_mix_out, m_ln_mix_g,
                                m_ln_mix_b, m_w_ffn_gate, m_w_ffn_up, m_w_ffn_down, m_ln_ffn_g, m_ln_ffn_b)))
    v = dict(zip(WEIGHT_NAMES, (v_ln_in_g, v_ln_in_b, v_w_in, v_conv_w, v_conv_b, v_dt_bias, v_a_log, v_d_skip,
                                v_ssd_norm_w, v_att_sinks, v_w_ssd_out, v_w_att_out, v_w_mix_out, v_ln_mix_g,
                                v_ln_mix_b, v_w_ffn_gate, v_w_ffn_up, v_w_ffn_down, v_ln_ffn_g, v_ln_ffn_b)))
    loss, grad_x, outs = _step(x, loss_target, w, m, v)
    result = [loss, grad_x]
    for o in outs:
        result.extend(o[n] for n in WEIGHT_NAMES)
    return tuple(result)
```

```python
import functools
import math

import jax
import jax.numpy as jnp
from jax import lax
from jax.experimental import pallas as pl
from jax.experimental.pallas import tpu as pltpu

F32 = jnp.float32
BF16 = jnp.bfloat16

D_MODEL = 1024
DEPTH = 2
N_DEV = 8
ATT_HEADS = 16
ATT_KV_HEADS = 2
ATT_HEAD_DIM = 64
ATT_BLOCK = 128
SSD_D_INNER = 2048
SSD_HEADS = 32
SSD_GROUPS = 4
SSD_STATE = 128
SSD_CHUNK = 128
FFN_HIDDEN = 2816
LN_EPS = 1e-5
RMS_EPS = 1e-5
ALPHA = (2 * DEPTH) ** 0.25
Q_DIM = 1024
KV_DIM = 128
BC_DIM = 512
IN_DIM = 8480
IN_SHARD = IN_DIM // N_DEV
DT_PAD = 512

ADAM_LR = 0.001
ADAM_B1 = 0.9
ADAM_B2 = 0.999
ADAM_EPS = 1e-08
ADAM_WD = 0.01
ADAM_STEP = 10

LANE = 128
VMEM_LIMIT = 48 * 1024 * 1024
PACK_W = 1024
NEG = -1e30

_NN = (((1,), (0,)), ((), ()))
_NT = (((1,), (1,)), ((), ()))
_TN = (((0,), (0,)), ((), ()))
MESH_ID = pl.DeviceIdType.MESH


def _dot(a, b, dims=_NN):
    return lax.dot_general(a, b, dims, preferred_element_type=F32)


def _dot_hi(a, b):
    return lax.dot_general(a, b, _NN, preferred_element_type=F32, precision=lax.Precision.HIGHEST)


def _sig(x):
    return 1.0 / (1.0 + jnp.exp(-x))


def _softplus(x):
    return jnp.maximum(x, 0.0) + jnp.log(1.0 + jnp.exp(-jnp.abs(x)))


def _cparams(*sem):
    return pltpu.CompilerParams(dimension_semantics=sem, vmem_limit_bytes=VMEM_LIMIT)


def _pick(n, cap):
    if n <= cap:
        return n
    best = None
    for t in range(LANE, cap + 1, LANE):
        if n % t == 0:
            best = t
    assert best is not None, (n, cap)
    return best


def _tile(n):
    if n <= 1024 or n % 1024 == 0:
        return min(n, 1024)
    return _pick(n, 1408)


def _rows(n):
    return min(512, n)


def _mm(a, b, mode, name, add=None, add_scale=1.0, out_dtype=F32, dep=None):
    if mode == "nn":
        m, k = a.shape
        n = b.shape[1]
    elif mode == "nt":
        m, k = a.shape
        n = b.shape[0]
    else:
        k, m = a.shape
        n = b.shape[1]
    tm = _tile(m)
    tn = _tile(n)
    tk = _tile(k)
    nk = k // tk
    has_add = add is not None
    dims = {"nn": _NN, "nt": _NT, "tn": _TN}[mode]

    def body(*refs):
        if dep is not None:
            refs = refs[:-3] + refs[-2:]
        if has_add:
            a_ref, b_ref, add_ref, o_ref, acc_ref = refs
        else:
            a_ref, b_ref, o_ref, acc_ref = refs
        kk = pl.program_id(2)

        @pl.when(kk == 0)
        def _():
            if has_add:
                acc_ref[...] = add_scale * add_ref[...].astype(F32)
            else:
                acc_ref[...] = jnp.zeros_like(acc_ref)

        acc_ref[...] += _dot(a_ref[...].astype(BF16), b_ref[...].astype(BF16), dims)

        @pl.when(kk == nk - 1)
        def _():
            o_ref[...] = acc_ref[...].astype(o_ref.dtype)

    if mode == "nn":
        a_spec = pl.BlockSpec((tm, tk), lambda i, j, kk: (i, kk))
        b_spec = pl.BlockSpec((tk, tn), lambda i, j, kk: (kk, j))
    elif mode == "nt":
        a_spec = pl.BlockSpec((tm, tk), lambda i, j, kk: (i, kk))
        b_spec = pl.BlockSpec((tn, tk), lambda i, j, kk: (j, kk))
    else:
        a_spec = pl.BlockSpec((tk, tm), lambda i, j, kk: (kk, i))
        b_spec = pl.BlockSpec((tk, tn), lambda i, j, kk: (kk, j))
    o_spec = pl.BlockSpec((tm, tn), lambda i, j, kk: (i, j))
    in_specs = [a_spec, b_spec] + ([o_spec] if has_add else [])
    args = (a, b) + ((add,) if has_add else ())
    if dep is not None:
        in_specs.append(pl.BlockSpec((8, LANE), lambda i, j, kk: (0, 0)))
        args += (dep,)
    return pl.pallas_call(
        body, name=name, grid=(m // tm, n // tn, nk),
        in_specs=in_specs, out_specs=o_spec,
        out_shape=jax.ShapeDtypeStruct((m, n), out_dtype),
        scratch_shapes=[pltpu.VMEM((tm, tn), F32)],
        compiler_params=_cparams("parallel", "parallel", "arbitrary"),
    )(*args)


def _vec_spec(width):
    return pl.BlockSpec((1, width), lambda i: (0, 0))


def _ln_fwd(a, b, gamma, beta, alpha, name):
    n_rows, dm = a.shape
    has_b = b is not None

    def body(*refs):
        if has_b:
            a_ref, b_ref, g_ref, be_ref, o_ref = refs
            u = alpha * a_ref[...] + b_ref[...]
        else:
            a_ref, g_ref, be_ref, o_ref = refs
            u = a_ref[...]
        mu = jnp.mean(u, axis=-1, keepdims=True)
        d = u - mu
        var = jnp.mean(d * d, axis=-1, keepdims=True)
        o_ref[...] = d * lax.rsqrt(var + LN_EPS) * g_ref[...] + be_ref[...]

    row = pl.BlockSpec((_rows(n_rows),dm), lambda i: (i, 0))
    in_specs = [row] + ([row] if has_b else []) + [_vec_spec(dm), _vec_spec(dm)]
    args = (a,) + ((b,) if has_b else ()) + (gamma.reshape(1, dm), beta.reshape(1, dm))
    return pl.pallas_call(
        body, name=name, grid=(n_rows // _rows(n_rows),), in_specs=in_specs, out_specs=row,
        out_shape=jax.ShapeDtypeStruct((n_rows, dm), F32),
        compiler_params=_cparams("parallel"),
    )(*args)


def _ln_bwd(a, b, gamma, dy, alpha, name):
    n_rows, dm = a.shape
    has_b = b is not None

    def body(*refs):
        if has_b:
            a_ref, b_ref, g_ref, dy_ref, du_ref, acc_ref = refs
            u = alpha * a_ref[...] + b_ref[...]
        else:
            a_ref, g_ref, dy_ref, du_ref, acc_ref = refs
            u = a_ref[...]

        @pl.when(pl.program_id(0) == 0)
        def _():
            acc_ref[...] = jnp.zeros_like(acc_ref)

        mu = jnp.mean(u, axis=-1, keepdims=True)
        d = u - mu
        var = jnp.mean(d * d, axis=-1, keepdims=True)
        rstd = lax.rsqrt(var + LN_EPS)
        xhat = d * rstd
        dyv = dy_ref[...]
        acc_ref[0:1, :] += jnp.sum(dyv * xhat, axis=0, keepdims=True)
        acc_ref[1:2, :] += jnp.sum(dyv, axis=0, keepdims=True)
        dxh = dyv * g_ref[...]
        m1 = jnp.mean(dxh, axis=-1, keepdims=True)
        m2 = jnp.mean(dxh * xhat, axis=-1, keepdims=True)
        du_ref[...] = rstd * (dxh - m1 - xhat * m2)

    row = pl.BlockSpec((_rows(n_rows),dm), lambda i: (i, 0))
    in_specs = [row] + ([row] if has_b else []) + [_vec_spec(dm), row]
    args = (a,) + ((b,) if has_b else ()) + (gamma.reshape(1, dm), dy)
    return pl.pallas_call(
        body, name=name, grid=(n_rows // _rows(n_rows),), in_specs=in_specs,
        out_specs=(row, pl.BlockSpec((8, dm), lambda i: (0, 0))),
        out_shape=(jax.ShapeDtypeStruct((n_rows, dm), F32), jax.ShapeDtypeStruct((8, dm), F32)),
        compiler_params=_cparams("arbitrary"),
    )(*args)


def _loss_fwd_bwd(y, target, name):
    n_rows, dm = y.shape

    def body(y_ref, t_ref, acc_ref, dy_ref):
        @pl.when(pl.program_id(0) == 0)
        def _():
            acc_ref[...] = jnp.zeros_like(acc_ref)

        d = y_ref[...] - t_ref[...]
        acc_ref[...] += jnp.sum(d * d)
        dy_ref[...] = d * (1.0 / dm)

    row = pl.BlockSpec((_rows(n_rows),dm), lambda i: (i, 0))
    return pl.pallas_call(
        body, name=name, grid=(n_rows // _rows(n_rows),), in_specs=[row, row],
        out_specs=(pl.BlockSpec((8, LANE), lambda i: (0, 0)), row),
        out_shape=(jax.ShapeDtypeStruct((8, LANE), F32), jax.ShapeDtypeStruct((n_rows, dm), F32)),
        compiler_params=_cparams("arbitrary"),
    )(y, target)


def _swiglu_fwd(g, u, name):
    n_rows, w = g.shape
    tw = _pick(w, 1408)

    def body(g_ref, u_ref, o_ref):
        gv = g_ref[...]
        o_ref[...] = (gv * _sig(gv) * u_ref[...]).astype(BF16)

    blk = pl.BlockSpec((_rows(n_rows),tw), lambda i, j: (i, j))
    return pl.pallas_call(
        body, name=name, grid=(n_rows // _rows(n_rows), w // tw), in_specs=[blk, blk], out_specs=blk,
        out_shape=jax.ShapeDtypeStruct((n_rows, w), BF16),
        compiler_params=_cparams("parallel", "parallel"),
    )(g, u)


def _swiglu_bwd(g, u, dact, name):
    n_rows, w = g.shape
    tw = _pick(w, 1408)

    def body(g_ref, u_ref, da_ref, dg_ref, du_ref):
        gv = g_ref[...]
        s = _sig(gv)
        da = da_ref[...]
        dg_ref[...] = (da * u_ref[...] * (s * (1.0 + gv * (1.0 - s)))).astype(BF16)
        du_ref[...] = (da * gv * s).astype(BF16)

    blk = pl.BlockSpec((_rows(n_rows),tw), lambda i, j: (i, j))
    return pl.pallas_call(
        body, name=name, grid=(n_rows // _rows(n_rows), w // tw), in_specs=[blk, blk, blk], out_specs=(blk, blk),
        out_shape=(jax.ShapeDtypeStruct((n_rows, w), BF16), jax.ShapeDtypeStruct((n_rows, w), BF16)),
        compiler_params=_cparams("parallel", "parallel"),
    )(g, u, dact)


def _merge_fwd(gl, ya, yb, name):
    n_rows, dm = ya.shape

    def body(gl_ref, ya_ref, yb_ref, o_ref):
        ga = _sig(gl_ref[:, :dm])
        gb = _sig(gl_ref[:, dm:])
        o_ref[...] = (ga * ya_ref[...] + gb * yb_ref[...]).astype(BF16)

    row = pl.BlockSpec((_rows(n_rows),dm), lambda i: (i, 0))
    row2 = pl.BlockSpec((_rows(n_rows),2 * dm), lambda i: (i, 0))
    return pl.pallas_call(
        body, name=name, grid=(n_rows // _rows(n_rows),), in_specs=[row2, row, row], out_specs=row,
        out_shape=jax.ShapeDtypeStruct((n_rows, dm), BF16),
        compiler_params=_cparams("parallel"),
    )(gl, ya, yb)


def _merge_bwd(gl, ya, yb, dmerged, name):
    n_rows, dm = ya.shape

    def body(gl_ref, ya_ref, yb_ref, dm_ref, dya_ref, dyb_ref, dgl_ref):
        ga = _sig(gl_ref[:, :dm])
        gb = _sig(gl_ref[:, dm:])
        dmv = dm_ref[...]
        dya_ref[...] = (dmv * ga).astype(BF16)
        dyb_ref[...] = (dmv * gb).astype(BF16)
        dgl_ref[:, :dm] = (dmv * ya_ref[...] * ga * (1.0 - ga)).astype(BF16)
        dgl_ref[:, dm:] = (dmv * yb_ref[...] * gb * (1.0 - gb)).astype(BF16)

    row = pl.BlockSpec((_rows(n_rows),dm), lambda i: (i, 0))
    row2 = pl.BlockSpec((_rows(n_rows),2 * dm), lambda i: (i, 0))
    return pl.pallas_call(
        body, name=name, grid=(n_rows // _rows(n_rows),), in_specs=[row2, row, row, row], out_specs=(row, row, row2),
        out_shape=(jax.ShapeDtypeStruct((n_rows, dm), BF16), jax.ShapeDtypeStruct((n_rows, dm), BF16),
                   jax.ShapeDtypeStruct((n_rows, 2 * dm), BF16)),
        compiler_params=_cparams("parallel"),
    )(gl, ya, yb, dmerged)


CONV_TAPS = 4
CONV_COLS = 512
HALO = 8


def _shift_down(cur, prev8, s, row8):
    r = pltpu.roll(cur, s, axis=0)
    top = jnp.where(row8 < s, pltpu.roll(prev8, s, axis=0), r[0:HALO])
    return jnp.concatenate([top, r[HALO:]], axis=0)


def _shift_up(cur, next8, s, row8):
    n = cur.shape[0]
    r = pltpu.roll(cur, n - s, axis=0)
    bot = jnp.where(row8 >= HALO - s, pltpu.roll(next8, HALO - s, axis=0), r[n - HALO:])
    return jnp.concatenate([r[:n - HALO], bot], axis=0)


def _conv_pre(u_ref, prev_ref, w_ref, b_ref, li):
    cur = u_ref[...]
    prev8 = jnp.where(li == 0, 0.0, prev_ref[...])
    row8 = lax.broadcasted_iota(jnp.int32, prev8.shape, 0)
    shifted = [cur] + [_shift_down(cur, prev8, s, row8) for s in range(1, CONV_TAPS)]
    acc = b_ref[...] + shifted[0] * w_ref[CONV_TAPS - 1:CONV_TAPS, :]
    for s in range(1, CONV_TAPS):
        acc = acc + shifted[s] * w_ref[CONV_TAPS - 1 - s:CONV_TAPS - s, :]
    return acc, shifted


def _conv_specs(n_rows, tl):
    cur = pl.BlockSpec((tl, CONV_COLS), lambda cj, li: (li, cj))
    prev = pl.BlockSpec((HALO, CONV_COLS), lambda cj, li: (jnp.maximum(li * (tl // HALO) - 1, 0), cj))
    nxt = pl.BlockSpec((HALO, CONV_COLS),
                       lambda cj, li: (jnp.minimum((li + 1) * (tl // HALO), n_rows // HALO - 1), cj))
    par = pl.BlockSpec((8, CONV_COLS), lambda cj, li: (0, cj))
    return cur, prev, nxt, par


def _conv_fwd(u, w8, b8, name):
    n_rows, c = u.shape
    tl = _rows(n_rows)
    cur, prev, _, par = _conv_specs(n_rows, tl)

    def body(u_ref, prev_ref, w_ref, b_ref, o_ref):
        acc, _ = _conv_pre(u_ref, prev_ref, w_ref, b_ref[0:1, :], pl.program_id(1))
        o_ref[...] = acc * _sig(acc)

    return pl.pallas_call(
        body, name=name, grid=(c // CONV_COLS, n_rows // tl), in_specs=[cur, prev, par, par], out_specs=cur,
        out_shape=jax.ShapeDtypeStruct((n_rows, c), F32),
        compiler_params=_cparams("parallel", "parallel"),
    )(u, u, w8, b8)


def _conv_bwd_pre(u, w8, b8, dout, name):
    n_rows, c = u.shape
    tl = _rows(n_rows)
    cur, prev, _, par = _conv_specs(n_rows, tl)

    def body(u_ref, prev_ref, w_ref, b_ref, do_ref, dc_ref, acc_ref):
        @pl.when(pl.program_id(1) == 0)
        def _():
            acc_ref[...] = jnp.zeros_like(acc_ref)

        acc, shifted = _conv_pre(u_ref, prev_ref, w_ref, b_ref[0:1, :], pl.program_id(1))
        sg = _sig(acc)
        dc = do_ref[...] * (sg * (1.0 + acc * (1.0 - sg)))
        dc_ref[...] = dc
        for k in range(CONV_TAPS):
            acc_ref[k:k + 1, :] += jnp.sum(dc * shifted[CONV_TAPS - 1 - k], axis=0, keepdims=True)
        acc_ref[CONV_TAPS:CONV_TAPS + 1, :] += jnp.sum(dc, axis=0, keepdims=True)

    return pl.pallas_call(
        body, name=name, grid=(c // CONV_COLS, n_rows // tl), in_specs=[cur, prev, par, par, cur],
        out_specs=(cur, par),
        out_shape=(jax.ShapeDtypeStruct((n_rows, c), F32), jax.ShapeDtypeStruct((8, c), F32)),
        compiler_params=_cparams("parallel", "arbitrary"),
    )(u, u, w8, b8, dout)


def _conv_bwd_in(dc, w8, name):
    n_rows, c = dc.shape
    tl = _rows(n_rows)
    cur, _, nxt, par = _conv_specs(n_rows, tl)
    n_l = n_rows // tl

    def body(dc_ref, next_ref, w_ref, o_ref):
        cur_v = dc_ref[...]
        next8 = jnp.where(pl.program_id(1) == n_l - 1, 0.0, next_ref[...])
        row8 = lax.broadcasted_iota(jnp.int32, next8.shape, 0)
        acc = cur_v * w_ref[CONV_TAPS - 1:CONV_TAPS, :]
        for s in range(1, CONV_TAPS):
            acc = acc + _shift_up(cur_v, next8, s, row8) * w_ref[CONV_TAPS - 1 - s:CONV_TAPS - s, :]
        o_ref[...] = acc.astype(BF16)

    return pl.pallas_call(
        body, name=name, grid=(c // CONV_COLS, n_l), in_specs=[cur, nxt, par], out_specs=cur,
        out_shape=jax.ShapeDtypeStruct((n_rows, c), BF16),
        compiler_params=_cparams("parallel", "parallel"),
    )(dc, dc, w8)


NORM_GROUP = SSD_D_INNER // SSD_GROUPS


def _gnorm_fwd(y, z, w, name):
    n_rows, c = y.shape

    def body(y_ref, z_ref, w_ref, o_ref):
        zv = z_ref[...]
        yg = y_ref[...] * (zv * _sig(zv))
        r = lax.rsqrt(jnp.mean(yg * yg, axis=-1, keepdims=True) + RMS_EPS)
        o_ref[...] = (yg * r * w_ref[...]).astype(BF16)

    blk = pl.BlockSpec((_rows(n_rows),NORM_GROUP), lambda i, j: (i, j))
    wspec = pl.BlockSpec((1, NORM_GROUP), lambda i, j: (0, j))
    return pl.pallas_call(
        body, name=name, grid=(n_rows // _rows(n_rows), c // NORM_GROUP), in_specs=[blk, blk, wspec], out_specs=blk,
        out_shape=jax.ShapeDtypeStruct((n_rows, c), BF16),
        compiler_params=_cparams("parallel", "parallel"),
    )(y, z, w.reshape(1, c))


def _gnorm_bwd(y, z, w, dyn, name):
    n_rows, c = y.shape

    def body(y_ref, z_ref, w_ref, dn_ref, dy_ref, dz_ref, acc_ref):
        @pl.when(pl.program_id(1) == 0)
        def _():
            acc_ref[...] = jnp.zeros_like(acc_ref)

        zv = z_ref[...]
        yv = y_ref[...]
        sz = _sig(zv)
        silu = zv * sz
        yg = yv * silu
        r = lax.rsqrt(jnp.mean(yg * yg, axis=-1, keepdims=True) + RMS_EPS)
        nrm = yg * r
        dn = dn_ref[...]
        acc_ref[0:1, :] += jnp.sum(dn * nrm, axis=0, keepdims=True)
        dnw = dn * w_ref[...]
        dyg = r * (dnw - nrm * jnp.mean(dnw * nrm, axis=-1, keepdims=True))
        dy_ref[...] = dyg * silu
        dz_ref[...] = (dyg * yv * (sz * (1.0 + zv * (1.0 - sz)))).astype(BF16)

    blk = pl.BlockSpec((_rows(n_rows),NORM_GROUP), lambda j, i: (i, j))
    wspec = pl.BlockSpec((1, NORM_GROUP), lambda j, i: (0, j))
    aspec = pl.BlockSpec((8, NORM_GROUP), lambda j, i: (0, j))
    return pl.pallas_call(
        body, name=name, grid=(c // NORM_GROUP, n_rows // _rows(n_rows)), in_specs=[blk, blk, wspec, blk],
        out_specs=(blk, blk, aspec),
        out_shape=(jax.ShapeDtypeStruct((n_rows, c), F32), jax.ShapeDtypeStruct((n_rows, c), BF16),
                   jax.ShapeDtypeStruct((8, c), F32)),
        compiler_params=_cparams("parallel", "arbitrary"),
    )(y, z, w.reshape(1, c), dyn)


ATT_SCALE = ATT_HEAD_DIM ** -0.5
ATT_SLOPES = [2.0 ** (-8.0 * (h + 1) / ATT_HEADS) for h in range(ATT_HEADS)]
Q_PER_KV = ATT_HEADS // ATT_KV_HEADS


def _dup_half(t, g, lo):
    tr = pltpu.roll(t, ATT_HEAD_DIM, axis=1)
    return jnp.where(lo, t, tr) if g == 0 else jnp.where(lo, tr, t)


def _att_band(kv_ref, kvp_ref, n):
    cur = kv_ref[...]
    prev = jnp.where(n == 0, 0.0, kvp_ref[...])
    lo = lax.broadcasted_iota(jnp.int32, (ATT_BLOCK, LANE), 1) < ATT_HEAD_DIM
    bands = []
    for g in range(ATT_KV_HEADS):
        kb = jnp.concatenate([_dup_half(prev[:, :LANE], g, lo), _dup_half(cur[:, :LANE], g, lo)], axis=0)
        vb = jnp.concatenate([_dup_half(prev[:, LANE:], g, lo), _dup_half(cur[:, LANE:], g, lo)], axis=0)
        bands.append((kb.astype(BF16), vb.astype(BF16)))
    return bands


def _att_tile(n):
    shape = (2 * ATT_BLOCK, ATT_BLOCK)
    row = lax.broadcasted_iota(jnp.int32, shape, 0)
    i = row & (ATT_BLOCK - 1)
    s = lax.broadcasted_iota(jnp.int32, shape, 1)
    upper = s > i
    dist = ((i - s) & (ATT_BLOCK - 1)).astype(F32)
    dead = upper & (n == 0)
    return upper, dist, dead, row[:, 0:1] < ATT_BLOCK


def _stack_pair(t, lo):
    return jnp.concatenate([jnp.where(lo, t, 0.0), jnp.where(lo, 0.0, t)], axis=0).astype(BF16)


def _att_probs(qs, kb, s_ref, j, tile):
    upper, dist, dead, first = tile
    s2 = _dot(qs, kb, _NT)
    slope = jnp.where(first, ATT_SLOPES[2 * j], ATT_SLOPES[2 * j + 1])
    sink = jnp.where(first, s_ref[0:1, 2 * j:2 * j + 1], s_ref[0:1, 2 * j + 1:2 * j + 2])
    s = jnp.where(upper, s2[:, :ATT_BLOCK], s2[:, ATT_BLOCK:]) - slope * dist
    s = jnp.where(dead, NEG, s)
    m = jnp.maximum(jnp.max(s, axis=-1, keepdims=True), sink)
    p = jnp.exp(s - m)
    es = jnp.exp(sink - m)
    inv = 1.0 / (jnp.sum(p, axis=-1, keepdims=True) + es)
    return p * inv, es * inv


def _band_split(t, upper):
    return jnp.concatenate([jnp.where(upper, t, 0.0), jnp.where(upper, 0.0, t)], axis=1)


def _att_fwd(q, kv, sinks8, name):
    n_rows = q.shape[0]
    nb = n_rows // ATT_BLOCK

    def body(q_ref, kv_ref, kvp_ref, s_ref, o_ref):
        n = pl.program_id(0)
        bands = _att_band(kv_ref, kvp_ref, n)
        lo = lax.broadcasted_iota(jnp.int32, (ATT_BLOCK, LANE), 1) < ATT_HEAD_DIM
        tile = _att_tile(n)
        for j in range(ATT_HEADS // 2):
            kb, vb = bands[2 * j // Q_PER_KV]
            qs = _stack_pair(q_ref[:, j * LANE:(j + 1) * LANE] * ATT_SCALE, lo)
            p, _ = _att_probs(qs, kb, s_ref, j, tile)
            out = _dot(_band_split(p, tile[0]).astype(BF16), vb)
            o_ref[:, j * LANE:(j + 1) * LANE] = jnp.where(lo, out[:ATT_BLOCK], out[ATT_BLOCK:]).astype(BF16)

    return pl.pallas_call(
        body, name=name, grid=(nb,),
        in_specs=[pl.BlockSpec((ATT_BLOCK, Q_DIM), lambda n: (n, 0)),
                  pl.BlockSpec((ATT_BLOCK, 2 * LANE), lambda n: (n, 0)),
                  pl.BlockSpec((ATT_BLOCK, 2 * LANE), lambda n: (jnp.maximum(n - 1, 0), 0)),
                  pl.BlockSpec((8, LANE), lambda n: (0, 0))],
        out_specs=pl.BlockSpec((ATT_BLOCK, Q_DIM), lambda n: (n, 0)),
        out_shape=jax.ShapeDtypeStruct((n_rows, Q_DIM), BF16),
        compiler_params=_cparams("parallel"),
    )(q, kv, kv, sinks8)


def _att_bwd(q, kv, sinks8, dout, name):
    n_rows = q.shape[0]
    nb = n_rows // ATT_BLOCK

    def body(q_ref, kv_ref, kvp_ref, s_ref, do_ref, dq_ref, dkv_ref, acc_ref, carry_ref):
        n = pl.program_id(0)

        @pl.when(n == 0)
        def _():
            acc_ref[...] = jnp.zeros_like(acc_ref)
            carry_ref[...] = jnp.zeros_like(carry_ref)

        @pl.when(n == nb)
        def _():
            dkv_ref[...] = carry_ref[...].astype(BF16)

        @pl.when(n < nb)
        def _():
            bands = _att_band(kv_ref, kvp_ref, n)
            lo = lax.broadcasted_iota(jnp.int32, (ATT_BLOCK, LANE), 1) < ATT_HEAD_DIM
            lane1 = lax.broadcasted_iota(jnp.int32, (1, LANE), 1)
            tile = _att_tile(n)
            upper, first = tile[0], tile[3]
            dk_acc = [jnp.zeros((2 * ATT_BLOCK, LANE), F32) for _ in range(ATT_KV_HEADS)]
            dv_acc = [jnp.zeros((2 * ATT_BLOCK, LANE), F32) for _ in range(ATT_KV_HEADS)]
            dsink = jnp.zeros((1, LANE), F32)
            for j in range(ATT_HEADS // 2):
                g = 2 * j // Q_PER_KV
                kb, vb = bands[g]
                qs = _stack_pair(q_ref[:, j * LANE:(j + 1) * LANE] * ATT_SCALE, lo)
                dos = _stack_pair(do_ref[:, j * LANE:(j + 1) * LANE].astype(F32), lo)
                p, ps = _att_probs(qs, kb, s_ref, j, tile)
                dp2 = _dot(dos, vb, _NT)
                dp = jnp.where(upper, dp2[:, :ATT_BLOCK], dp2[:, ATT_BLOCK:])
                delta = jnp.sum(p * dp, axis=-1, keepdims=True)
                ds2 = _band_split(p * (dp - delta), upper)
                psd = ps * delta
                dsink = jnp.where(lane1 == 2 * j, -jnp.sum(jnp.where(first, psd, 0.0)), dsink)
                dsink = jnp.where(lane1 == 2 * j + 1, -jnp.sum(jnp.where(first, 0.0, psd)), dsink)
                dq = _dot(ds2.astype(BF16), kb) * ATT_SCALE
                dq_ref[:, j * LANE:(j + 1) * LANE] = jnp.where(lo, dq[:ATT_BLOCK], dq[ATT_BLOCK:]).astype(BF16)
                dk_acc[g] = dk_acc[g] + _dot(ds2.T.astype(BF16), qs)
                dv_acc[g] = dv_acc[g] + _dot(_band_split(p, upper).T.astype(BF16), dos)
            acc_ref[0:1, :] += dsink
            lo2 = lax.broadcasted_iota(jnp.int32, (2 * ATT_BLOCK, LANE), 1) < ATT_HEAD_DIM
            folded = []
            for acc in (dk_acc, dv_acc):
                t0 = acc[0] + pltpu.roll(acc[0], ATT_HEAD_DIM, axis=1)
                t1 = acc[1] + pltpu.roll(acc[1], ATT_HEAD_DIM, axis=1)
                folded.append(jnp.where(lo2, t0, t1))
            band = jnp.concatenate(folded, axis=1)
            dkv_ref[...] = (carry_ref[...] + band[:ATT_BLOCK]).astype(BF16)
            carry_ref[...] = band[ATT_BLOCK:]

    def qmap(n):
        return (jnp.minimum(n, nb - 1), 0)

    return pl.pallas_call(
        body, name=name, grid=(nb + 1,),
        in_specs=[pl.BlockSpec((ATT_BLOCK, Q_DIM), qmap),
                  pl.BlockSpec((ATT_BLOCK, 2 * LANE), qmap),
                  pl.BlockSpec((ATT_BLOCK, 2 * LANE), lambda n: (jnp.maximum(jnp.minimum(n, nb - 1) - 1, 0), 0)),
                  pl.BlockSpec((8, LANE), lambda n: (0, 0)),
                  pl.BlockSpec((ATT_BLOCK, Q_DIM), qmap)],
        out_specs=(pl.BlockSpec((ATT_BLOCK, Q_DIM), qmap),
                   pl.BlockSpec((ATT_BLOCK, 2 * LANE), lambda n: (jnp.maximum(n - 1, 0), 0)),
                   pl.BlockSpec((8, LANE), lambda n: (0, 0))),
        out_shape=(jax.ShapeDtypeStruct((n_rows, Q_DIM), BF16), jax.ShapeDtypeStruct((n_rows, 2 * LANE), BF16),
                   jax.ShapeDtypeStruct((8, LANE), F32)),
        scratch_shapes=[pltpu.VMEM((ATT_BLOCK, 2 * LANE), F32)],
        compiler_params=_cparams("arbitrary"),
    )(q, kv, kv, sinks8, dout)


HEADS_PER_GROUP = SSD_HEADS // SSD_GROUPS
PAIRS_PER_GROUP = HEADS_PER_GROUP // 2
T = SSD_CHUNK


def _ssd_scalars(dtr_ref, par_ref):
    dt = _softplus(dtr_ref[...] + par_ref[0:1, :])
    a = -jnp.exp(par_ref[1:2, :])
    ri = lax.broadcasted_iota(jnp.int32, (T, T), 0)
    ci = lax.broadcasted_iota(jnp.int32, (T, T), 1)
    tril = (ri >= ci).astype(F32)
    cs = _dot_hi(tril, dt * a)
    cst = cs.T
    return dt, a, cs, cst, ri, ci


def _lane_pick(lo, arr, k0):
    return jnp.where(lo, arr[:, k0:k0 + 1], arr[:, k0 + 1:k0 + 2])


def _ssd_fwd(xs, bm, cm, dtr, par, name):
    n_rows = xs.shape[0]
    nc = n_rows // T
    gw = PAIRS_PER_GROUP * LANE

    def body(x_ref, b_ref, c_ref, dtr_ref, par_ref, y_ref, hs_ref, h_ref):
        @pl.when(pl.program_id(1) == 0)
        def _():
            h_ref[...] = jnp.zeros_like(h_ref)

        dt, a, cs, cst, ri, ci = _ssd_scalars(dtr_ref, par_ref)
        tri = ri >= ci
        lo = lax.broadcasted_iota(jnp.int32, (T, LANE), 1) < SSD_CHUNK // 2
        ecs = jnp.exp(cs)
        dect = jnp.exp(cst[:, T - 1:T] - cst)
        etot = jnp.exp(cs[T - 1:T, :])
        bg = b_ref[...]
        cg = c_ref[...]
        bgt = bg.T
        cb = _dot(cg.astype(BF16), bg.astype(BF16), _NT)
        for j in range(PAIRS_PER_GROUP):
            xp = x_ref[:, j * LANE:(j + 1) * LANE]
            xdt = (xp * _lane_pick(lo, dt, 2 * j)).astype(BF16)
            hp = h_ref[j]
            hs_ref[0, 0, j] = hp
            hp_b = hp.astype(BF16)
            ys, ss = [], []
            for half in range(2):
                k = 2 * j + half
                lm = jnp.exp(jnp.where(tri, cs[:, k:k + 1] - cst[k:k + 1, :], NEG))
                yh = _dot((lm * cb).astype(BF16), xdt) + _dot((cg * ecs[:, k:k + 1]).astype(BF16), hp_b)
                ys.append(yh)
                ss.append(_dot((bgt * dect[k:k + 1, :]).astype(BF16), xdt))
            dsk = jnp.where(lo[0:1, :], par_ref[2:3, 2 * j:2 * j + 1], par_ref[2:3, 2 * j + 1:2 * j + 2])
            y_ref[:, j * LANE:(j + 1) * LANE] = jnp.where(lo, ys[0], ys[1]) + dsk * xp
            et = jnp.where(lo[0:1, :], etot[:, 2 * j:2 * j + 1], etot[:, 2 * j + 1:2 * j + 2])
            h_ref[j] = hp * et + jnp.where(lo, ss[0], ss[1])

    return pl.pallas_call(
        body, name=name, grid=(SSD_GROUPS, nc),
        in_specs=[pl.BlockSpec((T, gw), lambda g, c: (c, g)),
                  pl.BlockSpec((T, SSD_STATE), lambda g, c: (c, g)),
                  pl.BlockSpec((T, SSD_STATE), lambda g, c: (c, g)),
                  pl.BlockSpec((T, LANE), lambda g, c: (c, g)),
                  pl.BlockSpec((8, LANE), lambda g, c: (0, g))],
        out_specs=(pl.BlockSpec((T, gw), lambda g, c: (c, g)),
                   pl.BlockSpec((1, 1, PAIRS_PER_GROUP, SSD_STATE, LANE), lambda g, c: (g, c, 0, 0, 0))),
        out_shape=(jax.ShapeDtypeStruct((n_rows, SSD_D_INNER), F32),
                   jax.ShapeDtypeStruct((SSD_GROUPS, nc, PAIRS_PER_GROUP, SSD_STATE, LANE), F32)),
        scratch_shapes=[pltpu.VMEM((PAIRS_PER_GROUP, SSD_STATE, LANE), F32)],
        compiler_params=_cparams("parallel", "arbitrary"),
    )(xs, bm, cm, dtr, par)


def _ssd_bwd(xs, bm, cm, dtr, par, hs, dy, name):
    n_rows = xs.shape[0]
    nc = n_rows // T
    gw = PAIRS_PER_GROUP * LANE

    def body(x_ref, b_ref, c_ref, dtr_ref, par_ref, hs_ref, dy_ref,
             dx_ref, db_ref, dc_ref, ddtr_ref, acc_ref, dh_ref):
        @pl.when(pl.program_id(1) == 0)
        def _():
            dh_ref[...] = jnp.zeros_like(dh_ref)
            acc_ref[...] = jnp.zeros_like(acc_ref)

        dt, a, cs, cst, ri, ci = _ssd_scalars(dtr_ref, par_ref)
        tri = ri >= ci
        trit = ci >= ri
        lane = lax.broadcasted_iota(jnp.int32, (T, LANE), 1)
        lo = lane < SSD_CHUNK // 2
        lane1 = lane[0:1, :]
        ecs = jnp.exp(cs)
        ecst = jnp.exp(cst)
        dec = jnp.exp(cs[T - 1:T, :] - cs)
        etot = jnp.exp(cs[T - 1:T, :])
        bg = b_ref[...]
        cg = c_ref[...]
        bg_b = bg.astype(BF16)
        cg_b = cg.astype(BF16)
        cgt = cg.T
        cb = _dot(cg_b, bg_b, _NT)
        cbt = _dot(bg_b, cg_b, _NT)
        dbg = jnp.zeros((T, SSD_STATE), F32)
        dcg = jnp.zeros((T, SSD_STATE), F32)
        dcs_acc = jnp.zeros((T, LANE), F32)
        ddt_acc = jnp.zeros((T, LANE), F32)
        dsk_acc = jnp.zeros((1, LANE), F32)
        last_row = lax.broadcasted_iota(jnp.int32, (T, 1), 0) == T - 1
        for j in range(PAIRS_PER_GROUP):
            xp = x_ref[:, j * LANE:(j + 1) * LANE]
            dtl = _lane_pick(lo, dt, 2 * j)
            xdt = xp * dtl
            hp = hs_ref[0, 0, j]
            dhn = dh_ref[j]
            dyp = dy_ref[:, j * LANE:(j + 1) * LANE]
            dxdt = jnp.zeros((T, LANE), F32)
            et = jnp.where(lo[0:1, :], etot[:, 2 * j:2 * j + 1], etot[:, 2 * j + 1:2 * j + 2])
            dh_new = dhn * et
            for half in range(2):
                k = 2 * j + half
                msk = lo if half == 0 else jnp.logical_not(lo)
                xh_f = jnp.where(msk, xdt, 0.0)
                dyh_f = jnp.where(msk, dyp, 0.0)
                hh_f = jnp.where(msk, hp, 0.0)
                dhh_f = jnp.where(msk, dhn, 0.0)
                xh, dyh, hh, dhh = (v.astype(BF16) for v in (xh_f, dyh_f, hh_f, dhh_f))
                cs_col = cs[:, k:k + 1]
                cs_row = cst[k:k + 1, :]
                lm = jnp.exp(jnp.where(tri, cs_col - cs_row, NEG))
                lmt = jnp.exp(jnp.where(trit, cs_row - cs_col, NEG))
                dm = _dot(dyh, xh, _NT)
                dmt = _dot(xh, dyh, _NT)
                mm = lm * cb
                mmt = lmt * cbt
                bdh = _dot((bg * dec[:, k:k + 1]).astype(BF16), dhh)
                dxdt = dxdt + _dot(mmt.astype(BF16), dyh) + bdh
                dcg = dcg + _dot((dm * lm).astype(BF16), bg_b) + _dot(dyh, hh, _NT) * ecs[:, k:k + 1]
                dbg = dbg + _dot((dmt * lmt).astype(BF16), cg_b) + _dot(xh, dhh, _NT) * dec[:, k:k + 1]
                dh_new = dh_new + _dot((cgt * ecst[k:k + 1, :]).astype(BF16), dyh)
                yo = _dot((cg * ecs[:, k:k + 1]).astype(BF16), hh)
                e1 = jnp.sum(dm * mm, axis=-1, keepdims=True)
                e2 = jnp.sum(dmt * mmt, axis=-1, keepdims=True)
                e3 = jnp.sum(dyh_f * yo, axis=-1, keepdims=True)
                e4 = jnp.sum(xh_f * bdh, axis=-1, keepdims=True)
                tsum = jnp.sum(e4) + etot[:, k:k + 1] * jnp.sum(hh_f * dhh_f)
                dcs_h = e1 - e2 + e3 - e4 + jnp.where(last_row, tsum, 0.0)
                dcs_acc = jnp.where(lane == k, dcs_h, dcs_acc)
                dsk_acc = jnp.where(lane1 == k, jnp.sum(dyh_f * xp), dsk_acc)
            ddt_lo = jnp.sum(jnp.where(lo, dxdt * xp, 0.0), axis=-1, keepdims=True)
            ddt_hi = jnp.sum(jnp.where(lo, 0.0, dxdt * xp), axis=-1, keepdims=True)
            ddt_acc = jnp.where(lane == 2 * j, ddt_lo, jnp.where(lane == 2 * j + 1, ddt_hi, ddt_acc))
            dsk = jnp.where(lo[0:1, :], par_ref[2:3, 2 * j:2 * j + 1], par_ref[2:3, 2 * j + 1:2 * j + 2])
            dx_ref[:, j * LANE:(j + 1) * LANE] = dxdt * dtl + dsk * dyp
            dh_ref[j] = dh_new
        db_ref[...] = dbg
        dc_ref[...] = dcg
        triu = (ci >= ri).astype(F32)
        dda = _dot_hi(triu, dcs_acc)
        ddt = ddt_acc + dda * a
        ddtr = ddt * _sig(dtr_ref[...] + par_ref[0:1, :])
        ddtr_ref[...] = ddtr.astype(BF16)
        acc_ref[0:1, :] += jnp.sum(ddtr, axis=0, keepdims=True)
        acc_ref[1:2, :] += jnp.sum(dda * dt, axis=0, keepdims=True) * a
        acc_ref[2:3, :] += dsk_acc

    def rev(g, c):
        return (nc - 1 - c, g)

    return pl.pallas_call(
        body, name=name, grid=(SSD_GROUPS, nc),
        in_specs=[pl.BlockSpec((T, gw), rev),
                  pl.BlockSpec((T, SSD_STATE), rev),
                  pl.BlockSpec((T, SSD_STATE), rev),
                  pl.BlockSpec((T, LANE), rev),
                  pl.BlockSpec((8, LANE), lambda g, c: (0, g)),
                  pl.BlockSpec((1, 1, PAIRS_PER_GROUP, SSD_STATE, LANE), lambda g, c: (g, nc - 1 - c, 0, 0, 0)),
                  pl.BlockSpec((T, gw), rev)],
        out_specs=(pl.BlockSpec((T, gw), rev),
                   pl.BlockSpec((T, SSD_STATE), rev),
                   pl.BlockSpec((T, SSD_STATE), rev),
                   pl.BlockSpec((T, LANE), rev),
                   pl.BlockSpec((8, LANE), lambda g, c: (0, g))),
        out_shape=(jax.ShapeDtypeStruct((n_rows, SSD_D_INNER), F32),
                   jax.ShapeDtypeStruct((n_rows, BC_DIM), F32),
                   jax.ShapeDtypeStruct((n_rows, BC_DIM), F32),
                   jax.ShapeDtypeStruct((n_rows, DT_PAD), BF16),
                   jax.ShapeDtypeStruct((8, DT_PAD), F32)),
        scratch_shapes=[pltpu.VMEM((PAIRS_PER_GROUP, SSD_STATE, LANE), F32)],
        compiler_params=_cparams("parallel", "arbitrary"),
    )(xs, bm, cm, dtr, par, hs, dy)


ADAM_ROWS = 256


def _adamw(lands, w, m, v, name):
    na = len(lands)
    n_slots, r, wd = lands[0].shape
    tr = r if r <= 2 * ADAM_ROWS else ADAM_ROWS
    nj = r // tr
    bc1 = 1.0 - ADAM_B1 ** ADAM_STEP
    bc2 = 1.0 - ADAM_B2 ** ADAM_STEP

    def body(*refs):
        l_refs = refs[:na]
        w_ref, m_ref, v_ref, g_ref, d_ref, nm_ref, nv_ref = refs[na:]
        for a in range(na):
            @pl.when(pl.program_id(0) == a)
            def _(l_ref=l_refs[a]):
                g = l_ref[0].astype(F32)
                for s in range(1, n_slots):
                    g = g + l_ref[s].astype(F32)
                mn = ADAM_B1 * m_ref[0] + (1.0 - ADAM_B1) * g
                vn = ADAM_B2 * v_ref[0] + (1.0 - ADAM_B2) * (g * g)
                mh = mn / bc1
                vh = vn / bc2
                g_ref[0] = g
                nm_ref[0] = mn
                nv_ref[0] = vn
                d_ref[0] = -ADAM_LR * (mh / (jnp.sqrt(vh) + ADAM_EPS) + ADAM_WD * w_ref[0])

    def land_spec(a):
        return pl.BlockSpec((n_slots, tr, wd),
                            lambda i, j: (0, jnp.where(i == a, j, jnp.where(i < a, 0, nj - 1)), 0))

    blk = pl.BlockSpec((1, tr, wd), lambda i, j: (i, j, 0))
    shp = jax.ShapeDtypeStruct((na, r, wd), F32)
    return pl.pallas_call(
        body, name=name, grid=(na, nj), in_specs=[land_spec(a) for a in range(na)] + [blk, blk, blk],
        out_specs=(blk, blk, blk, blk), out_shape=(shp, shp, shp, shp),
        compiler_params=_cparams("arbitrary", "arbitrary"),
    )(*lands, w, m, v)


def _mesh_pos():
    return lax.axis_index("x"), lax.axis_index("y"), lax.axis_index("c")


def _peer(pos, k):
    x, y, c = pos
    px = 1 - x if (k >> 2) & 1 else x
    py = 1 - y if (k >> 1) & 1 else y
    pc = 1 - c if k & 1 else c
    return px, py, pc


def _flat(pos):
    return 4 * pos[0] + 2 * pos[1] + pos[2]


HBM_SPEC = pl.BlockSpec(memory_space=pl.ANY)


ROW_SHARDED = ("w_ssd_out", "w_att_out", "w_mix_out", "w_ffn_down")
COL_SHARDED = ("w_in", "w_ffn_gate", "w_ffn_up")
GATHERED = ROW_SHARDED + COL_SHARDED + ("conv_w",)
BIG = ROW_SHARDED + COL_SHARDED


SEM_SPEC = pl.BlockSpec(memory_space=pltpu.SEMAPHORE)
TOKEN = jax.ShapeDtypeStruct((8, LANE), F32)
SPLIT_EFFECT = pltpu.SideEffectType.DATAFLOW_SIDE_EFFECTING
GATHER_ROWS = "gather_rows"
GATHER_SLOT = "gather_slot"
SCATTER_ROWS = "scatter_rows"
SCATTER_SLOT = "scatter_slot"


def _land_shape(kind, src):
    if kind == GATHER_ROWS:
        return (N_DEV * src.shape[0],) + src.shape[1:]
    if kind == GATHER_SLOT:
        return (N_DEV,) + src.shape
    if kind == SCATTER_ROWS:
        return (N_DEV, src.shape[0] // N_DEV) + src.shape[1:]
    return src.shape


def _views(kind, src_ref, land_ref, pos, k):
    me = _flat(pos)
    if kind == GATHER_ROWS:
        r = src_ref.shape[0]
        return src_ref, land_ref.at[pl.ds(pl.multiple_of(me * r, 16), r), :]
    if kind == GATHER_SLOT:
        return src_ref, land_ref.at[me]
    dev = _flat(_peer(pos, k))
    if kind == SCATTER_ROWS:
        r = land_ref.shape[1]
        return src_ref.at[pl.ds(pl.multiple_of(dev * r, 16), r), :], land_ref.at[k]
    return src_ref.at[dev], land_ref.at[k]


def _hbm(x):
    return pltpu.with_memory_space_constraint(x, pltpu.HBM)


def _exchange_start(items, after, name):
    kinds = [k for k, _ in items]
    srcs = [_hbm(s) for _, s in items]
    lands = [_hbm(lax.empty(_land_shape(k, s), s.dtype)) for k, s in items]
    n = len(items)
    n_copy = n * (N_DEV - 1)

    def body(*refs):
        src_refs, land_refs = refs[:n], refs[n:2 * n]
        send_sems, recv_sems = refs[2 * n + 1], refs[2 * n + 2]
        token_ref = refs[4 * n + 3]
        pos = _mesh_pos()
        for i, kind in enumerate(kinds):
            for k in range(1, N_DEV):
                s, d = _views(kind, src_refs[i], land_refs[i], pos, k)
                j = i * (N_DEV - 1) + k - 1
                pltpu.make_async_remote_copy(src_ref=s, dst_ref=d, send_sem=send_sems.at[j], recv_sem=recv_sems.at[j],
                                             device_id=_peer(pos, k), device_id_type=MESH_ID).start()
        token_ref[...] = jnp.zeros_like(token_ref)

    arrs = srcs + lands
    outs = pl.pallas_call(
        body, name=name,
        in_specs=[HBM_SPEC] * (2 * n + 1),
        out_specs=[SEM_SPEC, SEM_SPEC] + [HBM_SPEC] * (2 * n) + [pl.BlockSpec(memory_space=pltpu.VMEM)],
        out_shape=[pltpu.SemaphoreType.DMA((n_copy,)), pltpu.SemaphoreType.DMA((n_copy,))]
        + [pltpu.HBM(a.shape, a.dtype) for a in arrs] + [TOKEN],
        input_output_aliases={i: 2 + i for i in range(2 * n)},
        compiler_params=pltpu.CompilerParams(has_side_effects=SPLIT_EFFECT),
    )(*arrs, after)
    return {"kinds": kinds, "send": outs[0], "recv": outs[1], "arrs": outs[2:2 + 2 * n], "token": outs[-1]}


def _exchange_wait(ex, after, name):
    kinds = ex["kinds"]
    n = len(kinds)

    def body(*refs):
        src_refs, land_refs = refs[:n], refs[n:2 * n]
        send_sems, recv_sems = refs[2 * n], refs[2 * n + 1]
        token_ref = refs[-1]
        pos = _mesh_pos()
        for i, kind in enumerate(kinds):
            for k in range(1, N_DEV):
                s, d = _views(kind, src_refs[i], land_refs[i], pos, k)
                j = i * (N_DEV - 1) + k - 1
                cp = pltpu.make_async_remote_copy(src_ref=s, dst_ref=d, send_sem=send_sems.at[j],
                                                  recv_sem=recv_sems.at[j], device_id=_peer(pos, k),
                                                  device_id_type=MESH_ID)
                cp.wait_send()
                cp.wait_recv()
        token_ref[...] = jnp.zeros_like(token_ref)

    outs = pl.pallas_call(
        body, name=name,
        in_specs=[HBM_SPEC] * (2 * n) + [SEM_SPEC, SEM_SPEC, HBM_SPEC],
        out_specs=[HBM_SPEC] * (2 * n) + [pl.BlockSpec(memory_space=pltpu.VMEM)],
        out_shape=[pltpu.HBM(a.shape, a.dtype) for a in ex["arrs"]] + [TOKEN],
        input_output_aliases={i: i for i in range(2 * n)},
        compiler_params=pltpu.CompilerParams(has_side_effects=SPLIT_EFFECT),
    )(*ex["arrs"], ex["send"], ex["recv"], after)
    lands = [_place_own(k, s, d) for k, s, d in zip(kinds, outs[:n], outs[n:2 * n])]
    return lands, outs[-1]


def _place_own(kind, src, land):
    me = _flat(_mesh_pos())
    zeros = (0,) * (src.ndim - 1)
    if kind == GATHER_ROWS:
        return lax.dynamic_update_slice(land, src, (me * src.shape[0],) + zeros)
    if kind == GATHER_SLOT:
        return lax.dynamic_update_slice(land, src[None], (me,) + (0,) * src.ndim)
    if kind == SCATTER_ROWS:
        r = land.shape[1]
        own = lax.dynamic_slice(src, (me * r,) + zeros, (r,) + src.shape[1:])
    else:
        own = lax.dynamic_index_in_dim(src, me, 0, keepdims=False)
    return lax.dynamic_update_slice(land, own[None], (0,) * land.ndim)


def _all_gather_small(x, name):
    r, w = x.shape

    def body(x_ref, out_ref, send_sems, recv_sems):
        pos = _mesh_pos()
        me = _flat(pos)
        copies = []
        for k in range(1, N_DEV):
            cp = pltpu.make_async_remote_copy(
                src_ref=x_ref, dst_ref=out_ref.at[me], send_sem=send_sems.at[k - 1], recv_sem=recv_sems.at[k - 1],
                device_id=_peer(pos, k), device_id_type=MESH_ID)
            cp.start()
            copies.append(cp)
        out_ref[me] = x_ref[...]
        for cp in copies:
            cp.wait()

    vmem = pl.BlockSpec(memory_space=pltpu.VMEM)
    return pl.pallas_call(
        body, name=name, in_specs=[vmem], out_specs=vmem,
        out_shape=jax.ShapeDtypeStruct((N_DEV, r, w), x.dtype),
        scratch_shapes=[pltpu.SemaphoreType.DMA((N_DEV - 1,)), pltpu.SemaphoreType.DMA((N_DEV - 1,))],
        compiler_params=pltpu.CompilerParams(has_side_effects=True),
    )(x)


def _cols(g, lo, hi):
    c = g.shape[-1]
    parts = []
    for d in range(N_DEV):
        a, b = max(lo, d * c), min(hi, (d + 1) * c)
        if a < b:
            parts.append(g[d, :, a - d * c:b - d * c])
    return parts[0] if len(parts) == 1 else jnp.concatenate(parts, axis=1)


def _col_chunks(g):
    c = g.shape[-1] // N_DEV
    return jnp.stack([g[:, d * c:(d + 1) * c] for d in range(N_DEV)])


IN_PART = ("w_in", "conv_w")
OUT_PART = ROW_SHARDED + ("w_ffn_gate", "w_ffn_up")


def _gather_items(w, names, l):
    items = []
    for n in names:
        blk = w[n][l] if n == "conv_w" else w[n][l].astype(BF16)
        items.append((GATHER_ROWS if n in ROW_SHARDED else GATHER_SLOT, blk))
    return items


def _scatter_items(grads, names):
    return [(SCATTER_ROWS, grads[n]) if n in ROW_SHARDED else (SCATTER_SLOT, _col_chunks(grads[n]))
            for n in names]


SMALL = ("ln_in_g", "ln_in_b", "conv_b", "dt_bias", "a_log", "d_skip", "ssd_norm_w", "att_sinks",
         "ln_mix_g", "ln_mix_b", "ln_ffn_g", "ln_ffn_b")


def _pack_small(vals):
    flat = jnp.concatenate([vals[n].reshape(-1) for n in SMALL])
    n = flat.shape[0]
    rows = -(-n // LANE)
    rows = -(-rows // 8) * 8
    return jnp.pad(flat, (0, rows * LANE - n)).reshape(rows, LANE)


def _unpack_small(buf, shapes):
    flat = buf.reshape(-1)
    off = 0
    out = {}
    for n in SMALL:
        cnt = math.prod(shapes[n])
        out[n] = flat[off:off + cnt].reshape(shapes[n])
        off += cnt
    return out


def _to_group_major(v):
    lead = v.shape[:-1]
    t = v.reshape(lead + (SSD_GROUPS, HEADS_PER_GROUP))
    t = jnp.pad(t, [(0, 0)] * len(lead) + [(0, 0), (0, LANE - HEADS_PER_GROUP)])
    return t.reshape(lead + (DT_PAD,))


def _from_group_major(v):
    lead = v.shape[:-1]
    return v.reshape(lead + (SSD_GROUPS, LANE))[..., :HEADS_PER_GROUP].reshape(lead + (SSD_HEADS,))


def _rows8(v):
    return jnp.pad(v, ((0, 8 - v.shape[0]), (0, 0)))


IN_OFFS = {"q": (0, 1024), "kv": (1024, 1280), "z": (1280, 3328), "xs": (3328, 5376), "b": (5376, 5888),
           "c": (5888, 6400), "dt": (6400, 6432), "gl": (6432, 8480)}
PIECES = ("q", "kv", "z", "xs", "b", "c", "dt", "gl")


def _split_w_in(g):
    out = {p: _cols(g, lo, hi) for p, (lo, hi) in IN_OFFS.items()}
    out["dt"] = _to_group_major(out["dt"])
    return out


def _join_dw_in(dws):
    dws = dict(dws)
    dws["dt"] = _from_group_major(dws["dt"])
    return jnp.concatenate([dws[p] for p in PIECES], axis=1)


def _params_out(W):
    p = {n: W[n] for n in ROW_SHARDED}
    for n in ("w_ffn_gate", "w_ffn_up"):
        p[n] = _cols(W[n], 0, FFN_HIDDEN)
    return p


def _params_in(l, W, sm):
    p = {"w_in": _split_w_in(W["w_in"])}
    cw = _cols(W["conv_w"], 0, SSD_D_INNER + 2 * BC_DIM)
    cb = sm["conv_b"][l]
    segs = {"xs": (0, 2048), "b": (2048, 2560), "c": (2560, 3072)}
    p["conv_w8"] = {s: _rows8(cw[:, lo:hi]) for s, (lo, hi) in segs.items()}
    p["conv_b8"] = {s: _rows8(cb[None, lo:hi]) for s, (lo, hi) in segs.items()}
    p["ssd_par"] = _rows8(jnp.stack([_to_group_major(sm["dt_bias"][l]), _to_group_major(sm["a_log"][l]),
                                     _to_group_major(sm["d_skip"][l])]))
    p["norm_w"] = sm["ssd_norm_w"][l]
    p["sinks8"] = _rows8(jnp.pad(sm["att_sinks"][l], (0, LANE - ATT_HEADS))[None])
    for n in ("ln_mix_g", "ln_mix_b", "ln_ffn_g", "ln_ffn_b"):
        p[n] = sm[n][l]
    return p


def _fwd_mixers(h0, p, l, dep=None):
    tag = f"l{l}_"
    a = {"h0": h0}
    for pc in PIECES:
        a[pc] = _mm(h0, p["w_in"][pc], "nn", tag + "proj_" + pc, dep=dep)
    for s in ("xs", "b", "c"):
        a[s + "c"] = _conv_fwd(a[s], p["conv_w8"][s], p["conv_b8"][s], tag + "conv_" + s)
    a["y"], a["hs"] = _ssd_fwd(a["xsc"], a["bc"], a["cc"], a["dt"], p["ssd_par"], tag + "ssd_fwd")
    a["yn"] = _gnorm_fwd(a["y"], a["z"], p["norm_w"], tag + "gnorm")
    a["att"] = _att_fwd(a["q"], a["kv"], p["sinks8"], tag + "att_fwd")
    return a


def _fwd_out(a, p, l, dep=None):
    tag = f"l{l}_"
    h0 = a["h0"]
    a["ya"] = _mm(a["yn"], p["w_ssd_out"], "nn", tag + "ssd_out", dep=dep)
    a["yb"] = _mm(a["att"], p["w_att_out"], "nn", tag + "att_out", dep=dep)
    a["merged"] = _merge_fwd(a["gl"], a["ya"], a["yb"], tag + "merge")
    a["mix"] = _mm(a["merged"], p["w_mix_out"], "nn", tag + "mix_out")
    a["h1"] = _ln_fwd(h0, a["mix"], p["ln_mix_g"], p["ln_mix_b"], ALPHA, tag + "ln_mix")
    a["fg"] = _mm(a["h1"], p["w_ffn_gate"], "nn", tag + "ffn_gate")
    a["fu"] = _mm(a["h1"], p["w_ffn_up"], "nn", tag + "ffn_up")
    a["act"] = _swiglu_fwd(a["fg"], a["fu"], tag + "swiglu")
    a["ffn"] = _mm(a["act"], p["w_ffn_down"], "nn", tag + "ffn_down")
    a["h2"] = _ln_fwd(a["h1"], a["ffn"], p["ln_ffn_g"], p["ln_ffn_b"], ALPHA, tag + "ln_ffn")
    return a


def _dw(x, dy, name, dep=None):
    return _mm(x, dy, "tn", name, out_dtype=BF16, dep=dep)


def _bwd_out(a, p, dh2, l, dep=None):
    tag = f"l{l}_b_"
    gw, gs = {}, {}
    du2, acc = _ln_bwd(a["h1"], a["ffn"], p["ln_ffn_g"], dh2, ALPHA, tag + "ln_ffn")
    gs["ln_ffn_g"], gs["ln_ffn_b"] = acc[0], acc[1]
    gw["w_ffn_down"] = _dw(a["act"], du2, tag + "dw_down", dep=dep)
    dact = _mm(du2, p["w_ffn_down"], "nt", tag + "dact", dep=dep)
    dfg, dfu = _swiglu_bwd(a["fg"], a["fu"], dact, tag + "swiglu")
    gw["w_ffn_gate"] = _dw(a["h1"], dfg, tag + "dw_gate")
    gw["w_ffn_up"] = _dw(a["h1"], dfu, tag + "dw_up")
    dh1 = _mm(dfg, p["w_ffn_gate"], "nt", tag + "dh1_gate", add=du2, add_scale=ALPHA)
    dh1 = _mm(dfu, p["w_ffn_up"], "nt", tag + "dh1_up", add=dh1)
    du1, acc = _ln_bwd(a["h0"], a["mix"], p["ln_mix_g"], dh1, ALPHA, tag + "ln_mix")
    gs["ln_mix_g"], gs["ln_mix_b"] = acc[0], acc[1]
    gw["w_mix_out"] = _dw(a["merged"], du1, tag + "dw_mix")
    dmerged = _mm(du1, p["w_mix_out"], "nt", tag + "dmerged")
    dya, dyb, dgl = _merge_bwd(a["gl"], a["ya"], a["yb"], dmerged, tag + "merge")
    gw["w_ssd_out"] = _dw(a["yn"], dya, tag + "dw_ssd")
    gw["w_att_out"] = _dw(a["att"], dyb, tag + "dw_att")
    return {"du1": du1, "dya": dya, "dyb": dyb, "dgl": dgl}, gw, gs


def _bwd_mixers(a, p, carry, l, dep=None):
    tag = f"l{l}_b_"
    gs = {}
    du1, dgl = carry["du1"], carry["dgl"]
    dyn = _mm(carry["dya"], p["w_ssd_out"], "nt", tag + "dyn", dep=dep)
    datt = _mm(carry["dyb"], p["w_att_out"], "nt", tag + "datt", out_dtype=BF16, dep=dep)
    dq, dkv, acc = _att_bwd(a["q"], a["kv"], p["sinks8"], datt, tag + "att")
    gs["att_sinks"] = acc[0, :ATT_HEADS]
    dy, dz, acc = _gnorm_bwd(a["y"], a["z"], p["norm_w"], dyn, tag + "gnorm")
    gs["ssd_norm_w"] = acc[0]
    dxs, dbm, dcm, ddt, acc = _ssd_bwd(a["xsc"], a["bc"], a["cc"], a["dt"], p["ssd_par"], a["hs"], dy,
                                       tag + "ssd")
    gs["dt_bias"], gs["a_log"], gs["d_skip"] = (_from_group_major(acc[i]) for i in range(3))
    dpieces = {"q": dq, "kv": dkv, "z": dz, "dt": ddt, "gl": dgl}
    dconv_w, dconv_b = [], []
    for s, dout in (("xs", dxs), ("b", dbm), ("c", dcm)):
        dc, acc = _conv_bwd_pre(a[s], p["conv_w8"][s], p["conv_b8"][s], dout, tag + "conv_pre_" + s)
        dconv_w.append(acc[:CONV_TAPS])
        dconv_b.append(acc[CONV_TAPS])
        dpieces[s] = _conv_bwd_in(dc, p["conv_w8"][s], tag + "conv_in_" + s)
    gconv = jnp.concatenate(dconv_w, axis=1)
    gs["conv_b"] = jnp.concatenate(dconv_b)
    dws = {}
    dh0 = du1
    scale = ALPHA
    for pc in PIECES:
        dws[pc] = _dw(a["h0"], dpieces[pc], tag + "dw_in_" + pc)
        dh0 = _mm(dpieces[pc], p["w_in"][pc], "nt", tag + "dh0_" + pc, add=dh0, add_scale=scale)
        scale = 1.0
    return dh0, _join_dw_in(dws), gconv, gs


def _step(x, target, w, m, v):
    x2 = x[0]
    t2 = target[0]
    tok = jnp.zeros(TOKEN.shape, TOKEN.dtype)

    ex = _exchange_start(_gather_items(w, IN_PART, 0), tok, "gather_l0_in_start")
    lands, tok = _exchange_wait(ex, ex["token"], "gather_l0_in_wait")
    p0 = _params_in(0, dict(zip(IN_PART, lands)), w)
    ex = _exchange_start(_gather_items(w, OUT_PART, 0) + _gather_items(w, IN_PART, 1), tok,
                         "gather_l0_out_l1_in_start")
    h = _ln_fwd(x2, None, w["ln_in_g"], w["ln_in_b"], 1.0, "ln_in")
    a0 = _fwd_mixers(h, p0, 0, dep=ex["token"])
    lands, tok = _exchange_wait(ex, a0["att"], "gather_l0_out_l1_in_wait")
    p0.update(_params_out(dict(zip(OUT_PART, lands))))
    p1 = _params_in(1, dict(zip(IN_PART, lands[len(OUT_PART):])), w)
    ex = _exchange_start(_gather_items(w, OUT_PART, 1), tok, "gather_l1_out_start")
    a0 = _fwd_out(a0, p0, 0, dep=ex["token"])
    lands, tok = _exchange_wait(ex, a0["h2"], "gather_l1_out_wait")
    p1.update(_params_out(dict(zip(OUT_PART, lands))))
    a1 = _fwd_out(_fwd_mixers(a0["h2"], p1, 1), p1, 1)

    sse, dh = _loss_fwd_bwd(a1["h2"], t2, "loss")
    loss = lax.psum(0.5 / D_MODEL * sse[0, 0], ("x", "y", "c"))

    carry, gw1, gs1 = _bwd_out(a1, p1, dh, 1)
    dh, gw1["w_in"], gw1["conv_w"], gs = _bwd_mixers(a1, p1, carry, 1)
    gs1.update(gs)
    ex1 = _exchange_start(_scatter_items(gw1, GATHERED), tok, "scatter_l1_start")
    carry, gw0, gs0 = _bwd_out(a0, p0, dh, 0, dep=ex1["token"])
    lands, tok = _exchange_wait(ex1, carry["dgl"], "scatter_l1_wait")
    land1 = dict(zip(GATHERED, lands))
    ex0 = _exchange_start(_scatter_items(gw0, OUT_PART), tok, "scatter_l0_out_start")
    dh, gw0["w_in"], gw0["conv_w"], gs = _bwd_mixers(a0, p0, carry, 0, dep=ex0["token"])
    gs0.update(gs)
    lands, tok = _exchange_wait(ex0, dh, "scatter_l0_out_wait")
    land0 = dict(zip(OUT_PART, lands))
    ex0 = _exchange_start(_scatter_items(gw0, IN_PART), tok, "scatter_l0_in_start")
    grad_x2, acc = _ln_bwd(x2, None, w["ln_in_g"], dh, 1.0, "ln_in_b")

    outs = [{} for _ in range(4)]

    def update(names):
        res = None
        for n in names:
            res = _adamw([land0[n], land1[n]], w[n], m[n], v[n], "adamw_" + n)
            for o, t in zip(outs, res):
                o[n] = t
        return res[1]

    update(OUT_PART)
    gsm = {"ln_in_g": acc[0], "ln_in_b": acc[1]}
    for n in SMALL[2:]:
        gsm[n] = jnp.stack([gs0[n], gs1[n]])
    small_shapes = {n: w[n].shape for n in SMALL}
    land_s = _all_gather_small(_pack_small(gsm), "small_grads_all_gather")
    res = _adamw([land_s], _pack_small(w)[None], _pack_small(m)[None], _pack_small(v)[None], "adamw_small")
    for o, t in zip(outs, res):
        o.update(_unpack_small(t[0], small_shapes))
    lands, _ = _exchange_wait(ex0, res[1], "scatter_l0_in_wait")
    land0.update(zip(IN_PART, lands))
    update(IN_PART)
    return loss, grad_x2[None], outs


WEIGHT_NAMES = ("ln_in_g", "ln_in_b", "w_in", "conv_w", "conv_b", "dt_bias", "a_log", "d_skip", "ssd_norm_w",
                "att_sinks", "w_ssd_out", "w_att_out", "w_mix_out", "ln_mix_g", "ln_mix_b", "w_ffn_gate",
                "w_ffn_up", "w_ffn_down", "ln_ffn_g", "ln_ffn_b")


def kernel(x, ln_in_g, ln_in_b, w_in, conv_w, conv_b, dt_bias, a_log, d_skip, ssd_norm_w, att_sinks, w_ssd_out, w_att_out, w_mix_out, ln_mix_g, ln_mix_b, w_ffn_gate, w_ffn_up, w_ffn_down, ln_ffn_g, ln_ffn_b, loss_target, m_ln_in_g, m_ln_in_b, m_w_in, m_conv_w, m_conv_b, m_dt_bias, m_a_log, m_d_skip, m_ssd_norm_w, m_att_sinks, m_w_ssd_out, m_w_att_out, m_w_mix_out, m_ln_mix_g, m_ln_mix_b, m_w_ffn_gate, m_w_ffn_up, m_w_ffn_down, m_ln_ffn_g, m_ln_ffn_b, v_ln_in_g, v_ln_in_b, v_w_in, v_conv_w, v_conv_b, v_dt_bias, v_a_log, v_d_skip, v_ssd_norm_w, v_att_sinks, v_w_ssd_out, v_w_att_out, v_w_mix_out, v_ln_mix_g, v_ln_mix_b, v_w_ffn_gate, v_w_ffn_up, v_w_ffn_down, v_ln_ffn_g, v_ln_ffn_b):
    w = dict(zip(WEIGHT_NAMES, (ln_in_g, ln_in_b, w_in, conv_w, conv_b, dt_bias, a_log, d_skip, ssd_norm_w,
                                att_sinks, w_ssd_out, w_att_out, w_mix_out, ln_mix_g, ln_mix_b, w_ffn_gate,
                                w_ffn_up, w_ffn_down, ln_ffn_g, ln_ffn_b)))
    m = dict(zip(WEIGHT_NAMES, (m_ln_in_g, m_ln_in_b, m_w_in, m_conv_w, m_conv_b, m_dt_bias, m_a_log, m_d_skip,
                                m_ssd_norm_w, m_att_sinks, m_w_ssd_out, m_w_att_out, m_w_mix_out, m_ln_mix_g,
                                m_ln_mix_b, m_w_ffn_gate, m_w_ffn_up, m_w_ffn_down, m_ln_ffn_g, m_ln_ffn_b)))
    v = dict(zip(WEIGHT_NAMES, (v_ln_in_g, v_ln_in_b, v_w_in, v_conv_w, v_conv_b, v_dt_bias, v_a_log, v_d_skip,
                                v_ssd_norm_w, v_att_sinks, v_w_ssd_out, v_w_att_out, v_w_mix_out, v_ln_mix_g,
                                v_ln_mix_b, v_w_ffn_gate, v_w_ffn_up, v_w_ffn_down, v_ln_ffn_g, v_ln_ffn_b)))
    loss, grad_x, outs = _step(x, loss_target, w, m, v)
    result = [loss, grad_x]
    for o in outs:
        result.extend(o[n] for n in WEIGHT_NAMES)
    return tuple(result)
```

```python
import functools
import math

import jax
import jax.numpy as jnp
from jax import lax
from jax.experimental import pallas as pl
from jax.experimental.pallas import tpu as pltpu

F32 = jnp.float32
BF16 = jnp.bfloat16

D_MODEL = 1024
DEPTH = 2
N_DEV = 8
ATT_HEADS = 16
ATT_KV_HEADS = 2
ATT_HEAD_DIM = 64
ATT_BLOCK = 128
SSD_D_INNER = 2048
SSD_HEADS = 32
SSD_GROUPS = 4
SSD_STATE = 128
SSD_CHUNK = 128
FFN_HIDDEN = 2816
LN_EPS = 1e-5
RMS_EPS = 1e-5
ALPHA = (2 * DEPTH) ** 0.25
Q_DIM = 1024
KV_DIM = 128
BC_DIM = 512
IN_DIM = 8480
IN_SHARD = IN_DIM // N_DEV
DT_PAD = 512

ADAM_LR = 0.001
ADAM_B1 = 0.9
ADAM_B2 = 0.999
ADAM_EPS = 1e-08
ADAM_WD = 0.01
ADAM_STEP = 10

LANE = 128
VMEM_LIMIT = 48 * 1024 * 1024
PACK_W = 1024
NEG = -1e30

_NN = (((1,), (0,)), ((), ()))
_NT = (((1,), (1,)), ((), ()))
_TN = (((0,), (0,)), ((), ()))
MESH_ID = pl.DeviceIdType.MESH


def _dot(a, b, dims=_NN):
    return lax.dot_general(a, b, dims, preferred_element_type=F32)


def _dot_hi(a, b):
    return lax.dot_general(a, b, _NN, preferred_element_type=F32, precision=lax.Precision.HIGHEST)


def _sig(x):
    return 1.0 / (1.0 + jnp.exp(-x))


def _softplus(x):
    return jnp.maximum(x, 0.0) + jnp.log(1.0 + jnp.exp(-jnp.abs(x)))


def _cparams(*sem):
    return pltpu.CompilerParams(dimension_semantics=sem, vmem_limit_bytes=VMEM_LIMIT)


def _pick(n, cap):
    if n <= cap:
        return n
    best = None
    for t in range(LANE, cap + 1, LANE):
        if n % t == 0:
            best = t
    assert best is not None, (n, cap)
    return best


def _tile(n):
    if n <= 1024 or n % 1024 == 0:
        return min(n, 1024)
    return _pick(n, 1408)


def _rows(n):
    return min(512, n)


def _mm(a, b, mode, name, add=None, add_scale=1.0, out_dtype=F32, dep=None):
    if mode == "nn":
        m, k = a.shape
        n = b.shape[1]
    elif mode == "nt":
        m, k = a.shape
        n = b.shape[0]
    else:
        k, m = a.shape
        n = b.shape[1]
    tm = _tile(m)
    tn = _tile(n)
    tk = _tile(k)
    nk = k // tk
    has_add = add is not None
    dims = {"nn": _NN, "nt": _NT, "tn": _TN}[mode]

    def body(*refs):
        if dep is not None:
            refs = refs[:-3] + refs[-2:]
        if has_add:
            a_ref, b_ref, add_ref, o_ref, acc_ref = refs
        else:
            a_ref, b_ref, o_ref, acc_ref = refs
        kk = pl.program_id(2)

        @pl.when(kk == 0)
        def _():
            if has_add:
                acc_ref[...] = add_scale * add_ref[...].astype(F32)
            else:
                acc_ref[...] = jnp.zeros_like(acc_ref)

        acc_ref[...] += _dot(a_ref[...].astype(BF16), b_ref[...].astype(BF16), dims)

        @pl.when(kk == nk - 1)
        def _():
            o_ref[...] = acc_ref[...].astype(o_ref.dtype)

    if mode == "nn":
        a_spec = pl.BlockSpec((tm, tk), lambda i, j, kk: (i, kk))
        b_spec = pl.BlockSpec((tk, tn), lambda i, j, kk: (kk, j))
    elif mode == "nt":
        a_spec = pl.BlockSpec((tm, tk), lambda i, j, kk: (i, kk))
        b_spec = pl.BlockSpec((tn, tk), lambda i, j, kk: (j, kk))
    else:
        a_spec = pl.BlockSpec((tk, tm), lambda i, j, kk: (kk, i))
        b_spec = pl.BlockSpec((tk, tn), lambda i, j, kk: (kk, j))
    o_spec = pl.BlockSpec((tm, tn), lambda i, j, kk: (i, j))
    in_specs = [a_spec, b_spec] + ([o_spec] if has_add else [])
    args = (a, b) + ((add,) if has_add else ())
    if dep is not None:
        in_specs.append(pl.BlockSpec((8, LANE), lambda i, j, kk: (0, 0)))
        args += (dep,)
    return pl.pallas_call(
        body, name=name, grid=(m // tm, n // tn, nk),
        in_specs=in_specs, out_specs=o_spec,
        out_shape=jax.ShapeDtypeStruct((m, n), out_dtype),
        scratch_shapes=[pltpu.VMEM((tm, tn), F32)],
        compiler_params=_cparams("parallel", "parallel", "arbitrary"),
    )(*args)


def _vec_spec(width):
    return pl.BlockSpec((1, width), lambda i: (0, 0))


def _ln_fwd(a, b, gamma, beta, alpha, name):
    n_rows, dm = a.shape
    has_b = b is not None

    def body(*refs):
        if has_b:
            a_ref, b_ref, g_ref, be_ref, o_ref = refs
            u = alpha * a_ref[...] + b_ref[...]
        else:
            a_ref, g_ref, be_ref, o_ref = refs
            u = a_ref[...]
        mu = jnp.mean(u, axis=-1, keepdims=True)
        d = u - mu
        var = jnp.mean(d * d, axis=-1, keepdims=True)
        o_ref[...] = d * lax.rsqrt(var + LN_EPS) * g_ref[...] + be_ref[...]

    row = pl.BlockSpec((_rows(n_rows),dm), lambda i: (i, 0))
    in_specs = [row] + ([row] if has_b else []) + [_vec_spec(dm), _vec_spec(dm)]
    args = (a,) + ((b,) if has_b else ()) + (gamma.reshape(1, dm), beta.reshape(1, dm))
    return pl.pallas_call(
        body, name=name, grid=(n_rows // _rows(n_rows),), in_specs=in_specs, out_specs=row,
        out_shape=jax.ShapeDtypeStruct((n_rows, dm), F32),
        compiler_params=_cparams("parallel"),
    )(*args)


def _ln_bwd(a, b, gamma, dy, alpha, name):
    n_rows, dm = a.shape
    has_b = b is not None

    def body(*refs):
        if has_b:
            a_ref, b_ref, g_ref, dy_ref, du_ref, acc_ref = refs
            u = alpha * a_ref[...] + b_ref[...]
        else:
            a_ref, g_ref, dy_ref, du_ref, acc_ref = refs
            u = a_ref[...]

        @pl.when(pl.program_id(0) == 0)
        def _():
            acc_ref[...] = jnp.zeros_like(acc_ref)

        mu = jnp.mean(u, axis=-1, keepdims=True)
        d = u - mu
        var = jnp.mean(d * d, axis=-1, keepdims=True)
        rstd = lax.rsqrt(var + LN_EPS)
        xhat = d * rstd
        dyv = dy_ref[...]
        acc_ref[0:1, :] += jnp.sum(dyv * xhat, axis=0, keepdims=True)
        acc_ref[1:2, :] += jnp.sum(dyv, axis=0, keepdims=True)
        dxh = dyv * g_ref[...]
        m1 = jnp.mean(dxh, axis=-1, keepdims=True)
        m2 = jnp.mean(dxh * xhat, axis=-1, keepdims=True)
        du_ref[...] = rstd * (dxh - m1 - xhat * m2)

    row = pl.BlockSpec((_rows(n_rows),dm), lambda i: (i, 0))
    in_specs = [row] + ([row] if has_b else []) + [_vec_spec(dm), row]
    args = (a,) + ((b,) if has_b else ()) + (gamma.reshape(1, dm), dy)
    return pl.pallas_call(
        body, name=name, grid=(n_rows // _rows(n_rows),), in_specs=in_specs,
        out_specs=(row, pl.BlockSpec((8, dm), lambda i: (0, 0))),
        out_shape=(jax.ShapeDtypeStruct((n_rows, dm), F32), jax.ShapeDtypeStruct((8, dm), F32)),
        compiler_params=_cparams("arbitrary"),
    )(*args)


def _loss_fwd_bwd(y, target, name):
    n_rows, dm = y.shape

    def body(y_ref, t_ref, acc_ref, dy_ref):
        @pl.when(pl.program_id(0) == 0)
        def _():
            acc_ref[...] = jnp.zeros_like(acc_ref)

        d = y_ref[...] - t_ref[...]
        acc_ref[...] += jnp.sum(d * d)
        dy_ref[...] = d * (1.0 / dm)

    row = pl.BlockSpec((_rows(n_rows),dm), lambda i: (i, 0))
    return pl.pallas_call(
        body, name=name, grid=(n_rows // _rows(n_rows),), in_specs=[row, row],
        out_specs=(pl.BlockSpec((8, LANE), lambda i: (0, 0)), row),
        out_shape=(jax.ShapeDtypeStruct((8, LANE), F32), jax.ShapeDtypeStruct((n_rows, dm), F32)),
        compiler_params=_cparams("arbitrary"),
    )(y, target)


def _swiglu_fwd(g, u, name):
    n_rows, w = g.shape
    tw = _pick(w, 1408)

    def body(g_ref, u_ref, o_ref):
        gv = g_ref[...]
        o_ref[...] = (gv * _sig(gv) * u_ref[...]).astype(BF16)

    blk = pl.BlockSpec((_rows(n_rows),tw), lambda i, j: (i, j))
    return pl.pallas_call(
        body, name=name, grid=(n_rows // _rows(n_rows), w // tw), in_specs=[blk, blk], out_specs=blk,
        out_shape=jax.ShapeDtypeStruct((n_rows, w), BF16),
        compiler_params=_cparams("parallel", "parallel"),
    )(g, u)


def _swiglu_bwd(g, u, dact, name):
    n_rows, w = g.shape
    tw = _pick(w, 1408)

    def body(g_ref, u_ref, da_ref, dg_ref, du_ref):
        gv = g_ref[...]
        s = _sig(gv)
        da = da_ref[...]
        dg_ref[...] = (da * u_ref[...] * (s * (1.0 + gv * (1.0 - s)))).astype(BF16)
        du_ref[...] = (da * gv * s).astype(BF16)

    blk = pl.BlockSpec((_rows(n_rows),tw), lambda i, j: (i, j))
    return pl.pallas_call(
        body, name=name, grid=(n_rows // _rows(n_rows), w // tw), in_specs=[blk, blk, blk], out_specs=(blk, blk),
        out_shape=(jax.ShapeDtypeStruct((n_rows, w), BF16), jax.ShapeDtypeStruct((n_rows, w), BF16)),
        compiler_params=_cparams("parallel", "parallel"),
    )(g, u, dact)


def _merge_fwd(gl, ya, yb, name):
    n_rows, dm = ya.shape

    def body(gl_ref, ya_ref, yb_ref, o_ref):
        ga = _sig(gl_ref[:, :dm])
        gb = _sig(gl_ref[:, dm:])
        o_ref[...] = (ga * ya_ref[...] + gb * yb_ref[...]).astype(BF16)

    row = pl.BlockSpec((_rows(n_rows),dm), lambda i: (i, 0))
    row2 = pl.BlockSpec((_rows(n_rows),2 * dm), lambda i: (i, 0))
    return pl.pallas_call(
        body, name=name, grid=(n_rows // _rows(n_rows),), in_specs=[row2, row, row], out_specs=row,
        out_shape=jax.ShapeDtypeStruct((n_rows, dm), BF16),
        compiler_params=_cparams("parallel"),
    )(gl, ya, yb)


def _merge_bwd(gl, ya, yb, dmerged, name):
    n_rows, dm = ya.shape

    def body(gl_ref, ya_ref, yb_ref, dm_ref, dya_ref, dyb_ref, dgl_ref):
        ga = _sig(gl_ref[:, :dm])
        gb = _sig(gl_ref[:, dm:])
        dmv = dm_ref[...]
        dya_ref[...] = (dmv * ga).astype(BF16)
        dyb_ref[...] = (dmv * gb).astype(BF16)
        dgl_ref[:, :dm] = (dmv * ya_ref[...] * ga * (1.0 - ga)).astype(BF16)
        dgl_ref[:, dm:] = (dmv * yb_ref[...] * gb * (1.0 - gb)).astype(BF16)

    row = pl.BlockSpec((_rows(n_rows),dm), lambda i: (i, 0))
    row2 = pl.BlockSpec((_rows(n_rows),2 * dm), lambda i: (i, 0))
    return pl.pallas_call(
        body, name=name, grid=(n_rows // _rows(n_rows),), in_specs=[row2, row, row, row], out_specs=(row, row, row2),
        out_shape=(jax.ShapeDtypeStruct((n_rows, dm), BF16), jax.ShapeDtypeStruct((n_rows, dm), BF16),
                   jax.ShapeDtypeStruct((n_rows, 2 * dm), BF16)),
        compiler_params=_cparams("parallel"),
    )(gl, ya, yb, dmerged)


CONV_TAPS = 4
CONV_COLS = 512
HALO = 8


def _shift_down(cur, prev8, s, row8):
    r = pltpu.roll(cur, s, axis=0)
    top = jnp.where(row8 < s, pltpu.roll(prev8, s, axis=0), r[0:HALO])
    return jnp.concatenate([top, r[HALO:]], axis=0)


def _shift_up(cur, next8, s, row8):
    n = cur.shape[0]
    r = pltpu.roll(cur, n - s, axis=0)
    bot = jnp.where(row8 >= HALO - s, pltpu.roll(next8, HALO - s, axis=0), r[n - HALO:])
    return jnp.concatenate([r[:n - HALO], bot], axis=0)


def _conv_pre(u_ref, prev_ref, w_ref, b_ref, li):
    cur = u_ref[...]
    prev8 = jnp.where(li == 0, 0.0, prev_ref[...])
    row8 = lax.broadcasted_iota(jnp.int32, prev8.shape, 0)
    shifted = [cur] + [_shift_down(cur, prev8, s, row8) for s in range(1, CONV_TAPS)]
    acc = b_ref[...] + shifted[0] * w_ref[CONV_TAPS - 1:CONV_TAPS, :]
    for s in range(1, CONV_TAPS):
        acc = acc + shifted[s] * w_ref[CONV_TAPS - 1 - s:CONV_TAPS - s, :]
    return acc, shifted


def _conv_specs(n_rows, tl):
    cur = pl.BlockSpec((tl, CONV_COLS), lambda cj, li: (li, cj))
    prev = pl.BlockSpec((HALO, CONV_COLS), lambda cj, li: (jnp.maximum(li * (tl // HALO) - 1, 0), cj))
    nxt = pl.BlockSpec((HALO, CONV_COLS),
                       lambda cj, li: (jnp.minimum((li + 1) * (tl // HALO), n_rows // HALO - 1), cj))
    par = pl.BlockSpec((8, CONV_COLS), lambda cj, li: (0, cj))
    return cur, prev, nxt, par


def _conv_fwd(u, w8, b8, name):
    n_rows, c = u.shape
    tl = _rows(n_rows)
    cur, prev, _, par = _conv_specs(n_rows, tl)

    def body(u_ref, prev_ref, w_ref, b_ref, o_ref):
        acc, _ = _conv_pre(u_ref, prev_ref, w_ref, b_ref[0:1, :], pl.program_id(1))
        o_ref[...] = acc * _sig(acc)

    return pl.pallas_call(
        body, name=name, grid=(c // CONV_COLS, n_rows // tl), in_specs=[cur, prev, par, par], out_specs=cur,
        out_shape=jax.ShapeDtypeStruct((n_rows, c), F32),
        compiler_params=_cparams("parallel", "parallel"),
    )(u, u, w8, b8)


def _conv_bwd_pre(u, w8, b8, dout, name):
    n_rows, c = u.shape
    tl = _rows(n_rows)
    cur, prev, _, par = _conv_specs(n_rows, tl)

    def body(u_ref, prev_ref, w_ref, b_ref, do_ref, dc_ref, acc_ref):
        @pl.when(pl.program_id(1) == 0)
        def _():
            acc_ref[...] = jnp.zeros_like(acc_ref)

        acc, shifted = _conv_pre(u_ref, prev_ref, w_ref, b_ref[0:1, :], pl.program_id(1))
        sg = _sig(acc)
        dc = do_ref[...] * (sg * (1.0 + acc * (1.0 - sg)))
        dc_ref[...] = dc
        for k in range(CONV_TAPS):
            acc_ref[k:k + 1, :] += jnp.sum(dc * shifted[CONV_TAPS - 1 - k], axis=0, keepdims=True)
        acc_ref[CONV_TAPS:CONV_TAPS + 1, :] += jnp.sum(dc, axis=0, keepdims=True)

    return pl.pallas_call(
        body, name=name, grid=(c // CONV_COLS, n_rows // tl), in_specs=[cur, prev, par, par, cur],
        out_specs=(cur, par),
        out_shape=(jax.ShapeDtypeStruct((n_rows, c), F32), jax.ShapeDtypeStruct((8, c), F32)),
        compiler_params=_cparams("parallel", "arbitrary"),
    )(u, u, w8, b8, dout)


def _conv_bwd_in(dc, w8, name):
    n_rows, c = dc.shape
    tl = _rows(n_rows)
    cur, _, nxt, par = _conv_specs(n_rows, tl)
    n_l = n_rows // tl

    def body(dc_ref, next_ref, w_ref, o_ref):
        cur_v = dc_ref[...]
        next8 = jnp.where(pl.program_id(1) == n_l - 1, 0.0, next_ref[...])
        row8 = lax.broadcasted_iota(jnp.int32, next8.shape, 0)
        acc = cur_v * w_ref[CONV_TAPS - 1:CONV_TAPS, :]
        for s in range(1, CONV_TAPS):
            acc = acc + _shift_up(cur_v, next8, s, row8) * w_ref[CONV_TAPS - 1 - s:CONV_TAPS - s, :]
        o_ref[...] = acc.astype(BF16)

    return pl.pallas_call(
        body, name=name, grid=(c // CONV_COLS, n_l), in_specs=[cur, nxt, par], out_specs=cur,
        out_shape=jax.ShapeDtypeStruct((n_rows, c), BF16),
        compiler_params=_cparams("parallel", "parallel"),
    )(dc, dc, w8)


NORM_GROUP = SSD_D_INNER // SSD_GROUPS


def _gnorm_fwd(y, z, w, name):
    n_rows, c = y.shape

    def body(y_ref, z_ref, w_ref, o_ref):
        zv = z_ref[...]
        yg = y_ref[...] * (zv * _sig(zv))
        r = lax.rsqrt(jnp.mean(yg * yg, axis=-1, keepdims=True) + RMS_EPS)
        o_ref[...] = (yg * r * w_ref[...]).astype(BF16)

    blk = pl.BlockSpec((_rows(n_rows),NORM_GROUP), lambda i, j: (i, j))
    wspec = pl.BlockSpec((1, NORM_GROUP), lambda i, j: (0, j))
    return pl.pallas_call(
        body, name=name, grid=(n_rows // _rows(n_rows), c // NORM_GROUP), in_specs=[blk, blk, wspec], out_specs=blk,
        out_shape=jax.ShapeDtypeStruct((n_rows, c), BF16),
        compiler_params=_cparams("parallel", "parallel"),
    )(y, z, w.reshape(1, c))


def _gnorm_bwd(y, z, w, dyn, name):
    n_rows, c = y.shape

    def body(y_ref, z_ref, w_ref, dn_ref, dy_ref, dz_ref, acc_ref):
        @pl.when(pl.program_id(1) == 0)
        def _():
            acc_ref[...] = jnp.zeros_like(acc_ref)

        zv = z_ref[...]
        yv = y_ref[...]
        sz = _sig(zv)
        silu = zv * sz
        yg = yv * silu
        r = lax.rsqrt(jnp.mean(yg * yg, axis=-1, keepdims=True) + RMS_EPS)
        nrm = yg * r
        dn = dn_ref[...]
        acc_ref[0:1, :] += jnp.sum(dn * nrm, axis=0, keepdims=True)
        dnw = dn * w_ref[...]
        dyg = r * (dnw - nrm * jnp.mean(dnw * nrm, axis=-1, keepdims=True))
        dy_ref[...] = dyg * silu
        dz_ref[...] = (dyg * yv * (sz * (1.0 + zv * (1.0 - sz)))).astype(BF16)

    blk = pl.BlockSpec((_rows(n_rows),NORM_GROUP), lambda j, i: (i, j))
    wspec = pl.BlockSpec((1, NORM_GROUP), lambda j, i: (0, j))
    aspec = pl.BlockSpec((8, NORM_GROUP), lambda j, i: (0, j))
    return pl.pallas_call(
        body, name=name, grid=(c // NORM_GROUP, n_rows // _rows(n_rows)), in_specs=[blk, blk, wspec, blk],
        out_specs=(blk, blk, aspec),
        out_shape=(jax.ShapeDtypeStruct((n_rows, c), F32), jax.ShapeDtypeStruct((n_rows, c), BF16),
                   jax.ShapeDtypeStruct((8, c), F32)),
        compiler_params=_cparams("parallel", "arbitrary"),
    )(y, z, w.reshape(1, c), dyn)


ATT_SCALE = ATT_HEAD_DIM ** -0.5
ATT_SLOPES = [2.0 ** (-8.0 * (h + 1) / ATT_HEADS) for h in range(ATT_HEADS)]
Q_PER_KV = ATT_HEADS // ATT_KV_HEADS


def _dup_half(t, g, lo):
    tr = pltpu.roll(t, ATT_HEAD_DIM, axis=1)
    return jnp.where(lo, t, tr) if g == 0 else jnp.where(lo, tr, t)


def _att_band(kv_ref, kvp_ref, n):
    cur = kv_ref[...]
    prev = jnp.where(n == 0, 0.0, kvp_ref[...])
    lo = lax.broadcasted_iota(jnp.int32, (ATT_BLOCK, LANE), 1) < ATT_HEAD_DIM
    bands = []
    for g in range(ATT_KV_HEADS):
        kb = jnp.concatenate([_dup_half(prev[:, :LANE], g, lo), _dup_half(cur[:, :LANE], g, lo)], axis=0)
        vb = jnp.concatenate([_dup_half(prev[:, LANE:], g, lo), _dup_half(cur[:, LANE:], g, lo)], axis=0)
        bands.append((kb.astype(BF16), vb.astype(BF16)))
    return bands


def _att_tile(n):
    shape = (2 * ATT_BLOCK, ATT_BLOCK)
    row = lax.broadcasted_iota(jnp.int32, shape, 0)
    i = row & (ATT_BLOCK - 1)
    s = lax.broadcasted_iota(jnp.int32, shape, 1)
    upper = s > i
    dist = ((i - s) & (ATT_BLOCK - 1)).astype(F32)
    dead = upper & (n == 0)
    return upper, dist, dead, row[:, 0:1] < ATT_BLOCK


def _stack_pair(t, lo):
    return jnp.concatenate([jnp.where(lo, t, 0.0), jnp.where(lo, 0.0, t)], axis=0).astype(BF16)


def _att_probs(qs, kb, s_ref, j, tile):
    upper, dist, dead, first = tile
    s2 = _dot(qs, kb, _NT)
    slope = jnp.where(first, ATT_SLOPES[2 * j], ATT_SLOPES[2 * j + 1])
    sink = jnp.where(first, s_ref[0:1, 2 * j:2 * j + 1], s_ref[0:1, 2 * j + 1:2 * j + 2])
    s = jnp.where(upper, s2[:, :ATT_BLOCK], s2[:, ATT_BLOCK:]) - slope * dist
    s = jnp.where(dead, NEG, s)
    m = jnp.maximum(jnp.max(s, axis=-1, keepdims=True), sink)
    p = jnp.exp(s - m)
    es = jnp.exp(sink - m)
    inv = 1.0 / (jnp.sum(p, axis=-1, keepdims=True) + es)
    return p * inv, es * inv


def _band_split(t, upper):
    return jnp.concatenate([jnp.where(upper, t, 0.0), jnp.where(upper, 0.0, t)], axis=1)


def _att_fwd(q, kv, sinks8, name):
    n_rows = q.shape[0]
    nb = n_rows // ATT_BLOCK

    def body(q_ref, kv_ref, kvp_ref, s_ref, o_ref):
        n = pl.program_id(0)
        bands = _att_band(kv_ref, kvp_ref, n)
        lo = lax.broadcasted_iota(jnp.int32, (ATT_BLOCK, LANE), 1) < ATT_HEAD_DIM
        tile = _att_tile(n)
        for j in range(ATT_HEADS // 2):
            kb, vb = bands[2 * j // Q_PER_KV]
            qs = _stack_pair(q_ref[:, j * LANE:(j + 1) * LANE] * ATT_SCALE, lo)
            p, _ = _att_probs(qs, kb, s_ref, j, tile)
            out = _dot(_band_split(p, tile[0]).astype(BF16), vb)
            o_ref[:, j * LANE:(j + 1) * LANE] = jnp.where(lo, out[:ATT_BLOCK], out[ATT_BLOCK:]).astype(BF16)

    return pl.pallas_call(
        body, name=name, grid=(nb,),
        in_specs=[pl.BlockSpec((ATT_BLOCK, Q_DIM), lambda n: (n, 0)),
                  pl.BlockSpec((ATT_BLOCK, 2 * LANE), lambda n: (n, 0)),
                  pl.BlockSpec((ATT_BLOCK, 2 * LANE), lambda n: (jnp.maximum(n - 1, 0), 0)),
                  pl.BlockSpec((8, LANE), lambda n: (0, 0))],
        out_specs=pl.BlockSpec((ATT_BLOCK, Q_DIM), lambda n: (n, 0)),
        out_shape=jax.ShapeDtypeStruct((n_rows, Q_DIM), BF16),
        compiler_params=_cparams("parallel"),
    )(q, kv, kv, sinks8)


def _att_bwd(q, kv, sinks8, dout, name):
    n_rows = q.shape[0]
    nb = n_rows // ATT_BLOCK

    def body(q_ref, kv_ref, kvp_ref, s_ref, do_ref, dq_ref, dkv_ref, acc_ref, carry_ref):
        n = pl.program_id(0)

        @pl.when(n == 0)
        def _():
            acc_ref[...] = jnp.zeros_like(acc_ref)
            carry_ref[...] = jnp.zeros_like(carry_ref)

        @pl.when(n == nb)
        def _():
            dkv_ref[...] = carry_ref[...].astype(BF16)

        @pl.when(n < nb)
        def _():
            bands = _att_band(kv_ref, kvp_ref, n)
            lo = lax.broadcasted_iota(jnp.int32, (ATT_BLOCK, LANE), 1) < ATT_HEAD_DIM
            lane1 = lax.broadcasted_iota(jnp.int32, (1, LANE), 1)
            tile = _att_tile(n)
            upper, first = tile[0], tile[3]
            dk_acc = [jnp.zeros((2 * ATT_BLOCK, LANE), F32) for _ in range(ATT_KV_HEADS)]
            dv_acc = [jnp.zeros((2 * ATT_BLOCK, LANE), F32) for _ in range(ATT_KV_HEADS)]
            dsink = jnp.zeros((1, LANE), F32)
            for j in range(ATT_HEADS // 2):
                g = 2 * j // Q_PER_KV
                kb, vb = bands[g]
                qs = _stack_pair(q_ref[:, j * LANE:(j + 1) * LANE] * ATT_SCALE, lo)
                dos = _stack_pair(do_ref[:, j * LANE:(j + 1) * LANE].astype(F32), lo)
                p, ps = _att_probs(qs, kb, s_ref, j, tile)
                dp2 = _dot(dos, vb, _NT)
                dp = jnp.where(upper, dp2[:, :ATT_BLOCK], dp2[:, ATT_BLOCK:])
                delta = jnp.sum(p * dp, axis=-1, keepdims=True)
                ds2 = _band_split(p * (dp - delta), upper)
                psd = ps * delta
                dsink = jnp.where(lane1 == 2 * j, -jnp.sum(jnp.where(first, psd, 0.0)), dsink)
                dsink = jnp.where(lane1 == 2 * j + 1, -jnp.sum(jnp.where(first, 0.0, psd)), dsink)
                dq = _dot(ds2.astype(BF16), kb) * ATT_SCALE
                dq_ref[:, j * LANE:(j + 1) * LANE] = jnp.where(lo, dq[:ATT_BLOCK], dq[ATT_BLOCK:]).astype(BF16)
                dk_acc[g] = dk_acc[g] + _dot(ds2.T.astype(BF16), qs)
                dv_acc[g] = dv_acc[g] + _dot(_band_split(p, upper).T.astype(BF16), dos)
            acc_ref[0:1, :] += dsink
            lo2 = lax.broadcasted_iota(jnp.int32, (2 * ATT_BLOCK, LANE), 1) < ATT_HEAD_DIM
            folded = []
            for acc in (dk_acc, dv_acc):
                t0 = acc[0] + pltpu.roll(acc[0], ATT_HEAD_DIM, axis=1)
                t1 = acc[1] + pltpu.roll(acc[1], ATT_HEAD_DIM, axis=1)
                folded.append(jnp.where(lo2, t0, t1))
            band = jnp.concatenate(folded, axis=1)
            dkv_ref[...] = (carry_ref[...] + band[:ATT_BLOCK]).astype(BF16)
            carry_ref[...] = band[ATT_BLOCK:]

    def qmap(n):
        return (jnp.minimum(n, nb - 1), 0)

    return pl.pallas_call(
        body, name=name, grid=(nb + 1,),
        in_specs=[pl.BlockSpec((ATT_BLOCK, Q_DIM), qmap),
                  pl.BlockSpec((ATT_BLOCK, 2 * LANE), qmap),
                  pl.BlockSpec((ATT_BLOCK, 2 * LANE), lambda n: (jnp.maximum(jnp.minimum(n, nb - 1) - 1, 0), 0)),
                  pl.BlockSpec((8, LANE), lambda n: (0, 0)),
                  pl.BlockSpec((ATT_BLOCK, Q_DIM), qmap)],
        out_specs=(pl.BlockSpec((ATT_BLOCK, Q_DIM), qmap),
                   pl.BlockSpec((ATT_BLOCK, 2 * LANE), lambda n: (jnp.maximum(n - 1, 0), 0)),
                   pl.BlockSpec((8, LANE), lambda n: (0, 0))),
        out_shape=(jax.ShapeDtypeStruct((n_rows, Q_DIM), BF16), jax.ShapeDtypeStruct((n_rows, 2 * LANE), BF16),
                   jax.ShapeDtypeStruct((8, LANE), F32)),
        scratch_shapes=[pltpu.VMEM((ATT_BLOCK, 2 * LANE), F32)],
        compiler_params=_cparams("arbitrary"),
    )(q, kv, kv, sinks8, dout)


HEADS_PER_GROUP = SSD_HEADS // SSD_GROUPS
PAIRS_PER_GROUP = HEADS_PER_GROUP // 2
T = SSD_CHUNK


def _ssd_scalars(dtr_ref, par_ref):
    dt = _softplus(dtr_ref[...] + par_ref[0:1, :])
    a = -jnp.exp(par_ref[1:2, :])
    ri = lax.broadcasted_iota(jnp.int32, (T, T), 0)
    ci = lax.broadcasted_iota(jnp.int32, (T, T), 1)
    tril = (ri >= ci).astype(F32)
    cs = _dot_hi(tril, dt * a)
    cst = cs.T
    return dt, a, cs, cst, ri, ci


def _ssd_stacked_masks():
    row = lax.broadcasted_iota(jnp.int32, (2 * T, T), 0)
    t = row & (T - 1)
    s = lax.broadcasted_iota(jnp.int32, (2 * T, T), 1)
    return t >= s, s >= t, row[:, 0:1] < T


def _col_s(arr, k0):
    return jnp.concatenate([arr[:, k0:k0 + 1], arr[:, k0 + 1:k0 + 2]], axis=0)


def _row_s(arr_t, k0, first):
    return jnp.where(first, arr_t[k0:k0 + 1, :], arr_t[k0 + 1:k0 + 2, :])


def _lane_pick(lo, arr, k0):
    return jnp.where(lo, arr[:, k0:k0 + 1], arr[:, k0 + 1:k0 + 2])


def _ssd_fwd(xs, bm, cm, dtr, par, name):
    n_rows = xs.shape[0]
    nc = n_rows // T
    gw = PAIRS_PER_GROUP * LANE

    def body(x_ref, b_ref, c_ref, dtr_ref, par_ref, y_ref, hs_ref, h_ref):
        @pl.when(pl.program_id(1) == 0)
        def _():
            h_ref[...] = jnp.zeros_like(h_ref)

        dt, a, cs, cst, _, _ = _ssd_scalars(dtr_ref, par_ref)
        tri_s, _, first = _ssd_stacked_masks()
        lo = lax.broadcasted_iota(jnp.int32, (T, LANE), 1) < SSD_CHUNK // 2
        ecs = jnp.exp(cs)
        dect = jnp.exp(cst[:, T - 1:T] - cst)
        etot = jnp.exp(cs[T - 1:T, :])
        bg = b_ref[...]
        cg = c_ref[...]
        cb = _dot(cg.astype(BF16), bg.astype(BF16), _NT)
        cb_s = jnp.concatenate([cb, cb], axis=0)
        cg_s = jnp.concatenate([cg, cg], axis=0)
        bgt_s = jnp.concatenate([bg.T, bg.T], axis=0)
        for j in range(PAIRS_PER_GROUP):
            k0, k1 = 2 * j, 2 * j + 1
            xp = x_ref[:, j * LANE:(j + 1) * LANE]
            hp = h_ref[j]
            hs_ref[0, 0, j] = hp
            rhs = jnp.concatenate([(xp * _lane_pick(lo, dt, k0)).astype(BF16), hp.astype(BF16)], axis=0)
            lm_s = jnp.exp(jnp.where(tri_s, _col_s(cs, k0) - _row_s(cst, k0, first), NEG))
            lhs = jnp.concatenate([lm_s * cb_s, cg_s * _col_s(ecs, k0)], axis=1).astype(BF16)
            y_s = _dot(lhs, rhs)
            s_s = _dot((bgt_s * _row_s(dect, k0, first)).astype(BF16), rhs[:T])
            dsk = jnp.where(lo[0:1, :], par_ref[2:3, k0:k0 + 1], par_ref[2:3, k1:k1 + 1])
            y_ref[:, j * LANE:(j + 1) * LANE] = jnp.where(lo, y_s[:T], y_s[T:]) + dsk * xp
            et = jnp.where(lo[0:1, :], etot[:, k0:k0 + 1], etot[:, k1:k1 + 1])
            h_ref[j] = hp * et + jnp.where(lo, s_s[:T], s_s[T:])

    return pl.pallas_call(
        body, name=name, grid=(SSD_GROUPS, nc),
        in_specs=[pl.BlockSpec((T, gw), lambda g, c: (c, g)),
                  pl.BlockSpec((T, SSD_STATE), lambda g, c: (c, g)),
                  pl.BlockSpec((T, SSD_STATE), lambda g, c: (c, g)),
                  pl.BlockSpec((T, LANE), lambda g, c: (c, g)),
                  pl.BlockSpec((8, LANE), lambda g, c: (0, g))],
        out_specs=(pl.BlockSpec((T, gw), lambda g, c: (c, g)),
                   pl.BlockSpec((1, 1, PAIRS_PER_GROUP, SSD_STATE, LANE), lambda g, c: (g, c, 0, 0, 0))),
        out_shape=(jax.ShapeDtypeStruct((n_rows, SSD_D_INNER), F32),
                   jax.ShapeDtypeStruct((SSD_GROUPS, nc, PAIRS_PER_GROUP, SSD_STATE, LANE), F32)),
        scratch_shapes=[pltpu.VMEM((PAIRS_PER_GROUP, SSD_STATE, LANE), F32)],
        compiler_params=_cparams("parallel", "arbitrary"),
    )(xs, bm, cm, dtr, par)


def _ssd_bwd(xs, bm, cm, dtr, par, hs, dy, name):
    n_rows = xs.shape[0]
    nc = n_rows // T
    gw = PAIRS_PER_GROUP * LANE

    def body(x_ref, b_ref, c_ref, dtr_ref, par_ref, hs_ref, dy_ref,
             dx_ref, db_ref, dc_ref, ddtr_ref, acc_ref, dh_ref):
        @pl.when(pl.program_id(1) == 0)
        def _():
            dh_ref[...] = jnp.zeros_like(dh_ref)
            acc_ref[...] = jnp.zeros_like(acc_ref)

        dt, a, cs, cst, ri, ci = _ssd_scalars(dtr_ref, par_ref)
        tri_s, trit_s, first = _ssd_stacked_masks()
        lane = lax.broadcasted_iota(jnp.int32, (T, LANE), 1)
        lo = lane < SSD_CHUNK // 2
        lane1 = lane[0:1, :]
        ecs = jnp.exp(cs)
        ecst = jnp.exp(cst)
        dec = jnp.exp(cs[T - 1:T, :] - cs)
        etot = jnp.exp(cs[T - 1:T, :])
        bg = b_ref[...]
        cg = c_ref[...]
        bg_b = bg.astype(BF16)
        cg_b = cg.astype(BF16)
        cb = _dot(cg_b, bg_b, _NT)
        cbt = _dot(bg_b, cg_b, _NT)
        cb_s = jnp.concatenate([cb, cb], axis=0)
        cbt_s = jnp.concatenate([cbt, cbt], axis=0)
        bg_s = jnp.concatenate([bg, bg], axis=0)
        cg_s = jnp.concatenate([cg, cg], axis=0)
        cgt_s = jnp.concatenate([cg.T, cg.T], axis=0)
        dbg = jnp.zeros((T, SSD_STATE), F32)
        dcg = jnp.zeros((T, SSD_STATE), F32)
        dcs_acc = jnp.zeros((T, LANE), F32)
        ddt_acc = jnp.zeros((T, LANE), F32)
        dsk_acc = jnp.zeros((1, LANE), F32)
        last_row = lax.broadcasted_iota(jnp.int32, (T, 1), 0) == T - 1
        for j in range(PAIRS_PER_GROUP):
            k0, k1 = 2 * j, 2 * j + 1
            xp = x_ref[:, j * LANE:(j + 1) * LANE]
            dtl = _lane_pick(lo, dt, k0)
            xdt = xp * dtl
            hp = hs_ref[0, 0, j]
            dhn = dh_ref[j]
            dyp = dy_ref[:, j * LANE:(j + 1) * LANE]
            xdt_b, hp_b, dhn_b, dyp_b = (v.astype(BF16) for v in (xdt, hp, dhn, dyp))
            cs_c, cs_r = _col_s(cs, k0), _row_s(cst, k0, first)
            lm_s = jnp.exp(jnp.where(tri_s, cs_c - cs_r, NEG))
            lmt_s = jnp.exp(jnp.where(trit_s, cs_r - cs_c, NEG))
            dec_c, ecs_c = _col_s(dec, k0), _col_s(ecs, k0)
            r1 = _dot(_stack_pair(dyp, lo), jnp.concatenate([xdt_b, hp_b], axis=0), _NT)
            r2 = _dot(_stack_pair(xdt, lo), jnp.concatenate([dyp_b, dhn_b], axis=0), _NT)
            dm_s, dyh_s = r1[:, :T], r1[:, T:]
            dmt_s, xdh_s = r2[:, :T], r2[:, T:]
            mm_s = lm_s * cb_s
            mmt_s = lmt_s * cbt_s
            bdec_s = bg_s * dec_c
            cexp_s = cg_s * ecs_c
            dx_s = _dot(jnp.concatenate([mmt_s, bdec_s], axis=1).astype(BF16),
                        jnp.concatenate([dyp_b, dhn_b], axis=0))
            dxdt = jnp.where(lo, dx_s[:T], dx_s[T:])
            dc_s = _dot((dm_s * lm_s).astype(BF16), bg_b) + dyh_s * ecs_c
            db_s = _dot((dmt_s * lmt_s).astype(BF16), cg_b) + xdh_s * dec_c
            dcg = dcg + dc_s[:T] + dc_s[T:]
            dbg = dbg + db_s[:T] + db_s[T:]
            dh_s = _dot((cgt_s * _row_s(ecst, k0, first)).astype(BF16), dyp_b)
            et = jnp.where(lo[0:1, :], etot[:, k0:k0 + 1], etot[:, k1:k1 + 1])
            dh_ref[j] = dhn * et + jnp.where(lo, dh_s[:T], dh_s[T:])
            e4 = jnp.sum(bdec_s * xdh_s, axis=-1, keepdims=True)
            dcs_s = (jnp.sum(dm_s * mm_s, axis=-1, keepdims=True) - jnp.sum(dmt_s * mmt_s, axis=-1, keepdims=True)
                     + jnp.sum(cexp_s * dyh_s, axis=-1, keepdims=True) - e4)
            hd = hp * dhn
            tsum0 = jnp.sum(e4[:T]) + etot[:, k0:k0 + 1] * jnp.sum(jnp.where(lo, hd, 0.0))
            tsum1 = jnp.sum(e4[T:]) + etot[:, k1:k1 + 1] * jnp.sum(jnp.where(lo, 0.0, hd))
            dcs0 = dcs_s[:T] + jnp.where(last_row, tsum0, 0.0)
            dcs1 = dcs_s[T:] + jnp.where(last_row, tsum1, 0.0)
            dcs_acc = jnp.where(lane == k0, dcs0, jnp.where(lane == k1, dcs1, dcs_acc))
            prod = dxdt * xp
            ddt_lo = jnp.sum(jnp.where(lo, prod, 0.0), axis=-1, keepdims=True)
            ddt_hi = jnp.sum(jnp.where(lo, 0.0, prod), axis=-1, keepdims=True)
            ddt_acc = jnp.where(lane == k0, ddt_lo, jnp.where(lane == k1, ddt_hi, ddt_acc))
            dyx = dyp * xp
            dsk_acc = jnp.where(lane1 == k0, jnp.sum(jnp.where(lo, dyx, 0.0)),
                                jnp.where(lane1 == k1, jnp.sum(jnp.where(lo, 0.0, dyx)), dsk_acc))
            dsk = jnp.where(lo[0:1, :], par_ref[2:3, k0:k0 + 1], par_ref[2:3, k1:k1 + 1])
            dx_ref[:, j * LANE:(j + 1) * LANE] = dxdt * dtl + dsk * dyp
        db_ref[...] = dbg
        dc_ref[...] = dcg
        triu = (ci >= ri).astype(F32)
        dda = _dot_hi(triu, dcs_acc)
        ddt = ddt_acc + dda * a
        ddtr = ddt * _sig(dtr_ref[...] + par_ref[0:1, :])
        ddtr_ref[...] = ddtr.astype(BF16)
        acc_ref[0:1, :] += jnp.sum(ddtr, axis=0, keepdims=True)
        acc_ref[1:2, :] += jnp.sum(dda * dt, axis=0, keepdims=True) * a
        acc_ref[2:3, :] += dsk_acc

    def rev(g, c):
        return (nc - 1 - c, g)

    return pl.pallas_call(
        body, name=name, grid=(SSD_GROUPS, nc),
        in_specs=[pl.BlockSpec((T, gw), rev),
                  pl.BlockSpec((T, SSD_STATE), rev),
                  pl.BlockSpec((T, SSD_STATE), rev),
                  pl.BlockSpec((T, LANE), rev),
                  pl.BlockSpec((8, LANE), lambda g, c: (0, g)),
                  pl.BlockSpec((1, 1, PAIRS_PER_GROUP, SSD_STATE, LANE), lambda g, c: (g, nc - 1 - c, 0, 0, 0)),
                  pl.BlockSpec((T, gw), rev)],
        out_specs=(pl.BlockSpec((T, gw), rev),
                   pl.BlockSpec((T, SSD_STATE), rev),
                   pl.BlockSpec((T, SSD_STATE), rev),
                   pl.BlockSpec((T, LANE), rev),
                   pl.BlockSpec((8, LANE), lambda g, c: (0, g))),
        out_shape=(jax.ShapeDtypeStruct((n_rows, SSD_D_INNER), F32),
                   jax.ShapeDtypeStruct((n_rows, BC_DIM), F32),
                   jax.ShapeDtypeStruct((n_rows, BC_DIM), F32),
                   jax.ShapeDtypeStruct((n_rows, DT_PAD), BF16),
                   jax.ShapeDtypeStruct((8, DT_PAD), F32)),
        scratch_shapes=[pltpu.VMEM((PAIRS_PER_GROUP, SSD_STATE, LANE), F32)],
        compiler_params=_cparams("parallel", "arbitrary"),
    )(xs, bm, cm, dtr, par, hs, dy)


ADAM_ROWS = 256


def _adamw(lands, w, m, v, name):
    na = len(lands)
    n_slots, r, wd = lands[0].shape
    tr = r if r <= 2 * ADAM_ROWS else ADAM_ROWS
    nj = r // tr
    bc1 = 1.0 - ADAM_B1 ** ADAM_STEP
    bc2 = 1.0 - ADAM_B2 ** ADAM_STEP

    def body(*refs):
        l_refs = refs[:na]
        w_ref, m_ref, v_ref, g_ref, d_ref, nm_ref, nv_ref = refs[na:]
        for a in range(na):
            @pl.when(pl.program_id(0) == a)
            def _(l_ref=l_refs[a]):
                g = l_ref[0].astype(F32)
                for s in range(1, n_slots):
                    g = g + l_ref[s].astype(F32)
                mn = ADAM_B1 * m_ref[0] + (1.0 - ADAM_B1) * g
                vn = ADAM_B2 * v_ref[0] + (1.0 - ADAM_B2) * (g * g)
                mh = mn / bc1
                vh = vn / bc2
                g_ref[0] = g
                nm_ref[0] = mn
                nv_ref[0] = vn
                d_ref[0] = -ADAM_LR * (mh / (jnp.sqrt(vh) + ADAM_EPS) + ADAM_WD * w_ref[0])

    def land_spec(a):
        return pl.BlockSpec((n_slots, tr, wd),
                            lambda i, j: (0, jnp.where(i == a, j, jnp.where(i < a, 0, nj - 1)), 0))

    blk = pl.BlockSpec((1, tr, wd), lambda i, j: (i, j, 0))
    shp = jax.ShapeDtypeStruct((na, r, wd), F32)
    return pl.pallas_call(
        body, name=name, grid=(na, nj), in_specs=[land_spec(a) for a in range(na)] + [blk, blk, blk],
        out_specs=(blk, blk, blk, blk), out_shape=(shp, shp, shp, shp),
        compiler_params=_cparams("arbitrary", "arbitrary"),
    )(*lands, w, m, v)


def _mesh_pos():
    return lax.axis_index("x"), lax.axis_index("y"), lax.axis_index("c")


def _peer(pos, k):
    x, y, c = pos
    px = 1 - x if (k >> 2) & 1 else x
    py = 1 - y if (k >> 1) & 1 else y
    pc = 1 - c if k & 1 else c
    return px, py, pc


def _flat(pos):
    return 4 * pos[0] + 2 * pos[1] + pos[2]


HBM_SPEC = pl.BlockSpec(memory_space=pl.ANY)


ROW_SHARDED = ("w_ssd_out", "w_att_out", "w_mix_out", "w_ffn_down")
COL_SHARDED = ("w_in", "w_ffn_gate", "w_ffn_up")
GATHERED = ROW_SHARDED + COL_SHARDED + ("conv_w",)
BIG = ROW_SHARDED + COL_SHARDED


SEM_SPEC = pl.BlockSpec(memory_space=pltpu.SEMAPHORE)
TOKEN = jax.ShapeDtypeStruct((8, LANE), F32)
SPLIT_EFFECT = pltpu.SideEffectType.DATAFLOW_SIDE_EFFECTING
GATHER_ROWS = "gather_rows"
GATHER_SLOT = "gather_slot"
SCATTER_ROWS = "scatter_rows"
SCATTER_SLOT = "scatter_slot"


def _land_shape(kind, src):
    if kind == GATHER_ROWS:
        return (N_DEV * src.shape[0],) + src.shape[1:]
    if kind == GATHER_SLOT:
        return (N_DEV,) + src.shape
    if kind == SCATTER_ROWS:
        return (N_DEV, src.shape[0] // N_DEV) + src.shape[1:]
    return src.shape


def _views(kind, src_ref, land_ref, pos, k):
    me = _flat(pos)
    if kind == GATHER_ROWS:
        r = src_ref.shape[0]
        return src_ref, land_ref.at[pl.ds(pl.multiple_of(me * r, 16), r), :]
    if kind == GATHER_SLOT:
        return src_ref, land_ref.at[me]
    dev = _flat(_peer(pos, k))
    if kind == SCATTER_ROWS:
        r = land_ref.shape[1]
        return src_ref.at[pl.ds(pl.multiple_of(dev * r, 16), r), :], land_ref.at[k]
    return src_ref.at[dev], land_ref.at[k]


def _hbm(x):
    return pltpu.with_memory_space_constraint(x, pltpu.HBM)


def _exchange_start(items, after, name):
    kinds = [k for k, _ in items]
    srcs = [_hbm(s) for _, s in items]
    lands = [_hbm(lax.empty(_land_shape(k, s), s.dtype)) for k, s in items]
    n = len(items)
    n_copy = n * (N_DEV - 1)

    def body(*refs):
        src_refs, land_refs = refs[:n], refs[n:2 * n]
        send_sems, recv_sems = refs[2 * n + 1], refs[2 * n + 2]
        token_ref = refs[4 * n + 3]
        pos = _mesh_pos()
        for i, kind in enumerate(kinds):
            for k in range(1, N_DEV):
                s, d = _views(kind, src_refs[i], land_refs[i], pos, k)
                j = i * (N_DEV - 1) + k - 1
                pltpu.make_async_remote_copy(src_ref=s, dst_ref=d, send_sem=send_sems.at[j], recv_sem=recv_sems.at[j],
                                             device_id=_peer(pos, k), device_id_type=MESH_ID).start()
        token_ref[...] = jnp.zeros_like(token_ref)

    arrs = srcs + lands
    outs = pl.pallas_call(
        body, name=name,
        in_specs=[HBM_SPEC] * (2 * n + 1),
        out_specs=[SEM_SPEC, SEM_SPEC] + [HBM_SPEC] * (2 * n) + [pl.BlockSpec(memory_space=pltpu.VMEM)],
        out_shape=[pltpu.SemaphoreType.DMA((n_copy,)), pltpu.SemaphoreType.DMA((n_copy,))]
        + [pltpu.HBM(a.shape, a.dtype) for a in arrs] + [TOKEN],
        input_output_aliases={i: 2 + i for i in range(2 * n)},
        compiler_params=pltpu.CompilerParams(has_side_effects=SPLIT_EFFECT),
    )(*arrs, after)
    return {"kinds": kinds, "send": outs[0], "recv": outs[1], "arrs": outs[2:2 + 2 * n], "token": outs[-1]}


def _exchange_wait(ex, after, name):
    kinds = ex["kinds"]
    n = len(kinds)

    def body(*refs):
        src_refs, land_refs = refs[:n], refs[n:2 * n]
        send_sems, recv_sems = refs[2 * n], refs[2 * n + 1]
        token_ref = refs[-1]
        pos = _mesh_pos()
        for i, kind in enumerate(kinds):
            for k in range(1, N_DEV):
                s, d = _views(kind, src_refs[i], land_refs[i], pos, k)
                j = i * (N_DEV - 1) + k - 1
                cp = pltpu.make_async_remote_copy(src_ref=s, dst_ref=d, send_sem=send_sems.at[j],
                                                  recv_sem=recv_sems.at[j], device_id=_peer(pos, k),
                                                  device_id_type=MESH_ID)
                cp.wait_send()
                cp.wait_recv()
        token_ref[...] = jnp.zeros_like(token_ref)

    outs = pl.pallas_call(
        body, name=name,
        in_specs=[HBM_SPEC] * (2 * n) + [SEM_SPEC, SEM_SPEC, HBM_SPEC],
        out_specs=[HBM_SPEC] * (2 * n) + [pl.BlockSpec(memory_space=pltpu.VMEM)],
        out_shape=[pltpu.HBM(a.shape, a.dtype) for a in ex["arrs"]] + [TOKEN],
        input_output_aliases={i: i for i in range(2 * n)},
        compiler_params=pltpu.CompilerParams(has_side_effects=SPLIT_EFFECT),
    )(*ex["arrs"], ex["send"], ex["recv"], after)
    lands = [_place_own(k, s, d) for k, s, d in zip(kinds, outs[:n], outs[n:2 * n])]
    return lands, outs[-1]


def _place_own(kind, src, land):
    me = _flat(_mesh_pos())
    zeros = (0,) * (src.ndim - 1)
    if kind == GATHER_ROWS:
        return lax.dynamic_update_slice(land, src, (me * src.shape[0],) + zeros)
    if kind == GATHER_SLOT:
        return lax.dynamic_update_slice(land, src[None], (me,) + (0,) * src.ndim)
    if kind == SCATTER_ROWS:
        r = land.shape[1]
        own = lax.dynamic_slice(src, (me * r,) + zeros, (r,) + src.shape[1:])
    else:
        own = lax.dynamic_index_in_dim(src, me, 0, keepdims=False)
    return lax.dynamic_update_slice(land, own[None], (0,) * land.ndim)


def _all_gather_small(x, name):
    r, w = x.shape

    def body(x_ref, out_ref, send_sems, recv_sems):
        pos = _mesh_pos()
        me = _flat(pos)
        copies = []
        for k in range(1, N_DEV):
            cp = pltpu.make_async_remote_copy(
                src_ref=x_ref, dst_ref=out_ref.at[me], send_sem=send_sems.at[k - 1], recv_sem=recv_sems.at[k - 1],
                device_id=_peer(pos, k), device_id_type=MESH_ID)
            cp.start()
            copies.append(cp)
        out_ref[me] = x_ref[...]
        for cp in copies:
            cp.wait()

    vmem = pl.BlockSpec(memory_space=pltpu.VMEM)
    return pl.pallas_call(
        body, name=name, in_specs=[vmem], out_specs=vmem,
        out_shape=jax.ShapeDtypeStruct((N_DEV, r, w), x.dtype),
        scratch_shapes=[pltpu.SemaphoreType.DMA((N_DEV - 1,)), pltpu.SemaphoreType.DMA((N_DEV - 1,))],
        compiler_params=pltpu.CompilerParams(has_side_effects=True),
    )(x)


def _cols(g, lo, hi):
    c = g.shape[-1]
    parts = []
    for d in range(N_DEV):
        a, b = max(lo, d * c), min(hi, (d + 1) * c)
        if a < b:
            parts.append(g[d, :, a - d * c:b - d * c])
    return parts[0] if len(parts) == 1 else jnp.concatenate(parts, axis=1)


def _col_chunks(g):
    c = g.shape[-1] // N_DEV
    return jnp.stack([g[:, d * c:(d + 1) * c] for d in range(N_DEV)])


IN_PART = ("w_in", "conv_w")
OUT_PART = ROW_SHARDED + ("w_ffn_gate", "w_ffn_up")


def _gather_items(w, names, l):
    items = []
    for n in names:
        blk = w[n][l] if n == "conv_w" else w[n][l].astype(BF16)
        items.append((GATHER_ROWS if n in ROW_SHARDED else GATHER_SLOT, blk))
    return items


def _scatter_items(grads, names):
    return [(SCATTER_ROWS, grads[n]) if n in ROW_SHARDED else (SCATTER_SLOT, _col_chunks(grads[n]))
            for n in names]


SMALL = ("ln_in_g", "ln_in_b", "conv_b", "dt_bias", "a_log", "d_skip", "ssd_norm_w", "att_sinks",
         "ln_mix_g", "ln_mix_b", "ln_ffn_g", "ln_ffn_b")


def _pack_small(vals):
    flat = jnp.concatenate([vals[n].reshape(-1) for n in SMALL])
    n = flat.shape[0]
    rows = -(-n // LANE)
    rows = -(-rows // 8) * 8
    return jnp.pad(flat, (0, rows * LANE - n)).reshape(rows, LANE)


def _unpack_small(buf, shapes):
    flat = buf.reshape(-1)
    off = 0
    out = {}
    for n in SMALL:
        cnt = math.prod(shapes[n])
        out[n] = flat[off:off + cnt].reshape(shapes[n])
        off += cnt
    return out


def _to_group_major(v):
    lead = v.shape[:-1]
    t = v.reshape(lead + (SSD_GROUPS, HEADS_PER_GROUP))
    t = jnp.pad(t, [(0, 0)] * len(lead) + [(0, 0), (0, LANE - HEADS_PER_GROUP)])
    return t.reshape(lead + (DT_PAD,))


def _from_group_major(v):
    lead = v.shape[:-1]
    return v.reshape(lead + (SSD_GROUPS, LANE))[..., :HEADS_PER_GROUP].reshape(lead + (SSD_HEADS,))


def _rows8(v):
    return jnp.pad(v, ((0, 8 - v.shape[0]), (0, 0)))


IN_OFFS = {"q": (0, 1024), "kv": (1024, 1280), "z": (1280, 3328), "xs": (3328, 5376), "b": (5376, 5888),
           "c": (5888, 6400), "dt": (6400, 6432), "gl": (6432, 8480)}
PIECES = ("q", "kv", "z", "xs", "b", "c", "dt", "gl")


def _split_w_in(g):
    out = {p: _cols(g, lo, hi) for p, (lo, hi) in IN_OFFS.items()}
    out["dt"] = _to_group_major(out["dt"])
    return out


def _join_dw_in(dws):
    dws = dict(dws)
    dws["dt"] = _from_group_major(dws["dt"])
    return jnp.concatenate([dws[p] for p in PIECES], axis=1)


def _params_out(W):
    p = {n: W[n] for n in ROW_SHARDED}
    for n in ("w_ffn_gate", "w_ffn_up"):
        p[n] = _cols(W[n], 0, FFN_HIDDEN)
    return p


def _params_in(l, W, sm):
    p = {"w_in": _split_w_in(W["w_in"])}
    cw = _cols(W["conv_w"], 0, SSD_D_INNER + 2 * BC_DIM)
    cb = sm["conv_b"][l]
    segs = {"xs": (0, 2048), "b": (2048, 2560), "c": (2560, 3072)}
    p["conv_w8"] = {s: _rows8(cw[:, lo:hi]) for s, (lo, hi) in segs.items()}
    p["conv_b8"] = {s: _rows8(cb[None, lo:hi]) for s, (lo, hi) in segs.items()}
    p["ssd_par"] = _rows8(jnp.stack([_to_group_major(sm["dt_bias"][l]), _to_group_major(sm["a_log"][l]),
                                     _to_group_major(sm["d_skip"][l])]))
    p["norm_w"] = sm["ssd_norm_w"][l]
    p["sinks8"] = _rows8(jnp.pad(sm["att_sinks"][l], (0, LANE - ATT_HEADS))[None])
    for n in ("ln_mix_g", "ln_mix_b", "ln_ffn_g", "ln_ffn_b"):
        p[n] = sm[n][l]
    return p


def _fwd_mixers(h0, p, l, dep=None):
    tag = f"l{l}_"
    a = {"h0": h0}
    for pc in PIECES:
        a[pc] = _mm(h0, p["w_in"][pc], "nn", tag + "proj_" + pc, dep=dep)
    for s in ("xs", "b", "c"):
        a[s + "c"] = _conv_fwd(a[s], p["conv_w8"][s], p["conv_b8"][s], tag + "conv_" + s)
    a["y"], a["hs"] = _ssd_fwd(a["xsc"], a["bc"], a["cc"], a["dt"], p["ssd_par"], tag + "ssd_fwd")
    a["yn"] = _gnorm_fwd(a["y"], a["z"], p["norm_w"], tag + "gnorm")
    a["att"] = _att_fwd(a["q"], a["kv"], p["sinks8"], tag + "att_fwd")
    return a


def _fwd_out(a, p, l, dep=None):
    tag = f"l{l}_"
    h0 = a["h0"]
    a["ya"] = _mm(a["yn"], p["w_ssd_out"], "nn", tag + "ssd_out", dep=dep)
    a["yb"] = _mm(a["att"], p["w_att_out"], "nn", tag + "att_out", dep=dep)
    a["merged"] = _merge_fwd(a["gl"], a["ya"], a["yb"], tag + "merge")
    a["mix"] = _mm(a["merged"], p["w_mix_out"], "nn", tag + "mix_out")
    a["h1"] = _ln_fwd(h0, a["mix"], p["ln_mix_g"], p["ln_mix_b"], ALPHA, tag + "ln_mix")
    a["fg"] = _mm(a["h1"], p["w_ffn_gate"], "nn", tag + "ffn_gate")
    a["fu"] = _mm(a["h1"], p["w_ffn_up"], "nn", tag + "ffn_up")
    a["act"] = _swiglu_fwd(a["fg"], a["fu"], tag + "swiglu")
    a["ffn"] = _mm(a["act"], p["w_ffn_down"], "nn", tag + "ffn_down")
    a["h2"] = _ln_fwd(a["h1"], a["ffn"], p["ln_ffn_g"], p["ln_ffn_b"], ALPHA, tag + "ln_ffn")
    return a


def _dw(x, dy, name, dep=None):
    return _mm(x, dy, "tn", name, out_dtype=BF16, dep=dep)


def _bwd_out(a, p, dh2, l, dep=None):
    tag = f"l{l}_b_"
    gw, gs = {}, {}
    du2, acc = _ln_bwd(a["h1"], a["ffn"], p["ln_ffn_g"], dh2, ALPHA, tag + "ln_ffn")
    gs["ln_ffn_g"], gs["ln_ffn_b"] = acc[0], acc[1]
    gw["w_ffn_down"] = _dw(a["act"], du2, tag + "dw_down", dep=dep)
    dact = _mm(du2, p["w_ffn_down"], "nt", tag + "dact", dep=dep)
    dfg, dfu = _swiglu_bwd(a["fg"], a["fu"], dact, tag + "swiglu")
    gw["w_ffn_gate"] = _dw(a["h1"], dfg, tag + "dw_gate")
    gw["w_ffn_up"] = _dw(a["h1"], dfu, tag + "dw_up")
    dh1 = _mm(dfg, p["w_ffn_gate"], "nt", tag + "dh1_gate", add=du2, add_scale=ALPHA)
    dh1 = _mm(dfu, p["w_ffn_up"], "nt", tag + "dh1_up", add=dh1)
    du1, acc = _ln_bwd(a["h0"], a["mix"], p["ln_mix_g"], dh1, ALPHA, tag + "ln_mix")
    gs["ln_mix_g"], gs["ln_mix_b"] = acc[0], acc[1]
    gw["w_mix_out"] = _dw(a["merged"], du1, tag + "dw_mix")
    dmerged = _mm(du1, p["w_mix_out"], "nt", tag + "dmerged")
    dya, dyb, dgl = _merge_bwd(a["gl"], a["ya"], a["yb"], dmerged, tag + "merge")
    gw["w_ssd_out"] = _dw(a["yn"], dya, tag + "dw_ssd")
    gw["w_att_out"] = _dw(a["att"], dyb, tag + "dw_att")
    return {"du1": du1, "dya": dya, "dyb": dyb, "dgl": dgl}, gw, gs


def _bwd_mixers(a, p, carry, l, dep=None):
    tag = f"l{l}_b_"
    gs = {}
    du1, dgl = carry["du1"], carry["dgl"]
    dyn = _mm(carry["dya"], p["w_ssd_out"], "nt", tag + "dyn", dep=dep)
    datt = _mm(carry["dyb"], p["w_att_out"], "nt", tag + "datt", out_dtype=BF16, dep=dep)
    dq, dkv, acc = _att_bwd(a["q"], a["kv"], p["sinks8"], datt, tag + "att")
    gs["att_sinks"] = acc[0, :ATT_HEADS]
    dy, dz, acc = _gnorm_bwd(a["y"], a["z"], p["norm_w"], dyn, tag + "gnorm")
    gs["ssd_norm_w"] = acc[0]
    dxs, dbm, dcm, ddt, acc = _ssd_bwd(a["xsc"], a["bc"], a["cc"], a["dt"], p["ssd_par"], a["hs"], dy,
                                       tag + "ssd")
    gs["dt_bias"], gs["a_log"], gs["d_skip"] = (_from_group_major(acc[i]) for i in range(3))
    dpieces = {"q": dq, "kv": dkv, "z": dz, "dt": ddt, "gl": dgl}
    dconv_w, dconv_b = [], []
    for s, dout in (("xs", dxs), ("b", dbm), ("c", dcm)):
        dc, acc = _conv_bwd_pre(a[s], p["conv_w8"][s], p["conv_b8"][s], dout, tag + "conv_pre_" + s)
        dconv_w.append(acc[:CONV_TAPS])
        dconv_b.append(acc[CONV_TAPS])
        dpieces[s] = _conv_bwd_in(dc, p["conv_w8"][s], tag + "conv_in_" + s)
    gconv = jnp.concatenate(dconv_w, axis=1)
    gs["conv_b"] = jnp.concatenate(dconv_b)
    dws = {}
    dh0 = du1
    scale = ALPHA
    for pc in PIECES:
        dws[pc] = _dw(a["h0"], dpieces[pc], tag + "dw_in_" + pc)
        dh0 = _mm(dpieces[pc], p["w_in"][pc], "nt", tag + "dh0_" + pc, add=dh0, add_scale=scale)
        scale = 1.0
    return dh0, _join_dw_in(dws), gconv, gs


def _step(x, target, w, m, v):
    x2 = x[0]
    t2 = target[0]
    tok = jnp.zeros(TOKEN.shape, TOKEN.dtype)

    ex = _exchange_start(_gather_items(w, IN_PART, 0), tok, "gather_l0_in_start")
    lands, tok = _exchange_wait(ex, ex["token"], "gather_l0_in_wait")
    p0 = _params_in(0, dict(zip(IN_PART, lands)), w)
    ex = _exchange_start(_gather_items(w, OUT_PART, 0) + _gather_items(w, IN_PART, 1), tok,
                         "gather_l0_out_l1_in_start")
    h = _ln_fwd(x2, None, w["ln_in_g"], w["ln_in_b"], 1.0, "ln_in")
    a0 = _fwd_mixers(h, p0, 0, dep=ex["token"])
    lands, tok = _exchange_wait(ex, a0["att"], "gather_l0_out_l1_in_wait")
    p0.update(_params_out(dict(zip(OUT_PART, lands))))
    p1 = _params_in(1, dict(zip(IN_PART, lands[len(OUT_PART):])), w)
    ex = _exchange_start(_gather_items(w, OUT_PART, 1), tok, "gather_l1_out_start")
    a0 = _fwd_out(a0, p0, 0, dep=ex["token"])
    lands, tok = _exchange_wait(ex, a0["h2"], "gather_l1_out_wait")
    p1.update(_params_out(dict(zip(OUT_PART, lands))))
    a1 = _fwd_out(_fwd_mixers(a0["h2"], p1, 1), p1, 1)

    sse, dh = _loss_fwd_bwd(a1["h2"], t2, "loss")
    loss = lax.psum(0.5 / D_MODEL * sse[0, 0], ("x", "y", "c"))

    carry, gw1, gs1 = _bwd_out(a1, p1, dh, 1)
    dh, gw1["w_in"], gw1["conv_w"], gs = _bwd_mixers(a1, p1, carry, 1)
    gs1.update(gs)
    ex1 = _exchange_start(_scatter_items(gw1, GATHERED), tok, "scatter_l1_start")
    carry, gw0, gs0 = _bwd_out(a0, p0, dh, 0, dep=ex1["token"])
    lands, tok = _exchange_wait(ex1, carry["dgl"], "scatter_l1_wait")
    land1 = dict(zip(GATHERED, lands))
    ex0 = _exchange_start(_scatter_items(gw0, OUT_PART), tok, "scatter_l0_out_start")
    dh, gw0["w_in"], gw0["conv_w"], gs = _bwd_mixers(a0, p0, carry, 0, dep=ex0["token"])
    gs0.update(gs)
    lands, tok = _exchange_wait(ex0, dh, "scatter_l0_out_wait")
    land0 = dict(zip(OUT_PART, lands))
    ex0 = _exchange_start(_scatter_items(gw0, IN_PART), tok, "scatter_l0_in_start")
    grad_x2, acc = _ln_bwd(x2, None, w["ln_in_g"], dh, 1.0, "ln_in_b")

    outs = [{} for _ in range(4)]

    def update(names):
        res = None
        for n in names:
            res = _adamw([land0[n], land1[n]], w[n], m[n], v[n], "adamw_" + n)
            for o, t in zip(outs, res):
                o[n] = t
        return res[1]

    update(OUT_PART)
    gsm = {"ln_in_g": acc[0], "ln_in_b": acc[1]}
    for n in SMALL[2:]:
        gsm[n] = jnp.stack([gs0[n], gs1[n]])
    small_shapes = {n: w[n].shape for n in SMALL}
    land_s = _all_gather_small(_pack_small(gsm), "small_grads_all_gather")
    res = _adamw([land_s], _pack_small(w)[None], _pack_small(m)[None], _pack_small(v)[None], "adamw_small")
    for o, t in zip(outs, res):
        o.update(_unpack_small(t[0], small_shapes))
    lands, _ = _exchange_wait(ex0, res[1], "scatter_l0_in_wait")
    land0.update(zip(IN_PART, lands))
    update(IN_PART)
    return loss, grad_x2[None], outs


WEIGHT_NAMES = ("ln_in_g", "ln_in_b", "w_in", "conv_w", "conv_b", "dt_bias", "a_log", "d_skip", "ssd_norm_w",
                "att_sinks", "w_ssd_out", "w_att_out", "w_mix_out", "ln_mix_g", "ln_mix_b", "w_ffn_gate",
                "w_ffn_up", "w_ffn_down", "ln_ffn_g", "ln_ffn_b")


def kernel(x, ln_in_g, ln_in_b, w_in, conv_w, conv_b, dt_bias, a_log, d_skip, ssd_norm_w, att_sinks, w_ssd_out, w_att_out, w_mix_out, ln_mix_g, ln_mix_b, w_ffn_gate, w_ffn_up, w_ffn_down, ln_ffn_g, ln_ffn_b, loss_target, m_ln_in_g, m_ln_in_b, m_w_in, m_conv_w, m_conv_b, m_dt_bias, m_a_log, m_d_skip, m_ssd_norm_w, m_att_sinks, m_w_ssd_out, m_w_att_out, m_w_mix_out, m_ln_mix_g, m_ln_mix_b, m_w_ffn_gate, m_w_ffn_up, m_w_ffn_down, m_ln_ffn_g, m_ln_ffn_b, v_ln_in_g, v_ln_in_b, v_w_in, v_conv_w, v_conv_b, v_dt_bias, v_a_log, v_d_skip, v_ssd_norm_w, v_att_sinks, v_w_ssd_out, v_w_att_out, v_w_mix_out, v_ln_mix_g, v_ln_mix_b, v_w_ffn_gate, v_w_ffn_up, v_w_ffn_down, v_ln_ffn_g, v_ln_ffn_b):
    w = dict(zip(WEIGHT_NAMES, (ln_in_g, ln_in_b, w_in, conv_w, conv_b, dt_bias, a_log, d_skip, ssd_norm_w,
                                att_sinks, w_ssd_out, w_att_out, w_mix_out, ln_mix_g, ln_mix_b, w_ffn_gate,
                                w_ffn_up, w_ffn_down, ln_ffn_g, ln_ffn_b)))
    m = dict(zip(WEIGHT_NAMES, (m_ln_in_g, m_ln_in_b, m_w_in, m_conv_w, m_conv_b, m_dt_bias, m_a_log, m_d_skip,
                                m_ssd_norm_w, m_att_sinks, m_w_ssd_out, m_w_att_out, m_w_mix_out, m_ln_mix_g,
                                m_ln_mix_b, m_w_ffn_gate, m_w_ffn_up, m_w_ffn_down, m_ln_ffn_g, m_ln_ffn_b)))
    v = dict(zip(WEIGHT_NAMES, (v_ln_in_g, v_ln_in_b, v_w_in, v_conv_w, v_conv_b, v_dt_bias, v_a_log, v_d_skip,
                                v_ssd_norm_w, v_att_sinks, v_w_ssd_out, v_w_att_out, v_w_mix_out, v_ln_mix_g,
                                v_ln_mix_b, v_w_ffn_gate, v_w_ffn_up, v_w_ffn_down, v_ln_ffn_g, v_ln_ffn_b)))
    loss, grad_x, outs = _step(x, loss_target, w, m, v)
    result = [loss, grad_x]
    for o in outs:
        result.extend(o[n] for n in WEIGHT_NAMES)
    return tuple(result)
```

```python
import functools
import math

import jax
import jax.numpy as jnp
from jax import lax
from jax.experimental import pallas as pl
from jax.experimental.pallas import tpu as pltpu

F32 = jnp.float32
BF16 = jnp.bfloat16

D_MODEL = 1024
DEPTH = 2
N_DEV = 8
ATT_HEADS = 16
ATT_KV_HEADS = 2
ATT_HEAD_DIM = 64
ATT_BLOCK = 128
SSD_D_INNER = 2048
SSD_HEADS = 32
SSD_GROUPS = 4
SSD_STATE = 128
SSD_CHUNK = 128
FFN_HIDDEN = 2816
LN_EPS = 1e-5
RMS_EPS = 1e-5
ALPHA = (2 * DEPTH) ** 0.25
Q_DIM = 1024
KV_DIM = 128
BC_DIM = 512
IN_DIM = 8480
IN_SHARD = IN_DIM // N_DEV
DT_PAD = 512

ADAM_LR = 0.001
ADAM_B1 = 0.9
ADAM_B2 = 0.999
ADAM_EPS = 1e-08
ADAM_WD = 0.01
ADAM_STEP = 10

LANE = 128
VMEM_LIMIT = 48 * 1024 * 1024
PACK_W = 1024
NEG = -1e30

_NN = (((1,), (0,)), ((), ()))
_NT = (((1,), (1,)), ((), ()))
_TN = (((0,), (0,)), ((), ()))
MESH_ID = pl.DeviceIdType.MESH


def _dot(a, b, dims=_NN):
    return lax.dot_general(a, b, dims, preferred_element_type=F32)


def _dot_hi(a, b):
    return lax.dot_general(a, b, _NN, preferred_element_type=F32, precision=lax.Precision.HIGHEST)


def _sig(x):
    return 1.0 / (1.0 + jnp.exp(-x))


def _softplus(x):
    return jnp.maximum(x, 0.0) + jnp.log(1.0 + jnp.exp(-jnp.abs(x)))


def _cparams(*sem):
    return pltpu.CompilerParams(dimension_semantics=sem, vmem_limit_bytes=VMEM_LIMIT)


def _pick(n, cap):
    if n <= cap:
        return n
    best = None
    for t in range(LANE, cap + 1, LANE):
        if n % t == 0:
            best = t
    assert best is not None, (n, cap)
    return best


def _tile(n):
    if n <= 1024 or n % 1024 == 0:
        return min(n, 1024)
    return _pick(n, 1408)


def _rows(n):
    return min(512, n)


def _window(x):
    return x if isinstance(x, tuple) else (x, 0, x.shape[1])


def _mm(a, b, mode, name, add=None, add_scale=1.0, out_dtype=F32, dep=None):
    if mode == "nn":
        m, k = a.shape
        n = b.shape[1]
    elif mode == "nt":
        m, k = a.shape
        n = b.shape[0]
    else:
        k, m = a.shape
        n = b.shape[1]
    tm = _tile(m)
    tn = _tile(n)
    tk = _tile(k)
    nk = k // tk
    has_add = add is not None
    dims = {"nn": _NN, "nt": _NT, "tn": _TN}[mode]

    def body(*refs):
        if dep is not None:
            refs = refs[:-3] + refs[-2:]
        if has_add:
            a_ref, b_ref, add_ref, o_ref, acc_ref = refs
        else:
            a_ref, b_ref, o_ref, acc_ref = refs
        kk = pl.program_id(2)

        @pl.when(kk == 0)
        def _():
            if has_add:
                acc_ref[...] = add_scale * add_ref[...].astype(F32)
            else:
                acc_ref[...] = jnp.zeros_like(acc_ref)

        acc_ref[...] += _dot(a_ref[...].astype(BF16), b_ref[...].astype(BF16), dims)

        @pl.when(kk == nk - 1)
        def _():
            o_ref[...] = acc_ref[...].astype(o_ref.dtype)

    if mode == "nn":
        a_spec = pl.BlockSpec((tm, tk), lambda i, j, kk: (i, kk))
        b_spec = pl.BlockSpec((tk, tn), lambda i, j, kk: (kk, j))
    elif mode == "nt":
        a_spec = pl.BlockSpec((tm, tk), lambda i, j, kk: (i, kk))
        b_spec = pl.BlockSpec((tn, tk), lambda i, j, kk: (j, kk))
    else:
        a_spec = pl.BlockSpec((tk, tm), lambda i, j, kk: (kk, i))
        b_spec = pl.BlockSpec((tk, tn), lambda i, j, kk: (kk, j))
    o_spec = pl.BlockSpec((tm, tn), lambda i, j, kk: (i, j))
    in_specs = [a_spec, b_spec] + ([o_spec] if has_add else [])
    args = (a, b) + ((add,) if has_add else ())
    if dep is not None:
        in_specs.append(pl.BlockSpec((8, LANE), lambda i, j, kk: (0, 0)))
        args += (dep,)
    return pl.pallas_call(
        body, name=name, grid=(m // tm, n // tn, nk),
        in_specs=in_specs, out_specs=o_spec,
        out_shape=jax.ShapeDtypeStruct((m, n), out_dtype),
        scratch_shapes=[pltpu.VMEM((tm, tn), F32)],
        compiler_params=_cparams("parallel", "parallel", "arbitrary"),
    )(*args)


def _vec_spec(width):
    return pl.BlockSpec((1, width), lambda i: (0, 0))


def _ln_fwd(a, b, gamma, beta, alpha, name):
    n_rows, dm = a.shape
    has_b = b is not None

    def body(*refs):
        if has_b:
            a_ref, b_ref, g_ref, be_ref, o_ref = refs
            u = alpha * a_ref[...] + b_ref[...]
        else:
            a_ref, g_ref, be_ref, o_ref = refs
            u = a_ref[...]
        mu = jnp.mean(u, axis=-1, keepdims=True)
        d = u - mu
        var = jnp.mean(d * d, axis=-1, keepdims=True)
        o_ref[...] = d * lax.rsqrt(var + LN_EPS) * g_ref[...] + be_ref[...]

    row = pl.BlockSpec((_rows(n_rows),dm), lambda i: (i, 0))
    in_specs = [row] + ([row] if has_b else []) + [_vec_spec(dm), _vec_spec(dm)]
    args = (a,) + ((b,) if has_b else ()) + (gamma.reshape(1, dm), beta.reshape(1, dm))
    return pl.pallas_call(
        body, name=name, grid=(n_rows // _rows(n_rows),), in_specs=in_specs, out_specs=row,
        out_shape=jax.ShapeDtypeStruct((n_rows, dm), F32),
        compiler_params=_cparams("parallel"),
    )(*args)


def _ln_bwd(a, b, gamma, dy, alpha, name):
    n_rows, dm = a.shape
    has_b = b is not None

    def body(*refs):
        if has_b:
            a_ref, b_ref, g_ref, dy_ref, du_ref, acc_ref = refs
            u = alpha * a_ref[...] + b_ref[...]
        else:
            a_ref, g_ref, dy_ref, du_ref, acc_ref = refs
            u = a_ref[...]

        @pl.when(pl.program_id(0) == 0)
        def _():
            acc_ref[...] = jnp.zeros_like(acc_ref)

        mu = jnp.mean(u, axis=-1, keepdims=True)
        d = u - mu
        var = jnp.mean(d * d, axis=-1, keepdims=True)
        rstd = lax.rsqrt(var + LN_EPS)
        xhat = d * rstd
        dyv = dy_ref[...]
        acc_ref[0:1, :] += jnp.sum(dyv * xhat, axis=0, keepdims=True)
        acc_ref[1:2, :] += jnp.sum(dyv, axis=0, keepdims=True)
        dxh = dyv * g_ref[...]
        m1 = jnp.mean(dxh, axis=-1, keepdims=True)
        m2 = jnp.mean(dxh * xhat, axis=-1, keepdims=True)
        du_ref[...] = rstd * (dxh - m1 - xhat * m2)

    row = pl.BlockSpec((_rows(n_rows),dm), lambda i: (i, 0))
    in_specs = [row] + ([row] if has_b else []) + [_vec_spec(dm), row]
    args = (a,) + ((b,) if has_b else ()) + (gamma.reshape(1, dm), dy)
    return pl.pallas_call(
        body, name=name, grid=(n_rows // _rows(n_rows),), in_specs=in_specs,
        out_specs=(row, pl.BlockSpec((8, dm), lambda i: (0, 0))),
        out_shape=(jax.ShapeDtypeStruct((n_rows, dm), F32), jax.ShapeDtypeStruct((8, dm), F32)),
        compiler_params=_cparams("arbitrary"),
    )(*args)


def _loss_fwd_bwd(y, target, name):
    n_rows, dm = y.shape

    def body(y_ref, t_ref, acc_ref, dy_ref):
        @pl.when(pl.program_id(0) == 0)
        def _():
            acc_ref[...] = jnp.zeros_like(acc_ref)

        d = y_ref[...] - t_ref[...]
        acc_ref[...] += jnp.sum(d * d)
        dy_ref[...] = d * (1.0 / dm)

    row = pl.BlockSpec((_rows(n_rows),dm), lambda i: (i, 0))
    return pl.pallas_call(
        body, name=name, grid=(n_rows // _rows(n_rows),), in_specs=[row, row],
        out_specs=(pl.BlockSpec((8, LANE), lambda i: (0, 0)), row),
        out_shape=(jax.ShapeDtypeStruct((8, LANE), F32), jax.ShapeDtypeStruct((n_rows, dm), F32)),
        compiler_params=_cparams("arbitrary"),
    )(y, target)


def _swiglu_fwd(g, u, name):
    n_rows, w = g.shape
    tw = _pick(w, 1408)

    def body(g_ref, u_ref, o_ref):
        gv = g_ref[...]
        o_ref[...] = (gv * _sig(gv) * u_ref[...]).astype(BF16)

    blk = pl.BlockSpec((_rows(n_rows),tw), lambda i, j: (i, j))
    return pl.pallas_call(
        body, name=name, grid=(n_rows // _rows(n_rows), w // tw), in_specs=[blk, blk], out_specs=blk,
        out_shape=jax.ShapeDtypeStruct((n_rows, w), BF16),
        compiler_params=_cparams("parallel", "parallel"),
    )(g, u)


def _swiglu_bwd(g, u, dact, name):
    n_rows, w = g.shape
    tw = _pick(w, 1408)

    def body(g_ref, u_ref, da_ref, dg_ref, du_ref):
        gv = g_ref[...]
        s = _sig(gv)
        da = da_ref[...]
        dg_ref[...] = (da * u_ref[...] * (s * (1.0 + gv * (1.0 - s)))).astype(BF16)
        du_ref[...] = (da * gv * s).astype(BF16)

    blk = pl.BlockSpec((_rows(n_rows),tw), lambda i, j: (i, j))
    return pl.pallas_call(
        body, name=name, grid=(n_rows // _rows(n_rows), w // tw), in_specs=[blk, blk, blk], out_specs=(blk, blk),
        out_shape=(jax.ShapeDtypeStruct((n_rows, w), BF16), jax.ShapeDtypeStruct((n_rows, w), BF16)),
        compiler_params=_cparams("parallel", "parallel"),
    )(g, u, dact)


def _gate_specs(gl, n_rows, dm):
    arr, g0, _ = _window(gl)
    return arr, [pl.BlockSpec((_rows(n_rows), dm), lambda i, k=k: (i, g0 // dm + k)) for k in range(2)]


def _merge_fwd(gl, ya, yb, name):
    n_rows, dm = ya.shape
    gl_arr, gspecs = _gate_specs(gl, n_rows, dm)

    def body(ga_ref, gb_ref, ya_ref, yb_ref, o_ref):
        o_ref[...] = (_sig(ga_ref[...]) * ya_ref[...] + _sig(gb_ref[...]) * yb_ref[...]).astype(BF16)

    row = pl.BlockSpec((_rows(n_rows),dm), lambda i: (i, 0))
    return pl.pallas_call(
        body, name=name, grid=(n_rows // _rows(n_rows),), in_specs=gspecs + [row, row], out_specs=row,
        out_shape=jax.ShapeDtypeStruct((n_rows, dm), BF16),
        compiler_params=_cparams("parallel"),
    )(gl_arr, gl_arr, ya, yb)


def _merge_bwd(gl, ya, yb, dmerged, name):
    n_rows, dm = ya.shape

    gl_arr, gspecs = _gate_specs(gl, n_rows, dm)

    def body(ga_ref, gb_ref, ya_ref, yb_ref, dm_ref, dya_ref, dyb_ref, dgl_ref):
        ga = _sig(ga_ref[...])
        gb = _sig(gb_ref[...])
        dmv = dm_ref[...]
        dya_ref[...] = (dmv * ga).astype(BF16)
        dyb_ref[...] = (dmv * gb).astype(BF16)
        dgl_ref[:, :dm] = (dmv * ya_ref[...] * ga * (1.0 - ga)).astype(BF16)
        dgl_ref[:, dm:] = (dmv * yb_ref[...] * gb * (1.0 - gb)).astype(BF16)

    row = pl.BlockSpec((_rows(n_rows),dm), lambda i: (i, 0))
    row2 = pl.BlockSpec((_rows(n_rows),2 * dm), lambda i: (i, 0))
    return pl.pallas_call(
        body, name=name, grid=(n_rows // _rows(n_rows),), in_specs=gspecs + [row, row, row], out_specs=(row, row, row2),
        out_shape=(jax.ShapeDtypeStruct((n_rows, dm), BF16), jax.ShapeDtypeStruct((n_rows, dm), BF16),
                   jax.ShapeDtypeStruct((n_rows, 2 * dm), BF16)),
        compiler_params=_cparams("parallel"),
    )(gl_arr, gl_arr, ya, yb, dmerged)


CONV_TAPS = 4
CONV_COLS = 512
HALO = 8


def _shift_down(cur, prev8, s, row8):
    r = pltpu.roll(cur, s, axis=0)
    top = jnp.where(row8 < s, pltpu.roll(prev8, s, axis=0), r[0:HALO])
    return jnp.concatenate([top, r[HALO:]], axis=0)


def _shift_up(cur, next8, s, row8):
    n = cur.shape[0]
    r = pltpu.roll(cur, n - s, axis=0)
    bot = jnp.where(row8 >= HALO - s, pltpu.roll(next8, HALO - s, axis=0), r[n - HALO:])
    return jnp.concatenate([r[:n - HALO], bot], axis=0)


def _conv_pre(u_ref, prev_ref, w_ref, b_ref, li):
    cur = u_ref[...]
    prev8 = jnp.where(li == 0, 0.0, prev_ref[...])
    row8 = lax.broadcasted_iota(jnp.int32, prev8.shape, 0)
    shifted = [cur] + [_shift_down(cur, prev8, s, row8) for s in range(1, CONV_TAPS)]
    acc = b_ref[...] + shifted[0] * w_ref[CONV_TAPS - 1:CONV_TAPS, :]
    for s in range(1, CONV_TAPS):
        acc = acc + shifted[s] * w_ref[CONV_TAPS - 1 - s:CONV_TAPS - s, :]
    return acc, shifted


def _conv_specs(n_rows, tl, col0=0):
    off = col0 // CONV_COLS
    cur = pl.BlockSpec((tl, CONV_COLS), lambda cj, li: (li, cj + off))
    prev = pl.BlockSpec((HALO, CONV_COLS), lambda cj, li: (jnp.maximum(li * (tl // HALO) - 1, 0), cj + off))
    nxt = pl.BlockSpec((HALO, CONV_COLS),
                       lambda cj, li: (jnp.minimum((li + 1) * (tl // HALO), n_rows // HALO - 1), cj + off))
    par = pl.BlockSpec((8, CONV_COLS), lambda cj, li: (0, cj + off))
    return cur, prev, nxt, par


def _conv_fwd(u, w8, b8, name):
    u, u0, c = _window(u)
    n_rows = u.shape[0]
    tl = _rows(n_rows)
    cur, _, _, par = _conv_specs(n_rows, tl)
    ucur, prev, _, _ = _conv_specs(n_rows, tl, u0)

    def body(u_ref, prev_ref, w_ref, b_ref, o_ref):
        acc, _ = _conv_pre(u_ref, prev_ref, w_ref, b_ref[0:1, :], pl.program_id(1))
        o_ref[...] = acc * _sig(acc)

    return pl.pallas_call(
        body, name=name, grid=(c // CONV_COLS, n_rows // tl), in_specs=[ucur, prev, par, par], out_specs=cur,
        out_shape=jax.ShapeDtypeStruct((n_rows, c), F32),
        compiler_params=_cparams("parallel", "parallel"),
    )(u, u, w8, b8)


def _conv_bwd_pre(u, w8, b8, dout, name):
    u, u0, c = _window(u)
    n_rows = u.shape[0]
    tl = _rows(n_rows)
    cur, _, _, par = _conv_specs(n_rows, tl)
    ucur, prev, _, _ = _conv_specs(n_rows, tl, u0)

    def body(u_ref, prev_ref, w_ref, b_ref, do_ref, dc_ref, acc_ref):
        @pl.when(pl.program_id(1) == 0)
        def _():
            acc_ref[...] = jnp.zeros_like(acc_ref)

        acc, shifted = _conv_pre(u_ref, prev_ref, w_ref, b_ref[0:1, :], pl.program_id(1))
        sg = _sig(acc)
        dc = do_ref[...] * (sg * (1.0 + acc * (1.0 - sg)))
        dc_ref[...] = dc
        for k in range(CONV_TAPS):
            acc_ref[k:k + 1, :] += jnp.sum(dc * shifted[CONV_TAPS - 1 - k], axis=0, keepdims=True)
        acc_ref[CONV_TAPS:CONV_TAPS + 1, :] += jnp.sum(dc, axis=0, keepdims=True)

    return pl.pallas_call(
        body, name=name, grid=(c // CONV_COLS, n_rows // tl), in_specs=[ucur, prev, par, par, cur],
        out_specs=(cur, par),
        out_shape=(jax.ShapeDtypeStruct((n_rows, c), F32), jax.ShapeDtypeStruct((8, c), F32)),
        compiler_params=_cparams("parallel", "arbitrary"),
    )(u, u, w8, b8, dout)


def _conv_bwd_in(dc, w8, name):
    n_rows, c = dc.shape
    tl = _rows(n_rows)
    cur, _, nxt, par = _conv_specs(n_rows, tl)
    n_l = n_rows // tl

    def body(dc_ref, next_ref, w_ref, o_ref):
        cur_v = dc_ref[...]
        next8 = jnp.where(pl.program_id(1) == n_l - 1, 0.0, next_ref[...])
        row8 = lax.broadcasted_iota(jnp.int32, next8.shape, 0)
        acc = cur_v * w_ref[CONV_TAPS - 1:CONV_TAPS, :]
        for s in range(1, CONV_TAPS):
            acc = acc + _shift_up(cur_v, next8, s, row8) * w_ref[CONV_TAPS - 1 - s:CONV_TAPS - s, :]
        o_ref[...] = acc.astype(BF16)

    return pl.pallas_call(
        body, name=name, grid=(c // CONV_COLS, n_l), in_specs=[cur, nxt, par], out_specs=cur,
        out_shape=jax.ShapeDtypeStruct((n_rows, c), BF16),
        compiler_params=_cparams("parallel", "parallel"),
    )(dc, dc, w8)


NORM_GROUP = SSD_D_INNER // SSD_GROUPS


def _gnorm_fwd(y, z, w, name):
    n_rows, c = y.shape
    z, z0, _ = _window(z)
    zoff = z0 // NORM_GROUP

    def body(y_ref, z_ref, w_ref, o_ref):
        zv = z_ref[...]
        yg = y_ref[...] * (zv * _sig(zv))
        r = lax.rsqrt(jnp.mean(yg * yg, axis=-1, keepdims=True) + RMS_EPS)
        o_ref[...] = (yg * r * w_ref[...]).astype(BF16)

    blk = pl.BlockSpec((_rows(n_rows),NORM_GROUP), lambda i, j: (i, j))
    zblk = pl.BlockSpec((_rows(n_rows),NORM_GROUP), lambda i, j: (i, j + zoff))
    wspec = pl.BlockSpec((1, NORM_GROUP), lambda i, j: (0, j))
    return pl.pallas_call(
        body, name=name, grid=(n_rows // _rows(n_rows), c // NORM_GROUP), in_specs=[blk, zblk, wspec], out_specs=blk,
        out_shape=jax.ShapeDtypeStruct((n_rows, c), BF16),
        compiler_params=_cparams("parallel", "parallel"),
    )(y, z, w.reshape(1, c))


def _gnorm_bwd(y, z, w, dyn, name):
    n_rows, c = y.shape
    z, z0, _ = _window(z)
    zoff = z0 // NORM_GROUP

    def body(y_ref, z_ref, w_ref, dn_ref, dy_ref, dz_ref, acc_ref):
        @pl.when(pl.program_id(1) == 0)
        def _():
            acc_ref[...] = jnp.zeros_like(acc_ref)

        zv = z_ref[...]
        yv = y_ref[...]
        sz = _sig(zv)
        silu = zv * sz
        yg = yv * silu
        r = lax.rsqrt(jnp.mean(yg * yg, axis=-1, keepdims=True) + RMS_EPS)
        nrm = yg * r
        dn = dn_ref[...]
        acc_ref[0:1, :] += jnp.sum(dn * nrm, axis=0, keepdims=True)
        dnw = dn * w_ref[...]
        dyg = r * (dnw - nrm * jnp.mean(dnw * nrm, axis=-1, keepdims=True))
        dy_ref[...] = dyg * silu
        dz_ref[...] = (dyg * yv * (sz * (1.0 + zv * (1.0 - sz)))).astype(BF16)

    blk = pl.BlockSpec((_rows(n_rows),NORM_GROUP), lambda j, i: (i, j))
    zblk = pl.BlockSpec((_rows(n_rows),NORM_GROUP), lambda j, i: (i, j + zoff))
    wspec = pl.BlockSpec((1, NORM_GROUP), lambda j, i: (0, j))
    aspec = pl.BlockSpec((8, NORM_GROUP), lambda j, i: (0, j))
    return pl.pallas_call(
        body, name=name, grid=(c // NORM_GROUP, n_rows // _rows(n_rows)), in_specs=[blk, zblk, wspec, blk],
        out_specs=(blk, blk, aspec),
        out_shape=(jax.ShapeDtypeStruct((n_rows, c), F32), jax.ShapeDtypeStruct((n_rows, c), BF16),
                   jax.ShapeDtypeStruct((8, c), F32)),
        compiler_params=_cparams("parallel", "arbitrary"),
    )(y, z, w.reshape(1, c), dyn)


ATT_SCALE = ATT_HEAD_DIM ** -0.5
ATT_SLOPES = [2.0 ** (-8.0 * (h + 1) / ATT_HEADS) for h in range(ATT_HEADS)]
Q_PER_KV = ATT_HEADS // ATT_KV_HEADS


def _dup_half(t, g, lo):
    tr = pltpu.roll(t, ATT_HEAD_DIM, axis=1)
    return jnp.where(lo, t, tr) if g == 0 else jnp.where(lo, tr, t)


def _att_band(kv_ref, kvp_ref, n):
    cur = kv_ref[...]
    prev = jnp.where(n == 0, 0.0, kvp_ref[...])
    lo = lax.broadcasted_iota(jnp.int32, (ATT_BLOCK, LANE), 1) < ATT_HEAD_DIM
    bands = []
    for g in range(ATT_KV_HEADS):
        kb = jnp.concatenate([_dup_half(prev[:, :LANE], g, lo), _dup_half(cur[:, :LANE], g, lo)], axis=0)
        vb = jnp.concatenate([_dup_half(prev[:, LANE:], g, lo), _dup_half(cur[:, LANE:], g, lo)], axis=0)
        bands.append((kb.astype(BF16), vb.astype(BF16)))
    return bands


def _att_tile(n):
    shape = (2 * ATT_BLOCK, ATT_BLOCK)
    row = lax.broadcasted_iota(jnp.int32, shape, 0)
    i = row & (ATT_BLOCK - 1)
    s = lax.broadcasted_iota(jnp.int32, shape, 1)
    upper = s > i
    dist = ((i - s) & (ATT_BLOCK - 1)).astype(F32)
    dead = upper & (n == 0)
    return upper, dist, dead, row[:, 0:1] < ATT_BLOCK


def _stack_pair(t, lo):
    return jnp.concatenate([jnp.where(lo, t, 0.0), jnp.where(lo, 0.0, t)], axis=0).astype(BF16)


def _att_probs(qs, kb, s_ref, j, tile):
    upper, dist, dead, first = tile
    s2 = _dot(qs, kb, _NT)
    slope = jnp.where(first, ATT_SLOPES[2 * j], ATT_SLOPES[2 * j + 1])
    sink = jnp.where(first, s_ref[0:1, 2 * j:2 * j + 1], s_ref[0:1, 2 * j + 1:2 * j + 2])
    s = jnp.where(upper, s2[:, :ATT_BLOCK], s2[:, ATT_BLOCK:]) - slope * dist
    s = jnp.where(dead, NEG, s)
    m = jnp.maximum(jnp.max(s, axis=-1, keepdims=True), sink)
    p = jnp.exp(s - m)
    es = jnp.exp(sink - m)
    inv = 1.0 / (jnp.sum(p, axis=-1, keepdims=True) + es)
    return p * inv, es * inv


def _band_split(t, upper):
    return jnp.concatenate([jnp.where(upper, t, 0.0), jnp.where(upper, 0.0, t)], axis=1)


def _att_fwd(q, kv, sinks8, name):
    q, q0, _ = _window(q)
    kv, kv0, _ = _window(kv)
    qoff, kvoff = q0 // Q_DIM, kv0 // (2 * LANE)
    n_rows = q.shape[0]
    nb = n_rows // ATT_BLOCK

    def body(q_ref, kv_ref, kvp_ref, s_ref, o_ref):
        n = pl.program_id(0)
        bands = _att_band(kv_ref, kvp_ref, n)
        lo = lax.broadcasted_iota(jnp.int32, (ATT_BLOCK, LANE), 1) < ATT_HEAD_DIM
        tile = _att_tile(n)
        for j in range(ATT_HEADS // 2):
            kb, vb = bands[2 * j // Q_PER_KV]
            qs = _stack_pair(q_ref[:, j * LANE:(j + 1) * LANE] * ATT_SCALE, lo)
            p, _ = _att_probs(qs, kb, s_ref, j, tile)
            out = _dot(_band_split(p, tile[0]).astype(BF16), vb)
            o_ref[:, j * LANE:(j + 1) * LANE] = jnp.where(lo, out[:ATT_BLOCK], out[ATT_BLOCK:]).astype(BF16)

    return pl.pallas_call(
        body, name=name, grid=(nb,),
        in_specs=[pl.BlockSpec((ATT_BLOCK, Q_DIM), lambda n: (n, qoff)),
                  pl.BlockSpec((ATT_BLOCK, 2 * LANE), lambda n: (n, kvoff)),
                  pl.BlockSpec((ATT_BLOCK, 2 * LANE), lambda n: (jnp.maximum(n - 1, 0), kvoff)),
                  pl.BlockSpec((8, LANE), lambda n: (0, 0))],
        out_specs=pl.BlockSpec((ATT_BLOCK, Q_DIM), lambda n: (n, 0)),
        out_shape=jax.ShapeDtypeStruct((n_rows, Q_DIM), BF16),
        compiler_params=_cparams("parallel"),
    )(q, kv, kv, sinks8)


def _att_bwd(q, kv, sinks8, dout, name):
    q, q0, _ = _window(q)
    kv, kv0, _ = _window(kv)
    qoff, kvoff = q0 // Q_DIM, kv0 // (2 * LANE)
    n_rows = q.shape[0]
    nb = n_rows // ATT_BLOCK

    def body(q_ref, kv_ref, kvp_ref, s_ref, do_ref, dq_ref, dkv_ref, acc_ref, carry_ref):
        n = pl.program_id(0)

        @pl.when(n == 0)
        def _():
            acc_ref[...] = jnp.zeros_like(acc_ref)
            carry_ref[...] = jnp.zeros_like(carry_ref)

        @pl.when(n == nb)
        def _():
            dkv_ref[...] = carry_ref[...].astype(BF16)

        @pl.when(n < nb)
        def _():
            bands = _att_band(kv_ref, kvp_ref, n)
            lo = lax.broadcasted_iota(jnp.int32, (ATT_BLOCK, LANE), 1) < ATT_HEAD_DIM
            lane1 = lax.broadcasted_iota(jnp.int32, (1, LANE), 1)
            tile = _att_tile(n)
            upper, first = tile[0], tile[3]
            dk_acc = [jnp.zeros((2 * ATT_BLOCK, LANE), F32) for _ in range(ATT_KV_HEADS)]
            dv_acc = [jnp.zeros((2 * ATT_BLOCK, LANE), F32) for _ in range(ATT_KV_HEADS)]
            dsink = jnp.zeros((1, LANE), F32)
            for j in range(ATT_HEADS // 2):
                g = 2 * j // Q_PER_KV
                kb, vb = bands[g]
                qs = _stack_pair(q_ref[:, j * LANE:(j + 1) * LANE] * ATT_SCALE, lo)
                dos = _stack_pair(do_ref[:, j * LANE:(j + 1) * LANE].astype(F32), lo)
                p, ps = _att_probs(qs, kb, s_ref, j, tile)
                dp2 = _dot(dos, vb, _NT)
                dp = jnp.where(upper, dp2[:, :ATT_BLOCK], dp2[:, ATT_BLOCK:])
                delta = jnp.sum(p * dp, axis=-1, keepdims=True)
                ds2 = _band_split(p * (dp - delta), upper)
                psd = ps * delta
                dsink = jnp.where(lane1 == 2 * j, -jnp.sum(jnp.where(first, psd, 0.0)), dsink)
                dsink = jnp.where(lane1 == 2 * j + 1, -jnp.sum(jnp.where(first, 0.0, psd)), dsink)
                dq = _dot(ds2.astype(BF16), kb) * ATT_SCALE
                dq_ref[:, j * LANE:(j + 1) * LANE] = jnp.where(lo, dq[:ATT_BLOCK], dq[ATT_BLOCK:]).astype(BF16)
                dk_acc[g] = dk_acc[g] + _dot(ds2.T.astype(BF16), qs)
                dv_acc[g] = dv_acc[g] + _dot(_band_split(p, upper).T.astype(BF16), dos)
            acc_ref[0:1, :] += dsink
            lo2 = lax.broadcasted_iota(jnp.int32, (2 * ATT_BLOCK, LANE), 1) < ATT_HEAD_DIM
            folded = []
            for acc in (dk_acc, dv_acc):
                t0 = acc[0] + pltpu.roll(acc[0], ATT_HEAD_DIM, axis=1)
                t1 = acc[1] + pltpu.roll(acc[1], ATT_HEAD_DIM, axis=1)
                folded.append(jnp.where(lo2, t0, t1))
            band = jnp.concatenate(folded, axis=1)
            dkv_ref[...] = (carry_ref[...] + band[:ATT_BLOCK]).astype(BF16)
            carry_ref[...] = band[ATT_BLOCK:]

    def qmap(n):
        return (jnp.minimum(n, nb - 1), 0)

    return pl.pallas_call(
        body, name=name, grid=(nb + 1,),
        in_specs=[pl.BlockSpec((ATT_BLOCK, Q_DIM), lambda n: (jnp.minimum(n, nb - 1), qoff)),
                  pl.BlockSpec((ATT_BLOCK, 2 * LANE), lambda n: (jnp.minimum(n, nb - 1), kvoff)),
                  pl.BlockSpec((ATT_BLOCK, 2 * LANE),
                               lambda n: (jnp.maximum(jnp.minimum(n, nb - 1) - 1, 0), kvoff)),
                  pl.BlockSpec((8, LANE), lambda n: (0, 0)),
                  pl.BlockSpec((ATT_BLOCK, Q_DIM), qmap)],
        out_specs=(pl.BlockSpec((ATT_BLOCK, Q_DIM), qmap),
                   pl.BlockSpec((ATT_BLOCK, 2 * LANE), lambda n: (jnp.maximum(n - 1, 0), 0)),
                   pl.BlockSpec((8, LANE), lambda n: (0, 0))),
        out_shape=(jax.ShapeDtypeStruct((n_rows, Q_DIM), BF16), jax.ShapeDtypeStruct((n_rows, 2 * LANE), BF16),
                   jax.ShapeDtypeStruct((8, LANE), F32)),
        scratch_shapes=[pltpu.VMEM((ATT_BLOCK, 2 * LANE), F32)],
        compiler_params=_cparams("arbitrary"),
    )(q, kv, kv, sinks8, dout)


HEADS_PER_GROUP = SSD_HEADS // SSD_GROUPS
PAIRS_PER_GROUP = HEADS_PER_GROUP // 2
T = SSD_CHUNK


def _ssd_scalars(dtr_ref, par_ref):
    dt = _softplus(dtr_ref[...] + par_ref[0:1, :])
    a = -jnp.exp(par_ref[1:2, :])
    ri = lax.broadcasted_iota(jnp.int32, (T, T), 0)
    ci = lax.broadcasted_iota(jnp.int32, (T, T), 1)
    tril = (ri >= ci).astype(F32)
    cs = _dot_hi(tril, dt * a)
    cst = cs.T
    return dt, a, cs, cst, ri, ci


def _ssd_stacked_masks():
    row = lax.broadcasted_iota(jnp.int32, (2 * T, T), 0)
    t = row & (T - 1)
    s = lax.broadcasted_iota(jnp.int32, (2 * T, T), 1)
    return t >= s, s >= t, row[:, 0:1] < T


def _col_s(arr, k0):
    return jnp.concatenate([arr[:, k0:k0 + 1], arr[:, k0 + 1:k0 + 2]], axis=0)


def _row_s(arr_t, k0, first):
    return jnp.where(first, arr_t[k0:k0 + 1, :], arr_t[k0 + 1:k0 + 2, :])


def _lane_pick(lo, arr, k0):
    return jnp.where(lo, arr[:, k0:k0 + 1], arr[:, k0 + 1:k0 + 2])


def _ssd_fwd(xs, bm, cm, dtr, par, name):
    dtr, dt0, _ = _window(dtr)
    dtoff = dt0 // LANE
    n_rows = xs.shape[0]
    nc = n_rows // T
    gw = PAIRS_PER_GROUP * LANE

    def body(x_ref, b_ref, c_ref, dtr_ref, par_ref, y_ref, hs_ref, h_ref):
        @pl.when(pl.program_id(1) == 0)
        def _():
            h_ref[...] = jnp.zeros_like(h_ref)

        dt, a, cs, cst, _, _ = _ssd_scalars(dtr_ref, par_ref)
        tri_s, _, first = _ssd_stacked_masks()
        lo = lax.broadcasted_iota(jnp.int32, (T, LANE), 1) < SSD_CHUNK // 2
        ecs = jnp.exp(cs)
        dect = jnp.exp(cst[:, T - 1:T] - cst)
        etot = jnp.exp(cs[T - 1:T, :])
        bg = b_ref[...]
        cg = c_ref[...]
        cb = _dot(cg.astype(BF16), bg.astype(BF16), _NT)
        cb_s = jnp.concatenate([cb, cb], axis=0)
        cg_s = jnp.concatenate([cg, cg], axis=0)
        bgt_s = jnp.concatenate([bg.T, bg.T], axis=0)
        for j in range(PAIRS_PER_GROUP):
            k0, k1 = 2 * j, 2 * j + 1
            xp = x_ref[:, j * LANE:(j + 1) * LANE]
            hp = h_ref[j]
            hs_ref[0, 0, j] = hp
            rhs = jnp.concatenate([(xp * _lane_pick(lo, dt, k0)).astype(BF16), hp.astype(BF16)], axis=0)
            lm_s = jnp.exp(jnp.where(tri_s, _col_s(cs, k0) - _row_s(cst, k0, first), NEG))
            lhs = jnp.concatenate([lm_s * cb_s, cg_s * _col_s(ecs, k0)], axis=1).astype(BF16)
            y_s = _dot(lhs, rhs)
            s_s = _dot((bgt_s * _row_s(dect, k0, first)).astype(BF16), rhs[:T])
            dsk = jnp.where(lo[0:1, :], par_ref[2:3, k0:k0 + 1], par_ref[2:3, k1:k1 + 1])
            y_ref[:, j * LANE:(j + 1) * LANE] = jnp.where(lo, y_s[:T], y_s[T:]) + dsk * xp
            et = jnp.where(lo[0:1, :], etot[:, k0:k0 + 1], etot[:, k1:k1 + 1])
            h_ref[j] = hp * et + jnp.where(lo, s_s[:T], s_s[T:])

    return pl.pallas_call(
        body, name=name, grid=(SSD_GROUPS, nc),
        in_specs=[pl.BlockSpec((T, gw), lambda g, c: (c, g)),
                  pl.BlockSpec((T, SSD_STATE), lambda g, c: (c, g)),
                  pl.BlockSpec((T, SSD_STATE), lambda g, c: (c, g)),
                  pl.BlockSpec((T, LANE), lambda g, c: (c, g + dtoff)),
                  pl.BlockSpec((8, LANE), lambda g, c: (0, g))],
        out_specs=(pl.BlockSpec((T, gw), lambda g, c: (c, g)),
                   pl.BlockSpec((1, 1, PAIRS_PER_GROUP, SSD_STATE, LANE), lambda g, c: (g, c, 0, 0, 0))),
        out_shape=(jax.ShapeDtypeStruct((n_rows, SSD_D_INNER), F32),
                   jax.ShapeDtypeStruct((SSD_GROUPS, nc, PAIRS_PER_GROUP, SSD_STATE, LANE), F32)),
        scratch_shapes=[pltpu.VMEM((PAIRS_PER_GROUP, SSD_STATE, LANE), F32)],
        compiler_params=_cparams("parallel", "arbitrary"),
    )(xs, bm, cm, dtr, par)


def _ssd_bwd(xs, bm, cm, dtr, par, hs, dy, name):
    dtr, dt0, _ = _window(dtr)
    dtoff = dt0 // LANE
    n_rows = xs.shape[0]
    nc = n_rows // T
    gw = PAIRS_PER_GROUP * LANE

    def body(x_ref, b_ref, c_ref, dtr_ref, par_ref, hs_ref, dy_ref,
             dx_ref, db_ref, dc_ref, ddtr_ref, acc_ref, dh_ref):
        @pl.when(pl.program_id(1) == 0)
        def _():
            dh_ref[...] = jnp.zeros_like(dh_ref)
            acc_ref[...] = jnp.zeros_like(acc_ref)

        dt, a, cs, cst, ri, ci = _ssd_scalars(dtr_ref, par_ref)
        tri_s, trit_s, first = _ssd_stacked_masks()
        lane = lax.broadcasted_iota(jnp.int32, (T, LANE), 1)
        lo = lane < SSD_CHUNK // 2
        lane1 = lane[0:1, :]
        ecs = jnp.exp(cs)
        ecst = jnp.exp(cst)
        dec = jnp.exp(cs[T - 1:T, :] - cs)
        etot = jnp.exp(cs[T - 1:T, :])
        bg = b_ref[...]
        cg = c_ref[...]
        bg_b = bg.astype(BF16)
        cg_b = cg.astype(BF16)
        cb = _dot(cg_b, bg_b, _NT)
        cbt = _dot(bg_b, cg_b, _NT)
        cb_s = jnp.concatenate([cb, cb], axis=0)
        cbt_s = jnp.concatenate([cbt, cbt], axis=0)
        bg_s = jnp.concatenate([bg, bg], axis=0)
        cg_s = jnp.concatenate([cg, cg], axis=0)
        cgt_s = jnp.concatenate([cg.T, cg.T], axis=0)
        dbg = jnp.zeros((T, SSD_STATE), F32)
        dcg = jnp.zeros((T, SSD_STATE), F32)
        dcs_acc = jnp.zeros((T, LANE), F32)
        ddt_acc = jnp.zeros((T, LANE), F32)
        dsk_acc = jnp.zeros((1, LANE), F32)
        last_row = lax.broadcasted_iota(jnp.int32, (T, 1), 0) == T - 1
        for j in range(PAIRS_PER_GROUP):
            k0, k1 = 2 * j, 2 * j + 1
            xp = x_ref[:, j * LANE:(j + 1) * LANE]
            dtl = _lane_pick(lo, dt, k0)
            xdt = xp * dtl
            hp = hs_ref[0, 0, j]
            dhn = dh_ref[j]
            dyp = dy_ref[:, j * LANE:(j + 1) * LANE]
            xdt_b, hp_b, dhn_b, dyp_b = (v.astype(BF16) for v in (xdt, hp, dhn, dyp))
            cs_c, cs_r = _col_s(cs, k0), _row_s(cst, k0, first)
            lm_s = jnp.exp(jnp.where(tri_s, cs_c - cs_r, NEG))
            lmt_s = jnp.exp(jnp.where(trit_s, cs_r - cs_c, NEG))
            dec_c, ecs_c = _col_s(dec, k0), _col_s(ecs, k0)
            r1 = _dot(_stack_pair(dyp, lo), jnp.concatenate([xdt_b, hp_b], axis=0), _NT)
            r2 = _dot(_stack_pair(xdt, lo), jnp.concatenate([dyp_b, dhn_b], axis=0), _NT)
            dm_s, dyh_s = r1[:, :T], r1[:, T:]
            dmt_s, xdh_s = r2[:, :T], r2[:, T:]
            mm_s = lm_s * cb_s
            mmt_s = lmt_s * cbt_s
            bdec_s = bg_s * dec_c
            cexp_s = cg_s * ecs_c
            dx_s = _dot(jnp.concatenate([mmt_s, bdec_s], axis=1).astype(BF16),
                        jnp.concatenate([dyp_b, dhn_b], axis=0))
            dxdt = jnp.where(lo, dx_s[:T], dx_s[T:])
            dc_s = _dot((dm_s * lm_s).astype(BF16), bg_b) + dyh_s * ecs_c
            db_s = _dot((dmt_s * lmt_s).astype(BF16), cg_b) + xdh_s * dec_c
            dcg = dcg + dc_s[:T] + dc_s[T:]
            dbg = dbg + db_s[:T] + db_s[T:]
            dh_s = _dot((cgt_s * _row_s(ecst, k0, first)).astype(BF16), dyp_b)
            et = jnp.where(lo[0:1, :], etot[:, k0:k0 + 1], etot[:, k1:k1 + 1])
            dh_ref[j] = dhn * et + jnp.where(lo, dh_s[:T], dh_s[T:])
            e4 = jnp.sum(bdec_s * xdh_s, axis=-1, keepdims=True)
            dcs_s = (jnp.sum(dm_s * mm_s, axis=-1, keepdims=True) - jnp.sum(dmt_s * mmt_s, axis=-1, keepdims=True)
                     + jnp.sum(cexp_s * dyh_s, axis=-1, keepdims=True) - e4)
            hd = hp * dhn
            tsum0 = jnp.sum(e4[:T]) + etot[:, k0:k0 + 1] * jnp.sum(jnp.where(lo, hd, 0.0))
            tsum1 = jnp.sum(e4[T:]) + etot[:, k1:k1 + 1] * jnp.sum(jnp.where(lo, 0.0, hd))
            dcs0 = dcs_s[:T] + jnp.where(last_row, tsum0, 0.0)
            dcs1 = dcs_s[T:] + jnp.where(last_row, tsum1, 0.0)
            dcs_acc = jnp.where(lane == k0, dcs0, jnp.where(lane == k1, dcs1, dcs_acc))
            prod = dxdt * xp
            ddt_lo = jnp.sum(jnp.where(lo, prod, 0.0), axis=-1, keepdims=True)
            ddt_hi = jnp.sum(jnp.where(lo, 0.0, prod), axis=-1, keepdims=True)
            ddt_acc = jnp.where(lane == k0, ddt_lo, jnp.where(lane == k1, ddt_hi, ddt_acc))
            dyx = dyp * xp
            dsk_acc = jnp.where(lane1 == k0, jnp.sum(jnp.where(lo, dyx, 0.0)),
                                jnp.where(lane1 == k1, jnp.sum(jnp.where(lo, 0.0, dyx)), dsk_acc))
            dsk = jnp.where(lo[0:1, :], par_ref[2:3, k0:k0 + 1], par_ref[2:3, k1:k1 + 1])
            dx_ref[:, j * LANE:(j + 1) * LANE] = dxdt * dtl + dsk * dyp
        db_ref[...] = dbg
        dc_ref[...] = dcg
        triu = (ci >= ri).astype(F32)
        dda = _dot_hi(triu, dcs_acc)
        ddt = ddt_acc + dda * a
        ddtr = ddt * _sig(dtr_ref[...] + par_ref[0:1, :])
        ddtr_ref[...] = ddtr.astype(BF16)
        acc_ref[0:1, :] += jnp.sum(ddtr, axis=0, keepdims=True)
        acc_ref[1:2, :] += jnp.sum(dda * dt, axis=0, keepdims=True) * a
        acc_ref[2:3, :] += dsk_acc

    def rev(g, c):
        return (nc - 1 - c, g)

    return pl.pallas_call(
        body, name=name, grid=(SSD_GROUPS, nc),
        in_specs=[pl.BlockSpec((T, gw), rev),
                  pl.BlockSpec((T, SSD_STATE), rev),
                  pl.BlockSpec((T, SSD_STATE), rev),
                  pl.BlockSpec((T, LANE), lambda g, c: (nc - 1 - c, g + dtoff)),
                  pl.BlockSpec((8, LANE), lambda g, c: (0, g)),
                  pl.BlockSpec((1, 1, PAIRS_PER_GROUP, SSD_STATE, LANE), lambda g, c: (g, nc - 1 - c, 0, 0, 0)),
                  pl.BlockSpec((T, gw), rev)],
        out_specs=(pl.BlockSpec((T, gw), rev),
                   pl.BlockSpec((T, SSD_STATE), rev),
                   pl.BlockSpec((T, SSD_STATE), rev),
                   pl.BlockSpec((T, LANE), rev),
                   pl.BlockSpec((8, LANE), lambda g, c: (0, g))),
        out_shape=(jax.ShapeDtypeStruct((n_rows, SSD_D_INNER), F32),
                   jax.ShapeDtypeStruct((n_rows, BC_DIM), F32),
                   jax.ShapeDtypeStruct((n_rows, BC_DIM), F32),
                   jax.ShapeDtypeStruct((n_rows, DT_PAD), BF16),
                   jax.ShapeDtypeStruct((8, DT_PAD), F32)),
        scratch_shapes=[pltpu.VMEM((PAIRS_PER_GROUP, SSD_STATE, LANE), F32)],
        compiler_params=_cparams("parallel", "arbitrary"),
    )(xs, bm, cm, dtr, par, hs, dy)


ADAM_ROWS = 256


def _adamw(lands, w, m, v, name):
    na = len(lands)
    n_slots, r, wd = lands[0].shape
    tr = r if r <= 2 * ADAM_ROWS else ADAM_ROWS
    nj = r // tr
    bc1 = 1.0 - ADAM_B1 ** ADAM_STEP
    bc2 = 1.0 - ADAM_B2 ** ADAM_STEP

    def body(*refs):
        l_refs = refs[:na]
        w_ref, m_ref, v_ref, g_ref, d_ref, nm_ref, nv_ref = refs[na:]
        for a in range(na):
            @pl.when(pl.program_id(0) == a)
            def _(l_ref=l_refs[a]):
                g = l_ref[0].astype(F32)
                for s in range(1, n_slots):
                    g = g + l_ref[s].astype(F32)
                mn = ADAM_B1 * m_ref[0] + (1.0 - ADAM_B1) * g
                vn = ADAM_B2 * v_ref[0] + (1.0 - ADAM_B2) * (g * g)
                mh = mn / bc1
                vh = vn / bc2
                g_ref[0] = g
                nm_ref[0] = mn
                nv_ref[0] = vn
                d_ref[0] = -ADAM_LR * (mh / (jnp.sqrt(vh) + ADAM_EPS) + ADAM_WD * w_ref[0])

    def land_spec(a):
        return pl.BlockSpec((n_slots, tr, wd),
                            lambda i, j: (0, jnp.where(i == a, j, jnp.where(i < a, 0, nj - 1)), 0))

    blk = pl.BlockSpec((1, tr, wd), lambda i, j: (i, j, 0))
    shp = jax.ShapeDtypeStruct((na, r, wd), F32)
    return pl.pallas_call(
        body, name=name, grid=(na, nj), in_specs=[land_spec(a) for a in range(na)] + [blk, blk, blk],
        out_specs=(blk, blk, blk, blk), out_shape=(shp, shp, shp, shp),
        compiler_params=_cparams("arbitrary", "arbitrary"),
    )(*lands, w, m, v)


def _mesh_pos():
    return lax.axis_index("x"), lax.axis_index("y"), lax.axis_index("c")


def _peer(pos, k):
    x, y, c = pos
    px = 1 - x if (k >> 2) & 1 else x
    py = 1 - y if (k >> 1) & 1 else y
    pc = 1 - c if k & 1 else c
    return px, py, pc


def _flat(pos):
    return 4 * pos[0] + 2 * pos[1] + pos[2]


HBM_SPEC = pl.BlockSpec(memory_space=pl.ANY)


ROW_SHARDED = ("w_ssd_out", "w_att_out", "w_mix_out", "w_ffn_down")
COL_SHARDED = ("w_in", "w_ffn_gate", "w_ffn_up")
GATHERED = ROW_SHARDED + COL_SHARDED + ("conv_w",)
BIG = ROW_SHARDED + COL_SHARDED


SEM_SPEC = pl.BlockSpec(memory_space=pltpu.SEMAPHORE)
TOKEN = jax.ShapeDtypeStruct((8, LANE), F32)
SPLIT_EFFECT = pltpu.SideEffectType.DATAFLOW_SIDE_EFFECTING
GATHER_ROWS = "gather_rows"
GATHER_SLOT = "gather_slot"
SCATTER_ROWS = "scatter_rows"
SCATTER_SLOT = "scatter_slot"


def _land_shape(kind, src):
    if kind == GATHER_ROWS:
        return (N_DEV * src.shape[0],) + src.shape[1:]
    if kind == GATHER_SLOT:
        return (N_DEV,) + src.shape
    if kind == SCATTER_ROWS:
        return (N_DEV, src.shape[0] // N_DEV) + src.shape[1:]
    return src.shape


def _views(kind, src_ref, land_ref, pos, k):
    me = _flat(pos)
    if kind == GATHER_ROWS:
        r = src_ref.shape[0]
        return src_ref, land_ref.at[pl.ds(pl.multiple_of(me * r, 16), r), :]
    if kind == GATHER_SLOT:
        return src_ref, land_ref.at[me]
    dev = _flat(_peer(pos, k))
    if kind == SCATTER_ROWS:
        r = land_ref.shape[1]
        return src_ref.at[pl.ds(pl.multiple_of(dev * r, 16), r), :], land_ref.at[k]
    return src_ref.at[dev], land_ref.at[k]


def _hbm(x):
    return pltpu.with_memory_space_constraint(x, pltpu.HBM)


def _exchange_start(items, after, name):
    kinds = [k for k, _ in items]
    srcs = [_hbm(s) for _, s in items]
    lands = [_hbm(lax.empty(_land_shape(k, s), s.dtype)) for k, s in items]
    n = len(items)
    n_copy = n * (N_DEV - 1)

    def body(*refs):
        src_refs, land_refs = refs[:n], refs[n:2 * n]
        send_sems, recv_sems = refs[2 * n + 1], refs[2 * n + 2]
        token_ref = refs[4 * n + 3]
        pos = _mesh_pos()
        for i, kind in enumerate(kinds):
            for k in range(1, N_DEV):
                s, d = _views(kind, src_refs[i], land_refs[i], pos, k)
                j = i * (N_DEV - 1) + k - 1
                pltpu.make_async_remote_copy(src_ref=s, dst_ref=d, send_sem=send_sems.at[j], recv_sem=recv_sems.at[j],
                                             device_id=_peer(pos, k), device_id_type=MESH_ID).start()
        token_ref[...] = jnp.zeros_like(token_ref)

    arrs = srcs + lands
    outs = pl.pallas_call(
        body, name=name,
        in_specs=[HBM_SPEC] * (2 * n + 1),
        out_specs=[SEM_SPEC, SEM_SPEC] + [HBM_SPEC] * (2 * n) + [pl.BlockSpec(memory_space=pltpu.VMEM)],
        out_shape=[pltpu.SemaphoreType.DMA((n_copy,)), pltpu.SemaphoreType.DMA((n_copy,))]
        + [pltpu.HBM(a.shape, a.dtype) for a in arrs] + [TOKEN],
        input_output_aliases={i: 2 + i for i in range(2 * n)},
        compiler_params=pltpu.CompilerParams(has_side_effects=SPLIT_EFFECT),
    )(*arrs, after)
    return {"kinds": kinds, "send": outs[0], "recv": outs[1], "arrs": outs[2:2 + 2 * n], "token": outs[-1]}


def _exchange_wait(ex, after, name):
    kinds = ex["kinds"]
    n = len(kinds)

    def body(*refs):
        src_refs, land_refs = refs[:n], refs[n:2 * n]
        send_sems, recv_sems = refs[2 * n], refs[2 * n + 1]
        token_ref = refs[-1]
        pos = _mesh_pos()
        for i, kind in enumerate(kinds):
            for k in range(1, N_DEV):
                s, d = _views(kind, src_refs[i], land_refs[i], pos, k)
                j = i * (N_DEV - 1) + k - 1
                cp = pltpu.make_async_remote_copy(src_ref=s, dst_ref=d, send_sem=send_sems.at[j],
                                                  recv_sem=recv_sems.at[j], device_id=_peer(pos, k),
                                                  device_id_type=MESH_ID)
                cp.wait_send()
                cp.wait_recv()
        token_ref[...] = jnp.zeros_like(token_ref)

    outs = pl.pallas_call(
        body, name=name,
        in_specs=[HBM_SPEC] * (2 * n) + [SEM_SPEC, SEM_SPEC, HBM_SPEC],
        out_specs=[HBM_SPEC] * (2 * n) + [pl.BlockSpec(memory_space=pltpu.VMEM)],
        out_shape=[pltpu.HBM(a.shape, a.dtype) for a in ex["arrs"]] + [TOKEN],
        input_output_aliases={i: i for i in range(2 * n)},
        compiler_params=pltpu.CompilerParams(has_side_effects=SPLIT_EFFECT),
    )(*ex["arrs"], ex["send"], ex["recv"], after)
    lands = [_place_own(k, s, d) for k, s, d in zip(kinds, outs[:n], outs[n:2 * n])]
    return lands, outs[-1]


def _place_own(kind, src, land):
    me = _flat(_mesh_pos())
    zeros = (0,) * (src.ndim - 1)
    if kind == GATHER_ROWS:
        return lax.dynamic_update_slice(land, src, (me * src.shape[0],) + zeros)
    if kind == GATHER_SLOT:
        return lax.dynamic_update_slice(land, src[None], (me,) + (0,) * src.ndim)
    if kind == SCATTER_ROWS:
        r = land.shape[1]
        own = lax.dynamic_slice(src, (me * r,) + zeros, (r,) + src.shape[1:])
    else:
        own = lax.dynamic_index_in_dim(src, me, 0, keepdims=False)
    return lax.dynamic_update_slice(land, own[None], (0,) * land.ndim)


def _all_gather_small(x, name):
    r, w = x.shape

    def body(x_ref, out_ref, send_sems, recv_sems):
        pos = _mesh_pos()
        me = _flat(pos)
        copies = []
        for k in range(1, N_DEV):
            cp = pltpu.make_async_remote_copy(
                src_ref=x_ref, dst_ref=out_ref.at[me], send_sem=send_sems.at[k - 1], recv_sem=recv_sems.at[k - 1],
                device_id=_peer(pos, k), device_id_type=MESH_ID)
            cp.start()
            copies.append(cp)
        out_ref[me] = x_ref[...]
        for cp in copies:
            cp.wait()

    vmem = pl.BlockSpec(memory_space=pltpu.VMEM)
    return pl.pallas_call(
        body, name=name, in_specs=[vmem], out_specs=vmem,
        out_shape=jax.ShapeDtypeStruct((N_DEV, r, w), x.dtype),
        scratch_shapes=[pltpu.SemaphoreType.DMA((N_DEV - 1,)), pltpu.SemaphoreType.DMA((N_DEV - 1,))],
        compiler_params=pltpu.CompilerParams(has_side_effects=True),
    )(x)


def _cols(g, lo, hi):
    c = g.shape[-1]
    parts = []
    for d in range(N_DEV):
        a, b = max(lo, d * c), min(hi, (d + 1) * c)
        if a < b:
            parts.append(g[d, :, a - d * c:b - d * c])
    return parts[0] if len(parts) == 1 else jnp.concatenate(parts, axis=1)


def _col_chunks(g):
    c = g.shape[-1] // N_DEV
    return jnp.stack([g[:, d * c:(d + 1) * c] for d in range(N_DEV)])


IN_PART = ("w_in", "conv_w")
OUT_PART = ROW_SHARDED + ("w_ffn_gate", "w_ffn_up")


def _gather_items(w, names, l):
    items = []
    for n in names:
        blk = w[n][l] if n == "conv_w" else w[n][l].astype(BF16)
        items.append((GATHER_ROWS if n in ROW_SHARDED else GATHER_SLOT, blk))
    return items


def _scatter_items(grads, names):
    return [(SCATTER_ROWS, grads[n]) if n in ROW_SHARDED else (SCATTER_SLOT, _col_chunks(grads[n]))
            for n in names]


SMALL = ("ln_in_g", "ln_in_b", "conv_b", "dt_bias", "a_log", "d_skip", "ssd_norm_w", "att_sinks",
         "ln_mix_g", "ln_mix_b", "ln_ffn_g", "ln_ffn_b")


def _pack_small(vals):
    flat = jnp.concatenate([vals[n].reshape(-1) for n in SMALL])
    n = flat.shape[0]
    rows = -(-n // LANE)
    rows = -(-rows // 8) * 8
    return jnp.pad(flat, (0, rows * LANE - n)).reshape(rows, LANE)


def _unpack_small(buf, shapes):
    flat = buf.reshape(-1)
    off = 0
    out = {}
    for n in SMALL:
        cnt = math.prod(shapes[n])
        out[n] = flat[off:off + cnt].reshape(shapes[n])
        off += cnt
    return out


def _to_group_major(v):
    lead = v.shape[:-1]
    t = v.reshape(lead + (SSD_GROUPS, HEADS_PER_GROUP))
    t = jnp.pad(t, [(0, 0)] * len(lead) + [(0, 0), (0, LANE - HEADS_PER_GROUP)])
    return t.reshape(lead + (DT_PAD,))


def _from_group_major(v):
    lead = v.shape[:-1]
    return v.reshape(lead + (SSD_GROUPS, LANE))[..., :HEADS_PER_GROUP].reshape(lead + (SSD_HEADS,))


def _rows8(v):
    return jnp.pad(v, ((0, 8 - v.shape[0]), (0, 0)))


IN_OFFS = {"q": (0, 1024), "kv": (1024, 1280), "z": (1280, 3328), "xs": (3328, 5376), "b": (5376, 5888),
           "c": (5888, 6400), "dt": (6400, 6432), "gl": (6432, 8480)}
PIECES = ("q", "kv", "z", "xs", "b", "c", "dt", "gl")


CAT = ("q", "z", "xs", "gl", "b", "c", "kv", "dt")
CAT_WIDTH = {"q": 1024, "z": 2048, "xs": 2048, "gl": 2048, "b": 512, "c": 512, "kv": 256, "dt": DT_PAD}
CAT_OFF = {p: sum(CAT_WIDTH[q] for q in CAT[:i]) for i, p in enumerate(CAT)}
CAT_DIM = sum(CAT_WIDTH.values())


def _cat_w_in(g):
    pieces = {p: _cols(g, lo, hi) for p, (lo, hi) in IN_OFFS.items()}
    pieces["dt"] = _to_group_major(pieces["dt"])
    return jnp.concatenate([pieces[p] for p in CAT], axis=1)


def _uncat_dw_in(dw):
    pieces = {p: dw[:, CAT_OFF[p]:CAT_OFF[p] + CAT_WIDTH[p]] for p in CAT}
    pieces["dt"] = _from_group_major(pieces["dt"])
    return jnp.concatenate([pieces[p] for p in PIECES], axis=1)


def _params_out(W):
    p = {n: W[n] for n in ROW_SHARDED}
    for n in ("w_ffn_gate", "w_ffn_up"):
        p[n] = _cols(W[n], 0, FFN_HIDDEN)
    return p


def _params_in(l, W, sm):
    p = {"w_cat": _cat_w_in(W["w_in"])}
    cw = _cols(W["conv_w"], 0, SSD_D_INNER + 2 * BC_DIM)
    cb = sm["conv_b"][l]
    segs = {"xs": (0, 2048), "b": (2048, 2560), "c": (2560, 3072)}
    p["conv_w8"] = {s: _rows8(cw[:, lo:hi]) for s, (lo, hi) in segs.items()}
    p["conv_b8"] = {s: _rows8(cb[None, lo:hi]) for s, (lo, hi) in segs.items()}
    p["ssd_par"] = _rows8(jnp.stack([_to_group_major(sm["dt_bias"][l]), _to_group_major(sm["a_log"][l]),
                                     _to_group_major(sm["d_skip"][l])]))
    p["norm_w"] = sm["ssd_norm_w"][l]
    p["sinks8"] = _rows8(jnp.pad(sm["att_sinks"][l], (0, LANE - ATT_HEADS))[None])
    for n in ("ln_mix_g", "ln_mix_b", "ln_ffn_g", "ln_ffn_b"):
        p[n] = sm[n][l]
    return p


def _fwd_mixers(h0, p, l, dep=None):
    tag = f"l{l}_"
    a = {"h0": h0}
    proj = _mm(h0, p["w_cat"], "nn", tag + "proj", dep=dep)
    for pc in CAT:
        a[pc] = (proj, CAT_OFF[pc], CAT_WIDTH[pc])
    for s in ("xs", "b", "c"):
        a[s + "c"] = _conv_fwd(a[s], p["conv_w8"][s], p["conv_b8"][s], tag + "conv_" + s)
    a["y"], a["hs"] = _ssd_fwd(a["xsc"], a["bc"], a["cc"], a["dt"], p["ssd_par"], tag + "ssd_fwd")
    a["yn"] = _gnorm_fwd(a["y"], a["z"], p["norm_w"], tag + "gnorm")
    a["att"] = _att_fwd(a["q"], a["kv"], p["sinks8"], tag + "att_fwd")
    return a


def _fwd_out(a, p, l, dep=None):
    tag = f"l{l}_"
    h0 = a["h0"]
    a["ya"] = _mm(a["yn"], p["w_ssd_out"], "nn", tag + "ssd_out", dep=dep)
    a["yb"] = _mm(a["att"], p["w_att_out"], "nn", tag + "att_out", dep=dep)
    a["merged"] = _merge_fwd(a["gl"], a["ya"], a["yb"], tag + "merge")
    a["mix"] = _mm(a["merged"], p["w_mix_out"], "nn", tag + "mix_out")
    a["h1"] = _ln_fwd(h0, a["mix"], p["ln_mix_g"], p["ln_mix_b"], ALPHA, tag + "ln_mix")
    a["fg"] = _mm(a["h1"], p["w_ffn_gate"], "nn", tag + "ffn_gate")
    a["fu"] = _mm(a["h1"], p["w_ffn_up"], "nn", tag + "ffn_up")
    a["act"] = _swiglu_fwd(a["fg"], a["fu"], tag + "swiglu")
    a["ffn"] = _mm(a["act"], p["w_ffn_down"], "nn", tag + "ffn_down")
    a["h2"] = _ln_fwd(a["h1"], a["ffn"], p["ln_ffn_g"], p["ln_ffn_b"], ALPHA, tag + "ln_ffn")
    return a


def _dw(x, dy, name, dep=None):
    return _mm(x, dy, "tn", name, out_dtype=BF16, dep=dep)


def _bwd_out(a, p, dh2, l, dep=None):
    tag = f"l{l}_b_"
    gw, gs = {}, {}
    du2, acc = _ln_bwd(a["h1"], a["ffn"], p["ln_ffn_g"], dh2, ALPHA, tag + "ln_ffn")
    gs["ln_ffn_g"], gs["ln_ffn_b"] = acc[0], acc[1]
    gw["w_ffn_down"] = _dw(a["act"], du2, tag + "dw_down", dep=dep)
    dact = _mm(du2, p["w_ffn_down"], "nt", tag + "dact", dep=dep)
    dfg, dfu = _swiglu_bwd(a["fg"], a["fu"], dact, tag + "swiglu")
    gw["w_ffn_gate"] = _dw(a["h1"], dfg, tag + "dw_gate")
    gw["w_ffn_up"] = _dw(a["h1"], dfu, tag + "dw_up")
    dh1 = _mm(dfg, p["w_ffn_gate"], "nt", tag + "dh1_gate", add=du2, add_scale=ALPHA)
    dh1 = _mm(dfu, p["w_ffn_up"], "nt", tag + "dh1_up", add=dh1)
    du1, acc = _ln_bwd(a["h0"], a["mix"], p["ln_mix_g"], dh1, ALPHA, tag + "ln_mix")
    gs["ln_mix_g"], gs["ln_mix_b"] = acc[0], acc[1]
    gw["w_mix_out"] = _dw(a["merged"], du1, tag + "dw_mix")
    dmerged = _mm(du1, p["w_mix_out"], "nt", tag + "dmerged")
    dya, dyb, dgl = _merge_bwd(a["gl"], a["ya"], a["yb"], dmerged, tag + "merge")
    gw["w_ssd_out"] = _dw(a["yn"], dya, tag + "dw_ssd")
    gw["w_att_out"] = _dw(a["att"], dyb, tag + "dw_att")
    return {"du1": du1, "dya": dya, "dyb": dyb, "dgl": dgl}, gw, gs


def _bwd_mixers(a, p, carry, l, dep=None):
    tag = f"l{l}_b_"
    gs = {}
    du1, dgl = carry["du1"], carry["dgl"]
    dyn = _mm(carry["dya"], p["w_ssd_out"], "nt", tag + "dyn", dep=dep)
    datt = _mm(carry["dyb"], p["w_att_out"], "nt", tag + "datt", out_dtype=BF16, dep=dep)
    dq, dkv, acc = _att_bwd(a["q"], a["kv"], p["sinks8"], datt, tag + "att")
    gs["att_sinks"] = acc[0, :ATT_HEADS]
    dy, dz, acc = _gnorm_bwd(a["y"], a["z"], p["norm_w"], dyn, tag + "gnorm")
    gs["ssd_norm_w"] = acc[0]
    dxs, dbm, dcm, ddt, acc = _ssd_bwd(a["xsc"], a["bc"], a["cc"], a["dt"], p["ssd_par"], a["hs"], dy,
                                       tag + "ssd")
    gs["dt_bias"], gs["a_log"], gs["d_skip"] = (_from_group_major(acc[i]) for i in range(3))
    dpieces = {"q": dq, "kv": dkv, "z": dz, "dt": ddt, "gl": dgl}
    dconv_w, dconv_b = [], []
    for s, dout in (("xs", dxs), ("b", dbm), ("c", dcm)):
        dc, acc = _conv_bwd_pre(a[s], p["conv_w8"][s], p["conv_b8"][s], dout, tag + "conv_pre_" + s)
        dconv_w.append(acc[:CONV_TAPS])
        dconv_b.append(acc[CONV_TAPS])
        dpieces[s] = _conv_bwd_in(dc, p["conv_w8"][s], tag + "conv_in_" + s)
    gconv = jnp.concatenate(dconv_w, axis=1)
    gs["conv_b"] = jnp.concatenate(dconv_b)
    dproj = jnp.concatenate([dpieces[pc] for pc in CAT], axis=1)
    dw = _dw(a["h0"], dproj, tag + "dw_in")
    dh0 = _mm(dproj, p["w_cat"], "nt", tag + "dh0", add=du1, add_scale=ALPHA)
    return dh0, _uncat_dw_in(dw), gconv, gs


def _step(x, target, w, m, v):
    x2 = x[0]
    t2 = target[0]
    tok = jnp.zeros(TOKEN.shape, TOKEN.dtype)

    ex = _exchange_start(_gather_items(w, IN_PART, 0), tok, "gather_l0_in_start")
    lands, tok = _exchange_wait(ex, ex["token"], "gather_l0_in_wait")
    p0 = _params_in(0, dict(zip(IN_PART, lands)), w)
    ex = _exchange_start(_gather_items(w, OUT_PART, 0) + _gather_items(w, IN_PART, 1), tok,
                         "gather_l0_out_l1_in_start")
    h = _ln_fwd(x2, None, w["ln_in_g"], w["ln_in_b"], 1.0, "ln_in")
    a0 = _fwd_mixers(h, p0, 0, dep=ex["token"])
    lands, tok = _exchange_wait(ex, a0["att"], "gather_l0_out_l1_in_wait")
    p0.update(_params_out(dict(zip(OUT_PART, lands))))
    p1 = _params_in(1, dict(zip(IN_PART, lands[len(OUT_PART):])), w)
    ex = _exchange_start(_gather_items(w, OUT_PART, 1), tok, "gather_l1_out_start")
    a0 = _fwd_out(a0, p0, 0, dep=ex["token"])
    lands, tok = _exchange_wait(ex, a0["h2"], "gather_l1_out_wait")
    p1.update(_params_out(dict(zip(OUT_PART, lands))))
    a1 = _fwd_out(_fwd_mixers(a0["h2"], p1, 1), p1, 1)

    sse, dh = _loss_fwd_bwd(a1["h2"], t2, "loss")
    loss = lax.psum(0.5 / D_MODEL * sse[0, 0], ("x", "y", "c"))

    carry, gw1, gs1 = _bwd_out(a1, p1, dh, 1)
    dh, gw1["w_in"], gw1["conv_w"], gs = _bwd_mixers(a1, p1, carry, 1)
    gs1.update(gs)
    ex1 = _exchange_start(_scatter_items(gw1, GATHERED), tok, "scatter_l1_start")
    carry, gw0, gs0 = _bwd_out(a0, p0, dh, 0, dep=ex1["token"])
    lands, tok = _exchange_wait(ex1, carry["dgl"], "scatter_l1_wait")
    land1 = dict(zip(GATHERED, lands))
    ex0 = _exchange_start(_scatter_items(gw0, OUT_PART), tok, "scatter_l0_out_start")
    dh, gw0["w_in"], gw0["conv_w"], gs = _bwd_mixers(a0, p0, carry, 0, dep=ex0["token"])
    gs0.update(gs)
    lands, tok = _exchange_wait(ex0, dh, "scatter_l0_out_wait")
    land0 = dict(zip(OUT_PART, lands))
    ex0 = _exchange_start(_scatter_items(gw0, IN_PART), tok, "scatter_l0_in_start")
    grad_x2, acc = _ln_bwd(x2, None, w["ln_in_g"], dh, 1.0, "ln_in_b")

    outs = [{} for _ in range(4)]

    def update(names):
        res = None
        for n in names:
            res = _adamw([land0[n], land1[n]], w[n], m[n], v[n], "adamw_" + n)
            for o, t in zip(outs, res):
                o[n] = t
        return res[1]

    update(OUT_PART)
    gsm = {"ln_in_g": acc[0], "ln_in_b": acc[1]}
    for n in SMALL[2:]:
        gsm[n] = jnp.stack([gs0[n], gs1[n]])
    small_shapes = {n: w[n].shape for n in SMALL}
    land_s = _all_gather_small(_pack_small(gsm), "small_grads_all_gather")
    res = _adamw([land_s], _pack_small(w)[None], _pack_small(m)[None], _pack_small(v)[None], "adamw_small")
    for o, t in zip(outs, res):
        o.update(_unpack_small(t[0], small_shapes))
    lands, _ = _exchange_wait(ex0, res[1], "scatter_l0_in_wait")
    land0.update(zip(IN_PART, lands))
    update(IN_PART)
    return loss, grad_x2[None], outs


WEIGHT_NAMES = ("ln_in_g", "ln_in_b", "w_in", "conv_w", "conv_b", "dt_bias", "a_log", "d_skip", "ssd_norm_w",
                "att_sinks", "w_ssd_out", "w_att_out", "w_mix_out", "ln_mix_g", "ln_mix_b", "w_ffn_gate",
                "w_ffn_up", "w_ffn_down", "ln_ffn_g", "ln_ffn_b")


def kernel(x, ln_in_g, ln_in_b, w_in, conv_w, conv_b, dt_bias, a_log, d_skip, ssd_norm_w, att_sinks, w_ssd_out, w_att_out, w_mix_out, ln_mix_g, ln_mix_b, w_ffn_gate, w_ffn_up, w_ffn_down, ln_ffn_g, ln_ffn_b, loss_target, m_ln_in_g, m_ln_in_b, m_w_in, m_conv_w, m_conv_b, m_dt_bias, m_a_log, m_d_skip, m_ssd_norm_w, m_att_sinks, m_w_ssd_out, m_w_att_out, m_w_mix_out, m_ln_mix_g, m_ln_mix_b, m_w_ffn_gate, m_w_ffn_up, m_w_ffn_down, m_ln_ffn_g, m_ln_ffn_b, v_ln_in_g, v_ln_in_b, v_w_in, v_conv_w, v_conv_b, v_dt_bias, v_a_log, v_d_skip, v_ssd_norm_w, v_att_sinks, v_w_ssd_out, v_w_att_out, v_w_mix_out, v_ln_mix_g, v_ln_mix_b, v_w_ffn_gate, v_w_ffn_up, v_w_ffn_down, v_ln_ffn_g, v_ln_ffn_b):
    w = dict(zip(WEIGHT_NAMES, (ln_in_g, ln_in_b, w_in, conv_w, conv_b, dt_bias, a_log, d_skip, ssd_norm_w,
                                att_sinks, w_ssd_out, w_att_out, w_mix_out, ln_mix_g, ln_mix_b, w_ffn_gate,
                                w_ffn_up, w_ffn_down, ln_ffn_g, ln_ffn_b)))
    m = dict(zip(WEIGHT_NAMES, (m_ln_in_g, m_ln_in_b, m_w_in, m_conv_w, m_conv_b, m_dt_bias, m_a_log, m_d_skip,
                                m_ssd_norm_w, m_att_sinks, m_w_ssd_out, m_w_att_out, m_w_mix_out, m_ln_mix_g,
                                m_ln_mix_b, m_w_ffn_gate, m_w_ffn_up, m_w_ffn_down, m_ln_ffn_g, m_ln_ffn_b)))
    v = dict(zip(WEIGHT_NAMES, (v_ln_in_g, v_ln_in_b, v_w_in, v_conv_w, v_conv_b, v_dt_bias, v_a_log, v_d_skip,
                                v_ssd_norm_w, v_att_sinks, v_w_ssd_out, v_w_att_out, v_w_mix_out, v_ln_mix_g,
                                v_ln_mix_b, v_w_ffn_gate, v_w_ffn_up, v_w_ffn_down, v_ln_ffn_g, v_ln_ffn_b)))
    loss, grad_x, outs = _step(x, loss_target, w, m, v)
    result = [loss, grad_x]
    for o in outs:
        result.extend(o[n] for n in WEIGHT_NAMES)
    return tuple(result)
```

```python
import functools
import math

import jax
import jax.numpy as jnp
from jax import lax
from jax.experimental import pallas as pl
from jax.experimental.pallas import tpu as pltpu

F32 = jnp.float32
BF16 = jnp.bfloat16

D_MODEL = 1024
DEPTH = 2
N_DEV = 8
ATT_HEADS = 16
ATT_KV_HEADS = 2
ATT_HEAD_DIM = 64
ATT_BLOCK = 128
SSD_D_INNER = 2048
SSD_HEADS = 32
SSD_GROUPS = 4
SSD_STATE = 128
SSD_CHUNK = 128
FFN_HIDDEN = 2816
LN_EPS = 1e-5
RMS_EPS = 1e-5
ALPHA = (2 * DEPTH) ** 0.25
Q_DIM = 1024
KV_DIM = 128
BC_DIM = 512
IN_DIM = 8480
IN_SHARD = IN_DIM // N_DEV
DT_PAD = 512

ADAM_LR = 0.001
ADAM_B1 = 0.9
ADAM_B2 = 0.999
ADAM_EPS = 1e-08
ADAM_WD = 0.01
ADAM_STEP = 10

LANE = 128
VMEM_LIMIT = 48 * 1024 * 1024
PACK_W = 1024
NEG = -1e30

_NN = (((1,), (0,)), ((), ()))
_NT = (((1,), (1,)), ((), ()))
_TN = (((0,), (0,)), ((), ()))
MESH_ID = pl.DeviceIdType.MESH


def _dot(a, b, dims=_NN):
    return lax.dot_general(a, b, dims, preferred_element_type=F32)


def _dot_hi(a, b):
    return lax.dot_general(a, b, _NN, preferred_element_type=F32, precision=lax.Precision.HIGHEST)


def _sig(x):
    return 1.0 / (1.0 + jnp.exp(-x))


def _softplus(x):
    return jnp.maximum(x, 0.0) + jnp.log(1.0 + jnp.exp(-jnp.abs(x)))


def _cparams(*sem):
    return pltpu.CompilerParams(dimension_semantics=sem, vmem_limit_bytes=VMEM_LIMIT)


def _pick(n, cap):
    if n <= cap:
        return n
    best = None
    for t in range(LANE, cap + 1, LANE):
        if n % t == 0:
            best = t
    assert best is not None, (n, cap)
    return best


def _tile(n):
    if n <= 1024 or n % 1024 == 0:
        return min(n, 1024)
    return _pick(n, 1408)


def _rows(n):
    return min(512, n)


def _window(x):
    return x if isinstance(x, tuple) else (x, 0, x.shape[1])


def _into(into, n_in, out_idx):
    buf, col0, width = into
    if buf is None:
        return [], [], {}, col0, width
    return [buf], [pl.BlockSpec(memory_space=pl.ANY)], {n_in: out_idx}, col0, width


def _mm(a, b, mode, name, add=None, add_scale=1.0, out_dtype=F32, dep=None):
    if mode == "nn":
        m, k = a.shape
        n = b.shape[1]
    elif mode == "nt":
        m, k = a.shape
        n = b.shape[0]
    else:
        k, m = a.shape
        n = b.shape[1]
    tm = _tile(m)
    tn = _pick(n, 2176) if mode == "tn" and n > 1024 else _tile(n)
    tk = _tile(k)
    nk = k // tk
    has_add = add is not None
    dims = {"nn": _NN, "nt": _NT, "tn": _TN}[mode]

    def body(*refs):
        if dep is not None:
            refs = refs[:-3] + refs[-2:]
        if has_add:
            a_ref, b_ref, add_ref, o_ref, acc_ref = refs
        else:
            a_ref, b_ref, o_ref, acc_ref = refs
        kk = pl.program_id(2)

        @pl.when(kk == 0)
        def _():
            if has_add:
                acc_ref[...] = add_scale * add_ref[...].astype(F32)
            else:
                acc_ref[...] = jnp.zeros_like(acc_ref)

        acc_ref[...] += _dot(a_ref[...].astype(BF16), b_ref[...].astype(BF16), dims)

        @pl.when(kk == nk - 1)
        def _():
            o_ref[...] = acc_ref[...].astype(o_ref.dtype)

    if mode == "nn":
        a_spec = pl.BlockSpec((tm, tk), lambda i, j, kk: (i, kk))
        b_spec = pl.BlockSpec((tk, tn), lambda i, j, kk: (kk, j))
    elif mode == "nt":
        a_spec = pl.BlockSpec((tm, tk), lambda i, j, kk: (i, kk))
        b_spec = pl.BlockSpec((tn, tk), lambda i, j, kk: (j, kk))
    else:
        a_spec = pl.BlockSpec((tk, tm), lambda i, j, kk: (kk, i))
        b_spec = pl.BlockSpec((tk, tn), lambda i, j, kk: (kk, j))
    o_spec = pl.BlockSpec((tm, tn), lambda i, j, kk: (i, j))
    in_specs = [a_spec, b_spec] + ([o_spec] if has_add else [])
    args = (a, b) + ((add,) if has_add else ())
    if dep is not None:
        in_specs.append(pl.BlockSpec((8, LANE), lambda i, j, kk: (0, 0)))
        args += (dep,)
    return pl.pallas_call(
        body, name=name, grid=(m // tm, n // tn, nk),
        in_specs=in_specs, out_specs=o_spec,
        out_shape=jax.ShapeDtypeStruct((m, n), out_dtype),
        scratch_shapes=[pltpu.VMEM((tm, tn), F32)],
        compiler_params=_cparams("parallel", "parallel", "arbitrary"),
    )(*args)


def _vec_spec(width):
    return pl.BlockSpec((1, width), lambda i: (0, 0))


def _ln_fwd(a, b, gamma, beta, alpha, name):
    n_rows, dm = a.shape
    has_b = b is not None

    def body(*refs):
        if has_b:
            a_ref, b_ref, g_ref, be_ref, o_ref = refs
            u = alpha * a_ref[...] + b_ref[...]
        else:
            a_ref, g_ref, be_ref, o_ref = refs
            u = a_ref[...]
        mu = jnp.mean(u, axis=-1, keepdims=True)
        d = u - mu
        var = jnp.mean(d * d, axis=-1, keepdims=True)
        o_ref[...] = d * lax.rsqrt(var + LN_EPS) * g_ref[...] + be_ref[...]

    row = pl.BlockSpec((_rows(n_rows),dm), lambda i: (i, 0))
    in_specs = [row] + ([row] if has_b else []) + [_vec_spec(dm), _vec_spec(dm)]
    args = (a,) + ((b,) if has_b else ()) + (gamma.reshape(1, dm), beta.reshape(1, dm))
    return pl.pallas_call(
        body, name=name, grid=(n_rows // _rows(n_rows),), in_specs=in_specs, out_specs=row,
        out_shape=jax.ShapeDtypeStruct((n_rows, dm), F32),
        compiler_params=_cparams("parallel"),
    )(*args)


def _ln_bwd(a, b, gamma, dy, alpha, name):
    n_rows, dm = a.shape
    has_b = b is not None

    def body(*refs):
        if has_b:
            a_ref, b_ref, g_ref, dy_ref, du_ref, acc_ref = refs
            u = alpha * a_ref[...] + b_ref[...]
        else:
            a_ref, g_ref, dy_ref, du_ref, acc_ref = refs
            u = a_ref[...]

        @pl.when(pl.program_id(0) == 0)
        def _():
            acc_ref[...] = jnp.zeros_like(acc_ref)

        mu = jnp.mean(u, axis=-1, keepdims=True)
        d = u - mu
        var = jnp.mean(d * d, axis=-1, keepdims=True)
        rstd = lax.rsqrt(var + LN_EPS)
        xhat = d * rstd
        dyv = dy_ref[...]
        acc_ref[0:1, :] += jnp.sum(dyv * xhat, axis=0, keepdims=True)
        acc_ref[1:2, :] += jnp.sum(dyv, axis=0, keepdims=True)
        dxh = dyv * g_ref[...]
        m1 = jnp.mean(dxh, axis=-1, keepdims=True)
        m2 = jnp.mean(dxh * xhat, axis=-1, keepdims=True)
        du_ref[...] = rstd * (dxh - m1 - xhat * m2)

    row = pl.BlockSpec((_rows(n_rows),dm), lambda i: (i, 0))
    in_specs = [row] + ([row] if has_b else []) + [_vec_spec(dm), row]
    args = (a,) + ((b,) if has_b else ()) + (gamma.reshape(1, dm), dy)
    return pl.pallas_call(
        body, name=name, grid=(n_rows // _rows(n_rows),), in_specs=in_specs,
        out_specs=(row, pl.BlockSpec((8, dm), lambda i: (0, 0))),
        out_shape=(jax.ShapeDtypeStruct((n_rows, dm), F32), jax.ShapeDtypeStruct((8, dm), F32)),
        compiler_params=_cparams("arbitrary"),
    )(*args)


def _loss_fwd_bwd(y, target, name):
    n_rows, dm = y.shape

    def body(y_ref, t_ref, acc_ref, dy_ref):
        @pl.when(pl.program_id(0) == 0)
        def _():
            acc_ref[...] = jnp.zeros_like(acc_ref)

        d = y_ref[...] - t_ref[...]
        acc_ref[...] += jnp.sum(d * d)
        dy_ref[...] = d * (1.0 / dm)

    row = pl.BlockSpec((_rows(n_rows),dm), lambda i: (i, 0))
    return pl.pallas_call(
        body, name=name, grid=(n_rows // _rows(n_rows),), in_specs=[row, row],
        out_specs=(pl.BlockSpec((8, LANE), lambda i: (0, 0)), row),
        out_shape=(jax.ShapeDtypeStruct((8, LANE), F32), jax.ShapeDtypeStruct((n_rows, dm), F32)),
        compiler_params=_cparams("arbitrary"),
    )(y, target)


def _swiglu_fwd(g, u, name):
    n_rows, w = g.shape
    tw = _pick(w, 1408)

    def body(g_ref, u_ref, o_ref):
        gv = g_ref[...]
        o_ref[...] = (gv * _sig(gv) * u_ref[...]).astype(BF16)

    blk = pl.BlockSpec((_rows(n_rows),tw), lambda i, j: (i, j))
    return pl.pallas_call(
        body, name=name, grid=(n_rows // _rows(n_rows), w // tw), in_specs=[blk, blk], out_specs=blk,
        out_shape=jax.ShapeDtypeStruct((n_rows, w), BF16),
        compiler_params=_cparams("parallel", "parallel"),
    )(g, u)


def _swiglu_bwd(g, u, dact, name):
    n_rows, w = g.shape
    tw = _pick(w, 1408)

    def body(g_ref, u_ref, da_ref, dg_ref, du_ref):
        gv = g_ref[...]
        s = _sig(gv)
        da = da_ref[...]
        dg_ref[...] = (da * u_ref[...] * (s * (1.0 + gv * (1.0 - s)))).astype(BF16)
        du_ref[...] = (da * gv * s).astype(BF16)

    blk = pl.BlockSpec((_rows(n_rows),tw), lambda i, j: (i, j))
    return pl.pallas_call(
        body, name=name, grid=(n_rows // _rows(n_rows), w // tw), in_specs=[blk, blk, blk], out_specs=(blk, blk),
        out_shape=(jax.ShapeDtypeStruct((n_rows, w), BF16), jax.ShapeDtypeStruct((n_rows, w), BF16)),
        compiler_params=_cparams("parallel", "parallel"),
    )(g, u, dact)


def _gate_specs(gl, n_rows, dm):
    arr, g0, _ = _window(gl)
    return arr, [pl.BlockSpec((_rows(n_rows), dm), lambda i, k=k: (i, g0 // dm + k)) for k in range(2)]


def _merge_fwd(gl, ya, yb, name):
    n_rows, dm = ya.shape
    gl_arr, gspecs = _gate_specs(gl, n_rows, dm)

    def body(ga_ref, gb_ref, ya_ref, yb_ref, o_ref):
        o_ref[...] = (_sig(ga_ref[...]) * ya_ref[...] + _sig(gb_ref[...]) * yb_ref[...]).astype(BF16)

    row = pl.BlockSpec((_rows(n_rows),dm), lambda i: (i, 0))
    return pl.pallas_call(
        body, name=name, grid=(n_rows // _rows(n_rows),), in_specs=gspecs + [row, row], out_specs=row,
        out_shape=jax.ShapeDtypeStruct((n_rows, dm), BF16),
        compiler_params=_cparams("parallel"),
    )(gl_arr, gl_arr, ya, yb)


def _merge_bwd(gl, ya, yb, dmerged, name, into):
    n_rows, dm = ya.shape
    gl_arr, gspecs = _gate_specs(gl, n_rows, dm)
    extra, extra_specs, aliases, col0, width = _into(into, 5, 2)

    def body(*refs):
        ga_ref, gb_ref, ya_ref, yb_ref, dm_ref = refs[:5]
        dya_ref, dyb_ref, dgl_ref = refs[-3:]
        ga = _sig(ga_ref[...])
        gb = _sig(gb_ref[...])
        dmv = dm_ref[...]
        dya_ref[...] = (dmv * ga).astype(BF16)
        dyb_ref[...] = (dmv * gb).astype(BF16)
        dgl_ref[:, :dm] = (dmv * ya_ref[...] * ga * (1.0 - ga)).astype(BF16)
        dgl_ref[:, dm:] = (dmv * yb_ref[...] * gb * (1.0 - gb)).astype(BF16)

    row = pl.BlockSpec((_rows(n_rows),dm), lambda i: (i, 0))
    row2 = pl.BlockSpec((_rows(n_rows),2 * dm), lambda i: (i, col0 // (2 * dm)))
    return pl.pallas_call(
        body, name=name, grid=(n_rows // _rows(n_rows),), in_specs=gspecs + [row, row, row] + extra_specs,
        out_specs=(row, row, row2),
        out_shape=(jax.ShapeDtypeStruct((n_rows, dm), BF16), jax.ShapeDtypeStruct((n_rows, dm), BF16),
                   jax.ShapeDtypeStruct((n_rows, width), BF16)),
        input_output_aliases=aliases,
        compiler_params=_cparams("parallel"),
    )(gl_arr, gl_arr, ya, yb, dmerged, *extra)


CONV_TAPS = 4
CONV_COLS = 512
HALO = 8


def _shift_down(cur, prev8, s, row8):
    r = pltpu.roll(cur, s, axis=0)
    top = jnp.where(row8 < s, pltpu.roll(prev8, s, axis=0), r[0:HALO])
    return jnp.concatenate([top, r[HALO:]], axis=0)


def _shift_up(cur, next8, s, row8):
    n = cur.shape[0]
    r = pltpu.roll(cur, n - s, axis=0)
    bot = jnp.where(row8 >= HALO - s, pltpu.roll(next8, HALO - s, axis=0), r[n - HALO:])
    return jnp.concatenate([r[:n - HALO], bot], axis=0)


def _conv_pre(u_ref, prev_ref, w_ref, b_ref, li):
    cur = u_ref[...]
    prev8 = jnp.where(li == 0, 0.0, prev_ref[...])
    row8 = lax.broadcasted_iota(jnp.int32, prev8.shape, 0)
    shifted = [cur] + [_shift_down(cur, prev8, s, row8) for s in range(1, CONV_TAPS)]
    acc = b_ref[...] + shifted[0] * w_ref[CONV_TAPS - 1:CONV_TAPS, :]
    for s in range(1, CONV_TAPS):
        acc = acc + shifted[s] * w_ref[CONV_TAPS - 1 - s:CONV_TAPS - s, :]
    return acc, shifted


def _conv_specs(n_rows, tl, col0=0):
    off = col0 // CONV_COLS
    cur = pl.BlockSpec((tl, CONV_COLS), lambda cj, li: (li, cj + off))
    prev = pl.BlockSpec((HALO, CONV_COLS), lambda cj, li: (jnp.maximum(li * (tl // HALO) - 1, 0), cj + off))
    nxt = pl.BlockSpec((HALO, CONV_COLS),
                       lambda cj, li: (jnp.minimum((li + 1) * (tl // HALO), n_rows // HALO - 1), cj + off))
    par = pl.BlockSpec((8, CONV_COLS), lambda cj, li: (0, cj + off))
    return cur, prev, nxt, par


def _conv_fwd(u, w8, b8, name):
    u, u0, c = _window(u)
    n_rows = u.shape[0]
    tl = _rows(n_rows)
    cur, _, _, par = _conv_specs(n_rows, tl)
    ucur, prev, _, _ = _conv_specs(n_rows, tl, u0)

    def body(u_ref, prev_ref, w_ref, b_ref, o_ref):
        acc, _ = _conv_pre(u_ref, prev_ref, w_ref, b_ref[0:1, :], pl.program_id(1))
        o_ref[...] = acc * _sig(acc)

    return pl.pallas_call(
        body, name=name, grid=(c // CONV_COLS, n_rows // tl), in_specs=[ucur, prev, par, par], out_specs=cur,
        out_shape=jax.ShapeDtypeStruct((n_rows, c), F32),
        compiler_params=_cparams("parallel", "parallel"),
    )(u, u, w8, b8)


def _conv_bwd_pre(u, w8, b8, dout, name):
    u, u0, c = _window(u)
    n_rows = u.shape[0]
    tl = _rows(n_rows)
    cur, _, _, par = _conv_specs(n_rows, tl)
    ucur, prev, _, _ = _conv_specs(n_rows, tl, u0)

    def body(u_ref, prev_ref, w_ref, b_ref, do_ref, dc_ref, acc_ref):
        @pl.when(pl.program_id(1) == 0)
        def _():
            acc_ref[...] = jnp.zeros_like(acc_ref)

        acc, shifted = _conv_pre(u_ref, prev_ref, w_ref, b_ref[0:1, :], pl.program_id(1))
        sg = _sig(acc)
        dc = do_ref[...] * (sg * (1.0 + acc * (1.0 - sg)))
        dc_ref[...] = dc
        for k in range(CONV_TAPS):
            acc_ref[k:k + 1, :] += jnp.sum(dc * shifted[CONV_TAPS - 1 - k], axis=0, keepdims=True)
        acc_ref[CONV_TAPS:CONV_TAPS + 1, :] += jnp.sum(dc, axis=0, keepdims=True)

    return pl.pallas_call(
        body, name=name, grid=(c // CONV_COLS, n_rows // tl), in_specs=[ucur, prev, par, par, cur],
        out_specs=(cur, par),
        out_shape=(jax.ShapeDtypeStruct((n_rows, c), F32), jax.ShapeDtypeStruct((8, c), F32)),
        compiler_params=_cparams("parallel", "arbitrary"),
    )(u, u, w8, b8, dout)


def _conv_bwd_in(dc, w8, name, into):
    n_rows, c = dc.shape
    tl = _rows(n_rows)
    cur, _, nxt, par = _conv_specs(n_rows, tl)
    n_l = n_rows // tl
    extra, extra_specs, aliases, col0, width = _into(into, 3, 0)
    out_spec = _conv_specs(n_rows, tl, col0)[0]

    def body(*refs):
        dc_ref, next_ref, w_ref = refs[:3]
        o_ref = refs[-1]
        cur_v = dc_ref[...]
        next8 = jnp.where(pl.program_id(1) == n_l - 1, 0.0, next_ref[...])
        row8 = lax.broadcasted_iota(jnp.int32, next8.shape, 0)
        acc = cur_v * w_ref[CONV_TAPS - 1:CONV_TAPS, :]
        for s in range(1, CONV_TAPS):
            acc = acc + _shift_up(cur_v, next8, s, row8) * w_ref[CONV_TAPS - 1 - s:CONV_TAPS - s, :]
        o_ref[...] = acc.astype(BF16)

    return pl.pallas_call(
        body, name=name, grid=(c // CONV_COLS, n_l), in_specs=[cur, nxt, par] + extra_specs, out_specs=out_spec,
        out_shape=jax.ShapeDtypeStruct((n_rows, width), BF16), input_output_aliases=aliases,
        compiler_params=_cparams("parallel", "parallel"),
    )(dc, dc, w8, *extra)


NORM_GROUP = SSD_D_INNER // SSD_GROUPS


def _gnorm_fwd(y, z, w, name):
    n_rows, c = y.shape
    z, z0, _ = _window(z)
    zoff = z0 // NORM_GROUP

    def body(y_ref, z_ref, w_ref, o_ref):
        zv = z_ref[...]
        yg = y_ref[...] * (zv * _sig(zv))
        r = lax.rsqrt(jnp.mean(yg * yg, axis=-1, keepdims=True) + RMS_EPS)
        o_ref[...] = (yg * r * w_ref[...]).astype(BF16)

    blk = pl.BlockSpec((_rows(n_rows),NORM_GROUP), lambda i, j: (i, j))
    zblk = pl.BlockSpec((_rows(n_rows),NORM_GROUP), lambda i, j: (i, j + zoff))
    wspec = pl.BlockSpec((1, NORM_GROUP), lambda i, j: (0, j))
    return pl.pallas_call(
        body, name=name, grid=(n_rows // _rows(n_rows), c // NORM_GROUP), in_specs=[blk, zblk, wspec], out_specs=blk,
        out_shape=jax.ShapeDtypeStruct((n_rows, c), BF16),
        compiler_params=_cparams("parallel", "parallel"),
    )(y, z, w.reshape(1, c))


def _gnorm_bwd(y, z, w, dyn, name, into):
    n_rows, c = y.shape
    z, z0, _ = _window(z)
    zoff = z0 // NORM_GROUP
    extra, extra_specs, aliases, col0, width = _into(into, 4, 1)
    doff = col0 // NORM_GROUP

    def body(*refs):
        y_ref, z_ref, w_ref, dn_ref = refs[:4]
        dy_ref, dz_ref, acc_ref = refs[-3:]
        @pl.when(pl.program_id(1) == 0)
        def _():
            acc_ref[...] = jnp.zeros_like(acc_ref)

        zv = z_ref[...]
        yv = y_ref[...]
        sz = _sig(zv)
        silu = zv * sz
        yg = yv * silu
        r = lax.rsqrt(jnp.mean(yg * yg, axis=-1, keepdims=True) + RMS_EPS)
        nrm = yg * r
        dn = dn_ref[...]
        acc_ref[0:1, :] += jnp.sum(dn * nrm, axis=0, keepdims=True)
        dnw = dn * w_ref[...]
        dyg = r * (dnw - nrm * jnp.mean(dnw * nrm, axis=-1, keepdims=True))
        dy_ref[...] = dyg * silu
        dz_ref[...] = (dyg * yv * (sz * (1.0 + zv * (1.0 - sz)))).astype(BF16)

    blk = pl.BlockSpec((_rows(n_rows),NORM_GROUP), lambda j, i: (i, j))
    zblk = pl.BlockSpec((_rows(n_rows),NORM_GROUP), lambda j, i: (i, j + zoff))
    wspec = pl.BlockSpec((1, NORM_GROUP), lambda j, i: (0, j))
    aspec = pl.BlockSpec((8, NORM_GROUP), lambda j, i: (0, j))
    return pl.pallas_call(
        body, name=name, grid=(c // NORM_GROUP, n_rows // _rows(n_rows)),
        in_specs=[blk, zblk, wspec, blk] + extra_specs,
        out_specs=(blk, pl.BlockSpec((_rows(n_rows), NORM_GROUP), lambda j, i: (i, j + doff)), aspec),
        out_shape=(jax.ShapeDtypeStruct((n_rows, c), F32), jax.ShapeDtypeStruct((n_rows, width), BF16),
                   jax.ShapeDtypeStruct((8, c), F32)),
        input_output_aliases=aliases,
        compiler_params=_cparams("parallel", "arbitrary"),
    )(y, z, w.reshape(1, c), dyn, *extra)


ATT_SCALE = ATT_HEAD_DIM ** -0.5
ATT_SLOPES = [2.0 ** (-8.0 * (h + 1) / ATT_HEADS) for h in range(ATT_HEADS)]
Q_PER_KV = ATT_HEADS // ATT_KV_HEADS


def _dup_half(t, g, lo):
    tr = pltpu.roll(t, ATT_HEAD_DIM, axis=1)
    return jnp.where(lo, t, tr) if g == 0 else jnp.where(lo, tr, t)


def _att_band(kv_ref, kvp_ref, n):
    cur = kv_ref[...]
    prev = jnp.where(n == 0, 0.0, kvp_ref[...])
    lo = lax.broadcasted_iota(jnp.int32, (ATT_BLOCK, LANE), 1) < ATT_HEAD_DIM
    bands = []
    for g in range(ATT_KV_HEADS):
        kb = jnp.concatenate([_dup_half(prev[:, :LANE], g, lo), _dup_half(cur[:, :LANE], g, lo)], axis=0)
        vb = jnp.concatenate([_dup_half(prev[:, LANE:], g, lo), _dup_half(cur[:, LANE:], g, lo)], axis=0)
        bands.append((kb.astype(BF16), vb.astype(BF16)))
    return bands


def _att_tile(n):
    shape = (2 * ATT_BLOCK, ATT_BLOCK)
    row = lax.broadcasted_iota(jnp.int32, shape, 0)
    i = row & (ATT_BLOCK - 1)
    s = lax.broadcasted_iota(jnp.int32, shape, 1)
    upper = s > i
    dist = ((i - s) & (ATT_BLOCK - 1)).astype(F32)
    dead = upper & (n == 0)
    return upper, dist, dead, row[:, 0:1] < ATT_BLOCK


def _stack_pair(t, lo):
    return jnp.concatenate([jnp.where(lo, t, 0.0), jnp.where(lo, 0.0, t)], axis=0).astype(BF16)


def _att_probs(qs, kb, s_ref, j, tile):
    upper, dist, dead, first = tile
    s2 = _dot(qs, kb, _NT)
    slope = jnp.where(first, ATT_SLOPES[2 * j], ATT_SLOPES[2 * j + 1])
    sink = jnp.where(first, s_ref[0:1, 2 * j:2 * j + 1], s_ref[0:1, 2 * j + 1:2 * j + 2])
    s = jnp.where(upper, s2[:, :ATT_BLOCK], s2[:, ATT_BLOCK:]) - slope * dist
    s = jnp.where(dead, NEG, s)
    m = jnp.maximum(jnp.max(s, axis=-1, keepdims=True), sink)
    p = jnp.exp(s - m)
    es = jnp.exp(sink - m)
    inv = 1.0 / (jnp.sum(p, axis=-1, keepdims=True) + es)
    return p * inv, es * inv


def _band_split(t, upper):
    return jnp.concatenate([jnp.where(upper, t, 0.0), jnp.where(upper, 0.0, t)], axis=1)


def _att_fwd(q, kv, sinks8, name):
    q, q0, _ = _window(q)
    kv, kv0, _ = _window(kv)
    qoff, kvoff = q0 // Q_DIM, kv0 // (2 * LANE)
    n_rows = q.shape[0]
    nb = n_rows // ATT_BLOCK

    def body(q_ref, kv_ref, kvp_ref, s_ref, o_ref):
        n = pl.program_id(0)
        bands = _att_band(kv_ref, kvp_ref, n)
        lo = lax.broadcasted_iota(jnp.int32, (ATT_BLOCK, LANE), 1) < ATT_HEAD_DIM
        tile = _att_tile(n)
        for j in range(ATT_HEADS // 2):
            kb, vb = bands[2 * j // Q_PER_KV]
            qs = _stack_pair(q_ref[:, j * LANE:(j + 1) * LANE] * ATT_SCALE, lo)
            p, _ = _att_probs(qs, kb, s_ref, j, tile)
            out = _dot(_band_split(p, tile[0]).astype(BF16), vb)
            o_ref[:, j * LANE:(j + 1) * LANE] = jnp.where(lo, out[:ATT_BLOCK], out[ATT_BLOCK:]).astype(BF16)

    return pl.pallas_call(
        body, name=name, grid=(nb,),
        in_specs=[pl.BlockSpec((ATT_BLOCK, Q_DIM), lambda n: (n, qoff)),
                  pl.BlockSpec((ATT_BLOCK, 2 * LANE), lambda n: (n, kvoff)),
                  pl.BlockSpec((ATT_BLOCK, 2 * LANE), lambda n: (jnp.maximum(n - 1, 0), kvoff)),
                  pl.BlockSpec((8, LANE), lambda n: (0, 0))],
        out_specs=pl.BlockSpec((ATT_BLOCK, Q_DIM), lambda n: (n, 0)),
        out_shape=jax.ShapeDtypeStruct((n_rows, Q_DIM), BF16),
        compiler_params=_cparams("parallel"),
    )(q, kv, kv, sinks8)


def _att_bwd(q, kv, sinks8, dout, name, into):
    q, q0, _ = _window(q)
    kv, kv0, _ = _window(kv)
    qoff, kvoff = q0 // Q_DIM, kv0 // (2 * LANE)
    n_rows = q.shape[0]
    nb = n_rows // ATT_BLOCK

    extra, extra_specs, aliases, col0, width = _into(into, 5, 0)
    dqoff = col0 // Q_DIM

    def body(*refs):
        q_ref, kv_ref, kvp_ref, s_ref, do_ref = refs[:5]
        dq_ref, dkv_ref, acc_ref, carry_ref = refs[-4:]
        n = pl.program_id(0)

        @pl.when(n == 0)
        def _():
            acc_ref[...] = jnp.zeros_like(acc_ref)
            carry_ref[...] = jnp.zeros_like(carry_ref)

        @pl.when(n == nb)
        def _():
            dkv_ref[...] = carry_ref[...].astype(BF16)

        @pl.when(n < nb)
        def _():
            bands = _att_band(kv_ref, kvp_ref, n)
            lo = lax.broadcasted_iota(jnp.int32, (ATT_BLOCK, LANE), 1) < ATT_HEAD_DIM
            lane1 = lax.broadcasted_iota(jnp.int32, (1, LANE), 1)
            tile = _att_tile(n)
            upper, first = tile[0], tile[3]
            dk_acc = [jnp.zeros((2 * ATT_BLOCK, LANE), F32) for _ in range(ATT_KV_HEADS)]
            dv_acc = [jnp.zeros((2 * ATT_BLOCK, LANE), F32) for _ in range(ATT_KV_HEADS)]
            dsink = jnp.zeros((1, LANE), F32)
            for j in range(ATT_HEADS // 2):
                g = 2 * j // Q_PER_KV
                kb, vb = bands[g]
                qs = _stack_pair(q_ref[:, j * LANE:(j + 1) * LANE] * ATT_SCALE, lo)
                dos = _stack_pair(do_ref[:, j * LANE:(j + 1) * LANE].astype(F32), lo)
                p, ps = _att_probs(qs, kb, s_ref, j, tile)
                dp2 = _dot(dos, vb, _NT)
                dp = jnp.where(upper, dp2[:, :ATT_BLOCK], dp2[:, ATT_BLOCK:])
                delta = jnp.sum(p * dp, axis=-1, keepdims=True)
                ds2 = _band_split(p * (dp - delta), upper)
                psd = ps * delta
                dsink = jnp.where(lane1 == 2 * j, -jnp.sum(jnp.where(first, psd, 0.0)), dsink)
                dsink = jnp.where(lane1 == 2 * j + 1, -jnp.sum(jnp.where(first, 0.0, psd)), dsink)
                dq = _dot(ds2.astype(BF16), kb) * ATT_SCALE
                dq_ref[:, j * LANE:(j + 1) * LANE] = jnp.where(lo, dq[:ATT_BLOCK], dq[ATT_BLOCK:]).astype(BF16)
                dk_acc[g] = dk_acc[g] + _dot(ds2.T.astype(BF16), qs)
                dv_acc[g] = dv_acc[g] + _dot(_band_split(p, upper).T.astype(BF16), dos)
            acc_ref[0:1, :] += dsink
            lo2 = lax.broadcasted_iota(jnp.int32, (2 * ATT_BLOCK, LANE), 1) < ATT_HEAD_DIM
            folded = []
            for acc in (dk_acc, dv_acc):
                t0 = acc[0] + pltpu.roll(acc[0], ATT_HEAD_DIM, axis=1)
                t1 = acc[1] + pltpu.roll(acc[1], ATT_HEAD_DIM, axis=1)
                folded.append(jnp.where(lo2, t0, t1))
            band = jnp.concatenate(folded, axis=1)
            dkv_ref[...] = (carry_ref[...] + band[:ATT_BLOCK]).astype(BF16)
            carry_ref[...] = band[ATT_BLOCK:]

    def qmap(n):
        return (jnp.minimum(n, nb - 1), 0)

    return pl.pallas_call(
        body, name=name, grid=(nb + 1,),
        in_specs=[pl.BlockSpec((ATT_BLOCK, Q_DIM), lambda n: (jnp.minimum(n, nb - 1), qoff)),
                  pl.BlockSpec((ATT_BLOCK, 2 * LANE), lambda n: (jnp.minimum(n, nb - 1), kvoff)),
                  pl.BlockSpec((ATT_BLOCK, 2 * LANE),
                               lambda n: (jnp.maximum(jnp.minimum(n, nb - 1) - 1, 0), kvoff)),
                  pl.BlockSpec((8, LANE), lambda n: (0, 0)),
                  pl.BlockSpec((ATT_BLOCK, Q_DIM), qmap)] + extra_specs,
        out_specs=(pl.BlockSpec((ATT_BLOCK, Q_DIM), lambda n: (jnp.minimum(n, nb - 1), dqoff)),
                   pl.BlockSpec((ATT_BLOCK, 2 * LANE), lambda n: (jnp.maximum(n - 1, 0), 0)),
                   pl.BlockSpec((8, LANE), lambda n: (0, 0))),
        out_shape=(jax.ShapeDtypeStruct((n_rows, width), BF16), jax.ShapeDtypeStruct((n_rows, 2 * LANE), BF16),
                   jax.ShapeDtypeStruct((8, LANE), F32)),
        input_output_aliases=aliases,
        scratch_shapes=[pltpu.VMEM((ATT_BLOCK, 2 * LANE), F32)],
        compiler_params=_cparams("arbitrary"),
    )(q, kv, kv, sinks8, dout, *extra)


HEADS_PER_GROUP = SSD_HEADS // SSD_GROUPS
PAIRS_PER_GROUP = HEADS_PER_GROUP // 2
T = SSD_CHUNK


def _ssd_scalars(dtr_ref, par_ref):
    dt = _softplus(dtr_ref[...] + par_ref[0:1, :])
    a = -jnp.exp(par_ref[1:2, :])
    ri = lax.broadcasted_iota(jnp.int32, (T, T), 0)
    ci = lax.broadcasted_iota(jnp.int32, (T, T), 1)
    tril = (ri >= ci).astype(F32)
    cs = _dot_hi(tril, dt * a)
    cst = cs.T
    return dt, a, cs, cst, ri, ci


def _ssd_stacked_masks():
    row = lax.broadcasted_iota(jnp.int32, (2 * T, T), 0)
    t = row & (T - 1)
    s = lax.broadcasted_iota(jnp.int32, (2 * T, T), 1)
    return t >= s, s >= t, row[:, 0:1] < T


def _col_s(arr, k0):
    return jnp.concatenate([arr[:, k0:k0 + 1], arr[:, k0 + 1:k0 + 2]], axis=0)


def _row_s(arr_t, k0, first):
    return jnp.where(first, arr_t[k0:k0 + 1, :], arr_t[k0 + 1:k0 + 2, :])


def _lane_pick(lo, arr, k0):
    return jnp.where(lo, arr[:, k0:k0 + 1], arr[:, k0 + 1:k0 + 2])


def _ssd_fwd(xs, bm, cm, dtr, par, name):
    dtr, dt0, _ = _window(dtr)
    dtoff = dt0 // LANE
    n_rows = xs.shape[0]
    nc = n_rows // T
    gw = PAIRS_PER_GROUP * LANE

    def body(x_ref, b_ref, c_ref, dtr_ref, par_ref, y_ref, hs_ref, h_ref):
        @pl.when(pl.program_id(1) == 0)
        def _():
            h_ref[...] = jnp.zeros_like(h_ref)

        dt, a, cs, cst, _, _ = _ssd_scalars(dtr_ref, par_ref)
        tri_s, _, first = _ssd_stacked_masks()
        lo = lax.broadcasted_iota(jnp.int32, (T, LANE), 1) < SSD_CHUNK // 2
        ecs = jnp.exp(cs)
        dect = jnp.exp(cst[:, T - 1:T] - cst)
        etot = jnp.exp(cs[T - 1:T, :])
        bg = b_ref[...]
        cg = c_ref[...]
        cb = _dot(cg.astype(BF16), bg.astype(BF16), _NT)
        cb_s = jnp.concatenate([cb, cb], axis=0)
        cg_s = jnp.concatenate([cg, cg], axis=0)
        bgt_s = jnp.concatenate([bg.T, bg.T], axis=0)
        for j in range(PAIRS_PER_GROUP):
            k0, k1 = 2 * j, 2 * j + 1
            xp = x_ref[:, j * LANE:(j + 1) * LANE]
            hp = h_ref[j]
            hs_ref[0, 0, j] = hp
            rhs = jnp.concatenate([(xp * _lane_pick(lo, dt, k0)).astype(BF16), hp.astype(BF16)], axis=0)
            lm_s = jnp.exp(jnp.where(tri_s, _col_s(cs, k0) - _row_s(cst, k0, first), NEG))
            lhs = jnp.concatenate([lm_s * cb_s, cg_s * _col_s(ecs, k0)], axis=1).astype(BF16)
            y_s = _dot(lhs, rhs)
            s_s = _dot((bgt_s * _row_s(dect, k0, first)).astype(BF16), rhs[:T])
            dsk = jnp.where(lo[0:1, :], par_ref[2:3, k0:k0 + 1], par_ref[2:3, k1:k1 + 1])
            y_ref[:, j * LANE:(j + 1) * LANE] = jnp.where(lo, y_s[:T], y_s[T:]) + dsk * xp
            et = jnp.where(lo[0:1, :], etot[:, k0:k0 + 1], etot[:, k1:k1 + 1])
            h_ref[j] = hp * et + jnp.where(lo, s_s[:T], s_s[T:])

    return pl.pallas_call(
        body, name=name, grid=(SSD_GROUPS, nc),
        in_specs=[pl.BlockSpec((T, gw), lambda g, c: (c, g)),
                  pl.BlockSpec((T, SSD_STATE), lambda g, c: (c, g)),
                  pl.BlockSpec((T, SSD_STATE), lambda g, c: (c, g)),
                  pl.BlockSpec((T, LANE), lambda g, c: (c, g + dtoff)),
                  pl.BlockSpec((8, LANE), lambda g, c: (0, g))],
        out_specs=(pl.BlockSpec((T, gw), lambda g, c: (c, g)),
                   pl.BlockSpec((1, 1, PAIRS_PER_GROUP, SSD_STATE, LANE), lambda g, c: (g, c, 0, 0, 0))),
        out_shape=(jax.ShapeDtypeStruct((n_rows, SSD_D_INNER), F32),
                   jax.ShapeDtypeStruct((SSD_GROUPS, nc, PAIRS_PER_GROUP, SSD_STATE, LANE), F32)),
        scratch_shapes=[pltpu.VMEM((PAIRS_PER_GROUP, SSD_STATE, LANE), F32)],
        compiler_params=_cparams("parallel", "arbitrary"),
    )(xs, bm, cm, dtr, par)


def _ssd_bwd(xs, bm, cm, dtr, par, hs, dy, name, into):
    dtr, dt0, _ = _window(dtr)
    dtoff = dt0 // LANE
    n_rows = xs.shape[0]
    nc = n_rows // T
    gw = PAIRS_PER_GROUP * LANE

    extra, extra_specs, aliases, col0, width = _into(into, 7, 3)
    ddoff = col0 // LANE

    def body(*refs):
        x_ref, b_ref, c_ref, dtr_ref, par_ref, hs_ref, dy_ref = refs[:7]
        dx_ref, db_ref, dc_ref, ddtr_ref, acc_ref, dh_ref = refs[-6:]
        @pl.when(pl.program_id(1) == 0)
        def _():
            dh_ref[...] = jnp.zeros_like(dh_ref)
            acc_ref[...] = jnp.zeros_like(acc_ref)

        dt, a, cs, cst, ri, ci = _ssd_scalars(dtr_ref, par_ref)
        tri_s, trit_s, first = _ssd_stacked_masks()
        lane = lax.broadcasted_iota(jnp.int32, (T, LANE), 1)
        lo = lane < SSD_CHUNK // 2
        lane1 = lane[0:1, :]
        ecs = jnp.exp(cs)
        ecst = jnp.exp(cst)
        dec = jnp.exp(cs[T - 1:T, :] - cs)
        etot = jnp.exp(cs[T - 1:T, :])
        bg = b_ref[...]
        cg = c_ref[...]
        bg_b = bg.astype(BF16)
        cg_b = cg.astype(BF16)
        cb = _dot(cg_b, bg_b, _NT)
        cbt = _dot(bg_b, cg_b, _NT)
        cb_s = jnp.concatenate([cb, cb], axis=0)
        cbt_s = jnp.concatenate([cbt, cbt], axis=0)
        bg_s = jnp.concatenate([bg, bg], axis=0)
        cg_s = jnp.concatenate([cg, cg], axis=0)
        cgt_s = jnp.concatenate([cg.T, cg.T], axis=0)
        dbg = jnp.zeros((T, SSD_STATE), F32)
        dcg = jnp.zeros((T, SSD_STATE), F32)
        dcs_acc = jnp.zeros((T, LANE), F32)
        ddt_acc = jnp.zeros((T, LANE), F32)
        dsk_acc = jnp.zeros((1, LANE), F32)
        last_row = lax.broadcasted_iota(jnp.int32, (T, 1), 0) == T - 1
        for j in range(PAIRS_PER_GROUP):
            k0, k1 = 2 * j, 2 * j + 1
            xp = x_ref[:, j * LANE:(j + 1) * LANE]
            dtl = _lane_pick(lo, dt, k0)
            xdt = xp * dtl
            hp = hs_ref[0, 0, j]
            dhn = dh_ref[j]
            dyp = dy_ref[:, j * LANE:(j + 1) * LANE]
            xdt_b, hp_b, dhn_b, dyp_b = (v.astype(BF16) for v in (xdt, hp, dhn, dyp))
            cs_c, cs_r = _col_s(cs, k0), _row_s(cst, k0, first)
            lm_s = jnp.exp(jnp.where(tri_s, cs_c - cs_r, NEG))
            lmt_s = jnp.exp(jnp.where(trit_s, cs_r - cs_c, NEG))
            dec_c, ecs_c = _col_s(dec, k0), _col_s(ecs, k0)
            r1 = _dot(_stack_pair(dyp, lo), jnp.concatenate([xdt_b, hp_b], axis=0), _NT)
            r2 = _dot(_stack_pair(xdt, lo), jnp.concatenate([dyp_b, dhn_b], axis=0), _NT)
            dm_s, dyh_s = r1[:, :T], r1[:, T:]
            dmt_s, xdh_s = r2[:, :T], r2[:, T:]
            mm_s = lm_s * cb_s
            mmt_s = lmt_s * cbt_s
            bdec_s = bg_s * dec_c
            cexp_s = cg_s * ecs_c
            dx_s = _dot(jnp.concatenate([mmt_s, bdec_s], axis=1).astype(BF16),
                        jnp.concatenate([dyp_b, dhn_b], axis=0))
            dxdt = jnp.where(lo, dx_s[:T], dx_s[T:])
            dc_s = _dot((dm_s * lm_s).astype(BF16), bg_b) + dyh_s * ecs_c
            db_s = _dot((dmt_s * lmt_s).astype(BF16), cg_b) + xdh_s * dec_c
            dcg = dcg + dc_s[:T] + dc_s[T:]
            dbg = dbg + db_s[:T] + db_s[T:]
            dh_s = _dot((cgt_s * _row_s(ecst, k0, first)).astype(BF16), dyp_b)
            et = jnp.where(lo[0:1, :], etot[:, k0:k0 + 1], etot[:, k1:k1 + 1])
            dh_ref[j] = dhn * et + jnp.where(lo, dh_s[:T], dh_s[T:])
            e4 = jnp.sum(bdec_s * xdh_s, axis=-1, keepdims=True)
            dcs_s = (jnp.sum(dm_s * mm_s, axis=-1, keepdims=True) - jnp.sum(dmt_s * mmt_s, axis=-1, keepdims=True)
                     + jnp.sum(cexp_s * dyh_s, axis=-1, keepdims=True) - e4)
            hd = hp * dhn
            tsum0 = jnp.sum(e4[:T]) + etot[:, k0:k0 + 1] * jnp.sum(jnp.where(lo, hd, 0.0))
            tsum1 = jnp.sum(e4[T:]) + etot[:, k1:k1 + 1] * jnp.sum(jnp.where(lo, 0.0, hd))
            dcs0 = dcs_s[:T] + jnp.where(last_row, tsum0, 0.0)
            dcs1 = dcs_s[T:] + jnp.where(last_row, tsum1, 0.0)
            dcs_acc = jnp.where(lane == k0, dcs0, jnp.where(lane == k1, dcs1, dcs_acc))
            prod = dxdt * xp
            ddt_lo = jnp.sum(jnp.where(lo, prod, 0.0), axis=-1, keepdims=True)
            ddt_hi = jnp.sum(jnp.where(lo, 0.0, prod), axis=-1, keepdims=True)
            ddt_acc = jnp.where(lane == k0, ddt_lo, jnp.where(lane == k1, ddt_hi, ddt_acc))
            dyx = dyp * xp
            dsk_acc = jnp.where(lane1 == k0, jnp.sum(jnp.where(lo, dyx, 0.0)),
                                jnp.where(lane1 == k1, jnp.sum(jnp.where(lo, 0.0, dyx)), dsk_acc))
            dsk = jnp.where(lo[0:1, :], par_ref[2:3, k0:k0 + 1], par_ref[2:3, k1:k1 + 1])
            dx_ref[:, j * LANE:(j + 1) * LANE] = dxdt * dtl + dsk * dyp
        db_ref[...] = dbg
        dc_ref[...] = dcg
        triu = (ci >= ri).astype(F32)
        dda = _dot_hi(triu, dcs_acc)
        ddt = ddt_acc + dda * a
        ddtr = ddt * _sig(dtr_ref[...] + par_ref[0:1, :])
        ddtr_ref[...] = ddtr.astype(BF16)
        acc_ref[0:1, :] += jnp.sum(ddtr, axis=0, keepdims=True)
        acc_ref[1:2, :] += jnp.sum(dda * dt, axis=0, keepdims=True) * a
        acc_ref[2:3, :] += dsk_acc

    def rev(g, c):
        return (nc - 1 - c, g)

    return pl.pallas_call(
        body, name=name, grid=(SSD_GROUPS, nc),
        in_specs=[pl.BlockSpec((T, gw), rev),
                  pl.BlockSpec((T, SSD_STATE), rev),
                  pl.BlockSpec((T, SSD_STATE), rev),
                  pl.BlockSpec((T, LANE), lambda g, c: (nc - 1 - c, g + dtoff)),
                  pl.BlockSpec((8, LANE), lambda g, c: (0, g)),
                  pl.BlockSpec((1, 1, PAIRS_PER_GROUP, SSD_STATE, LANE), lambda g, c: (g, nc - 1 - c, 0, 0, 0)),
                  pl.BlockSpec((T, gw), rev)] + extra_specs,
        out_specs=(pl.BlockSpec((T, gw), rev),
                   pl.BlockSpec((T, SSD_STATE), rev),
                   pl.BlockSpec((T, SSD_STATE), rev),
                   pl.BlockSpec((T, LANE), lambda g, c: (nc - 1 - c, g + ddoff)),
                   pl.BlockSpec((8, LANE), lambda g, c: (0, g))),
        out_shape=(jax.ShapeDtypeStruct((n_rows, SSD_D_INNER), F32),
                   jax.ShapeDtypeStruct((n_rows, BC_DIM), F32),
                   jax.ShapeDtypeStruct((n_rows, BC_DIM), F32),
                   jax.ShapeDtypeStruct((n_rows, width), BF16),
                   jax.ShapeDtypeStruct((8, DT_PAD), F32)),
        input_output_aliases=aliases,
        scratch_shapes=[pltpu.VMEM((PAIRS_PER_GROUP, SSD_STATE, LANE), F32)],
        compiler_params=_cparams("parallel", "arbitrary"),
    )(xs, bm, cm, dtr, par, hs, dy, *extra)


ADAM_ROWS = 256


def _adamw(lands, w, m, v, name):
    na = len(lands)
    n_slots, r, wd = lands[0].shape
    tr = r if r <= 2 * ADAM_ROWS else ADAM_ROWS
    nj = r // tr
    bc1 = 1.0 - ADAM_B1 ** ADAM_STEP
    bc2 = 1.0 - ADAM_B2 ** ADAM_STEP

    def body(*refs):
        l_refs = refs[:na]
        w_ref, m_ref, v_ref, g_ref, d_ref, nm_ref, nv_ref = refs[na:]
        for a in range(na):
            @pl.when(pl.program_id(0) == a)
            def _(l_ref=l_refs[a]):
                g = l_ref[0].astype(F32)
                for s in range(1, n_slots):
                    g = g + l_ref[s].astype(F32)
                mn = ADAM_B1 * m_ref[0] + (1.0 - ADAM_B1) * g
                vn = ADAM_B2 * v_ref[0] + (1.0 - ADAM_B2) * (g * g)
                mh = mn / bc1
                vh = vn / bc2
                g_ref[0] = g
                nm_ref[0] = mn
                nv_ref[0] = vn
                d_ref[0] = -ADAM_LR * (mh / (jnp.sqrt(vh) + ADAM_EPS) + ADAM_WD * w_ref[0])

    def land_spec(a):
        return pl.BlockSpec((n_slots, tr, wd),
                            lambda i, j: (0, jnp.where(i == a, j, jnp.where(i < a, 0, nj - 1)), 0))

    blk = pl.BlockSpec((1, tr, wd), lambda i, j: (i, j, 0))
    shp = jax.ShapeDtypeStruct((na, r, wd), F32)
    return pl.pallas_call(
        body, name=name, grid=(na, nj), in_specs=[land_spec(a) for a in range(na)] + [blk, blk, blk],
        out_specs=(blk, blk, blk, blk), out_shape=(shp, shp, shp, shp),
        compiler_params=_cparams("arbitrary", "arbitrary"),
    )(*lands, w, m, v)


def _mesh_pos():
    return lax.axis_index("x"), lax.axis_index("y"), lax.axis_index("c")


def _peer(pos, k):
    x, y, c = pos
    px = 1 - x if (k >> 2) & 1 else x
    py = 1 - y if (k >> 1) & 1 else y
    pc = 1 - c if k & 1 else c
    return px, py, pc


def _flat(pos):
    return 4 * pos[0] + 2 * pos[1] + pos[2]


HBM_SPEC = pl.BlockSpec(memory_space=pl.ANY)


ROW_SHARDED = ("w_ssd_out", "w_att_out", "w_mix_out", "w_ffn_down")
COL_SHARDED = ("w_in", "w_ffn_gate", "w_ffn_up")
GATHERED = ROW_SHARDED + COL_SHARDED + ("conv_w",)
BIG = ROW_SHARDED + COL_SHARDED


SEM_SPEC = pl.BlockSpec(memory_space=pltpu.SEMAPHORE)
TOKEN = jax.ShapeDtypeStruct((8, LANE), F32)
SPLIT_EFFECT = pltpu.SideEffectType.DATAFLOW_SIDE_EFFECTING
GATHER_ROWS = "gather_rows"
GATHER_SLOT = "gather_slot"
SCATTER_ROWS = "scatter_rows"
SCATTER_SLOT = "scatter_slot"


def _land_shape(kind, src):
    if kind == GATHER_ROWS:
        return (N_DEV * src.shape[0],) + src.shape[1:]
    if kind == GATHER_SLOT:
        return (N_DEV,) + src.shape
    if kind == SCATTER_ROWS:
        return (N_DEV, src.shape[0] // N_DEV) + src.shape[1:]
    return src.shape


def _views(kind, src_ref, land_ref, pos, k):
    me = _flat(pos)
    if kind == GATHER_ROWS:
        r = src_ref.shape[0]
        return src_ref, land_ref.at[pl.ds(pl.multiple_of(me * r, 16), r), :]
    if kind == GATHER_SLOT:
        return src_ref, land_ref.at[me]
    dev = _flat(_peer(pos, k))
    if kind == SCATTER_ROWS:
        r = land_ref.shape[1]
        return src_ref.at[pl.ds(pl.multiple_of(dev * r, 16), r), :], land_ref.at[k]
    return src_ref.at[dev], land_ref.at[k]


def _hbm(x):
    return pltpu.with_memory_space_constraint(x, pltpu.HBM)


def _exchange_start(items, after, name):
    kinds = [k for k, _ in items]
    srcs = [_hbm(s) for _, s in items]
    lands = [_hbm(lax.empty(_land_shape(k, s), s.dtype)) for k, s in items]
    n = len(items)
    n_copy = n * (N_DEV - 1)

    def body(*refs):
        src_refs, land_refs = refs[:n], refs[n:2 * n]
        send_sems, recv_sems = refs[2 * n + 1], refs[2 * n + 2]
        token_ref = refs[4 * n + 3]
        pos = _mesh_pos()
        for i, kind in enumerate(kinds):
            for k in range(1, N_DEV):
                s, d = _views(kind, src_refs[i], land_refs[i], pos, k)
                j = i * (N_DEV - 1) + k - 1
                pltpu.make_async_remote_copy(src_ref=s, dst_ref=d, send_sem=send_sems.at[j], recv_sem=recv_sems.at[j],
                                             device_id=_peer(pos, k), device_id_type=MESH_ID).start()
        token_ref[...] = jnp.zeros_like(token_ref)

    arrs = srcs + lands
    outs = pl.pallas_call(
        body, name=name,
        in_specs=[HBM_SPEC] * (2 * n + 1),
        out_specs=[SEM_SPEC, SEM_SPEC] + [HBM_SPEC] * (2 * n) + [pl.BlockSpec(memory_space=pltpu.VMEM)],
        out_shape=[pltpu.SemaphoreType.DMA((n_copy,)), pltpu.SemaphoreType.DMA((n_copy,))]
        + [pltpu.HBM(a.shape, a.dtype) for a in arrs] + [TOKEN],
        input_output_aliases={i: 2 + i for i in range(2 * n)},
        compiler_params=pltpu.CompilerParams(has_side_effects=SPLIT_EFFECT),
    )(*arrs, after)
    return {"kinds": kinds, "send": outs[0], "recv": outs[1], "arrs": outs[2:2 + 2 * n], "token": outs[-1]}


def _exchange_wait(ex, after, name):
    kinds = ex["kinds"]
    n = len(kinds)

    def body(*refs):
        src_refs, land_refs = refs[:n], refs[n:2 * n]
        send_sems, recv_sems = refs[2 * n], refs[2 * n + 1]
        token_ref = refs[-1]
        pos = _mesh_pos()
        for i, kind in enumerate(kinds):
            for k in range(1, N_DEV):
                s, d = _views(kind, src_refs[i], land_refs[i], pos, k)
                j = i * (N_DEV - 1) + k - 1
                cp = pltpu.make_async_remote_copy(src_ref=s, dst_ref=d, send_sem=send_sems.at[j],
                                                  recv_sem=recv_sems.at[j], device_id=_peer(pos, k),
                                                  device_id_type=MESH_ID)
                cp.wait_send()
                cp.wait_recv()
        token_ref[...] = jnp.zeros_like(token_ref)

    outs = pl.pallas_call(
        body, name=name,
        in_specs=[HBM_SPEC] * (2 * n) + [SEM_SPEC, SEM_SPEC, HBM_SPEC],
        out_specs=[HBM_SPEC] * (2 * n) + [pl.BlockSpec(memory_space=pltpu.VMEM)],
        out_shape=[pltpu.HBM(a.shape, a.dtype) for a in ex["arrs"]] + [TOKEN],
        input_output_aliases={i: i for i in range(2 * n)},
        compiler_params=pltpu.CompilerParams(has_side_effects=SPLIT_EFFECT),
    )(*ex["arrs"], ex["send"], ex["recv"], after)
    lands = [_place_own(k, s, d) for k, s, d in zip(kinds, outs[:n], outs[n:2 * n])]
    return lands, outs[-1]


def _place_own(kind, src, land):
    me = _flat(_mesh_pos())
    zeros = (0,) * (src.ndim - 1)
    if kind == GATHER_ROWS:
        return lax.dynamic_update_slice(land, src, (me * src.shape[0],) + zeros)
    if kind == GATHER_SLOT:
        return lax.dynamic_update_slice(land, src[None], (me,) + (0,) * src.ndim)
    if kind == SCATTER_ROWS:
        r = land.shape[1]
        own = lax.dynamic_slice(src, (me * r,) + zeros, (r,) + src.shape[1:])
    else:
        own = lax.dynamic_index_in_dim(src, me, 0, keepdims=False)
    return lax.dynamic_update_slice(land, own[None], (0,) * land.ndim)


def _all_gather_small(x, name):
    r, w = x.shape

    def body(x_ref, out_ref, send_sems, recv_sems):
        pos = _mesh_pos()
        me = _flat(pos)
        copies = []
        for k in range(1, N_DEV):
            cp = pltpu.make_async_remote_copy(
                src_ref=x_ref, dst_ref=out_ref.at[me], send_sem=send_sems.at[k - 1], recv_sem=recv_sems.at[k - 1],
                device_id=_peer(pos, k), device_id_type=MESH_ID)
            cp.start()
            copies.append(cp)
        out_ref[me] = x_ref[...]
        for cp in copies:
            cp.wait()

    vmem = pl.BlockSpec(memory_space=pltpu.VMEM)
    return pl.pallas_call(
        body, name=name, in_specs=[vmem], out_specs=vmem,
        out_shape=jax.ShapeDtypeStruct((N_DEV, r, w), x.dtype),
        scratch_shapes=[pltpu.SemaphoreType.DMA((N_DEV - 1,)), pltpu.SemaphoreType.DMA((N_DEV - 1,))],
        compiler_params=pltpu.CompilerParams(has_side_effects=True),
    )(x)


def _cols(g, lo, hi):
    c = g.shape[-1]
    parts = []
    for d in range(N_DEV):
        a, b = max(lo, d * c), min(hi, (d + 1) * c)
        if a < b:
            parts.append(g[d, :, a - d * c:b - d * c])
    return parts[0] if len(parts) == 1 else jnp.concatenate(parts, axis=1)


def _col_chunks(g):
    c = g.shape[-1] // N_DEV
    return jnp.stack([g[:, d * c:(d + 1) * c] for d in range(N_DEV)])


IN_PART = ("w_in", "conv_w")
OUT_PART = ROW_SHARDED + ("w_ffn_gate", "w_ffn_up")


def _gather_items(w, names, l):
    items = []
    for n in names:
        blk = w[n][l] if n == "conv_w" else w[n][l].astype(BF16)
        items.append((GATHER_ROWS if n in ROW_SHARDED else GATHER_SLOT, blk))
    return items


def _scatter_items(grads, names):
    return [(SCATTER_ROWS, grads[n]) if n in ROW_SHARDED else (SCATTER_SLOT, _col_chunks(grads[n]))
            for n in names]


SMALL = ("ln_in_g", "ln_in_b", "conv_b", "dt_bias", "a_log", "d_skip", "ssd_norm_w", "att_sinks",
         "ln_mix_g", "ln_mix_b", "ln_ffn_g", "ln_ffn_b")


def _pack_small(vals):
    flat = jnp.concatenate([vals[n].reshape(-1) for n in SMALL])
    n = flat.shape[0]
    rows = -(-n // LANE)
    rows = -(-rows // 8) * 8
    return jnp.pad(flat, (0, rows * LANE - n)).reshape(rows, LANE)


def _unpack_small(buf, shapes):
    flat = buf.reshape(-1)
    off = 0
    out = {}
    for n in SMALL:
        cnt = math.prod(shapes[n])
        out[n] = flat[off:off + cnt].reshape(shapes[n])
        off += cnt
    return out


def _to_group_major(v):
    lead = v.shape[:-1]
    t = v.reshape(lead + (SSD_GROUPS, HEADS_PER_GROUP))
    t = jnp.pad(t, [(0, 0)] * len(lead) + [(0, 0), (0, LANE - HEADS_PER_GROUP)])
    return t.reshape(lead + (DT_PAD,))


def _from_group_major(v):
    lead = v.shape[:-1]
    return v.reshape(lead + (SSD_GROUPS, LANE))[..., :HEADS_PER_GROUP].reshape(lead + (SSD_HEADS,))


def _rows8(v):
    return jnp.pad(v, ((0, 8 - v.shape[0]), (0, 0)))


IN_OFFS = {"q": (0, 1024), "kv": (1024, 1280), "z": (1280, 3328), "xs": (3328, 5376), "b": (5376, 5888),
           "c": (5888, 6400), "dt": (6400, 6432), "gl": (6432, 8480)}
PIECES = ("q", "kv", "z", "xs", "b", "c", "dt", "gl")


CAT = ("z", "xs", "gl", "q", "b", "c", "dt", "kv")
CAT_WIDTH = {"q": 1024, "z": 2048, "xs": 2048, "gl": 2048, "b": 512, "c": 512, "kv": 256, "dt": DT_PAD}
CAT_OFF = {p: sum(CAT_WIDTH[q] for q in CAT[:i]) for i, p in enumerate(CAT)}
CAT_DIM = sum(CAT_WIDTH.values())
MAIN_DIM = CAT_OFF["kv"]


def _cat_w_in(g):
    pieces = {p: _cols(g, lo, hi) for p, (lo, hi) in IN_OFFS.items()}
    pieces["dt"] = _to_group_major(pieces["dt"])
    return jnp.concatenate([pieces[p] for p in CAT], axis=1)


def _uncat_dw_in(dw):
    pieces = {p: dw[:, CAT_OFF[p]:CAT_OFF[p] + CAT_WIDTH[p]] for p in CAT}
    pieces["dt"] = _from_group_major(pieces["dt"])
    return jnp.concatenate([pieces[p] for p in PIECES], axis=1)


def _params_out(W):
    p = {n: W[n] for n in ROW_SHARDED}
    for n in ("w_ffn_gate", "w_ffn_up"):
        p[n] = _cols(W[n], 0, FFN_HIDDEN)
    return p


def _params_in(l, W, sm):
    p = {"w_cat": _cat_w_in(W["w_in"])}
    cw = _cols(W["conv_w"], 0, SSD_D_INNER + 2 * BC_DIM)
    cb = sm["conv_b"][l]
    segs = {"xs": (0, 2048), "b": (2048, 2560), "c": (2560, 3072)}
    p["conv_w8"] = {s: _rows8(cw[:, lo:hi]) for s, (lo, hi) in segs.items()}
    p["conv_b8"] = {s: _rows8(cb[None, lo:hi]) for s, (lo, hi) in segs.items()}
    p["ssd_par"] = _rows8(jnp.stack([_to_group_major(sm["dt_bias"][l]), _to_group_major(sm["a_log"][l]),
                                     _to_group_major(sm["d_skip"][l])]))
    p["norm_w"] = sm["ssd_norm_w"][l]
    p["sinks8"] = _rows8(jnp.pad(sm["att_sinks"][l], (0, LANE - ATT_HEADS))[None])
    for n in ("ln_mix_g", "ln_mix_b", "ln_ffn_g", "ln_ffn_b"):
        p[n] = sm[n][l]
    return p


def _fwd_mixers(h0, p, l, dep=None):
    tag = f"l{l}_"
    a = {"h0": h0}
    proj = _mm(h0, p["w_cat"], "nn", tag + "proj", dep=dep)
    for pc in CAT:
        a[pc] = (proj, CAT_OFF[pc], CAT_WIDTH[pc])
    for s in ("xs", "b", "c"):
        a[s + "c"] = _conv_fwd(a[s], p["conv_w8"][s], p["conv_b8"][s], tag + "conv_" + s)
    a["y"], a["hs"] = _ssd_fwd(a["xsc"], a["bc"], a["cc"], a["dt"], p["ssd_par"], tag + "ssd_fwd")
    a["yn"] = _gnorm_fwd(a["y"], a["z"], p["norm_w"], tag + "gnorm")
    a["att"] = _att_fwd(a["q"], a["kv"], p["sinks8"], tag + "att_fwd")
    return a


def _fwd_out(a, p, l, dep=None):
    tag = f"l{l}_"
    h0 = a["h0"]
    a["ya"] = _mm(a["yn"], p["w_ssd_out"], "nn", tag + "ssd_out", dep=dep)
    a["yb"] = _mm(a["att"], p["w_att_out"], "nn", tag + "att_out", dep=dep)
    a["merged"] = _merge_fwd(a["gl"], a["ya"], a["yb"], tag + "merge")
    a["mix"] = _mm(a["merged"], p["w_mix_out"], "nn", tag + "mix_out")
    a["h1"] = _ln_fwd(h0, a["mix"], p["ln_mix_g"], p["ln_mix_b"], ALPHA, tag + "ln_mix")
    a["fg"] = _mm(a["h1"], p["w_ffn_gate"], "nn", tag + "ffn_gate")
    a["fu"] = _mm(a["h1"], p["w_ffn_up"], "nn", tag + "ffn_up")
    a["act"] = _swiglu_fwd(a["fg"], a["fu"], tag + "swiglu")
    a["ffn"] = _mm(a["act"], p["w_ffn_down"], "nn", tag + "ffn_down")
    a["h2"] = _ln_fwd(a["h1"], a["ffn"], p["ln_ffn_g"], p["ln_ffn_b"], ALPHA, tag + "ln_ffn")
    return a


def _dw(x, dy, name, dep=None):
    return _mm(x, dy, "tn", name, out_dtype=BF16, dep=dep)


def _bwd_out(a, p, dh2, l, dep=None):
    tag = f"l{l}_b_"
    gw, gs = {}, {}
    du2, acc = _ln_bwd(a["h1"], a["ffn"], p["ln_ffn_g"], dh2, ALPHA, tag + "ln_ffn")
    gs["ln_ffn_g"], gs["ln_ffn_b"] = acc[0], acc[1]
    gw["w_ffn_down"] = _dw(a["act"], du2, tag + "dw_down", dep=dep)
    dact = _mm(du2, p["w_ffn_down"], "nt", tag + "dact", dep=dep)
    dfg, dfu = _swiglu_bwd(a["fg"], a["fu"], dact, tag + "swiglu")
    gw["w_ffn_gate"] = _dw(a["h1"], dfg, tag + "dw_gate")
    gw["w_ffn_up"] = _dw(a["h1"], dfu, tag + "dw_up")
    dh1 = _mm(dfg, p["w_ffn_gate"], "nt", tag + "dh1_gate", add=du2, add_scale=ALPHA)
    dh1 = _mm(dfu, p["w_ffn_up"], "nt", tag + "dh1_up", add=dh1)
    du1, acc = _ln_bwd(a["h0"], a["mix"], p["ln_mix_g"], dh1, ALPHA, tag + "ln_mix")
    gs["ln_mix_g"], gs["ln_mix_b"] = acc[0], acc[1]
    gw["w_mix_out"] = _dw(a["merged"], du1, tag + "dw_mix")
    dmerged = _mm(du1, p["w_mix_out"], "nt", tag + "dmerged")
    dya, dyb, dproj = _merge_bwd(a["gl"], a["ya"], a["yb"], dmerged, tag + "merge",
                                 (None, CAT_OFF["gl"], MAIN_DIM))
    gw["w_ssd_out"] = _dw(a["yn"], dya, tag + "dw_ssd")
    gw["w_att_out"] = _dw(a["att"], dyb, tag + "dw_att")
    return {"du1": du1, "dya": dya, "dyb": dyb, "dproj": dproj}, gw, gs


def _bwd_mixers(a, p, carry, l, dep=None):
    tag = f"l{l}_b_"
    gs = {}
    du1, dproj = carry["du1"], carry["dproj"]

    def win(pc):
        return (dproj, CAT_OFF[pc], MAIN_DIM)

    dyn = _mm(carry["dya"], p["w_ssd_out"], "nt", tag + "dyn", dep=dep)
    datt = _mm(carry["dyb"], p["w_att_out"], "nt", tag + "datt", out_dtype=BF16, dep=dep)
    dproj, dkv, acc = _att_bwd(a["q"], a["kv"], p["sinks8"], datt, tag + "att", win("q"))
    gs["att_sinks"] = acc[0, :ATT_HEADS]
    dy, dproj, acc = _gnorm_bwd(a["y"], a["z"], p["norm_w"], dyn, tag + "gnorm", win("z"))
    gs["ssd_norm_w"] = acc[0]
    dxs, dbm, dcm, dproj, acc = _ssd_bwd(a["xsc"], a["bc"], a["cc"], a["dt"], p["ssd_par"], a["hs"], dy,
                                         tag + "ssd", win("dt"))
    gs["dt_bias"], gs["a_log"], gs["d_skip"] = (_from_group_major(acc[i]) for i in range(3))
    dconv_w, dconv_b = [], []
    for s, dout in (("xs", dxs), ("b", dbm), ("c", dcm)):
        dc, acc = _conv_bwd_pre(a[s], p["conv_w8"][s], p["conv_b8"][s], dout, tag + "conv_pre_" + s)
        dconv_w.append(acc[:CONV_TAPS])
        dconv_b.append(acc[CONV_TAPS])
        dproj = _conv_bwd_in(dc, p["conv_w8"][s], tag + "conv_in_" + s, win(s))
    gconv = jnp.concatenate(dconv_w, axis=1)
    gs["conv_b"] = jnp.concatenate(dconv_b)
    w_main, w_kv = p["w_cat"][:, :MAIN_DIM], p["w_cat"][:, MAIN_DIM:]
    dw = jnp.concatenate([_dw(a["h0"], dproj, tag + "dw_in"), _dw(a["h0"], dkv, tag + "dw_in_kv")], axis=1)
    dh0 = _mm(dproj, w_main, "nt", tag + "dh0", add=du1, add_scale=ALPHA)
    dh0 = _mm(dkv, w_kv, "nt", tag + "dh0_kv", add=dh0)
    return dh0, _uncat_dw_in(dw), gconv, gs


def _step(x, target, w, m, v):
    x2 = x[0]
    t2 = target[0]
    tok = jnp.zeros(TOKEN.shape, TOKEN.dtype)

    ex = _exchange_start(_gather_items(w, IN_PART, 0), tok, "gather_l0_in_start")
    lands, tok = _exchange_wait(ex, ex["token"], "gather_l0_in_wait")
    p0 = _params_in(0, dict(zip(IN_PART, lands)), w)
    ex = _exchange_start(_gather_items(w, OUT_PART, 0) + _gather_items(w, IN_PART, 1), tok,
                         "gather_l0_out_l1_in_start")
    h = _ln_fwd(x2, None, w["ln_in_g"], w["ln_in_b"], 1.0, "ln_in")
    a0 = _fwd_mixers(h, p0, 0, dep=ex["token"])
    lands, tok = _exchange_wait(ex, a0["att"], "gather_l0_out_l1_in_wait")
    p0.update(_params_out(dict(zip(OUT_PART, lands))))
    p1 = _params_in(1, dict(zip(IN_PART, lands[len(OUT_PART):])), w)
    ex = _exchange_start(_gather_items(w, OUT_PART, 1), tok, "gather_l1_out_start")
    a0 = _fwd_out(a0, p0, 0, dep=ex["token"])
    lands, tok = _exchange_wait(ex, a0["h2"], "gather_l1_out_wait")
    p1.update(_params_out(dict(zip(OUT_PART, lands))))
    a1 = _fwd_out(_fwd_mixers(a0["h2"], p1, 1), p1, 1)

    sse, dh = _loss_fwd_bwd(a1["h2"], t2, "loss")
    loss = lax.psum(0.5 / D_MODEL * sse[0, 0], ("x", "y", "c"))

    carry, gw1, gs1 = _bwd_out(a1, p1, dh, 1)
    dh, gw1["w_in"], gw1["conv_w"], gs = _bwd_mixers(a1, p1, carry, 1)
    gs1.update(gs)
    ex1 = _exchange_start(_scatter_items(gw1, GATHERED), tok, "scatter_l1_start")
    carry, gw0, gs0 = _bwd_out(a0, p0, dh, 0, dep=ex1["token"])
    lands, tok = _exchange_wait(ex1, carry["dyb"], "scatter_l1_wait")
    land1 = dict(zip(GATHERED, lands))
    ex0 = _exchange_start(_scatter_items(gw0, OUT_PART), tok, "scatter_l0_out_start")
    dh, gw0["w_in"], gw0["conv_w"], gs = _bwd_mixers(a0, p0, carry, 0, dep=ex0["token"])
    gs0.update(gs)
    lands, tok = _exchange_wait(ex0, dh, "scatter_l0_out_wait")
    land0 = dict(zip(OUT_PART, lands))
    ex0 = _exchange_start(_scatter_items(gw0, IN_PART), tok, "scatter_l0_in_start")
    grad_x2, acc = _ln_bwd(x2, None, w["ln_in_g"], dh, 1.0, "ln_in_b")

    outs = [{} for _ in range(4)]

    def update(names):
        res = None
        for n in names:
            res = _adamw([land0[n], land1[n]], w[n], m[n], v[n], "adamw_" + n)
            for o, t in zip(outs, res):
                o[n] = t
        return res[1]

    update(OUT_PART)
    gsm = {"ln_in_g": acc[0], "ln_in_b": acc[1]}
    for n in SMALL[2:]:
        gsm[n] = jnp.stack([gs0[n], gs1[n]])
    small_shapes = {n: w[n].shape for n in SMALL}
    land_s = _all_gather_small(_pack_small(gsm), "small_grads_all_gather")
    res = _adamw([land_s], _pack_small(w)[None], _pack_small(m)[None], _pack_small(v)[None], "adamw_small")
    for o, t in zip(outs, res):
        o.update(_unpack_small(t[0], small_shapes))
    lands, _ = _exchange_wait(ex0, res[1], "scatter_l0_in_wait")
    land0.update(zip(IN_PART, lands))
    update(IN_PART)
    return loss, grad_x2[None], outs


WEIGHT_NAMES = ("ln_in_g", "ln_in_b", "w_in", "conv_w", "conv_b", "dt_bias", "a_log", "d_skip", "ssd_norm_w",
                "att_sinks", "w_ssd_out", "w_att_out", "w_mix_out", "ln_mix_g", "ln_mix_b", "w_ffn_gate",
                "w_ffn_up", "w_ffn_down", "ln_ffn_g", "ln_ffn_b")


def kernel(x, ln_in_g, ln_in_b, w_in, conv_w, conv_b, dt_bias, a_log, d_skip, ssd_norm_w, att_sinks, w_ssd_out, w_att_out, w_mix_out, ln_mix_g, ln_mix_b, w_ffn_gate, w_ffn_up, w_ffn_down, ln_ffn_g, ln_ffn_b, loss_target, m_ln_in_g, m_ln_in_b, m_w_in, m_conv_w, m_conv_b, m_dt_bias, m_a_log, m_d_skip, m_ssd_norm_w, m_att_sinks, m_w_ssd_out, m_w_att_out, m_w_mix_out, m_ln_mix_g, m_ln_mix_b, m_w_ffn_gate, m_w_ffn_up, m_w_ffn_down, m_ln_ffn_g, m_ln_ffn_b, v_ln_in_g, v_ln_in_b, v_w_in, v_conv_w, v_conv_b, v_dt_bias, v_a_log, v_d_skip, v_ssd_norm_w, v_att_sinks, v_w_ssd_out, v_w_att_out, v_w_mix_out, v_ln_mix_g, v_ln_mix_b, v_w_ffn_gate, v_w_ffn_up, v_w_ffn_down, v_ln_ffn_g, v_ln_ffn_b):
    w = dict(zip(WEIGHT_NAMES, (ln_in_g, ln_in_b, w_in, conv_w, conv_b, dt_bias, a_log, d_skip, ssd_norm_w,
                                att_sinks, w_ssd_out, w_att_out, w_mix_out, ln_mix_g, ln_mix_b, w_ffn_gate,
                                w_ffn_up, w_ffn_down, ln_ffn_g, ln_ffn_b)))
    m = dict(zip(WEIGHT_NAMES, (m_ln_in_g, m_ln_in_b, m_w_in, m_conv_w, m_conv_b, m_dt_bias, m_a_log, m_d_skip,
                                m_ssd_norm_w, m_att_sinks, m_w_ssd_out, m_w_att_out, m_w_mix_out, m_ln_mix_g,
                                m_ln_mix_b, m_w_ffn_gate, m_w_ffn_up, m_w_ffn_down, m_ln_ffn_g, m_ln_ffn_b)))
    v = dict(zip(WEIGHT_NAMES, (v_ln_in_g, v_ln_in_b, v_w_in, v_conv_w, v_conv_b, v_dt_bias, v_a_log, v_d_skip,
                                v_ssd_norm_w, v_att_sinks, v_w_ssd_out, v_w_att_out, v_w_mix_out, v_ln_mix_g,
                                v_ln_mix_b, v_w_ffn_gate, v_w_ffn_up, v_w_ffn_down, v_ln_ffn_g, v_ln_ffn_b)))
    loss, grad_x, outs = _step(x, loss_target, w, m, v)
    result = [loss, grad_x]
    for o in outs:
        result.extend(o[n] for n in WEIGHT_NAMES)
    return tuple(result)
```

```python
import functools
import math

import jax
import jax.numpy as jnp
from jax import lax
from jax.experimental import pallas as pl
from jax.experimental.pallas import tpu as pltpu

F32 = jnp.float32
BF16 = jnp.bfloat16

D_MODEL = 1024
DEPTH = 2
N_DEV = 8
ATT_HEADS = 16
ATT_KV_HEADS = 2
ATT_HEAD_DIM = 64
ATT_BLOCK = 128
SSD_D_INNER = 2048
SSD_HEADS = 32
SSD_GROUPS = 4
SSD_STATE = 128
SSD_CHUNK = 128
FFN_HIDDEN = 2816
LN_EPS = 1e-5
RMS_EPS = 1e-5
ALPHA = (2 * DEPTH) ** 0.25
Q_DIM = 1024
KV_DIM = 128
BC_DIM = 512
IN_DIM = 8480
IN_SHARD = IN_DIM // N_DEV
DT_PAD = 512

ADAM_LR = 0.001
ADAM_B1 = 0.9
ADAM_B2 = 0.999
ADAM_EPS = 1e-08
ADAM_WD = 0.01
ADAM_STEP = 10

LANE = 128
VMEM_LIMIT = 48 * 1024 * 1024
PACK_W = 1024
NEG = -1e30

_NN = (((1,), (0,)), ((), ()))
_NT = (((1,), (1,)), ((), ()))
_TN = (((0,), (0,)), ((), ()))
MESH_ID = pl.DeviceIdType.MESH


def _dot(a, b, dims=_NN):
    return lax.dot_general(a, b, dims, preferred_element_type=F32)


def _dot_hi(a, b):
    return lax.dot_general(a, b, _NN, preferred_element_type=F32, precision=lax.Precision.HIGHEST)


def _sig(x):
    return 1.0 / (1.0 + jnp.exp(-x))


def _softplus(x):
    return jnp.maximum(x, 0.0) + jnp.log(1.0 + jnp.exp(-jnp.abs(x)))


def _cparams(*sem):
    return pltpu.CompilerParams(dimension_semantics=sem, vmem_limit_bytes=VMEM_LIMIT)


def _pick(n, cap):
    if n <= cap:
        return n
    best = None
    for t in range(LANE, cap + 1, LANE):
        if n % t == 0:
            best = t
    assert best is not None, (n, cap)
    return best


def _tile(n):
    if n <= 1024 or n % 1024 == 0:
        return min(n, 1024)
    return _pick(n, 1408)


def _rows(n):
    return min(512, n)


def _window(x):
    return x if isinstance(x, tuple) else (x, 0, x.shape[1])


def _into(into, n_in, out_idx):
    buf, col0, width = into
    if buf is None:
        return [], [], {}, col0, width
    return [buf], [pl.BlockSpec(memory_space=pl.ANY)], {n_in: out_idx}, col0, width


def _mm(a, b, mode, name, add=None, add_scale=1.0, out_dtype=F32, dep=None):
    if mode == "nn":
        m, k = a.shape
        n = b.shape[1]
    elif mode == "nt":
        m, k = a.shape
        n = b.shape[0]
    else:
        k, m = a.shape
        n = b.shape[1]
    tm = _tile(m)
    tn = _pick(n, 2176) if mode == "tn" and n > 1024 else _tile(n)
    tk = _pick(k, 2176) if mode == "nt" and a.dtype == BF16 and k > 2816 else _tile(k)
    nk = k // tk
    has_add = add is not None
    dims = {"nn": _NN, "nt": _NT, "tn": _TN}[mode]

    def body(*refs):
        if dep is not None:
            refs = refs[:-3] + refs[-2:]
        if has_add:
            a_ref, b_ref, add_ref, o_ref, acc_ref = refs
        else:
            a_ref, b_ref, o_ref, acc_ref = refs
        kk = pl.program_id(2)

        @pl.when(kk == 0)
        def _():
            if has_add:
                acc_ref[...] = add_scale * add_ref[...].astype(F32)
            else:
                acc_ref[...] = jnp.zeros_like(acc_ref)

        acc_ref[...] += _dot(a_ref[...].astype(BF16), b_ref[...].astype(BF16), dims)

        @pl.when(kk == nk - 1)
        def _():
            o_ref[...] = acc_ref[...].astype(o_ref.dtype)

    if mode == "nn":
        a_spec = pl.BlockSpec((tm, tk), lambda i, j, kk: (i, kk))
        b_spec = pl.BlockSpec((tk, tn), lambda i, j, kk: (kk, j))
    elif mode == "nt":
        a_spec = pl.BlockSpec((tm, tk), lambda i, j, kk: (i, kk))
        b_spec = pl.BlockSpec((tn, tk), lambda i, j, kk: (j, kk))
    else:
        a_spec = pl.BlockSpec((tk, tm), lambda i, j, kk: (kk, i))
        b_spec = pl.BlockSpec((tk, tn), lambda i, j, kk: (kk, j))
    o_spec = pl.BlockSpec((tm, tn), lambda i, j, kk: (i, j))
    in_specs = [a_spec, b_spec] + ([o_spec] if has_add else [])
    args = (a, b) + ((add,) if has_add else ())
    if dep is not None:
        in_specs.append(pl.BlockSpec((8, LANE), lambda i, j, kk: (0, 0)))
        args += (dep,)
    return pl.pallas_call(
        body, name=name, grid=(m // tm, n // tn, nk),
        in_specs=in_specs, out_specs=o_spec,
        out_shape=jax.ShapeDtypeStruct((m, n), out_dtype),
        scratch_shapes=[pltpu.VMEM((tm, tn), F32)],
        compiler_params=_cparams("parallel", "parallel", "arbitrary"),
    )(*args)


def _vec_spec(width):
    return pl.BlockSpec((1, width), lambda i: (0, 0))


def _ln_fwd(a, b, gamma, beta, alpha, name):
    n_rows, dm = a.shape
    has_b = b is not None

    def body(*refs):
        if has_b:
            a_ref, b_ref, g_ref, be_ref, o_ref = refs
            u = alpha * a_ref[...] + b_ref[...]
        else:
            a_ref, g_ref, be_ref, o_ref = refs
            u = a_ref[...]
        mu = jnp.mean(u, axis=-1, keepdims=True)
        d = u - mu
        var = jnp.mean(d * d, axis=-1, keepdims=True)
        o_ref[...] = d * lax.rsqrt(var + LN_EPS) * g_ref[...] + be_ref[...]

    row = pl.BlockSpec((_rows(n_rows),dm), lambda i: (i, 0))
    in_specs = [row] + ([row] if has_b else []) + [_vec_spec(dm), _vec_spec(dm)]
    args = (a,) + ((b,) if has_b else ()) + (gamma.reshape(1, dm), beta.reshape(1, dm))
    return pl.pallas_call(
        body, name=name, grid=(n_rows // _rows(n_rows),), in_specs=in_specs, out_specs=row,
        out_shape=jax.ShapeDtypeStruct((n_rows, dm), F32),
        compiler_params=_cparams("parallel"),
    )(*args)


def _ln_bwd(a, b, gamma, dy, alpha, name):
    n_rows, dm = a.shape
    has_b = b is not None

    def body(*refs):
        if has_b:
            a_ref, b_ref, g_ref, dy_ref, du_ref, acc_ref = refs
            u = alpha * a_ref[...] + b_ref[...]
        else:
            a_ref, g_ref, dy_ref, du_ref, acc_ref = refs
            u = a_ref[...]

        @pl.when(pl.program_id(0) == 0)
        def _():
            acc_ref[...] = jnp.zeros_like(acc_ref)

        mu = jnp.mean(u, axis=-1, keepdims=True)
        d = u - mu
        var = jnp.mean(d * d, axis=-1, keepdims=True)
        rstd = lax.rsqrt(var + LN_EPS)
        xhat = d * rstd
        dyv = dy_ref[...]
        acc_ref[0:1, :] += jnp.sum(dyv * xhat, axis=0, keepdims=True)
        acc_ref[1:2, :] += jnp.sum(dyv, axis=0, keepdims=True)
        dxh = dyv * g_ref[...]
        m1 = jnp.mean(dxh, axis=-1, keepdims=True)
        m2 = jnp.mean(dxh * xhat, axis=-1, keepdims=True)
        du_ref[...] = rstd * (dxh - m1 - xhat * m2)

    row = pl.BlockSpec((_rows(n_rows),dm), lambda i: (i, 0))
    in_specs = [row] + ([row] if has_b else []) + [_vec_spec(dm), row]
    args = (a,) + ((b,) if has_b else ()) + (gamma.reshape(1, dm), dy)
    return pl.pallas_call(
        body, name=name, grid=(n_rows // _rows(n_rows),), in_specs=in_specs,
        out_specs=(row, pl.BlockSpec((8, dm), lambda i: (0, 0))),
        out_shape=(jax.ShapeDtypeStruct((n_rows, dm), F32), jax.ShapeDtypeStruct((8, dm), F32)),
        compiler_params=_cparams("arbitrary"),
    )(*args)


def _loss_fwd_bwd(y, target, name):
    n_rows, dm = y.shape

    def body(y_ref, t_ref, acc_ref, dy_ref):
        @pl.when(pl.program_id(0) == 0)
        def _():
            acc_ref[...] = jnp.zeros_like(acc_ref)

        d = y_ref[...] - t_ref[...]
        acc_ref[...] += jnp.sum(d * d)
        dy_ref[...] = d * (1.0 / dm)

    row = pl.BlockSpec((_rows(n_rows),dm), lambda i: (i, 0))
    return pl.pallas_call(
        body, name=name, grid=(n_rows // _rows(n_rows),), in_specs=[row, row],
        out_specs=(pl.BlockSpec((8, LANE), lambda i: (0, 0)), row),
        out_shape=(jax.ShapeDtypeStruct((8, LANE), F32), jax.ShapeDtypeStruct((n_rows, dm), F32)),
        compiler_params=_cparams("arbitrary"),
    )(y, target)


def _swiglu_fwd(g, u, name):
    n_rows, w = g.shape
    tw = _pick(w, 1408)

    def body(g_ref, u_ref, o_ref):
        gv = g_ref[...]
        o_ref[...] = (gv * _sig(gv) * u_ref[...]).astype(BF16)

    blk = pl.BlockSpec((_rows(n_rows),tw), lambda i, j: (i, j))
    return pl.pallas_call(
        body, name=name, grid=(n_rows // _rows(n_rows), w // tw), in_specs=[blk, blk], out_specs=blk,
        out_shape=jax.ShapeDtypeStruct((n_rows, w), BF16),
        compiler_params=_cparams("parallel", "parallel"),
    )(g, u)


def _swiglu_bwd(g, u, dact, name):
    n_rows, w = g.shape
    tw = _pick(w, 1408)

    def body(g_ref, u_ref, da_ref, dg_ref, du_ref):
        gv = g_ref[...]
        s = _sig(gv)
        da = da_ref[...]
        dg_ref[...] = (da * u_ref[...] * (s * (1.0 + gv * (1.0 - s)))).astype(BF16)
        du_ref[...] = (da * gv * s).astype(BF16)

    blk = pl.BlockSpec((_rows(n_rows),tw), lambda i, j: (i, j))
    return pl.pallas_call(
        body, name=name, grid=(n_rows // _rows(n_rows), w // tw), in_specs=[blk, blk, blk], out_specs=(blk, blk),
        out_shape=(jax.ShapeDtypeStruct((n_rows, w), BF16), jax.ShapeDtypeStruct((n_rows, w), BF16)),
        compiler_params=_cparams("parallel", "parallel"),
    )(g, u, dact)


def _gate_specs(gl, n_rows, dm):
    arr, g0, _ = _window(gl)
    return arr, [pl.BlockSpec((_rows(n_rows), dm), lambda i, k=k: (i, g0 // dm + k)) for k in range(2)]


def _merge_fwd(gl, ya, yb, name):
    n_rows, dm = ya.shape
    gl_arr, gspecs = _gate_specs(gl, n_rows, dm)

    def body(ga_ref, gb_ref, ya_ref, yb_ref, o_ref):
        o_ref[...] = (_sig(ga_ref[...]) * ya_ref[...] + _sig(gb_ref[...]) * yb_ref[...]).astype(BF16)

    row = pl.BlockSpec((_rows(n_rows),dm), lambda i: (i, 0))
    return pl.pallas_call(
        body, name=name, grid=(n_rows // _rows(n_rows),), in_specs=gspecs + [row, row], out_specs=row,
        out_shape=jax.ShapeDtypeStruct((n_rows, dm), BF16),
        compiler_params=_cparams("parallel"),
    )(gl_arr, gl_arr, ya, yb)


def _merge_bwd(gl, ya, yb, dmerged, name, into):
    n_rows, dm = ya.shape
    gl_arr, gspecs = _gate_specs(gl, n_rows, dm)
    extra, extra_specs, aliases, col0, width = _into(into, 5, 2)

    def body(*refs):
        ga_ref, gb_ref, ya_ref, yb_ref, dm_ref = refs[:5]
        dya_ref, dyb_ref, dgl_ref = refs[-3:]
        ga = _sig(ga_ref[...])
        gb = _sig(gb_ref[...])
        dmv = dm_ref[...]
        dya_ref[...] = (dmv * ga).astype(BF16)
        dyb_ref[...] = (dmv * gb).astype(BF16)
        dgl_ref[:, :dm] = (dmv * ya_ref[...] * ga * (1.0 - ga)).astype(BF16)
        dgl_ref[:, dm:] = (dmv * yb_ref[...] * gb * (1.0 - gb)).astype(BF16)

    row = pl.BlockSpec((_rows(n_rows),dm), lambda i: (i, 0))
    row2 = pl.BlockSpec((_rows(n_rows),2 * dm), lambda i: (i, col0 // (2 * dm)))
    return pl.pallas_call(
        body, name=name, grid=(n_rows // _rows(n_rows),), in_specs=gspecs + [row, row, row] + extra_specs,
        out_specs=(row, row, row2),
        out_shape=(jax.ShapeDtypeStruct((n_rows, dm), BF16), jax.ShapeDtypeStruct((n_rows, dm), BF16),
                   jax.ShapeDtypeStruct((n_rows, width), BF16)),
        input_output_aliases=aliases,
        compiler_params=_cparams("parallel"),
    )(gl_arr, gl_arr, ya, yb, dmerged, *extra)


CONV_TAPS = 4
CONV_COLS = 512
HALO = 8


def _shift_down(cur, prev8, s, row8):
    r = pltpu.roll(cur, s, axis=0)
    top = jnp.where(row8 < s, pltpu.roll(prev8, s, axis=0), r[0:HALO])
    return jnp.concatenate([top, r[HALO:]], axis=0)


def _shift_up(cur, next8, s, row8):
    n = cur.shape[0]
    r = pltpu.roll(cur, n - s, axis=0)
    bot = jnp.where(row8 >= HALO - s, pltpu.roll(next8, HALO - s, axis=0), r[n - HALO:])
    return jnp.concatenate([r[:n - HALO], bot], axis=0)


def _conv_pre(u_ref, prev_ref, w_ref, b_ref, li):
    cur = u_ref[...]
    prev8 = jnp.where(li == 0, 0.0, prev_ref[...])
    row8 = lax.broadcasted_iota(jnp.int32, prev8.shape, 0)
    shifted = [cur] + [_shift_down(cur, prev8, s, row8) for s in range(1, CONV_TAPS)]
    acc = b_ref[...] + shifted[0] * w_ref[CONV_TAPS - 1:CONV_TAPS, :]
    for s in range(1, CONV_TAPS):
        acc = acc + shifted[s] * w_ref[CONV_TAPS - 1 - s:CONV_TAPS - s, :]
    return acc, shifted


def _conv_specs(n_rows, tl, col0=0):
    off = col0 // CONV_COLS
    cur = pl.BlockSpec((tl, CONV_COLS), lambda cj, li: (li, cj + off))
    prev = pl.BlockSpec((HALO, CONV_COLS), lambda cj, li: (jnp.maximum(li * (tl // HALO) - 1, 0), cj + off))
    nxt = pl.BlockSpec((HALO, CONV_COLS),
                       lambda cj, li: (jnp.minimum((li + 1) * (tl // HALO), n_rows // HALO - 1), cj + off))
    par = pl.BlockSpec((8, CONV_COLS), lambda cj, li: (0, cj + off))
    return cur, prev, nxt, par


def _conv_fwd(u, w8, b8, name):
    u, u0, c = _window(u)
    n_rows = u.shape[0]
    tl = _rows(n_rows)
    cur, _, _, par = _conv_specs(n_rows, tl)
    ucur, prev, _, _ = _conv_specs(n_rows, tl, u0)

    def body(u_ref, prev_ref, w_ref, b_ref, o_ref):
        acc, _ = _conv_pre(u_ref, prev_ref, w_ref, b_ref[0:1, :], pl.program_id(1))
        o_ref[...] = acc * _sig(acc)

    return pl.pallas_call(
        body, name=name, grid=(c // CONV_COLS, n_rows // tl), in_specs=[ucur, prev, par, par], out_specs=cur,
        out_shape=jax.ShapeDtypeStruct((n_rows, c), F32),
        compiler_params=_cparams("parallel", "parallel"),
    )(u, u, w8, b8)


def _conv_bwd_pre(u, w8, b8, dout, name):
    u, u0, c = _window(u)
    n_rows = u.shape[0]
    tl = _rows(n_rows)
    cur, _, _, par = _conv_specs(n_rows, tl)
    ucur, prev, _, _ = _conv_specs(n_rows, tl, u0)

    def body(u_ref, prev_ref, w_ref, b_ref, do_ref, dc_ref, acc_ref):
        @pl.when(pl.program_id(1) == 0)
        def _():
            acc_ref[...] = jnp.zeros_like(acc_ref)

        acc, shifted = _conv_pre(u_ref, prev_ref, w_ref, b_ref[0:1, :], pl.program_id(1))
        sg = _sig(acc)
        dc = do_ref[...] * (sg * (1.0 + acc * (1.0 - sg)))
        dc_ref[...] = dc
        for k in range(CONV_TAPS):
            acc_ref[k:k + 1, :] += jnp.sum(dc * shifted[CONV_TAPS - 1 - k], axis=0, keepdims=True)
        acc_ref[CONV_TAPS:CONV_TAPS + 1, :] += jnp.sum(dc, axis=0, keepdims=True)

    return pl.pallas_call(
        body, name=name, grid=(c // CONV_COLS, n_rows // tl), in_specs=[ucur, prev, par, par, cur],
        out_specs=(cur, par),
        out_shape=(jax.ShapeDtypeStruct((n_rows, c), F32), jax.ShapeDtypeStruct((8, c), F32)),
        compiler_params=_cparams("parallel", "arbitrary"),
    )(u, u, w8, b8, dout)


def _conv_bwd_in(dc, w8, name, into):
    n_rows, c = dc.shape
    tl = _rows(n_rows)
    cur, _, nxt, par = _conv_specs(n_rows, tl)
    n_l = n_rows // tl
    extra, extra_specs, aliases, col0, width = _into(into, 3, 0)
    out_spec = _conv_specs(n_rows, tl, col0)[0]

    def body(*refs):
        dc_ref, next_ref, w_ref = refs[:3]
        o_ref = refs[-1]
        cur_v = dc_ref[...]
        next8 = jnp.where(pl.program_id(1) == n_l - 1, 0.0, next_ref[...])
        row8 = lax.broadcasted_iota(jnp.int32, next8.shape, 0)
        acc = cur_v * w_ref[CONV_TAPS - 1:CONV_TAPS, :]
        for s in range(1, CONV_TAPS):
            acc = acc + _shift_up(cur_v, next8, s, row8) * w_ref[CONV_TAPS - 1 - s:CONV_TAPS - s, :]
        o_ref[...] = acc.astype(BF16)

    return pl.pallas_call(
        body, name=name, grid=(c // CONV_COLS, n_l), in_specs=[cur, nxt, par] + extra_specs, out_specs=out_spec,
        out_shape=jax.ShapeDtypeStruct((n_rows, width), BF16), input_output_aliases=aliases,
        compiler_params=_cparams("parallel", "parallel"),
    )(dc, dc, w8, *extra)


NORM_GROUP = SSD_D_INNER // SSD_GROUPS


def _gnorm_fwd(y, z, w, name):
    n_rows, c = y.shape
    z, z0, _ = _window(z)
    zoff = z0 // NORM_GROUP

    def body(y_ref, z_ref, w_ref, o_ref):
        zv = z_ref[...]
        yg = y_ref[...] * (zv * _sig(zv))
        r = lax.rsqrt(jnp.mean(yg * yg, axis=-1, keepdims=True) + RMS_EPS)
        o_ref[...] = (yg * r * w_ref[...]).astype(BF16)

    blk = pl.BlockSpec((_rows(n_rows),NORM_GROUP), lambda i, j: (i, j))
    zblk = pl.BlockSpec((_rows(n_rows),NORM_GROUP), lambda i, j: (i, j + zoff))
    wspec = pl.BlockSpec((1, NORM_GROUP), lambda i, j: (0, j))
    return pl.pallas_call(
        body, name=name, grid=(n_rows // _rows(n_rows), c // NORM_GROUP), in_specs=[blk, zblk, wspec], out_specs=blk,
        out_shape=jax.ShapeDtypeStruct((n_rows, c), BF16),
        compiler_params=_cparams("parallel", "parallel"),
    )(y, z, w.reshape(1, c))


def _gnorm_bwd(y, z, w, dyn, name, into):
    n_rows, c = y.shape
    z, z0, _ = _window(z)
    zoff = z0 // NORM_GROUP
    extra, extra_specs, aliases, col0, width = _into(into, 4, 1)
    doff = col0 // NORM_GROUP

    def body(*refs):
        y_ref, z_ref, w_ref, dn_ref = refs[:4]
        dy_ref, dz_ref, acc_ref = refs[-3:]
        @pl.when(pl.program_id(1) == 0)
        def _():
            acc_ref[...] = jnp.zeros_like(acc_ref)

        zv = z_ref[...]
        yv = y_ref[...]
        sz = _sig(zv)
        silu = zv * sz
        yg = yv * silu
        r = lax.rsqrt(jnp.mean(yg * yg, axis=-1, keepdims=True) + RMS_EPS)
        nrm = yg * r
        dn = dn_ref[...]
        acc_ref[0:1, :] += jnp.sum(dn * nrm, axis=0, keepdims=True)
        dnw = dn * w_ref[...]
        dyg = r * (dnw - nrm * jnp.mean(dnw * nrm, axis=-1, keepdims=True))
        dy_ref[...] = dyg * silu
        dz_ref[...] = (dyg * yv * (sz * (1.0 + zv * (1.0 - sz)))).astype(BF16)

    blk = pl.BlockSpec((_rows(n_rows),NORM_GROUP), lambda j, i: (i, j))
    zblk = pl.BlockSpec((_rows(n_rows),NORM_GROUP), lambda j, i: (i, j + zoff))
    wspec = pl.BlockSpec((1, NORM_GROUP), lambda j, i: (0, j))
    aspec = pl.BlockSpec((8, NORM_GROUP), lambda j, i: (0, j))
    return pl.pallas_call(
        body, name=name, grid=(c // NORM_GROUP, n_rows // _rows(n_rows)),
        in_specs=[blk, zblk, wspec, blk] + extra_specs,
        out_specs=(blk, pl.BlockSpec((_rows(n_rows), NORM_GROUP), lambda j, i: (i, j + doff)), aspec),
        out_shape=(jax.ShapeDtypeStruct((n_rows, c), F32), jax.ShapeDtypeStruct((n_rows, width), BF16),
                   jax.ShapeDtypeStruct((8, c), F32)),
        input_output_aliases=aliases,
        compiler_params=_cparams("parallel", "arbitrary"),
    )(y, z, w.reshape(1, c), dyn, *extra)


ATT_SCALE = ATT_HEAD_DIM ** -0.5
ATT_SLOPES = [2.0 ** (-8.0 * (h + 1) / ATT_HEADS) for h in range(ATT_HEADS)]
Q_PER_KV = ATT_HEADS // ATT_KV_HEADS


def _dup_half(t, g, lo):
    tr = pltpu.roll(t, ATT_HEAD_DIM, axis=1)
    return jnp.where(lo, t, tr) if g == 0 else jnp.where(lo, tr, t)


def _att_band(kv_ref, kvp_ref, n):
    cur = kv_ref[...]
    prev = jnp.where(n == 0, 0.0, kvp_ref[...])
    lo = lax.broadcasted_iota(jnp.int32, (ATT_BLOCK, LANE), 1) < ATT_HEAD_DIM
    bands = []
    for g in range(ATT_KV_HEADS):
        kb = jnp.concatenate([_dup_half(prev[:, :LANE], g, lo), _dup_half(cur[:, :LANE], g, lo)], axis=0)
        vb = jnp.concatenate([_dup_half(prev[:, LANE:], g, lo), _dup_half(cur[:, LANE:], g, lo)], axis=0)
        bands.append((kb.astype(BF16), vb.astype(BF16)))
    return bands


def _att_tile(n):
    shape = (2 * ATT_BLOCK, ATT_BLOCK)
    row = lax.broadcasted_iota(jnp.int32, shape, 0)
    i = row & (ATT_BLOCK - 1)
    s = lax.broadcasted_iota(jnp.int32, shape, 1)
    upper = s > i
    dist = ((i - s) & (ATT_BLOCK - 1)).astype(F32)
    dead = upper & (n == 0)
    return upper, dist, dead, row[:, 0:1] < ATT_BLOCK


def _stack_pair(t, lo):
    return jnp.concatenate([jnp.where(lo, t, 0.0), jnp.where(lo, 0.0, t)], axis=0).astype(BF16)


def _att_probs(qs, kb, s_ref, j, tile):
    upper, dist, dead, first = tile
    s2 = _dot(qs, kb, _NT)
    slope = jnp.where(first, ATT_SLOPES[2 * j], ATT_SLOPES[2 * j + 1])
    sink = jnp.where(first, s_ref[0:1, 2 * j:2 * j + 1], s_ref[0:1, 2 * j + 1:2 * j + 2])
    s = jnp.where(upper, s2[:, :ATT_BLOCK], s2[:, ATT_BLOCK:]) - slope * dist
    s = jnp.where(dead, NEG, s)
    m = jnp.maximum(jnp.max(s, axis=-1, keepdims=True), sink)
    p = jnp.exp(s - m)
    es = jnp.exp(sink - m)
    inv = 1.0 / (jnp.sum(p, axis=-1, keepdims=True) + es)
    return p * inv, es * inv


def _band_split(t, upper):
    return jnp.concatenate([jnp.where(upper, t, 0.0), jnp.where(upper, 0.0, t)], axis=1)


def _att_fwd(q, kv, sinks8, name):
    q, q0, _ = _window(q)
    kv, kv0, _ = _window(kv)
    qoff, kvoff = q0 // Q_DIM, kv0 // (2 * LANE)
    n_rows = q.shape[0]
    nb = n_rows // ATT_BLOCK

    def body(q_ref, kv_ref, kvp_ref, s_ref, o_ref):
        n = pl.program_id(0)
        bands = _att_band(kv_ref, kvp_ref, n)
        lo = lax.broadcasted_iota(jnp.int32, (ATT_BLOCK, LANE), 1) < ATT_HEAD_DIM
        tile = _att_tile(n)
        for j in range(ATT_HEADS // 2):
            kb, vb = bands[2 * j // Q_PER_KV]
            qs = _stack_pair(q_ref[:, j * LANE:(j + 1) * LANE] * ATT_SCALE, lo)
            p, _ = _att_probs(qs, kb, s_ref, j, tile)
            out = _dot(_band_split(p, tile[0]).astype(BF16), vb)
            o_ref[:, j * LANE:(j + 1) * LANE] = jnp.where(lo, out[:ATT_BLOCK], out[ATT_BLOCK:]).astype(BF16)

    return pl.pallas_call(
        body, name=name, grid=(nb,),
        in_specs=[pl.BlockSpec((ATT_BLOCK, Q_DIM), lambda n: (n, qoff)),
                  pl.BlockSpec((ATT_BLOCK, 2 * LANE), lambda n: (n, kvoff)),
                  pl.BlockSpec((ATT_BLOCK, 2 * LANE), lambda n: (jnp.maximum(n - 1, 0), kvoff)),
                  pl.BlockSpec((8, LANE), lambda n: (0, 0))],
        out_specs=pl.BlockSpec((ATT_BLOCK, Q_DIM), lambda n: (n, 0)),
        out_shape=jax.ShapeDtypeStruct((n_rows, Q_DIM), BF16),
        compiler_params=_cparams("parallel"),
    )(q, kv, kv, sinks8)


def _att_bwd(q, kv, sinks8, dout, name, into):
    q, q0, _ = _window(q)
    kv, kv0, _ = _window(kv)
    qoff, kvoff = q0 // Q_DIM, kv0 // (2 * LANE)
    n_rows = q.shape[0]
    nb = n_rows // ATT_BLOCK

    extra, extra_specs, aliases, col0, width = _into(into, 5, 0)
    dqoff = col0 // Q_DIM

    def body(*refs):
        q_ref, kv_ref, kvp_ref, s_ref, do_ref = refs[:5]
        dq_ref, dkv_ref, acc_ref, carry_ref = refs[-4:]
        n = pl.program_id(0)

        @pl.when(n == 0)
        def _():
            acc_ref[...] = jnp.zeros_like(acc_ref)
            carry_ref[...] = jnp.zeros_like(carry_ref)

        @pl.when(n == nb)
        def _():
            dkv_ref[...] = carry_ref[...].astype(BF16)

        @pl.when(n < nb)
        def _():
            bands = _att_band(kv_ref, kvp_ref, n)
            lo = lax.broadcasted_iota(jnp.int32, (ATT_BLOCK, LANE), 1) < ATT_HEAD_DIM
            lane1 = lax.broadcasted_iota(jnp.int32, (1, LANE), 1)
            tile = _att_tile(n)
            upper, first = tile[0], tile[3]
            dk_acc = [jnp.zeros((2 * ATT_BLOCK, LANE), F32) for _ in range(ATT_KV_HEADS)]
            dv_acc = [jnp.zeros((2 * ATT_BLOCK, LANE), F32) for _ in range(ATT_KV_HEADS)]
            dsink = jnp.zeros((1, LANE), F32)
            for j in range(ATT_HEADS // 2):
                g = 2 * j // Q_PER_KV
                kb, vb = bands[g]
                qs = _stack_pair(q_ref[:, j * LANE:(j + 1) * LANE] * ATT_SCALE, lo)
                dos = _stack_pair(do_ref[:, j * LANE:(j + 1) * LANE].astype(F32), lo)
                p, ps = _att_probs(qs, kb, s_ref, j, tile)
                dp2 = _dot(dos, vb, _NT)
                dp = jnp.where(upper, dp2[:, :ATT_BLOCK], dp2[:, ATT_BLOCK:])
                delta = jnp.sum(p * dp, axis=-1, keepdims=True)
                ds2 = _band_split(p * (dp - delta), upper)
                psd = ps * delta
                dsink = jnp.where(lane1 == 2 * j, -jnp.sum(jnp.where(first, psd, 0.0)), dsink)
                dsink = jnp.where(lane1 == 2 * j + 1, -jnp.sum(jnp.where(first, 0.0, psd)), dsink)
                dq = _dot(ds2.astype(BF16), kb) * ATT_SCALE
                dq_ref[:, j * LANE:(j + 1) * LANE] = jnp.where(lo, dq[:ATT_BLOCK], dq[ATT_BLOCK:]).astype(BF16)
                dk_acc[g] = dk_acc[g] + _dot(ds2.T.astype(BF16), qs)
                dv_acc[g] = dv_acc[g] + _dot(_band_split(p, upper).T.astype(BF16), dos)
            acc_ref[0:1, :] += dsink
            lo2 = lax.broadcasted_iota(jnp.int32, (2 * ATT_BLOCK, LANE), 1) < ATT_HEAD_DIM
            folded = []
            for acc in (dk_acc, dv_acc):
                t0 = acc[0] + pltpu.roll(acc[0], ATT_HEAD_DIM, axis=1)
                t1 = acc[1] + pltpu.roll(acc[1], ATT_HEAD_DIM, axis=1)
                folded.append(jnp.where(lo2, t0, t1))
            band = jnp.concatenate(folded, axis=1)
            dkv_ref[...] = (carry_ref[...] + band[:ATT_BLOCK]).astype(BF16)
            carry_ref[...] = band[ATT_BLOCK:]

    def qmap(n):
        return (jnp.minimum(n, nb - 1), 0)

    return pl.pallas_call(
        body, name=name, grid=(nb + 1,),
        in_specs=[pl.BlockSpec((ATT_BLOCK, Q_DIM), lambda n: (jnp.minimum(n, nb - 1), qoff)),
                  pl.BlockSpec((ATT_BLOCK, 2 * LANE), lambda n: (jnp.minimum(n, nb - 1), kvoff)),
                  pl.BlockSpec((ATT_BLOCK, 2 * LANE),
                               lambda n: (jnp.maximum(jnp.minimum(n, nb - 1) - 1, 0), kvoff)),
                  pl.BlockSpec((8, LANE), lambda n: (0, 0)),
                  pl.BlockSpec((ATT_BLOCK, Q_DIM), qmap)] + extra_specs,
        out_specs=(pl.BlockSpec((ATT_BLOCK, Q_DIM), lambda n: (jnp.minimum(n, nb - 1), dqoff)),
                   pl.BlockSpec((ATT_BLOCK, 2 * LANE), lambda n: (jnp.maximum(n - 1, 0), 0)),
                   pl.BlockSpec((8, LANE), lambda n: (0, 0))),
        out_shape=(jax.ShapeDtypeStruct((n_rows, width), BF16), jax.ShapeDtypeStruct((n_rows, 2 * LANE), BF16),
                   jax.ShapeDtypeStruct((8, LANE), F32)),
        input_output_aliases=aliases,
        scratch_shapes=[pltpu.VMEM((ATT_BLOCK, 2 * LANE), F32)],
        compiler_params=_cparams("arbitrary"),
    )(q, kv, kv, sinks8, dout, *extra)


HEADS_PER_GROUP = SSD_HEADS // SSD_GROUPS
PAIRS_PER_GROUP = HEADS_PER_GROUP // 2
T = SSD_CHUNK


def _ssd_scalars(dtr_ref, par_ref):
    dt = _softplus(dtr_ref[...] + par_ref[0:1, :])
    a = -jnp.exp(par_ref[1:2, :])
    ri = lax.broadcasted_iota(jnp.int32, (T, T), 0)
    ci = lax.broadcasted_iota(jnp.int32, (T, T), 1)
    tril = (ri >= ci).astype(F32)
    cs = _dot_hi(tril, dt * a)
    cst = cs.T
    return dt, a, cs, cst, ri, ci


def _ssd_stacked_masks():
    row = lax.broadcasted_iota(jnp.int32, (2 * T, T), 0)
    t = row & (T - 1)
    s = lax.broadcasted_iota(jnp.int32, (2 * T, T), 1)
    return t >= s, s >= t, row[:, 0:1] < T


def _col_s(arr, k0):
    return jnp.concatenate([arr[:, k0:k0 + 1], arr[:, k0 + 1:k0 + 2]], axis=0)


def _row_s(arr_t, k0, first):
    return jnp.where(first, arr_t[k0:k0 + 1, :], arr_t[k0 + 1:k0 + 2, :])


def _lane_pick(lo, arr, k0):
    return jnp.where(lo, arr[:, k0:k0 + 1], arr[:, k0 + 1:k0 + 2])


def _ssd_fwd_stacked(xs, bm, cm, dtr, par, name):
    dtr, dt0, _ = _window(dtr)
    dtoff = dt0 // LANE
    n_rows = xs.shape[0]
    nc = n_rows // T
    gw = PAIRS_PER_GROUP * LANE

    def body(x_ref, b_ref, c_ref, dtr_ref, par_ref, y_ref, hs_ref, h_ref):
        @pl.when(pl.program_id(1) == 0)
        def _():
            h_ref[...] = jnp.zeros_like(h_ref)

        dt, a, cs, cst, _, _ = _ssd_scalars(dtr_ref, par_ref)
        tri_s, _, first = _ssd_stacked_masks()
        lo = lax.broadcasted_iota(jnp.int32, (T, LANE), 1) < SSD_CHUNK // 2
        ecs = jnp.exp(cs)
        dect = jnp.exp(cst[:, T - 1:T] - cst)
        etot = jnp.exp(cs[T - 1:T, :])
        bg = b_ref[...]
        cg = c_ref[...]
        cb = _dot(cg.astype(BF16), bg.astype(BF16), _NT)
        cb_s = jnp.concatenate([cb, cb], axis=0)
        cg_s = jnp.concatenate([cg, cg], axis=0)
        bgt_s = jnp.concatenate([bg.T, bg.T], axis=0)
        for j in range(PAIRS_PER_GROUP):
            k0, k1 = 2 * j, 2 * j + 1
            xp = x_ref[:, j * LANE:(j + 1) * LANE]
            hp = h_ref[j]
            hs_ref[0, 0, j] = hp
            rhs = jnp.concatenate([(xp * _lane_pick(lo, dt, k0)).astype(BF16), hp.astype(BF16)], axis=0)
            lm_s = jnp.exp(jnp.where(tri_s, _col_s(cs, k0) - _row_s(cst, k0, first), NEG))
            lhs = jnp.concatenate([lm_s * cb_s, cg_s * _col_s(ecs, k0)], axis=1).astype(BF16)
            y_s = _dot(lhs, rhs)
            s_s = _dot((bgt_s * _row_s(dect, k0, first)).astype(BF16), rhs[:T])
            dsk = jnp.where(lo[0:1, :], par_ref[2:3, k0:k0 + 1], par_ref[2:3, k1:k1 + 1])
            y_ref[:, j * LANE:(j + 1) * LANE] = jnp.where(lo, y_s[:T], y_s[T:]) + dsk * xp
            et = jnp.where(lo[0:1, :], etot[:, k0:k0 + 1], etot[:, k1:k1 + 1])
            h_ref[j] = hp * et + jnp.where(lo, s_s[:T], s_s[T:])

    return pl.pallas_call(
        body, name=name, grid=(SSD_GROUPS, nc),
        in_specs=[pl.BlockSpec((T, gw), lambda g, c: (c, g)),
                  pl.BlockSpec((T, SSD_STATE), lambda g, c: (c, g)),
                  pl.BlockSpec((T, SSD_STATE), lambda g, c: (c, g)),
                  pl.BlockSpec((T, LANE), lambda g, c: (c, g + dtoff)),
                  pl.BlockSpec((8, LANE), lambda g, c: (0, g))],
        out_specs=(pl.BlockSpec((T, gw), lambda g, c: (c, g)),
                   pl.BlockSpec((1, 1, PAIRS_PER_GROUP, SSD_STATE, LANE), lambda g, c: (g, c, 0, 0, 0))),
        out_shape=(jax.ShapeDtypeStruct((n_rows, SSD_D_INNER), F32),
                   jax.ShapeDtypeStruct((SSD_GROUPS, nc, PAIRS_PER_GROUP, SSD_STATE, LANE), F32)),
        scratch_shapes=[pltpu.VMEM((PAIRS_PER_GROUP, SSD_STATE, LANE), F32)],
        compiler_params=_cparams("parallel", "arbitrary"),
    )(xs, bm, cm, dtr, par)


def _ssd_bwd_stacked(xs, bm, cm, dtr, par, hs, dy, name, into):
    dtr, dt0, _ = _window(dtr)
    dtoff = dt0 // LANE
    n_rows = xs.shape[0]
    nc = n_rows // T
    gw = PAIRS_PER_GROUP * LANE

    extra, extra_specs, aliases, col0, width = _into(into, 7, 3)
    ddoff = col0 // LANE

    def body(*refs):
        x_ref, b_ref, c_ref, dtr_ref, par_ref, hs_ref, dy_ref = refs[:7]
        dx_ref, db_ref, dc_ref, ddtr_ref, acc_ref, dh_ref = refs[-6:]
        @pl.when(pl.program_id(1) == 0)
        def _():
            dh_ref[...] = jnp.zeros_like(dh_ref)
            acc_ref[...] = jnp.zeros_like(acc_ref)

        dt, a, cs, cst, ri, ci = _ssd_scalars(dtr_ref, par_ref)
        tri_s, trit_s, first = _ssd_stacked_masks()
        lane = lax.broadcasted_iota(jnp.int32, (T, LANE), 1)
        lo = lane < SSD_CHUNK // 2
        lane1 = lane[0:1, :]
        ecs = jnp.exp(cs)
        ecst = jnp.exp(cst)
        dec = jnp.exp(cs[T - 1:T, :] - cs)
        etot = jnp.exp(cs[T - 1:T, :])
        bg = b_ref[...]
        cg = c_ref[...]
        bg_b = bg.astype(BF16)
        cg_b = cg.astype(BF16)
        cb = _dot(cg_b, bg_b, _NT)
        cbt = _dot(bg_b, cg_b, _NT)
        cb_s = jnp.concatenate([cb, cb], axis=0)
        cbt_s = jnp.concatenate([cbt, cbt], axis=0)
        bg_s = jnp.concatenate([bg, bg], axis=0)
        cg_s = jnp.concatenate([cg, cg], axis=0)
        cgt_s = jnp.concatenate([cg.T, cg.T], axis=0)
        dbg = jnp.zeros((T, SSD_STATE), F32)
        dcg = jnp.zeros((T, SSD_STATE), F32)
        dcs_acc = jnp.zeros((T, LANE), F32)
        ddt_acc = jnp.zeros((T, LANE), F32)
        dsk_acc = jnp.zeros((1, LANE), F32)
        last_row = lax.broadcasted_iota(jnp.int32, (T, 1), 0) == T - 1
        for j in range(PAIRS_PER_GROUP):
            k0, k1 = 2 * j, 2 * j + 1
            xp = x_ref[:, j * LANE:(j + 1) * LANE]
            dtl = _lane_pick(lo, dt, k0)
            xdt = xp * dtl
            hp = hs_ref[0, 0, j]
            dhn = dh_ref[j]
            dyp = dy_ref[:, j * LANE:(j + 1) * LANE]
            xdt_b, hp_b, dhn_b, dyp_b = (v.astype(BF16) for v in (xdt, hp, dhn, dyp))
            cs_c, cs_r = _col_s(cs, k0), _row_s(cst, k0, first)
            lm_s = jnp.exp(jnp.where(tri_s, cs_c - cs_r, NEG))
            lmt_s = jnp.exp(jnp.where(trit_s, cs_r - cs_c, NEG))
            dec_c, ecs_c = _col_s(dec, k0), _col_s(ecs, k0)
            r1 = _dot(_stack_pair(dyp, lo), jnp.concatenate([xdt_b, hp_b], axis=0), _NT)
            r2 = _dot(_stack_pair(xdt, lo), jnp.concatenate([dyp_b, dhn_b], axis=0), _NT)
            dm_s, dyh_s = r1[:, :T], r1[:, T:]
            dmt_s, xdh_s = r2[:, :T], r2[:, T:]
            mm_s = lm_s * cb_s
            mmt_s = lmt_s * cbt_s
            bdec_s = bg_s * dec_c
            cexp_s = cg_s * ecs_c
            dx_s = _dot(jnp.concatenate([mmt_s, bdec_s], axis=1).astype(BF16),
                        jnp.concatenate([dyp_b, dhn_b], axis=0))
            dxdt = jnp.where(lo, dx_s[:T], dx_s[T:])
            dc_s = _dot((dm_s * lm_s).astype(BF16), bg_b) + dyh_s * ecs_c
            db_s = _dot((dmt_s * lmt_s).astype(BF16), cg_b) + xdh_s * dec_c
            dcg = dcg + dc_s[:T] + dc_s[T:]
            dbg = dbg + db_s[:T] + db_s[T:]
            dh_s = _dot((cgt_s * _row_s(ecst, k0, first)).astype(BF16), dyp_b)
            et = jnp.where(lo[0:1, :], etot[:, k0:k0 + 1], etot[:, k1:k1 + 1])
            dh_ref[j] = dhn * et + jnp.where(lo, dh_s[:T], dh_s[T:])
            e4 = jnp.sum(bdec_s * xdh_s, axis=-1, keepdims=True)
            dcs_s = (jnp.sum(dm_s * mm_s, axis=-1, keepdims=True) - jnp.sum(dmt_s * mmt_s, axis=-1, keepdims=True)
                     + jnp.sum(cexp_s * dyh_s, axis=-1, keepdims=True) - e4)
            hd = hp * dhn
            tsum0 = jnp.sum(e4[:T]) + etot[:, k0:k0 + 1] * jnp.sum(jnp.where(lo, hd, 0.0))
            tsum1 = jnp.sum(e4[T:]) + etot[:, k1:k1 + 1] * jnp.sum(jnp.where(lo, 0.0, hd))
            dcs0 = dcs_s[:T] + jnp.where(last_row, tsum0, 0.0)
            dcs1 = dcs_s[T:] + jnp.where(last_row, tsum1, 0.0)
            dcs_acc = jnp.where(lane == k0, dcs0, jnp.where(lane == k1, dcs1, dcs_acc))
            prod = dxdt * xp
            ddt_lo = jnp.sum(jnp.where(lo, prod, 0.0), axis=-1, keepdims=True)
            ddt_hi = jnp.sum(jnp.where(lo, 0.0, prod), axis=-1, keepdims=True)
            ddt_acc = jnp.where(lane == k0, ddt_lo, jnp.where(lane == k1, ddt_hi, ddt_acc))
            dyx = dyp * xp
            dsk_acc = jnp.where(lane1 == k0, jnp.sum(jnp.where(lo, dyx, 0.0)),
                                jnp.where(lane1 == k1, jnp.sum(jnp.where(lo, 0.0, dyx)), dsk_acc))
            dsk = jnp.where(lo[0:1, :], par_ref[2:3, k0:k0 + 1], par_ref[2:3, k1:k1 + 1])
            dx_ref[:, j * LANE:(j + 1) * LANE] = dxdt * dtl + dsk * dyp
        db_ref[...] = dbg
        dc_ref[...] = dcg
        triu = (ci >= ri).astype(F32)
        dda = _dot_hi(triu, dcs_acc)
        ddt = ddt_acc + dda * a
        ddtr = ddt * _sig(dtr_ref[...] + par_ref[0:1, :])
        ddtr_ref[...] = ddtr.astype(BF16)
        acc_ref[0:1, :] += jnp.sum(ddtr, axis=0, keepdims=True)
        acc_ref[1:2, :] += jnp.sum(dda * dt, axis=0, keepdims=True) * a
        acc_ref[2:3, :] += dsk_acc

    def rev(g, c):
        return (nc - 1 - c, g)

    return pl.pallas_call(
        body, name=name, grid=(SSD_GROUPS, nc),
        in_specs=[pl.BlockSpec((T, gw), rev),
                  pl.BlockSpec((T, SSD_STATE), rev),
                  pl.BlockSpec((T, SSD_STATE), rev),
                  pl.BlockSpec((T, LANE), lambda g, c: (nc - 1 - c, g + dtoff)),
                  pl.BlockSpec((8, LANE), lambda g, c: (0, g)),
                  pl.BlockSpec((1, 1, PAIRS_PER_GROUP, SSD_STATE, LANE), lambda g, c: (g, nc - 1 - c, 0, 0, 0)),
                  pl.BlockSpec((T, gw), rev)] + extra_specs,
        out_specs=(pl.BlockSpec((T, gw), rev),
                   pl.BlockSpec((T, SSD_STATE), rev),
                   pl.BlockSpec((T, SSD_STATE), rev),
                   pl.BlockSpec((T, LANE), lambda g, c: (nc - 1 - c, g + ddoff)),
                   pl.BlockSpec((8, LANE), lambda g, c: (0, g))),
        out_shape=(jax.ShapeDtypeStruct((n_rows, SSD_D_INNER), F32),
                   jax.ShapeDtypeStruct((n_rows, BC_DIM), F32),
                   jax.ShapeDtypeStruct((n_rows, BC_DIM), F32),
                   jax.ShapeDtypeStruct((n_rows, width), BF16),
                   jax.ShapeDtypeStruct((8, DT_PAD), F32)),
        input_output_aliases=aliases,
        scratch_shapes=[pltpu.VMEM((PAIRS_PER_GROUP, SSD_STATE, LANE), F32)],
        compiler_params=_cparams("parallel", "arbitrary"),
    )(xs, bm, cm, dtr, par, hs, dy, *extra)


def _cumsum_mm(mat, x):
    hi = x.astype(BF16)
    r = x - hi.astype(F32)
    mid = r.astype(BF16)
    lo = (r - mid.astype(F32)).astype(BF16)
    w = x.shape[1]
    out = _dot(mat, jnp.concatenate([hi, mid, lo], axis=1))
    return out[:, :w] + out[:, w:2 * w] + out[:, 2 * w:]


def _ssd_prep(dtr_ref, par_ref):
    dt = _softplus(dtr_ref[...] + par_ref[0:1, :])
    a = -jnp.exp(par_ref[1:2, :])
    ri = lax.broadcasted_iota(jnp.int32, (T, T), 0)
    ci = lax.broadcasted_iota(jnp.int32, (T, T), 1)
    cs = _cumsum_mm((ri >= ci).astype(BF16), dt * a)
    lo = lax.broadcasted_iota(jnp.int32, (T, LANE), 1) < SSD_CHUNK // 2

    def expand(arr):
        rows = arr.shape[0]
        return jnp.concatenate([jnp.where(lo[:rows], arr[:, 2 * j:2 * j + 1], arr[:, 2 * j + 1:2 * j + 2])
                                for j in range(PAIRS_PER_GROUP)], axis=1)

    tot = cs[T - 1:T, :]
    return {"dt": dt, "a": a, "cs": cs, "cst": cs.T, "lo": lo, "ri": ri, "ci": ci, "expand": expand,
            "dt_x": expand(dt), "ecs_x": expand(jnp.exp(cs)), "dec_x": expand(jnp.exp(tot - cs)),
            "et_x": expand(jnp.exp(tot)), "etot": jnp.exp(tot), "dsk_x": expand(par_ref[2:3, :])}


def _wide_masks():
    r = lax.broadcasted_iota(jnp.int32, (T, 2 * T), 0)
    l = lax.broadcasted_iota(jnp.int32, (T, 2 * T), 1)
    s = l & (T - 1)
    return r >= s, s >= r, l < T


def _wide_cs(q, k0, even):
    cs, cst = q["cs"], q["cst"]
    col = jnp.where(even, cs[:, k0:k0 + 1], cs[:, k0 + 1:k0 + 2])
    row = jnp.concatenate([cst[k0:k0 + 1, :], cst[k0 + 1:k0 + 2, :]], axis=1)
    return col, row


def _ssd_fwd(xs, bm, cm, dtr, par, name):
    dtr, dt0, _ = _window(dtr)
    dtoff = dt0 // LANE
    n_rows = xs.shape[0]
    nc = n_rows // T
    gw = PAIRS_PER_GROUP * LANE

    def body(x_ref, b_ref, c_ref, dtr_ref, par_ref, y_ref, hs_ref, h_ref):
        @pl.when(pl.program_id(1) == 0)
        def _():
            h_ref[...] = jnp.zeros_like(h_ref)

        q = _ssd_prep(dtr_ref, par_ref)
        lo = q["lo"]
        tri_w, _, even = _wide_masks()
        bg_b = b_ref[...].astype(BF16)
        cg_b = c_ref[...].astype(BF16)
        xv = x_ref[...]
        xdt = xv * q["dt_x"]
        h = h_ref[...]
        hs_ref[0, 0] = h
        yo = q["ecs_x"] * _dot(cg_b, h.astype(BF16))
        h_ref[...] = h * q["et_x"] + _dot(b_ref[...].T.astype(BF16), (xdt * q["dec_x"]).astype(BF16))
        cb = _dot(cg_b, bg_b, _NT)
        cb_w = jnp.concatenate([cb, cb], axis=1)
        for j in range(PAIRS_PER_GROUP):
            col, row = _wide_cs(q, 2 * j, even)
            m_w = (jnp.exp(jnp.where(tri_w, col - row, NEG)) * cb_w).astype(BF16)
            sl = slice(j * LANE, (j + 1) * LANE)
            y_ref[:, sl] = (_dot(m_w, _stack_pair(xdt[:, sl], lo)) + yo[:, sl] + q["dsk_x"][:, sl] * xv[:, sl])

    return pl.pallas_call(
        body, name=name, grid=(SSD_GROUPS, nc),
        in_specs=[pl.BlockSpec((T, gw), lambda g, c: (c, g)),
                  pl.BlockSpec((T, SSD_STATE), lambda g, c: (c, g)),
                  pl.BlockSpec((T, SSD_STATE), lambda g, c: (c, g)),
                  pl.BlockSpec((T, LANE), lambda g, c: (c, g + dtoff)),
                  pl.BlockSpec((8, LANE), lambda g, c: (0, g))],
        out_specs=(pl.BlockSpec((T, gw), lambda g, c: (c, g)),
                   pl.BlockSpec((1, 1, SSD_STATE, gw), lambda g, c: (g, c, 0, 0))),
        out_shape=(jax.ShapeDtypeStruct((n_rows, SSD_D_INNER), F32),
                   jax.ShapeDtypeStruct((SSD_GROUPS, nc, SSD_STATE, gw), F32)),
        scratch_shapes=[pltpu.VMEM((SSD_STATE, gw), F32)],
        compiler_params=_cparams("parallel", "arbitrary"),
    )(xs, bm, cm, dtr, par)


def _ssd_bwd(xs, bm, cm, dtr, par, hs, dy, name, into):
    dtr, dt0, _ = _window(dtr)
    dtoff = dt0 // LANE
    n_rows = xs.shape[0]
    nc = n_rows // T
    gw = PAIRS_PER_GROUP * LANE
    extra, extra_specs, aliases, col0, width = _into(into, 7, 3)
    ddoff = col0 // LANE

    def body(*refs):
        x_ref, b_ref, c_ref, dtr_ref, par_ref, hs_ref, dy_ref = refs[:7]
        dx_ref, db_ref, dc_ref, ddtr_ref, acc_ref, dh_ref = refs[-6:]

        @pl.when(pl.program_id(1) == 0)
        def _():
            dh_ref[...] = jnp.zeros_like(dh_ref)
            acc_ref[...] = jnp.zeros_like(acc_ref)

        q = _ssd_prep(dtr_ref, par_ref)
        lo, dt, a = q["lo"], q["dt"], q["a"]
        tri_w, trit_w, even = _wide_masks()
        lane = lax.broadcasted_iota(jnp.int32, (T, LANE), 1)
        lane1 = lane[0:1, :]
        last_row = lax.broadcasted_iota(jnp.int32, (T, 1), 0) == T - 1
        bg_b = b_ref[...].astype(BF16)
        cg_b = c_ref[...].astype(BF16)
        xv = x_ref[...]
        dyv = dy_ref[...]
        xdt = xv * q["dt_x"]
        h = hs_ref[0, 0]
        dhn = dh_ref[...]
        h_b, dhn_b = h.astype(BF16), dhn.astype(BF16)
        yo = q["ecs_x"] * _dot(cg_b, h_b)
        bdh = q["dec_x"] * _dot(bg_b, dhn_b)
        dye = (dyv * q["ecs_x"]).astype(BF16)
        xd = (xdt * q["dec_x"]).astype(BF16)
        dcg = _dot(dye, h_b, _NT)
        dbg = _dot(xd, dhn_b, _NT)
        dh_ref[...] = dhn * q["et_x"] + _dot(c_ref[...].T.astype(BF16), dye)
        e3_all = dyv * yo
        e4_all = xdt * bdh
        hsum = jnp.sum(h * dhn, axis=0, keepdims=True)
        dyx = dyv * xv
        cb = _dot(cg_b, bg_b, _NT)
        cbt = _dot(bg_b, cg_b, _NT)
        cb_w = jnp.concatenate([cb, cb], axis=1)
        cbt_w = jnp.concatenate([cbt, cbt], axis=1)
        dcb = jnp.zeros((T, T), F32)
        dcbt = jnp.zeros((T, T), F32)
        dcs_acc = jnp.zeros((T, LANE), F32)
        ddt_acc = jnp.zeros((T, LANE), F32)
        dsk_acc = jnp.zeros((1, LANE), F32)

        def halves(t):
            m = lo[:t.shape[0]]
            return (jnp.sum(jnp.where(m, t, 0.0), axis=-1, keepdims=True),
                    jnp.sum(jnp.where(m, 0.0, t), axis=-1, keepdims=True))

        for j in range(PAIRS_PER_GROUP):
            k0, k1 = 2 * j, 2 * j + 1
            sl = slice(j * LANE, (j + 1) * LANE)
            col, row = _wide_cs(q, k0, even)
            lm_w = jnp.exp(jnp.where(tri_w, col - row, NEG))
            lmt_w = jnp.exp(jnp.where(trit_w, row - col, NEG))
            dyp, xp = dyv[:, sl], xdt[:, sl]
            dym, xm = _stack_pair(dyp, lo), _stack_pair(xp, lo)
            dm_w = _dot(dyp.astype(BF16), xm, _NT)
            dmt_w = _dot(xp.astype(BF16), dym, _NT)
            mm_w = lm_w * cb_w
            mmt_w = lmt_w * cbt_w
            dxdt = _dot(mmt_w.astype(BF16), dym) + bdh[:, sl]
            g1 = dm_w * lm_w
            g2 = dmt_w * lmt_w
            dcb = dcb + g1[:, :T] + g1[:, T:]
            dcbt = dcbt + g2[:, :T] + g2[:, T:]
            e1 = dm_w * mm_w
            e2 = dmt_w * mmt_w
            e3 = halves(e3_all[:, sl])
            e4 = halves(e4_all[:, sl])
            hs2 = halves(hsum[:, sl])
            for half, k in ((0, k0), (1, k1)):
                hsl = slice(half * T, (half + 1) * T)
                tsum = jnp.sum(e4[half]) + q["etot"][:, k:k + 1] * hs2[half]
                dcs_h = (jnp.sum(e1[:, hsl], axis=-1, keepdims=True) - jnp.sum(e2[:, hsl], axis=-1, keepdims=True)
                         + e3[half] - e4[half] + jnp.where(last_row, tsum, 0.0))
                dcs_acc = jnp.where(lane == k, dcs_h, dcs_acc)
            ddt2 = halves(dxdt * xv[:, sl])
            ddt_acc = jnp.where(lane == k0, ddt2[0], jnp.where(lane == k1, ddt2[1], ddt_acc))
            dsk2 = halves(jnp.sum(dyx[:, sl], axis=0, keepdims=True))
            dsk_acc = jnp.where(lane1 == k0, dsk2[0], jnp.where(lane1 == k1, dsk2[1], dsk_acc))
            dx_ref[:, sl] = dxdt * q["dt_x"][:, sl] + q["dsk_x"][:, sl] * dyp
        dc_ref[...] = dcg + _dot(dcb.astype(BF16), bg_b)
        db_ref[...] = dbg + _dot(dcbt.astype(BF16), cg_b)
        dda = _cumsum_mm((q["ci"] >= q["ri"]).astype(BF16), dcs_acc)
        ddt = ddt_acc + dda * a
        ddtr = ddt * _sig(dtr_ref[...] + par_ref[0:1, :])
        ddtr_ref[...] = ddtr.astype(BF16)
        acc_ref[0:1, :] += jnp.sum(ddtr, axis=0, keepdims=True)
        acc_ref[1:2, :] += jnp.sum(dda * dt, axis=0, keepdims=True) * a
        acc_ref[2:3, :] += dsk_acc

    def rev(g, c):
        return (nc - 1 - c, g)

    return pl.pallas_call(
        body, name=name, grid=(SSD_GROUPS, nc),
        in_specs=[pl.BlockSpec((T, gw), rev),
                  pl.BlockSpec((T, SSD_STATE), rev),
                  pl.BlockSpec((T, SSD_STATE), rev),
                  pl.BlockSpec((T, LANE), lambda g, c: (nc - 1 - c, g + dtoff)),
                  pl.BlockSpec((8, LANE), lambda g, c: (0, g)),
                  pl.BlockSpec((1, 1, SSD_STATE, gw), lambda g, c: (g, nc - 1 - c, 0, 0)),
                  pl.BlockSpec((T, gw), rev)] + extra_specs,
        out_specs=(pl.BlockSpec((T, gw), rev),
                   pl.BlockSpec((T, SSD_STATE), rev),
                   pl.BlockSpec((T, SSD_STATE), rev),
                   pl.BlockSpec((T, LANE), lambda g, c: (nc - 1 - c, g + ddoff)),
                   pl.BlockSpec((8, LANE), lambda g, c: (0, g))),
        out_shape=(jax.ShapeDtypeStruct((n_rows, SSD_D_INNER), F32),
                   jax.ShapeDtypeStruct((n_rows, BC_DIM), F32),
                   jax.ShapeDtypeStruct((n_rows, BC_DIM), F32),
                   jax.ShapeDtypeStruct((n_rows, width), BF16),
                   jax.ShapeDtypeStruct((8, DT_PAD), F32)),
        input_output_aliases=aliases,
        scratch_shapes=[pltpu.VMEM((SSD_STATE, gw), F32)],
        compiler_params=_cparams("parallel", "arbitrary"),
    )(xs, bm, cm, dtr, par, hs, dy, *extra)


ADAM_ROWS = 256


def _adamw(lands, w, m, v, name):
    na = len(lands)
    n_slots, r, wd = lands[0].shape
    tr = r if r <= 2 * ADAM_ROWS else ADAM_ROWS
    nj = r // tr
    bc1 = 1.0 - ADAM_B1 ** ADAM_STEP
    bc2 = 1.0 - ADAM_B2 ** ADAM_STEP

    def body(*refs):
        l_refs = refs[:na]
        w_ref, m_ref, v_ref, g_ref, d_ref, nm_ref, nv_ref = refs[na:]
        for a in range(na):
            @pl.when(pl.program_id(0) == a)
            def _(l_ref=l_refs[a]):
                g = l_ref[0].astype(F32)
                for s in range(1, n_slots):
                    g = g + l_ref[s].astype(F32)
                mn = ADAM_B1 * m_ref[0] + (1.0 - ADAM_B1) * g
                vn = ADAM_B2 * v_ref[0] + (1.0 - ADAM_B2) * (g * g)
                mh = mn / bc1
                vh = vn / bc2
                g_ref[0] = g
                nm_ref[0] = mn
                nv_ref[0] = vn
                d_ref[0] = -ADAM_LR * (mh / (jnp.sqrt(vh) + ADAM_EPS) + ADAM_WD * w_ref[0])

    def land_spec(a):
        return pl.BlockSpec((n_slots, tr, wd),
                            lambda i, j: (0, jnp.where(i == a, j, jnp.where(i < a, 0, nj - 1)), 0))

    blk = pl.BlockSpec((1, tr, wd), lambda i, j: (i, j, 0))
    shp = jax.ShapeDtypeStruct((na, r, wd), F32)
    return pl.pallas_call(
        body, name=name, grid=(na, nj), in_specs=[land_spec(a) for a in range(na)] + [blk, blk, blk],
        out_specs=(blk, blk, blk, blk), out_shape=(shp, shp, shp, shp),
        compiler_params=_cparams("arbitrary", "arbitrary"),
    )(*lands, w, m, v)


def _mesh_pos():
    return lax.axis_index("x"), lax.axis_index("y"), lax.axis_index("c")


def _peer(pos, k):
    x, y, c = pos
    px = 1 - x if (k >> 2) & 1 else x
    py = 1 - y if (k >> 1) & 1 else y
    pc = 1 - c if k & 1 else c
    return px, py, pc


def _flat(pos):
    return 4 * pos[0] + 2 * pos[1] + pos[2]


HBM_SPEC = pl.BlockSpec(memory_space=pl.ANY)


ROW_SHARDED = ("w_ssd_out", "w_att_out", "w_mix_out", "w_ffn_down")
COL_SHARDED = ("w_in", "w_ffn_gate", "w_ffn_up")
GATHERED = ROW_SHARDED + COL_SHARDED + ("conv_w",)
BIG = ROW_SHARDED + COL_SHARDED


SEM_SPEC = pl.BlockSpec(memory_space=pltpu.SEMAPHORE)
TOKEN = jax.ShapeDtypeStruct((8, LANE), F32)
SPLIT_EFFECT = pltpu.SideEffectType.DATAFLOW_SIDE_EFFECTING
GATHER_ROWS = "gather_rows"
GATHER_SLOT = "gather_slot"
SCATTER_ROWS = "scatter_rows"
SCATTER_SLOT = "scatter_slot"


def _land_shape(kind, src):
    if kind == GATHER_ROWS:
        return (N_DEV * src.shape[0],) + src.shape[1:]
    if kind == GATHER_SLOT:
        return (N_DEV,) + src.shape
    if kind == SCATTER_ROWS:
        return (N_DEV, src.shape[0] // N_DEV) + src.shape[1:]
    return src.shape


def _views(kind, src_ref, land_ref, pos, k):
    me = _flat(pos)
    if kind == GATHER_ROWS:
        r = src_ref.shape[0]
        return src_ref, land_ref.at[pl.ds(pl.multiple_of(me * r, 16), r), :]
    if kind == GATHER_SLOT:
        return src_ref, land_ref.at[me]
    dev = _flat(_peer(pos, k))
    if kind == SCATTER_ROWS:
        r = land_ref.shape[1]
        return src_ref.at[pl.ds(pl.multiple_of(dev * r, 16), r), :], land_ref.at[k]
    return src_ref.at[dev], land_ref.at[k]


def _hbm(x):
    return pltpu.with_memory_space_constraint(x, pltpu.HBM)


def _exchange_start(items, after, name):
    kinds = [k for k, _ in items]
    srcs = [_hbm(s) for _, s in items]
    lands = [_hbm(lax.empty(_land_shape(k, s), s.dtype)) for k, s in items]
    n = len(items)
    n_copy = n * (N_DEV - 1)

    def body(*refs):
        src_refs, land_refs = refs[:n], refs[n:2 * n]
        send_sems, recv_sems = refs[2 * n + 1], refs[2 * n + 2]
        token_ref = refs[4 * n + 3]
        pos = _mesh_pos()
        for i, kind in enumerate(kinds):
            for k in range(1, N_DEV):
                s, d = _views(kind, src_refs[i], land_refs[i], pos, k)
                j = i * (N_DEV - 1) + k - 1
                pltpu.make_async_remote_copy(src_ref=s, dst_ref=d, send_sem=send_sems.at[j], recv_sem=recv_sems.at[j],
                                             device_id=_peer(pos, k), device_id_type=MESH_ID).start()
        token_ref[...] = jnp.zeros_like(token_ref)

    arrs = srcs + lands
    outs = pl.pallas_call(
        body, name=name,
        in_specs=[HBM_SPEC] * (2 * n + 1),
        out_specs=[SEM_SPEC, SEM_SPEC] + [HBM_SPEC] * (2 * n) + [pl.BlockSpec(memory_space=pltpu.VMEM)],
        out_shape=[pltpu.SemaphoreType.DMA((n_copy,)), pltpu.SemaphoreType.DMA((n_copy,))]
        + [pltpu.HBM(a.shape, a.dtype) for a in arrs] + [TOKEN],
        input_output_aliases={i: 2 + i for i in range(2 * n)},
        compiler_params=pltpu.CompilerParams(has_side_effects=SPLIT_EFFECT),
    )(*arrs, after)
    return {"kinds": kinds, "send": outs[0], "recv": outs[1], "arrs": outs[2:2 + 2 * n], "token": outs[-1]}


def _exchange_wait(ex, after, name):
    kinds = ex["kinds"]
    n = len(kinds)

    def body(*refs):
        src_refs, land_refs = refs[:n], refs[n:2 * n]
        send_sems, recv_sems = refs[2 * n], refs[2 * n + 1]
        token_ref = refs[-1]
        pos = _mesh_pos()
        for i, kind in enumerate(kinds):
            for k in range(1, N_DEV):
                s, d = _views(kind, src_refs[i], land_refs[i], pos, k)
                j = i * (N_DEV - 1) + k - 1
                cp = pltpu.make_async_remote_copy(src_ref=s, dst_ref=d, send_sem=send_sems.at[j],
                                                  recv_sem=recv_sems.at[j], device_id=_peer(pos, k),
                                                  device_id_type=MESH_ID)
                cp.wait_send()
                cp.wait_recv()
        token_ref[...] = jnp.zeros_like(token_ref)

    outs = pl.pallas_call(
        body, name=name,
        in_specs=[HBM_SPEC] * (2 * n) + [SEM_SPEC, SEM_SPEC, HBM_SPEC],
        out_specs=[HBM_SPEC] * (2 * n) + [pl.BlockSpec(memory_space=pltpu.VMEM)],
        out_shape=[pltpu.HBM(a.shape, a.dtype) for a in ex["arrs"]] + [TOKEN],
        input_output_aliases={i: i for i in range(2 * n)},
        compiler_params=pltpu.CompilerParams(has_side_effects=SPLIT_EFFECT),
    )(*ex["arrs"], ex["send"], ex["recv"], after)
    lands = [_place_own(k, s, d) for k, s, d in zip(kinds, outs[:n], outs[n:2 * n])]
    return lands, outs[-1]


def _place_own(kind, src, land):
    me = _flat(_mesh_pos())
    zeros = (0,) * (src.ndim - 1)
    if kind == GATHER_ROWS:
        return lax.dynamic_update_slice(land, src, (me * src.shape[0],) + zeros)
    if kind == GATHER_SLOT:
        return lax.dynamic_update_slice(land, src[None], (me,) + (0,) * src.ndim)
    if kind == SCATTER_ROWS:
        r = land.shape[1]
        own = lax.dynamic_slice(src, (me * r,) + zeros, (r,) + src.shape[1:])
    else:
        own = lax.dynamic_index_in_dim(src, me, 0, keepdims=False)
    return lax.dynamic_update_slice(land, own[None], (0,) * land.ndim)


def _all_gather_small(x, name):
    r, w = x.shape

    def body(x_ref, out_ref, send_sems, recv_sems):
        pos = _mesh_pos()
        me = _flat(pos)
        copies = []
        for k in range(1, N_DEV):
            cp = pltpu.make_async_remote_copy(
                src_ref=x_ref, dst_ref=out_ref.at[me], send_sem=send_sems.at[k - 1], recv_sem=recv_sems.at[k - 1],
                device_id=_peer(pos, k), device_id_type=MESH_ID)
            cp.start()
            copies.append(cp)
        out_ref[me] = x_ref[...]
        for cp in copies:
            cp.wait()

    vmem = pl.BlockSpec(memory_space=pltpu.VMEM)
    return pl.pallas_call(
        body, name=name, in_specs=[vmem], out_specs=vmem,
        out_shape=jax.ShapeDtypeStruct((N_DEV, r, w), x.dtype),
        scratch_shapes=[pltpu.SemaphoreType.DMA((N_DEV - 1,)), pltpu.SemaphoreType.DMA((N_DEV - 1,))],
        compiler_params=pltpu.CompilerParams(has_side_effects=True),
    )(x)


def _cols(g, lo, hi):
    c = g.shape[-1]
    parts = []
    for d in range(N_DEV):
        a, b = max(lo, d * c), min(hi, (d + 1) * c)
        if a < b:
            parts.append(g[d, :, a - d * c:b - d * c])
    return parts[0] if len(parts) == 1 else jnp.concatenate(parts, axis=1)


def _col_chunks(g):
    c = g.shape[-1] // N_DEV
    return jnp.stack([g[:, d * c:(d + 1) * c] for d in range(N_DEV)])


IN_PART = ("w_in", "conv_w")
OUT_PART = ROW_SHARDED + ("w_ffn_gate", "w_ffn_up")


def _gather_items(w, names, l):
    items = []
    for n in names:
        blk = w[n][l] if n == "conv_w" else w[n][l].astype(BF16)
        items.append((GATHER_ROWS if n in ROW_SHARDED else GATHER_SLOT, blk))
    return items


def _scatter_items(grads, names):
    return [(SCATTER_ROWS, grads[n]) if n in ROW_SHARDED else (SCATTER_SLOT, _col_chunks(grads[n]))
            for n in names]


SMALL = ("ln_in_g", "ln_in_b", "conv_b", "dt_bias", "a_log", "d_skip", "ssd_norm_w", "att_sinks",
         "ln_mix_g", "ln_mix_b", "ln_ffn_g", "ln_ffn_b")


def _pack_small(vals):
    flat = jnp.concatenate([vals[n].reshape(-1) for n in SMALL])
    n = flat.shape[0]
    rows = -(-n // LANE)
    rows = -(-rows // 8) * 8
    return jnp.pad(flat, (0, rows * LANE - n)).reshape(rows, LANE)


def _unpack_small(buf, shapes):
    flat = buf.reshape(-1)
    off = 0
    out = {}
    for n in SMALL:
        cnt = math.prod(shapes[n])
        out[n] = flat[off:off + cnt].reshape(shapes[n])
        off += cnt
    return out


def _to_group_major(v):
    lead = v.shape[:-1]
    t = v.reshape(lead + (SSD_GROUPS, HEADS_PER_GROUP))
    t = jnp.pad(t, [(0, 0)] * len(lead) + [(0, 0), (0, LANE - HEADS_PER_GROUP)])
    return t.reshape(lead + (DT_PAD,))


def _from_group_major(v):
    lead = v.shape[:-1]
    return v.reshape(lead + (SSD_GROUPS, LANE))[..., :HEADS_PER_GROUP].reshape(lead + (SSD_HEADS,))


def _rows8(v):
    return jnp.pad(v, ((0, 8 - v.shape[0]), (0, 0)))


IN_OFFS = {"q": (0, 1024), "kv": (1024, 1280), "z": (1280, 3328), "xs": (3328, 5376), "b": (5376, 5888),
           "c": (5888, 6400), "dt": (6400, 6432), "gl": (6432, 8480)}
PIECES = ("q", "kv", "z", "xs", "b", "c", "dt", "gl")


CAT = ("z", "xs", "gl", "q", "b", "c", "dt", "kv")
CAT_WIDTH = {"q": 1024, "z": 2048, "xs": 2048, "gl": 2048, "b": 512, "c": 512, "kv": 256, "dt": DT_PAD}
CAT_OFF = {p: sum(CAT_WIDTH[q] for q in CAT[:i]) for i, p in enumerate(CAT)}
CAT_DIM = sum(CAT_WIDTH.values())
MAIN_DIM = CAT_OFF["kv"]


def _cat_w_in(g):
    pieces = {p: _cols(g, lo, hi) for p, (lo, hi) in IN_OFFS.items()}
    pieces["dt"] = _to_group_major(pieces["dt"])
    return jnp.concatenate([pieces[p] for p in CAT], axis=1)


def _uncat_dw_in(dw):
    pieces = {p: dw[:, CAT_OFF[p]:CAT_OFF[p] + CAT_WIDTH[p]] for p in CAT}
    pieces["dt"] = _from_group_major(pieces["dt"])
    return jnp.concatenate([pieces[p] for p in PIECES], axis=1)


def _params_out(W):
    p = {n: W[n] for n in ROW_SHARDED}
    for n in ("w_ffn_gate", "w_ffn_up"):
        p[n] = _cols(W[n], 0, FFN_HIDDEN)
    return p


def _params_in(l, W, sm):
    p = {"w_cat": _cat_w_in(W["w_in"])}
    cw = _cols(W["conv_w"], 0, SSD_D_INNER + 2 * BC_DIM)
    cb = sm["conv_b"][l]
    segs = {"xs": (0, 2048), "b": (2048, 2560), "c": (2560, 3072)}
    p["conv_w8"] = {s: _rows8(cw[:, lo:hi]) for s, (lo, hi) in segs.items()}
    p["conv_b8"] = {s: _rows8(cb[None, lo:hi]) for s, (lo, hi) in segs.items()}
    p["ssd_par"] = _rows8(jnp.stack([_to_group_major(sm["dt_bias"][l]), _to_group_major(sm["a_log"][l]),
                                     _to_group_major(sm["d_skip"][l])]))
    p["norm_w"] = sm["ssd_norm_w"][l]
    p["sinks8"] = _rows8(jnp.pad(sm["att_sinks"][l], (0, LANE - ATT_HEADS))[None])
    for n in ("ln_mix_g", "ln_mix_b", "ln_ffn_g", "ln_ffn_b"):
        p[n] = sm[n][l]
    return p


def _fwd_mixers(h0, p, l, dep=None):
    tag = f"l{l}_"
    a = {"h0": h0}
    proj = _mm(h0, p["w_cat"], "nn", tag + "proj", dep=dep)
    for pc in CAT:
        a[pc] = (proj, CAT_OFF[pc], CAT_WIDTH[pc])
    for s in ("xs", "b", "c"):
        a[s + "c"] = _conv_fwd(a[s], p["conv_w8"][s], p["conv_b8"][s], tag + "conv_" + s)
    a["y"], a["hs"] = _ssd_fwd(a["xsc"], a["bc"], a["cc"], a["dt"], p["ssd_par"], tag + "ssd_fwd")
    a["yn"] = _gnorm_fwd(a["y"], a["z"], p["norm_w"], tag + "gnorm")
    a["att"] = _att_fwd(a["q"], a["kv"], p["sinks8"], tag + "att_fwd")
    return a


def _fwd_out(a, p, l, dep=None):
    tag = f"l{l}_"
    h0 = a["h0"]
    a["ya"] = _mm(a["yn"], p["w_ssd_out"], "nn", tag + "ssd_out", dep=dep)
    a["yb"] = _mm(a["att"], p["w_att_out"], "nn", tag + "att_out", dep=dep)
    a["merged"] = _merge_fwd(a["gl"], a["ya"], a["yb"], tag + "merge")
    a["mix"] = _mm(a["merged"], p["w_mix_out"], "nn", tag + "mix_out")
    a["h1"] = _ln_fwd(h0, a["mix"], p["ln_mix_g"], p["ln_mix_b"], ALPHA, tag + "ln_mix")
    a["fg"] = _mm(a["h1"], p["w_ffn_gate"], "nn", tag + "ffn_gate")
    a["fu"] = _mm(a["h1"], p["w_ffn_up"], "nn", tag + "ffn_up")
    a["act"] = _swiglu_fwd(a["fg"], a["fu"], tag + "swiglu")
    a["ffn"] = _mm(a["act"], p["w_ffn_down"], "nn", tag + "ffn_down")
    a["h2"] = _ln_fwd(a["h1"], a["ffn"], p["ln_ffn_g"], p["ln_ffn_b"], ALPHA, tag + "ln_ffn")
    return a


def _dw(x, dy, name, dep=None):
    return _mm(x, dy, "tn", name, out_dtype=BF16, dep=dep)


def _bwd_out(a, p, dh2, l, dep=None):
    tag = f"l{l}_b_"
    gw, gs = {}, {}
    du2, acc = _ln_bwd(a["h1"], a["ffn"], p["ln_ffn_g"], dh2, ALPHA, tag + "ln_ffn")
    gs["ln_ffn_g"], gs["ln_ffn_b"] = acc[0], acc[1]
    gw["w_ffn_down"] = _dw(a["act"], du2, tag + "dw_down", dep=dep)
    dact = _mm(du2, p["w_ffn_down"], "nt", tag + "dact", dep=dep)
    dfg, dfu = _swiglu_bwd(a["fg"], a["fu"], dact, tag + "swiglu")
    gw["w_ffn_gate"] = _dw(a["h1"], dfg, tag + "dw_gate")
    gw["w_ffn_up"] = _dw(a["h1"], dfu, tag + "dw_up")
    dh1 = _mm(dfg, p["w_ffn_gate"], "nt", tag + "dh1_gate", add=du2, add_scale=ALPHA)
    dh1 = _mm(dfu, p["w_ffn_up"], "nt", tag + "dh1_up", add=dh1)
    du1, acc = _ln_bwd(a["h0"], a["mix"], p["ln_mix_g"], dh1, ALPHA, tag + "ln_mix")
    gs["ln_mix_g"], gs["ln_mix_b"] = acc[0], acc[1]
    gw["w_mix_out"] = _dw(a["merged"], du1, tag + "dw_mix")
    dmerged = _mm(du1, p["w_mix_out"], "nt", tag + "dmerged")
    dya, dyb, dproj = _merge_bwd(a["gl"], a["ya"], a["yb"], dmerged, tag + "merge",
                                 (None, CAT_OFF["gl"], MAIN_DIM))
    gw["w_ssd_out"] = _dw(a["yn"], dya, tag + "dw_ssd")
    gw["w_att_out"] = _dw(a["att"], dyb, tag + "dw_att")
    return {"du1": du1, "dya": dya, "dyb": dyb, "dproj": dproj}, gw, gs


def _bwd_mixers(a, p, carry, l, dep=None):
    tag = f"l{l}_b_"
    gs = {}
    du1, dproj = carry["du1"], carry["dproj"]

    def win(pc):
        return (dproj, CAT_OFF[pc], MAIN_DIM)

    dyn = _mm(carry["dya"], p["w_ssd_out"], "nt", tag + "dyn", dep=dep)
    datt = _mm(carry["dyb"], p["w_att_out"], "nt", tag + "datt", out_dtype=BF16, dep=dep)
    dproj, dkv, acc = _att_bwd(a["q"], a["kv"], p["sinks8"], datt, tag + "att", win("q"))
    gs["att_sinks"] = acc[0, :ATT_HEADS]
    dy, dproj, acc = _gnorm_bwd(a["y"], a["z"], p["norm_w"], dyn, tag + "gnorm", win("z"))
    gs["ssd_norm_w"] = acc[0]
    dxs, dbm, dcm, dproj, acc = _ssd_bwd(a["xsc"], a["bc"], a["cc"], a["dt"], p["ssd_par"], a["hs"], dy,
                                         tag + "ssd", win("dt"))
    gs["dt_bias"], gs["a_log"], gs["d_skip"] = (_from_group_major(acc[i]) for i in range(3))
    dconv_w, dconv_b = [], []
    for s, dout in (("xs", dxs), ("b", dbm), ("c", dcm)):
        dc, acc = _conv_bwd_pre(a[s], p["conv_w8"][s], p["conv_b8"][s], dout, tag + "conv_pre_" + s)
        dconv_w.append(acc[:CONV_TAPS])
        dconv_b.append(acc[CONV_TAPS])
        dproj = _conv_bwd_in(dc, p["conv_w8"][s], tag + "conv_in_" + s, win(s))
    gconv = jnp.concatenate(dconv_w, axis=1)
    gs["conv_b"] = jnp.concatenate(dconv_b)
    w_main, w_kv = p["w_cat"][:, :MAIN_DIM], p["w_cat"][:, MAIN_DIM:]
    dw = jnp.concatenate([_dw(a["h0"], dproj, tag + "dw_in"), _dw(a["h0"], dkv, tag + "dw_in_kv")], axis=1)
    dh0 = _mm(dproj, w_main, "nt", tag + "dh0", add=du1, add_scale=ALPHA)
    dh0 = _mm(dkv, w_kv, "nt", tag + "dh0_kv", add=dh0)
    return dh0, _uncat_dw_in(dw), gconv, gs


def _step(x, target, w, m, v):
    x2 = x[0]
    t2 = target[0]
    tok = jnp.zeros(TOKEN.shape, TOKEN.dtype)

    ex = _exchange_start(_gather_items(w, IN_PART, 0), tok, "gather_l0_in_start")
    lands, tok = _exchange_wait(ex, ex["token"], "gather_l0_in_wait")
    p0 = _params_in(0, dict(zip(IN_PART, lands)), w)
    ex = _exchange_start(_gather_items(w, OUT_PART, 0) + _gather_items(w, IN_PART, 1), tok,
                         "gather_l0_out_l1_in_start")
    h = _ln_fwd(x2, None, w["ln_in_g"], w["ln_in_b"], 1.0, "ln_in")
    a0 = _fwd_mixers(h, p0, 0, dep=ex["token"])
    lands, tok = _exchange_wait(ex, a0["att"], "gather_l0_out_l1_in_wait")
    p0.update(_params_out(dict(zip(OUT_PART, lands))))
    p1 = _params_in(1, dict(zip(IN_PART, lands[len(OUT_PART):])), w)
    ex = _exchange_start(_gather_items(w, OUT_PART, 1), tok, "gather_l1_out_start")
    a0 = _fwd_out(a0, p0, 0, dep=ex["token"])
    lands, tok = _exchange_wait(ex, a0["h2"], "gather_l1_out_wait")
    p1.update(_params_out(dict(zip(OUT_PART, lands))))
    a1 = _fwd_out(_fwd_mixers(a0["h2"], p1, 1), p1, 1)

    sse, dh = _loss_fwd_bwd(a1["h2"], t2, "loss")
    loss = lax.psum(0.5 / D_MODEL * sse[0, 0], ("x", "y", "c"))

    carry, gw1, gs1 = _bwd_out(a1, p1, dh, 1)
    dh, gw1["w_in"], gw1["conv_w"], gs = _bwd_mixers(a1, p1, carry, 1)
    gs1.update(gs)
    ex1 = _exchange_start(_scatter_items(gw1, GATHERED), tok, "scatter_l1_start")
    carry, gw0, gs0 = _bwd_out(a0, p0, dh, 0, dep=ex1["token"])
    lands, tok = _exchange_wait(ex1, carry["dyb"], "scatter_l1_wait")
    land1 = dict(zip(GATHERED, lands))
    ex0 = _exchange_start(_scatter_items(gw0, OUT_PART), tok, "scatter_l0_out_start")
    dh, gw0["w_in"], gw0["conv_w"], gs = _bwd_mixers(a0, p0, carry, 0, dep=ex0["token"])
    gs0.update(gs)
    lands, tok = _exchange_wait(ex0, dh, "scatter_l0_out_wait")
    land0 = dict(zip(OUT_PART, lands))
    ex0 = _exchange_start(_scatter_items(gw0, IN_PART), tok, "scatter_l0_in_start")
    grad_x2, acc = _ln_bwd(x2, None, w["ln_in_g"], dh, 1.0, "ln_in_b")

    outs = [{} for _ in range(4)]

    def update(names):
        res = None
        for n in names:
            res = _adamw([land0[n], land1[n]], w[n], m[n], v[n], "adamw_" + n)
            for o, t in zip(outs, res):
                o[n] = t
        return res[1]

    update(OUT_PART)
    gsm = {"ln_in_g": acc[0], "ln_in_b": acc[1]}
    for n in SMALL[2:]:
        gsm[n] = jnp.stack([gs0[n], gs1[n]])
    small_shapes = {n: w[n].shape for n in SMALL}
    land_s = _all_gather_small(_pack_small(gsm), "small_grads_all_gather")
    res = _adamw([land_s], _pack_small(w)[None], _pack_small(m)[None], _pack_small(v)[None], "adamw_small")
    for o, t in zip(outs, res):
        o.update(_unpack_small(t[0], small_shapes))
    lands, _ = _exchange_wait(ex0, res[1], "scatter_l0_in_wait")
    land0.update(zip(IN_PART, lands))
    update(IN_PART)
    return loss, grad_x2[None], outs


WEIGHT_NAMES = ("ln_in_g", "ln_in_b", "w_in", "conv_w", "conv_b", "dt_bias", "a_log", "d_skip", "ssd_norm_w",
                "att_sinks", "w_ssd_out", "w_att_out", "w_mix_out", "ln_mix_g", "ln_mix_b", "w_ffn_gate",
                "w_ffn_up", "w_ffn_down", "ln_ffn_g", "ln_ffn_b")


def kernel(x, ln_in_g, ln_in_b, w_in, conv_w, conv_b, dt_bias, a_log, d_skip, ssd_norm_w, att_sinks, w_ssd_out, w_att_out, w_mix_out, ln_mix_g, ln_mix_b, w_ffn_gate, w_ffn_up, w_ffn_down, ln_ffn_g, ln_ffn_b, loss_target, m_ln_in_g, m_ln_in_b, m_w_in, m_conv_w, m_conv_b, m_dt_bias, m_a_log, m_d_skip, m_ssd_norm_w, m_att_sinks, m_w_ssd_out, m_w_att_out, m_w_mix_out, m_ln_mix_g, m_ln_mix_b, m_w_ffn_gate, m_w_ffn_up, m_w_ffn_down, m_ln_ffn_g, m_ln_ffn_b, v_ln_in_g, v_ln_in_b, v_w_in, v_conv_w, v_conv_b, v_dt_bias, v_a_log, v_d_skip, v_ssd_norm_w, v_att_sinks, v_w_ssd_out, v_w_att_out, v_w_mix_out, v_ln_mix_g, v_ln_mix_b, v_w_ffn_gate, v_w_ffn_up, v_w_ffn_down, v_ln_ffn_g, v_ln_ffn_b):
    w = dict(zip(WEIGHT_NAMES, (ln_in_g, ln_in_b, w_in, conv_w, conv_b, dt_bias, a_log, d_skip, ssd_norm_w,
                                att_sinks, w_ssd_out, w_att_out, w_mix_out, ln_mix_g, ln_mix_b, w_ffn_gate,
                                w_ffn_up, w_ffn_down, ln_ffn_g, ln_ffn_b)))
    m = dict(zip(WEIGHT_NAMES, (m_ln_in_g, m_ln_in_b, m_w_in, m_conv_w, m_conv_b, m_dt_bias, m_a_log, m_d_skip,
                                m_ssd_norm_w, m_att_sinks, m_w_ssd_out, m_w_att_out, m_w_mix_out, m_ln_mix_g,
                                m_ln_mix_b, m_w_ffn_gate, m_w_ffn_up, m_w_ffn_down, m_ln_ffn_g, m_ln_ffn_b)))
    v = dict(zip(WEIGHT_NAMES, (v_ln_in_g, v_ln_in_b, v_w_in, v_conv_w, v_conv_b, v_dt_bias, v_a_log, v_d_skip,
                                v_ssd_norm_w, v_att_sinks, v_w_ssd_out, v_w_att_out, v_w_mix_out, v_ln_mix_g,
                                v_ln_mix_b, v_w_ffn_gate, v_w_ffn_up, v_w_ffn_down, v_ln_ffn_g, v_ln_ffn_b)))
    loss, grad_x, outs = _step(x, loss_target, w, m, v)
    result = [loss, grad_x]
    for o in outs:
        result.extend(o[n] for n in WEIGHT_NAMES)
    return tuple(result)
```

```python
import functools
import math

import jax
import jax.numpy as jnp
from jax import lax
from jax.experimental import pallas as pl
from jax.experimental.pallas import tpu as pltpu

F32 = jnp.float32
BF16 = jnp.bfloat16

D_MODEL = 1024
DEPTH = 2
N_DEV = 8
ATT_HEADS = 16
ATT_KV_HEADS = 2
ATT_HEAD_DIM = 64
ATT_BLOCK = 128
SSD_D_INNER = 2048
SSD_HEADS = 32
SSD_GROUPS = 4
SSD_STATE = 128
SSD_CHUNK = 128
FFN_HIDDEN = 2816
LN_EPS = 1e-5
RMS_EPS = 1e-5
ALPHA = (2 * DEPTH) ** 0.25
Q_DIM = 1024
KV_DIM = 128
BC_DIM = 512
IN_DIM = 8480
IN_SHARD = IN_DIM // N_DEV
DT_PAD = 512

ADAM_LR = 0.001
ADAM_B1 = 0.9
ADAM_B2 = 0.999
ADAM_EPS = 1e-08
ADAM_WD = 0.01
ADAM_STEP = 10

LANE = 128
VMEM_LIMIT = 48 * 1024 * 1024
PACK_W = 1024
NEG = -1e30

_NN = (((1,), (0,)), ((), ()))
_NT = (((1,), (1,)), ((), ()))
_TN = (((0,), (0,)), ((), ()))
MESH_ID = pl.DeviceIdType.MESH


def _dot(a, b, dims=_NN):
    return lax.dot_general(a, b, dims, preferred_element_type=F32)


def _dot_hi(a, b):
    return lax.dot_general(a, b, _NN, preferred_element_type=F32, precision=lax.Precision.HIGHEST)


def _sig(x):
    return 1.0 / (1.0 + jnp.exp(-x))


def _softplus(x):
    return jnp.maximum(x, 0.0) + jnp.log(1.0 + jnp.exp(-jnp.abs(x)))


def _cparams(*sem):
    return pltpu.CompilerParams(dimension_semantics=sem, vmem_limit_bytes=VMEM_LIMIT)


def _pick(n, cap):
    if n <= cap:
        return n
    best = None
    for t in range(LANE, cap + 1, LANE):
        if n % t == 0:
            best = t
    assert best is not None, (n, cap)
    return best


def _tile(n):
    if n <= 1024 or n % 1024 == 0:
        return min(n, 1024)
    return _pick(n, 1408)


def _rows(n):
    return min(512, n)


def _window(x):
    return x if isinstance(x, tuple) else (x, 0, x.shape[1])


def _into(into, n_in, out_idx):
    buf, col0, width = into
    if buf is None:
        return [], [], {}, col0, width
    return [buf], [pl.BlockSpec(memory_space=pl.ANY)], {n_in: out_idx}, col0, width


def _mm(a, b, mode, name, add=None, add_scale=1.0, out_dtype=F32, dep=None):
    if mode == "nn":
        m, k = a.shape
        n = b.shape[1]
    elif mode == "nt":
        m, k = a.shape
        n = b.shape[0]
    else:
        k, m = a.shape
        n = b.shape[1]
    tm = _tile(m)
    tn = _pick(n, 2176) if mode == "tn" and n > 1024 else _tile(n)
    tk = _pick(k, 2176) if mode == "nt" and a.dtype == BF16 and k > 2816 else _tile(k)
    nk = k // tk
    has_add = add is not None
    dims = {"nn": _NN, "nt": _NT, "tn": _TN}[mode]

    def body(*refs):
        if dep is not None:
            refs = refs[:-3] + refs[-2:]
        if has_add:
            a_ref, b_ref, add_ref, o_ref, acc_ref = refs
        else:
            a_ref, b_ref, o_ref, acc_ref = refs
        kk = pl.program_id(2)

        @pl.when(kk == 0)
        def _():
            if has_add:
                acc_ref[...] = add_scale * add_ref[...].astype(F32)
            else:
                acc_ref[...] = jnp.zeros_like(acc_ref)

        acc_ref[...] += _dot(a_ref[...].astype(BF16), b_ref[...].astype(BF16), dims)

        @pl.when(kk == nk - 1)
        def _():
            o_ref[...] = acc_ref[...].astype(o_ref.dtype)

    if mode == "nn":
        a_spec = pl.BlockSpec((tm, tk), lambda i, j, kk: (i, kk))
        b_spec = pl.BlockSpec((tk, tn), lambda i, j, kk: (kk, j))
    elif mode == "nt":
        a_spec = pl.BlockSpec((tm, tk), lambda i, j, kk: (i, kk))
        b_spec = pl.BlockSpec((tn, tk), lambda i, j, kk: (j, kk))
    else:
        a_spec = pl.BlockSpec((tk, tm), lambda i, j, kk: (kk, i))
        b_spec = pl.BlockSpec((tk, tn), lambda i, j, kk: (kk, j))
    o_spec = pl.BlockSpec((tm, tn), lambda i, j, kk: (i, j))
    in_specs = [a_spec, b_spec] + ([o_spec] if has_add else [])
    args = (a, b) + ((add,) if has_add else ())
    if dep is not None:
        in_specs.append(pl.BlockSpec((8, LANE), lambda i, j, kk: (0, 0)))
        args += (dep,)
    return pl.pallas_call(
        body, name=name, grid=(m // tm, n // tn, nk),
        in_specs=in_specs, out_specs=o_spec,
        out_shape=jax.ShapeDtypeStruct((m, n), out_dtype),
        scratch_shapes=[pltpu.VMEM((tm, tn), F32)],
        compiler_params=_cparams("parallel", "parallel", "arbitrary"),
    )(*args)


def _vec_spec(width):
    return pl.BlockSpec((1, width), lambda i: (0, 0))


def _ln_fwd(a, b, gamma, beta, alpha, name):
    n_rows, dm = a.shape
    has_b = b is not None

    def body(*refs):
        if has_b:
            a_ref, b_ref, g_ref, be_ref, o_ref = refs
            u = alpha * a_ref[...] + b_ref[...]
        else:
            a_ref, g_ref, be_ref, o_ref = refs
            u = a_ref[...]
        mu = jnp.mean(u, axis=-1, keepdims=True)
        d = u - mu
        var = jnp.mean(d * d, axis=-1, keepdims=True)
        o_ref[...] = d * lax.rsqrt(var + LN_EPS) * g_ref[...] + be_ref[...]

    row = pl.BlockSpec((_rows(n_rows),dm), lambda i: (i, 0))
    in_specs = [row] + ([row] if has_b else []) + [_vec_spec(dm), _vec_spec(dm)]
    args = (a,) + ((b,) if has_b else ()) + (gamma.reshape(1, dm), beta.reshape(1, dm))
    return pl.pallas_call(
        body, name=name, grid=(n_rows // _rows(n_rows),), in_specs=in_specs, out_specs=row,
        out_shape=jax.ShapeDtypeStruct((n_rows, dm), F32),
        compiler_params=_cparams("parallel"),
    )(*args)


def _ln_bwd(a, b, gamma, dy, alpha, name):
    n_rows, dm = a.shape
    has_b = b is not None

    def body(*refs):
        if has_b:
            a_ref, b_ref, g_ref, dy_ref, du_ref, acc_ref = refs
            u = alpha * a_ref[...] + b_ref[...]
        else:
            a_ref, g_ref, dy_ref, du_ref, acc_ref = refs
            u = a_ref[...]

        @pl.when(pl.program_id(0) == 0)
        def _():
            acc_ref[...] = jnp.zeros_like(acc_ref)

        mu = jnp.mean(u, axis=-1, keepdims=True)
        d = u - mu
        var = jnp.mean(d * d, axis=-1, keepdims=True)
        rstd = lax.rsqrt(var + LN_EPS)
        xhat = d * rstd
        dyv = dy_ref[...]
        acc_ref[0:1, :] += jnp.sum(dyv * xhat, axis=0, keepdims=True)
        acc_ref[1:2, :] += jnp.sum(dyv, axis=0, keepdims=True)
        dxh = dyv * g_ref[...]
        m1 = jnp.mean(dxh, axis=-1, keepdims=True)
        m2 = jnp.mean(dxh * xhat, axis=-1, keepdims=True)
        du_ref[...] = rstd * (dxh - m1 - xhat * m2)

    row = pl.BlockSpec((_rows(n_rows),dm), lambda i: (i, 0))
    in_specs = [row] + ([row] if has_b else []) + [_vec_spec(dm), row]
    args = (a,) + ((b,) if has_b else ()) + (gamma.reshape(1, dm), dy)
    return pl.pallas_call(
        body, name=name, grid=(n_rows // _rows(n_rows),), in_specs=in_specs,
        out_specs=(row, pl.BlockSpec((8, dm), lambda i: (0, 0))),
        out_shape=(jax.ShapeDtypeStruct((n_rows, dm), F32), jax.ShapeDtypeStruct((8, dm), F32)),
        compiler_params=_cparams("arbitrary"),
    )(*args)


def _loss_fwd_bwd(y, target, name):
    n_rows, dm = y.shape

    def body(y_ref, t_ref, acc_ref, dy_ref):
        @pl.when(pl.program_id(0) == 0)
        def _():
            acc_ref[...] = jnp.zeros_like(acc_ref)

        d = y_ref[...] - t_ref[...]
        acc_ref[...] += jnp.sum(d * d)
        dy_ref[...] = d * (1.0 / dm)

    row = pl.BlockSpec((_rows(n_rows),dm), lambda i: (i, 0))
    return pl.pallas_call(
        body, name=name, grid=(n_rows // _rows(n_rows),), in_specs=[row, row],
        out_specs=(pl.BlockSpec((8, LANE), lambda i: (0, 0)), row),
        out_shape=(jax.ShapeDtypeStruct((8, LANE), F32), jax.ShapeDtypeStruct((n_rows, dm), F32)),
        compiler_params=_cparams("arbitrary"),
    )(y, target)


def _swiglu_fwd(g, u, name):
    n_rows, w = g.shape
    tw = _pick(w, 1408)

    def body(g_ref, u_ref, o_ref):
        gv = g_ref[...]
        o_ref[...] = (gv * _sig(gv) * u_ref[...]).astype(BF16)

    blk = pl.BlockSpec((_rows(n_rows),tw), lambda i, j: (i, j))
    return pl.pallas_call(
        body, name=name, grid=(n_rows // _rows(n_rows), w // tw), in_specs=[blk, blk], out_specs=blk,
        out_shape=jax.ShapeDtypeStruct((n_rows, w), BF16),
        compiler_params=_cparams("parallel", "parallel"),
    )(g, u)


def _swiglu_bwd(g, u, dact, name):
    n_rows, w = g.shape
    tw = _pick(w, 1408)

    def body(g_ref, u_ref, da_ref, dg_ref, du_ref):
        gv = g_ref[...]
        s = _sig(gv)
        da = da_ref[...]
        dg_ref[...] = (da * u_ref[...] * (s * (1.0 + gv * (1.0 - s)))).astype(BF16)
        du_ref[...] = (da * gv * s).astype(BF16)

    blk = pl.BlockSpec((_rows(n_rows),tw), lambda i, j: (i, j))
    return pl.pallas_call(
        body, name=name, grid=(n_rows // _rows(n_rows), w // tw), in_specs=[blk, blk, blk], out_specs=(blk, blk),
        out_shape=(jax.ShapeDtypeStruct((n_rows, w), BF16), jax.ShapeDtypeStruct((n_rows, w), BF16)),
        compiler_params=_cparams("parallel", "parallel"),
    )(g, u, dact)


def _gate_specs(gl, n_rows, dm):
    arr, g0, _ = _window(gl)
    return arr, [pl.BlockSpec((_rows(n_rows), dm), lambda i, k=k: (i, g0 // dm + k)) for k in range(2)]


def _merge_fwd(gl, ya, yb, name):
    n_rows, dm = ya.shape
    gl_arr, gspecs = _gate_specs(gl, n_rows, dm)

    def body(ga_ref, gb_ref, ya_ref, yb_ref, o_ref):
        o_ref[...] = (_sig(ga_ref[...]) * ya_ref[...] + _sig(gb_ref[...]) * yb_ref[...]).astype(BF16)

    row = pl.BlockSpec((_rows(n_rows),dm), lambda i: (i, 0))
    return pl.pallas_call(
        body, name=name, grid=(n_rows // _rows(n_rows),), in_specs=gspecs + [row, row], out_specs=row,
        out_shape=jax.ShapeDtypeStruct((n_rows, dm), BF16),
        compiler_params=_cparams("parallel"),
    )(gl_arr, gl_arr, ya, yb)


def _merge_bwd(gl, ya, yb, dmerged, name, into):
    n_rows, dm = ya.shape
    gl_arr, gspecs = _gate_specs(gl, n_rows, dm)
    extra, extra_specs, aliases, col0, width = _into(into, 5, 2)

    def body(*refs):
        ga_ref, gb_ref, ya_ref, yb_ref, dm_ref = refs[:5]
        dya_ref, dyb_ref, dgl_ref = refs[-3:]
        ga = _sig(ga_ref[...])
        gb = _sig(gb_ref[...])
        dmv = dm_ref[...]
        dya_ref[...] = (dmv * ga).astype(BF16)
        dyb_ref[...] = (dmv * gb).astype(BF16)
        dgl_ref[:, :dm] = (dmv * ya_ref[...] * ga * (1.0 - ga)).astype(BF16)
        dgl_ref[:, dm:] = (dmv * yb_ref[...] * gb * (1.0 - gb)).astype(BF16)

    row = pl.BlockSpec((_rows(n_rows),dm), lambda i: (i, 0))
    row2 = pl.BlockSpec((_rows(n_rows),2 * dm), lambda i: (i, col0 // (2 * dm)))
    return pl.pallas_call(
        body, name=name, grid=(n_rows // _rows(n_rows),), in_specs=gspecs + [row, row, row] + extra_specs,
        out_specs=(row, row, row2),
        out_shape=(jax.ShapeDtypeStruct((n_rows, dm), BF16), jax.ShapeDtypeStruct((n_rows, dm), BF16),
                   jax.ShapeDtypeStruct((n_rows, width), BF16)),
        input_output_aliases=aliases,
        compiler_params=_cparams("parallel"),
    )(gl_arr, gl_arr, ya, yb, dmerged, *extra)


CONV_TAPS = 4
CONV_COLS = 512
HALO = 8


def _shift_down(cur, prev8, s, row8):
    r = pltpu.roll(cur, s, axis=0)
    top = jnp.where(row8 < s, pltpu.roll(prev8, s, axis=0), r[0:HALO])
    return jnp.concatenate([top, r[HALO:]], axis=0)


def _shift_up(cur, next8, s, row8):
    n = cur.shape[0]
    r = pltpu.roll(cur, n - s, axis=0)
    bot = jnp.where(row8 >= HALO - s, pltpu.roll(next8, HALO - s, axis=0), r[n - HALO:])
    return jnp.concatenate([r[:n - HALO], bot], axis=0)


def _conv_pre(u_ref, prev_ref, w_ref, b_ref, li):
    cur = u_ref[...]
    prev8 = jnp.where(li == 0, 0.0, prev_ref[...])
    row8 = lax.broadcasted_iota(jnp.int32, prev8.shape, 0)
    shifted = [cur] + [_shift_down(cur, prev8, s, row8) for s in range(1, CONV_TAPS)]
    acc = b_ref[...] + shifted[0] * w_ref[CONV_TAPS - 1:CONV_TAPS, :]
    for s in range(1, CONV_TAPS):
        acc = acc + shifted[s] * w_ref[CONV_TAPS - 1 - s:CONV_TAPS - s, :]
    return acc, shifted


def _conv_specs(n_rows, tl, col0=0):
    off = col0 // CONV_COLS
    cur = pl.BlockSpec((tl, CONV_COLS), lambda cj, li: (li, cj + off))
    prev = pl.BlockSpec((HALO, CONV_COLS), lambda cj, li: (jnp.maximum(li * (tl // HALO) - 1, 0), cj + off))
    nxt = pl.BlockSpec((HALO, CONV_COLS),
                       lambda cj, li: (jnp.minimum((li + 1) * (tl // HALO), n_rows // HALO - 1), cj + off))
    par = pl.BlockSpec((8, CONV_COLS), lambda cj, li: (0, cj + off))
    return cur, prev, nxt, par


def _conv_fwd(u, w8, b8, name):
    u, u0, c = _window(u)
    n_rows = u.shape[0]
    tl = _rows(n_rows)
    cur, _, _, par = _conv_specs(n_rows, tl)
    ucur, prev, _, _ = _conv_specs(n_rows, tl, u0)

    def body(u_ref, prev_ref, w_ref, b_ref, o_ref):
        acc, _ = _conv_pre(u_ref, prev_ref, w_ref, b_ref[0:1, :], pl.program_id(1))
        o_ref[...] = acc * _sig(acc)

    return pl.pallas_call(
        body, name=name, grid=(c // CONV_COLS, n_rows // tl), in_specs=[ucur, prev, par, par], out_specs=cur,
        out_shape=jax.ShapeDtypeStruct((n_rows, c), F32),
        compiler_params=_cparams("parallel", "parallel"),
    )(u, u, w8, b8)


def _conv_bwd_pre(u, w8, b8, dout, name):
    u, u0, c = _window(u)
    n_rows = u.shape[0]
    tl = _rows(n_rows)
    cur, _, _, par = _conv_specs(n_rows, tl)
    ucur, prev, _, _ = _conv_specs(n_rows, tl, u0)

    def body(u_ref, prev_ref, w_ref, b_ref, do_ref, dc_ref, acc_ref):
        @pl.when(pl.program_id(1) == 0)
        def _():
            acc_ref[...] = jnp.zeros_like(acc_ref)

        acc, shifted = _conv_pre(u_ref, prev_ref, w_ref, b_ref[0:1, :], pl.program_id(1))
        sg = _sig(acc)
        dc = do_ref[...] * (sg * (1.0 + acc * (1.0 - sg)))
        dc_ref[...] = dc
        for k in range(CONV_TAPS):
            acc_ref[k:k + 1, :] += jnp.sum(dc * shifted[CONV_TAPS - 1 - k], axis=0, keepdims=True)
        acc_ref[CONV_TAPS:CONV_TAPS + 1, :] += jnp.sum(dc, axis=0, keepdims=True)

    return pl.pallas_call(
        body, name=name, grid=(c // CONV_COLS, n_rows // tl), in_specs=[ucur, prev, par, par, cur],
        out_specs=(cur, par),
        out_shape=(jax.ShapeDtypeStruct((n_rows, c), F32), jax.ShapeDtypeStruct((8, c), F32)),
        compiler_params=_cparams("parallel", "arbitrary"),
    )(u, u, w8, b8, dout)


def _conv_bwd_in(dc, w8, name, into):
    n_rows, c = dc.shape
    tl = _rows(n_rows)
    cur, _, nxt, par = _conv_specs(n_rows, tl)
    n_l = n_rows // tl
    extra, extra_specs, aliases, col0, width = _into(into, 3, 0)
    out_spec = _conv_specs(n_rows, tl, col0)[0]

    def body(*refs):
        dc_ref, next_ref, w_ref = refs[:3]
        o_ref = refs[-1]
        cur_v = dc_ref[...]
        next8 = jnp.where(pl.program_id(1) == n_l - 1, 0.0, next_ref[...])
        row8 = lax.broadcasted_iota(jnp.int32, next8.shape, 0)
        acc = cur_v * w_ref[CONV_TAPS - 1:CONV_TAPS, :]
        for s in range(1, CONV_TAPS):
            acc = acc + _shift_up(cur_v, next8, s, row8) * w_ref[CONV_TAPS - 1 - s:CONV_TAPS - s, :]
        o_ref[...] = acc.astype(BF16)

    return pl.pallas_call(
        body, name=name, grid=(c // CONV_COLS, n_l), in_specs=[cur, nxt, par] + extra_specs, out_specs=out_spec,
        out_shape=jax.ShapeDtypeStruct((n_rows, width), BF16), input_output_aliases=aliases,
        compiler_params=_cparams("parallel", "parallel"),
    )(dc, dc, w8, *extra)


NORM_GROUP = SSD_D_INNER // SSD_GROUPS


def _gnorm_fwd(y, z, w, name):
    n_rows, c = y.shape
    z, z0, _ = _window(z)
    zoff = z0 // NORM_GROUP

    def body(y_ref, z_ref, w_ref, o_ref):
        zv = z_ref[...]
        yg = y_ref[...] * (zv * _sig(zv))
        r = lax.rsqrt(jnp.mean(yg * yg, axis=-1, keepdims=True) + RMS_EPS)
        o_ref[...] = (yg * r * w_ref[...]).astype(BF16)

    blk = pl.BlockSpec((_rows(n_rows),NORM_GROUP), lambda i, j: (i, j))
    zblk = pl.BlockSpec((_rows(n_rows),NORM_GROUP), lambda i, j: (i, j + zoff))
    wspec = pl.BlockSpec((1, NORM_GROUP), lambda i, j: (0, j))
    return pl.pallas_call(
        body, name=name, grid=(n_rows // _rows(n_rows), c // NORM_GROUP), in_specs=[blk, zblk, wspec], out_specs=blk,
        out_shape=jax.ShapeDtypeStruct((n_rows, c), BF16),
        compiler_params=_cparams("parallel", "parallel"),
    )(y, z, w.reshape(1, c))


def _gnorm_bwd(y, z, w, dyn, name, into):
    n_rows, c = y.shape
    z, z0, _ = _window(z)
    zoff = z0 // NORM_GROUP
    extra, extra_specs, aliases, col0, width = _into(into, 4, 1)
    doff = col0 // NORM_GROUP

    def body(*refs):
        y_ref, z_ref, w_ref, dn_ref = refs[:4]
        dy_ref, dz_ref, acc_ref = refs[-3:]
        @pl.when(pl.program_id(1) == 0)
        def _():
            acc_ref[...] = jnp.zeros_like(acc_ref)

        zv = z_ref[...]
        yv = y_ref[...]
        sz = _sig(zv)
        silu = zv * sz
        yg = yv * silu
        r = lax.rsqrt(jnp.mean(yg * yg, axis=-1, keepdims=True) + RMS_EPS)
        nrm = yg * r
        dn = dn_ref[...]
        acc_ref[0:1, :] += jnp.sum(dn * nrm, axis=0, keepdims=True)
        dnw = dn * w_ref[...]
        dyg = r * (dnw - nrm * jnp.mean(dnw * nrm, axis=-1, keepdims=True))
        dy_ref[...] = dyg * silu
        dz_ref[...] = (dyg * yv * (sz * (1.0 + zv * (1.0 - sz)))).astype(BF16)

    blk = pl.BlockSpec((_rows(n_rows),NORM_GROUP), lambda j, i: (i, j))
    zblk = pl.BlockSpec((_rows(n_rows),NORM_GROUP), lambda j, i: (i, j + zoff))
    wspec = pl.BlockSpec((1, NORM_GROUP), lambda j, i: (0, j))
    aspec = pl.BlockSpec((8, NORM_GROUP), lambda j, i: (0, j))
    return pl.pallas_call(
        body, name=name, grid=(c // NORM_GROUP, n_rows // _rows(n_rows)),
        in_specs=[blk, zblk, wspec, blk] + extra_specs,
        out_specs=(blk, pl.BlockSpec((_rows(n_rows), NORM_GROUP), lambda j, i: (i, j + doff)), aspec),
        out_shape=(jax.ShapeDtypeStruct((n_rows, c), F32), jax.ShapeDtypeStruct((n_rows, width), BF16),
                   jax.ShapeDtypeStruct((8, c), F32)),
        input_output_aliases=aliases,
        compiler_params=_cparams("parallel", "arbitrary"),
    )(y, z, w.reshape(1, c), dyn, *extra)


ATT_SCALE = ATT_HEAD_DIM ** -0.5
ATT_SLOPES = [2.0 ** (-8.0 * (h + 1) / ATT_HEADS) for h in range(ATT_HEADS)]
Q_PER_KV = ATT_HEADS // ATT_KV_HEADS


def _dup_half(t, g, lo):
    tr = pltpu.roll(t, ATT_HEAD_DIM, axis=1)
    return jnp.where(lo, t, tr) if g == 0 else jnp.where(lo, tr, t)


def _att_band(kv_ref, kvp_ref, n):
    cur = kv_ref[...]
    prev = jnp.where(n == 0, 0.0, kvp_ref[...])
    lo = lax.broadcasted_iota(jnp.int32, (ATT_BLOCK, LANE), 1) < ATT_HEAD_DIM
    bands = []
    for g in range(ATT_KV_HEADS):
        kb = jnp.concatenate([_dup_half(prev[:, :LANE], g, lo), _dup_half(cur[:, :LANE], g, lo)], axis=0)
        vb = jnp.concatenate([_dup_half(prev[:, LANE:], g, lo), _dup_half(cur[:, LANE:], g, lo)], axis=0)
        bands.append((kb.astype(BF16), vb.astype(BF16)))
    return bands


def _att_tile(n):
    shape = (2 * ATT_BLOCK, ATT_BLOCK)
    row = lax.broadcasted_iota(jnp.int32, shape, 0)
    i = row & (ATT_BLOCK - 1)
    s = lax.broadcasted_iota(jnp.int32, shape, 1)
    upper = s > i
    dist = ((i - s) & (ATT_BLOCK - 1)).astype(F32)
    dead = upper & (n == 0)
    return upper, dist, dead, row[:, 0:1] < ATT_BLOCK


def _stack_pair(t, lo):
    return jnp.concatenate([jnp.where(lo, t, 0.0), jnp.where(lo, 0.0, t)], axis=0).astype(BF16)


def _att_probs(qs, kb, s_ref, j, tile):
    upper, dist, dead, first = tile
    s2 = _dot(qs, kb, _NT)
    slope = jnp.where(first, ATT_SLOPES[2 * j], ATT_SLOPES[2 * j + 1])
    sink = jnp.where(first, s_ref[0:1, 2 * j:2 * j + 1], s_ref[0:1, 2 * j + 1:2 * j + 2])
    s = jnp.where(upper, s2[:, :ATT_BLOCK], s2[:, ATT_BLOCK:]) - slope * dist
    s = jnp.where(dead, NEG, s)
    m = jnp.maximum(jnp.max(s, axis=-1, keepdims=True), sink)
    p = jnp.exp(s - m)
    es = jnp.exp(sink - m)
    inv = 1.0 / (jnp.sum(p, axis=-1, keepdims=True) + es)
    return p * inv, es * inv


def _band_split(t, upper):
    return jnp.concatenate([jnp.where(upper, t, 0.0), jnp.where(upper, 0.0, t)], axis=1)


def _att_fwd(q, kv, sinks8, name):
    q, q0, _ = _window(q)
    kv, kv0, _ = _window(kv)
    qoff, kvoff = q0 // Q_DIM, kv0 // (2 * LANE)
    n_rows = q.shape[0]
    nb = n_rows // ATT_BLOCK

    def body(q_ref, kv_ref, kvp_ref, s_ref, o_ref):
        n = pl.program_id(0)
        bands = _att_band(kv_ref, kvp_ref, n)
        lo = lax.broadcasted_iota(jnp.int32, (ATT_BLOCK, LANE), 1) < ATT_HEAD_DIM
        tile = _att_tile(n)
        for j in range(ATT_HEADS // 2):
            kb, vb = bands[2 * j // Q_PER_KV]
            qs = _stack_pair(q_ref[:, j * LANE:(j + 1) * LANE] * ATT_SCALE, lo)
            p, _ = _att_probs(qs, kb, s_ref, j, tile)
            out = _dot(_band_split(p, tile[0]).astype(BF16), vb)
            o_ref[:, j * LANE:(j + 1) * LANE] = jnp.where(lo, out[:ATT_BLOCK], out[ATT_BLOCK:]).astype(BF16)

    return pl.pallas_call(
        body, name=name, grid=(nb,),
        in_specs=[pl.BlockSpec((ATT_BLOCK, Q_DIM), lambda n: (n, qoff)),
                  pl.BlockSpec((ATT_BLOCK, 2 * LANE), lambda n: (n, kvoff)),
                  pl.BlockSpec((ATT_BLOCK, 2 * LANE), lambda n: (jnp.maximum(n - 1, 0), kvoff)),
                  pl.BlockSpec((8, LANE), lambda n: (0, 0))],
        out_specs=pl.BlockSpec((ATT_BLOCK, Q_DIM), lambda n: (n, 0)),
        out_shape=jax.ShapeDtypeStruct((n_rows, Q_DIM), BF16),
        compiler_params=_cparams("parallel"),
    )(q, kv, kv, sinks8)


def _att_bwd(q, kv, sinks8, dout, name, into):
    q, q0, _ = _window(q)
    kv, kv0, _ = _window(kv)
    qoff, kvoff = q0 // Q_DIM, kv0 // (2 * LANE)
    n_rows = q.shape[0]
    nb = n_rows // ATT_BLOCK

    extra, extra_specs, aliases, col0, width = _into(into, 5, 0)
    dqoff = col0 // Q_DIM

    def body(*refs):
        q_ref, kv_ref, kvp_ref, s_ref, do_ref = refs[:5]
        dq_ref, dkv_ref, acc_ref, carry_ref = refs[-4:]
        n = pl.program_id(0)

        @pl.when(n == 0)
        def _():
            acc_ref[...] = jnp.zeros_like(acc_ref)
            carry_ref[...] = jnp.zeros_like(carry_ref)

        @pl.when(n == nb)
        def _():
            dkv_ref[...] = carry_ref[...].astype(BF16)

        @pl.when(n < nb)
        def _():
            bands = _att_band(kv_ref, kvp_ref, n)
            lo = lax.broadcasted_iota(jnp.int32, (ATT_BLOCK, LANE), 1) < ATT_HEAD_DIM
            lane1 = lax.broadcasted_iota(jnp.int32, (1, LANE), 1)
            tile = _att_tile(n)
            upper, first = tile[0], tile[3]
            dk_acc = [jnp.zeros((2 * ATT_BLOCK, LANE), F32) for _ in range(ATT_KV_HEADS)]
            dv_acc = [jnp.zeros((2 * ATT_BLOCK, LANE), F32) for _ in range(ATT_KV_HEADS)]
            dsink = jnp.zeros((1, LANE), F32)
            for j in range(ATT_HEADS // 2):
                g = 2 * j // Q_PER_KV
                kb, vb = bands[g]
                qs = _stack_pair(q_ref[:, j * LANE:(j + 1) * LANE] * ATT_SCALE, lo)
                dos = _stack_pair(do_ref[:, j * LANE:(j + 1) * LANE].astype(F32), lo)
                p, ps = _att_probs(qs, kb, s_ref, j, tile)
                dp2 = _dot(dos, vb, _NT)
                dp = jnp.where(upper, dp2[:, :ATT_BLOCK], dp2[:, ATT_BLOCK:])
                delta = jnp.sum(p * dp, axis=-1, keepdims=True)
                ds2 = _band_split(p * (dp - delta), upper)
                psd = ps * delta
                dsink = jnp.where(lane1 == 2 * j, -jnp.sum(jnp.where(first, psd, 0.0)), dsink)
                dsink = jnp.where(lane1 == 2 * j + 1, -jnp.sum(jnp.where(first, 0.0, psd)), dsink)
                dq = _dot(ds2.astype(BF16), kb) * ATT_SCALE
                dq_ref[:, j * LANE:(j + 1) * LANE] = jnp.where(lo, dq[:ATT_BLOCK], dq[ATT_BLOCK:]).astype(BF16)
                dk_acc[g] = dk_acc[g] + _dot(ds2.T.astype(BF16), qs)
                dv_acc[g] = dv_acc[g] + _dot(_band_split(p, upper).T.astype(BF16), dos)
            acc_ref[0:1, :] += dsink
            lo2 = lax.broadcasted_iota(jnp.int32, (2 * ATT_BLOCK, LANE), 1) < ATT_HEAD_DIM
            folded = []
            for acc in (dk_acc, dv_acc):
                t0 = acc[0] + pltpu.roll(acc[0], ATT_HEAD_DIM, axis=1)
                t1 = acc[1] + pltpu.roll(acc[1], ATT_HEAD_DIM, axis=1)
                folded.append(jnp.where(lo2, t0, t1))
            band = jnp.concatenate(folded, axis=1)
            dkv_ref[...] = (carry_ref[...] + band[:ATT_BLOCK]).astype(BF16)
            carry_ref[...] = band[ATT_BLOCK:]

    def qmap(n):
        return (jnp.minimum(n, nb - 1), 0)

    return pl.pallas_call(
        body, name=name, grid=(nb + 1,),
        in_specs=[pl.BlockSpec((ATT_BLOCK, Q_DIM), lambda n: (jnp.minimum(n, nb - 1), qoff)),
                  pl.BlockSpec((ATT_BLOCK, 2 * LANE), lambda n: (jnp.minimum(n, nb - 1), kvoff)),
                  pl.BlockSpec((ATT_BLOCK, 2 * LANE),
                               lambda n: (jnp.maximum(jnp.minimum(n, nb - 1) - 1, 0), kvoff)),
                  pl.BlockSpec((8, LANE), lambda n: (0, 0)),
                  pl.BlockSpec((ATT_BLOCK, Q_DIM), qmap)] + extra_specs,
        out_specs=(pl.BlockSpec((ATT_BLOCK, Q_DIM), lambda n: (jnp.minimum(n, nb - 1), dqoff)),
                   pl.BlockSpec((ATT_BLOCK, 2 * LANE), lambda n: (jnp.maximum(n - 1, 0), 0)),
                   pl.BlockSpec((8, LANE), lambda n: (0, 0))),
        out_shape=(jax.ShapeDtypeStruct((n_rows, width), BF16), jax.ShapeDtypeStruct((n_rows, 2 * LANE), BF16),
                   jax.ShapeDtypeStruct((8, LANE), F32)),
        input_output_aliases=aliases,
        scratch_shapes=[pltpu.VMEM((ATT_BLOCK, 2 * LANE), F32)],
        compiler_params=_cparams("arbitrary"),
    )(q, kv, kv, sinks8, dout, *extra)


HEADS_PER_GROUP = SSD_HEADS // SSD_GROUPS
PAIRS_PER_GROUP = HEADS_PER_GROUP // 2
T = SSD_CHUNK


def _ssd_scalars(dtr_ref, par_ref):
    dt = _softplus(dtr_ref[...] + par_ref[0:1, :])
    a = -jnp.exp(par_ref[1:2, :])
    ri = lax.broadcasted_iota(jnp.int32, (T, T), 0)
    ci = lax.broadcasted_iota(jnp.int32, (T, T), 1)
    tril = (ri >= ci).astype(F32)
    cs = _dot_hi(tril, dt * a)
    cst = cs.T
    return dt, a, cs, cst, ri, ci


def _ssd_stacked_masks():
    row = lax.broadcasted_iota(jnp.int32, (2 * T, T), 0)
    t = row & (T - 1)
    s = lax.broadcasted_iota(jnp.int32, (2 * T, T), 1)
    return t >= s, s >= t, row[:, 0:1] < T


def _col_s(arr, k0):
    return jnp.concatenate([arr[:, k0:k0 + 1], arr[:, k0 + 1:k0 + 2]], axis=0)


def _row_s(arr_t, k0, first):
    return jnp.where(first, arr_t[k0:k0 + 1, :], arr_t[k0 + 1:k0 + 2, :])


def _lane_pick(lo, arr, k0):
    return jnp.where(lo, arr[:, k0:k0 + 1], arr[:, k0 + 1:k0 + 2])


def _ssd_fwd_stacked(xs, bm, cm, dtr, par, name):
    dtr, dt0, _ = _window(dtr)
    dtoff = dt0 // LANE
    n_rows = xs.shape[0]
    nc = n_rows // T
    gw = PAIRS_PER_GROUP * LANE

    def body(x_ref, b_ref, c_ref, dtr_ref, par_ref, y_ref, hs_ref, h_ref):
        @pl.when(pl.program_id(1) == 0)
        def _():
            h_ref[...] = jnp.zeros_like(h_ref)

        dt, a, cs, cst, _, _ = _ssd_scalars(dtr_ref, par_ref)
        tri_s, _, first = _ssd_stacked_masks()
        lo = lax.broadcasted_iota(jnp.int32, (T, LANE), 1) < SSD_CHUNK // 2
        ecs = jnp.exp(cs)
        dect = jnp.exp(cst[:, T - 1:T] - cst)
        etot = jnp.exp(cs[T - 1:T, :])
        bg = b_ref[...]
        cg = c_ref[...]
        cb = _dot(cg.astype(BF16), bg.astype(BF16), _NT)
        cb_s = jnp.concatenate([cb, cb], axis=0)
        cg_s = jnp.concatenate([cg, cg], axis=0)
        bgt_s = jnp.concatenate([bg.T, bg.T], axis=0)
        for j in range(PAIRS_PER_GROUP):
            k0, k1 = 2 * j, 2 * j + 1
            xp = x_ref[:, j * LANE:(j + 1) * LANE]
            hp = h_ref[j]
            hs_ref[0, 0, j] = hp
            rhs = jnp.concatenate([(xp * _lane_pick(lo, dt, k0)).astype(BF16), hp.astype(BF16)], axis=0)
            lm_s = jnp.exp(jnp.where(tri_s, _col_s(cs, k0) - _row_s(cst, k0, first), NEG))
            lhs = jnp.concatenate([lm_s * cb_s, cg_s * _col_s(ecs, k0)], axis=1).astype(BF16)
            y_s = _dot(lhs, rhs)
            s_s = _dot((bgt_s * _row_s(dect, k0, first)).astype(BF16), rhs[:T])
            dsk = jnp.where(lo[0:1, :], par_ref[2:3, k0:k0 + 1], par_ref[2:3, k1:k1 + 1])
            y_ref[:, j * LANE:(j + 1) * LANE] = jnp.where(lo, y_s[:T], y_s[T:]) + dsk * xp
            et = jnp.where(lo[0:1, :], etot[:, k0:k0 + 1], etot[:, k1:k1 + 1])
            h_ref[j] = hp * et + jnp.where(lo, s_s[:T], s_s[T:])

    return pl.pallas_call(
        body, name=name, grid=(SSD_GROUPS, nc),
        in_specs=[pl.BlockSpec((T, gw), lambda g, c: (c, g)),
                  pl.BlockSpec((T, SSD_STATE), lambda g, c: (c, g)),
                  pl.BlockSpec((T, SSD_STATE), lambda g, c: (c, g)),
                  pl.BlockSpec((T, LANE), lambda g, c: (c, g + dtoff)),
                  pl.BlockSpec((8, LANE), lambda g, c: (0, g))],
        out_specs=(pl.BlockSpec((T, gw), lambda g, c: (c, g)),
                   pl.BlockSpec((1, 1, PAIRS_PER_GROUP, SSD_STATE, LANE), lambda g, c: (g, c, 0, 0, 0))),
        out_shape=(jax.ShapeDtypeStruct((n_rows, SSD_D_INNER), F32),
                   jax.ShapeDtypeStruct((SSD_GROUPS, nc, PAIRS_PER_GROUP, SSD_STATE, LANE), F32)),
        scratch_shapes=[pltpu.VMEM((PAIRS_PER_GROUP, SSD_STATE, LANE), F32)],
        compiler_params=_cparams("parallel", "arbitrary"),
    )(xs, bm, cm, dtr, par)


def _ssd_bwd_stacked(xs, bm, cm, dtr, par, hs, dy, name, into):
    dtr, dt0, _ = _window(dtr)
    dtoff = dt0 // LANE
    n_rows = xs.shape[0]
    nc = n_rows // T
    gw = PAIRS_PER_GROUP * LANE

    extra, extra_specs, aliases, col0, width = _into(into, 7, 3)
    ddoff = col0 // LANE

    def body(*refs):
        x_ref, b_ref, c_ref, dtr_ref, par_ref, hs_ref, dy_ref = refs[:7]
        dx_ref, db_ref, dc_ref, ddtr_ref, acc_ref, dh_ref = refs[-6:]
        @pl.when(pl.program_id(1) == 0)
        def _():
            dh_ref[...] = jnp.zeros_like(dh_ref)
            acc_ref[...] = jnp.zeros_like(acc_ref)

        dt, a, cs, cst, ri, ci = _ssd_scalars(dtr_ref, par_ref)
        tri_s, trit_s, first = _ssd_stacked_masks()
        lane = lax.broadcasted_iota(jnp.int32, (T, LANE), 1)
        lo = lane < SSD_CHUNK // 2
        lane1 = lane[0:1, :]
        ecs = jnp.exp(cs)
        ecst = jnp.exp(cst)
        dec = jnp.exp(cs[T - 1:T, :] - cs)
        etot = jnp.exp(cs[T - 1:T, :])
        bg = b_ref[...]
        cg = c_ref[...]
        bg_b = bg.astype(BF16)
        cg_b = cg.astype(BF16)
        cb = _dot(cg_b, bg_b, _NT)
        cbt = _dot(bg_b, cg_b, _NT)
        cb_s = jnp.concatenate([cb, cb], axis=0)
        cbt_s = jnp.concatenate([cbt, cbt], axis=0)
        bg_s = jnp.concatenate([bg, bg], axis=0)
        cg_s = jnp.concatenate([cg, cg], axis=0)
        cgt_s = jnp.concatenate([cg.T, cg.T], axis=0)
        dbg = jnp.zeros((T, SSD_STATE), F32)
        dcg = jnp.zeros((T, SSD_STATE), F32)
        dcs_acc = jnp.zeros((T, LANE), F32)
        ddt_acc = jnp.zeros((T, LANE), F32)
        dsk_acc = jnp.zeros((1, LANE), F32)
        last_row = lax.broadcasted_iota(jnp.int32, (T, 1), 0) == T - 1
        for j in range(PAIRS_PER_GROUP):
            k0, k1 = 2 * j, 2 * j + 1
            xp = x_ref[:, j * LANE:(j + 1) * LANE]
            dtl = _lane_pick(lo, dt, k0)
            xdt = xp * dtl
            hp = hs_ref[0, 0, j]
            dhn = dh_ref[j]
            dyp = dy_ref[:, j * LANE:(j + 1) * LANE]
            xdt_b, hp_b, dhn_b, dyp_b = (v.astype(BF16) for v in (xdt, hp, dhn, dyp))
            cs_c, cs_r = _col_s(cs, k0), _row_s(cst, k0, first)
            lm_s = jnp.exp(jnp.where(tri_s, cs_c - cs_r, NEG))
            lmt_s = jnp.exp(jnp.where(trit_s, cs_r - cs_c, NEG))
            dec_c, ecs_c = _col_s(dec, k0), _col_s(ecs, k0)
            r1 = _dot(_stack_pair(dyp, lo), jnp.concatenate([xdt_b, hp_b], axis=0), _NT)
            r2 = _dot(_stack_pair(xdt, lo), jnp.concatenate([dyp_b, dhn_b], axis=0), _NT)
            dm_s, dyh_s = r1[:, :T], r1[:, T:]
            dmt_s, xdh_s = r2[:, :T], r2[:, T:]
            mm_s = lm_s * cb_s
            mmt_s = lmt_s * cbt_s
            bdec_s = bg_s * dec_c
            cexp_s = cg_s * ecs_c
            dx_s = _dot(jnp.concatenate([mmt_s, bdec_s], axis=1).astype(BF16),
                        jnp.concatenate([dyp_b, dhn_b], axis=0))
            dxdt = jnp.where(lo, dx_s[:T], dx_s[T:])
            dc_s = _dot((dm_s * lm_s).astype(BF16), bg_b) + dyh_s * ecs_c
            db_s = _dot((dmt_s * lmt_s).astype(BF16), cg_b) + xdh_s * dec_c
            dcg = dcg + dc_s[:T] + dc_s[T:]
            dbg = dbg + db_s[:T] + db_s[T:]
            dh_s = _dot((cgt_s * _row_s(ecst, k0, first)).astype(BF16), dyp_b)
            et = jnp.where(lo[0:1, :], etot[:, k0:k0 + 1], etot[:, k1:k1 + 1])
            dh_ref[j] = dhn * et + jnp.where(lo, dh_s[:T], dh_s[T:])
            e4 = jnp.sum(bdec_s * xdh_s, axis=-1, keepdims=True)
            dcs_s = (jnp.sum(dm_s * mm_s, axis=-1, keepdims=True) - jnp.sum(dmt_s * mmt_s, axis=-1, keepdims=True)
                     + jnp.sum(cexp_s * dyh_s, axis=-1, keepdims=True) - e4)
            hd = hp * dhn
            tsum0 = jnp.sum(e4[:T]) + etot[:, k0:k0 + 1] * jnp.sum(jnp.where(lo, hd, 0.0))
            tsum1 = jnp.sum(e4[T:]) + etot[:, k1:k1 + 1] * jnp.sum(jnp.where(lo, 0.0, hd))
            dcs0 = dcs_s[:T] + jnp.where(last_row, tsum0, 0.0)
            dcs1 = dcs_s[T:] + jnp.where(last_row, tsum1, 0.0)
            dcs_acc = jnp.where(lane == k0, dcs0, jnp.where(lane == k1, dcs1, dcs_acc))
            prod = dxdt * xp
            ddt_lo = jnp.sum(jnp.where(lo, prod, 0.0), axis=-1, keepdims=True)
            ddt_hi = jnp.sum(jnp.where(lo, 0.0, prod), axis=-1, keepdims=True)
            ddt_acc = jnp.where(lane == k0, ddt_lo, jnp.where(lane == k1, ddt_hi, ddt_acc))
            dyx = dyp * xp
            dsk_acc = jnp.where(lane1 == k0, jnp.sum(jnp.where(lo, dyx, 0.0)),
                                jnp.where(lane1 == k1, jnp.sum(jnp.where(lo, 0.0, dyx)), dsk_acc))
            dsk = jnp.where(lo[0:1, :], par_ref[2:3, k0:k0 + 1], par_ref[2:3, k1:k1 + 1])
            dx_ref[:, j * LANE:(j + 1) * LANE] = dxdt * dtl + dsk * dyp
        db_ref[...] = dbg
        dc_ref[...] = dcg
        triu = (ci >= ri).astype(F32)
        dda = _dot_hi(triu, dcs_acc)
        ddt = ddt_acc + dda * a
        ddtr = ddt * _sig(dtr_ref[...] + par_ref[0:1, :])
        ddtr_ref[...] = ddtr.astype(BF16)
        acc_ref[0:1, :] += jnp.sum(ddtr, axis=0, keepdims=True)
        acc_ref[1:2, :] += jnp.sum(dda * dt, axis=0, keepdims=True) * a
        acc_ref[2:3, :] += dsk_acc

    def rev(g, c):
        return (nc - 1 - c, g)

    return pl.pallas_call(
        body, name=name, grid=(SSD_GROUPS, nc),
        in_specs=[pl.BlockSpec((T, gw), rev),
                  pl.BlockSpec((T, SSD_STATE), rev),
                  pl.BlockSpec((T, SSD_STATE), rev),
                  pl.BlockSpec((T, LANE), lambda g, c: (nc - 1 - c, g + dtoff)),
                  pl.BlockSpec((8, LANE), lambda g, c: (0, g)),
                  pl.BlockSpec((1, 1, PAIRS_PER_GROUP, SSD_STATE, LANE), lambda g, c: (g, nc - 1 - c, 0, 0, 0)),
                  pl.BlockSpec((T, gw), rev)] + extra_specs,
        out_specs=(pl.BlockSpec((T, gw), rev),
                   pl.BlockSpec((T, SSD_STATE), rev),
                   pl.BlockSpec((T, SSD_STATE), rev),
                   pl.BlockSpec((T, LANE), lambda g, c: (nc - 1 - c, g + ddoff)),
                   pl.BlockSpec((8, LANE), lambda g, c: (0, g))),
        out_shape=(jax.ShapeDtypeStruct((n_rows, SSD_D_INNER), F32),
                   jax.ShapeDtypeStruct((n_rows, BC_DIM), F32),
                   jax.ShapeDtypeStruct((n_rows, BC_DIM), F32),
                   jax.ShapeDtypeStruct((n_rows, width), BF16),
                   jax.ShapeDtypeStruct((8, DT_PAD), F32)),
        input_output_aliases=aliases,
        scratch_shapes=[pltpu.VMEM((PAIRS_PER_GROUP, SSD_STATE, LANE), F32)],
        compiler_params=_cparams("parallel", "arbitrary"),
    )(xs, bm, cm, dtr, par, hs, dy, *extra)


def _cumsum_mm(mat, x):
    hi = x.astype(BF16)
    r = x - hi.astype(F32)
    mid = r.astype(BF16)
    lo = (r - mid.astype(F32)).astype(BF16)
    w = x.shape[1]
    out = _dot(mat, jnp.concatenate([hi, mid, lo], axis=1))
    return out[:, :w] + out[:, w:2 * w] + out[:, 2 * w:]


def _ssd_prep(dtr_ref, par_ref):
    dt = _softplus(dtr_ref[...] + par_ref[0:1, :])
    a = -jnp.exp(par_ref[1:2, :])
    ri = lax.broadcasted_iota(jnp.int32, (T, T), 0)
    ci = lax.broadcasted_iota(jnp.int32, (T, T), 1)
    cs = _cumsum_mm((ri >= ci).astype(BF16), dt * a)
    lo = lax.broadcasted_iota(jnp.int32, (T, LANE), 1) < SSD_CHUNK // 2

    def expand(arr):
        rows = arr.shape[0]
        return jnp.concatenate([jnp.where(lo[:rows], arr[:, 2 * j:2 * j + 1], arr[:, 2 * j + 1:2 * j + 2])
                                for j in range(PAIRS_PER_GROUP)], axis=1)

    tot = cs[T - 1:T, :]
    return {"dt": dt, "a": a, "cs": cs, "cst": cs.T, "lo": lo, "ri": ri, "ci": ci, "expand": expand,
            "dt_x": expand(dt), "ecs_x": expand(jnp.exp(cs)), "dec_x": expand(jnp.exp(tot - cs)),
            "et_x": expand(jnp.exp(tot)), "etot": jnp.exp(tot), "dsk_x": expand(par_ref[2:3, :])}


def _wide_masks():
    r = lax.broadcasted_iota(jnp.int32, (T, 2 * T), 0)
    l = lax.broadcasted_iota(jnp.int32, (T, 2 * T), 1)
    s = l & (T - 1)
    return r >= s, s >= r, l < T


def _wide_cs(q, k0, even):
    cs, cst = q["cs"], q["cst"]
    col = jnp.where(even, cs[:, k0:k0 + 1], cs[:, k0 + 1:k0 + 2])
    row = jnp.concatenate([cst[k0:k0 + 1, :], cst[k0 + 1:k0 + 2, :]], axis=1)
    return col, row


def _ssd_fwd(xs, bm, cm, dtr, par, name):
    dtr, dt0, _ = _window(dtr)
    dtoff = dt0 // LANE
    n_rows = xs.shape[0]
    nc = n_rows // T
    gw = PAIRS_PER_GROUP * LANE

    def body(x_ref, b_ref, c_ref, dtr_ref, par_ref, y_ref, hs_ref, h_ref):
        @pl.when(pl.program_id(1) == 0)
        def _():
            h_ref[...] = jnp.zeros_like(h_ref)

        q = _ssd_prep(dtr_ref, par_ref)
        lo = q["lo"]
        tri_w, _, even = _wide_masks()
        bg_b = b_ref[...].astype(BF16)
        cg_b = c_ref[...].astype(BF16)
        xv = x_ref[...]
        xdt = xv * q["dt_x"]
        h = h_ref[...]
        hs_ref[0, 0] = h
        yo = q["ecs_x"] * _dot(cg_b, h.astype(BF16))
        h_ref[...] = h * q["et_x"] + _dot(b_ref[...].T.astype(BF16), (xdt * q["dec_x"]).astype(BF16))
        cb = _dot(cg_b, bg_b, _NT)
        cb_w = jnp.concatenate([cb, cb], axis=1)
        for j in range(PAIRS_PER_GROUP):
            col, row = _wide_cs(q, 2 * j, even)
            m_w = (jnp.exp(jnp.where(tri_w, col - row, NEG)) * cb_w).astype(BF16)
            sl = slice(j * LANE, (j + 1) * LANE)
            y_ref[:, sl] = (_dot(m_w, _stack_pair(xdt[:, sl], lo)) + yo[:, sl] + q["dsk_x"][:, sl] * xv[:, sl])

    return pl.pallas_call(
        body, name=name, grid=(SSD_GROUPS, nc),
        in_specs=[pl.BlockSpec((T, gw), lambda g, c: (c, g)),
                  pl.BlockSpec((T, SSD_STATE), lambda g, c: (c, g)),
                  pl.BlockSpec((T, SSD_STATE), lambda g, c: (c, g)),
                  pl.BlockSpec((T, LANE), lambda g, c: (c, g + dtoff)),
                  pl.BlockSpec((8, LANE), lambda g, c: (0, g))],
        out_specs=(pl.BlockSpec((T, gw), lambda g, c: (c, g)),
                   pl.BlockSpec((1, 1, SSD_STATE, gw), lambda g, c: (g, c, 0, 0))),
        out_shape=(jax.ShapeDtypeStruct((n_rows, SSD_D_INNER), F32),
                   jax.ShapeDtypeStruct((SSD_GROUPS, nc, SSD_STATE, gw), F32)),
        scratch_shapes=[pltpu.VMEM((SSD_STATE, gw), F32)],
        compiler_params=_cparams("parallel", "arbitrary"),
    )(xs, bm, cm, dtr, par)


def _ssd_bwd(xs, bm, cm, dtr, par, hs, dy, name, into):
    dtr, dt0, _ = _window(dtr)
    dtoff = dt0 // LANE
    n_rows = xs.shape[0]
    nc = n_rows // T
    gw = PAIRS_PER_GROUP * LANE
    extra, extra_specs, aliases, col0, width = _into(into, 7, 3)
    ddoff = col0 // LANE

    def body(*refs):
        x_ref, b_ref, c_ref, dtr_ref, par_ref, hs_ref, dy_ref = refs[:7]
        dx_ref, db_ref, dc_ref, ddtr_ref, acc_ref, dh_ref = refs[-6:]

        @pl.when(pl.program_id(1) == 0)
        def _():
            dh_ref[...] = jnp.zeros_like(dh_ref)
            acc_ref[...] = jnp.zeros_like(acc_ref)

        q = _ssd_prep(dtr_ref, par_ref)
        lo, dt, a = q["lo"], q["dt"], q["a"]
        tri_w, trit_w, even = _wide_masks()
        lane = lax.broadcasted_iota(jnp.int32, (T, LANE), 1)
        lane1 = lane[0:1, :]
        last_row = lax.broadcasted_iota(jnp.int32, (T, 1), 0) == T - 1
        bg_b = b_ref[...].astype(BF16)
        cg_b = c_ref[...].astype(BF16)
        xv = x_ref[...]
        dyv = dy_ref[...]
        xdt = xv * q["dt_x"]
        h = hs_ref[0, 0]
        dhn = dh_ref[...]
        h_b, dhn_b = h.astype(BF16), dhn.astype(BF16)
        yo = q["ecs_x"] * _dot(cg_b, h_b)
        bdh = q["dec_x"] * _dot(bg_b, dhn_b)
        dye = (dyv * q["ecs_x"]).astype(BF16)
        xd = (xdt * q["dec_x"]).astype(BF16)
        dcg = _dot(dye, h_b, _NT)
        dbg = _dot(xd, dhn_b, _NT)
        dh_ref[...] = dhn * q["et_x"] + _dot(c_ref[...].T.astype(BF16), dye)
        e4_all = xdt * bdh
        f_all = dyv * yo - e4_all
        tot_row = jnp.sum(e4_all, axis=0, keepdims=True) + q["et_x"] * jnp.sum(h * dhn, axis=0, keepdims=True)
        dsk_row = jnp.sum(dyv * xv, axis=0, keepdims=True)
        cb = _dot(cg_b, bg_b, _NT)
        cbt = _dot(bg_b, cg_b, _NT)
        cb_w = jnp.concatenate([cb, cb], axis=1)
        cbt_w = jnp.concatenate([cbt, cbt], axis=1)
        dcb = jnp.zeros((T, T), F32)
        dcbt = jnp.zeros((T, T), F32)
        dcs_acc = jnp.zeros((T, LANE), F32)
        ddt_acc = jnp.zeros((T, LANE), F32)
        dsk_acc = jnp.zeros((1, LANE), F32)
        tot_acc = jnp.zeros((1, LANE), F32)
        ind_r = lax.broadcasted_iota(jnp.int32, (2 * T, LANE), 0)
        ind_l = lax.broadcasted_iota(jnp.int32, (2 * T, LANE), 1)

        def halves(t):
            return (jnp.sum(jnp.where(lo[0:1], t, 0.0), axis=-1, keepdims=True),
                    jnp.sum(jnp.where(lo[0:1], 0.0, t), axis=-1, keepdims=True))

        def split2(t):
            hi = t.astype(BF16)
            return jnp.concatenate([hi, (t - hi.astype(F32)).astype(BF16)], axis=1)

        for j in range(PAIRS_PER_GROUP):
            k0, k1 = 2 * j, 2 * j + 1
            sl = slice(j * LANE, (j + 1) * LANE)
            col, row = _wide_cs(q, k0, even)
            lm_w = jnp.exp(jnp.where(tri_w, col - row, NEG))
            lmt_w = jnp.exp(jnp.where(trit_w, row - col, NEG))
            dyp, xp = dyv[:, sl], xdt[:, sl]
            dym, xm = _stack_pair(dyp, lo), _stack_pair(xp, lo)
            dm_w = _dot(dyp.astype(BF16), xm, _NT)
            dmt_w = _dot(xp.astype(BF16), dym, _NT)
            mm_w = lm_w * cb_w
            mmt_w = lmt_w * cbt_w
            dxdt = _dot(mmt_w.astype(BF16), dym) + bdh[:, sl]
            g1 = dm_w * lm_w
            g2 = dmt_w * lmt_w
            dcb = dcb + g1[:, :T] + g1[:, T:]
            dcbt = dcbt + g2[:, :T] + g2[:, T:]
            ind_w = jnp.where(ind_l == jnp.where(ind_r < T, k0, k1), 1.0, 0.0).astype(BF16)
            ind_p = jnp.where(ind_l[:T] == jnp.where(ind_r[:T] < SSD_CHUNK // 2, k0, k1), 1.0, 0.0).astype(BF16)
            dcs_acc = dcs_acc + _dot(
                jnp.concatenate([split2(dm_w * mm_w - dmt_w * mmt_w), split2(f_all[:, sl])], axis=1),
                jnp.concatenate([ind_w, ind_w, ind_p, ind_p], axis=0))
            ddt_acc = ddt_acc + _dot(split2(dxdt * xv[:, sl]), jnp.concatenate([ind_p, ind_p], axis=0))
            tot2 = halves(tot_row[:, sl])
            tot_acc = jnp.where(lane1 == k0, tot2[0], jnp.where(lane1 == k1, tot2[1], tot_acc))
            dsk2 = halves(dsk_row[:, sl])
            dsk_acc = jnp.where(lane1 == k0, dsk2[0], jnp.where(lane1 == k1, dsk2[1], dsk_acc))
            dx_ref[:, sl] = dxdt * q["dt_x"][:, sl] + q["dsk_x"][:, sl] * dyp
        dcs_acc = dcs_acc + jnp.where(last_row, tot_acc, 0.0)
        dc_ref[...] = dcg + _dot(dcb.astype(BF16), bg_b)
        db_ref[...] = dbg + _dot(dcbt.astype(BF16), cg_b)
        dda = _cumsum_mm((q["ci"] >= q["ri"]).astype(BF16), dcs_acc)
        ddt = ddt_acc + dda * a
        ddtr = ddt * _sig(dtr_ref[...] + par_ref[0:1, :])
        ddtr_ref[...] = ddtr.astype(BF16)
        acc_ref[0:1, :] += jnp.sum(ddtr, axis=0, keepdims=True)
        acc_ref[1:2, :] += jnp.sum(dda * dt, axis=0, keepdims=True) * a
        acc_ref[2:3, :] += dsk_acc

    def rev(g, c):
        return (nc - 1 - c, g)

    return pl.pallas_call(
        body, name=name, grid=(SSD_GROUPS, nc),
        in_specs=[pl.BlockSpec((T, gw), rev),
                  pl.BlockSpec((T, SSD_STATE), rev),
                  pl.BlockSpec((T, SSD_STATE), rev),
                  pl.BlockSpec((T, LANE), lambda g, c: (nc - 1 - c, g + dtoff)),
                  pl.BlockSpec((8, LANE), lambda g, c: (0, g)),
                  pl.BlockSpec((1, 1, SSD_STATE, gw), lambda g, c: (g, nc - 1 - c, 0, 0)),
                  pl.BlockSpec((T, gw), rev)] + extra_specs,
        out_specs=(pl.BlockSpec((T, gw), rev),
                   pl.BlockSpec((T, SSD_STATE), rev),
                   pl.BlockSpec((T, SSD_STATE), rev),
                   pl.BlockSpec((T, LANE), lambda g, c: (nc - 1 - c, g + ddoff)),
                   pl.BlockSpec((8, LANE), lambda g, c: (0, g))),
        out_shape=(jax.ShapeDtypeStruct((n_rows, SSD_D_INNER), F32),
                   jax.ShapeDtypeStruct((n_rows, BC_DIM), F32),
                   jax.ShapeDtypeStruct((n_rows, BC_DIM), F32),
                   jax.ShapeDtypeStruct((n_rows, width), BF16),
                   jax.ShapeDtypeStruct((8, DT_PAD), F32)),
        input_output_aliases=aliases,
        scratch_shapes=[pltpu.VMEM((SSD_STATE, gw), F32)],
        compiler_params=_cparams("parallel", "arbitrary"),
    )(xs, bm, cm, dtr, par, hs, dy, *extra)


ADAM_ROWS = 256


def _adamw(lands, w, m, v, name):
    na = len(lands)
    n_slots, r, wd = lands[0].shape
    tr = r if r <= 2 * ADAM_ROWS else ADAM_ROWS
    nj = r // tr
    bc1 = 1.0 - ADAM_B1 ** ADAM_STEP
    bc2 = 1.0 - ADAM_B2 ** ADAM_STEP

    def body(*refs):
        l_refs = refs[:na]
        w_ref, m_ref, v_ref, g_ref, d_ref, nm_ref, nv_ref = refs[na:]
        for a in range(na):
            @pl.when(pl.program_id(0) == a)
            def _(l_ref=l_refs[a]):
                g = l_ref[0].astype(F32)
                for s in range(1, n_slots):
                    g = g + l_ref[s].astype(F32)
                mn = ADAM_B1 * m_ref[0] + (1.0 - ADAM_B1) * g
                vn = ADAM_B2 * v_ref[0] + (1.0 - ADAM_B2) * (g * g)
                mh = mn / bc1
                vh = vn / bc2
                g_ref[0] = g
                nm_ref[0] = mn
                nv_ref[0] = vn
                d_ref[0] = -ADAM_LR * (mh / (jnp.sqrt(vh) + ADAM_EPS) + ADAM_WD * w_ref[0])

    def land_spec(a):
        return pl.BlockSpec((n_slots, tr, wd),
                            lambda i, j: (0, jnp.where(i == a, j, jnp.where(i < a, 0, nj - 1)), 0))

    blk = pl.BlockSpec((1, tr, wd), lambda i, j: (i, j, 0))
    shp = jax.ShapeDtypeStruct((na, r, wd), F32)
    return pl.pallas_call(
        body, name=name, grid=(na, nj), in_specs=[land_spec(a) for a in range(na)] + [blk, blk, blk],
        out_specs=(blk, blk, blk, blk), out_shape=(shp, shp, shp, shp),
        compiler_params=_cparams("arbitrary", "arbitrary"),
    )(*lands, w, m, v)


def _mesh_pos():
    return lax.axis_index("x"), lax.axis_index("y"), lax.axis_index("c")


def _peer(pos, k):
    x, y, c = pos
    px = 1 - x if (k >> 2) & 1 else x
    py = 1 - y if (k >> 1) & 1 else y
    pc = 1 - c if k & 1 else c
    return px, py, pc


def _flat(pos):
    return 4 * pos[0] + 2 * pos[1] + pos[2]


HBM_SPEC = pl.BlockSpec(memory_space=pl.ANY)


ROW_SHARDED = ("w_ssd_out", "w_att_out", "w_mix_out", "w_ffn_down")
COL_SHARDED = ("w_in", "w_ffn_gate", "w_ffn_up")
GATHERED = ROW_SHARDED + COL_SHARDED + ("conv_w",)
BIG = ROW_SHARDED + COL_SHARDED


SEM_SPEC = pl.BlockSpec(memory_space=pltpu.SEMAPHORE)
TOKEN = jax.ShapeDtypeStruct((8, LANE), F32)
SPLIT_EFFECT = pltpu.SideEffectType.DATAFLOW_SIDE_EFFECTING
GATHER_ROWS = "gather_rows"
GATHER_SLOT = "gather_slot"
SCATTER_ROWS = "scatter_rows"
SCATTER_SLOT = "scatter_slot"


def _land_shape(kind, src):
    if kind == GATHER_ROWS:
        return (N_DEV * src.shape[0],) + src.shape[1:]
    if kind == GATHER_SLOT:
        return (N_DEV,) + src.shape
    if kind == SCATTER_ROWS:
        return (N_DEV, src.shape[0] // N_DEV) + src.shape[1:]
    return src.shape


def _views(kind, src_ref, land_ref, pos, k):
    me = _flat(pos)
    if kind == GATHER_ROWS:
        r = src_ref.shape[0]
        return src_ref, land_ref.at[pl.ds(pl.multiple_of(me * r, 16), r), :]
    if kind == GATHER_SLOT:
        return src_ref, land_ref.at[me]
    dev = _flat(_peer(pos, k))
    if kind == SCATTER_ROWS:
        r = land_ref.shape[1]
        return src_ref.at[pl.ds(pl.multiple_of(dev * r, 16), r), :], land_ref.at[k]
    return src_ref.at[dev], land_ref.at[k]


def _hbm(x):
    return pltpu.with_memory_space_constraint(x, pltpu.HBM)


def _exchange_start(items, after, name):
    kinds = [k for k, _ in items]
    srcs = [_hbm(s) for _, s in items]
    lands = [_hbm(lax.empty(_land_shape(k, s), s.dtype)) for k, s in items]
    n = len(items)
    n_copy = n * (N_DEV - 1)

    def body(*refs):
        src_refs, land_refs = refs[:n], refs[n:2 * n]
        send_sems, recv_sems = refs[2 * n + 1], refs[2 * n + 2]
        token_ref = refs[4 * n + 3]
        pos = _mesh_pos()
        for i, kind in enumerate(kinds):
            for k in range(1, N_DEV):
                s, d = _views(kind, src_refs[i], land_refs[i], pos, k)
                j = i * (N_DEV - 1) + k - 1
                pltpu.make_async_remote_copy(src_ref=s, dst_ref=d, send_sem=send_sems.at[j], recv_sem=recv_sems.at[j],
                                             device_id=_peer(pos, k), device_id_type=MESH_ID).start()
        token_ref[...] = jnp.zeros_like(token_ref)

    arrs = srcs + lands
    outs = pl.pallas_call(
        body, name=name,
        in_specs=[HBM_SPEC] * (2 * n + 1),
        out_specs=[SEM_SPEC, SEM_SPEC] + [HBM_SPEC] * (2 * n) + [pl.BlockSpec(memory_space=pltpu.VMEM)],
        out_shape=[pltpu.SemaphoreType.DMA((n_copy,)), pltpu.SemaphoreType.DMA((n_copy,))]
        + [pltpu.HBM(a.shape, a.dtype) for a in arrs] + [TOKEN],
        input_output_aliases={i: 2 + i for i in range(2 * n)},
        compiler_params=pltpu.CompilerParams(has_side_effects=SPLIT_EFFECT),
    )(*arrs, after)
    return {"kinds": kinds, "send": outs[0], "recv": outs[1], "arrs": outs[2:2 + 2 * n], "token": outs[-1]}


def _exchange_wait(ex, after, name):
    kinds = ex["kinds"]
    n = len(kinds)

    def body(*refs):
        src_refs, land_refs = refs[:n], refs[n:2 * n]
        send_sems, recv_sems = refs[2 * n], refs[2 * n + 1]
        token_ref = refs[-1]
        pos = _mesh_pos()
        for i, kind in enumerate(kinds):
            for k in range(1, N_DEV):
                s, d = _views(kind, src_refs[i], land_refs[i], pos, k)
                j = i * (N_DEV - 1) + k - 1
                cp = pltpu.make_async_remote_copy(src_ref=s, dst_ref=d, send_sem=send_sems.at[j],
                                                  recv_sem=recv_sems.at[j], device_id=_peer(pos, k),
                                                  device_id_type=MESH_ID)
                cp.wait_send()
                cp.wait_recv()
        token_ref[...] = jnp.zeros_like(token_ref)

    outs = pl.pallas_call(
        body, name=name,
        in_specs=[HBM_SPEC] * (2 * n) + [SEM_SPEC, SEM_SPEC, HBM_SPEC],
        out_specs=[HBM_SPEC] * (2 * n) + [pl.BlockSpec(memory_space=pltpu.VMEM)],
        out_shape=[pltpu.HBM(a.shape, a.dtype) for a in ex["arrs"]] + [TOKEN],
        input_output_aliases={i: i for i in range(2 * n)},
        compiler_params=pltpu.CompilerParams(has_side_effects=SPLIT_EFFECT),
    )(*ex["arrs"], ex["send"], ex["recv"], after)
    lands = [_place_own(k, s, d) for k, s, d in zip(kinds, outs[:n], outs[n:2 * n])]
    return lands, outs[-1]


def _place_own(kind, src, land):
    me = _flat(_mesh_pos())
    zeros = (0,) * (src.ndim - 1)
    if kind == GATHER_ROWS:
        return lax.dynamic_update_slice(land, src, (me * src.shape[0],) + zeros)
    if kind == GATHER_SLOT:
        return lax.dynamic_update_slice(land, src[None], (me,) + (0,) * src.ndim)
    if kind == SCATTER_ROWS:
        r = land.shape[1]
        own = lax.dynamic_slice(src, (me * r,) + zeros, (r,) + src.shape[1:])
    else:
        own = lax.dynamic_index_in_dim(src, me, 0, keepdims=False)
    return lax.dynamic_update_slice(land, own[None], (0,) * land.ndim)


def _all_gather_small(x, name):
    r, w = x.shape

    def body(x_ref, out_ref, send_sems, recv_sems):
        pos = _mesh_pos()
        me = _flat(pos)
        copies = []
        for k in range(1, N_DEV):
            cp = pltpu.make_async_remote_copy(
                src_ref=x_ref, dst_ref=out_ref.at[me], send_sem=send_sems.at[k - 1], recv_sem=recv_sems.at[k - 1],
                device_id=_peer(pos, k), device_id_type=MESH_ID)
            cp.start()
            copies.append(cp)
        out_ref[me] = x_ref[...]
        for cp in copies:
            cp.wait()

    vmem = pl.BlockSpec(memory_space=pltpu.VMEM)
    return pl.pallas_call(
        body, name=name, in_specs=[vmem], out_specs=vmem,
        out_shape=jax.ShapeDtypeStruct((N_DEV, r, w), x.dtype),
        scratch_shapes=[pltpu.SemaphoreType.DMA((N_DEV - 1,)), pltpu.SemaphoreType.DMA((N_DEV - 1,))],
        compiler_params=pltpu.CompilerParams(has_side_effects=True),
    )(x)


def _cols(g, lo, hi):
    c = g.shape[-1]
    parts = []
    for d in range(N_DEV):
        a, b = max(lo, d * c), min(hi, (d + 1) * c)
        if a < b:
            parts.append(g[d, :, a - d * c:b - d * c])
    return parts[0] if len(parts) == 1 else jnp.concatenate(parts, axis=1)


def _col_chunks(g):
    c = g.shape[-1] // N_DEV
    return jnp.stack([g[:, d * c:(d + 1) * c] for d in range(N_DEV)])


IN_PART = ("w_in", "conv_w")
OUT_PART = ROW_SHARDED + ("w_ffn_gate", "w_ffn_up")


def _gather_items(w, names, l):
    items = []
    for n in names:
        blk = w[n][l] if n == "conv_w" else w[n][l].astype(BF16)
        items.append((GATHER_ROWS if n in ROW_SHARDED else GATHER_SLOT, blk))
    return items


def _scatter_items(grads, names):
    return [(SCATTER_ROWS, grads[n]) if n in ROW_SHARDED else (SCATTER_SLOT, _col_chunks(grads[n]))
            for n in names]


SMALL = ("ln_in_g", "ln_in_b", "conv_b", "dt_bias", "a_log", "d_skip", "ssd_norm_w", "att_sinks",
         "ln_mix_g", "ln_mix_b", "ln_ffn_g", "ln_ffn_b")


def _pack_small(vals):
    flat = jnp.concatenate([vals[n].reshape(-1) for n in SMALL])
    n = flat.shape[0]
    rows = -(-n // LANE)
    rows = -(-rows // 8) * 8
    return jnp.pad(flat, (0, rows * LANE - n)).reshape(rows, LANE)


def _unpack_small(buf, shapes):
    flat = buf.reshape(-1)
    off = 0
    out = {}
    for n in SMALL:
        cnt = math.prod(shapes[n])
        out[n] = flat[off:off + cnt].reshape(shapes[n])
        off += cnt
    return out


def _to_group_major(v):
    lead = v.shape[:-1]
    t = v.reshape(lead + (SSD_GROUPS, HEADS_PER_GROUP))
    t = jnp.pad(t, [(0, 0)] * len(lead) + [(0, 0), (0, LANE - HEADS_PER_GROUP)])
    return t.reshape(lead + (DT_PAD,))


def _from_group_major(v):
    lead = v.shape[:-1]
    return v.reshape(lead + (SSD_GROUPS, LANE))[..., :HEADS_PER_GROUP].reshape(lead + (SSD_HEADS,))


def _rows8(v):
    return jnp.pad(v, ((0, 8 - v.shape[0]), (0, 0)))


IN_OFFS = {"q": (0, 1024), "kv": (1024, 1280), "z": (1280, 3328), "xs": (3328, 5376), "b": (5376, 5888),
           "c": (5888, 6400), "dt": (6400, 6432), "gl": (6432, 8480)}
PIECES = ("q", "kv", "z", "xs", "b", "c", "dt", "gl")


CAT = ("z", "xs", "gl", "q", "b", "c", "dt", "kv")
CAT_WIDTH = {"q": 1024, "z": 2048, "xs": 2048, "gl": 2048, "b": 512, "c": 512, "kv": 256, "dt": DT_PAD}
CAT_OFF = {p: sum(CAT_WIDTH[q] for q in CAT[:i]) for i, p in enumerate(CAT)}
CAT_DIM = sum(CAT_WIDTH.values())
MAIN_DIM = CAT_OFF["kv"]


def _cat_w_in(g):
    pieces = {p: _cols(g, lo, hi) for p, (lo, hi) in IN_OFFS.items()}
    pieces["dt"] = _to_group_major(pieces["dt"])
    return jnp.concatenate([pieces[p] for p in CAT], axis=1)


def _uncat_dw_in(dw):
    pieces = {p: dw[:, CAT_OFF[p]:CAT_OFF[p] + CAT_WIDTH[p]] for p in CAT}
    pieces["dt"] = _from_group_major(pieces["dt"])
    return jnp.concatenate([pieces[p] for p in PIECES], axis=1)


def _params_out(W):
    p = {n: W[n] for n in ROW_SHARDED}
    for n in ("w_ffn_gate", "w_ffn_up"):
        p[n] = _cols(W[n], 0, FFN_HIDDEN)
    return p


def _params_in(l, W, sm):
    p = {"w_cat": _cat_w_in(W["w_in"])}
    cw = _cols(W["conv_w"], 0, SSD_D_INNER + 2 * BC_DIM)
    cb = sm["conv_b"][l]
    segs = {"xs": (0, 2048), "b": (2048, 2560), "c": (2560, 3072)}
    p["conv_w8"] = {s: _rows8(cw[:, lo:hi]) for s, (lo, hi) in segs.items()}
    p["conv_b8"] = {s: _rows8(cb[None, lo:hi]) for s, (lo, hi) in segs.items()}
    p["ssd_par"] = _rows8(jnp.stack([_to_group_major(sm["dt_bias"][l]), _to_group_major(sm["a_log"][l]),
                                     _to_group_major(sm["d_skip"][l])]))
    p["norm_w"] = sm["ssd_norm_w"][l]
    p["sinks8"] = _rows8(jnp.pad(sm["att_sinks"][l], (0, LANE - ATT_HEADS))[None])
    for n in ("ln_mix_g", "ln_mix_b", "ln_ffn_g", "ln_ffn_b"):
        p[n] = sm[n][l]
    return p


def _fwd_mixers(h0, p, l, dep=None):
    tag = f"l{l}_"
    a = {"h0": h0}
    proj = _mm(h0, p["w_cat"], "nn", tag + "proj", dep=dep)
    for pc in CAT:
        a[pc] = (proj, CAT_OFF[pc], CAT_WIDTH[pc])
    for s in ("xs", "b", "c"):
        a[s + "c"] = _conv_fwd(a[s], p["conv_w8"][s], p["conv_b8"][s], tag + "conv_" + s)
    a["y"], a["hs"] = _ssd_fwd(a["xsc"], a["bc"], a["cc"], a["dt"], p["ssd_par"], tag + "ssd_fwd")
    a["yn"] = _gnorm_fwd(a["y"], a["z"], p["norm_w"], tag + "gnorm")
    a["att"] = _att_fwd(a["q"], a["kv"], p["sinks8"], tag + "att_fwd")
    return a


def _fwd_out(a, p, l, dep=None):
    tag = f"l{l}_"
    h0 = a["h0"]
    a["ya"] = _mm(a["yn"], p["w_ssd_out"], "nn", tag + "ssd_out", dep=dep)
    a["yb"] = _mm(a["att"], p["w_att_out"], "nn", tag + "att_out", dep=dep)
    a["merged"] = _merge_fwd(a["gl"], a["ya"], a["yb"], tag + "merge")
    a["mix"] = _mm(a["merged"], p["w_mix_out"], "nn", tag + "mix_out")
    a["h1"] = _ln_fwd(h0, a["mix"], p["ln_mix_g"], p["ln_mix_b"], ALPHA, tag + "ln_mix")
    a["fg"] = _mm(a["h1"], p["w_ffn_gate"], "nn", tag + "ffn_gate")
    a["fu"] = _mm(a["h1"], p["w_ffn_up"], "nn", tag + "ffn_up")
    a["act"] = _swiglu_fwd(a["fg"], a["fu"], tag + "swiglu")
    a["ffn"] = _mm(a["act"], p["w_ffn_down"], "nn", tag + "ffn_down")
    a["h2"] = _ln_fwd(a["h1"], a["ffn"], p["ln_ffn_g"], p["ln_ffn_b"], ALPHA, tag + "ln_ffn")
    return a


def _dw(x, dy, name, dep=None):
    return _mm(x, dy, "tn", name, out_dtype=BF16, dep=dep)


def _bwd_out(a, p, dh2, l, dep=None):
    tag = f"l{l}_b_"
    gw, gs = {}, {}
    du2, acc = _ln_bwd(a["h1"], a["ffn"], p["ln_ffn_g"], dh2, ALPHA, tag + "ln_ffn")
    gs["ln_ffn_g"], gs["ln_ffn_b"] = acc[0], acc[1]
    gw["w_ffn_down"] = _dw(a["act"], du2, tag + "dw_down", dep=dep)
    dact = _mm(du2, p["w_ffn_down"], "nt", tag + "dact", dep=dep)
    dfg, dfu = _swiglu_bwd(a["fg"], a["fu"], dact, tag + "swiglu")
    gw["w_ffn_gate"] = _dw(a["h1"], dfg, tag + "dw_gate")
    gw["w_ffn_up"] = _dw(a["h1"], dfu, tag + "dw_up")
    dh1 = _mm(dfg, p["w_ffn_gate"], "nt", tag + "dh1_gate", add=du2, add_scale=ALPHA)
    dh1 = _mm(dfu, p["w_ffn_up"], "nt", tag + "dh1_up", add=dh1)
    du1, acc = _ln_bwd(a["h0"], a["mix"], p["ln_mix_g"], dh1, ALPHA, tag + "ln_mix")
    gs["ln_mix_g"], gs["ln_mix_b"] = acc[0], acc[1]
    gw["w_mix_out"] = _dw(a["merged"], du1, tag + "dw_mix")
    dmerged = _mm(du1, p["w_mix_out"], "nt", tag + "dmerged")
    dya, dyb, dproj = _merge_bwd(a["gl"], a["ya"], a["yb"], dmerged, tag + "merge",
                                 (None, CAT_OFF["gl"], MAIN_DIM))
    gw["w_ssd_out"] = _dw(a["yn"], dya, tag + "dw_ssd")
    gw["w_att_out"] = _dw(a["att"], dyb, tag + "dw_att")
    return {"du1": du1, "dya": dya, "dyb": dyb, "dproj": dproj}, gw, gs


def _bwd_mixers(a, p, carry, l, dep=None):
    tag = f"l{l}_b_"
    gs = {}
    du1, dproj = carry["du1"], carry["dproj"]

    def win(pc):
        return (dproj, CAT_OFF[pc], MAIN_DIM)

    dyn = _mm(carry["dya"], p["w_ssd_out"], "nt", tag + "dyn", dep=dep)
    datt = _mm(carry["dyb"], p["w_att_out"], "nt", tag + "datt", out_dtype=BF16, dep=dep)
    dproj, dkv, acc = _att_bwd(a["q"], a["kv"], p["sinks8"], datt, tag + "att", win("q"))
    gs["att_sinks"] = acc[0, :ATT_HEADS]
    dy, dproj, acc = _gnorm_bwd(a["y"], a["z"], p["norm_w"], dyn, tag + "gnorm", win("z"))
    gs["ssd_norm_w"] = acc[0]
    dxs, dbm, dcm, dproj, acc = _ssd_bwd(a["xsc"], a["bc"], a["cc"], a["dt"], p["ssd_par"], a["hs"], dy,
                                         tag + "ssd", win("dt"))
    gs["dt_bias"], gs["a_log"], gs["d_skip"] = (_from_group_major(acc[i]) for i in range(3))
    dconv_w, dconv_b = [], []
    for s, dout in (("xs", dxs), ("b", dbm), ("c", dcm)):
        dc, acc = _conv_bwd_pre(a[s], p["conv_w8"][s], p["conv_b8"][s], dout, tag + "conv_pre_" + s)
        dconv_w.append(acc[:CONV_TAPS])
        dconv_b.append(acc[CONV_TAPS])
        dproj = _conv_bwd_in(dc, p["conv_w8"][s], tag + "conv_in_" + s, win(s))
    gconv = jnp.concatenate(dconv_w, axis=1)
    gs["conv_b"] = jnp.concatenate(dconv_b)
    w_main, w_kv = p["w_cat"][:, :MAIN_DIM], p["w_cat"][:, MAIN_DIM:]
    dw = jnp.concatenate([_dw(a["h0"], dproj, tag + "dw_in"), _dw(a["h0"], dkv, tag + "dw_in_kv")], axis=1)
    dh0 = _mm(dproj, w_main, "nt", tag + "dh0", add=du1, add_scale=ALPHA)
    dh0 = _mm(dkv, w_kv, "nt", tag + "dh0_kv", add=dh0)
    return dh0, _uncat_dw_in(dw), gconv, gs


def _step(x, target, w, m, v):
    x2 = x[0]
    t2 = target[0]
    tok = jnp.zeros(TOKEN.shape, TOKEN.dtype)

    ex = _exchange_start(_gather_items(w, IN_PART, 0), tok, "gather_l0_in_start")
    lands, tok = _exchange_wait(ex, ex["token"], "gather_l0_in_wait")
    p0 = _params_in(0, dict(zip(IN_PART, lands)), w)
    ex = _exchange_start(_gather_items(w, OUT_PART, 0) + _gather_items(w, IN_PART, 1), tok,
                         "gather_l0_out_l1_in_start")
    h = _ln_fwd(x2, None, w["ln_in_g"], w["ln_in_b"], 1.0, "ln_in")
    a0 = _fwd_mixers(h, p0, 0, dep=ex["token"])
    lands, tok = _exchange_wait(ex, a0["att"], "gather_l0_out_l1_in_wait")
    p0.update(_params_out(dict(zip(OUT_PART, lands))))
    p1 = _params_in(1, dict(zip(IN_PART, lands[len(OUT_PART):])), w)
    ex = _exchange_start(_gather_items(w, OUT_PART, 1), tok, "gather_l1_out_start")
    a0 = _fwd_out(a0, p0, 0, dep=ex["token"])
    lands, tok = _exchange_wait(ex, a0["h2"], "gather_l1_out_wait")
    p1.update(_params_out(dict(zip(OUT_PART, lands))))
    a1 = _fwd_out(_fwd_mixers(a0["h2"], p1, 1), p1, 1)

    sse, dh = _loss_fwd_bwd(a1["h2"], t2, "loss")
    loss = lax.psum(0.5 / D_MODEL * sse[0, 0], ("x", "y", "c"))

    carry, gw1, gs1 = _bwd_out(a1, p1, dh, 1)
    dh, gw1["w_in"], gw1["conv_w"], gs = _bwd_mixers(a1, p1, carry, 1)
    gs1.update(gs)
    ex1 = _exchange_start(_scatter_items(gw1, GATHERED), tok, "scatter_l1_start")
    carry, gw0, gs0 = _bwd_out(a0, p0, dh, 0, dep=ex1["token"])
    lands, tok = _exchange_wait(ex1, carry["dyb"], "scatter_l1_wait")
    land1 = dict(zip(GATHERED, lands))
    ex0 = _exchange_start(_scatter_items(gw0, OUT_PART), tok, "scatter_l0_out_start")
    dh, gw0["w_in"], gw0["conv_w"], gs = _bwd_mixers(a0, p0, carry, 0, dep=ex0["token"])
    gs0.update(gs)
    lands, tok = _exchange_wait(ex0, dh, "scatter_l0_out_wait")
    land0 = dict(zip(OUT_PART, lands))
    ex0 = _exchange_start(_scatter_items(gw0, IN_PART), tok, "scatter_l0_in_start")
    grad_x2, acc = _ln_bwd(x2, None, w["ln_in_g"], dh, 1.0, "ln_in_b")

    outs = [{} for _ in range(4)]

    def update(names):
        res = None
        for n in names:
            res = _adamw([land0[n], land1[n]], w[n], m[n], v[n], "adamw_" + n)
            for o, t in zip(outs, res):
                o[n] = t
        return res[1]

    update(OUT_PART)
    gsm = {"ln_in_g": acc[0], "ln_in_b": acc[1]}
    for n in SMALL[2:]:
        gsm[n] = jnp.stack([gs0[n], gs1[n]])
    small_shapes = {n: w[n].shape for n in SMALL}
    land_s = _all_gather_small(_pack_small(gsm), "small_grads_all_gather")
    res = _adamw([land_s], _pack_small(w)[None], _pack_small(m)[None], _pack_small(v)[None], "adamw_small")
    for o, t in zip(outs, res):
        o.update(_unpack_small(t[0], small_shapes))
    lands, _ = _exchange_wait(ex0, res[1], "scatter_l0_in_wait")
    land0.update(zip(IN_PART, lands))
    update(IN_PART)
    return loss, grad_x2[None], outs


WEIGHT_NAMES = ("ln_in_g", "ln_in_b", "w_in", "conv_w", "conv_b", "dt_bias", "a_log", "d_skip", "ssd_norm_w",
                "att_sinks", "w_ssd_out", "w_att_out", "w_mix_out", "ln_mix_g", "ln_mix_b", "w_ffn_gate",
                "w_ffn_up", "w_ffn_down", "ln_ffn_g", "ln_ffn_b")


def kernel(x, ln_in_g, ln_in_b, w_in, conv_w, conv_b, dt_bias, a_log, d_skip, ssd_norm_w, att_sinks, w_ssd_out, w_att_out, w_mix_out, ln_mix_g, ln_mix_b, w_ffn_gate, w_ffn_up, w_ffn_down, ln_ffn_g, ln_ffn_b, loss_target, m_ln_in_g, m_ln_in_b, m_w_in, m_conv_w, m_conv_b, m_dt_bias, m_a_log, m_d_skip, m_ssd_norm_w, m_att_sinks, m_w_ssd_out, m_w_att_out, m_w_mix_out, m_ln_mix_g, m_ln_mix_b, m_w_ffn_gate, m_w_ffn_up, m_w_ffn_down, m_ln_ffn_g, m_ln_ffn_b, v_ln_in_g, v_ln_in_b, v_w_in, v_conv_w, v_conv_b, v_dt_bias, v_a_log, v_d_skip, v_ssd_norm_w, v_att_sinks, v_w_ssd_out, v_w_att_out, v_w_mix_out, v_ln_mix_g, v_ln_mix_b, v_w_ffn_gate, v_w_ffn_up, v_w_ffn_down, v_ln_ffn_g, v_ln_ffn_b):
    w = dict(zip(WEIGHT_NAMES, (ln_in_g, ln_in_b, w_in, conv_w, conv_b, dt_bias, a_log, d_skip, ssd_norm_w,
                                att_sinks, w_ssd_out, w_att_out, w_mix_out, ln_mix_g, ln_mix_b, w_ffn_gate,
                                w_ffn_up, w_ffn_down, ln_ffn_g, ln_ffn_b)))
    m = dict(zip(WEIGHT_NAMES, (m_ln_in_g, m_ln_in_b, m_w_in, m_conv_w, m_conv_b, m_dt_bias, m_a_log, m_d_skip,
                                m_ssd_norm_w, m_att_sinks, m_w_ssd_out, m_w_att_out, m_w_mix_out, m_ln_mix_g,
                                m_ln_mix_b, m_w_ffn_gate, m_w_ffn_up, m_w_ffn_down, m_ln_ffn_g, m_ln_ffn_b)))
    v = dict(zip(WEIGHT_NAMES, (v_ln_in_g, v_ln_in_b, v_w_in, v_conv_w, v_conv_b, v_dt_bias, v_a_log, v_d_skip,
                                v_ssd_norm_w, v_att_sinks, v_w_ssd_out, v_w_att_out, v_w_mix_out, v_ln_mix_g,
                                v_ln_mix_b, v_w_ffn_gate, v_w_ffn_up, v_w_ffn_down, v_ln_ffn_g, v_ln_ffn_b)))
    loss, grad_x, outs = _step(x, loss_target, w, m, v)
    result = [loss, grad_x]
    for o in outs:
        result.extend(o[n] for n in WEIGHT_NAMES)
    return tuple(result)
```

```python
import functools
import math

import jax
import jax.numpy as jnp
from jax import lax
from jax.experimental import pallas as pl
from jax.experimental.pallas import tpu as pltpu

F32 = jnp.float32
BF16 = jnp.bfloat16

D_MODEL = 1024
DEPTH = 2
N_DEV = 8
ATT_HEADS = 16
ATT_KV_HEADS = 2
ATT_HEAD_DIM = 64
ATT_BLOCK = 128
SSD_D_INNER = 2048
SSD_HEADS = 32
SSD_GROUPS = 4
SSD_STATE = 128
SSD_CHUNK = 128
FFN_HIDDEN = 2816
LN_EPS = 1e-5
RMS_EPS = 1e-5
ALPHA = (2 * DEPTH) ** 0.25
Q_DIM = 1024
KV_DIM = 128
BC_DIM = 512
IN_DIM = 8480
IN_SHARD = IN_DIM // N_DEV
DT_PAD = 512

ADAM_LR = 0.001
ADAM_B1 = 0.9
ADAM_B2 = 0.999
ADAM_EPS = 1e-08
ADAM_WD = 0.01
ADAM_STEP = 10

LANE = 128
VMEM_LIMIT = 48 * 1024 * 1024
PACK_W = 1024
NEG = -1e30

_NN = (((1,), (0,)), ((), ()))
_NT = (((1,), (1,)), ((), ()))
_TN = (((0,), (0,)), ((), ()))
MESH_ID = pl.DeviceIdType.MESH


def _dot(a, b, dims=_NN):
    return lax.dot_general(a, b, dims, preferred_element_type=F32)


def _dot_hi(a, b):
    return lax.dot_general(a, b, _NN, preferred_element_type=F32, precision=lax.Precision.HIGHEST)


def _sig(x):
    return 1.0 / (1.0 + jnp.exp(-x))


def _softplus(x):
    return jnp.maximum(x, 0.0) + jnp.log(1.0 + jnp.exp(-jnp.abs(x)))


def _cparams(*sem):
    return pltpu.CompilerParams(dimension_semantics=sem, vmem_limit_bytes=VMEM_LIMIT)


def _pick(n, cap):
    if n <= cap:
        return n
    best = None
    for t in range(LANE, cap + 1, LANE):
        if n % t == 0:
            best = t
    assert best is not None, (n, cap)
    return best


def _tile(n):
    if n <= 1024 or n % 1024 == 0:
        return min(n, 1024)
    return _pick(n, 1408)


def _rows(n):
    return min(512, n)


def _window(x):
    return x if isinstance(x, tuple) else (x, 0, x.shape[1])


def _into(into, n_in, out_idx):
    buf, col0, width = into
    if buf is None:
        return [], [], {}, col0, width
    return [buf], [pl.BlockSpec(memory_space=pl.ANY)], {n_in: out_idx}, col0, width


def _mm(a, b, mode, name, add=None, add_scale=1.0, out_dtype=F32, dep=None):
    if mode == "nn":
        m, k = a.shape
        n = b.shape[1]
    elif mode == "nt":
        m, k = a.shape
        n = b.shape[0]
    else:
        k, m = a.shape
        n = b.shape[1]
    tm = _tile(m)
    tn = _pick(n, 2176) if mode == "tn" and n > 1024 else _tile(n)
    tk = _pick(k, 2176) if mode == "nt" and a.dtype == BF16 and k > 2816 else _tile(k)
    nk = k // tk
    has_add = add is not None
    dims = {"nn": _NN, "nt": _NT, "tn": _TN}[mode]

    def body(*refs):
        if dep is not None:
            refs = refs[:-3] + refs[-2:]
        if has_add:
            a_ref, b_ref, add_ref, o_ref, acc_ref = refs
        else:
            a_ref, b_ref, o_ref, acc_ref = refs
        kk = pl.program_id(2)

        @pl.when(kk == 0)
        def _():
            if has_add:
                acc_ref[...] = add_scale * add_ref[...].astype(F32)
            else:
                acc_ref[...] = jnp.zeros_like(acc_ref)

        acc_ref[...] += _dot(a_ref[...].astype(BF16), b_ref[...].astype(BF16), dims)

        @pl.when(kk == nk - 1)
        def _():
            o_ref[...] = acc_ref[...].astype(o_ref.dtype)

    if mode == "nn":
        a_spec = pl.BlockSpec((tm, tk), lambda i, j, kk: (i, kk))
        b_spec = pl.BlockSpec((tk, tn), lambda i, j, kk: (kk, j))
    elif mode == "nt":
        a_spec = pl.BlockSpec((tm, tk), lambda i, j, kk: (i, kk))
        b_spec = pl.BlockSpec((tn, tk), lambda i, j, kk: (j, kk))
    else:
        a_spec = pl.BlockSpec((tk, tm), lambda i, j, kk: (kk, i))
        b_spec = pl.BlockSpec((tk, tn), lambda i, j, kk: (kk, j))
    o_spec = pl.BlockSpec((tm, tn), lambda i, j, kk: (i, j))
    in_specs = [a_spec, b_spec] + ([o_spec] if has_add else [])
    args = (a, b) + ((add,) if has_add else ())
    if dep is not None:
        in_specs.append(pl.BlockSpec((8, LANE), lambda i, j, kk: (0, 0)))
        args += (dep,)
    return pl.pallas_call(
        body, name=name, grid=(m // tm, n // tn, nk),
        in_specs=in_specs, out_specs=o_spec,
        out_shape=jax.ShapeDtypeStruct((m, n), out_dtype),
        scratch_shapes=[pltpu.VMEM((tm, tn), F32)],
        compiler_params=_cparams("parallel", "parallel", "arbitrary"),
    )(*args)


def _vec_spec(width):
    return pl.BlockSpec((1, width), lambda i: (0, 0))


def _ln_fwd(a, b, gamma, beta, alpha, name):
    n_rows, dm = a.shape
    has_b = b is not None

    def body(*refs):
        if has_b:
            a_ref, b_ref, g_ref, be_ref, o_ref = refs
            u = alpha * a_ref[...] + b_ref[...]
        else:
            a_ref, g_ref, be_ref, o_ref = refs
            u = a_ref[...]
        mu = jnp.mean(u, axis=-1, keepdims=True)
        d = u - mu
        var = jnp.mean(d * d, axis=-1, keepdims=True)
        o_ref[...] = d * lax.rsqrt(var + LN_EPS) * g_ref[...] + be_ref[...]

    row = pl.BlockSpec((_rows(n_rows),dm), lambda i: (i, 0))
    in_specs = [row] + ([row] if has_b else []) + [_vec_spec(dm), _vec_spec(dm)]
    args = (a,) + ((b,) if has_b else ()) + (gamma.reshape(1, dm), beta.reshape(1, dm))
    return pl.pallas_call(
        body, name=name, grid=(n_rows // _rows(n_rows),), in_specs=in_specs, out_specs=row,
        out_shape=jax.ShapeDtypeStruct((n_rows, dm), F32),
        compiler_params=_cparams("parallel"),
    )(*args)


def _ln_bwd(a, b, gamma, dy, alpha, name):
    n_rows, dm = a.shape
    has_b = b is not None

    def body(*refs):
        if has_b:
            a_ref, b_ref, g_ref, dy_ref, du_ref, acc_ref = refs
            u = alpha * a_ref[...] + b_ref[...]
        else:
            a_ref, g_ref, dy_ref, du_ref, acc_ref = refs
            u = a_ref[...]

        @pl.when(pl.program_id(0) == 0)
        def _():
            acc_ref[...] = jnp.zeros_like(acc_ref)

        mu = jnp.mean(u, axis=-1, keepdims=True)
        d = u - mu
        var = jnp.mean(d * d, axis=-1, keepdims=True)
        rstd = lax.rsqrt(var + LN_EPS)
        xhat = d * rstd
        dyv = dy_ref[...]
        acc_ref[0:1, :] += jnp.sum(dyv * xhat, axis=0, keepdims=True)
        acc_ref[1:2, :] += jnp.sum(dyv, axis=0, keepdims=True)
        dxh = dyv * g_ref[...]
        m1 = jnp.mean(dxh, axis=-1, keepdims=True)
        m2 = jnp.mean(dxh * xhat, axis=-1, keepdims=True)
        du_ref[...] = rstd * (dxh - m1 - xhat * m2)

    row = pl.BlockSpec((_rows(n_rows),dm), lambda i: (i, 0))
    in_specs = [row] + ([row] if has_b else []) + [_vec_spec(dm), row]
    args = (a,) + ((b,) if has_b else ()) + (gamma.reshape(1, dm), dy)
    return pl.pallas_call(
        body, name=name, grid=(n_rows // _rows(n_rows),), in_specs=in_specs,
        out_specs=(row, pl.BlockSpec((8, dm), lambda i: (0, 0))),
        out_shape=(jax.ShapeDtypeStruct((n_rows, dm), F32), jax.ShapeDtypeStruct((8, dm), F32)),
        compiler_params=_cparams("arbitrary"),
    )(*args)


def _loss_fwd_bwd(y, target, name):
    n_rows, dm = y.shape

    def body(y_ref, t_ref, acc_ref, dy_ref):
        @pl.when(pl.program_id(0) == 0)
        def _():
            acc_ref[...] = jnp.zeros_like(acc_ref)

        d = y_ref[...] - t_ref[...]
        acc_ref[...] += jnp.sum(d * d)
        dy_ref[...] = d * (1.0 / dm)

    row = pl.BlockSpec((_rows(n_rows),dm), lambda i: (i, 0))
    return pl.pallas_call(
        body, name=name, grid=(n_rows // _rows(n_rows),), in_specs=[row, row],
        out_specs=(pl.BlockSpec((8, LANE), lambda i: (0, 0)), row),
        out_shape=(jax.ShapeDtypeStruct((8, LANE), F32), jax.ShapeDtypeStruct((n_rows, dm), F32)),
        compiler_params=_cparams("arbitrary"),
    )(y, target)


def _swiglu_fwd(g, u, name):
    n_rows, w = g.shape
    tw = _pick(w, 1408)

    def body(g_ref, u_ref, o_ref):
        gv = g_ref[...]
        o_ref[...] = (gv * _sig(gv) * u_ref[...]).astype(BF16)

    blk = pl.BlockSpec((_rows(n_rows),tw), lambda i, j: (i, j))
    return pl.pallas_call(
        body, name=name, grid=(n_rows // _rows(n_rows), w // tw), in_specs=[blk, blk], out_specs=blk,
        out_shape=jax.ShapeDtypeStruct((n_rows, w), BF16),
        compiler_params=_cparams("parallel", "parallel"),
    )(g, u)


def _swiglu_bwd(g, u, dact, name):
    n_rows, w = g.shape
    tw = _pick(w, 1408)

    def body(g_ref, u_ref, da_ref, dg_ref, du_ref):
        gv = g_ref[...]
        s = _sig(gv)
        da = da_ref[...]
        dg_ref[...] = (da * u_ref[...] * (s * (1.0 + gv * (1.0 - s)))).astype(BF16)
        du_ref[...] = (da * gv * s).astype(BF16)

    blk = pl.BlockSpec((_rows(n_rows),tw), lambda i, j: (i, j))
    return pl.pallas_call(
        body, name=name, grid=(n_rows // _rows(n_rows), w // tw), in_specs=[blk, blk, blk], out_specs=(blk, blk),
        out_shape=(jax.ShapeDtypeStruct((n_rows, w), BF16), jax.ShapeDtypeStruct((n_rows, w), BF16)),
        compiler_params=_cparams("parallel", "parallel"),
    )(g, u, dact)


def _gate_specs(gl, n_rows, dm):
    arr, g0, _ = _window(gl)
    return arr, [pl.BlockSpec((_rows(n_rows), dm), lambda i, k=k: (i, g0 // dm + k)) for k in range(2)]


def _merge_fwd(gl, ya, yb, name):
    n_rows, dm = ya.shape
    gl_arr, gspecs = _gate_specs(gl, n_rows, dm)

    def body(ga_ref, gb_ref, ya_ref, yb_ref, o_ref):
        o_ref[...] = (_sig(ga_ref[...]) * ya_ref[...] + _sig(gb_ref[...]) * yb_ref[...]).astype(BF16)

    row = pl.BlockSpec((_rows(n_rows),dm), lambda i: (i, 0))
    return pl.pallas_call(
        body, name=name, grid=(n_rows // _rows(n_rows),), in_specs=gspecs + [row, row], out_specs=row,
        out_shape=jax.ShapeDtypeStruct((n_rows, dm), BF16),
        compiler_params=_cparams("parallel"),
    )(gl_arr, gl_arr, ya, yb)


def _merge_bwd(gl, ya, yb, dmerged, name, into):
    n_rows, dm = ya.shape
    gl_arr, gspecs = _gate_specs(gl, n_rows, dm)
    extra, extra_specs, aliases, col0, width = _into(into, 5, 2)

    def body(*refs):
        ga_ref, gb_ref, ya_ref, yb_ref, dm_ref = refs[:5]
        dya_ref, dyb_ref, dgl_ref = refs[-3:]
        ga = _sig(ga_ref[...])
        gb = _sig(gb_ref[...])
        dmv = dm_ref[...]
        dya_ref[...] = (dmv * ga).astype(BF16)
        dyb_ref[...] = (dmv * gb).astype(BF16)
        dgl_ref[:, :dm] = (dmv * ya_ref[...] * ga * (1.0 - ga)).astype(BF16)
        dgl_ref[:, dm:] = (dmv * yb_ref[...] * gb * (1.0 - gb)).astype(BF16)

    row = pl.BlockSpec((_rows(n_rows),dm), lambda i: (i, 0))
    row2 = pl.BlockSpec((_rows(n_rows),2 * dm), lambda i: (i, col0 // (2 * dm)))
    return pl.pallas_call(
        body, name=name, grid=(n_rows // _rows(n_rows),), in_specs=gspecs + [row, row, row] + extra_specs,
        out_specs=(row, row, row2),
        out_shape=(jax.ShapeDtypeStruct((n_rows, dm), BF16), jax.ShapeDtypeStruct((n_rows, dm), BF16),
                   jax.ShapeDtypeStruct((n_rows, width), BF16)),
        input_output_aliases=aliases,
        compiler_params=_cparams("parallel"),
    )(gl_arr, gl_arr, ya, yb, dmerged, *extra)


CONV_TAPS = 4
CONV_COLS = 512
HALO = 8


def _shift_down(cur, prev8, s, row8):
    r = pltpu.roll(cur, s, axis=0)
    top = jnp.where(row8 < s, pltpu.roll(prev8, s, axis=0), r[0:HALO])
    return jnp.concatenate([top, r[HALO:]], axis=0)


def _shift_up(cur, next8, s, row8):
    n = cur.shape[0]
    r = pltpu.roll(cur, n - s, axis=0)
    bot = jnp.where(row8 >= HALO - s, pltpu.roll(next8, HALO - s, axis=0), r[n - HALO:])
    return jnp.concatenate([r[:n - HALO], bot], axis=0)


def _conv_pre(u_ref, prev_ref, w_ref, b_ref, li):
    cur = u_ref[...]
    prev8 = jnp.where(li == 0, 0.0, prev_ref[...])
    row8 = lax.broadcasted_iota(jnp.int32, prev8.shape, 0)
    shifted = [cur] + [_shift_down(cur, prev8, s, row8) for s in range(1, CONV_TAPS)]
    acc = b_ref[...] + shifted[0] * w_ref[CONV_TAPS - 1:CONV_TAPS, :]
    for s in range(1, CONV_TAPS):
        acc = acc + shifted[s] * w_ref[CONV_TAPS - 1 - s:CONV_TAPS - s, :]
    return acc, shifted


def _conv_specs(n_rows, tl, col0=0):
    off = col0 // CONV_COLS
    cur = pl.BlockSpec((tl, CONV_COLS), lambda cj, li: (li, cj + off))
    prev = pl.BlockSpec((HALO, CONV_COLS), lambda cj, li: (jnp.maximum(li * (tl // HALO) - 1, 0), cj + off))
    nxt = pl.BlockSpec((HALO, CONV_COLS),
                       lambda cj, li: (jnp.minimum((li + 1) * (tl // HALO), n_rows // HALO - 1), cj + off))
    par = pl.BlockSpec((8, CONV_COLS), lambda cj, li: (0, cj + off))
    return cur, prev, nxt, par


def _conv_fwd(u, w8, b8, name):
    u, u0, c = _window(u)
    n_rows = u.shape[0]
    tl = _rows(n_rows)
    cur, _, _, par = _conv_specs(n_rows, tl)
    ucur, prev, _, _ = _conv_specs(n_rows, tl, u0)

    def body(u_ref, prev_ref, w_ref, b_ref, o_ref):
        acc, _ = _conv_pre(u_ref, prev_ref, w_ref, b_ref[0:1, :], pl.program_id(1))
        o_ref[...] = acc * _sig(acc)

    return pl.pallas_call(
        body, name=name, grid=(c // CONV_COLS, n_rows // tl), in_specs=[ucur, prev, par, par], out_specs=cur,
        out_shape=jax.ShapeDtypeStruct((n_rows, c), F32),
        compiler_params=_cparams("parallel", "parallel"),
    )(u, u, w8, b8)


def _conv_bwd_pre(u, w8, b8, dout, name):
    u, u0, c = _window(u)
    n_rows = u.shape[0]
    tl = _rows(n_rows)
    cur, _, _, par = _conv_specs(n_rows, tl)
    ucur, prev, _, _ = _conv_specs(n_rows, tl, u0)

    def body(u_ref, prev_ref, w_ref, b_ref, do_ref, dc_ref, acc_ref):
        @pl.when(pl.program_id(1) == 0)
        def _():
            acc_ref[...] = jnp.zeros_like(acc_ref)

        acc, shifted = _conv_pre(u_ref, prev_ref, w_ref, b_ref[0:1, :], pl.program_id(1))
        sg = _sig(acc)
        dc = do_ref[...] * (sg * (1.0 + acc * (1.0 - sg)))
        dc_ref[...] = dc
        for k in range(CONV_TAPS):
            acc_ref[k:k + 1, :] += jnp.sum(dc * shifted[CONV_TAPS - 1 - k], axis=0, keepdims=True)
        acc_ref[CONV_TAPS:CONV_TAPS + 1, :] += jnp.sum(dc, axis=0, keepdims=True)

    return pl.pallas_call(
        body, name=name, grid=(c // CONV_COLS, n_rows // tl), in_specs=[ucur, prev, par, par, cur],
        out_specs=(cur, par),
        out_shape=(jax.ShapeDtypeStruct((n_rows, c), F32), jax.ShapeDtypeStruct((8, c), F32)),
        compiler_params=_cparams("parallel", "arbitrary"),
    )(u, u, w8, b8, dout)


def _conv_bwd_in(dc, w8, name, into):
    n_rows, c = dc.shape
    tl = _rows(n_rows)
    cur, _, nxt, par = _conv_specs(n_rows, tl)
    n_l = n_rows // tl
    extra, extra_specs, aliases, col0, width = _into(into, 3, 0)
    out_spec = _conv_specs(n_rows, tl, col0)[0]

    def body(*refs):
        dc_ref, next_ref, w_ref = refs[:3]
        o_ref = refs[-1]
        cur_v = dc_ref[...]
        next8 = jnp.where(pl.program_id(1) == n_l - 1, 0.0, next_ref[...])
        row8 = lax.broadcasted_iota(jnp.int32, next8.shape, 0)
        acc = cur_v * w_ref[CONV_TAPS - 1:CONV_TAPS, :]
        for s in range(1, CONV_TAPS):
            acc = acc + _shift_up(cur_v, next8, s, row8) * w_ref[CONV_TAPS - 1 - s:CONV_TAPS - s, :]
        o_ref[...] = acc.astype(BF16)

    return pl.pallas_call(
        body, name=name, grid=(c // CONV_COLS, n_l), in_specs=[cur, nxt, par] + extra_specs, out_specs=out_spec,
        out_shape=jax.ShapeDtypeStruct((n_rows, width), BF16), input_output_aliases=aliases,
        compiler_params=_cparams("parallel", "parallel"),
    )(dc, dc, w8, *extra)


NORM_GROUP = SSD_D_INNER // SSD_GROUPS


def _gnorm_fwd(y, z, w, name):
    n_rows, c = y.shape
    z, z0, _ = _window(z)
    zoff = z0 // NORM_GROUP

    def body(y_ref, z_ref, w_ref, o_ref):
        zv = z_ref[...]
        yg = y_ref[...] * (zv * _sig(zv))
        r = lax.rsqrt(jnp.mean(yg * yg, axis=-1, keepdims=True) + RMS_EPS)
        o_ref[...] = (yg * r * w_ref[...]).astype(BF16)

    blk = pl.BlockSpec((_rows(n_rows),NORM_GROUP), lambda i, j: (i, j))
    zblk = pl.BlockSpec((_rows(n_rows),NORM_GROUP), lambda i, j: (i, j + zoff))
    wspec = pl.BlockSpec((1, NORM_GROUP), lambda i, j: (0, j))
    return pl.pallas_call(
        body, name=name, grid=(n_rows // _rows(n_rows), c // NORM_GROUP), in_specs=[blk, zblk, wspec], out_specs=blk,
        out_shape=jax.ShapeDtypeStruct((n_rows, c), BF16),
        compiler_params=_cparams("parallel", "parallel"),
    )(y, z, w.reshape(1, c))


def _gnorm_bwd(y, z, w, dyn, name, into):
    n_rows, c = y.shape
    z, z0, _ = _window(z)
    zoff = z0 // NORM_GROUP
    extra, extra_specs, aliases, col0, width = _into(into, 4, 1)
    doff = col0 // NORM_GROUP

    def body(*refs):
        y_ref, z_ref, w_ref, dn_ref = refs[:4]
        dy_ref, dz_ref, acc_ref = refs[-3:]
        @pl.when(pl.program_id(1) == 0)
        def _():
            acc_ref[...] = jnp.zeros_like(acc_ref)

        zv = z_ref[...]
        yv = y_ref[...]
        sz = _sig(zv)
        silu = zv * sz
        yg = yv * silu
        r = lax.rsqrt(jnp.mean(yg * yg, axis=-1, keepdims=True) + RMS_EPS)
        nrm = yg * r
        dn = dn_ref[...]
        acc_ref[0:1, :] += jnp.sum(dn * nrm, axis=0, keepdims=True)
        dnw = dn * w_ref[...]
        dyg = r * (dnw - nrm * jnp.mean(dnw * nrm, axis=-1, keepdims=True))
        dy_ref[...] = dyg * silu
        dz_ref[...] = (dyg * yv * (sz * (1.0 + zv * (1.0 - sz)))).astype(BF16)

    blk = pl.BlockSpec((_rows(n_rows),NORM_GROUP), lambda j, i: (i, j))
    zblk = pl.BlockSpec((_rows(n_rows),NORM_GROUP), lambda j, i: (i, j + zoff))
    wspec = pl.BlockSpec((1, NORM_GROUP), lambda j, i: (0, j))
    aspec = pl.BlockSpec((8, NORM_GROUP), lambda j, i: (0, j))
    return pl.pallas_call(
        body, name=name, grid=(c // NORM_GROUP, n_rows // _rows(n_rows)),
        in_specs=[blk, zblk, wspec, blk] + extra_specs,
        out_specs=(blk, pl.BlockSpec((_rows(n_rows), NORM_GROUP), lambda j, i: (i, j + doff)), aspec),
        out_shape=(jax.ShapeDtypeStruct((n_rows, c), F32), jax.ShapeDtypeStruct((n_rows, width), BF16),
                   jax.ShapeDtypeStruct((8, c), F32)),
        input_output_aliases=aliases,
        compiler_params=_cparams("parallel", "arbitrary"),
    )(y, z, w.reshape(1, c), dyn, *extra)


ATT_SCALE = ATT_HEAD_DIM ** -0.5
ATT_SLOPES = [2.0 ** (-8.0 * (h + 1) / ATT_HEADS) for h in range(ATT_HEADS)]
Q_PER_KV = ATT_HEADS // ATT_KV_HEADS


def _dup_half(t, g, lo):
    tr = pltpu.roll(t, ATT_HEAD_DIM, axis=1)
    return jnp.where(lo, t, tr) if g == 0 else jnp.where(lo, tr, t)


def _att_band(kv_ref, kvp_ref, n):
    cur = kv_ref[...]
    prev = jnp.where(n == 0, 0.0, kvp_ref[...])
    lo = lax.broadcasted_iota(jnp.int32, (ATT_BLOCK, LANE), 1) < ATT_HEAD_DIM
    bands = []
    for g in range(ATT_KV_HEADS):
        kb = jnp.concatenate([_dup_half(prev[:, :LANE], g, lo), _dup_half(cur[:, :LANE], g, lo)], axis=0)
        vb = jnp.concatenate([_dup_half(prev[:, LANE:], g, lo), _dup_half(cur[:, LANE:], g, lo)], axis=0)
        bands.append((kb.astype(BF16), vb.astype(BF16)))
    return bands


def _att_tile(n):
    shape = (2 * ATT_BLOCK, ATT_BLOCK)
    row = lax.broadcasted_iota(jnp.int32, shape, 0)
    i = row & (ATT_BLOCK - 1)
    s = lax.broadcasted_iota(jnp.int32, shape, 1)
    upper = s > i
    dist = ((i - s) & (ATT_BLOCK - 1)).astype(F32)
    dead = upper & (n == 0)
    return upper, dist, dead, row[:, 0:1] < ATT_BLOCK


def _stack_pair(t, lo):
    return jnp.concatenate([jnp.where(lo, t, 0.0), jnp.where(lo, 0.0, t)], axis=0).astype(BF16)


def _att_exp(qs, kb, s_ref, j, tile):
    upper, dist, dead, first = tile
    s2 = _dot(qs, kb, _NT)
    slope = jnp.where(first, ATT_SLOPES[2 * j], ATT_SLOPES[2 * j + 1])
    sink = jnp.where(first, s_ref[0:1, 2 * j:2 * j + 1], s_ref[0:1, 2 * j + 1:2 * j + 2])
    s = jnp.where(upper, s2[:, :ATT_BLOCK], s2[:, ATT_BLOCK:]) - slope * dist
    s = jnp.where(dead, NEG, s)
    m = jnp.maximum(jnp.max(s, axis=-1, keepdims=True), sink)
    return jnp.exp(s - m), jnp.exp(sink - m)


def _band_split(t, upper):
    return jnp.concatenate([jnp.where(upper, t, 0.0), jnp.where(upper, 0.0, t)], axis=1)


def _att_fwd(q, kv, sinks8, name):
    q, q0, _ = _window(q)
    kv, kv0, _ = _window(kv)
    qoff, kvoff = q0 // Q_DIM, kv0 // (2 * LANE)
    n_rows = q.shape[0]
    nb = n_rows // ATT_BLOCK

    def body(q_ref, kv_ref, kvp_ref, s_ref, o_ref):
        n = pl.program_id(0)
        bands = _att_band(kv_ref, kvp_ref, n)
        lo = lax.broadcasted_iota(jnp.int32, (ATT_BLOCK, LANE), 1) < ATT_HEAD_DIM
        tile = _att_tile(n)
        ones_b = jnp.ones((2 * ATT_BLOCK, LANE), BF16)
        for j in range(ATT_HEADS // 2):
            kb, vb = bands[2 * j // Q_PER_KV]
            qs = _stack_pair(q_ref[:, j * LANE:(j + 1) * LANE] * ATT_SCALE, lo)
            p, es = _att_exp(qs, kb, s_ref, j, tile)
            pv = _dot(_band_split(p, tile[0]).astype(BF16), jnp.concatenate([vb, ones_b], axis=1))
            out = pv[:, :LANE] / (pv[:, LANE:] + es)
            o_ref[:, j * LANE:(j + 1) * LANE] = jnp.where(lo, out[:ATT_BLOCK], out[ATT_BLOCK:]).astype(BF16)

    return pl.pallas_call(
        body, name=name, grid=(nb,),
        in_specs=[pl.BlockSpec((ATT_BLOCK, Q_DIM), lambda n: (n, qoff)),
                  pl.BlockSpec((ATT_BLOCK, 2 * LANE), lambda n: (n, kvoff)),
                  pl.BlockSpec((ATT_BLOCK, 2 * LANE), lambda n: (jnp.maximum(n - 1, 0), kvoff)),
                  pl.BlockSpec((8, LANE), lambda n: (0, 0))],
        out_specs=pl.BlockSpec((ATT_BLOCK, Q_DIM), lambda n: (n, 0)),
        out_shape=jax.ShapeDtypeStruct((n_rows, Q_DIM), BF16),
        compiler_params=_cparams("parallel"),
    )(q, kv, kv, sinks8)


def _att_bwd(q, kv, sinks8, att, dout, name, into):
    q, q0, _ = _window(q)
    kv, kv0, _ = _window(kv)
    qoff, kvoff = q0 // Q_DIM, kv0 // (2 * LANE)
    n_rows = q.shape[0]
    nb = n_rows // ATT_BLOCK

    extra, extra_specs, aliases, col0, width = _into(into, 6, 0)
    dqoff = col0 // Q_DIM

    def body(*refs):
        q_ref, kv_ref, kvp_ref, s_ref, o_ref, do_ref = refs[:6]
        dq_ref, dkv_ref, acc_ref, carry_ref = refs[-4:]
        n = pl.program_id(0)

        @pl.when(n == 0)
        def _():
            acc_ref[...] = jnp.zeros_like(acc_ref)
            carry_ref[...] = jnp.zeros_like(carry_ref)

        @pl.when(n == nb)
        def _():
            dkv_ref[...] = carry_ref[...].astype(BF16)

        @pl.when(n < nb)
        def _():
            bands = _att_band(kv_ref, kvp_ref, n)
            lo = lax.broadcasted_iota(jnp.int32, (ATT_BLOCK, LANE), 1) < ATT_HEAD_DIM
            lane1 = lax.broadcasted_iota(jnp.int32, (1, LANE), 1)
            tile = _att_tile(n)
            upper, first = tile[0], tile[3]
            ones_b = jnp.ones((ATT_BLOCK, LANE), BF16)
            ones2_b = jnp.ones((2 * LANE, LANE), BF16)
            dk_acc = [jnp.zeros((2 * ATT_BLOCK, LANE), F32) for _ in range(ATT_KV_HEADS)]
            dv_acc = [jnp.zeros((2 * ATT_BLOCK, LANE), F32) for _ in range(ATT_KV_HEADS)]
            dsink = jnp.zeros((1, LANE), F32)
            for j in range(ATT_HEADS // 2):
                g = 2 * j // Q_PER_KV
                kb, vb = bands[g]
                qs = _stack_pair(q_ref[:, j * LANE:(j + 1) * LANE] * ATT_SCALE, lo)
                dop = do_ref[:, j * LANE:(j + 1) * LANE].astype(F32)
                dos = _stack_pair(dop, lo)
                pu, es = _att_exp(qs, kb, s_ref, j, tile)
                inv = 1.0 / (_dot(pu.astype(BF16), ones_b) + es)
                p = pu * inv
                od = dop * o_ref[:, j * LANE:(j + 1) * LANE].astype(F32)
                od = jnp.concatenate([jnp.where(lo, od, 0.0), jnp.where(lo, 0.0, od)], axis=0)
                od_hi = od.astype(BF16)
                delta = _dot(jnp.concatenate([od_hi, (od - od_hi.astype(F32)).astype(BF16)], axis=1), ones2_b)
                dp2 = _dot(dos, vb, _NT)
                dp = jnp.where(upper, dp2[:, :ATT_BLOCK], dp2[:, ATT_BLOCK:])
                ds2 = _band_split(p * (dp - delta), upper)
                psd = jnp.sum(es * inv * delta, axis=0, keepdims=True)
                psd0 = jnp.sum(jnp.where(first, es * inv * delta, 0.0), axis=0, keepdims=True)
                dsink = jnp.where(lane1 == 2 * j, -psd0, jnp.where(lane1 == 2 * j + 1, psd0 - psd, dsink))
                dq = _dot(ds2.astype(BF16), kb) * ATT_SCALE
                dq_ref[:, j * LANE:(j + 1) * LANE] = jnp.where(lo, dq[:ATT_BLOCK], dq[ATT_BLOCK:]).astype(BF16)
                dk_acc[g] = dk_acc[g] + _dot(ds2.T.astype(BF16), qs)
                dv_acc[g] = dv_acc[g] + _dot(_band_split(p, upper).T.astype(BF16), dos)
            acc_ref[0:1, :] += dsink
            lo2 = lax.broadcasted_iota(jnp.int32, (2 * ATT_BLOCK, LANE), 1) < ATT_HEAD_DIM
            folded = []
            for acc in (dk_acc, dv_acc):
                t0 = acc[0] + pltpu.roll(acc[0], ATT_HEAD_DIM, axis=1)
                t1 = acc[1] + pltpu.roll(acc[1], ATT_HEAD_DIM, axis=1)
                folded.append(jnp.where(lo2, t0, t1))
            band = jnp.concatenate(folded, axis=1)
            dkv_ref[...] = (carry_ref[...] + band[:ATT_BLOCK]).astype(BF16)
            carry_ref[...] = band[ATT_BLOCK:]

    def qmap(n):
        return (jnp.minimum(n, nb - 1), 0)

    return pl.pallas_call(
        body, name=name, grid=(nb + 1,),
        in_specs=[pl.BlockSpec((ATT_BLOCK, Q_DIM), lambda n: (jnp.minimum(n, nb - 1), qoff)),
                  pl.BlockSpec((ATT_BLOCK, 2 * LANE), lambda n: (jnp.minimum(n, nb - 1), kvoff)),
                  pl.BlockSpec((ATT_BLOCK, 2 * LANE),
                               lambda n: (jnp.maximum(jnp.minimum(n, nb - 1) - 1, 0), kvoff)),
                  pl.BlockSpec((8, LANE), lambda n: (0, 0)),
                  pl.BlockSpec((ATT_BLOCK, Q_DIM), qmap),
                  pl.BlockSpec((ATT_BLOCK, Q_DIM), qmap)] + extra_specs,
        out_specs=(pl.BlockSpec((ATT_BLOCK, Q_DIM), lambda n: (jnp.minimum(n, nb - 1), dqoff)),
                   pl.BlockSpec((ATT_BLOCK, 2 * LANE), lambda n: (jnp.maximum(n - 1, 0), 0)),
                   pl.BlockSpec((8, LANE), lambda n: (0, 0))),
        out_shape=(jax.ShapeDtypeStruct((n_rows, width), BF16), jax.ShapeDtypeStruct((n_rows, 2 * LANE), BF16),
                   jax.ShapeDtypeStruct((8, LANE), F32)),
        input_output_aliases=aliases,
        scratch_shapes=[pltpu.VMEM((ATT_BLOCK, 2 * LANE), F32)],
        compiler_params=_cparams("arbitrary"),
    )(q, kv, kv, sinks8, att, dout, *extra)


HEADS_PER_GROUP = SSD_HEADS // SSD_GROUPS
PAIRS_PER_GROUP = HEADS_PER_GROUP // 2
T = SSD_CHUNK


def _ssd_scalars(dtr_ref, par_ref):
    dt = _softplus(dtr_ref[...] + par_ref[0:1, :])
    a = -jnp.exp(par_ref[1:2, :])
    ri = lax.broadcasted_iota(jnp.int32, (T, T), 0)
    ci = lax.broadcasted_iota(jnp.int32, (T, T), 1)
    tril = (ri >= ci).astype(F32)
    cs = _dot_hi(tril, dt * a)
    cst = cs.T
    return dt, a, cs, cst, ri, ci


def _ssd_stacked_masks():
    row = lax.broadcasted_iota(jnp.int32, (2 * T, T), 0)
    t = row & (T - 1)
    s = lax.broadcasted_iota(jnp.int32, (2 * T, T), 1)
    return t >= s, s >= t, row[:, 0:1] < T


def _col_s(arr, k0):
    return jnp.concatenate([arr[:, k0:k0 + 1], arr[:, k0 + 1:k0 + 2]], axis=0)


def _row_s(arr_t, k0, first):
    return jnp.where(first, arr_t[k0:k0 + 1, :], arr_t[k0 + 1:k0 + 2, :])


def _lane_pick(lo, arr, k0):
    return jnp.where(lo, arr[:, k0:k0 + 1], arr[:, k0 + 1:k0 + 2])


def _ssd_fwd_stacked(xs, bm, cm, dtr, par, name):
    dtr, dt0, _ = _window(dtr)
    dtoff = dt0 // LANE
    n_rows = xs.shape[0]
    nc = n_rows // T
    gw = PAIRS_PER_GROUP * LANE

    def body(x_ref, b_ref, c_ref, dtr_ref, par_ref, y_ref, hs_ref, h_ref):
        @pl.when(pl.program_id(1) == 0)
        def _():
            h_ref[...] = jnp.zeros_like(h_ref)

        dt, a, cs, cst, _, _ = _ssd_scalars(dtr_ref, par_ref)
        tri_s, _, first = _ssd_stacked_masks()
        lo = lax.broadcasted_iota(jnp.int32, (T, LANE), 1) < SSD_CHUNK // 2
        ecs = jnp.exp(cs)
        dect = jnp.exp(cst[:, T - 1:T] - cst)
        etot = jnp.exp(cs[T - 1:T, :])
        bg = b_ref[...]
        cg = c_ref[...]
        cb = _dot(cg.astype(BF16), bg.astype(BF16), _NT)
        cb_s = jnp.concatenate([cb, cb], axis=0)
        cg_s = jnp.concatenate([cg, cg], axis=0)
        bgt_s = jnp.concatenate([bg.T, bg.T], axis=0)
        for j in range(PAIRS_PER_GROUP):
            k0, k1 = 2 * j, 2 * j + 1
            xp = x_ref[:, j * LANE:(j + 1) * LANE]
            hp = h_ref[j]
            hs_ref[0, 0, j] = hp
            rhs = jnp.concatenate([(xp * _lane_pick(lo, dt, k0)).astype(BF16), hp.astype(BF16)], axis=0)
            lm_s = jnp.exp(jnp.where(tri_s, _col_s(cs, k0) - _row_s(cst, k0, first), NEG))
            lhs = jnp.concatenate([lm_s * cb_s, cg_s * _col_s(ecs, k0)], axis=1).astype(BF16)
            y_s = _dot(lhs, rhs)
            s_s = _dot((bgt_s * _row_s(dect, k0, first)).astype(BF16), rhs[:T])
            dsk = jnp.where(lo[0:1, :], par_ref[2:3, k0:k0 + 1], par_ref[2:3, k1:k1 + 1])
            y_ref[:, j * LANE:(j + 1) * LANE] = jnp.where(lo, y_s[:T], y_s[T:]) + dsk * xp
            et = jnp.where(lo[0:1, :], etot[:, k0:k0 + 1], etot[:, k1:k1 + 1])
            h_ref[j] = hp * et + jnp.where(lo, s_s[:T], s_s[T:])

    return pl.pallas_call(
        body, name=name, grid=(SSD_GROUPS, nc),
        in_specs=[pl.BlockSpec((T, gw), lambda g, c: (c, g)),
                  pl.BlockSpec((T, SSD_STATE), lambda g, c: (c, g)),
                  pl.BlockSpec((T, SSD_STATE), lambda g, c: (c, g)),
                  pl.BlockSpec((T, LANE), lambda g, c: (c, g + dtoff)),
                  pl.BlockSpec((8, LANE), lambda g, c: (0, g))],
        out_specs=(pl.BlockSpec((T, gw), lambda g, c: (c, g)),
                   pl.BlockSpec((1, 1, PAIRS_PER_GROUP, SSD_STATE, LANE), lambda g, c: (g, c, 0, 0, 0))),
        out_shape=(jax.ShapeDtypeStruct((n_rows, SSD_D_INNER), F32),
                   jax.ShapeDtypeStruct((SSD_GROUPS, nc, PAIRS_PER_GROUP, SSD_STATE, LANE), F32)),
        scratch_shapes=[pltpu.VMEM((PAIRS_PER_GROUP, SSD_STATE, LANE), F32)],
        compiler_params=_cparams("parallel", "arbitrary"),
    )(xs, bm, cm, dtr, par)


def _ssd_bwd_stacked(xs, bm, cm, dtr, par, hs, dy, name, into):
    dtr, dt0, _ = _window(dtr)
    dtoff = dt0 // LANE
    n_rows = xs.shape[0]
    nc = n_rows // T
    gw = PAIRS_PER_GROUP * LANE

    extra, extra_specs, aliases, col0, width = _into(into, 7, 3)
    ddoff = col0 // LANE

    def body(*refs):
        x_ref, b_ref, c_ref, dtr_ref, par_ref, hs_ref, dy_ref = refs[:7]
        dx_ref, db_ref, dc_ref, ddtr_ref, acc_ref, dh_ref = refs[-6:]
        @pl.when(pl.program_id(1) == 0)
        def _():
            dh_ref[...] = jnp.zeros_like(dh_ref)
            acc_ref[...] = jnp.zeros_like(acc_ref)

        dt, a, cs, cst, ri, ci = _ssd_scalars(dtr_ref, par_ref)
        tri_s, trit_s, first = _ssd_stacked_masks()
        lane = lax.broadcasted_iota(jnp.int32, (T, LANE), 1)
        lo = lane < SSD_CHUNK // 2
        lane1 = lane[0:1, :]
        ecs = jnp.exp(cs)
        ecst = jnp.exp(cst)
        dec = jnp.exp(cs[T - 1:T, :] - cs)
        etot = jnp.exp(cs[T - 1:T, :])
        bg = b_ref[...]
        cg = c_ref[...]
        bg_b = bg.astype(BF16)
        cg_b = cg.astype(BF16)
        cb = _dot(cg_b, bg_b, _NT)
        cbt = _dot(bg_b, cg_b, _NT)
        cb_s = jnp.concatenate([cb, cb], axis=0)
        cbt_s = jnp.concatenate([cbt, cbt], axis=0)
        bg_s = jnp.concatenate([bg, bg], axis=0)
        cg_s = jnp.concatenate([cg, cg], axis=0)
        cgt_s = jnp.concatenate([cg.T, cg.T], axis=0)
        dbg = jnp.zeros((T, SSD_STATE), F32)
        dcg = jnp.zeros((T, SSD_STATE), F32)
        dcs_acc = jnp.zeros((T, LANE), F32)
        ddt_acc = jnp.zeros((T, LANE), F32)
        dsk_acc = jnp.zeros((1, LANE), F32)
        last_row = lax.broadcasted_iota(jnp.int32, (T, 1), 0) == T - 1
        for j in range(PAIRS_PER_GROUP):
            k0, k1 = 2 * j, 2 * j + 1
            xp = x_ref[:, j * LANE:(j + 1) * LANE]
            dtl = _lane_pick(lo, dt, k0)
            xdt = xp * dtl
            hp = hs_ref[0, 0, j]
            dhn = dh_ref[j]
            dyp = dy_ref[:, j * LANE:(j + 1) * LANE]
            xdt_b, hp_b, dhn_b, dyp_b = (v.astype(BF16) for v in (xdt, hp, dhn, dyp))
            cs_c, cs_r = _col_s(cs, k0), _row_s(cst, k0, first)
            lm_s = jnp.exp(jnp.where(tri_s, cs_c - cs_r, NEG))
            lmt_s = jnp.exp(jnp.where(trit_s, cs_r - cs_c, NEG))
            dec_c, ecs_c = _col_s(dec, k0), _col_s(ecs, k0)
            r1 = _dot(_stack_pair(dyp, lo), jnp.concatenate([xdt_b, hp_b], axis=0), _NT)
            r2 = _dot(_stack_pair(xdt, lo), jnp.concatenate([dyp_b, dhn_b], axis=0), _NT)
            dm_s, dyh_s = r1[:, :T], r1[:, T:]
            dmt_s, xdh_s = r2[:, :T], r2[:, T:]
            mm_s = lm_s * cb_s
            mmt_s = lmt_s * cbt_s
            bdec_s = bg_s * dec_c
            cexp_s = cg_s * ecs_c
            dx_s = _dot(jnp.concatenate([mmt_s, bdec_s], axis=1).astype(BF16),
                        jnp.concatenate([dyp_b, dhn_b], axis=0))
            dxdt = jnp.where(lo, dx_s[:T], dx_s[T:])
            dc_s = _dot((dm_s * lm_s).astype(BF16), bg_b) + dyh_s * ecs_c
            db_s = _dot((dmt_s * lmt_s).astype(BF16), cg_b) + xdh_s * dec_c
            dcg = dcg + dc_s[:T] + dc_s[T:]
            dbg = dbg + db_s[:T] + db_s[T:]
            dh_s = _dot((cgt_s * _row_s(ecst, k0, first)).astype(BF16), dyp_b)
            et = jnp.where(lo[0:1, :], etot[:, k0:k0 + 1], etot[:, k1:k1 + 1])
            dh_ref[j] = dhn * et + jnp.where(lo, dh_s[:T], dh_s[T:])
            e4 = jnp.sum(bdec_s * xdh_s, axis=-1, keepdims=True)
            dcs_s = (jnp.sum(dm_s * mm_s, axis=-1, keepdims=True) - jnp.sum(dmt_s * mmt_s, axis=-1, keepdims=True)
                     + jnp.sum(cexp_s * dyh_s, axis=-1, keepdims=True) - e4)
            hd = hp * dhn
            tsum0 = jnp.sum(e4[:T]) + etot[:, k0:k0 + 1] * jnp.sum(jnp.where(lo, hd, 0.0))
            tsum1 = jnp.sum(e4[T:]) + etot[:, k1:k1 + 1] * jnp.sum(jnp.where(lo, 0.0, hd))
            dcs0 = dcs_s[:T] + jnp.where(last_row, tsum0, 0.0)
            dcs1 = dcs_s[T:] + jnp.where(last_row, tsum1, 0.0)
            dcs_acc = jnp.where(lane == k0, dcs0, jnp.where(lane == k1, dcs1, dcs_acc))
            prod = dxdt * xp
            ddt_lo = jnp.sum(jnp.where(lo, prod, 0.0), axis=-1, keepdims=True)
            ddt_hi = jnp.sum(jnp.where(lo, 0.0, prod), axis=-1, keepdims=True)
            ddt_acc = jnp.where(lane == k0, ddt_lo, jnp.where(lane == k1, ddt_hi, ddt_acc))
            dyx = dyp * xp
            dsk_acc = jnp.where(lane1 == k0, jnp.sum(jnp.where(lo, dyx, 0.0)),
                                jnp.where(lane1 == k1, jnp.sum(jnp.where(lo, 0.0, dyx)), dsk_acc))
            dsk = jnp.where(lo[0:1, :], par_ref[2:3, k0:k0 + 1], par_ref[2:3, k1:k1 + 1])
            dx_ref[:, j * LANE:(j + 1) * LANE] = dxdt * dtl + dsk * dyp
        db_ref[...] = dbg
        dc_ref[...] = dcg
        triu = (ci >= ri).astype(F32)
        dda = _dot_hi(triu, dcs_acc)
        ddt = ddt_acc + dda * a
        ddtr = ddt * _sig(dtr_ref[...] + par_ref[0:1, :])
        ddtr_ref[...] = ddtr.astype(BF16)
        acc_ref[0:1, :] += jnp.sum(ddtr, axis=0, keepdims=True)
        acc_ref[1:2, :] += jnp.sum(dda * dt, axis=0, keepdims=True) * a
        acc_ref[2:3, :] += dsk_acc

    def rev(g, c):
        return (nc - 1 - c, g)

    return pl.pallas_call(
        body, name=name, grid=(SSD_GROUPS, nc),
        in_specs=[pl.BlockSpec((T, gw), rev),
                  pl.BlockSpec((T, SSD_STATE), rev),
                  pl.BlockSpec((T, SSD_STATE), rev),
                  pl.BlockSpec((T, LANE), lambda g, c: (nc - 1 - c, g + dtoff)),
                  pl.BlockSpec((8, LANE), lambda g, c: (0, g)),
                  pl.BlockSpec((1, 1, PAIRS_PER_GROUP, SSD_STATE, LANE), lambda g, c: (g, nc - 1 - c, 0, 0, 0)),
                  pl.BlockSpec((T, gw), rev)] + extra_specs,
        out_specs=(pl.BlockSpec((T, gw), rev),
                   pl.BlockSpec((T, SSD_STATE), rev),
                   pl.BlockSpec((T, SSD_STATE), rev),
                   pl.BlockSpec((T, LANE), lambda g, c: (nc - 1 - c, g + ddoff)),
                   pl.BlockSpec((8, LANE), lambda g, c: (0, g))),
        out_shape=(jax.ShapeDtypeStruct((n_rows, SSD_D_INNER), F32),
                   jax.ShapeDtypeStruct((n_rows, BC_DIM), F32),
                   jax.ShapeDtypeStruct((n_rows, BC_DIM), F32),
                   jax.ShapeDtypeStruct((n_rows, width), BF16),
                   jax.ShapeDtypeStruct((8, DT_PAD), F32)),
        input_output_aliases=aliases,
        scratch_shapes=[pltpu.VMEM((PAIRS_PER_GROUP, SSD_STATE, LANE), F32)],
        compiler_params=_cparams("parallel", "arbitrary"),
    )(xs, bm, cm, dtr, par, hs, dy, *extra)


def _cumsum_mm(mat, x):
    hi = x.astype(BF16)
    r = x - hi.astype(F32)
    mid = r.astype(BF16)
    lo = (r - mid.astype(F32)).astype(BF16)
    w = x.shape[1]
    out = _dot(mat, jnp.concatenate([hi, mid, lo], axis=1))
    return out[:, :w] + out[:, w:2 * w] + out[:, 2 * w:]


def _ssd_prep(dtr_ref, par_ref):
    dt = _softplus(dtr_ref[...] + par_ref[0:1, :])
    a = -jnp.exp(par_ref[1:2, :])
    ri = lax.broadcasted_iota(jnp.int32, (T, T), 0)
    ci = lax.broadcasted_iota(jnp.int32, (T, T), 1)
    cs = _cumsum_mm((ri >= ci).astype(BF16), dt * a)
    lo = lax.broadcasted_iota(jnp.int32, (T, LANE), 1) < SSD_CHUNK // 2

    def expand(arr):
        rows = arr.shape[0]
        return jnp.concatenate([jnp.where(lo[:rows], arr[:, 2 * j:2 * j + 1], arr[:, 2 * j + 1:2 * j + 2])
                                for j in range(PAIRS_PER_GROUP)], axis=1)

    tot = cs[T - 1:T, :]
    return {"dt": dt, "a": a, "cs": cs, "cst": cs.T, "lo": lo, "ri": ri, "ci": ci, "expand": expand,
            "dt_x": expand(dt), "ecs_x": expand(jnp.exp(cs)), "dec_x": expand(jnp.exp(tot - cs)),
            "et_x": expand(jnp.exp(tot)), "etot": jnp.exp(tot), "dsk_x": expand(par_ref[2:3, :])}


def _wide_masks():
    r = lax.broadcasted_iota(jnp.int32, (T, 2 * T), 0)
    l = lax.broadcasted_iota(jnp.int32, (T, 2 * T), 1)
    s = l & (T - 1)
    return r >= s, s >= r, l < T


def _wide_cs(q, k0, even):
    cs, cst = q["cs"], q["cst"]
    col = jnp.where(even, cs[:, k0:k0 + 1], cs[:, k0 + 1:k0 + 2])
    row = jnp.concatenate([cst[k0:k0 + 1, :], cst[k0 + 1:k0 + 2, :]], axis=1)
    return col, row


def _ssd_fwd(xs, bm, cm, dtr, par, name):
    dtr, dt0, _ = _window(dtr)
    dtoff = dt0 // LANE
    n_rows = xs.shape[0]
    nc = n_rows // T
    gw = PAIRS_PER_GROUP * LANE

    def body(x_ref, b_ref, c_ref, dtr_ref, par_ref, y_ref, hs_ref, h_ref):
        @pl.when(pl.program_id(1) == 0)
        def _():
            h_ref[...] = jnp.zeros_like(h_ref)

        q = _ssd_prep(dtr_ref, par_ref)
        lo = q["lo"]
        tri_w, _, even = _wide_masks()
        bg_b = b_ref[...].astype(BF16)
        cg_b = c_ref[...].astype(BF16)
        xv = x_ref[...]
        xdt = xv * q["dt_x"]
        h = h_ref[...]
        hs_ref[0, 0] = h
        yo = q["ecs_x"] * _dot(cg_b, h.astype(BF16))
        h_ref[...] = h * q["et_x"] + _dot(b_ref[...].T.astype(BF16), (xdt * q["dec_x"]).astype(BF16))
        cb = _dot(cg_b, bg_b, _NT)
        cb_w = jnp.concatenate([cb, cb], axis=1)
        for j in range(PAIRS_PER_GROUP):
            col, row = _wide_cs(q, 2 * j, even)
            m_w = (jnp.exp(jnp.where(tri_w, col - row, NEG)) * cb_w).astype(BF16)
            sl = slice(j * LANE, (j + 1) * LANE)
            y_ref[:, sl] = (_dot(m_w, _stack_pair(xdt[:, sl], lo)) + yo[:, sl] + q["dsk_x"][:, sl] * xv[:, sl])

    return pl.pallas_call(
        body, name=name, grid=(SSD_GROUPS, nc),
        in_specs=[pl.BlockSpec((T, gw), lambda g, c: (c, g)),
                  pl.BlockSpec((T, SSD_STATE), lambda g, c: (c, g)),
                  pl.BlockSpec((T, SSD_STATE), lambda g, c: (c, g)),
                  pl.BlockSpec((T, LANE), lambda g, c: (c, g + dtoff)),
                  pl.BlockSpec((8, LANE), lambda g, c: (0, g))],
        out_specs=(pl.BlockSpec((T, gw), lambda g, c: (c, g)),
                   pl.BlockSpec((1, 1, SSD_STATE, gw), lambda g, c: (g, c, 0, 0))),
        out_shape=(jax.ShapeDtypeStruct((n_rows, SSD_D_INNER), F32),
                   jax.ShapeDtypeStruct((SSD_GROUPS, nc, SSD_STATE, gw), F32)),
        scratch_shapes=[pltpu.VMEM((SSD_STATE, gw), F32)],
        compiler_params=_cparams("parallel", "arbitrary"),
    )(xs, bm, cm, dtr, par)


def _ssd_bwd(xs, bm, cm, dtr, par, hs, dy, name, into):
    dtr, dt0, _ = _window(dtr)
    dtoff = dt0 // LANE
    n_rows = xs.shape[0]
    nc = n_rows // T
    gw = PAIRS_PER_GROUP * LANE
    extra, extra_specs, aliases, col0, width = _into(into, 7, 3)
    ddoff = col0 // LANE

    def body(*refs):
        x_ref, b_ref, c_ref, dtr_ref, par_ref, hs_ref, dy_ref = refs[:7]
        dx_ref, db_ref, dc_ref, ddtr_ref, acc_ref, dh_ref = refs[-6:]

        @pl.when(pl.program_id(1) == 0)
        def _():
            dh_ref[...] = jnp.zeros_like(dh_ref)
            acc_ref[...] = jnp.zeros_like(acc_ref)

        q = _ssd_prep(dtr_ref, par_ref)
        lo, dt, a = q["lo"], q["dt"], q["a"]
        tri_w, trit_w, even = _wide_masks()
        lane = lax.broadcasted_iota(jnp.int32, (T, LANE), 1)
        lane1 = lane[0:1, :]
        last_row = lax.broadcasted_iota(jnp.int32, (T, 1), 0) == T - 1
        bg_b = b_ref[...].astype(BF16)
        cg_b = c_ref[...].astype(BF16)
        xv = x_ref[...]
        dyv = dy_ref[...]
        xdt = xv * q["dt_x"]
        h = hs_ref[0, 0]
        dhn = dh_ref[...]
        h_b, dhn_b = h.astype(BF16), dhn.astype(BF16)
        yo = q["ecs_x"] * _dot(cg_b, h_b)
        bdh = q["dec_x"] * _dot(bg_b, dhn_b)
        dye = (dyv * q["ecs_x"]).astype(BF16)
        xd = (xdt * q["dec_x"]).astype(BF16)
        dcg = _dot(dye, h_b, _NT)
        dbg = _dot(xd, dhn_b, _NT)
        dh_ref[...] = dhn * q["et_x"] + _dot(c_ref[...].T.astype(BF16), dye)
        e4_all = xdt * bdh
        f_all = dyv * yo - e4_all
        tot_row = jnp.sum(e4_all, axis=0, keepdims=True) + q["et_x"] * jnp.sum(h * dhn, axis=0, keepdims=True)
        dsk_row = jnp.sum(dyv * xv, axis=0, keepdims=True)
        cb = _dot(cg_b, bg_b, _NT)
        cbt = _dot(bg_b, cg_b, _NT)
        cb_w = jnp.concatenate([cb, cb], axis=1)
        cbt_w = jnp.concatenate([cbt, cbt], axis=1)
        dcb = jnp.zeros((T, T), F32)
        dcbt = jnp.zeros((T, T), F32)
        dcs_acc = jnp.zeros((T, LANE), F32)
        ddt_acc = jnp.zeros((T, LANE), F32)
        dsk_acc = jnp.zeros((1, LANE), F32)
        tot_acc = jnp.zeros((1, LANE), F32)
        ind_r = lax.broadcasted_iota(jnp.int32, (2 * T, LANE), 0)
        ind_l = lax.broadcasted_iota(jnp.int32, (2 * T, LANE), 1)

        def halves(t):
            return (jnp.sum(jnp.where(lo[0:1], t, 0.0), axis=-1, keepdims=True),
                    jnp.sum(jnp.where(lo[0:1], 0.0, t), axis=-1, keepdims=True))

        def split2(t):
            hi = t.astype(BF16)
            return jnp.concatenate([hi, (t - hi.astype(F32)).astype(BF16)], axis=1)

        for j in range(PAIRS_PER_GROUP):
            k0, k1 = 2 * j, 2 * j + 1
            sl = slice(j * LANE, (j + 1) * LANE)
            col, row = _wide_cs(q, k0, even)
            lm_w = jnp.exp(jnp.where(tri_w, col - row, NEG))
            lmt_w = jnp.exp(jnp.where(trit_w, row - col, NEG))
            dyp, xp = dyv[:, sl], xdt[:, sl]
            dym, xm = _stack_pair(dyp, lo), _stack_pair(xp, lo)
            dm_w = _dot(dyp.astype(BF16), xm, _NT)
            dmt_w = _dot(xp.astype(BF16), dym, _NT)
            mm_w = lm_w * cb_w
            mmt_w = lmt_w * cbt_w
            dxdt = _dot(mmt_w.astype(BF16), dym) + bdh[:, sl]
            g1 = dm_w * lm_w
            g2 = dmt_w * lmt_w
            dcb = dcb + g1[:, :T] + g1[:, T:]
            dcbt = dcbt + g2[:, :T] + g2[:, T:]
            ind_w = jnp.where(ind_l == jnp.where(ind_r < T, k0, k1), 1.0, 0.0).astype(BF16)
            ind_p = jnp.where(ind_l[:T] == jnp.where(ind_r[:T] < SSD_CHUNK // 2, k0, k1), 1.0, 0.0).astype(BF16)
            dcs_acc = dcs_acc + _dot(
                jnp.concatenate([split2(dm_w * mm_w - dmt_w * mmt_w), split2(f_all[:, sl])], axis=1),
                jnp.concatenate([ind_w, ind_w, ind_p, ind_p], axis=0))
            ddt_acc = ddt_acc + _dot(split2(dxdt * xv[:, sl]), jnp.concatenate([ind_p, ind_p], axis=0))
            tot2 = halves(tot_row[:, sl])
            tot_acc = jnp.where(lane1 == k0, tot2[0], jnp.where(lane1 == k1, tot2[1], tot_acc))
            dsk2 = halves(dsk_row[:, sl])
            dsk_acc = jnp.where(lane1 == k0, dsk2[0], jnp.where(lane1 == k1, dsk2[1], dsk_acc))
            dx_ref[:, sl] = dxdt * q["dt_x"][:, sl] + q["dsk_x"][:, sl] * dyp
        dcs_acc = dcs_acc + jnp.where(last_row, tot_acc, 0.0)
        dc_ref[...] = dcg + _dot(dcb.astype(BF16), bg_b)
        db_ref[...] = dbg + _dot(dcbt.astype(BF16), cg_b)
        dda = _cumsum_mm((q["ci"] >= q["ri"]).astype(BF16), dcs_acc)
        ddt = ddt_acc + dda * a
        ddtr = ddt * _sig(dtr_ref[...] + par_ref[0:1, :])
        ddtr_ref[...] = ddtr.astype(BF16)
        acc_ref[0:1, :] += jnp.sum(ddtr, axis=0, keepdims=True)
        acc_ref[1:2, :] += jnp.sum(dda * dt, axis=0, keepdims=True) * a
        acc_ref[2:3, :] += dsk_acc

    def rev(g, c):
        return (nc - 1 - c, g)

    return pl.pallas_call(
        body, name=name, grid=(SSD_GROUPS, nc),
        in_specs=[pl.BlockSpec((T, gw), rev),
                  pl.BlockSpec((T, SSD_STATE), rev),
                  pl.BlockSpec((T, SSD_STATE), rev),
                  pl.BlockSpec((T, LANE), lambda g, c: (nc - 1 - c, g + dtoff)),
                  pl.BlockSpec((8, LANE), lambda g, c: (0, g)),
                  pl.BlockSpec((1, 1, SSD_STATE, gw), lambda g, c: (g, nc - 1 - c, 0, 0)),
                  pl.BlockSpec((T, gw), rev)] + extra_specs,
        out_specs=(pl.BlockSpec((T, gw), rev),
                   pl.BlockSpec((T, SSD_STATE), rev),
                   pl.BlockSpec((T, SSD_STATE), rev),
                   pl.BlockSpec((T, LANE), lambda g, c: (nc - 1 - c, g + ddoff)),
                   pl.BlockSpec((8, LANE), lambda g, c: (0, g))),
        out_shape=(jax.ShapeDtypeStruct((n_rows, SSD_D_INNER), F32),
                   jax.ShapeDtypeStruct((n_rows, BC_DIM), F32),
                   jax.ShapeDtypeStruct((n_rows, BC_DIM), F32),
                   jax.ShapeDtypeStruct((n_rows, width), BF16),
                   jax.ShapeDtypeStruct((8, DT_PAD), F32)),
        input_output_aliases=aliases,
        scratch_shapes=[pltpu.VMEM((SSD_STATE, gw), F32)],
        compiler_params=_cparams("parallel", "arbitrary"),
    )(xs, bm, cm, dtr, par, hs, dy, *extra)


ADAM_ROWS = 256


def _adamw(lands, w, m, v, name):
    na = len(lands)
    n_slots, r, wd = lands[0].shape
    tr = r if r <= 2 * ADAM_ROWS else ADAM_ROWS
    nj = r // tr
    bc1 = 1.0 - ADAM_B1 ** ADAM_STEP
    bc2 = 1.0 - ADAM_B2 ** ADAM_STEP

    def body(*refs):
        l_refs = refs[:na]
        w_ref, m_ref, v_ref, g_ref, d_ref, nm_ref, nv_ref = refs[na:]
        for a in range(na):
            @pl.when(pl.program_id(0) == a)
            def _(l_ref=l_refs[a]):
                g = l_ref[0].astype(F32)
                for s in range(1, n_slots):
                    g = g + l_ref[s].astype(F32)
                mn = ADAM_B1 * m_ref[0] + (1.0 - ADAM_B1) * g
                vn = ADAM_B2 * v_ref[0] + (1.0 - ADAM_B2) * (g * g)
                mh = mn / bc1
                vh = vn / bc2
                g_ref[0] = g
                nm_ref[0] = mn
                nv_ref[0] = vn
                d_ref[0] = -ADAM_LR * (mh / (jnp.sqrt(vh) + ADAM_EPS) + ADAM_WD * w_ref[0])

    def land_spec(a):
        return pl.BlockSpec((n_slots, tr, wd),
                            lambda i, j: (0, jnp.where(i == a, j, jnp.where(i < a, 0, nj - 1)), 0))

    blk = pl.BlockSpec((1, tr, wd), lambda i, j: (i, j, 0))
    shp = jax.ShapeDtypeStruct((na, r, wd), F32)
    return pl.pallas_call(
        body, name=name, grid=(na, nj), in_specs=[land_spec(a) for a in range(na)] + [blk, blk, blk],
        out_specs=(blk, blk, blk, blk), out_shape=(shp, shp, shp, shp),
        compiler_params=_cparams("arbitrary", "arbitrary"),
    )(*lands, w, m, v)


def _mesh_pos():
    return lax.axis_index("x"), lax.axis_index("y"), lax.axis_index("c")


def _peer(pos, k):
    x, y, c = pos
    px = 1 - x if (k >> 2) & 1 else x
    py = 1 - y if (k >> 1) & 1 else y
    pc = 1 - c if k & 1 else c
    return px, py, pc


def _flat(pos):
    return 4 * pos[0] + 2 * pos[1] + pos[2]


HBM_SPEC = pl.BlockSpec(memory_space=pl.ANY)


ROW_SHARDED = ("w_ssd_out", "w_att_out", "w_mix_out", "w_ffn_down")
COL_SHARDED = ("w_in", "w_ffn_gate", "w_ffn_up")
GATHERED = ROW_SHARDED + COL_SHARDED + ("conv_w",)
BIG = ROW_SHARDED + COL_SHARDED


SEM_SPEC = pl.BlockSpec(memory_space=pltpu.SEMAPHORE)
TOKEN = jax.ShapeDtypeStruct((8, LANE), F32)
SPLIT_EFFECT = pltpu.SideEffectType.DATAFLOW_SIDE_EFFECTING
GATHER_ROWS = "gather_rows"
GATHER_SLOT = "gather_slot"
SCATTER_ROWS = "scatter_rows"
SCATTER_SLOT = "scatter_slot"


def _land_shape(kind, src):
    if kind == GATHER_ROWS:
        return (N_DEV * src.shape[0],) + src.shape[1:]
    if kind == GATHER_SLOT:
        return (N_DEV,) + src.shape
    if kind == SCATTER_ROWS:
        return (N_DEV, src.shape[0] // N_DEV) + src.shape[1:]
    return src.shape


def _views(kind, src_ref, land_ref, pos, k):
    me = _flat(pos)
    if kind == GATHER_ROWS:
        r = src_ref.shape[0]
        return src_ref, land_ref.at[pl.ds(pl.multiple_of(me * r, 16), r), :]
    if kind == GATHER_SLOT:
        return src_ref, land_ref.at[me]
    dev = _flat(_peer(pos, k))
    if kind == SCATTER_ROWS:
        r = land_ref.shape[1]
        return src_ref.at[pl.ds(pl.multiple_of(dev * r, 16), r), :], land_ref.at[k]
    return src_ref.at[dev], land_ref.at[k]


def _hbm(x):
    return pltpu.with_memory_space_constraint(x, pltpu.HBM)


def _exchange_start(items, after, name):
    kinds = [k for k, _ in items]
    srcs = [_hbm(s) for _, s in items]
    lands = [_hbm(lax.empty(_land_shape(k, s), s.dtype)) for k, s in items]
    n = len(items)
    n_copy = n * (N_DEV - 1)

    def body(*refs):
        src_refs, land_refs = refs[:n], refs[n:2 * n]
        send_sems, recv_sems = refs[2 * n + 1], refs[2 * n + 2]
        token_ref = refs[4 * n + 3]
        pos = _mesh_pos()
        for i, kind in enumerate(kinds):
            for k in range(1, N_DEV):
                s, d = _views(kind, src_refs[i], land_refs[i], pos, k)
                j = i * (N_DEV - 1) + k - 1
                pltpu.make_async_remote_copy(src_ref=s, dst_ref=d, send_sem=send_sems.at[j], recv_sem=recv_sems.at[j],
                                             device_id=_peer(pos, k), device_id_type=MESH_ID).start()
        token_ref[...] = jnp.zeros_like(token_ref)

    arrs = srcs + lands
    outs = pl.pallas_call(
        body, name=name,
        in_specs=[HBM_SPEC] * (2 * n + 1),
        out_specs=[SEM_SPEC, SEM_SPEC] + [HBM_SPEC] * (2 * n) + [pl.BlockSpec(memory_space=pltpu.VMEM)],
        out_shape=[pltpu.SemaphoreType.DMA((n_copy,)), pltpu.SemaphoreType.DMA((n_copy,))]
        + [pltpu.HBM(a.shape, a.dtype) for a in arrs] + [TOKEN],
        input_output_aliases={i: 2 + i for i in range(2 * n)},
        compiler_params=pltpu.CompilerParams(has_side_effects=SPLIT_EFFECT),
    )(*arrs, after)
    return {"kinds": kinds, "send": outs[0], "recv": outs[1], "arrs": outs[2:2 + 2 * n], "token": outs[-1]}


def _exchange_wait(ex, after, name):
    kinds = ex["kinds"]
    n = len(kinds)

    def body(*refs):
        src_refs, land_refs = refs[:n], refs[n:2 * n]
        send_sems, recv_sems = refs[2 * n], refs[2 * n + 1]
        token_ref = refs[-1]
        pos = _mesh_pos()
        for i, kind in enumerate(kinds):
            for k in range(1, N_DEV):
                s, d = _views(kind, src_refs[i], land_refs[i], pos, k)
                j = i * (N_DEV - 1) + k - 1
                cp = pltpu.make_async_remote_copy(src_ref=s, dst_ref=d, send_sem=send_sems.at[j],
                                                  recv_sem=recv_sems.at[j], device_id=_peer(pos, k),
                                                  device_id_type=MESH_ID)
                cp.wait_send()
                cp.wait_recv()
        token_ref[...] = jnp.zeros_like(token_ref)

    outs = pl.pallas_call(
        body, name=name,
        in_specs=[HBM_SPEC] * (2 * n) + [SEM_SPEC, SEM_SPEC, HBM_SPEC],
        out_specs=[HBM_SPEC] * (2 * n) + [pl.BlockSpec(memory_space=pltpu.VMEM)],
        out_shape=[pltpu.HBM(a.shape, a.dtype) for a in ex["arrs"]] + [TOKEN],
        input_output_aliases={i: i for i in range(2 * n)},
        compiler_params=pltpu.CompilerParams(has_side_effects=SPLIT_EFFECT),
    )(*ex["arrs"], ex["send"], ex["recv"], after)
    lands = [_place_own(k, s, d) for k, s, d in zip(kinds, outs[:n], outs[n:2 * n])]
    return lands, outs[-1]


def _place_own(kind, src, land):
    me = _flat(_mesh_pos())
    zeros = (0,) * (src.ndim - 1)
    if kind == GATHER_ROWS:
        return lax.dynamic_update_slice(land, src, (me * src.shape[0],) + zeros)
    if kind == GATHER_SLOT:
        return lax.dynamic_update_slice(land, src[None], (me,) + (0,) * src.ndim)
    if kind == SCATTER_ROWS:
        r = land.shape[1]
        own = lax.dynamic_slice(src, (me * r,) + zeros, (r,) + src.shape[1:])
    else:
        own = lax.dynamic_index_in_dim(src, me, 0, keepdims=False)
    return lax.dynamic_update_slice(land, own[None], (0,) * land.ndim)


def _all_gather_small(x, name):
    r, w = x.shape

    def body(x_ref, out_ref, send_sems, recv_sems):
        pos = _mesh_pos()
        me = _flat(pos)
        copies = []
        for k in range(1, N_DEV):
            cp = pltpu.make_async_remote_copy(
                src_ref=x_ref, dst_ref=out_ref.at[me], send_sem=send_sems.at[k - 1], recv_sem=recv_sems.at[k - 1],
                device_id=_peer(pos, k), device_id_type=MESH_ID)
            cp.start()
            copies.append(cp)
        out_ref[me] = x_ref[...]
        for cp in copies:
            cp.wait()

    vmem = pl.BlockSpec(memory_space=pltpu.VMEM)
    return pl.pallas_call(
        body, name=name, in_specs=[vmem], out_specs=vmem,
        out_shape=jax.ShapeDtypeStruct((N_DEV, r, w), x.dtype),
        scratch_shapes=[pltpu.SemaphoreType.DMA((N_DEV - 1,)), pltpu.SemaphoreType.DMA((N_DEV - 1,))],
        compiler_params=pltpu.CompilerParams(has_side_effects=True),
    )(x)


def _cols(g, lo, hi):
    c = g.shape[-1]
    parts = []
    for d in range(N_DEV):
        a, b = max(lo, d * c), min(hi, (d + 1) * c)
        if a < b:
            parts.append(g[d, :, a - d * c:b - d * c])
    return parts[0] if len(parts) == 1 else jnp.concatenate(parts, axis=1)


def _col_chunks(g):
    c = g.shape[-1] // N_DEV
    return jnp.stack([g[:, d * c:(d + 1) * c] for d in range(N_DEV)])


IN_PART = ("w_in", "conv_w")
OUT_PART = ROW_SHARDED + ("w_ffn_gate", "w_ffn_up")


def _gather_items(w, names, l):
    items = []
    for n in names:
        blk = w[n][l] if n == "conv_w" else w[n][l].astype(BF16)
        items.append((GATHER_ROWS if n in ROW_SHARDED else GATHER_SLOT, blk))
    return items


def _scatter_items(grads, names):
    return [(SCATTER_ROWS, grads[n]) if n in ROW_SHARDED else (SCATTER_SLOT, _col_chunks(grads[n]))
            for n in names]


SMALL = ("ln_in_g", "ln_in_b", "conv_b", "dt_bias", "a_log", "d_skip", "ssd_norm_w", "att_sinks",
         "ln_mix_g", "ln_mix_b", "ln_ffn_g", "ln_ffn_b")


def _pack_small(vals):
    flat = jnp.concatenate([vals[n].reshape(-1) for n in SMALL])
    n = flat.shape[0]
    rows = -(-n // LANE)
    rows = -(-rows // 8) * 8
    return jnp.pad(flat, (0, rows * LANE - n)).reshape(rows, LANE)


def _unpack_small(buf, shapes):
    flat = buf.reshape(-1)
    off = 0
    out = {}
    for n in SMALL:
        cnt = math.prod(shapes[n])
        out[n] = flat[off:off + cnt].reshape(shapes[n])
        off += cnt
    return out


def _to_group_major(v):
    lead = v.shape[:-1]
    t = v.reshape(lead + (SSD_GROUPS, HEADS_PER_GROUP))
    t = jnp.pad(t, [(0, 0)] * len(lead) + [(0, 0), (0, LANE - HEADS_PER_GROUP)])
    return t.reshape(lead + (DT_PAD,))


def _from_group_major(v):
    lead = v.shape[:-1]
    return v.reshape(lead + (SSD_GROUPS, LANE))[..., :HEADS_PER_GROUP].reshape(lead + (SSD_HEADS,))


def _rows8(v):
    return jnp.pad(v, ((0, 8 - v.shape[0]), (0, 0)))


IN_OFFS = {"q": (0, 1024), "kv": (1024, 1280), "z": (1280, 3328), "xs": (3328, 5376), "b": (5376, 5888),
           "c": (5888, 6400), "dt": (6400, 6432), "gl": (6432, 8480)}
PIECES = ("q", "kv", "z", "xs", "b", "c", "dt", "gl")


CAT = ("z", "xs", "gl", "q", "b", "c", "dt", "kv")
CAT_WIDTH = {"q": 1024, "z": 2048, "xs": 2048, "gl": 2048, "b": 512, "c": 512, "kv": 256, "dt": DT_PAD}
CAT_OFF = {p: sum(CAT_WIDTH[q] for q in CAT[:i]) for i, p in enumerate(CAT)}
CAT_DIM = sum(CAT_WIDTH.values())
MAIN_DIM = CAT_OFF["kv"]


def _cat_w_in(g):
    pieces = {p: _cols(g, lo, hi) for p, (lo, hi) in IN_OFFS.items()}
    pieces["dt"] = _to_group_major(pieces["dt"])
    return jnp.concatenate([pieces[p] for p in CAT], axis=1)


def _uncat_dw_in(dw):
    pieces = {p: dw[:, CAT_OFF[p]:CAT_OFF[p] + CAT_WIDTH[p]] for p in CAT}
    pieces["dt"] = _from_group_major(pieces["dt"])
    return jnp.concatenate([pieces[p] for p in PIECES], axis=1)


def _params_out(W):
    p = {n: W[n] for n in ROW_SHARDED}
    for n in ("w_ffn_gate", "w_ffn_up"):
        p[n] = _cols(W[n], 0, FFN_HIDDEN)
    return p


def _params_in(l, W, sm):
    p = {"w_cat": _cat_w_in(W["w_in"])}
    cw = _cols(W["conv_w"], 0, SSD_D_INNER + 2 * BC_DIM)
    cb = sm["conv_b"][l]
    segs = {"xs": (0, 2048), "b": (2048, 2560), "c": (2560, 3072)}
    p["conv_w8"] = {s: _rows8(cw[:, lo:hi]) for s, (lo, hi) in segs.items()}
    p["conv_b8"] = {s: _rows8(cb[None, lo:hi]) for s, (lo, hi) in segs.items()}
    p["ssd_par"] = _rows8(jnp.stack([_to_group_major(sm["dt_bias"][l]), _to_group_major(sm["a_log"][l]),
                                     _to_group_major(sm["d_skip"][l])]))
    p["norm_w"] = sm["ssd_norm_w"][l]
    p["sinks8"] = _rows8(jnp.pad(sm["att_sinks"][l], (0, LANE - ATT_HEADS))[None])
    for n in ("ln_mix_g", "ln_mix_b", "ln_ffn_g", "ln_ffn_b"):
        p[n] = sm[n][l]
    return p


def _fwd_mixers(h0, p, l, dep=None):
    tag = f"l{l}_"
    a = {"h0": h0}
    proj = _mm(h0, p["w_cat"], "nn", tag + "proj", dep=dep)
    for pc in CAT:
        a[pc] = (proj, CAT_OFF[pc], CAT_WIDTH[pc])
    for s in ("xs", "b", "c"):
        a[s + "c"] = _conv_fwd(a[s], p["conv_w8"][s], p["conv_b8"][s], tag + "conv_" + s)
    a["y"], a["hs"] = _ssd_fwd(a["xsc"], a["bc"], a["cc"], a["dt"], p["ssd_par"], tag + "ssd_fwd")
    a["yn"] = _gnorm_fwd(a["y"], a["z"], p["norm_w"], tag + "gnorm")
    a["att"] = _att_fwd(a["q"], a["kv"], p["sinks8"], tag + "att_fwd")
    return a


def _fwd_out(a, p, l, dep=None):
    tag = f"l{l}_"
    h0 = a["h0"]
    a["ya"] = _mm(a["yn"], p["w_ssd_out"], "nn", tag + "ssd_out", dep=dep)
    a["yb"] = _mm(a["att"], p["w_att_out"], "nn", tag + "att_out", dep=dep)
    a["merged"] = _merge_fwd(a["gl"], a["ya"], a["yb"], tag + "merge")
    a["mix"] = _mm(a["merged"], p["w_mix_out"], "nn", tag + "mix_out")
    a["h1"] = _ln_fwd(h0, a["mix"], p["ln_mix_g"], p["ln_mix_b"], ALPHA, tag + "ln_mix")
    a["fg"] = _mm(a["h1"], p["w_ffn_gate"], "nn", tag + "ffn_gate")
    a["fu"] = _mm(a["h1"], p["w_ffn_up"], "nn", tag + "ffn_up")
    a["act"] = _swiglu_fwd(a["fg"], a["fu"], tag + "swiglu")
    a["ffn"] = _mm(a["act"], p["w_ffn_down"], "nn", tag + "ffn_down")
    a["h2"] = _ln_fwd(a["h1"], a["ffn"], p["ln_ffn_g"], p["ln_ffn_b"], ALPHA, tag + "ln_ffn")
    return a


def _dw(x, dy, name, dep=None):
    return _mm(x, dy, "tn", name, out_dtype=BF16, dep=dep)


def _bwd_out(a, p, dh2, l, dep=None):
    tag = f"l{l}_b_"
    gw, gs = {}, {}
    du2, acc = _ln_bwd(a["h1"], a["ffn"], p["ln_ffn_g"], dh2, ALPHA, tag + "ln_ffn")
    gs["ln_ffn_g"], gs["ln_ffn_b"] = acc[0], acc[1]
    gw["w_ffn_down"] = _dw(a["act"], du2, tag + "dw_down", dep=dep)
    dact = _mm(du2, p["w_ffn_down"], "nt", tag + "dact", dep=dep)
    dfg, dfu = _swiglu_bwd(a["fg"], a["fu"], dact, tag + "swiglu")
    gw["w_ffn_gate"] = _dw(a["h1"], dfg, tag + "dw_gate")
    gw["w_ffn_up"] = _dw(a["h1"], dfu, tag + "dw_up")
    dh1 = _mm(dfg, p["w_ffn_gate"], "nt", tag + "dh1_gate", add=du2, add_scale=ALPHA)
    dh1 = _mm(dfu, p["w_ffn_up"], "nt", tag + "dh1_up", add=dh1)
    du1, acc = _ln_bwd(a["h0"], a["mix"], p["ln_mix_g"], dh1, ALPHA, tag + "ln_mix")
    gs["ln_mix_g"], gs["ln_mix_b"] = acc[0], acc[1]
    gw["w_mix_out"] = _dw(a["merged"], du1, tag + "dw_mix")
    dmerged = _mm(du1, p["w_mix_out"], "nt", tag + "dmerged")
    dya, dyb, dproj = _merge_bwd(a["gl"], a["ya"], a["yb"], dmerged, tag + "merge",
                                 (None, CAT_OFF["gl"], MAIN_DIM))
    gw["w_ssd_out"] = _dw(a["yn"], dya, tag + "dw_ssd")
    gw["w_att_out"] = _dw(a["att"], dyb, tag + "dw_att")
    return {"du1": du1, "dya": dya, "dyb": dyb, "dproj": dproj}, gw, gs


def _bwd_mixers(a, p, carry, l, dep=None):
    tag = f"l{l}_b_"
    gs = {}
    du1, dproj = carry["du1"], carry["dproj"]

    def win(pc):
        return (dproj, CAT_OFF[pc], MAIN_DIM)

    dyn = _mm(carry["dya"], p["w_ssd_out"], "nt", tag + "dyn", dep=dep)
    datt = _mm(carry["dyb"], p["w_att_out"], "nt", tag + "datt", out_dtype=BF16, dep=dep)
    dproj, dkv, acc = _att_bwd(a["q"], a["kv"], p["sinks8"], a["att"], datt, tag + "att", win("q"))
    gs["att_sinks"] = acc[0, :ATT_HEADS]
    dy, dproj, acc = _gnorm_bwd(a["y"], a["z"], p["norm_w"], dyn, tag + "gnorm", win("z"))
    gs["ssd_norm_w"] = acc[0]
    dxs, dbm, dcm, dproj, acc = _ssd_bwd(a["xsc"], a["bc"], a["cc"], a["dt"], p["ssd_par"], a["hs"], dy,
                                         tag + "ssd", win("dt"))
    gs["dt_bias"], gs["a_log"], gs["d_skip"] = (_from_group_major(acc[i]) for i in range(3))
    dconv_w, dconv_b = [], []
    for s, dout in (("xs", dxs), ("b", dbm), ("c", dcm)):
        dc, acc = _conv_bwd_pre(a[s], p["conv_w8"][s], p["conv_b8"][s], dout, tag + "conv_pre_" + s)
        dconv_w.append(acc[:CONV_TAPS])
        dconv_b.append(acc[CONV_TAPS])
        dproj = _conv_bwd_in(dc, p["conv_w8"][s], tag + "conv_in_" + s, win(s))
    gconv = jnp.concatenate(dconv_w, axis=1)
    gs["conv_b"] = jnp.concatenate(dconv_b)
    w_main, w_kv = p["w_cat"][:, :MAIN_DIM], p["w_cat"][:, MAIN_DIM:]
    dw = jnp.concatenate([_dw(a["h0"], dproj, tag + "dw_in"), _dw(a["h0"], dkv, tag + "dw_in_kv")], axis=1)
    dh0 = _mm(dproj, w_main, "nt", tag + "dh0", add=du1, add_scale=ALPHA)
    dh0 = _mm(dkv, w_kv, "nt", tag + "dh0_kv", add=dh0)
    return dh0, _uncat_dw_in(dw), gconv, gs


def _step(x, target, w, m, v):
    x2 = x[0]
    t2 = target[0]
    tok = jnp.zeros(TOKEN.shape, TOKEN.dtype)

    ex = _exchange_start(_gather_items(w, IN_PART, 0), tok, "gather_l0_in_start")
    lands, tok = _exchange_wait(ex, ex["token"], "gather_l0_in_wait")
    p0 = _params_in(0, dict(zip(IN_PART, lands)), w)
    ex = _exchange_start(_gather_items(w, OUT_PART, 0) + _gather_items(w, IN_PART, 1), tok,
                         "gather_l0_out_l1_in_start")
    h = _ln_fwd(x2, None, w["ln_in_g"], w["ln_in_b"], 1.0, "ln_in")
    a0 = _fwd_mixers(h, p0, 0, dep=ex["token"])
    lands, tok = _exchange_wait(ex, a0["att"], "gather_l0_out_l1_in_wait")
    p0.update(_params_out(dict(zip(OUT_PART, lands))))
    p1 = _params_in(1, dict(zip(IN_PART, lands[len(OUT_PART):])), w)
    ex = _exchange_start(_gather_items(w, OUT_PART, 1), tok, "gather_l1_out_start")
    a0 = _fwd_out(a0, p0, 0, dep=ex["token"])
    lands, tok = _exchange_wait(ex, a0["h2"], "gather_l1_out_wait")
    p1.update(_params_out(dict(zip(OUT_PART, lands))))
    a1 = _fwd_out(_fwd_mixers(a0["h2"], p1, 1), p1, 1)

    sse, dh = _loss_fwd_bwd(a1["h2"], t2, "loss")
    loss = lax.psum(0.5 / D_MODEL * sse[0, 0], ("x", "y", "c"))

    carry, gw1, gs1 = _bwd_out(a1, p1, dh, 1)
    dh, gw1["w_in"], gw1["conv_w"], gs = _bwd_mixers(a1, p1, carry, 1)
    gs1.update(gs)
    ex1 = _exchange_start(_scatter_items(gw1, GATHERED), tok, "scatter_l1_start")
    carry, gw0, gs0 = _bwd_out(a0, p0, dh, 0, dep=ex1["token"])
    lands, tok = _exchange_wait(ex1, carry["dyb"], "scatter_l1_wait")
    land1 = dict(zip(GATHERED, lands))
    ex0 = _exchange_start(_scatter_items(gw0, OUT_PART), tok, "scatter_l0_out_start")
    dh, gw0["w_in"], gw0["conv_w"], gs = _bwd_mixers(a0, p0, carry, 0, dep=ex0["token"])
    gs0.update(gs)
    lands, tok = _exchange_wait(ex0, dh, "scatter_l0_out_wait")
    land0 = dict(zip(OUT_PART, lands))
    ex0 = _exchange_start(_scatter_items(gw0, IN_PART), tok, "scatter_l0_in_start")
    grad_x2, acc = _ln_bwd(x2, None, w["ln_in_g"], dh, 1.0, "ln_in_b")

    outs = [{} for _ in range(4)]

    def update(names):
        res = None
        for n in names:
            res = _adamw([land0[n], land1[n]], w[n], m[n], v[n], "adamw_" + n)
            for o, t in zip(outs, res):
                o[n] = t
        return res[1]

    update(OUT_PART)
    gsm = {"ln_in_g": acc[0], "ln_in_b": acc[1]}
    for n in SMALL[2:]:
        gsm[n] = jnp.stack([gs0[n], gs1[n]])
    small_shapes = {n: w[n].shape for n in SMALL}
    land_s = _all_gather_small(_pack_small(gsm), "small_grads_all_gather")
    res = _adamw([land_s], _pack_small(w)[None], _pack_small(m)[None], _pack_small(v)[None], "adamw_small")
    for o, t in zip(outs, res):
        o.update(_unpack_small(t[0], small_shapes))
    lands, _ = _exchange_wait(ex0, res[1], "scatter_l0_in_wait")
    land0.update(zip(IN_PART, lands))
    update(IN_PART)
    return loss, grad_x2[None], outs


WEIGHT_NAMES = ("ln_in_g", "ln_in_b", "w_in", "conv_w", "conv_b", "dt_bias", "a_log", "d_skip", "ssd_norm_w",
                "att_sinks", "w_ssd_out", "w_att_out", "w_mix_out", "ln_mix_g", "ln_mix_b", "w_ffn_gate",
                "w_ffn_up", "w_ffn_down", "ln_ffn_g", "ln_ffn_b")


def kernel(x, ln_in_g, ln_in_b, w_in, conv_w, conv_b, dt_bias, a_log, d_skip, ssd_norm_w, att_sinks, w_ssd_out, w_att_out, w_mix_out, ln_mix_g, ln_mix_b, w_ffn_gate, w_ffn_up, w_ffn_down, ln_ffn_g, ln_ffn_b, loss_target, m_ln_in_g, m_ln_in_b, m_w_in, m_conv_w, m_conv_b, m_dt_bias, m_a_log, m_d_skip, m_ssd_norm_w, m_att_sinks, m_w_ssd_out, m_w_att_out, m_w_mix_out, m_ln_mix_g, m_ln_mix_b, m_w_ffn_gate, m_w_ffn_up, m_w_ffn_down, m_ln_ffn_g, m_ln_ffn_b, v_ln_in_g, v_ln_in_b, v_w_in, v_conv_w, v_conv_b, v_dt_bias, v_a_log, v_d_skip, v_ssd_norm_w, v_att_sinks, v_w_ssd_out, v_w_att_out, v_w_mix_out, v_ln_mix_g, v_ln_mix_b, v_w_ffn_gate, v_w_ffn_up, v_w_ffn_down, v_ln_ffn_g, v_ln_ffn_b):
    w = dict(zip(WEIGHT_NAMES, (ln_in_g, ln_in_b, w_in, conv_w, conv_b, dt_bias, a_log, d_skip, ssd_norm_w,
                                att_sinks, w_ssd_out, w_att_out, w_mix_out, ln_mix_g, ln_mix_b, w_ffn_gate,
                                w_ffn_up, w_ffn_down, ln_ffn_g, ln_ffn_b)))
    m = dict(zip(WEIGHT_NAMES, (m_ln_in_g, m_ln_in_b, m_w_in, m_conv_w, m_conv_b, m_dt_bias, m_a_log, m_d_skip,
                                m_ssd_norm_w, m_att_sinks, m_w_ssd_out, m_w_att_out, m_w_mix_out, m_ln_mix_g,
                                m_ln_mix_b, m_w_ffn_gate, m_w_ffn_up, m_w_ffn_down, m_ln_ffn_g, m_ln_ffn_b)))
    v = dict(zip(WEIGHT_NAMES, (v_ln_in_g, v_ln_in_b, v_w_in, v_conv_w, v_conv_b, v_dt_bias, v_a_log, v_d_skip,
                                v_ssd_norm_w, v_att_sinks, v_w_ssd_out, v_w_att_out, v_w_mix_out, v_ln_mix_g,
                                v_ln_mix_b, v_w_ffn_gate, v_w_ffn_up, v_w_ffn_down, v_ln_ffn_g, v_ln_ffn_b)))
    loss, grad_x, outs = _step(x, loss_target, w, m, v)
    result = [loss, grad_x]
    for o in outs:
        result.extend(o[n] for n in WEIGHT_NAMES)
    return tuple(result)
```

```python
import functools
import math

import jax
import jax.numpy as jnp
from jax import lax
from jax.experimental import pallas as pl
from jax.experimental.pallas import tpu as pltpu

F32 = jnp.float32
BF16 = jnp.bfloat16

D_MODEL = 1024
DEPTH = 2
N_DEV = 8
ATT_HEADS = 16
ATT_KV_HEADS = 2
ATT_HEAD_DIM = 64
ATT_BLOCK = 128
SSD_D_INNER = 2048
SSD_HEADS = 32
SSD_GROUPS = 4
SSD_STATE = 128
SSD_CHUNK = 128
FFN_HIDDEN = 2816
LN_EPS = 1e-5
RMS_EPS = 1e-5
ALPHA = (2 * DEPTH) ** 0.25
Q_DIM = 1024
KV_DIM = 128
BC_DIM = 512
IN_DIM = 8480
IN_SHARD = IN_DIM // N_DEV
DT_PAD = 512

ADAM_LR = 0.001
ADAM_B1 = 0.9
ADAM_B2 = 0.999
ADAM_EPS = 1e-08
ADAM_WD = 0.01
ADAM_STEP = 10

LANE = 128
VMEM_LIMIT = 48 * 1024 * 1024
PACK_W = 1024
NEG = -1e30

_NN = (((1,), (0,)), ((), ()))
_NT = (((1,), (1,)), ((), ()))
_TN = (((0,), (0,)), ((), ()))
MESH_ID = pl.DeviceIdType.MESH


def _dot(a, b, dims=_NN):
    return lax.dot_general(a, b, dims, preferred_element_type=F32)


def _dot_hi(a, b):
    return lax.dot_general(a, b, _NN, preferred_element_type=F32, precision=lax.Precision.HIGHEST)


def _sig(x):
    return 1.0 / (1.0 + jnp.exp(-x))


def _softplus(x):
    return jnp.maximum(x, 0.0) + jnp.log(1.0 + jnp.exp(-jnp.abs(x)))


def _cparams(*sem):
    return pltpu.CompilerParams(dimension_semantics=sem, vmem_limit_bytes=VMEM_LIMIT)


def _pick(n, cap):
    if n <= cap:
        return n
    best = None
    for t in range(LANE, cap + 1, LANE):
        if n % t == 0:
            best = t
    assert best is not None, (n, cap)
    return best


def _tile(n):
    if n <= 1024 or n % 1024 == 0:
        return min(n, 1024)
    return _pick(n, 1408)


def _rows(n):
    return min(512, n)


def _window(x):
    return x if isinstance(x, tuple) else (x, 0, x.shape[1])


def _into(into, n_in, out_idx):
    buf, col0, width = into
    if buf is None:
        return [], [], {}, col0, width
    return [buf], [pl.BlockSpec(memory_space=pl.ANY)], {n_in: out_idx}, col0, width


def _mm(a, b, mode, name, add=None, add_scale=1.0, out_dtype=F32, dep=None):
    if mode == "nn":
        m, k = a.shape
        n = b.shape[1]
    elif mode == "nt":
        m, k = a.shape
        n = b.shape[0]
    else:
        k, m = a.shape
        n = b.shape[1]
    tm = _tile(m)
    tn = _pick(n, 2176) if mode == "tn" and n > 1024 else _tile(n)
    tk = _pick(k, 2176) if mode == "nt" and a.dtype == BF16 and k > 2816 else _tile(k)
    nk = k // tk
    has_add = add is not None
    dims = {"nn": _NN, "nt": _NT, "tn": _TN}[mode]

    def body(*refs):
        if dep is not None:
            refs = refs[:-3] + refs[-2:]
        if has_add:
            a_ref, b_ref, add_ref, o_ref, acc_ref = refs
        else:
            a_ref, b_ref, o_ref, acc_ref = refs
        kk = pl.program_id(2)

        @pl.when(kk == 0)
        def _():
            if has_add:
                acc_ref[...] = add_scale * add_ref[...].astype(F32)
            else:
                acc_ref[...] = jnp.zeros_like(acc_ref)

        acc_ref[...] += _dot(a_ref[...].astype(BF16), b_ref[...].astype(BF16), dims)

        @pl.when(kk == nk - 1)
        def _():
            o_ref[...] = acc_ref[...].astype(o_ref.dtype)

    if mode == "nn":
        a_spec = pl.BlockSpec((tm, tk), lambda i, j, kk: (i, kk))
        b_spec = pl.BlockSpec((tk, tn), lambda i, j, kk: (kk, j))
    elif mode == "nt":
        a_spec = pl.BlockSpec((tm, tk), lambda i, j, kk: (i, kk))
        b_spec = pl.BlockSpec((tn, tk), lambda i, j, kk: (j, kk))
    else:
        a_spec = pl.BlockSpec((tk, tm), lambda i, j, kk: (kk, i))
        b_spec = pl.BlockSpec((tk, tn), lambda i, j, kk: (kk, j))
    o_spec = pl.BlockSpec((tm, tn), lambda i, j, kk: (i, j))
    in_specs = [a_spec, b_spec] + ([o_spec] if has_add else [])
    args = (a, b) + ((add,) if has_add else ())
    if dep is not None:
        in_specs.append(pl.BlockSpec((8, LANE), lambda i, j, kk: (0, 0)))
        args += (dep,)
    return pl.pallas_call(
        body, name=name, grid=(m // tm, n // tn, nk),
        in_specs=in_specs, out_specs=o_spec,
        out_shape=jax.ShapeDtypeStruct((m, n), out_dtype),
        scratch_shapes=[pltpu.VMEM((tm, tn), F32)],
        compiler_params=_cparams("parallel", "parallel", "arbitrary"),
    )(*args)


def _vec_spec(width):
    return pl.BlockSpec((1, width), lambda i: (0, 0))


def _ln_fwd(a, b, gamma, beta, alpha, name):
    n_rows, dm = a.shape
    has_b = b is not None

    def body(*refs):
        if has_b:
            a_ref, b_ref, g_ref, be_ref, o_ref = refs
            u = alpha * a_ref[...] + b_ref[...]
        else:
            a_ref, g_ref, be_ref, o_ref = refs
            u = a_ref[...]
        mu = jnp.mean(u, axis=-1, keepdims=True)
        d = u - mu
        var = jnp.mean(d * d, axis=-1, keepdims=True)
        o_ref[...] = d * lax.rsqrt(var + LN_EPS) * g_ref[...] + be_ref[...]

    row = pl.BlockSpec((_rows(n_rows),dm), lambda i: (i, 0))
    in_specs = [row] + ([row] if has_b else []) + [_vec_spec(dm), _vec_spec(dm)]
    args = (a,) + ((b,) if has_b else ()) + (gamma.reshape(1, dm), beta.reshape(1, dm))
    return pl.pallas_call(
        body, name=name, grid=(n_rows // _rows(n_rows),), in_specs=in_specs, out_specs=row,
        out_shape=jax.ShapeDtypeStruct((n_rows, dm), F32),
        compiler_params=_cparams("parallel"),
    )(*args)


def _ln_bwd(a, b, gamma, dy, alpha, name):
    n_rows, dm = a.shape
    has_b = b is not None

    def body(*refs):
        if has_b:
            a_ref, b_ref, g_ref, dy_ref, du_ref, acc_ref = refs
            u = alpha * a_ref[...] + b_ref[...]
        else:
            a_ref, g_ref, dy_ref, du_ref, acc_ref = refs
            u = a_ref[...]

        @pl.when(pl.program_id(0) == 0)
        def _():
            acc_ref[...] = jnp.zeros_like(acc_ref)

        mu = jnp.mean(u, axis=-1, keepdims=True)
        d = u - mu
        var = jnp.mean(d * d, axis=-1, keepdims=True)
        rstd = lax.rsqrt(var + LN_EPS)
        xhat = d * rstd
        dyv = dy_ref[...]
        acc_ref[0:1, :] += jnp.sum(dyv * xhat, axis=0, keepdims=True)
        acc_ref[1:2, :] += jnp.sum(dyv, axis=0, keepdims=True)
        dxh = dyv * g_ref[...]
        m1 = jnp.mean(dxh, axis=-1, keepdims=True)
        m2 = jnp.mean(dxh * xhat, axis=-1, keepdims=True)
        du_ref[...] = rstd * (dxh - m1 - xhat * m2)

    row = pl.BlockSpec((_rows(n_rows),dm), lambda i: (i, 0))
    in_specs = [row] + ([row] if has_b else []) + [_vec_spec(dm), row]
    args = (a,) + ((b,) if has_b else ()) + (gamma.reshape(1, dm), dy)
    return pl.pallas_call(
        body, name=name, grid=(n_rows // _rows(n_rows),), in_specs=in_specs,
        out_specs=(row, pl.BlockSpec((8, dm), lambda i: (0, 0))),
        out_shape=(jax.ShapeDtypeStruct((n_rows, dm), F32), jax.ShapeDtypeStruct((8, dm), F32)),
        compiler_params=_cparams("arbitrary"),
    )(*args)


def _loss_fwd_bwd(y, target, name):
    n_rows, dm = y.shape

    def body(y_ref, t_ref, acc_ref, dy_ref):
        @pl.when(pl.program_id(0) == 0)
        def _():
            acc_ref[...] = jnp.zeros_like(acc_ref)

        d = y_ref[...] - t_ref[...]
        acc_ref[...] += jnp.sum(d * d)
        dy_ref[...] = d * (1.0 / dm)

    row = pl.BlockSpec((_rows(n_rows),dm), lambda i: (i, 0))
    return pl.pallas_call(
        body, name=name, grid=(n_rows // _rows(n_rows),), in_specs=[row, row],
        out_specs=(pl.BlockSpec((8, LANE), lambda i: (0, 0)), row),
        out_shape=(jax.ShapeDtypeStruct((8, LANE), F32), jax.ShapeDtypeStruct((n_rows, dm), F32)),
        compiler_params=_cparams("arbitrary"),
    )(y, target)


def _swiglu_fwd(g, u, name):
    n_rows, w = g.shape
    tw = _pick(w, 1408)

    def body(g_ref, u_ref, o_ref):
        gv = g_ref[...]
        o_ref[...] = (gv * _sig(gv) * u_ref[...]).astype(BF16)

    blk = pl.BlockSpec((_rows(n_rows),tw), lambda i, j: (i, j))
    return pl.pallas_call(
        body, name=name, grid=(n_rows // _rows(n_rows), w // tw), in_specs=[blk, blk], out_specs=blk,
        out_shape=jax.ShapeDtypeStruct((n_rows, w), BF16),
        compiler_params=_cparams("parallel", "parallel"),
    )(g, u)


def _swiglu_bwd(g, u, dact, name):
    n_rows, w = g.shape
    tw = _pick(w, 1408)

    def body(g_ref, u_ref, da_ref, dg_ref, du_ref):
        gv = g_ref[...]
        s = _sig(gv)
        da = da_ref[...]
        dg_ref[...] = (da * u_ref[...] * (s * (1.0 + gv * (1.0 - s)))).astype(BF16)
        du_ref[...] = (da * gv * s).astype(BF16)

    blk = pl.BlockSpec((_rows(n_rows),tw), lambda i, j: (i, j))
    return pl.pallas_call(
        body, name=name, grid=(n_rows // _rows(n_rows), w // tw), in_specs=[blk, blk, blk], out_specs=(blk, blk),
        out_shape=(jax.ShapeDtypeStruct((n_rows, w), BF16), jax.ShapeDtypeStruct((n_rows, w), BF16)),
        compiler_params=_cparams("parallel", "parallel"),
    )(g, u, dact)


def _gate_specs(gl, n_rows, dm):
    arr, g0, _ = _window(gl)
    return arr, [pl.BlockSpec((_rows(n_rows), dm), lambda i, k=k: (i, g0 // dm + k)) for k in range(2)]


def _merge_fwd(gl, ya, yb, name):
    n_rows, dm = ya.shape
    gl_arr, gspecs = _gate_specs(gl, n_rows, dm)

    def body(ga_ref, gb_ref, ya_ref, yb_ref, o_ref):
        o_ref[...] = (_sig(ga_ref[...]) * ya_ref[...] + _sig(gb_ref[...]) * yb_ref[...]).astype(BF16)

    row = pl.BlockSpec((_rows(n_rows),dm), lambda i: (i, 0))
    return pl.pallas_call(
        body, name=name, grid=(n_rows // _rows(n_rows),), in_specs=gspecs + [row, row], out_specs=row,
        out_shape=jax.ShapeDtypeStruct((n_rows, dm), BF16),
        compiler_params=_cparams("parallel"),
    )(gl_arr, gl_arr, ya, yb)


def _merge_bwd(gl, ya, yb, dmerged, name, into):
    n_rows, dm = ya.shape
    gl_arr, gspecs = _gate_specs(gl, n_rows, dm)
    extra, extra_specs, aliases, col0, width = _into(into, 5, 2)

    def body(*refs):
        ga_ref, gb_ref, ya_ref, yb_ref, dm_ref = refs[:5]
        dya_ref, dyb_ref, dgl_ref = refs[-3:]
        ga = _sig(ga_ref[...])
        gb = _sig(gb_ref[...])
        dmv = dm_ref[...]
        dya_ref[...] = (dmv * ga).astype(BF16)
        dyb_ref[...] = (dmv * gb).astype(BF16)
        dgl_ref[:, :dm] = (dmv * ya_ref[...] * ga * (1.0 - ga)).astype(BF16)
        dgl_ref[:, dm:] = (dmv * yb_ref[...] * gb * (1.0 - gb)).astype(BF16)

    row = pl.BlockSpec((_rows(n_rows),dm), lambda i: (i, 0))
    row2 = pl.BlockSpec((_rows(n_rows),2 * dm), lambda i: (i, col0 // (2 * dm)))
    return pl.pallas_call(
        body, name=name, grid=(n_rows // _rows(n_rows),), in_specs=gspecs + [row, row, row] + extra_specs,
        out_specs=(row, row, row2),
        out_shape=(jax.ShapeDtypeStruct((n_rows, dm), BF16), jax.ShapeDtypeStruct((n_rows, dm), BF16),
                   jax.ShapeDtypeStruct((n_rows, width), BF16)),
        input_output_aliases=aliases,
        compiler_params=_cparams("parallel"),
    )(gl_arr, gl_arr, ya, yb, dmerged, *extra)


CONV_TAPS = 4
CONV_COLS = 512
HALO = 8


def _shift_down(cur, prev8, s, row8):
    r = pltpu.roll(cur, s, axis=0)
    top = jnp.where(row8 < s, pltpu.roll(prev8, s, axis=0), r[0:HALO])
    return jnp.concatenate([top, r[HALO:]], axis=0)


def _shift_up(cur, next8, s, row8):
    n = cur.shape[0]
    r = pltpu.roll(cur, n - s, axis=0)
    bot = jnp.where(row8 >= HALO - s, pltpu.roll(next8, HALO - s, axis=0), r[n - HALO:])
    return jnp.concatenate([r[:n - HALO], bot], axis=0)


def _conv_pre(u_ref, prev_ref, w_ref, b_ref, li):
    cur = u_ref[...]
    prev8 = jnp.where(li == 0, 0.0, prev_ref[...])
    row8 = lax.broadcasted_iota(jnp.int32, prev8.shape, 0)
    shifted = [cur] + [_shift_down(cur, prev8, s, row8) for s in range(1, CONV_TAPS)]
    acc = b_ref[...] + shifted[0] * w_ref[CONV_TAPS - 1:CONV_TAPS, :]
    for s in range(1, CONV_TAPS):
        acc = acc + shifted[s] * w_ref[CONV_TAPS - 1 - s:CONV_TAPS - s, :]
    return acc, shifted


def _conv_specs(n_rows, tl, col0=0):
    off = col0 // CONV_COLS
    cur = pl.BlockSpec((tl, CONV_COLS), lambda cj, li: (li, cj + off))
    prev = pl.BlockSpec((HALO, CONV_COLS), lambda cj, li: (jnp.maximum(li * (tl // HALO) - 1, 0), cj + off))
    nxt = pl.BlockSpec((HALO, CONV_COLS),
                       lambda cj, li: (jnp.minimum((li + 1) * (tl // HALO), n_rows // HALO - 1), cj + off))
    par = pl.BlockSpec((8, CONV_COLS), lambda cj, li: (0, cj + off))
    return cur, prev, nxt, par


def _conv_fwd(u, w8, b8, name):
    u, u0, c = _window(u)
    n_rows = u.shape[0]
    tl = _rows(n_rows)
    cur, _, _, par = _conv_specs(n_rows, tl)
    ucur, prev, _, _ = _conv_specs(n_rows, tl, u0)

    def body(u_ref, prev_ref, w_ref, b_ref, o_ref):
        acc, _ = _conv_pre(u_ref, prev_ref, w_ref, b_ref[0:1, :], pl.program_id(1))
        o_ref[...] = acc * _sig(acc)

    return pl.pallas_call(
        body, name=name, grid=(c // CONV_COLS, n_rows // tl), in_specs=[ucur, prev, par, par], out_specs=cur,
        out_shape=jax.ShapeDtypeStruct((n_rows, c), F32),
        compiler_params=_cparams("parallel", "parallel"),
    )(u, u, w8, b8)


def _conv_bwd_pre(u, w8, b8, dout, name):
    u, u0, c = _window(u)
    n_rows = u.shape[0]
    tl = _rows(n_rows)
    cur, _, _, par = _conv_specs(n_rows, tl)
    ucur, prev, _, _ = _conv_specs(n_rows, tl, u0)

    def body(u_ref, prev_ref, w_ref, b_ref, do_ref, dc_ref, acc_ref):
        @pl.when(pl.program_id(1) == 0)
        def _():
            acc_ref[...] = jnp.zeros_like(acc_ref)

        acc, shifted = _conv_pre(u_ref, prev_ref, w_ref, b_ref[0:1, :], pl.program_id(1))
        sg = _sig(acc)
        dc = do_ref[...] * (sg * (1.0 + acc * (1.0 - sg)))
        dc_ref[...] = dc
        for k in range(CONV_TAPS):
            acc_ref[k:k + 1, :] += jnp.sum(dc * shifted[CONV_TAPS - 1 - k], axis=0, keepdims=True)
        acc_ref[CONV_TAPS:CONV_TAPS + 1, :] += jnp.sum(dc, axis=0, keepdims=True)

    return pl.pallas_call(
        body, name=name, grid=(c // CONV_COLS, n_rows // tl), in_specs=[ucur, prev, par, par, cur],
        out_specs=(cur, par),
        out_shape=(jax.ShapeDtypeStruct((n_rows, c), F32), jax.ShapeDtypeStruct((8, c), F32)),
        compiler_params=_cparams("parallel", "arbitrary"),
    )(u, u, w8, b8, dout)


def _conv_bwd_in(dc, w8, name, into):
    n_rows, c = dc.shape
    tl = _rows(n_rows)
    cur, _, nxt, par = _conv_specs(n_rows, tl)
    n_l = n_rows // tl
    extra, extra_specs, aliases, col0, width = _into(into, 3, 0)
    out_spec = _conv_specs(n_rows, tl, col0)[0]

    def body(*refs):
        dc_ref, next_ref, w_ref = refs[:3]
        o_ref = refs[-1]
        cur_v = dc_ref[...]
        next8 = jnp.where(pl.program_id(1) == n_l - 1, 0.0, next_ref[...])
        row8 = lax.broadcasted_iota(jnp.int32, next8.shape, 0)
        acc = cur_v * w_ref[CONV_TAPS - 1:CONV_TAPS, :]
        for s in range(1, CONV_TAPS):
            acc = acc + _shift_up(cur_v, next8, s, row8) * w_ref[CONV_TAPS - 1 - s:CONV_TAPS - s, :]
        o_ref[...] = acc.astype(BF16)

    return pl.pallas_call(
        body, name=name, grid=(c // CONV_COLS, n_l), in_specs=[cur, nxt, par] + extra_specs, out_specs=out_spec,
        out_shape=jax.ShapeDtypeStruct((n_rows, width), BF16), input_output_aliases=aliases,
        compiler_params=_cparams("parallel", "parallel"),
    )(dc, dc, w8, *extra)


NORM_GROUP = SSD_D_INNER // SSD_GROUPS


def _gnorm_fwd(y, z, w, name):
    n_rows, c = y.shape
    z, z0, _ = _window(z)
    zoff = z0 // NORM_GROUP

    def body(y_ref, z_ref, w_ref, o_ref):
        zv = z_ref[...]
        yg = y_ref[...] * (zv * _sig(zv))
        r = lax.rsqrt(jnp.mean(yg * yg, axis=-1, keepdims=True) + RMS_EPS)
        o_ref[...] = (yg * r * w_ref[...]).astype(BF16)

    blk = pl.BlockSpec((_rows(n_rows),NORM_GROUP), lambda i, j: (i, j))
    zblk = pl.BlockSpec((_rows(n_rows),NORM_GROUP), lambda i, j: (i, j + zoff))
    wspec = pl.BlockSpec((1, NORM_GROUP), lambda i, j: (0, j))
    return pl.pallas_call(
        body, name=name, grid=(n_rows // _rows(n_rows), c // NORM_GROUP), in_specs=[blk, zblk, wspec], out_specs=blk,
        out_shape=jax.ShapeDtypeStruct((n_rows, c), BF16),
        compiler_params=_cparams("parallel", "parallel"),
    )(y, z, w.reshape(1, c))


def _gnorm_bwd(y, z, w, dyn, name, into):
    n_rows, c = y.shape
    z, z0, _ = _window(z)
    zoff = z0 // NORM_GROUP
    extra, extra_specs, aliases, col0, width = _into(into, 4, 1)
    doff = col0 // NORM_GROUP

    def body(*refs):
        y_ref, z_ref, w_ref, dn_ref = refs[:4]
        dy_ref, dz_ref, acc_ref = refs[-3:]
        @pl.when(pl.program_id(1) == 0)
        def _():
            acc_ref[...] = jnp.zeros_like(acc_ref)

        zv = z_ref[...]
        yv = y_ref[...]
        sz = _sig(zv)
        silu = zv * sz
        yg = yv * silu
        r = lax.rsqrt(jnp.mean(yg * yg, axis=-1, keepdims=True) + RMS_EPS)
        nrm = yg * r
        dn = dn_ref[...]
        acc_ref[0:1, :] += jnp.sum(dn * nrm, axis=0, keepdims=True)
        dnw = dn * w_ref[...]
        dyg = r * (dnw - nrm * jnp.mean(dnw * nrm, axis=-1, keepdims=True))
        dy_ref[...] = dyg * silu
        dz_ref[...] = (dyg * yv * (sz * (1.0 + zv * (1.0 - sz)))).astype(BF16)

    blk = pl.BlockSpec((_rows(n_rows),NORM_GROUP), lambda j, i: (i, j))
    zblk = pl.BlockSpec((_rows(n_rows),NORM_GROUP), lambda j, i: (i, j + zoff))
    wspec = pl.BlockSpec((1, NORM_GROUP), lambda j, i: (0, j))
    aspec = pl.BlockSpec((8, NORM_GROUP), lambda j, i: (0, j))
    return pl.pallas_call(
        body, name=name, grid=(c // NORM_GROUP, n_rows // _rows(n_rows)),
        in_specs=[blk, zblk, wspec, blk] + extra_specs,
        out_specs=(blk, pl.BlockSpec((_rows(n_rows), NORM_GROUP), lambda j, i: (i, j + doff)), aspec),
        out_shape=(jax.ShapeDtypeStruct((n_rows, c), F32), jax.ShapeDtypeStruct((n_rows, width), BF16),
                   jax.ShapeDtypeStruct((8, c), F32)),
        input_output_aliases=aliases,
        compiler_params=_cparams("parallel", "arbitrary"),
    )(y, z, w.reshape(1, c), dyn, *extra)


ATT_SCALE = ATT_HEAD_DIM ** -0.5
ATT_SLOPES = [2.0 ** (-8.0 * (h + 1) / ATT_HEADS) for h in range(ATT_HEADS)]
Q_PER_KV = ATT_HEADS // ATT_KV_HEADS


def _dup_half(t, g, lo):
    tr = pltpu.roll(t, ATT_HEAD_DIM, axis=1)
    return jnp.where(lo, t, tr) if g == 0 else jnp.where(lo, tr, t)


def _att_band(kv_ref, kvp_ref, n):
    cur = kv_ref[...]
    prev = jnp.where(n == 0, 0.0, kvp_ref[...])
    lo = lax.broadcasted_iota(jnp.int32, (ATT_BLOCK, LANE), 1) < ATT_HEAD_DIM
    bands = []
    for g in range(ATT_KV_HEADS):
        kb = jnp.concatenate([_dup_half(prev[:, :LANE], g, lo), _dup_half(cur[:, :LANE], g, lo)], axis=0)
        vb = jnp.concatenate([_dup_half(prev[:, LANE:], g, lo), _dup_half(cur[:, LANE:], g, lo)], axis=0)
        bands.append((kb.astype(BF16), vb.astype(BF16)))
    return bands


def _att_tile(n):
    shape = (2 * ATT_BLOCK, ATT_BLOCK)
    row = lax.broadcasted_iota(jnp.int32, shape, 0)
    i = row & (ATT_BLOCK - 1)
    s = lax.broadcasted_iota(jnp.int32, shape, 1)
    upper = s > i
    dist = ((i - s) & (ATT_BLOCK - 1)).astype(F32)
    dead = upper & (n == 0)
    return upper, dist, dead, row[:, 0:1] < ATT_BLOCK


def _stack_pair(t, lo):
    return jnp.concatenate([jnp.where(lo, t, 0.0), jnp.where(lo, 0.0, t)], axis=0).astype(BF16)


def _att_exp(qs, kb, s_ref, j, tile):
    upper, dist, dead, first = tile
    s2 = _dot(qs, kb, _NT)
    slope = jnp.where(first, ATT_SLOPES[2 * j], ATT_SLOPES[2 * j + 1])
    sink = jnp.where(first, s_ref[0:1, 2 * j:2 * j + 1], s_ref[0:1, 2 * j + 1:2 * j + 2])
    s = jnp.where(upper, s2[:, :ATT_BLOCK], s2[:, ATT_BLOCK:]) - slope * dist
    s = jnp.where(dead, NEG, s)
    m = jnp.maximum(jnp.max(s, axis=-1, keepdims=True), sink)
    return jnp.exp(s - m), jnp.exp(sink - m)


def _band_split(t, upper):
    return jnp.concatenate([jnp.where(upper, t, 0.0), jnp.where(upper, 0.0, t)], axis=1)


def _att_fwd(q, kv, sinks8, name):
    q, q0, _ = _window(q)
    kv, kv0, _ = _window(kv)
    qoff, kvoff = q0 // Q_DIM, kv0 // (2 * LANE)
    n_rows = q.shape[0]
    nb = n_rows // ATT_BLOCK

    def body(q_ref, kv_ref, kvp_ref, s_ref, o_ref):
        n = pl.program_id(0)
        bands = _att_band(kv_ref, kvp_ref, n)
        lo = lax.broadcasted_iota(jnp.int32, (ATT_BLOCK, LANE), 1) < ATT_HEAD_DIM
        tile = _att_tile(n)
        ones_b = jnp.ones((2 * ATT_BLOCK, LANE), BF16)
        for j in range(ATT_HEADS // 2):
            kb, vb = bands[2 * j // Q_PER_KV]
            qs = _stack_pair(q_ref[:, j * LANE:(j + 1) * LANE] * ATT_SCALE, lo)
            p, es = _att_exp(qs, kb, s_ref, j, tile)
            pv = _dot(_band_split(p, tile[0]).astype(BF16), jnp.concatenate([vb, ones_b], axis=1))
            out = pv[:, :LANE] / (pv[:, LANE:] + es)
            o_ref[:, j * LANE:(j + 1) * LANE] = jnp.where(lo, out[:ATT_BLOCK], out[ATT_BLOCK:]).astype(BF16)

    return pl.pallas_call(
        body, name=name, grid=(nb,),
        in_specs=[pl.BlockSpec((ATT_BLOCK, Q_DIM), lambda n: (n, qoff)),
                  pl.BlockSpec((ATT_BLOCK, 2 * LANE), lambda n: (n, kvoff)),
                  pl.BlockSpec((ATT_BLOCK, 2 * LANE), lambda n: (jnp.maximum(n - 1, 0), kvoff)),
                  pl.BlockSpec((8, LANE), lambda n: (0, 0))],
        out_specs=pl.BlockSpec((ATT_BLOCK, Q_DIM), lambda n: (n, 0)),
        out_shape=jax.ShapeDtypeStruct((n_rows, Q_DIM), BF16),
        compiler_params=_cparams("parallel"),
    )(q, kv, kv, sinks8)


def _att_bwd(q, kv, sinks8, att, dout, name, into):
    q, q0, _ = _window(q)
    kv, kv0, _ = _window(kv)
    qoff, kvoff = q0 // Q_DIM, kv0 // (2 * LANE)
    n_rows = q.shape[0]
    nb = n_rows // ATT_BLOCK

    extra, extra_specs, aliases, col0, width = _into(into, 6, 0)
    dqoff = col0 // Q_DIM

    def body(*refs):
        q_ref, kv_ref, kvp_ref, s_ref, o_ref, do_ref = refs[:6]
        dq_ref, dkv_ref, acc_ref, carry_ref = refs[-4:]
        n = pl.program_id(0)

        @pl.when(n == 0)
        def _():
            acc_ref[...] = jnp.zeros_like(acc_ref)
            carry_ref[...] = jnp.zeros_like(carry_ref)

        @pl.when(n == nb)
        def _():
            dkv_ref[...] = carry_ref[...].astype(BF16)

        @pl.when(n < nb)
        def _():
            bands = _att_band(kv_ref, kvp_ref, n)
            lo = lax.broadcasted_iota(jnp.int32, (ATT_BLOCK, LANE), 1) < ATT_HEAD_DIM
            lane1 = lax.broadcasted_iota(jnp.int32, (1, LANE), 1)
            tile = _att_tile(n)
            upper, first = tile[0], tile[3]
            ones_b = jnp.ones((ATT_BLOCK, LANE), BF16)
            ones2_b = jnp.ones((2 * LANE, LANE), BF16)
            dk_acc = [jnp.zeros((2 * ATT_BLOCK, LANE), F32) for _ in range(ATT_KV_HEADS)]
            dv_acc = [jnp.zeros((2 * ATT_BLOCK, LANE), F32) for _ in range(ATT_KV_HEADS)]
            dsink = jnp.zeros((1, LANE), F32)
            for j in range(ATT_HEADS // 2):
                g = 2 * j // Q_PER_KV
                kb, vb = bands[g]
                qs = _stack_pair(q_ref[:, j * LANE:(j + 1) * LANE] * ATT_SCALE, lo)
                dop = do_ref[:, j * LANE:(j + 1) * LANE].astype(F32)
                dos = _stack_pair(dop, lo)
                pu, es = _att_exp(qs, kb, s_ref, j, tile)
                inv = 1.0 / (_dot(pu.astype(BF16), ones_b) + es)
                p = pu * inv
                od = dop * o_ref[:, j * LANE:(j + 1) * LANE].astype(F32)
                od = jnp.concatenate([jnp.where(lo, od, 0.0), jnp.where(lo, 0.0, od)], axis=0)
                od_hi = od.astype(BF16)
                delta = _dot(jnp.concatenate([od_hi, (od - od_hi.astype(F32)).astype(BF16)], axis=1), ones2_b)
                dp2 = _dot(dos, vb, _NT)
                dp = jnp.where(upper, dp2[:, :ATT_BLOCK], dp2[:, ATT_BLOCK:])
                ds2 = _band_split(p * (dp - delta), upper)
                psd = jnp.sum(es * inv * delta, axis=0, keepdims=True)
                psd0 = jnp.sum(jnp.where(first, es * inv * delta, 0.0), axis=0, keepdims=True)
                dsink = jnp.where(lane1 == 2 * j, -psd0, jnp.where(lane1 == 2 * j + 1, psd0 - psd, dsink))
                dq = _dot(ds2.astype(BF16), kb) * ATT_SCALE
                dq_ref[:, j * LANE:(j + 1) * LANE] = jnp.where(lo, dq[:ATT_BLOCK], dq[ATT_BLOCK:]).astype(BF16)
                dk_acc[g] = dk_acc[g] + _dot(ds2.T.astype(BF16), qs)
                dv_acc[g] = dv_acc[g] + _dot(_band_split(p, upper).T.astype(BF16), dos)
            acc_ref[0:1, :] += dsink
            lo2 = lax.broadcasted_iota(jnp.int32, (2 * ATT_BLOCK, LANE), 1) < ATT_HEAD_DIM
            folded = []
            for acc in (dk_acc, dv_acc):
                t0 = acc[0] + pltpu.roll(acc[0], ATT_HEAD_DIM, axis=1)
                t1 = acc[1] + pltpu.roll(acc[1], ATT_HEAD_DIM, axis=1)
                folded.append(jnp.where(lo2, t0, t1))
            band = jnp.concatenate(folded, axis=1)
            dkv_ref[...] = (carry_ref[...] + band[:ATT_BLOCK]).astype(BF16)
            carry_ref[...] = band[ATT_BLOCK:]

    def qmap(n):
        return (jnp.minimum(n, nb - 1), 0)

    return pl.pallas_call(
        body, name=name, grid=(nb + 1,),
        in_specs=[pl.BlockSpec((ATT_BLOCK, Q_DIM), lambda n: (jnp.minimum(n, nb - 1), qoff)),
                  pl.BlockSpec((ATT_BLOCK, 2 * LANE), lambda n: (jnp.minimum(n, nb - 1), kvoff)),
                  pl.BlockSpec((ATT_BLOCK, 2 * LANE),
                               lambda n: (jnp.maximum(jnp.minimum(n, nb - 1) - 1, 0), kvoff)),
                  pl.BlockSpec((8, LANE), lambda n: (0, 0)),
                  pl.BlockSpec((ATT_BLOCK, Q_DIM), qmap),
                  pl.BlockSpec((ATT_BLOCK, Q_DIM), qmap)] + extra_specs,
        out_specs=(pl.BlockSpec((ATT_BLOCK, Q_DIM), lambda n: (jnp.minimum(n, nb - 1), dqoff)),
                   pl.BlockSpec((ATT_BLOCK, 2 * LANE), lambda n: (jnp.maximum(n - 1, 0), 0)),
                   pl.BlockSpec((8, LANE), lambda n: (0, 0))),
        out_shape=(jax.ShapeDtypeStruct((n_rows, width), BF16), jax.ShapeDtypeStruct((n_rows, 2 * LANE), BF16),
                   jax.ShapeDtypeStruct((8, LANE), F32)),
        input_output_aliases=aliases,
        scratch_shapes=[pltpu.VMEM((ATT_BLOCK, 2 * LANE), F32)],
        compiler_params=_cparams("arbitrary"),
    )(q, kv, kv, sinks8, att, dout, *extra)


HEADS_PER_GROUP = SSD_HEADS // SSD_GROUPS
PAIRS_PER_GROUP = HEADS_PER_GROUP // 2
T = SSD_CHUNK


def _ssd_scalars(dtr_ref, par_ref):
    dt = _softplus(dtr_ref[...] + par_ref[0:1, :])
    a = -jnp.exp(par_ref[1:2, :])
    ri = lax.broadcasted_iota(jnp.int32, (T, T), 0)
    ci = lax.broadcasted_iota(jnp.int32, (T, T), 1)
    tril = (ri >= ci).astype(F32)
    cs = _dot_hi(tril, dt * a)
    cst = cs.T
    return dt, a, cs, cst, ri, ci


def _ssd_stacked_masks():
    row = lax.broadcasted_iota(jnp.int32, (2 * T, T), 0)
    t = row & (T - 1)
    s = lax.broadcasted_iota(jnp.int32, (2 * T, T), 1)
    return t >= s, s >= t, row[:, 0:1] < T


def _col_s(arr, k0):
    return jnp.concatenate([arr[:, k0:k0 + 1], arr[:, k0 + 1:k0 + 2]], axis=0)


def _row_s(arr_t, k0, first):
    return jnp.where(first, arr_t[k0:k0 + 1, :], arr_t[k0 + 1:k0 + 2, :])


def _lane_pick(lo, arr, k0):
    return jnp.where(lo, arr[:, k0:k0 + 1], arr[:, k0 + 1:k0 + 2])


def _ssd_fwd_stacked(xs, bm, cm, dtr, par, name):
    dtr, dt0, _ = _window(dtr)
    dtoff = dt0 // LANE
    n_rows = xs.shape[0]
    nc = n_rows // T
    gw = PAIRS_PER_GROUP * LANE

    def body(x_ref, b_ref, c_ref, dtr_ref, par_ref, y_ref, hs_ref, h_ref):
        @pl.when(pl.program_id(1) == 0)
        def _():
            h_ref[...] = jnp.zeros_like(h_ref)

        dt, a, cs, cst, _, _ = _ssd_scalars(dtr_ref, par_ref)
        tri_s, _, first = _ssd_stacked_masks()
        lo = lax.broadcasted_iota(jnp.int32, (T, LANE), 1) < SSD_CHUNK // 2
        ecs = jnp.exp(cs)
        dect = jnp.exp(cst[:, T - 1:T] - cst)
        etot = jnp.exp(cs[T - 1:T, :])
        bg = b_ref[...]
        cg = c_ref[...]
        cb = _dot(cg.astype(BF16), bg.astype(BF16), _NT)
        cb_s = jnp.concatenate([cb, cb], axis=0)
        cg_s = jnp.concatenate([cg, cg], axis=0)
        bgt_s = jnp.concatenate([bg.T, bg.T], axis=0)
        for j in range(PAIRS_PER_GROUP):
            k0, k1 = 2 * j, 2 * j + 1
            xp = x_ref[:, j * LANE:(j + 1) * LANE]
            hp = h_ref[j]
            hs_ref[0, 0, j] = hp
            rhs = jnp.concatenate([(xp * _lane_pick(lo, dt, k0)).astype(BF16), hp.astype(BF16)], axis=0)
            lm_s = jnp.exp(jnp.where(tri_s, _col_s(cs, k0) - _row_s(cst, k0, first), NEG))
            lhs = jnp.concatenate([lm_s * cb_s, cg_s * _col_s(ecs, k0)], axis=1).astype(BF16)
            y_s = _dot(lhs, rhs)
            s_s = _dot((bgt_s * _row_s(dect, k0, first)).astype(BF16), rhs[:T])
            dsk = jnp.where(lo[0:1, :], par_ref[2:3, k0:k0 + 1], par_ref[2:3, k1:k1 + 1])
            y_ref[:, j * LANE:(j + 1) * LANE] = jnp.where(lo, y_s[:T], y_s[T:]) + dsk * xp
            et = jnp.where(lo[0:1, :], etot[:, k0:k0 + 1], etot[:, k1:k1 + 1])
            h_ref[j] = hp * et + jnp.where(lo, s_s[:T], s_s[T:])

    return pl.pallas_call(
        body, name=name, grid=(SSD_GROUPS, nc),
        in_specs=[pl.BlockSpec((T, gw), lambda g, c: (c, g)),
                  pl.BlockSpec((T, SSD_STATE), lambda g, c: (c, g)),
                  pl.BlockSpec((T, SSD_STATE), lambda g, c: (c, g)),
                  pl.BlockSpec((T, LANE), lambda g, c: (c, g + dtoff)),
                  pl.BlockSpec((8, LANE), lambda g, c: (0, g))],
        out_specs=(pl.BlockSpec((T, gw), lambda g, c: (c, g)),
                   pl.BlockSpec((1, 1, PAIRS_PER_GROUP, SSD_STATE, LANE), lambda g, c: (g, c, 0, 0, 0))),
        out_shape=(jax.ShapeDtypeStruct((n_rows, SSD_D_INNER), F32),
                   jax.ShapeDtypeStruct((SSD_GROUPS, nc, PAIRS_PER_GROUP, SSD_STATE, LANE), F32)),
        scratch_shapes=[pltpu.VMEM((PAIRS_PER_GROUP, SSD_STATE, LANE), F32)],
        compiler_params=_cparams("parallel", "arbitrary"),
    )(xs, bm, cm, dtr, par)


def _ssd_bwd_stacked(xs, bm, cm, dtr, par, hs, dy, name, into):
    dtr, dt0, _ = _window(dtr)
    dtoff = dt0 // LANE
    n_rows = xs.shape[0]
    nc = n_rows // T
    gw = PAIRS_PER_GROUP * LANE

    extra, extra_specs, aliases, col0, width = _into(into, 7, 3)
    ddoff = col0 // LANE

    def body(*refs):
        x_ref, b_ref, c_ref, dtr_ref, par_ref, hs_ref, dy_ref = refs[:7]
        dx_ref, db_ref, dc_ref, ddtr_ref, acc_ref, dh_ref = refs[-6:]
        @pl.when(pl.program_id(1) == 0)
        def _():
            dh_ref[...] = jnp.zeros_like(dh_ref)
            acc_ref[...] = jnp.zeros_like(acc_ref)

        dt, a, cs, cst, ri, ci = _ssd_scalars(dtr_ref, par_ref)
        tri_s, trit_s, first = _ssd_stacked_masks()
        lane = lax.broadcasted_iota(jnp.int32, (T, LANE), 1)
        lo = lane < SSD_CHUNK // 2
        lane1 = lane[0:1, :]
        ecs = jnp.exp(cs)
        ecst = jnp.exp(cst)
        dec = jnp.exp(cs[T - 1:T, :] - cs)
        etot = jnp.exp(cs[T - 1:T, :])
        bg = b_ref[...]
        cg = c_ref[...]
        bg_b = bg.astype(BF16)
        cg_b = cg.astype(BF16)
        cb = _dot(cg_b, bg_b, _NT)
        cbt = _dot(bg_b, cg_b, _NT)
        cb_s = jnp.concatenate([cb, cb], axis=0)
        cbt_s = jnp.concatenate([cbt, cbt], axis=0)
        bg_s = jnp.concatenate([bg, bg], axis=0)
        cg_s = jnp.concatenate([cg, cg], axis=0)
        cgt_s = jnp.concatenate([cg.T, cg.T], axis=0)
        dbg = jnp.zeros((T, SSD_STATE), F32)
        dcg = jnp.zeros((T, SSD_STATE), F32)
        dcs_acc = jnp.zeros((T, LANE), F32)
        ddt_acc = jnp.zeros((T, LANE), F32)
        dsk_acc = jnp.zeros((1, LANE), F32)
        last_row = lax.broadcasted_iota(jnp.int32, (T, 1), 0) == T - 1
        for j in range(PAIRS_PER_GROUP):
            k0, k1 = 2 * j, 2 * j + 1
            xp = x_ref[:, j * LANE:(j + 1) * LANE]
            dtl = _lane_pick(lo, dt, k0)
            xdt = xp * dtl
            hp = hs_ref[0, 0, j]
            dhn = dh_ref[j]
            dyp = dy_ref[:, j * LANE:(j + 1) * LANE]
            xdt_b, hp_b, dhn_b, dyp_b = (v.astype(BF16) for v in (xdt, hp, dhn, dyp))
            cs_c, cs_r = _col_s(cs, k0), _row_s(cst, k0, first)
            lm_s = jnp.exp(jnp.where(tri_s, cs_c - cs_r, NEG))
            lmt_s = jnp.exp(jnp.where(trit_s, cs_r - cs_c, NEG))
            dec_c, ecs_c = _col_s(dec, k0), _col_s(ecs, k0)
            r1 = _dot(_stack_pair(dyp, lo), jnp.concatenate([xdt_b, hp_b], axis=0), _NT)
            r2 = _dot(_stack_pair(xdt, lo), jnp.concatenate([dyp_b, dhn_b], axis=0), _NT)
            dm_s, dyh_s = r1[:, :T], r1[:, T:]
            dmt_s, xdh_s = r2[:, :T], r2[:, T:]
            mm_s = lm_s * cb_s
            mmt_s = lmt_s * cbt_s
            bdec_s = bg_s * dec_c
            cexp_s = cg_s * ecs_c
            dx_s = _dot(jnp.concatenate([mmt_s, bdec_s], axis=1).astype(BF16),
                        jnp.concatenate([dyp_b, dhn_b], axis=0))
            dxdt = jnp.where(lo, dx_s[:T], dx_s[T:])
            dc_s = _dot((dm_s * lm_s).astype(BF16), bg_b) + dyh_s * ecs_c
            db_s = _dot((dmt_s * lmt_s).astype(BF16), cg_b) + xdh_s * dec_c
            dcg = dcg + dc_s[:T] + dc_s[T:]
            dbg = dbg + db_s[:T] + db_s[T:]
            dh_s = _dot((cgt_s * _row_s(ecst, k0, first)).astype(BF16), dyp_b)
            et = jnp.where(lo[0:1, :], etot[:, k0:k0 + 1], etot[:, k1:k1 + 1])
            dh_ref[j] = dhn * et + jnp.where(lo, dh_s[:T], dh_s[T:])
            e4 = jnp.sum(bdec_s * xdh_s, axis=-1, keepdims=True)
            dcs_s = (jnp.sum(dm_s * mm_s, axis=-1, keepdims=True) - jnp.sum(dmt_s * mmt_s, axis=-1, keepdims=True)
                     + jnp.sum(cexp_s * dyh_s, axis=-1, keepdims=True) - e4)
            hd = hp * dhn
            tsum0 = jnp.sum(e4[:T]) + etot[:, k0:k0 + 1] * jnp.sum(jnp.where(lo, hd, 0.0))
            tsum1 = jnp.sum(e4[T:]) + etot[:, k1:k1 + 1] * jnp.sum(jnp.where(lo, 0.0, hd))
            dcs0 = dcs_s[:T] + jnp.where(last_row, tsum0, 0.0)
            dcs1 = dcs_s[T:] + jnp.where(last_row, tsum1, 0.0)
            dcs_acc = jnp.where(lane == k0, dcs0, jnp.where(lane == k1, dcs1, dcs_acc))
            prod = dxdt * xp
            ddt_lo = jnp.sum(jnp.where(lo, prod, 0.0), axis=-1, keepdims=True)
            ddt_hi = jnp.sum(jnp.where(lo, 0.0, prod), axis=-1, keepdims=True)
            ddt_acc = jnp.where(lane == k0, ddt_lo, jnp.where(lane == k1, ddt_hi, ddt_acc))
            dyx = dyp * xp
            dsk_acc = jnp.where(lane1 == k0, jnp.sum(jnp.where(lo, dyx, 0.0)),
                                jnp.where(lane1 == k1, jnp.sum(jnp.where(lo, 0.0, dyx)), dsk_acc))
            dsk = jnp.where(lo[0:1, :], par_ref[2:3, k0:k0 + 1], par_ref[2:3, k1:k1 + 1])
            dx_ref[:, j * LANE:(j + 1) * LANE] = dxdt * dtl + dsk * dyp
        db_ref[...] = dbg
        dc_ref[...] = dcg
        triu = (ci >= ri).astype(F32)
        dda = _dot_hi(triu, dcs_acc)
        ddt = ddt_acc + dda * a
        ddtr = ddt * _sig(dtr_ref[...] + par_ref[0:1, :])
        ddtr_ref[...] = ddtr.astype(BF16)
        acc_ref[0:1, :] += jnp.sum(ddtr, axis=0, keepdims=True)
        acc_ref[1:2, :] += jnp.sum(dda * dt, axis=0, keepdims=True) * a
        acc_ref[2:3, :] += dsk_acc

    def rev(g, c):
        return (nc - 1 - c, g)

    return pl.pallas_call(
        body, name=name, grid=(SSD_GROUPS, nc),
        in_specs=[pl.BlockSpec((T, gw), rev),
                  pl.BlockSpec((T, SSD_STATE), rev),
                  pl.BlockSpec((T, SSD_STATE), rev),
                  pl.BlockSpec((T, LANE), lambda g, c: (nc - 1 - c, g + dtoff)),
                  pl.BlockSpec((8, LANE), lambda g, c: (0, g)),
                  pl.BlockSpec((1, 1, PAIRS_PER_GROUP, SSD_STATE, LANE), lambda g, c: (g, nc - 1 - c, 0, 0, 0)),
                  pl.BlockSpec((T, gw), rev)] + extra_specs,
        out_specs=(pl.BlockSpec((T, gw), rev),
                   pl.BlockSpec((T, SSD_STATE), rev),
                   pl.BlockSpec((T, SSD_STATE), rev),
                   pl.BlockSpec((T, LANE), lambda g, c: (nc - 1 - c, g + ddoff)),
                   pl.BlockSpec((8, LANE), lambda g, c: (0, g))),
        out_shape=(jax.ShapeDtypeStruct((n_rows, SSD_D_INNER), F32),
                   jax.ShapeDtypeStruct((n_rows, BC_DIM), F32),
                   jax.ShapeDtypeStruct((n_rows, BC_DIM), F32),
                   jax.ShapeDtypeStruct((n_rows, width), BF16),
                   jax.ShapeDtypeStruct((8, DT_PAD), F32)),
        input_output_aliases=aliases,
        scratch_shapes=[pltpu.VMEM((PAIRS_PER_GROUP, SSD_STATE, LANE), F32)],
        compiler_params=_cparams("parallel", "arbitrary"),
    )(xs, bm, cm, dtr, par, hs, dy, *extra)


def _cumsum_mm(mat, x):
    hi = x.astype(BF16)
    r = x - hi.astype(F32)
    mid = r.astype(BF16)
    lo = (r - mid.astype(F32)).astype(BF16)
    w = x.shape[1]
    out = _dot(mat, jnp.concatenate([hi, mid, lo], axis=1))
    return out[:, :w] + out[:, w:2 * w] + out[:, 2 * w:]


def _ssd_prep(dtr_ref, par_ref):
    dt = _softplus(dtr_ref[...] + par_ref[0:1, :])
    a = -jnp.exp(par_ref[1:2, :])
    ri = lax.broadcasted_iota(jnp.int32, (T, T), 0)
    ci = lax.broadcasted_iota(jnp.int32, (T, T), 1)
    cs = _cumsum_mm((ri >= ci).astype(BF16), dt * a)
    lo = lax.broadcasted_iota(jnp.int32, (T, LANE), 1) < SSD_CHUNK // 2

    def expand(arr):
        rows = arr.shape[0]
        return jnp.concatenate([jnp.where(lo[:rows], arr[:, 2 * j:2 * j + 1], arr[:, 2 * j + 1:2 * j + 2])
                                for j in range(PAIRS_PER_GROUP)], axis=1)

    tot = cs[T - 1:T, :]
    return {"dt": dt, "a": a, "cs": cs, "cst": cs.T, "lo": lo, "ri": ri, "ci": ci, "expand": expand,
            "dt_x": expand(dt), "ecs_x": expand(jnp.exp(cs)), "dec_x": expand(jnp.exp(tot - cs)),
            "et_x": expand(jnp.exp(tot)), "etot": jnp.exp(tot), "dsk_x": expand(par_ref[2:3, :])}


def _wide_masks():
    r = lax.broadcasted_iota(jnp.int32, (T, 2 * T), 0)
    l = lax.broadcasted_iota(jnp.int32, (T, 2 * T), 1)
    s = l & (T - 1)
    return r >= s, s >= r, l < T


def _wide_cs(q, k0, even):
    cs, cst = q["cs"], q["cst"]
    col = jnp.where(even, cs[:, k0:k0 + 1], cs[:, k0 + 1:k0 + 2])
    row = jnp.concatenate([cst[k0:k0 + 1, :], cst[k0 + 1:k0 + 2, :]], axis=1)
    return col, row


def _ssd_fwd(xs, bm, cm, dtr, par, name):
    dtr, dt0, _ = _window(dtr)
    dtoff = dt0 // LANE
    n_rows = xs.shape[0]
    nc = n_rows // T
    gw = PAIRS_PER_GROUP * LANE

    def body(x_ref, b_ref, c_ref, dtr_ref, par_ref, y_ref, hs_ref, h_ref):
        @pl.when(pl.program_id(1) == 0)
        def _():
            h_ref[...] = jnp.zeros_like(h_ref)

        q = _ssd_prep(dtr_ref, par_ref)
        lo = q["lo"]
        tri_w, _, even = _wide_masks()
        bg_b = b_ref[...].astype(BF16)
        cg_b = c_ref[...].astype(BF16)
        xv = x_ref[...]
        xdt = xv * q["dt_x"]
        h = h_ref[...]
        hs_ref[0, 0] = h
        yo = q["ecs_x"] * _dot(cg_b, h.astype(BF16))
        h_ref[...] = h * q["et_x"] + _dot(b_ref[...].T.astype(BF16), (xdt * q["dec_x"]).astype(BF16))
        cb = _dot(cg_b, bg_b, _NT)
        cb_w = jnp.concatenate([cb, cb], axis=1)
        for j in range(PAIRS_PER_GROUP):
            col, row = _wide_cs(q, 2 * j, even)
            m_w = (jnp.exp(jnp.where(tri_w, col - row, NEG)) * cb_w).astype(BF16)
            sl = slice(j * LANE, (j + 1) * LANE)
            y_ref[:, sl] = (_dot(m_w, _stack_pair(xdt[:, sl], lo)) + yo[:, sl] + q["dsk_x"][:, sl] * xv[:, sl])

    return pl.pallas_call(
        body, name=name, grid=(SSD_GROUPS, nc),
        in_specs=[pl.BlockSpec((T, gw), lambda g, c: (c, g)),
                  pl.BlockSpec((T, SSD_STATE), lambda g, c: (c, g)),
                  pl.BlockSpec((T, SSD_STATE), lambda g, c: (c, g)),
                  pl.BlockSpec((T, LANE), lambda g, c: (c, g + dtoff)),
                  pl.BlockSpec((8, LANE), lambda g, c: (0, g))],
        out_specs=(pl.BlockSpec((T, gw), lambda g, c: (c, g)),
                   pl.BlockSpec((1, 1, SSD_STATE, gw), lambda g, c: (g, c, 0, 0))),
        out_shape=(jax.ShapeDtypeStruct((n_rows, SSD_D_INNER), F32),
                   jax.ShapeDtypeStruct((SSD_GROUPS, nc, SSD_STATE, gw), F32)),
        scratch_shapes=[pltpu.VMEM((SSD_STATE, gw), F32)],
        compiler_params=_cparams("parallel", "arbitrary"),
    )(xs, bm, cm, dtr, par)


def _ssd_bwd(xs, bm, cm, dtr, par, hs, dy, name, into):
    dtr, dt0, _ = _window(dtr)
    dtoff = dt0 // LANE
    n_rows = xs.shape[0]
    nc = n_rows // T
    gw = PAIRS_PER_GROUP * LANE
    extra, extra_specs, aliases, col0, width = _into(into, 7, 3)
    ddoff = col0 // LANE

    def body(*refs):
        x_ref, b_ref, c_ref, dtr_ref, par_ref, hs_ref, dy_ref = refs[:7]
        dx_ref, db_ref, dc_ref, ddtr_ref, acc_ref, dh_ref = refs[-6:]

        @pl.when(pl.program_id(1) == 0)
        def _():
            dh_ref[...] = jnp.zeros_like(dh_ref)
            acc_ref[...] = jnp.zeros_like(acc_ref)

        q = _ssd_prep(dtr_ref, par_ref)
        lo, dt, a = q["lo"], q["dt"], q["a"]
        tri_w, trit_w, even = _wide_masks()
        lane = lax.broadcasted_iota(jnp.int32, (T, LANE), 1)
        lane1 = lane[0:1, :]
        last_row = lax.broadcasted_iota(jnp.int32, (T, 1), 0) == T - 1
        bg_b = b_ref[...].astype(BF16)
        cg_b = c_ref[...].astype(BF16)
        xv = x_ref[...]
        dyv = dy_ref[...]
        xdt = xv * q["dt_x"]
        h = hs_ref[0, 0]
        dhn = dh_ref[...]
        h_b, dhn_b = h.astype(BF16), dhn.astype(BF16)
        yo = q["ecs_x"] * _dot(cg_b, h_b)
        bdh = q["dec_x"] * _dot(bg_b, dhn_b)
        dye = (dyv * q["ecs_x"]).astype(BF16)
        xd = (xdt * q["dec_x"]).astype(BF16)
        dcg = _dot(dye, h_b, _NT)
        dbg = _dot(xd, dhn_b, _NT)
        dh_ref[...] = dhn * q["et_x"] + _dot(c_ref[...].T.astype(BF16), dye)
        e4_all = xdt * bdh
        f_all = dyv * yo - e4_all
        tot_row = jnp.sum(e4_all, axis=0, keepdims=True) + q["et_x"] * jnp.sum(h * dhn, axis=0, keepdims=True)
        dsk_row = jnp.sum(dyv * xv, axis=0, keepdims=True)
        cb = _dot(cg_b, bg_b, _NT)
        cbt = _dot(bg_b, cg_b, _NT)
        cb_w = jnp.concatenate([cb, cb], axis=1)
        cbt_w = jnp.concatenate([cbt, cbt], axis=1)
        dcb = jnp.zeros((T, T), F32)
        dcbt = jnp.zeros((T, T), F32)
        dcs_acc = jnp.zeros((T, LANE), F32)
        ddt_acc = jnp.zeros((T, LANE), F32)
        dsk_acc = jnp.zeros((1, LANE), F32)
        tot_acc = jnp.zeros((1, LANE), F32)
        ind_r = lax.broadcasted_iota(jnp.int32, (2 * T, LANE), 0)
        ind_l = lax.broadcasted_iota(jnp.int32, (2 * T, LANE), 1)

        def halves(t):
            return (jnp.sum(jnp.where(lo[0:1], t, 0.0), axis=-1, keepdims=True),
                    jnp.sum(jnp.where(lo[0:1], 0.0, t), axis=-1, keepdims=True))

        def split2(t):
            hi = t.astype(BF16)
            return jnp.concatenate([hi, (t - hi.astype(F32)).astype(BF16)], axis=1)

        for j in range(PAIRS_PER_GROUP):
            k0, k1 = 2 * j, 2 * j + 1
            sl = slice(j * LANE, (j + 1) * LANE)
            col, row = _wide_cs(q, k0, even)
            lm_w = jnp.exp(jnp.where(tri_w, col - row, NEG))
            lmt_w = jnp.exp(jnp.where(trit_w, row - col, NEG))
            dyp, xp = dyv[:, sl], xdt[:, sl]
            dym, xm = _stack_pair(dyp, lo), _stack_pair(xp, lo)
            dm_w = _dot(dyp.astype(BF16), xm, _NT)
            dmt_w = _dot(xp.astype(BF16), dym, _NT)
            mm_w = lm_w * cb_w
            mmt_w = lmt_w * cbt_w
            dxdt = _dot(mmt_w.astype(BF16), dym) + bdh[:, sl]
            g1 = dm_w * lm_w
            g2 = dmt_w * lmt_w
            dcb = dcb + g1[:, :T] + g1[:, T:]
            dcbt = dcbt + g2[:, :T] + g2[:, T:]
            ind_w = jnp.where(ind_l == jnp.where(ind_r < T, k0, k1), 1.0, 0.0).astype(BF16)
            ind_p = jnp.where(ind_l[:T] == jnp.where(ind_r[:T] < SSD_CHUNK // 2, k0, k1), 1.0, 0.0).astype(BF16)
            dcs_acc = dcs_acc + _dot(
                jnp.concatenate([split2(dm_w * mm_w - dmt_w * mmt_w), split2(f_all[:, sl])], axis=1),
                jnp.concatenate([ind_w, ind_w, ind_p, ind_p], axis=0))
            ddt_acc = ddt_acc + _dot(split2(dxdt * xv[:, sl]), jnp.concatenate([ind_p, ind_p], axis=0))
            tot2 = halves(tot_row[:, sl])
            tot_acc = jnp.where(lane1 == k0, tot2[0], jnp.where(lane1 == k1, tot2[1], tot_acc))
            dsk2 = halves(dsk_row[:, sl])
            dsk_acc = jnp.where(lane1 == k0, dsk2[0], jnp.where(lane1 == k1, dsk2[1], dsk_acc))
            dx_ref[:, sl] = dxdt * q["dt_x"][:, sl] + q["dsk_x"][:, sl] * dyp
        dcs_acc = dcs_acc + jnp.where(last_row, tot_acc, 0.0)
        dc_ref[...] = dcg + _dot(dcb.astype(BF16), bg_b)
        db_ref[...] = dbg + _dot(dcbt.astype(BF16), cg_b)
        dda = _cumsum_mm((q["ci"] >= q["ri"]).astype(BF16), dcs_acc)
        ddt = ddt_acc + dda * a
        ddtr = ddt * _sig(dtr_ref[...] + par_ref[0:1, :])
        ddtr_ref[...] = ddtr.astype(BF16)
        acc_ref[0:1, :] += jnp.sum(ddtr, axis=0, keepdims=True)
        acc_ref[1:2, :] += jnp.sum(dda * dt, axis=0, keepdims=True) * a
        acc_ref[2:3, :] += dsk_acc

    def rev(g, c):
        return (nc - 1 - c, g)

    return pl.pallas_call(
        body, name=name, grid=(SSD_GROUPS, nc),
        in_specs=[pl.BlockSpec((T, gw), rev),
                  pl.BlockSpec((T, SSD_STATE), rev),
                  pl.BlockSpec((T, SSD_STATE), rev),
                  pl.BlockSpec((T, LANE), lambda g, c: (nc - 1 - c, g + dtoff)),
                  pl.BlockSpec((8, LANE), lambda g, c: (0, g)),
                  pl.BlockSpec((1, 1, SSD_STATE, gw), lambda g, c: (g, nc - 1 - c, 0, 0)),
                  pl.BlockSpec((T, gw), rev)] + extra_specs,
        out_specs=(pl.BlockSpec((T, gw), rev),
                   pl.BlockSpec((T, SSD_STATE), rev),
                   pl.BlockSpec((T, SSD_STATE), rev),
                   pl.BlockSpec((T, LANE), lambda g, c: (nc - 1 - c, g + ddoff)),
                   pl.BlockSpec((8, LANE), lambda g, c: (0, g))),
        out_shape=(jax.ShapeDtypeStruct((n_rows, SSD_D_INNER), F32),
                   jax.ShapeDtypeStruct((n_rows, BC_DIM), F32),
                   jax.ShapeDtypeStruct((n_rows, BC_DIM), F32),
                   jax.ShapeDtypeStruct((n_rows, width), BF16),
                   jax.ShapeDtypeStruct((8, DT_PAD), F32)),
        input_output_aliases=aliases,
        scratch_shapes=[pltpu.VMEM((SSD_STATE, gw), F32)],
        compiler_params=_cparams("parallel", "arbitrary"),
    )(xs, bm, cm, dtr, par, hs, dy, *extra)


ADAM_ROWS = 256


def _adamw(lands, w, m, v, name):
    na = len(lands)
    n_slots, r, wd = lands[0].shape
    tr = r if r <= 2 * ADAM_ROWS else ADAM_ROWS
    nj = r // tr
    bc1 = 1.0 - ADAM_B1 ** ADAM_STEP
    bc2 = 1.0 - ADAM_B2 ** ADAM_STEP

    def body(*refs):
        l_refs = refs[:na]
        w_ref, m_ref, v_ref, g_ref, d_ref, nm_ref, nv_ref = refs[na:]
        for a in range(na):
            @pl.when(pl.program_id(0) == a)
            def _(l_ref=l_refs[a]):
                g = l_ref[0].astype(F32)
                for s in range(1, n_slots):
                    g = g + l_ref[s].astype(F32)
                mn = ADAM_B1 * m_ref[0] + (1.0 - ADAM_B1) * g
                vn = ADAM_B2 * v_ref[0] + (1.0 - ADAM_B2) * (g * g)
                mh = mn / bc1
                vh = vn / bc2
                g_ref[0] = g
                nm_ref[0] = mn
                nv_ref[0] = vn
                d_ref[0] = -ADAM_LR * (mh / (jnp.sqrt(vh) + ADAM_EPS) + ADAM_WD * w_ref[0])

    def land_spec(a):
        return pl.BlockSpec((n_slots, tr, wd),
                            lambda i, j: (0, jnp.where(i == a, j, jnp.where(i < a, 0, nj - 1)), 0))

    blk = pl.BlockSpec((1, tr, wd), lambda i, j: (i, j, 0))
    shp = jax.ShapeDtypeStruct((na, r, wd), F32)
    return pl.pallas_call(
        body, name=name, grid=(na, nj), in_specs=[land_spec(a) for a in range(na)] + [blk, blk, blk],
        out_specs=(blk, blk, blk, blk), out_shape=(shp, shp, shp, shp),
        compiler_params=_cparams("arbitrary", "arbitrary"),
    )(*lands, w, m, v)


def _mesh_pos():
    return lax.axis_index("x"), lax.axis_index("y"), lax.axis_index("c")


def _peer(pos, k):
    x, y, c = pos
    px = 1 - x if (k >> 2) & 1 else x
    py = 1 - y if (k >> 1) & 1 else y
    pc = 1 - c if k & 1 else c
    return px, py, pc


def _flat(pos):
    return 4 * pos[0] + 2 * pos[1] + pos[2]


HBM_SPEC = pl.BlockSpec(memory_space=pl.ANY)


ROW_SHARDED = ("w_ssd_out", "w_att_out", "w_mix_out", "w_ffn_down")
COL_SHARDED = ("w_in", "w_ffn_gate", "w_ffn_up")
GATHERED = ROW_SHARDED + COL_SHARDED + ("conv_w",)
BIG = ROW_SHARDED + COL_SHARDED


SEM_SPEC = pl.BlockSpec(memory_space=pltpu.SEMAPHORE)
TOKEN = jax.ShapeDtypeStruct((8, LANE), F32)
SPLIT_EFFECT = pltpu.SideEffectType.DATAFLOW_SIDE_EFFECTING
GATHER_ROWS = "gather_rows"
GATHER_SLOT = "gather_slot"
SCATTER_ROWS = "scatter_rows"
SCATTER_SLOT = "scatter_slot"


def _land_shape(kind, src):
    if kind == GATHER_ROWS:
        return (N_DEV * src.shape[0],) + src.shape[1:]
    if kind == GATHER_SLOT:
        return (N_DEV,) + src.shape
    if kind == SCATTER_ROWS:
        return (N_DEV, src.shape[0] // N_DEV) + src.shape[1:]
    return src.shape


def _views(kind, src_ref, land_ref, pos, k):
    me = _flat(pos)
    if kind == GATHER_ROWS:
        r = src_ref.shape[0]
        return src_ref, land_ref.at[pl.ds(pl.multiple_of(me * r, 16), r), :]
    if kind == GATHER_SLOT:
        return src_ref, land_ref.at[me]
    dev = _flat(_peer(pos, k))
    if kind == SCATTER_ROWS:
        r = land_ref.shape[1]
        return src_ref.at[pl.ds(pl.multiple_of(dev * r, 16), r), :], land_ref.at[k]
    return src_ref.at[dev], land_ref.at[k]


def _hbm(x):
    return pltpu.with_memory_space_constraint(x, pltpu.HBM)


def _exchange_start(items, after, name):
    kinds = [k for k, _ in items]
    srcs = [_hbm(s) for _, s in items]
    lands = [_hbm(lax.empty(_land_shape(k, s), s.dtype)) for k, s in items]
    n = len(items)
    n_copy = n * (N_DEV - 1)

    def body(*refs):
        src_refs, land_refs = refs[:n], refs[n:2 * n]
        send_sems, recv_sems = refs[2 * n + 1], refs[2 * n + 2]
        token_ref = refs[4 * n + 3]
        pos = _mesh_pos()
        for i, kind in enumerate(kinds):
            for k in range(1, N_DEV):
                s, d = _views(kind, src_refs[i], land_refs[i], pos, k)
                j = i * (N_DEV - 1) + k - 1
                pltpu.make_async_remote_copy(src_ref=s, dst_ref=d, send_sem=send_sems.at[j], recv_sem=recv_sems.at[j],
                                             device_id=_peer(pos, k), device_id_type=MESH_ID).start()
        token_ref[...] = jnp.zeros_like(token_ref)

    arrs = srcs + lands
    outs = pl.pallas_call(
        body, name=name,
        in_specs=[HBM_SPEC] * (2 * n + 1),
        out_specs=[SEM_SPEC, SEM_SPEC] + [HBM_SPEC] * (2 * n) + [pl.BlockSpec(memory_space=pltpu.VMEM)],
        out_shape=[pltpu.SemaphoreType.DMA((n_copy,)), pltpu.SemaphoreType.DMA((n_copy,))]
        + [pltpu.HBM(a.shape, a.dtype) for a in arrs] + [TOKEN],
        input_output_aliases={i: 2 + i for i in range(2 * n)},
        compiler_params=pltpu.CompilerParams(has_side_effects=SPLIT_EFFECT),
    )(*arrs, after)
    return {"kinds": kinds, "send": outs[0], "recv": outs[1], "arrs": outs[2:2 + 2 * n], "token": outs[-1]}


def _exchange_wait(ex, after, name):
    kinds = ex["kinds"]
    n = len(kinds)

    def body(*refs):
        src_refs, land_refs = refs[:n], refs[n:2 * n]
        send_sems, recv_sems = refs[2 * n], refs[2 * n + 1]
        token_ref = refs[-1]
        pos = _mesh_pos()
        for i, kind in enumerate(kinds):
            for k in range(1, N_DEV):
                s, d = _views(kind, src_refs[i], land_refs[i], pos, k)
                j = i * (N_DEV - 1) + k - 1
                cp = pltpu.make_async_remote_copy(src_ref=s, dst_ref=d, send_sem=send_sems.at[j],
                                                  recv_sem=recv_sems.at[j], device_id=_peer(pos, k),
                                                  device_id_type=MESH_ID)
                cp.wait_send()
                cp.wait_recv()
        token_ref[...] = jnp.zeros_like(token_ref)

    outs = pl.pallas_call(
        body, name=name,
        in_specs=[HBM_SPEC] * (2 * n) + [SEM_SPEC, SEM_SPEC, HBM_SPEC],
        out_specs=[HBM_SPEC] * (2 * n) + [pl.BlockSpec(memory_space=pltpu.VMEM)],
        out_shape=[pltpu.HBM(a.shape, a.dtype) for a in ex["arrs"]] + [TOKEN],
        input_output_aliases={i: i for i in range(2 * n)},
        compiler_params=pltpu.CompilerParams(has_side_effects=SPLIT_EFFECT),
    )(*ex["arrs"], ex["send"], ex["recv"], after)
    lands = [_place_own(k, s, d) for k, s, d in zip(kinds, outs[:n], outs[n:2 * n])]
    return lands, outs[-1]


def _place_own(kind, src, land):
    me = _flat(_mesh_pos())
    zeros = (0,) * (src.ndim - 1)
    if kind == GATHER_ROWS:
        return lax.dynamic_update_slice(land, src, (me * src.shape[0],) + zeros)
    if kind == GATHER_SLOT:
        return lax.dynamic_update_slice(land, src[None], (me,) + (0,) * src.ndim)
    if kind == SCATTER_ROWS:
        r = land.shape[1]
        own = lax.dynamic_slice(src, (me * r,) + zeros, (r,) + src.shape[1:])
    else:
        own = lax.dynamic_index_in_dim(src, me, 0, keepdims=False)
    return lax.dynamic_update_slice(land, own[None], (0,) * land.ndim)


def _all_gather_small(x, name):
    r, w = x.shape

    def body(x_ref, out_ref, send_sems, recv_sems):
        pos = _mesh_pos()
        me = _flat(pos)
        copies = []
        for k in range(1, N_DEV):
            cp = pltpu.make_async_remote_copy(
                src_ref=x_ref, dst_ref=out_ref.at[me], send_sem=send_sems.at[k - 1], recv_sem=recv_sems.at[k - 1],
                device_id=_peer(pos, k), device_id_type=MESH_ID)
            cp.start()
            copies.append(cp)
        out_ref[me] = x_ref[...]
        for cp in copies:
            cp.wait()

    vmem = pl.BlockSpec(memory_space=pltpu.VMEM)
    return pl.pallas_call(
        body, name=name, in_specs=[vmem], out_specs=vmem,
        out_shape=jax.ShapeDtypeStruct((N_DEV, r, w), x.dtype),
        scratch_shapes=[pltpu.SemaphoreType.DMA((N_DEV - 1,)), pltpu.SemaphoreType.DMA((N_DEV - 1,))],
        compiler_params=pltpu.CompilerParams(has_side_effects=True),
    )(x)


def _cols(g, lo, hi):
    c = g.shape[-1]
    parts = []
    for d in range(N_DEV):
        a, b = max(lo, d * c), min(hi, (d + 1) * c)
        if a < b:
            parts.append(g[d, :, a - d * c:b - d * c])
    return parts[0] if len(parts) == 1 else jnp.concatenate(parts, axis=1)


def _col_chunks(g):
    c = g.shape[-1] // N_DEV
    return jnp.stack([g[:, d * c:(d + 1) * c] for d in range(N_DEV)])


IN_PART = ("w_in", "conv_w")
OUT_PART = ROW_SHARDED + ("w_ffn_gate", "w_ffn_up")


def _gather_items(w, names, l):
    items = []
    for n in names:
        blk = w[n][l] if n == "conv_w" else w[n][l].astype(BF16)
        items.append((GATHER_ROWS if n in ROW_SHARDED else GATHER_SLOT, blk))
    return items


def _scatter_items(grads, names):
    return [(SCATTER_ROWS, grads[n]) if n in ROW_SHARDED else (SCATTER_SLOT, _col_chunks(grads[n]))
            for n in names]


SMALL = ("ln_in_g", "ln_in_b", "conv_b", "dt_bias", "a_log", "d_skip", "ssd_norm_w", "att_sinks",
         "ln_mix_g", "ln_mix_b", "ln_ffn_g", "ln_ffn_b")


def _pack_small(vals):
    flat = jnp.concatenate([vals[n].reshape(-1) for n in SMALL])
    n = flat.shape[0]
    rows = -(-n // LANE)
    rows = -(-rows // 8) * 8
    return jnp.pad(flat, (0, rows * LANE - n)).reshape(rows, LANE)


def _unpack_small(buf, shapes):
    flat = buf.reshape(-1)
    off = 0
    out = {}
    for n in SMALL:
        cnt = math.prod(shapes[n])
        out[n] = flat[off:off + cnt].reshape(shapes[n])
        off += cnt
    return out


def _to_group_major(v):
    lead = v.shape[:-1]
    t = v.reshape(lead + (SSD_GROUPS, HEADS_PER_GROUP))
    t = jnp.pad(t, [(0, 0)] * len(lead) + [(0, 0), (0, LANE - HEADS_PER_GROUP)])
    return t.reshape(lead + (DT_PAD,))


def _from_group_major(v):
    lead = v.shape[:-1]
    return v.reshape(lead + (SSD_GROUPS, LANE))[..., :HEADS_PER_GROUP].reshape(lead + (SSD_HEADS,))


def _rows8(v):
    return jnp.pad(v, ((0, 8 - v.shape[0]), (0, 0)))


IN_OFFS = {"q": (0, 1024), "kv": (1024, 1280), "z": (1280, 3328), "xs": (3328, 5376), "b": (5376, 5888),
           "c": (5888, 6400), "dt": (6400, 6432), "gl": (6432, 8480)}
PIECES = ("q", "kv", "z", "xs", "b", "c", "dt", "gl")


CAT = ("z", "xs", "gl", "q", "b", "c", "dt", "kv")
CAT_WIDTH = {"q": 1024, "z": 2048, "xs": 2048, "gl": 2048, "b": 512, "c": 512, "kv": 256, "dt": DT_PAD}
CAT_OFF = {p: sum(CAT_WIDTH[q] for q in CAT[:i]) for i, p in enumerate(CAT)}
CAT_DIM = sum(CAT_WIDTH.values())
MAIN_DIM = CAT_OFF["kv"]


def _cat_w_in(g):
    pieces = {p: _cols(g, lo, hi) for p, (lo, hi) in IN_OFFS.items()}
    pieces["dt"] = _to_group_major(pieces["dt"])
    return jnp.concatenate([pieces[p] for p in CAT], axis=1)


def _uncat_dw_in(dw):
    pieces = {p: dw[:, CAT_OFF[p]:CAT_OFF[p] + CAT_WIDTH[p]] for p in CAT}
    pieces["dt"] = _from_group_major(pieces["dt"])
    return jnp.concatenate([pieces[p] for p in PIECES], axis=1)


def _params_out(W):
    p = {n: W[n] for n in ROW_SHARDED}
    for n in ("w_ffn_gate", "w_ffn_up"):
        p[n] = _cols(W[n], 0, FFN_HIDDEN)
    return p


def _params_in(l, W, sm):
    p = {"w_cat": _cat_w_in(W["w_in"])}
    cw = _cols(W["conv_w"], 0, SSD_D_INNER + 2 * BC_DIM)
    cb = sm["conv_b"][l]
    segs = {"xs": (0, 2048), "b": (2048, 2560), "c": (2560, 3072)}
    p["conv_w8"] = {s: _rows8(cw[:, lo:hi]) for s, (lo, hi) in segs.items()}
    p["conv_b8"] = {s: _rows8(cb[None, lo:hi]) for s, (lo, hi) in segs.items()}
    p["ssd_par"] = _rows8(jnp.stack([_to_group_major(sm["dt_bias"][l]), _to_group_major(sm["a_log"][l]),
                                     _to_group_major(sm["d_skip"][l])]))
    p["norm_w"] = sm["ssd_norm_w"][l]
    p["sinks8"] = _rows8(jnp.pad(sm["att_sinks"][l], (0, LANE - ATT_HEADS))[None])
    for n in ("ln_mix_g", "ln_mix_b", "ln_ffn_g", "ln_ffn_b"):
        p[n] = sm[n][l]
    return p


def _fwd_mixers(h0, p, l, dep=None):
    tag = f"l{l}_"
    a = {"h0": h0}
    proj = _mm(h0, p["w_cat"], "nn", tag + "proj", dep=dep)
    for pc in CAT:
        a[pc] = (proj, CAT_OFF[pc], CAT_WIDTH[pc])
    for s in ("xs", "b", "c"):
        a[s + "c"] = _conv_fwd(a[s], p["conv_w8"][s], p["conv_b8"][s], tag + "conv_" + s)
    a["y"], a["hs"] = _ssd_fwd(a["xsc"], a["bc"], a["cc"], a["dt"], p["ssd_par"], tag + "ssd_fwd")
    a["yn"] = _gnorm_fwd(a["y"], a["z"], p["norm_w"], tag + "gnorm")
    a["att"] = _att_fwd(a["q"], a["kv"], p["sinks8"], tag + "att_fwd")
    return a


def _fwd_out(a, p, l, dep=None):
    tag = f"l{l}_"
    h0 = a["h0"]
    a["ya"] = _mm(a["yn"], p["w_ssd_out"], "nn", tag + "ssd_out", dep=dep)
    a["yb"] = _mm(a["att"], p["w_att_out"], "nn", tag + "att_out", dep=dep)
    a["merged"] = _merge_fwd(a["gl"], a["ya"], a["yb"], tag + "merge")
    a["mix"] = _mm(a["merged"], p["w_mix_out"], "nn", tag + "mix_out")
    a["h1"] = _ln_fwd(h0, a["mix"], p["ln_mix_g"], p["ln_mix_b"], ALPHA, tag + "ln_mix")
    a["fg"] = _mm(a["h1"], p["w_ffn_gate"], "nn", tag + "ffn_gate")
    a["fu"] = _mm(a["h1"], p["w_ffn_up"], "nn", tag + "ffn_up")
    a["act"] = _swiglu_fwd(a["fg"], a["fu"], tag + "swiglu")
    a["ffn"] = _mm(a["act"], p["w_ffn_down"], "nn", tag + "ffn_down")
    a["h2"] = _ln_fwd(a["h1"], a["ffn"], p["ln_ffn_g"], p["ln_ffn_b"], ALPHA, tag + "ln_ffn")
    return a


def _dw(x, dy, name, dep=None):
    return _mm(x, dy, "tn", name, out_dtype=BF16, dep=dep)


def _bwd_out(a, p, dh2, l, dep=None):
    tag = f"l{l}_b_"
    gw, gs = {}, {}
    du2, acc = _ln_bwd(a["h1"], a["ffn"], p["ln_ffn_g"], dh2, ALPHA, tag + "ln_ffn")
    gs["ln_ffn_g"], gs["ln_ffn_b"] = acc[0], acc[1]
    gw["w_ffn_down"] = _dw(a["act"], du2, tag + "dw_down", dep=dep)
    dact = _mm(du2, p["w_ffn_down"], "nt", tag + "dact", dep=dep)
    dfg, dfu = _swiglu_bwd(a["fg"], a["fu"], dact, tag + "swiglu")
    gw["w_ffn_gate"] = _dw(a["h1"], dfg, tag + "dw_gate")
    gw["w_ffn_up"] = _dw(a["h1"], dfu, tag + "dw_up")
    dh1 = _mm(dfg, p["w_ffn_gate"], "nt", tag + "dh1_gate", add=du2, add_scale=ALPHA)
    dh1 = _mm(dfu, p["w_ffn_up"], "nt", tag + "dh1_up", add=dh1)
    du1, acc = _ln_bwd(a["h0"], a["mix"], p["ln_mix_g"], dh1, ALPHA, tag + "ln_mix")
    gs["ln_mix_g"], gs["ln_mix_b"] = acc[0], acc[1]
    gw["w_mix_out"] = _dw(a["merged"], du1, tag + "dw_mix")
    dmerged = _mm(du1, p["w_mix_out"], "nt", tag + "dmerged")
    dya, dyb, dproj = _merge_bwd(a["gl"], a["ya"], a["yb"], dmerged, tag + "merge",
                                 (None, CAT_OFF["gl"], MAIN_DIM))
    gw["w_ssd_out"] = _dw(a["yn"], dya, tag + "dw_ssd")
    gw["w_att_out"] = _dw(a["att"], dyb, tag + "dw_att")
    return {"du1": du1, "dya": dya, "dyb": dyb, "dproj": dproj}, gw, gs


def _bwd_mixers(a, p, carry, l, dep=None):
    tag = f"l{l}_b_"
    gs = {}
    du1, dproj = carry["du1"], carry["dproj"]

    def win(pc):
        return (dproj, CAT_OFF[pc], MAIN_DIM)

    dyn = _mm(carry["dya"], p["w_ssd_out"], "nt", tag + "dyn", dep=dep)
    datt = _mm(carry["dyb"], p["w_att_out"], "nt", tag + "datt", out_dtype=BF16, dep=dep)
    dproj, dkv, acc = _att_bwd(a["q"], a["kv"], p["sinks8"], a["att"], datt, tag + "att", win("q"))
    gs["att_sinks"] = acc[0, :ATT_HEADS]
    dy, dproj, acc = _gnorm_bwd(a["y"], a["z"], p["norm_w"], dyn, tag + "gnorm", win("z"))
    gs["ssd_norm_w"] = acc[0]
    dxs, dbm, dcm, dproj, acc = _ssd_bwd(a["xsc"], a["bc"], a["cc"], a["dt"], p["ssd_par"], a["hs"], dy,
                                         tag + "ssd", win("dt"))
    gs["dt_bias"], gs["a_log"], gs["d_skip"] = (_from_group_major(acc[i]) for i in range(3))
    dconv_w, dconv_b = [], []
    for s, dout in (("xs", dxs), ("b", dbm), ("c", dcm)):
        dc, acc = _conv_bwd_pre(a[s], p["conv_w8"][s], p["conv_b8"][s], dout, tag + "conv_pre_" + s)
        dconv_w.append(acc[:CONV_TAPS])
        dconv_b.append(acc[CONV_TAPS])
        dproj = _conv_bwd_in(dc, p["conv_w8"][s], tag + "conv_in_" + s, win(s))
    gconv = jnp.concatenate(dconv_w, axis=1)
    gs["conv_b"] = jnp.concatenate(dconv_b)
    w_main, w_kv = p["w_cat"][:, :MAIN_DIM], p["w_cat"][:, MAIN_DIM:]
    dw = jnp.concatenate([_dw(a["h0"], dproj, tag + "dw_in"), _dw(a["h0"], dkv, tag + "dw_in_kv")], axis=1)

    def grad_h0(dep=None):
        dh0 = _mm(dproj, w_main, "nt", tag + "dh0", add=du1, add_scale=ALPHA, dep=dep)
        return _mm(dkv, w_kv, "nt", tag + "dh0_kv", add=dh0)

    return grad_h0, _uncat_dw_in(dw), gconv, gs


def _step(x, target, w, m, v):
    x2 = x[0]
    t2 = target[0]
    tok = jnp.zeros(TOKEN.shape, TOKEN.dtype)

    ex = _exchange_start(_gather_items(w, IN_PART, 0), tok, "gather_l0_in_start")
    lands, tok = _exchange_wait(ex, ex["token"], "gather_l0_in_wait")
    p0 = _params_in(0, dict(zip(IN_PART, lands)), w)
    ex = _exchange_start(_gather_items(w, OUT_PART, 0) + _gather_items(w, IN_PART, 1), tok,
                         "gather_l0_out_l1_in_start")
    h = _ln_fwd(x2, None, w["ln_in_g"], w["ln_in_b"], 1.0, "ln_in")
    a0 = _fwd_mixers(h, p0, 0, dep=ex["token"])
    lands, tok = _exchange_wait(ex, a0["att"], "gather_l0_out_l1_in_wait")
    p0.update(_params_out(dict(zip(OUT_PART, lands))))
    p1 = _params_in(1, dict(zip(IN_PART, lands[len(OUT_PART):])), w)
    ex = _exchange_start(_gather_items(w, OUT_PART, 1), tok, "gather_l1_out_start")
    a0 = _fwd_out(a0, p0, 0, dep=ex["token"])
    lands, tok = _exchange_wait(ex, a0["h2"], "gather_l1_out_wait")
    p1.update(_params_out(dict(zip(OUT_PART, lands))))
    a1 = _fwd_out(_fwd_mixers(a0["h2"], p1, 1), p1, 1)

    sse, dh = _loss_fwd_bwd(a1["h2"], t2, "loss")
    loss = lax.psum(0.5 / D_MODEL * sse[0, 0], ("x", "y", "c"))

    carry, gw1, gs1 = _bwd_out(a1, p1, dh, 1)
    grad_h0, gw1["w_in"], gw1["conv_w"], gs = _bwd_mixers(a1, p1, carry, 1)
    dh = grad_h0()
    gs1.update(gs)
    ex1 = _exchange_start(_scatter_items(gw1, GATHERED), tok, "scatter_l1_start")
    carry, gw0, gs0 = _bwd_out(a0, p0, dh, 0, dep=ex1["token"])
    lands, tok = _exchange_wait(ex1, carry["dyb"], "scatter_l1_wait")
    land1 = dict(zip(GATHERED, lands))
    ex0 = _exchange_start(_scatter_items(gw0, OUT_PART), tok, "scatter_l0_out_start")
    grad_h0, gw0["w_in"], gw0["conv_w"], gs = _bwd_mixers(a0, p0, carry, 0, dep=ex0["token"])
    gs0.update(gs)
    lands, tok = _exchange_wait(ex0, gw0["w_in"], "scatter_l0_out_wait")
    land0 = dict(zip(OUT_PART, lands))
    ex0 = _exchange_start(_scatter_items(gw0, IN_PART), tok, "scatter_l0_in_start")
    dh = grad_h0(dep=ex0["token"])
    grad_x2, acc = _ln_bwd(x2, None, w["ln_in_g"], dh, 1.0, "ln_in_b")

    outs = [{} for _ in range(4)]

    def update(names):
        res = None
        for n in names:
            res = _adamw([land0[n], land1[n]], w[n], m[n], v[n], "adamw_" + n)
            for o, t in zip(outs, res):
                o[n] = t
        return res[1]

    update(OUT_PART)
    gsm = {"ln_in_g": acc[0], "ln_in_b": acc[1]}
    for n in SMALL[2:]:
        gsm[n] = jnp.stack([gs0[n], gs1[n]])
    small_shapes = {n: w[n].shape for n in SMALL}
    land_s = _all_gather_small(_pack_small(gsm), "small_grads_all_gather")
    res = _adamw([land_s], _pack_small(w)[None], _pack_small(m)[None], _pack_small(v)[None], "adamw_small")
    for o, t in zip(outs, res):
        o.update(_unpack_small(t[0], small_shapes))
    lands, _ = _exchange_wait(ex0, res[1], "scatter_l0_in_wait")
    land0.update(zip(IN_PART, lands))
    update(IN_PART)
    return loss, grad_x2[None], outs


WEIGHT_NAMES = ("ln_in_g", "ln_in_b", "w_in", "conv_w", "conv_b", "dt_bias", "a_log", "d_skip", "ssd_norm_w",
                "att_sinks", "w_ssd_out", "w_att_out", "w_mix_out", "ln_mix_g", "ln_mix_b", "w_ffn_gate",
                "w_ffn_up", "w_ffn_down", "ln_ffn_g", "ln_ffn_b")


def kernel(x, ln_in_g, ln_in_b, w_in, conv_w, conv_b, dt_bias, a_log, d_skip, ssd_norm_w, att_sinks, w_ssd_out, w_att_out, w_mix_out, ln_mix_g, ln_mix_b, w_ffn_gate, w_ffn_up, w_ffn_down, ln_ffn_g, ln_ffn_b, loss_target, m_ln_in_g, m_ln_in_b, m_w_in, m_conv_w, m_conv_b, m_dt_bias, m_a_log, m_d_skip, m_ssd_norm_w, m_att_sinks, m_w_ssd_out, m_w_att_out, m_w_mix_out, m_ln_mix_g, m_ln_mix_b, m_w_ffn_gate, m_w_ffn_up, m_w_ffn_down, m_ln_ffn_g, m_ln_ffn_b, v_ln_in_g, v_ln_in_b, v_w_in, v_conv_w, v_conv_b, v_dt_bias, v_a_log, v_d_skip, v_ssd_norm_w, v_att_sinks, v_w_ssd_out, v_w_att_out, v_w_mix_out, v_ln_mix_g, v_ln_mix_b, v_w_ffn_gate, v_w_ffn_up, v_w_ffn_down, v_ln_ffn_g, v_ln_ffn_b):
    w = dict(zip(WEIGHT_NAMES, (ln_in_g, ln_in_b, w_in, conv_w, conv_b, dt_bias, a_log, d_skip, ssd_norm_w,
                                att_sinks, w_ssd_out, w_att_out, w_mix_out, ln_mix_g, ln_mix_b, w_ffn_gate,
                                w_ffn_up, w_ffn_down, ln_ffn_g, ln_ffn_b)))
    m = dict(zip(WEIGHT_NAMES, (m_ln_in_g, m_ln_in_b, m_w_in, m_conv_w, m_conv_b, m_dt_bias, m_a_log, m_d_skip,
                                m_ssd_norm_w, m_att_sinks, m_w_ssd_out, m_w_att_out, m_w_mix_out, m_ln_mix_g,
                                m_ln_mix_b, m_w_ffn_gate, m_w_ffn_up, m_w_ffn_down, m_ln_ffn_g, m_ln_ffn_b)))
    v = dict(zip(WEIGHT_NAMES, (v_ln_in_g, v_ln_in_b, v_w_in, v_conv_w, v_conv_b, v_dt_bias, v_a_log, v_d_skip,
                                v_ssd_norm_w, v_att_sinks, v_w_ssd_out, v_w_att_out, v_w_mix_out, v_ln_mix_g,
                                v_ln_mix_b, v_w_ffn_gate, v_w_ffn_up, v_w_ffn_down, v_ln_ffn_g, v_ln_ffn_b)))
    loss, grad_x, outs = _step(x, loss_target, w, m, v)
    result = [loss, grad_x]
    for o in outs:
        result.extend(o[n] for n in WEIGHT_NAMES)
    return tuple(result)
```

```python
import functools
import math

import jax
import jax.numpy as jnp
from jax import lax
from jax.experimental import pallas as pl
from jax.experimental.pallas import tpu as pltpu

F32 = jnp.float32
BF16 = jnp.bfloat16

D_MODEL = 1024
DEPTH = 2
N_DEV = 8
ATT_HEADS = 16
ATT_KV_HEADS = 2
ATT_HEAD_DIM = 64
ATT_BLOCK = 128
SSD_D_INNER = 2048
SSD_HEADS = 32
SSD_GROUPS = 4
SSD_STATE = 128
SSD_CHUNK = 128
FFN_HIDDEN = 2816
LN_EPS = 1e-5
RMS_EPS = 1e-5
ALPHA = (2 * DEPTH) ** 0.25
Q_DIM = 1024
KV_DIM = 128
BC_DIM = 512
IN_DIM = 8480
IN_SHARD = IN_DIM // N_DEV
DT_PAD = 512

ADAM_LR = 0.001
ADAM_B1 = 0.9
ADAM_B2 = 0.999
ADAM_EPS = 1e-08
ADAM_WD = 0.01
ADAM_STEP = 10

LANE = 128
VMEM_LIMIT = 48 * 1024 * 1024
PACK_W = 1024
NEG = -1e30

_NN = (((1,), (0,)), ((), ()))
_NT = (((1,), (1,)), ((), ()))
_TN = (((0,), (0,)), ((), ()))
MESH_ID = pl.DeviceIdType.MESH


def _dot(a, b, dims=_NN):
    return lax.dot_general(a, b, dims, preferred_element_type=F32)


def _dot_hi(a, b):
    return lax.dot_general(a, b, _NN, preferred_element_type=F32, precision=lax.Precision.HIGHEST)


def _sig(x):
    return 1.0 / (1.0 + jnp.exp(-x))


def _softplus(x):
    return jnp.maximum(x, 0.0) + jnp.log(1.0 + jnp.exp(-jnp.abs(x)))


def _cparams(*sem):
    return pltpu.CompilerParams(dimension_semantics=sem, vmem_limit_bytes=VMEM_LIMIT)


def _pick(n, cap):
    if n <= cap:
        return n
    best = None
    for t in range(LANE, cap + 1, LANE):
        if n % t == 0:
            best = t
    assert best is not None, (n, cap)
    return best


def _tile(n):
    if n <= 1024 or n % 1024 == 0:
        return min(n, 1024)
    return _pick(n, 1408)


def _rows(n):
    return min(512, n)


def _window(x):
    return x if isinstance(x, tuple) else (x, 0, x.shape[1])


def _into(into, n_in, out_idx):
    buf, col0, width = into
    if buf is None:
        return [], [], {}, col0, width
    return [buf], [pl.BlockSpec(memory_space=pl.ANY)], {n_in: out_idx}, col0, width


def _mm(a, b, mode, name, add=None, add_scale=1.0, out_dtype=F32, dep=None):
    if mode == "nn":
        m, k = a.shape
        n = b.shape[1]
    elif mode == "nt":
        m, k = a.shape
        n = b.shape[0]
    else:
        k, m = a.shape
        n = b.shape[1]
    tm = _tile(m)
    tn = _pick(n, 2176) if mode == "tn" and n > 1024 else _tile(n)
    tk = _pick(k, 2176) if mode == "nt" and a.dtype == BF16 and k > 2816 else _tile(k)
    nk = k // tk
    has_add = add is not None
    dims = {"nn": _NN, "nt": _NT, "tn": _TN}[mode]

    def body(*refs):
        if dep is not None:
            refs = refs[:-3] + refs[-2:]
        if has_add:
            a_ref, b_ref, add_ref, o_ref, acc_ref = refs
        else:
            a_ref, b_ref, o_ref, acc_ref = refs
        kk = pl.program_id(2)

        @pl.when(kk == 0)
        def _():
            if has_add:
                acc_ref[...] = add_scale * add_ref[...].astype(F32)
            else:
                acc_ref[...] = jnp.zeros_like(acc_ref)

        acc_ref[...] += _dot(a_ref[...].astype(BF16), b_ref[...].astype(BF16), dims)

        @pl.when(kk == nk - 1)
        def _():
            o_ref[...] = acc_ref[...].astype(o_ref.dtype)

    if mode == "nn":
        a_spec = pl.BlockSpec((tm, tk), lambda i, j, kk: (i, kk))
        b_spec = pl.BlockSpec((tk, tn), lambda i, j, kk: (kk, j))
    elif mode == "nt":
        a_spec = pl.BlockSpec((tm, tk), lambda i, j, kk: (i, kk))
        b_spec = pl.BlockSpec((tn, tk), lambda i, j, kk: (j, kk))
    else:
        a_spec = pl.BlockSpec((tk, tm), lambda i, j, kk: (kk, i))
        b_spec = pl.BlockSpec((tk, tn), lambda i, j, kk: (kk, j))
    o_spec = pl.BlockSpec((tm, tn), lambda i, j, kk: (i, j))
    in_specs = [a_spec, b_spec] + ([o_spec] if has_add else [])
    args = (a, b) + ((add,) if has_add else ())
    if dep is not None:
        in_specs.append(pl.BlockSpec((8, LANE), lambda i, j, kk: (0, 0)))
        args += (dep,)
    return pl.pallas_call(
        body, name=name, grid=(m // tm, n // tn, nk),
        in_specs=in_specs, out_specs=o_spec,
        out_shape=jax.ShapeDtypeStruct((m, n), out_dtype),
        scratch_shapes=[pltpu.VMEM((tm, tn), F32)],
        compiler_params=_cparams("parallel", "parallel", "arbitrary"),
    )(*args)


def _vec_spec(width):
    return pl.BlockSpec((1, width), lambda i: (0, 0))


def _ln_fwd(a, b, gamma, beta, alpha, name):
    n_rows, dm = a.shape
    has_b = b is not None

    def body(*refs):
        if has_b:
            a_ref, b_ref, g_ref, be_ref, o_ref = refs
            u = alpha * a_ref[...] + b_ref[...]
        else:
            a_ref, g_ref, be_ref, o_ref = refs
            u = a_ref[...]
        mu = jnp.mean(u, axis=-1, keepdims=True)
        d = u - mu
        var = jnp.mean(d * d, axis=-1, keepdims=True)
        o_ref[...] = d * lax.rsqrt(var + LN_EPS) * g_ref[...] + be_ref[...]

    row = pl.BlockSpec((_rows(n_rows),dm), lambda i: (i, 0))
    in_specs = [row] + ([row] if has_b else []) + [_vec_spec(dm), _vec_spec(dm)]
    args = (a,) + ((b,) if has_b else ()) + (gamma.reshape(1, dm), beta.reshape(1, dm))
    return pl.pallas_call(
        body, name=name, grid=(n_rows // _rows(n_rows),), in_specs=in_specs, out_specs=row,
        out_shape=jax.ShapeDtypeStruct((n_rows, dm), F32),
        compiler_params=_cparams("parallel"),
    )(*args)


def _ln_bwd(a, b, gamma, dy, alpha, name):
    n_rows, dm = a.shape
    has_b = b is not None

    def body(*refs):
        if has_b:
            a_ref, b_ref, g_ref, dy_ref, du_ref, acc_ref = refs
            u = alpha * a_ref[...] + b_ref[...]
        else:
            a_ref, g_ref, dy_ref, du_ref, acc_ref = refs
            u = a_ref[...]

        @pl.when(pl.program_id(0) == 0)
        def _():
            acc_ref[...] = jnp.zeros_like(acc_ref)

        mu = jnp.mean(u, axis=-1, keepdims=True)
        d = u - mu
        var = jnp.mean(d * d, axis=-1, keepdims=True)
        rstd = lax.rsqrt(var + LN_EPS)
        xhat = d * rstd
        dyv = dy_ref[...]
        acc_ref[0:1, :] += jnp.sum(dyv * xhat, axis=0, keepdims=True)
        acc_ref[1:2, :] += jnp.sum(dyv, axis=0, keepdims=True)
        dxh = dyv * g_ref[...]
        m1 = jnp.mean(dxh, axis=-1, keepdims=True)
        m2 = jnp.mean(dxh * xhat, axis=-1, keepdims=True)
        du_ref[...] = rstd * (dxh - m1 - xhat * m2)

    row = pl.BlockSpec((_rows(n_rows),dm), lambda i: (i, 0))
    in_specs = [row] + ([row] if has_b else []) + [_vec_spec(dm), row]
    args = (a,) + ((b,) if has_b else ()) + (gamma.reshape(1, dm), dy)
    return pl.pallas_call(
        body, name=name, grid=(n_rows // _rows(n_rows),), in_specs=in_specs,
        out_specs=(row, pl.BlockSpec((8, dm), lambda i: (0, 0))),
        out_shape=(jax.ShapeDtypeStruct((n_rows, dm), F32), jax.ShapeDtypeStruct((8, dm), F32)),
        compiler_params=_cparams("arbitrary"),
    )(*args)


def _loss_fwd_bwd(y, target, name):
    n_rows, dm = y.shape

    def body(y_ref, t_ref, acc_ref, dy_ref):
        @pl.when(pl.program_id(0) == 0)
        def _():
            acc_ref[...] = jnp.zeros_like(acc_ref)

        d = y_ref[...] - t_ref[...]
        acc_ref[...] += jnp.sum(d * d)
        dy_ref[...] = d * (1.0 / dm)

    row = pl.BlockSpec((_rows(n_rows),dm), lambda i: (i, 0))
    return pl.pallas_call(
        body, name=name, grid=(n_rows // _rows(n_rows),), in_specs=[row, row],
        out_specs=(pl.BlockSpec((8, LANE), lambda i: (0, 0)), row),
        out_shape=(jax.ShapeDtypeStruct((8, LANE), F32), jax.ShapeDtypeStruct((n_rows, dm), F32)),
        compiler_params=_cparams("arbitrary"),
    )(y, target)


def _swiglu_fwd(g, u, name):
    n_rows, w = g.shape
    tw = _pick(w, 1408)

    def body(g_ref, u_ref, o_ref):
        gv = g_ref[...]
        o_ref[...] = (gv * _sig(gv) * u_ref[...]).astype(BF16)

    blk = pl.BlockSpec((_rows(n_rows),tw), lambda i, j: (i, j))
    return pl.pallas_call(
        body, name=name, grid=(n_rows // _rows(n_rows), w // tw), in_specs=[blk, blk], out_specs=blk,
        out_shape=jax.ShapeDtypeStruct((n_rows, w), BF16),
        compiler_params=_cparams("parallel", "parallel"),
    )(g, u)


def _swiglu_bwd(g, u, dact, name):
    n_rows, w = g.shape
    tw = _pick(w, 1408)

    def body(g_ref, u_ref, da_ref, dg_ref, du_ref):
        gv = g_ref[...]
        s = _sig(gv)
        da = da_ref[...]
        dg_ref[...] = (da * u_ref[...] * (s * (1.0 + gv * (1.0 - s)))).astype(BF16)
        du_ref[...] = (da * gv * s).astype(BF16)

    blk = pl.BlockSpec((_rows(n_rows),tw), lambda i, j: (i, j))
    return pl.pallas_call(
        body, name=name, grid=(n_rows // _rows(n_rows), w // tw), in_specs=[blk, blk, blk], out_specs=(blk, blk),
        out_shape=(jax.ShapeDtypeStruct((n_rows, w), BF16), jax.ShapeDtypeStruct((n_rows, w), BF16)),
        compiler_params=_cparams("parallel", "parallel"),
    )(g, u, dact)


FFN_ROWS = 512


def _ffn_in(h, wg, wu, name, dep=None):
    m, k = h.shape
    n = wg.shape[1]
    tm, tn = min(FFN_ROWS, m), _tile(n)

    def body(*refs):
        h_ref, wg_ref, wu_ref = refs[:3]
        g_ref, u_ref, act_ref = refs[-3:]
        hb = h_ref[...].astype(BF16)
        g = _dot(hb, wg_ref[...])
        u = _dot(hb, wu_ref[...])
        g_ref[...] = g
        u_ref[...] = u
        act_ref[...] = (g * _sig(g) * u).astype(BF16)

    rows = pl.BlockSpec((tm, k), lambda j, i: (i, 0))
    wcol = pl.BlockSpec((k, tn), lambda j, i: (0, j))
    out = pl.BlockSpec((tm, tn), lambda j, i: (i, j))
    in_specs, args = [rows, wcol, wcol], (h, wg, wu)
    if dep is not None:
        in_specs.append(pl.BlockSpec((8, LANE), lambda j, i: (0, 0)))
        args += (dep,)
    return pl.pallas_call(
        body, name=name, grid=(n // tn, m // tm), in_specs=in_specs, out_specs=(out, out, out),
        out_shape=(jax.ShapeDtypeStruct((m, n), F32), jax.ShapeDtypeStruct((m, n), F32),
                   jax.ShapeDtypeStruct((m, n), BF16)),
        compiler_params=_cparams("parallel", "parallel"),
    )(*args)


def _ffn_dact(dy, wd, g, u, name, dep=None):
    m, k = dy.shape
    n = wd.shape[0]
    tm, tn = min(FFN_ROWS, m), _tile(n)

    def body(*refs):
        dy_ref, wd_ref, g_ref, u_ref = refs[:4]
        dg_ref, du_ref = refs[-2:]
        da = _dot(dy_ref[...].astype(BF16), wd_ref[...], _NT)
        gv = g_ref[...]
        s = _sig(gv)
        dg_ref[...] = (da * u_ref[...] * (s * (1.0 + gv * (1.0 - s)))).astype(BF16)
        du_ref[...] = (da * gv * s).astype(BF16)

    rows = pl.BlockSpec((tm, k), lambda j, i: (i, 0))
    wrow = pl.BlockSpec((tn, k), lambda j, i: (j, 0))
    out = pl.BlockSpec((tm, tn), lambda j, i: (i, j))
    in_specs, args = [rows, wrow, out, out], (dy, wd, g, u)
    if dep is not None:
        in_specs.append(pl.BlockSpec((8, LANE), lambda j, i: (0, 0)))
        args += (dep,)
    return pl.pallas_call(
        body, name=name, grid=(n // tn, m // tm), in_specs=in_specs, out_specs=(out, out),
        out_shape=(jax.ShapeDtypeStruct((m, n), BF16), jax.ShapeDtypeStruct((m, n), BF16)),
        compiler_params=_cparams("parallel", "parallel"),
    )(*args)


def _gate_specs(gl, n_rows, dm):
    arr, g0, _ = _window(gl)
    return arr, [pl.BlockSpec((_rows(n_rows), dm), lambda i, k=k: (i, g0 // dm + k)) for k in range(2)]


def _merge_fwd(gl, ya, yb, name):
    n_rows, dm = ya.shape
    gl_arr, gspecs = _gate_specs(gl, n_rows, dm)

    def body(ga_ref, gb_ref, ya_ref, yb_ref, o_ref):
        o_ref[...] = (_sig(ga_ref[...]) * ya_ref[...] + _sig(gb_ref[...]) * yb_ref[...]).astype(BF16)

    row = pl.BlockSpec((_rows(n_rows),dm), lambda i: (i, 0))
    return pl.pallas_call(
        body, name=name, grid=(n_rows // _rows(n_rows),), in_specs=gspecs + [row, row], out_specs=row,
        out_shape=jax.ShapeDtypeStruct((n_rows, dm), BF16),
        compiler_params=_cparams("parallel"),
    )(gl_arr, gl_arr, ya, yb)


def _merge_bwd(gl, ya, yb, dmerged, name, into):
    n_rows, dm = ya.shape
    gl_arr, gspecs = _gate_specs(gl, n_rows, dm)
    extra, extra_specs, aliases, col0, width = _into(into, 5, 2)

    def body(*refs):
        ga_ref, gb_ref, ya_ref, yb_ref, dm_ref = refs[:5]
        dya_ref, dyb_ref, dgl_ref = refs[-3:]
        ga = _sig(ga_ref[...])
        gb = _sig(gb_ref[...])
        dmv = dm_ref[...]
        dya_ref[...] = (dmv * ga).astype(BF16)
        dyb_ref[...] = (dmv * gb).astype(BF16)
        dgl_ref[:, :dm] = (dmv * ya_ref[...] * ga * (1.0 - ga)).astype(BF16)
        dgl_ref[:, dm:] = (dmv * yb_ref[...] * gb * (1.0 - gb)).astype(BF16)

    row = pl.BlockSpec((_rows(n_rows),dm), lambda i: (i, 0))
    row2 = pl.BlockSpec((_rows(n_rows),2 * dm), lambda i: (i, col0 // (2 * dm)))
    return pl.pallas_call(
        body, name=name, grid=(n_rows // _rows(n_rows),), in_specs=gspecs + [row, row, row] + extra_specs,
        out_specs=(row, row, row2),
        out_shape=(jax.ShapeDtypeStruct((n_rows, dm), BF16), jax.ShapeDtypeStruct((n_rows, dm), BF16),
                   jax.ShapeDtypeStruct((n_rows, width), BF16)),
        input_output_aliases=aliases,
        compiler_params=_cparams("parallel"),
    )(gl_arr, gl_arr, ya, yb, dmerged, *extra)


CONV_TAPS = 4
CONV_COLS = 512
HALO = 8


def _shift_down(cur, prev8, s, row8):
    r = pltpu.roll(cur, s, axis=0)
    top = jnp.where(row8 < s, pltpu.roll(prev8, s, axis=0), r[0:HALO])
    return jnp.concatenate([top, r[HALO:]], axis=0)


def _shift_up(cur, next8, s, row8):
    n = cur.shape[0]
    r = pltpu.roll(cur, n - s, axis=0)
    bot = jnp.where(row8 >= HALO - s, pltpu.roll(next8, HALO - s, axis=0), r[n - HALO:])
    return jnp.concatenate([r[:n - HALO], bot], axis=0)


def _conv_pre(u_ref, prev_ref, w_ref, b_ref, li):
    cur = u_ref[...]
    prev8 = jnp.where(li == 0, 0.0, prev_ref[...])
    row8 = lax.broadcasted_iota(jnp.int32, prev8.shape, 0)
    shifted = [cur] + [_shift_down(cur, prev8, s, row8) for s in range(1, CONV_TAPS)]
    acc = b_ref[...] + shifted[0] * w_ref[CONV_TAPS - 1:CONV_TAPS, :]
    for s in range(1, CONV_TAPS):
        acc = acc + shifted[s] * w_ref[CONV_TAPS - 1 - s:CONV_TAPS - s, :]
    return acc, shifted


def _conv_specs(n_rows, tl, col0=0):
    off = col0 // CONV_COLS
    cur = pl.BlockSpec((tl, CONV_COLS), lambda cj, li: (li, cj + off))
    prev = pl.BlockSpec((HALO, CONV_COLS), lambda cj, li: (jnp.maximum(li * (tl // HALO) - 1, 0), cj + off))
    nxt = pl.BlockSpec((HALO, CONV_COLS),
                       lambda cj, li: (jnp.minimum((li + 1) * (tl // HALO), n_rows // HALO - 1), cj + off))
    par = pl.BlockSpec((8, CONV_COLS), lambda cj, li: (0, cj + off))
    return cur, prev, nxt, par


def _conv_fwd(u, w8, b8, name):
    u, u0, c = _window(u)
    n_rows = u.shape[0]
    tl = _rows(n_rows)
    cur, _, _, par = _conv_specs(n_rows, tl)
    ucur, prev, _, _ = _conv_specs(n_rows, tl, u0)

    def body(u_ref, prev_ref, w_ref, b_ref, o_ref):
        acc, _ = _conv_pre(u_ref, prev_ref, w_ref, b_ref[0:1, :], pl.program_id(1))
        o_ref[...] = acc * _sig(acc)

    return pl.pallas_call(
        body, name=name, grid=(c // CONV_COLS, n_rows // tl), in_specs=[ucur, prev, par, par], out_specs=cur,
        out_shape=jax.ShapeDtypeStruct((n_rows, c), F32),
        compiler_params=_cparams("parallel", "parallel"),
    )(u, u, w8, b8)


def _conv_bwd_pre(u, w8, b8, dout, name):
    u, u0, c = _window(u)
    n_rows = u.shape[0]
    tl = _rows(n_rows)
    cur, _, _, par = _conv_specs(n_rows, tl)
    ucur, prev, _, _ = _conv_specs(n_rows, tl, u0)

    def body(u_ref, prev_ref, w_ref, b_ref, do_ref, dc_ref, acc_ref):
        @pl.when(pl.program_id(1) == 0)
        def _():
            acc_ref[...] = jnp.zeros_like(acc_ref)

        acc, shifted = _conv_pre(u_ref, prev_ref, w_ref, b_ref[0:1, :], pl.program_id(1))
        sg = _sig(acc)
        dc = do_ref[...] * (sg * (1.0 + acc * (1.0 - sg)))
        dc_ref[...] = dc
        for k in range(CONV_TAPS):
            acc_ref[k:k + 1, :] += jnp.sum(dc * shifted[CONV_TAPS - 1 - k], axis=0, keepdims=True)
        acc_ref[CONV_TAPS:CONV_TAPS + 1, :] += jnp.sum(dc, axis=0, keepdims=True)

    return pl.pallas_call(
        body, name=name, grid=(c // CONV_COLS, n_rows // tl), in_specs=[ucur, prev, par, par, cur],
        out_specs=(cur, par),
        out_shape=(jax.ShapeDtypeStruct((n_rows, c), F32), jax.ShapeDtypeStruct((8, c), F32)),
        compiler_params=_cparams("parallel", "arbitrary"),
    )(u, u, w8, b8, dout)


def _conv_bwd_in(dc, w8, name, into):
    n_rows, c = dc.shape
    tl = _rows(n_rows)
    cur, _, nxt, par = _conv_specs(n_rows, tl)
    n_l = n_rows // tl
    extra, extra_specs, aliases, col0, width = _into(into, 3, 0)
    out_spec = _conv_specs(n_rows, tl, col0)[0]

    def body(*refs):
        dc_ref, next_ref, w_ref = refs[:3]
        o_ref = refs[-1]
        cur_v = dc_ref[...]
        next8 = jnp.where(pl.program_id(1) == n_l - 1, 0.0, next_ref[...])
        row8 = lax.broadcasted_iota(jnp.int32, next8.shape, 0)
        acc = cur_v * w_ref[CONV_TAPS - 1:CONV_TAPS, :]
        for s in range(1, CONV_TAPS):
            acc = acc + _shift_up(cur_v, next8, s, row8) * w_ref[CONV_TAPS - 1 - s:CONV_TAPS - s, :]
        o_ref[...] = acc.astype(BF16)

    return pl.pallas_call(
        body, name=name, grid=(c // CONV_COLS, n_l), in_specs=[cur, nxt, par] + extra_specs, out_specs=out_spec,
        out_shape=jax.ShapeDtypeStruct((n_rows, width), BF16), input_output_aliases=aliases,
        compiler_params=_cparams("parallel", "parallel"),
    )(dc, dc, w8, *extra)


NORM_GROUP = SSD_D_INNER // SSD_GROUPS


def _gnorm_fwd(y, z, w, name):
    n_rows, c = y.shape
    z, z0, _ = _window(z)
    zoff = z0 // NORM_GROUP

    def body(y_ref, z_ref, w_ref, o_ref):
        zv = z_ref[...]
        yg = y_ref[...] * (zv * _sig(zv))
        r = lax.rsqrt(jnp.mean(yg * yg, axis=-1, keepdims=True) + RMS_EPS)
        o_ref[...] = (yg * r * w_ref[...]).astype(BF16)

    blk = pl.BlockSpec((_rows(n_rows),NORM_GROUP), lambda i, j: (i, j))
    zblk = pl.BlockSpec((_rows(n_rows),NORM_GROUP), lambda i, j: (i, j + zoff))
    wspec = pl.BlockSpec((1, NORM_GROUP), lambda i, j: (0, j))
    return pl.pallas_call(
        body, name=name, grid=(n_rows // _rows(n_rows), c // NORM_GROUP), in_specs=[blk, zblk, wspec], out_specs=blk,
        out_shape=jax.ShapeDtypeStruct((n_rows, c), BF16),
        compiler_params=_cparams("parallel", "parallel"),
    )(y, z, w.reshape(1, c))


def _gnorm_bwd(y, z, w, dyn, name, into):
    n_rows, c = y.shape
    z, z0, _ = _window(z)
    zoff = z0 // NORM_GROUP
    extra, extra_specs, aliases, col0, width = _into(into, 4, 1)
    doff = col0 // NORM_GROUP

    def body(*refs):
        y_ref, z_ref, w_ref, dn_ref = refs[:4]
        dy_ref, dz_ref, acc_ref = refs[-3:]
        @pl.when(pl.program_id(1) == 0)
        def _():
            acc_ref[...] = jnp.zeros_like(acc_ref)

        zv = z_ref[...]
        yv = y_ref[...]
        sz = _sig(zv)
        silu = zv * sz
        yg = yv * silu
        r = lax.rsqrt(jnp.mean(yg * yg, axis=-1, keepdims=True) + RMS_EPS)
        nrm = yg * r
        dn = dn_ref[...]
        acc_ref[0:1, :] += jnp.sum(dn * nrm, axis=0, keepdims=True)
        dnw = dn * w_ref[...]
        dyg = r * (dnw - nrm * jnp.mean(dnw * nrm, axis=-1, keepdims=True))
        dy_ref[...] = dyg * silu
        dz_ref[...] = (dyg * yv * (sz * (1.0 + zv * (1.0 - sz)))).astype(BF16)

    blk = pl.BlockSpec((_rows(n_rows),NORM_GROUP), lambda j, i: (i, j))
    zblk = pl.BlockSpec((_rows(n_rows),NORM_GROUP), lambda j, i: (i, j + zoff))
    wspec = pl.BlockSpec((1, NORM_GROUP), lambda j, i: (0, j))
    aspec = pl.BlockSpec((8, NORM_GROUP), lambda j, i: (0, j))
    return pl.pallas_call(
        body, name=name, grid=(c // NORM_GROUP, n_rows // _rows(n_rows)),
        in_specs=[blk, zblk, wspec, blk] + extra_specs,
        out_specs=(blk, pl.BlockSpec((_rows(n_rows), NORM_GROUP), lambda j, i: (i, j + doff)), aspec),
        out_shape=(jax.ShapeDtypeStruct((n_rows, c), F32), jax.ShapeDtypeStruct((n_rows, width), BF16),
                   jax.ShapeDtypeStruct((8, c), F32)),
        input_output_aliases=aliases,
        compiler_params=_cparams("parallel", "arbitrary"),
    )(y, z, w.reshape(1, c), dyn, *extra)


ATT_SCALE = ATT_HEAD_DIM ** -0.5
ATT_SLOPES = [2.0 ** (-8.0 * (h + 1) / ATT_HEADS) for h in range(ATT_HEADS)]
Q_PER_KV = ATT_HEADS // ATT_KV_HEADS


def _dup_half(t, g, lo):
    tr = pltpu.roll(t, ATT_HEAD_DIM, axis=1)
    return jnp.where(lo, t, tr) if g == 0 else jnp.where(lo, tr, t)


def _att_band(kv_ref, kvp_ref, n):
    cur = kv_ref[...]
    prev = jnp.where(n == 0, 0.0, kvp_ref[...])
    lo = lax.broadcasted_iota(jnp.int32, (ATT_BLOCK, LANE), 1) < ATT_HEAD_DIM
    bands = []
    for g in range(ATT_KV_HEADS):
        kb = jnp.concatenate([_dup_half(prev[:, :LANE], g, lo), _dup_half(cur[:, :LANE], g, lo)], axis=0)
        vb = jnp.concatenate([_dup_half(prev[:, LANE:], g, lo), _dup_half(cur[:, LANE:], g, lo)], axis=0)
        bands.append((kb.astype(BF16), vb.astype(BF16)))
    return bands


def _att_tile(n):
    shape = (2 * ATT_BLOCK, ATT_BLOCK)
    row = lax.broadcasted_iota(jnp.int32, shape, 0)
    i = row & (ATT_BLOCK - 1)
    s = lax.broadcasted_iota(jnp.int32, shape, 1)
    upper = s > i
    dist = ((i - s) & (ATT_BLOCK - 1)).astype(F32)
    dead = upper & (n == 0)
    return upper, dist, dead, row[:, 0:1] < ATT_BLOCK


def _stack_pair(t, lo):
    return jnp.concatenate([jnp.where(lo, t, 0.0), jnp.where(lo, 0.0, t)], axis=0).astype(BF16)


def _att_exp(qs, kb, s_ref, j, tile):
    upper, dist, dead, first = tile
    s2 = _dot(qs, kb, _NT)
    slope = jnp.where(first, ATT_SLOPES[2 * j], ATT_SLOPES[2 * j + 1])
    sink = jnp.where(first, s_ref[0:1, 2 * j:2 * j + 1], s_ref[0:1, 2 * j + 1:2 * j + 2])
    s = jnp.where(upper, s2[:, :ATT_BLOCK], s2[:, ATT_BLOCK:]) - slope * dist
    s = jnp.where(dead, NEG, s)
    m = jnp.maximum(jnp.max(s, axis=-1, keepdims=True), sink)
    return jnp.exp(s - m), jnp.exp(sink - m)


def _band_split(t, upper):
    return jnp.concatenate([jnp.where(upper, t, 0.0), jnp.where(upper, 0.0, t)], axis=1)


def _att_fwd(q, kv, sinks8, name):
    q, q0, _ = _window(q)
    kv, kv0, _ = _window(kv)
    qoff, kvoff = q0 // Q_DIM, kv0 // (2 * LANE)
    n_rows = q.shape[0]
    nb = n_rows // ATT_BLOCK

    def body(q_ref, kv_ref, kvp_ref, s_ref, o_ref):
        n = pl.program_id(0)
        bands = _att_band(kv_ref, kvp_ref, n)
        lo = lax.broadcasted_iota(jnp.int32, (ATT_BLOCK, LANE), 1) < ATT_HEAD_DIM
        tile = _att_tile(n)
        ones_b = jnp.ones((2 * ATT_BLOCK, LANE), BF16)
        for j in range(ATT_HEADS // 2):
            kb, vb = bands[2 * j // Q_PER_KV]
            qs = _stack_pair(q_ref[:, j * LANE:(j + 1) * LANE] * ATT_SCALE, lo)
            p, es = _att_exp(qs, kb, s_ref, j, tile)
            pv = _dot(_band_split(p, tile[0]).astype(BF16), jnp.concatenate([vb, ones_b], axis=1))
            out = pv[:, :LANE] / (pv[:, LANE:] + es)
            o_ref[:, j * LANE:(j + 1) * LANE] = jnp.where(lo, out[:ATT_BLOCK], out[ATT_BLOCK:]).astype(BF16)

    return pl.pallas_call(
        body, name=name, grid=(nb,),
        in_specs=[pl.BlockSpec((ATT_BLOCK, Q_DIM), lambda n: (n, qoff)),
                  pl.BlockSpec((ATT_BLOCK, 2 * LANE), lambda n: (n, kvoff)),
                  pl.BlockSpec((ATT_BLOCK, 2 * LANE), lambda n: (jnp.maximum(n - 1, 0), kvoff)),
                  pl.BlockSpec((8, LANE), lambda n: (0, 0))],
        out_specs=pl.BlockSpec((ATT_BLOCK, Q_DIM), lambda n: (n, 0)),
        out_shape=jax.ShapeDtypeStruct((n_rows, Q_DIM), BF16),
        compiler_params=_cparams("parallel"),
    )(q, kv, kv, sinks8)


def _att_bwd(q, kv, sinks8, att, dout, name, into):
    q, q0, _ = _window(q)
    kv, kv0, _ = _window(kv)
    qoff, kvoff = q0 // Q_DIM, kv0 // (2 * LANE)
    n_rows = q.shape[0]
    nb = n_rows // ATT_BLOCK

    extra, extra_specs, aliases, col0, width = _into(into, 6, 0)
    dqoff = col0 // Q_DIM

    def body(*refs):
        q_ref, kv_ref, kvp_ref, s_ref, o_ref, do_ref = refs[:6]
        dq_ref, dkv_ref, acc_ref, carry_ref = refs[-4:]
        n = pl.program_id(0)

        @pl.when(n == 0)
        def _():
            acc_ref[...] = jnp.zeros_like(acc_ref)
            carry_ref[...] = jnp.zeros_like(carry_ref)

        @pl.when(n == nb)
        def _():
            dkv_ref[...] = carry_ref[...].astype(BF16)

        @pl.when(n < nb)
        def _():
            bands = _att_band(kv_ref, kvp_ref, n)
            lo = lax.broadcasted_iota(jnp.int32, (ATT_BLOCK, LANE), 1) < ATT_HEAD_DIM
            lane1 = lax.broadcasted_iota(jnp.int32, (1, LANE), 1)
            tile = _att_tile(n)
            upper, first = tile[0], tile[3]
            ones_b = jnp.ones((ATT_BLOCK, LANE), BF16)
            ones2_b = jnp.ones((2 * LANE, LANE), BF16)
            dk_acc = [jnp.zeros((2 * ATT_BLOCK, LANE), F32) for _ in range(ATT_KV_HEADS)]
            dv_acc = [jnp.zeros((2 * ATT_BLOCK, LANE), F32) for _ in range(ATT_KV_HEADS)]
            dsink = jnp.zeros((1, LANE), F32)
            for j in range(ATT_HEADS // 2):
                g = 2 * j // Q_PER_KV
                kb, vb = bands[g]
                qs = _stack_pair(q_ref[:, j * LANE:(j + 1) * LANE] * ATT_SCALE, lo)
                dop = do_ref[:, j * LANE:(j + 1) * LANE].astype(F32)
                dos = _stack_pair(dop, lo)
                pu, es = _att_exp(qs, kb, s_ref, j, tile)
                inv = 1.0 / (_dot(pu.astype(BF16), ones_b) + es)
                p = pu * inv
                od = dop * o_ref[:, j * LANE:(j + 1) * LANE].astype(F32)
                od = jnp.concatenate([jnp.where(lo, od, 0.0), jnp.where(lo, 0.0, od)], axis=0)
                od_hi = od.astype(BF16)
                delta = _dot(jnp.concatenate([od_hi, (od - od_hi.astype(F32)).astype(BF16)], axis=1), ones2_b)
                dp2 = _dot(dos, vb, _NT)
                dp = jnp.where(upper, dp2[:, :ATT_BLOCK], dp2[:, ATT_BLOCK:])
                ds2 = _band_split(p * (dp - delta), upper)
                psd = jnp.sum(es * inv * delta, axis=0, keepdims=True)
                psd0 = jnp.sum(jnp.where(first, es * inv * delta, 0.0), axis=0, keepdims=True)
                dsink = jnp.where(lane1 == 2 * j, -psd0, jnp.where(lane1 == 2 * j + 1, psd0 - psd, dsink))
                dq = _dot(ds2.astype(BF16), kb) * ATT_SCALE
                dq_ref[:, j * LANE:(j + 1) * LANE] = jnp.where(lo, dq[:ATT_BLOCK], dq[ATT_BLOCK:]).astype(BF16)
                dk_acc[g] = dk_acc[g] + _dot(ds2.T.astype(BF16), qs)
                dv_acc[g] = dv_acc[g] + _dot(_band_split(p, upper).T.astype(BF16), dos)
            acc_ref[0:1, :] += dsink
            lo2 = lax.broadcasted_iota(jnp.int32, (2 * ATT_BLOCK, LANE), 1) < ATT_HEAD_DIM
            folded = []
            for acc in (dk_acc, dv_acc):
                t0 = acc[0] + pltpu.roll(acc[0], ATT_HEAD_DIM, axis=1)
                t1 = acc[1] + pltpu.roll(acc[1], ATT_HEAD_DIM, axis=1)
                folded.append(jnp.where(lo2, t0, t1))
            band = jnp.concatenate(folded, axis=1)
            dkv_ref[...] = (carry_ref[...] + band[:ATT_BLOCK]).astype(BF16)
            carry_ref[...] = band[ATT_BLOCK:]

    def qmap(n):
        return (jnp.minimum(n, nb - 1), 0)

    return pl.pallas_call(
        body, name=name, grid=(nb + 1,),
        in_specs=[pl.BlockSpec((ATT_BLOCK, Q_DIM), lambda n: (jnp.minimum(n, nb - 1), qoff)),
                  pl.BlockSpec((ATT_BLOCK, 2 * LANE), lambda n: (jnp.minimum(n, nb - 1), kvoff)),
                  pl.BlockSpec((ATT_BLOCK, 2 * LANE),
                               lambda n: (jnp.maximum(jnp.minimum(n, nb - 1) - 1, 0), kvoff)),
                  pl.BlockSpec((8, LANE), lambda n: (0, 0)),
                  pl.BlockSpec((ATT_BLOCK, Q_DIM), qmap),
                  pl.BlockSpec((ATT_BLOCK, Q_DIM), qmap)] + extra_specs,
        out_specs=(pl.BlockSpec((ATT_BLOCK, Q_DIM), lambda n: (jnp.minimum(n, nb - 1), dqoff)),
                   pl.BlockSpec((ATT_BLOCK, 2 * LANE), lambda n: (jnp.maximum(n - 1, 0), 0)),
                   pl.BlockSpec((8, LANE), lambda n: (0, 0))),
        out_shape=(jax.ShapeDtypeStruct((n_rows, width), BF16), jax.ShapeDtypeStruct((n_rows, 2 * LANE), BF16),
                   jax.ShapeDtypeStruct((8, LANE), F32)),
        input_output_aliases=aliases,
        scratch_shapes=[pltpu.VMEM((ATT_BLOCK, 2 * LANE), F32)],
        compiler_params=_cparams("arbitrary"),
    )(q, kv, kv, sinks8, att, dout, *extra)


HEADS_PER_GROUP = SSD_HEADS // SSD_GROUPS
PAIRS_PER_GROUP = HEADS_PER_GROUP // 2
T = SSD_CHUNK


def _ssd_scalars(dtr_ref, par_ref):
    dt = _softplus(dtr_ref[...] + par_ref[0:1, :])
    a = -jnp.exp(par_ref[1:2, :])
    ri = lax.broadcasted_iota(jnp.int32, (T, T), 0)
    ci = lax.broadcasted_iota(jnp.int32, (T, T), 1)
    tril = (ri >= ci).astype(F32)
    cs = _dot_hi(tril, dt * a)
    cst = cs.T
    return dt, a, cs, cst, ri, ci


def _ssd_stacked_masks():
    row = lax.broadcasted_iota(jnp.int32, (2 * T, T), 0)
    t = row & (T - 1)
    s = lax.broadcasted_iota(jnp.int32, (2 * T, T), 1)
    return t >= s, s >= t, row[:, 0:1] < T


def _col_s(arr, k0):
    return jnp.concatenate([arr[:, k0:k0 + 1], arr[:, k0 + 1:k0 + 2]], axis=0)


def _row_s(arr_t, k0, first):
    return jnp.where(first, arr_t[k0:k0 + 1, :], arr_t[k0 + 1:k0 + 2, :])


def _lane_pick(lo, arr, k0):
    return jnp.where(lo, arr[:, k0:k0 + 1], arr[:, k0 + 1:k0 + 2])


def _ssd_fwd_stacked(xs, bm, cm, dtr, par, name):
    dtr, dt0, _ = _window(dtr)
    dtoff = dt0 // LANE
    n_rows = xs.shape[0]
    nc = n_rows // T
    gw = PAIRS_PER_GROUP * LANE

    def body(x_ref, b_ref, c_ref, dtr_ref, par_ref, y_ref, hs_ref, h_ref):
        @pl.when(pl.program_id(1) == 0)
        def _():
            h_ref[...] = jnp.zeros_like(h_ref)

        dt, a, cs, cst, _, _ = _ssd_scalars(dtr_ref, par_ref)
        tri_s, _, first = _ssd_stacked_masks()
        lo = lax.broadcasted_iota(jnp.int32, (T, LANE), 1) < SSD_CHUNK // 2
        ecs = jnp.exp(cs)
        dect = jnp.exp(cst[:, T - 1:T] - cst)
        etot = jnp.exp(cs[T - 1:T, :])
        bg = b_ref[...]
        cg = c_ref[...]
        cb = _dot(cg.astype(BF16), bg.astype(BF16), _NT)
        cb_s = jnp.concatenate([cb, cb], axis=0)
        cg_s = jnp.concatenate([cg, cg], axis=0)
        bgt_s = jnp.concatenate([bg.T, bg.T], axis=0)
        for j in range(PAIRS_PER_GROUP):
            k0, k1 = 2 * j, 2 * j + 1
            xp = x_ref[:, j * LANE:(j + 1) * LANE]
            hp = h_ref[j]
            hs_ref[0, 0, j] = hp
            rhs = jnp.concatenate([(xp * _lane_pick(lo, dt, k0)).astype(BF16), hp.astype(BF16)], axis=0)
            lm_s = jnp.exp(jnp.where(tri_s, _col_s(cs, k0) - _row_s(cst, k0, first), NEG))
            lhs = jnp.concatenate([lm_s * cb_s, cg_s * _col_s(ecs, k0)], axis=1).astype(BF16)
            y_s = _dot(lhs, rhs)
            s_s = _dot((bgt_s * _row_s(dect, k0, first)).astype(BF16), rhs[:T])
            dsk = jnp.where(lo[0:1, :], par_ref[2:3, k0:k0 + 1], par_ref[2:3, k1:k1 + 1])
            y_ref[:, j * LANE:(j + 1) * LANE] = jnp.where(lo, y_s[:T], y_s[T:]) + dsk * xp
            et = jnp.where(lo[0:1, :], etot[:, k0:k0 + 1], etot[:, k1:k1 + 1])
            h_ref[j] = hp * et + jnp.where(lo, s_s[:T], s_s[T:])

    return pl.pallas_call(
        body, name=name, grid=(SSD_GROUPS, nc),
        in_specs=[pl.BlockSpec((T, gw), lambda g, c: (c, g)),
                  pl.BlockSpec((T, SSD_STATE), lambda g, c: (c, g)),
                  pl.BlockSpec((T, SSD_STATE), lambda g, c: (c, g)),
                  pl.BlockSpec((T, LANE), lambda g, c: (c, g + dtoff)),
                  pl.BlockSpec((8, LANE), lambda g, c: (0, g))],
        out_specs=(pl.BlockSpec((T, gw), lambda g, c: (c, g)),
                   pl.BlockSpec((1, 1, PAIRS_PER_GROUP, SSD_STATE, LANE), lambda g, c: (g, c, 0, 0, 0))),
        out_shape=(jax.ShapeDtypeStruct((n_rows, SSD_D_INNER), F32),
                   jax.ShapeDtypeStruct((SSD_GROUPS, nc, PAIRS_PER_GROUP, SSD_STATE, LANE), F32)),
        scratch_shapes=[pltpu.VMEM((PAIRS_PER_GROUP, SSD_STATE, LANE), F32)],
        compiler_params=_cparams("parallel", "arbitrary"),
    )(xs, bm, cm, dtr, par)


def _ssd_bwd_stacked(xs, bm, cm, dtr, par, hs, dy, name, into):
    dtr, dt0, _ = _window(dtr)
    dtoff = dt0 // LANE
    n_rows = xs.shape[0]
    nc = n_rows // T
    gw = PAIRS_PER_GROUP * LANE

    extra, extra_specs, aliases, col0, width = _into(into, 7, 3)
    ddoff = col0 // LANE

    def body(*refs):
        x_ref, b_ref, c_ref, dtr_ref, par_ref, hs_ref, dy_ref = refs[:7]
        dx_ref, db_ref, dc_ref, ddtr_ref, acc_ref, dh_ref = refs[-6:]
        @pl.when(pl.program_id(1) == 0)
        def _():
            dh_ref[...] = jnp.zeros_like(dh_ref)
            acc_ref[...] = jnp.zeros_like(acc_ref)

        dt, a, cs, cst, ri, ci = _ssd_scalars(dtr_ref, par_ref)
        tri_s, trit_s, first = _ssd_stacked_masks()
        lane = lax.broadcasted_iota(jnp.int32, (T, LANE), 1)
        lo = lane < SSD_CHUNK // 2
        lane1 = lane[0:1, :]
        ecs = jnp.exp(cs)
        ecst = jnp.exp(cst)
        dec = jnp.exp(cs[T - 1:T, :] - cs)
        etot = jnp.exp(cs[T - 1:T, :])
        bg = b_ref[...]
        cg = c_ref[...]
        bg_b = bg.astype(BF16)
        cg_b = cg.astype(BF16)
        cb = _dot(cg_b, bg_b, _NT)
        cbt = _dot(bg_b, cg_b, _NT)
        cb_s = jnp.concatenate([cb, cb], axis=0)
        cbt_s = jnp.concatenate([cbt, cbt], axis=0)
        bg_s = jnp.concatenate([bg, bg], axis=0)
        cg_s = jnp.concatenate([cg, cg], axis=0)
        cgt_s = jnp.concatenate([cg.T, cg.T], axis=0)
        dbg = jnp.zeros((T, SSD_STATE), F32)
        dcg = jnp.zeros((T, SSD_STATE), F32)
        dcs_acc = jnp.zeros((T, LANE), F32)
        ddt_acc = jnp.zeros((T, LANE), F32)
        dsk_acc = jnp.zeros((1, LANE), F32)
        last_row = lax.broadcasted_iota(jnp.int32, (T, 1), 0) == T - 1
        for j in range(PAIRS_PER_GROUP):
            k0, k1 = 2 * j, 2 * j + 1
            xp = x_ref[:, j * LANE:(j + 1) * LANE]
            dtl = _lane_pick(lo, dt, k0)
            xdt = xp * dtl
            hp = hs_ref[0, 0, j]
            dhn = dh_ref[j]
            dyp = dy_ref[:, j * LANE:(j + 1) * LANE]
            xdt_b, hp_b, dhn_b, dyp_b = (v.astype(BF16) for v in (xdt, hp, dhn, dyp))
            cs_c, cs_r = _col_s(cs, k0), _row_s(cst, k0, first)
            lm_s = jnp.exp(jnp.where(tri_s, cs_c - cs_r, NEG))
            lmt_s = jnp.exp(jnp.where(trit_s, cs_r - cs_c, NEG))
            dec_c, ecs_c = _col_s(dec, k0), _col_s(ecs, k0)
            r1 = _dot(_stack_pair(dyp, lo), jnp.concatenate([xdt_b, hp_b], axis=0), _NT)
            r2 = _dot(_stack_pair(xdt, lo), jnp.concatenate([dyp_b, dhn_b], axis=0), _NT)
            dm_s, dyh_s = r1[:, :T], r1[:, T:]
            dmt_s, xdh_s = r2[:, :T], r2[:, T:]
            mm_s = lm_s * cb_s
            mmt_s = lmt_s * cbt_s
            bdec_s = bg_s * dec_c
            cexp_s = cg_s * ecs_c
            dx_s = _dot(jnp.concatenate([mmt_s, bdec_s], axis=1).astype(BF16),
                        jnp.concatenate([dyp_b, dhn_b], axis=0))
            dxdt = jnp.where(lo, dx_s[:T], dx_s[T:])
            dc_s = _dot((dm_s * lm_s).astype(BF16), bg_b) + dyh_s * ecs_c
            db_s = _dot((dmt_s * lmt_s).astype(BF16), cg_b) + xdh_s * dec_c
            dcg = dcg + dc_s[:T] + dc_s[T:]
            dbg = dbg + db_s[:T] + db_s[T:]
            dh_s = _dot((cgt_s * _row_s(ecst, k0, first)).astype(BF16), dyp_b)
            et = jnp.where(lo[0:1, :], etot[:, k0:k0 + 1], etot[:, k1:k1 + 1])
            dh_ref[j] = dhn * et + jnp.where(lo, dh_s[:T], dh_s[T:])
            e4 = jnp.sum(bdec_s * xdh_s, axis=-1, keepdims=True)
            dcs_s = (jnp.sum(dm_s * mm_s, axis=-1, keepdims=True) - jnp.sum(dmt_s * mmt_s, axis=-1, keepdims=True)
                     + jnp.sum(cexp_s * dyh_s, axis=-1, keepdims=True) - e4)
            hd = hp * dhn
            tsum0 = jnp.sum(e4[:T]) + etot[:, k0:k0 + 1] * jnp.sum(jnp.where(lo, hd, 0.0))
            tsum1 = jnp.sum(e4[T:]) + etot[:, k1:k1 + 1] * jnp.sum(jnp.where(lo, 0.0, hd))
            dcs0 = dcs_s[:T] + jnp.where(last_row, tsum0, 0.0)
            dcs1 = dcs_s[T:] + jnp.where(last_row, tsum1, 0.0)
            dcs_acc = jnp.where(lane == k0, dcs0, jnp.where(lane == k1, dcs1, dcs_acc))
            prod = dxdt * xp
            ddt_lo = jnp.sum(jnp.where(lo, prod, 0.0), axis=-1, keepdims=True)
            ddt_hi = jnp.sum(jnp.where(lo, 0.0, prod), axis=-1, keepdims=True)
            ddt_acc = jnp.where(lane == k0, ddt_lo, jnp.where(lane == k1, ddt_hi, ddt_acc))
            dyx = dyp * xp
            dsk_acc = jnp.where(lane1 == k0, jnp.sum(jnp.where(lo, dyx, 0.0)),
                                jnp.where(lane1 == k1, jnp.sum(jnp.where(lo, 0.0, dyx)), dsk_acc))
            dsk = jnp.where(lo[0:1, :], par_ref[2:3, k0:k0 + 1], par_ref[2:3, k1:k1 + 1])
            dx_ref[:, j * LANE:(j + 1) * LANE] = dxdt * dtl + dsk * dyp
        db_ref[...] = dbg
        dc_ref[...] = dcg
        triu = (ci >= ri).astype(F32)
        dda = _dot_hi(triu, dcs_acc)
        ddt = ddt_acc + dda * a
        ddtr = ddt * _sig(dtr_ref[...] + par_ref[0:1, :])
        ddtr_ref[...] = ddtr.astype(BF16)
        acc_ref[0:1, :] += jnp.sum(ddtr, axis=0, keepdims=True)
        acc_ref[1:2, :] += jnp.sum(dda * dt, axis=0, keepdims=True) * a
        acc_ref[2:3, :] += dsk_acc

    def rev(g, c):
        return (nc - 1 - c, g)

    return pl.pallas_call(
        body, name=name, grid=(SSD_GROUPS, nc),
        in_specs=[pl.BlockSpec((T, gw), rev),
                  pl.BlockSpec((T, SSD_STATE), rev),
                  pl.BlockSpec((T, SSD_STATE), rev),
                  pl.BlockSpec((T, LANE), lambda g, c: (nc - 1 - c, g + dtoff)),
                  pl.BlockSpec((8, LANE), lambda g, c: (0, g)),
                  pl.BlockSpec((1, 1, PAIRS_PER_GROUP, SSD_STATE, LANE), lambda g, c: (g, nc - 1 - c, 0, 0, 0)),
                  pl.BlockSpec((T, gw), rev)] + extra_specs,
        out_specs=(pl.BlockSpec((T, gw), rev),
                   pl.BlockSpec((T, SSD_STATE), rev),
                   pl.BlockSpec((T, SSD_STATE), rev),
                   pl.BlockSpec((T, LANE), lambda g, c: (nc - 1 - c, g + ddoff)),
                   pl.BlockSpec((8, LANE), lambda g, c: (0, g))),
        out_shape=(jax.ShapeDtypeStruct((n_rows, SSD_D_INNER), F32),
                   jax.ShapeDtypeStruct((n_rows, BC_DIM), F32),
                   jax.ShapeDtypeStruct((n_rows, BC_DIM), F32),
                   jax.ShapeDtypeStruct((n_rows, width), BF16),
                   jax.ShapeDtypeStruct((8, DT_PAD), F32)),
        input_output_aliases=aliases,
        scratch_shapes=[pltpu.VMEM((PAIRS_PER_GROUP, SSD_STATE, LANE), F32)],
        compiler_params=_cparams("parallel", "arbitrary"),
    )(xs, bm, cm, dtr, par, hs, dy, *extra)


def _cumsum_mm(mat, x):
    hi = x.astype(BF16)
    r = x - hi.astype(F32)
    mid = r.astype(BF16)
    lo = (r - mid.astype(F32)).astype(BF16)
    w = x.shape[1]
    out = _dot(mat, jnp.concatenate([hi, mid, lo], axis=1))
    return out[:, :w] + out[:, w:2 * w] + out[:, 2 * w:]


def _ssd_prep(dtr_ref, par_ref):
    dt = _softplus(dtr_ref[...] + par_ref[0:1, :])
    a = -jnp.exp(par_ref[1:2, :])
    ri = lax.broadcasted_iota(jnp.int32, (T, T), 0)
    ci = lax.broadcasted_iota(jnp.int32, (T, T), 1)
    cs = _cumsum_mm((ri >= ci).astype(BF16), dt * a)
    lo = lax.broadcasted_iota(jnp.int32, (T, LANE), 1) < SSD_CHUNK // 2

    def expand(arr):
        rows = arr.shape[0]
        return jnp.concatenate([jnp.where(lo[:rows], arr[:, 2 * j:2 * j + 1], arr[:, 2 * j + 1:2 * j + 2])
                                for j in range(PAIRS_PER_GROUP)], axis=1)

    tot = cs[T - 1:T, :]
    return {"dt": dt, "a": a, "cs": cs, "cst": cs.T, "lo": lo, "ri": ri, "ci": ci, "expand": expand,
            "dt_x": expand(dt), "ecs_x": expand(jnp.exp(cs)), "dec_x": expand(jnp.exp(tot - cs)),
            "et_x": expand(jnp.exp(tot)), "etot": jnp.exp(tot), "dsk_x": expand(par_ref[2:3, :])}


def _wide_masks():
    r = lax.broadcasted_iota(jnp.int32, (T, 2 * T), 0)
    l = lax.broadcasted_iota(jnp.int32, (T, 2 * T), 1)
    s = l & (T - 1)
    return r >= s, s >= r, l < T


def _wide_cs(q, k0, even):
    cs, cst = q["cs"], q["cst"]
    col = jnp.where(even, cs[:, k0:k0 + 1], cs[:, k0 + 1:k0 + 2])
    row = jnp.concatenate([cst[k0:k0 + 1, :], cst[k0 + 1:k0 + 2, :]], axis=1)
    return col, row


def _ssd_fwd(xs, bm, cm, dtr, par, name):
    dtr, dt0, _ = _window(dtr)
    dtoff = dt0 // LANE
    n_rows = xs.shape[0]
    nc = n_rows // T
    gw = PAIRS_PER_GROUP * LANE

    def body(x_ref, b_ref, c_ref, dtr_ref, par_ref, y_ref, hs_ref, h_ref):
        @pl.when(pl.program_id(1) == 0)
        def _():
            h_ref[...] = jnp.zeros_like(h_ref)

        q = _ssd_prep(dtr_ref, par_ref)
        lo = q["lo"]
        tri_w, _, even = _wide_masks()
        bg_b = b_ref[...].astype(BF16)
        cg_b = c_ref[...].astype(BF16)
        xv = x_ref[...]
        xdt = xv * q["dt_x"]
        h = h_ref[...]
        hs_ref[0, 0] = h
        yo = q["ecs_x"] * _dot(cg_b, h.astype(BF16))
        h_ref[...] = h * q["et_x"] + _dot(b_ref[...].T.astype(BF16), (xdt * q["dec_x"]).astype(BF16))
        cb = _dot(cg_b, bg_b, _NT)
        cb_w = jnp.concatenate([cb, cb], axis=1)
        for j in range(PAIRS_PER_GROUP):
            col, row = _wide_cs(q, 2 * j, even)
            m_w = (jnp.exp(jnp.where(tri_w, col - row, NEG)) * cb_w).astype(BF16)
            sl = slice(j * LANE, (j + 1) * LANE)
            y_ref[:, sl] = (_dot(m_w, _stack_pair(xdt[:, sl], lo)) + yo[:, sl] + q["dsk_x"][:, sl] * xv[:, sl])

    return pl.pallas_call(
        body, name=name, grid=(SSD_GROUPS, nc),
        in_specs=[pl.BlockSpec((T, gw), lambda g, c: (c, g)),
                  pl.BlockSpec((T, SSD_STATE), lambda g, c: (c, g)),
                  pl.BlockSpec((T, SSD_STATE), lambda g, c: (c, g)),
                  pl.BlockSpec((T, LANE), lambda g, c: (c, g + dtoff)),
                  pl.BlockSpec((8, LANE), lambda g, c: (0, g))],
        out_specs=(pl.BlockSpec((T, gw), lambda g, c: (c, g)),
                   pl.BlockSpec((1, 1, SSD_STATE, gw), lambda g, c: (g, c, 0, 0))),
        out_shape=(jax.ShapeDtypeStruct((n_rows, SSD_D_INNER), F32),
                   jax.ShapeDtypeStruct((SSD_GROUPS, nc, SSD_STATE, gw), F32)),
        scratch_shapes=[pltpu.VMEM((SSD_STATE, gw), F32)],
        compiler_params=_cparams("parallel", "arbitrary"),
    )(xs, bm, cm, dtr, par)


def _ssd_bwd(xs, bm, cm, dtr, par, hs, dy, name, into):
    dtr, dt0, _ = _window(dtr)
    dtoff = dt0 // LANE
    n_rows = xs.shape[0]
    nc = n_rows // T
    gw = PAIRS_PER_GROUP * LANE
    extra, extra_specs, aliases, col0, width = _into(into, 7, 3)
    ddoff = col0 // LANE

    def body(*refs):
        x_ref, b_ref, c_ref, dtr_ref, par_ref, hs_ref, dy_ref = refs[:7]
        dx_ref, db_ref, dc_ref, ddtr_ref, acc_ref, dh_ref = refs[-6:]

        @pl.when(pl.program_id(1) == 0)
        def _():
            dh_ref[...] = jnp.zeros_like(dh_ref)
            acc_ref[...] = jnp.zeros_like(acc_ref)

        q = _ssd_prep(dtr_ref, par_ref)
        lo, dt, a = q["lo"], q["dt"], q["a"]
        tri_w, trit_w, even = _wide_masks()
        lane = lax.broadcasted_iota(jnp.int32, (T, LANE), 1)
        lane1 = lane[0:1, :]
        last_row = lax.broadcasted_iota(jnp.int32, (T, 1), 0) == T - 1
        bg_b = b_ref[...].astype(BF16)
        cg_b = c_ref[...].astype(BF16)
        xv = x_ref[...]
        dyv = dy_ref[...]
        xdt = xv * q["dt_x"]
        h = hs_ref[0, 0]
        dhn = dh_ref[...]
        h_b, dhn_b = h.astype(BF16), dhn.astype(BF16)
        yo = q["ecs_x"] * _dot(cg_b, h_b)
        bdh = q["dec_x"] * _dot(bg_b, dhn_b)
        dye = (dyv * q["ecs_x"]).astype(BF16)
        xd = (xdt * q["dec_x"]).astype(BF16)
        dcg = _dot(dye, h_b, _NT)
        dbg = _dot(xd, dhn_b, _NT)
        dh_ref[...] = dhn * q["et_x"] + _dot(c_ref[...].T.astype(BF16), dye)
        e4_all = xdt * bdh
        f_all = dyv * yo - e4_all
        tot_row = jnp.sum(e4_all, axis=0, keepdims=True) + q["et_x"] * jnp.sum(h * dhn, axis=0, keepdims=True)
        dsk_row = jnp.sum(dyv * xv, axis=0, keepdims=True)
        cb = _dot(cg_b, bg_b, _NT)
        cbt = _dot(bg_b, cg_b, _NT)
        cb_w = jnp.concatenate([cb, cb], axis=1)
        cbt_w = jnp.concatenate([cbt, cbt], axis=1)
        dcb = jnp.zeros((T, T), F32)
        dcbt = jnp.zeros((T, T), F32)
        dcs_acc = jnp.zeros((T, LANE), F32)
        ddt_acc = jnp.zeros((T, LANE), F32)
        dsk_acc = jnp.zeros((1, LANE), F32)
        tot_acc = jnp.zeros((1, LANE), F32)
        ind_r = lax.broadcasted_iota(jnp.int32, (2 * T, LANE), 0)
        ind_l = lax.broadcasted_iota(jnp.int32, (2 * T, LANE), 1)

        def halves(t):
            return (jnp.sum(jnp.where(lo[0:1], t, 0.0), axis=-1, keepdims=True),
                    jnp.sum(jnp.where(lo[0:1], 0.0, t), axis=-1, keepdims=True))

        def split2(t):
            hi = t.astype(BF16)
            return jnp.concatenate([hi, (t - hi.astype(F32)).astype(BF16)], axis=1)

        for j in range(PAIRS_PER_GROUP):
            k0, k1 = 2 * j, 2 * j + 1
            sl = slice(j * LANE, (j + 1) * LANE)
            col, row = _wide_cs(q, k0, even)
            lm_w = jnp.exp(jnp.where(tri_w, col - row, NEG))
            lmt_w = jnp.exp(jnp.where(trit_w, row - col, NEG))
            dyp, xp = dyv[:, sl], xdt[:, sl]
            dym, xm = _stack_pair(dyp, lo), _stack_pair(xp, lo)
            dm_w = _dot(dyp.astype(BF16), xm, _NT)
            dmt_w = _dot(xp.astype(BF16), dym, _NT)
            mm_w = lm_w * cb_w
            mmt_w = lmt_w * cbt_w
            dxdt = _dot(mmt_w.astype(BF16), dym) + bdh[:, sl]
            g1 = dm_w * lm_w
            g2 = dmt_w * lmt_w
            dcb = dcb + g1[:, :T] + g1[:, T:]
            dcbt = dcbt + g2[:, :T] + g2[:, T:]
            ind_w = jnp.where(ind_l == jnp.where(ind_r < T, k0, k1), 1.0, 0.0).astype(BF16)
            ind_p = jnp.where(ind_l[:T] == jnp.where(ind_r[:T] < SSD_CHUNK // 2, k0, k1), 1.0, 0.0).astype(BF16)
            dcs_acc = dcs_acc + _dot(
                jnp.concatenate([split2(dm_w * mm_w - dmt_w * mmt_w), split2(f_all[:, sl])], axis=1),
                jnp.concatenate([ind_w, ind_w, ind_p, ind_p], axis=0))
            ddt_acc = ddt_acc + _dot(split2(dxdt * xv[:, sl]), jnp.concatenate([ind_p, ind_p], axis=0))
            tot2 = halves(tot_row[:, sl])
            tot_acc = jnp.where(lane1 == k0, tot2[0], jnp.where(lane1 == k1, tot2[1], tot_acc))
            dsk2 = halves(dsk_row[:, sl])
            dsk_acc = jnp.where(lane1 == k0, dsk2[0], jnp.where(lane1 == k1, dsk2[1], dsk_acc))
            dx_ref[:, sl] = dxdt * q["dt_x"][:, sl] + q["dsk_x"][:, sl] * dyp
        dcs_acc = dcs_acc + jnp.where(last_row, tot_acc, 0.0)
        dc_ref[...] = dcg + _dot(dcb.astype(BF16), bg_b)
        db_ref[...] = dbg + _dot(dcbt.astype(BF16), cg_b)
        dda = _cumsum_mm((q["ci"] >= q["ri"]).astype(BF16), dcs_acc)
        ddt = ddt_acc + dda * a
        ddtr = ddt * _sig(dtr_ref[...] + par_ref[0:1, :])
        ddtr_ref[...] = ddtr.astype(BF16)
        acc_ref[0:1, :] += jnp.sum(ddtr, axis=0, keepdims=True)
        acc_ref[1:2, :] += jnp.sum(dda * dt, axis=0, keepdims=True) * a
        acc_ref[2:3, :] += dsk_acc

    def rev(g, c):
        return (nc - 1 - c, g)

    return pl.pallas_call(
        body, name=name, grid=(SSD_GROUPS, nc),
        in_specs=[pl.BlockSpec((T, gw), rev),
                  pl.BlockSpec((T, SSD_STATE), rev),
                  pl.BlockSpec((T, SSD_STATE), rev),
                  pl.BlockSpec((T, LANE), lambda g, c: (nc - 1 - c, g + dtoff)),
                  pl.BlockSpec((8, LANE), lambda g, c: (0, g)),
                  pl.BlockSpec((1, 1, SSD_STATE, gw), lambda g, c: (g, nc - 1 - c, 0, 0)),
                  pl.BlockSpec((T, gw), rev)] + extra_specs,
        out_specs=(pl.BlockSpec((T, gw), rev),
                   pl.BlockSpec((T, SSD_STATE), rev),
                   pl.BlockSpec((T, SSD_STATE), rev),
                   pl.BlockSpec((T, LANE), lambda g, c: (nc - 1 - c, g + ddoff)),
                   pl.BlockSpec((8, LANE), lambda g, c: (0, g))),
        out_shape=(jax.ShapeDtypeStruct((n_rows, SSD_D_INNER), F32),
                   jax.ShapeDtypeStruct((n_rows, BC_DIM), F32),
                   jax.ShapeDtypeStruct((n_rows, BC_DIM), F32),
                   jax.ShapeDtypeStruct((n_rows, width), BF16),
                   jax.ShapeDtypeStruct((8, DT_PAD), F32)),
        input_output_aliases=aliases,
        scratch_shapes=[pltpu.VMEM((SSD_STATE, gw), F32)],
        compiler_params=_cparams("parallel", "arbitrary"),
    )(xs, bm, cm, dtr, par, hs, dy, *extra)


ADAM_ROWS = 256


def _adamw(lands, w, m, v, name):
    na = len(lands)
    n_slots, r, wd = lands[0].shape
    tr = r if r <= 2 * ADAM_ROWS else ADAM_ROWS
    nj = r // tr
    bc1 = 1.0 - ADAM_B1 ** ADAM_STEP
    bc2 = 1.0 - ADAM_B2 ** ADAM_STEP

    def body(*refs):
        l_refs = refs[:na]
        w_ref, m_ref, v_ref, g_ref, d_ref, nm_ref, nv_ref = refs[na:]
        for a in range(na):
            @pl.when(pl.program_id(0) == a)
            def _(l_ref=l_refs[a]):
                g = l_ref[0].astype(F32)
                for s in range(1, n_slots):
                    g = g + l_ref[s].astype(F32)
                mn = ADAM_B1 * m_ref[0] + (1.0 - ADAM_B1) * g
                vn = ADAM_B2 * v_ref[0] + (1.0 - ADAM_B2) * (g * g)
                mh = mn / bc1
                vh = vn / bc2
                g_ref[0] = g
                nm_ref[0] = mn
                nv_ref[0] = vn
                d_ref[0] = -ADAM_LR * (mh / (jnp.sqrt(vh) + ADAM_EPS) + ADAM_WD * w_ref[0])

    def land_spec(a):
        return pl.BlockSpec((n_slots, tr, wd),
                            lambda i, j: (0, jnp.where(i == a, j, jnp.where(i < a, 0, nj - 1)), 0))

    blk = pl.BlockSpec((1, tr, wd), lambda i, j: (i, j, 0))
    shp = jax.ShapeDtypeStruct((na, r, wd), F32)
    return pl.pallas_call(
        body, name=name, grid=(na, nj), in_specs=[land_spec(a) for a in range(na)] + [blk, blk, blk],
        out_specs=(blk, blk, blk, blk), out_shape=(shp, shp, shp, shp),
        compiler_params=_cparams("arbitrary", "arbitrary"),
    )(*lands, w, m, v)


def _mesh_pos():
    return lax.axis_index("x"), lax.axis_index("y"), lax.axis_index("c")


def _peer(pos, k):
    x, y, c = pos
    px = 1 - x if (k >> 2) & 1 else x
    py = 1 - y if (k >> 1) & 1 else y
    pc = 1 - c if k & 1 else c
    return px, py, pc


def _flat(pos):
    return 4 * pos[0] + 2 * pos[1] + pos[2]


HBM_SPEC = pl.BlockSpec(memory_space=pl.ANY)


ROW_SHARDED = ("w_ssd_out", "w_att_out", "w_mix_out", "w_ffn_down")
COL_SHARDED = ("w_in", "w_ffn_gate", "w_ffn_up")
GATHERED = ROW_SHARDED + COL_SHARDED + ("conv_w",)
BIG = ROW_SHARDED + COL_SHARDED


SEM_SPEC = pl.BlockSpec(memory_space=pltpu.SEMAPHORE)
TOKEN = jax.ShapeDtypeStruct((8, LANE), F32)
SPLIT_EFFECT = pltpu.SideEffectType.DATAFLOW_SIDE_EFFECTING
GATHER_ROWS = "gather_rows"
GATHER_SLOT = "gather_slot"
SCATTER_ROWS = "scatter_rows"
SCATTER_SLOT = "scatter_slot"


def _land_shape(kind, src):
    if kind == GATHER_ROWS:
        return (N_DEV * src.shape[0],) + src.shape[1:]
    if kind == GATHER_SLOT:
        return (N_DEV,) + src.shape
    if kind == SCATTER_ROWS:
        return (N_DEV, src.shape[0] // N_DEV) + src.shape[1:]
    return src.shape


def _views(kind, src_ref, land_ref, pos, k):
    me = _flat(pos)
    if kind == GATHER_ROWS:
        r = src_ref.shape[0]
        return src_ref, land_ref.at[pl.ds(pl.multiple_of(me * r, 16), r), :]
    if kind == GATHER_SLOT:
        return src_ref, land_ref.at[me]
    dev = _flat(_peer(pos, k))
    if kind == SCATTER_ROWS:
        r = land_ref.shape[1]
        return src_ref.at[pl.ds(pl.multiple_of(dev * r, 16), r), :], land_ref.at[k]
    return src_ref.at[dev], land_ref.at[k]


def _hbm(x):
    return pltpu.with_memory_space_constraint(x, pltpu.HBM)


def _exchange_start(items, after, name):
    kinds = [k for k, _ in items]
    srcs = [_hbm(s) for _, s in items]
    lands = [_hbm(lax.empty(_land_shape(k, s), s.dtype)) for k, s in items]
    n = len(items)
    n_copy = n * (N_DEV - 1)

    def body(*refs):
        src_refs, land_refs = refs[:n], refs[n:2 * n]
        send_sems, recv_sems = refs[2 * n + 1], refs[2 * n + 2]
        token_ref = refs[4 * n + 3]
        pos = _mesh_pos()
        for i, kind in enumerate(kinds):
            for k in range(1, N_DEV):
                s, d = _views(kind, src_refs[i], land_refs[i], pos, k)
                j = i * (N_DEV - 1) + k - 1
                pltpu.make_async_remote_copy(src_ref=s, dst_ref=d, send_sem=send_sems.at[j], recv_sem=recv_sems.at[j],
                                             device_id=_peer(pos, k), device_id_type=MESH_ID).start()
        token_ref[...] = jnp.zeros_like(token_ref)

    arrs = srcs + lands
    outs = pl.pallas_call(
        body, name=name,
        in_specs=[HBM_SPEC] * (2 * n + 1),
        out_specs=[SEM_SPEC, SEM_SPEC] + [HBM_SPEC] * (2 * n) + [pl.BlockSpec(memory_space=pltpu.VMEM)],
        out_shape=[pltpu.SemaphoreType.DMA((n_copy,)), pltpu.SemaphoreType.DMA((n_copy,))]
        + [pltpu.HBM(a.shape, a.dtype) for a in arrs] + [TOKEN],
        input_output_aliases={i: 2 + i for i in range(2 * n)},
        compiler_params=pltpu.CompilerParams(has_side_effects=SPLIT_EFFECT),
    )(*arrs, after)
    return {"kinds": kinds, "send": outs[0], "recv": outs[1], "arrs": outs[2:2 + 2 * n], "token": outs[-1]}


def _exchange_wait(ex, after, name):
    kinds = ex["kinds"]
    n = len(kinds)

    def body(*refs):
        src_refs, land_refs = refs[:n], refs[n:2 * n]
        send_sems, recv_sems = refs[2 * n], refs[2 * n + 1]
        token_ref = refs[-1]
        pos = _mesh_pos()
        for i, kind in enumerate(kinds):
            for k in range(1, N_DEV):
                s, d = _views(kind, src_refs[i], land_refs[i], pos, k)
                j = i * (N_DEV - 1) + k - 1
                cp = pltpu.make_async_remote_copy(src_ref=s, dst_ref=d, send_sem=send_sems.at[j],
                                                  recv_sem=recv_sems.at[j], device_id=_peer(pos, k),
                                                  device_id_type=MESH_ID)
                cp.wait_send()
                cp.wait_recv()
        token_ref[...] = jnp.zeros_like(token_ref)

    outs = pl.pallas_call(
        body, name=name,
        in_specs=[HBM_SPEC] * (2 * n) + [SEM_SPEC, SEM_SPEC, HBM_SPEC],
        out_specs=[HBM_SPEC] * (2 * n) + [pl.BlockSpec(memory_space=pltpu.VMEM)],
        out_shape=[pltpu.HBM(a.shape, a.dtype) for a in ex["arrs"]] + [TOKEN],
        input_output_aliases={i: i for i in range(2 * n)},
        compiler_params=pltpu.CompilerParams(has_side_effects=SPLIT_EFFECT),
    )(*ex["arrs"], ex["send"], ex["recv"], after)
    lands = [_place_own(k, s, d) for k, s, d in zip(kinds, outs[:n], outs[n:2 * n])]
    return lands, outs[-1]


def _place_own(kind, src, land):
    me = _flat(_mesh_pos())
    zeros = (0,) * (src.ndim - 1)
    if kind == GATHER_ROWS:
        return lax.dynamic_update_slice(land, src, (me * src.shape[0],) + zeros)
    if kind == GATHER_SLOT:
        return lax.dynamic_update_slice(land, src[None], (me,) + (0,) * src.ndim)
    if kind == SCATTER_ROWS:
        r = land.shape[1]
        own = lax.dynamic_slice(src, (me * r,) + zeros, (r,) + src.shape[1:])
    else:
        own = lax.dynamic_index_in_dim(src, me, 0, keepdims=False)
    return lax.dynamic_update_slice(land, own[None], (0,) * land.ndim)


def _all_gather_small(x, name):
    r, w = x.shape

    def body(x_ref, out_ref, send_sems, recv_sems):
        pos = _mesh_pos()
        me = _flat(pos)
        copies = []
        for k in range(1, N_DEV):
            cp = pltpu.make_async_remote_copy(
                src_ref=x_ref, dst_ref=out_ref.at[me], send_sem=send_sems.at[k - 1], recv_sem=recv_sems.at[k - 1],
                device_id=_peer(pos, k), device_id_type=MESH_ID)
            cp.start()
            copies.append(cp)
        out_ref[me] = x_ref[...]
        for cp in copies:
            cp.wait()

    vmem = pl.BlockSpec(memory_space=pltpu.VMEM)
    return pl.pallas_call(
        body, name=name, in_specs=[vmem], out_specs=vmem,
        out_shape=jax.ShapeDtypeStruct((N_DEV, r, w), x.dtype),
        scratch_shapes=[pltpu.SemaphoreType.DMA((N_DEV - 1,)), pltpu.SemaphoreType.DMA((N_DEV - 1,))],
        compiler_params=pltpu.CompilerParams(has_side_effects=True),
    )(x)


def _cols(g, lo, hi):
    c = g.shape[-1]
    parts = []
    for d in range(N_DEV):
        a, b = max(lo, d * c), min(hi, (d + 1) * c)
        if a < b:
            parts.append(g[d, :, a - d * c:b - d * c])
    return parts[0] if len(parts) == 1 else jnp.concatenate(parts, axis=1)


def _col_chunks(g):
    c = g.shape[-1] // N_DEV
    return jnp.stack([g[:, d * c:(d + 1) * c] for d in range(N_DEV)])


IN_PART = ("w_in", "conv_w")
OUT_PART = ROW_SHARDED + ("w_ffn_gate", "w_ffn_up")


def _gather_items(w, names, l):
    items = []
    for n in names:
        blk = w[n][l] if n == "conv_w" else w[n][l].astype(BF16)
        items.append((GATHER_ROWS if n in ROW_SHARDED else GATHER_SLOT, blk))
    return items


def _scatter_items(grads, names):
    return [(SCATTER_ROWS, grads[n]) if n in ROW_SHARDED else (SCATTER_SLOT, _col_chunks(grads[n]))
            for n in names]


SMALL = ("ln_in_g", "ln_in_b", "conv_b", "dt_bias", "a_log", "d_skip", "ssd_norm_w", "att_sinks",
         "ln_mix_g", "ln_mix_b", "ln_ffn_g", "ln_ffn_b")


def _pack_small(vals):
    flat = jnp.concatenate([vals[n].reshape(-1) for n in SMALL])
    n = flat.shape[0]
    rows = -(-n // LANE)
    rows = -(-rows // 8) * 8
    return jnp.pad(flat, (0, rows * LANE - n)).reshape(rows, LANE)


def _unpack_small(buf, shapes):
    flat = buf.reshape(-1)
    off = 0
    out = {}
    for n in SMALL:
        cnt = math.prod(shapes[n])
        out[n] = flat[off:off + cnt].reshape(shapes[n])
        off += cnt
    return out


def _to_group_major(v):
    lead = v.shape[:-1]
    t = v.reshape(lead + (SSD_GROUPS, HEADS_PER_GROUP))
    t = jnp.pad(t, [(0, 0)] * len(lead) + [(0, 0), (0, LANE - HEADS_PER_GROUP)])
    return t.reshape(lead + (DT_PAD,))


def _from_group_major(v):
    lead = v.shape[:-1]
    return v.reshape(lead + (SSD_GROUPS, LANE))[..., :HEADS_PER_GROUP].reshape(lead + (SSD_HEADS,))


def _rows8(v):
    return jnp.pad(v, ((0, 8 - v.shape[0]), (0, 0)))


IN_OFFS = {"q": (0, 1024), "kv": (1024, 1280), "z": (1280, 3328), "xs": (3328, 5376), "b": (5376, 5888),
           "c": (5888, 6400), "dt": (6400, 6432), "gl": (6432, 8480)}
PIECES = ("q", "kv", "z", "xs", "b", "c", "dt", "gl")


CAT = ("z", "xs", "gl", "q", "b", "c", "dt", "kv")
CAT_WIDTH = {"q": 1024, "z": 2048, "xs": 2048, "gl": 2048, "b": 512, "c": 512, "kv": 256, "dt": DT_PAD}
CAT_OFF = {p: sum(CAT_WIDTH[q] for q in CAT[:i]) for i, p in enumerate(CAT)}
CAT_DIM = sum(CAT_WIDTH.values())
MAIN_DIM = CAT_OFF["kv"]


def _cat_w_in(g):
    pieces = {p: _cols(g, lo, hi) for p, (lo, hi) in IN_OFFS.items()}
    pieces["dt"] = _to_group_major(pieces["dt"])
    return jnp.concatenate([pieces[p] for p in CAT], axis=1)


def _uncat_dw_in(dw):
    pieces = {p: dw[:, CAT_OFF[p]:CAT_OFF[p] + CAT_WIDTH[p]] for p in CAT}
    pieces["dt"] = _from_group_major(pieces["dt"])
    return jnp.concatenate([pieces[p] for p in PIECES], axis=1)


def _params_out(W):
    p = {n: W[n] for n in ROW_SHARDED}
    for n in ("w_ffn_gate", "w_ffn_up"):
        p[n] = _cols(W[n], 0, FFN_HIDDEN)
    return p


def _params_in(l, W, sm):
    p = {"w_cat": _cat_w_in(W["w_in"])}
    cw = _cols(W["conv_w"], 0, SSD_D_INNER + 2 * BC_DIM)
    cb = sm["conv_b"][l]
    segs = {"xs": (0, 2048), "b": (2048, 2560), "c": (2560, 3072)}
    p["conv_w8"] = {s: _rows8(cw[:, lo:hi]) for s, (lo, hi) in segs.items()}
    p["conv_b8"] = {s: _rows8(cb[None, lo:hi]) for s, (lo, hi) in segs.items()}
    p["ssd_par"] = _rows8(jnp.stack([_to_group_major(sm["dt_bias"][l]), _to_group_major(sm["a_log"][l]),
                                     _to_group_major(sm["d_skip"][l])]))
    p["norm_w"] = sm["ssd_norm_w"][l]
    p["sinks8"] = _rows8(jnp.pad(sm["att_sinks"][l], (0, LANE - ATT_HEADS))[None])
    for n in ("ln_mix_g", "ln_mix_b", "ln_ffn_g", "ln_ffn_b"):
        p[n] = sm[n][l]
    return p


def _fwd_mixers(h0, p, l, dep=None):
    tag = f"l{l}_"
    a = {"h0": h0}
    proj = _mm(h0, p["w_cat"], "nn", tag + "proj", dep=dep)
    for pc in CAT:
        a[pc] = (proj, CAT_OFF[pc], CAT_WIDTH[pc])
    for s in ("xs", "b", "c"):
        a[s + "c"] = _conv_fwd(a[s], p["conv_w8"][s], p["conv_b8"][s], tag + "conv_" + s)
    a["y"], a["hs"] = _ssd_fwd(a["xsc"], a["bc"], a["cc"], a["dt"], p["ssd_par"], tag + "ssd_fwd")
    a["yn"] = _gnorm_fwd(a["y"], a["z"], p["norm_w"], tag + "gnorm")
    a["att"] = _att_fwd(a["q"], a["kv"], p["sinks8"], tag + "att_fwd")
    return a


def _fwd_out(a, p, l, dep=None):
    tag = f"l{l}_"
    h0 = a["h0"]
    a["ya"] = _mm(a["yn"], p["w_ssd_out"], "nn", tag + "ssd_out", dep=dep)
    a["yb"] = _mm(a["att"], p["w_att_out"], "nn", tag + "att_out", dep=dep)
    a["merged"] = _merge_fwd(a["gl"], a["ya"], a["yb"], tag + "merge")
    a["mix"] = _mm(a["merged"], p["w_mix_out"], "nn", tag + "mix_out")
    a["h1"] = _ln_fwd(h0, a["mix"], p["ln_mix_g"], p["ln_mix_b"], ALPHA, tag + "ln_mix")
    a["fg"], a["fu"], a["act"] = _ffn_in(a["h1"], p["w_ffn_gate"], p["w_ffn_up"], tag + "ffn_in")
    a["ffn"] = _mm(a["act"], p["w_ffn_down"], "nn", tag + "ffn_down")
    a["h2"] = _ln_fwd(a["h1"], a["ffn"], p["ln_ffn_g"], p["ln_ffn_b"], ALPHA, tag + "ln_ffn")
    return a


def _dw(x, dy, name, dep=None):
    return _mm(x, dy, "tn", name, out_dtype=BF16, dep=dep)


def _bwd_out(a, p, dh2, l, dep=None):
    tag = f"l{l}_b_"
    gw, gs = {}, {}
    du2, acc = _ln_bwd(a["h1"], a["ffn"], p["ln_ffn_g"], dh2, ALPHA, tag + "ln_ffn")
    gs["ln_ffn_g"], gs["ln_ffn_b"] = acc[0], acc[1]
    gw["w_ffn_down"] = _dw(a["act"], du2, tag + "dw_down", dep=dep)
    dfg, dfu = _ffn_dact(du2, p["w_ffn_down"], a["fg"], a["fu"], tag + "ffn_dact", dep=dep)
    gw["w_ffn_gate"] = _dw(a["h1"], dfg, tag + "dw_gate")
    gw["w_ffn_up"] = _dw(a["h1"], dfu, tag + "dw_up")
    dh1 = _mm(dfg, p["w_ffn_gate"], "nt", tag + "dh1_gate", add=du2, add_scale=ALPHA)
    dh1 = _mm(dfu, p["w_ffn_up"], "nt", tag + "dh1_up", add=dh1)
    du1, acc = _ln_bwd(a["h0"], a["mix"], p["ln_mix_g"], dh1, ALPHA, tag + "ln_mix")
    gs["ln_mix_g"], gs["ln_mix_b"] = acc[0], acc[1]
    gw["w_mix_out"] = _dw(a["merged"], du1, tag + "dw_mix")
    dmerged = _mm(du1, p["w_mix_out"], "nt", tag + "dmerged")
    dya, dyb, dproj = _merge_bwd(a["gl"], a["ya"], a["yb"], dmerged, tag + "merge",
                                 (None, CAT_OFF["gl"], MAIN_DIM))
    gw["w_ssd_out"] = _dw(a["yn"], dya, tag + "dw_ssd")
    gw["w_att_out"] = _dw(a["att"], dyb, tag + "dw_att")
    return {"du1": du1, "dya": dya, "dyb": dyb, "dproj": dproj}, gw, gs


def _bwd_mixers(a, p, carry, l, dep=None):
    tag = f"l{l}_b_"
    gs = {}
    du1, dproj = carry["du1"], carry["dproj"]

    def win(pc):
        return (dproj, CAT_OFF[pc], MAIN_DIM)

    dyn = _mm(carry["dya"], p["w_ssd_out"], "nt", tag + "dyn", dep=dep)
    datt = _mm(carry["dyb"], p["w_att_out"], "nt", tag + "datt", out_dtype=BF16, dep=dep)
    dproj, dkv, acc = _att_bwd(a["q"], a["kv"], p["sinks8"], a["att"], datt, tag + "att", win("q"))
    gs["att_sinks"] = acc[0, :ATT_HEADS]
    dy, dproj, acc = _gnorm_bwd(a["y"], a["z"], p["norm_w"], dyn, tag + "gnorm", win("z"))
    gs["ssd_norm_w"] = acc[0]
    dxs, dbm, dcm, dproj, acc = _ssd_bwd(a["xsc"], a["bc"], a["cc"], a["dt"], p["ssd_par"], a["hs"], dy,
                                         tag + "ssd", win("dt"))
    gs["dt_bias"], gs["a_log"], gs["d_skip"] = (_from_group_major(acc[i]) for i in range(3))
    dconv_w, dconv_b = [], []
    for s, dout in (("xs", dxs), ("b", dbm), ("c", dcm)):
        dc, acc = _conv_bwd_pre(a[s], p["conv_w8"][s], p["conv_b8"][s], dout, tag + "conv_pre_" + s)
        dconv_w.append(acc[:CONV_TAPS])
        dconv_b.append(acc[CONV_TAPS])
        dproj = _conv_bwd_in(dc, p["conv_w8"][s], tag + "conv_in_" + s, win(s))
    gconv = jnp.concatenate(dconv_w, axis=1)
    gs["conv_b"] = jnp.concatenate(dconv_b)
    w_main, w_kv = p["w_cat"][:, :MAIN_DIM], p["w_cat"][:, MAIN_DIM:]
    dw = jnp.concatenate([_dw(a["h0"], dproj, tag + "dw_in"), _dw(a["h0"], dkv, tag + "dw_in_kv")], axis=1)

    def grad_h0(dep=None):
        dh0 = _mm(dproj, w_main, "nt", tag + "dh0", add=du1, add_scale=ALPHA, dep=dep)
        return _mm(dkv, w_kv, "nt", tag + "dh0_kv", add=dh0)

    return grad_h0, _uncat_dw_in(dw), gconv, gs


def _step(x, target, w, m, v):
    x2 = x[0]
    t2 = target[0]
    tok = jnp.zeros(TOKEN.shape, TOKEN.dtype)

    ex = _exchange_start(_gather_items(w, IN_PART, 0), tok, "gather_l0_in_start")
    h = _ln_fwd(x2, None, w["ln_in_g"], w["ln_in_b"], 1.0, "ln_in")
    lands, tok = _exchange_wait(ex, h, "gather_l0_in_wait")
    p0 = _params_in(0, dict(zip(IN_PART, lands)), w)
    ex = _exchange_start(_gather_items(w, OUT_PART, 0) + _gather_items(w, IN_PART, 1), tok,
                         "gather_l0_out_l1_in_start")
    a0 = _fwd_mixers(h, p0, 0, dep=ex["token"])
    lands, tok = _exchange_wait(ex, a0["att"], "gather_l0_out_l1_in_wait")
    p0.update(_params_out(dict(zip(OUT_PART, lands))))
    p1 = _params_in(1, dict(zip(IN_PART, lands[len(OUT_PART):])), w)
    ex = _exchange_start(_gather_items(w, OUT_PART, 1), tok, "gather_l1_out_start")
    a0 = _fwd_out(a0, p0, 0, dep=ex["token"])
    lands, tok = _exchange_wait(ex, a0["h2"], "gather_l1_out_wait")
    p1.update(_params_out(dict(zip(OUT_PART, lands))))
    a1 = _fwd_out(_fwd_mixers(a0["h2"], p1, 1), p1, 1)

    sse, dh = _loss_fwd_bwd(a1["h2"], t2, "loss")
    loss = lax.psum(0.5 / D_MODEL * sse[0, 0], ("x", "y", "c"))

    carry, gw1, gs1 = _bwd_out(a1, p1, dh, 1)
    grad_h0, gw1["w_in"], gw1["conv_w"], gs = _bwd_mixers(a1, p1, carry, 1)
    dh = grad_h0()
    gs1.update(gs)
    ex1 = _exchange_start(_scatter_items(gw1, GATHERED), tok, "scatter_l1_start")
    carry, gw0, gs0 = _bwd_out(a0, p0, dh, 0, dep=ex1["token"])
    lands, tok = _exchange_wait(ex1, carry["dyb"], "scatter_l1_wait")
    land1 = dict(zip(GATHERED, lands))
    ex0 = _exchange_start(_scatter_items(gw0, OUT_PART), tok, "scatter_l0_out_start")
    grad_h0, gw0["w_in"], gw0["conv_w"], gs = _bwd_mixers(a0, p0, carry, 0, dep=ex0["token"])
    gs0.update(gs)
    lands, tok = _exchange_wait(ex0, gw0["w_in"], "scatter_l0_out_wait")
    land0 = dict(zip(OUT_PART, lands))
    ex0 = _exchange_start(_scatter_items(gw0, IN_PART), tok, "scatter_l0_in_start")
    dh = grad_h0(dep=ex0["token"])
    grad_x2, acc = _ln_bwd(x2, None, w["ln_in_g"], dh, 1.0, "ln_in_b")

    outs = [{} for _ in range(4)]

    def update(names):
        res = None
        for n in names:
            res = _adamw([land0[n], land1[n]], w[n], m[n], v[n], "adamw_" + n)
            for o, t in zip(outs, res):
                o[n] = t
        return res[1]

    update(OUT_PART)
    gsm = {"ln_in_g": acc[0], "ln_in_b": acc[1]}
    for n in SMALL[2:]:
        gsm[n] = jnp.stack([gs0[n], gs1[n]])
    small_shapes = {n: w[n].shape for n in SMALL}
    land_s = _all_gather_small(_pack_small(gsm), "small_grads_all_gather")
    res = _adamw([land_s], _pack_small(w)[None], _pack_small(m)[None], _pack_small(v)[None], "adamw_small")
    for o, t in zip(outs, res):
        o.update(_unpack_small(t[0], small_shapes))
    lands, _ = _exchange_wait(ex0, res[1], "scatter_l0_in_wait")
    land0.update(zip(IN_PART, lands))
    update(IN_PART)
    return loss, grad_x2[None], outs


WEIGHT_NAMES = ("ln_in_g", "ln_in_b", "w_in", "conv_w", "conv_b", "dt_bias", "a_log", "d_skip", "ssd_norm_w",
                "att_sinks", "w_ssd_out", "w_att_out", "w_mix_out", "ln_mix_g", "ln_mix_b", "w_ffn_gate",
                "w_ffn_up", "w_ffn_down", "ln_ffn_g", "ln_ffn_b")


def kernel(x, ln_in_g, ln_in_b, w_in, conv_w, conv_b, dt_bias, a_log, d_skip, ssd_norm_w, att_sinks, w_ssd_out, w_att_out, w_mix_out, ln_mix_g, ln_mix_b, w_ffn_gate, w_ffn_up, w_ffn_down, ln_ffn_g, ln_ffn_b, loss_target, m_ln_in_g, m_ln_in_b, m_w_in, m_conv_w, m_conv_b, m_dt_bias, m_a_log, m_d_skip, m_ssd_norm_w, m_att_sinks, m_w_ssd_out, m_w_att_out, m_w_mix_out, m_ln_mix_g, m_ln_mix_b, m_w_ffn_gate, m_w_ffn_up, m_w_ffn_down, m_ln_ffn_g, m_ln_ffn_b, v_ln_in_g, v_ln_in_b, v_w_in, v_conv_w, v_conv_b, v_dt_bias, v_a_log, v_d_skip, v_ssd_norm_w, v_att_sinks, v_w_ssd_out, v_w_att_out, v_w_mix_out, v_ln_mix_g, v_ln_mix_b, v_w_ffn_gate, v_w_ffn_up, v_w_ffn_down, v_ln_ffn_g, v_ln_ffn_b):
    w = dict(zip(WEIGHT_NAMES, (ln_in_g, ln_in_b, w_in, conv_w, conv_b, dt_bias, a_log, d_skip, ssd_norm_w,
                                att_sinks, w_ssd_out, w_att_out, w_mix_out, ln_mix_g, ln_mix_b, w_ffn_gate,
                                w_ffn_up, w_ffn_down, ln_ffn_g, ln_ffn_b)))
    m = dict(zip(WEIGHT_NAMES, (m_ln_in_g, m_ln_in_b, m_w_in, m_conv_w, m_conv_b, m_dt_bias, m_a_log, m_d_skip,
                                m_ssd_norm_w, m_att_sinks, m_w_ssd_out, m_w_att_out, m_w_mix_out, m_ln_mix_g,
                                m_ln_mix_b, m_w_ffn_gate, m_w_ffn_up, m_w_ffn_down, m_ln_ffn_g, m_ln_ffn_b)))
    v = dict(zip(WEIGHT_NAMES, (v_ln_in_g, v_ln_in_b, v_w_in, v_conv_w, v_conv_b, v_dt_bias, v_a_log, v_d_skip,
                                v_ssd_norm_w, v_att_sinks, v_w_ssd_out, v_w_att_out, v_w_mix_out, v_ln_mix_g,
                                v_ln_mix_b, v_w_ffn_gate, v_w_ffn_up, v_w_ffn_down, v_ln_ffn_g, v_ln_ffn_b)))
    loss, grad_x, outs = _step(x, loss_target, w, m, v)
    result = [loss, grad_x]
    for o in outs:
        result.extend(o[n] for n in WEIGHT_NAMES)
    return tuple(result)
```

```python
import functools
import math

import jax
import jax.numpy as jnp
from jax import lax
from jax.experimental import pallas as pl
from jax.experimental.pallas import tpu as pltpu

F32 = jnp.float32
BF16 = jnp.bfloat16

D_MODEL = 1024
DEPTH = 2
N_DEV = 8
ATT_HEADS = 16
ATT_KV_HEADS = 2
ATT_HEAD_DIM = 64
ATT_BLOCK = 128
SSD_D_INNER = 2048
SSD_HEADS = 32
SSD_GROUPS = 4
SSD_STATE = 128
SSD_CHUNK = 128
FFN_HIDDEN = 2816
LN_EPS = 1e-5
RMS_EPS = 1e-5
ALPHA = (2 * DEPTH) ** 0.25
Q_DIM = 1024
KV_DIM = 128
BC_DIM = 512
IN_DIM = 8480
IN_SHARD = IN_DIM // N_DEV
DT_PAD = 512

ADAM_LR = 0.001
ADAM_B1 = 0.9
ADAM_B2 = 0.999
ADAM_EPS = 1e-08
ADAM_WD = 0.01
ADAM_STEP = 10

LANE = 128
VMEM_LIMIT = 48 * 1024 * 1024
PACK_W = 1024
NEG = -1e30

_NN = (((1,), (0,)), ((), ()))
_NT = (((1,), (1,)), ((), ()))
_TN = (((0,), (0,)), ((), ()))
MESH_ID = pl.DeviceIdType.MESH


def _dot(a, b, dims=_NN):
    return lax.dot_general(a, b, dims, preferred_element_type=F32)


def _dot_hi(a, b):
    return lax.dot_general(a, b, _NN, preferred_element_type=F32, precision=lax.Precision.HIGHEST)


def _sig(x):
    return 1.0 / (1.0 + jnp.exp(-x))


def _softplus(x):
    return jnp.maximum(x, 0.0) + jnp.log(1.0 + jnp.exp(-jnp.abs(x)))


def _cparams(*sem):
    return pltpu.CompilerParams(dimension_semantics=sem, vmem_limit_bytes=VMEM_LIMIT)


def _pick(n, cap):
    if n <= cap:
        return n
    best = None
    for t in range(LANE, cap + 1, LANE):
        if n % t == 0:
            best = t
    assert best is not None, (n, cap)
    return best


def _tile(n):
    if n <= 1024 or n % 1024 == 0:
        return min(n, 1024)
    return _pick(n, 1408)


def _rows(n):
    return min(512, n)


def _window(x):
    return x if isinstance(x, tuple) else (x, 0, x.shape[1])


def _into(into, n_in, out_idx):
    buf, col0, width = into
    if buf is None:
        return [], [], {}, col0, width
    return [buf], [pl.BlockSpec(memory_space=pl.ANY)], {n_in: out_idx}, col0, width


def _mm(a, b, mode, name, add=None, add_scale=1.0, out_dtype=F32, dep=None):
    if mode == "nn":
        m, k = a.shape
        n = b.shape[1]
    elif mode == "nt":
        m, k = a.shape
        n = b.shape[0]
    else:
        k, m = a.shape
        n = b.shape[1]
    tm = _tile(m)
    tn = _pick(n, 2176) if mode == "tn" and n > 1024 else _tile(n)
    tk = _pick(k, 2176) if mode == "nt" and a.dtype == BF16 and k > 2816 else _tile(k)
    nk = k // tk
    has_add = add is not None
    dims = {"nn": _NN, "nt": _NT, "tn": _TN}[mode]

    def body(*refs):
        if dep is not None:
            refs = refs[:-3] + refs[-2:]
        if has_add:
            a_ref, b_ref, add_ref, o_ref, acc_ref = refs
        else:
            a_ref, b_ref, o_ref, acc_ref = refs
        kk = pl.program_id(2)

        @pl.when(kk == 0)
        def _():
            if has_add:
                acc_ref[...] = add_scale * add_ref[...].astype(F32)
            else:
                acc_ref[...] = jnp.zeros_like(acc_ref)

        acc_ref[...] += _dot(a_ref[...].astype(BF16), b_ref[...].astype(BF16), dims)

        @pl.when(kk == nk - 1)
        def _():
            o_ref[...] = acc_ref[...].astype(o_ref.dtype)

    if mode == "nn":
        a_spec = pl.BlockSpec((tm, tk), lambda i, j, kk: (i, kk))
        b_spec = pl.BlockSpec((tk, tn), lambda i, j, kk: (kk, j))
    elif mode == "nt":
        a_spec = pl.BlockSpec((tm, tk), lambda i, j, kk: (i, kk))
        b_spec = pl.BlockSpec((tn, tk), lambda i, j, kk: (j, kk))
    else:
        a_spec = pl.BlockSpec((tk, tm), lambda i, j, kk: (kk, i))
        b_spec = pl.BlockSpec((tk, tn), lambda i, j, kk: (kk, j))
    o_spec = pl.BlockSpec((tm, tn), lambda i, j, kk: (i, j))
    in_specs = [a_spec, b_spec] + ([o_spec] if has_add else [])
    args = (a, b) + ((add,) if has_add else ())
    if dep is not None:
        in_specs.append(pl.BlockSpec((8, LANE), lambda i, j, kk: (0, 0)))
        args += (dep,)
    return pl.pallas_call(
        body, name=name, grid=(m // tm, n // tn, nk),
        in_specs=in_specs, out_specs=o_spec,
        out_shape=jax.ShapeDtypeStruct((m, n), out_dtype),
        scratch_shapes=[pltpu.VMEM((tm, tn), F32)],
        compiler_params=_cparams("parallel", "parallel", "arbitrary"),
    )(*args)


def _vec_spec(width):
    return pl.BlockSpec((1, width), lambda i: (0, 0))


def _ln_fwd(a, b, gamma, beta, alpha, name):
    n_rows, dm = a.shape
    has_b = b is not None

    def body(*refs):
        if has_b:
            a_ref, b_ref, g_ref, be_ref, o_ref = refs
            u = alpha * a_ref[...] + b_ref[...]
        else:
            a_ref, g_ref, be_ref, o_ref = refs
            u = a_ref[...]
        mu = jnp.mean(u, axis=-1, keepdims=True)
        d = u - mu
        var = jnp.mean(d * d, axis=-1, keepdims=True)
        o_ref[...] = d * lax.rsqrt(var + LN_EPS) * g_ref[...] + be_ref[...]

    row = pl.BlockSpec((_rows(n_rows),dm), lambda i: (i, 0))
    in_specs = [row] + ([row] if has_b else []) + [_vec_spec(dm), _vec_spec(dm)]
    args = (a,) + ((b,) if has_b else ()) + (gamma.reshape(1, dm), beta.reshape(1, dm))
    return pl.pallas_call(
        body, name=name, grid=(n_rows // _rows(n_rows),), in_specs=in_specs, out_specs=row,
        out_shape=jax.ShapeDtypeStruct((n_rows, dm), F32),
        compiler_params=_cparams("parallel"),
    )(*args)


def _ln_bwd(a, b, gamma, dy, alpha, name):
    n_rows, dm = a.shape
    has_b = b is not None

    def body(*refs):
        if has_b:
            a_ref, b_ref, g_ref, dy_ref, du_ref, acc_ref = refs
            u = alpha * a_ref[...] + b_ref[...]
        else:
            a_ref, g_ref, dy_ref, du_ref, acc_ref = refs
            u = a_ref[...]

        @pl.when(pl.program_id(0) == 0)
        def _():
            acc_ref[...] = jnp.zeros_like(acc_ref)

        mu = jnp.mean(u, axis=-1, keepdims=True)
        d = u - mu
        var = jnp.mean(d * d, axis=-1, keepdims=True)
        rstd = lax.rsqrt(var + LN_EPS)
        xhat = d * rstd
        dyv = dy_ref[...]
        acc_ref[0:1, :] += jnp.sum(dyv * xhat, axis=0, keepdims=True)
        acc_ref[1:2, :] += jnp.sum(dyv, axis=0, keepdims=True)
        dxh = dyv * g_ref[...]
        m1 = jnp.mean(dxh, axis=-1, keepdims=True)
        m2 = jnp.mean(dxh * xhat, axis=-1, keepdims=True)
        du_ref[...] = rstd * (dxh - m1 - xhat * m2)

    row = pl.BlockSpec((_rows(n_rows),dm), lambda i: (i, 0))
    in_specs = [row] + ([row] if has_b else []) + [_vec_spec(dm), row]
    args = (a,) + ((b,) if has_b else ()) + (gamma.reshape(1, dm), dy)
    return pl.pallas_call(
        body, name=name, grid=(n_rows // _rows(n_rows),), in_specs=in_specs,
        out_specs=(row, pl.BlockSpec((8, dm), lambda i: (0, 0))),
        out_shape=(jax.ShapeDtypeStruct((n_rows, dm), F32), jax.ShapeDtypeStruct((8, dm), F32)),
        compiler_params=_cparams("arbitrary"),
    )(*args)


def _loss_fwd_bwd(y, target, name):
    n_rows, dm = y.shape

    def body(y_ref, t_ref, acc_ref, dy_ref):
        @pl.when(pl.program_id(0) == 0)
        def _():
            acc_ref[...] = jnp.zeros_like(acc_ref)

        d = y_ref[...] - t_ref[...]
        acc_ref[...] += jnp.sum(d * d)
        dy_ref[...] = d * (1.0 / dm)

    row = pl.BlockSpec((_rows(n_rows),dm), lambda i: (i, 0))
    return pl.pallas_call(
        body, name=name, grid=(n_rows // _rows(n_rows),), in_specs=[row, row],
        out_specs=(pl.BlockSpec((8, LANE), lambda i: (0, 0)), row),
        out_shape=(jax.ShapeDtypeStruct((8, LANE), F32), jax.ShapeDtypeStruct((n_rows, dm), F32)),
        compiler_params=_cparams("arbitrary"),
    )(y, target)


def _swiglu_fwd(g, u, name):
    n_rows, w = g.shape
    tw = _pick(w, 1408)

    def body(g_ref, u_ref, o_ref):
        gv = g_ref[...]
        o_ref[...] = (gv * _sig(gv) * u_ref[...]).astype(BF16)

    blk = pl.BlockSpec((_rows(n_rows),tw), lambda i, j: (i, j))
    return pl.pallas_call(
        body, name=name, grid=(n_rows // _rows(n_rows), w // tw), in_specs=[blk, blk], out_specs=blk,
        out_shape=jax.ShapeDtypeStruct((n_rows, w), BF16),
        compiler_params=_cparams("parallel", "parallel"),
    )(g, u)


def _swiglu_bwd(g, u, dact, name):
    n_rows, w = g.shape
    tw = _pick(w, 1408)

    def body(g_ref, u_ref, da_ref, dg_ref, du_ref):
        gv = g_ref[...]
        s = _sig(gv)
        da = da_ref[...]
        dg_ref[...] = (da * u_ref[...] * (s * (1.0 + gv * (1.0 - s)))).astype(BF16)
        du_ref[...] = (da * gv * s).astype(BF16)

    blk = pl.BlockSpec((_rows(n_rows),tw), lambda i, j: (i, j))
    return pl.pallas_call(
        body, name=name, grid=(n_rows // _rows(n_rows), w // tw), in_specs=[blk, blk, blk], out_specs=(blk, blk),
        out_shape=(jax.ShapeDtypeStruct((n_rows, w), BF16), jax.ShapeDtypeStruct((n_rows, w), BF16)),
        compiler_params=_cparams("parallel", "parallel"),
    )(g, u, dact)


FFN_ROWS = 512


def _ffn_in(h, wg, wu, name, dep=None):
    m, k = h.shape
    n = wg.shape[1]
    tm, tn = min(FFN_ROWS, m), _tile(n)

    def body(*refs):
        h_ref, wg_ref, wu_ref = refs[:3]
        g_ref, u_ref, act_ref = refs[-3:]
        hb = h_ref[...].astype(BF16)
        g = _dot(hb, wg_ref[...])
        u = _dot(hb, wu_ref[...])
        g_ref[...] = g
        u_ref[...] = u
        act_ref[...] = (g * _sig(g) * u).astype(BF16)

    rows = pl.BlockSpec((tm, k), lambda j, i: (i, 0))
    wcol = pl.BlockSpec((k, tn), lambda j, i: (0, j))
    out = pl.BlockSpec((tm, tn), lambda j, i: (i, j))
    in_specs, args = [rows, wcol, wcol], (h, wg, wu)
    if dep is not None:
        in_specs.append(pl.BlockSpec((8, LANE), lambda j, i: (0, 0)))
        args += (dep,)
    return pl.pallas_call(
        body, name=name, grid=(n // tn, m // tm), in_specs=in_specs, out_specs=(out, out, out),
        out_shape=(jax.ShapeDtypeStruct((m, n), F32), jax.ShapeDtypeStruct((m, n), F32),
                   jax.ShapeDtypeStruct((m, n), BF16)),
        compiler_params=_cparams("parallel", "parallel"),
    )(*args)


def _ffn_dact(dy, wd, g, u, name, dep=None):
    m, k = dy.shape
    n = wd.shape[0]
    tm, tn = min(FFN_ROWS, m), _tile(n)

    def body(*refs):
        dy_ref, wd_ref, g_ref, u_ref = refs[:4]
        dg_ref, du_ref = refs[-2:]
        da = _dot(dy_ref[...].astype(BF16), wd_ref[...], _NT)
        gv = g_ref[...]
        s = _sig(gv)
        dg_ref[...] = (da * u_ref[...] * (s * (1.0 + gv * (1.0 - s)))).astype(BF16)
        du_ref[...] = (da * gv * s).astype(BF16)

    rows = pl.BlockSpec((tm, k), lambda j, i: (i, 0))
    wrow = pl.BlockSpec((tn, k), lambda j, i: (j, 0))
    out = pl.BlockSpec((tm, tn), lambda j, i: (i, j))
    in_specs, args = [rows, wrow, out, out], (dy, wd, g, u)
    if dep is not None:
        in_specs.append(pl.BlockSpec((8, LANE), lambda j, i: (0, 0)))
        args += (dep,)
    return pl.pallas_call(
        body, name=name, grid=(n // tn, m // tm), in_specs=in_specs, out_specs=(out, out),
        out_shape=(jax.ShapeDtypeStruct((m, n), BF16), jax.ShapeDtypeStruct((m, n), BF16)),
        compiler_params=_cparams("parallel", "parallel"),
    )(*args)


def _gate_specs(gl, n_rows, dm):
    arr, g0, _ = _window(gl)
    return arr, [pl.BlockSpec((_rows(n_rows), dm), lambda i, k=k: (i, g0 // dm + k)) for k in range(2)]


def _merge_fwd(gl, ya, yb, name):
    n_rows, dm = ya.shape
    gl_arr, gspecs = _gate_specs(gl, n_rows, dm)

    def body(ga_ref, gb_ref, ya_ref, yb_ref, o_ref):
        o_ref[...] = (_sig(ga_ref[...]) * ya_ref[...] + _sig(gb_ref[...]) * yb_ref[...]).astype(BF16)

    row = pl.BlockSpec((_rows(n_rows),dm), lambda i: (i, 0))
    return pl.pallas_call(
        body, name=name, grid=(n_rows // _rows(n_rows),), in_specs=gspecs + [row, row], out_specs=row,
        out_shape=jax.ShapeDtypeStruct((n_rows, dm), BF16),
        compiler_params=_cparams("parallel"),
    )(gl_arr, gl_arr, ya, yb)


def _merge_bwd(gl, ya, yb, dmerged, name, into):
    n_rows, dm = ya.shape
    gl_arr, gspecs = _gate_specs(gl, n_rows, dm)
    extra, extra_specs, aliases, col0, width = _into(into, 5, 2)

    def body(*refs):
        ga_ref, gb_ref, ya_ref, yb_ref, dm_ref = refs[:5]
        dya_ref, dyb_ref, dgl_ref = refs[-3:]
        ga = _sig(ga_ref[...])
        gb = _sig(gb_ref[...])
        dmv = dm_ref[...]
        dya_ref[...] = (dmv * ga).astype(BF16)
        dyb_ref[...] = (dmv * gb).astype(BF16)
        dgl_ref[:, :dm] = (dmv * ya_ref[...] * ga * (1.0 - ga)).astype(BF16)
        dgl_ref[:, dm:] = (dmv * yb_ref[...] * gb * (1.0 - gb)).astype(BF16)

    row = pl.BlockSpec((_rows(n_rows),dm), lambda i: (i, 0))
    row2 = pl.BlockSpec((_rows(n_rows),2 * dm), lambda i: (i, col0 // (2 * dm)))
    return pl.pallas_call(
        body, name=name, grid=(n_rows // _rows(n_rows),), in_specs=gspecs + [row, row, row] + extra_specs,
        out_specs=(row, row, row2),
        out_shape=(jax.ShapeDtypeStruct((n_rows, dm), BF16), jax.ShapeDtypeStruct((n_rows, dm), BF16),
                   jax.ShapeDtypeStruct((n_rows, width), BF16)),
        input_output_aliases=aliases,
        compiler_params=_cparams("parallel"),
    )(gl_arr, gl_arr, ya, yb, dmerged, *extra)


CONV_TAPS = 4
CONV_COLS = 512
HALO = 8


def _shift_down(cur, prev8, s, row8):
    r = pltpu.roll(cur, s, axis=0)
    top = jnp.where(row8 < s, pltpu.roll(prev8, s, axis=0), r[0:HALO])
    return jnp.concatenate([top, r[HALO:]], axis=0)


def _shift_up(cur, next8, s, row8):
    n = cur.shape[0]
    r = pltpu.roll(cur, n - s, axis=0)
    bot = jnp.where(row8 >= HALO - s, pltpu.roll(next8, HALO - s, axis=0), r[n - HALO:])
    return jnp.concatenate([r[:n - HALO], bot], axis=0)


def _conv_pre(u_ref, prev_ref, w_ref, b_ref, li):
    cur = u_ref[...]
    prev8 = jnp.where(li == 0, 0.0, prev_ref[...])
    row8 = lax.broadcasted_iota(jnp.int32, prev8.shape, 0)
    shifted = [cur] + [_shift_down(cur, prev8, s, row8) for s in range(1, CONV_TAPS)]
    acc = b_ref[...] + shifted[0] * w_ref[CONV_TAPS - 1:CONV_TAPS, :]
    for s in range(1, CONV_TAPS):
        acc = acc + shifted[s] * w_ref[CONV_TAPS - 1 - s:CONV_TAPS - s, :]
    return acc, shifted


def _conv_specs(n_rows, tl, col0=0):
    off = col0 // CONV_COLS
    cur = pl.BlockSpec((tl, CONV_COLS), lambda cj, li: (li, cj + off))
    prev = pl.BlockSpec((HALO, CONV_COLS), lambda cj, li: (jnp.maximum(li * (tl // HALO) - 1, 0), cj + off))
    nxt = pl.BlockSpec((HALO, CONV_COLS),
                       lambda cj, li: (jnp.minimum((li + 1) * (tl // HALO), n_rows // HALO - 1), cj + off))
    par = pl.BlockSpec((8, CONV_COLS), lambda cj, li: (0, cj + off))
    return cur, prev, nxt, par


def _conv_fwd(u, w8, b8, name):
    u, u0, c = _window(u)
    n_rows = u.shape[0]
    tl = _rows(n_rows)
    cur, _, _, par = _conv_specs(n_rows, tl)
    ucur, prev, _, _ = _conv_specs(n_rows, tl, u0)

    def body(u_ref, prev_ref, w_ref, b_ref, o_ref):
        acc, _ = _conv_pre(u_ref, prev_ref, w_ref, b_ref[0:1, :], pl.program_id(1))
        o_ref[...] = acc * _sig(acc)

    return pl.pallas_call(
        body, name=name, grid=(c // CONV_COLS, n_rows // tl), in_specs=[ucur, prev, par, par], out_specs=cur,
        out_shape=jax.ShapeDtypeStruct((n_rows, c), F32),
        compiler_params=_cparams("parallel", "parallel"),
    )(u, u, w8, b8)


def _conv_bwd_pre(u, w8, b8, dout, name):
    u, u0, c = _window(u)
    n_rows = u.shape[0]
    tl = _rows(n_rows)
    cur, _, _, par = _conv_specs(n_rows, tl)
    ucur, prev, _, _ = _conv_specs(n_rows, tl, u0)

    def body(u_ref, prev_ref, w_ref, b_ref, do_ref, dc_ref, acc_ref):
        @pl.when(pl.program_id(1) == 0)
        def _():
            acc_ref[...] = jnp.zeros_like(acc_ref)

        acc, shifted = _conv_pre(u_ref, prev_ref, w_ref, b_ref[0:1, :], pl.program_id(1))
        sg = _sig(acc)
        dc = do_ref[...] * (sg * (1.0 + acc * (1.0 - sg)))
        dc_ref[...] = dc
        for k in range(CONV_TAPS):
            acc_ref[k:k + 1, :] += jnp.sum(dc * shifted[CONV_TAPS - 1 - k], axis=0, keepdims=True)
        acc_ref[CONV_TAPS:CONV_TAPS + 1, :] += jnp.sum(dc, axis=0, keepdims=True)

    return pl.pallas_call(
        body, name=name, grid=(c // CONV_COLS, n_rows // tl), in_specs=[ucur, prev, par, par, cur],
        out_specs=(cur, par),
        out_shape=(jax.ShapeDtypeStruct((n_rows, c), F32), jax.ShapeDtypeStruct((8, c), F32)),
        compiler_params=_cparams("parallel", "arbitrary"),
    )(u, u, w8, b8, dout)


def _conv_bwd_in(dc, w8, name, into):
    n_rows, c = dc.shape
    tl = _rows(n_rows)
    cur, _, nxt, par = _conv_specs(n_rows, tl)
    n_l = n_rows // tl
    extra, extra_specs, aliases, col0, width = _into(into, 3, 0)
    out_spec = _conv_specs(n_rows, tl, col0)[0]

    def body(*refs):
        dc_ref, next_ref, w_ref = refs[:3]
        o_ref = refs[-1]
        cur_v = dc_ref[...]
        next8 = jnp.where(pl.program_id(1) == n_l - 1, 0.0, next_ref[...])
        row8 = lax.broadcasted_iota(jnp.int32, next8.shape, 0)
        acc = cur_v * w_ref[CONV_TAPS - 1:CONV_TAPS, :]
        for s in range(1, CONV_TAPS):
            acc = acc + _shift_up(cur_v, next8, s, row8) * w_ref[CONV_TAPS - 1 - s:CONV_TAPS - s, :]
        o_ref[...] = acc.astype(BF16)

    return pl.pallas_call(
        body, name=name, grid=(c // CONV_COLS, n_l), in_specs=[cur, nxt, par] + extra_specs, out_specs=out_spec,
        out_shape=jax.ShapeDtypeStruct((n_rows, width), BF16), input_output_aliases=aliases,
        compiler_params=_cparams("parallel", "parallel"),
    )(dc, dc, w8, *extra)


NORM_GROUP = SSD_D_INNER // SSD_GROUPS


def _gnorm_fwd(y, z, w, name):
    n_rows, c = y.shape
    z, z0, _ = _window(z)
    zoff = z0 // NORM_GROUP

    def body(y_ref, z_ref, w_ref, o_ref):
        zv = z_ref[...]
        yg = y_ref[...] * (zv * _sig(zv))
        r = lax.rsqrt(jnp.mean(yg * yg, axis=-1, keepdims=True) + RMS_EPS)
        o_ref[...] = (yg * r * w_ref[...]).astype(BF16)

    blk = pl.BlockSpec((_rows(n_rows),NORM_GROUP), lambda i, j: (i, j))
    zblk = pl.BlockSpec((_rows(n_rows),NORM_GROUP), lambda i, j: (i, j + zoff))
    wspec = pl.BlockSpec((1, NORM_GROUP), lambda i, j: (0, j))
    return pl.pallas_call(
        body, name=name, grid=(n_rows // _rows(n_rows), c // NORM_GROUP), in_specs=[blk, zblk, wspec], out_specs=blk,
        out_shape=jax.ShapeDtypeStruct((n_rows, c), BF16),
        compiler_params=_cparams("parallel", "parallel"),
    )(y, z, w.reshape(1, c))


def _gnorm_bwd(y, z, w, dyn, name, into):
    n_rows, c = y.shape
    z, z0, _ = _window(z)
    zoff = z0 // NORM_GROUP
    extra, extra_specs, aliases, col0, width = _into(into, 4, 1)
    doff = col0 // NORM_GROUP

    def body(*refs):
        y_ref, z_ref, w_ref, dn_ref = refs[:4]
        dy_ref, dz_ref, acc_ref = refs[-3:]
        @pl.when(pl.program_id(1) == 0)
        def _():
            acc_ref[...] = jnp.zeros_like(acc_ref)

        zv = z_ref[...]
        yv = y_ref[...]
        sz = _sig(zv)
        silu = zv * sz
        yg = yv * silu
        r = lax.rsqrt(jnp.mean(yg * yg, axis=-1, keepdims=True) + RMS_EPS)
        nrm = yg * r
        dn = dn_ref[...]
        acc_ref[0:1, :] += jnp.sum(dn * nrm, axis=0, keepdims=True)
        dnw = dn * w_ref[...]
        dyg = r * (dnw - nrm * jnp.mean(dnw * nrm, axis=-1, keepdims=True))
        dy_ref[...] = dyg * silu
        dz_ref[...] = (dyg * yv * (sz * (1.0 + zv * (1.0 - sz)))).astype(BF16)

    blk = pl.BlockSpec((_rows(n_rows),NORM_GROUP), lambda j, i: (i, j))
    zblk = pl.BlockSpec((_rows(n_rows),NORM_GROUP), lambda j, i: (i, j + zoff))
    wspec = pl.BlockSpec((1, NORM_GROUP), lambda j, i: (0, j))
    aspec = pl.BlockSpec((8, NORM_GROUP), lambda j, i: (0, j))
    return pl.pallas_call(
        body, name=name, grid=(c // NORM_GROUP, n_rows // _rows(n_rows)),
        in_specs=[blk, zblk, wspec, blk] + extra_specs,
        out_specs=(blk, pl.BlockSpec((_rows(n_rows), NORM_GROUP), lambda j, i: (i, j + doff)), aspec),
        out_shape=(jax.ShapeDtypeStruct((n_rows, c), F32), jax.ShapeDtypeStruct((n_rows, width), BF16),
                   jax.ShapeDtypeStruct((8, c), F32)),
        input_output_aliases=aliases,
        compiler_params=_cparams("parallel", "arbitrary"),
    )(y, z, w.reshape(1, c), dyn, *extra)


ATT_SCALE = ATT_HEAD_DIM ** -0.5
ATT_SLOPES = [2.0 ** (-8.0 * (h + 1) / ATT_HEADS) for h in range(ATT_HEADS)]
Q_PER_KV = ATT_HEADS // ATT_KV_HEADS


def _dup_half(t, g, lo):
    tr = pltpu.roll(t, ATT_HEAD_DIM, axis=1)
    return jnp.where(lo, t, tr) if g == 0 else jnp.where(lo, tr, t)


def _att_band(kv_ref, kvp_ref, n):
    cur = kv_ref[...]
    prev = jnp.where(n == 0, 0.0, kvp_ref[...])
    lo = lax.broadcasted_iota(jnp.int32, (ATT_BLOCK, LANE), 1) < ATT_HEAD_DIM
    bands = []
    for g in range(ATT_KV_HEADS):
        kb = jnp.concatenate([_dup_half(prev[:, :LANE], g, lo), _dup_half(cur[:, :LANE], g, lo)], axis=0)
        vb = jnp.concatenate([_dup_half(prev[:, LANE:], g, lo), _dup_half(cur[:, LANE:], g, lo)], axis=0)
        bands.append((kb.astype(BF16), vb.astype(BF16)))
    return bands


def _att_tile(n):
    shape = (2 * ATT_BLOCK, ATT_BLOCK)
    row = lax.broadcasted_iota(jnp.int32, shape, 0)
    i = row & (ATT_BLOCK - 1)
    s = lax.broadcasted_iota(jnp.int32, shape, 1)
    upper = s > i
    dist = ((i - s) & (ATT_BLOCK - 1)).astype(F32)
    dead = upper & (n == 0)
    return upper, dist, dead, row[:, 0:1] < ATT_BLOCK


def _stack_pair(t, lo):
    return jnp.concatenate([jnp.where(lo, t, 0.0), jnp.where(lo, 0.0, t)], axis=0).astype(BF16)


def _att_exp(qs, kb, s_ref, j, tile):
    upper, dist, dead, first = tile
    s2 = _dot(qs, kb, _NT)
    slope = jnp.where(first, ATT_SLOPES[2 * j], ATT_SLOPES[2 * j + 1])
    sink = jnp.where(first, s_ref[0:1, 2 * j:2 * j + 1], s_ref[0:1, 2 * j + 1:2 * j + 2])
    s = jnp.where(upper, s2[:, :ATT_BLOCK], s2[:, ATT_BLOCK:]) - slope * dist
    s = jnp.where(dead, NEG, s)
    m = jnp.maximum(jnp.max(s, axis=-1, keepdims=True), sink)
    return jnp.exp(s - m), jnp.exp(sink - m)


def _band_split(t, upper):
    return jnp.concatenate([jnp.where(upper, t, 0.0), jnp.where(upper, 0.0, t)], axis=1)


def _att_fwd(q, kv, sinks8, name):
    q, q0, _ = _window(q)
    kv, kv0, _ = _window(kv)
    qoff, kvoff = q0 // Q_DIM, kv0 // (2 * LANE)
    n_rows = q.shape[0]
    nb = n_rows // ATT_BLOCK

    def body(q_ref, kv_ref, kvp_ref, s_ref, o_ref):
        n = pl.program_id(0)
        bands = _att_band(kv_ref, kvp_ref, n)
        lo = lax.broadcasted_iota(jnp.int32, (ATT_BLOCK, LANE), 1) < ATT_HEAD_DIM
        tile = _att_tile(n)
        ones_b = jnp.ones((2 * ATT_BLOCK, LANE), BF16)
        for j in range(ATT_HEADS // 2):
            kb, vb = bands[2 * j // Q_PER_KV]
            qs = _stack_pair(q_ref[:, j * LANE:(j + 1) * LANE] * ATT_SCALE, lo)
            p, es = _att_exp(qs, kb, s_ref, j, tile)
            pv = _dot(_band_split(p, tile[0]).astype(BF16), jnp.concatenate([vb, ones_b], axis=1))
            out = pv[:, :LANE] / (pv[:, LANE:] + es)
            o_ref[:, j * LANE:(j + 1) * LANE] = jnp.where(lo, out[:ATT_BLOCK], out[ATT_BLOCK:]).astype(BF16)

    return pl.pallas_call(
        body, name=name, grid=(nb,),
        in_specs=[pl.BlockSpec((ATT_BLOCK, Q_DIM), lambda n: (n, qoff)),
                  pl.BlockSpec((ATT_BLOCK, 2 * LANE), lambda n: (n, kvoff)),
                  pl.BlockSpec((ATT_BLOCK, 2 * LANE), lambda n: (jnp.maximum(n - 1, 0), kvoff)),
                  pl.BlockSpec((8, LANE), lambda n: (0, 0))],
        out_specs=pl.BlockSpec((ATT_BLOCK, Q_DIM), lambda n: (n, 0)),
        out_shape=jax.ShapeDtypeStruct((n_rows, Q_DIM), BF16),
        compiler_params=_cparams("parallel"),
    )(q, kv, kv, sinks8)


def _att_bwd(q, kv, sinks8, att, dout, name, into):
    q, q0, _ = _window(q)
    kv, kv0, _ = _window(kv)
    qoff, kvoff = q0 // Q_DIM, kv0 // (2 * LANE)
    n_rows = q.shape[0]
    nb = n_rows // ATT_BLOCK

    extra, extra_specs, aliases, col0, width = _into(into, 6, 0)
    dqoff = col0 // Q_DIM

    def body(*refs):
        q_ref, kv_ref, kvp_ref, s_ref, o_ref, do_ref = refs[:6]
        dq_ref, dkv_ref, acc_ref, carry_ref = refs[-4:]
        n = pl.program_id(0)

        @pl.when(n == 0)
        def _():
            acc_ref[...] = jnp.zeros_like(acc_ref)
            carry_ref[...] = jnp.zeros_like(carry_ref)

        @pl.when(n == nb)
        def _():
            dkv_ref[...] = carry_ref[...].astype(BF16)

        @pl.when(n < nb)
        def _():
            bands = _att_band(kv_ref, kvp_ref, n)
            lo = lax.broadcasted_iota(jnp.int32, (ATT_BLOCK, LANE), 1) < ATT_HEAD_DIM
            lane1 = lax.broadcasted_iota(jnp.int32, (1, LANE), 1)
            tile = _att_tile(n)
            upper, first = tile[0], tile[3]
            ones_b = jnp.ones((ATT_BLOCK, LANE), BF16)
            ones2_b = jnp.ones((2 * LANE, LANE), BF16)
            dk_acc = [jnp.zeros((2 * ATT_BLOCK, LANE), F32) for _ in range(ATT_KV_HEADS)]
            dv_acc = [jnp.zeros((2 * ATT_BLOCK, LANE), F32) for _ in range(ATT_KV_HEADS)]
            dsink = jnp.zeros((1, LANE), F32)
            for j in range(ATT_HEADS // 2):
                g = 2 * j // Q_PER_KV
                kb, vb = bands[g]
                qs = _stack_pair(q_ref[:, j * LANE:(j + 1) * LANE] * ATT_SCALE, lo)
                dop = do_ref[:, j * LANE:(j + 1) * LANE].astype(F32)
                dos = _stack_pair(dop, lo)
                pu, es = _att_exp(qs, kb, s_ref, j, tile)
                inv = 1.0 / (_dot(pu.astype(BF16), ones_b) + es)
                p = pu * inv
                od = dop * o_ref[:, j * LANE:(j + 1) * LANE].astype(F32)
                od = jnp.concatenate([jnp.where(lo, od, 0.0), jnp.where(lo, 0.0, od)], axis=0)
                od_hi = od.astype(BF16)
                delta = _dot(jnp.concatenate([od_hi, (od - od_hi.astype(F32)).astype(BF16)], axis=1), ones2_b)
                dp2 = _dot(dos, vb, _NT)
                dp = jnp.where(upper, dp2[:, :ATT_BLOCK], dp2[:, ATT_BLOCK:])
                ds2 = _band_split(p * (dp - delta), upper)
                psd = jnp.sum(es * inv * delta, axis=0, keepdims=True)
                psd0 = jnp.sum(jnp.where(first, es * inv * delta, 0.0), axis=0, keepdims=True)
                dsink = jnp.where(lane1 == 2 * j, -psd0, jnp.where(lane1 == 2 * j + 1, psd0 - psd, dsink))
                dq = _dot(ds2.astype(BF16), kb) * ATT_SCALE
                dq_ref[:, j * LANE:(j + 1) * LANE] = jnp.where(lo, dq[:ATT_BLOCK], dq[ATT_BLOCK:]).astype(BF16)
                dk_acc[g] = dk_acc[g] + _dot(ds2.T.astype(BF16), qs)
                dv_acc[g] = dv_acc[g] + _dot(_band_split(p, upper).T.astype(BF16), dos)
            acc_ref[0:1, :] += dsink
            lo2 = lax.broadcasted_iota(jnp.int32, (2 * ATT_BLOCK, LANE), 1) < ATT_HEAD_DIM
            folded = []
            for acc in (dk_acc, dv_acc):
                t0 = acc[0] + pltpu.roll(acc[0], ATT_HEAD_DIM, axis=1)
                t1 = acc[1] + pltpu.roll(acc[1], ATT_HEAD_DIM, axis=1)
                folded.append(jnp.where(lo2, t0, t1))
            band = jnp.concatenate(folded, axis=1)
            dkv_ref[...] = (carry_ref[...] + band[:ATT_BLOCK]).astype(BF16)
            carry_ref[...] = band[ATT_BLOCK:]

    def qmap(n):
        return (jnp.minimum(n, nb - 1), 0)

    return pl.pallas_call(
        body, name=name, grid=(nb + 1,),
        in_specs=[pl.BlockSpec((ATT_BLOCK, Q_DIM), lambda n: (jnp.minimum(n, nb - 1), qoff)),
                  pl.BlockSpec((ATT_BLOCK, 2 * LANE), lambda n: (jnp.minimum(n, nb - 1), kvoff)),
                  pl.BlockSpec((ATT_BLOCK, 2 * LANE),
                               lambda n: (jnp.maximum(jnp.minimum(n, nb - 1) - 1, 0), kvoff)),
                  pl.BlockSpec((8, LANE), lambda n: (0, 0)),
                  pl.BlockSpec((ATT_BLOCK, Q_DIM), qmap),
                  pl.BlockSpec((ATT_BLOCK, Q_DIM), qmap)] + extra_specs,
        out_specs=(pl.BlockSpec((ATT_BLOCK, Q_DIM), lambda n: (jnp.minimum(n, nb - 1), dqoff)),
                   pl.BlockSpec((ATT_BLOCK, 2 * LANE), lambda n: (jnp.maximum(n - 1, 0), 0)),
                   pl.BlockSpec((8, LANE), lambda n: (0, 0))),
        out_shape=(jax.ShapeDtypeStruct((n_rows, width), BF16), jax.ShapeDtypeStruct((n_rows, 2 * LANE), BF16),
                   jax.ShapeDtypeStruct((8, LANE), F32)),
        input_output_aliases=aliases,
        scratch_shapes=[pltpu.VMEM((ATT_BLOCK, 2 * LANE), F32)],
        compiler_params=_cparams("arbitrary"),
    )(q, kv, kv, sinks8, att, dout, *extra)


HEADS_PER_GROUP = SSD_HEADS // SSD_GROUPS
PAIRS_PER_GROUP = HEADS_PER_GROUP // 2
T = SSD_CHUNK


def _ssd_scalars(dtr_ref, par_ref):
    dt = _softplus(dtr_ref[...] + par_ref[0:1, :])
    a = -jnp.exp(par_ref[1:2, :])
    ri = lax.broadcasted_iota(jnp.int32, (T, T), 0)
    ci = lax.broadcasted_iota(jnp.int32, (T, T), 1)
    tril = (ri >= ci).astype(F32)
    cs = _dot_hi(tril, dt * a)
    cst = cs.T
    return dt, a, cs, cst, ri, ci


def _ssd_stacked_masks():
    row = lax.broadcasted_iota(jnp.int32, (2 * T, T), 0)
    t = row & (T - 1)
    s = lax.broadcasted_iota(jnp.int32, (2 * T, T), 1)
    return t >= s, s >= t, row[:, 0:1] < T


def _col_s(arr, k0):
    return jnp.concatenate([arr[:, k0:k0 + 1], arr[:, k0 + 1:k0 + 2]], axis=0)


def _row_s(arr_t, k0, first):
    return jnp.where(first, arr_t[k0:k0 + 1, :], arr_t[k0 + 1:k0 + 2, :])


def _lane_pick(lo, arr, k0):
    return jnp.where(lo, arr[:, k0:k0 + 1], arr[:, k0 + 1:k0 + 2])


def _ssd_fwd_stacked(xs, bm, cm, dtr, par, name):
    dtr, dt0, _ = _window(dtr)
    dtoff = dt0 // LANE
    n_rows = xs.shape[0]
    nc = n_rows // T
    gw = PAIRS_PER_GROUP * LANE

    def body(x_ref, b_ref, c_ref, dtr_ref, par_ref, y_ref, hs_ref, h_ref):
        @pl.when(pl.program_id(1) == 0)
        def _():
            h_ref[...] = jnp.zeros_like(h_ref)

        dt, a, cs, cst, _, _ = _ssd_scalars(dtr_ref, par_ref)
        tri_s, _, first = _ssd_stacked_masks()
        lo = lax.broadcasted_iota(jnp.int32, (T, LANE), 1) < SSD_CHUNK // 2
        ecs = jnp.exp(cs)
        dect = jnp.exp(cst[:, T - 1:T] - cst)
        etot = jnp.exp(cs[T - 1:T, :])
        bg = b_ref[...]
        cg = c_ref[...]
        cb = _dot(cg.astype(BF16), bg.astype(BF16), _NT)
        cb_s = jnp.concatenate([cb, cb], axis=0)
        cg_s = jnp.concatenate([cg, cg], axis=0)
        bgt_s = jnp.concatenate([bg.T, bg.T], axis=0)
        for j in range(PAIRS_PER_GROUP):
            k0, k1 = 2 * j, 2 * j + 1
            xp = x_ref[:, j * LANE:(j + 1) * LANE]
            hp = h_ref[j]
            hs_ref[0, 0, j] = hp
            rhs = jnp.concatenate([(xp * _lane_pick(lo, dt, k0)).astype(BF16), hp.astype(BF16)], axis=0)
            lm_s = jnp.exp(jnp.where(tri_s, _col_s(cs, k0) - _row_s(cst, k0, first), NEG))
            lhs = jnp.concatenate([lm_s * cb_s, cg_s * _col_s(ecs, k0)], axis=1).astype(BF16)
            y_s = _dot(lhs, rhs)
            s_s = _dot((bgt_s * _row_s(dect, k0, first)).astype(BF16), rhs[:T])
            dsk = jnp.where(lo[0:1, :], par_ref[2:3, k0:k0 + 1], par_ref[2:3, k1:k1 + 1])
            y_ref[:, j * LANE:(j + 1) * LANE] = jnp.where(lo, y_s[:T], y_s[T:]) + dsk * xp
            et = jnp.where(lo[0:1, :], etot[:, k0:k0 + 1], etot[:, k1:k1 + 1])
            h_ref[j] = hp * et + jnp.where(lo, s_s[:T], s_s[T:])

    return pl.pallas_call(
        body, name=name, grid=(SSD_GROUPS, nc),
        in_specs=[pl.BlockSpec((T, gw), lambda g, c: (c, g)),
                  pl.BlockSpec((T, SSD_STATE), lambda g, c: (c, g)),
                  pl.BlockSpec((T, SSD_STATE), lambda g, c: (c, g)),
                  pl.BlockSpec((T, LANE), lambda g, c: (c, g + dtoff)),
                  pl.BlockSpec((8, LANE), lambda g, c: (0, g))],
        out_specs=(pl.BlockSpec((T, gw), lambda g, c: (c, g)),
                   pl.BlockSpec((1, 1, PAIRS_PER_GROUP, SSD_STATE, LANE), lambda g, c: (g, c, 0, 0, 0))),
        out_shape=(jax.ShapeDtypeStruct((n_rows, SSD_D_INNER), F32),
                   jax.ShapeDtypeStruct((SSD_GROUPS, nc, PAIRS_PER_GROUP, SSD_STATE, LANE), F32)),
        scratch_shapes=[pltpu.VMEM((PAIRS_PER_GROUP, SSD_STATE, LANE), F32)],
        compiler_params=_cparams("parallel", "arbitrary"),
    )(xs, bm, cm, dtr, par)


def _ssd_bwd_stacked(xs, bm, cm, dtr, par, hs, dy, name, into):
    dtr, dt0, _ = _window(dtr)
    dtoff = dt0 // LANE
    n_rows = xs.shape[0]
    nc = n_rows // T
    gw = PAIRS_PER_GROUP * LANE

    extra, extra_specs, aliases, col0, width = _into(into, 7, 3)
    ddoff = col0 // LANE

    def body(*refs):
        x_ref, b_ref, c_ref, dtr_ref, par_ref, hs_ref, dy_ref = refs[:7]
        dx_ref, db_ref, dc_ref, ddtr_ref, acc_ref, dh_ref = refs[-6:]
        @pl.when(pl.program_id(1) == 0)
        def _():
            dh_ref[...] = jnp.zeros_like(dh_ref)
            acc_ref[...] = jnp.zeros_like(acc_ref)

        dt, a, cs, cst, ri, ci = _ssd_scalars(dtr_ref, par_ref)
        tri_s, trit_s, first = _ssd_stacked_masks()
        lane = lax.broadcasted_iota(jnp.int32, (T, LANE), 1)
        lo = lane < SSD_CHUNK // 2
        lane1 = lane[0:1, :]
        ecs = jnp.exp(cs)
        ecst = jnp.exp(cst)
        dec = jnp.exp(cs[T - 1:T, :] - cs)
        etot = jnp.exp(cs[T - 1:T, :])
        bg = b_ref[...]
        cg = c_ref[...]
        bg_b = bg.astype(BF16)
        cg_b = cg.astype(BF16)
        cb = _dot(cg_b, bg_b, _NT)
        cbt = _dot(bg_b, cg_b, _NT)
        cb_s = jnp.concatenate([cb, cb], axis=0)
        cbt_s = jnp.concatenate([cbt, cbt], axis=0)
        bg_s = jnp.concatenate([bg, bg], axis=0)
        cg_s = jnp.concatenate([cg, cg], axis=0)
        cgt_s = jnp.concatenate([cg.T, cg.T], axis=0)
        dbg = jnp.zeros((T, SSD_STATE), F32)
        dcg = jnp.zeros((T, SSD_STATE), F32)
        dcs_acc = jnp.zeros((T, LANE), F32)
        ddt_acc = jnp.zeros((T, LANE), F32)
        dsk_acc = jnp.zeros((1, LANE), F32)
        last_row = lax.broadcasted_iota(jnp.int32, (T, 1), 0) == T - 1
        for j in range(PAIRS_PER_GROUP):
            k0, k1 = 2 * j, 2 * j + 1
            xp = x_ref[:, j * LANE:(j + 1) * LANE]
            dtl = _lane_pick(lo, dt, k0)
            xdt = xp * dtl
            hp = hs_ref[0, 0, j]
            dhn = dh_ref[j]
            dyp = dy_ref[:, j * LANE:(j + 1) * LANE]
            xdt_b, hp_b, dhn_b, dyp_b = (v.astype(BF16) for v in (xdt, hp, dhn, dyp))
            cs_c, cs_r = _col_s(cs, k0), _row_s(cst, k0, first)
            lm_s = jnp.exp(jnp.where(tri_s, cs_c - cs_r, NEG))
            lmt_s = jnp.exp(jnp.where(trit_s, cs_r - cs_c, NEG))
            dec_c, ecs_c = _col_s(dec, k0), _col_s(ecs, k0)
            r1 = _dot(_stack_pair(dyp, lo), jnp.concatenate([xdt_b, hp_b], axis=0), _NT)
            r2 = _dot(_stack_pair(xdt, lo), jnp.concatenate([dyp_b, dhn_b], axis=0), _NT)
            dm_s, dyh_s = r1[:, :T], r1[:, T:]
            dmt_s, xdh_s = r2[:, :T], r2[:, T:]
            mm_s = lm_s * cb_s
            mmt_s = lmt_s * cbt_s
            bdec_s = bg_s * dec_c
            cexp_s = cg_s * ecs_c
            dx_s = _dot(jnp.concatenate([mmt_s, bdec_s], axis=1).astype(BF16),
                        jnp.concatenate([dyp_b, dhn_b], axis=0))
            dxdt = jnp.where(lo, dx_s[:T], dx_s[T:])
            dc_s = _dot((dm_s * lm_s).astype(BF16), bg_b) + dyh_s * ecs_c
            db_s = _dot((dmt_s * lmt_s).astype(BF16), cg_b) + xdh_s * dec_c
            dcg = dcg + dc_s[:T] + dc_s[T:]
            dbg = dbg + db_s[:T] + db_s[T:]
            dh_s = _dot((cgt_s * _row_s(ecst, k0, first)).astype(BF16), dyp_b)
            et = jnp.where(lo[0:1, :], etot[:, k0:k0 + 1], etot[:, k1:k1 + 1])
            dh_ref[j] = dhn * et + jnp.where(lo, dh_s[:T], dh_s[T:])
            e4 = jnp.sum(bdec_s * xdh_s, axis=-1, keepdims=True)
            dcs_s = (jnp.sum(dm_s * mm_s, axis=-1, keepdims=True) - jnp.sum(dmt_s * mmt_s, axis=-1, keepdims=True)
                     + jnp.sum(cexp_s * dyh_s, axis=-1, keepdims=True) - e4)
            hd = hp * dhn
            tsum0 = jnp.sum(e4[:T]) + etot[:, k0:k0 + 1] * jnp.sum(jnp.where(lo, hd, 0.0))
            tsum1 = jnp.sum(e4[T:]) + etot[:, k1:k1 + 1] * jnp.sum(jnp.where(lo, 0.0, hd))
            dcs0 = dcs_s[:T] + jnp.where(last_row, tsum0, 0.0)
            dcs1 = dcs_s[T:] + jnp.where(last_row, tsum1, 0.0)
            dcs_acc = jnp.where(lane == k0, dcs0, jnp.where(lane == k1, dcs1, dcs_acc))
            prod = dxdt * xp
            ddt_lo = jnp.sum(jnp.where(lo, prod, 0.0), axis=-1, keepdims=True)
            ddt_hi = jnp.sum(jnp.where(lo, 0.0, prod), axis=-1, keepdims=True)
            ddt_acc = jnp.where(lane == k0, ddt_lo, jnp.where(lane == k1, ddt_hi, ddt_acc))
            dyx = dyp * xp
            dsk_acc = jnp.where(lane1 == k0, jnp.sum(jnp.where(lo, dyx, 0.0)),
                                jnp.where(lane1 == k1, jnp.sum(jnp.where(lo, 0.0, dyx)), dsk_acc))
            dsk = jnp.where(lo[0:1, :], par_ref[2:3, k0:k0 + 1], par_ref[2:3, k1:k1 + 1])
            dx_ref[:, j * LANE:(j + 1) * LANE] = dxdt * dtl + dsk * dyp
        db_ref[...] = dbg
        dc_ref[...] = dcg
        triu = (ci >= ri).astype(F32)
        dda = _dot_hi(triu, dcs_acc)
        ddt = ddt_acc + dda * a
        ddtr = ddt * _sig(dtr_ref[...] + par_ref[0:1, :])
        ddtr_ref[...] = ddtr.astype(BF16)
        acc_ref[0:1, :] += jnp.sum(ddtr, axis=0, keepdims=True)
        acc_ref[1:2, :] += jnp.sum(dda * dt, axis=0, keepdims=True) * a
        acc_ref[2:3, :] += dsk_acc

    def rev(g, c):
        return (nc - 1 - c, g)

    return pl.pallas_call(
        body, name=name, grid=(SSD_GROUPS, nc),
        in_specs=[pl.BlockSpec((T, gw), rev),
                  pl.BlockSpec((T, SSD_STATE), rev),
                  pl.BlockSpec((T, SSD_STATE), rev),
                  pl.BlockSpec((T, LANE), lambda g, c: (nc - 1 - c, g + dtoff)),
                  pl.BlockSpec((8, LANE), lambda g, c: (0, g)),
                  pl.BlockSpec((1, 1, PAIRS_PER_GROUP, SSD_STATE, LANE), lambda g, c: (g, nc - 1 - c, 0, 0, 0)),
                  pl.BlockSpec((T, gw), rev)] + extra_specs,
        out_specs=(pl.BlockSpec((T, gw), rev),
                   pl.BlockSpec((T, SSD_STATE), rev),
                   pl.BlockSpec((T, SSD_STATE), rev),
                   pl.BlockSpec((T, LANE), lambda g, c: (nc - 1 - c, g + ddoff)),
                   pl.BlockSpec((8, LANE), lambda g, c: (0, g))),
        out_shape=(jax.ShapeDtypeStruct((n_rows, SSD_D_INNER), F32),
                   jax.ShapeDtypeStruct((n_rows, BC_DIM), F32),
                   jax.ShapeDtypeStruct((n_rows, BC_DIM), F32),
                   jax.ShapeDtypeStruct((n_rows, width), BF16),
                   jax.ShapeDtypeStruct((8, DT_PAD), F32)),
        input_output_aliases=aliases,
        scratch_shapes=[pltpu.VMEM((PAIRS_PER_GROUP, SSD_STATE, LANE), F32)],
        compiler_params=_cparams("parallel", "arbitrary"),
    )(xs, bm, cm, dtr, par, hs, dy, *extra)


def _cumsum_mm(mat, x, left=True):
    hi = x.astype(BF16)
    r = x - hi.astype(F32)
    mid = r.astype(BF16)
    lo = (r - mid.astype(F32)).astype(BF16)
    if not left:
        return _dot(jnp.concatenate([hi, mid, lo], axis=1), jnp.concatenate([mat, mat, mat], axis=0))
    w = x.shape[1]
    out = _dot(mat, jnp.concatenate([hi, mid, lo], axis=1))
    return out[:, :w] + out[:, w:2 * w] + out[:, 2 * w:]


def _ssd_prep(dtr_ref, par_ref):
    dt = _softplus(dtr_ref[...] + par_ref[0:1, :])
    a = -jnp.exp(par_ref[1:2, :])
    ri = lax.broadcasted_iota(jnp.int32, (T, T), 0)
    ci = lax.broadcasted_iota(jnp.int32, (T, T), 1)
    cs = _cumsum_mm((ri >= ci).astype(BF16), dt * a)
    lo = lax.broadcasted_iota(jnp.int32, (T, LANE), 1) < SSD_CHUNK // 2

    def expand_row(arr):
        return jnp.concatenate([jnp.where(lo[:1], arr[:, 2 * j:2 * j + 1], arr[:, 2 * j + 1:2 * j + 2])
                                for j in range(PAIRS_PER_GROUP)], axis=1)

    gw = PAIRS_PER_GROUP * LANE
    eh = lax.broadcasted_iota(jnp.int32, (LANE, gw), 0)
    ep = lax.broadcasted_iota(jnp.int32, (LANE, gw), 1)
    both = _cumsum_mm(jnp.where(eh * (SSD_CHUNK // 2) == (ep & -(SSD_CHUNK // 2)), 1.0, 0.0).astype(BF16),
                      jnp.concatenate([dt, cs], axis=0), left=False)
    dt_x, cs_x = both[:T], both[T:]
    tot_x = cs_x[T - 1:T, :]
    return {"dt": dt, "a": a, "cs": cs, "cst": cs.T, "lo": lo, "ri": ri, "ci": ci, "cs_x": cs_x,
            "dt_x": dt_x, "ecs_x": jnp.exp(cs_x), "dec_x": jnp.exp(tot_x - cs_x),
            "et_x": jnp.exp(tot_x), "dsk_x": expand_row(par_ref[2:3, :])}


def _wide_masks():
    r = lax.broadcasted_iota(jnp.int32, (T, 2 * T), 0)
    l = lax.broadcasted_iota(jnp.int32, (T, 2 * T), 1)
    s = l & (T - 1)
    return r >= s, s >= r, l < T


def _wide_cs(q, k0, even):
    cst, lo = q["cst"], q["lo"]
    t = q["cs_x"][:, (k0 // 2) * LANE:(k0 // 2 + 1) * LANE]
    tr = pltpu.roll(t, SSD_CHUNK // 2, axis=1)
    col = jnp.concatenate([jnp.where(lo, t, tr), jnp.where(lo, tr, t)], axis=1)
    row = jnp.concatenate([cst[k0:k0 + 1, :], cst[k0 + 1:k0 + 2, :]], axis=1)
    return col, row


def _ssd_fwd(xs, bm, cm, dtr, par, name):
    dtr, dt0, _ = _window(dtr)
    dtoff = dt0 // LANE
    n_rows = xs.shape[0]
    nc = n_rows // T
    gw = PAIRS_PER_GROUP * LANE

    def body(x_ref, b_ref, c_ref, dtr_ref, par_ref, y_ref, hs_ref, h_ref):
        @pl.when(pl.program_id(1) == 0)
        def _():
            h_ref[...] = jnp.zeros_like(h_ref)

        q = _ssd_prep(dtr_ref, par_ref)
        lo = q["lo"]
        tri_w, _, even = _wide_masks()
        bg_b = b_ref[...].astype(BF16)
        cg_b = c_ref[...].astype(BF16)
        xv = x_ref[...]
        xdt = xv * q["dt_x"]
        h = h_ref[...]
        hs_ref[0, 0] = h
        yo = q["ecs_x"] * _dot(cg_b, h.astype(BF16))
        h_ref[...] = h * q["et_x"] + _dot(b_ref[...].T.astype(BF16), (xdt * q["dec_x"]).astype(BF16))
        cb = _dot(cg_b, bg_b, _NT)
        cb_w = jnp.concatenate([cb, cb], axis=1)
        for j in range(PAIRS_PER_GROUP):
            col, row = _wide_cs(q, 2 * j, even)
            m_w = (jnp.exp(jnp.where(tri_w, col - row, NEG)) * cb_w).astype(BF16)
            sl = slice(j * LANE, (j + 1) * LANE)
            y_ref[:, sl] = (_dot(m_w, _stack_pair(xdt[:, sl], lo)) + yo[:, sl] + q["dsk_x"][:, sl] * xv[:, sl])

    return pl.pallas_call(
        body, name=name, grid=(SSD_GROUPS, nc),
        in_specs=[pl.BlockSpec((T, gw), lambda g, c: (c, g)),
                  pl.BlockSpec((T, SSD_STATE), lambda g, c: (c, g)),
                  pl.BlockSpec((T, SSD_STATE), lambda g, c: (c, g)),
                  pl.BlockSpec((T, LANE), lambda g, c: (c, g + dtoff)),
                  pl.BlockSpec((8, LANE), lambda g, c: (0, g))],
        out_specs=(pl.BlockSpec((T, gw), lambda g, c: (c, g)),
                   pl.BlockSpec((1, 1, SSD_STATE, gw), lambda g, c: (g, c, 0, 0))),
        out_shape=(jax.ShapeDtypeStruct((n_rows, SSD_D_INNER), F32),
                   jax.ShapeDtypeStruct((SSD_GROUPS, nc, SSD_STATE, gw), F32)),
        scratch_shapes=[pltpu.VMEM((SSD_STATE, gw), F32)],
        compiler_params=_cparams("parallel", "arbitrary"),
    )(xs, bm, cm, dtr, par)


def _ssd_bwd(xs, bm, cm, dtr, par, hs, dy, name, into):
    dtr, dt0, _ = _window(dtr)
    dtoff = dt0 // LANE
    n_rows = xs.shape[0]
    nc = n_rows // T
    gw = PAIRS_PER_GROUP * LANE
    extra, extra_specs, aliases, col0, width = _into(into, 7, 3)
    ddoff = col0 // LANE

    def body(*refs):
        x_ref, b_ref, c_ref, dtr_ref, par_ref, hs_ref, dy_ref = refs[:7]
        dx_ref, db_ref, dc_ref, ddtr_ref, acc_ref, dh_ref = refs[-6:]

        @pl.when(pl.program_id(1) == 0)
        def _():
            dh_ref[...] = jnp.zeros_like(dh_ref)
            acc_ref[...] = jnp.zeros_like(acc_ref)

        q = _ssd_prep(dtr_ref, par_ref)
        lo, dt, a = q["lo"], q["dt"], q["a"]
        tri_w, trit_w, even = _wide_masks()
        lane = lax.broadcasted_iota(jnp.int32, (T, LANE), 1)
        lane1 = lane[0:1, :]
        last_row = lax.broadcasted_iota(jnp.int32, (T, 1), 0) == T - 1
        bg_b = b_ref[...].astype(BF16)
        cg_b = c_ref[...].astype(BF16)
        xv = x_ref[...]
        dyv = dy_ref[...]
        xdt = xv * q["dt_x"]
        h = hs_ref[0, 0]
        dhn = dh_ref[...]
        h_b, dhn_b = h.astype(BF16), dhn.astype(BF16)
        yo = q["ecs_x"] * _dot(cg_b, h_b)
        bdh = q["dec_x"] * _dot(bg_b, dhn_b)
        dye = (dyv * q["ecs_x"]).astype(BF16)
        xd = (xdt * q["dec_x"]).astype(BF16)
        dcg = _dot(dye, h_b, _NT)
        dbg = _dot(xd, dhn_b, _NT)
        dh_ref[...] = dhn * q["et_x"] + _dot(c_ref[...].T.astype(BF16), dye)
        e4_all = xdt * bdh
        f_all = dyv * yo - e4_all
        tot_row = jnp.sum(e4_all, axis=0, keepdims=True) + q["et_x"] * jnp.sum(h * dhn, axis=0, keepdims=True)
        dsk_row = jnp.sum(dyv * xv, axis=0, keepdims=True)
        cb = _dot(cg_b, bg_b, _NT)
        cbt = _dot(bg_b, cg_b, _NT)
        cb_w = jnp.concatenate([cb, cb], axis=1)
        cbt_w = jnp.concatenate([cbt, cbt], axis=1)
        dcb = jnp.zeros((T, T), F32)
        dcbt = jnp.zeros((T, T), F32)
        dcs_acc = jnp.zeros((T, LANE), F32)
        ddt_acc = jnp.zeros((T, LANE), F32)
        dsk_acc = jnp.zeros((1, LANE), F32)
        tot_acc = jnp.zeros((1, LANE), F32)
        ind_r = lax.broadcasted_iota(jnp.int32, (2 * T, LANE), 0)
        ind_l = lax.broadcasted_iota(jnp.int32, (2 * T, LANE), 1)

        def halves(t):
            return (jnp.sum(jnp.where(lo[0:1], t, 0.0), axis=-1, keepdims=True),
                    jnp.sum(jnp.where(lo[0:1], 0.0, t), axis=-1, keepdims=True))

        def split2(t):
            hi = t.astype(BF16)
            return jnp.concatenate([hi, (t - hi.astype(F32)).astype(BF16)], axis=1)

        for j in range(PAIRS_PER_GROUP):
            k0, k1 = 2 * j, 2 * j + 1
            sl = slice(j * LANE, (j + 1) * LANE)
            col, row = _wide_cs(q, k0, even)
            lm_w = jnp.exp(jnp.where(tri_w, col - row, NEG))
            lmt_w = jnp.exp(jnp.where(trit_w, row - col, NEG))
            dyp, xp = dyv[:, sl], xdt[:, sl]
            dym, xm = _stack_pair(dyp, lo), _stack_pair(xp, lo)
            dm_w = _dot(dyp.astype(BF16), xm, _NT)
            dmt_w = _dot(xp.astype(BF16), dym, _NT)
            mm_w = lm_w * cb_w
            mmt_w = lmt_w * cbt_w
            dxdt = _dot(mmt_w.astype(BF16), dym) + bdh[:, sl]
            g1 = dm_w * lm_w
            g2 = dmt_w * lmt_w
            dcb = dcb + g1[:, :T] + g1[:, T:]
            dcbt = dcbt + g2[:, :T] + g2[:, T:]
            ind_w = jnp.where(ind_l == jnp.where(ind_r < T, k0, k1), 1.0, 0.0).astype(BF16)
            ind_p = jnp.where(ind_l[:T] == jnp.where(ind_r[:T] < SSD_CHUNK // 2, k0, k1), 1.0, 0.0).astype(BF16)
            dcs_acc = dcs_acc + _dot(
                jnp.concatenate([split2(dm_w * mm_w - dmt_w * mmt_w), split2(f_all[:, sl])], axis=1),
                jnp.concatenate([ind_w, ind_w, ind_p, ind_p], axis=0))
            ddt_acc = ddt_acc + _dot(split2(dxdt * xv[:, sl]), jnp.concatenate([ind_p, ind_p], axis=0))
            tot2 = halves(tot_row[:, sl])
            tot_acc = jnp.where(lane1 == k0, tot2[0], jnp.where(lane1 == k1, tot2[1], tot_acc))
            dsk2 = halves(dsk_row[:, sl])
            dsk_acc = jnp.where(lane1 == k0, dsk2[0], jnp.where(lane1 == k1, dsk2[1], dsk_acc))
            dx_ref[:, sl] = dxdt * q["dt_x"][:, sl] + q["dsk_x"][:, sl] * dyp
        dcs_acc = dcs_acc + jnp.where(last_row, tot_acc, 0.0)
        dc_ref[...] = dcg + _dot(dcb.astype(BF16), bg_b)
        db_ref[...] = dbg + _dot(dcbt.astype(BF16), cg_b)
        dda = _cumsum_mm((q["ci"] >= q["ri"]).astype(BF16), dcs_acc)
        ddt = ddt_acc + dda * a
        ddtr = ddt * _sig(dtr_ref[...] + par_ref[0:1, :])
        ddtr_ref[...] = ddtr.astype(BF16)
        acc_ref[0:1, :] += jnp.sum(ddtr, axis=0, keepdims=True)
        acc_ref[1:2, :] += jnp.sum(dda * dt, axis=0, keepdims=True) * a
        acc_ref[2:3, :] += dsk_acc

    def rev(g, c):
        return (nc - 1 - c, g)

    return pl.pallas_call(
        body, name=name, grid=(SSD_GROUPS, nc),
        in_specs=[pl.BlockSpec((T, gw), rev),
                  pl.BlockSpec((T, SSD_STATE), rev),
                  pl.BlockSpec((T, SSD_STATE), rev),
                  pl.BlockSpec((T, LANE), lambda g, c: (nc - 1 - c, g + dtoff)),
                  pl.BlockSpec((8, LANE), lambda g, c: (0, g)),
                  pl.BlockSpec((1, 1, SSD_STATE, gw), lambda g, c: (g, nc - 1 - c, 0, 0)),
                  pl.BlockSpec((T, gw), rev)] + extra_specs,
        out_specs=(pl.BlockSpec((T, gw), rev),
                   pl.BlockSpec((T, SSD_STATE), rev),
                   pl.BlockSpec((T, SSD_STATE), rev),
                   pl.BlockSpec((T, LANE), lambda g, c: (nc - 1 - c, g + ddoff)),
                   pl.BlockSpec((8, LANE), lambda g, c: (0, g))),
        out_shape=(jax.ShapeDtypeStruct((n_rows, SSD_D_INNER), F32),
                   jax.ShapeDtypeStruct((n_rows, BC_DIM), F32),
                   jax.ShapeDtypeStruct((n_rows, BC_DIM), F32),
                   jax.ShapeDtypeStruct((n_rows, width), BF16),
                   jax.ShapeDtypeStruct((8, DT_PAD), F32)),
        input_output_aliases=aliases,
        scratch_shapes=[pltpu.VMEM((SSD_STATE, gw), F32)],
        compiler_params=_cparams("parallel", "arbitrary"),
    )(xs, bm, cm, dtr, par, hs, dy, *extra)


ADAM_ROWS = 256


def _adamw(lands, w, m, v, name):
    na = len(lands)
    n_slots, r, wd = lands[0].shape
    tr = r if r <= 2 * ADAM_ROWS else ADAM_ROWS
    nj = r // tr
    bc1 = 1.0 - ADAM_B1 ** ADAM_STEP
    bc2 = 1.0 - ADAM_B2 ** ADAM_STEP

    def body(*refs):
        l_refs = refs[:na]
        w_ref, m_ref, v_ref, g_ref, d_ref, nm_ref, nv_ref = refs[na:]
        for a in range(na):
            @pl.when(pl.program_id(0) == a)
            def _(l_ref=l_refs[a]):
                g = l_ref[0].astype(F32)
                for s in range(1, n_slots):
                    g = g + l_ref[s].astype(F32)
                mn = ADAM_B1 * m_ref[0] + (1.0 - ADAM_B1) * g
                vn = ADAM_B2 * v_ref[0] + (1.0 - ADAM_B2) * (g * g)
                mh = mn / bc1
                vh = vn / bc2
                g_ref[0] = g
                nm_ref[0] = mn
                nv_ref[0] = vn
                d_ref[0] = -ADAM_LR * (mh / (jnp.sqrt(vh) + ADAM_EPS) + ADAM_WD * w_ref[0])

    def land_spec(a):
        return pl.BlockSpec((n_slots, tr, wd),
                            lambda i, j: (0, jnp.where(i == a, j, jnp.where(i < a, 0, nj - 1)), 0))

    blk = pl.BlockSpec((1, tr, wd), lambda i, j: (i, j, 0))
    shp = jax.ShapeDtypeStruct((na, r, wd), F32)
    return pl.pallas_call(
        body, name=name, grid=(na, nj), in_specs=[land_spec(a) for a in range(na)] + [blk, blk, blk],
        out_specs=(blk, blk, blk, blk), out_shape=(shp, shp, shp, shp),
        compiler_params=_cparams("arbitrary", "arbitrary"),
    )(*lands, w, m, v)


def _mesh_pos():
    return lax.axis_index("x"), lax.axis_index("y"), lax.axis_index("c")


def _peer(pos, k):
    x, y, c = pos
    px = 1 - x if (k >> 2) & 1 else x
    py = 1 - y if (k >> 1) & 1 else y
    pc = 1 - c if k & 1 else c
    return px, py, pc


def _flat(pos):
    return 4 * pos[0] + 2 * pos[1] + pos[2]


HBM_SPEC = pl.BlockSpec(memory_space=pl.ANY)


ROW_SHARDED = ("w_ssd_out", "w_att_out", "w_mix_out", "w_ffn_down")
COL_SHARDED = ("w_in", "w_ffn_gate", "w_ffn_up")
GATHERED = ROW_SHARDED + COL_SHARDED + ("conv_w",)
BIG = ROW_SHARDED + COL_SHARDED


SEM_SPEC = pl.BlockSpec(memory_space=pltpu.SEMAPHORE)
TOKEN = jax.ShapeDtypeStruct((8, LANE), F32)
SPLIT_EFFECT = pltpu.SideEffectType.DATAFLOW_SIDE_EFFECTING
GATHER_ROWS = "gather_rows"
GATHER_SLOT = "gather_slot"
SCATTER_ROWS = "scatter_rows"
SCATTER_SLOT = "scatter_slot"


def _land_shape(kind, src):
    if kind == GATHER_ROWS:
        return (N_DEV * src.shape[0],) + src.shape[1:]
    if kind == GATHER_SLOT:
        return (N_DEV,) + src.shape
    if kind == SCATTER_ROWS:
        return (N_DEV, src.shape[0] // N_DEV) + src.shape[1:]
    return src.shape


def _views(kind, src_ref, land_ref, pos, k):
    me = _flat(pos)
    if kind == GATHER_ROWS:
        r = src_ref.shape[0]
        return src_ref, land_ref.at[pl.ds(pl.multiple_of(me * r, 16), r), :]
    if kind == GATHER_SLOT:
        return src_ref, land_ref.at[me]
    dev = _flat(_peer(pos, k))
    if kind == SCATTER_ROWS:
        r = land_ref.shape[1]
        return src_ref.at[pl.ds(pl.multiple_of(dev * r, 16), r), :], land_ref.at[k]
    return src_ref.at[dev], land_ref.at[k]


def _hbm(x):
    return pltpu.with_memory_space_constraint(x, pltpu.HBM)


def _exchange_start(items, after, name):
    kinds = [k for k, _ in items]
    srcs = [_hbm(s) for _, s in items]
    lands = [_hbm(lax.empty(_land_shape(k, s), s.dtype)) for k, s in items]
    n = len(items)
    n_copy = n * (N_DEV - 1)

    def body(*refs):
        src_refs, land_refs = refs[:n], refs[n:2 * n]
        send_sems, recv_sems = refs[2 * n + 1], refs[2 * n + 2]
        token_ref = refs[4 * n + 3]
        pos = _mesh_pos()
        for i, kind in enumerate(kinds):
            for k in range(1, N_DEV):
                s, d = _views(kind, src_refs[i], land_refs[i], pos, k)
                j = i * (N_DEV - 1) + k - 1
                pltpu.make_async_remote_copy(src_ref=s, dst_ref=d, send_sem=send_sems.at[j], recv_sem=recv_sems.at[j],
                                             device_id=_peer(pos, k), device_id_type=MESH_ID).start()
        token_ref[...] = jnp.zeros_like(token_ref)

    arrs = srcs + lands
    outs = pl.pallas_call(
        body, name=name,
        in_specs=[HBM_SPEC] * (2 * n + 1),
        out_specs=[SEM_SPEC, SEM_SPEC] + [HBM_SPEC] * (2 * n) + [pl.BlockSpec(memory_space=pltpu.VMEM)],
        out_shape=[pltpu.SemaphoreType.DMA((n_copy,)), pltpu.SemaphoreType.DMA((n_copy,))]
        + [pltpu.HBM(a.shape, a.dtype) for a in arrs] + [TOKEN],
        input_output_aliases={i: 2 + i for i in range(2 * n)},
        compiler_params=pltpu.CompilerParams(has_side_effects=SPLIT_EFFECT),
    )(*arrs, after)
    return {"kinds": kinds, "send": outs[0], "recv": outs[1], "arrs": outs[2:2 + 2 * n], "token": outs[-1]}


def _exchange_wait(ex, after, name):
    kinds = ex["kinds"]
    n = len(kinds)

    def body(*refs):
        src_refs, land_refs = refs[:n], refs[n:2 * n]
        send_sems, recv_sems = refs[2 * n], refs[2 * n + 1]
        token_ref = refs[-1]
        pos = _mesh_pos()
        for i, kind in enumerate(kinds):
            for k in range(1, N_DEV):
                s, d = _views(kind, src_refs[i], land_refs[i], pos, k)
                j = i * (N_DEV - 1) + k - 1
                cp = pltpu.make_async_remote_copy(src_ref=s, dst_ref=d, send_sem=send_sems.at[j],
                                                  recv_sem=recv_sems.at[j], device_id=_peer(pos, k),
                                                  device_id_type=MESH_ID)
                cp.wait_send()
                cp.wait_recv()
        token_ref[...] = jnp.zeros_like(token_ref)

    outs = pl.pallas_call(
        body, name=name,
        in_specs=[HBM_SPEC] * (2 * n) + [SEM_SPEC, SEM_SPEC, HBM_SPEC],
        out_specs=[HBM_SPEC] * (2 * n) + [pl.BlockSpec(memory_space=pltpu.VMEM)],
        out_shape=[pltpu.HBM(a.shape, a.dtype) for a in ex["arrs"]] + [TOKEN],
        input_output_aliases={i: i for i in range(2 * n)},
        compiler_params=pltpu.CompilerParams(has_side_effects=SPLIT_EFFECT),
    )(*ex["arrs"], ex["send"], ex["recv"], after)
    lands = [_place_own(k, s, d) for k, s, d in zip(kinds, outs[:n], outs[n:2 * n])]
    return lands, outs[-1]


def _place_own(kind, src, land):
    me = _flat(_mesh_pos())
    zeros = (0,) * (src.ndim - 1)
    if kind == GATHER_ROWS:
        return lax.dynamic_update_slice(land, src, (me * src.shape[0],) + zeros)
    if kind == GATHER_SLOT:
        return lax.dynamic_update_slice(land, src[None], (me,) + (0,) * src.ndim)
    if kind == SCATTER_ROWS:
        r = land.shape[1]
        own = lax.dynamic_slice(src, (me * r,) + zeros, (r,) + src.shape[1:])
    else:
        own = lax.dynamic_index_in_dim(src, me, 0, keepdims=False)
    return lax.dynamic_update_slice(land, own[None], (0,) * land.ndim)


def _all_gather_small(x, name):
    r, w = x.shape

    def body(x_ref, out_ref, send_sems, recv_sems):
        pos = _mesh_pos()
        me = _flat(pos)
        copies = []
        for k in range(1, N_DEV):
            cp = pltpu.make_async_remote_copy(
                src_ref=x_ref, dst_ref=out_ref.at[me], send_sem=send_sems.at[k - 1], recv_sem=recv_sems.at[k - 1],
                device_id=_peer(pos, k), device_id_type=MESH_ID)
            cp.start()
            copies.append(cp)
        out_ref[me] = x_ref[...]
        for cp in copies:
            cp.wait()

    vmem = pl.BlockSpec(memory_space=pltpu.VMEM)
    return pl.pallas_call(
        body, name=name, in_specs=[vmem], out_specs=vmem,
        out_shape=jax.ShapeDtypeStruct((N_DEV, r, w), x.dtype),
        scratch_shapes=[pltpu.SemaphoreType.DMA((N_DEV - 1,)), pltpu.SemaphoreType.DMA((N_DEV - 1,))],
        compiler_params=pltpu.CompilerParams(has_side_effects=True),
    )(x)


def _cols(g, lo, hi):
    c = g.shape[-1]
    parts = []
    for d in range(N_DEV):
        a, b = max(lo, d * c), min(hi, (d + 1) * c)
        if a < b:
            parts.append(g[d, :, a - d * c:b - d * c])
    return parts[0] if len(parts) == 1 else jnp.concatenate(parts, axis=1)


def _col_chunks(g):
    c = g.shape[-1] // N_DEV
    return jnp.stack([g[:, d * c:(d + 1) * c] for d in range(N_DEV)])


IN_PART = ("w_in", "conv_w")
OUT_PART = ROW_SHARDED + ("w_ffn_gate", "w_ffn_up")


def _gather_items(w, names, l):
    items = []
    for n in names:
        blk = w[n][l] if n == "conv_w" else w[n][l].astype(BF16)
        items.append((GATHER_ROWS if n in ROW_SHARDED else GATHER_SLOT, blk))
    return items


def _scatter_items(grads, names):
    return [(SCATTER_ROWS, grads[n]) if n in ROW_SHARDED else (SCATTER_SLOT, _col_chunks(grads[n]))
            for n in names]


SMALL = ("ln_in_g", "ln_in_b", "conv_b", "dt_bias", "a_log", "d_skip", "ssd_norm_w", "att_sinks",
         "ln_mix_g", "ln_mix_b", "ln_ffn_g", "ln_ffn_b")


def _pack_small(vals):
    flat = jnp.concatenate([vals[n].reshape(-1) for n in SMALL])
    n = flat.shape[0]
    rows = -(-n // LANE)
    rows = -(-rows // 8) * 8
    return jnp.pad(flat, (0, rows * LANE - n)).reshape(rows, LANE)


def _unpack_small(buf, shapes):
    flat = buf.reshape(-1)
    off = 0
    out = {}
    for n in SMALL:
        cnt = math.prod(shapes[n])
        out[n] = flat[off:off + cnt].reshape(shapes[n])
        off += cnt
    return out


def _to_group_major(v):
    lead = v.shape[:-1]
    t = v.reshape(lead + (SSD_GROUPS, HEADS_PER_GROUP))
    t = jnp.pad(t, [(0, 0)] * len(lead) + [(0, 0), (0, LANE - HEADS_PER_GROUP)])
    return t.reshape(lead + (DT_PAD,))


def _from_group_major(v):
    lead = v.shape[:-1]
    return v.reshape(lead + (SSD_GROUPS, LANE))[..., :HEADS_PER_GROUP].reshape(lead + (SSD_HEADS,))


def _rows8(v):
    return jnp.pad(v, ((0, 8 - v.shape[0]), (0, 0)))


IN_OFFS = {"q": (0, 1024), "kv": (1024, 1280), "z": (1280, 3328), "xs": (3328, 5376), "b": (5376, 5888),
           "c": (5888, 6400), "dt": (6400, 6432), "gl": (6432, 8480)}
PIECES = ("q", "kv", "z", "xs", "b", "c", "dt", "gl")


CAT = ("z", "xs", "gl", "q", "b", "c", "dt", "kv")
CAT_WIDTH = {"q": 1024, "z": 2048, "xs": 2048, "gl": 2048, "b": 512, "c": 512, "kv": 256, "dt": DT_PAD}
CAT_OFF = {p: sum(CAT_WIDTH[q] for q in CAT[:i]) for i, p in enumerate(CAT)}
CAT_DIM = sum(CAT_WIDTH.values())
MAIN_DIM = CAT_OFF["kv"]


def _cat_w_in(g):
    pieces = {p: _cols(g, lo, hi) for p, (lo, hi) in IN_OFFS.items()}
    pieces["dt"] = _to_group_major(pieces["dt"])
    return jnp.concatenate([pieces[p] for p in CAT], axis=1)


def _uncat_dw_in(dw):
    pieces = {p: dw[:, CAT_OFF[p]:CAT_OFF[p] + CAT_WIDTH[p]] for p in CAT}
    pieces["dt"] = _from_group_major(pieces["dt"])
    return jnp.concatenate([pieces[p] for p in PIECES], axis=1)


def _params_out(W):
    p = {n: W[n] for n in ROW_SHARDED}
    for n in ("w_ffn_gate", "w_ffn_up"):
        p[n] = _cols(W[n], 0, FFN_HIDDEN)
    return p


def _params_in(l, W, sm):
    p = {"w_cat": _cat_w_in(W["w_in"])}
    cw = _cols(W["conv_w"], 0, SSD_D_INNER + 2 * BC_DIM)
    cb = sm["conv_b"][l]
    segs = {"xs": (0, 2048), "b": (2048, 2560), "c": (2560, 3072)}
    p["conv_w8"] = {s: _rows8(cw[:, lo:hi]) for s, (lo, hi) in segs.items()}
    p["conv_b8"] = {s: _rows8(cb[None, lo:hi]) for s, (lo, hi) in segs.items()}
    p["ssd_par"] = _rows8(jnp.stack([_to_group_major(sm["dt_bias"][l]), _to_group_major(sm["a_log"][l]),
                                     _to_group_major(sm["d_skip"][l])]))
    p["norm_w"] = sm["ssd_norm_w"][l]
    p["sinks8"] = _rows8(jnp.pad(sm["att_sinks"][l], (0, LANE - ATT_HEADS))[None])
    for n in ("ln_mix_g", "ln_mix_b", "ln_ffn_g", "ln_ffn_b"):
        p[n] = sm[n][l]
    return p


def _fwd_mixers(h0, p, l, dep=None):
    tag = f"l{l}_"
    a = {"h0": h0}
    proj = _mm(h0, p["w_cat"], "nn", tag + "proj", dep=dep)
    for pc in CAT:
        a[pc] = (proj, CAT_OFF[pc], CAT_WIDTH[pc])
    for s in ("xs", "b", "c"):
        a[s + "c"] = _conv_fwd(a[s], p["conv_w8"][s], p["conv_b8"][s], tag + "conv_" + s)
    a["y"], a["hs"] = _ssd_fwd(a["xsc"], a["bc"], a["cc"], a["dt"], p["ssd_par"], tag + "ssd_fwd")
    a["yn"] = _gnorm_fwd(a["y"], a["z"], p["norm_w"], tag + "gnorm")
    a["att"] = _att_fwd(a["q"], a["kv"], p["sinks8"], tag + "att_fwd")
    return a


def _fwd_out(a, p, l, dep=None):
    tag = f"l{l}_"
    h0 = a["h0"]
    a["ya"] = _mm(a["yn"], p["w_ssd_out"], "nn", tag + "ssd_out", dep=dep)
    a["yb"] = _mm(a["att"], p["w_att_out"], "nn", tag + "att_out", dep=dep)
    a["merged"] = _merge_fwd(a["gl"], a["ya"], a["yb"], tag + "merge")
    a["mix"] = _mm(a["merged"], p["w_mix_out"], "nn", tag + "mix_out")
    a["h1"] = _ln_fwd(h0, a["mix"], p["ln_mix_g"], p["ln_mix_b"], ALPHA, tag + "ln_mix")
    a["fg"], a["fu"], a["act"] = _ffn_in(a["h1"], p["w_ffn_gate"], p["w_ffn_up"], tag + "ffn_in")
    a["ffn"] = _mm(a["act"], p["w_ffn_down"], "nn", tag + "ffn_down")
    a["h2"] = _ln_fwd(a["h1"], a["ffn"], p["ln_ffn_g"], p["ln_ffn_b"], ALPHA, tag + "ln_ffn")
    return a


def _dw(x, dy, name, dep=None):
    return _mm(x, dy, "tn", name, out_dtype=BF16, dep=dep)


def _bwd_out(a, p, dh2, l, dep=None):
    tag = f"l{l}_b_"
    gw, gs = {}, {}
    du2, acc = _ln_bwd(a["h1"], a["ffn"], p["ln_ffn_g"], dh2, ALPHA, tag + "ln_ffn")
    gs["ln_ffn_g"], gs["ln_ffn_b"] = acc[0], acc[1]
    gw["w_ffn_down"] = _dw(a["act"], du2, tag + "dw_down", dep=dep)
    dfg, dfu = _ffn_dact(du2, p["w_ffn_down"], a["fg"], a["fu"], tag + "ffn_dact", dep=dep)
    gw["w_ffn_gate"] = _dw(a["h1"], dfg, tag + "dw_gate")
    gw["w_ffn_up"] = _dw(a["h1"], dfu, tag + "dw_up")
    dh1 = _mm(dfg, p["w_ffn_gate"], "nt", tag + "dh1_gate", add=du2, add_scale=ALPHA)
    dh1 = _mm(dfu, p["w_ffn_up"], "nt", tag + "dh1_up", add=dh1)
    du1, acc = _ln_bwd(a["h0"], a["mix"], p["ln_mix_g"], dh1, ALPHA, tag + "ln_mix")
    gs["ln_mix_g"], gs["ln_mix_b"] = acc[0], acc[1]
    gw["w_mix_out"] = _dw(a["merged"], du1, tag + "dw_mix")
    dmerged = _mm(du1, p["w_mix_out"], "nt", tag + "dmerged")
    dya, dyb, dproj = _merge_bwd(a["gl"], a["ya"], a["yb"], dmerged, tag + "merge",
                                 (None, CAT_OFF["gl"], MAIN_DIM))
    gw["w_ssd_out"] = _dw(a["yn"], dya, tag + "dw_ssd")
    gw["w_att_out"] = _dw(a["att"], dyb, tag + "dw_att")
    return {"du1": du1, "dya": dya, "dyb": dyb, "dproj": dproj}, gw, gs


def _bwd_mixers(a, p, carry, l, dep=None):
    tag = f"l{l}_b_"
    gs = {}
    du1, dproj = carry["du1"], carry["dproj"]

    def win(pc):
        return (dproj, CAT_OFF[pc], MAIN_DIM)

    dyn = _mm(carry["dya"], p["w_ssd_out"], "nt", tag + "dyn", dep=dep)
    datt = _mm(carry["dyb"], p["w_att_out"], "nt", tag + "datt", out_dtype=BF16, dep=dep)
    dproj, dkv, acc = _att_bwd(a["q"], a["kv"], p["sinks8"], a["att"], datt, tag + "att", win("q"))
    gs["att_sinks"] = acc[0, :ATT_HEADS]
    dy, dproj, acc = _gnorm_bwd(a["y"], a["z"], p["norm_w"], dyn, tag + "gnorm", win("z"))
    gs["ssd_norm_w"] = acc[0]
    dxs, dbm, dcm, dproj, acc = _ssd_bwd(a["xsc"], a["bc"], a["cc"], a["dt"], p["ssd_par"], a["hs"], dy,
                                         tag + "ssd", win("dt"))
    gs["dt_bias"], gs["a_log"], gs["d_skip"] = (_from_group_major(acc[i]) for i in range(3))
    dconv_w, dconv_b = [], []
    for s, dout in (("xs", dxs), ("b", dbm), ("c", dcm)):
        dc, acc = _conv_bwd_pre(a[s], p["conv_w8"][s], p["conv_b8"][s], dout, tag + "conv_pre_" + s)
        dconv_w.append(acc[:CONV_TAPS])
        dconv_b.append(acc[CONV_TAPS])
        dproj = _conv_bwd_in(dc, p["conv_w8"][s], tag + "conv_in_" + s, win(s))
    gconv = jnp.concatenate(dconv_w, axis=1)
    gs["conv_b"] = jnp.concatenate(dconv_b)
    w_main, w_kv = p["w_cat"][:, :MAIN_DIM], p["w_cat"][:, MAIN_DIM:]
    dw = jnp.concatenate([_dw(a["h0"], dproj, tag + "dw_in"), _dw(a["h0"], dkv, tag + "dw_in_kv")], axis=1)

    def grad_h0(dep=None):
        dh0 = _mm(dproj, w_main, "nt", tag + "dh0", add=du1, add_scale=ALPHA, dep=dep)
        return _mm(dkv, w_kv, "nt", tag + "dh0_kv", add=dh0)

    return grad_h0, _uncat_dw_in(dw), gconv, gs


def _step(x, target, w, m, v):
    x2 = x[0]
    t2 = target[0]
    tok = jnp.zeros(TOKEN.shape, TOKEN.dtype)

    ex = _exchange_start(_gather_items(w, IN_PART, 0), tok, "gather_l0_in_start")
    h = _ln_fwd(x2, None, w["ln_in_g"], w["ln_in_b"], 1.0, "ln_in")
    lands, tok = _exchange_wait(ex, h, "gather_l0_in_wait")
    p0 = _params_in(0, dict(zip(IN_PART, lands)), w)
    ex = _exchange_start(_gather_items(w, OUT_PART, 0) + _gather_items(w, IN_PART, 1), tok,
                         "gather_l0_out_l1_in_start")
    a0 = _fwd_mixers(h, p0, 0, dep=ex["token"])
    lands, tok = _exchange_wait(ex, a0["att"], "gather_l0_out_l1_in_wait")
    p0.update(_params_out(dict(zip(OUT_PART, lands))))
    p1 = _params_in(1, dict(zip(IN_PART, lands[len(OUT_PART):])), w)
    ex = _exchange_start(_gather_items(w, OUT_PART, 1), tok, "gather_l1_out_start")
    a0 = _fwd_out(a0, p0, 0, dep=ex["token"])
    lands, tok = _exchange_wait(ex, a0["h2"], "gather_l1_out_wait")
    p1.update(_params_out(dict(zip(OUT_PART, lands))))
    a1 = _fwd_out(_fwd_mixers(a0["h2"], p1, 1), p1, 1)

    sse, dh = _loss_fwd_bwd(a1["h2"], t2, "loss")
    loss = lax.psum(0.5 / D_MODEL * sse[0, 0], ("x", "y", "c"))

    carry, gw1, gs1 = _bwd_out(a1, p1, dh, 1)
    grad_h0, gw1["w_in"], gw1["conv_w"], gs = _bwd_mixers(a1, p1, carry, 1)
    dh = grad_h0()
    gs1.update(gs)
    ex1 = _exchange_start(_scatter_items(gw1, GATHERED), tok, "scatter_l1_start")
    carry, gw0, gs0 = _bwd_out(a0, p0, dh, 0, dep=ex1["token"])
    lands, tok = _exchange_wait(ex1, carry["dyb"], "scatter_l1_wait")
    land1 = dict(zip(GATHERED, lands))
    ex0 = _exchange_start(_scatter_items(gw0, OUT_PART), tok, "scatter_l0_out_start")
    grad_h0, gw0["w_in"], gw0["conv_w"], gs = _bwd_mixers(a0, p0, carry, 0, dep=ex0["token"])
    gs0.update(gs)
    lands, tok = _exchange_wait(ex0, gw0["w_in"], "scatter_l0_out_wait")
    land0 = dict(zip(OUT_PART, lands))
    ex0 = _exchange_start(_scatter_items(gw0, IN_PART), tok, "scatter_l0_in_start")
    dh = grad_h0(dep=ex0["token"])
    grad_x2, acc = _ln_bwd(x2, None, w["ln_in_g"], dh, 1.0, "ln_in_b")

    outs = [{} for _ in range(4)]

    def update(names):
        res = None
        for n in names:
            res = _adamw([land0[n], land1[n]], w[n], m[n], v[n], "adamw_" + n)
            for o, t in zip(outs, res):
                o[n] = t
        return res[1]

    update(OUT_PART)
    gsm = {"ln_in_g": acc[0], "ln_in_b": acc[1]}
    for n in SMALL[2:]:
        gsm[n] = jnp.stack([gs0[n], gs1[n]])
    small_shapes = {n: w[n].shape for n in SMALL}
    land_s = _all_gather_small(_pack_small(gsm), "small_grads_all_gather")
    res = _adamw([land_s], _pack_small(w)[None], _pack_small(m)[None], _pack_small(v)[None], "adamw_small")
    for o, t in zip(outs, res):
        o.update(_unpack_small(t[0], small_shapes))
    lands, _ = _exchange_wait(ex0, res[1], "scatter_l0_in_wait")
    land0.update(zip(IN_PART, lands))
    update(IN_PART)
    return loss, grad_x2[None], outs


WEIGHT_NAMES = ("ln_in_g", "ln_in_b", "w_in", "conv_w", "conv_b", "dt_bias", "a_log", "d_skip", "ssd_norm_w",
                "att_sinks", "w_ssd_out", "w_att_out", "w_mix_out", "ln_mix_g", "ln_mix_b", "w_ffn_gate",
                "w_ffn_up", "w_ffn_down", "ln_ffn_g", "ln_ffn_b")


def kernel(x, ln_in_g, ln_in_b, w_in, conv_w, conv_b, dt_bias, a_log, d_skip, ssd_norm_w, att_sinks, w_ssd_out, w_att_out, w_mix_out, ln_mix_g, ln_mix_b, w_ffn_gate, w_ffn_up, w_ffn_down, ln_ffn_g, ln_ffn_b, loss_target, m_ln_in_g, m_ln_in_b, m_w_in, m_conv_w, m_conv_b, m_dt_bias, m_a_log, m_d_skip, m_ssd_norm_w, m_att_sinks, m_w_ssd_out, m_w_att_out, m_w_mix_out, m_ln_mix_g, m_ln_mix_b, m_w_ffn_gate, m_w_ffn_up, m_w_ffn_down, m_ln_ffn_g, m_ln_ffn_b, v_ln_in_g, v_ln_in_b, v_w_in, v_conv_w, v_conv_b, v_dt_bias, v_a_log, v_d_skip, v_ssd_norm_w, v_att_sinks, v_w_ssd_out, v_w_att_out, v_w_mix_out, v_ln_mix_g, v_ln_mix_b, v_w_ffn_gate, v_w_ffn_up, v_w_ffn_down, v_ln_ffn_g, v_ln_ffn_b):
    w = dict(zip(WEIGHT_NAMES, (ln_in_g, ln_in_b, w_in, conv_w, conv_b, dt_bias, a_log, d_skip, ssd_norm_w,
                                att_sinks, w_ssd_out, w_att_out, w_mix_out, ln_mix_g, ln_mix_b, w_ffn_gate,
                                w_ffn_up, w_ffn_down, ln_ffn_g, ln_ffn_b)))
    m = dict(zip(WEIGHT_NAMES, (m_ln_in_g, m_ln_in_b, m_w_in, m_conv_w, m_conv_b, m_dt_bias, m_a_log, m_d_skip,
                                m_ssd_norm_w, m_att_sinks, m_w_ssd_out, m_w_att_out, m_w_mix_out, m_ln_mix_g,
                                m_ln_mix_b, m_w_ffn_gate, m_w_ffn_up, m_w_ffn_down, m_ln_ffn_g, m_ln_ffn_b)))
    v = dict(zip(WEIGHT_NAMES, (v_ln_in_g, v_ln_in_b, v_w_in, v_conv_w, v_conv_b, v_dt_bias, v_a_log, v_d_skip,
                                v_ssd_norm_w, v_att_sinks, v_w_ssd_out, v_w_att_out, v_w_mix_out, v_ln_mix_g,
                                v_ln_mix_b, v_w_ffn_gate, v_w_ffn_up, v_w_ffn_down, v_ln_ffn_g, v_ln_ffn_b)))
    loss, grad_x, outs = _step(x, loss_target, w, m, v)
    result = [loss, grad_x]
    for o in outs:
        result.extend(o[n] for n in WEIGHT_NAMES)
    return tuple(result)
```

```python
import math

import jax
import jax.numpy as jnp
from jax import lax
from jax.experimental import pallas as pl
from jax.experimental.pallas import tpu as pltpu

F32 = jnp.float32
BF16 = jnp.bfloat16

D_MODEL = 1024
DEPTH = 2
N_DEV = 8
ATT_HEADS = 16
ATT_KV_HEADS = 2
ATT_HEAD_DIM = 64
ATT_BLOCK = 128
SSD_D_INNER = 2048
SSD_HEADS = 32
SSD_GROUPS = 4
SSD_STATE = 128
SSD_CHUNK = 128
FFN_HIDDEN = 2816
LN_EPS = 1e-5
RMS_EPS = 1e-5
ALPHA = (2 * DEPTH) ** 0.25
Q_DIM = 1024
BC_DIM = 512
DT_PAD = 512

ADAM_LR = 0.001
ADAM_B1 = 0.9
ADAM_B2 = 0.999
ADAM_EPS = 1e-08
ADAM_WD = 0.01
ADAM_STEP = 10

LANE = 128
VMEM_LIMIT = 48 * 1024 * 1024
NEG = -1e30

_NN = (((1,), (0,)), ((), ()))
_NT = (((1,), (1,)), ((), ()))
_TN = (((0,), (0,)), ((), ()))
MESH_ID = pl.DeviceIdType.MESH


def _dot(a, b, dims=_NN):
    return lax.dot_general(a, b, dims, preferred_element_type=F32)


def _sig(x):
    return 1.0 / (1.0 + jnp.exp(-x))


def _softplus(x):
    return jnp.maximum(x, 0.0) + jnp.log(1.0 + jnp.exp(-jnp.abs(x)))


def _cparams(*sem):
    return pltpu.CompilerParams(dimension_semantics=sem, vmem_limit_bytes=VMEM_LIMIT)


def _pick(n, cap):
    if n <= cap:
        return n
    best = None
    for t in range(LANE, cap + 1, LANE):
        if n % t == 0:
            best = t
    assert best is not None, (n, cap)
    return best


def _tile(n):
    if n <= 1024 or n % 1024 == 0:
        return min(n, 1024)
    return _pick(n, 1408)


def _rows(n):
    return min(512, n)


def _window(x):
    return x if isinstance(x, tuple) else (x, 0, x.shape[1])


def _into(into, n_in, out_idx):
    buf, col0, width = into
    if buf is None:
        return [], [], {}, col0, width
    return [buf], [pl.BlockSpec(memory_space=pl.ANY)], {n_in: out_idx}, col0, width


def _mm(a, b, mode, name, add=None, add_scale=1.0, out_dtype=F32, dep=None):
    if mode == "nn":
        m, k = a.shape
        n = b.shape[1]
    elif mode == "nt":
        m, k = a.shape
        n = b.shape[0]
    else:
        k, m = a.shape
        n = b.shape[1]
    tm = _tile(m)
    tn = _pick(n, 2176) if mode == "tn" and n > 1024 else _tile(n)
    tk = _pick(k, 2176) if mode == "nt" and a.dtype == BF16 and k > 2816 else _tile(k)
    nk = k // tk
    has_add = add is not None
    dims = {"nn": _NN, "nt": _NT, "tn": _TN}[mode]

    def body(*refs):
        if dep is not None:
            refs = refs[:-3] + refs[-2:]
        if has_add:
            a_ref, b_ref, add_ref, o_ref, acc_ref = refs
        else:
            a_ref, b_ref, o_ref, acc_ref = refs
        kk = pl.program_id(2)

        @pl.when(kk == 0)
        def _():
            if has_add:
                acc_ref[...] = add_scale * add_ref[...].astype(F32)
            else:
                acc_ref[...] = jnp.zeros_like(acc_ref)

        acc_ref[...] += _dot(a_ref[...].astype(BF16), b_ref[...].astype(BF16), dims)

        @pl.when(kk == nk - 1)
        def _():
            o_ref[...] = acc_ref[...].astype(o_ref.dtype)

    if mode == "nn":
        a_spec = pl.BlockSpec((tm, tk), lambda i, j, kk: (i, kk))
        b_spec = pl.BlockSpec((tk, tn), lambda i, j, kk: (kk, j))
    elif mode == "nt":
        a_spec = pl.BlockSpec((tm, tk), lambda i, j, kk: (i, kk))
        b_spec = pl.BlockSpec((tn, tk), lambda i, j, kk: (j, kk))
    else:
        a_spec = pl.BlockSpec((tk, tm), lambda i, j, kk: (kk, i))
        b_spec = pl.BlockSpec((tk, tn), lambda i, j, kk: (kk, j))
    o_spec = pl.BlockSpec((tm, tn), lambda i, j, kk: (i, j))
    in_specs = [a_spec, b_spec] + ([o_spec] if has_add else [])
    args = (a, b) + ((add,) if has_add else ())
    if dep is not None:
        in_specs.append(pl.BlockSpec((8, LANE), lambda i, j, kk: (0, 0)))
        args += (dep,)
    return pl.pallas_call(
        body, name=name, grid=(m // tm, n // tn, nk),
        in_specs=in_specs, out_specs=o_spec,
        out_shape=jax.ShapeDtypeStruct((m, n), out_dtype),
        scratch_shapes=[pltpu.VMEM((tm, tn), F32)],
        compiler_params=_cparams("parallel", "parallel", "arbitrary"),
    )(*args)


def _vec_spec(width):
    return pl.BlockSpec((1, width), lambda i: (0, 0))


def _ln_fwd(a, b, gamma, beta, alpha, name):
    n_rows, dm = a.shape
    has_b = b is not None

    def body(*refs):
        if has_b:
            a_ref, b_ref, g_ref, be_ref, o_ref = refs
            u = alpha * a_ref[...] + b_ref[...]
        else:
            a_ref, g_ref, be_ref, o_ref = refs
            u = a_ref[...]
        mu = jnp.mean(u, axis=-1, keepdims=True)
        d = u - mu
        var = jnp.mean(d * d, axis=-1, keepdims=True)
        o_ref[...] = d * lax.rsqrt(var + LN_EPS) * g_ref[...] + be_ref[...]

    row = pl.BlockSpec((_rows(n_rows),dm), lambda i: (i, 0))
    in_specs = [row] + ([row] if has_b else []) + [_vec_spec(dm), _vec_spec(dm)]
    args = (a,) + ((b,) if has_b else ()) + (gamma.reshape(1, dm), beta.reshape(1, dm))
    return pl.pallas_call(
        body, name=name, grid=(n_rows // _rows(n_rows),), in_specs=in_specs, out_specs=row,
        out_shape=jax.ShapeDtypeStruct((n_rows, dm), F32),
        compiler_params=_cparams("parallel"),
    )(*args)


def _ln_bwd(a, b, gamma, dy, alpha, name):
    n_rows, dm = a.shape
    has_b = b is not None

    def body(*refs):
        if has_b:
            a_ref, b_ref, g_ref, dy_ref, du_ref, acc_ref = refs
            u = alpha * a_ref[...] + b_ref[...]
        else:
            a_ref, g_ref, dy_ref, du_ref, acc_ref = refs
            u = a_ref[...]

        @pl.when(pl.program_id(0) == 0)
        def _():
            acc_ref[...] = jnp.zeros_like(acc_ref)

        mu = jnp.mean(u, axis=-1, keepdims=True)
        d = u - mu
        var = jnp.mean(d * d, axis=-1, keepdims=True)
        rstd = lax.rsqrt(var + LN_EPS)
        xhat = d * rstd
        dyv = dy_ref[...]
        acc_ref[0:1, :] += jnp.sum(dyv * xhat, axis=0, keepdims=True)
        acc_ref[1:2, :] += jnp.sum(dyv, axis=0, keepdims=True)
        dxh = dyv * g_ref[...]
        m1 = jnp.mean(dxh, axis=-1, keepdims=True)
        m2 = jnp.mean(dxh * xhat, axis=-1, keepdims=True)
        du_ref[...] = rstd * (dxh - m1 - xhat * m2)

    row = pl.BlockSpec((_rows(n_rows),dm), lambda i: (i, 0))
    in_specs = [row] + ([row] if has_b else []) + [_vec_spec(dm), row]
    args = (a,) + ((b,) if has_b else ()) + (gamma.reshape(1, dm), dy)
    return pl.pallas_call(
        body, name=name, grid=(n_rows // _rows(n_rows),), in_specs=in_specs,
        out_specs=(row, pl.BlockSpec((8, dm), lambda i: (0, 0))),
        out_shape=(jax.ShapeDtypeStruct((n_rows, dm), F32), jax.ShapeDtypeStruct((8, dm), F32)),
        compiler_params=_cparams("arbitrary"),
    )(*args)


def _loss_fwd_bwd(y, target, name):
    n_rows, dm = y.shape

    def body(y_ref, t_ref, acc_ref, dy_ref):
        @pl.when(pl.program_id(0) == 0)
        def _():
            acc_ref[...] = jnp.zeros_like(acc_ref)

        d = y_ref[...] - t_ref[...]
        acc_ref[...] += jnp.sum(d * d)
        dy_ref[...] = d * (1.0 / dm)

    row = pl.BlockSpec((_rows(n_rows),dm), lambda i: (i, 0))
    return pl.pallas_call(
        body, name=name, grid=(n_rows // _rows(n_rows),), in_specs=[row, row],
        out_specs=(pl.BlockSpec((8, LANE), lambda i: (0, 0)), row),
        out_shape=(jax.ShapeDtypeStruct((8, LANE), F32), jax.ShapeDtypeStruct((n_rows, dm), F32)),
        compiler_params=_cparams("arbitrary"),
    )(y, target)


FFN_ROWS = 512


def _ffn_in(h, wg, wu, name, dep=None):
    m, k = h.shape
    n = wg.shape[1]
    tm, tn = min(FFN_ROWS, m), _tile(n)

    def body(*refs):
        h_ref, wg_ref, wu_ref = refs[:3]
        g_ref, u_ref, act_ref = refs[-3:]
        hb = h_ref[...].astype(BF16)
        g = _dot(hb, wg_ref[...])
        u = _dot(hb, wu_ref[...])
        g_ref[...] = g
        u_ref[...] = u
        act_ref[...] = (g * _sig(g) * u).astype(BF16)

    rows = pl.BlockSpec((tm, k), lambda j, i: (i, 0))
    wcol = pl.BlockSpec((k, tn), lambda j, i: (0, j))
    out = pl.BlockSpec((tm, tn), lambda j, i: (i, j))
    in_specs, args = [rows, wcol, wcol], (h, wg, wu)
    if dep is not None:
        in_specs.append(pl.BlockSpec((8, LANE), lambda j, i: (0, 0)))
        args += (dep,)
    return pl.pallas_call(
        body, name=name, grid=(n // tn, m // tm), in_specs=in_specs, out_specs=(out, out, out),
        out_shape=(jax.ShapeDtypeStruct((m, n), F32), jax.ShapeDtypeStruct((m, n), F32),
                   jax.ShapeDtypeStruct((m, n), BF16)),
        compiler_params=_cparams("parallel", "parallel"),
    )(*args)


def _ffn_dact(dy, wd, g, u, name, dep=None):
    m, k = dy.shape
    n = wd.shape[0]
    tm, tn = min(FFN_ROWS, m), _tile(n)

    def body(*refs):
        dy_ref, wd_ref, g_ref, u_ref = refs[:4]
        dg_ref, du_ref = refs[-2:]
        da = _dot(dy_ref[...].astype(BF16), wd_ref[...], _NT)
        gv = g_ref[...]
        s = _sig(gv)
        dg_ref[...] = (da * u_ref[...] * (s * (1.0 + gv * (1.0 - s)))).astype(BF16)
        du_ref[...] = (da * gv * s).astype(BF16)

    rows = pl.BlockSpec((tm, k), lambda j, i: (i, 0))
    wrow = pl.BlockSpec((tn, k), lambda j, i: (j, 0))
    out = pl.BlockSpec((tm, tn), lambda j, i: (i, j))
    in_specs, args = [rows, wrow, out, out], (dy, wd, g, u)
    if dep is not None:
        in_specs.append(pl.BlockSpec((8, LANE), lambda j, i: (0, 0)))
        args += (dep,)
    return pl.pallas_call(
        body, name=name, grid=(n // tn, m // tm), in_specs=in_specs, out_specs=(out, out),
        out_shape=(jax.ShapeDtypeStruct((m, n), BF16), jax.ShapeDtypeStruct((m, n), BF16)),
        compiler_params=_cparams("parallel", "parallel"),
    )(*args)


def _gate_specs(gl, n_rows, dm):
    arr, g0, _ = _window(gl)
    return arr, [pl.BlockSpec((_rows(n_rows), dm), lambda i, k=k: (i, g0 // dm + k)) for k in range(2)]


def _merge_fwd(gl, ya, yb, name):
    n_rows, dm = ya.shape
    gl_arr, gspecs = _gate_specs(gl, n_rows, dm)

    def body(ga_ref, gb_ref, ya_ref, yb_ref, o_ref):
        o_ref[...] = (_sig(ga_ref[...]) * ya_ref[...] + _sig(gb_ref[...]) * yb_ref[...]).astype(BF16)

    row = pl.BlockSpec((_rows(n_rows),dm), lambda i: (i, 0))
    return pl.pallas_call(
        body, name=name, grid=(n_rows // _rows(n_rows),), in_specs=gspecs + [row, row], out_specs=row,
        out_shape=jax.ShapeDtypeStruct((n_rows, dm), BF16),
        compiler_params=_cparams("parallel"),
    )(gl_arr, gl_arr, ya, yb)


def _merge_bwd(gl, ya, yb, dmerged, name, into):
    n_rows, dm = ya.shape
    gl_arr, gspecs = _gate_specs(gl, n_rows, dm)
    extra, extra_specs, aliases, col0, width = _into(into, 5, 2)

    def body(*refs):
        ga_ref, gb_ref, ya_ref, yb_ref, dm_ref = refs[:5]
        dya_ref, dyb_ref, dgl_ref = refs[-3:]
        ga = _sig(ga_ref[...])
        gb = _sig(gb_ref[...])
        dmv = dm_ref[...]
        dya_ref[...] = (dmv * ga).astype(BF16)
        dyb_ref[...] = (dmv * gb).astype(BF16)
        dgl_ref[:, :dm] = (dmv * ya_ref[...] * ga * (1.0 - ga)).astype(BF16)
        dgl_ref[:, dm:] = (dmv * yb_ref[...] * gb * (1.0 - gb)).astype(BF16)

    row = pl.BlockSpec((_rows(n_rows),dm), lambda i: (i, 0))
    row2 = pl.BlockSpec((_rows(n_rows),2 * dm), lambda i: (i, col0 // (2 * dm)))
    return pl.pallas_call(
        body, name=name, grid=(n_rows // _rows(n_rows),), in_specs=gspecs + [row, row, row] + extra_specs,
        out_specs=(row, row, row2),
        out_shape=(jax.ShapeDtypeStruct((n_rows, dm), BF16), jax.ShapeDtypeStruct((n_rows, dm), BF16),
                   jax.ShapeDtypeStruct((n_rows, width), BF16)),
        input_output_aliases=aliases,
        compiler_params=_cparams("parallel"),
    )(gl_arr, gl_arr, ya, yb, dmerged, *extra)


CONV_TAPS = 4
CONV_COLS = 512
HALO = 8


def _shift_down(cur, prev8, s, row8):
    r = pltpu.roll(cur, s, axis=0)
    top = jnp.where(row8 < s, pltpu.roll(prev8, s, axis=0), r[0:HALO])
    return jnp.concatenate([top, r[HALO:]], axis=0)


def _shift_up(cur, next8, s, row8):
    n = cur.shape[0]
    r = pltpu.roll(cur, n - s, axis=0)
    bot = jnp.where(row8 >= HALO - s, pltpu.roll(next8, HALO - s, axis=0), r[n - HALO:])
    return jnp.concatenate([r[:n - HALO], bot], axis=0)


def _conv_pre(u_ref, prev_ref, w_ref, b_ref, li):
    cur = u_ref[...]
    prev8 = jnp.where(li == 0, 0.0, prev_ref[...])
    row8 = lax.broadcasted_iota(jnp.int32, prev8.shape, 0)
    shifted = [cur] + [_shift_down(cur, prev8, s, row8) for s in range(1, CONV_TAPS)]
    acc = b_ref[...] + shifted[0] * w_ref[CONV_TAPS - 1:CONV_TAPS, :]
    for s in range(1, CONV_TAPS):
        acc = acc + shifted[s] * w_ref[CONV_TAPS - 1 - s:CONV_TAPS - s, :]
    return acc, shifted


def _conv_specs(n_rows, tl, col0=0):
    off = col0 // CONV_COLS
    cur = pl.BlockSpec((tl, CONV_COLS), lambda cj, li: (li, cj + off))
    prev = pl.BlockSpec((HALO, CONV_COLS), lambda cj, li: (jnp.maximum(li * (tl // HALO) - 1, 0), cj + off))
    nxt = pl.BlockSpec((HALO, CONV_COLS),
                       lambda cj, li: (jnp.minimum((li + 1) * (tl // HALO), n_rows // HALO - 1), cj + off))
    par = pl.BlockSpec((8, CONV_COLS), lambda cj, li: (0, cj + off))
    return cur, prev, nxt, par


def _conv_fwd(u, w8, b8, name):
    u, u0, c = _window(u)
    n_rows = u.shape[0]
    tl = _rows(n_rows)
    cur, _, _, par = _conv_specs(n_rows, tl)
    ucur, prev, _, _ = _conv_specs(n_rows, tl, u0)

    def body(u_ref, prev_ref, w_ref, b_ref, o_ref):
        acc, _ = _conv_pre(u_ref, prev_ref, w_ref, b_ref[0:1, :], pl.program_id(1))
        o_ref[...] = acc * _sig(acc)

    return pl.pallas_call(
        body, name=name, grid=(c // CONV_COLS, n_rows // tl), in_specs=[ucur, prev, par, par], out_specs=cur,
        out_shape=jax.ShapeDtypeStruct((n_rows, c), F32),
        compiler_params=_cparams("parallel", "parallel"),
    )(u, u, w8, b8)


def _conv_bwd_pre(u, w8, b8, dout, name):
    u, u0, c = _window(u)
    n_rows = u.shape[0]
    tl = _rows(n_rows)
    cur, _, _, par = _conv_specs(n_rows, tl)
    ucur, prev, _, _ = _conv_specs(n_rows, tl, u0)

    def body(u_ref, prev_ref, w_ref, b_ref, do_ref, dc_ref, acc_ref):
        @pl.when(pl.program_id(1) == 0)
        def _():
            acc_ref[...] = jnp.zeros_like(acc_ref)

        acc, shifted = _conv_pre(u_ref, prev_ref, w_ref, b_ref[0:1, :], pl.program_id(1))
        sg = _sig(acc)
        dc = do_ref[...] * (sg * (1.0 + acc * (1.0 - sg)))
        dc_ref[...] = dc
        for k in range(CONV_TAPS):
            acc_ref[k:k + 1, :] += jnp.sum(dc * shifted[CONV_TAPS - 1 - k], axis=0, keepdims=True)
        acc_ref[CONV_TAPS:CONV_TAPS + 1, :] += jnp.sum(dc, axis=0, keepdims=True)

    return pl.pallas_call(
        body, name=name, grid=(c // CONV_COLS, n_rows // tl), in_specs=[ucur, prev, par, par, cur],
        out_specs=(cur, par),
        out_shape=(jax.ShapeDtypeStruct((n_rows, c), F32), jax.ShapeDtypeStruct((8, c), F32)),
        compiler_params=_cparams("parallel", "arbitrary"),
    )(u, u, w8, b8, dout)


def _conv_bwd_in(dc, w8, name, into):
    n_rows, c = dc.shape
    tl = _rows(n_rows)
    cur, _, nxt, par = _conv_specs(n_rows, tl)
    n_l = n_rows // tl
    extra, extra_specs, aliases, col0, width = _into(into, 3, 0)
    out_spec = _conv_specs(n_rows, tl, col0)[0]

    def body(*refs):
        dc_ref, next_ref, w_ref = refs[:3]
        o_ref = refs[-1]
        cur_v = dc_ref[...]
        next8 = jnp.where(pl.program_id(1) == n_l - 1, 0.0, next_ref[...])
        row8 = lax.broadcasted_iota(jnp.int32, next8.shape, 0)
        acc = cur_v * w_ref[CONV_TAPS - 1:CONV_TAPS, :]
        for s in range(1, CONV_TAPS):
            acc = acc + _shift_up(cur_v, next8, s, row8) * w_ref[CONV_TAPS - 1 - s:CONV_TAPS - s, :]
        o_ref[...] = acc.astype(BF16)

    return pl.pallas_call(
        body, name=name, grid=(c // CONV_COLS, n_l), in_specs=[cur, nxt, par] + extra_specs, out_specs=out_spec,
        out_shape=jax.ShapeDtypeStruct((n_rows, width), BF16), input_output_aliases=aliases,
        compiler_params=_cparams("parallel", "parallel"),
    )(dc, dc, w8, *extra)


NORM_GROUP = SSD_D_INNER // SSD_GROUPS


def _gnorm_fwd(y, z, w, name):
    n_rows, c = y.shape
    z, z0, _ = _window(z)
    zoff = z0 // NORM_GROUP

    def body(y_ref, z_ref, w_ref, o_ref):
        zv = z_ref[...]
        yg = y_ref[...] * (zv * _sig(zv))
        r = lax.rsqrt(jnp.mean(yg * yg, axis=-1, keepdims=True) + RMS_EPS)
        o_ref[...] = (yg * r * w_ref[...]).astype(BF16)

    blk = pl.BlockSpec((_rows(n_rows),NORM_GROUP), lambda i, j: (i, j))
    zblk = pl.BlockSpec((_rows(n_rows),NORM_GROUP), lambda i, j: (i, j + zoff))
    wspec = pl.BlockSpec((1, NORM_GROUP), lambda i, j: (0, j))
    return pl.pallas_call(
        body, name=name, grid=(n_rows // _rows(n_rows), c // NORM_GROUP), in_specs=[blk, zblk, wspec], out_specs=blk,
        out_shape=jax.ShapeDtypeStruct((n_rows, c), BF16),
        compiler_params=_cparams("parallel", "parallel"),
    )(y, z, w.reshape(1, c))


def _gnorm_bwd(y, z, w, dyn, name, into):
    n_rows, c = y.shape
    z, z0, _ = _window(z)
    zoff = z0 // NORM_GROUP
    extra, extra_specs, aliases, col0, width = _into(into, 4, 1)
    doff = col0 // NORM_GROUP

    def body(*refs):
        y_ref, z_ref, w_ref, dn_ref = refs[:4]
        dy_ref, dz_ref, acc_ref = refs[-3:]
        @pl.when(pl.program_id(1) == 0)
        def _():
            acc_ref[...] = jnp.zeros_like(acc_ref)

        zv = z_ref[...]
        yv = y_ref[...]
        sz = _sig(zv)
        silu = zv * sz
        yg = yv * silu
        r = lax.rsqrt(jnp.mean(yg * yg, axis=-1, keepdims=True) + RMS_EPS)
        nrm = yg * r
        dn = dn_ref[...]
        acc_ref[0:1, :] += jnp.sum(dn * nrm, axis=0, keepdims=True)
        dnw = dn * w_ref[...]
        dyg = r * (dnw - nrm * jnp.mean(dnw * nrm, axis=-1, keepdims=True))
        dy_ref[...] = dyg * silu
        dz_ref[...] = (dyg * yv * (sz * (1.0 + zv * (1.0 - sz)))).astype(BF16)

    blk = pl.BlockSpec((_rows(n_rows),NORM_GROUP), lambda j, i: (i, j))
    zblk = pl.BlockSpec((_rows(n_rows),NORM_GROUP), lambda j, i: (i, j + zoff))
    wspec = pl.BlockSpec((1, NORM_GROUP), lambda j, i: (0, j))
    aspec = pl.BlockSpec((8, NORM_GROUP), lambda j, i: (0, j))
    return pl.pallas_call(
        body, name=name, grid=(c // NORM_GROUP, n_rows // _rows(n_rows)),
        in_specs=[blk, zblk, wspec, blk] + extra_specs,
        out_specs=(blk, pl.BlockSpec((_rows(n_rows), NORM_GROUP), lambda j, i: (i, j + doff)), aspec),
        out_shape=(jax.ShapeDtypeStruct((n_rows, c), F32), jax.ShapeDtypeStruct((n_rows, width), BF16),
                   jax.ShapeDtypeStruct((8, c), F32)),
        input_output_aliases=aliases,
        compiler_params=_cparams("parallel", "arbitrary"),
    )(y, z, w.reshape(1, c), dyn, *extra)


ATT_SCALE = ATT_HEAD_DIM ** -0.5
ATT_SLOPES = [2.0 ** (-8.0 * (h + 1) / ATT_HEADS) for h in range(ATT_HEADS)]
Q_PER_KV = ATT_HEADS // ATT_KV_HEADS


def _dup_half(t, g, lo):
    tr = pltpu.roll(t, ATT_HEAD_DIM, axis=1)
    return jnp.where(lo, t, tr) if g == 0 else jnp.where(lo, tr, t)


def _att_band(kv_ref, kvp_ref, n):
    cur = kv_ref[...]
    prev = jnp.where(n == 0, 0.0, kvp_ref[...])
    lo = lax.broadcasted_iota(jnp.int32, (ATT_BLOCK, LANE), 1) < ATT_HEAD_DIM
    bands = []
    for g in range(ATT_KV_HEADS):
        kb = jnp.concatenate([_dup_half(prev[:, :LANE], g, lo), _dup_half(cur[:, :LANE], g, lo)], axis=0)
        vb = jnp.concatenate([_dup_half(prev[:, LANE:], g, lo), _dup_half(cur[:, LANE:], g, lo)], axis=0)
        bands.append((kb.astype(BF16), vb.astype(BF16)))
    return bands


def _att_tile(n):
    shape = (2 * ATT_BLOCK, ATT_BLOCK)
    row = lax.broadcasted_iota(jnp.int32, shape, 0)
    i = row & (ATT_BLOCK - 1)
    s = lax.broadcasted_iota(jnp.int32, shape, 1)
    upper = s > i
    dist = ((i - s) & (ATT_BLOCK - 1)).astype(F32)
    dead = upper & (n == 0)
    return upper, dist, dead, row[:, 0:1] < ATT_BLOCK


def _stack_pair(t, lo):
    return jnp.concatenate([jnp.where(lo, t, 0.0), jnp.where(lo, 0.0, t)], axis=0).astype(BF16)


def _att_exp(qs, kb, s_ref, j, tile):
    upper, dist, dead, first = tile
    s2 = _dot(qs, kb, _NT)
    slope = jnp.where(first, ATT_SLOPES[2 * j], ATT_SLOPES[2 * j + 1])
    sink = jnp.where(first, s_ref[0:1, 2 * j:2 * j + 1], s_ref[0:1, 2 * j + 1:2 * j + 2])
    s = jnp.where(upper, s2[:, :ATT_BLOCK], s2[:, ATT_BLOCK:]) - slope * dist
    s = jnp.where(dead, NEG, s)
    m = jnp.maximum(jnp.max(s, axis=-1, keepdims=True), sink)
    return jnp.exp(s - m), jnp.exp(sink - m)


def _band_split(t, upper):
    return jnp.concatenate([jnp.where(upper, t, 0.0), jnp.where(upper, 0.0, t)], axis=1)


def _att_fwd(q, kv, sinks8, name):
    q, q0, _ = _window(q)
    kv, kv0, _ = _window(kv)
    qoff, kvoff = q0 // Q_DIM, kv0 // (2 * LANE)
    n_rows = q.shape[0]
    nb = n_rows // ATT_BLOCK

    def body(q_ref, kv_ref, kvp_ref, s_ref, o_ref):
        n = pl.program_id(0)
        bands = _att_band(kv_ref, kvp_ref, n)
        lo = lax.broadcasted_iota(jnp.int32, (ATT_BLOCK, LANE), 1) < ATT_HEAD_DIM
        tile = _att_tile(n)
        ones_b = jnp.ones((2 * ATT_BLOCK, LANE), BF16)
        for j in range(ATT_HEADS // 2):
            kb, vb = bands[2 * j // Q_PER_KV]
            qs = _stack_pair(q_ref[:, j * LANE:(j + 1) * LANE] * ATT_SCALE, lo)
            p, es = _att_exp(qs, kb, s_ref, j, tile)
            pv = _dot(_band_split(p, tile[0]).astype(BF16), jnp.concatenate([vb, ones_b], axis=1))
            out = pv[:, :LANE] / (pv[:, LANE:] + es)
            o_ref[:, j * LANE:(j + 1) * LANE] = jnp.where(lo, out[:ATT_BLOCK], out[ATT_BLOCK:]).astype(BF16)

    return pl.pallas_call(
        body, name=name, grid=(nb,),
        in_specs=[pl.BlockSpec((ATT_BLOCK, Q_DIM), lambda n: (n, qoff)),
                  pl.BlockSpec((ATT_BLOCK, 2 * LANE), lambda n: (n, kvoff)),
                  pl.BlockSpec((ATT_BLOCK, 2 * LANE), lambda n: (jnp.maximum(n - 1, 0), kvoff)),
                  pl.BlockSpec((8, LANE), lambda n: (0, 0))],
        out_specs=pl.BlockSpec((ATT_BLOCK, Q_DIM), lambda n: (n, 0)),
        out_shape=jax.ShapeDtypeStruct((n_rows, Q_DIM), BF16),
        compiler_params=_cparams("parallel"),
    )(q, kv, kv, sinks8)


def _att_bwd(q, kv, sinks8, att, dout, name, into):
    q, q0, _ = _window(q)
    kv, kv0, _ = _window(kv)
    qoff, kvoff = q0 // Q_DIM, kv0 // (2 * LANE)
    n_rows = q.shape[0]
    nb = n_rows // ATT_BLOCK

    extra, extra_specs, aliases, col0, width = _into(into, 6, 0)
    dqoff = col0 // Q_DIM

    def body(*refs):
        q_ref, kv_ref, kvp_ref, s_ref, o_ref, do_ref = refs[:6]
        dq_ref, dkv_ref, acc_ref, carry_ref = refs[-4:]
        n = pl.program_id(0)

        @pl.when(n == 0)
        def _():
            acc_ref[...] = jnp.zeros_like(acc_ref)
            carry_ref[...] = jnp.zeros_like(carry_ref)

        @pl.when(n == nb)
        def _():
            dkv_ref[...] = carry_ref[...].astype(BF16)

        @pl.when(n < nb)
        def _():
            bands = _att_band(kv_ref, kvp_ref, n)
            lo = lax.broadcasted_iota(jnp.int32, (ATT_BLOCK, LANE), 1) < ATT_HEAD_DIM
            lane1 = lax.broadcasted_iota(jnp.int32, (1, LANE), 1)
            tile = _att_tile(n)
            upper, first = tile[0], tile[3]
            ones_b = jnp.ones((ATT_BLOCK, LANE), BF16)
            ones2_b = jnp.ones((2 * LANE, LANE), BF16)
            dk_acc = [jnp.zeros((2 * ATT_BLOCK, LANE), F32) for _ in range(ATT_KV_HEADS)]
            dv_acc = [jnp.zeros((2 * ATT_BLOCK, LANE), F32) for _ in range(ATT_KV_HEADS)]
            dsink = jnp.zeros((1, LANE), F32)
            for j in range(ATT_HEADS // 2):
                g = 2 * j // Q_PER_KV
                kb, vb = bands[g]
                qs = _stack_pair(q_ref[:, j * LANE:(j + 1) * LANE] * ATT_SCALE, lo)
                dop = do_ref[:, j * LANE:(j + 1) * LANE].astype(F32)
                dos = _stack_pair(dop, lo)
                pu, es = _att_exp(qs, kb, s_ref, j, tile)
                inv = 1.0 / (_dot(pu.astype(BF16), ones_b) + es)
                p = pu * inv
                od = dop * o_ref[:, j * LANE:(j + 1) * LANE].astype(F32)
                od = jnp.concatenate([jnp.where(lo, od, 0.0), jnp.where(lo, 0.0, od)], axis=0)
                od_hi = od.astype(BF16)
                delta = _dot(jnp.concatenate([od_hi, (od - od_hi.astype(F32)).astype(BF16)], axis=1), ones2_b)
                dp2 = _dot(dos, vb, _NT)
                dp = jnp.where(upper, dp2[:, :ATT_BLOCK], dp2[:, ATT_BLOCK:])
                ds2 = _band_split(p * (dp - delta), upper)
                psd = jnp.sum(es * inv * delta, axis=0, keepdims=True)
                psd0 = jnp.sum(jnp.where(first, es * inv * delta, 0.0), axis=0, keepdims=True)
                dsink = jnp.where(lane1 == 2 * j, -psd0, jnp.where(lane1 == 2 * j + 1, psd0 - psd, dsink))
                ds2_b = ds2.astype(BF16)
                dq = _dot(ds2_b, kb) * ATT_SCALE
                dq_ref[:, j * LANE:(j + 1) * LANE] = jnp.where(lo, dq[:ATT_BLOCK], dq[ATT_BLOCK:]).astype(BF16)
                dk_acc[g] = dk_acc[g] + _dot(ds2_b, qs, _TN)
                dv_acc[g] = dv_acc[g] + _dot(_band_split(p, upper).astype(BF16), dos, _TN)
            acc_ref[0:1, :] += dsink
            lo2 = lax.broadcasted_iota(jnp.int32, (2 * ATT_BLOCK, LANE), 1) < ATT_HEAD_DIM
            folded = []
            for acc in (dk_acc, dv_acc):
                t0 = acc[0] + pltpu.roll(acc[0], ATT_HEAD_DIM, axis=1)
                t1 = acc[1] + pltpu.roll(acc[1], ATT_HEAD_DIM, axis=1)
                folded.append(jnp.where(lo2, t0, t1))
            band = jnp.concatenate(folded, axis=1)
            dkv_ref[...] = (carry_ref[...] + band[:ATT_BLOCK]).astype(BF16)
            carry_ref[...] = band[ATT_BLOCK:]

    def qmap(n):
        return (jnp.minimum(n, nb - 1), 0)

    return pl.pallas_call(
        body, name=name, grid=(nb + 1,),
        in_specs=[pl.BlockSpec((ATT_BLOCK, Q_DIM), lambda n: (jnp.minimum(n, nb - 1), qoff)),
                  pl.BlockSpec((ATT_BLOCK, 2 * LANE), lambda n: (jnp.minimum(n, nb - 1), kvoff)),
                  pl.BlockSpec((ATT_BLOCK, 2 * LANE),
                               lambda n: (jnp.maximum(jnp.minimum(n, nb - 1) - 1, 0), kvoff)),
                  pl.BlockSpec((8, LANE), lambda n: (0, 0)),
                  pl.BlockSpec((ATT_BLOCK, Q_DIM), qmap),
                  pl.BlockSpec((ATT_BLOCK, Q_DIM), qmap)] + extra_specs,
        out_specs=(pl.BlockSpec((ATT_BLOCK, Q_DIM), lambda n: (jnp.minimum(n, nb - 1), dqoff)),
                   pl.BlockSpec((ATT_BLOCK, 2 * LANE), lambda n: (jnp.maximum(n - 1, 0), 0)),
                   pl.BlockSpec((8, LANE), lambda n: (0, 0))),
        out_shape=(jax.ShapeDtypeStruct((n_rows, width), BF16), jax.ShapeDtypeStruct((n_rows, 2 * LANE), BF16),
                   jax.ShapeDtypeStruct((8, LANE), F32)),
        input_output_aliases=aliases,
        scratch_shapes=[pltpu.VMEM((ATT_BLOCK, 2 * LANE), F32)],
        compiler_params=_cparams("arbitrary"),
    )(q, kv, kv, sinks8, att, dout, *extra)


HEADS_PER_GROUP = SSD_HEADS // SSD_GROUPS
PAIRS_PER_GROUP = HEADS_PER_GROUP // 2
T = SSD_CHUNK


def _cumsum_mm(mat, x):
    hi = x.astype(BF16)
    r = x - hi.astype(F32)
    mid = r.astype(BF16)
    lo = (r - mid.astype(F32)).astype(BF16)
    w = x.shape[1]
    out = _dot(mat, jnp.concatenate([hi, mid, lo], axis=1))
    return out[:, :w] + out[:, w:2 * w] + out[:, 2 * w:]


def _ssd_prep(dtr_ref, par_ref):
    dt = _softplus(dtr_ref[...] + par_ref[0:1, :])
    a = -jnp.exp(par_ref[1:2, :])
    ri = lax.broadcasted_iota(jnp.int32, (T, T), 0)
    ci = lax.broadcasted_iota(jnp.int32, (T, T), 1)
    cs = _cumsum_mm((ri >= ci).astype(BF16), dt * a)
    lo = lax.broadcasted_iota(jnp.int32, (T, LANE), 1) < SSD_CHUNK // 2

    def expand(arr):
        rows = arr.shape[0]
        return jnp.concatenate([jnp.where(lo[:rows], arr[:, 2 * j:2 * j + 1], arr[:, 2 * j + 1:2 * j + 2])
                                for j in range(PAIRS_PER_GROUP)], axis=1)

    tot = cs[T - 1:T, :]
    return {"dt": dt, "a": a, "cs": cs, "cst": cs.T, "lo": lo, "ri": ri, "ci": ci, "expand": expand,
            "dt_x": expand(dt), "ecs_x": expand(jnp.exp(cs)), "dec_x": expand(jnp.exp(tot - cs)),
            "et_x": expand(jnp.exp(tot)), "etot": jnp.exp(tot), "dsk_x": expand(par_ref[2:3, :])}


def _wide_masks():
    r = lax.broadcasted_iota(jnp.int32, (T, 2 * T), 0)
    l = lax.broadcasted_iota(jnp.int32, (T, 2 * T), 1)
    s = l & (T - 1)
    return r >= s, s >= r, l < T


def _wide_cs(q, k0, even):
    cs, cst = q["cs"], q["cst"]
    col = jnp.where(even, cs[:, k0:k0 + 1], cs[:, k0 + 1:k0 + 2])
    row = jnp.concatenate([cst[k0:k0 + 1, :], cst[k0 + 1:k0 + 2, :]], axis=1)
    return col, row


def _ssd_fwd(xs, bm, cm, dtr, par, name):
    dtr, dt0, _ = _window(dtr)
    dtoff = dt0 // LANE
    n_rows = xs.shape[0]
    nc = n_rows // T
    gw = PAIRS_PER_GROUP * LANE

    def body(x_ref, b_ref, c_ref, dtr_ref, par_ref, y_ref, hs_ref, h_ref):
        @pl.when(pl.program_id(1) == 0)
        def _():
            h_ref[...] = jnp.zeros_like(h_ref)

        q = _ssd_prep(dtr_ref, par_ref)
        lo = q["lo"]
        tri_w, _, even = _wide_masks()
        bg_b = b_ref[...].astype(BF16)
        cg_b = c_ref[...].astype(BF16)
        xv = x_ref[...]
        xdt = xv * q["dt_x"]
        h = h_ref[...]
        hs_ref[0, 0] = h
        yo = q["ecs_x"] * _dot(cg_b, h.astype(BF16))
        h_ref[...] = h * q["et_x"] + _dot(b_ref[...].T.astype(BF16), (xdt * q["dec_x"]).astype(BF16))
        cb = _dot(cg_b, bg_b, _NT)
        cb_w = jnp.concatenate([cb, cb], axis=1)
        for j in range(PAIRS_PER_GROUP):
            col, row = _wide_cs(q, 2 * j, even)
            m_w = (jnp.exp(jnp.where(tri_w, col - row, NEG)) * cb_w).astype(BF16)
            sl = slice(j * LANE, (j + 1) * LANE)
            y_ref[:, sl] = (_dot(m_w, _stack_pair(xdt[:, sl], lo)) + yo[:, sl] + q["dsk_x"][:, sl] * xv[:, sl])

    return pl.pallas_call(
        body, name=name, grid=(SSD_GROUPS, nc),
        in_specs=[pl.BlockSpec((T, gw), lambda g, c: (c, g)),
                  pl.BlockSpec((T, SSD_STATE), lambda g, c: (c, g)),
                  pl.BlockSpec((T, SSD_STATE), lambda g, c: (c, g)),
                  pl.BlockSpec((T, LANE), lambda g, c: (c, g + dtoff)),
                  pl.BlockSpec((8, LANE), lambda g, c: (0, g))],
        out_specs=(pl.BlockSpec((T, gw), lambda g, c: (c, g)),
                   pl.BlockSpec((1, 1, SSD_STATE, gw), lambda g, c: (g, c, 0, 0))),
        out_shape=(jax.ShapeDtypeStruct((n_rows, SSD_D_INNER), F32),
                   jax.ShapeDtypeStruct((SSD_GROUPS, nc, SSD_STATE, gw), F32)),
        scratch_shapes=[pltpu.VMEM((SSD_STATE, gw), F32)],
        compiler_params=_cparams("parallel", "arbitrary"),
    )(xs, bm, cm, dtr, par)


def _ssd_bwd(xs, bm, cm, dtr, par, hs, dy, name, into):
    dtr, dt0, _ = _window(dtr)
    dtoff = dt0 // LANE
    n_rows = xs.shape[0]
    nc = n_rows // T
    gw = PAIRS_PER_GROUP * LANE
    extra, extra_specs, aliases, col0, width = _into(into, 7, 3)
    ddoff = col0 // LANE

    def body(*refs):
        x_ref, b_ref, c_ref, dtr_ref, par_ref, hs_ref, dy_ref = refs[:7]
        dx_ref, db_ref, dc_ref, ddtr_ref, acc_ref, dh_ref = refs[-6:]

        @pl.when(pl.program_id(1) == 0)
        def _():
            dh_ref[...] = jnp.zeros_like(dh_ref)
            acc_ref[...] = jnp.zeros_like(acc_ref)

        q = _ssd_prep(dtr_ref, par_ref)
        lo, dt, a = q["lo"], q["dt"], q["a"]
        tri_w, trit_w, even = _wide_masks()
        lane = lax.broadcasted_iota(jnp.int32, (T, LANE), 1)
        lane1 = lane[0:1, :]
        last_row = lax.broadcasted_iota(jnp.int32, (T, 1), 0) == T - 1
        bg_b = b_ref[...].astype(BF16)
        cg_b = c_ref[...].astype(BF16)
        xv = x_ref[...]
        dyv = dy_ref[...]
        xdt = xv * q["dt_x"]
        h = hs_ref[0, 0]
        dhn = dh_ref[...]
        h_b, dhn_b = h.astype(BF16), dhn.astype(BF16)
        yo = q["ecs_x"] * _dot(cg_b, h_b)
        bdh = q["dec_x"] * _dot(bg_b, dhn_b)
        dye = (dyv * q["ecs_x"]).astype(BF16)
        xd = (xdt * q["dec_x"]).astype(BF16)
        dcg = _dot(dye, h_b, _NT)
        dbg = _dot(xd, dhn_b, _NT)
        dh_ref[...] = dhn * q["et_x"] + _dot(c_ref[...].T.astype(BF16), dye)
        e4_all = xdt * bdh
        f_all = dyv * yo - e4_all
        tot_row = jnp.sum(e4_all, axis=0, keepdims=True) + q["et_x"] * jnp.sum(h * dhn, axis=0, keepdims=True)
        dsk_row = jnp.sum(dyv * xv, axis=0, keepdims=True)
        cb = _dot(cg_b, bg_b, _NT)
        cbt = _dot(bg_b, cg_b, _NT)
        cb_w = jnp.concatenate([cb, cb], axis=1)
        cbt_w = jnp.concatenate([cbt, cbt], axis=1)
        dcb = jnp.zeros((T, T), F32)
        dcbt = jnp.zeros((T, T), F32)
        dcs_acc = jnp.zeros((T, LANE), F32)
        ddt_acc = jnp.zeros((T, LANE), F32)
        dsk_acc = jnp.zeros((1, LANE), F32)
        tot_acc = jnp.zeros((1, LANE), F32)
        ind_r = lax.broadcasted_iota(jnp.int32, (2 * T, LANE), 0)
        ind_l = lax.broadcasted_iota(jnp.int32, (2 * T, LANE), 1)

        def halves(t):
            return (jnp.sum(jnp.where(lo[0:1], t, 0.0), axis=-1, keepdims=True),
                    jnp.sum(jnp.where(lo[0:1], 0.0, t), axis=-1, keepdims=True))

        def split2(t):
            hi = t.astype(BF16)
            return jnp.concatenate([hi, (t - hi.astype(F32)).astype(BF16)], axis=1)

        for j in range(PAIRS_PER_GROUP):
            k0, k1 = 2 * j, 2 * j + 1
            sl = slice(j * LANE, (j + 1) * LANE)
            col, row = _wide_cs(q, k0, even)
            lm_w = jnp.exp(jnp.where(tri_w, col - row, NEG))
            lmt_w = jnp.exp(jnp.where(trit_w, row - col, NEG))
            dyp, xp = dyv[:, sl], xdt[:, sl]
            dym, xm = _stack_pair(dyp, lo), _stack_pair(xp, lo)
            dm_w = _dot(dyp.astype(BF16), xm, _NT)
            dmt_w = _dot(xp.astype(BF16), dym, _NT)
            mm_w = lm_w * cb_w
            mmt_w = lmt_w * cbt_w
            dxdt = _dot(mmt_w.astype(BF16), dym) + bdh[:, sl]
            g1 = dm_w * lm_w
            g2 = dmt_w * lmt_w
            dcb = dcb + g1[:, :T] + g1[:, T:]
            dcbt = dcbt + g2[:, :T] + g2[:, T:]
            ind_w = jnp.where(ind_l == jnp.where(ind_r < T, k0, k1), 1.0, 0.0).astype(BF16)
            ind_p = jnp.where(ind_l[:T] == jnp.where(ind_r[:T] < SSD_CHUNK // 2, k0, k1), 1.0, 0.0).astype(BF16)
            dcs_acc = dcs_acc + _dot(
                jnp.concatenate([split2(dm_w * mm_w - dmt_w * mmt_w), split2(f_all[:, sl])], axis=1),
                jnp.concatenate([ind_w, ind_w, ind_p, ind_p], axis=0))
            ddt_acc = ddt_acc + _dot(split2(dxdt * xv[:, sl]), jnp.concatenate([ind_p, ind_p], axis=0))
            tot2 = halves(tot_row[:, sl])
            tot_acc = jnp.where(lane1 == k0, tot2[0], jnp.where(lane1 == k1, tot2[1], tot_acc))
            dsk2 = halves(dsk_row[:, sl])
            dsk_acc = jnp.where(lane1 == k0, dsk2[0], jnp.where(lane1 == k1, dsk2[1], dsk_acc))
            dx_ref[:, sl] = dxdt * q["dt_x"][:, sl] + q["dsk_x"][:, sl] * dyp
        dcs_acc = dcs_acc + jnp.where(last_row, tot_acc, 0.0)
        dc_ref[...] = dcg + _dot(dcb.astype(BF16), bg_b)
        db_ref[...] = dbg + _dot(dcbt.astype(BF16), cg_b)
        dda = _cumsum_mm((q["ci"] >= q["ri"]).astype(BF16), dcs_acc)
        ddt = ddt_acc + dda * a
        ddtr = ddt * _sig(dtr_ref[...] + par_ref[0:1, :])
        ddtr_ref[...] = ddtr.astype(BF16)
        acc_ref[0:1, :] += jnp.sum(ddtr, axis=0, keepdims=True)
        acc_ref[1:2, :] += jnp.sum(dda * dt, axis=0, keepdims=True) * a
        acc_ref[2:3, :] += dsk_acc

    def rev(g, c):
        return (nc - 1 - c, g)

    return pl.pallas_call(
        body, name=name, grid=(SSD_GROUPS, nc),
        in_specs=[pl.BlockSpec((T, gw), rev),
                  pl.BlockSpec((T, SSD_STATE), rev),
                  pl.BlockSpec((T, SSD_STATE), rev),
                  pl.BlockSpec((T, LANE), lambda g, c: (nc - 1 - c, g + dtoff)),
                  pl.BlockSpec((8, LANE), lambda g, c: (0, g)),
                  pl.BlockSpec((1, 1, SSD_STATE, gw), lambda g, c: (g, nc - 1 - c, 0, 0)),
                  pl.BlockSpec((T, gw), rev)] + extra_specs,
        out_specs=(pl.BlockSpec((T, gw), rev),
                   pl.BlockSpec((T, SSD_STATE), rev),
                   pl.BlockSpec((T, SSD_STATE), rev),
                   pl.BlockSpec((T, LANE), lambda g, c: (nc - 1 - c, g + ddoff)),
                   pl.BlockSpec((8, LANE), lambda g, c: (0, g))),
        out_shape=(jax.ShapeDtypeStruct((n_rows, SSD_D_INNER), F32),
                   jax.ShapeDtypeStruct((n_rows, BC_DIM), F32),
                   jax.ShapeDtypeStruct((n_rows, BC_DIM), F32),
                   jax.ShapeDtypeStruct((n_rows, width), BF16),
                   jax.ShapeDtypeStruct((8, DT_PAD), F32)),
        input_output_aliases=aliases,
        scratch_shapes=[pltpu.VMEM((SSD_STATE, gw), F32)],
        compiler_params=_cparams("parallel", "arbitrary"),
    )(xs, bm, cm, dtr, par, hs, dy, *extra)


ADAM_ROWS = 256


def _adamw(lands, w, m, v, name):
    na = len(lands)
    n_slots, r, wd = lands[0].shape
    tr = r if r <= 2 * ADAM_ROWS else ADAM_ROWS
    nj = r // tr
    bc1 = 1.0 - ADAM_B1 ** ADAM_STEP
    bc2 = 1.0 - ADAM_B2 ** ADAM_STEP

    def body(*refs):
        l_refs = refs[:na]
        w_ref, m_ref, v_ref, g_ref, d_ref, nm_ref, nv_ref = refs[na:]
        for a in range(na):
            @pl.when(pl.program_id(0) == a)
            def _(l_ref=l_refs[a]):
                g = l_ref[0].astype(F32)
                for s in range(1, n_slots):
                    g = g + l_ref[s].astype(F32)
                mn = ADAM_B1 * m_ref[0] + (1.0 - ADAM_B1) * g
                vn = ADAM_B2 * v_ref[0] + (1.0 - ADAM_B2) * (g * g)
                mh = mn / bc1
                vh = vn / bc2
                g_ref[0] = g
                nm_ref[0] = mn
                nv_ref[0] = vn
                d_ref[0] = -ADAM_LR * (mh / (jnp.sqrt(vh) + ADAM_EPS) + ADAM_WD * w_ref[0])

    def land_spec(a):
        return pl.BlockSpec((n_slots, tr, wd),
                            lambda i, j: (0, jnp.where(i == a, j, jnp.where(i < a, 0, nj - 1)), 0))

    blk = pl.BlockSpec((1, tr, wd), lambda i, j: (i, j, 0))
    shp = jax.ShapeDtypeStruct((na, r, wd), F32)
    return pl.pallas_call(
        body, name=name, grid=(na, nj), in_specs=[land_spec(a) for a in range(na)] + [blk, blk, blk],
        out_specs=(blk, blk, blk, blk), out_shape=(shp, shp, shp, shp),
        compiler_params=_cparams("arbitrary", "arbitrary"),
    )(*lands, w, m, v)


def _mesh_pos():
    return lax.axis_index("x"), lax.axis_index("y"), lax.axis_index("c")


def _peer(pos, k):
    x, y, c = pos
    px = 1 - x if (k >> 2) & 1 else x
    py = 1 - y if (k >> 1) & 1 else y
    pc = 1 - c if k & 1 else c
    return px, py, pc


def _flat(pos):
    return 4 * pos[0] + 2 * pos[1] + pos[2]


HBM_SPEC = pl.BlockSpec(memory_space=pl.ANY)


ROW_SHARDED = ("w_ssd_out", "w_att_out", "w_mix_out", "w_ffn_down")
COL_SHARDED = ("w_in", "w_ffn_gate", "w_ffn_up")
GATHERED = ROW_SHARDED + COL_SHARDED + ("conv_w",)


SEM_SPEC = pl.BlockSpec(memory_space=pltpu.SEMAPHORE)
TOKEN = jax.ShapeDtypeStruct((8, LANE), F32)
SPLIT_EFFECT = pltpu.SideEffectType.DATAFLOW_SIDE_EFFECTING
GATHER_ROWS = "gather_rows"
GATHER_SLOT = "gather_slot"
SCATTER_ROWS = "scatter_rows"
SCATTER_SLOT = "scatter_slot"


def _land_shape(kind, src):
    if kind == GATHER_ROWS:
        return (N_DEV * src.shape[0],) + src.shape[1:]
    if kind == GATHER_SLOT:
        return (N_DEV,) + src.shape
    if kind == SCATTER_ROWS:
        return (N_DEV, src.shape[0] // N_DEV) + src.shape[1:]
    return src.shape


def _views(kind, src_ref, land_ref, pos, k):
    me = _flat(pos)
    if kind == GATHER_ROWS:
        r = src_ref.shape[0]
        return src_ref, land_ref.at[pl.ds(pl.multiple_of(me * r, 16), r), :]
    if kind == GATHER_SLOT:
        return src_ref, land_ref.at[me]
    dev = _flat(_peer(pos, k))
    if kind == SCATTER_ROWS:
        r = land_ref.shape[1]
        return src_ref.at[pl.ds(pl.multiple_of(dev * r, 16), r), :], land_ref.at[k]
    return src_ref.at[dev], land_ref.at[k]


def _hbm(x):
    return pltpu.with_memory_space_constraint(x, pltpu.HBM)


def _exchange_start(items, after, name):
    kinds = [k for k, _ in items]
    srcs = [_hbm(s) for _, s in items]
    lands = [_hbm(lax.empty(_land_shape(k, s), s.dtype)) for k, s in items]
    n = len(items)
    n_copy = n * (N_DEV - 1)

    def body(*refs):
        src_refs, land_refs = refs[:n], refs[n:2 * n]
        send_sems, recv_sems = refs[2 * n + 1], refs[2 * n + 2]
        token_ref = refs[4 * n + 3]
        pos = _mesh_pos()
        for i, kind in enumerate(kinds):
            for k in range(1, N_DEV):
                s, d = _views(kind, src_refs[i], land_refs[i], pos, k)
                j = i * (N_DEV - 1) + k - 1
                pltpu.make_async_remote_copy(src_ref=s, dst_ref=d, send_sem=send_sems.at[j], recv_sem=recv_sems.at[j],
                                             device_id=_peer(pos, k), device_id_type=MESH_ID).start()
        token_ref[...] = jnp.zeros_like(token_ref)

    arrs = srcs + lands
    outs = pl.pallas_call(
        body, name=name,
        in_specs=[HBM_SPEC] * (2 * n + 1),
        out_specs=[SEM_SPEC, SEM_SPEC] + [HBM_SPEC] * (2 * n) + [pl.BlockSpec(memory_space=pltpu.VMEM)],
        out_shape=[pltpu.SemaphoreType.DMA((n_copy,)), pltpu.SemaphoreType.DMA((n_copy,))]
        + [pltpu.HBM(a.shape, a.dtype) for a in arrs] + [TOKEN],
        input_output_aliases={i: 2 + i for i in range(2 * n)},
        compiler_params=pltpu.CompilerParams(has_side_effects=SPLIT_EFFECT),
    )(*arrs, after)
    return {"kinds": kinds, "send": outs[0], "recv": outs[1], "arrs": outs[2:2 + 2 * n], "token": outs[-1]}


def _exchange_wait(ex, after, name):
    kinds = ex["kinds"]
    n = len(kinds)

    def body(*refs):
        src_refs, land_refs = refs[:n], refs[n:2 * n]
        send_sems, recv_sems = refs[2 * n], refs[2 * n + 1]
        token_ref = refs[-1]
        pos = _mesh_pos()
        for i, kind in enumerate(kinds):
            for k in range(1, N_DEV):
                s, d = _views(kind, src_refs[i], land_refs[i], pos, k)
                j = i * (N_DEV - 1) + k - 1
                cp = pltpu.make_async_remote_copy(src_ref=s, dst_ref=d, send_sem=send_sems.at[j],
                                                  recv_sem=recv_sems.at[j], device_id=_peer(pos, k),
                                                  device_id_type=MESH_ID)
                cp.wait_send()
                cp.wait_recv()
        token_ref[...] = jnp.zeros_like(token_ref)

    outs = pl.pallas_call(
        body, name=name,
        in_specs=[HBM_SPEC] * (2 * n) + [SEM_SPEC, SEM_SPEC, HBM_SPEC],
        out_specs=[HBM_SPEC] * (2 * n) + [pl.BlockSpec(memory_space=pltpu.VMEM)],
        out_shape=[pltpu.HBM(a.shape, a.dtype) for a in ex["arrs"]] + [TOKEN],
        input_output_aliases={i: i for i in range(2 * n)},
        compiler_params=pltpu.CompilerParams(has_side_effects=SPLIT_EFFECT),
    )(*ex["arrs"], ex["send"], ex["recv"], after)
    lands = [_place_own(k, s, d) for k, s, d in zip(kinds, outs[:n], outs[n:2 * n])]
    return lands, outs[-1]


def _place_own(kind, src, land):
    me = _flat(_mesh_pos())
    zeros = (0,) * (src.ndim - 1)
    if kind == GATHER_ROWS:
        return lax.dynamic_update_slice(land, src, (me * src.shape[0],) + zeros)
    if kind == GATHER_SLOT:
        return lax.dynamic_update_slice(land, src[None], (me,) + (0,) * src.ndim)
    if kind == SCATTER_ROWS:
        r = land.shape[1]
        own = lax.dynamic_slice(src, (me * r,) + zeros, (r,) + src.shape[1:])
    else:
        own = lax.dynamic_index_in_dim(src, me, 0, keepdims=False)
    return lax.dynamic_update_slice(land, own[None], (0,) * land.ndim)


def _all_gather_small(x, name):
    r, w = x.shape

    def body(x_ref, out_ref, send_sems, recv_sems):
        pos = _mesh_pos()
        me = _flat(pos)
        copies = []
        for k in range(1, N_DEV):
            cp = pltpu.make_async_remote_copy(
                src_ref=x_ref, dst_ref=out_ref.at[me], send_sem=send_sems.at[k - 1], recv_sem=recv_sems.at[k - 1],
                device_id=_peer(pos, k), device_id_type=MESH_ID)
            cp.start()
            copies.append(cp)
        out_ref[me] = x_ref[...]
        for cp in copies:
            cp.wait()

    vmem = pl.BlockSpec(memory_space=pltpu.VMEM)
    return pl.pallas_call(
        body, name=name, in_specs=[vmem], out_specs=vmem,
        out_shape=jax.ShapeDtypeStruct((N_DEV, r, w), x.dtype),
        scratch_shapes=[pltpu.SemaphoreType.DMA((N_DEV - 1,)), pltpu.SemaphoreType.DMA((N_DEV - 1,))],
        compiler_params=pltpu.CompilerParams(has_side_effects=True),
    )(x)


def _cols(g, lo, hi):
    c = g.shape[-1]
    parts = []
    for d in range(N_DEV):
        a, b = max(lo, d * c), min(hi, (d + 1) * c)
        if a < b:
            parts.append(g[d, :, a - d * c:b - d * c])
    return parts[0] if len(parts) == 1 else jnp.concatenate(parts, axis=1)


def _col_chunks(g):
    c = g.shape[-1] // N_DEV
    return jnp.stack([g[:, d * c:(d + 1) * c] for d in range(N_DEV)])


IN_PART = ("w_in", "conv_w")
OUT_PART = ROW_SHARDED + ("w_ffn_gate", "w_ffn_up")


def _gather_items(w, names, l):
    items = []
    for n in names:
        blk = w[n][l] if n == "conv_w" else w[n][l].astype(BF16)
        items.append((GATHER_ROWS if n in ROW_SHARDED else GATHER_SLOT, blk))
    return items


def _scatter_items(grads, names):
    return [(SCATTER_ROWS, grads[n]) if n in ROW_SHARDED else (SCATTER_SLOT, _col_chunks(grads[n]))
            for n in names]


SMALL = ("ln_in_g", "ln_in_b", "conv_b", "dt_bias", "a_log", "d_skip", "ssd_norm_w", "att_sinks",
         "ln_mix_g", "ln_mix_b", "ln_ffn_g", "ln_ffn_b")


def _pack_small(vals):
    flat = jnp.concatenate([vals[n].reshape(-1) for n in SMALL])
    n = flat.shape[0]
    rows = -(-n // LANE)
    rows = -(-rows // 8) * 8
    return jnp.pad(flat, (0, rows * LANE - n)).reshape(rows, LANE)


def _unpack_small(buf, shapes):
    flat = buf.reshape(-1)
    off = 0
    out = {}
    for n in SMALL:
        cnt = math.prod(shapes[n])
        out[n] = flat[off:off + cnt].reshape(shapes[n])
        off += cnt
    return out


def _to_group_major(v):
    lead = v.shape[:-1]
    t = v.reshape(lead + (SSD_GROUPS, HEADS_PER_GROUP))
    t = jnp.pad(t, [(0, 0)] * len(lead) + [(0, 0), (0, LANE - HEADS_PER_GROUP)])
    return t.reshape(lead + (DT_PAD,))


def _from_group_major(v):
    lead = v.shape[:-1]
    return v.reshape(lead + (SSD_GROUPS, LANE))[..., :HEADS_PER_GROUP].reshape(lead + (SSD_HEADS,))


def _rows8(v):
    return jnp.pad(v, ((0, 8 - v.shape[0]), (0, 0)))


IN_OFFS = {"q": (0, 1024), "kv": (1024, 1280), "z": (1280, 3328), "xs": (3328, 5376), "b": (5376, 5888),
           "c": (5888, 6400), "dt": (6400, 6432), "gl": (6432, 8480)}
PIECES = ("q", "kv", "z", "xs", "b", "c", "dt", "gl")


CAT = ("z", "xs", "gl", "q", "b", "c", "dt", "kv")
CAT_WIDTH = {"q": 1024, "z": 2048, "xs": 2048, "gl": 2048, "b": 512, "c": 512, "kv": 256, "dt": DT_PAD}
CAT_OFF = {p: sum(CAT_WIDTH[q] for q in CAT[:i]) for i, p in enumerate(CAT)}
CAT_DIM = sum(CAT_WIDTH.values())
MAIN_DIM = CAT_OFF["kv"]


def _cat_w_in(g):
    pieces = {p: _cols(g, lo, hi) for p, (lo, hi) in IN_OFFS.items()}
    pieces["dt"] = _to_group_major(pieces["dt"])
    return jnp.concatenate([pieces[p] for p in CAT], axis=1)


def _uncat_dw_in(dw):
    pieces = {p: dw[:, CAT_OFF[p]:CAT_OFF[p] + CAT_WIDTH[p]] for p in CAT}
    pieces["dt"] = _from_group_major(pieces["dt"])
    return jnp.concatenate([pieces[p] for p in PIECES], axis=1)


def _params_out(W):
    p = {n: W[n] for n in ROW_SHARDED}
    for n in ("w_ffn_gate", "w_ffn_up"):
        p[n] = _cols(W[n], 0, FFN_HIDDEN)
    return p


def _params_in(l, W, sm):
    p = {"w_cat": _cat_w_in(W["w_in"])}
    cw = _cols(W["conv_w"], 0, SSD_D_INNER + 2 * BC_DIM)
    cb = sm["conv_b"][l]
    segs = {"xs": (0, 2048), "b": (2048, 2560), "c": (2560, 3072)}
    p["conv_w8"] = {s: _rows8(cw[:, lo:hi]) for s, (lo, hi) in segs.items()}
    p["conv_b8"] = {s: _rows8(cb[None, lo:hi]) for s, (lo, hi) in segs.items()}
    p["ssd_par"] = _rows8(jnp.stack([_to_group_major(sm["dt_bias"][l]), _to_group_major(sm["a_log"][l]),
                                     _to_group_major(sm["d_skip"][l])]))
    p["norm_w"] = sm["ssd_norm_w"][l]
    p["sinks8"] = _rows8(jnp.pad(sm["att_sinks"][l], (0, LANE - ATT_HEADS))[None])
    for n in ("ln_mix_g", "ln_mix_b", "ln_ffn_g", "ln_ffn_b"):
        p[n] = sm[n][l]
    return p


def _fwd_mixers(h0, p, l, dep=None):
    tag = f"l{l}_"
    a = {"h0": h0}
    proj = _mm(h0, p["w_cat"], "nn", tag + "proj", dep=dep)
    for pc in CAT:
        a[pc] = (proj, CAT_OFF[pc], CAT_WIDTH[pc])
    for s in ("xs", "b", "c"):
        a[s + "c"] = _conv_fwd(a[s], p["conv_w8"][s], p["conv_b8"][s], tag + "conv_" + s)
    a["y"], a["hs"] = _ssd_fwd(a["xsc"], a["bc"], a["cc"], a["dt"], p["ssd_par"], tag + "ssd_fwd")
    a["yn"] = _gnorm_fwd(a["y"], a["z"], p["norm_w"], tag + "gnorm")
    a["att"] = _att_fwd(a["q"], a["kv"], p["sinks8"], tag + "att_fwd")
    return a


def _fwd_out(a, p, l, dep=None):
    tag = f"l{l}_"
    h0 = a["h0"]
    a["ya"] = _mm(a["yn"], p["w_ssd_out"], "nn", tag + "ssd_out", dep=dep)
    a["yb"] = _mm(a["att"], p["w_att_out"], "nn", tag + "att_out", dep=dep)
    a["merged"] = _merge_fwd(a["gl"], a["ya"], a["yb"], tag + "merge")
    a["mix"] = _mm(a["merged"], p["w_mix_out"], "nn", tag + "mix_out")
    a["h1"] = _ln_fwd(h0, a["mix"], p["ln_mix_g"], p["ln_mix_b"], ALPHA, tag + "ln_mix")
    a["fg"], a["fu"], a["act"] = _ffn_in(a["h1"], p["w_ffn_gate"], p["w_ffn_up"], tag + "ffn_in")
    a["ffn"] = _mm(a["act"], p["w_ffn_down"], "nn", tag + "ffn_down")
    a["h2"] = _ln_fwd(a["h1"], a["ffn"], p["ln_ffn_g"], p["ln_ffn_b"], ALPHA, tag + "ln_ffn")
    return a


def _dw(x, dy, name, dep=None):
    return _mm(x, dy, "tn", name, out_dtype=BF16, dep=dep)


def _bwd_out(a, p, dh2, l, dep=None):
    tag = f"l{l}_b_"
    gw, gs = {}, {}
    du2, acc = _ln_bwd(a["h1"], a["ffn"], p["ln_ffn_g"], dh2, ALPHA, tag + "ln_ffn")
    gs["ln_ffn_g"], gs["ln_ffn_b"] = acc[0], acc[1]
    gw["w_ffn_down"] = _dw(a["act"], du2, tag + "dw_down", dep=dep)
    dfg, dfu = _ffn_dact(du2, p["w_ffn_down"], a["fg"], a["fu"], tag + "ffn_dact", dep=dep)
    gw["w_ffn_gate"] = _dw(a["h1"], dfg, tag + "dw_gate")
    gw["w_ffn_up"] = _dw(a["h1"], dfu, tag + "dw_up")
    dh1 = _mm(dfg, p["w_ffn_gate"], "nt", tag + "dh1_gate", add=du2, add_scale=ALPHA)
    dh1 = _mm(dfu, p["w_ffn_up"], "nt", tag + "dh1_up", add=dh1)
    du1, acc = _ln_bwd(a["h0"], a["mix"], p["ln_mix_g"], dh1, ALPHA, tag + "ln_mix")
    gs["ln_mix_g"], gs["ln_mix_b"] = acc[0], acc[1]
    gw["w_mix_out"] = _dw(a["merged"], du1, tag + "dw_mix")
    dmerged = _mm(du1, p["w_mix_out"], "nt", tag + "dmerged")
    dya, dyb, dproj = _merge_bwd(a["gl"], a["ya"], a["yb"], dmerged, tag + "merge",
                                 (None, CAT_OFF["gl"], MAIN_DIM))
    gw["w_ssd_out"] = _dw(a["yn"], dya, tag + "dw_ssd")
    gw["w_att_out"] = _dw(a["att"], dyb, tag + "dw_att")
    return {"du1": du1, "dya": dya, "dyb": dyb, "dproj": dproj}, gw, gs


def _bwd_mixers(a, p, carry, l, dep=None):
    tag = f"l{l}_b_"
    gs = {}
    du1, dproj = carry["du1"], carry["dproj"]

    def win(pc):
        return (dproj, CAT_OFF[pc], MAIN_DIM)

    dyn = _mm(carry["dya"], p["w_ssd_out"], "nt", tag + "dyn", dep=dep)
    datt = _mm(carry["dyb"], p["w_att_out"], "nt", tag + "datt", out_dtype=BF16, dep=dep)
    dproj, dkv, acc = _att_bwd(a["q"], a["kv"], p["sinks8"], a["att"], datt, tag + "att", win("q"))
    gs["att_sinks"] = acc[0, :ATT_HEADS]
    dy, dproj, acc = _gnorm_bwd(a["y"], a["z"], p["norm_w"], dyn, tag + "gnorm", win("z"))
    gs["ssd_norm_w"] = acc[0]
    dxs, dbm, dcm, dproj, acc = _ssd_bwd(a["xsc"], a["bc"], a["cc"], a["dt"], p["ssd_par"], a["hs"], dy,
                                         tag + "ssd", win("dt"))
    gs["dt_bias"], gs["a_log"], gs["d_skip"] = (_from_group_major(acc[i]) for i in range(3))
    dconv_w, dconv_b = [], []
    for s, dout in (("xs", dxs), ("b", dbm), ("c", dcm)):
        dc, acc = _conv_bwd_pre(a[s], p["conv_w8"][s], p["conv_b8"][s], dout, tag + "conv_pre_" + s)
        dconv_w.append(acc[:CONV_TAPS])
        dconv_b.append(acc[CONV_TAPS])
        dproj = _conv_bwd_in(dc, p["conv_w8"][s], tag + "conv_in_" + s, win(s))
    gconv = jnp.concatenate(dconv_w, axis=1)
    gs["conv_b"] = jnp.concatenate(dconv_b)
    w_main, w_kv = p["w_cat"][:, :MAIN_DIM], p["w_cat"][:, MAIN_DIM:]
    dw = jnp.concatenate([_dw(a["h0"], dproj, tag + "dw_in"), _dw(a["h0"], dkv, tag + "dw_in_kv")], axis=1)

    def grad_h0(dep=None):
        dh0 = _mm(dproj, w_main, "nt", tag + "dh0", add=du1, add_scale=ALPHA, dep=dep)
        return _mm(dkv, w_kv, "nt", tag + "dh0_kv", add=dh0)

    return grad_h0, _uncat_dw_in(dw), gconv, gs


def _step(x, target, w, m, v):
    x2 = x[0]
    t2 = target[0]
    tok = jnp.zeros(TOKEN.shape, TOKEN.dtype)

    ex = _exchange_start(_gather_items(w, IN_PART, 0), tok, "gather_l0_in_start")
    h = _ln_fwd(x2, None, w["ln_in_g"], w["ln_in_b"], 1.0, "ln_in")
    lands, tok = _exchange_wait(ex, h, "gather_l0_in_wait")
    p0 = _params_in(0, dict(zip(IN_PART, lands)), w)
    ex = _exchange_start(_gather_items(w, OUT_PART, 0) + _gather_items(w, IN_PART, 1), tok,
                         "gather_l0_out_l1_in_start")
    a0 = _fwd_mixers(h, p0, 0, dep=ex["token"])
    lands, tok = _exchange_wait(ex, a0["att"], "gather_l0_out_l1_in_wait")
    p0.update(_params_out(dict(zip(OUT_PART, lands))))
    p1 = _params_in(1, dict(zip(IN_PART, lands[len(OUT_PART):])), w)
    ex = _exchange_start(_gather_items(w, OUT_PART, 1), tok, "gather_l1_out_start")
    a0 = _fwd_out(a0, p0, 0, dep=ex["token"])
    lands, tok = _exchange_wait(ex, a0["h2"], "gather_l1_out_wait")
    p1.update(_params_out(dict(zip(OUT_PART, lands))))
    a1 = _fwd_out(_fwd_mixers(a0["h2"], p1, 1), p1, 1)

    sse, dh = _loss_fwd_bwd(a1["h2"], t2, "loss")
    loss = lax.psum(0.5 / D_MODEL * sse[0, 0], ("x", "y", "c"))

    carry, gw1, gs1 = _bwd_out(a1, p1, dh, 1)
    grad_h0, gw1["w_in"], gw1["conv_w"], gs = _bwd_mixers(a1, p1, carry, 1)
    dh = grad_h0()
    gs1.update(gs)
    ex1 = _exchange_start(_scatter_items(gw1, GATHERED), tok, "scatter_l1_start")
    carry, gw0, gs0 = _bwd_out(a0, p0, dh, 0, dep=ex1["token"])
    lands, tok = _exchange_wait(ex1, carry["dyb"], "scatter_l1_wait")
    land1 = dict(zip(GATHERED, lands))
    ex0 = _exchange_start(_scatter_items(gw0, OUT_PART), tok, "scatter_l0_out_start")
    grad_h0, gw0["w_in"], gw0["conv_w"], gs = _bwd_mixers(a0, p0, carry, 0, dep=ex0["token"])
    gs0.update(gs)
    lands, tok = _exchange_wait(ex0, gw0["w_in"], "scatter_l0_out_wait")
    land0 = dict(zip(OUT_PART, lands))
    ex0 = _exchange_start(_scatter_items(gw0, IN_PART), tok, "scatter_l0_in_start")
    dh = grad_h0(dep=ex0["token"])
    grad_x2, acc = _ln_bwd(x2, None, w["ln_in_g"], dh, 1.0, "ln_in_b")

    outs = [{} for _ in range(4)]

    def update(names):
        res = None
        for n in names:
            res = _adamw([land0[n], land1[n]], w[n], m[n], v[n], "adamw_" + n)
            for o, t in zip(outs, res):
                o[n] = t
        return res[1]

    update(OUT_PART)
    gsm = {"ln_in_g": acc[0], "ln_in_b": acc[1]}
    for n in SMALL[2:]:
        gsm[n] = jnp.stack([gs0[n], gs1[n]])
    small_shapes = {n: w[n].shape for n in SMALL}
    land_s = _all_gather_small(_pack_small(gsm), "small_grads_all_gather")
    res = _adamw([land_s], _pack_small(w)[None], _pack_small(m)[None], _pack_small(v)[None], "adamw_small")
    for o, t in zip(outs, res):
        o.update(_unpack_small(t[0], small_shapes))
    lands, _ = _exchange_wait(ex0, res[1], "scatter_l0_in_wait")
    land0.update(zip(IN_PART, lands))
    update(IN_PART)
    return loss, grad_x2[None], outs


WEIGHT_NAMES = ("ln_in_g", "ln_in_b", "w_in", "conv_w", "conv_b", "dt_bias", "a_log", "d_skip", "ssd_norm_w",
                "att_sinks", "w_ssd_out", "w_att_out", "w_mix_out", "ln_mix_g", "ln_mix_b", "w_ffn_gate",
                "w_ffn_up", "w_ffn_down", "ln_ffn_g", "ln_ffn_b")


def kernel(x, ln_in_g, ln_in_b, w_in, conv_w, conv_b, dt_bias, a_log, d_skip, ssd_norm_w, att_sinks, w_ssd_out, w_att_out, w_mix_out, ln_mix_g, ln_mix_b, w_ffn_gate, w_ffn_up, w_ffn_down, ln_ffn_g, ln_ffn_b, loss_target, m_ln_in_g, m_ln_in_b, m_w_in, m_conv_w, m_conv_b, m_dt_bias, m_a_log, m_d_skip, m_ssd_norm_w, m_att_sinks, m_w_ssd_out, m_w_att_out, m_w_mix_out, m_ln_mix_g, m_ln_mix_b, m_w_ffn_gate, m_w_ffn_up, m_w_ffn_down, m_ln_ffn_g, m_ln_ffn_b, v_ln_in_g, v_ln_in_b, v_w_in, v_conv_w, v_conv_b, v_dt_bias, v_a_log, v_d_skip, v_ssd_norm_w, v_att_sinks, v_w_ssd_out, v_w_att_out, v_w_mix_out, v_ln_mix_g, v_ln_mix_b, v_w_ffn_gate, v_w_ffn_up, v_w_ffn_down, v_ln_ffn_g, v_ln_ffn_b):
    w = dict(zip(WEIGHT_NAMES, (ln_in_g, ln_in_b, w_in, conv_w, conv_b, dt_bias, a_log, d_skip, ssd_norm_w,
                                att_sinks, w_ssd_out, w_att_out, w_mix_out, ln_mix_g, ln_mix_b, w_ffn_gate,
                                w_ffn_up, w_ffn_down, ln_ffn_g, ln_ffn_b)))
    m = dict(zip(WEIGHT_NAMES, (m_ln_in_g, m_ln_in_b, m_w_in, m_conv_w, m_conv_b, m_dt_bias, m_a_log, m_d_skip,
                                m_ssd_norm_w, m_att_sinks, m_w_ssd_out, m_w_att_out, m_w_mix_out, m_ln_mix_g,
                                m_ln_mix_b, m_w_ffn_gate, m_w_ffn_up, m_w_ffn_down, m_ln_ffn_g, m_ln_ffn_b)))
    v = dict(zip(WEIGHT_NAMES, (v_ln_in_g, v_ln_in_b, v_w_in, v_conv_w, v_conv_b, v_dt_bias, v_a_log, v_d_skip,
                                v_ssd_norm_w, v_att_sinks, v_w_ssd_out, v_w_att_out, v_w_mix_out, v_ln_mix_g,
                                v_ln_mix_b, v_w_ffn_gate, v_w_ffn_up, v_w_ffn_down, v_ln_ffn_g, v_ln_ffn_b)))
    loss, grad_x, outs = _step(x, loss_target, w, m, v)
    result = [loss, grad_x]
    for o in outs:
        result.extend(o[n] for n in WEIGHT_NAMES)
    return tuple(result)
```

```python
import math

import jax
import jax.numpy as jnp
from jax import lax
from jax.experimental import pallas as pl
from jax.experimental.pallas import tpu as pltpu

F32 = jnp.float32
BF16 = jnp.bfloat16

D_MODEL = 1024
DEPTH = 2
N_DEV = 8
ATT_HEADS = 16
ATT_KV_HEADS = 2
ATT_HEAD_DIM = 64
ATT_BLOCK = 128
SSD_D_INNER = 2048
SSD_HEADS = 32
SSD_GROUPS = 4
SSD_STATE = 128
SSD_CHUNK = 128
FFN_HIDDEN = 2816
LN_EPS = 1e-5
RMS_EPS = 1e-5
ALPHA = (2 * DEPTH) ** 0.25
Q_DIM = 1024
BC_DIM = 512
DT_PAD = 512

ADAM_LR = 0.001
ADAM_B1 = 0.9
ADAM_B2 = 0.999
ADAM_EPS = 1e-08
ADAM_WD = 0.01
ADAM_STEP = 10

LANE = 128
VMEM_LIMIT = 48 * 1024 * 1024
NEG = -1e30

_NN = (((1,), (0,)), ((), ()))
_NT = (((1,), (1,)), ((), ()))
_TN = (((0,), (0,)), ((), ()))
MESH_ID = pl.DeviceIdType.MESH


def _dot(a, b, dims=_NN):
    return lax.dot_general(a, b, dims, preferred_element_type=F32)


def _sig(x):
    return 1.0 / (1.0 + jnp.exp(-x))


def _softplus(x):
    return jnp.maximum(x, 0.0) + jnp.log(1.0 + jnp.exp(-jnp.abs(x)))


def _cparams(*sem):
    return pltpu.CompilerParams(dimension_semantics=sem, vmem_limit_bytes=VMEM_LIMIT)


def _pick(n, cap):
    if n <= cap:
        return n
    best = None
    for t in range(LANE, cap + 1, LANE):
        if n % t == 0:
            best = t
    assert best is not None, (n, cap)
    return best


def _tile(n):
    if n <= 1024 or n % 1024 == 0:
        return min(n, 1024)
    return _pick(n, 1408)


def _rows(n):
    return min(512, n)


def _window(x):
    return x if isinstance(x, tuple) else (x, 0, x.shape[1])


def _into(into, n_in, out_idx):
    buf, col0, width = into
    if buf is None:
        return [], [], {}, col0, width
    return [buf], [pl.BlockSpec(memory_space=pl.ANY)], {n_in: out_idx}, col0, width


def _mm(a, b, mode, name, add=None, add_scale=1.0, out_dtype=F32, dep=None):
    if mode == "nn":
        m, k = a.shape
        n = b.shape[1]
    elif mode == "nt":
        m, k = a.shape
        n = b.shape[0]
    else:
        k, m = a.shape
        n = b.shape[1]
    tm = _tile(m)
    tn = _pick(n, 2176) if mode == "tn" and n > 1024 else _tile(n)
    tk = _pick(k, 2176) if mode == "nt" and a.dtype == BF16 and k > 2816 else _tile(k)
    nk = k // tk
    has_add = add is not None
    dims = {"nn": _NN, "nt": _NT, "tn": _TN}[mode]

    def body(*refs):
        if dep is not None:
            refs = refs[:-3] + refs[-2:]
        if has_add:
            a_ref, b_ref, add_ref, o_ref, acc_ref = refs
        else:
            a_ref, b_ref, o_ref, acc_ref = refs
        kk = pl.program_id(2)

        @pl.when(kk == 0)
        def _():
            if has_add:
                acc_ref[...] = add_scale * add_ref[...].astype(F32)
            else:
                acc_ref[...] = jnp.zeros_like(acc_ref)

        acc_ref[...] += _dot(a_ref[...].astype(BF16), b_ref[...].astype(BF16), dims)

        @pl.when(kk == nk - 1)
        def _():
            o_ref[...] = acc_ref[...].astype(o_ref.dtype)

    if mode == "nn":
        a_spec = pl.BlockSpec((tm, tk), lambda i, j, kk: (i, kk))
        b_spec = pl.BlockSpec((tk, tn), lambda i, j, kk: (kk, j))
    elif mode == "nt":
        a_spec = pl.BlockSpec((tm, tk), lambda i, j, kk: (i, kk))
        b_spec = pl.BlockSpec((tn, tk), lambda i, j, kk: (j, kk))
    else:
        a_spec = pl.BlockSpec((tk, tm), lambda i, j, kk: (kk, i))
        b_spec = pl.BlockSpec((tk, tn), lambda i, j, kk: (kk, j))
    o_spec = pl.BlockSpec((tm, tn), lambda i, j, kk: (i, j))
    in_specs = [a_spec, b_spec] + ([o_spec] if has_add else [])
    args = (a, b) + ((add,) if has_add else ())
    if dep is not None:
        in_specs.append(pl.BlockSpec((8, LANE), lambda i, j, kk: (0, 0)))
        args += (dep,)
    return pl.pallas_call(
        body, name=name, grid=(m // tm, n // tn, nk),
        in_specs=in_specs, out_specs=o_spec,
        out_shape=jax.ShapeDtypeStruct((m, n), out_dtype),
        scratch_shapes=[pltpu.VMEM((tm, tn), F32)],
        compiler_params=_cparams("parallel", "parallel", "arbitrary"),
    )(*args)


def _vec_spec(width):
    return pl.BlockSpec((1, width), lambda i: (0, 0))


def _ln_fwd(a, b, gamma, beta, alpha, name):
    n_rows, dm = a.shape
    has_b = b is not None

    def body(*refs):
        if has_b:
            a_ref, b_ref, g_ref, be_ref, o_ref = refs
            u = alpha * a_ref[...] + b_ref[...]
        else:
            a_ref, g_ref, be_ref, o_ref = refs
            u = a_ref[...]
        mu = jnp.mean(u, axis=-1, keepdims=True)
        d = u - mu
        var = jnp.mean(d * d, axis=-1, keepdims=True)
        o_ref[...] = d * lax.rsqrt(var + LN_EPS) * g_ref[...] + be_ref[...]

    row = pl.BlockSpec((_rows(n_rows),dm), lambda i: (i, 0))
    in_specs = [row] + ([row] if has_b else []) + [_vec_spec(dm), _vec_spec(dm)]
    args = (a,) + ((b,) if has_b else ()) + (gamma.reshape(1, dm), beta.reshape(1, dm))
    return pl.pallas_call(
        body, name=name, grid=(n_rows // _rows(n_rows),), in_specs=in_specs, out_specs=row,
        out_shape=jax.ShapeDtypeStruct((n_rows, dm), F32),
        compiler_params=_cparams("parallel"),
    )(*args)


def _ln_bwd(a, b, gamma, dy, alpha, name):
    n_rows, dm = a.shape
    has_b = b is not None

    def body(*refs):
        if has_b:
            a_ref, b_ref, g_ref, dy_ref, du_ref, acc_ref = refs
            u = alpha * a_ref[...] + b_ref[...]
        else:
            a_ref, g_ref, dy_ref, du_ref, acc_ref = refs
            u = a_ref[...]

        @pl.when(pl.program_id(0) == 0)
        def _():
            acc_ref[...] = jnp.zeros_like(acc_ref)

        mu = jnp.mean(u, axis=-1, keepdims=True)
        d = u - mu
        var = jnp.mean(d * d, axis=-1, keepdims=True)
        rstd = lax.rsqrt(var + LN_EPS)
        xhat = d * rstd
        dyv = dy_ref[...]
        acc_ref[0:1, :] += jnp.sum(dyv * xhat, axis=0, keepdims=True)
        acc_ref[1:2, :] += jnp.sum(dyv, axis=0, keepdims=True)
        dxh = dyv * g_ref[...]
        m1 = jnp.mean(dxh, axis=-1, keepdims=True)
        m2 = jnp.mean(dxh * xhat, axis=-1, keepdims=True)
        du_ref[...] = rstd * (dxh - m1 - xhat * m2)

    row = pl.BlockSpec((_rows(n_rows),dm), lambda i: (i, 0))
    in_specs = [row] + ([row] if has_b else []) + [_vec_spec(dm), row]
    args = (a,) + ((b,) if has_b else ()) + (gamma.reshape(1, dm), dy)
    return pl.pallas_call(
        body, name=name, grid=(n_rows // _rows(n_rows),), in_specs=in_specs,
        out_specs=(row, pl.BlockSpec((8, dm), lambda i: (0, 0))),
        out_shape=(jax.ShapeDtypeStruct((n_rows, dm), F32), jax.ShapeDtypeStruct((8, dm), F32)),
        compiler_params=_cparams("arbitrary"),
    )(*args)


def _loss_fwd_bwd(y, target, name):
    n_rows, dm = y.shape

    def body(y_ref, t_ref, acc_ref, dy_ref):
        @pl.when(pl.program_id(0) == 0)
        def _():
            acc_ref[...] = jnp.zeros_like(acc_ref)

        d = y_ref[...] - t_ref[...]
        acc_ref[...] += jnp.sum(d * d)
        dy_ref[...] = d * (1.0 / dm)

    row = pl.BlockSpec((_rows(n_rows),dm), lambda i: (i, 0))
    return pl.pallas_call(
        body, name=name, grid=(n_rows // _rows(n_rows),), in_specs=[row, row],
        out_specs=(pl.BlockSpec((8, LANE), lambda i: (0, 0)), row),
        out_shape=(jax.ShapeDtypeStruct((8, LANE), F32), jax.ShapeDtypeStruct((n_rows, dm), F32)),
        compiler_params=_cparams("arbitrary"),
    )(y, target)


FFN_ROWS = 512


def _ffn_in(h, wg, wu, name, dep=None):
    m, k = h.shape
    n = wg.shape[1]
    tm, tn = min(FFN_ROWS, m), _tile(n)

    def body(*refs):
        h_ref, wg_ref, wu_ref = refs[:3]
        g_ref, u_ref, act_ref = refs[-3:]
        hb = h_ref[...].astype(BF16)
        g = _dot(hb, wg_ref[...])
        u = _dot(hb, wu_ref[...])
        g_ref[...] = g
        u_ref[...] = u
        act_ref[...] = (g * _sig(g) * u).astype(BF16)

    rows = pl.BlockSpec((tm, k), lambda j, i: (i, 0))
    wcol = pl.BlockSpec((k, tn), lambda j, i: (0, j))
    out = pl.BlockSpec((tm, tn), lambda j, i: (i, j))
    in_specs, args = [rows, wcol, wcol], (h, wg, wu)
    if dep is not None:
        in_specs.append(pl.BlockSpec((8, LANE), lambda j, i: (0, 0)))
        args += (dep,)
    return pl.pallas_call(
        body, name=name, grid=(n // tn, m // tm), in_specs=in_specs, out_specs=(out, out, out),
        out_shape=(jax.ShapeDtypeStruct((m, n), F32), jax.ShapeDtypeStruct((m, n), F32),
                   jax.ShapeDtypeStruct((m, n), BF16)),
        compiler_params=_cparams("parallel", "parallel"),
    )(*args)


def _ffn_dact(dy, wd, g, u, name, dep=None):
    m, k = dy.shape
    n = wd.shape[0]
    tm, tn = min(FFN_ROWS, m), _tile(n)

    def body(*refs):
        dy_ref, wd_ref, g_ref, u_ref = refs[:4]
        dg_ref, du_ref = refs[-2:]
        da = _dot(dy_ref[...].astype(BF16), wd_ref[...], _NT)
        gv = g_ref[...]
        s = _sig(gv)
        dg_ref[...] = (da * u_ref[...] * (s * (1.0 + gv * (1.0 - s)))).astype(BF16)
        du_ref[...] = (da * gv * s).astype(BF16)

    rows = pl.BlockSpec((tm, k), lambda j, i: (i, 0))
    wrow = pl.BlockSpec((tn, k), lambda j, i: (j, 0))
    out = pl.BlockSpec((tm, tn), lambda j, i: (i, j))
    in_specs, args = [rows, wrow, out, out], (dy, wd, g, u)
    if dep is not None:
        in_specs.append(pl.BlockSpec((8, LANE), lambda j, i: (0, 0)))
        args += (dep,)
    return pl.pallas_call(
        body, name=name, grid=(n // tn, m // tm), in_specs=in_specs, out_specs=(out, out),
        out_shape=(jax.ShapeDtypeStruct((m, n), BF16), jax.ShapeDtypeStruct((m, n), BF16)),
        compiler_params=_cparams("parallel", "parallel"),
    )(*args)


def _gate_specs(gl, n_rows, dm):
    arr, g0, _ = _window(gl)
    return arr, [pl.BlockSpec((_rows(n_rows), dm), lambda i, k=k: (i, g0 // dm + k)) for k in range(2)]


def _merge_fwd(gl, ya, yb, name):
    n_rows, dm = ya.shape
    gl_arr, gspecs = _gate_specs(gl, n_rows, dm)

    def body(ga_ref, gb_ref, ya_ref, yb_ref, o_ref):
        o_ref[...] = (_sig(ga_ref[...]) * ya_ref[...] + _sig(gb_ref[...]) * yb_ref[...]).astype(BF16)

    row = pl.BlockSpec((_rows(n_rows),dm), lambda i: (i, 0))
    return pl.pallas_call(
        body, name=name, grid=(n_rows // _rows(n_rows),), in_specs=gspecs + [row, row], out_specs=row,
        out_shape=jax.ShapeDtypeStruct((n_rows, dm), BF16),
        compiler_params=_cparams("parallel"),
    )(gl_arr, gl_arr, ya, yb)


def _merge_bwd(gl, ya, yb, dmerged, name, into):
    n_rows, dm = ya.shape
    gl_arr, gspecs = _gate_specs(gl, n_rows, dm)
    extra, extra_specs, aliases, col0, width = _into(into, 5, 2)

    def body(*refs):
        ga_ref, gb_ref, ya_ref, yb_ref, dm_ref = refs[:5]
        dya_ref, dyb_ref, dgl_ref = refs[-3:]
        ga = _sig(ga_ref[...])
        gb = _sig(gb_ref[...])
        dmv = dm_ref[...]
        dya_ref[...] = (dmv * ga).astype(BF16)
        dyb_ref[...] = (dmv * gb).astype(BF16)
        dgl_ref[:, :dm] = (dmv * ya_ref[...] * ga * (1.0 - ga)).astype(BF16)
        dgl_ref[:, dm:] = (dmv * yb_ref[...] * gb * (1.0 - gb)).astype(BF16)

    row = pl.BlockSpec((_rows(n_rows),dm), lambda i: (i, 0))
    row2 = pl.BlockSpec((_rows(n_rows),2 * dm), lambda i: (i, col0 // (2 * dm)))
    return pl.pallas_call(
        body, name=name, grid=(n_rows // _rows(n_rows),), in_specs=gspecs + [row, row, row] + extra_specs,
        out_specs=(row, row, row2),
        out_shape=(jax.ShapeDtypeStruct((n_rows, dm), BF16), jax.ShapeDtypeStruct((n_rows, dm), BF16),
                   jax.ShapeDtypeStruct((n_rows, width), BF16)),
        input_output_aliases=aliases,
        compiler_params=_cparams("parallel"),
    )(gl_arr, gl_arr, ya, yb, dmerged, *extra)


CONV_TAPS = 4
CONV_COLS = 512
HALO = 8


def _shift_down(cur, prev8, s, row8):
    r = pltpu.roll(cur, s, axis=0)
    top = jnp.where(row8 < s, pltpu.roll(prev8, s, axis=0), r[0:HALO])
    return jnp.concatenate([top, r[HALO:]], axis=0)


def _shift_up(cur, next8, s, row8):
    n = cur.shape[0]
    r = pltpu.roll(cur, n - s, axis=0)
    bot = jnp.where(row8 >= HALO - s, pltpu.roll(next8, HALO - s, axis=0), r[n - HALO:])
    return jnp.concatenate([r[:n - HALO], bot], axis=0)


def _conv_pre(u_ref, prev_ref, w_ref, b_ref, li):
    cur = u_ref[...]
    prev8 = jnp.where(li == 0, 0.0, prev_ref[...])
    row8 = lax.broadcasted_iota(jnp.int32, prev8.shape, 0)
    shifted = [cur] + [_shift_down(cur, prev8, s, row8) for s in range(1, CONV_TAPS)]
    acc = b_ref[...] + shifted[0] * w_ref[CONV_TAPS - 1:CONV_TAPS, :]
    for s in range(1, CONV_TAPS):
        acc = acc + shifted[s] * w_ref[CONV_TAPS - 1 - s:CONV_TAPS - s, :]
    return acc, shifted


def _conv_specs(n_rows, tl, col0=0):
    off = col0 // CONV_COLS
    cur = pl.BlockSpec((tl, CONV_COLS), lambda cj, li: (li, cj + off))
    prev = pl.BlockSpec((HALO, CONV_COLS), lambda cj, li: (jnp.maximum(li * (tl // HALO) - 1, 0), cj + off))
    nxt = pl.BlockSpec((HALO, CONV_COLS),
                       lambda cj, li: (jnp.minimum((li + 1) * (tl // HALO), n_rows // HALO - 1), cj + off))
    par = pl.BlockSpec((8, CONV_COLS), lambda cj, li: (0, cj + off))
    return cur, prev, nxt, par


def _conv_fwd(u, w8, b8, name):
    u, u0, c = _window(u)
    n_rows = u.shape[0]
    tl = _rows(n_rows)
    cur, _, _, par = _conv_specs(n_rows, tl)
    ucur, prev, _, _ = _conv_specs(n_rows, tl, u0)

    def body(u_ref, prev_ref, w_ref, b_ref, o_ref):
        acc, _ = _conv_pre(u_ref, prev_ref, w_ref, b_ref[0:1, :], pl.program_id(1))
        o_ref[...] = acc * _sig(acc)

    return pl.pallas_call(
        body, name=name, grid=(c // CONV_COLS, n_rows // tl), in_specs=[ucur, prev, par, par], out_specs=cur,
        out_shape=jax.ShapeDtypeStruct((n_rows, c), F32),
        compiler_params=_cparams("parallel", "parallel"),
    )(u, u, w8, b8)


def _conv_bwd_pre(u, w8, b8, dout, name):
    u, u0, c = _window(u)
    n_rows = u.shape[0]
    tl = _rows(n_rows)
    cur, _, _, par = _conv_specs(n_rows, tl)
    ucur, prev, _, _ = _conv_specs(n_rows, tl, u0)

    def body(u_ref, prev_ref, w_ref, b_ref, do_ref, dc_ref, acc_ref):
        @pl.when(pl.program_id(1) == 0)
        def _():
            acc_ref[...] = jnp.zeros_like(acc_ref)

        acc, shifted = _conv_pre(u_ref, prev_ref, w_ref, b_ref[0:1, :], pl.program_id(1))
        sg = _sig(acc)
        dc = do_ref[...] * (sg * (1.0 + acc * (1.0 - sg)))
        dc_ref[...] = dc
        for k in range(CONV_TAPS):
            acc_ref[k:k + 1, :] += jnp.sum(dc * shifted[CONV_TAPS - 1 - k], axis=0, keepdims=True)
        acc_ref[CONV_TAPS:CONV_TAPS + 1, :] += jnp.sum(dc, axis=0, keepdims=True)

    return pl.pallas_call(
        body, name=name, grid=(c // CONV_COLS, n_rows // tl), in_specs=[ucur, prev, par, par, cur],
        out_specs=(cur, par),
        out_shape=(jax.ShapeDtypeStruct((n_rows, c), F32), jax.ShapeDtypeStruct((8, c), F32)),
        compiler_params=_cparams("parallel", "arbitrary"),
    )(u, u, w8, b8, dout)


def _conv_bwd_in(dc, w8, name, into):
    n_rows, c = dc.shape
    tl = _rows(n_rows)
    cur, _, nxt, par = _conv_specs(n_rows, tl)
    n_l = n_rows // tl
    extra, extra_specs, aliases, col0, width = _into(into, 3, 0)
    out_spec = _conv_specs(n_rows, tl, col0)[0]

    def body(*refs):
        dc_ref, next_ref, w_ref = refs[:3]
        o_ref = refs[-1]
        cur_v = dc_ref[...]
        next8 = jnp.where(pl.program_id(1) == n_l - 1, 0.0, next_ref[...])
        row8 = lax.broadcasted_iota(jnp.int32, next8.shape, 0)
        acc = cur_v * w_ref[CONV_TAPS - 1:CONV_TAPS, :]
        for s in range(1, CONV_TAPS):
            acc = acc + _shift_up(cur_v, next8, s, row8) * w_ref[CONV_TAPS - 1 - s:CONV_TAPS - s, :]
        o_ref[...] = acc.astype(BF16)

    return pl.pallas_call(
        body, name=name, grid=(c // CONV_COLS, n_l), in_specs=[cur, nxt, par] + extra_specs, out_specs=out_spec,
        out_shape=jax.ShapeDtypeStruct((n_rows, width), BF16), input_output_aliases=aliases,
        compiler_params=_cparams("parallel", "parallel"),
    )(dc, dc, w8, *extra)


NORM_GROUP = SSD_D_INNER // SSD_GROUPS


def _gnorm_fwd(y, z, w, name):
    n_rows, c = y.shape
    z, z0, _ = _window(z)
    zoff = z0 // NORM_GROUP

    def body(y_ref, z_ref, w_ref, o_ref):
        zv = z_ref[...]
        yg = y_ref[...] * (zv * _sig(zv))
        r = lax.rsqrt(jnp.mean(yg * yg, axis=-1, keepdims=True) + RMS_EPS)
        o_ref[...] = (yg * r * w_ref[...]).astype(BF16)

    blk = pl.BlockSpec((_rows(n_rows),NORM_GROUP), lambda i, j: (i, j))
    zblk = pl.BlockSpec((_rows(n_rows),NORM_GROUP), lambda i, j: (i, j + zoff))
    wspec = pl.BlockSpec((1, NORM_GROUP), lambda i, j: (0, j))
    return pl.pallas_call(
        body, name=name, grid=(n_rows // _rows(n_rows), c // NORM_GROUP), in_specs=[blk, zblk, wspec], out_specs=blk,
        out_shape=jax.ShapeDtypeStruct((n_rows, c), BF16),
        compiler_params=_cparams("parallel", "parallel"),
    )(y, z, w.reshape(1, c))


def _gnorm_bwd(y, z, w, dyn, name, into):
    n_rows, c = y.shape
    z, z0, _ = _window(z)
    zoff = z0 // NORM_GROUP
    extra, extra_specs, aliases, col0, width = _into(into, 4, 1)
    doff = col0 // NORM_GROUP

    def body(*refs):
        y_ref, z_ref, w_ref, dn_ref = refs[:4]
        dy_ref, dz_ref, acc_ref = refs[-3:]
        @pl.when(pl.program_id(1) == 0)
        def _():
            acc_ref[...] = jnp.zeros_like(acc_ref)

        zv = z_ref[...]
        yv = y_ref[...]
        sz = _sig(zv)
        silu = zv * sz
        yg = yv * silu
        r = lax.rsqrt(jnp.mean(yg * yg, axis=-1, keepdims=True) + RMS_EPS)
        nrm = yg * r
        dn = dn_ref[...]
        acc_ref[0:1, :] += jnp.sum(dn * nrm, axis=0, keepdims=True)
        dnw = dn * w_ref[...]
        dyg = r * (dnw - nrm * jnp.mean(dnw * nrm, axis=-1, keepdims=True))
        dy_ref[...] = dyg * silu
        dz_ref[...] = (dyg * yv * (sz * (1.0 + zv * (1.0 - sz)))).astype(BF16)

    blk = pl.BlockSpec((_rows(n_rows),NORM_GROUP), lambda j, i: (i, j))
    zblk = pl.BlockSpec((_rows(n_rows),NORM_GROUP), lambda j, i: (i, j + zoff))
    wspec = pl.BlockSpec((1, NORM_GROUP), lambda j, i: (0, j))
    aspec = pl.BlockSpec((8, NORM_GROUP), lambda j, i: (0, j))
    return pl.pallas_call(
        body, name=name, grid=(c // NORM_GROUP, n_rows // _rows(n_rows)),
        in_specs=[blk, zblk, wspec, blk] + extra_specs,
        out_specs=(blk, pl.BlockSpec((_rows(n_rows), NORM_GROUP), lambda j, i: (i, j + doff)), aspec),
        out_shape=(jax.ShapeDtypeStruct((n_rows, c), F32), jax.ShapeDtypeStruct((n_rows, width), BF16),
                   jax.ShapeDtypeStruct((8, c), F32)),
        input_output_aliases=aliases,
        compiler_params=_cparams("parallel", "arbitrary"),
    )(y, z, w.reshape(1, c), dyn, *extra)


ATT_SCALE = ATT_HEAD_DIM ** -0.5
ATT_SLOPES = [2.0 ** (-8.0 * (h + 1) / ATT_HEADS) for h in range(ATT_HEADS)]
Q_PER_KV = ATT_HEADS // ATT_KV_HEADS


def _dup_half(t, g, lo):
    tr = pltpu.roll(t, ATT_HEAD_DIM, axis=1)
    return jnp.where(lo, t, tr) if g == 0 else jnp.where(lo, tr, t)


def _att_band(kv_ref, kvp_ref, n):
    cur = kv_ref[...]
    prev = jnp.where(n == 0, 0.0, kvp_ref[...])
    lo = lax.broadcasted_iota(jnp.int32, (ATT_BLOCK, LANE), 1) < ATT_HEAD_DIM
    bands = []
    for g in range(ATT_KV_HEADS):
        kb = jnp.concatenate([_dup_half(prev[:, :LANE], g, lo), _dup_half(cur[:, :LANE], g, lo)], axis=0)
        vb = jnp.concatenate([_dup_half(prev[:, LANE:], g, lo), _dup_half(cur[:, LANE:], g, lo)], axis=0)
        bands.append((kb.astype(BF16), vb.astype(BF16)))
    return bands


def _att_tile(n):
    shape = (2 * ATT_BLOCK, ATT_BLOCK)
    row = lax.broadcasted_iota(jnp.int32, shape, 0)
    i = row & (ATT_BLOCK - 1)
    s = lax.broadcasted_iota(jnp.int32, shape, 1)
    upper = s > i
    dist = ((i - s) & (ATT_BLOCK - 1)).astype(F32)
    dead = upper & (n == 0)
    return upper, dist, dead, row[:, 0:1] < ATT_BLOCK


def _stack_pair(t, lo):
    return jnp.concatenate([jnp.where(lo, t, 0.0), jnp.where(lo, 0.0, t)], axis=0).astype(BF16)


def _att_exp(qs, kb, s_ref, j, tile):
    upper, dist, dead, first = tile
    s2 = _dot(qs, kb, _NT)
    slope = jnp.where(first, ATT_SLOPES[2 * j], ATT_SLOPES[2 * j + 1])
    sink = jnp.where(first, s_ref[0:1, 2 * j:2 * j + 1], s_ref[0:1, 2 * j + 1:2 * j + 2])
    s = jnp.where(upper, s2[:, :ATT_BLOCK], s2[:, ATT_BLOCK:]) - slope * dist
    s = jnp.where(dead, NEG, s)
    m = jnp.maximum(jnp.max(s, axis=-1, keepdims=True), sink)
    return jnp.exp(s - m), jnp.exp(sink - m)


def _band_split(t, upper):
    return jnp.concatenate([jnp.where(upper, t, 0.0), jnp.where(upper, 0.0, t)], axis=1)


def _att_fwd(q, kv, sinks8, name):
    q, q0, _ = _window(q)
    kv, kv0, _ = _window(kv)
    qoff, kvoff = q0 // Q_DIM, kv0 // (2 * LANE)
    n_rows = q.shape[0]
    nb = n_rows // ATT_BLOCK

    def body(q_ref, kv_ref, kvp_ref, s_ref, o_ref):
        n = pl.program_id(0)
        bands = _att_band(kv_ref, kvp_ref, n)
        lo = lax.broadcasted_iota(jnp.int32, (ATT_BLOCK, LANE), 1) < ATT_HEAD_DIM
        tile = _att_tile(n)
        ones_b = jnp.ones((2 * ATT_BLOCK, LANE), BF16)
        for j in range(ATT_HEADS // 2):
            kb, vb = bands[2 * j // Q_PER_KV]
            qs = _stack_pair(q_ref[:, j * LANE:(j + 1) * LANE] * ATT_SCALE, lo)
            p, es = _att_exp(qs, kb, s_ref, j, tile)
            pv = _dot(_band_split(p, tile[0]).astype(BF16), jnp.concatenate([vb, ones_b], axis=1))
            out = pv[:, :LANE] / (pv[:, LANE:] + es)
            o_ref[:, j * LANE:(j + 1) * LANE] = jnp.where(lo, out[:ATT_BLOCK], out[ATT_BLOCK:]).astype(BF16)

    return pl.pallas_call(
        body, name=name, grid=(nb,),
        in_specs=[pl.BlockSpec((ATT_BLOCK, Q_DIM), lambda n: (n, qoff)),
                  pl.BlockSpec((ATT_BLOCK, 2 * LANE), lambda n: (n, kvoff)),
                  pl.BlockSpec((ATT_BLOCK, 2 * LANE), lambda n: (jnp.maximum(n - 1, 0), kvoff)),
                  pl.BlockSpec((8, LANE), lambda n: (0, 0))],
        out_specs=pl.BlockSpec((ATT_BLOCK, Q_DIM), lambda n: (n, 0)),
        out_shape=jax.ShapeDtypeStruct((n_rows, Q_DIM), BF16),
        compiler_params=_cparams("parallel"),
    )(q, kv, kv, sinks8)


def _att_bwd(q, kv, sinks8, att, dout, name, into):
    q, q0, _ = _window(q)
    kv, kv0, _ = _window(kv)
    qoff, kvoff = q0 // Q_DIM, kv0 // (2 * LANE)
    n_rows = q.shape[0]
    nb = n_rows // ATT_BLOCK

    extra, extra_specs, aliases, col0, width = _into(into, 6, 0)
    dqoff = col0 // Q_DIM

    def body(*refs):
        q_ref, kv_ref, kvp_ref, s_ref, o_ref, do_ref = refs[:6]
        dq_ref, dkv_ref, acc_ref, carry_ref = refs[-4:]
        n = pl.program_id(0)

        @pl.when(n == 0)
        def _():
            acc_ref[...] = jnp.zeros_like(acc_ref)
            carry_ref[...] = jnp.zeros_like(carry_ref)

        @pl.when(n == nb)
        def _():
            dkv_ref[...] = carry_ref[...].astype(BF16)

        @pl.when(n < nb)
        def _():
            bands = _att_band(kv_ref, kvp_ref, n)
            lo = lax.broadcasted_iota(jnp.int32, (ATT_BLOCK, LANE), 1) < ATT_HEAD_DIM
            lane1 = lax.broadcasted_iota(jnp.int32, (1, LANE), 1)
            tile = _att_tile(n)
            upper, first = tile[0], tile[3]
            ones_b = jnp.ones((ATT_BLOCK, LANE), BF16)
            ones2_b = jnp.ones((2 * LANE, LANE), BF16)
            dk_acc = [jnp.zeros((2 * ATT_BLOCK, LANE), F32) for _ in range(ATT_KV_HEADS)]
            dv_acc = [jnp.zeros((2 * ATT_BLOCK, LANE), F32) for _ in range(ATT_KV_HEADS)]
            dsink = jnp.zeros((1, LANE), F32)
            for j in range(ATT_HEADS // 2):
                g = 2 * j // Q_PER_KV
                kb, vb = bands[g]
                qs = _stack_pair(q_ref[:, j * LANE:(j + 1) * LANE] * ATT_SCALE, lo)
                dop = do_ref[:, j * LANE:(j + 1) * LANE].astype(F32)
                dos = _stack_pair(dop, lo)
                pu, es = _att_exp(qs, kb, s_ref, j, tile)
                inv = 1.0 / (_dot(pu.astype(BF16), ones_b) + es)
                p = pu * inv
                od = dop * o_ref[:, j * LANE:(j + 1) * LANE].astype(F32)
                od = jnp.concatenate([jnp.where(lo, od, 0.0), jnp.where(lo, 0.0, od)], axis=0)
                od_hi = od.astype(BF16)
                delta = _dot(jnp.concatenate([od_hi, (od - od_hi.astype(F32)).astype(BF16)], axis=1), ones2_b)
                dp2 = _dot(dos, vb, _NT)
                dp = jnp.where(upper, dp2[:, :ATT_BLOCK], dp2[:, ATT_BLOCK:])
                ds2 = _band_split(p * (dp - delta), upper)
                psd = jnp.sum(es * inv * delta, axis=0, keepdims=True)
                psd0 = jnp.sum(jnp.where(first, es * inv * delta, 0.0), axis=0, keepdims=True)
                dsink = jnp.where(lane1 == 2 * j, -psd0, jnp.where(lane1 == 2 * j + 1, psd0 - psd, dsink))
                ds2_b = ds2.astype(BF16)
                dq = _dot(ds2_b, kb) * ATT_SCALE
                dq_ref[:, j * LANE:(j + 1) * LANE] = jnp.where(lo, dq[:ATT_BLOCK], dq[ATT_BLOCK:]).astype(BF16)
                dk_acc[g] = dk_acc[g] + _dot(ds2_b, qs, _TN)
                dv_acc[g] = dv_acc[g] + _dot(_band_split(p, upper).astype(BF16), dos, _TN)
            acc_ref[0:1, :] += dsink
            lo2 = lax.broadcasted_iota(jnp.int32, (2 * ATT_BLOCK, LANE), 1) < ATT_HEAD_DIM
            folded = []
            for acc in (dk_acc, dv_acc):
                t0 = acc[0] + pltpu.roll(acc[0], ATT_HEAD_DIM, axis=1)
                t1 = acc[1] + pltpu.roll(acc[1], ATT_HEAD_DIM, axis=1)
                folded.append(jnp.where(lo2, t0, t1))
            band = jnp.concatenate(folded, axis=1)
            dkv_ref[...] = (carry_ref[...] + band[:ATT_BLOCK]).astype(BF16)
            carry_ref[...] = band[ATT_BLOCK:]

    def qmap(n):
        return (jnp.minimum(n, nb - 1), 0)

    return pl.pallas_call(
        body, name=name, grid=(nb + 1,),
        in_specs=[pl.BlockSpec((ATT_BLOCK, Q_DIM), lambda n: (jnp.minimum(n, nb - 1), qoff)),
                  pl.BlockSpec((ATT_BLOCK, 2 * LANE), lambda n: (jnp.minimum(n, nb - 1), kvoff)),
                  pl.BlockSpec((ATT_BLOCK, 2 * LANE),
                               lambda n: (jnp.maximum(jnp.minimum(n, nb - 1) - 1, 0), kvoff)),
                  pl.BlockSpec((8, LANE), lambda n: (0, 0)),
                  pl.BlockSpec((ATT_BLOCK, Q_DIM), qmap),
                  pl.BlockSpec((ATT_BLOCK, Q_DIM), qmap)] + extra_specs,
        out_specs=(pl.BlockSpec((ATT_BLOCK, Q_DIM), lambda n: (jnp.minimum(n, nb - 1), dqoff)),
                   pl.BlockSpec((ATT_BLOCK, 2 * LANE), lambda n: (jnp.maximum(n - 1, 0), 0)),
                   pl.BlockSpec((8, LANE), lambda n: (0, 0))),
        out_shape=(jax.ShapeDtypeStruct((n_rows, width), BF16), jax.ShapeDtypeStruct((n_rows, 2 * LANE), BF16),
                   jax.ShapeDtypeStruct((8, LANE), F32)),
        input_output_aliases=aliases,
        scratch_shapes=[pltpu.VMEM((ATT_BLOCK, 2 * LANE), F32)],
        compiler_params=_cparams("arbitrary"),
    )(q, kv, kv, sinks8, att, dout, *extra)


HEADS_PER_GROUP = SSD_HEADS // SSD_GROUPS
PAIRS_PER_GROUP = HEADS_PER_GROUP // 2
T = SSD_CHUNK


def _cumsum_mm(mat, x):
    hi = x.astype(BF16)
    r = x - hi.astype(F32)
    mid = r.astype(BF16)
    lo = (r - mid.astype(F32)).astype(BF16)
    w = x.shape[1]
    out = _dot(mat, jnp.concatenate([hi, mid, lo], axis=1))
    return out[:, :w] + out[:, w:2 * w] + out[:, 2 * w:]


def _ssd_prep(dtr_ref, par_ref):
    dt = _softplus(dtr_ref[...] + par_ref[0:1, :])
    a = -jnp.exp(par_ref[1:2, :])
    ri = lax.broadcasted_iota(jnp.int32, (T, T), 0)
    ci = lax.broadcasted_iota(jnp.int32, (T, T), 1)
    cs = _cumsum_mm((ri >= ci).astype(BF16), dt * a)
    lo = lax.broadcasted_iota(jnp.int32, (T, LANE), 1) < SSD_CHUNK // 2

    def expand(arr):
        rows = arr.shape[0]
        return jnp.concatenate([jnp.where(lo[:rows], arr[:, 2 * j:2 * j + 1], arr[:, 2 * j + 1:2 * j + 2])
                                for j in range(PAIRS_PER_GROUP)], axis=1)

    tot = cs[T - 1:T, :]
    return {"dt": dt, "a": a, "cs": cs, "cst": cs.T, "lo": lo, "ri": ri, "ci": ci, "expand": expand,
            "dt_x": expand(dt), "ecs_x": expand(jnp.exp(cs)), "dec_x": expand(jnp.exp(tot - cs)),
            "et_x": expand(jnp.exp(tot)), "etot": jnp.exp(tot), "dsk_x": expand(par_ref[2:3, :])}


def _wide_masks():
    r = lax.broadcasted_iota(jnp.int32, (T, 2 * T), 0)
    l = lax.broadcasted_iota(jnp.int32, (T, 2 * T), 1)
    s = l & (T - 1)
    return r >= s, s >= r, l < T


def _wide_cs(q, k0, even):
    cs, cst = q["cs"], q["cst"]
    col = jnp.where(even, cs[:, k0:k0 + 1], cs[:, k0 + 1:k0 + 2])
    row = jnp.concatenate([cst[k0:k0 + 1, :], cst[k0 + 1:k0 + 2, :]], axis=1)
    return col, row


def _ssd_fwd(xs, bm, cm, dtr, par, name):
    dtr, dt0, _ = _window(dtr)
    dtoff = dt0 // LANE
    n_rows = xs.shape[0]
    nc = n_rows // T
    gw = PAIRS_PER_GROUP * LANE

    def body(x_ref, b_ref, c_ref, dtr_ref, par_ref, y_ref, hs_ref, h_ref):
        @pl.when(pl.program_id(1) == 0)
        def _():
            h_ref[...] = jnp.zeros_like(h_ref)

        q = _ssd_prep(dtr_ref, par_ref)
        lo = q["lo"]
        tri_w, _, even = _wide_masks()
        bg_b = b_ref[...].astype(BF16)
        cg_b = c_ref[...].astype(BF16)
        xv = x_ref[...]
        xdt = xv * q["dt_x"]
        h = h_ref[...]
        hs_ref[0, 0] = h
        yo = q["ecs_x"] * _dot(cg_b, h.astype(BF16))
        h_ref[...] = h * q["et_x"] + _dot(bg_b, (xdt * q["dec_x"]).astype(BF16), _TN)
        cb = _dot(cg_b, bg_b, _NT)
        cb_w = jnp.concatenate([cb, cb], axis=1)
        for j in range(PAIRS_PER_GROUP):
            col, row = _wide_cs(q, 2 * j, even)
            m_w = (jnp.exp(jnp.where(tri_w, col - row, NEG)) * cb_w).astype(BF16)
            sl = slice(j * LANE, (j + 1) * LANE)
            y_ref[:, sl] = (_dot(m_w, _stack_pair(xdt[:, sl], lo)) + yo[:, sl] + q["dsk_x"][:, sl] * xv[:, sl])

    return pl.pallas_call(
        body, name=name, grid=(SSD_GROUPS, nc),
        in_specs=[pl.BlockSpec((T, gw), lambda g, c: (c, g)),
                  pl.BlockSpec((T, SSD_STATE), lambda g, c: (c, g)),
                  pl.BlockSpec((T, SSD_STATE), lambda g, c: (c, g)),
                  pl.BlockSpec((T, LANE), lambda g, c: (c, g + dtoff)),
                  pl.BlockSpec((8, LANE), lambda g, c: (0, g))],
        out_specs=(pl.BlockSpec((T, gw), lambda g, c: (c, g)),
                   pl.BlockSpec((1, 1, SSD_STATE, gw), lambda g, c: (g, c, 0, 0))),
        out_shape=(jax.ShapeDtypeStruct((n_rows, SSD_D_INNER), F32),
                   jax.ShapeDtypeStruct((SSD_GROUPS, nc, SSD_STATE, gw), F32)),
        scratch_shapes=[pltpu.VMEM((SSD_STATE, gw), F32)],
        compiler_params=_cparams("parallel", "arbitrary"),
    )(xs, bm, cm, dtr, par)


def _ssd_bwd(xs, bm, cm, dtr, par, hs, dy, name, into):
    dtr, dt0, _ = _window(dtr)
    dtoff = dt0 // LANE
    n_rows = xs.shape[0]
    nc = n_rows // T
    gw = PAIRS_PER_GROUP * LANE
    extra, extra_specs, aliases, col0, width = _into(into, 7, 3)
    ddoff = col0 // LANE

    def body(*refs):
        x_ref, b_ref, c_ref, dtr_ref, par_ref, hs_ref, dy_ref = refs[:7]
        dx_ref, db_ref, dc_ref, ddtr_ref, acc_ref, dh_ref = refs[-6:]

        @pl.when(pl.program_id(1) == 0)
        def _():
            dh_ref[...] = jnp.zeros_like(dh_ref)
            acc_ref[...] = jnp.zeros_like(acc_ref)

        q = _ssd_prep(dtr_ref, par_ref)
        lo, dt, a = q["lo"], q["dt"], q["a"]
        tri_w, trit_w, even = _wide_masks()
        lane = lax.broadcasted_iota(jnp.int32, (T, LANE), 1)
        lane1 = lane[0:1, :]
        last_row = lax.broadcasted_iota(jnp.int32, (T, 1), 0) == T - 1
        bg_b = b_ref[...].astype(BF16)
        cg_b = c_ref[...].astype(BF16)
        xv = x_ref[...]
        dyv = dy_ref[...]
        xdt = xv * q["dt_x"]
        h = hs_ref[0, 0]
        dhn = dh_ref[...]
        h_b, dhn_b = h.astype(BF16), dhn.astype(BF16)
        yo = q["ecs_x"] * _dot(cg_b, h_b)
        bdh = q["dec_x"] * _dot(bg_b, dhn_b)
        dye = (dyv * q["ecs_x"]).astype(BF16)
        xd = (xdt * q["dec_x"]).astype(BF16)
        dcg = _dot(dye, h_b, _NT)
        dbg = _dot(xd, dhn_b, _NT)
        dh_ref[...] = dhn * q["et_x"] + _dot(cg_b, dye, _TN)
        e4_all = xdt * bdh
        f_all = dyv * yo - e4_all
        tot_row = jnp.sum(e4_all, axis=0, keepdims=True) + q["et_x"] * jnp.sum(h * dhn, axis=0, keepdims=True)
        dsk_row = jnp.sum(dyv * xv, axis=0, keepdims=True)
        cb = _dot(cg_b, bg_b, _NT)
        cbt = _dot(bg_b, cg_b, _NT)
        cb_w = jnp.concatenate([cb, cb], axis=1)
        cbt_w = jnp.concatenate([cbt, cbt], axis=1)
        dcb = jnp.zeros((T, T), F32)
        dcbt = jnp.zeros((T, T), F32)
        dcs_acc = jnp.zeros((T, LANE), F32)
        ddt_acc = jnp.zeros((T, LANE), F32)
        dsk_acc = jnp.zeros((1, LANE), F32)
        tot_acc = jnp.zeros((1, LANE), F32)
        ind_r = lax.broadcasted_iota(jnp.int32, (2 * T, LANE), 0)
        ind_l = lax.broadcasted_iota(jnp.int32, (2 * T, LANE), 1)

        def halves(t):
            return (jnp.sum(jnp.where(lo[0:1], t, 0.0), axis=-1, keepdims=True),
                    jnp.sum(jnp.where(lo[0:1], 0.0, t), axis=-1, keepdims=True))

        def split2(t):
            hi = t.astype(BF16)
            return jnp.concatenate([hi, (t - hi.astype(F32)).astype(BF16)], axis=1)

        for j in range(PAIRS_PER_GROUP):
            k0, k1 = 2 * j, 2 * j + 1
            sl = slice(j * LANE, (j + 1) * LANE)
            col, row = _wide_cs(q, k0, even)
            lm_w = jnp.exp(jnp.where(tri_w, col - row, NEG))
            lmt_w = jnp.exp(jnp.where(trit_w, row - col, NEG))
            dyp, xp = dyv[:, sl], xdt[:, sl]
            dym, xm = _stack_pair(dyp, lo), _stack_pair(xp, lo)
            dm_w = _dot(dyp.astype(BF16), xm, _NT)
            dmt_w = _dot(xp.astype(BF16), dym, _NT)
            mm_w = lm_w * cb_w
            mmt_w = lmt_w * cbt_w
            dxdt = _dot(mmt_w.astype(BF16), dym) + bdh[:, sl]
            g1 = dm_w * lm_w
            g2 = dmt_w * lmt_w
            dcb = dcb + g1[:, :T] + g1[:, T:]
            dcbt = dcbt + g2[:, :T] + g2[:, T:]
            ind_w = jnp.where(ind_l == jnp.where(ind_r < T, k0, k1), 1.0, 0.0).astype(BF16)
            ind_p = jnp.where(ind_l[:T] == jnp.where(ind_r[:T] < SSD_CHUNK // 2, k0, k1), 1.0, 0.0).astype(BF16)
            dcs_acc = dcs_acc + _dot(
                jnp.concatenate([split2(dm_w * mm_w - dmt_w * mmt_w), split2(f_all[:, sl])], axis=1),
                jnp.concatenate([ind_w, ind_w, ind_p, ind_p], axis=0))
            ddt_acc = ddt_acc + _dot(split2(dxdt * xv[:, sl]), jnp.concatenate([ind_p, ind_p], axis=0))
            tot2 = halves(tot_row[:, sl])
            tot_acc = jnp.where(lane1 == k0, tot2[0], jnp.where(lane1 == k1, tot2[1], tot_acc))
            dsk2 = halves(dsk_row[:, sl])
            dsk_acc = jnp.where(lane1 == k0, dsk2[0], jnp.where(lane1 == k1, dsk2[1], dsk_acc))
            dx_ref[:, sl] = dxdt * q["dt_x"][:, sl] + q["dsk_x"][:, sl] * dyp
        dcs_acc = dcs_acc + jnp.where(last_row, tot_acc, 0.0)
        dc_ref[...] = dcg + _dot(dcb.astype(BF16), bg_b)
        db_ref[...] = dbg + _dot(dcbt.astype(BF16), cg_b)
        dda = _cumsum_mm((q["ci"] >= q["ri"]).astype(BF16), dcs_acc)
        ddt = ddt_acc + dda * a
        ddtr = ddt * _sig(dtr_ref[...] + par_ref[0:1, :])
        ddtr_ref[...] = ddtr.astype(BF16)
        acc_ref[0:1, :] += jnp.sum(ddtr, axis=0, keepdims=True)
        acc_ref[1:2, :] += jnp.sum(dda * dt, axis=0, keepdims=True) * a
        acc_ref[2:3, :] += dsk_acc

    def rev(g, c):
        return (nc - 1 - c, g)

    return pl.pallas_call(
        body, name=name, grid=(SSD_GROUPS, nc),
        in_specs=[pl.BlockSpec((T, gw), rev),
                  pl.BlockSpec((T, SSD_STATE), rev),
                  pl.BlockSpec((T, SSD_STATE), rev),
                  pl.BlockSpec((T, LANE), lambda g, c: (nc - 1 - c, g + dtoff)),
                  pl.BlockSpec((8, LANE), lambda g, c: (0, g)),
                  pl.BlockSpec((1, 1, SSD_STATE, gw), lambda g, c: (g, nc - 1 - c, 0, 0)),
                  pl.BlockSpec((T, gw), rev)] + extra_specs,
        out_specs=(pl.BlockSpec((T, gw), rev),
                   pl.BlockSpec((T, SSD_STATE), rev),
                   pl.BlockSpec((T, SSD_STATE), rev),
                   pl.BlockSpec((T, LANE), lambda g, c: (nc - 1 - c, g + ddoff)),
                   pl.BlockSpec((8, LANE), lambda g, c: (0, g))),
        out_shape=(jax.ShapeDtypeStruct((n_rows, SSD_D_INNER), F32),
                   jax.ShapeDtypeStruct((n_rows, BC_DIM), F32),
                   jax.ShapeDtypeStruct((n_rows, BC_DIM), F32),
                   jax.ShapeDtypeStruct((n_rows, width), BF16),
                   jax.ShapeDtypeStruct((8, DT_PAD), F32)),
        input_output_aliases=aliases,
        scratch_shapes=[pltpu.VMEM((SSD_STATE, gw), F32)],
        compiler_params=_cparams("parallel", "arbitrary"),
    )(xs, bm, cm, dtr, par, hs, dy, *extra)


ADAM_ROWS = 256


def _adamw(lands, w, m, v, name):
    na = len(lands)
    n_slots, r, wd = lands[0].shape
    tr = r if r <= 2 * ADAM_ROWS else ADAM_ROWS
    nj = r // tr
    bc1 = 1.0 - ADAM_B1 ** ADAM_STEP
    bc2 = 1.0 - ADAM_B2 ** ADAM_STEP

    def body(*refs):
        l_refs = refs[:na]
        w_ref, m_ref, v_ref, g_ref, d_ref, nm_ref, nv_ref = refs[na:]
        for a in range(na):
            @pl.when(pl.program_id(0) == a)
            def _(l_ref=l_refs[a]):
                g = l_ref[0].astype(F32)
                for s in range(1, n_slots):
                    g = g + l_ref[s].astype(F32)
                mn = ADAM_B1 * m_ref[0] + (1.0 - ADAM_B1) * g
                vn = ADAM_B2 * v_ref[0] + (1.0 - ADAM_B2) * (g * g)
                mh = mn / bc1
                vh = vn / bc2
                g_ref[0] = g
                nm_ref[0] = mn
                nv_ref[0] = vn
                d_ref[0] = -ADAM_LR * (mh / (jnp.sqrt(vh) + ADAM_EPS) + ADAM_WD * w_ref[0])

    def land_spec(a):
        return pl.BlockSpec((n_slots, tr, wd),
                            lambda i, j: (0, jnp.where(i == a, j, jnp.where(i < a, 0, nj - 1)), 0))

    blk = pl.BlockSpec((1, tr, wd), lambda i, j: (i, j, 0))
    shp = jax.ShapeDtypeStruct((na, r, wd), F32)
    return pl.pallas_call(
        body, name=name, grid=(na, nj), in_specs=[land_spec(a) for a in range(na)] + [blk, blk, blk],
        out_specs=(blk, blk, blk, blk), out_shape=(shp, shp, shp, shp),
        compiler_params=_cparams("arbitrary", "arbitrary"),
    )(*lands, w, m, v)


def _mesh_pos():
    return lax.axis_index("x"), lax.axis_index("y"), lax.axis_index("c")


def _peer(pos, k):
    x, y, c = pos
    px = 1 - x if (k >> 2) & 1 else x
    py = 1 - y if (k >> 1) & 1 else y
    pc = 1 - c if k & 1 else c
    return px, py, pc


def _flat(pos):
    return 4 * pos[0] + 2 * pos[1] + pos[2]


HBM_SPEC = pl.BlockSpec(memory_space=pl.ANY)


ROW_SHARDED = ("w_ssd_out", "w_att_out", "w_mix_out", "w_ffn_down")
COL_SHARDED = ("w_in", "w_ffn_gate", "w_ffn_up")
GATHERED = ROW_SHARDED + COL_SHARDED + ("conv_w",)


SEM_SPEC = pl.BlockSpec(memory_space=pltpu.SEMAPHORE)
TOKEN = jax.ShapeDtypeStruct((8, LANE), F32)
SPLIT_EFFECT = pltpu.SideEffectType.DATAFLOW_SIDE_EFFECTING
GATHER_ROWS = "gather_rows"
GATHER_SLOT = "gather_slot"
SCATTER_ROWS = "scatter_rows"
SCATTER_SLOT = "scatter_slot"


def _land_shape(kind, src):
    if kind == GATHER_ROWS:
        return (N_DEV * src.shape[0],) + src.shape[1:]
    if kind == GATHER_SLOT:
        return (N_DEV,) + src.shape
    if kind == SCATTER_ROWS:
        return (N_DEV, src.shape[0] // N_DEV) + src.shape[1:]
    return src.shape


def _views(kind, src_ref, land_ref, pos, k):
    me = _flat(pos)
    if kind == GATHER_ROWS:
        r = src_ref.shape[0]
        return src_ref, land_ref.at[pl.ds(pl.multiple_of(me * r, 16), r), :]
    if kind == GATHER_SLOT:
        return src_ref, land_ref.at[me]
    dev = _flat(_peer(pos, k))
    if kind == SCATTER_ROWS:
        r = land_ref.shape[1]
        return src_ref.at[pl.ds(pl.multiple_of(dev * r, 16), r), :], land_ref.at[k]
    return src_ref.at[dev], land_ref.at[k]


def _hbm(x):
    return pltpu.with_memory_space_constraint(x, pltpu.HBM)


def _exchange_start(items, after, name):
    kinds = [k for k, _ in items]
    srcs = [_hbm(s) for _, s in items]
    lands = [_hbm(lax.empty(_land_shape(k, s), s.dtype)) for k, s in items]
    n = len(items)
    n_copy = n * (N_DEV - 1)

    def body(*refs):
        src_refs, land_refs = refs[:n], refs[n:2 * n]
        send_sems, recv_sems = refs[2 * n + 1], refs[2 * n + 2]
        token_ref = refs[4 * n + 3]
        pos = _mesh_pos()
        for i, kind in enumerate(kinds):
            for k in range(1, N_DEV):
                s, d = _views(kind, src_refs[i], land_refs[i], pos, k)
                j = i * (N_DEV - 1) + k - 1
                pltpu.make_async_remote_copy(src_ref=s, dst_ref=d, send_sem=send_sems.at[j], recv_sem=recv_sems.at[j],
                                             device_id=_peer(pos, k), device_id_type=MESH_ID).start()
        token_ref[...] = jnp.zeros_like(token_ref)

    arrs = srcs + lands
    outs = pl.pallas_call(
        body, name=name,
        in_specs=[HBM_SPEC] * (2 * n + 1),
        out_specs=[SEM_SPEC, SEM_SPEC] + [HBM_SPEC] * (2 * n) + [pl.BlockSpec(memory_space=pltpu.VMEM)],
        out_shape=[pltpu.SemaphoreType.DMA((n_copy,)), pltpu.SemaphoreType.DMA((n_copy,))]
        + [pltpu.HBM(a.shape, a.dtype) for a in arrs] + [TOKEN],
        input_output_aliases={i: 2 + i for i in range(2 * n)},
        compiler_params=pltpu.CompilerParams(has_side_effects=SPLIT_EFFECT),
    )(*arrs, after)
    return {"kinds": kinds, "send": outs[0], "recv": outs[1], "arrs": outs[2:2 + 2 * n], "token": outs[-1]}


def _exchange_wait(ex, after, name):
    kinds = ex["kinds"]
    n = len(kinds)

    def body(*refs):
        src_refs, land_refs = refs[:n], refs[n:2 * n]
        send_sems, recv_sems = refs[2 * n], refs[2 * n + 1]
        token_ref = refs[-1]
        pos = _mesh_pos()
        for i, kind in enumerate(kinds):
            for k in range(1, N_DEV):
                s, d = _views(kind, src_refs[i], land_refs[i], pos, k)
                j = i * (N_DEV - 1) + k - 1
                cp = pltpu.make_async_remote_copy(src_ref=s, dst_ref=d, send_sem=send_sems.at[j],
                                                  recv_sem=recv_sems.at[j], device_id=_peer(pos, k),
                                                  device_id_type=MESH_ID)
                cp.wait_send()
                cp.wait_recv()
        token_ref[...] = jnp.zeros_like(token_ref)

    outs = pl.pallas_call(
        body, name=name,
        in_specs=[HBM_SPEC] * (2 * n) + [SEM_SPEC, SEM_SPEC, HBM_SPEC],
        out_specs=[HBM_SPEC] * (2 * n) + [pl.BlockSpec(memory_space=pltpu.VMEM)],
        out_shape=[pltpu.HBM(a.shape, a.dtype) for a in ex["arrs"]] + [TOKEN],
        input_output_aliases={i: i for i in range(2 * n)},
        compiler_params=pltpu.CompilerParams(has_side_effects=SPLIT_EFFECT),
    )(*ex["arrs"], ex["send"], ex["recv"], after)
    lands = [_place_own(k, s, d) for k, s, d in zip(kinds, outs[:n], outs[n:2 * n])]
    return lands, outs[-1]


def _place_own(kind, src, land):
    me = _flat(_mesh_pos())
    zeros = (0,) * (src.ndim - 1)
    if kind == GATHER_ROWS:
        return lax.dynamic_update_slice(land, src, (me * src.shape[0],) + zeros)
    if kind == GATHER_SLOT:
        return lax.dynamic_update_slice(land, src[None], (me,) + (0,) * src.ndim)
    if kind == SCATTER_ROWS:
        r = land.shape[1]
        own = lax.dynamic_slice(src, (me * r,) + zeros, (r,) + src.shape[1:])
    else:
        own = lax.dynamic_index_in_dim(src, me, 0, keepdims=False)
    return lax.dynamic_update_slice(land, own[None], (0,) * land.ndim)


def _all_gather_small(x, name):
    r, w = x.shape

    def body(x_ref, out_ref, send_sems, recv_sems):
        pos = _mesh_pos()
        me = _flat(pos)
        copies = []
        for k in range(1, N_DEV):
            cp = pltpu.make_async_remote_copy(
                src_ref=x_ref, dst_ref=out_ref.at[me], send_sem=send_sems.at[k - 1], recv_sem=recv_sems.at[k - 1],
                device_id=_peer(pos, k), device_id_type=MESH_ID)
            cp.start()
            copies.append(cp)
        out_ref[me] = x_ref[...]
        for cp in copies:
            cp.wait()

    vmem = pl.BlockSpec(memory_space=pltpu.VMEM)
    return pl.pallas_call(
        body, name=name, in_specs=[vmem], out_specs=vmem,
        out_shape=jax.ShapeDtypeStruct((N_DEV, r, w), x.dtype),
        scratch_shapes=[pltpu.SemaphoreType.DMA((N_DEV - 1,)), pltpu.SemaphoreType.DMA((N_DEV - 1,))],
        compiler_params=pltpu.CompilerParams(has_side_effects=True),
    )(x)


def _cols(g, lo, hi):
    c = g.shape[-1]
    parts = []
    for d in range(N_DEV):
        a, b = max(lo, d * c), min(hi, (d + 1) * c)
        if a < b:
            parts.append(g[d, :, a - d * c:b - d * c])
    return parts[0] if len(parts) == 1 else jnp.concatenate(parts, axis=1)


def _col_chunks(g):
    c = g.shape[-1] // N_DEV
    return jnp.stack([g[:, d * c:(d + 1) * c] for d in range(N_DEV)])


IN_PART = ("w_in", "conv_w")
OUT_PART = ROW_SHARDED + ("w_ffn_gate", "w_ffn_up")


def _gather_items(w, names, l):
    items = []
    for n in names:
        blk = w[n][l] if n == "conv_w" else w[n][l].astype(BF16)
        items.append((GATHER_ROWS if n in ROW_SHARDED else GATHER_SLOT, blk))
    return items


def _scatter_items(grads, names):
    def chunked(g):
        return g if g.ndim == 3 else _col_chunks(g)

    return [(SCATTER_ROWS, grads[n]) if n in ROW_SHARDED else (SCATTER_SLOT, chunked(grads[n])) for n in names]


SMALL = ("ln_in_g", "ln_in_b", "conv_b", "dt_bias", "a_log", "d_skip", "ssd_norm_w", "att_sinks",
         "ln_mix_g", "ln_mix_b", "ln_ffn_g", "ln_ffn_b")


def _pack_small(vals):
    flat = jnp.concatenate([vals[n].reshape(-1) for n in SMALL])
    n = flat.shape[0]
    rows = -(-n // LANE)
    rows = -(-rows // 8) * 8
    return jnp.pad(flat, (0, rows * LANE - n)).reshape(rows, LANE)


def _unpack_small(buf, shapes):
    flat = buf.reshape(-1)
    off = 0
    out = {}
    for n in SMALL:
        cnt = math.prod(shapes[n])
        out[n] = flat[off:off + cnt].reshape(shapes[n])
        off += cnt
    return out


def _to_group_major(v):
    lead = v.shape[:-1]
    t = v.reshape(lead + (SSD_GROUPS, HEADS_PER_GROUP))
    t = jnp.pad(t, [(0, 0)] * len(lead) + [(0, 0), (0, LANE - HEADS_PER_GROUP)])
    return t.reshape(lead + (DT_PAD,))


def _from_group_major(v):
    lead = v.shape[:-1]
    return v.reshape(lead + (SSD_GROUPS, LANE))[..., :HEADS_PER_GROUP].reshape(lead + (SSD_HEADS,))


def _rows8(v):
    return jnp.pad(v, ((0, 8 - v.shape[0]), (0, 0)))


IN_OFFS = {"q": (0, 1024), "kv": (1024, 1280), "z": (1280, 3328), "xs": (3328, 5376), "b": (5376, 5888),
           "c": (5888, 6400), "dt": (6400, 6432), "gl": (6432, 8480)}
PIECES = ("q", "kv", "z", "xs", "b", "c", "dt", "gl")


CAT = ("z", "xs", "gl", "q", "b", "c", "dt", "kv")
CAT_WIDTH = {"q": 1024, "z": 2048, "xs": 2048, "gl": 2048, "b": 512, "c": 512, "kv": 256, "dt": DT_PAD}
CAT_OFF = {p: sum(CAT_WIDTH[q] for q in CAT[:i]) for i, p in enumerate(CAT)}
CAT_DIM = sum(CAT_WIDTH.values())
MAIN_DIM = CAT_OFF["kv"]


def _cat_w_in(g):
    pieces = {p: _cols(g, lo, hi) for p, (lo, hi) in IN_OFFS.items()}
    pieces["dt"] = _to_group_major(pieces["dt"])
    return jnp.concatenate([pieces[p] for p in CAT], axis=1)


def _dw_in_chunks(dw_main, dw_kv):
    dt = _from_group_major(dw_main[:, CAT_OFF["dt"]:CAT_OFF["dt"] + DT_PAD])
    shard = IN_OFFS[PIECES[-1]][1] // N_DEV

    def piece(pc, a, b):
        if pc == "dt":
            return dt[:, a:b]
        if pc == "kv":
            return dw_kv[:, a:b]
        return dw_main[:, CAT_OFF[pc] + a:CAT_OFF[pc] + b]

    chunks = []
    for d in range(N_DEV):
        parts = []
        for pc in PIECES:
            lo, hi = IN_OFFS[pc]
            a, b = max(lo, d * shard), min(hi, (d + 1) * shard)
            if a < b:
                parts.append(piece(pc, a - lo, b - lo))
        chunks.append(parts[0] if len(parts) == 1 else jnp.concatenate(parts, axis=1))
    return jnp.stack(chunks)


def _params_out(W):
    p = {n: W[n] for n in ROW_SHARDED}
    for n in ("w_ffn_gate", "w_ffn_up"):
        p[n] = _cols(W[n], 0, FFN_HIDDEN)
    return p


def _params_in(l, W, sm):
    p = {"w_cat": _cat_w_in(W["w_in"])}
    cw = _cols(W["conv_w"], 0, SSD_D_INNER + 2 * BC_DIM)
    cb = sm["conv_b"][l]
    segs = {"xs": (0, 2048), "b": (2048, 2560), "c": (2560, 3072)}
    p["conv_w8"] = {s: _rows8(cw[:, lo:hi]) for s, (lo, hi) in segs.items()}
    p["conv_b8"] = {s: _rows8(cb[None, lo:hi]) for s, (lo, hi) in segs.items()}
    p["ssd_par"] = _rows8(jnp.stack([_to_group_major(sm["dt_bias"][l]), _to_group_major(sm["a_log"][l]),
                                     _to_group_major(sm["d_skip"][l])]))
    p["norm_w"] = sm["ssd_norm_w"][l]
    p["sinks8"] = _rows8(jnp.pad(sm["att_sinks"][l], (0, LANE - ATT_HEADS))[None])
    for n in ("ln_mix_g", "ln_mix_b", "ln_ffn_g", "ln_ffn_b"):
        p[n] = sm[n][l]
    return p


def _fwd_mixers(h0, p, l, dep=None):
    tag = f"l{l}_"
    a = {"h0": h0}
    proj = _mm(h0, p["w_cat"], "nn", tag + "proj", dep=dep)
    for pc in CAT:
        a[pc] = (proj, CAT_OFF[pc], CAT_WIDTH[pc])
    for s in ("xs", "b", "c"):
        a[s + "c"] = _conv_fwd(a[s], p["conv_w8"][s], p["conv_b8"][s], tag + "conv_" + s)
    a["y"], a["hs"] = _ssd_fwd(a["xsc"], a["bc"], a["cc"], a["dt"], p["ssd_par"], tag + "ssd_fwd")
    a["yn"] = _gnorm_fwd(a["y"], a["z"], p["norm_w"], tag + "gnorm")
    a["att"] = _att_fwd(a["q"], a["kv"], p["sinks8"], tag + "att_fwd")
    return a


def _fwd_out(a, p, l, dep=None):
    tag = f"l{l}_"
    h0 = a["h0"]
    a["ya"] = _mm(a["yn"], p["w_ssd_out"], "nn", tag + "ssd_out", dep=dep)
    a["yb"] = _mm(a["att"], p["w_att_out"], "nn", tag + "att_out", dep=dep)
    a["merged"] = _merge_fwd(a["gl"], a["ya"], a["yb"], tag + "merge")
    a["mix"] = _mm(a["merged"], p["w_mix_out"], "nn", tag + "mix_out")
    a["h1"] = _ln_fwd(h0, a["mix"], p["ln_mix_g"], p["ln_mix_b"], ALPHA, tag + "ln_mix")
    a["fg"], a["fu"], a["act"] = _ffn_in(a["h1"], p["w_ffn_gate"], p["w_ffn_up"], tag + "ffn_in")
    a["ffn"] = _mm(a["act"], p["w_ffn_down"], "nn", tag + "ffn_down")
    a["h2"] = _ln_fwd(a["h1"], a["ffn"], p["ln_ffn_g"], p["ln_ffn_b"], ALPHA, tag + "ln_ffn")
    return a


def _dw(x, dy, name, dep=None):
    return _mm(x, dy, "tn", name, out_dtype=BF16, dep=dep)


def _bwd_out(a, p, dh2, l, dep=None):
    tag = f"l{l}_b_"
    gw, gs = {}, {}
    du2, acc = _ln_bwd(a["h1"], a["ffn"], p["ln_ffn_g"], dh2, ALPHA, tag + "ln_ffn")
    gs["ln_ffn_g"], gs["ln_ffn_b"] = acc[0], acc[1]
    gw["w_ffn_down"] = _dw(a["act"], du2, tag + "dw_down", dep=dep)
    dfg, dfu = _ffn_dact(du2, p["w_ffn_down"], a["fg"], a["fu"], tag + "ffn_dact", dep=dep)
    gw["w_ffn_gate"] = _dw(a["h1"], dfg, tag + "dw_gate")
    gw["w_ffn_up"] = _dw(a["h1"], dfu, tag + "dw_up")
    dh1 = _mm(dfg, p["w_ffn_gate"], "nt", tag + "dh1_gate", add=du2, add_scale=ALPHA)
    dh1 = _mm(dfu, p["w_ffn_up"], "nt", tag + "dh1_up", add=dh1)
    du1, acc = _ln_bwd(a["h0"], a["mix"], p["ln_mix_g"], dh1, ALPHA, tag + "ln_mix")
    gs["ln_mix_g"], gs["ln_mix_b"] = acc[0], acc[1]
    gw["w_mix_out"] = _dw(a["merged"], du1, tag + "dw_mix")
    dmerged = _mm(du1, p["w_mix_out"], "nt", tag + "dmerged")
    dya, dyb, dproj = _merge_bwd(a["gl"], a["ya"], a["yb"], dmerged, tag + "merge",
                                 (None, CAT_OFF["gl"], MAIN_DIM))
    gw["w_ssd_out"] = _dw(a["yn"], dya, tag + "dw_ssd")
    gw["w_att_out"] = _dw(a["att"], dyb, tag + "dw_att")
    return {"du1": du1, "dya": dya, "dyb": dyb, "dproj": dproj}, gw, gs


def _bwd_mixers(a, p, carry, l, dep=None):
    tag = f"l{l}_b_"
    gs = {}
    du1, dproj = carry["du1"], carry["dproj"]

    def win(pc):
        return (dproj, CAT_OFF[pc], MAIN_DIM)

    dyn = _mm(carry["dya"], p["w_ssd_out"], "nt", tag + "dyn", dep=dep)
    datt = _mm(carry["dyb"], p["w_att_out"], "nt", tag + "datt", out_dtype=BF16, dep=dep)
    dproj, dkv, acc = _att_bwd(a["q"], a["kv"], p["sinks8"], a["att"], datt, tag + "att", win("q"))
    gs["att_sinks"] = acc[0, :ATT_HEADS]
    dy, dproj, acc = _gnorm_bwd(a["y"], a["z"], p["norm_w"], dyn, tag + "gnorm", win("z"))
    gs["ssd_norm_w"] = acc[0]
    dxs, dbm, dcm, dproj, acc = _ssd_bwd(a["xsc"], a["bc"], a["cc"], a["dt"], p["ssd_par"], a["hs"], dy,
                                         tag + "ssd", win("dt"))
    gs["dt_bias"], gs["a_log"], gs["d_skip"] = (_from_group_major(acc[i]) for i in range(3))
    dconv_w, dconv_b = [], []
    for s, dout in (("xs", dxs), ("b", dbm), ("c", dcm)):
        dc, acc = _conv_bwd_pre(a[s], p["conv_w8"][s], p["conv_b8"][s], dout, tag + "conv_pre_" + s)
        dconv_w.append(acc[:CONV_TAPS])
        dconv_b.append(acc[CONV_TAPS])
        dproj = _conv_bwd_in(dc, p["conv_w8"][s], tag + "conv_in_" + s, win(s))
    gconv = jnp.concatenate(dconv_w, axis=1)
    gs["conv_b"] = jnp.concatenate(dconv_b)
    w_main, w_kv = p["w_cat"][:, :MAIN_DIM], p["w_cat"][:, MAIN_DIM:]
    dw_main, dw_kv = _dw(a["h0"], dproj, tag + "dw_in"), _dw(a["h0"], dkv, tag + "dw_in_kv")

    def grad_h0(dep=None):
        dh0 = _mm(dproj, w_main, "nt", tag + "dh0", add=du1, add_scale=ALPHA, dep=dep)
        return _mm(dkv, w_kv, "nt", tag + "dh0_kv", add=dh0)

    return grad_h0, _dw_in_chunks(dw_main, dw_kv), gconv, gs


def _step(x, target, w, m, v):
    x2 = x[0]
    t2 = target[0]
    tok = jnp.zeros(TOKEN.shape, TOKEN.dtype)

    ex = _exchange_start(_gather_items(w, IN_PART, 0), tok, "gather_l0_in_start")
    h = _ln_fwd(x2, None, w["ln_in_g"], w["ln_in_b"], 1.0, "ln_in")
    lands, tok = _exchange_wait(ex, h, "gather_l0_in_wait")
    p0 = _params_in(0, dict(zip(IN_PART, lands)), w)
    ex = _exchange_start(_gather_items(w, OUT_PART, 0) + _gather_items(w, IN_PART, 1), tok,
                         "gather_l0_out_l1_in_start")
    a0 = _fwd_mixers(h, p0, 0, dep=ex["token"])
    lands, tok = _exchange_wait(ex, a0["att"], "gather_l0_out_l1_in_wait")
    p0.update(_params_out(dict(zip(OUT_PART, lands))))
    p1 = _params_in(1, dict(zip(IN_PART, lands[len(OUT_PART):])), w)
    ex = _exchange_start(_gather_items(w, OUT_PART, 1), tok, "gather_l1_out_start")
    a0 = _fwd_out(a0, p0, 0, dep=ex["token"])
    lands, tok = _exchange_wait(ex, a0["h2"], "gather_l1_out_wait")
    p1.update(_params_out(dict(zip(OUT_PART, lands))))
    a1 = _fwd_out(_fwd_mixers(a0["h2"], p1, 1), p1, 1)

    sse, dh = _loss_fwd_bwd(a1["h2"], t2, "loss")
    loss = lax.psum(0.5 / D_MODEL * sse[0, 0], ("x", "y", "c"))

    carry, gw1, gs1 = _bwd_out(a1, p1, dh, 1)
    grad_h0, gw1["w_in"], gw1["conv_w"], gs = _bwd_mixers(a1, p1, carry, 1)
    dh = grad_h0()
    gs1.update(gs)
    ex1 = _exchange_start(_scatter_items(gw1, GATHERED), tok, "scatter_l1_start")
    carry, gw0, gs0 = _bwd_out(a0, p0, dh, 0, dep=ex1["token"])
    lands, tok = _exchange_wait(ex1, carry["dyb"], "scatter_l1_wait")
    land1 = dict(zip(GATHERED, lands))
    ex0 = _exchange_start(_scatter_items(gw0, OUT_PART), tok, "scatter_l0_out_start")
    grad_h0, gw0["w_in"], gw0["conv_w"], gs = _bwd_mixers(a0, p0, carry, 0, dep=ex0["token"])
    gs0.update(gs)
    lands, tok = _exchange_wait(ex0, gw0["w_in"], "scatter_l0_out_wait")
    land0 = dict(zip(OUT_PART, lands))
    ex0 = _exchange_start(_scatter_items(gw0, IN_PART), tok, "scatter_l0_in_start")
    dh = grad_h0(dep=ex0["token"])
    grad_x2, acc = _ln_bwd(x2, None, w["ln_in_g"], dh, 1.0, "ln_in_b")

    outs = [{} for _ in range(4)]

    def update(names):
        res = None
        for n in names:
            res = _adamw([land0[n], land1[n]], w[n], m[n], v[n], "adamw_" + n)
            for o, t in zip(outs, res):
                o[n] = t
        return res[1]

    update(OUT_PART)
    gsm = {"ln_in_g": acc[0], "ln_in_b": acc[1]}
    for n in SMALL[2:]:
        gsm[n] = jnp.stack([gs0[n], gs1[n]])
    small_shapes = {n: w[n].shape for n in SMALL}
    land_s = _all_gather_small(_pack_small(gsm), "small_grads_all_gather")
    res = _adamw([land_s], _pack_small(w)[None], _pack_small(m)[None], _pack_small(v)[None], "adamw_small")
    for o, t in zip(outs, res):
        o.update(_unpack_small(t[0], small_shapes))
    lands, _ = _exchange_wait(ex0, res[1], "scatter_l0_in_wait")
    land0.update(zip(IN_PART, lands))
    update(IN_PART)
    return loss, grad_x2[None], outs


WEIGHT_NAMES = ("ln_in_g", "ln_in_b", "w_in", "conv_w", "conv_b", "dt_bias", "a_log", "d_skip", "ssd_norm_w",
                "att_sinks", "w_ssd_out", "w_att_out", "w_mix_out", "ln_mix_g", "ln_mix_b", "w_ffn_gate",
                "w_ffn_up", "w_ffn_down", "ln_ffn_g", "ln_ffn_b")


def kernel(x, ln_in_g, ln_in_b, w_in, conv_w, conv_b, dt_bias, a_log, d_skip, ssd_norm_w, att_sinks, w_ssd_out, w_att_out, w_mix_out, ln_mix_g, ln_mix_b, w_ffn_gate, w_ffn_up, w_ffn_down, ln_ffn_g, ln_ffn_b, loss_target, m_ln_in_g, m_ln_in_b, m_w_in, m_conv_w, m_conv_b, m_dt_bias, m_a_log, m_d_skip, m_ssd_norm_w, m_att_sinks, m_w_ssd_out, m_w_att_out, m_w_mix_out, m_ln_mix_g, m_ln_mix_b, m_w_ffn_gate, m_w_ffn_up, m_w_ffn_down, m_ln_ffn_g, m_ln_ffn_b, v_ln_in_g, v_ln_in_b, v_w_in, v_conv_w, v_conv_b, v_dt_bias, v_a_log, v_d_skip, v_ssd_norm_w, v_att_sinks, v_w_ssd_out, v_w_att_out, v_w_mix_out, v_ln_mix_g, v_ln_mix_b, v_w_ffn_gate, v_w_ffn_up, v_w_ffn_down, v_ln_ffn_g, v_ln_ffn_b):
    w = dict(zip(WEIGHT_NAMES, (ln_in_g, ln_in_b, w_in, conv_w, conv_b, dt_bias, a_log, d_skip, ssd_norm_w,
                                att_sinks, w_ssd_out, w_att_out, w_mix_out, ln_mix_g, ln_mix_b, w_ffn_gate,
                                w_ffn_up, w_ffn_down, ln_ffn_g, ln_ffn_b)))
    m = dict(zip(WEIGHT_NAMES, (m_ln_in_g, m_ln_in_b, m_w_in, m_conv_w, m_conv_b, m_dt_bias, m_a_log, m_d_skip,
                                m_ssd_norm_w, m_att_sinks, m_w_ssd_out, m_w_att_out, m_w_mix_out, m_ln_mix_g,
                                m_ln_mix_b, m_w_ffn_gate, m_w_ffn_up, m_w_ffn_down, m_ln_ffn_g, m_ln_ffn_b)))
    v = dict(zip(WEIGHT_NAMES, (v_ln_in_g, v_ln_in_b, v_w_in, v_conv_w, v_conv_b, v_dt_bias, v_a_log, v_d_skip,
                                v_ssd_norm_w, v_att_sinks, v_w_ssd_out, v_w_att_out, v_w_mix_out, v_ln_mix_g,
                                v_ln_mix_b, v_w_ffn_gate, v_w_ffn_up, v_w_ffn_down, v_ln_ffn_g, v_ln_ffn_b)))
    loss, grad_x, outs = _step(x, loss_target, w, m, v)
    result = [loss, grad_x]
    for o in outs:
        result.extend(o[n] for n in WEIGHT_NAMES)
    return tuple(result)
```

```python
import math

import jax
import jax.numpy as jnp
from jax import lax
from jax.experimental import pallas as pl
from jax.experimental.pallas import tpu as pltpu

F32 = jnp.float32
BF16 = jnp.bfloat16

D_MODEL = 1024
DEPTH = 2
N_DEV = 8
ATT_HEADS = 16
ATT_KV_HEADS = 2
ATT_HEAD_DIM = 64
ATT_BLOCK = 128
SSD_D_INNER = 2048
SSD_HEADS = 32
SSD_GROUPS = 4
SSD_STATE = 128
SSD_CHUNK = 128
FFN_HIDDEN = 2816
LN_EPS = 1e-5
RMS_EPS = 1e-5
ALPHA = (2 * DEPTH) ** 0.25
Q_DIM = 1024
BC_DIM = 512
DT_PAD = 512

ADAM_LR = 0.001
ADAM_B1 = 0.9
ADAM_B2 = 0.999
ADAM_EPS = 1e-08
ADAM_WD = 0.01
ADAM_STEP = 10

LANE = 128
VMEM_LIMIT = 48 * 1024 * 1024
NEG = -1e30

_NN = (((1,), (0,)), ((), ()))
_NT = (((1,), (1,)), ((), ()))
_TN = (((0,), (0,)), ((), ()))
MESH_ID = pl.DeviceIdType.MESH


def _dot(a, b, dims=_NN):
    return lax.dot_general(a, b, dims, preferred_element_type=F32)


def _sig(x):
    return 1.0 / (1.0 + jnp.exp(-x))


def _softplus(x):
    return jnp.maximum(x, 0.0) + jnp.log(1.0 + jnp.exp(-jnp.abs(x)))


def _cparams(*sem):
    return pltpu.CompilerParams(dimension_semantics=sem, vmem_limit_bytes=VMEM_LIMIT)


def _pick(n, cap):
    if n <= cap:
        return n
    best = None
    for t in range(LANE, cap + 1, LANE):
        if n % t == 0:
            best = t
    assert best is not None, (n, cap)
    return best


def _tile(n):
    if n <= 1024 or n % 1024 == 0:
        return min(n, 1024)
    return _pick(n, 1408)


def _rows(n):
    return min(512, n)


def _window(x):
    return x if isinstance(x, tuple) else (x, 0, x.shape[1])


def _into(into, n_in, out_idx):
    buf, col0, width = into
    if buf is None:
        return [], [], {}, col0, width
    return [buf], [pl.BlockSpec(memory_space=pl.ANY)], {n_in: out_idx}, col0, width


def _mm(a, b, mode, name, add=None, add_scale=1.0, out_dtype=F32, dep=None):
    if mode == "nn":
        m, k = a.shape
        n = b.shape[1]
    elif mode == "nt":
        m, k = a.shape
        n = b.shape[0]
    else:
        k, m = a.shape
        n = b.shape[1]
    tm = _tile(m)
    tn = _pick(n, 2176) if mode == "tn" and n > 1024 else _tile(n)
    tk = _pick(k, 2176) if mode == "nt" and a.dtype == BF16 and k > 2816 else _tile(k)
    nk = k // tk
    has_add = add is not None
    dims = {"nn": _NN, "nt": _NT, "tn": _TN}[mode]

    def body(*refs):
        if dep is not None:
            refs = refs[:-3] + refs[-2:]
        if has_add:
            a_ref, b_ref, add_ref, o_ref, acc_ref = refs
        else:
            a_ref, b_ref, o_ref, acc_ref = refs
        kk = pl.program_id(2)

        @pl.when(kk == 0)
        def _():
            if has_add:
                acc_ref[...] = add_scale * add_ref[...].astype(F32)
            else:
                acc_ref[...] = jnp.zeros_like(acc_ref)

        acc_ref[...] += _dot(a_ref[...].astype(BF16), b_ref[...].astype(BF16), dims)

        @pl.when(kk == nk - 1)
        def _():
            o_ref[...] = acc_ref[...].astype(o_ref.dtype)

    if mode == "nn":
        a_spec = pl.BlockSpec((tm, tk), lambda i, j, kk: (i, kk))
        b_spec = pl.BlockSpec((tk, tn), lambda i, j, kk: (kk, j))
    elif mode == "nt":
        a_spec = pl.BlockSpec((tm, tk), lambda i, j, kk: (i, kk))
        b_spec = pl.BlockSpec((tn, tk), lambda i, j, kk: (j, kk))
    else:
        a_spec = pl.BlockSpec((tk, tm), lambda i, j, kk: (kk, i))
        b_spec = pl.BlockSpec((tk, tn), lambda i, j, kk: (kk, j))
    o_spec = pl.BlockSpec((tm, tn), lambda i, j, kk: (i, j))
    in_specs = [a_spec, b_spec] + ([o_spec] if has_add else [])
    args = (a, b) + ((add,) if has_add else ())
    if dep is not None:
        in_specs.append(pl.BlockSpec((8, LANE), lambda i, j, kk: (0, 0)))
        args += (dep,)
    return pl.pallas_call(
        body, name=name, grid=(m // tm, n // tn, nk),
        in_specs=in_specs, out_specs=o_spec,
        out_shape=jax.ShapeDtypeStruct((m, n), out_dtype),
        scratch_shapes=[pltpu.VMEM((tm, tn), F32)],
        compiler_params=_cparams("parallel", "parallel", "arbitrary"),
    )(*args)


def _vec_spec(width):
    return pl.BlockSpec((1, width), lambda i: (0, 0))


def _ln_fwd(a, b, gamma, beta, alpha, name):
    n_rows, dm = a.shape
    has_b = b is not None

    def body(*refs):
        if has_b:
            a_ref, b_ref, g_ref, be_ref, o_ref = refs
            u = alpha * a_ref[...] + b_ref[...]
        else:
            a_ref, g_ref, be_ref, o_ref = refs
            u = a_ref[...]
        mu = jnp.mean(u, axis=-1, keepdims=True)
        d = u - mu
        var = jnp.mean(d * d, axis=-1, keepdims=True)
        o_ref[...] = d * lax.rsqrt(var + LN_EPS) * g_ref[...] + be_ref[...]

    row = pl.BlockSpec((_rows(n_rows),dm), lambda i: (i, 0))
    in_specs = [row] + ([row] if has_b else []) + [_vec_spec(dm), _vec_spec(dm)]
    args = (a,) + ((b,) if has_b else ()) + (gamma.reshape(1, dm), beta.reshape(1, dm))
    return pl.pallas_call(
        body, name=name, grid=(n_rows // _rows(n_rows),), in_specs=in_specs, out_specs=row,
        out_shape=jax.ShapeDtypeStruct((n_rows, dm), F32),
        compiler_params=_cparams("parallel"),
    )(*args)


def _ln_bwd(a, b, gamma, dy, alpha, name):
    n_rows, dm = a.shape
    has_b = b is not None

    def body(*refs):
        if has_b:
            a_ref, b_ref, g_ref, dy_ref, du_ref, acc_ref = refs
            u = alpha * a_ref[...] + b_ref[...]
        else:
            a_ref, g_ref, dy_ref, du_ref, acc_ref = refs
            u = a_ref[...]

        @pl.when(pl.program_id(0) == 0)
        def _():
            acc_ref[...] = jnp.zeros_like(acc_ref)

        mu = jnp.mean(u, axis=-1, keepdims=True)
        d = u - mu
        var = jnp.mean(d * d, axis=-1, keepdims=True)
        rstd = lax.rsqrt(var + LN_EPS)
        xhat = d * rstd
        dyv = dy_ref[...]
        acc_ref[0:1, :] += jnp.sum(dyv * xhat, axis=0, keepdims=True)
        acc_ref[1:2, :] += jnp.sum(dyv, axis=0, keepdims=True)
        dxh = dyv * g_ref[...]
        m1 = jnp.mean(dxh, axis=-1, keepdims=True)
        m2 = jnp.mean(dxh * xhat, axis=-1, keepdims=True)
        du_ref[...] = rstd * (dxh - m1 - xhat * m2)

    row = pl.BlockSpec((_rows(n_rows),dm), lambda i: (i, 0))
    in_specs = [row] + ([row] if has_b else []) + [_vec_spec(dm), row]
    args = (a,) + ((b,) if has_b else ()) + (gamma.reshape(1, dm), dy)
    return pl.pallas_call(
        body, name=name, grid=(n_rows // _rows(n_rows),), in_specs=in_specs,
        out_specs=(row, pl.BlockSpec((8, dm), lambda i: (0, 0))),
        out_shape=(jax.ShapeDtypeStruct((n_rows, dm), F32), jax.ShapeDtypeStruct((8, dm), F32)),
        compiler_params=_cparams("arbitrary"),
    )(*args)


def _loss_fwd_bwd(y, target, name):
    n_rows, dm = y.shape

    def body(y_ref, t_ref, acc_ref, dy_ref):
        @pl.when(pl.program_id(0) == 0)
        def _():
            acc_ref[...] = jnp.zeros_like(acc_ref)

        d = y_ref[...] - t_ref[...]
        acc_ref[...] += jnp.sum(d * d)
        dy_ref[...] = d * (1.0 / dm)

    row = pl.BlockSpec((_rows(n_rows),dm), lambda i: (i, 0))
    return pl.pallas_call(
        body, name=name, grid=(n_rows // _rows(n_rows),), in_specs=[row, row],
        out_specs=(pl.BlockSpec((8, LANE), lambda i: (0, 0)), row),
        out_shape=(jax.ShapeDtypeStruct((8, LANE), F32), jax.ShapeDtypeStruct((n_rows, dm), F32)),
        compiler_params=_cparams("arbitrary"),
    )(y, target)


FFN_ROWS = 512


def _ffn_in(h, wg, wu, name, dep=None):
    m, k = h.shape
    n = wg.shape[1]
    tm, tn = min(FFN_ROWS, m), _tile(n)

    def body(*refs):
        h_ref, wg_ref, wu_ref = refs[:3]
        g_ref, u_ref, act_ref = refs[-3:]
        hb = h_ref[...].astype(BF16)
        g = _dot(hb, wg_ref[...])
        u = _dot(hb, wu_ref[...])
        g_ref[...] = g
        u_ref[...] = u
        act_ref[...] = (g * _sig(g) * u).astype(BF16)

    rows = pl.BlockSpec((tm, k), lambda j, i: (i, 0))
    wcol = pl.BlockSpec((k, tn), lambda j, i: (0, j))
    out = pl.BlockSpec((tm, tn), lambda j, i: (i, j))
    in_specs, args = [rows, wcol, wcol], (h, wg, wu)
    if dep is not None:
        in_specs.append(pl.BlockSpec((8, LANE), lambda j, i: (0, 0)))
        args += (dep,)
    return pl.pallas_call(
        body, name=name, grid=(n // tn, m // tm), in_specs=in_specs, out_specs=(out, out, out),
        out_shape=(jax.ShapeDtypeStruct((m, n), F32), jax.ShapeDtypeStruct((m, n), F32),
                   jax.ShapeDtypeStruct((m, n), BF16)),
        compiler_params=_cparams("parallel", "parallel"),
    )(*args)


def _ffn_dact(dy, wd, g, u, name, dep=None):
    m, k = dy.shape
    n = wd.shape[0]
    tm, tn = min(FFN_ROWS, m), _tile(n)

    def body(*refs):
        dy_ref, wd_ref, g_ref, u_ref = refs[:4]
        dg_ref, du_ref = refs[-2:]
        da = _dot(dy_ref[...].astype(BF16), wd_ref[...], _NT)
        gv = g_ref[...]
        s = _sig(gv)
        dg_ref[...] = (da * u_ref[...] * (s * (1.0 + gv * (1.0 - s)))).astype(BF16)
        du_ref[...] = (da * gv * s).astype(BF16)

    rows = pl.BlockSpec((tm, k), lambda j, i: (i, 0))
    wrow = pl.BlockSpec((tn, k), lambda j, i: (j, 0))
    out = pl.BlockSpec((tm, tn), lambda j, i: (i, j))
    in_specs, args = [rows, wrow, out, out], (dy, wd, g, u)
    if dep is not None:
        in_specs.append(pl.BlockSpec((8, LANE), lambda j, i: (0, 0)))
        args += (dep,)
    return pl.pallas_call(
        body, name=name, grid=(n // tn, m // tm), in_specs=in_specs, out_specs=(out, out),
        out_shape=(jax.ShapeDtypeStruct((m, n), BF16), jax.ShapeDtypeStruct((m, n), BF16)),
        compiler_params=_cparams("parallel", "parallel"),
    )(*args)


def _gate_specs(gl, n_rows, dm):
    arr, g0, _ = _window(gl)
    return arr, [pl.BlockSpec((_rows(n_rows), dm), lambda i, k=k: (i, g0 // dm + k)) for k in range(2)]


def _merge_fwd(gl, ya, yb, name):
    n_rows, dm = ya.shape
    gl_arr, gspecs = _gate_specs(gl, n_rows, dm)

    def body(ga_ref, gb_ref, ya_ref, yb_ref, o_ref):
        o_ref[...] = (_sig(ga_ref[...]) * ya_ref[...] + _sig(gb_ref[...]) * yb_ref[...]).astype(BF16)

    row = pl.BlockSpec((_rows(n_rows),dm), lambda i: (i, 0))
    return pl.pallas_call(
        body, name=name, grid=(n_rows // _rows(n_rows),), in_specs=gspecs + [row, row], out_specs=row,
        out_shape=jax.ShapeDtypeStruct((n_rows, dm), BF16),
        compiler_params=_cparams("parallel"),
    )(gl_arr, gl_arr, ya, yb)


def _merge_bwd(gl, ya, yb, dmerged, name, into):
    n_rows, dm = ya.shape
    gl_arr, gspecs = _gate_specs(gl, n_rows, dm)
    extra, extra_specs, aliases, col0, width = _into(into, 5, 2)

    def body(*refs):
        ga_ref, gb_ref, ya_ref, yb_ref, dm_ref = refs[:5]
        dya_ref, dyb_ref, dgl_ref = refs[-3:]
        ga = _sig(ga_ref[...])
        gb = _sig(gb_ref[...])
        dmv = dm_ref[...]
        dya_ref[...] = (dmv * ga).astype(BF16)
        dyb_ref[...] = (dmv * gb).astype(BF16)
        dgl_ref[:, :dm] = (dmv * ya_ref[...] * ga * (1.0 - ga)).astype(BF16)
        dgl_ref[:, dm:] = (dmv * yb_ref[...] * gb * (1.0 - gb)).astype(BF16)

    row = pl.BlockSpec((_rows(n_rows),dm), lambda i: (i, 0))
    row2 = pl.BlockSpec((_rows(n_rows),2 * dm), lambda i: (i, col0 // (2 * dm)))
    return pl.pallas_call(
        body, name=name, grid=(n_rows // _rows(n_rows),), in_specs=gspecs + [row, row, row] + extra_specs,
        out_specs=(row, row, row2),
        out_shape=(jax.ShapeDtypeStruct((n_rows, dm), BF16), jax.ShapeDtypeStruct((n_rows, dm), BF16),
                   jax.ShapeDtypeStruct((n_rows, width), BF16)),
        input_output_aliases=aliases,
        compiler_params=_cparams("parallel"),
    )(gl_arr, gl_arr, ya, yb, dmerged, *extra)


CONV_TAPS = 4
CONV_COLS = 512
HALO = 8


def _shift_down(cur, prev8, s, row8):
    r = pltpu.roll(cur, s, axis=0)
    top = jnp.where(row8 < s, pltpu.roll(prev8, s, axis=0), r[0:HALO])
    return jnp.concatenate([top, r[HALO:]], axis=0)


def _shift_up(cur, next8, s, row8):
    n = cur.shape[0]
    r = pltpu.roll(cur, n - s, axis=0)
    bot = jnp.where(row8 >= HALO - s, pltpu.roll(next8, HALO - s, axis=0), r[n - HALO:])
    return jnp.concatenate([r[:n - HALO], bot], axis=0)


def _conv_pre(u_ref, prev_ref, w_ref, b_ref, li):
    cur = u_ref[...]
    prev8 = jnp.where(li == 0, 0.0, prev_ref[...])
    row8 = lax.broadcasted_iota(jnp.int32, prev8.shape, 0)
    shifted = [cur] + [_shift_down(cur, prev8, s, row8) for s in range(1, CONV_TAPS)]
    acc = b_ref[...] + shifted[0] * w_ref[CONV_TAPS - 1:CONV_TAPS, :]
    for s in range(1, CONV_TAPS):
        acc = acc + shifted[s] * w_ref[CONV_TAPS - 1 - s:CONV_TAPS - s, :]
    return acc, shifted


def _conv_specs(n_rows, tl, col0=0):
    off = col0 // CONV_COLS
    cur = pl.BlockSpec((tl, CONV_COLS), lambda cj, li: (li, cj + off))
    prev = pl.BlockSpec((HALO, CONV_COLS), lambda cj, li: (jnp.maximum(li * (tl // HALO) - 1, 0), cj + off))
    nxt = pl.BlockSpec((HALO, CONV_COLS),
                       lambda cj, li: (jnp.minimum((li + 1) * (tl // HALO), n_rows // HALO - 1), cj + off))
    par = pl.BlockSpec((8, CONV_COLS), lambda cj, li: (0, cj + off))
    return cur, prev, nxt, par


def _conv_fwd(u, w8, b8, name):
    u, u0, c = _window(u)
    n_rows = u.shape[0]
    tl = _rows(n_rows)
    cur, _, _, par = _conv_specs(n_rows, tl)
    ucur, prev, _, _ = _conv_specs(n_rows, tl, u0)

    def body(u_ref, prev_ref, w_ref, b_ref, o_ref):
        acc, _ = _conv_pre(u_ref, prev_ref, w_ref, b_ref[0:1, :], pl.program_id(1))
        o_ref[...] = acc * _sig(acc)

    return pl.pallas_call(
        body, name=name, grid=(c // CONV_COLS, n_rows // tl), in_specs=[ucur, prev, par, par], out_specs=cur,
        out_shape=jax.ShapeDtypeStruct((n_rows, c), F32),
        compiler_params=_cparams("parallel", "parallel"),
    )(u, u, w8, b8)


def _conv_bwd_pre(u, w8, b8, dout, name):
    u, u0, c = _window(u)
    n_rows = u.shape[0]
    tl = _rows(n_rows)
    cur, _, _, par = _conv_specs(n_rows, tl)
    ucur, prev, _, _ = _conv_specs(n_rows, tl, u0)

    def body(u_ref, prev_ref, w_ref, b_ref, do_ref, dc_ref, acc_ref):
        @pl.when(pl.program_id(1) == 0)
        def _():
            acc_ref[...] = jnp.zeros_like(acc_ref)

        acc, shifted = _conv_pre(u_ref, prev_ref, w_ref, b_ref[0:1, :], pl.program_id(1))
        sg = _sig(acc)
        dc = do_ref[...] * (sg * (1.0 + acc * (1.0 - sg)))
        dc_ref[...] = dc
        for k in range(CONV_TAPS):
            acc_ref[k:k + 1, :] += jnp.sum(dc * shifted[CONV_TAPS - 1 - k], axis=0, keepdims=True)
        acc_ref[CONV_TAPS:CONV_TAPS + 1, :] += jnp.sum(dc, axis=0, keepdims=True)

    return pl.pallas_call(
        body, name=name, grid=(c // CONV_COLS, n_rows // tl), in_specs=[ucur, prev, par, par, cur],
        out_specs=(cur, par),
        out_shape=(jax.ShapeDtypeStruct((n_rows, c), F32), jax.ShapeDtypeStruct((8, c), F32)),
        compiler_params=_cparams("parallel", "arbitrary"),
    )(u, u, w8, b8, dout)


def _conv_bwd_in(dc, w8, name, into):
    n_rows, c = dc.shape
    tl = _rows(n_rows)
    cur, _, nxt, par = _conv_specs(n_rows, tl)
    n_l = n_rows // tl
    extra, extra_specs, aliases, col0, width = _into(into, 3, 0)
    out_spec = _conv_specs(n_rows, tl, col0)[0]

    def body(*refs):
        dc_ref, next_ref, w_ref = refs[:3]
        o_ref = refs[-1]
        cur_v = dc_ref[...]
        next8 = jnp.where(pl.program_id(1) == n_l - 1, 0.0, next_ref[...])
        row8 = lax.broadcasted_iota(jnp.int32, next8.shape, 0)
        acc = cur_v * w_ref[CONV_TAPS - 1:CONV_TAPS, :]
        for s in range(1, CONV_TAPS):
            acc = acc + _shift_up(cur_v, next8, s, row8) * w_ref[CONV_TAPS - 1 - s:CONV_TAPS - s, :]
        o_ref[...] = acc.astype(BF16)

    return pl.pallas_call(
        body, name=name, grid=(c // CONV_COLS, n_l), in_specs=[cur, nxt, par] + extra_specs, out_specs=out_spec,
        out_shape=jax.ShapeDtypeStruct((n_rows, width), BF16), input_output_aliases=aliases,
        compiler_params=_cparams("parallel", "parallel"),
    )(dc, dc, w8, *extra)


NORM_GROUP = SSD_D_INNER // SSD_GROUPS


def _gnorm_fwd(y, z, w, name):
    n_rows, c = y.shape
    z, z0, _ = _window(z)
    zoff = z0 // NORM_GROUP

    def body(y_ref, z_ref, w_ref, o_ref):
        zv = z_ref[...]
        yg = y_ref[...] * (zv * _sig(zv))
        r = lax.rsqrt(jnp.mean(yg * yg, axis=-1, keepdims=True) + RMS_EPS)
        o_ref[...] = (yg * r * w_ref[...]).astype(BF16)

    blk = pl.BlockSpec((_rows(n_rows),NORM_GROUP), lambda i, j: (i, j))
    zblk = pl.BlockSpec((_rows(n_rows),NORM_GROUP), lambda i, j: (i, j + zoff))
    wspec = pl.BlockSpec((1, NORM_GROUP), lambda i, j: (0, j))
    return pl.pallas_call(
        body, name=name, grid=(n_rows // _rows(n_rows), c // NORM_GROUP), in_specs=[blk, zblk, wspec], out_specs=blk,
        out_shape=jax.ShapeDtypeStruct((n_rows, c), BF16),
        compiler_params=_cparams("parallel", "parallel"),
    )(y, z, w.reshape(1, c))


def _gnorm_bwd(y, z, w, dyn, name, into):
    n_rows, c = y.shape
    z, z0, _ = _window(z)
    zoff = z0 // NORM_GROUP
    extra, extra_specs, aliases, col0, width = _into(into, 4, 1)
    doff = col0 // NORM_GROUP

    def body(*refs):
        y_ref, z_ref, w_ref, dn_ref = refs[:4]
        dy_ref, dz_ref, acc_ref = refs[-3:]
        @pl.when(pl.program_id(1) == 0)
        def _():
            acc_ref[...] = jnp.zeros_like(acc_ref)

        zv = z_ref[...]
        yv = y_ref[...]
        sz = _sig(zv)
        silu = zv * sz
        yg = yv * silu
        r = lax.rsqrt(jnp.mean(yg * yg, axis=-1, keepdims=True) + RMS_EPS)
        nrm = yg * r
        dn = dn_ref[...]
        acc_ref[0:1, :] += jnp.sum(dn * nrm, axis=0, keepdims=True)
        dnw = dn * w_ref[...]
        dyg = r * (dnw - nrm * jnp.mean(dnw * nrm, axis=-1, keepdims=True))
        dy_ref[...] = dyg * silu
        dz_ref[...] = (dyg * yv * (sz * (1.0 + zv * (1.0 - sz)))).astype(BF16)

    blk = pl.BlockSpec((_rows(n_rows),NORM_GROUP), lambda j, i: (i, j))
    zblk = pl.BlockSpec((_rows(n_rows),NORM_GROUP), lambda j, i: (i, j + zoff))
    wspec = pl.BlockSpec((1, NORM_GROUP), lambda j, i: (0, j))
    aspec = pl.BlockSpec((8, NORM_GROUP), lambda j, i: (0, j))
    return pl.pallas_call(
        body, name=name, grid=(c // NORM_GROUP, n_rows // _rows(n_rows)),
        in_specs=[blk, zblk, wspec, blk] + extra_specs,
        out_specs=(blk, pl.BlockSpec((_rows(n_rows), NORM_GROUP), lambda j, i: (i, j + doff)), aspec),
        out_shape=(jax.ShapeDtypeStruct((n_rows, c), F32), jax.ShapeDtypeStruct((n_rows, width), BF16),
                   jax.ShapeDtypeStruct((8, c), F32)),
        input_output_aliases=aliases,
        compiler_params=_cparams("parallel", "arbitrary"),
    )(y, z, w.reshape(1, c), dyn, *extra)


ATT_SCALE = ATT_HEAD_DIM ** -0.5
ATT_SLOPES = [2.0 ** (-8.0 * (h + 1) / ATT_HEADS) for h in range(ATT_HEADS)]
Q_PER_KV = ATT_HEADS // ATT_KV_HEADS


def _dup_half(t, g, lo):
    tr = pltpu.roll(t, ATT_HEAD_DIM, axis=1)
    return jnp.where(lo, t, tr) if g == 0 else jnp.where(lo, tr, t)


def _att_band(kv_ref, kvp_ref, n):
    cur = kv_ref[...]
    prev = jnp.where(n == 0, 0.0, kvp_ref[...])
    lo = lax.broadcasted_iota(jnp.int32, (ATT_BLOCK, LANE), 1) < ATT_HEAD_DIM
    bands = []
    for g in range(ATT_KV_HEADS):
        kb = jnp.concatenate([_dup_half(prev[:, :LANE], g, lo), _dup_half(cur[:, :LANE], g, lo)], axis=0)
        vb = jnp.concatenate([_dup_half(prev[:, LANE:], g, lo), _dup_half(cur[:, LANE:], g, lo)], axis=0)
        bands.append((kb.astype(BF16), vb.astype(BF16)))
    return bands


def _att_tile(n):
    shape = (2 * ATT_BLOCK, ATT_BLOCK)
    row = lax.broadcasted_iota(jnp.int32, shape, 0)
    i = row & (ATT_BLOCK - 1)
    s = lax.broadcasted_iota(jnp.int32, shape, 1)
    upper = s > i
    dist = ((i - s) & (ATT_BLOCK - 1)).astype(F32)
    dead = upper & (n == 0)
    return upper, dist, dead, row[:, 0:1] < ATT_BLOCK


def _stack_pair(t, lo):
    return jnp.concatenate([jnp.where(lo, t, 0.0), jnp.where(lo, 0.0, t)], axis=0).astype(BF16)


def _att_exp(qs, kb, s_ref, j, tile):
    upper, dist, dead, first = tile
    s2 = _dot(qs, kb, _NT)
    slope = jnp.where(first, ATT_SLOPES[2 * j], ATT_SLOPES[2 * j + 1])
    sink = jnp.where(first, s_ref[0:1, 2 * j:2 * j + 1], s_ref[0:1, 2 * j + 1:2 * j + 2])
    s = jnp.where(upper, s2[:, :ATT_BLOCK], s2[:, ATT_BLOCK:]) - slope * dist
    s = jnp.where(dead, NEG, s)
    m = jnp.maximum(jnp.max(s, axis=-1, keepdims=True), sink)
    return jnp.exp(s - m), jnp.exp(sink - m)


def _band_split(t, upper):
    return jnp.concatenate([jnp.where(upper, t, 0.0), jnp.where(upper, 0.0, t)], axis=1)


def _att_fwd(q, kv, sinks8, name):
    q, q0, _ = _window(q)
    kv, kv0, _ = _window(kv)
    qoff, kvoff = q0 // Q_DIM, kv0 // (2 * LANE)
    n_rows = q.shape[0]
    nb = n_rows // ATT_BLOCK

    def body(q_ref, kv_ref, kvp_ref, s_ref, o_ref):
        n = pl.program_id(0)
        bands = _att_band(kv_ref, kvp_ref, n)
        lo = lax.broadcasted_iota(jnp.int32, (ATT_BLOCK, LANE), 1) < ATT_HEAD_DIM
        tile = _att_tile(n)
        ones_b = jnp.ones((2 * ATT_BLOCK, LANE), BF16)
        for j in range(ATT_HEADS // 2):
            kb, vb = bands[2 * j // Q_PER_KV]
            qs = _stack_pair(q_ref[:, j * LANE:(j + 1) * LANE] * ATT_SCALE, lo)
            p, es = _att_exp(qs, kb, s_ref, j, tile)
            pv = _dot(_band_split(p, tile[0]).astype(BF16), jnp.concatenate([vb, ones_b], axis=1))
            out = pv[:, :LANE] / (pv[:, LANE:] + es)
            o_ref[:, j * LANE:(j + 1) * LANE] = jnp.where(lo, out[:ATT_BLOCK], out[ATT_BLOCK:]).astype(BF16)

    return pl.pallas_call(
        body, name=name, grid=(nb,),
        in_specs=[pl.BlockSpec((ATT_BLOCK, Q_DIM), lambda n: (n, qoff)),
                  pl.BlockSpec((ATT_BLOCK, 2 * LANE), lambda n: (n, kvoff)),
                  pl.BlockSpec((ATT_BLOCK, 2 * LANE), lambda n: (jnp.maximum(n - 1, 0), kvoff)),
                  pl.BlockSpec((8, LANE), lambda n: (0, 0))],
        out_specs=pl.BlockSpec((ATT_BLOCK, Q_DIM), lambda n: (n, 0)),
        out_shape=jax.ShapeDtypeStruct((n_rows, Q_DIM), BF16),
        compiler_params=_cparams("parallel"),
    )(q, kv, kv, sinks8)


def _att_bwd(q, kv, sinks8, dout, name, into):
    q, q0, _ = _window(q)
    kv, kv0, _ = _window(kv)
    qoff, kvoff = q0 // Q_DIM, kv0 // (2 * LANE)
    n_rows = q.shape[0]
    nb = n_rows // ATT_BLOCK

    extra, extra_specs, aliases, col0, width = _into(into, 5, 0)
    dqoff = col0 // Q_DIM

    def body(*refs):
        q_ref, kv_ref, kvp_ref, s_ref, do_ref = refs[:5]
        dq_ref, dkv_ref, acc_ref, carry_ref = refs[-4:]
        n = pl.program_id(0)

        @pl.when(n == 0)
        def _():
            acc_ref[...] = jnp.zeros_like(acc_ref)
            carry_ref[...] = jnp.zeros_like(carry_ref)

        @pl.when(n == nb)
        def _():
            dkv_ref[...] = carry_ref[...].astype(BF16)

        @pl.when(n < nb)
        def _():
            bands = _att_band(kv_ref, kvp_ref, n)
            lo = lax.broadcasted_iota(jnp.int32, (ATT_BLOCK, LANE), 1) < ATT_HEAD_DIM
            lane1 = lax.broadcasted_iota(jnp.int32, (1, LANE), 1)
            tile = _att_tile(n)
            upper, first = tile[0], tile[3]
            ones_b = jnp.ones((ATT_BLOCK, LANE), BF16)
            ones2_b = jnp.ones((2 * LANE, LANE), BF16)
            dk_acc = [jnp.zeros((2 * ATT_BLOCK, LANE), F32) for _ in range(ATT_KV_HEADS)]
            dv_acc = [jnp.zeros((2 * ATT_BLOCK, LANE), F32) for _ in range(ATT_KV_HEADS)]
            dsink = jnp.zeros((1, LANE), F32)
            for j in range(ATT_HEADS // 2):
                g = 2 * j // Q_PER_KV
                kb, vb = bands[g]
                qs = _stack_pair(q_ref[:, j * LANE:(j + 1) * LANE] * ATT_SCALE, lo)
                dop = do_ref[:, j * LANE:(j + 1) * LANE].astype(F32)
                dos = _stack_pair(dop, lo)
                pu, es = _att_exp(qs, kb, s_ref, j, tile)
                inv = 1.0 / (_dot(pu.astype(BF16), ones_b) + es)
                p = pu * inv
                dp2 = _dot(dos, vb, _NT)
                dp = jnp.where(upper, dp2[:, :ATT_BLOCK], dp2[:, ATT_BLOCK:])
                pd = p * dp
                pd_hi = pd.astype(BF16)
                delta = _dot(jnp.concatenate([pd_hi, (pd - pd_hi.astype(F32)).astype(BF16)], axis=1), ones2_b)
                ds2 = _band_split(pd - p * delta, upper)
                psd = jnp.sum(es * inv * delta, axis=0, keepdims=True)
                psd0 = jnp.sum(jnp.where(first, es * inv * delta, 0.0), axis=0, keepdims=True)
                dsink = jnp.where(lane1 == 2 * j, -psd0, jnp.where(lane1 == 2 * j + 1, psd0 - psd, dsink))
                ds2_b = ds2.astype(BF16)
                dq = _dot(ds2_b, kb) * ATT_SCALE
                dq_ref[:, j * LANE:(j + 1) * LANE] = jnp.where(lo, dq[:ATT_BLOCK], dq[ATT_BLOCK:]).astype(BF16)
                dk_acc[g] = dk_acc[g] + _dot(ds2_b, qs, _TN)
                dv_acc[g] = dv_acc[g] + _dot(_band_split(p, upper).astype(BF16), dos, _TN)
            acc_ref[0:1, :] += dsink
            lo2 = lax.broadcasted_iota(jnp.int32, (2 * ATT_BLOCK, LANE), 1) < ATT_HEAD_DIM
            folded = []
            for acc in (dk_acc, dv_acc):
                t0 = acc[0] + pltpu.roll(acc[0], ATT_HEAD_DIM, axis=1)
                t1 = acc[1] + pltpu.roll(acc[1], ATT_HEAD_DIM, axis=1)
                folded.append(jnp.where(lo2, t0, t1))
            band = jnp.concatenate(folded, axis=1)
            dkv_ref[...] = (carry_ref[...] + band[:ATT_BLOCK]).astype(BF16)
            carry_ref[...] = band[ATT_BLOCK:]

    def qmap(n):
        return (jnp.minimum(n, nb - 1), 0)

    return pl.pallas_call(
        body, name=name, grid=(nb + 1,),
        in_specs=[pl.BlockSpec((ATT_BLOCK, Q_DIM), lambda n: (jnp.minimum(n, nb - 1), qoff)),
                  pl.BlockSpec((ATT_BLOCK, 2 * LANE), lambda n: (jnp.minimum(n, nb - 1), kvoff)),
                  pl.BlockSpec((ATT_BLOCK, 2 * LANE),
                               lambda n: (jnp.maximum(jnp.minimum(n, nb - 1) - 1, 0), kvoff)),
                  pl.BlockSpec((8, LANE), lambda n: (0, 0)),
                  pl.BlockSpec((ATT_BLOCK, Q_DIM), qmap)] + extra_specs,
        out_specs=(pl.BlockSpec((ATT_BLOCK, Q_DIM), lambda n: (jnp.minimum(n, nb - 1), dqoff)),
                   pl.BlockSpec((ATT_BLOCK, 2 * LANE), lambda n: (jnp.maximum(n - 1, 0), 0)),
                   pl.BlockSpec((8, LANE), lambda n: (0, 0))),
        out_shape=(jax.ShapeDtypeStruct((n_rows, width), BF16), jax.ShapeDtypeStruct((n_rows, 2 * LANE), BF16),
                   jax.ShapeDtypeStruct((8, LANE), F32)),
        input_output_aliases=aliases,
        scratch_shapes=[pltpu.VMEM((ATT_BLOCK, 2 * LANE), F32)],
        compiler_params=_cparams("arbitrary"),
    )(q, kv, kv, sinks8, dout, *extra)


HEADS_PER_GROUP = SSD_HEADS // SSD_GROUPS
PAIRS_PER_GROUP = HEADS_PER_GROUP // 2
T = SSD_CHUNK


def _cumsum_mm(mat, x):
    hi = x.astype(BF16)
    r = x - hi.astype(F32)
    mid = r.astype(BF16)
    lo = (r - mid.astype(F32)).astype(BF16)
    w = x.shape[1]
    out = _dot(mat, jnp.concatenate([hi, mid, lo], axis=1))
    return out[:, :w] + out[:, w:2 * w] + out[:, 2 * w:]


def _ssd_prep(dtr_ref, par_ref):
    dt = _softplus(dtr_ref[...] + par_ref[0:1, :])
    a = -jnp.exp(par_ref[1:2, :])
    ri = lax.broadcasted_iota(jnp.int32, (T, T), 0)
    ci = lax.broadcasted_iota(jnp.int32, (T, T), 1)
    cs = _cumsum_mm((ri >= ci).astype(BF16), dt * a)
    lo = lax.broadcasted_iota(jnp.int32, (T, LANE), 1) < SSD_CHUNK // 2

    def expand(arr):
        rows = arr.shape[0]
        return jnp.concatenate([jnp.where(lo[:rows], arr[:, 2 * j:2 * j + 1], arr[:, 2 * j + 1:2 * j + 2])
                                for j in range(PAIRS_PER_GROUP)], axis=1)

    tot = cs[T - 1:T, :]
    return {"dt": dt, "a": a, "cs": cs, "cst": cs.T, "lo": lo, "ri": ri, "ci": ci, "expand": expand,
            "dt_x": expand(dt), "ecs_x": expand(jnp.exp(cs)), "dec_x": expand(jnp.exp(tot - cs)),
            "et_x": expand(jnp.exp(tot)), "etot": jnp.exp(tot), "dsk_x": expand(par_ref[2:3, :])}


def _wide_masks():
    r = lax.broadcasted_iota(jnp.int32, (T, 2 * T), 0)
    l = lax.broadcasted_iota(jnp.int32, (T, 2 * T), 1)
    s = l & (T - 1)
    return r >= s, s >= r, l < T


def _wide_cs(q, k0, even):
    cs, cst = q["cs"], q["cst"]
    col = jnp.where(even, cs[:, k0:k0 + 1], cs[:, k0 + 1:k0 + 2])
    row = jnp.concatenate([cst[k0:k0 + 1, :], cst[k0 + 1:k0 + 2, :]], axis=1)
    return col, row


def _ssd_fwd(xs, bm, cm, dtr, par, name):
    dtr, dt0, _ = _window(dtr)
    dtoff = dt0 // LANE
    n_rows = xs.shape[0]
    nc = n_rows // T
    gw = PAIRS_PER_GROUP * LANE

    def body(x_ref, b_ref, c_ref, dtr_ref, par_ref, y_ref, hs_ref, h_ref):
        @pl.when(pl.program_id(1) == 0)
        def _():
            h_ref[...] = jnp.zeros_like(h_ref)

        q = _ssd_prep(dtr_ref, par_ref)
        lo = q["lo"]
        tri_w, _, even = _wide_masks()
        bg_b = b_ref[...].astype(BF16)
        cg_b = c_ref[...].astype(BF16)
        xv = x_ref[...]
        xdt = xv * q["dt_x"]
        h = h_ref[...]
        hs_ref[0, 0] = h
        yo = q["ecs_x"] * _dot(cg_b, h.astype(BF16))
        h_ref[...] = h * q["et_x"] + _dot(bg_b, (xdt * q["dec_x"]).astype(BF16), _TN)
        cb = _dot(cg_b, bg_b, _NT)
        cb_w = jnp.concatenate([cb, cb], axis=1)
        for j in range(PAIRS_PER_GROUP):
            col, row = _wide_cs(q, 2 * j, even)
            m_w = (jnp.exp(jnp.where(tri_w, col - row, NEG)) * cb_w).astype(BF16)
            sl = slice(j * LANE, (j + 1) * LANE)
            y_ref[:, sl] = (_dot(m_w, _stack_pair(xdt[:, sl], lo)) + yo[:, sl] + q["dsk_x"][:, sl] * xv[:, sl])

    return pl.pallas_call(
        body, name=name, grid=(SSD_GROUPS, nc),
        in_specs=[pl.BlockSpec((T, gw), lambda g, c: (c, g)),
                  pl.BlockSpec((T, SSD_STATE), lambda g, c: (c, g)),
                  pl.BlockSpec((T, SSD_STATE), lambda g, c: (c, g)),
                  pl.BlockSpec((T, LANE), lambda g, c: (c, g + dtoff)),
                  pl.BlockSpec((8, LANE), lambda g, c: (0, g))],
        out_specs=(pl.BlockSpec((T, gw), lambda g, c: (c, g)),
                   pl.BlockSpec((1, 1, SSD_STATE, gw), lambda g, c: (g, c, 0, 0))),
        out_shape=(jax.ShapeDtypeStruct((n_rows, SSD_D_INNER), F32),
                   jax.ShapeDtypeStruct((SSD_GROUPS, nc, SSD_STATE, gw), F32)),
        scratch_shapes=[pltpu.VMEM((SSD_STATE, gw), F32)],
        compiler_params=_cparams("parallel", "arbitrary"),
    )(xs, bm, cm, dtr, par)


def _ssd_bwd(xs, bm, cm, dtr, par, hs, dy, name, into):
    dtr, dt0, _ = _window(dtr)
    dtoff = dt0 // LANE
    n_rows = xs.shape[0]
    nc = n_rows // T
    gw = PAIRS_PER_GROUP * LANE
    extra, extra_specs, aliases, col0, width = _into(into, 7, 3)
    ddoff = col0 // LANE

    def body(*refs):
        x_ref, b_ref, c_ref, dtr_ref, par_ref, hs_ref, dy_ref = refs[:7]
        dx_ref, db_ref, dc_ref, ddtr_ref, acc_ref, dh_ref = refs[-6:]

        @pl.when(pl.program_id(1) == 0)
        def _():
            dh_ref[...] = jnp.zeros_like(dh_ref)
            acc_ref[...] = jnp.zeros_like(acc_ref)

        q = _ssd_prep(dtr_ref, par_ref)
        lo, dt, a = q["lo"], q["dt"], q["a"]
        tri_w, trit_w, even = _wide_masks()
        lane = lax.broadcasted_iota(jnp.int32, (T, LANE), 1)
        lane1 = lane[0:1, :]
        last_row = lax.broadcasted_iota(jnp.int32, (T, 1), 0) == T - 1
        bg_b = b_ref[...].astype(BF16)
        cg_b = c_ref[...].astype(BF16)
        xv = x_ref[...]
        dyv = dy_ref[...]
        xdt = xv * q["dt_x"]
        h = hs_ref[0, 0]
        dhn = dh_ref[...]
        h_b, dhn_b = h.astype(BF16), dhn.astype(BF16)
        yo = q["ecs_x"] * _dot(cg_b, h_b)
        bdh = q["dec_x"] * _dot(bg_b, dhn_b)
        dye = (dyv * q["ecs_x"]).astype(BF16)
        xd = (xdt * q["dec_x"]).astype(BF16)
        dcg = _dot(dye, h_b, _NT)
        dbg = _dot(xd, dhn_b, _NT)
        dh_ref[...] = dhn * q["et_x"] + _dot(cg_b, dye, _TN)
        e4_all = xdt * bdh
        f_all = dyv * yo - e4_all
        tot_row = jnp.sum(e4_all, axis=0, keepdims=True) + q["et_x"] * jnp.sum(h * dhn, axis=0, keepdims=True)
        dsk_row = jnp.sum(dyv * xv, axis=0, keepdims=True)
        cb = _dot(cg_b, bg_b, _NT)
        cbt = _dot(bg_b, cg_b, _NT)
        cb_w = jnp.concatenate([cb, cb], axis=1)
        cbt_w = jnp.concatenate([cbt, cbt], axis=1)
        dcb = jnp.zeros((T, T), F32)
        dcbt = jnp.zeros((T, T), F32)
        dcs_acc = jnp.zeros((T, LANE), F32)
        ddt_acc = jnp.zeros((T, LANE), F32)
        dsk_acc = jnp.zeros((1, LANE), F32)
        tot_acc = jnp.zeros((1, LANE), F32)
        ind_r = lax.broadcasted_iota(jnp.int32, (2 * T, LANE), 0)
        ind_l = lax.broadcasted_iota(jnp.int32, (2 * T, LANE), 1)

        def halves(t):
            return (jnp.sum(jnp.where(lo[0:1], t, 0.0), axis=-1, keepdims=True),
                    jnp.sum(jnp.where(lo[0:1], 0.0, t), axis=-1, keepdims=True))

        def split2(t):
            hi = t.astype(BF16)
            return jnp.concatenate([hi, (t - hi.astype(F32)).astype(BF16)], axis=1)

        for j in range(PAIRS_PER_GROUP):
            k0, k1 = 2 * j, 2 * j + 1
            sl = slice(j * LANE, (j + 1) * LANE)
            col, row = _wide_cs(q, k0, even)
            lm_w = jnp.exp(jnp.where(tri_w, col - row, NEG))
            lmt_w = jnp.exp(jnp.where(trit_w, row - col, NEG))
            dyp, xp = dyv[:, sl], xdt[:, sl]
            dym, xm = _stack_pair(dyp, lo), _stack_pair(xp, lo)
            dm_w = _dot(dyp.astype(BF16), xm, _NT)
            dmt_w = _dot(xp.astype(BF16), dym, _NT)
            mm_w = lm_w * cb_w
            mmt_w = lmt_w * cbt_w
            dxdt = _dot(mmt_w.astype(BF16), dym) + bdh[:, sl]
            g1 = dm_w * lm_w
            g2 = dmt_w * lmt_w
            dcb = dcb + g1[:, :T] + g1[:, T:]
            dcbt = dcbt + g2[:, :T] + g2[:, T:]
            ind_w = jnp.where(ind_l == jnp.where(ind_r < T, k0, k1), 1.0, 0.0).astype(BF16)
            ind_p = jnp.where(ind_l[:T] == jnp.where(ind_r[:T] < SSD_CHUNK // 2, k0, k1), 1.0, 0.0).astype(BF16)
            dcs_acc = dcs_acc + _dot(
                jnp.concatenate([split2(dm_w * mm_w - dmt_w * mmt_w), split2(f_all[:, sl])], axis=1),
                jnp.concatenate([ind_w, ind_w, ind_p, ind_p], axis=0))
            ddt_acc = ddt_acc + _dot(split2(dxdt * xv[:, sl]), jnp.concatenate([ind_p, ind_p], axis=0))
            tot2 = halves(tot_row[:, sl])
            tot_acc = jnp.where(lane1 == k0, tot2[0], jnp.where(lane1 == k1, tot2[1], tot_acc))
            dsk2 = halves(dsk_row[:, sl])
            dsk_acc = jnp.where(lane1 == k0, dsk2[0], jnp.where(lane1 == k1, dsk2[1], dsk_acc))
            dx_ref[:, sl] = dxdt * q["dt_x"][:, sl] + q["dsk_x"][:, sl] * dyp
        dcs_acc = dcs_acc + jnp.where(last_row, tot_acc, 0.0)
        dc_ref[...] = dcg + _dot(dcb.astype(BF16), bg_b)
        db_ref[...] = dbg + _dot(dcbt.astype(BF16), cg_b)
        dda = _cumsum_mm((q["ci"] >= q["ri"]).astype(BF16), dcs_acc)
        ddt = ddt_acc + dda * a
        ddtr = ddt * _sig(dtr_ref[...] + par_ref[0:1, :])
        ddtr_ref[...] = ddtr.astype(BF16)
        acc_ref[0:1, :] += jnp.sum(ddtr, axis=0, keepdims=True)
        acc_ref[1:2, :] += jnp.sum(dda * dt, axis=0, keepdims=True) * a
        acc_ref[2:3, :] += dsk_acc

    def rev(g, c):
        return (nc - 1 - c, g)

    return pl.pallas_call(
        body, name=name, grid=(SSD_GROUPS, nc),
        in_specs=[pl.BlockSpec((T, gw), rev),
                  pl.BlockSpec((T, SSD_STATE), rev),
                  pl.BlockSpec((T, SSD_STATE), rev),
                  pl.BlockSpec((T, LANE), lambda g, c: (nc - 1 - c, g + dtoff)),
                  pl.BlockSpec((8, LANE), lambda g, c: (0, g)),
                  pl.BlockSpec((1, 1, SSD_STATE, gw), lambda g, c: (g, nc - 1 - c, 0, 0)),
                  pl.BlockSpec((T, gw), rev)] + extra_specs,
        out_specs=(pl.BlockSpec((T, gw), rev),
                   pl.BlockSpec((T, SSD_STATE), rev),
                   pl.BlockSpec((T, SSD_STATE), rev),
                   pl.BlockSpec((T, LANE), lambda g, c: (nc - 1 - c, g + ddoff)),
                   pl.BlockSpec((8, LANE), lambda g, c: (0, g))),
        out_shape=(jax.ShapeDtypeStruct((n_rows, SSD_D_INNER), F32),
                   jax.ShapeDtypeStruct((n_rows, BC_DIM), F32),
                   jax.ShapeDtypeStruct((n_rows, BC_DIM), F32),
                   jax.ShapeDtypeStruct((n_rows, width), BF16),
                   jax.ShapeDtypeStruct((8, DT_PAD), F32)),
        input_output_aliases=aliases,
        scratch_shapes=[pltpu.VMEM((SSD_STATE, gw), F32)],
        compiler_params=_cparams("parallel", "arbitrary"),
    )(xs, bm, cm, dtr, par, hs, dy, *extra)


ADAM_ROWS = 256


def _adamw(lands, w, m, v, name):
    na = len(lands)
    n_slots, r, wd = lands[0].shape
    tr = r if r <= 2 * ADAM_ROWS else ADAM_ROWS
    nj = r // tr
    bc1 = 1.0 - ADAM_B1 ** ADAM_STEP
    bc2 = 1.0 - ADAM_B2 ** ADAM_STEP

    def body(*refs):
        l_refs = refs[:na]
        w_ref, m_ref, v_ref, g_ref, d_ref, nm_ref, nv_ref = refs[na:]
        for a in range(na):
            @pl.when(pl.program_id(0) == a)
            def _(l_ref=l_refs[a]):
                g = l_ref[0].astype(F32)
                for s in range(1, n_slots):
                    g = g + l_ref[s].astype(F32)
                mn = ADAM_B1 * m_ref[0] + (1.0 - ADAM_B1) * g
                vn = ADAM_B2 * v_ref[0] + (1.0 - ADAM_B2) * (g * g)
                mh = mn / bc1
                vh = vn / bc2
                g_ref[0] = g
                nm_ref[0] = mn
                nv_ref[0] = vn
                d_ref[0] = -ADAM_LR * (mh / (jnp.sqrt(vh) + ADAM_EPS) + ADAM_WD * w_ref[0])

    def land_spec(a):
        return pl.BlockSpec((n_slots, tr, wd),
                            lambda i, j: (0, jnp.where(i == a, j, jnp.where(i < a, 0, nj - 1)), 0))

    blk = pl.BlockSpec((1, tr, wd), lambda i, j: (i, j, 0))
    shp = jax.ShapeDtypeStruct((na, r, wd), F32)
    return pl.pallas_call(
        body, name=name, grid=(na, nj), in_specs=[land_spec(a) for a in range(na)] + [blk, blk, blk],
        out_specs=(blk, blk, blk, blk), out_shape=(shp, shp, shp, shp),
        compiler_params=_cparams("arbitrary", "arbitrary"),
    )(*lands, w, m, v)


def _mesh_pos():
    return lax.axis_index("x"), lax.axis_index("y"), lax.axis_index("c")


def _peer(pos, k):
    x, y, c = pos
    px = 1 - x if (k >> 2) & 1 else x
    py = 1 - y if (k >> 1) & 1 else y
    pc = 1 - c if k & 1 else c
    return px, py, pc


def _flat(pos):
    return 4 * pos[0] + 2 * pos[1] + pos[2]


HBM_SPEC = pl.BlockSpec(memory_space=pl.ANY)


ROW_SHARDED = ("w_ssd_out", "w_att_out", "w_mix_out", "w_ffn_down")
COL_SHARDED = ("w_in", "w_ffn_gate", "w_ffn_up")
GATHERED = ROW_SHARDED + COL_SHARDED + ("conv_w",)


SEM_SPEC = pl.BlockSpec(memory_space=pltpu.SEMAPHORE)
TOKEN = jax.ShapeDtypeStruct((8, LANE), F32)
SPLIT_EFFECT = pltpu.SideEffectType.DATAFLOW_SIDE_EFFECTING
GATHER_ROWS = "gather_rows"
GATHER_SLOT = "gather_slot"
SCATTER_ROWS = "scatter_rows"
SCATTER_SLOT = "scatter_slot"


def _land_shape(kind, src):
    if kind == GATHER_ROWS:
        return (N_DEV * src.shape[0],) + src.shape[1:]
    if kind == GATHER_SLOT:
        return (N_DEV,) + src.shape
    if kind == SCATTER_ROWS:
        return (N_DEV, src.shape[0] // N_DEV) + src.shape[1:]
    return src.shape


def _views(kind, src_ref, land_ref, pos, k):
    me = _flat(pos)
    if kind == GATHER_ROWS:
        r = src_ref.shape[0]
        return src_ref, land_ref.at[pl.ds(pl.multiple_of(me * r, 16), r), :]
    if kind == GATHER_SLOT:
        return src_ref, land_ref.at[me]
    dev = _flat(_peer(pos, k))
    if kind == SCATTER_ROWS:
        r = land_ref.shape[1]
        return src_ref.at[pl.ds(pl.multiple_of(dev * r, 16), r), :], land_ref.at[k]
    return src_ref.at[dev], land_ref.at[k]


def _hbm(x):
    return pltpu.with_memory_space_constraint(x, pltpu.HBM)


def _exchange_start(items, after, name):
    kinds = [k for k, _ in items]
    srcs = [_hbm(s) for _, s in items]
    lands = [_hbm(lax.empty(_land_shape(k, s), s.dtype)) for k, s in items]
    n = len(items)
    n_copy = n * (N_DEV - 1)

    def body(*refs):
        src_refs, land_refs = refs[:n], refs[n:2 * n]
        send_sems, recv_sems = refs[2 * n + 1], refs[2 * n + 2]
        token_ref = refs[4 * n + 3]
        pos = _mesh_pos()
        for i, kind in enumerate(kinds):
            for k in range(1, N_DEV):
                s, d = _views(kind, src_refs[i], land_refs[i], pos, k)
                j = i * (N_DEV - 1) + k - 1
                pltpu.make_async_remote_copy(src_ref=s, dst_ref=d, send_sem=send_sems.at[j], recv_sem=recv_sems.at[j],
                                             device_id=_peer(pos, k), device_id_type=MESH_ID).start()
        token_ref[...] = jnp.zeros_like(token_ref)

    arrs = srcs + lands
    outs = pl.pallas_call(
        body, name=name,
        in_specs=[HBM_SPEC] * (2 * n + 1),
        out_specs=[SEM_SPEC, SEM_SPEC] + [HBM_SPEC] * (2 * n) + [pl.BlockSpec(memory_space=pltpu.VMEM)],
        out_shape=[pltpu.SemaphoreType.DMA((n_copy,)), pltpu.SemaphoreType.DMA((n_copy,))]
        + [pltpu.HBM(a.shape, a.dtype) for a in arrs] + [TOKEN],
        input_output_aliases={i: 2 + i for i in range(2 * n)},
        compiler_params=pltpu.CompilerParams(has_side_effects=SPLIT_EFFECT),
    )(*arrs, after)
    return {"kinds": kinds, "send": outs[0], "recv": outs[1], "arrs": outs[2:2 + 2 * n], "token": outs[-1]}


def _exchange_wait(ex, after, name):
    kinds = ex["kinds"]
    n = len(kinds)

    def body(*refs):
        src_refs, land_refs = refs[:n], refs[n:2 * n]
        send_sems, recv_sems = refs[2 * n], refs[2 * n + 1]
        token_ref = refs[-1]
        pos = _mesh_pos()
        for i, kind in enumerate(kinds):
            for k in range(1, N_DEV):
                s, d = _views(kind, src_refs[i], land_refs[i], pos, k)
                j = i * (N_DEV - 1) + k - 1
                cp = pltpu.make_async_remote_copy(src_ref=s, dst_ref=d, send_sem=send_sems.at[j],
                                                  recv_sem=recv_sems.at[j], device_id=_peer(pos, k),
                                                  device_id_type=MESH_ID)
                cp.wait_send()
                cp.wait_recv()
        token_ref[...] = jnp.zeros_like(token_ref)

    outs = pl.pallas_call(
        body, name=name,
        in_specs=[HBM_SPEC] * (2 * n) + [SEM_SPEC, SEM_SPEC, HBM_SPEC],
        out_specs=[HBM_SPEC] * (2 * n) + [pl.BlockSpec(memory_space=pltpu.VMEM)],
        out_shape=[pltpu.HBM(a.shape, a.dtype) for a in ex["arrs"]] + [TOKEN],
        input_output_aliases={i: i for i in range(2 * n)},
        compiler_params=pltpu.CompilerParams(has_side_effects=SPLIT_EFFECT),
    )(*ex["arrs"], ex["send"], ex["recv"], after)
    lands = [_place_own(k, s, d) for k, s, d in zip(kinds, outs[:n], outs[n:2 * n])]
    return lands, outs[-1]


def _place_own(kind, src, land):
    me = _flat(_mesh_pos())
    zeros = (0,) * (src.ndim - 1)
    if kind == GATHER_ROWS:
        return lax.dynamic_update_slice(land, src, (me * src.shape[0],) + zeros)
    if kind == GATHER_SLOT:
        return lax.dynamic_update_slice(land, src[None], (me,) + (0,) * src.ndim)
    if kind == SCATTER_ROWS:
        r = land.shape[1]
        own = lax.dynamic_slice(src, (me * r,) + zeros, (r,) + src.shape[1:])
    else:
        own = lax.dynamic_index_in_dim(src, me, 0, keepdims=False)
    return lax.dynamic_update_slice(land, own[None], (0,) * land.ndim)


def _all_gather_small(x, name):
    r, w = x.shape

    def body(x_ref, out_ref, send_sems, recv_sems):
        pos = _mesh_pos()
        me = _flat(pos)
        copies = []
        for k in range(1, N_DEV):
            cp = pltpu.make_async_remote_copy(
                src_ref=x_ref, dst_ref=out_ref.at[me], send_sem=send_sems.at[k - 1], recv_sem=recv_sems.at[k - 1],
                device_id=_peer(pos, k), device_id_type=MESH_ID)
            cp.start()
            copies.append(cp)
        out_ref[me] = x_ref[...]
        for cp in copies:
            cp.wait()

    vmem = pl.BlockSpec(memory_space=pltpu.VMEM)
    return pl.pallas_call(
        body, name=name, in_specs=[vmem], out_specs=vmem,
        out_shape=jax.ShapeDtypeStruct((N_DEV, r, w), x.dtype),
        scratch_shapes=[pltpu.SemaphoreType.DMA((N_DEV - 1,)), pltpu.SemaphoreType.DMA((N_DEV - 1,))],
        compiler_params=pltpu.CompilerParams(has_side_effects=True),
    )(x)


def _cols(g, lo, hi):
    c = g.shape[-1]
    parts = []
    for d in range(N_DEV):
        a, b = max(lo, d * c), min(hi, (d + 1) * c)
        if a < b:
            parts.append(g[d, :, a - d * c:b - d * c])
    return parts[0] if len(parts) == 1 else jnp.concatenate(parts, axis=1)


def _col_chunks(g):
    c = g.shape[-1] // N_DEV
    return jnp.stack([g[:, d * c:(d + 1) * c] for d in range(N_DEV)])


IN_PART = ("w_in", "conv_w")
OUT_PART = ROW_SHARDED + ("w_ffn_gate", "w_ffn_up")


def _gather_items(w, names, l):
    items = []
    for n in names:
        blk = w[n][l] if n == "conv_w" else w[n][l].astype(BF16)
        items.append((GATHER_ROWS if n in ROW_SHARDED else GATHER_SLOT, blk))
    return items


def _scatter_items(grads, names):
    def chunked(g):
        return g if g.ndim == 3 else _col_chunks(g)

    return [(SCATTER_ROWS, grads[n]) if n in ROW_SHARDED else (SCATTER_SLOT, chunked(grads[n])) for n in names]


SMALL = ("ln_in_g", "ln_in_b", "conv_b", "dt_bias", "a_log", "d_skip", "ssd_norm_w", "att_sinks",
         "ln_mix_g", "ln_mix_b", "ln_ffn_g", "ln_ffn_b")


def _pack_small(vals):
    flat = jnp.concatenate([vals[n].reshape(-1) for n in SMALL])
    n = flat.shape[0]
    rows = -(-n // LANE)
    rows = -(-rows // 8) * 8
    return jnp.pad(flat, (0, rows * LANE - n)).reshape(rows, LANE)


def _unpack_small(buf, shapes):
    flat = buf.reshape(-1)
    off = 0
    out = {}
    for n in SMALL:
        cnt = math.prod(shapes[n])
        out[n] = flat[off:off + cnt].reshape(shapes[n])
        off += cnt
    return out


def _to_group_major(v):
    lead = v.shape[:-1]
    t = v.reshape(lead + (SSD_GROUPS, HEADS_PER_GROUP))
    t = jnp.pad(t, [(0, 0)] * len(lead) + [(0, 0), (0, LANE - HEADS_PER_GROUP)])
    return t.reshape(lead + (DT_PAD,))


def _from_group_major(v):
    lead = v.shape[:-1]
    return v.reshape(lead + (SSD_GROUPS, LANE))[..., :HEADS_PER_GROUP].reshape(lead + (SSD_HEADS,))


def _rows8(v):
    return jnp.pad(v, ((0, 8 - v.shape[0]), (0, 0)))


IN_OFFS = {"q": (0, 1024), "kv": (1024, 1280), "z": (1280, 3328), "xs": (3328, 5376), "b": (5376, 5888),
           "c": (5888, 6400), "dt": (6400, 6432), "gl": (6432, 8480)}
PIECES = ("q", "kv", "z", "xs", "b", "c", "dt", "gl")


CAT = ("z", "xs", "gl", "q", "b", "c", "dt", "kv")
CAT_WIDTH = {"q": 1024, "z": 2048, "xs": 2048, "gl": 2048, "b": 512, "c": 512, "kv": 256, "dt": DT_PAD}
CAT_OFF = {p: sum(CAT_WIDTH[q] for q in CAT[:i]) for i, p in enumerate(CAT)}
CAT_DIM = sum(CAT_WIDTH.values())
MAIN_DIM = CAT_OFF["kv"]


def _cat_w_in(g):
    pieces = {p: _cols(g, lo, hi) for p, (lo, hi) in IN_OFFS.items()}
    pieces["dt"] = _to_group_major(pieces["dt"])
    return jnp.concatenate([pieces[p] for p in CAT], axis=1)


def _dw_in_chunks(dw_main, dw_kv):
    dt = _from_group_major(dw_main[:, CAT_OFF["dt"]:CAT_OFF["dt"] + DT_PAD])
    shard = IN_OFFS[PIECES[-1]][1] // N_DEV

    def piece(pc, a, b):
        if pc == "dt":
            return dt[:, a:b]
        if pc == "kv":
            return dw_kv[:, a:b]
        return dw_main[:, CAT_OFF[pc] + a:CAT_OFF[pc] + b]

    chunks = []
    for d in range(N_DEV):
        parts = []
        for pc in PIECES:
            lo, hi = IN_OFFS[pc]
            a, b = max(lo, d * shard), min(hi, (d + 1) * shard)
            if a < b:
                parts.append(piece(pc, a - lo, b - lo))
        chunks.append(parts[0] if len(parts) == 1 else jnp.concatenate(parts, axis=1))
    return jnp.stack(chunks)


def _params_out(W):
    p = {n: W[n] for n in ROW_SHARDED}
    for n in ("w_ffn_gate", "w_ffn_up"):
        p[n] = _cols(W[n], 0, FFN_HIDDEN)
    return p


def _params_in(l, W, sm):
    p = {"w_cat": _cat_w_in(W["w_in"])}
    cw = _cols(W["conv_w"], 0, SSD_D_INNER + 2 * BC_DIM)
    cb = sm["conv_b"][l]
    segs = {"xs": (0, 2048), "b": (2048, 2560), "c": (2560, 3072)}
    p["conv_w8"] = {s: _rows8(cw[:, lo:hi]) for s, (lo, hi) in segs.items()}
    p["conv_b8"] = {s: _rows8(cb[None, lo:hi]) for s, (lo, hi) in segs.items()}
    p["ssd_par"] = _rows8(jnp.stack([_to_group_major(sm["dt_bias"][l]), _to_group_major(sm["a_log"][l]),
                                     _to_group_major(sm["d_skip"][l])]))
    p["norm_w"] = sm["ssd_norm_w"][l]
    p["sinks8"] = _rows8(jnp.pad(sm["att_sinks"][l], (0, LANE - ATT_HEADS))[None])
    for n in ("ln_mix_g", "ln_mix_b", "ln_ffn_g", "ln_ffn_b"):
        p[n] = sm[n][l]
    return p


def _fwd_mixers(h0, p, l, dep=None):
    tag = f"l{l}_"
    a = {"h0": h0}
    proj = _mm(h0, p["w_cat"], "nn", tag + "proj", dep=dep)
    for pc in CAT:
        a[pc] = (proj, CAT_OFF[pc], CAT_WIDTH[pc])
    for s in ("xs", "b", "c"):
        a[s + "c"] = _conv_fwd(a[s], p["conv_w8"][s], p["conv_b8"][s], tag + "conv_" + s)
    a["y"], a["hs"] = _ssd_fwd(a["xsc"], a["bc"], a["cc"], a["dt"], p["ssd_par"], tag + "ssd_fwd")
    a["yn"] = _gnorm_fwd(a["y"], a["z"], p["norm_w"], tag + "gnorm")
    a["att"] = _att_fwd(a["q"], a["kv"], p["sinks8"], tag + "att_fwd")
    return a


def _fwd_out(a, p, l, dep=None):
    tag = f"l{l}_"
    h0 = a["h0"]
    a["ya"] = _mm(a["yn"], p["w_ssd_out"], "nn", tag + "ssd_out", dep=dep)
    a["yb"] = _mm(a["att"], p["w_att_out"], "nn", tag + "att_out", dep=dep)
    a["merged"] = _merge_fwd(a["gl"], a["ya"], a["yb"], tag + "merge")
    a["mix"] = _mm(a["merged"], p["w_mix_out"], "nn", tag + "mix_out")
    a["h1"] = _ln_fwd(h0, a["mix"], p["ln_mix_g"], p["ln_mix_b"], ALPHA, tag + "ln_mix")
    a["fg"], a["fu"], a["act"] = _ffn_in(a["h1"], p["w_ffn_gate"], p["w_ffn_up"], tag + "ffn_in")
    a["ffn"] = _mm(a["act"], p["w_ffn_down"], "nn", tag + "ffn_down")
    a["h2"] = _ln_fwd(a["h1"], a["ffn"], p["ln_ffn_g"], p["ln_ffn_b"], ALPHA, tag + "ln_ffn")
    return a


def _dw(x, dy, name, dep=None):
    return _mm(x, dy, "tn", name, out_dtype=BF16, dep=dep)


def _bwd_out(a, p, dh2, l, dep=None):
    tag = f"l{l}_b_"
    gw, gs = {}, {}
    du2, acc = _ln_bwd(a["h1"], a["ffn"], p["ln_ffn_g"], dh2, ALPHA, tag + "ln_ffn")
    gs["ln_ffn_g"], gs["ln_ffn_b"] = acc[0], acc[1]
    gw["w_ffn_down"] = _dw(a["act"], du2, tag + "dw_down", dep=dep)
    dfg, dfu = _ffn_dact(du2, p["w_ffn_down"], a["fg"], a["fu"], tag + "ffn_dact", dep=dep)
    gw["w_ffn_gate"] = _dw(a["h1"], dfg, tag + "dw_gate")
    gw["w_ffn_up"] = _dw(a["h1"], dfu, tag + "dw_up")
    dh1 = _mm(dfg, p["w_ffn_gate"], "nt", tag + "dh1_gate", add=du2, add_scale=ALPHA)
    dh1 = _mm(dfu, p["w_ffn_up"], "nt", tag + "dh1_up", add=dh1)
    du1, acc = _ln_bwd(a["h0"], a["mix"], p["ln_mix_g"], dh1, ALPHA, tag + "ln_mix")
    gs["ln_mix_g"], gs["ln_mix_b"] = acc[0], acc[1]
    gw["w_mix_out"] = _dw(a["merged"], du1, tag + "dw_mix")
    dmerged = _mm(du1, p["w_mix_out"], "nt", tag + "dmerged")
    dya, dyb, dproj = _merge_bwd(a["gl"], a["ya"], a["yb"], dmerged, tag + "merge",
                                 (None, CAT_OFF["gl"], MAIN_DIM))
    gw["w_ssd_out"] = _dw(a["yn"], dya, tag + "dw_ssd")
    gw["w_att_out"] = _dw(a["att"], dyb, tag + "dw_att")
    return {"du1": du1, "dya": dya, "dyb": dyb, "dproj": dproj}, gw, gs


def _bwd_mixers(a, p, carry, l, dep=None):
    tag = f"l{l}_b_"
    gs = {}
    du1, dproj = carry["du1"], carry["dproj"]

    def win(pc):
        return (dproj, CAT_OFF[pc], MAIN_DIM)

    dyn = _mm(carry["dya"], p["w_ssd_out"], "nt", tag + "dyn", dep=dep)
    datt = _mm(carry["dyb"], p["w_att_out"], "nt", tag + "datt", out_dtype=BF16, dep=dep)
    dproj, dkv, acc = _att_bwd(a["q"], a["kv"], p["sinks8"], datt, tag + "att", win("q"))
    gs["att_sinks"] = acc[0, :ATT_HEADS]
    dy, dproj, acc = _gnorm_bwd(a["y"], a["z"], p["norm_w"], dyn, tag + "gnorm", win("z"))
    gs["ssd_norm_w"] = acc[0]
    dxs, dbm, dcm, dproj, acc = _ssd_bwd(a["xsc"], a["bc"], a["cc"], a["dt"], p["ssd_par"], a["hs"], dy,
                                         tag + "ssd", win("dt"))
    gs["dt_bias"], gs["a_log"], gs["d_skip"] = (_from_group_major(acc[i]) for i in range(3))
    dconv_w, dconv_b = [], []
    for s, dout in (("xs", dxs), ("b", dbm), ("c", dcm)):
        dc, acc = _conv_bwd_pre(a[s], p["conv_w8"][s], p["conv_b8"][s], dout, tag + "conv_pre_" + s)
        dconv_w.append(acc[:CONV_TAPS])
        dconv_b.append(acc[CONV_TAPS])
        dproj = _conv_bwd_in(dc, p["conv_w8"][s], tag + "conv_in_" + s, win(s))
    gconv = jnp.concatenate(dconv_w, axis=1)
    gs["conv_b"] = jnp.concatenate(dconv_b)
    w_main, w_kv = p["w_cat"][:, :MAIN_DIM], p["w_cat"][:, MAIN_DIM:]
    dw_main, dw_kv = _dw(a["h0"], dproj, tag + "dw_in"), _dw(a["h0"], dkv, tag + "dw_in_kv")

    def grad_h0(dep=None):
        dh0 = _mm(dproj, w_main, "nt", tag + "dh0", add=du1, add_scale=ALPHA, dep=dep)
        return _mm(dkv, w_kv, "nt", tag + "dh0_kv", add=dh0)

    return grad_h0, _dw_in_chunks(dw_main, dw_kv), gconv, gs


def _step(x, target, w, m, v):
    x2 = x[0]
    t2 = target[0]
    tok = jnp.zeros(TOKEN.shape, TOKEN.dtype)

    ex = _exchange_start(_gather_items(w, IN_PART, 0), tok, "gather_l0_in_start")
    h = _ln_fwd(x2, None, w["ln_in_g"], w["ln_in_b"], 1.0, "ln_in")
    lands, tok = _exchange_wait(ex, h, "gather_l0_in_wait")
    p0 = _params_in(0, dict(zip(IN_PART, lands)), w)
    ex = _exchange_start(_gather_items(w, OUT_PART, 0) + _gather_items(w, IN_PART, 1), tok,
                         "gather_l0_out_l1_in_start")
    a0 = _fwd_mixers(h, p0, 0, dep=ex["token"])
    lands, tok = _exchange_wait(ex, a0["att"], "gather_l0_out_l1_in_wait")
    p0.update(_params_out(dict(zip(OUT_PART, lands))))
    p1 = _params_in(1, dict(zip(IN_PART, lands[len(OUT_PART):])), w)
    ex = _exchange_start(_gather_items(w, OUT_PART, 1), tok, "gather_l1_out_start")
    a0 = _fwd_out(a0, p0, 0, dep=ex["token"])
    lands, tok = _exchange_wait(ex, a0["h2"], "gather_l1_out_wait")
    p1.update(_params_out(dict(zip(OUT_PART, lands))))
    a1 = _fwd_out(_fwd_mixers(a0["h2"], p1, 1), p1, 1)

    sse, dh = _loss_fwd_bwd(a1["h2"], t2, "loss")
    loss = lax.psum(0.5 / D_MODEL * sse[0, 0], ("x", "y", "c"))

    carry, gw1, gs1 = _bwd_out(a1, p1, dh, 1)
    grad_h0, gw1["w_in"], gw1["conv_w"], gs = _bwd_mixers(a1, p1, carry, 1)
    dh = grad_h0()
    gs1.update(gs)
    ex1 = _exchange_start(_scatter_items(gw1, GATHERED), tok, "scatter_l1_start")
    carry, gw0, gs0 = _bwd_out(a0, p0, dh, 0, dep=ex1["token"])
    lands, tok = _exchange_wait(ex1, carry["dyb"], "scatter_l1_wait")
    land1 = dict(zip(GATHERED, lands))
    ex0 = _exchange_start(_scatter_items(gw0, OUT_PART), tok, "scatter_l0_out_start")
    grad_h0, gw0["w_in"], gw0["conv_w"], gs = _bwd_mixers(a0, p0, carry, 0, dep=ex0["token"])
    gs0.update(gs)
    lands, tok = _exchange_wait(ex0, gw0["w_in"], "scatter_l0_out_wait")
    land0 = dict(zip(OUT_PART, lands))
    ex0 = _exchange_start(_scatter_items(gw0, IN_PART), tok, "scatter_l0_in_start")
    dh = grad_h0(dep=ex0["token"])
    grad_x2, acc = _ln_bwd(x2, None, w["ln_in_g"], dh, 1.0, "ln_in_b")

    outs = [{} for _ in range(4)]

    def update(names):
        res = None
        for n in names:
            res = _adamw([land0[n], land1[n]], w[n], m[n], v[n], "adamw_" + n)
            for o, t in zip(outs, res):
                o[n] = t
        return res[1]

    update(OUT_PART)
    gsm = {"ln_in_g": acc[0], "ln_in_b": acc[1]}
    for n in SMALL[2:]:
        gsm[n] = jnp.stack([gs0[n], gs1[n]])
    small_shapes = {n: w[n].shape for n in SMALL}
    land_s = _all_gather_small(_pack_small(gsm), "small_grads_all_gather")
    res = _adamw([land_s], _pack_small(w)[None], _pack_small(m)[None], _pack_small(v)[None], "adamw_small")
    for o, t in zip(outs, res):
        o.update(_unpack_small(t[0], small_shapes))
    lands, _ = _exchange_wait(ex0, res[1], "scatter_l0_in_wait")
    land0.update(zip(IN_PART, lands))
    update(IN_PART)
    return loss, grad_x2[None], outs


WEIGHT_NAMES = ("ln_in_g", "ln_in_b", "w_in", "conv_w", "conv_b", "dt_bias", "a_log", "d_skip", "ssd_norm_w",
                "att_sinks", "w_ssd_out", "w_att_out", "w_mix_out", "ln_mix_g", "ln_mix_b", "w_ffn_gate",
                "w_ffn_up", "w_ffn_down", "ln_ffn_g", "ln_ffn_b")


def kernel(x, ln_in_g, ln_in_b, w_in, conv_w, conv_b, dt_bias, a_log, d_skip, ssd_norm_w, att_sinks, w_ssd_out, w_att_out, w_mix_out, ln_mix_g, ln_mix_b, w_ffn_gate, w_ffn_up, w_ffn_down, ln_ffn_g, ln_ffn_b, loss_target, m_ln_in_g, m_ln_in_b, m_w_in, m_conv_w, m_conv_b, m_dt_bias, m_a_log, m_d_skip, m_ssd_norm_w, m_att_sinks, m_w_ssd_out, m_w_att_out, m_w_mix_out, m_ln_mix_g, m_ln_mix_b, m_w_ffn_gate, m_w_ffn_up, m_w_ffn_down, m_ln_ffn_g, m_ln_ffn_b, v_ln_in_g, v_ln_in_b, v_w_in, v_conv_w, v_conv_b, v_dt_bias, v_a_log, v_d_skip, v_ssd_norm_w, v_att_sinks, v_w_ssd_out, v_w_att_out, v_w_mix_out, v_ln_mix_g, v_ln_mix_b, v_w_ffn_gate, v_w_ffn_up, v_w_ffn_down, v_ln_ffn_g, v_ln_ffn_b):
    w = dict(zip(WEIGHT_NAMES, (ln_in_g, ln_in_b, w_in, conv_w, conv_b, dt_bias, a_log, d_skip, ssd_norm_w,
                                att_sinks, w_ssd_out, w_att_out, w_mix_out, ln_mix_g, ln_mix_b, w_ffn_gate,
                                w_ffn_up, w_ffn_down, ln_ffn_g, ln_ffn_b)))
    m = dict(zip(WEIGHT_NAMES, (m_ln_in_g, m_ln_in_b, m_w_in, m_conv_w, m_conv_b, m_dt_bias, m_a_log, m_d_skip,
                                m_ssd_norm_w, m_att_sinks, m_w_ssd_out, m_w_att_out, m_w_mix_out, m_ln_mix_g,
                                m_ln_mix_b, m_w_ffn_gate, m_w_ffn_up, m_w_ffn_down, m_ln_ffn_g, m_ln_ffn_b)))
    v = dict(zip(WEIGHT_NAMES, (v_ln_in_g, v_ln_in_b, v_w_in, v_conv_w, v_conv_b, v_dt_bias, v_a_log, v_d_skip,
                                v_ssd_norm_w, v_att_sinks, v_w_ssd_out, v_w_att_out, v_w_mix_out, v_ln_mix_g,
                                v_ln_mix_b, v_w_ffn_gate, v_w_ffn_up, v_w_ffn_down, v_ln_ffn_g, v_ln_ffn_b)))
    loss, grad_x, outs = _step(x, loss_target, w, m, v)
    result = [loss, grad_x]
    for o in outs:
        result.extend(o[n] for n in WEIGHT_NAMES)
    return tuple(result)
```

```python
import math

import jax
import jax.numpy as jnp
from jax import lax
from jax.experimental import pallas as pl
from jax.experimental.pallas import tpu as pltpu

F32 = jnp.float32
BF16 = jnp.bfloat16

D_MODEL = 1024
DEPTH = 2
N_DEV = 8
ATT_HEADS = 16
ATT_KV_HEADS = 2
ATT_HEAD_DIM = 64
ATT_BLOCK = 128
SSD_D_INNER = 2048
SSD_HEADS = 32
SSD_GROUPS = 4
SSD_STATE = 128
SSD_CHUNK = 128
FFN_HIDDEN = 2816
LN_EPS = 1e-5
RMS_EPS = 1e-5
ALPHA = (2 * DEPTH) ** 0.25
Q_DIM = 1024
BC_DIM = 512
DT_PAD = 512

ADAM_LR = 0.001
ADAM_B1 = 0.9
ADAM_B2 = 0.999
ADAM_EPS = 1e-08
ADAM_WD = 0.01
ADAM_STEP = 10

LANE = 128
VMEM_LIMIT = 48 * 1024 * 1024
NEG = -1e30

_NN = (((1,), (0,)), ((), ()))
_NT = (((1,), (1,)), ((), ()))
_TN = (((0,), (0,)), ((), ()))
MESH_ID = pl.DeviceIdType.MESH


def _dot(a, b, dims=_NN):
    return lax.dot_general(a, b, dims, preferred_element_type=F32)


def _sig(x):
    return 1.0 / (1.0 + jnp.exp(-x))


def _softplus(x):
    return jnp.maximum(x, 0.0) + jnp.log(1.0 + jnp.exp(-jnp.abs(x)))


def _cparams(*sem):
    return pltpu.CompilerParams(dimension_semantics=sem, vmem_limit_bytes=VMEM_LIMIT)


def _pick(n, cap):
    if n <= cap:
        return n
    best = None
    for t in range(LANE, cap + 1, LANE):
        if n % t == 0:
            best = t
    assert best is not None, (n, cap)
    return best


def _tile(n):
    if n <= 1024 or n % 1024 == 0:
        return min(n, 1024)
    return _pick(n, 1408)


def _rows(n):
    return min(512, n)


def _window(x):
    return x if isinstance(x, tuple) else (x, 0, x.shape[1])


def _into(into, n_in, out_idx):
    buf, col0, width = into
    if buf is None:
        return [], [], {}, col0, width
    return [buf], [pl.BlockSpec(memory_space=pl.ANY)], {n_in: out_idx}, col0, width


def _mm(a, b, mode, name, add=None, add_scale=1.0, out_dtype=F32, dep=None):
    if mode == "nn":
        m, k = a.shape
        n = b.shape[1]
    elif mode == "nt":
        m, k = a.shape
        n = b.shape[0]
    else:
        k, m = a.shape
        n = b.shape[1]
    tm = _tile(m)
    tn = _pick(n, 2176) if mode == "tn" and n > 1024 else _tile(n)
    tk = _pick(k, 2176) if mode == "nt" and a.dtype == BF16 and k > 2816 else _tile(k)
    nk = k // tk
    has_add = add is not None
    dims = {"nn": _NN, "nt": _NT, "tn": _TN}[mode]

    def body(*refs):
        if dep is not None:
            refs = refs[:-3] + refs[-2:]
        if has_add:
            a_ref, b_ref, add_ref, o_ref, acc_ref = refs
        else:
            a_ref, b_ref, o_ref, acc_ref = refs
        kk = pl.program_id(2)

        @pl.when(kk == 0)
        def _():
            if has_add:
                acc_ref[...] = add_scale * add_ref[...].astype(F32)
            else:
                acc_ref[...] = jnp.zeros_like(acc_ref)

        acc_ref[...] += _dot(a_ref[...].astype(BF16), b_ref[...].astype(BF16), dims)

        @pl.when(kk == nk - 1)
        def _():
            o_ref[...] = acc_ref[...].astype(o_ref.dtype)

    if mode == "nn":
        a_spec = pl.BlockSpec((tm, tk), lambda i, j, kk: (i, kk))
        b_spec = pl.BlockSpec((tk, tn), lambda i, j, kk: (kk, j))
    elif mode == "nt":
        a_spec = pl.BlockSpec((tm, tk), lambda i, j, kk: (i, kk))
        b_spec = pl.BlockSpec((tn, tk), lambda i, j, kk: (j, kk))
    else:
        a_spec = pl.BlockSpec((tk, tm), lambda i, j, kk: (kk, i))
        b_spec = pl.BlockSpec((tk, tn), lambda i, j, kk: (kk, j))
    o_spec = pl.BlockSpec((tm, tn), lambda i, j, kk: (i, j))
    in_specs = [a_spec, b_spec] + ([o_spec] if has_add else [])
    args = (a, b) + ((add,) if has_add else ())
    if dep is not None:
        in_specs.append(pl.BlockSpec((8, LANE), lambda i, j, kk: (0, 0)))
        args += (dep,)
    return pl.pallas_call(
        body, name=name, grid=(m // tm, n // tn, nk),
        in_specs=in_specs, out_specs=o_spec,
        out_shape=jax.ShapeDtypeStruct((m, n), out_dtype),
        scratch_shapes=[pltpu.VMEM((tm, tn), F32)],
        compiler_params=_cparams("parallel", "parallel", "arbitrary"),
    )(*args)


def _vec_spec(width):
    return pl.BlockSpec((1, width), lambda i: (0, 0))


def _ln_fwd(a, b, gamma, beta, alpha, name):
    n_rows, dm = a.shape
    has_b = b is not None

    def body(*refs):
        if has_b:
            a_ref, b_ref, g_ref, be_ref, o_ref = refs
            u = alpha * a_ref[...] + b_ref[...]
        else:
            a_ref, g_ref, be_ref, o_ref = refs
            u = a_ref[...]
        mu = jnp.mean(u, axis=-1, keepdims=True)
        d = u - mu
        var = jnp.mean(d * d, axis=-1, keepdims=True)
        o_ref[...] = d * lax.rsqrt(var + LN_EPS) * g_ref[...] + be_ref[...]

    row = pl.BlockSpec((_rows(n_rows),dm), lambda i: (i, 0))
    in_specs = [row] + ([row] if has_b else []) + [_vec_spec(dm), _vec_spec(dm)]
    args = (a,) + ((b,) if has_b else ()) + (gamma.reshape(1, dm), beta.reshape(1, dm))
    return pl.pallas_call(
        body, name=name, grid=(n_rows // _rows(n_rows),), in_specs=in_specs, out_specs=row,
        out_shape=jax.ShapeDtypeStruct((n_rows, dm), F32),
        compiler_params=_cparams("parallel"),
    )(*args)


def _ln_bwd(a, b, gamma, dy, alpha, name):
    n_rows, dm = a.shape
    has_b = b is not None

    def body(*refs):
        if has_b:
            a_ref, b_ref, g_ref, dy_ref, du_ref, acc_ref = refs
            u = alpha * a_ref[...] + b_ref[...]
        else:
            a_ref, g_ref, dy_ref, du_ref, acc_ref = refs
            u = a_ref[...]

        @pl.when(pl.program_id(0) == 0)
        def _():
            acc_ref[...] = jnp.zeros_like(acc_ref)

        mu = jnp.mean(u, axis=-1, keepdims=True)
        d = u - mu
        var = jnp.mean(d * d, axis=-1, keepdims=True)
        rstd = lax.rsqrt(var + LN_EPS)
        xhat = d * rstd
        dyv = dy_ref[...]
        acc_ref[0:1, :] += jnp.sum(dyv * xhat, axis=0, keepdims=True)
        acc_ref[1:2, :] += jnp.sum(dyv, axis=0, keepdims=True)
        dxh = dyv * g_ref[...]
        m1 = jnp.mean(dxh, axis=-1, keepdims=True)
        m2 = jnp.mean(dxh * xhat, axis=-1, keepdims=True)
        du_ref[...] = rstd * (dxh - m1 - xhat * m2)

    row = pl.BlockSpec((_rows(n_rows),dm), lambda i: (i, 0))
    in_specs = [row] + ([row] if has_b else []) + [_vec_spec(dm), row]
    args = (a,) + ((b,) if has_b else ()) + (gamma.reshape(1, dm), dy)
    return pl.pallas_call(
        body, name=name, grid=(n_rows // _rows(n_rows),), in_specs=in_specs,
        out_specs=(row, pl.BlockSpec((8, dm), lambda i: (0, 0))),
        out_shape=(jax.ShapeDtypeStruct((n_rows, dm), F32), jax.ShapeDtypeStruct((8, dm), F32)),
        compiler_params=_cparams("arbitrary"),
    )(*args)


def _loss_fwd_bwd(y, target, name):
    n_rows, dm = y.shape

    def body(y_ref, t_ref, acc_ref, dy_ref):
        @pl.when(pl.program_id(0) == 0)
        def _():
            acc_ref[...] = jnp.zeros_like(acc_ref)

        d = y_ref[...] - t_ref[...]
        acc_ref[...] += jnp.sum(d * d)
        dy_ref[...] = d * (1.0 / dm)

    row = pl.BlockSpec((_rows(n_rows),dm), lambda i: (i, 0))
    return pl.pallas_call(
        body, name=name, grid=(n_rows // _rows(n_rows),), in_specs=[row, row],
        out_specs=(pl.BlockSpec((8, LANE), lambda i: (0, 0)), row),
        out_shape=(jax.ShapeDtypeStruct((8, LANE), F32), jax.ShapeDtypeStruct((n_rows, dm), F32)),
        compiler_params=_cparams("arbitrary"),
    )(y, target)


FFN_ROWS = 512


def _ffn_in(h, wg, wu, name, dep=None):
    m, k = h.shape
    n = wg.shape[1]
    tm, tn = min(FFN_ROWS, m), _tile(n)

    def body(*refs):
        h_ref, wg_ref, wu_ref = refs[:3]
        g_ref, u_ref, act_ref = refs[-3:]
        hb = h_ref[...].astype(BF16)
        g = _dot(hb, wg_ref[...])
        u = _dot(hb, wu_ref[...])
        g_ref[...] = g
        u_ref[...] = u
        act_ref[...] = (g * _sig(g) * u).astype(BF16)

    rows = pl.BlockSpec((tm, k), lambda j, i: (i, 0))
    wcol = pl.BlockSpec((k, tn), lambda j, i: (0, j))
    out = pl.BlockSpec((tm, tn), lambda j, i: (i, j))
    in_specs, args = [rows, wcol, wcol], (h, wg, wu)
    if dep is not None:
        in_specs.append(pl.BlockSpec((8, LANE), lambda j, i: (0, 0)))
        args += (dep,)
    return pl.pallas_call(
        body, name=name, grid=(n // tn, m // tm), in_specs=in_specs, out_specs=(out, out, out),
        out_shape=(jax.ShapeDtypeStruct((m, n), F32), jax.ShapeDtypeStruct((m, n), F32),
                   jax.ShapeDtypeStruct((m, n), BF16)),
        compiler_params=_cparams("parallel", "parallel"),
    )(*args)


def _ffn_dact(dy, wd, g, u, name, dep=None):
    m, k = dy.shape
    n = wd.shape[0]
    tm, tn = min(FFN_ROWS, m), _tile(n)

    def body(*refs):
        dy_ref, wd_ref, g_ref, u_ref = refs[:4]
        dg_ref, du_ref = refs[-2:]
        da = _dot(dy_ref[...].astype(BF16), wd_ref[...], _NT)
        gv = g_ref[...]
        s = _sig(gv)
        dg_ref[...] = (da * u_ref[...] * (s * (1.0 + gv * (1.0 - s)))).astype(BF16)
        du_ref[...] = (da * gv * s).astype(BF16)

    rows = pl.BlockSpec((tm, k), lambda j, i: (i, 0))
    wrow = pl.BlockSpec((tn, k), lambda j, i: (j, 0))
    out = pl.BlockSpec((tm, tn), lambda j, i: (i, j))
    in_specs, args = [rows, wrow, out, out], (dy, wd, g, u)
    if dep is not None:
        in_specs.append(pl.BlockSpec((8, LANE), lambda j, i: (0, 0)))
        args += (dep,)
    return pl.pallas_call(
        body, name=name, grid=(n // tn, m // tm), in_specs=in_specs, out_specs=(out, out),
        out_shape=(jax.ShapeDtypeStruct((m, n), BF16), jax.ShapeDtypeStruct((m, n), BF16)),
        compiler_params=_cparams("parallel", "parallel"),
    )(*args)


def _gate_specs(gl, n_rows, dm):
    arr, g0, _ = _window(gl)
    return arr, [pl.BlockSpec((_rows(n_rows), dm), lambda i, k=k: (i, g0 // dm + k)) for k in range(2)]


def _merge_fwd(gl, ya, yb, name):
    n_rows, dm = ya.shape
    gl_arr, gspecs = _gate_specs(gl, n_rows, dm)

    def body(ga_ref, gb_ref, ya_ref, yb_ref, o_ref):
        o_ref[...] = (_sig(ga_ref[...]) * ya_ref[...] + _sig(gb_ref[...]) * yb_ref[...]).astype(BF16)

    row = pl.BlockSpec((_rows(n_rows),dm), lambda i: (i, 0))
    return pl.pallas_call(
        body, name=name, grid=(n_rows // _rows(n_rows),), in_specs=gspecs + [row, row], out_specs=row,
        out_shape=jax.ShapeDtypeStruct((n_rows, dm), BF16),
        compiler_params=_cparams("parallel"),
    )(gl_arr, gl_arr, ya, yb)


def _merge_bwd(gl, ya, yb, dmerged, name, into):
    n_rows, dm = ya.shape
    gl_arr, gspecs = _gate_specs(gl, n_rows, dm)
    extra, extra_specs, aliases, col0, width = _into(into, 5, 2)

    def body(*refs):
        ga_ref, gb_ref, ya_ref, yb_ref, dm_ref = refs[:5]
        dya_ref, dyb_ref, dgl_ref = refs[-3:]
        ga = _sig(ga_ref[...])
        gb = _sig(gb_ref[...])
        dmv = dm_ref[...]
        dya_ref[...] = (dmv * ga).astype(BF16)
        dyb_ref[...] = (dmv * gb).astype(BF16)
        dgl_ref[:, :dm] = (dmv * ya_ref[...] * ga * (1.0 - ga)).astype(BF16)
        dgl_ref[:, dm:] = (dmv * yb_ref[...] * gb * (1.0 - gb)).astype(BF16)

    row = pl.BlockSpec((_rows(n_rows),dm), lambda i: (i, 0))
    row2 = pl.BlockSpec((_rows(n_rows),2 * dm), lambda i: (i, col0 // (2 * dm)))
    return pl.pallas_call(
        body, name=name, grid=(n_rows // _rows(n_rows),), in_specs=gspecs + [row, row, row] + extra_specs,
        out_specs=(row, row, row2),
        out_shape=(jax.ShapeDtypeStruct((n_rows, dm), BF16), jax.ShapeDtypeStruct((n_rows, dm), BF16),
                   jax.ShapeDtypeStruct((n_rows, width), BF16)),
        input_output_aliases=aliases,
        compiler_params=_cparams("parallel"),
    )(gl_arr, gl_arr, ya, yb, dmerged, *extra)


CONV_TAPS = 4
CONV_COLS = 512
HALO = 8


def _shift_down(cur, prev8, s, row8):
    r = pltpu.roll(cur, s, axis=0)
    top = jnp.where(row8 < s, pltpu.roll(prev8, s, axis=0), r[0:HALO])
    return jnp.concatenate([top, r[HALO:]], axis=0)


def _shift_up(cur, next8, s, row8):
    n = cur.shape[0]
    r = pltpu.roll(cur, n - s, axis=0)
    bot = jnp.where(row8 >= HALO - s, pltpu.roll(next8, HALO - s, axis=0), r[n - HALO:])
    return jnp.concatenate([r[:n - HALO], bot], axis=0)


def _conv_pre(u_ref, prev_ref, w_ref, b_ref, li):
    cur = u_ref[...]
    prev8 = jnp.where(li == 0, 0.0, prev_ref[...])
    row8 = lax.broadcasted_iota(jnp.int32, prev8.shape, 0)
    shifted = [cur] + [_shift_down(cur, prev8, s, row8) for s in range(1, CONV_TAPS)]
    acc = b_ref[...] + shifted[0] * w_ref[CONV_TAPS - 1:CONV_TAPS, :]
    for s in range(1, CONV_TAPS):
        acc = acc + shifted[s] * w_ref[CONV_TAPS - 1 - s:CONV_TAPS - s, :]
    return acc, shifted


def _conv_specs(n_rows, tl, col0=0):
    off = col0 // CONV_COLS
    cur = pl.BlockSpec((tl, CONV_COLS), lambda cj, li: (li, cj + off))
    prev = pl.BlockSpec((HALO, CONV_COLS), lambda cj, li: (jnp.maximum(li * (tl // HALO) - 1, 0), cj + off))
    nxt = pl.BlockSpec((HALO, CONV_COLS),
                       lambda cj, li: (jnp.minimum((li + 1) * (tl // HALO), n_rows // HALO - 1), cj + off))
    par = pl.BlockSpec((8, CONV_COLS), lambda cj, li: (0, cj + off))
    return cur, prev, nxt, par


def _conv_fwd(u, w8, b8, name):
    u, u0, c = _window(u)
    n_rows = u.shape[0]
    tl = _rows(n_rows)
    cur, _, _, par = _conv_specs(n_rows, tl)
    ucur, prev, _, _ = _conv_specs(n_rows, tl, u0)

    def body(u_ref, prev_ref, w_ref, b_ref, o_ref):
        acc, _ = _conv_pre(u_ref, prev_ref, w_ref, b_ref[0:1, :], pl.program_id(1))
        o_ref[...] = acc * _sig(acc)

    return pl.pallas_call(
        body, name=name, grid=(c // CONV_COLS, n_rows // tl), in_specs=[ucur, prev, par, par], out_specs=cur,
        out_shape=jax.ShapeDtypeStruct((n_rows, c), F32),
        compiler_params=_cparams("parallel", "parallel"),
    )(u, u, w8, b8)


def _conv_bwd_pre(u, w8, b8, dout, name):
    u, u0, c = _window(u)
    n_rows = u.shape[0]
    tl = _rows(n_rows)
    cur, _, _, par = _conv_specs(n_rows, tl)
    ucur, prev, _, _ = _conv_specs(n_rows, tl, u0)

    def body(u_ref, prev_ref, w_ref, b_ref, do_ref, dc_ref, acc_ref):
        @pl.when(pl.program_id(1) == 0)
        def _():
            acc_ref[...] = jnp.zeros_like(acc_ref)

        acc, shifted = _conv_pre(u_ref, prev_ref, w_ref, b_ref[0:1, :], pl.program_id(1))
        sg = _sig(acc)
        dc = do_ref[...] * (sg * (1.0 + acc * (1.0 - sg)))
        dc_ref[...] = dc
        for k in range(CONV_TAPS):
            acc_ref[k:k + 1, :] += jnp.sum(dc * shifted[CONV_TAPS - 1 - k], axis=0, keepdims=True)
        acc_ref[CONV_TAPS:CONV_TAPS + 1, :] += jnp.sum(dc, axis=0, keepdims=True)

    return pl.pallas_call(
        body, name=name, grid=(c // CONV_COLS, n_rows // tl), in_specs=[ucur, prev, par, par, cur],
        out_specs=(cur, par),
        out_shape=(jax.ShapeDtypeStruct((n_rows, c), F32), jax.ShapeDtypeStruct((8, c), F32)),
        compiler_params=_cparams("parallel", "arbitrary"),
    )(u, u, w8, b8, dout)


def _conv_bwd_in(dc, w8, name, into):
    n_rows, c = dc.shape
    tl = _rows(n_rows)
    cur, _, nxt, par = _conv_specs(n_rows, tl)
    n_l = n_rows // tl
    extra, extra_specs, aliases, col0, width = _into(into, 3, 0)
    out_spec = _conv_specs(n_rows, tl, col0)[0]

    def body(*refs):
        dc_ref, next_ref, w_ref = refs[:3]
        o_ref = refs[-1]
        cur_v = dc_ref[...]
        next8 = jnp.where(pl.program_id(1) == n_l - 1, 0.0, next_ref[...])
        row8 = lax.broadcasted_iota(jnp.int32, next8.shape, 0)
        acc = cur_v * w_ref[CONV_TAPS - 1:CONV_TAPS, :]
        for s in range(1, CONV_TAPS):
            acc = acc + _shift_up(cur_v, next8, s, row8) * w_ref[CONV_TAPS - 1 - s:CONV_TAPS - s, :]
        o_ref[...] = acc.astype(BF16)

    return pl.pallas_call(
        body, name=name, grid=(c // CONV_COLS, n_l), in_specs=[cur, nxt, par] + extra_specs, out_specs=out_spec,
        out_shape=jax.ShapeDtypeStruct((n_rows, width), BF16), input_output_aliases=aliases,
        compiler_params=_cparams("parallel", "parallel"),
    )(dc, dc, w8, *extra)


NORM_GROUP = SSD_D_INNER // SSD_GROUPS


def _gnorm_fwd(y, z, w, name):
    n_rows, c = y.shape
    z, z0, _ = _window(z)
    zoff = z0 // NORM_GROUP

    def body(y_ref, z_ref, w_ref, o_ref):
        zv = z_ref[...]
        yg = y_ref[...] * (zv * _sig(zv))
        r = lax.rsqrt(jnp.mean(yg * yg, axis=-1, keepdims=True) + RMS_EPS)
        o_ref[...] = (yg * r * w_ref[...]).astype(BF16)

    blk = pl.BlockSpec((_rows(n_rows),NORM_GROUP), lambda i, j: (i, j))
    zblk = pl.BlockSpec((_rows(n_rows),NORM_GROUP), lambda i, j: (i, j + zoff))
    wspec = pl.BlockSpec((1, NORM_GROUP), lambda i, j: (0, j))
    return pl.pallas_call(
        body, name=name, grid=(n_rows // _rows(n_rows), c // NORM_GROUP), in_specs=[blk, zblk, wspec], out_specs=blk,
        out_shape=jax.ShapeDtypeStruct((n_rows, c), BF16),
        compiler_params=_cparams("parallel", "parallel"),
    )(y, z, w.reshape(1, c))


def _gnorm_bwd(y, z, w, dyn, name, into):
    n_rows, c = y.shape
    z, z0, _ = _window(z)
    zoff = z0 // NORM_GROUP
    extra, extra_specs, aliases, col0, width = _into(into, 4, 1)
    doff = col0 // NORM_GROUP

    def body(*refs):
        y_ref, z_ref, w_ref, dn_ref = refs[:4]
        dy_ref, dz_ref, acc_ref = refs[-3:]
        @pl.when(pl.program_id(1) == 0)
        def _():
            acc_ref[...] = jnp.zeros_like(acc_ref)

        zv = z_ref[...]
        yv = y_ref[...]
        sz = _sig(zv)
        silu = zv * sz
        yg = yv * silu
        r = lax.rsqrt(jnp.mean(yg * yg, axis=-1, keepdims=True) + RMS_EPS)
        nrm = yg * r
        dn = dn_ref[...]
        acc_ref[0:1, :] += jnp.sum(dn * nrm, axis=0, keepdims=True)
        dnw = dn * w_ref[...]
        dyg = r * (dnw - nrm * jnp.mean(dnw * nrm, axis=-1, keepdims=True))
        dy_ref[...] = dyg * silu
        dz_ref[...] = (dyg * yv * (sz * (1.0 + zv * (1.0 - sz)))).astype(BF16)

    blk = pl.BlockSpec((_rows(n_rows),NORM_GROUP), lambda j, i: (i, j))
    zblk = pl.BlockSpec((_rows(n_rows),NORM_GROUP), lambda j, i: (i, j + zoff))
    wspec = pl.BlockSpec((1, NORM_GROUP), lambda j, i: (0, j))
    aspec = pl.BlockSpec((8, NORM_GROUP), lambda j, i: (0, j))
    return pl.pallas_call(
        body, name=name, grid=(c // NORM_GROUP, n_rows // _rows(n_rows)),
        in_specs=[blk, zblk, wspec, blk] + extra_specs,
        out_specs=(blk, pl.BlockSpec((_rows(n_rows), NORM_GROUP), lambda j, i: (i, j + doff)), aspec),
        out_shape=(jax.ShapeDtypeStruct((n_rows, c), F32), jax.ShapeDtypeStruct((n_rows, width), BF16),
                   jax.ShapeDtypeStruct((8, c), F32)),
        input_output_aliases=aliases,
        compiler_params=_cparams("parallel", "arbitrary"),
    )(y, z, w.reshape(1, c), dyn, *extra)


ATT_SCALE = ATT_HEAD_DIM ** -0.5
ATT_SLOPES = [2.0 ** (-8.0 * (h + 1) / ATT_HEADS) for h in range(ATT_HEADS)]
Q_PER_KV = ATT_HEADS // ATT_KV_HEADS


def _dup_half(t, g, lo):
    tr = pltpu.roll(t, ATT_HEAD_DIM, axis=1)
    return jnp.where(lo, t, tr) if g == 0 else jnp.where(lo, tr, t)


def _att_band(kv_ref, kvp_ref, n):
    cur = kv_ref[...]
    prev = jnp.where(n == 0, 0.0, kvp_ref[...])
    lo = lax.broadcasted_iota(jnp.int32, (ATT_BLOCK, LANE), 1) < ATT_HEAD_DIM
    bands = []
    for g in range(ATT_KV_HEADS):
        kb = jnp.concatenate([_dup_half(prev[:, :LANE], g, lo), _dup_half(cur[:, :LANE], g, lo)], axis=0)
        vb = jnp.concatenate([_dup_half(prev[:, LANE:], g, lo), _dup_half(cur[:, LANE:], g, lo)], axis=0)
        bands.append((kb.astype(BF16), vb.astype(BF16)))
    return bands


def _att_tile(n):
    shape = (2 * ATT_BLOCK, ATT_BLOCK)
    row = lax.broadcasted_iota(jnp.int32, shape, 0)
    i = row & (ATT_BLOCK - 1)
    s = lax.broadcasted_iota(jnp.int32, shape, 1)
    upper = s > i
    dist = ((i - s) & (ATT_BLOCK - 1)).astype(F32)
    dead = upper & (n == 0)
    return upper, dist, dead, row[:, 0:1] < ATT_BLOCK


def _stack_pair(t, lo):
    return jnp.concatenate([jnp.where(lo, t, 0.0), jnp.where(lo, 0.0, t)], axis=0).astype(BF16)


def _att_exp(qs, kb, s_ref, j, tile):
    upper, dist, dead, first = tile
    s2 = _dot(qs, kb, _NT)
    slope = jnp.where(first, ATT_SLOPES[2 * j], ATT_SLOPES[2 * j + 1])
    sink = jnp.where(first, s_ref[0:1, 2 * j:2 * j + 1], s_ref[0:1, 2 * j + 1:2 * j + 2])
    s = jnp.where(upper, s2[:, :ATT_BLOCK], s2[:, ATT_BLOCK:]) - slope * dist
    s = jnp.where(dead, NEG, s)
    m = jnp.maximum(jnp.max(s, axis=-1, keepdims=True), sink)
    return jnp.exp(s - m), jnp.exp(sink - m)


def _band_split(t, upper):
    return jnp.concatenate([jnp.where(upper, t, 0.0), jnp.where(upper, 0.0, t)], axis=1)


def _att_fwd(q, kv, sinks8, name):
    q, q0, _ = _window(q)
    kv, kv0, _ = _window(kv)
    qoff, kvoff = q0 // Q_DIM, kv0 // (2 * LANE)
    n_rows = q.shape[0]
    nb = n_rows // ATT_BLOCK

    def body(q_ref, kv_ref, kvp_ref, s_ref, o_ref, o32_ref):
        n = pl.program_id(0)
        bands = _att_band(kv_ref, kvp_ref, n)
        lo = lax.broadcasted_iota(jnp.int32, (ATT_BLOCK, LANE), 1) < ATT_HEAD_DIM
        tile = _att_tile(n)
        ones_b = jnp.ones((2 * ATT_BLOCK, LANE), BF16)
        for j in range(ATT_HEADS // 2):
            kb, vb = bands[2 * j // Q_PER_KV]
            qs = _stack_pair(q_ref[:, j * LANE:(j + 1) * LANE] * ATT_SCALE, lo)
            p, es = _att_exp(qs, kb, s_ref, j, tile)
            pv = _dot(_band_split(p, tile[0]).astype(BF16), jnp.concatenate([vb, ones_b], axis=1))
            out = pv[:, :LANE] / (pv[:, LANE:] + es)
            out = jnp.where(lo, out[:ATT_BLOCK], out[ATT_BLOCK:])
            o_ref[:, j * LANE:(j + 1) * LANE] = out.astype(BF16)
            o32_ref[:, j * LANE:(j + 1) * LANE] = out

    return pl.pallas_call(
        body, name=name, grid=(nb,),
        in_specs=[pl.BlockSpec((ATT_BLOCK, Q_DIM), lambda n: (n, qoff)),
                  pl.BlockSpec((ATT_BLOCK, 2 * LANE), lambda n: (n, kvoff)),
                  pl.BlockSpec((ATT_BLOCK, 2 * LANE), lambda n: (jnp.maximum(n - 1, 0), kvoff)),
                  pl.BlockSpec((8, LANE), lambda n: (0, 0))],
        out_specs=(pl.BlockSpec((ATT_BLOCK, Q_DIM), lambda n: (n, 0)),) * 2,
        out_shape=(jax.ShapeDtypeStruct((n_rows, Q_DIM), BF16), jax.ShapeDtypeStruct((n_rows, Q_DIM), F32)),
        compiler_params=_cparams("parallel"),
    )(q, kv, kv, sinks8)


def _att_bwd(q, kv, sinks8, out32, dout, name, into):
    q, q0, _ = _window(q)
    kv, kv0, _ = _window(kv)
    qoff, kvoff = q0 // Q_DIM, kv0 // (2 * LANE)
    n_rows = q.shape[0]
    nb = n_rows // ATT_BLOCK

    extra, extra_specs, aliases, col0, width = _into(into, 6, 0)
    dqoff = col0 // Q_DIM

    def body(*refs):
        q_ref, kv_ref, kvp_ref, s_ref, o_ref, do_ref = refs[:6]
        dq_ref, dkv_ref, acc_ref, carry_ref = refs[-4:]
        n = pl.program_id(0)

        @pl.when(n == 0)
        def _():
            acc_ref[...] = jnp.zeros_like(acc_ref)
            carry_ref[...] = jnp.zeros_like(carry_ref)

        @pl.when(n == nb)
        def _():
            dkv_ref[...] = carry_ref[...].astype(BF16)

        @pl.when(n < nb)
        def _():
            bands = _att_band(kv_ref, kvp_ref, n)
            lo = lax.broadcasted_iota(jnp.int32, (ATT_BLOCK, LANE), 1) < ATT_HEAD_DIM
            lane1 = lax.broadcasted_iota(jnp.int32, (1, LANE), 1)
            tile = _att_tile(n)
            upper, first = tile[0], tile[3]
            ones_b = jnp.ones((ATT_BLOCK, LANE), BF16)
            ones2_b = jnp.ones((2 * LANE, LANE), BF16)
            dk_acc = [jnp.zeros((2 * ATT_BLOCK, LANE), F32) for _ in range(ATT_KV_HEADS)]
            dv_acc = [jnp.zeros((2 * ATT_BLOCK, LANE), F32) for _ in range(ATT_KV_HEADS)]
            dsink = jnp.zeros((1, LANE), F32)
            for j in range(ATT_HEADS // 2):
                g = 2 * j // Q_PER_KV
                kb, vb = bands[g]
                qs = _stack_pair(q_ref[:, j * LANE:(j + 1) * LANE] * ATT_SCALE, lo)
                dop = do_ref[:, j * LANE:(j + 1) * LANE].astype(F32)
                dos = _stack_pair(dop, lo)
                pu, es = _att_exp(qs, kb, s_ref, j, tile)
                inv = 1.0 / (_dot(pu.astype(BF16), ones_b) + es)
                p = pu * inv
                od = dos.astype(F32) * jnp.concatenate([o_ref[:, j * LANE:(j + 1) * LANE]] * 2, axis=0)
                od_hi = od.astype(BF16)
                delta = _dot(jnp.concatenate([od_hi, (od - od_hi.astype(F32)).astype(BF16)], axis=1), ones2_b)
                dp2 = _dot(dos, vb, _NT)
                dp = jnp.where(upper, dp2[:, :ATT_BLOCK], dp2[:, ATT_BLOCK:])
                ds2 = _band_split(p * (dp - delta), upper)
                psd = jnp.sum(es * inv * delta, axis=0, keepdims=True)
                psd0 = jnp.sum(jnp.where(first, es * inv * delta, 0.0), axis=0, keepdims=True)
                dsink = jnp.where(lane1 == 2 * j, -psd0, jnp.where(lane1 == 2 * j + 1, psd0 - psd, dsink))
                ds2_b = ds2.astype(BF16)
                dq = _dot(ds2_b, kb) * ATT_SCALE
                dq_ref[:, j * LANE:(j + 1) * LANE] = jnp.where(lo, dq[:ATT_BLOCK], dq[ATT_BLOCK:]).astype(BF16)
                dk_acc[g] = dk_acc[g] + _dot(ds2_b, qs, _TN)
                dv_acc[g] = dv_acc[g] + _dot(_band_split(p, upper).astype(BF16), dos, _TN)
            acc_ref[0:1, :] += dsink
            lo2 = lax.broadcasted_iota(jnp.int32, (2 * ATT_BLOCK, LANE), 1) < ATT_HEAD_DIM
            folded = []
            for acc in (dk_acc, dv_acc):
                t0 = acc[0] + pltpu.roll(acc[0], ATT_HEAD_DIM, axis=1)
                t1 = acc[1] + pltpu.roll(acc[1], ATT_HEAD_DIM, axis=1)
                folded.append(jnp.where(lo2, t0, t1))
            band = jnp.concatenate(folded, axis=1)
            dkv_ref[...] = (carry_ref[...] + band[:ATT_BLOCK]).astype(BF16)
            carry_ref[...] = band[ATT_BLOCK:]

    def qmap(n):
        return (jnp.minimum(n, nb - 1), 0)

    return pl.pallas_call(
        body, name=name, grid=(nb + 1,),
        in_specs=[pl.BlockSpec((ATT_BLOCK, Q_DIM), lambda n: (jnp.minimum(n, nb - 1), qoff)),
                  pl.BlockSpec((ATT_BLOCK, 2 * LANE), lambda n: (jnp.minimum(n, nb - 1), kvoff)),
                  pl.BlockSpec((ATT_BLOCK, 2 * LANE),
                               lambda n: (jnp.maximum(jnp.minimum(n, nb - 1) - 1, 0), kvoff)),
                  pl.BlockSpec((8, LANE), lambda n: (0, 0)),
                  pl.BlockSpec((ATT_BLOCK, Q_DIM), qmap),
                  pl.BlockSpec((ATT_BLOCK, Q_DIM), qmap)] + extra_specs,
        out_specs=(pl.BlockSpec((ATT_BLOCK, Q_DIM), lambda n: (jnp.minimum(n, nb - 1), dqoff)),
                   pl.BlockSpec((ATT_BLOCK, 2 * LANE), lambda n: (jnp.maximum(n - 1, 0), 0)),
                   pl.BlockSpec((8, LANE), lambda n: (0, 0))),
        out_shape=(jax.ShapeDtypeStruct((n_rows, width), BF16), jax.ShapeDtypeStruct((n_rows, 2 * LANE), BF16),
                   jax.ShapeDtypeStruct((8, LANE), F32)),
        input_output_aliases=aliases,
        scratch_shapes=[pltpu.VMEM((ATT_BLOCK, 2 * LANE), F32)],
        compiler_params=_cparams("arbitrary"),
    )(q, kv, kv, sinks8, out32, dout, *extra)


HEADS_PER_GROUP = SSD_HEADS // SSD_GROUPS
PAIRS_PER_GROUP = HEADS_PER_GROUP // 2
T = SSD_CHUNK


def _cumsum_mm(mat, x):
    hi = x.astype(BF16)
    r = x - hi.astype(F32)
    mid = r.astype(BF16)
    lo = (r - mid.astype(F32)).astype(BF16)
    w = x.shape[1]
    out = _dot(mat, jnp.concatenate([hi, mid, lo], axis=1))
    return out[:, :w] + out[:, w:2 * w] + out[:, 2 * w:]


def _ssd_prep(dtr_ref, par_ref):
    dt = _softplus(dtr_ref[...] + par_ref[0:1, :])
    a = -jnp.exp(par_ref[1:2, :])
    ri = lax.broadcasted_iota(jnp.int32, (T, T), 0)
    ci = lax.broadcasted_iota(jnp.int32, (T, T), 1)
    cs = _cumsum_mm((ri >= ci).astype(BF16), dt * a)
    lo = lax.broadcasted_iota(jnp.int32, (T, LANE), 1) < SSD_CHUNK // 2

    def expand(arr):
        rows = arr.shape[0]
        return jnp.concatenate([jnp.where(lo[:rows], arr[:, 2 * j:2 * j + 1], arr[:, 2 * j + 1:2 * j + 2])
                                for j in range(PAIRS_PER_GROUP)], axis=1)

    tot = cs[T - 1:T, :]
    return {"dt": dt, "a": a, "cs": cs, "cst": cs.T, "lo": lo, "ri": ri, "ci": ci, "expand": expand,
            "dt_x": expand(dt), "ecs_x": expand(jnp.exp(cs)), "dec_x": expand(jnp.exp(tot - cs)),
            "et_x": expand(jnp.exp(tot)), "etot": jnp.exp(tot), "dsk_x": expand(par_ref[2:3, :])}


def _wide_masks():
    r = lax.broadcasted_iota(jnp.int32, (T, 2 * T), 0)
    l = lax.broadcasted_iota(jnp.int32, (T, 2 * T), 1)
    s = l & (T - 1)
    return r >= s, s >= r, l < T


def _wide_cs(q, k0, even):
    cs, cst = q["cs"], q["cst"]
    col = jnp.where(even, cs[:, k0:k0 + 1], cs[:, k0 + 1:k0 + 2])
    row = jnp.concatenate([cst[k0:k0 + 1, :], cst[k0 + 1:k0 + 2, :]], axis=1)
    return col, row


def _ssd_fwd(xs, bm, cm, dtr, par, name):
    dtr, dt0, _ = _window(dtr)
    dtoff = dt0 // LANE
    n_rows = xs.shape[0]
    nc = n_rows // T
    gw = PAIRS_PER_GROUP * LANE

    def body(x_ref, b_ref, c_ref, dtr_ref, par_ref, y_ref, hs_ref, h_ref):
        @pl.when(pl.program_id(1) == 0)
        def _():
            h_ref[...] = jnp.zeros_like(h_ref)

        q = _ssd_prep(dtr_ref, par_ref)
        lo = q["lo"]
        tri_w, _, even = _wide_masks()
        bg_b = b_ref[...].astype(BF16)
        cg_b = c_ref[...].astype(BF16)
        xv = x_ref[...]
        xdt = xv * q["dt_x"]
        h = h_ref[...]
        hs_ref[0, 0] = h
        yo = q["ecs_x"] * _dot(cg_b, h.astype(BF16))
        h_ref[...] = h * q["et_x"] + _dot(bg_b, (xdt * q["dec_x"]).astype(BF16), _TN)
        cb = _dot(cg_b, bg_b, _NT)
        cb_w = jnp.concatenate([cb, cb], axis=1)
        for j in range(PAIRS_PER_GROUP):
            col, row = _wide_cs(q, 2 * j, even)
            m_w = (jnp.exp(jnp.where(tri_w, col - row, NEG)) * cb_w).astype(BF16)
            sl = slice(j * LANE, (j + 1) * LANE)
            y_ref[:, sl] = (_dot(m_w, _stack_pair(xdt[:, sl], lo)) + yo[:, sl] + q["dsk_x"][:, sl] * xv[:, sl])

    return pl.pallas_call(
        body, name=name, grid=(SSD_GROUPS, nc),
        in_specs=[pl.BlockSpec((T, gw), lambda g, c: (c, g)),
                  pl.BlockSpec((T, SSD_STATE), lambda g, c: (c, g)),
                  pl.BlockSpec((T, SSD_STATE), lambda g, c: (c, g)),
                  pl.BlockSpec((T, LANE), lambda g, c: (c, g + dtoff)),
                  pl.BlockSpec((8, LANE), lambda g, c: (0, g))],
        out_specs=(pl.BlockSpec((T, gw), lambda g, c: (c, g)),
                   pl.BlockSpec((1, 1, SSD_STATE, gw), lambda g, c: (g, c, 0, 0))),
        out_shape=(jax.ShapeDtypeStruct((n_rows, SSD_D_INNER), F32),
                   jax.ShapeDtypeStruct((SSD_GROUPS, nc, SSD_STATE, gw), F32)),
        scratch_shapes=[pltpu.VMEM((SSD_STATE, gw), F32)],
        compiler_params=_cparams("parallel", "arbitrary"),
    )(xs, bm, cm, dtr, par)


def _ssd_bwd(xs, bm, cm, dtr, par, hs, dy, name, into):
    dtr, dt0, _ = _window(dtr)
    dtoff = dt0 // LANE
    n_rows = xs.shape[0]
    nc = n_rows // T
    gw = PAIRS_PER_GROUP * LANE
    extra, extra_specs, aliases, col0, width = _into(into, 7, 3)
    ddoff = col0 // LANE

    def body(*refs):
        x_ref, b_ref, c_ref, dtr_ref, par_ref, hs_ref, dy_ref = refs[:7]
        dx_ref, db_ref, dc_ref, ddtr_ref, acc_ref, dh_ref = refs[-6:]

        @pl.when(pl.program_id(1) == 0)
        def _():
            dh_ref[...] = jnp.zeros_like(dh_ref)
            acc_ref[...] = jnp.zeros_like(acc_ref)

        q = _ssd_prep(dtr_ref, par_ref)
        lo, dt, a = q["lo"], q["dt"], q["a"]
        tri_w, trit_w, even = _wide_masks()
        lane = lax.broadcasted_iota(jnp.int32, (T, LANE), 1)
        lane1 = lane[0:1, :]
        last_row = lax.broadcasted_iota(jnp.int32, (T, 1), 0) == T - 1
        bg_b = b_ref[...].astype(BF16)
        cg_b = c_ref[...].astype(BF16)
        xv = x_ref[...]
        dyv = dy_ref[...]
        xdt = xv * q["dt_x"]
        h = hs_ref[0, 0]
        dhn = dh_ref[...]
        h_b, dhn_b = h.astype(BF16), dhn.astype(BF16)
        yo = q["ecs_x"] * _dot(cg_b, h_b)
        bdh = q["dec_x"] * _dot(bg_b, dhn_b)
        dye = (dyv * q["ecs_x"]).astype(BF16)
        xd = (xdt * q["dec_x"]).astype(BF16)
        dcg = _dot(dye, h_b, _NT)
        dbg = _dot(xd, dhn_b, _NT)
        dh_ref[...] = dhn * q["et_x"] + _dot(cg_b, dye, _TN)
        e4_all = xdt * bdh
        f_all = dyv * yo - e4_all
        tot_row = jnp.sum(e4_all, axis=0, keepdims=True) + q["et_x"] * jnp.sum(h * dhn, axis=0, keepdims=True)
        dsk_row = jnp.sum(dyv * xv, axis=0, keepdims=True)
        cb = _dot(cg_b, bg_b, _NT)
        cbt = _dot(bg_b, cg_b, _NT)
        cb_w = jnp.concatenate([cb, cb], axis=1)
        cbt_w = jnp.concatenate([cbt, cbt], axis=1)
        dcb = jnp.zeros((T, T), F32)
        dcbt = jnp.zeros((T, T), F32)
        dcs_acc = jnp.zeros((T, LANE), F32)
        ddt_acc = jnp.zeros((T, LANE), F32)
        dsk_acc = jnp.zeros((1, LANE), F32)
        tot_acc = jnp.zeros((1, LANE), F32)
        ind_r = lax.broadcasted_iota(jnp.int32, (2 * T, LANE), 0)
        ind_l = lax.broadcasted_iota(jnp.int32, (2 * T, LANE), 1)

        def halves(t):
            return (jnp.sum(jnp.where(lo[0:1], t, 0.0), axis=-1, keepdims=True),
                    jnp.sum(jnp.where(lo[0:1], 0.0, t), axis=-1, keepdims=True))

        def split2(t):
            hi = t.astype(BF16)
            return jnp.concatenate([hi, (t - hi.astype(F32)).astype(BF16)], axis=1)

        for j in range(PAIRS_PER_GROUP):
            k0, k1 = 2 * j, 2 * j + 1
            sl = slice(j * LANE, (j + 1) * LANE)
            col, row = _wide_cs(q, k0, even)
            lm_w = jnp.exp(jnp.where(tri_w, col - row, NEG))
            lmt_w = jnp.exp(jnp.where(trit_w, row - col, NEG))
            dyp, xp = dyv[:, sl], xdt[:, sl]
            dym, xm = _stack_pair(dyp, lo), _stack_pair(xp, lo)
            dm_w = _dot(dyp.astype(BF16), xm, _NT)
            dmt_w = _dot(xp.astype(BF16), dym, _NT)
            mm_w = lm_w * cb_w
            mmt_w = lmt_w * cbt_w
            dxdt = _dot(mmt_w.astype(BF16), dym) + bdh[:, sl]
            g1 = dm_w * lm_w
            g2 = dmt_w * lmt_w
            dcb = dcb + g1[:, :T] + g1[:, T:]
            dcbt = dcbt + g2[:, :T] + g2[:, T:]
            ind_w = jnp.where(ind_l == jnp.where(ind_r < T, k0, k1), 1.0, 0.0).astype(BF16)
            ind_p = jnp.where(ind_l[:T] == jnp.where(ind_r[:T] < SSD_CHUNK // 2, k0, k1), 1.0, 0.0).astype(BF16)
            dcs_acc = dcs_acc + _dot(
                jnp.concatenate([split2(dm_w * mm_w - dmt_w * mmt_w), split2(f_all[:, sl])], axis=1),
                jnp.concatenate([ind_w, ind_w, ind_p, ind_p], axis=0))
            ddt_acc = ddt_acc + _dot(split2(dxdt * xv[:, sl]), jnp.concatenate([ind_p, ind_p], axis=0))
            tot2 = halves(tot_row[:, sl])
            tot_acc = jnp.where(lane1 == k0, tot2[0], jnp.where(lane1 == k1, tot2[1], tot_acc))
            dsk2 = halves(dsk_row[:, sl])
            dsk_acc = jnp.where(lane1 == k0, dsk2[0], jnp.where(lane1 == k1, dsk2[1], dsk_acc))
            dx_ref[:, sl] = dxdt * q["dt_x"][:, sl] + q["dsk_x"][:, sl] * dyp
        dcs_acc = dcs_acc + jnp.where(last_row, tot_acc, 0.0)
        dc_ref[...] = dcg + _dot(dcb.astype(BF16), bg_b)
        db_ref[...] = dbg + _dot(dcbt.astype(BF16), cg_b)
        dda = _cumsum_mm((q["ci"] >= q["ri"]).astype(BF16), dcs_acc)
        ddt = ddt_acc + dda * a
        ddtr = ddt * _sig(dtr_ref[...] + par_ref[0:1, :])
        ddtr_ref[...] = ddtr.astype(BF16)
        acc_ref[0:1, :] += jnp.sum(ddtr, axis=0, keepdims=True)
        acc_ref[1:2, :] += jnp.sum(dda * dt, axis=0, keepdims=True) * a
        acc_ref[2:3, :] += dsk_acc

    def rev(g, c):
        return (nc - 1 - c, g)

    return pl.pallas_call(
        body, name=name, grid=(SSD_GROUPS, nc),
        in_specs=[pl.BlockSpec((T, gw), rev),
                  pl.BlockSpec((T, SSD_STATE), rev),
                  pl.BlockSpec((T, SSD_STATE), rev),
                  pl.BlockSpec((T, LANE), lambda g, c: (nc - 1 - c, g + dtoff)),
                  pl.BlockSpec((8, LANE), lambda g, c: (0, g)),
                  pl.BlockSpec((1, 1, SSD_STATE, gw), lambda g, c: (g, nc - 1 - c, 0, 0)),
                  pl.BlockSpec((T, gw), rev)] + extra_specs,
        out_specs=(pl.BlockSpec((T, gw), rev),
                   pl.BlockSpec((T, SSD_STATE), rev),
                   pl.BlockSpec((T, SSD_STATE), rev),
                   pl.BlockSpec((T, LANE), lambda g, c: (nc - 1 - c, g + ddoff)),
                   pl.BlockSpec((8, LANE), lambda g, c: (0, g))),
        out_shape=(jax.ShapeDtypeStruct((n_rows, SSD_D_INNER), F32),
                   jax.ShapeDtypeStruct((n_rows, BC_DIM), F32),
                   jax.ShapeDtypeStruct((n_rows, BC_DIM), F32),
                   jax.ShapeDtypeStruct((n_rows, width), BF16),
                   jax.ShapeDtypeStruct((8, DT_PAD), F32)),
        input_output_aliases=aliases,
        scratch_shapes=[pltpu.VMEM((SSD_STATE, gw), F32)],
        compiler_params=_cparams("parallel", "arbitrary"),
    )(xs, bm, cm, dtr, par, hs, dy, *extra)


ADAM_ROWS = 256


def _adamw(lands, w, m, v, name):
    na = len(lands)
    n_slots, r, wd = lands[0].shape
    tr = r if r <= 2 * ADAM_ROWS else ADAM_ROWS
    nj = r // tr
    bc1 = 1.0 - ADAM_B1 ** ADAM_STEP
    bc2 = 1.0 - ADAM_B2 ** ADAM_STEP

    def body(*refs):
        l_refs = refs[:na]
        w_ref, m_ref, v_ref, g_ref, d_ref, nm_ref, nv_ref = refs[na:]
        for a in range(na):
            @pl.when(pl.program_id(0) == a)
            def _(l_ref=l_refs[a]):
                g = l_ref[0].astype(F32)
                for s in range(1, n_slots):
                    g = g + l_ref[s].astype(F32)
                mn = ADAM_B1 * m_ref[0] + (1.0 - ADAM_B1) * g
                vn = ADAM_B2 * v_ref[0] + (1.0 - ADAM_B2) * (g * g)
                mh = mn / bc1
                vh = vn / bc2
                g_ref[0] = g
                nm_ref[0] = mn
                nv_ref[0] = vn
                d_ref[0] = -ADAM_LR * (mh / (jnp.sqrt(vh) + ADAM_EPS) + ADAM_WD * w_ref[0])

    def land_spec(a):
        return pl.BlockSpec((n_slots, tr, wd),
                            lambda i, j: (0, jnp.where(i == a, j, jnp.where(i < a, 0, nj - 1)), 0))

    blk = pl.BlockSpec((1, tr, wd), lambda i, j: (i, j, 0))
    shp = jax.ShapeDtypeStruct((na, r, wd), F32)
    return pl.pallas_call(
        body, name=name, grid=(na, nj), in_specs=[land_spec(a) for a in range(na)] + [blk, blk, blk],
        out_specs=(blk, blk, blk, blk), out_shape=(shp, shp, shp, shp),
        compiler_params=_cparams("arbitrary", "arbitrary"),
    )(*lands, w, m, v)


def _mesh_pos():
    return lax.axis_index("x"), lax.axis_index("y"), lax.axis_index("c")


def _peer(pos, k):
    x, y, c = pos
    px = 1 - x if (k >> 2) & 1 else x
    py = 1 - y if (k >> 1) & 1 else y
    pc = 1 - c if k & 1 else c
    return px, py, pc


def _flat(pos):
    return 4 * pos[0] + 2 * pos[1] + pos[2]


HBM_SPEC = pl.BlockSpec(memory_space=pl.ANY)


ROW_SHARDED = ("w_ssd_out", "w_att_out", "w_mix_out", "w_ffn_down")
COL_SHARDED = ("w_in", "w_ffn_gate", "w_ffn_up")
GATHERED = ROW_SHARDED + COL_SHARDED + ("conv_w",)


SEM_SPEC = pl.BlockSpec(memory_space=pltpu.SEMAPHORE)
TOKEN = jax.ShapeDtypeStruct((8, LANE), F32)
SPLIT_EFFECT = pltpu.SideEffectType.DATAFLOW_SIDE_EFFECTING
GATHER_ROWS = "gather_rows"
GATHER_SLOT = "gather_slot"
SCATTER_ROWS = "scatter_rows"
SCATTER_SLOT = "scatter_slot"


def _land_shape(kind, src):
    if kind == GATHER_ROWS:
        return (N_DEV * src.shape[0],) + src.shape[1:]
    if kind == GATHER_SLOT:
        return (N_DEV,) + src.shape
    if kind == SCATTER_ROWS:
        return (N_DEV, src.shape[0] // N_DEV) + src.shape[1:]
    return src.shape


def _views(kind, src_ref, land_ref, pos, k):
    me = _flat(pos)
    if kind == GATHER_ROWS:
        r = src_ref.shape[0]
        return src_ref, land_ref.at[pl.ds(pl.multiple_of(me * r, 16), r), :]
    if kind == GATHER_SLOT:
        return src_ref, land_ref.at[me]
    dev = _flat(_peer(pos, k))
    if kind == SCATTER_ROWS:
        r = land_ref.shape[1]
        return src_ref.at[pl.ds(pl.multiple_of(dev * r, 16), r), :], land_ref.at[k]
    return src_ref.at[dev], land_ref.at[k]


def _hbm(x):
    return pltpu.with_memory_space_constraint(x, pltpu.HBM)


def _exchange_start(items, after, name):
    kinds = [k for k, _ in items]
    srcs = [_hbm(s) for _, s in items]
    lands = [_hbm(lax.empty(_land_shape(k, s), s.dtype)) for k, s in items]
    n = len(items)
    n_copy = n * (N_DEV - 1)

    def body(*refs):
        src_refs, land_refs = refs[:n], refs[n:2 * n]
        send_sems, recv_sems = refs[2 * n + 1], refs[2 * n + 2]
        token_ref = refs[4 * n + 3]
        pos = _mesh_pos()
        for i, kind in enumerate(kinds):
            for k in range(1, N_DEV):
                s, d = _views(kind, src_refs[i], land_refs[i], pos, k)
                j = i * (N_DEV - 1) + k - 1
                pltpu.make_async_remote_copy(src_ref=s, dst_ref=d, send_sem=send_sems.at[j], recv_sem=recv_sems.at[j],
                                             device_id=_peer(pos, k), device_id_type=MESH_ID).start()
        token_ref[...] = jnp.zeros_like(token_ref)

    arrs = srcs + lands
    outs = pl.pallas_call(
        body, name=name,
        in_specs=[HBM_SPEC] * (2 * n + 1),
        out_specs=[SEM_SPEC, SEM_SPEC] + [HBM_SPEC] * (2 * n) + [pl.BlockSpec(memory_space=pltpu.VMEM)],
        out_shape=[pltpu.SemaphoreType.DMA((n_copy,)), pltpu.SemaphoreType.DMA((n_copy,))]
        + [pltpu.HBM(a.shape, a.dtype) for a in arrs] + [TOKEN],
        input_output_aliases={i: 2 + i for i in range(2 * n)},
        compiler_params=pltpu.CompilerParams(has_side_effects=SPLIT_EFFECT),
    )(*arrs, after)
    return {"kinds": kinds, "send": outs[0], "recv": outs[1], "arrs": outs[2:2 + 2 * n], "token": outs[-1]}


def _exchange_wait(ex, after, name):
    kinds = ex["kinds"]
    n = len(kinds)

    def body(*refs):
        src_refs, land_refs = refs[:n], refs[n:2 * n]
        send_sems, recv_sems = refs[2 * n], refs[2 * n + 1]
        token_ref = refs[-1]
        pos = _mesh_pos()
        for i, kind in enumerate(kinds):
            for k in range(1, N_DEV):
                s, d = _views(kind, src_refs[i], land_refs[i], pos, k)
                j = i * (N_DEV - 1) + k - 1
                cp = pltpu.make_async_remote_copy(src_ref=s, dst_ref=d, send_sem=send_sems.at[j],
                                                  recv_sem=recv_sems.at[j], device_id=_peer(pos, k),
                                                  device_id_type=MESH_ID)
                cp.wait_send()
                cp.wait_recv()
        token_ref[...] = jnp.zeros_like(token_ref)

    outs = pl.pallas_call(
        body, name=name,
        in_specs=[HBM_SPEC] * (2 * n) + [SEM_SPEC, SEM_SPEC, HBM_SPEC],
        out_specs=[HBM_SPEC] * (2 * n) + [pl.BlockSpec(memory_space=pltpu.VMEM)],
        out_shape=[pltpu.HBM(a.shape, a.dtype) for a in ex["arrs"]] + [TOKEN],
        input_output_aliases={i: i for i in range(2 * n)},
        compiler_params=pltpu.CompilerParams(has_side_effects=SPLIT_EFFECT),
    )(*ex["arrs"], ex["send"], ex["recv"], after)
    lands = [_place_own(k, s, d) for k, s, d in zip(kinds, outs[:n], outs[n:2 * n])]
    return lands, outs[-1]


def _place_own(kind, src, land):
    me = _flat(_mesh_pos())
    zeros = (0,) * (src.ndim - 1)
    if kind == GATHER_ROWS:
        return lax.dynamic_update_slice(land, src, (me * src.shape[0],) + zeros)
    if kind == GATHER_SLOT:
        return lax.dynamic_update_slice(land, src[None], (me,) + (0,) * src.ndim)
    if kind == SCATTER_ROWS:
        r = land.shape[1]
        own = lax.dynamic_slice(src, (me * r,) + zeros, (r,) + src.shape[1:])
    else:
        own = lax.dynamic_index_in_dim(src, me, 0, keepdims=False)
    return lax.dynamic_update_slice(land, own[None], (0,) * land.ndim)


def _all_gather_small(x, name):
    r, w = x.shape

    def body(x_ref, out_ref, send_sems, recv_sems):
        pos = _mesh_pos()
        me = _flat(pos)
        copies = []
        for k in range(1, N_DEV):
            cp = pltpu.make_async_remote_copy(
                src_ref=x_ref, dst_ref=out_ref.at[me], send_sem=send_sems.at[k - 1], recv_sem=recv_sems.at[k - 1],
                device_id=_peer(pos, k), device_id_type=MESH_ID)
            cp.start()
            copies.append(cp)
        out_ref[me] = x_ref[...]
        for cp in copies:
            cp.wait()

    vmem = pl.BlockSpec(memory_space=pltpu.VMEM)
    return pl.pallas_call(
        body, name=name, in_specs=[vmem], out_specs=vmem,
        out_shape=jax.ShapeDtypeStruct((N_DEV, r, w), x.dtype),
        scratch_shapes=[pltpu.SemaphoreType.DMA((N_DEV - 1,)), pltpu.SemaphoreType.DMA((N_DEV - 1,))],
        compiler_params=pltpu.CompilerParams(has_side_effects=True),
    )(x)


def _cols(g, lo, hi):
    c = g.shape[-1]
    parts = []
    for d in range(N_DEV):
        a, b = max(lo, d * c), min(hi, (d + 1) * c)
        if a < b:
            parts.append(g[d, :, a - d * c:b - d * c])
    return parts[0] if len(parts) == 1 else jnp.concatenate(parts, axis=1)


def _col_chunks(g):
    c = g.shape[-1] // N_DEV
    return jnp.stack([g[:, d * c:(d + 1) * c] for d in range(N_DEV)])


IN_PART = ("w_in", "conv_w")
OUT_PART = ROW_SHARDED + ("w_ffn_gate", "w_ffn_up")


def _gather_items(w, names, l):
    items = []
    for n in names:
        blk = w[n][l] if n == "conv_w" else w[n][l].astype(BF16)
        items.append((GATHER_ROWS if n in ROW_SHARDED else GATHER_SLOT, blk))
    return items


def _scatter_items(grads, names):
    def chunked(g):
        return g if g.ndim == 3 else _col_chunks(g)

    return [(SCATTER_ROWS, grads[n]) if n in ROW_SHARDED else (SCATTER_SLOT, chunked(grads[n])) for n in names]


SMALL = ("ln_in_g", "ln_in_b", "conv_b", "dt_bias", "a_log", "d_skip", "ssd_norm_w", "att_sinks",
         "ln_mix_g", "ln_mix_b", "ln_ffn_g", "ln_ffn_b")


def _pack_small(vals):
    flat = jnp.concatenate([vals[n].reshape(-1) for n in SMALL])
    n = flat.shape[0]
    rows = -(-n // LANE)
    rows = -(-rows // 8) * 8
    return jnp.pad(flat, (0, rows * LANE - n)).reshape(rows, LANE)


def _unpack_small(buf, shapes):
    flat = buf.reshape(-1)
    off = 0
    out = {}
    for n in SMALL:
        cnt = math.prod(shapes[n])
        out[n] = flat[off:off + cnt].reshape(shapes[n])
        off += cnt
    return out


def _to_group_major(v):
    lead = v.shape[:-1]
    t = v.reshape(lead + (SSD_GROUPS, HEADS_PER_GROUP))
    t = jnp.pad(t, [(0, 0)] * len(lead) + [(0, 0), (0, LANE - HEADS_PER_GROUP)])
    return t.reshape(lead + (DT_PAD,))


def _from_group_major(v):
    lead = v.shape[:-1]
    return v.reshape(lead + (SSD_GROUPS, LANE))[..., :HEADS_PER_GROUP].reshape(lead + (SSD_HEADS,))


def _rows8(v):
    return jnp.pad(v, ((0, 8 - v.shape[0]), (0, 0)))


IN_OFFS = {"q": (0, 1024), "kv": (1024, 1280), "z": (1280, 3328), "xs": (3328, 5376), "b": (5376, 5888),
           "c": (5888, 6400), "dt": (6400, 6432), "gl": (6432, 8480)}
PIECES = ("q", "kv", "z", "xs", "b", "c", "dt", "gl")


CAT = ("z", "xs", "gl", "q", "b", "c", "dt", "kv")
CAT_WIDTH = {"q": 1024, "z": 2048, "xs": 2048, "gl": 2048, "b": 512, "c": 512, "kv": 256, "dt": DT_PAD}
CAT_OFF = {p: sum(CAT_WIDTH[q] for q in CAT[:i]) for i, p in enumerate(CAT)}
CAT_DIM = sum(CAT_WIDTH.values())
MAIN_DIM = CAT_OFF["kv"]


def _cat_w_in(g):
    pieces = {p: _cols(g, lo, hi) for p, (lo, hi) in IN_OFFS.items()}
    pieces["dt"] = _to_group_major(pieces["dt"])
    return jnp.concatenate([pieces[p] for p in CAT], axis=1)


def _dw_in_chunks(dw_main, dw_kv):
    dt = _from_group_major(dw_main[:, CAT_OFF["dt"]:CAT_OFF["dt"] + DT_PAD])
    shard = IN_OFFS[PIECES[-1]][1] // N_DEV

    def piece(pc, a, b):
        if pc == "dt":
            return dt[:, a:b]
        if pc == "kv":
            return dw_kv[:, a:b]
        return dw_main[:, CAT_OFF[pc] + a:CAT_OFF[pc] + b]

    chunks = []
    for d in range(N_DEV):
        parts = []
        for pc in PIECES:
            lo, hi = IN_OFFS[pc]
            a, b = max(lo, d * shard), min(hi, (d + 1) * shard)
            if a < b:
                parts.append(piece(pc, a - lo, b - lo))
        chunks.append(parts[0] if len(parts) == 1 else jnp.concatenate(parts, axis=1))
    return jnp.stack(chunks)


def _params_out(W):
    p = {n: W[n] for n in ROW_SHARDED}
    for n in ("w_ffn_gate", "w_ffn_up"):
        p[n] = _cols(W[n], 0, FFN_HIDDEN)
    return p


def _params_in(l, W, sm):
    p = {"w_cat": _cat_w_in(W["w_in"])}
    cw = _cols(W["conv_w"], 0, SSD_D_INNER + 2 * BC_DIM)
    cb = sm["conv_b"][l]
    segs = {"xs": (0, 2048), "b": (2048, 2560), "c": (2560, 3072)}
    p["conv_w8"] = {s: _rows8(cw[:, lo:hi]) for s, (lo, hi) in segs.items()}
    p["conv_b8"] = {s: _rows8(cb[None, lo:hi]) for s, (lo, hi) in segs.items()}
    p["ssd_par"] = _rows8(jnp.stack([_to_group_major(sm["dt_bias"][l]), _to_group_major(sm["a_log"][l]),
                                     _to_group_major(sm["d_skip"][l])]))
    p["norm_w"] = sm["ssd_norm_w"][l]
    p["sinks8"] = _rows8(jnp.pad(sm["att_sinks"][l], (0, LANE - ATT_HEADS))[None])
    for n in ("ln_mix_g", "ln_mix_b", "ln_ffn_g", "ln_ffn_b"):
        p[n] = sm[n][l]
    return p


def _fwd_mixers(h0, p, l, dep=None):
    tag = f"l{l}_"
    a = {"h0": h0}
    proj = _mm(h0, p["w_cat"], "nn", tag + "proj", dep=dep)
    for pc in CAT:
        a[pc] = (proj, CAT_OFF[pc], CAT_WIDTH[pc])
    for s in ("xs", "b", "c"):
        a[s + "c"] = _conv_fwd(a[s], p["conv_w8"][s], p["conv_b8"][s], tag + "conv_" + s)
    a["y"], a["hs"] = _ssd_fwd(a["xsc"], a["bc"], a["cc"], a["dt"], p["ssd_par"], tag + "ssd_fwd")
    a["yn"] = _gnorm_fwd(a["y"], a["z"], p["norm_w"], tag + "gnorm")
    a["att"], a["att32"] = _att_fwd(a["q"], a["kv"], p["sinks8"], tag + "att_fwd")
    return a


def _fwd_out(a, p, l, dep=None):
    tag = f"l{l}_"
    h0 = a["h0"]
    a["ya"] = _mm(a["yn"], p["w_ssd_out"], "nn", tag + "ssd_out", dep=dep)
    a["yb"] = _mm(a["att"], p["w_att_out"], "nn", tag + "att_out", dep=dep)
    a["merged"] = _merge_fwd(a["gl"], a["ya"], a["yb"], tag + "merge")
    a["mix"] = _mm(a["merged"], p["w_mix_out"], "nn", tag + "mix_out")
    a["h1"] = _ln_fwd(h0, a["mix"], p["ln_mix_g"], p["ln_mix_b"], ALPHA, tag + "ln_mix")
    a["fg"], a["fu"], a["act"] = _ffn_in(a["h1"], p["w_ffn_gate"], p["w_ffn_up"], tag + "ffn_in")
    a["ffn"] = _mm(a["act"], p["w_ffn_down"], "nn", tag + "ffn_down")
    a["h2"] = _ln_fwd(a["h1"], a["ffn"], p["ln_ffn_g"], p["ln_ffn_b"], ALPHA, tag + "ln_ffn")
    return a


def _dw(x, dy, name, dep=None):
    return _mm(x, dy, "tn", name, out_dtype=BF16, dep=dep)


def _bwd_out(a, p, dh2, l, dep=None):
    tag = f"l{l}_b_"
    gw, gs = {}, {}
    du2, acc = _ln_bwd(a["h1"], a["ffn"], p["ln_ffn_g"], dh2, ALPHA, tag + "ln_ffn")
    gs["ln_ffn_g"], gs["ln_ffn_b"] = acc[0], acc[1]
    gw["w_ffn_down"] = _dw(a["act"], du2, tag + "dw_down", dep=dep)
    dfg, dfu = _ffn_dact(du2, p["w_ffn_down"], a["fg"], a["fu"], tag + "ffn_dact", dep=dep)
    gw["w_ffn_gate"] = _dw(a["h1"], dfg, tag + "dw_gate")
    gw["w_ffn_up"] = _dw(a["h1"], dfu, tag + "dw_up")
    dh1 = _mm(dfg, p["w_ffn_gate"], "nt", tag + "dh1_gate", add=du2, add_scale=ALPHA)
    dh1 = _mm(dfu, p["w_ffn_up"], "nt", tag + "dh1_up", add=dh1)
    du1, acc = _ln_bwd(a["h0"], a["mix"], p["ln_mix_g"], dh1, ALPHA, tag + "ln_mix")
    gs["ln_mix_g"], gs["ln_mix_b"] = acc[0], acc[1]
    gw["w_mix_out"] = _dw(a["merged"], du1, tag + "dw_mix")
    dmerged = _mm(du1, p["w_mix_out"], "nt", tag + "dmerged")
    dya, dyb, dproj = _merge_bwd(a["gl"], a["ya"], a["yb"], dmerged, tag + "merge",
                                 (None, CAT_OFF["gl"], MAIN_DIM))
    gw["w_ssd_out"] = _dw(a["yn"], dya, tag + "dw_ssd")
    gw["w_att_out"] = _dw(a["att"], dyb, tag + "dw_att")
    return {"du1": du1, "dya": dya, "dyb": dyb, "dproj": dproj}, gw, gs


def _bwd_mixers(a, p, carry, l, dep=None):
    tag = f"l{l}_b_"
    gs = {}
    du1, dproj = carry["du1"], carry["dproj"]

    def win(pc):
        return (dproj, CAT_OFF[pc], MAIN_DIM)

    dyn = _mm(carry["dya"], p["w_ssd_out"], "nt", tag + "dyn", dep=dep)
    datt = _mm(carry["dyb"], p["w_att_out"], "nt", tag + "datt", out_dtype=BF16, dep=dep)
    dproj, dkv, acc = _att_bwd(a["q"], a["kv"], p["sinks8"], a["att32"], datt, tag + "att", win("q"))
    gs["att_sinks"] = acc[0, :ATT_HEADS]
    dy, dproj, acc = _gnorm_bwd(a["y"], a["z"], p["norm_w"], dyn, tag + "gnorm", win("z"))
    gs["ssd_norm_w"] = acc[0]
    dxs, dbm, dcm, dproj, acc = _ssd_bwd(a["xsc"], a["bc"], a["cc"], a["dt"], p["ssd_par"], a["hs"], dy,
                                         tag + "ssd", win("dt"))
    gs["dt_bias"], gs["a_log"], gs["d_skip"] = (_from_group_major(acc[i]) for i in range(3))
    dconv_w, dconv_b = [], []
    for s, dout in (("xs", dxs), ("b", dbm), ("c", dcm)):
        dc, acc = _conv_bwd_pre(a[s], p["conv_w8"][s], p["conv_b8"][s], dout, tag + "conv_pre_" + s)
        dconv_w.append(acc[:CONV_TAPS])
        dconv_b.append(acc[CONV_TAPS])
        dproj = _conv_bwd_in(dc, p["conv_w8"][s], tag + "conv_in_" + s, win(s))
    gconv = jnp.concatenate(dconv_w, axis=1)
    gs["conv_b"] = jnp.concatenate(dconv_b)
    w_main, w_kv = p["w_cat"][:, :MAIN_DIM], p["w_cat"][:, MAIN_DIM:]
    dw_main, dw_kv = _dw(a["h0"], dproj, tag + "dw_in"), _dw(a["h0"], dkv, tag + "dw_in_kv")

    def grad_h0(dep=None):
        dh0 = _mm(dproj, w_main, "nt", tag + "dh0", add=du1, add_scale=ALPHA, dep=dep)
        return _mm(dkv, w_kv, "nt", tag + "dh0_kv", add=dh0)

    return grad_h0, _dw_in_chunks(dw_main, dw_kv), gconv, gs


def _step(x, target, w, m, v):
    x2 = x[0]
    t2 = target[0]
    tok = jnp.zeros(TOKEN.shape, TOKEN.dtype)

    ex = _exchange_start(_gather_items(w, IN_PART, 0), tok, "gather_l0_in_start")
    h = _ln_fwd(x2, None, w["ln_in_g"], w["ln_in_b"], 1.0, "ln_in")
    lands, tok = _exchange_wait(ex, h, "gather_l0_in_wait")
    p0 = _params_in(0, dict(zip(IN_PART, lands)), w)
    ex = _exchange_start(_gather_items(w, OUT_PART, 0) + _gather_items(w, IN_PART, 1), tok,
                         "gather_l0_out_l1_in_start")
    a0 = _fwd_mixers(h, p0, 0, dep=ex["token"])
    lands, tok = _exchange_wait(ex, a0["att"], "gather_l0_out_l1_in_wait")
    p0.update(_params_out(dict(zip(OUT_PART, lands))))
    p1 = _params_in(1, dict(zip(IN_PART, lands[len(OUT_PART):])), w)
    ex = _exchange_start(_gather_items(w, OUT_PART, 1), tok, "gather_l1_out_start")
    a0 = _fwd_out(a0, p0, 0, dep=ex["token"])
    lands, tok = _exchange_wait(ex, a0["h2"], "gather_l1_out_wait")
    p1.update(_params_out(dict(zip(OUT_PART, lands))))
    a1 = _fwd_out(_fwd_mixers(a0["h2"], p1, 1), p1, 1)

    sse, dh = _loss_fwd_bwd(a1["h2"], t2, "loss")
    loss = lax.psum(0.5 / D_MODEL * sse[0, 0], ("x", "y", "c"))

    carry, gw1, gs1 = _bwd_out(a1, p1, dh, 1)
    grad_h0, gw1["w_in"], gw1["conv_w"], gs = _bwd_mixers(a1, p1, carry, 1)
    dh = grad_h0()
    gs1.update(gs)
    ex1 = _exchange_start(_scatter_items(gw1, GATHERED), tok, "scatter_l1_start")
    carry, gw0, gs0 = _bwd_out(a0, p0, dh, 0, dep=ex1["token"])
    lands, tok = _exchange_wait(ex1, carry["dyb"], "scatter_l1_wait")
    land1 = dict(zip(GATHERED, lands))
    ex0 = _exchange_start(_scatter_items(gw0, OUT_PART), tok, "scatter_l0_out_start")
    grad_h0, gw0["w_in"], gw0["conv_w"], gs = _bwd_mixers(a0, p0, carry, 0, dep=ex0["token"])
    gs0.update(gs)
    lands, tok = _exchange_wait(ex0, gw0["w_in"], "scatter_l0_out_wait")
    land0 = dict(zip(OUT_PART, lands))
    ex0 = _exchange_start(_scatter_items(gw0, IN_PART), tok, "scatter_l0_in_start")
    dh = grad_h0(dep=ex0["token"])
    grad_x2, acc = _ln_bwd(x2, None, w["ln_in_g"], dh, 1.0, "ln_in_b")

    outs = [{} for _ in range(4)]

    def update(names):
        res = None
        for n in names:
            res = _adamw([land0[n], land1[n]], w[n], m[n], v[n], "adamw_" + n)
            for o, t in zip(outs, res):
                o[n] = t
        return res[1]

    update(OUT_PART)
    gsm = {"ln_in_g": acc[0], "ln_in_b": acc[1]}
    for n in SMALL[2:]:
        gsm[n] = jnp.stack([gs0[n], gs1[n]])
    small_shapes = {n: w[n].shape for n in SMALL}
    land_s = _all_gather_small(_pack_small(gsm), "small_grads_all_gather")
    res = _adamw([land_s], _pack_small(w)[None], _pack_small(m)[None], _pack_small(v)[None], "adamw_small")
    for o, t in zip(outs, res):
        o.update(_unpack_small(t[0], small_shapes))
    lands, _ = _exchange_wait(ex0, res[1], "scatter_l0_in_wait")
    land0.update(zip(IN_PART, lands))
    update(IN_PART)
    return loss, grad_x2[None], outs


WEIGHT_NAMES = ("ln_in_g", "ln_in_b", "w_in", "conv_w", "conv_b", "dt_bias", "a_log", "d_skip", "ssd_norm_w",
                "att_sinks", "w_ssd_out", "w_att_out", "w_mix_out", "ln_mix_g", "ln_mix_b", "w_ffn_gate",
                "w_ffn_up", "w_ffn_down", "ln_ffn_g", "ln_ffn_b")


def kernel(x, ln_in_g, ln_in_b, w_in, conv_w, conv_b, dt_bias, a_log, d_skip, ssd_norm_w, att_sinks, w_ssd_out, w_att_out, w_mix_out, ln_mix_g, ln_mix_b, w_ffn_gate, w_ffn_up, w_ffn_down, ln_ffn_g, ln_ffn_b, loss_target, m_ln_in_g, m_ln_in_b, m_w_in, m_conv_w, m_conv_b, m_dt_bias, m_a_log, m_d_skip, m_ssd_norm_w, m_att_sinks, m_w_ssd_out, m_w_att_out, m_w_mix_out, m_ln_mix_g, m_ln_mix_b, m_w_ffn_gate, m_w_ffn_up, m_w_ffn_down, m_ln_ffn_g, m_ln_ffn_b, v_ln_in_g, v_ln_in_b, v_w_in, v_conv_w, v_conv_b, v_dt_bias, v_a_log, v_d_skip, v_ssd_norm_w, v_att_sinks, v_w_ssd_out, v_w_att_out, v_w_mix_out, v_ln_mix_g, v_ln_mix_b, v_w_ffn_gate, v_w_ffn_up, v_w_ffn_down, v_ln_ffn_g, v_ln_ffn_b):
    w = dict(zip(WEIGHT_NAMES, (ln_in_g, ln_in_b, w_in, conv_w, conv_b, dt_bias, a_log, d_skip, ssd_norm_w,
                                att_sinks, w_ssd_out, w_att_out, w_mix_out, ln_mix_g, ln_mix_b, w_ffn_gate,
                                w_ffn_up, w_ffn_down, ln_ffn_g, ln_ffn_b)))
    m = dict(zip(WEIGHT_NAMES, (m_ln_in_g, m_ln_in_b, m_w_in, m_conv_w, m_conv_b, m_dt_bias, m_a_log, m_d_skip,
                                m_ssd_norm_w, m_att_sinks, m_w_ssd_out, m_w_att_out, m_w_mix_out, m_ln_mix_g,
                                m_ln_mix_b, m_w_ffn_gate, m_w_ffn_up, m_w_ffn_down, m_ln_ffn_g, m_ln_ffn_b)))
    v = dict(zip(WEIGHT_NAMES, (v_ln_in_g, v_ln_in_b, v_w_in, v_conv_w, v_conv_b, v_dt_bias, v_a_log, v_d_skip,
                                v_ssd_norm_w, v_att_sinks, v_w_ssd_out, v_w_att_out, v_w_mix_out, v_ln_mix_g,
                                v_ln_mix_b, v_w_ffn_gate, v_w_ffn_up, v_w_ffn_down, v_ln_ffn_g, v_ln_ffn_b)))
    loss, grad_x, outs = _step(x, loss_target, w, m, v)
    result = [loss, grad_x]
    for o in outs:
        result.extend(o[n] for n in WEIGHT_NAMES)
    return tuple(result)
```

```python
import math

import jax
import jax.numpy as jnp
from jax import lax
from jax.experimental import pallas as pl
from jax.experimental.pallas import tpu as pltpu

F32 = jnp.float32
BF16 = jnp.bfloat16

D_MODEL = 1024
DEPTH = 2
N_DEV = 8
ATT_HEADS = 16
ATT_KV_HEADS = 2
ATT_HEAD_DIM = 64
ATT_BLOCK = 128
SSD_D_INNER = 2048
SSD_HEADS = 32
SSD_GROUPS = 4
SSD_STATE = 128
SSD_CHUNK = 128
FFN_HIDDEN = 2816
LN_EPS = 1e-5
RMS_EPS = 1e-5
ALPHA = (2 * DEPTH) ** 0.25
Q_DIM = 1024
BC_DIM = 512
DT_PAD = 512

ADAM_LR = 0.001
ADAM_B1 = 0.9
ADAM_B2 = 0.999
ADAM_EPS = 1e-08
ADAM_WD = 0.01
ADAM_STEP = 10

LANE = 128
VMEM_LIMIT = 48 * 1024 * 1024
NEG = -1e30

_NN = (((1,), (0,)), ((), ()))
_NT = (((1,), (1,)), ((), ()))
_TN = (((0,), (0,)), ((), ()))
MESH_ID = pl.DeviceIdType.MESH


def _dot(a, b, dims=_NN):
    return lax.dot_general(a, b, dims, preferred_element_type=F32)


def _sig(x):
    return 1.0 / (1.0 + jnp.exp(-x))


def _softplus(x):
    return jnp.maximum(x, 0.0) + jnp.log(1.0 + jnp.exp(-jnp.abs(x)))


def _cparams(*sem):
    return pltpu.CompilerParams(dimension_semantics=sem, vmem_limit_bytes=VMEM_LIMIT)


def _pick(n, cap):
    if n <= cap:
        return n
    best = None
    for t in range(LANE, cap + 1, LANE):
        if n % t == 0:
            best = t
    assert best is not None, (n, cap)
    return best


def _tile(n):
    if n <= 1024 or n % 1024 == 0:
        return min(n, 1024)
    return _pick(n, 1408)


def _rows(n):
    return min(512, n)


def _window(x):
    return x if isinstance(x, tuple) else (x, 0, x.shape[1])


def _into(into, n_in, out_idx):
    buf, col0, width = into
    if buf is None:
        return [], [], {}, col0, width
    return [buf], [pl.BlockSpec(memory_space=pl.ANY)], {n_in: out_idx}, col0, width


def _mm(a, b, mode, name, add=None, add_scale=1.0, out_dtype=F32, dep=None):
    if mode == "nn":
        m, k = a.shape
        n = b.shape[1]
    elif mode == "nt":
        m, k = a.shape
        n = b.shape[0]
    else:
        k, m = a.shape
        n = b.shape[1]
    tm = _tile(m)
    tn = _pick(n, 2176) if mode == "tn" and n > 1024 else _tile(n)
    tk = _pick(k, 2176) if mode == "nt" and a.dtype == BF16 and k > 2816 else _tile(k)
    nk = k // tk
    has_add = add is not None
    dims = {"nn": _NN, "nt": _NT, "tn": _TN}[mode]

    def body(*refs):
        if dep is not None:
            refs = refs[:-3] + refs[-2:]
        if has_add:
            a_ref, b_ref, add_ref, o_ref, acc_ref = refs
        else:
            a_ref, b_ref, o_ref, acc_ref = refs
        kk = pl.program_id(2)

        @pl.when(kk == 0)
        def _():
            if has_add:
                acc_ref[...] = add_scale * add_ref[...].astype(F32)
            else:
                acc_ref[...] = jnp.zeros_like(acc_ref)

        acc_ref[...] += _dot(a_ref[...].astype(BF16), b_ref[...].astype(BF16), dims)

        @pl.when(kk == nk - 1)
        def _():
            o_ref[...] = acc_ref[...].astype(o_ref.dtype)

    if mode == "nn":
        a_spec = pl.BlockSpec((tm, tk), lambda i, j, kk: (i, kk))
        b_spec = pl.BlockSpec((tk, tn), lambda i, j, kk: (kk, j))
    elif mode == "nt":
        a_spec = pl.BlockSpec((tm, tk), lambda i, j, kk: (i, kk))
        b_spec = pl.BlockSpec((tn, tk), lambda i, j, kk: (j, kk))
    else:
        a_spec = pl.BlockSpec((tk, tm), lambda i, j, kk: (kk, i))
        b_spec = pl.BlockSpec((tk, tn), lambda i, j, kk: (kk, j))
    o_spec = pl.BlockSpec((tm, tn), lambda i, j, kk: (i, j))
    in_specs = [a_spec, b_spec] + ([o_spec] if has_add else [])
    args = (a, b) + ((add,) if has_add else ())
    if dep is not None:
        in_specs.append(pl.BlockSpec((8, LANE), lambda i, j, kk: (0, 0)))
        args += (dep,)
    return pl.pallas_call(
        body, name=name, grid=(m // tm, n // tn, nk),
        in_specs=in_specs, out_specs=o_spec,
        out_shape=jax.ShapeDtypeStruct((m, n), out_dtype),
        scratch_shapes=[pltpu.VMEM((tm, tn), F32)],
        compiler_params=_cparams("parallel", "parallel", "arbitrary"),
    )(*args)


def _vec_spec(width):
    return pl.BlockSpec((1, width), lambda i: (0, 0))


def _ln_fwd(a, b, gamma, beta, alpha, name):
    n_rows, dm = a.shape
    has_b = b is not None

    def body(*refs):
        if has_b:
            a_ref, b_ref, g_ref, be_ref, o_ref = refs
            u = alpha * a_ref[...] + b_ref[...]
        else:
            a_ref, g_ref, be_ref, o_ref = refs
            u = a_ref[...]
        mu = jnp.mean(u, axis=-1, keepdims=True)
        d = u - mu
        var = jnp.mean(d * d, axis=-1, keepdims=True)
        o_ref[...] = d * lax.rsqrt(var + LN_EPS) * g_ref[...] + be_ref[...]

    row = pl.BlockSpec((_rows(n_rows),dm), lambda i: (i, 0))
    in_specs = [row] + ([row] if has_b else []) + [_vec_spec(dm), _vec_spec(dm)]
    args = (a,) + ((b,) if has_b else ()) + (gamma.reshape(1, dm), beta.reshape(1, dm))
    return pl.pallas_call(
        body, name=name, grid=(n_rows // _rows(n_rows),), in_specs=in_specs, out_specs=row,
        out_shape=jax.ShapeDtypeStruct((n_rows, dm), F32),
        compiler_params=_cparams("parallel"),
    )(*args)


def _ln_bwd(a, b, gamma, dy, alpha, name):
    n_rows, dm = a.shape
    has_b = b is not None

    def body(*refs):
        if has_b:
            a_ref, b_ref, g_ref, dy_ref, du_ref, acc_ref = refs
            u = alpha * a_ref[...] + b_ref[...]
        else:
            a_ref, g_ref, dy_ref, du_ref, acc_ref = refs
            u = a_ref[...]

        @pl.when(pl.program_id(0) == 0)
        def _():
            acc_ref[...] = jnp.zeros_like(acc_ref)

        mu = jnp.mean(u, axis=-1, keepdims=True)
        d = u - mu
        var = jnp.mean(d * d, axis=-1, keepdims=True)
        rstd = lax.rsqrt(var + LN_EPS)
        xhat = d * rstd
        dyv = dy_ref[...]
        acc_ref[0:1, :] += jnp.sum(dyv * xhat, axis=0, keepdims=True)
        acc_ref[1:2, :] += jnp.sum(dyv, axis=0, keepdims=True)
        dxh = dyv * g_ref[...]
        m1 = jnp.mean(dxh, axis=-1, keepdims=True)
        m2 = jnp.mean(dxh * xhat, axis=-1, keepdims=True)
        du_ref[...] = rstd * (dxh - m1 - xhat * m2)

    row = pl.BlockSpec((_rows(n_rows),dm), lambda i: (i, 0))
    in_specs = [row] + ([row] if has_b else []) + [_vec_spec(dm), row]
    args = (a,) + ((b,) if has_b else ()) + (gamma.reshape(1, dm), dy)
    return pl.pallas_call(
        body, name=name, grid=(n_rows // _rows(n_rows),), in_specs=in_specs,
        out_specs=(row, pl.BlockSpec((8, dm), lambda i: (0, 0))),
        out_shape=(jax.ShapeDtypeStruct((n_rows, dm), F32), jax.ShapeDtypeStruct((8, dm), F32)),
        compiler_params=_cparams("arbitrary"),
    )(*args)


def _loss_fwd_bwd(y, target, name):
    n_rows, dm = y.shape

    def body(y_ref, t_ref, acc_ref, dy_ref):
        @pl.when(pl.program_id(0) == 0)
        def _():
            acc_ref[...] = jnp.zeros_like(acc_ref)

        d = y_ref[...] - t_ref[...]
        acc_ref[...] += jnp.sum(d * d)
        dy_ref[...] = d * (1.0 / dm)

    row = pl.BlockSpec((_rows(n_rows),dm), lambda i: (i, 0))
    return pl.pallas_call(
        body, name=name, grid=(n_rows // _rows(n_rows),), in_specs=[row, row],
        out_specs=(pl.BlockSpec((8, LANE), lambda i: (0, 0)), row),
        out_shape=(jax.ShapeDtypeStruct((8, LANE), F32), jax.ShapeDtypeStruct((n_rows, dm), F32)),
        compiler_params=_cparams("arbitrary"),
    )(y, target)


FFN_ROWS = 512


def _ffn_in(h, wg, wu, name, dep=None):
    m, k = h.shape
    n = wg.shape[0]
    tm, tn = min(FFN_ROWS, m), _tile(n)

    def body(*refs):
        h_ref, wg_ref, wu_ref = refs[:3]
        g_ref, u_ref, act_ref = refs[-3:]
        hb = h_ref[...].astype(BF16)
        g = _dot(hb, wg_ref[...], _NT)
        u = _dot(hb, wu_ref[...], _NT)
        g_ref[...] = g
        u_ref[...] = u
        act_ref[...] = (g * _sig(g) * u).astype(BF16)

    rows = pl.BlockSpec((tm, k), lambda j, i: (i, 0))
    wrow = pl.BlockSpec((tn, k), lambda j, i: (j, 0))
    out = pl.BlockSpec((tm, tn), lambda j, i: (i, j))
    in_specs, args = [rows, wrow, wrow], (h, wg, wu)
    if dep is not None:
        in_specs.append(pl.BlockSpec((8, LANE), lambda j, i: (0, 0)))
        args += (dep,)
    return pl.pallas_call(
        body, name=name, grid=(n // tn, m // tm), in_specs=in_specs, out_specs=(out, out, out),
        out_shape=(jax.ShapeDtypeStruct((m, n), F32), jax.ShapeDtypeStruct((m, n), F32),
                   jax.ShapeDtypeStruct((m, n), BF16)),
        compiler_params=_cparams("parallel", "parallel"),
    )(*args)


def _ffn_dact(dy, wd, g, u, name, dep=None):
    m, k = dy.shape
    n = wd.shape[0]
    tm, tn = min(FFN_ROWS, m), _tile(n)

    def body(*refs):
        dy_ref, wd_ref, g_ref, u_ref = refs[:4]
        dg_ref, du_ref = refs[-2:]
        da = _dot(dy_ref[...].astype(BF16), wd_ref[...], _NT)
        gv = g_ref[...]
        s = _sig(gv)
        dg_ref[...] = (da * u_ref[...] * (s * (1.0 + gv * (1.0 - s)))).astype(BF16)
        du_ref[...] = (da * gv * s).astype(BF16)

    rows = pl.BlockSpec((tm, k), lambda j, i: (i, 0))
    wrow = pl.BlockSpec((tn, k), lambda j, i: (j, 0))
    out = pl.BlockSpec((tm, tn), lambda j, i: (i, j))
    in_specs, args = [rows, wrow, out, out], (dy, wd, g, u)
    if dep is not None:
        in_specs.append(pl.BlockSpec((8, LANE), lambda j, i: (0, 0)))
        args += (dep,)
    return pl.pallas_call(
        body, name=name, grid=(n // tn, m // tm), in_specs=in_specs, out_specs=(out, out),
        out_shape=(jax.ShapeDtypeStruct((m, n), BF16), jax.ShapeDtypeStruct((m, n), BF16)),
        compiler_params=_cparams("parallel", "parallel"),
    )(*args)


def _gate_specs(gl, n_rows, dm):
    arr, g0, _ = _window(gl)
    return arr, [pl.BlockSpec((_rows(n_rows), dm), lambda i, k=k: (i, g0 // dm + k)) for k in range(2)]


def _merge_fwd(gl, ya, yb, name):
    n_rows, dm = ya.shape
    gl_arr, gspecs = _gate_specs(gl, n_rows, dm)

    def body(ga_ref, gb_ref, ya_ref, yb_ref, o_ref):
        o_ref[...] = (_sig(ga_ref[...]) * ya_ref[...] + _sig(gb_ref[...]) * yb_ref[...]).astype(BF16)

    row = pl.BlockSpec((_rows(n_rows),dm), lambda i: (i, 0))
    return pl.pallas_call(
        body, name=name, grid=(n_rows // _rows(n_rows),), in_specs=gspecs + [row, row], out_specs=row,
        out_shape=jax.ShapeDtypeStruct((n_rows, dm), BF16),
        compiler_params=_cparams("parallel"),
    )(gl_arr, gl_arr, ya, yb)


def _merge_bwd(gl, ya, yb, dmerged, name, into):
    n_rows, dm = ya.shape
    gl_arr, gspecs = _gate_specs(gl, n_rows, dm)
    extra, extra_specs, aliases, col0, width = _into(into, 5, 2)

    def body(*refs):
        ga_ref, gb_ref, ya_ref, yb_ref, dm_ref = refs[:5]
        dya_ref, dyb_ref, dgl_ref = refs[-3:]
        ga = _sig(ga_ref[...])
        gb = _sig(gb_ref[...])
        dmv = dm_ref[...]
        dya_ref[...] = (dmv * ga).astype(BF16)
        dyb_ref[...] = (dmv * gb).astype(BF16)
        dgl_ref[:, :dm] = (dmv * ya_ref[...] * ga * (1.0 - ga)).astype(BF16)
        dgl_ref[:, dm:] = (dmv * yb_ref[...] * gb * (1.0 - gb)).astype(BF16)

    row = pl.BlockSpec((_rows(n_rows),dm), lambda i: (i, 0))
    row2 = pl.BlockSpec((_rows(n_rows),2 * dm), lambda i: (i, col0 // (2 * dm)))
    return pl.pallas_call(
        body, name=name, grid=(n_rows // _rows(n_rows),), in_specs=gspecs + [row, row, row] + extra_specs,
        out_specs=(row, row, row2),
        out_shape=(jax.ShapeDtypeStruct((n_rows, dm), BF16), jax.ShapeDtypeStruct((n_rows, dm), BF16),
                   jax.ShapeDtypeStruct((n_rows, width), BF16)),
        input_output_aliases=aliases,
        compiler_params=_cparams("parallel"),
    )(gl_arr, gl_arr, ya, yb, dmerged, *extra)


CONV_TAPS = 4
CONV_COLS = 512
HALO = 8


def _shift_down(cur, prev8, s, row8):
    r = pltpu.roll(cur, s, axis=0)
    top = jnp.where(row8 < s, pltpu.roll(prev8, s, axis=0), r[0:HALO])
    return jnp.concatenate([top, r[HALO:]], axis=0)


def _shift_up(cur, next8, s, row8):
    n = cur.shape[0]
    r = pltpu.roll(cur, n - s, axis=0)
    bot = jnp.where(row8 >= HALO - s, pltpu.roll(next8, HALO - s, axis=0), r[n - HALO:])
    return jnp.concatenate([r[:n - HALO], bot], axis=0)


def _conv_pre(u_ref, prev_ref, w_ref, b_ref, li):
    cur = u_ref[...]
    prev8 = jnp.where(li == 0, 0.0, prev_ref[...])
    row8 = lax.broadcasted_iota(jnp.int32, prev8.shape, 0)
    shifted = [cur] + [_shift_down(cur, prev8, s, row8) for s in range(1, CONV_TAPS)]
    acc = b_ref[...] + shifted[0] * w_ref[CONV_TAPS - 1:CONV_TAPS, :]
    for s in range(1, CONV_TAPS):
        acc = acc + shifted[s] * w_ref[CONV_TAPS - 1 - s:CONV_TAPS - s, :]
    return acc, shifted


def _conv_specs(n_rows, tl, col0=0):
    off = col0 // CONV_COLS
    cur = pl.BlockSpec((tl, CONV_COLS), lambda cj, li: (li, cj + off))
    prev = pl.BlockSpec((HALO, CONV_COLS), lambda cj, li: (jnp.maximum(li * (tl // HALO) - 1, 0), cj + off))
    nxt = pl.BlockSpec((HALO, CONV_COLS),
                       lambda cj, li: (jnp.minimum((li + 1) * (tl // HALO), n_rows // HALO - 1), cj + off))
    par = pl.BlockSpec((8, CONV_COLS), lambda cj, li: (0, cj + off))
    return cur, prev, nxt, par


def _conv_fwd(u, w8, b8, name):
    u, u0, c = _window(u)
    n_rows = u.shape[0]
    tl = _rows(n_rows)
    cur, _, _, par = _conv_specs(n_rows, tl)
    ucur, prev, _, _ = _conv_specs(n_rows, tl, u0)

    def body(u_ref, prev_ref, w_ref, b_ref, o_ref):
        acc, _ = _conv_pre(u_ref, prev_ref, w_ref, b_ref[0:1, :], pl.program_id(1))
        o_ref[...] = acc * _sig(acc)

    return pl.pallas_call(
        body, name=name, grid=(c // CONV_COLS, n_rows // tl), in_specs=[ucur, prev, par, par], out_specs=cur,
        out_shape=jax.ShapeDtypeStruct((n_rows, c), F32),
        compiler_params=_cparams("parallel", "parallel"),
    )(u, u, w8, b8)


def _conv_bwd_pre(u, w8, b8, dout, name):
    u, u0, c = _window(u)
    n_rows = u.shape[0]
    tl = _rows(n_rows)
    cur, _, _, par = _conv_specs(n_rows, tl)
    ucur, prev, _, _ = _conv_specs(n_rows, tl, u0)

    def body(u_ref, prev_ref, w_ref, b_ref, do_ref, dc_ref, acc_ref):
        @pl.when(pl.program_id(1) == 0)
        def _():
            acc_ref[...] = jnp.zeros_like(acc_ref)

        acc, shifted = _conv_pre(u_ref, prev_ref, w_ref, b_ref[0:1, :], pl.program_id(1))
        sg = _sig(acc)
        dc = do_ref[...] * (sg * (1.0 + acc * (1.0 - sg)))
        dc_ref[...] = dc
        for k in range(CONV_TAPS):
            acc_ref[k:k + 1, :] += jnp.sum(dc * shifted[CONV_TAPS - 1 - k], axis=0, keepdims=True)
        acc_ref[CONV_TAPS:CONV_TAPS + 1, :] += jnp.sum(dc, axis=0, keepdims=True)

    return pl.pallas_call(
        body, name=name, grid=(c // CONV_COLS, n_rows // tl), in_specs=[ucur, prev, par, par, cur],
        out_specs=(cur, par),
        out_shape=(jax.ShapeDtypeStruct((n_rows, c), F32), jax.ShapeDtypeStruct((8, c), F32)),
        compiler_params=_cparams("parallel", "arbitrary"),
    )(u, u, w8, b8, dout)


def _conv_bwd_in(dc, w8, name, into):
    n_rows, c = dc.shape
    tl = _rows(n_rows)
    cur, _, nxt, par = _conv_specs(n_rows, tl)
    n_l = n_rows // tl
    extra, extra_specs, aliases, col0, width = _into(into, 3, 0)
    out_spec = _conv_specs(n_rows, tl, col0)[0]

    def body(*refs):
        dc_ref, next_ref, w_ref = refs[:3]
        o_ref = refs[-1]
        cur_v = dc_ref[...]
        next8 = jnp.where(pl.program_id(1) == n_l - 1, 0.0, next_ref[...])
        row8 = lax.broadcasted_iota(jnp.int32, next8.shape, 0)
        acc = cur_v * w_ref[CONV_TAPS - 1:CONV_TAPS, :]
        for s in range(1, CONV_TAPS):
            acc = acc + _shift_up(cur_v, next8, s, row8) * w_ref[CONV_TAPS - 1 - s:CONV_TAPS - s, :]
        o_ref[...] = acc.astype(BF16)

    return pl.pallas_call(
        body, name=name, grid=(c // CONV_COLS, n_l), in_specs=[cur, nxt, par] + extra_specs, out_specs=out_spec,
        out_shape=jax.ShapeDtypeStruct((n_rows, width), BF16), input_output_aliases=aliases,
        compiler_params=_cparams("parallel", "parallel"),
    )(dc, dc, w8, *extra)


NORM_GROUP = SSD_D_INNER // SSD_GROUPS


def _gnorm_fwd(y, z, w, name):
    n_rows, c = y.shape
    z, z0, _ = _window(z)
    zoff = z0 // NORM_GROUP

    def body(y_ref, z_ref, w_ref, o_ref):
        zv = z_ref[...]
        yg = y_ref[...] * (zv * _sig(zv))
        r = lax.rsqrt(jnp.mean(yg * yg, axis=-1, keepdims=True) + RMS_EPS)
        o_ref[...] = (yg * r * w_ref[...]).astype(BF16)

    blk = pl.BlockSpec((_rows(n_rows),NORM_GROUP), lambda i, j: (i, j))
    zblk = pl.BlockSpec((_rows(n_rows),NORM_GROUP), lambda i, j: (i, j + zoff))
    wspec = pl.BlockSpec((1, NORM_GROUP), lambda i, j: (0, j))
    return pl.pallas_call(
        body, name=name, grid=(n_rows // _rows(n_rows), c // NORM_GROUP), in_specs=[blk, zblk, wspec], out_specs=blk,
        out_shape=jax.ShapeDtypeStruct((n_rows, c), BF16),
        compiler_params=_cparams("parallel", "parallel"),
    )(y, z, w.reshape(1, c))


def _gnorm_bwd(y, z, w, dyn, name, into):
    n_rows, c = y.shape
    z, z0, _ = _window(z)
    zoff = z0 // NORM_GROUP
    extra, extra_specs, aliases, col0, width = _into(into, 4, 1)
    doff = col0 // NORM_GROUP

    def body(*refs):
        y_ref, z_ref, w_ref, dn_ref = refs[:4]
        dy_ref, dz_ref, acc_ref = refs[-3:]
        @pl.when(pl.program_id(1) == 0)
        def _():
            acc_ref[...] = jnp.zeros_like(acc_ref)

        zv = z_ref[...]
        yv = y_ref[...]
        sz = _sig(zv)
        silu = zv * sz
        yg = yv * silu
        r = lax.rsqrt(jnp.mean(yg * yg, axis=-1, keepdims=True) + RMS_EPS)
        nrm = yg * r
        dn = dn_ref[...]
        acc_ref[0:1, :] += jnp.sum(dn * nrm, axis=0, keepdims=True)
        dnw = dn * w_ref[...]
        dyg = r * (dnw - nrm * jnp.mean(dnw * nrm, axis=-1, keepdims=True))
        dy_ref[...] = dyg * silu
        dz_ref[...] = (dyg * yv * (sz * (1.0 + zv * (1.0 - sz)))).astype(BF16)

    blk = pl.BlockSpec((_rows(n_rows),NORM_GROUP), lambda j, i: (i, j))
    zblk = pl.BlockSpec((_rows(n_rows),NORM_GROUP), lambda j, i: (i, j + zoff))
    wspec = pl.BlockSpec((1, NORM_GROUP), lambda j, i: (0, j))
    aspec = pl.BlockSpec((8, NORM_GROUP), lambda j, i: (0, j))
    return pl.pallas_call(
        body, name=name, grid=(c // NORM_GROUP, n_rows // _rows(n_rows)),
        in_specs=[blk, zblk, wspec, blk] + extra_specs,
        out_specs=(blk, pl.BlockSpec((_rows(n_rows), NORM_GROUP), lambda j, i: (i, j + doff)), aspec),
        out_shape=(jax.ShapeDtypeStruct((n_rows, c), F32), jax.ShapeDtypeStruct((n_rows, width), BF16),
                   jax.ShapeDtypeStruct((8, c), F32)),
        input_output_aliases=aliases,
        compiler_params=_cparams("parallel", "arbitrary"),
    )(y, z, w.reshape(1, c), dyn, *extra)


ATT_SCALE = ATT_HEAD_DIM ** -0.5
ATT_SLOPES = [2.0 ** (-8.0 * (h + 1) / ATT_HEADS) for h in range(ATT_HEADS)]
Q_PER_KV = ATT_HEADS // ATT_KV_HEADS


def _dup_half(t, g, lo):
    tr = pltpu.roll(t, ATT_HEAD_DIM, axis=1)
    return jnp.where(lo, t, tr) if g == 0 else jnp.where(lo, tr, t)


def _att_band(kv_ref, kvp_ref, n):
    cur = kv_ref[...]
    prev = jnp.where(n == 0, 0.0, kvp_ref[...])
    lo = lax.broadcasted_iota(jnp.int32, (ATT_BLOCK, LANE), 1) < ATT_HEAD_DIM
    bands = []
    for g in range(ATT_KV_HEADS):
        kb = jnp.concatenate([_dup_half(prev[:, :LANE], g, lo), _dup_half(cur[:, :LANE], g, lo)], axis=0)
        vb = jnp.concatenate([_dup_half(prev[:, LANE:], g, lo), _dup_half(cur[:, LANE:], g, lo)], axis=0)
        bands.append((kb.astype(BF16), vb.astype(BF16)))
    return bands


def _att_tile(n):
    shape = (2 * ATT_BLOCK, ATT_BLOCK)
    row = lax.broadcasted_iota(jnp.int32, shape, 0)
    i = row & (ATT_BLOCK - 1)
    s = lax.broadcasted_iota(jnp.int32, shape, 1)
    upper = s > i
    dist = ((i - s) & (ATT_BLOCK - 1)).astype(F32)
    dead = upper & (n == 0)
    return upper, dist, dead, row[:, 0:1] < ATT_BLOCK


def _stack_pair(t, lo):
    return jnp.concatenate([jnp.where(lo, t, 0.0), jnp.where(lo, 0.0, t)], axis=0).astype(BF16)


def _att_exp(qs, kb, s_ref, j, tile):
    upper, dist, dead, first = tile
    s2 = _dot(qs, kb, _NT)
    slope = jnp.where(first, ATT_SLOPES[2 * j], ATT_SLOPES[2 * j + 1])
    sink = jnp.where(first, s_ref[0:1, 2 * j:2 * j + 1], s_ref[0:1, 2 * j + 1:2 * j + 2])
    s = jnp.where(upper, s2[:, :ATT_BLOCK], s2[:, ATT_BLOCK:]) - slope * dist
    s = jnp.where(dead, NEG, s)
    m = jnp.maximum(jnp.max(s, axis=-1, keepdims=True), sink)
    return jnp.exp(s - m), jnp.exp(sink - m)


def _band_split(t, upper):
    return jnp.concatenate([jnp.where(upper, t, 0.0), jnp.where(upper, 0.0, t)], axis=1)


def _att_fwd(q, kv, sinks8, name):
    q, q0, _ = _window(q)
    kv, kv0, _ = _window(kv)
    qoff, kvoff = q0 // Q_DIM, kv0 // (2 * LANE)
    n_rows = q.shape[0]
    nb = n_rows // ATT_BLOCK

    def body(q_ref, kv_ref, kvp_ref, s_ref, o_ref, o32_ref):
        n = pl.program_id(0)
        bands = _att_band(kv_ref, kvp_ref, n)
        lo = lax.broadcasted_iota(jnp.int32, (ATT_BLOCK, LANE), 1) < ATT_HEAD_DIM
        tile = _att_tile(n)
        ones_b = jnp.ones((2 * ATT_BLOCK, LANE), BF16)
        for j in range(ATT_HEADS // 2):
            kb, vb = bands[2 * j // Q_PER_KV]
            qs = _stack_pair(q_ref[:, j * LANE:(j + 1) * LANE] * ATT_SCALE, lo)
            p, es = _att_exp(qs, kb, s_ref, j, tile)
            pv = _dot(_band_split(p, tile[0]).astype(BF16), jnp.concatenate([vb, ones_b], axis=1))
            out = pv[:, :LANE] / (pv[:, LANE:] + es)
            out = jnp.where(lo, out[:ATT_BLOCK], out[ATT_BLOCK:])
            o_ref[:, j * LANE:(j + 1) * LANE] = out.astype(BF16)
            o32_ref[:, j * LANE:(j + 1) * LANE] = out

    return pl.pallas_call(
        body, name=name, grid=(nb,),
        in_specs=[pl.BlockSpec((ATT_BLOCK, Q_DIM), lambda n: (n, qoff)),
                  pl.BlockSpec((ATT_BLOCK, 2 * LANE), lambda n: (n, kvoff)),
                  pl.BlockSpec((ATT_BLOCK, 2 * LANE), lambda n: (jnp.maximum(n - 1, 0), kvoff)),
                  pl.BlockSpec((8, LANE), lambda n: (0, 0))],
        out_specs=(pl.BlockSpec((ATT_BLOCK, Q_DIM), lambda n: (n, 0)),) * 2,
        out_shape=(jax.ShapeDtypeStruct((n_rows, Q_DIM), BF16), jax.ShapeDtypeStruct((n_rows, Q_DIM), F32)),
        compiler_params=_cparams("parallel"),
    )(q, kv, kv, sinks8)


def _att_bwd(q, kv, sinks8, out32, dout, name, into):
    q, q0, _ = _window(q)
    kv, kv0, _ = _window(kv)
    qoff, kvoff = q0 // Q_DIM, kv0 // (2 * LANE)
    n_rows = q.shape[0]
    nb = n_rows // ATT_BLOCK

    extra, extra_specs, aliases, col0, width = _into(into, 6, 0)
    dqoff = col0 // Q_DIM

    def body(*refs):
        q_ref, kv_ref, kvp_ref, s_ref, o_ref, do_ref = refs[:6]
        dq_ref, dkv_ref, acc_ref, carry_ref = refs[-4:]
        n = pl.program_id(0)

        @pl.when(n == 0)
        def _():
            acc_ref[...] = jnp.zeros_like(acc_ref)
            carry_ref[...] = jnp.zeros_like(carry_ref)

        @pl.when(n == nb)
        def _():
            dkv_ref[...] = carry_ref[...].astype(BF16)

        @pl.when(n < nb)
        def _():
            bands = _att_band(kv_ref, kvp_ref, n)
            lo = lax.broadcasted_iota(jnp.int32, (ATT_BLOCK, LANE), 1) < ATT_HEAD_DIM
            lane1 = lax.broadcasted_iota(jnp.int32, (1, LANE), 1)
            tile = _att_tile(n)
            upper, first = tile[0], tile[3]
            ones_b = jnp.ones((ATT_BLOCK, LANE), BF16)
            ones2_b = jnp.ones((2 * LANE, LANE), BF16)
            dk_acc = [jnp.zeros((2 * ATT_BLOCK, LANE), F32) for _ in range(ATT_KV_HEADS)]
            dv_acc = [jnp.zeros((2 * ATT_BLOCK, LANE), F32) for _ in range(ATT_KV_HEADS)]
            dsink = jnp.zeros((1, LANE), F32)
            for j in range(ATT_HEADS // 2):
                g = 2 * j // Q_PER_KV
                kb, vb = bands[g]
                qs = _stack_pair(q_ref[:, j * LANE:(j + 1) * LANE] * ATT_SCALE, lo)
                dop = do_ref[:, j * LANE:(j + 1) * LANE].astype(F32)
                dos = _stack_pair(dop, lo)
                pu, es = _att_exp(qs, kb, s_ref, j, tile)
                inv = 1.0 / (_dot(pu.astype(BF16), ones_b) + es)
                p = pu * inv
                od = dos.astype(F32) * jnp.concatenate([o_ref[:, j * LANE:(j + 1) * LANE]] * 2, axis=0)
                od_hi = od.astype(BF16)
                delta = _dot(jnp.concatenate([od_hi, (od - od_hi.astype(F32)).astype(BF16)], axis=1), ones2_b)
                dp2 = _dot(dos, vb, _NT)
                dp = jnp.where(upper, dp2[:, :ATT_BLOCK], dp2[:, ATT_BLOCK:])
                ds2 = _band_split(p * (dp - delta), upper)
                psd = jnp.sum(es * inv * delta, axis=0, keepdims=True)
                psd0 = jnp.sum(jnp.where(first, es * inv * delta, 0.0), axis=0, keepdims=True)
                dsink = jnp.where(lane1 == 2 * j, -psd0, jnp.where(lane1 == 2 * j + 1, psd0 - psd, dsink))
                ds2_b = ds2.astype(BF16)
                dq = _dot(ds2_b, kb) * ATT_SCALE
                dq_ref[:, j * LANE:(j + 1) * LANE] = jnp.where(lo, dq[:ATT_BLOCK], dq[ATT_BLOCK:]).astype(BF16)
                dk_acc[g] = dk_acc[g] + _dot(ds2_b, qs, _TN)
                dv_acc[g] = dv_acc[g] + _dot(_band_split(p, upper).astype(BF16), dos, _TN)
            acc_ref[0:1, :] += dsink
            lo2 = lax.broadcasted_iota(jnp.int32, (2 * ATT_BLOCK, LANE), 1) < ATT_HEAD_DIM
            folded = []
            for acc in (dk_acc, dv_acc):
                t0 = acc[0] + pltpu.roll(acc[0], ATT_HEAD_DIM, axis=1)
                t1 = acc[1] + pltpu.roll(acc[1], ATT_HEAD_DIM, axis=1)
                folded.append(jnp.where(lo2, t0, t1))
            band = jnp.concatenate(folded, axis=1)
            dkv_ref[...] = (carry_ref[...] + band[:ATT_BLOCK]).astype(BF16)
            carry_ref[...] = band[ATT_BLOCK:]

    def qmap(n):
        return (jnp.minimum(n, nb - 1), 0)

    return pl.pallas_call(
        body, name=name, grid=(nb + 1,),
        in_specs=[pl.BlockSpec((ATT_BLOCK, Q_DIM), lambda n: (jnp.minimum(n, nb - 1), qoff)),
                  pl.BlockSpec((ATT_BLOCK, 2 * LANE), lambda n: (jnp.minimum(n, nb - 1), kvoff)),
                  pl.BlockSpec((ATT_BLOCK, 2 * LANE),
                               lambda n: (jnp.maximum(jnp.minimum(n, nb - 1) - 1, 0), kvoff)),
                  pl.BlockSpec((8, LANE), lambda n: (0, 0)),
                  pl.BlockSpec((ATT_BLOCK, Q_DIM), qmap),
                  pl.BlockSpec((ATT_BLOCK, Q_DIM), qmap)] + extra_specs,
        out_specs=(pl.BlockSpec((ATT_BLOCK, Q_DIM), lambda n: (jnp.minimum(n, nb - 1), dqoff)),
                   pl.BlockSpec((ATT_BLOCK, 2 * LANE), lambda n: (jnp.maximum(n - 1, 0), 0)),
                   pl.BlockSpec((8, LANE), lambda n: (0, 0))),
        out_shape=(jax.ShapeDtypeStruct((n_rows, width), BF16), jax.ShapeDtypeStruct((n_rows, 2 * LANE), BF16),
                   jax.ShapeDtypeStruct((8, LANE), F32)),
        input_output_aliases=aliases,
        scratch_shapes=[pltpu.VMEM((ATT_BLOCK, 2 * LANE), F32)],
        compiler_params=_cparams("arbitrary"),
    )(q, kv, kv, sinks8, out32, dout, *extra)


HEADS_PER_GROUP = SSD_HEADS // SSD_GROUPS
PAIRS_PER_GROUP = HEADS_PER_GROUP // 2
T = SSD_CHUNK


def _cumsum_mm(mat, x):
    hi = x.astype(BF16)
    r = x - hi.astype(F32)
    mid = r.astype(BF16)
    lo = (r - mid.astype(F32)).astype(BF16)
    w = x.shape[1]
    out = _dot(mat, jnp.concatenate([hi, mid, lo], axis=1))
    return out[:, :w] + out[:, w:2 * w] + out[:, 2 * w:]


def _ssd_prep(dtr_ref, par_ref):
    dt = _softplus(dtr_ref[...] + par_ref[0:1, :])
    a = -jnp.exp(par_ref[1:2, :])
    ri = lax.broadcasted_iota(jnp.int32, (T, T), 0)
    ci = lax.broadcasted_iota(jnp.int32, (T, T), 1)
    cs = _cumsum_mm((ri >= ci).astype(BF16), dt * a)
    lo = lax.broadcasted_iota(jnp.int32, (T, LANE), 1) < SSD_CHUNK // 2

    def expand(arr):
        rows = arr.shape[0]
        return jnp.concatenate([jnp.where(lo[:rows], arr[:, 2 * j:2 * j + 1], arr[:, 2 * j + 1:2 * j + 2])
                                for j in range(PAIRS_PER_GROUP)], axis=1)

    tot = cs[T - 1:T, :]
    return {"dt": dt, "a": a, "cs": cs, "cst": cs.T, "lo": lo, "ri": ri, "ci": ci, "expand": expand,
            "dt_x": expand(dt), "ecs_x": expand(jnp.exp(cs)), "dec_x": expand(jnp.exp(tot - cs)),
            "et_x": expand(jnp.exp(tot)), "etot": jnp.exp(tot), "dsk_x": expand(par_ref[2:3, :])}


def _wide_masks():
    r = lax.broadcasted_iota(jnp.int32, (T, 2 * T), 0)
    l = lax.broadcasted_iota(jnp.int32, (T, 2 * T), 1)
    s = l & (T - 1)
    return r >= s, s >= r, l < T


def _wide_cs(q, k0, even):
    cs, cst = q["cs"], q["cst"]
    col = jnp.where(even, cs[:, k0:k0 + 1], cs[:, k0 + 1:k0 + 2])
    row = jnp.concatenate([cst[k0:k0 + 1, :], cst[k0 + 1:k0 + 2, :]], axis=1)
    return col, row


def _ssd_fwd(xs, bm, cm, dtr, par, name):
    dtr, dt0, _ = _window(dtr)
    dtoff = dt0 // LANE
    n_rows = xs.shape[0]
    nc = n_rows // T
    gw = PAIRS_PER_GROUP * LANE

    def body(x_ref, b_ref, c_ref, dtr_ref, par_ref, y_ref, hs_ref, h_ref):
        @pl.when(pl.program_id(1) == 0)
        def _():
            h_ref[...] = jnp.zeros_like(h_ref)

        q = _ssd_prep(dtr_ref, par_ref)
        lo = q["lo"]
        tri_w, _, even = _wide_masks()
        bg_b = b_ref[...].astype(BF16)
        cg_b = c_ref[...].astype(BF16)
        xv = x_ref[...]
        xdt = xv * q["dt_x"]
        h = h_ref[...]
        hs_ref[0, 0] = h
        yo = q["ecs_x"] * _dot(cg_b, h.astype(BF16))
        h_ref[...] = h * q["et_x"] + _dot(bg_b, (xdt * q["dec_x"]).astype(BF16), _TN)
        cb = _dot(cg_b, bg_b, _NT)
        cb_w = jnp.concatenate([cb, cb], axis=1)
        for j in range(PAIRS_PER_GROUP):
            col, row = _wide_cs(q, 2 * j, even)
            m_w = (jnp.exp(jnp.where(tri_w, col - row, NEG)) * cb_w).astype(BF16)
            sl = slice(j * LANE, (j + 1) * LANE)
            y_ref[:, sl] = (_dot(m_w, _stack_pair(xdt[:, sl], lo)) + yo[:, sl] + q["dsk_x"][:, sl] * xv[:, sl])

    return pl.pallas_call(
        body, name=name, grid=(SSD_GROUPS, nc),
        in_specs=[pl.BlockSpec((T, gw), lambda g, c: (c, g)),
                  pl.BlockSpec((T, SSD_STATE), lambda g, c: (c, g)),
                  pl.BlockSpec((T, SSD_STATE), lambda g, c: (c, g)),
                  pl.BlockSpec((T, LANE), lambda g, c: (c, g + dtoff)),
                  pl.BlockSpec((8, LANE), lambda g, c: (0, g))],
        out_specs=(pl.BlockSpec((T, gw), lambda g, c: (c, g)),
                   pl.BlockSpec((1, 1, SSD_STATE, gw), lambda g, c: (g, c, 0, 0))),
        out_shape=(jax.ShapeDtypeStruct((n_rows, SSD_D_INNER), F32),
                   jax.ShapeDtypeStruct((SSD_GROUPS, nc, SSD_STATE, gw), F32)),
        scratch_shapes=[pltpu.VMEM((SSD_STATE, gw), F32)],
        compiler_params=_cparams("parallel", "arbitrary"),
    )(xs, bm, cm, dtr, par)


def _ssd_bwd(xs, bm, cm, dtr, par, hs, dy, name, into):
    dtr, dt0, _ = _window(dtr)
    dtoff = dt0 // LANE
    n_rows = xs.shape[0]
    nc = n_rows // T
    gw = PAIRS_PER_GROUP * LANE
    extra, extra_specs, aliases, col0, width = _into(into, 7, 3)
    ddoff = col0 // LANE

    def body(*refs):
        x_ref, b_ref, c_ref, dtr_ref, par_ref, hs_ref, dy_ref = refs[:7]
        dx_ref, db_ref, dc_ref, ddtr_ref, acc_ref, dh_ref = refs[-6:]

        @pl.when(pl.program_id(1) == 0)
        def _():
            dh_ref[...] = jnp.zeros_like(dh_ref)
            acc_ref[...] = jnp.zeros_like(acc_ref)

        q = _ssd_prep(dtr_ref, par_ref)
        lo, dt, a = q["lo"], q["dt"], q["a"]
        tri_w, trit_w, even = _wide_masks()
        lane = lax.broadcasted_iota(jnp.int32, (T, LANE), 1)
        lane1 = lane[0:1, :]
        last_row = lax.broadcasted_iota(jnp.int32, (T, 1), 0) == T - 1
        bg_b = b_ref[...].astype(BF16)
        cg_b = c_ref[...].astype(BF16)
        xv = x_ref[...]
        dyv = dy_ref[...]
        xdt = xv * q["dt_x"]
        h = hs_ref[0, 0]
        dhn = dh_ref[...]
        h_b, dhn_b = h.astype(BF16), dhn.astype(BF16)
        yo = q["ecs_x"] * _dot(cg_b, h_b)
        bdh = q["dec_x"] * _dot(bg_b, dhn_b)
        dye = (dyv * q["ecs_x"]).astype(BF16)
        xd = (xdt * q["dec_x"]).astype(BF16)
        dcg = _dot(dye, h_b, _NT)
        dbg = _dot(xd, dhn_b, _NT)
        dh_ref[...] = dhn * q["et_x"] + _dot(cg_b, dye, _TN)
        e4_all = xdt * bdh
        f_all = dyv * yo - e4_all
        tot_row = jnp.sum(e4_all, axis=0, keepdims=True) + q["et_x"] * jnp.sum(h * dhn, axis=0, keepdims=True)
        dsk_row = jnp.sum(dyv * xv, axis=0, keepdims=True)
        cb = _dot(cg_b, bg_b, _NT)
        cbt = _dot(bg_b, cg_b, _NT)
        cb_w = jnp.concatenate([cb, cb], axis=1)
        cbt_w = jnp.concatenate([cbt, cbt], axis=1)
        dcb = jnp.zeros((T, T), F32)
        dcbt = jnp.zeros((T, T), F32)
        dcs_acc = jnp.zeros((T, LANE), F32)
        ddt_acc = jnp.zeros((T, LANE), F32)
        dsk_acc = jnp.zeros((1, LANE), F32)
        tot_acc = jnp.zeros((1, LANE), F32)
        ind_r = lax.broadcasted_iota(jnp.int32, (2 * T, LANE), 0)
        ind_l = lax.broadcasted_iota(jnp.int32, (2 * T, LANE), 1)

        def halves(t):
            return (jnp.sum(jnp.where(lo[0:1], t, 0.0), axis=-1, keepdims=True),
                    jnp.sum(jnp.where(lo[0:1], 0.0, t), axis=-1, keepdims=True))

        def split2(t):
            hi = t.astype(BF16)
            return jnp.concatenate([hi, (t - hi.astype(F32)).astype(BF16)], axis=1)

        for j in range(PAIRS_PER_GROUP):
            k0, k1 = 2 * j, 2 * j + 1
            sl = slice(j * LANE, (j + 1) * LANE)
            col, row = _wide_cs(q, k0, even)
            lm_w = jnp.exp(jnp.where(tri_w, col - row, NEG))
            lmt_w = jnp.exp(jnp.where(trit_w, row - col, NEG))
            dyp, xp = dyv[:, sl], xdt[:, sl]
            dym, xm = _stack_pair(dyp, lo), _stack_pair(xp, lo)
            dm_w = _dot(dyp.astype(BF16), xm, _NT)
            dmt_w = _dot(xp.astype(BF16), dym, _NT)
            mm_w = lm_w * cb_w
            mmt_w = lmt_w * cbt_w
            dxdt = _dot(mmt_w.astype(BF16), dym) + bdh[:, sl]
            g1 = dm_w * lm_w
            g2 = dmt_w * lmt_w
            dcb = dcb + g1[:, :T] + g1[:, T:]
            dcbt = dcbt + g2[:, :T] + g2[:, T:]
            ind_w = jnp.where(ind_l == jnp.where(ind_r < T, k0, k1), 1.0, 0.0).astype(BF16)
            ind_p = jnp.where(ind_l[:T] == jnp.where(ind_r[:T] < SSD_CHUNK // 2, k0, k1), 1.0, 0.0).astype(BF16)
            dcs_acc = dcs_acc + _dot(
                jnp.concatenate([split2(dm_w * mm_w - dmt_w * mmt_w), split2(f_all[:, sl])], axis=1),
                jnp.concatenate([ind_w, ind_w, ind_p, ind_p], axis=0))
            ddt_acc = ddt_acc + _dot(split2(dxdt * xv[:, sl]), jnp.concatenate([ind_p, ind_p], axis=0))
            tot2 = halves(tot_row[:, sl])
            tot_acc = jnp.where(lane1 == k0, tot2[0], jnp.where(lane1 == k1, tot2[1], tot_acc))
            dsk2 = halves(dsk_row[:, sl])
            dsk_acc = jnp.where(lane1 == k0, dsk2[0], jnp.where(lane1 == k1, dsk2[1], dsk_acc))
            dx_ref[:, sl] = dxdt * q["dt_x"][:, sl] + q["dsk_x"][:, sl] * dyp
        dcs_acc = dcs_acc + jnp.where(last_row, tot_acc, 0.0)
        dc_ref[...] = dcg + _dot(dcb.astype(BF16), bg_b)
        db_ref[...] = dbg + _dot(dcbt.astype(BF16), cg_b)
        dda = _cumsum_mm((q["ci"] >= q["ri"]).astype(BF16), dcs_acc)
        ddt = ddt_acc + dda * a
        ddtr = ddt * _sig(dtr_ref[...] + par_ref[0:1, :])
        ddtr_ref[...] = ddtr.astype(BF16)
        acc_ref[0:1, :] += jnp.sum(ddtr, axis=0, keepdims=True)
        acc_ref[1:2, :] += jnp.sum(dda * dt, axis=0, keepdims=True) * a
        acc_ref[2:3, :] += dsk_acc

    def rev(g, c):
        return (nc - 1 - c, g)

    return pl.pallas_call(
        body, name=name, grid=(SSD_GROUPS, nc),
        in_specs=[pl.BlockSpec((T, gw), rev),
                  pl.BlockSpec((T, SSD_STATE), rev),
                  pl.BlockSpec((T, SSD_STATE), rev),
                  pl.BlockSpec((T, LANE), lambda g, c: (nc - 1 - c, g + dtoff)),
                  pl.BlockSpec((8, LANE), lambda g, c: (0, g)),
                  pl.BlockSpec((1, 1, SSD_STATE, gw), lambda g, c: (g, nc - 1 - c, 0, 0)),
                  pl.BlockSpec((T, gw), rev)] + extra_specs,
        out_specs=(pl.BlockSpec((T, gw), rev),
                   pl.BlockSpec((T, SSD_STATE), rev),
                   pl.BlockSpec((T, SSD_STATE), rev),
                   pl.BlockSpec((T, LANE), lambda g, c: (nc - 1 - c, g + ddoff)),
                   pl.BlockSpec((8, LANE), lambda g, c: (0, g))),
        out_shape=(jax.ShapeDtypeStruct((n_rows, SSD_D_INNER), F32),
                   jax.ShapeDtypeStruct((n_rows, BC_DIM), F32),
                   jax.ShapeDtypeStruct((n_rows, BC_DIM), F32),
                   jax.ShapeDtypeStruct((n_rows, width), BF16),
                   jax.ShapeDtypeStruct((8, DT_PAD), F32)),
        input_output_aliases=aliases,
        scratch_shapes=[pltpu.VMEM((SSD_STATE, gw), F32)],
        compiler_params=_cparams("parallel", "arbitrary"),
    )(xs, bm, cm, dtr, par, hs, dy, *extra)


ADAM_ROWS = 256


def _adamw(lands, w, m, v, name):
    na = len(lands)
    n_slots, r, wd = lands[0].shape
    tr = r if r <= 2 * ADAM_ROWS else ADAM_ROWS
    nj = r // tr
    bc1 = 1.0 - ADAM_B1 ** ADAM_STEP
    bc2 = 1.0 - ADAM_B2 ** ADAM_STEP

    def body(*refs):
        l_refs = refs[:na]
        w_ref, m_ref, v_ref, g_ref, d_ref, nm_ref, nv_ref = refs[na:]
        for a in range(na):
            @pl.when(pl.program_id(0) == a)
            def _(l_ref=l_refs[a]):
                g = l_ref[0].astype(F32)
                for s in range(1, n_slots):
                    g = g + l_ref[s].astype(F32)
                mn = ADAM_B1 * m_ref[0] + (1.0 - ADAM_B1) * g
                vn = ADAM_B2 * v_ref[0] + (1.0 - ADAM_B2) * (g * g)
                mh = mn / bc1
                vh = vn / bc2
                g_ref[0] = g
                nm_ref[0] = mn
                nv_ref[0] = vn
                d_ref[0] = -ADAM_LR * (mh / (jnp.sqrt(vh) + ADAM_EPS) + ADAM_WD * w_ref[0])

    def land_spec(a):
        return pl.BlockSpec((n_slots, tr, wd),
                            lambda i, j: (0, jnp.where(i == a, j, jnp.where(i < a, 0, nj - 1)), 0))

    blk = pl.BlockSpec((1, tr, wd), lambda i, j: (i, j, 0))
    shp = jax.ShapeDtypeStruct((na, r, wd), F32)
    return pl.pallas_call(
        body, name=name, grid=(na, nj), in_specs=[land_spec(a) for a in range(na)] + [blk, blk, blk],
        out_specs=(blk, blk, blk, blk), out_shape=(shp, shp, shp, shp),
        compiler_params=_cparams("arbitrary", "arbitrary"),
    )(*lands, w, m, v)


def _mesh_pos():
    return lax.axis_index("x"), lax.axis_index("y"), lax.axis_index("c")


def _peer(pos, k):
    x, y, c = pos
    px = 1 - x if (k >> 2) & 1 else x
    py = 1 - y if (k >> 1) & 1 else y
    pc = 1 - c if k & 1 else c
    return px, py, pc


def _flat(pos):
    return 4 * pos[0] + 2 * pos[1] + pos[2]


HBM_SPEC = pl.BlockSpec(memory_space=pl.ANY)


ROW_SHARDED = ("w_ssd_out", "w_att_out", "w_mix_out", "w_ffn_down")
COL_SHARDED = ("w_in", "w_ffn_gate", "w_ffn_up")
GATHERED = ROW_SHARDED + COL_SHARDED + ("conv_w",)


SEM_SPEC = pl.BlockSpec(memory_space=pltpu.SEMAPHORE)
TOKEN = jax.ShapeDtypeStruct((8, LANE), F32)
SPLIT_EFFECT = pltpu.SideEffectType.DATAFLOW_SIDE_EFFECTING
GATHER_ROWS = "gather_rows"
GATHER_SLOT = "gather_slot"
SCATTER_ROWS = "scatter_rows"
SCATTER_SLOT = "scatter_slot"


def _land_shape(kind, src):
    if kind == GATHER_ROWS:
        return (N_DEV * src.shape[0],) + src.shape[1:]
    if kind == GATHER_SLOT:
        return (N_DEV,) + src.shape
    if kind == SCATTER_ROWS:
        return (N_DEV, src.shape[0] // N_DEV) + src.shape[1:]
    return src.shape


def _views(kind, src_ref, land_ref, pos, k):
    me = _flat(pos)
    if kind == GATHER_ROWS:
        r = src_ref.shape[0]
        return src_ref, land_ref.at[pl.ds(pl.multiple_of(me * r, 16), r), :]
    if kind == GATHER_SLOT:
        return src_ref, land_ref.at[me]
    dev = _flat(_peer(pos, k))
    if kind == SCATTER_ROWS:
        r = land_ref.shape[1]
        return src_ref.at[pl.ds(pl.multiple_of(dev * r, 16), r), :], land_ref.at[k]
    return src_ref.at[dev], land_ref.at[k]


def _hbm(x):
    return pltpu.with_memory_space_constraint(x, pltpu.HBM)


def _exchange_start(items, after, name):
    kinds = [k for k, _ in items]
    srcs = [_hbm(s) for _, s in items]
    lands = [_hbm(lax.empty(_land_shape(k, s), s.dtype)) for k, s in items]
    n = len(items)
    n_copy = n * (N_DEV - 1)

    def body(*refs):
        src_refs, land_refs = refs[:n], refs[n:2 * n]
        send_sems, recv_sems = refs[2 * n + 1], refs[2 * n + 2]
        token_ref = refs[4 * n + 3]
        pos = _mesh_pos()
        for i, kind in enumerate(kinds):
            for k in range(1, N_DEV):
                s, d = _views(kind, src_refs[i], land_refs[i], pos, k)
                j = i * (N_DEV - 1) + k - 1
                pltpu.make_async_remote_copy(src_ref=s, dst_ref=d, send_sem=send_sems.at[j], recv_sem=recv_sems.at[j],
                                             device_id=_peer(pos, k), device_id_type=MESH_ID).start()
        token_ref[...] = jnp.zeros_like(token_ref)

    arrs = srcs + lands
    outs = pl.pallas_call(
        body, name=name,
        in_specs=[HBM_SPEC] * (2 * n + 1),
        out_specs=[SEM_SPEC, SEM_SPEC] + [HBM_SPEC] * (2 * n) + [pl.BlockSpec(memory_space=pltpu.VMEM)],
        out_shape=[pltpu.SemaphoreType.DMA((n_copy,)), pltpu.SemaphoreType.DMA((n_copy,))]
        + [pltpu.HBM(a.shape, a.dtype) for a in arrs] + [TOKEN],
        input_output_aliases={i: 2 + i for i in range(2 * n)},
        compiler_params=pltpu.CompilerParams(has_side_effects=SPLIT_EFFECT),
    )(*arrs, after)
    return {"kinds": kinds, "send": outs[0], "recv": outs[1], "arrs": outs[2:2 + 2 * n], "token": outs[-1]}


def _exchange_wait(ex, after, name):
    kinds = ex["kinds"]
    n = len(kinds)

    def body(*refs):
        src_refs, land_refs = refs[:n], refs[n:2 * n]
        send_sems, recv_sems = refs[2 * n], refs[2 * n + 1]
        token_ref = refs[-1]
        pos = _mesh_pos()
        for i, kind in enumerate(kinds):
            for k in range(1, N_DEV):
                s, d = _views(kind, src_refs[i], land_refs[i], pos, k)
                j = i * (N_DEV - 1) + k - 1
                cp = pltpu.make_async_remote_copy(src_ref=s, dst_ref=d, send_sem=send_sems.at[j],
                                                  recv_sem=recv_sems.at[j], device_id=_peer(pos, k),
                                                  device_id_type=MESH_ID)
                cp.wait_send()
                cp.wait_recv()
        token_ref[...] = jnp.zeros_like(token_ref)

    outs = pl.pallas_call(
        body, name=name,
        in_specs=[HBM_SPEC] * (2 * n) + [SEM_SPEC, SEM_SPEC, HBM_SPEC],
        out_specs=[HBM_SPEC] * (2 * n) + [pl.BlockSpec(memory_space=pltpu.VMEM)],
        out_shape=[pltpu.HBM(a.shape, a.dtype) for a in ex["arrs"]] + [TOKEN],
        input_output_aliases={i: i for i in range(2 * n)},
        compiler_params=pltpu.CompilerParams(has_side_effects=SPLIT_EFFECT),
    )(*ex["arrs"], ex["send"], ex["recv"], after)
    lands = [_place_own(k, s, d) for k, s, d in zip(kinds, outs[:n], outs[n:2 * n])]
    return lands, outs[-1]


def _place_own(kind, src, land):
    me = _flat(_mesh_pos())
    zeros = (0,) * (src.ndim - 1)
    if kind == GATHER_ROWS:
        return lax.dynamic_update_slice(land, src, (me * src.shape[0],) + zeros)
    if kind == GATHER_SLOT:
        return lax.dynamic_update_slice(land, src[None], (me,) + (0,) * src.ndim)
    if kind == SCATTER_ROWS:
        r = land.shape[1]
        own = lax.dynamic_slice(src, (me * r,) + zeros, (r,) + src.shape[1:])
    else:
        own = lax.dynamic_index_in_dim(src, me, 0, keepdims=False)
    return lax.dynamic_update_slice(land, own[None], (0,) * land.ndim)


def _all_gather_small(x, name):
    r, w = x.shape

    def body(x_ref, out_ref, send_sems, recv_sems):
        pos = _mesh_pos()
        me = _flat(pos)
        copies = []
        for k in range(1, N_DEV):
            cp = pltpu.make_async_remote_copy(
                src_ref=x_ref, dst_ref=out_ref.at[me], send_sem=send_sems.at[k - 1], recv_sem=recv_sems.at[k - 1],
                device_id=_peer(pos, k), device_id_type=MESH_ID)
            cp.start()
            copies.append(cp)
        out_ref[me] = x_ref[...]
        for cp in copies:
            cp.wait()

    vmem = pl.BlockSpec(memory_space=pltpu.VMEM)
    return pl.pallas_call(
        body, name=name, in_specs=[vmem], out_specs=vmem,
        out_shape=jax.ShapeDtypeStruct((N_DEV, r, w), x.dtype),
        scratch_shapes=[pltpu.SemaphoreType.DMA((N_DEV - 1,)), pltpu.SemaphoreType.DMA((N_DEV - 1,))],
        compiler_params=pltpu.CompilerParams(has_side_effects=True),
    )(x)


def _cols(g, lo, hi):
    c = g.shape[-1]
    parts = []
    for d in range(N_DEV):
        a, b = max(lo, d * c), min(hi, (d + 1) * c)
        if a < b:
            parts.append(g[d, :, a - d * c:b - d * c])
    return parts[0] if len(parts) == 1 else jnp.concatenate(parts, axis=1)


def _col_chunks(g):
    c = g.shape[-1] // N_DEV
    return jnp.stack([g[:, d * c:(d + 1) * c] for d in range(N_DEV)])


IN_PART = ("w_in", "conv_w")
OUT_PART = ROW_SHARDED + ("w_ffn_gate", "w_ffn_up")
TRANSPOSED = ("w_ffn_gate", "w_ffn_up")


def _gather_items(w, names, l):
    items = []
    for n in names:
        blk = w[n][l] if n == "conv_w" else w[n][l].astype(BF16)
        if n in TRANSPOSED:
            blk = blk.T
        items.append((GATHER_ROWS if n in ROW_SHARDED + TRANSPOSED else GATHER_SLOT, blk))
    return items


def _scatter_items(grads, names):
    def chunked(g):
        return g if g.ndim == 3 else _col_chunks(g)

    return [(SCATTER_ROWS, grads[n]) if n in ROW_SHARDED + TRANSPOSED else (SCATTER_SLOT, chunked(grads[n]))
            for n in names]


SMALL = ("ln_in_g", "ln_in_b", "conv_b", "dt_bias", "a_log", "d_skip", "ssd_norm_w", "att_sinks",
         "ln_mix_g", "ln_mix_b", "ln_ffn_g", "ln_ffn_b")


def _pack_small(vals):
    flat = jnp.concatenate([vals[n].reshape(-1) for n in SMALL])
    n = flat.shape[0]
    rows = -(-n // LANE)
    rows = -(-rows // 8) * 8
    return jnp.pad(flat, (0, rows * LANE - n)).reshape(rows, LANE)


def _unpack_small(buf, shapes):
    flat = buf.reshape(-1)
    off = 0
    out = {}
    for n in SMALL:
        cnt = math.prod(shapes[n])
        out[n] = flat[off:off + cnt].reshape(shapes[n])
        off += cnt
    return out


def _to_group_major(v):
    lead = v.shape[:-1]
    t = v.reshape(lead + (SSD_GROUPS, HEADS_PER_GROUP))
    t = jnp.pad(t, [(0, 0)] * len(lead) + [(0, 0), (0, LANE - HEADS_PER_GROUP)])
    return t.reshape(lead + (DT_PAD,))


def _from_group_major(v):
    lead = v.shape[:-1]
    return v.reshape(lead + (SSD_GROUPS, LANE))[..., :HEADS_PER_GROUP].reshape(lead + (SSD_HEADS,))


def _rows8(v):
    return jnp.pad(v, ((0, 8 - v.shape[0]), (0, 0)))


IN_OFFS = {"q": (0, 1024), "kv": (1024, 1280), "z": (1280, 3328), "xs": (3328, 5376), "b": (5376, 5888),
           "c": (5888, 6400), "dt": (6400, 6432), "gl": (6432, 8480)}
PIECES = ("q", "kv", "z", "xs", "b", "c", "dt", "gl")


CAT = ("z", "xs", "gl", "q", "b", "c", "dt", "kv")
CAT_WIDTH = {"q": 1024, "z": 2048, "xs": 2048, "gl": 2048, "b": 512, "c": 512, "kv": 256, "dt": DT_PAD}
CAT_OFF = {p: sum(CAT_WIDTH[q] for q in CAT[:i]) for i, p in enumerate(CAT)}
CAT_DIM = sum(CAT_WIDTH.values())
MAIN_DIM = CAT_OFF["kv"]


def _cat_w_in(g):
    pieces = {p: _cols(g, lo, hi) for p, (lo, hi) in IN_OFFS.items()}
    pieces["dt"] = _to_group_major(pieces["dt"])
    return jnp.concatenate([pieces[p] for p in CAT], axis=1)


def _dw_in_chunks(dw_main, dw_kv):
    dt = _from_group_major(dw_main[:, CAT_OFF["dt"]:CAT_OFF["dt"] + DT_PAD])
    shard = IN_OFFS[PIECES[-1]][1] // N_DEV

    def piece(pc, a, b):
        if pc == "dt":
            return dt[:, a:b]
        if pc == "kv":
            return dw_kv[:, a:b]
        return dw_main[:, CAT_OFF[pc] + a:CAT_OFF[pc] + b]

    chunks = []
    for d in range(N_DEV):
        parts = []
        for pc in PIECES:
            lo, hi = IN_OFFS[pc]
            a, b = max(lo, d * shard), min(hi, (d + 1) * shard)
            if a < b:
                parts.append(piece(pc, a - lo, b - lo))
        chunks.append(parts[0] if len(parts) == 1 else jnp.concatenate(parts, axis=1))
    return jnp.stack(chunks)


def _params_out(W):
    return {n: W[n] for n in OUT_PART}


def _params_in(l, W, sm):
    p = {"w_cat": _cat_w_in(W["w_in"])}
    cw = _cols(W["conv_w"], 0, SSD_D_INNER + 2 * BC_DIM)
    cb = sm["conv_b"][l]
    segs = {"xs": (0, 2048), "b": (2048, 2560), "c": (2560, 3072)}
    p["conv_w8"] = {s: _rows8(cw[:, lo:hi]) for s, (lo, hi) in segs.items()}
    p["conv_b8"] = {s: _rows8(cb[None, lo:hi]) for s, (lo, hi) in segs.items()}
    p["ssd_par"] = _rows8(jnp.stack([_to_group_major(sm["dt_bias"][l]), _to_group_major(sm["a_log"][l]),
                                     _to_group_major(sm["d_skip"][l])]))
    p["norm_w"] = sm["ssd_norm_w"][l]
    p["sinks8"] = _rows8(jnp.pad(sm["att_sinks"][l], (0, LANE - ATT_HEADS))[None])
    for n in ("ln_mix_g", "ln_mix_b", "ln_ffn_g", "ln_ffn_b"):
        p[n] = sm[n][l]
    return p


def _fwd_mixers(h0, p, l, dep=None):
    tag = f"l{l}_"
    a = {"h0": h0}
    proj = _mm(h0, p["w_cat"], "nn", tag + "proj", dep=dep)
    for pc in CAT:
        a[pc] = (proj, CAT_OFF[pc], CAT_WIDTH[pc])
    for s in ("xs", "b", "c"):
        a[s + "c"] = _conv_fwd(a[s], p["conv_w8"][s], p["conv_b8"][s], tag + "conv_" + s)
    a["y"], a["hs"] = _ssd_fwd(a["xsc"], a["bc"], a["cc"], a["dt"], p["ssd_par"], tag + "ssd_fwd")
    a["yn"] = _gnorm_fwd(a["y"], a["z"], p["norm_w"], tag + "gnorm")
    a["att"], a["att32"] = _att_fwd(a["q"], a["kv"], p["sinks8"], tag + "att_fwd")
    return a


def _fwd_out(a, p, l, dep=None):
    tag = f"l{l}_"
    h0 = a["h0"]
    a["ya"] = _mm(a["yn"], p["w_ssd_out"], "nn", tag + "ssd_out", dep=dep)
    a["yb"] = _mm(a["att"], p["w_att_out"], "nn", tag + "att_out", dep=dep)
    a["merged"] = _merge_fwd(a["gl"], a["ya"], a["yb"], tag + "merge")
    a["mix"] = _mm(a["merged"], p["w_mix_out"], "nn", tag + "mix_out")
    a["h1"] = _ln_fwd(h0, a["mix"], p["ln_mix_g"], p["ln_mix_b"], ALPHA, tag + "ln_mix")
    a["fg"], a["fu"], a["act"] = _ffn_in(a["h1"], p["w_ffn_gate"], p["w_ffn_up"], tag + "ffn_in")
    a["ffn"] = _mm(a["act"], p["w_ffn_down"], "nn", tag + "ffn_down")
    a["h2"] = _ln_fwd(a["h1"], a["ffn"], p["ln_ffn_g"], p["ln_ffn_b"], ALPHA, tag + "ln_ffn")
    return a


def _dw(x, dy, name, dep=None):
    return _mm(x, dy, "tn", name, out_dtype=BF16, dep=dep)


def _bwd_out(a, p, dh2, l, dep=None):
    tag = f"l{l}_b_"
    gw, gs = {}, {}
    du2, acc = _ln_bwd(a["h1"], a["ffn"], p["ln_ffn_g"], dh2, ALPHA, tag + "ln_ffn")
    gs["ln_ffn_g"], gs["ln_ffn_b"] = acc[0], acc[1]
    gw["w_ffn_down"] = _dw(a["act"], du2, tag + "dw_down", dep=dep)
    dfg, dfu = _ffn_dact(du2, p["w_ffn_down"], a["fg"], a["fu"], tag + "ffn_dact", dep=dep)
    gw["w_ffn_gate"] = _dw(dfg, a["h1"], tag + "dw_gate")
    gw["w_ffn_up"] = _dw(dfu, a["h1"], tag + "dw_up")
    dh1 = _mm(dfg, p["w_ffn_gate"], "nn", tag + "dh1_gate", add=du2, add_scale=ALPHA)
    dh1 = _mm(dfu, p["w_ffn_up"], "nn", tag + "dh1_up", add=dh1)
    du1, acc = _ln_bwd(a["h0"], a["mix"], p["ln_mix_g"], dh1, ALPHA, tag + "ln_mix")
    gs["ln_mix_g"], gs["ln_mix_b"] = acc[0], acc[1]
    gw["w_mix_out"] = _dw(a["merged"], du1, tag + "dw_mix")
    dmerged = _mm(du1, p["w_mix_out"], "nt", tag + "dmerged")
    dya, dyb, dproj = _merge_bwd(a["gl"], a["ya"], a["yb"], dmerged, tag + "merge",
                                 (None, CAT_OFF["gl"], MAIN_DIM))
    gw["w_ssd_out"] = _dw(a["yn"], dya, tag + "dw_ssd")
    gw["w_att_out"] = _dw(a["att"], dyb, tag + "dw_att")
    return {"du1": du1, "dya": dya, "dyb": dyb, "dproj": dproj}, gw, gs


def _bwd_mixers(a, p, carry, l, dep=None):
    tag = f"l{l}_b_"
    gs = {}
    du1, dproj = carry["du1"], carry["dproj"]

    def win(pc):
        return (dproj, CAT_OFF[pc], MAIN_DIM)

    dyn = _mm(carry["dya"], p["w_ssd_out"], "nt", tag + "dyn", dep=dep)
    datt = _mm(carry["dyb"], p["w_att_out"], "nt", tag + "datt", out_dtype=BF16, dep=dep)
    dproj, dkv, acc = _att_bwd(a["q"], a["kv"], p["sinks8"], a["att32"], datt, tag + "att", win("q"))
    gs["att_sinks"] = acc[0, :ATT_HEADS]
    dy, dproj, acc = _gnorm_bwd(a["y"], a["z"], p["norm_w"], dyn, tag + "gnorm", win("z"))
    gs["ssd_norm_w"] = acc[0]
    dxs, dbm, dcm, dproj, acc = _ssd_bwd(a["xsc"], a["bc"], a["cc"], a["dt"], p["ssd_par"], a["hs"], dy,
                                         tag + "ssd", win("dt"))
    gs["dt_bias"], gs["a_log"], gs["d_skip"] = (_from_group_major(acc[i]) for i in range(3))
    dconv_w, dconv_b = [], []
    for s, dout in (("xs", dxs), ("b", dbm), ("c", dcm)):
        dc, acc = _conv_bwd_pre(a[s], p["conv_w8"][s], p["conv_b8"][s], dout, tag + "conv_pre_" + s)
        dconv_w.append(acc[:CONV_TAPS])
        dconv_b.append(acc[CONV_TAPS])
        dproj = _conv_bwd_in(dc, p["conv_w8"][s], tag + "conv_in_" + s, win(s))
    gconv = jnp.concatenate(dconv_w, axis=1)
    gs["conv_b"] = jnp.concatenate(dconv_b)
    w_main, w_kv = p["w_cat"][:, :MAIN_DIM], p["w_cat"][:, MAIN_DIM:]
    dw_main, dw_kv = _dw(a["h0"], dproj, tag + "dw_in"), _dw(a["h0"], dkv, tag + "dw_in_kv")

    def grad_h0(dep=None):
        dh0 = _mm(dproj, w_main, "nt", tag + "dh0", add=du1, add_scale=ALPHA, dep=dep)
        return _mm(dkv, w_kv, "nt", tag + "dh0_kv", add=dh0)

    return grad_h0, _dw_in_chunks(dw_main, dw_kv), gconv, gs


def _step(x, target, w, m, v):
    x2 = x[0]
    t2 = target[0]
    tok = jnp.zeros(TOKEN.shape, TOKEN.dtype)

    ex = _exchange_start(_gather_items(w, IN_PART, 0), tok, "gather_l0_in_start")
    h = _ln_fwd(x2, None, w["ln_in_g"], w["ln_in_b"], 1.0, "ln_in")
    lands, tok = _exchange_wait(ex, h, "gather_l0_in_wait")
    p0 = _params_in(0, dict(zip(IN_PART, lands)), w)
    ex = _exchange_start(_gather_items(w, OUT_PART, 0) + _gather_items(w, IN_PART, 1), tok,
                         "gather_l0_out_l1_in_start")
    a0 = _fwd_mixers(h, p0, 0, dep=ex["token"])
    lands, tok = _exchange_wait(ex, a0["att"], "gather_l0_out_l1_in_wait")
    p0.update(_params_out(dict(zip(OUT_PART, lands))))
    p1 = _params_in(1, dict(zip(IN_PART, lands[len(OUT_PART):])), w)
    ex = _exchange_start(_gather_items(w, OUT_PART, 1), tok, "gather_l1_out_start")
    a0 = _fwd_out(a0, p0, 0, dep=ex["token"])
    lands, tok = _exchange_wait(ex, a0["h2"], "gather_l1_out_wait")
    p1.update(_params_out(dict(zip(OUT_PART, lands))))
    a1 = _fwd_out(_fwd_mixers(a0["h2"], p1, 1), p1, 1)

    sse, dh = _loss_fwd_bwd(a1["h2"], t2, "loss")
    loss = lax.psum(0.5 / D_MODEL * sse[0, 0], ("x", "y", "c"))

    carry, gw1, gs1 = _bwd_out(a1, p1, dh, 1)
    grad_h0, gw1["w_in"], gw1["conv_w"], gs = _bwd_mixers(a1, p1, carry, 1)
    dh = grad_h0()
    gs1.update(gs)
    ex1 = _exchange_start(_scatter_items(gw1, GATHERED), tok, "scatter_l1_start")
    carry, gw0, gs0 = _bwd_out(a0, p0, dh, 0, dep=ex1["token"])
    lands, tok = _exchange_wait(ex1, carry["dyb"], "scatter_l1_wait")
    land1 = dict(zip(GATHERED, lands))
    ex0 = _exchange_start(_scatter_items(gw0, OUT_PART), tok, "scatter_l0_out_start")
    grad_h0, gw0["w_in"], gw0["conv_w"], gs = _bwd_mixers(a0, p0, carry, 0, dep=ex0["token"])
    gs0.update(gs)
    lands, tok = _exchange_wait(ex0, gw0["w_in"], "scatter_l0_out_wait")
    land0 = dict(zip(OUT_PART, lands))
    ex0 = _exchange_start(_scatter_items(gw0, IN_PART), tok, "scatter_l0_in_start")
    dh = grad_h0(dep=ex0["token"])
    grad_x2, acc = _ln_bwd(x2, None, w["ln_in_g"], dh, 1.0, "ln_in_b")

    outs = [{} for _ in range(4)]

    def update(names):
        res = None
        for n in names:
            if n in TRANSPOSED:
                res = _adamw([land0[n], land1[n]], *(jnp.swapaxes(t, 1, 2) for t in (w[n], m[n], v[n])),
                             "adamw_" + n)
                res = tuple(jnp.swapaxes(t, 1, 2) for t in res)
            else:
                res = _adamw([land0[n], land1[n]], w[n], m[n], v[n], "adamw_" + n)
            for o, t in zip(outs, res):
                o[n] = t
        return res[1]

    update(OUT_PART)
    gsm = {"ln_in_g": acc[0], "ln_in_b": acc[1]}
    for n in SMALL[2:]:
        gsm[n] = jnp.stack([gs0[n], gs1[n]])
    small_shapes = {n: w[n].shape for n in SMALL}
    land_s = _all_gather_small(_pack_small(gsm), "small_grads_all_gather")
    res = _adamw([land_s], _pack_small(w)[None], _pack_small(m)[None], _pack_small(v)[None], "adamw_small")
    for o, t in zip(outs, res):
        o.update(_unpack_small(t[0], small_shapes))
    lands, _ = _exchange_wait(ex0, res[1], "scatter_l0_in_wait")
    land0.update(zip(IN_PART, lands))
    update(IN_PART)
    return loss, grad_x2[None], outs


WEIGHT_NAMES = ("ln_in_g", "ln_in_b", "w_in", "conv_w", "conv_b", "dt_bias", "a_log", "d_skip", "ssd_norm_w",
                "att_sinks", "w_ssd_out", "w_att_out", "w_mix_out", "ln_mix_g", "ln_mix_b", "w_ffn_gate",
                "w_ffn_up", "w_ffn_down", "ln_ffn_g", "ln_ffn_b")


def kernel(x, ln_in_g, ln_in_b, w_in, conv_w, conv_b, dt_bias, a_log, d_skip, ssd_norm_w, att_sinks, w_ssd_out, w_att_out, w_mix_out, ln_mix_g, ln_mix_b, w_ffn_gate, w_ffn_up, w_ffn_down, ln_ffn_g, ln_ffn_b, loss_target, m_ln_in_g, m_ln_in_b, m_w_in, m_conv_w, m_conv_b, m_dt_bias, m_a_log, m_d_skip, m_ssd_norm_w, m_att_sinks, m_w_ssd_out, m_w_att_out, m_w_mix_out, m_ln_mix_g, m_ln_mix_b, m_w_ffn_gate, m_w_ffn_up, m_w_ffn_down, m_ln_ffn_g, m_ln_ffn_b, v_ln_in_g, v_ln_in_b, v_w_in, v_conv_w, v_conv_b, v_dt_bias, v_a_log, v_d_skip, v_ssd_norm_w, v_att_sinks, v_w_ssd_out, v_w_att_out, v_w_mix_out, v_ln_mix_g, v_ln_mix_b, v_w_ffn_gate, v_w_ffn_up, v_w_ffn_down, v_ln_ffn_g, v_ln_ffn_b):
    w = dict(zip(WEIGHT_NAMES, (ln_in_g, ln_in_b, w_in, conv_w, conv_b, dt_bias, a_log, d_skip, ssd_norm_w,
                                att_sinks, w_ssd_out, w_att_out, w_mix_out, ln_mix_g, ln_mix_b, w_ffn_gate,
                                w_ffn_up, w_ffn_down, ln_ffn_g, ln_ffn_b)))
    m = dict(zip(WEIGHT_NAMES, (m_ln_in_g, m_ln_in_b, m_w_in, m_conv_w, m_conv_b, m_dt_bias, m_a_log, m_d_skip,
                                m_ssd_norm_w, m_att_sinks, m_w_ssd_out, m_w_att_out, m_w_mix_out, m_ln_mix_g,
                                m_ln_mix_b, m_w_ffn_gate, m_w_ffn_up, m_w_ffn_down, m_ln_ffn_g, m_ln_ffn_b)))
    v = dict(zip(WEIGHT_NAMES, (v_ln_in_g, v_ln_in_b, v_w_in, v_conv_w, v_conv_b, v_dt_bias, v_a_log, v_d_skip,
                                v_ssd_norm_w, v_att_sinks, v_w_ssd_out, v_w_att_out, v_w_mix_out, v_ln_mix_g,
                                v_ln_mix_b, v_w_ffn_gate, v_w_ffn_up, v_w_ffn_down, v_ln_ffn_g, v_ln_ffn_b)))
    loss, grad_x, outs = _step(x, loss_target, w, m, v)
    result = [loss, grad_x]
    for o in outs:
        result.extend(o[n] for n in WEIGHT_NAMES)
    return tuple(result)
```

```python
import math

import jax
import jax.numpy as jnp
from jax import lax
from jax.experimental import pallas as pl
from jax.experimental.pallas import tpu as pltpu

F32 = jnp.float32
BF16 = jnp.bfloat16

D_MODEL = 1024
DEPTH = 2
N_DEV = 8
ATT_HEADS = 16
ATT_KV_HEADS = 2
ATT_HEAD_DIM = 64
ATT_BLOCK = 128
SSD_D_INNER = 2048
SSD_HEADS = 32
SSD_GROUPS = 4
SSD_STATE = 128
SSD_CHUNK = 128
FFN_HIDDEN = 2816
LN_EPS = 1e-5
RMS_EPS = 1e-5
ALPHA = (2 * DEPTH) ** 0.25
Q_DIM = 1024
BC_DIM = 512
DT_PAD = 512

ADAM_LR = 0.001
ADAM_B1 = 0.9
ADAM_B2 = 0.999
ADAM_EPS = 1e-08
ADAM_WD = 0.01
ADAM_STEP = 10

LANE = 128
VMEM_LIMIT = 48 * 1024 * 1024
NEG = -1e30

_NN = (((1,), (0,)), ((), ()))
_NT = (((1,), (1,)), ((), ()))
_TN = (((0,), (0,)), ((), ()))
MESH_ID = pl.DeviceIdType.MESH


def _dot(a, b, dims=_NN):
    return lax.dot_general(a, b, dims, preferred_element_type=F32)


def _sig(x):
    return 1.0 / (1.0 + jnp.exp(-x))


def _softplus(x):
    return jnp.maximum(x, 0.0) + jnp.log(1.0 + jnp.exp(-jnp.abs(x)))


def _cparams(*sem):
    return pltpu.CompilerParams(dimension_semantics=sem, vmem_limit_bytes=VMEM_LIMIT)


def _pick(n, cap):
    if n <= cap:
        return n
    best = None
    for t in range(LANE, cap + 1, LANE):
        if n % t == 0:
            best = t
    assert best is not None, (n, cap)
    return best


def _tile(n):
    if n <= 1024 or n % 1024 == 0:
        return min(n, 1024)
    return _pick(n, 1408)


def _rows(n):
    return min(512, n)


def _window(x):
    return x if isinstance(x, tuple) else (x, 0, x.shape[1])


def _into(into, n_in, out_idx):
    buf, col0, width = into
    if buf is None:
        return [], [], {}, col0, width
    return [buf], [pl.BlockSpec(memory_space=pl.ANY)], {n_in: out_idx}, col0, width


def _mm(a, b, mode, name, add=None, add_scale=1.0, out_dtype=F32, dep=None):
    if mode == "nn":
        m, k = a.shape
        n = b.shape[1]
    elif mode == "nt":
        m, k = a.shape
        n = b.shape[0]
    else:
        k, m = a.shape
        n = b.shape[1]
    tm = _tile(m)
    tn = _pick(n, 2176) if mode == "tn" and n > 1024 else _tile(n)
    tk = _pick(k, 2176) if mode == "nt" and a.dtype == BF16 and k > 2816 else _tile(k)
    nk = k // tk
    has_add = add is not None
    dims = {"nn": _NN, "nt": _NT, "tn": _TN}[mode]

    def body(*refs):
        if dep is not None:
            refs = refs[:-3] + refs[-2:]
        if has_add:
            a_ref, b_ref, add_ref, o_ref, acc_ref = refs
        else:
            a_ref, b_ref, o_ref, acc_ref = refs
        kk = pl.program_id(2)

        @pl.when(kk == 0)
        def _():
            if has_add:
                acc_ref[...] = add_scale * add_ref[...].astype(F32)
            else:
                acc_ref[...] = jnp.zeros_like(acc_ref)

        acc_ref[...] += _dot(a_ref[...].astype(BF16), b_ref[...].astype(BF16), dims)

        @pl.when(kk == nk - 1)
        def _():
            o_ref[...] = acc_ref[...].astype(o_ref.dtype)

    if mode == "nn":
        a_spec = pl.BlockSpec((tm, tk), lambda i, j, kk: (i, kk))
        b_spec = pl.BlockSpec((tk, tn), lambda i, j, kk: (kk, j))
    elif mode == "nt":
        a_spec = pl.BlockSpec((tm, tk), lambda i, j, kk: (i, kk))
        b_spec = pl.BlockSpec((tn, tk), lambda i, j, kk: (j, kk))
    else:
        a_spec = pl.BlockSpec((tk, tm), lambda i, j, kk: (kk, i))
        b_spec = pl.BlockSpec((tk, tn), lambda i, j, kk: (kk, j))
    o_spec = pl.BlockSpec((tm, tn), lambda i, j, kk: (i, j))
    in_specs = [a_spec, b_spec] + ([o_spec] if has_add else [])
    args = (a, b) + ((add,) if has_add else ())
    if dep is not None:
        in_specs.append(pl.BlockSpec((8, LANE), lambda i, j, kk: (0, 0)))
        args += (dep,)
    return pl.pallas_call(
        body, name=name, grid=(m // tm, n // tn, nk),
        in_specs=in_specs, out_specs=o_spec,
        out_shape=jax.ShapeDtypeStruct((m, n), out_dtype),
        scratch_shapes=[pltpu.VMEM((tm, tn), F32)],
        compiler_params=_cparams("parallel", "parallel", "arbitrary"),
    )(*args)


def _vec_spec(width):
    return pl.BlockSpec((1, width), lambda i: (0, 0))


def _ln_fwd(a, b, gamma, beta, alpha, name):
    n_rows, dm = a.shape
    has_b = b is not None

    def body(*refs):
        if has_b:
            a_ref, b_ref, g_ref, be_ref, o_ref = refs
            u = alpha * a_ref[...] + b_ref[...]
        else:
            a_ref, g_ref, be_ref, o_ref = refs
            u = a_ref[...]
        mu = jnp.mean(u, axis=-1, keepdims=True)
        d = u - mu
        var = jnp.mean(d * d, axis=-1, keepdims=True)
        o_ref[...] = d * lax.rsqrt(var + LN_EPS) * g_ref[...] + be_ref[...]

    row = pl.BlockSpec((_rows(n_rows),dm), lambda i: (i, 0))
    in_specs = [row] + ([row] if has_b else []) + [_vec_spec(dm), _vec_spec(dm)]
    args = (a,) + ((b,) if has_b else ()) + (gamma.reshape(1, dm), beta.reshape(1, dm))
    return pl.pallas_call(
        body, name=name, grid=(n_rows // _rows(n_rows),), in_specs=in_specs, out_specs=row,
        out_shape=jax.ShapeDtypeStruct((n_rows, dm), F32),
        compiler_params=_cparams("parallel"),
    )(*args)


def _ln_bwd(a, b, gamma, dy, alpha, name):
    n_rows, dm = a.shape
    has_b = b is not None

    def body(*refs):
        if has_b:
            a_ref, b_ref, g_ref, dy_ref, du_ref, acc_ref = refs
            u = alpha * a_ref[...] + b_ref[...]
        else:
            a_ref, g_ref, dy_ref, du_ref, acc_ref = refs
            u = a_ref[...]

        @pl.when(pl.program_id(0) == 0)
        def _():
            acc_ref[...] = jnp.zeros_like(acc_ref)

        mu = jnp.mean(u, axis=-1, keepdims=True)
        d = u - mu
        var = jnp.mean(d * d, axis=-1, keepdims=True)
        rstd = lax.rsqrt(var + LN_EPS)
        xhat = d * rstd
        dyv = dy_ref[...]
        acc_ref[0:1, :] += jnp.sum(dyv * xhat, axis=0, keepdims=True)
        acc_ref[1:2, :] += jnp.sum(dyv, axis=0, keepdims=True)
        dxh = dyv * g_ref[...]
        m1 = jnp.mean(dxh, axis=-1, keepdims=True)
        m2 = jnp.mean(dxh * xhat, axis=-1, keepdims=True)
        du_ref[...] = rstd * (dxh - m1 - xhat * m2)

    row = pl.BlockSpec((_rows(n_rows),dm), lambda i: (i, 0))
    in_specs = [row] + ([row] if has_b else []) + [_vec_spec(dm), row]
    args = (a,) + ((b,) if has_b else ()) + (gamma.reshape(1, dm), dy)
    return pl.pallas_call(
        body, name=name, grid=(n_rows // _rows(n_rows),), in_specs=in_specs,
        out_specs=(row, pl.BlockSpec((8, dm), lambda i: (0, 0))),
        out_shape=(jax.ShapeDtypeStruct((n_rows, dm), F32), jax.ShapeDtypeStruct((8, dm), F32)),
        compiler_params=_cparams("arbitrary"),
    )(*args)


def _loss_fwd_bwd(y, target, name):
    n_rows, dm = y.shape

    def body(y_ref, t_ref, acc_ref, dy_ref):
        @pl.when(pl.program_id(0) == 0)
        def _():
            acc_ref[...] = jnp.zeros_like(acc_ref)

        d = y_ref[...] - t_ref[...]
        acc_ref[...] += jnp.sum(d * d)
        dy_ref[...] = d * (1.0 / dm)

    row = pl.BlockSpec((_rows(n_rows),dm), lambda i: (i, 0))
    return pl.pallas_call(
        body, name=name, grid=(n_rows // _rows(n_rows),), in_specs=[row, row],
        out_specs=(pl.BlockSpec((8, LANE), lambda i: (0, 0)), row),
        out_shape=(jax.ShapeDtypeStruct((8, LANE), F32), jax.ShapeDtypeStruct((n_rows, dm), F32)),
        compiler_params=_cparams("arbitrary"),
    )(y, target)


FFN_ROWS = 512


def _ffn_in(h, wg, wu, name, dep=None):
    m, k = h.shape
    n = wg.shape[0]
    tm, tn = min(FFN_ROWS, m), _tile(n)

    def body(*refs):
        h_ref, wg_ref, wu_ref = refs[:3]
        g_ref, u_ref, act_ref = refs[-3:]
        hb = h_ref[...].astype(BF16)
        g = _dot(hb, wg_ref[...], _NT)
        u = _dot(hb, wu_ref[...], _NT)
        g_ref[...] = g
        u_ref[...] = u
        act_ref[...] = (g * _sig(g) * u).astype(BF16)

    rows = pl.BlockSpec((tm, k), lambda j, i: (i, 0))
    wrow = pl.BlockSpec((tn, k), lambda j, i: (j, 0))
    out = pl.BlockSpec((tm, tn), lambda j, i: (i, j))
    in_specs, args = [rows, wrow, wrow], (h, wg, wu)
    if dep is not None:
        in_specs.append(pl.BlockSpec((8, LANE), lambda j, i: (0, 0)))
        args += (dep,)
    return pl.pallas_call(
        body, name=name, grid=(n // tn, m // tm), in_specs=in_specs, out_specs=(out, out, out),
        out_shape=(jax.ShapeDtypeStruct((m, n), F32), jax.ShapeDtypeStruct((m, n), F32),
                   jax.ShapeDtypeStruct((m, n), BF16)),
        compiler_params=_cparams("parallel", "parallel"),
    )(*args)


def _ffn_dact(dy, wd, g, u, name, dep=None):
    m, k = dy.shape
    n = wd.shape[0]
    tm, tn = min(FFN_ROWS, m), _tile(n)

    def body(*refs):
        dy_ref, wd_ref, g_ref, u_ref = refs[:4]
        dg_ref, du_ref = refs[-2:]
        da = _dot(dy_ref[...].astype(BF16), wd_ref[...], _NT)
        gv = g_ref[...]
        s = _sig(gv)
        dg_ref[...] = (da * u_ref[...] * (s * (1.0 + gv * (1.0 - s)))).astype(BF16)
        du_ref[...] = (da * gv * s).astype(BF16)

    rows = pl.BlockSpec((tm, k), lambda j, i: (i, 0))
    wrow = pl.BlockSpec((tn, k), lambda j, i: (j, 0))
    out = pl.BlockSpec((tm, tn), lambda j, i: (i, j))
    in_specs, args = [rows, wrow, out, out], (dy, wd, g, u)
    if dep is not None:
        in_specs.append(pl.BlockSpec((8, LANE), lambda j, i: (0, 0)))
        args += (dep,)
    return pl.pallas_call(
        body, name=name, grid=(n // tn, m // tm), in_specs=in_specs, out_specs=(out, out),
        out_shape=(jax.ShapeDtypeStruct((m, n), BF16), jax.ShapeDtypeStruct((m, n), BF16)),
        compiler_params=_cparams("parallel", "parallel"),
    )(*args)


def _gate_specs(gl, n_rows, dm):
    arr, g0, _ = _window(gl)
    return arr, [pl.BlockSpec((_rows(n_rows), dm), lambda i, k=k: (i, g0 // dm + k)) for k in range(2)]


def _branch_out(yn, att, w_ssd, w_att, gl, name, dep=None):
    n_rows, dm = yn.shape[0], w_ssd.shape[1]
    gl_arr, gspecs = _gate_specs(gl, n_rows, dm)

    def body(*refs):
        ga_ref, gb_ref, yn_ref, att_ref, ws_ref, wa_ref = refs[:6]
        ya_ref, yb_ref, o_ref = refs[-3:]
        ya = _dot(yn_ref[...], ws_ref[...])
        yb = _dot(att_ref[...], wa_ref[...])
        ya_ref[...] = ya
        yb_ref[...] = yb
        o_ref[...] = (_sig(ga_ref[...]) * ya + _sig(gb_ref[...]) * yb).astype(BF16)

    tm = _rows(n_rows)
    row = pl.BlockSpec((tm, dm), lambda i: (i, 0))
    in_specs = gspecs + [pl.BlockSpec((tm, yn.shape[1]), lambda i: (i, 0)), row,
                         pl.BlockSpec(w_ssd.shape, lambda i: (0, 0)), pl.BlockSpec(w_att.shape, lambda i: (0, 0))]
    args = (gl_arr, gl_arr, yn, att, w_ssd, w_att)
    if dep is not None:
        in_specs.append(pl.BlockSpec((8, LANE), lambda i: (0, 0)))
        args += (dep,)
    return pl.pallas_call(
        body, name=name, grid=(n_rows // tm,), in_specs=in_specs, out_specs=(row, row, row),
        out_shape=(jax.ShapeDtypeStruct((n_rows, dm), F32), jax.ShapeDtypeStruct((n_rows, dm), F32),
                   jax.ShapeDtypeStruct((n_rows, dm), BF16)),
        compiler_params=_cparams("parallel"),
    )(*args)


def _merge_bwd(gl, ya, yb, dmerged, name, into):
    n_rows, dm = ya.shape
    gl_arr, gspecs = _gate_specs(gl, n_rows, dm)
    extra, extra_specs, aliases, col0, width = _into(into, 5, 2)

    def body(*refs):
        ga_ref, gb_ref, ya_ref, yb_ref, dm_ref = refs[:5]
        dya_ref, dyb_ref, dgl_ref = refs[-3:]
        ga = _sig(ga_ref[...])
        gb = _sig(gb_ref[...])
        dmv = dm_ref[...]
        dya_ref[...] = (dmv * ga).astype(BF16)
        dyb_ref[...] = (dmv * gb).astype(BF16)
        dgl_ref[:, :dm] = (dmv * ya_ref[...] * ga * (1.0 - ga)).astype(BF16)
        dgl_ref[:, dm:] = (dmv * yb_ref[...] * gb * (1.0 - gb)).astype(BF16)

    row = pl.BlockSpec((_rows(n_rows),dm), lambda i: (i, 0))
    row2 = pl.BlockSpec((_rows(n_rows),2 * dm), lambda i: (i, col0 // (2 * dm)))
    return pl.pallas_call(
        body, name=name, grid=(n_rows // _rows(n_rows),), in_specs=gspecs + [row, row, row] + extra_specs,
        out_specs=(row, row, row2),
        out_shape=(jax.ShapeDtypeStruct((n_rows, dm), BF16), jax.ShapeDtypeStruct((n_rows, dm), BF16),
                   jax.ShapeDtypeStruct((n_rows, width), BF16)),
        input_output_aliases=aliases,
        compiler_params=_cparams("parallel"),
    )(gl_arr, gl_arr, ya, yb, dmerged, *extra)


CONV_TAPS = 4
CONV_COLS = 512
HALO = 8


def _shift_down(cur, prev8, s, row8):
    r = pltpu.roll(cur, s, axis=0)
    top = jnp.where(row8 < s, pltpu.roll(prev8, s, axis=0), r[0:HALO])
    return jnp.concatenate([top, r[HALO:]], axis=0)


def _shift_up(cur, next8, s, row8):
    n = cur.shape[0]
    r = pltpu.roll(cur, n - s, axis=0)
    bot = jnp.where(row8 >= HALO - s, pltpu.roll(next8, HALO - s, axis=0), r[n - HALO:])
    return jnp.concatenate([r[:n - HALO], bot], axis=0)


def _conv_pre(u_ref, prev_ref, w_ref, b_ref, li):
    cur = u_ref[...]
    prev8 = jnp.where(li == 0, 0.0, prev_ref[...])
    row8 = lax.broadcasted_iota(jnp.int32, prev8.shape, 0)
    shifted = [cur] + [_shift_down(cur, prev8, s, row8) for s in range(1, CONV_TAPS)]
    acc = b_ref[...] + shifted[0] * w_ref[CONV_TAPS - 1:CONV_TAPS, :]
    for s in range(1, CONV_TAPS):
        acc = acc + shifted[s] * w_ref[CONV_TAPS - 1 - s:CONV_TAPS - s, :]
    return acc, shifted


def _conv_specs(n_rows, tl, col0=0):
    off = col0 // CONV_COLS
    cur = pl.BlockSpec((tl, CONV_COLS), lambda cj, li: (li, cj + off))
    prev = pl.BlockSpec((HALO, CONV_COLS), lambda cj, li: (jnp.maximum(li * (tl // HALO) - 1, 0), cj + off))
    nxt = pl.BlockSpec((HALO, CONV_COLS),
                       lambda cj, li: (jnp.minimum((li + 1) * (tl // HALO), n_rows // HALO - 1), cj + off))
    par = pl.BlockSpec((8, CONV_COLS), lambda cj, li: (0, cj + off))
    return cur, prev, nxt, par


def _conv_fwd(u, w8, b8, name):
    u, u0, c = _window(u)
    n_rows = u.shape[0]
    tl = _rows(n_rows)
    cur, _, _, par = _conv_specs(n_rows, tl)
    ucur, prev, _, _ = _conv_specs(n_rows, tl, u0)

    def body(u_ref, prev_ref, w_ref, b_ref, o_ref):
        acc, _ = _conv_pre(u_ref, prev_ref, w_ref, b_ref[0:1, :], pl.program_id(1))
        o_ref[...] = acc * _sig(acc)

    return pl.pallas_call(
        body, name=name, grid=(c // CONV_COLS, n_rows // tl), in_specs=[ucur, prev, par, par], out_specs=cur,
        out_shape=jax.ShapeDtypeStruct((n_rows, c), F32),
        compiler_params=_cparams("parallel", "parallel"),
    )(u, u, w8, b8)


def _conv_bwd_pre(u, w8, b8, dout, name):
    u, u0, c = _window(u)
    n_rows = u.shape[0]
    tl = _rows(n_rows)
    cur, _, _, par = _conv_specs(n_rows, tl)
    ucur, prev, _, _ = _conv_specs(n_rows, tl, u0)

    def body(u_ref, prev_ref, w_ref, b_ref, do_ref, dc_ref, acc_ref):
        @pl.when(pl.program_id(1) == 0)
        def _():
            acc_ref[...] = jnp.zeros_like(acc_ref)

        acc, shifted = _conv_pre(u_ref, prev_ref, w_ref, b_ref[0:1, :], pl.program_id(1))
        sg = _sig(acc)
        dc = do_ref[...] * (sg * (1.0 + acc * (1.0 - sg)))
        dc_ref[...] = dc
        for k in range(CONV_TAPS):
            acc_ref[k:k + 1, :] += jnp.sum(dc * shifted[CONV_TAPS - 1 - k], axis=0, keepdims=True)
        acc_ref[CONV_TAPS:CONV_TAPS + 1, :] += jnp.sum(dc, axis=0, keepdims=True)

    return pl.pallas_call(
        body, name=name, grid=(c // CONV_COLS, n_rows // tl), in_specs=[ucur, prev, par, par, cur],
        out_specs=(cur, par),
        out_shape=(jax.ShapeDtypeStruct((n_rows, c), F32), jax.ShapeDtypeStruct((8, c), F32)),
        compiler_params=_cparams("parallel", "arbitrary"),
    )(u, u, w8, b8, dout)


def _conv_bwd_in(dc, w8, name, into):
    n_rows, c = dc.shape
    tl = _rows(n_rows)
    cur, _, nxt, par = _conv_specs(n_rows, tl)
    n_l = n_rows // tl
    extra, extra_specs, aliases, col0, width = _into(into, 3, 0)
    out_spec = _conv_specs(n_rows, tl, col0)[0]

    def body(*refs):
        dc_ref, next_ref, w_ref = refs[:3]
        o_ref = refs[-1]
        cur_v = dc_ref[...]
        next8 = jnp.where(pl.program_id(1) == n_l - 1, 0.0, next_ref[...])
        row8 = lax.broadcasted_iota(jnp.int32, next8.shape, 0)
        acc = cur_v * w_ref[CONV_TAPS - 1:CONV_TAPS, :]
        for s in range(1, CONV_TAPS):
            acc = acc + _shift_up(cur_v, next8, s, row8) * w_ref[CONV_TAPS - 1 - s:CONV_TAPS - s, :]
        o_ref[...] = acc.astype(BF16)

    return pl.pallas_call(
        body, name=name, grid=(c // CONV_COLS, n_l), in_specs=[cur, nxt, par] + extra_specs, out_specs=out_spec,
        out_shape=jax.ShapeDtypeStruct((n_rows, width), BF16), input_output_aliases=aliases,
        compiler_params=_cparams("parallel", "parallel"),
    )(dc, dc, w8, *extra)


NORM_GROUP = SSD_D_INNER // SSD_GROUPS


def _gnorm_fwd(y, z, w, name):
    n_rows, c = y.shape
    z, z0, _ = _window(z)
    zoff = z0 // NORM_GROUP

    def body(y_ref, z_ref, w_ref, o_ref):
        zv = z_ref[...]
        yg = y_ref[...] * (zv * _sig(zv))
        r = lax.rsqrt(jnp.mean(yg * yg, axis=-1, keepdims=True) + RMS_EPS)
        o_ref[...] = (yg * r * w_ref[...]).astype(BF16)

    blk = pl.BlockSpec((_rows(n_rows),NORM_GROUP), lambda i, j: (i, j))
    zblk = pl.BlockSpec((_rows(n_rows),NORM_GROUP), lambda i, j: (i, j + zoff))
    wspec = pl.BlockSpec((1, NORM_GROUP), lambda i, j: (0, j))
    return pl.pallas_call(
        body, name=name, grid=(n_rows // _rows(n_rows), c // NORM_GROUP), in_specs=[blk, zblk, wspec], out_specs=blk,
        out_shape=jax.ShapeDtypeStruct((n_rows, c), BF16),
        compiler_params=_cparams("parallel", "parallel"),
    )(y, z, w.reshape(1, c))


def _gnorm_bwd(y, z, w, dyn, name, into):
    n_rows, c = y.shape
    z, z0, _ = _window(z)
    zoff = z0 // NORM_GROUP
    extra, extra_specs, aliases, col0, width = _into(into, 4, 1)
    doff = col0 // NORM_GROUP

    def body(*refs):
        y_ref, z_ref, w_ref, dn_ref = refs[:4]
        dy_ref, dz_ref, acc_ref = refs[-3:]
        @pl.when(pl.program_id(1) == 0)
        def _():
            acc_ref[...] = jnp.zeros_like(acc_ref)

        zv = z_ref[...]
        yv = y_ref[...]
        sz = _sig(zv)
        silu = zv * sz
        yg = yv * silu
        r = lax.rsqrt(jnp.mean(yg * yg, axis=-1, keepdims=True) + RMS_EPS)
        nrm = yg * r
        dn = dn_ref[...]
        acc_ref[0:1, :] += jnp.sum(dn * nrm, axis=0, keepdims=True)
        dnw = dn * w_ref[...]
        dyg = r * (dnw - nrm * jnp.mean(dnw * nrm, axis=-1, keepdims=True))
        dy_ref[...] = dyg * silu
        dz_ref[...] = (dyg * yv * (sz * (1.0 + zv * (1.0 - sz)))).astype(BF16)

    blk = pl.BlockSpec((_rows(n_rows),NORM_GROUP), lambda j, i: (i, j))
    zblk = pl.BlockSpec((_rows(n_rows),NORM_GROUP), lambda j, i: (i, j + zoff))
    wspec = pl.BlockSpec((1, NORM_GROUP), lambda j, i: (0, j))
    aspec = pl.BlockSpec((8, NORM_GROUP), lambda j, i: (0, j))
    return pl.pallas_call(
        body, name=name, grid=(c // NORM_GROUP, n_rows // _rows(n_rows)),
        in_specs=[blk, zblk, wspec, blk] + extra_specs,
        out_specs=(blk, pl.BlockSpec((_rows(n_rows), NORM_GROUP), lambda j, i: (i, j + doff)), aspec),
        out_shape=(jax.ShapeDtypeStruct((n_rows, c), F32), jax.ShapeDtypeStruct((n_rows, width), BF16),
                   jax.ShapeDtypeStruct((8, c), F32)),
        input_output_aliases=aliases,
        compiler_params=_cparams("parallel", "arbitrary"),
    )(y, z, w.reshape(1, c), dyn, *extra)


ATT_SCALE = ATT_HEAD_DIM ** -0.5
ATT_SLOPES = [2.0 ** (-8.0 * (h + 1) / ATT_HEADS) for h in range(ATT_HEADS)]
Q_PER_KV = ATT_HEADS // ATT_KV_HEADS


def _dup_half(t, g, lo):
    tr = pltpu.roll(t, ATT_HEAD_DIM, axis=1)
    return jnp.where(lo, t, tr) if g == 0 else jnp.where(lo, tr, t)


def _att_band(kv_ref, kvp_ref, n):
    cur = kv_ref[...]
    prev = jnp.where(n == 0, 0.0, kvp_ref[...])
    lo = lax.broadcasted_iota(jnp.int32, (ATT_BLOCK, LANE), 1) < ATT_HEAD_DIM
    bands = []
    for g in range(ATT_KV_HEADS):
        kb = jnp.concatenate([_dup_half(prev[:, :LANE], g, lo), _dup_half(cur[:, :LANE], g, lo)], axis=0)
        vb = jnp.concatenate([_dup_half(prev[:, LANE:], g, lo), _dup_half(cur[:, LANE:], g, lo)], axis=0)
        bands.append((kb.astype(BF16), vb.astype(BF16)))
    return bands


def _att_tile(n):
    shape = (2 * ATT_BLOCK, ATT_BLOCK)
    row = lax.broadcasted_iota(jnp.int32, shape, 0)
    i = row & (ATT_BLOCK - 1)
    s = lax.broadcasted_iota(jnp.int32, shape, 1)
    upper = s > i
    dist = ((i - s) & (ATT_BLOCK - 1)).astype(F32)
    dead = upper & (n == 0)
    return upper, dist, dead, row[:, 0:1] < ATT_BLOCK


def _stack_pair(t, lo):
    return jnp.concatenate([jnp.where(lo, t, 0.0), jnp.where(lo, 0.0, t)], axis=0).astype(BF16)


def _att_exp(qs, kb, s_ref, j, tile):
    upper, dist, dead, first = tile
    s2 = _dot(qs, kb, _NT)
    slope = jnp.where(first, ATT_SLOPES[2 * j], ATT_SLOPES[2 * j + 1])
    sink = jnp.where(first, s_ref[0:1, 2 * j:2 * j + 1], s_ref[0:1, 2 * j + 1:2 * j + 2])
    s = jnp.where(upper, s2[:, :ATT_BLOCK], s2[:, ATT_BLOCK:]) - slope * dist
    s = jnp.where(dead, NEG, s)
    m = jnp.maximum(jnp.max(s, axis=-1, keepdims=True), sink)
    return jnp.exp(s - m), jnp.exp(sink - m)


def _band_split(t, upper):
    return jnp.concatenate([jnp.where(upper, t, 0.0), jnp.where(upper, 0.0, t)], axis=1)


def _att_fwd(q, kv, sinks8, name):
    q, q0, _ = _window(q)
    kv, kv0, _ = _window(kv)
    qoff, kvoff = q0 // Q_DIM, kv0 // (2 * LANE)
    n_rows = q.shape[0]
    nb = n_rows // ATT_BLOCK

    def body(q_ref, kv_ref, kvp_ref, s_ref, o_ref, o32_ref):
        n = pl.program_id(0)
        bands = _att_band(kv_ref, kvp_ref, n)
        lo = lax.broadcasted_iota(jnp.int32, (ATT_BLOCK, LANE), 1) < ATT_HEAD_DIM
        tile = _att_tile(n)
        ones_b = jnp.ones((2 * ATT_BLOCK, LANE), BF16)
        for j in range(ATT_HEADS // 2):
            kb, vb = bands[2 * j // Q_PER_KV]
            qs = _stack_pair(q_ref[:, j * LANE:(j + 1) * LANE] * ATT_SCALE, lo)
            p, es = _att_exp(qs, kb, s_ref, j, tile)
            pv = _dot(_band_split(p, tile[0]).astype(BF16), jnp.concatenate([vb, ones_b], axis=1))
            out = pv[:, :LANE] / (pv[:, LANE:] + es)
            out = jnp.where(lo, out[:ATT_BLOCK], out[ATT_BLOCK:])
            o_ref[:, j * LANE:(j + 1) * LANE] = out.astype(BF16)
            o32_ref[:, j * LANE:(j + 1) * LANE] = out

    return pl.pallas_call(
        body, name=name, grid=(nb,),
        in_specs=[pl.BlockSpec((ATT_BLOCK, Q_DIM), lambda n: (n, qoff)),
                  pl.BlockSpec((ATT_BLOCK, 2 * LANE), lambda n: (n, kvoff)),
                  pl.BlockSpec((ATT_BLOCK, 2 * LANE), lambda n: (jnp.maximum(n - 1, 0), kvoff)),
                  pl.BlockSpec((8, LANE), lambda n: (0, 0))],
        out_specs=(pl.BlockSpec((ATT_BLOCK, Q_DIM), lambda n: (n, 0)),) * 2,
        out_shape=(jax.ShapeDtypeStruct((n_rows, Q_DIM), BF16), jax.ShapeDtypeStruct((n_rows, Q_DIM), F32)),
        compiler_params=_cparams("parallel"),
    )(q, kv, kv, sinks8)


def _att_bwd(q, kv, sinks8, out32, dout, name, into):
    q, q0, _ = _window(q)
    kv, kv0, _ = _window(kv)
    qoff, kvoff = q0 // Q_DIM, kv0 // (2 * LANE)
    n_rows = q.shape[0]
    nb = n_rows // ATT_BLOCK

    extra, extra_specs, aliases, col0, width = _into(into, 6, 0)
    dqoff = col0 // Q_DIM

    def body(*refs):
        q_ref, kv_ref, kvp_ref, s_ref, o_ref, do_ref = refs[:6]
        dq_ref, dkv_ref, acc_ref, carry_ref = refs[-4:]
        n = pl.program_id(0)

        @pl.when(n == 0)
        def _():
            acc_ref[...] = jnp.zeros_like(acc_ref)
            carry_ref[...] = jnp.zeros_like(carry_ref)

        @pl.when(n == nb)
        def _():
            dkv_ref[...] = carry_ref[...].astype(BF16)

        @pl.when(n < nb)
        def _():
            bands = _att_band(kv_ref, kvp_ref, n)
            lo = lax.broadcasted_iota(jnp.int32, (ATT_BLOCK, LANE), 1) < ATT_HEAD_DIM
            lane1 = lax.broadcasted_iota(jnp.int32, (1, LANE), 1)
            tile = _att_tile(n)
            upper, first = tile[0], tile[3]
            ones_b = jnp.ones((ATT_BLOCK, LANE), BF16)
            ones2_b = jnp.ones((2 * LANE, LANE), BF16)
            dk_acc = [jnp.zeros((2 * ATT_BLOCK, LANE), F32) for _ in range(ATT_KV_HEADS)]
            dv_acc = [jnp.zeros((2 * ATT_BLOCK, LANE), F32) for _ in range(ATT_KV_HEADS)]
            dsink = jnp.zeros((1, LANE), F32)
            for j in range(ATT_HEADS // 2):
                g = 2 * j // Q_PER_KV
                kb, vb = bands[g]
                qs = _stack_pair(q_ref[:, j * LANE:(j + 1) * LANE] * ATT_SCALE, lo)
                dop = do_ref[:, j * LANE:(j + 1) * LANE].astype(F32)
                dos = _stack_pair(dop, lo)
                pu, es = _att_exp(qs, kb, s_ref, j, tile)
                inv = 1.0 / (_dot(pu.astype(BF16), ones_b) + es)
                p = pu * inv
                od = dos.astype(F32) * jnp.concatenate([o_ref[:, j * LANE:(j + 1) * LANE]] * 2, axis=0)
                od_hi = od.astype(BF16)
                delta = _dot(jnp.concatenate([od_hi, (od - od_hi.astype(F32)).astype(BF16)], axis=1), ones2_b)
                dp2 = _dot(dos, vb, _NT)
                dp = jnp.where(upper, dp2[:, :ATT_BLOCK], dp2[:, ATT_BLOCK:])
                ds2 = _band_split(p * (dp - delta), upper)
                psd = jnp.sum(es * inv * delta, axis=0, keepdims=True)
                psd0 = jnp.sum(jnp.where(first, es * inv * delta, 0.0), axis=0, keepdims=True)
                dsink = jnp.where(lane1 == 2 * j, -psd0, jnp.where(lane1 == 2 * j + 1, psd0 - psd, dsink))
                ds2_b = ds2.astype(BF16)
                dq = _dot(ds2_b, kb) * ATT_SCALE
                dq_ref[:, j * LANE:(j + 1) * LANE] = jnp.where(lo, dq[:ATT_BLOCK], dq[ATT_BLOCK:]).astype(BF16)
                dk_acc[g] = dk_acc[g] + _dot(ds2_b, qs, _TN)
                dv_acc[g] = dv_acc[g] + _dot(_band_split(p, upper).astype(BF16), dos, _TN)
            acc_ref[0:1, :] += dsink
            lo2 = lax.broadcasted_iota(jnp.int32, (2 * ATT_BLOCK, LANE), 1) < ATT_HEAD_DIM
            folded = []
            for acc in (dk_acc, dv_acc):
                t0 = acc[0] + pltpu.roll(acc[0], ATT_HEAD_DIM, axis=1)
                t1 = acc[1] + pltpu.roll(acc[1], ATT_HEAD_DIM, axis=1)
                folded.append(jnp.where(lo2, t0, t1))
            band = jnp.concatenate(folded, axis=1)
            dkv_ref[...] = (carry_ref[...] + band[:ATT_BLOCK]).astype(BF16)
            carry_ref[...] = band[ATT_BLOCK:]

    def qmap(n):
        return (jnp.minimum(n, nb - 1), 0)

    return pl.pallas_call(
        body, name=name, grid=(nb + 1,),
        in_specs=[pl.BlockSpec((ATT_BLOCK, Q_DIM), lambda n: (jnp.minimum(n, nb - 1), qoff)),
                  pl.BlockSpec((ATT_BLOCK, 2 * LANE), lambda n: (jnp.minimum(n, nb - 1), kvoff)),
                  pl.BlockSpec((ATT_BLOCK, 2 * LANE),
                               lambda n: (jnp.maximum(jnp.minimum(n, nb - 1) - 1, 0), kvoff)),
                  pl.BlockSpec((8, LANE), lambda n: (0, 0)),
                  pl.BlockSpec((ATT_BLOCK, Q_DIM), qmap),
                  pl.BlockSpec((ATT_BLOCK, Q_DIM), qmap)] + extra_specs,
        out_specs=(pl.BlockSpec((ATT_BLOCK, Q_DIM), lambda n: (jnp.minimum(n, nb - 1), dqoff)),
                   pl.BlockSpec((ATT_BLOCK, 2 * LANE), lambda n: (jnp.maximum(n - 1, 0), 0)),
                   pl.BlockSpec((8, LANE), lambda n: (0, 0))),
        out_shape=(jax.ShapeDtypeStruct((n_rows, width), BF16), jax.ShapeDtypeStruct((n_rows, 2 * LANE), BF16),
                   jax.ShapeDtypeStruct((8, LANE), F32)),
        input_output_aliases=aliases,
        scratch_shapes=[pltpu.VMEM((ATT_BLOCK, 2 * LANE), F32)],
        compiler_params=_cparams("arbitrary"),
    )(q, kv, kv, sinks8, out32, dout, *extra)


HEADS_PER_GROUP = SSD_HEADS // SSD_GROUPS
PAIRS_PER_GROUP = HEADS_PER_GROUP // 2
T = SSD_CHUNK


def _cumsum_mm(mat, x):
    hi = x.astype(BF16)
    r = x - hi.astype(F32)
    mid = r.astype(BF16)
    lo = (r - mid.astype(F32)).astype(BF16)
    w = x.shape[1]
    out = _dot(mat, jnp.concatenate([hi, mid, lo], axis=1))
    return out[:, :w] + out[:, w:2 * w] + out[:, 2 * w:]


def _ssd_prep(dtr_ref, par_ref):
    dt = _softplus(dtr_ref[...] + par_ref[0:1, :])
    a = -jnp.exp(par_ref[1:2, :])
    ri = lax.broadcasted_iota(jnp.int32, (T, T), 0)
    ci = lax.broadcasted_iota(jnp.int32, (T, T), 1)
    cs = _cumsum_mm((ri >= ci).astype(BF16), dt * a)
    lo = lax.broadcasted_iota(jnp.int32, (T, LANE), 1) < SSD_CHUNK // 2

    def expand(arr):
        rows = arr.shape[0]
        return jnp.concatenate([jnp.where(lo[:rows], arr[:, 2 * j:2 * j + 1], arr[:, 2 * j + 1:2 * j + 2])
                                for j in range(PAIRS_PER_GROUP)], axis=1)

    tot = cs[T - 1:T, :]
    return {"dt": dt, "a": a, "cs": cs, "cst": cs.T, "lo": lo, "ri": ri, "ci": ci, "expand": expand,
            "dt_x": expand(dt), "ecs_x": expand(jnp.exp(cs)), "dec_x": expand(jnp.exp(tot - cs)),
            "et_x": expand(jnp.exp(tot)), "etot": jnp.exp(tot), "dsk_x": expand(par_ref[2:3, :])}


def _wide_masks():
    r = lax.broadcasted_iota(jnp.int32, (T, 2 * T), 0)
    l = lax.broadcasted_iota(jnp.int32, (T, 2 * T), 1)
    s = l & (T - 1)
    return r >= s, s >= r, l < T


def _wide_cs(q, k0, even):
    cs, cst = q["cs"], q["cst"]
    col = jnp.where(even, cs[:, k0:k0 + 1], cs[:, k0 + 1:k0 + 2])
    row = jnp.concatenate([cst[k0:k0 + 1, :], cst[k0 + 1:k0 + 2, :]], axis=1)
    return col, row


def _ssd_fwd(xs, bm, cm, dtr, par, name):
    dtr, dt0, _ = _window(dtr)
    dtoff = dt0 // LANE
    n_rows = xs.shape[0]
    nc = n_rows // T
    gw = PAIRS_PER_GROUP * LANE

    def body(x_ref, b_ref, c_ref, dtr_ref, par_ref, y_ref, hs_ref, h_ref):
        @pl.when(pl.program_id(1) == 0)
        def _():
            h_ref[...] = jnp.zeros_like(h_ref)

        q = _ssd_prep(dtr_ref, par_ref)
        lo = q["lo"]
        tri_w, _, even = _wide_masks()
        bg_b = b_ref[...].astype(BF16)
        cg_b = c_ref[...].astype(BF16)
        xv = x_ref[...]
        xdt = xv * q["dt_x"]
        h = h_ref[...]
        hs_ref[0, 0] = h
        yo = q["ecs_x"] * _dot(cg_b, h.astype(BF16))
        h_ref[...] = h * q["et_x"] + _dot(bg_b, (xdt * q["dec_x"]).astype(BF16), _TN)
        cb = _dot(cg_b, bg_b, _NT)
        cb_w = jnp.concatenate([cb, cb], axis=1)
        for j in range(PAIRS_PER_GROUP):
            col, row = _wide_cs(q, 2 * j, even)
            m_w = (jnp.exp(jnp.where(tri_w, col - row, NEG)) * cb_w).astype(BF16)
            sl = slice(j * LANE, (j + 1) * LANE)
            y_ref[:, sl] = (_dot(m_w, _stack_pair(xdt[:, sl], lo)) + yo[:, sl] + q["dsk_x"][:, sl] * xv[:, sl])

    return pl.pallas_call(
        body, name=name, grid=(SSD_GROUPS, nc),
        in_specs=[pl.BlockSpec((T, gw), lambda g, c: (c, g)),
                  pl.BlockSpec((T, SSD_STATE), lambda g, c: (c, g)),
                  pl.BlockSpec((T, SSD_STATE), lambda g, c: (c, g)),
                  pl.BlockSpec((T, LANE), lambda g, c: (c, g + dtoff)),
                  pl.BlockSpec((8, LANE), lambda g, c: (0, g))],
        out_specs=(pl.BlockSpec((T, gw), lambda g, c: (c, g)),
                   pl.BlockSpec((1, 1, SSD_STATE, gw), lambda g, c: (g, c, 0, 0))),
        out_shape=(jax.ShapeDtypeStruct((n_rows, SSD_D_INNER), F32),
                   jax.ShapeDtypeStruct((SSD_GROUPS, nc, SSD_STATE, gw), F32)),
        scratch_shapes=[pltpu.VMEM((SSD_STATE, gw), F32)],
        compiler_params=_cparams("parallel", "arbitrary"),
    )(xs, bm, cm, dtr, par)


def _ssd_bwd(xs, bm, cm, dtr, par, hs, dy, name, into):
    dtr, dt0, _ = _window(dtr)
    dtoff = dt0 // LANE
    n_rows = xs.shape[0]
    nc = n_rows // T
    gw = PAIRS_PER_GROUP * LANE
    extra, extra_specs, aliases, col0, width = _into(into, 7, 3)
    ddoff = col0 // LANE

    def body(*refs):
        x_ref, b_ref, c_ref, dtr_ref, par_ref, hs_ref, dy_ref = refs[:7]
        dx_ref, db_ref, dc_ref, ddtr_ref, acc_ref, dh_ref = refs[-6:]

        @pl.when(pl.program_id(1) == 0)
        def _():
            dh_ref[...] = jnp.zeros_like(dh_ref)
            acc_ref[...] = jnp.zeros_like(acc_ref)

        q = _ssd_prep(dtr_ref, par_ref)
        lo, dt, a = q["lo"], q["dt"], q["a"]
        tri_w, trit_w, even = _wide_masks()
        lane = lax.broadcasted_iota(jnp.int32, (T, LANE), 1)
        lane1 = lane[0:1, :]
        last_row = lax.broadcasted_iota(jnp.int32, (T, 1), 0) == T - 1
        bg_b = b_ref[...].astype(BF16)
        cg_b = c_ref[...].astype(BF16)
        xv = x_ref[...]
        dyv = dy_ref[...]
        xdt = xv * q["dt_x"]
        h = hs_ref[0, 0]
        dhn = dh_ref[...]
        h_b, dhn_b = h.astype(BF16), dhn.astype(BF16)
        yo = q["ecs_x"] * _dot(cg_b, h_b)
        bdh = q["dec_x"] * _dot(bg_b, dhn_b)
        dye = (dyv * q["ecs_x"]).astype(BF16)
        xd = (xdt * q["dec_x"]).astype(BF16)
        dcg = _dot(dye, h_b, _NT)
        dbg = _dot(xd, dhn_b, _NT)
        dh_ref[...] = dhn * q["et_x"] + _dot(cg_b, dye, _TN)
        e4_all = xdt * bdh
        f_all = dyv * yo - e4_all
        tot_row = jnp.sum(e4_all, axis=0, keepdims=True) + q["et_x"] * jnp.sum(h * dhn, axis=0, keepdims=True)
        dsk_row = jnp.sum(dyv * xv, axis=0, keepdims=True)
        cb = _dot(cg_b, bg_b, _NT)
        cbt = _dot(bg_b, cg_b, _NT)
        cb_w = jnp.concatenate([cb, cb], axis=1)
        cbt_w = jnp.concatenate([cbt, cbt], axis=1)
        dcb = jnp.zeros((T, T), F32)
        dcbt = jnp.zeros((T, T), F32)
        dcs_acc = jnp.zeros((T, LANE), F32)
        ddt_acc = jnp.zeros((T, LANE), F32)
        dsk_acc = jnp.zeros((1, LANE), F32)
        tot_acc = jnp.zeros((1, LANE), F32)
        ind_r = lax.broadcasted_iota(jnp.int32, (2 * T, LANE), 0)
        ind_l = lax.broadcasted_iota(jnp.int32, (2 * T, LANE), 1)

        def halves(t):
            return (jnp.sum(jnp.where(lo[0:1], t, 0.0), axis=-1, keepdims=True),
                    jnp.sum(jnp.where(lo[0:1], 0.0, t), axis=-1, keepdims=True))

        def split2(t):
            hi = t.astype(BF16)
            return jnp.concatenate([hi, (t - hi.astype(F32)).astype(BF16)], axis=1)

        for j in range(PAIRS_PER_GROUP):
            k0, k1 = 2 * j, 2 * j + 1
            sl = slice(j * LANE, (j + 1) * LANE)
            col, row = _wide_cs(q, k0, even)
            lm_w = jnp.exp(jnp.where(tri_w, col - row, NEG))
            lmt_w = jnp.exp(jnp.where(trit_w, row - col, NEG))
            dyp, xp = dyv[:, sl], xdt[:, sl]
            dym, xm = _stack_pair(dyp, lo), _stack_pair(xp, lo)
            dm_w = _dot(dyp.astype(BF16), xm, _NT)
            dmt_w = _dot(xp.astype(BF16), dym, _NT)
            mm_w = lm_w * cb_w
            mmt_w = lmt_w * cbt_w
            dxdt = _dot(mmt_w.astype(BF16), dym) + bdh[:, sl]
            g1 = dm_w * lm_w
            g2 = dmt_w * lmt_w
            dcb = dcb + g1[:, :T] + g1[:, T:]
            dcbt = dcbt + g2[:, :T] + g2[:, T:]
            ind_w = jnp.where(ind_l == jnp.where(ind_r < T, k0, k1), 1.0, 0.0).astype(BF16)
            ind_p = jnp.where(ind_l[:T] == jnp.where(ind_r[:T] < SSD_CHUNK // 2, k0, k1), 1.0, 0.0).astype(BF16)
            dcs_acc = dcs_acc + _dot(
                jnp.concatenate([split2(dm_w * mm_w - dmt_w * mmt_w), split2(f_all[:, sl])], axis=1),
                jnp.concatenate([ind_w, ind_w, ind_p, ind_p], axis=0))
            ddt_acc = ddt_acc + _dot(split2(dxdt * xv[:, sl]), jnp.concatenate([ind_p, ind_p], axis=0))
            tot2 = halves(tot_row[:, sl])
            tot_acc = jnp.where(lane1 == k0, tot2[0], jnp.where(lane1 == k1, tot2[1], tot_acc))
            dsk2 = halves(dsk_row[:, sl])
            dsk_acc = jnp.where(lane1 == k0, dsk2[0], jnp.where(lane1 == k1, dsk2[1], dsk_acc))
            dx_ref[:, sl] = dxdt * q["dt_x"][:, sl] + q["dsk_x"][:, sl] * dyp
        dcs_acc = dcs_acc + jnp.where(last_row, tot_acc, 0.0)
        dc_ref[...] = dcg + _dot(dcb.astype(BF16), bg_b)
        db_ref[...] = dbg + _dot(dcbt.astype(BF16), cg_b)
        dda = _cumsum_mm((q["ci"] >= q["ri"]).astype(BF16), dcs_acc)
        ddt = ddt_acc + dda * a
        ddtr = ddt * _sig(dtr_ref[...] + par_ref[0:1, :])
        ddtr_ref[...] = ddtr.astype(BF16)
        acc_ref[0:1, :] += jnp.sum(ddtr, axis=0, keepdims=True)
        acc_ref[1:2, :] += jnp.sum(dda * dt, axis=0, keepdims=True) * a
        acc_ref[2:3, :] += dsk_acc

    def rev(g, c):
        return (nc - 1 - c, g)

    return pl.pallas_call(
        body, name=name, grid=(SSD_GROUPS, nc),
        in_specs=[pl.BlockSpec((T, gw), rev),
                  pl.BlockSpec((T, SSD_STATE), rev),
                  pl.BlockSpec((T, SSD_STATE), rev),
                  pl.BlockSpec((T, LANE), lambda g, c: (nc - 1 - c, g + dtoff)),
                  pl.BlockSpec((8, LANE), lambda g, c: (0, g)),
                  pl.BlockSpec((1, 1, SSD_STATE, gw), lambda g, c: (g, nc - 1 - c, 0, 0)),
                  pl.BlockSpec((T, gw), rev)] + extra_specs,
        out_specs=(pl.BlockSpec((T, gw), rev),
                   pl.BlockSpec((T, SSD_STATE), rev),
                   pl.BlockSpec((T, SSD_STATE), rev),
                   pl.BlockSpec((T, LANE), lambda g, c: (nc - 1 - c, g + ddoff)),
                   pl.BlockSpec((8, LANE), lambda g, c: (0, g))),
        out_shape=(jax.ShapeDtypeStruct((n_rows, SSD_D_INNER), F32),
                   jax.ShapeDtypeStruct((n_rows, BC_DIM), F32),
                   jax.ShapeDtypeStruct((n_rows, BC_DIM), F32),
                   jax.ShapeDtypeStruct((n_rows, width), BF16),
                   jax.ShapeDtypeStruct((8, DT_PAD), F32)),
        input_output_aliases=aliases,
        scratch_shapes=[pltpu.VMEM((SSD_STATE, gw), F32)],
        compiler_params=_cparams("parallel", "arbitrary"),
    )(xs, bm, cm, dtr, par, hs, dy, *extra)


ADAM_ROWS = 256


def _adamw(lands, w, m, v, name):
    na = len(lands)
    n_slots, r, wd = lands[0].shape
    tr = r if r <= 2 * ADAM_ROWS else ADAM_ROWS
    nj = r // tr
    bc1 = 1.0 - ADAM_B1 ** ADAM_STEP
    bc2 = 1.0 - ADAM_B2 ** ADAM_STEP

    def body(*refs):
        l_refs = refs[:na]
        w_ref, m_ref, v_ref, g_ref, d_ref, nm_ref, nv_ref = refs[na:]
        for a in range(na):
            @pl.when(pl.program_id(0) == a)
            def _(l_ref=l_refs[a]):
                g = l_ref[0].astype(F32)
                for s in range(1, n_slots):
                    g = g + l_ref[s].astype(F32)
                mn = ADAM_B1 * m_ref[0] + (1.0 - ADAM_B1) * g
                vn = ADAM_B2 * v_ref[0] + (1.0 - ADAM_B2) * (g * g)
                mh = mn / bc1
                vh = vn / bc2
                g_ref[0] = g
                nm_ref[0] = mn
                nv_ref[0] = vn
                d_ref[0] = -ADAM_LR * (mh / (jnp.sqrt(vh) + ADAM_EPS) + ADAM_WD * w_ref[0])

    def land_spec(a):
        return pl.BlockSpec((n_slots, tr, wd),
                            lambda i, j: (0, jnp.where(i == a, j, jnp.where(i < a, 0, nj - 1)), 0))

    blk = pl.BlockSpec((1, tr, wd), lambda i, j: (i, j, 0))
    shp = jax.ShapeDtypeStruct((na, r, wd), F32)
    return pl.pallas_call(
        body, name=name, grid=(na, nj), in_specs=[land_spec(a) for a in range(na)] + [blk, blk, blk],
        out_specs=(blk, blk, blk, blk), out_shape=(shp, shp, shp, shp),
        compiler_params=_cparams("arbitrary", "arbitrary"),
    )(*lands, w, m, v)


def _mesh_pos():
    return lax.axis_index("x"), lax.axis_index("y"), lax.axis_index("c")


def _peer(pos, k):
    x, y, c = pos
    px = 1 - x if (k >> 2) & 1 else x
    py = 1 - y if (k >> 1) & 1 else y
    pc = 1 - c if k & 1 else c
    return px, py, pc


def _flat(pos):
    return 4 * pos[0] + 2 * pos[1] + pos[2]


HBM_SPEC = pl.BlockSpec(memory_space=pl.ANY)


ROW_SHARDED = ("w_ssd_out", "w_att_out", "w_mix_out", "w_ffn_down")
COL_SHARDED = ("w_in", "w_ffn_gate", "w_ffn_up")
GATHERED = ROW_SHARDED + COL_SHARDED + ("conv_w",)


SEM_SPEC = pl.BlockSpec(memory_space=pltpu.SEMAPHORE)
TOKEN = jax.ShapeDtypeStruct((8, LANE), F32)
SPLIT_EFFECT = pltpu.SideEffectType.DATAFLOW_SIDE_EFFECTING
GATHER_ROWS = "gather_rows"
GATHER_SLOT = "gather_slot"
SCATTER_ROWS = "scatter_rows"
SCATTER_SLOT = "scatter_slot"


def _land_shape(kind, src):
    if kind == GATHER_ROWS:
        return (N_DEV * src.shape[0],) + src.shape[1:]
    if kind == GATHER_SLOT:
        return (N_DEV,) + src.shape
    if kind == SCATTER_ROWS:
        return (N_DEV, src.shape[0] // N_DEV) + src.shape[1:]
    return src.shape


def _views(kind, src_ref, land_ref, pos, k):
    me = _flat(pos)
    if kind == GATHER_ROWS:
        r = src_ref.shape[0]
        return src_ref, land_ref.at[pl.ds(pl.multiple_of(me * r, 16), r), :]
    if kind == GATHER_SLOT:
        return src_ref, land_ref.at[me]
    dev = _flat(_peer(pos, k))
    if kind == SCATTER_ROWS:
        r = land_ref.shape[1]
        return src_ref.at[pl.ds(pl.multiple_of(dev * r, 16), r), :], land_ref.at[k]
    return src_ref.at[dev], land_ref.at[k]


def _hbm(x):
    return pltpu.with_memory_space_constraint(x, pltpu.HBM)


def _exchange_start(items, after, name):
    kinds = [k for k, _ in items]
    srcs = [_hbm(s) for _, s in items]
    lands = [_hbm(lax.empty(_land_shape(k, s), s.dtype)) for k, s in items]
    n = len(items)
    n_copy = n * (N_DEV - 1)

    def body(*refs):
        src_refs, land_refs = refs[:n], refs[n:2 * n]
        send_sems, recv_sems = refs[2 * n + 1], refs[2 * n + 2]
        token_ref = refs[4 * n + 3]
        pos = _mesh_pos()
        for i, kind in enumerate(kinds):
            for k in range(1, N_DEV):
                s, d = _views(kind, src_refs[i], land_refs[i], pos, k)
                j = i * (N_DEV - 1) + k - 1
                pltpu.make_async_remote_copy(src_ref=s, dst_ref=d, send_sem=send_sems.at[j], recv_sem=recv_sems.at[j],
                                             device_id=_peer(pos, k), device_id_type=MESH_ID).start()
        token_ref[...] = jnp.zeros_like(token_ref)

    arrs = srcs + lands
    outs = pl.pallas_call(
        body, name=name,
        in_specs=[HBM_SPEC] * (2 * n + 1),
        out_specs=[SEM_SPEC, SEM_SPEC] + [HBM_SPEC] * (2 * n) + [pl.BlockSpec(memory_space=pltpu.VMEM)],
        out_shape=[pltpu.SemaphoreType.DMA((n_copy,)), pltpu.SemaphoreType.DMA((n_copy,))]
        + [pltpu.HBM(a.shape, a.dtype) for a in arrs] + [TOKEN],
        input_output_aliases={i: 2 + i for i in range(2 * n)},
        compiler_params=pltpu.CompilerParams(has_side_effects=SPLIT_EFFECT),
    )(*arrs, after)
    return {"kinds": kinds, "send": outs[0], "recv": outs[1], "arrs": outs[2:2 + 2 * n], "token": outs[-1]}


def _exchange_wait(ex, after, name):
    kinds = ex["kinds"]
    n = len(kinds)

    def body(*refs):
        src_refs, land_refs = refs[:n], refs[n:2 * n]
        send_sems, recv_sems = refs[2 * n], refs[2 * n + 1]
        token_ref = refs[-1]
        pos = _mesh_pos()
        for i, kind in enumerate(kinds):
            for k in range(1, N_DEV):
                s, d = _views(kind, src_refs[i], land_refs[i], pos, k)
                j = i * (N_DEV - 1) + k - 1
                cp = pltpu.make_async_remote_copy(src_ref=s, dst_ref=d, send_sem=send_sems.at[j],
                                                  recv_sem=recv_sems.at[j], device_id=_peer(pos, k),
                                                  device_id_type=MESH_ID)
                cp.wait_send()
                cp.wait_recv()
        token_ref[...] = jnp.zeros_like(token_ref)

    outs = pl.pallas_call(
        body, name=name,
        in_specs=[HBM_SPEC] * (2 * n) + [SEM_SPEC, SEM_SPEC, HBM_SPEC],
        out_specs=[HBM_SPEC] * (2 * n) + [pl.BlockSpec(memory_space=pltpu.VMEM)],
        out_shape=[pltpu.HBM(a.shape, a.dtype) for a in ex["arrs"]] + [TOKEN],
        input_output_aliases={i: i for i in range(2 * n)},
        compiler_params=pltpu.CompilerParams(has_side_effects=SPLIT_EFFECT),
    )(*ex["arrs"], ex["send"], ex["recv"], after)
    lands = [_place_own(k, s, d) for k, s, d in zip(kinds, outs[:n], outs[n:2 * n])]
    return lands, outs[-1]


def _place_own(kind, src, land):
    me = _flat(_mesh_pos())
    zeros = (0,) * (src.ndim - 1)
    if kind == GATHER_ROWS:
        return lax.dynamic_update_slice(land, src, (me * src.shape[0],) + zeros)
    if kind == GATHER_SLOT:
        return lax.dynamic_update_slice(land, src[None], (me,) + (0,) * src.ndim)
    if kind == SCATTER_ROWS:
        r = land.shape[1]
        own = lax.dynamic_slice(src, (me * r,) + zeros, (r,) + src.shape[1:])
    else:
        own = lax.dynamic_index_in_dim(src, me, 0, keepdims=False)
    return lax.dynamic_update_slice(land, own[None], (0,) * land.ndim)


def _all_gather_small(x, name):
    r, w = x.shape

    def body(x_ref, out_ref, send_sems, recv_sems):
        pos = _mesh_pos()
        me = _flat(pos)
        copies = []
        for k in range(1, N_DEV):
            cp = pltpu.make_async_remote_copy(
                src_ref=x_ref, dst_ref=out_ref.at[me], send_sem=send_sems.at[k - 1], recv_sem=recv_sems.at[k - 1],
                device_id=_peer(pos, k), device_id_type=MESH_ID)
            cp.start()
            copies.append(cp)
        out_ref[me] = x_ref[...]
        for cp in copies:
            cp.wait()

    vmem = pl.BlockSpec(memory_space=pltpu.VMEM)
    return pl.pallas_call(
        body, name=name, in_specs=[vmem], out_specs=vmem,
        out_shape=jax.ShapeDtypeStruct((N_DEV, r, w), x.dtype),
        scratch_shapes=[pltpu.SemaphoreType.DMA((N_DEV - 1,)), pltpu.SemaphoreType.DMA((N_DEV - 1,))],
        compiler_params=pltpu.CompilerParams(has_side_effects=True),
    )(x)


def _cols(g, lo, hi):
    c = g.shape[-1]
    parts = []
    for d in range(N_DEV):
        a, b = max(lo, d * c), min(hi, (d + 1) * c)
        if a < b:
            parts.append(g[d, :, a - d * c:b - d * c])
    return parts[0] if len(parts) == 1 else jnp.concatenate(parts, axis=1)


def _col_chunks(g):
    c = g.shape[-1] // N_DEV
    return jnp.stack([g[:, d * c:(d + 1) * c] for d in range(N_DEV)])


IN_PART = ("w_in", "conv_w")
OUT_PART = ROW_SHARDED + ("w_ffn_gate", "w_ffn_up")
TRANSPOSED = ("w_ffn_gate", "w_ffn_up")


def _gather_items(w, names, l):
    items = []
    for n in names:
        blk = w[n][l] if n == "conv_w" else w[n][l].astype(BF16)
        if n in TRANSPOSED:
            blk = blk.T
        items.append((GATHER_ROWS if n in ROW_SHARDED + TRANSPOSED else GATHER_SLOT, blk))
    return items


def _scatter_items(grads, names):
    def chunked(g):
        return g if g.ndim == 3 else _col_chunks(g)

    return [(SCATTER_ROWS, grads[n]) if n in ROW_SHARDED + TRANSPOSED else (SCATTER_SLOT, chunked(grads[n]))
            for n in names]


SMALL = ("ln_in_g", "ln_in_b", "conv_b", "dt_bias", "a_log", "d_skip", "ssd_norm_w", "att_sinks",
         "ln_mix_g", "ln_mix_b", "ln_ffn_g", "ln_ffn_b")


def _pack_small(vals):
    flat = jnp.concatenate([vals[n].reshape(-1) for n in SMALL])
    n = flat.shape[0]
    rows = -(-n // LANE)
    rows = -(-rows // 8) * 8
    return jnp.pad(flat, (0, rows * LANE - n)).reshape(rows, LANE)


def _unpack_small(buf, shapes):
    flat = buf.reshape(-1)
    off = 0
    out = {}
    for n in SMALL:
        cnt = math.prod(shapes[n])
        out[n] = flat[off:off + cnt].reshape(shapes[n])
        off += cnt
    return out


def _to_group_major(v):
    lead = v.shape[:-1]
    t = v.reshape(lead + (SSD_GROUPS, HEADS_PER_GROUP))
    t = jnp.pad(t, [(0, 0)] * len(lead) + [(0, 0), (0, LANE - HEADS_PER_GROUP)])
    return t.reshape(lead + (DT_PAD,))


def _from_group_major(v):
    lead = v.shape[:-1]
    return v.reshape(lead + (SSD_GROUPS, LANE))[..., :HEADS_PER_GROUP].reshape(lead + (SSD_HEADS,))


def _rows8(v):
    return jnp.pad(v, ((0, 8 - v.shape[0]), (0, 0)))


IN_OFFS = {"q": (0, 1024), "kv": (1024, 1280), "z": (1280, 3328), "xs": (3328, 5376), "b": (5376, 5888),
           "c": (5888, 6400), "dt": (6400, 6432), "gl": (6432, 8480)}
PIECES = ("q", "kv", "z", "xs", "b", "c", "dt", "gl")


CAT = ("z", "xs", "gl", "q", "b", "c", "dt", "kv")
CAT_WIDTH = {"q": 1024, "z": 2048, "xs": 2048, "gl": 2048, "b": 512, "c": 512, "kv": 256, "dt": DT_PAD}
CAT_OFF = {p: sum(CAT_WIDTH[q] for q in CAT[:i]) for i, p in enumerate(CAT)}
CAT_DIM = sum(CAT_WIDTH.values())
MAIN_DIM = CAT_OFF["kv"]


def _cat_w_in(g):
    pieces = {p: _cols(g, lo, hi) for p, (lo, hi) in IN_OFFS.items()}
    pieces["dt"] = _to_group_major(pieces["dt"])
    return jnp.concatenate([pieces[p] for p in CAT], axis=1)


def _dw_in_chunks(dw_main, dw_kv):
    dt = _from_group_major(dw_main[:, CAT_OFF["dt"]:CAT_OFF["dt"] + DT_PAD])
    shard = IN_OFFS[PIECES[-1]][1] // N_DEV

    def piece(pc, a, b):
        if pc == "dt":
            return dt[:, a:b]
        if pc == "kv":
            return dw_kv[:, a:b]
        return dw_main[:, CAT_OFF[pc] + a:CAT_OFF[pc] + b]

    chunks = []
    for d in range(N_DEV):
        parts = []
        for pc in PIECES:
            lo, hi = IN_OFFS[pc]
            a, b = max(lo, d * shard), min(hi, (d + 1) * shard)
            if a < b:
                parts.append(piece(pc, a - lo, b - lo))
        chunks.append(parts[0] if len(parts) == 1 else jnp.concatenate(parts, axis=1))
    return jnp.stack(chunks)


def _params_out(W):
    return {n: W[n] for n in OUT_PART}


def _params_in(l, W, sm):
    p = {"w_cat": _cat_w_in(W["w_in"])}
    cw = _cols(W["conv_w"], 0, SSD_D_INNER + 2 * BC_DIM)
    cb = sm["conv_b"][l]
    segs = {"xs": (0, 2048), "b": (2048, 2560), "c": (2560, 3072)}
    p["conv_w8"] = {s: _rows8(cw[:, lo:hi]) for s, (lo, hi) in segs.items()}
    p["conv_b8"] = {s: _rows8(cb[None, lo:hi]) for s, (lo, hi) in segs.items()}
    p["ssd_par"] = _rows8(jnp.stack([_to_group_major(sm["dt_bias"][l]), _to_group_major(sm["a_log"][l]),
                                     _to_group_major(sm["d_skip"][l])]))
    p["norm_w"] = sm["ssd_norm_w"][l]
    p["sinks8"] = _rows8(jnp.pad(sm["att_sinks"][l], (0, LANE - ATT_HEADS))[None])
    for n in ("ln_mix_g", "ln_mix_b", "ln_ffn_g", "ln_ffn_b"):
        p[n] = sm[n][l]
    return p


def _fwd_mixers(h0, p, l, dep=None):
    tag = f"l{l}_"
    a = {"h0": h0}
    proj = _mm(h0, p["w_cat"], "nn", tag + "proj", dep=dep)
    for pc in CAT:
        a[pc] = (proj, CAT_OFF[pc], CAT_WIDTH[pc])
    for s in ("xs", "b", "c"):
        a[s + "c"] = _conv_fwd(a[s], p["conv_w8"][s], p["conv_b8"][s], tag + "conv_" + s)
    a["y"], a["hs"] = _ssd_fwd(a["xsc"], a["bc"], a["cc"], a["dt"], p["ssd_par"], tag + "ssd_fwd")
    a["yn"] = _gnorm_fwd(a["y"], a["z"], p["norm_w"], tag + "gnorm")
    a["att"], a["att32"] = _att_fwd(a["q"], a["kv"], p["sinks8"], tag + "att_fwd")
    return a


def _fwd_out(a, p, l, dep=None):
    tag = f"l{l}_"
    h0 = a["h0"]
    a["ya"], a["yb"], a["merged"] = _branch_out(a["yn"], a["att"], p["w_ssd_out"], p["w_att_out"], a["gl"],
                                                tag + "branch_out", dep=dep)
    a["mix"] = _mm(a["merged"], p["w_mix_out"], "nn", tag + "mix_out")
    a["h1"] = _ln_fwd(h0, a["mix"], p["ln_mix_g"], p["ln_mix_b"], ALPHA, tag + "ln_mix")
    a["fg"], a["fu"], a["act"] = _ffn_in(a["h1"], p["w_ffn_gate"], p["w_ffn_up"], tag + "ffn_in")
    a["ffn"] = _mm(a["act"], p["w_ffn_down"], "nn", tag + "ffn_down")
    a["h2"] = _ln_fwd(a["h1"], a["ffn"], p["ln_ffn_g"], p["ln_ffn_b"], ALPHA, tag + "ln_ffn")
    return a


def _dw(x, dy, name, dep=None):
    return _mm(x, dy, "tn", name, out_dtype=BF16, dep=dep)


def _bwd_out(a, p, dh2, l, dep=None):
    tag = f"l{l}_b_"
    gw, gs = {}, {}
    du2, acc = _ln_bwd(a["h1"], a["ffn"], p["ln_ffn_g"], dh2, ALPHA, tag + "ln_ffn")
    gs["ln_ffn_g"], gs["ln_ffn_b"] = acc[0], acc[1]
    gw["w_ffn_down"] = _dw(a["act"], du2, tag + "dw_down", dep=dep)
    dfg, dfu = _ffn_dact(du2, p["w_ffn_down"], a["fg"], a["fu"], tag + "ffn_dact", dep=dep)
    gw["w_ffn_gate"] = _dw(dfg, a["h1"], tag + "dw_gate")
    gw["w_ffn_up"] = _dw(dfu, a["h1"], tag + "dw_up")
    dh1 = _mm(dfg, p["w_ffn_gate"], "nn", tag + "dh1_gate", add=du2, add_scale=ALPHA)
    dh1 = _mm(dfu, p["w_ffn_up"], "nn", tag + "dh1_up", add=dh1)
    du1, acc = _ln_bwd(a["h0"], a["mix"], p["ln_mix_g"], dh1, ALPHA, tag + "ln_mix")
    gs["ln_mix_g"], gs["ln_mix_b"] = acc[0], acc[1]
    gw["w_mix_out"] = _dw(a["merged"], du1, tag + "dw_mix")
    dmerged = _mm(du1, p["w_mix_out"], "nt", tag + "dmerged")
    dya, dyb, dproj = _merge_bwd(a["gl"], a["ya"], a["yb"], dmerged, tag + "merge",
                                 (None, CAT_OFF["gl"], MAIN_DIM))
    gw["w_ssd_out"] = _dw(a["yn"], dya, tag + "dw_ssd")
    gw["w_att_out"] = _dw(a["att"], dyb, tag + "dw_att")
    return {"du1": du1, "dya": dya, "dyb": dyb, "dproj": dproj}, gw, gs


def _bwd_mixers(a, p, carry, l, dep=None):
    tag = f"l{l}_b_"
    gs = {}
    du1, dproj = carry["du1"], carry["dproj"]

    def win(pc):
        return (dproj, CAT_OFF[pc], MAIN_DIM)

    dyn = _mm(carry["dya"], p["w_ssd_out"], "nt", tag + "dyn", dep=dep)
    datt = _mm(carry["dyb"], p["w_att_out"], "nt", tag + "datt", out_dtype=BF16, dep=dep)
    dproj, dkv, acc = _att_bwd(a["q"], a["kv"], p["sinks8"], a["att32"], datt, tag + "att", win("q"))
    gs["att_sinks"] = acc[0, :ATT_HEADS]
    dy, dproj, acc = _gnorm_bwd(a["y"], a["z"], p["norm_w"], dyn, tag + "gnorm", win("z"))
    gs["ssd_norm_w"] = acc[0]
    dxs, dbm, dcm, dproj, acc = _ssd_bwd(a["xsc"], a["bc"], a["cc"], a["dt"], p["ssd_par"], a["hs"], dy,
                                         tag + "ssd", win("dt"))
    gs["dt_bias"], gs["a_log"], gs["d_skip"] = (_from_group_major(acc[i]) for i in range(3))
    dconv_w, dconv_b = [], []
    for s, dout in (("xs", dxs), ("b", dbm), ("c", dcm)):
        dc, acc = _conv_bwd_pre(a[s], p["conv_w8"][s], p["conv_b8"][s], dout, tag + "conv_pre_" + s)
        dconv_w.append(acc[:CONV_TAPS])
        dconv_b.append(acc[CONV_TAPS])
        dproj = _conv_bwd_in(dc, p["conv_w8"][s], tag + "conv_in_" + s, win(s))
    gconv = jnp.concatenate(dconv_w, axis=1)
    gs["conv_b"] = jnp.concatenate(dconv_b)
    w_main, w_kv = p["w_cat"][:, :MAIN_DIM], p["w_cat"][:, MAIN_DIM:]
    dw_main, dw_kv = _dw(a["h0"], dproj, tag + "dw_in"), _dw(a["h0"], dkv, tag + "dw_in_kv")

    def grad_h0(dep=None):
        dh0 = _mm(dproj, w_main, "nt", tag + "dh0", add=du1, add_scale=ALPHA, dep=dep)
        return _mm(dkv, w_kv, "nt", tag + "dh0_kv", add=dh0)

    return grad_h0, _dw_in_chunks(dw_main, dw_kv), gconv, gs


def _step(x, target, w, m, v):
    x2 = x[0]
    t2 = target[0]
    tok = jnp.zeros(TOKEN.shape, TOKEN.dtype)

    ex = _exchange_start(_gather_items(w, IN_PART, 0), tok, "gather_l0_in_start")
    h = _ln_fwd(x2, None, w["ln_in_g"], w["ln_in_b"], 1.0, "ln_in")
    lands, tok = _exchange_wait(ex, h, "gather_l0_in_wait")
    p0 = _params_in(0, dict(zip(IN_PART, lands)), w)
    ex = _exchange_start(_gather_items(w, OUT_PART, 0) + _gather_items(w, IN_PART, 1), tok,
                         "gather_l0_out_l1_in_start")
    a0 = _fwd_mixers(h, p0, 0, dep=ex["token"])
    lands, tok = _exchange_wait(ex, a0["att"], "gather_l0_out_l1_in_wait")
    p0.update(_params_out(dict(zip(OUT_PART, lands))))
    p1 = _params_in(1, dict(zip(IN_PART, lands[len(OUT_PART):])), w)
    ex = _exchange_start(_gather_items(w, OUT_PART, 1), tok, "gather_l1_out_start")
    a0 = _fwd_out(a0, p0, 0, dep=ex["token"])
    lands, tok = _exchange_wait(ex, a0["h2"], "gather_l1_out_wait")
    p1.update(_params_out(dict(zip(OUT_PART, lands))))
    a1 = _fwd_out(_fwd_mixers(a0["h2"], p1, 1), p1, 1)

    sse, dh = _loss_fwd_bwd(a1["h2"], t2, "loss")
    loss = lax.psum(0.5 / D_MODEL * sse[0, 0], ("x", "y", "c"))

    carry, gw1, gs1 = _bwd_out(a1, p1, dh, 1)
    grad_h0, gw1["w_in"], gw1["conv_w"], gs = _bwd_mixers(a1, p1, carry, 1)
    dh = grad_h0()
    gs1.update(gs)
    ex1 = _exchange_start(_scatter_items(gw1, GATHERED), tok, "scatter_l1_start")
    carry, gw0, gs0 = _bwd_out(a0, p0, dh, 0, dep=ex1["token"])
    lands, tok = _exchange_wait(ex1, carry["dyb"], "scatter_l1_wait")
    land1 = dict(zip(GATHERED, lands))
    ex0 = _exchange_start(_scatter_items(gw0, OUT_PART), tok, "scatter_l0_out_start")
    grad_h0, gw0["w_in"], gw0["conv_w"], gs = _bwd_mixers(a0, p0, carry, 0, dep=ex0["token"])
    gs0.update(gs)
    lands, tok = _exchange_wait(ex0, gw0["w_in"], "scatter_l0_out_wait")
    land0 = dict(zip(OUT_PART, lands))
    ex0 = _exchange_start(_scatter_items(gw0, IN_PART), tok, "scatter_l0_in_start")
    dh = grad_h0(dep=ex0["token"])
    grad_x2, acc = _ln_bwd(x2, None, w["ln_in_g"], dh, 1.0, "ln_in_b")

    outs = [{} for _ in range(4)]

    def update(names):
        res = None
        for n in names:
            if n in TRANSPOSED:
                res = _adamw([land0[n], land1[n]], *(jnp.swapaxes(t, 1, 2) for t in (w[n], m[n], v[n])),
                             "adamw_" + n)
                res = tuple(jnp.swapaxes(t, 1, 2) for t in res)
            else:
                res = _adamw([land0[n], land1[n]], w[n], m[n], v[n], "adamw_" + n)
            for o, t in zip(outs, res):
                o[n] = t
        return res[1]

    update(OUT_PART)
    gsm = {"ln_in_g": acc[0], "ln_in_b": acc[1]}
    for n in SMALL[2:]:
        gsm[n] = jnp.stack([gs0[n], gs1[n]])
    small_shapes = {n: w[n].shape for n in SMALL}
    land_s = _all_gather_small(_pack_small(gsm), "small_grads_all_gather")
    res = _adamw([land_s], _pack_small(w)[None], _pack_small(m)[None], _pack_small(v)[None], "adamw_small")
    for o, t in zip(outs, res):
        o.update(_unpack_small(t[0], small_shapes))
    lands, _ = _exchange_wait(ex0, res[1], "scatter_l0_in_wait")
    land0.update(zip(IN_PART, lands))
    update(IN_PART)
    return loss, grad_x2[None], outs


WEIGHT_NAMES = ("ln_in_g", "ln_in_b", "w_in", "conv_w", "conv_b", "dt_bias", "a_log", "d_skip", "ssd_norm_w",
                "att_sinks", "w_ssd_out", "w_att_out", "w_mix_out", "ln_mix_g", "ln_mix_b", "w_ffn_gate",
                "w_ffn_up", "w_ffn_down", "ln_ffn_g", "ln_ffn_b")


def kernel(x, ln_in_g, ln_in_b, w_in, conv_w, conv_b, dt_bias, a_log, d_skip, ssd_norm_w, att_sinks, w_ssd_out, w_att_out, w_mix_out, ln_mix_g, ln_mix_b, w_ffn_gate, w_ffn_up, w_ffn_down, ln_ffn_g, ln_ffn_b, loss_target, m_ln_in_g, m_ln_in_b, m_w_in, m_conv_w, m_conv_b, m_dt_bias, m_a_log, m_d_skip, m_ssd_norm_w, m_att_sinks, m_w_ssd_out, m_w_att_out, m_w_mix_out, m_ln_mix_g, m_ln_mix_b, m_w_ffn_gate, m_w_ffn_up, m_w_ffn_down, m_ln_ffn_g, m_ln_ffn_b, v_ln_in_g, v_ln_in_b, v_w_in, v_conv_w, v_conv_b, v_dt_bias, v_a_log, v_d_skip, v_ssd_norm_w, v_att_sinks, v_w_ssd_out, v_w_att_out, v_w_mix_out, v_ln_mix_g, v_ln_mix_b, v_w_ffn_gate, v_w_ffn_up, v_w_ffn_down, v_ln_ffn_g, v_ln_ffn_b):
    w = dict(zip(WEIGHT_NAMES, (ln_in_g, ln_in_b, w_in, conv_w, conv_b, dt_bias, a_log, d_skip, ssd_norm_w,
                                att_sinks, w_ssd_out, w_att_out, w_mix_out, ln_mix_g, ln_mix_b, w_ffn_gate,
                                w_ffn_up, w_ffn_down, ln_ffn_g, ln_ffn_b)))
    m = dict(zip(WEIGHT_NAMES, (m_ln_in_g, m_ln_in_b, m_w_in, m_conv_w, m_conv_b, m_dt_bias, m_a_log, m_d_skip,
                                m_ssd_norm_w, m_att_sinks, m_w_ssd_out, m_w_att_out, m_w_mix_out, m_ln_mix_g,
                                m_ln_mix_b, m_w_ffn_gate, m_w_ffn_up, m_w_ffn_down, m_ln_ffn_g, m_ln_ffn_b)))
    v = dict(zip(WEIGHT_NAMES, (v_ln_in_g, v_ln_in_b, v_w_in, v_conv_w, v_conv_b, v_dt_bias, v_a_log, v_d_skip,
                                v_ssd_norm_w, v_att_sinks, v_w_ssd_out, v_w_att_out, v_w_mix_out, v_ln_mix_g,
                                v_ln_mix_b, v_w_ffn_gate, v_w_ffn_up, v_w_ffn_down, v_ln_ffn_g, v_ln_ffn_b)))
    loss, grad_x, outs = _step(x, loss_target, w, m, v)
    result = [loss, grad_x]
    for o in outs:
        result.extend(o[n] for n in WEIGHT_NAMES)
    return tuple(result)
```

```python
import math

import jax
import jax.numpy as jnp
from jax import lax
from jax.experimental import pallas as pl
from jax.experimental.pallas import tpu as pltpu

F32 = jnp.float32
BF16 = jnp.bfloat16

D_MODEL = 1024
DEPTH = 2
N_DEV = 8
ATT_HEADS = 16
ATT_KV_HEADS = 2
ATT_HEAD_DIM = 64
ATT_BLOCK = 128
SSD_D_INNER = 2048
SSD_HEADS = 32
SSD_GROUPS = 4
SSD_STATE = 128
SSD_CHUNK = 128
FFN_HIDDEN = 2816
LN_EPS = 1e-5
RMS_EPS = 1e-5
ALPHA = (2 * DEPTH) ** 0.25
Q_DIM = 1024
BC_DIM = 512
DT_PAD = 512

ADAM_LR = 0.001
ADAM_B1 = 0.9
ADAM_B2 = 0.999
ADAM_EPS = 1e-08
ADAM_WD = 0.01
ADAM_STEP = 10

LANE = 128
VMEM_LIMIT = 48 * 1024 * 1024
NEG = -1e30

_NN = (((1,), (0,)), ((), ()))
_NT = (((1,), (1,)), ((), ()))
_TN = (((0,), (0,)), ((), ()))
MESH_ID = pl.DeviceIdType.MESH


def _dot(a, b, dims=_NN):
    return lax.dot_general(a, b, dims, preferred_element_type=F32)


def _sig(x):
    return 1.0 / (1.0 + jnp.exp(-x))


def _softplus(x):
    return jnp.maximum(x, 0.0) + jnp.log(1.0 + jnp.exp(-jnp.abs(x)))


def _cparams(*sem):
    return pltpu.CompilerParams(dimension_semantics=sem, vmem_limit_bytes=VMEM_LIMIT)


def _pick(n, cap):
    if n <= cap:
        return n
    best = None
    for t in range(LANE, cap + 1, LANE):
        if n % t == 0:
            best = t
    assert best is not None, (n, cap)
    return best


def _tile(n):
    if n <= 1024 or n % 1024 == 0:
        return min(n, 1024)
    return _pick(n, 1408)


def _rows(n):
    return min(512, n)


def _window(x):
    return x if isinstance(x, tuple) else (x, 0, x.shape[1])


def _into(into, n_in, out_idx):
    buf, col0, width = into
    if buf is None:
        return [], [], {}, col0, width
    return [buf], [pl.BlockSpec(memory_space=pl.ANY)], {n_in: out_idx}, col0, width


def _mm(a, b, mode, name, add=None, add_scale=1.0, out_dtype=F32, dep=None):
    if mode == "nn":
        m, k = a.shape
        n = b.shape[1]
    elif mode == "nt":
        m, k = a.shape
        n = b.shape[0]
    else:
        k, m = a.shape
        n = b.shape[1]
    tm = _tile(m)
    tn = _pick(n, 2176) if mode == "tn" and n > 1024 else _tile(n)
    tk = _pick(k, 2176) if mode == "nt" and a.dtype == BF16 and k > 2816 else _tile(k)
    nk = k // tk
    has_add = add is not None
    dims = {"nn": _NN, "nt": _NT, "tn": _TN}[mode]

    def body(*refs):
        if dep is not None:
            refs = refs[:-3] + refs[-2:]
        if has_add:
            a_ref, b_ref, add_ref, o_ref, acc_ref = refs
        else:
            a_ref, b_ref, o_ref, acc_ref = refs
        kk = pl.program_id(2)

        @pl.when(kk == 0)
        def _():
            if has_add:
                acc_ref[...] = add_scale * add_ref[...].astype(F32)
            else:
                acc_ref[...] = jnp.zeros_like(acc_ref)

        acc_ref[...] += _dot(a_ref[...].astype(BF16), b_ref[...].astype(BF16), dims)

        @pl.when(kk == nk - 1)
        def _():
            o_ref[...] = acc_ref[...].astype(o_ref.dtype)

    if mode == "nn":
        a_spec = pl.BlockSpec((tm, tk), lambda i, j, kk: (i, kk))
        b_spec = pl.BlockSpec((tk, tn), lambda i, j, kk: (kk, j))
    elif mode == "nt":
        a_spec = pl.BlockSpec((tm, tk), lambda i, j, kk: (i, kk))
        b_spec = pl.BlockSpec((tn, tk), lambda i, j, kk: (j, kk))
    else:
        a_spec = pl.BlockSpec((tk, tm), lambda i, j, kk: (kk, i))
        b_spec = pl.BlockSpec((tk, tn), lambda i, j, kk: (kk, j))
    o_spec = pl.BlockSpec((tm, tn), lambda i, j, kk: (i, j))
    in_specs = [a_spec, b_spec] + ([o_spec] if has_add else [])
    args = (a, b) + ((add,) if has_add else ())
    if dep is not None:
        in_specs.append(pl.BlockSpec((8, LANE), lambda i, j, kk: (0, 0)))
        args += (dep,)
    return pl.pallas_call(
        body, name=name, grid=(m // tm, n // tn, nk),
        in_specs=in_specs, out_specs=o_spec,
        out_shape=jax.ShapeDtypeStruct((m, n), out_dtype),
        scratch_shapes=[pltpu.VMEM((tm, tn), F32)],
        compiler_params=_cparams("parallel", "parallel", "arbitrary"),
    )(*args)


def _vec_spec(width):
    return pl.BlockSpec((1, width), lambda i: (0, 0))


def _ln_fwd(a, b, gamma, beta, alpha, name):
    n_rows, dm = a.shape
    has_b = b is not None

    def body(*refs):
        if has_b:
            a_ref, b_ref, g_ref, be_ref, o_ref = refs
            u = alpha * a_ref[...] + b_ref[...]
        else:
            a_ref, g_ref, be_ref, o_ref = refs
            u = a_ref[...]
        mu = jnp.mean(u, axis=-1, keepdims=True)
        d = u - mu
        var = jnp.mean(d * d, axis=-1, keepdims=True)
        o_ref[...] = d * lax.rsqrt(var + LN_EPS) * g_ref[...] + be_ref[...]

    row = pl.BlockSpec((_rows(n_rows),dm), lambda i: (i, 0))
    in_specs = [row] + ([row] if has_b else []) + [_vec_spec(dm), _vec_spec(dm)]
    args = (a,) + ((b,) if has_b else ()) + (gamma.reshape(1, dm), beta.reshape(1, dm))
    return pl.pallas_call(
        body, name=name, grid=(n_rows // _rows(n_rows),), in_specs=in_specs, out_specs=row,
        out_shape=jax.ShapeDtypeStruct((n_rows, dm), F32),
        compiler_params=_cparams("parallel"),
    )(*args)


def _ln_bwd(a, b, gamma, dy, alpha, name):
    n_rows, dm = a.shape
    has_b = b is not None

    def body(*refs):
        if has_b:
            a_ref, b_ref, g_ref, dy_ref, du_ref, acc_ref = refs
            u = alpha * a_ref[...] + b_ref[...]
        else:
            a_ref, g_ref, dy_ref, du_ref, acc_ref = refs
            u = a_ref[...]

        @pl.when(pl.program_id(0) == 0)
        def _():
            acc_ref[...] = jnp.zeros_like(acc_ref)

        mu = jnp.mean(u, axis=-1, keepdims=True)
        d = u - mu
        var = jnp.mean(d * d, axis=-1, keepdims=True)
        rstd = lax.rsqrt(var + LN_EPS)
        xhat = d * rstd
        dyv = dy_ref[...]
        acc_ref[0:1, :] += jnp.sum(dyv * xhat, axis=0, keepdims=True)
        acc_ref[1:2, :] += jnp.sum(dyv, axis=0, keepdims=True)
        dxh = dyv * g_ref[...]
        m1 = jnp.mean(dxh, axis=-1, keepdims=True)
        m2 = jnp.mean(dxh * xhat, axis=-1, keepdims=True)
        du_ref[...] = rstd * (dxh - m1 - xhat * m2)

    row = pl.BlockSpec((_rows(n_rows),dm), lambda i: (i, 0))
    in_specs = [row] + ([row] if has_b else []) + [_vec_spec(dm), row]
    args = (a,) + ((b,) if has_b else ()) + (gamma.reshape(1, dm), dy)
    return pl.pallas_call(
        body, name=name, grid=(n_rows // _rows(n_rows),), in_specs=in_specs,
        out_specs=(row, pl.BlockSpec((8, dm), lambda i: (0, 0))),
        out_shape=(jax.ShapeDtypeStruct((n_rows, dm), F32), jax.ShapeDtypeStruct((8, dm), F32)),
        compiler_params=_cparams("arbitrary"),
    )(*args)


def _loss_fwd_bwd(y, target, name):
    n_rows, dm = y.shape

    def body(y_ref, t_ref, acc_ref, dy_ref):
        @pl.when(pl.program_id(0) == 0)
        def _():
            acc_ref[...] = jnp.zeros_like(acc_ref)

        d = y_ref[...] - t_ref[...]
        acc_ref[...] += jnp.sum(d * d)
        dy_ref[...] = d * (1.0 / dm)

    row = pl.BlockSpec((_rows(n_rows),dm), lambda i: (i, 0))
    return pl.pallas_call(
        body, name=name, grid=(n_rows // _rows(n_rows),), in_specs=[row, row],
        out_specs=(pl.BlockSpec((8, LANE), lambda i: (0, 0)), row),
        out_shape=(jax.ShapeDtypeStruct((8, LANE), F32), jax.ShapeDtypeStruct((n_rows, dm), F32)),
        compiler_params=_cparams("arbitrary"),
    )(y, target)


FFN_ROWS = 512


def _ffn_in(h, wg, wu, name, dep=None):
    m, k = h.shape
    n = wg.shape[0]
    tm, tn = min(FFN_ROWS, m), _tile(n)

    def body(*refs):
        h_ref, wg_ref, wu_ref = refs[:3]
        g_ref, u_ref, act_ref = refs[-3:]
        hb = h_ref[...].astype(BF16)
        g = _dot(hb, wg_ref[...], _NT)
        u = _dot(hb, wu_ref[...], _NT)
        g_ref[...] = g
        u_ref[...] = u
        act_ref[...] = (g * _sig(g) * u).astype(BF16)

    rows = pl.BlockSpec((tm, k), lambda j, i: (i, 0))
    wrow = pl.BlockSpec((tn, k), lambda j, i: (j, 0))
    out = pl.BlockSpec((tm, tn), lambda j, i: (i, j))
    in_specs, args = [rows, wrow, wrow], (h, wg, wu)
    if dep is not None:
        in_specs.append(pl.BlockSpec((8, LANE), lambda j, i: (0, 0)))
        args += (dep,)
    return pl.pallas_call(
        body, name=name, grid=(n // tn, m // tm), in_specs=in_specs, out_specs=(out, out, out),
        out_shape=(jax.ShapeDtypeStruct((m, n), F32), jax.ShapeDtypeStruct((m, n), F32),
                   jax.ShapeDtypeStruct((m, n), BF16)),
        compiler_params=_cparams("parallel", "parallel"),
    )(*args)


def _ffn_dact(dy, wd, g, u, name, dep=None):
    m, k = dy.shape
    n = wd.shape[0]
    tm, tn = min(FFN_ROWS, m), _tile(n)

    def body(*refs):
        dy_ref, wd_ref, g_ref, u_ref = refs[:4]
        dg_ref, du_ref = refs[-2:]
        da = _dot(dy_ref[...].astype(BF16), wd_ref[...], _NT)
        gv = g_ref[...]
        s = _sig(gv)
        dg_ref[...] = (da * u_ref[...] * (s * (1.0 + gv * (1.0 - s)))).astype(BF16)
        du_ref[...] = (da * gv * s).astype(BF16)

    rows = pl.BlockSpec((tm, k), lambda j, i: (i, 0))
    wrow = pl.BlockSpec((tn, k), lambda j, i: (j, 0))
    out = pl.BlockSpec((tm, tn), lambda j, i: (i, j))
    in_specs, args = [rows, wrow, out, out], (dy, wd, g, u)
    if dep is not None:
        in_specs.append(pl.BlockSpec((8, LANE), lambda j, i: (0, 0)))
        args += (dep,)
    return pl.pallas_call(
        body, name=name, grid=(n // tn, m // tm), in_specs=in_specs, out_specs=(out, out),
        out_shape=(jax.ShapeDtypeStruct((m, n), BF16), jax.ShapeDtypeStruct((m, n), BF16)),
        compiler_params=_cparams("parallel", "parallel"),
    )(*args)


def _gate_specs(gl, n_rows, dm):
    arr, g0, _ = _window(gl)
    return arr, [pl.BlockSpec((_rows(n_rows), dm), lambda i, k=k: (i, g0 // dm + k)) for k in range(2)]


def _branch_out(yn, att, w_ssd, w_att, gl, name, dep=None):
    n_rows, dm = yn.shape[0], w_ssd.shape[1]
    gl_arr, gspecs = _gate_specs(gl, n_rows, dm)

    def body(*refs):
        ga_ref, gb_ref, yn_ref, att_ref, ws_ref, wa_ref = refs[:6]
        ya_ref, yb_ref, o_ref = refs[-3:]
        ya = _dot(yn_ref[...], ws_ref[...])
        yb = _dot(att_ref[...], wa_ref[...])
        ya_ref[...] = ya
        yb_ref[...] = yb
        o_ref[...] = (_sig(ga_ref[...]) * ya + _sig(gb_ref[...]) * yb).astype(BF16)

    tm = _rows(n_rows)
    row = pl.BlockSpec((tm, dm), lambda i: (i, 0))
    in_specs = gspecs + [pl.BlockSpec((tm, yn.shape[1]), lambda i: (i, 0)), row,
                         pl.BlockSpec(w_ssd.shape, lambda i: (0, 0)), pl.BlockSpec(w_att.shape, lambda i: (0, 0))]
    args = (gl_arr, gl_arr, yn, att, w_ssd, w_att)
    if dep is not None:
        in_specs.append(pl.BlockSpec((8, LANE), lambda i: (0, 0)))
        args += (dep,)
    return pl.pallas_call(
        body, name=name, grid=(n_rows // tm,), in_specs=in_specs, out_specs=(row, row, row),
        out_shape=(jax.ShapeDtypeStruct((n_rows, dm), F32), jax.ShapeDtypeStruct((n_rows, dm), F32),
                   jax.ShapeDtypeStruct((n_rows, dm), BF16)),
        compiler_params=_cparams("parallel"),
    )(*args)


def _merge_bwd(gl, ya, yb, dmix, w_mix, name, into):
    n_rows, dm = ya.shape
    gl_arr, gspecs = _gate_specs(gl, n_rows, dm)
    extra, extra_specs, aliases, col0, width = _into(into, 6, 2)

    def body(*refs):
        ga_ref, gb_ref, ya_ref, yb_ref, dx_ref, w_ref = refs[:6]
        dya_ref, dyb_ref, dgl_ref = refs[-3:]
        ga = _sig(ga_ref[...])
        gb = _sig(gb_ref[...])
        dmv = _dot(dx_ref[...].astype(BF16), w_ref[...], _NT)
        dya_ref[...] = (dmv * ga).astype(BF16)
        dyb_ref[...] = (dmv * gb).astype(BF16)
        dgl_ref[:, :dm] = (dmv * ya_ref[...] * ga * (1.0 - ga)).astype(BF16)
        dgl_ref[:, dm:] = (dmv * yb_ref[...] * gb * (1.0 - gb)).astype(BF16)

    row = pl.BlockSpec((_rows(n_rows),dm), lambda i: (i, 0))
    row2 = pl.BlockSpec((_rows(n_rows),2 * dm), lambda i: (i, col0 // (2 * dm)))
    return pl.pallas_call(
        body, name=name, grid=(n_rows // _rows(n_rows),),
        in_specs=gspecs + [row, row, row, pl.BlockSpec(w_mix.shape, lambda i: (0, 0))] + extra_specs,
        out_specs=(row, row, row2),
        out_shape=(jax.ShapeDtypeStruct((n_rows, dm), BF16), jax.ShapeDtypeStruct((n_rows, dm), BF16),
                   jax.ShapeDtypeStruct((n_rows, width), BF16)),
        input_output_aliases=aliases,
        compiler_params=_cparams("parallel"),
    )(gl_arr, gl_arr, ya, yb, dmix, w_mix, *extra)


CONV_TAPS = 4
CONV_COLS = 512
HALO = 8


def _shift_down(cur, prev8, s, row8):
    r = pltpu.roll(cur, s, axis=0)
    top = jnp.where(row8 < s, pltpu.roll(prev8, s, axis=0), r[0:HALO])
    return jnp.concatenate([top, r[HALO:]], axis=0)


def _shift_up(cur, next8, s, row8):
    n = cur.shape[0]
    r = pltpu.roll(cur, n - s, axis=0)
    bot = jnp.where(row8 >= HALO - s, pltpu.roll(next8, HALO - s, axis=0), r[n - HALO:])
    return jnp.concatenate([r[:n - HALO], bot], axis=0)


def _conv_pre(u_ref, prev_ref, w_ref, b_ref, li):
    cur = u_ref[...]
    prev8 = jnp.where(li == 0, 0.0, prev_ref[...])
    row8 = lax.broadcasted_iota(jnp.int32, prev8.shape, 0)
    shifted = [cur] + [_shift_down(cur, prev8, s, row8) for s in range(1, CONV_TAPS)]
    acc = b_ref[...] + shifted[0] * w_ref[CONV_TAPS - 1:CONV_TAPS, :]
    for s in range(1, CONV_TAPS):
        acc = acc + shifted[s] * w_ref[CONV_TAPS - 1 - s:CONV_TAPS - s, :]
    return acc, shifted


def _conv_specs(n_rows, tl, col0=0):
    off = col0 // CONV_COLS
    cur = pl.BlockSpec((tl, CONV_COLS), lambda cj, li: (li, cj + off))
    prev = pl.BlockSpec((HALO, CONV_COLS), lambda cj, li: (jnp.maximum(li * (tl // HALO) - 1, 0), cj + off))
    nxt = pl.BlockSpec((HALO, CONV_COLS),
                       lambda cj, li: (jnp.minimum((li + 1) * (tl // HALO), n_rows // HALO - 1), cj + off))
    par = pl.BlockSpec((8, CONV_COLS), lambda cj, li: (0, cj + off))
    return cur, prev, nxt, par


def _conv_fwd(u, w8, b8, name):
    u, u0, c = _window(u)
    n_rows = u.shape[0]
    tl = _rows(n_rows)
    cur, _, _, par = _conv_specs(n_rows, tl)
    ucur, prev, _, _ = _conv_specs(n_rows, tl, u0)

    def body(u_ref, prev_ref, w_ref, b_ref, o_ref):
        acc, _ = _conv_pre(u_ref, prev_ref, w_ref, b_ref[0:1, :], pl.program_id(1))
        o_ref[...] = acc * _sig(acc)

    return pl.pallas_call(
        body, name=name, grid=(c // CONV_COLS, n_rows // tl), in_specs=[ucur, prev, par, par], out_specs=cur,
        out_shape=jax.ShapeDtypeStruct((n_rows, c), F32),
        compiler_params=_cparams("parallel", "parallel"),
    )(u, u, w8, b8)


def _conv_bwd_pre(u, w8, b8, dout, name):
    u, u0, c = _window(u)
    n_rows = u.shape[0]
    tl = _rows(n_rows)
    cur, _, _, par = _conv_specs(n_rows, tl)
    ucur, prev, _, _ = _conv_specs(n_rows, tl, u0)

    def body(u_ref, prev_ref, w_ref, b_ref, do_ref, dc_ref, acc_ref):
        @pl.when(pl.program_id(1) == 0)
        def _():
            acc_ref[...] = jnp.zeros_like(acc_ref)

        acc, shifted = _conv_pre(u_ref, prev_ref, w_ref, b_ref[0:1, :], pl.program_id(1))
        sg = _sig(acc)
        dc = do_ref[...] * (sg * (1.0 + acc * (1.0 - sg)))
        dc_ref[...] = dc
        for k in range(CONV_TAPS):
            acc_ref[k:k + 1, :] += jnp.sum(dc * shifted[CONV_TAPS - 1 - k], axis=0, keepdims=True)
        acc_ref[CONV_TAPS:CONV_TAPS + 1, :] += jnp.sum(dc, axis=0, keepdims=True)

    return pl.pallas_call(
        body, name=name, grid=(c // CONV_COLS, n_rows // tl), in_specs=[ucur, prev, par, par, cur],
        out_specs=(cur, par),
        out_shape=(jax.ShapeDtypeStruct((n_rows, c), F32), jax.ShapeDtypeStruct((8, c), F32)),
        compiler_params=_cparams("parallel", "arbitrary"),
    )(u, u, w8, b8, dout)


def _conv_bwd_in(dc, w8, name, into):
    n_rows, c = dc.shape
    tl = _rows(n_rows)
    cur, _, nxt, par = _conv_specs(n_rows, tl)
    n_l = n_rows // tl
    extra, extra_specs, aliases, col0, width = _into(into, 3, 0)
    out_spec = _conv_specs(n_rows, tl, col0)[0]

    def body(*refs):
        dc_ref, next_ref, w_ref = refs[:3]
        o_ref = refs[-1]
        cur_v = dc_ref[...]
        next8 = jnp.where(pl.program_id(1) == n_l - 1, 0.0, next_ref[...])
        row8 = lax.broadcasted_iota(jnp.int32, next8.shape, 0)
        acc = cur_v * w_ref[CONV_TAPS - 1:CONV_TAPS, :]
        for s in range(1, CONV_TAPS):
            acc = acc + _shift_up(cur_v, next8, s, row8) * w_ref[CONV_TAPS - 1 - s:CONV_TAPS - s, :]
        o_ref[...] = acc.astype(BF16)

    return pl.pallas_call(
        body, name=name, grid=(c // CONV_COLS, n_l), in_specs=[cur, nxt, par] + extra_specs, out_specs=out_spec,
        out_shape=jax.ShapeDtypeStruct((n_rows, width), BF16), input_output_aliases=aliases,
        compiler_params=_cparams("parallel", "parallel"),
    )(dc, dc, w8, *extra)


NORM_GROUP = SSD_D_INNER // SSD_GROUPS


def _gnorm_fwd(y, z, w, name):
    n_rows, c = y.shape
    z, z0, _ = _window(z)
    zoff = z0 // NORM_GROUP

    def body(y_ref, z_ref, w_ref, o_ref):
        zv = z_ref[...]
        yg = y_ref[...] * (zv * _sig(zv))
        r = lax.rsqrt(jnp.mean(yg * yg, axis=-1, keepdims=True) + RMS_EPS)
        o_ref[...] = (yg * r * w_ref[...]).astype(BF16)

    blk = pl.BlockSpec((_rows(n_rows),NORM_GROUP), lambda i, j: (i, j))
    zblk = pl.BlockSpec((_rows(n_rows),NORM_GROUP), lambda i, j: (i, j + zoff))
    wspec = pl.BlockSpec((1, NORM_GROUP), lambda i, j: (0, j))
    return pl.pallas_call(
        body, name=name, grid=(n_rows // _rows(n_rows), c // NORM_GROUP), in_specs=[blk, zblk, wspec], out_specs=blk,
        out_shape=jax.ShapeDtypeStruct((n_rows, c), BF16),
        compiler_params=_cparams("parallel", "parallel"),
    )(y, z, w.reshape(1, c))


def _gnorm_bwd(y, z, w, dyn, name, into):
    n_rows, c = y.shape
    z, z0, _ = _window(z)
    zoff = z0 // NORM_GROUP
    extra, extra_specs, aliases, col0, width = _into(into, 4, 1)
    doff = col0 // NORM_GROUP

    def body(*refs):
        y_ref, z_ref, w_ref, dn_ref = refs[:4]
        dy_ref, dz_ref, acc_ref = refs[-3:]
        @pl.when(pl.program_id(1) == 0)
        def _():
            acc_ref[...] = jnp.zeros_like(acc_ref)

        zv = z_ref[...]
        yv = y_ref[...]
        sz = _sig(zv)
        silu = zv * sz
        yg = yv * silu
        r = lax.rsqrt(jnp.mean(yg * yg, axis=-1, keepdims=True) + RMS_EPS)
        nrm = yg * r
        dn = dn_ref[...]
        acc_ref[0:1, :] += jnp.sum(dn * nrm, axis=0, keepdims=True)
        dnw = dn * w_ref[...]
        dyg = r * (dnw - nrm * jnp.mean(dnw * nrm, axis=-1, keepdims=True))
        dy_ref[...] = dyg * silu
        dz_ref[...] = (dyg * yv * (sz * (1.0 + zv * (1.0 - sz)))).astype(BF16)

    blk = pl.BlockSpec((_rows(n_rows),NORM_GROUP), lambda j, i: (i, j))
    zblk = pl.BlockSpec((_rows(n_rows),NORM_GROUP), lambda j, i: (i, j + zoff))
    wspec = pl.BlockSpec((1, NORM_GROUP), lambda j, i: (0, j))
    aspec = pl.BlockSpec((8, NORM_GROUP), lambda j, i: (0, j))
    return pl.pallas_call(
        body, name=name, grid=(c // NORM_GROUP, n_rows // _rows(n_rows)),
        in_specs=[blk, zblk, wspec, blk] + extra_specs,
        out_specs=(blk, pl.BlockSpec((_rows(n_rows), NORM_GROUP), lambda j, i: (i, j + doff)), aspec),
        out_shape=(jax.ShapeDtypeStruct((n_rows, c), F32), jax.ShapeDtypeStruct((n_rows, width), BF16),
                   jax.ShapeDtypeStruct((8, c), F32)),
        input_output_aliases=aliases,
        compiler_params=_cparams("parallel", "arbitrary"),
    )(y, z, w.reshape(1, c), dyn, *extra)


ATT_SCALE = ATT_HEAD_DIM ** -0.5
ATT_SLOPES = [2.0 ** (-8.0 * (h + 1) / ATT_HEADS) for h in range(ATT_HEADS)]
Q_PER_KV = ATT_HEADS // ATT_KV_HEADS


def _dup_half(t, g, lo):
    tr = pltpu.roll(t, ATT_HEAD_DIM, axis=1)
    return jnp.where(lo, t, tr) if g == 0 else jnp.where(lo, tr, t)


def _att_band(kv_ref, kvp_ref, n):
    cur = kv_ref[...]
    prev = jnp.where(n == 0, 0.0, kvp_ref[...])
    lo = lax.broadcasted_iota(jnp.int32, (ATT_BLOCK, LANE), 1) < ATT_HEAD_DIM
    bands = []
    for g in range(ATT_KV_HEADS):
        kb = jnp.concatenate([_dup_half(prev[:, :LANE], g, lo), _dup_half(cur[:, :LANE], g, lo)], axis=0)
        vb = jnp.concatenate([_dup_half(prev[:, LANE:], g, lo), _dup_half(cur[:, LANE:], g, lo)], axis=0)
        bands.append((kb.astype(BF16), vb.astype(BF16)))
    return bands


def _att_tile(n):
    shape = (2 * ATT_BLOCK, ATT_BLOCK)
    row = lax.broadcasted_iota(jnp.int32, shape, 0)
    i = row & (ATT_BLOCK - 1)
    s = lax.broadcasted_iota(jnp.int32, shape, 1)
    upper = s > i
    dist = ((i - s) & (ATT_BLOCK - 1)).astype(F32)
    dead = upper & (n == 0)
    return upper, dist, dead, row[:, 0:1] < ATT_BLOCK


def _stack_pair(t, lo):
    return jnp.concatenate([jnp.where(lo, t, 0.0), jnp.where(lo, 0.0, t)], axis=0).astype(BF16)


def _att_exp(qs, kb, s_ref, j, tile):
    upper, dist, dead, first = tile
    s2 = _dot(qs, kb, _NT)
    slope = jnp.where(first, ATT_SLOPES[2 * j], ATT_SLOPES[2 * j + 1])
    sink = jnp.where(first, s_ref[0:1, 2 * j:2 * j + 1], s_ref[0:1, 2 * j + 1:2 * j + 2])
    s = jnp.where(upper, s2[:, :ATT_BLOCK], s2[:, ATT_BLOCK:]) - slope * dist
    s = jnp.where(dead, NEG, s)
    m = jnp.maximum(jnp.max(s, axis=-1, keepdims=True), sink)
    return jnp.exp(s - m), jnp.exp(sink - m)


def _band_split(t, upper):
    return jnp.concatenate([jnp.where(upper, t, 0.0), jnp.where(upper, 0.0, t)], axis=1)


def _att_fwd(q, kv, sinks8, name):
    q, q0, _ = _window(q)
    kv, kv0, _ = _window(kv)
    qoff, kvoff = q0 // Q_DIM, kv0 // (2 * LANE)
    n_rows = q.shape[0]
    nb = n_rows // ATT_BLOCK

    def body(q_ref, kv_ref, kvp_ref, s_ref, o_ref, o32_ref):
        n = pl.program_id(0)
        bands = _att_band(kv_ref, kvp_ref, n)
        lo = lax.broadcasted_iota(jnp.int32, (ATT_BLOCK, LANE), 1) < ATT_HEAD_DIM
        tile = _att_tile(n)
        ones_b = jnp.ones((2 * ATT_BLOCK, LANE), BF16)
        for j in range(ATT_HEADS // 2):
            kb, vb = bands[2 * j // Q_PER_KV]
            qs = _stack_pair(q_ref[:, j * LANE:(j + 1) * LANE] * ATT_SCALE, lo)
            p, es = _att_exp(qs, kb, s_ref, j, tile)
            pv = _dot(_band_split(p, tile[0]).astype(BF16), jnp.concatenate([vb, ones_b], axis=1))
            out = pv[:, :LANE] / (pv[:, LANE:] + es)
            out = jnp.where(lo, out[:ATT_BLOCK], out[ATT_BLOCK:])
            o_ref[:, j * LANE:(j + 1) * LANE] = out.astype(BF16)
            o32_ref[:, j * LANE:(j + 1) * LANE] = out

    return pl.pallas_call(
        body, name=name, grid=(nb,),
        in_specs=[pl.BlockSpec((ATT_BLOCK, Q_DIM), lambda n: (n, qoff)),
                  pl.BlockSpec((ATT_BLOCK, 2 * LANE), lambda n: (n, kvoff)),
                  pl.BlockSpec((ATT_BLOCK, 2 * LANE), lambda n: (jnp.maximum(n - 1, 0), kvoff)),
                  pl.BlockSpec((8, LANE), lambda n: (0, 0))],
        out_specs=(pl.BlockSpec((ATT_BLOCK, Q_DIM), lambda n: (n, 0)),) * 2,
        out_shape=(jax.ShapeDtypeStruct((n_rows, Q_DIM), BF16), jax.ShapeDtypeStruct((n_rows, Q_DIM), F32)),
        compiler_params=_cparams("parallel"),
    )(q, kv, kv, sinks8)


def _att_bwd(q, kv, sinks8, out32, dout, name, into):
    q, q0, _ = _window(q)
    kv, kv0, _ = _window(kv)
    qoff, kvoff = q0 // Q_DIM, kv0 // (2 * LANE)
    n_rows = q.shape[0]
    nb = n_rows // ATT_BLOCK

    extra, extra_specs, aliases, col0, width = _into(into, 6, 0)
    dqoff = col0 // Q_DIM

    def body(*refs):
        q_ref, kv_ref, kvp_ref, s_ref, o_ref, do_ref = refs[:6]
        dq_ref, dkv_ref, acc_ref, carry_ref = refs[-4:]
        n = pl.program_id(0)

        @pl.when(n == 0)
        def _():
            acc_ref[...] = jnp.zeros_like(acc_ref)
            carry_ref[...] = jnp.zeros_like(carry_ref)

        @pl.when(n == nb)
        def _():
            dkv_ref[...] = carry_ref[...].astype(BF16)

        @pl.when(n < nb)
        def _():
            bands = _att_band(kv_ref, kvp_ref, n)
            lo = lax.broadcasted_iota(jnp.int32, (ATT_BLOCK, LANE), 1) < ATT_HEAD_DIM
            lane1 = lax.broadcasted_iota(jnp.int32, (1, LANE), 1)
            tile = _att_tile(n)
            upper, first = tile[0], tile[3]
            ones_b = jnp.ones((ATT_BLOCK, LANE), BF16)
            ones2_b = jnp.ones((2 * LANE, LANE), BF16)
            dk_acc = [jnp.zeros((2 * ATT_BLOCK, LANE), F32) for _ in range(ATT_KV_HEADS)]
            dv_acc = [jnp.zeros((2 * ATT_BLOCK, LANE), F32) for _ in range(ATT_KV_HEADS)]
            dsink = jnp.zeros((1, LANE), F32)
            for j in range(ATT_HEADS // 2):
                g = 2 * j // Q_PER_KV
                kb, vb = bands[g]
                qs = _stack_pair(q_ref[:, j * LANE:(j + 1) * LANE] * ATT_SCALE, lo)
                dop = do_ref[:, j * LANE:(j + 1) * LANE].astype(F32)
                dos = _stack_pair(dop, lo)
                pu, es = _att_exp(qs, kb, s_ref, j, tile)
                inv = 1.0 / (_dot(pu.astype(BF16), ones_b) + es)
                p = pu * inv
                od = dos.astype(F32) * jnp.concatenate([o_ref[:, j * LANE:(j + 1) * LANE]] * 2, axis=0)
                od_hi = od.astype(BF16)
                delta = _dot(jnp.concatenate([od_hi, (od - od_hi.astype(F32)).astype(BF16)], axis=1), ones2_b)
                dp2 = _dot(dos, vb, _NT)
                dp = jnp.where(upper, dp2[:, :ATT_BLOCK], dp2[:, ATT_BLOCK:])
                ds2 = _band_split(p * (dp - delta), upper)
                psd = jnp.sum(es * inv * delta, axis=0, keepdims=True)
                psd0 = jnp.sum(jnp.where(first, es * inv * delta, 0.0), axis=0, keepdims=True)
                dsink = jnp.where(lane1 == 2 * j, -psd0, jnp.where(lane1 == 2 * j + 1, psd0 - psd, dsink))
                ds2_b = ds2.astype(BF16)
                dq = _dot(ds2_b, kb) * ATT_SCALE
                dq_ref[:, j * LANE:(j + 1) * LANE] = jnp.where(lo, dq[:ATT_BLOCK], dq[ATT_BLOCK:]).astype(BF16)
                dk_acc[g] = dk_acc[g] + _dot(ds2_b, qs, _TN)
                dv_acc[g] = dv_acc[g] + _dot(_band_split(p, upper).astype(BF16), dos, _TN)
            acc_ref[0:1, :] += dsink
            lo2 = lax.broadcasted_iota(jnp.int32, (2 * ATT_BLOCK, LANE), 1) < ATT_HEAD_DIM
            folded = []
            for acc in (dk_acc, dv_acc):
                t0 = acc[0] + pltpu.roll(acc[0], ATT_HEAD_DIM, axis=1)
                t1 = acc[1] + pltpu.roll(acc[1], ATT_HEAD_DIM, axis=1)
                folded.append(jnp.where(lo2, t0, t1))
            band = jnp.concatenate(folded, axis=1)
            dkv_ref[...] = (carry_ref[...] + band[:ATT_BLOCK]).astype(BF16)
            carry_ref[...] = band[ATT_BLOCK:]

    def qmap(n):
        return (jnp.minimum(n, nb - 1), 0)

    return pl.pallas_call(
        body, name=name, grid=(nb + 1,),
        in_specs=[pl.BlockSpec((ATT_BLOCK, Q_DIM), lambda n: (jnp.minimum(n, nb - 1), qoff)),
                  pl.BlockSpec((ATT_BLOCK, 2 * LANE), lambda n: (jnp.minimum(n, nb - 1), kvoff)),
                  pl.BlockSpec((ATT_BLOCK, 2 * LANE),
                               lambda n: (jnp.maximum(jnp.minimum(n, nb - 1) - 1, 0), kvoff)),
                  pl.BlockSpec((8, LANE), lambda n: (0, 0)),
                  pl.BlockSpec((ATT_BLOCK, Q_DIM), qmap),
                  pl.BlockSpec((ATT_BLOCK, Q_DIM), qmap)] + extra_specs,
        out_specs=(pl.BlockSpec((ATT_BLOCK, Q_DIM), lambda n: (jnp.minimum(n, nb - 1), dqoff)),
                   pl.BlockSpec((ATT_BLOCK, 2 * LANE), lambda n: (jnp.maximum(n - 1, 0), 0)),
                   pl.BlockSpec((8, LANE), lambda n: (0, 0))),
        out_shape=(jax.ShapeDtypeStruct((n_rows, width), BF16), jax.ShapeDtypeStruct((n_rows, 2 * LANE), BF16),
                   jax.ShapeDtypeStruct((8, LANE), F32)),
        input_output_aliases=aliases,
        scratch_shapes=[pltpu.VMEM((ATT_BLOCK, 2 * LANE), F32)],
        compiler_params=_cparams("arbitrary"),
    )(q, kv, kv, sinks8, out32, dout, *extra)


HEADS_PER_GROUP = SSD_HEADS // SSD_GROUPS
PAIRS_PER_GROUP = HEADS_PER_GROUP // 2
T = SSD_CHUNK


def _cumsum_mm(mat, x):
    hi = x.astype(BF16)
    r = x - hi.astype(F32)
    mid = r.astype(BF16)
    lo = (r - mid.astype(F32)).astype(BF16)
    w = x.shape[1]
    out = _dot(mat, jnp.concatenate([hi, mid, lo], axis=1))
    return out[:, :w] + out[:, w:2 * w] + out[:, 2 * w:]


def _ssd_prep(dtr_ref, par_ref):
    dt = _softplus(dtr_ref[...] + par_ref[0:1, :])
    a = -jnp.exp(par_ref[1:2, :])
    ri = lax.broadcasted_iota(jnp.int32, (T, T), 0)
    ci = lax.broadcasted_iota(jnp.int32, (T, T), 1)
    cs = _cumsum_mm((ri >= ci).astype(BF16), dt * a)
    lo = lax.broadcasted_iota(jnp.int32, (T, LANE), 1) < SSD_CHUNK // 2

    def expand(arr):
        rows = arr.shape[0]
        return jnp.concatenate([jnp.where(lo[:rows], arr[:, 2 * j:2 * j + 1], arr[:, 2 * j + 1:2 * j + 2])
                                for j in range(PAIRS_PER_GROUP)], axis=1)

    tot = cs[T - 1:T, :]
    return {"dt": dt, "a": a, "cs": cs, "cst": cs.T, "lo": lo, "ri": ri, "ci": ci, "expand": expand,
            "dt_x": expand(dt), "ecs_x": expand(jnp.exp(cs)), "dec_x": expand(jnp.exp(tot - cs)),
            "et_x": expand(jnp.exp(tot)), "etot": jnp.exp(tot), "dsk_x": expand(par_ref[2:3, :])}


def _wide_masks():
    r = lax.broadcasted_iota(jnp.int32, (T, 2 * T), 0)
    l = lax.broadcasted_iota(jnp.int32, (T, 2 * T), 1)
    s = l & (T - 1)
    return r >= s, s >= r, l < T


def _wide_cs(q, k0, even):
    cs, cst = q["cs"], q["cst"]
    col = jnp.where(even, cs[:, k0:k0 + 1], cs[:, k0 + 1:k0 + 2])
    row = jnp.concatenate([cst[k0:k0 + 1, :], cst[k0 + 1:k0 + 2, :]], axis=1)
    return col, row


def _ssd_fwd(xs, bm, cm, dtr, par, name):
    dtr, dt0, _ = _window(dtr)
    dtoff = dt0 // LANE
    n_rows = xs.shape[0]
    nc = n_rows // T
    gw = PAIRS_PER_GROUP * LANE

    def body(x_ref, b_ref, c_ref, dtr_ref, par_ref, y_ref, hs_ref, h_ref):
        @pl.when(pl.program_id(1) == 0)
        def _():
            h_ref[...] = jnp.zeros_like(h_ref)

        q = _ssd_prep(dtr_ref, par_ref)
        lo = q["lo"]
        tri_w, _, even = _wide_masks()
        bg_b = b_ref[...].astype(BF16)
        cg_b = c_ref[...].astype(BF16)
        xv = x_ref[...]
        xdt = xv * q["dt_x"]
        h = h_ref[...]
        hs_ref[0, 0] = h
        yo = q["ecs_x"] * _dot(cg_b, h.astype(BF16))
        h_ref[...] = h * q["et_x"] + _dot(bg_b, (xdt * q["dec_x"]).astype(BF16), _TN)
        cb = _dot(cg_b, bg_b, _NT)
        cb_w = jnp.concatenate([cb, cb], axis=1)
        for j in range(PAIRS_PER_GROUP):
            col, row = _wide_cs(q, 2 * j, even)
            m_w = (jnp.exp(jnp.where(tri_w, col - row, NEG)) * cb_w).astype(BF16)
            sl = slice(j * LANE, (j + 1) * LANE)
            y_ref[:, sl] = (_dot(m_w, _stack_pair(xdt[:, sl], lo)) + yo[:, sl] + q["dsk_x"][:, sl] * xv[:, sl])

    return pl.pallas_call(
        body, name=name, grid=(SSD_GROUPS, nc),
        in_specs=[pl.BlockSpec((T, gw), lambda g, c: (c, g)),
                  pl.BlockSpec((T, SSD_STATE), lambda g, c: (c, g)),
                  pl.BlockSpec((T, SSD_STATE), lambda g, c: (c, g)),
                  pl.BlockSpec((T, LANE), lambda g, c: (c, g + dtoff)),
                  pl.BlockSpec((8, LANE), lambda g, c: (0, g))],
        out_specs=(pl.BlockSpec((T, gw), lambda g, c: (c, g)),
                   pl.BlockSpec((1, 1, SSD_STATE, gw), lambda g, c: (g, c, 0, 0))),
        out_shape=(jax.ShapeDtypeStruct((n_rows, SSD_D_INNER), F32),
                   jax.ShapeDtypeStruct((SSD_GROUPS, nc, SSD_STATE, gw), F32)),
        scratch_shapes=[pltpu.VMEM((SSD_STATE, gw), F32)],
        compiler_params=_cparams("parallel", "arbitrary"),
    )(xs, bm, cm, dtr, par)


def _ssd_bwd(xs, bm, cm, dtr, par, hs, dy, name, into):
    dtr, dt0, _ = _window(dtr)
    dtoff = dt0 // LANE
    n_rows = xs.shape[0]
    nc = n_rows // T
    gw = PAIRS_PER_GROUP * LANE
    extra, extra_specs, aliases, col0, width = _into(into, 7, 3)
    ddoff = col0 // LANE

    def body(*refs):
        x_ref, b_ref, c_ref, dtr_ref, par_ref, hs_ref, dy_ref = refs[:7]
        dx_ref, db_ref, dc_ref, ddtr_ref, acc_ref, dh_ref = refs[-6:]

        @pl.when(pl.program_id(1) == 0)
        def _():
            dh_ref[...] = jnp.zeros_like(dh_ref)
            acc_ref[...] = jnp.zeros_like(acc_ref)

        q = _ssd_prep(dtr_ref, par_ref)
        lo, dt, a = q["lo"], q["dt"], q["a"]
        tri_w, trit_w, even = _wide_masks()
        lane = lax.broadcasted_iota(jnp.int32, (T, LANE), 1)
        lane1 = lane[0:1, :]
        last_row = lax.broadcasted_iota(jnp.int32, (T, 1), 0) == T - 1
        bg_b = b_ref[...].astype(BF16)
        cg_b = c_ref[...].astype(BF16)
        xv = x_ref[...]
        dyv = dy_ref[...]
        xdt = xv * q["dt_x"]
        h = hs_ref[0, 0]
        dhn = dh_ref[...]
        h_b, dhn_b = h.astype(BF16), dhn.astype(BF16)
        yo = q["ecs_x"] * _dot(cg_b, h_b)
        bdh = q["dec_x"] * _dot(bg_b, dhn_b)
        dye = (dyv * q["ecs_x"]).astype(BF16)
        xd = (xdt * q["dec_x"]).astype(BF16)
        dcg = _dot(dye, h_b, _NT)
        dbg = _dot(xd, dhn_b, _NT)
        dh_ref[...] = dhn * q["et_x"] + _dot(cg_b, dye, _TN)
        e4_all = xdt * bdh
        f_all = dyv * yo - e4_all
        tot_row = jnp.sum(e4_all, axis=0, keepdims=True) + q["et_x"] * jnp.sum(h * dhn, axis=0, keepdims=True)
        dsk_row = jnp.sum(dyv * xv, axis=0, keepdims=True)
        cb = _dot(cg_b, bg_b, _NT)
        cbt = _dot(bg_b, cg_b, _NT)
        cb_w = jnp.concatenate([cb, cb], axis=1)
        cbt_w = jnp.concatenate([cbt, cbt], axis=1)
        dcb = jnp.zeros((T, T), F32)
        dcbt = jnp.zeros((T, T), F32)
        dcs_acc = jnp.zeros((T, LANE), F32)
        ddt_acc = jnp.zeros((T, LANE), F32)
        dsk_acc = jnp.zeros((1, LANE), F32)
        tot_acc = jnp.zeros((1, LANE), F32)
        ind_r = lax.broadcasted_iota(jnp.int32, (2 * T, LANE), 0)
        ind_l = lax.broadcasted_iota(jnp.int32, (2 * T, LANE), 1)

        def halves(t):
            return (jnp.sum(jnp.where(lo[0:1], t, 0.0), axis=-1, keepdims=True),
                    jnp.sum(jnp.where(lo[0:1], 0.0, t), axis=-1, keepdims=True))

        def split2(t):
            hi = t.astype(BF16)
            return jnp.concatenate([hi, (t - hi.astype(F32)).astype(BF16)], axis=1)

        for j in range(PAIRS_PER_GROUP):
            k0, k1 = 2 * j, 2 * j + 1
            sl = slice(j * LANE, (j + 1) * LANE)
            col, row = _wide_cs(q, k0, even)
            lm_w = jnp.exp(jnp.where(tri_w, col - row, NEG))
            lmt_w = jnp.exp(jnp.where(trit_w, row - col, NEG))
            dyp, xp = dyv[:, sl], xdt[:, sl]
            dym, xm = _stack_pair(dyp, lo), _stack_pair(xp, lo)
            dm_w = _dot(dyp.astype(BF16), xm, _NT)
            dmt_w = _dot(xp.astype(BF16), dym, _NT)
            mm_w = lm_w * cb_w
            mmt_w = lmt_w * cbt_w
            dxdt = _dot(mmt_w.astype(BF16), dym) + bdh[:, sl]
            g1 = dm_w * lm_w
            g2 = dmt_w * lmt_w
            dcb = dcb + g1[:, :T] + g1[:, T:]
            dcbt = dcbt + g2[:, :T] + g2[:, T:]
            ind_w = jnp.where(ind_l == jnp.where(ind_r < T, k0, k1), 1.0, 0.0).astype(BF16)
            ind_p = jnp.where(ind_l[:T] == jnp.where(ind_r[:T] < SSD_CHUNK // 2, k0, k1), 1.0, 0.0).astype(BF16)
            dcs_acc = dcs_acc + _dot(
                jnp.concatenate([split2(dm_w * mm_w - dmt_w * mmt_w), split2(f_all[:, sl])], axis=1),
                jnp.concatenate([ind_w, ind_w, ind_p, ind_p], axis=0))
            ddt_acc = ddt_acc + _dot(split2(dxdt * xv[:, sl]), jnp.concatenate([ind_p, ind_p], axis=0))
            tot2 = halves(tot_row[:, sl])
            tot_acc = jnp.where(lane1 == k0, tot2[0], jnp.where(lane1 == k1, tot2[1], tot_acc))
            dsk2 = halves(dsk_row[:, sl])
            dsk_acc = jnp.where(lane1 == k0, dsk2[0], jnp.where(lane1 == k1, dsk2[1], dsk_acc))
            dx_ref[:, sl] = dxdt * q["dt_x"][:, sl] + q["dsk_x"][:, sl] * dyp
        dcs_acc = dcs_acc + jnp.where(last_row, tot_acc, 0.0)
        dc_ref[...] = dcg + _dot(dcb.astype(BF16), bg_b)
        db_ref[...] = dbg + _dot(dcbt.astype(BF16), cg_b)
        dda = _cumsum_mm((q["ci"] >= q["ri"]).astype(BF16), dcs_acc)
        ddt = ddt_acc + dda * a
        ddtr = ddt * _sig(dtr_ref[...] + par_ref[0:1, :])
        ddtr_ref[...] = ddtr.astype(BF16)
        acc_ref[0:1, :] += jnp.sum(ddtr, axis=0, keepdims=True)
        acc_ref[1:2, :] += jnp.sum(dda * dt, axis=0, keepdims=True) * a
        acc_ref[2:3, :] += dsk_acc

    def rev(g, c):
        return (nc - 1 - c, g)

    return pl.pallas_call(
        body, name=name, grid=(SSD_GROUPS, nc),
        in_specs=[pl.BlockSpec((T, gw), rev),
                  pl.BlockSpec((T, SSD_STATE), rev),
                  pl.BlockSpec((T, SSD_STATE), rev),
                  pl.BlockSpec((T, LANE), lambda g, c: (nc - 1 - c, g + dtoff)),
                  pl.BlockSpec((8, LANE), lambda g, c: (0, g)),
                  pl.BlockSpec((1, 1, SSD_STATE, gw), lambda g, c: (g, nc - 1 - c, 0, 0)),
                  pl.BlockSpec((T, gw), rev)] + extra_specs,
        out_specs=(pl.BlockSpec((T, gw), rev),
                   pl.BlockSpec((T, SSD_STATE), rev),
                   pl.BlockSpec((T, SSD_STATE), rev),
                   pl.BlockSpec((T, LANE), lambda g, c: (nc - 1 - c, g + ddoff)),
                   pl.BlockSpec((8, LANE), lambda g, c: (0, g))),
        out_shape=(jax.ShapeDtypeStruct((n_rows, SSD_D_INNER), F32),
                   jax.ShapeDtypeStruct((n_rows, BC_DIM), F32),
                   jax.ShapeDtypeStruct((n_rows, BC_DIM), F32),
                   jax.ShapeDtypeStruct((n_rows, width), BF16),
                   jax.ShapeDtypeStruct((8, DT_PAD), F32)),
        input_output_aliases=aliases,
        scratch_shapes=[pltpu.VMEM((SSD_STATE, gw), F32)],
        compiler_params=_cparams("parallel", "arbitrary"),
    )(xs, bm, cm, dtr, par, hs, dy, *extra)


ADAM_ROWS = 256


def _adamw(lands, w, m, v, name):
    na = len(lands)
    n_slots, r, wd = lands[0].shape
    tr = r if r <= 2 * ADAM_ROWS else ADAM_ROWS
    nj = r // tr
    bc1 = 1.0 - ADAM_B1 ** ADAM_STEP
    bc2 = 1.0 - ADAM_B2 ** ADAM_STEP

    def body(*refs):
        l_refs = refs[:na]
        w_ref, m_ref, v_ref, g_ref, d_ref, nm_ref, nv_ref = refs[na:]
        for a in range(na):
            @pl.when(pl.program_id(0) == a)
            def _(l_ref=l_refs[a]):
                g = l_ref[0].astype(F32)
                for s in range(1, n_slots):
                    g = g + l_ref[s].astype(F32)
                mn = ADAM_B1 * m_ref[0] + (1.0 - ADAM_B1) * g
                vn = ADAM_B2 * v_ref[0] + (1.0 - ADAM_B2) * (g * g)
                mh = mn / bc1
                vh = vn / bc2
                g_ref[0] = g
                nm_ref[0] = mn
                nv_ref[0] = vn
                d_ref[0] = -ADAM_LR * (mh / (jnp.sqrt(vh) + ADAM_EPS) + ADAM_WD * w_ref[0])

    def land_spec(a):
        return pl.BlockSpec((n_slots, tr, wd),
                            lambda i, j: (0, jnp.where(i == a, j, jnp.where(i < a, 0, nj - 1)), 0))

    blk = pl.BlockSpec((1, tr, wd), lambda i, j: (i, j, 0))
    shp = jax.ShapeDtypeStruct((na, r, wd), F32)
    return pl.pallas_call(
        body, name=name, grid=(na, nj), in_specs=[land_spec(a) for a in range(na)] + [blk, blk, blk],
        out_specs=(blk, blk, blk, blk), out_shape=(shp, shp, shp, shp),
        compiler_params=_cparams("arbitrary", "arbitrary"),
    )(*lands, w, m, v)


def _mesh_pos():
    return lax.axis_index("x"), lax.axis_index("y"), lax.axis_index("c")


def _peer(pos, k):
    x, y, c = pos
    px = 1 - x if (k >> 2) & 1 else x
    py = 1 - y if (k >> 1) & 1 else y
    pc = 1 - c if k & 1 else c
    return px, py, pc


def _flat(pos):
    return 4 * pos[0] + 2 * pos[1] + pos[2]


HBM_SPEC = pl.BlockSpec(memory_space=pl.ANY)


ROW_SHARDED = ("w_ssd_out", "w_att_out", "w_mix_out", "w_ffn_down")
COL_SHARDED = ("w_in", "w_ffn_gate", "w_ffn_up")
GATHERED = ROW_SHARDED + COL_SHARDED + ("conv_w",)


SEM_SPEC = pl.BlockSpec(memory_space=pltpu.SEMAPHORE)
TOKEN = jax.ShapeDtypeStruct((8, LANE), F32)
SPLIT_EFFECT = pltpu.SideEffectType.DATAFLOW_SIDE_EFFECTING
GATHER_ROWS = "gather_rows"
GATHER_SLOT = "gather_slot"
SCATTER_ROWS = "scatter_rows"
SCATTER_SLOT = "scatter_slot"


def _land_shape(kind, src):
    if kind == GATHER_ROWS:
        return (N_DEV * src.shape[0],) + src.shape[1:]
    if kind == GATHER_SLOT:
        return (N_DEV,) + src.shape
    if kind == SCATTER_ROWS:
        return (N_DEV, src.shape[0] // N_DEV) + src.shape[1:]
    return src.shape


def _views(kind, src_ref, land_ref, pos, k):
    me = _flat(pos)
    if kind == GATHER_ROWS:
        r = src_ref.shape[0]
        return src_ref, land_ref.at[pl.ds(pl.multiple_of(me * r, 16), r), :]
    if kind == GATHER_SLOT:
        return src_ref, land_ref.at[me]
    dev = _flat(_peer(pos, k))
    if kind == SCATTER_ROWS:
        r = land_ref.shape[1]
        return src_ref.at[pl.ds(pl.multiple_of(dev * r, 16), r), :], land_ref.at[k]
    return src_ref.at[dev], land_ref.at[k]


def _hbm(x):
    return pltpu.with_memory_space_constraint(x, pltpu.HBM)


def _exchange_start(items, after, name):
    kinds = [k for k, _ in items]
    srcs = [_hbm(s) for _, s in items]
    lands = [_hbm(lax.empty(_land_shape(k, s), s.dtype)) for k, s in items]
    n = len(items)
    n_copy = n * (N_DEV - 1)

    def body(*refs):
        src_refs, land_refs = refs[:n], refs[n:2 * n]
        send_sems, recv_sems = refs[2 * n + 1], refs[2 * n + 2]
        token_ref = refs[4 * n + 3]
        pos = _mesh_pos()
        for i, kind in enumerate(kinds):
            for k in range(1, N_DEV):
                s, d = _views(kind, src_refs[i], land_refs[i], pos, k)
                j = i * (N_DEV - 1) + k - 1
                pltpu.make_async_remote_copy(src_ref=s, dst_ref=d, send_sem=send_sems.at[j], recv_sem=recv_sems.at[j],
                                             device_id=_peer(pos, k), device_id_type=MESH_ID).start()
        token_ref[...] = jnp.zeros_like(token_ref)

    arrs = srcs + lands
    outs = pl.pallas_call(
        body, name=name,
        in_specs=[HBM_SPEC] * (2 * n + 1),
        out_specs=[SEM_SPEC, SEM_SPEC] + [HBM_SPEC] * (2 * n) + [pl.BlockSpec(memory_space=pltpu.VMEM)],
        out_shape=[pltpu.SemaphoreType.DMA((n_copy,)), pltpu.SemaphoreType.DMA((n_copy,))]
        + [pltpu.HBM(a.shape, a.dtype) for a in arrs] + [TOKEN],
        input_output_aliases={i: 2 + i for i in range(2 * n)},
        compiler_params=pltpu.CompilerParams(has_side_effects=SPLIT_EFFECT),
    )(*arrs, after)
    return {"kinds": kinds, "send": outs[0], "recv": outs[1], "arrs": outs[2:2 + 2 * n], "token": outs[-1]}


def _exchange_wait(ex, after, name):
    kinds = ex["kinds"]
    n = len(kinds)

    def body(*refs):
        src_refs, land_refs = refs[:n], refs[n:2 * n]
        send_sems, recv_sems = refs[2 * n], refs[2 * n + 1]
        token_ref = refs[-1]
        pos = _mesh_pos()
        for i, kind in enumerate(kinds):
            for k in range(1, N_DEV):
                s, d = _views(kind, src_refs[i], land_refs[i], pos, k)
                j = i * (N_DEV - 1) + k - 1
                cp = pltpu.make_async_remote_copy(src_ref=s, dst_ref=d, send_sem=send_sems.at[j],
                                                  recv_sem=recv_sems.at[j], device_id=_peer(pos, k),
                                                  device_id_type=MESH_ID)
                cp.wait_send()
                cp.wait_recv()
        token_ref[...] = jnp.zeros_like(token_ref)

    outs = pl.pallas_call(
        body, name=name,
        in_specs=[HBM_SPEC] * (2 * n) + [SEM_SPEC, SEM_SPEC, HBM_SPEC],
        out_specs=[HBM_SPEC] * (2 * n) + [pl.BlockSpec(memory_space=pltpu.VMEM)],
        out_shape=[pltpu.HBM(a.shape, a.dtype) for a in ex["arrs"]] + [TOKEN],
        input_output_aliases={i: i for i in range(2 * n)},
        compiler_params=pltpu.CompilerParams(has_side_effects=SPLIT_EFFECT),
    )(*ex["arrs"], ex["send"], ex["recv"], after)
    lands = [_place_own(k, s, d) for k, s, d in zip(kinds, outs[:n], outs[n:2 * n])]
    return lands, outs[-1]


def _place_own(kind, src, land):
    me = _flat(_mesh_pos())
    zeros = (0,) * (src.ndim - 1)
    if kind == GATHER_ROWS:
        return lax.dynamic_update_slice(land, src, (me * src.shape[0],) + zeros)
    if kind == GATHER_SLOT:
        return lax.dynamic_update_slice(land, src[None], (me,) + (0,) * src.ndim)
    if kind == SCATTER_ROWS:
        r = land.shape[1]
        own = lax.dynamic_slice(src, (me * r,) + zeros, (r,) + src.shape[1:])
    else:
        own = lax.dynamic_index_in_dim(src, me, 0, keepdims=False)
    return lax.dynamic_update_slice(land, own[None], (0,) * land.ndim)


def _all_gather_small(x, name):
    r, w = x.shape

    def body(x_ref, out_ref, send_sems, recv_sems):
        pos = _mesh_pos()
        me = _flat(pos)
        copies = []
        for k in range(1, N_DEV):
            cp = pltpu.make_async_remote_copy(
                src_ref=x_ref, dst_ref=out_ref.at[me], send_sem=send_sems.at[k - 1], recv_sem=recv_sems.at[k - 1],
                device_id=_peer(pos, k), device_id_type=MESH_ID)
            cp.start()
            copies.append(cp)
        out_ref[me] = x_ref[...]
        for cp in copies:
            cp.wait()

    vmem = pl.BlockSpec(memory_space=pltpu.VMEM)
    return pl.pallas_call(
        body, name=name, in_specs=[vmem], out_specs=vmem,
        out_shape=jax.ShapeDtypeStruct((N_DEV, r, w), x.dtype),
        scratch_shapes=[pltpu.SemaphoreType.DMA((N_DEV - 1,)), pltpu.SemaphoreType.DMA((N_DEV - 1,))],
        compiler_params=pltpu.CompilerParams(has_side_effects=True),
    )(x)


def _cols(g, lo, hi):
    c = g.shape[-1]
    parts = []
    for d in range(N_DEV):
        a, b = max(lo, d * c), min(hi, (d + 1) * c)
        if a < b:
            parts.append(g[d, :, a - d * c:b - d * c])
    return parts[0] if len(parts) == 1 else jnp.concatenate(parts, axis=1)


def _col_chunks(g):
    c = g.shape[-1] // N_DEV
    return jnp.stack([g[:, d * c:(d + 1) * c] for d in range(N_DEV)])


IN_PART = ("w_in", "conv_w")
OUT_PART = ROW_SHARDED + ("w_ffn_gate", "w_ffn_up")
TRANSPOSED = ("w_ffn_gate", "w_ffn_up")


def _gather_items(w, names, l):
    items = []
    for n in names:
        blk = w[n][l] if n == "conv_w" else w[n][l].astype(BF16)
        if n in TRANSPOSED:
            blk = blk.T
        items.append((GATHER_ROWS if n in ROW_SHARDED + TRANSPOSED else GATHER_SLOT, blk))
    return items


def _scatter_items(grads, names):
    def chunked(g):
        return g if g.ndim == 3 else _col_chunks(g)

    return [(SCATTER_ROWS, grads[n]) if n in ROW_SHARDED + TRANSPOSED else (SCATTER_SLOT, chunked(grads[n]))
            for n in names]


SMALL = ("ln_in_g", "ln_in_b", "conv_b", "dt_bias", "a_log", "d_skip", "ssd_norm_w", "att_sinks",
         "ln_mix_g", "ln_mix_b", "ln_ffn_g", "ln_ffn_b")


def _pack_small(vals):
    flat = jnp.concatenate([vals[n].reshape(-1) for n in SMALL])
    n = flat.shape[0]
    rows = -(-n // LANE)
    rows = -(-rows // 8) * 8
    return jnp.pad(flat, (0, rows * LANE - n)).reshape(rows, LANE)


def _unpack_small(buf, shapes):
    flat = buf.reshape(-1)
    off = 0
    out = {}
    for n in SMALL:
        cnt = math.prod(shapes[n])
        out[n] = flat[off:off + cnt].reshape(shapes[n])
        off += cnt
    return out


def _to_group_major(v):
    lead = v.shape[:-1]
    t = v.reshape(lead + (SSD_GROUPS, HEADS_PER_GROUP))
    t = jnp.pad(t, [(0, 0)] * len(lead) + [(0, 0), (0, LANE - HEADS_PER_GROUP)])
    return t.reshape(lead + (DT_PAD,))


def _from_group_major(v):
    lead = v.shape[:-1]
    return v.reshape(lead + (SSD_GROUPS, LANE))[..., :HEADS_PER_GROUP].reshape(lead + (SSD_HEADS,))


def _rows8(v):
    return jnp.pad(v, ((0, 8 - v.shape[0]), (0, 0)))


IN_OFFS = {"q": (0, 1024), "kv": (1024, 1280), "z": (1280, 3328), "xs": (3328, 5376), "b": (5376, 5888),
           "c": (5888, 6400), "dt": (6400, 6432), "gl": (6432, 8480)}
PIECES = ("q", "kv", "z", "xs", "b", "c", "dt", "gl")


CAT = ("z", "xs", "gl", "q", "b", "c", "dt", "kv")
CAT_WIDTH = {"q": 1024, "z": 2048, "xs": 2048, "gl": 2048, "b": 512, "c": 512, "kv": 256, "dt": DT_PAD}
CAT_OFF = {p: sum(CAT_WIDTH[q] for q in CAT[:i]) for i, p in enumerate(CAT)}
CAT_DIM = sum(CAT_WIDTH.values())
MAIN_DIM = CAT_OFF["kv"]


def _cat_w_in(g):
    pieces = {p: _cols(g, lo, hi) for p, (lo, hi) in IN_OFFS.items()}
    pieces["dt"] = _to_group_major(pieces["dt"])
    return jnp.concatenate([pieces[p] for p in CAT], axis=1)


def _dw_in_chunks(dw_main, dw_kv):
    dt = _from_group_major(dw_main[:, CAT_OFF["dt"]:CAT_OFF["dt"] + DT_PAD])
    shard = IN_OFFS[PIECES[-1]][1] // N_DEV

    def piece(pc, a, b):
        if pc == "dt":
            return dt[:, a:b]
        if pc == "kv":
            return dw_kv[:, a:b]
        return dw_main[:, CAT_OFF[pc] + a:CAT_OFF[pc] + b]

    chunks = []
    for d in range(N_DEV):
        parts = []
        for pc in PIECES:
            lo, hi = IN_OFFS[pc]
            a, b = max(lo, d * shard), min(hi, (d + 1) * shard)
            if a < b:
                parts.append(piece(pc, a - lo, b - lo))
        chunks.append(parts[0] if len(parts) == 1 else jnp.concatenate(parts, axis=1))
    return jnp.stack(chunks)


def _params_out(W):
    return {n: W[n] for n in OUT_PART}


def _params_in(l, W, sm):
    p = {"w_cat": _cat_w_in(W["w_in"])}
    cw = _cols(W["conv_w"], 0, SSD_D_INNER + 2 * BC_DIM)
    cb = sm["conv_b"][l]
    segs = {"xs": (0, 2048), "b": (2048, 2560), "c": (2560, 3072)}
    p["conv_w8"] = {s: _rows8(cw[:, lo:hi]) for s, (lo, hi) in segs.items()}
    p["conv_b8"] = {s: _rows8(cb[None, lo:hi]) for s, (lo, hi) in segs.items()}
    p["ssd_par"] = _rows8(jnp.stack([_to_group_major(sm["dt_bias"][l]), _to_group_major(sm["a_log"][l]),
                                     _to_group_major(sm["d_skip"][l])]))
    p["norm_w"] = sm["ssd_norm_w"][l]
    p["sinks8"] = _rows8(jnp.pad(sm["att_sinks"][l], (0, LANE - ATT_HEADS))[None])
    for n in ("ln_mix_g", "ln_mix_b", "ln_ffn_g", "ln_ffn_b"):
        p[n] = sm[n][l]
    return p


def _fwd_mixers(h0, p, l, dep=None):
    tag = f"l{l}_"
    a = {"h0": h0}
    proj = _mm(h0, p["w_cat"], "nn", tag + "proj", dep=dep)
    for pc in CAT:
        a[pc] = (proj, CAT_OFF[pc], CAT_WIDTH[pc])
    for s in ("xs", "b", "c"):
        a[s + "c"] = _conv_fwd(a[s], p["conv_w8"][s], p["conv_b8"][s], tag + "conv_" + s)
    a["y"], a["hs"] = _ssd_fwd(a["xsc"], a["bc"], a["cc"], a["dt"], p["ssd_par"], tag + "ssd_fwd")
    a["yn"] = _gnorm_fwd(a["y"], a["z"], p["norm_w"], tag + "gnorm")
    a["att"], a["att32"] = _att_fwd(a["q"], a["kv"], p["sinks8"], tag + "att_fwd")
    return a


def _fwd_out(a, p, l, dep=None):
    tag = f"l{l}_"
    h0 = a["h0"]
    a["ya"], a["yb"], a["merged"] = _branch_out(a["yn"], a["att"], p["w_ssd_out"], p["w_att_out"], a["gl"],
                                                tag + "branch_out", dep=dep)
    a["mix"] = _mm(a["merged"], p["w_mix_out"], "nn", tag + "mix_out")
    a["h1"] = _ln_fwd(h0, a["mix"], p["ln_mix_g"], p["ln_mix_b"], ALPHA, tag + "ln_mix")
    a["fg"], a["fu"], a["act"] = _ffn_in(a["h1"], p["w_ffn_gate"], p["w_ffn_up"], tag + "ffn_in")
    a["ffn"] = _mm(a["act"], p["w_ffn_down"], "nn", tag + "ffn_down")
    a["h2"] = _ln_fwd(a["h1"], a["ffn"], p["ln_ffn_g"], p["ln_ffn_b"], ALPHA, tag + "ln_ffn")
    return a


def _dw(x, dy, name, dep=None):
    return _mm(x, dy, "tn", name, out_dtype=BF16, dep=dep)


def _bwd_out(a, p, dh2, l, dep=None):
    tag = f"l{l}_b_"
    gw, gs = {}, {}
    du2, acc = _ln_bwd(a["h1"], a["ffn"], p["ln_ffn_g"], dh2, ALPHA, tag + "ln_ffn")
    gs["ln_ffn_g"], gs["ln_ffn_b"] = acc[0], acc[1]
    gw["w_ffn_down"] = _dw(a["act"], du2, tag + "dw_down", dep=dep)
    dfg, dfu = _ffn_dact(du2, p["w_ffn_down"], a["fg"], a["fu"], tag + "ffn_dact", dep=dep)
    gw["w_ffn_gate"] = _dw(dfg, a["h1"], tag + "dw_gate")
    gw["w_ffn_up"] = _dw(dfu, a["h1"], tag + "dw_up")
    dh1 = _mm(dfg, p["w_ffn_gate"], "nn", tag + "dh1_gate", add=du2, add_scale=ALPHA)
    dh1 = _mm(dfu, p["w_ffn_up"], "nn", tag + "dh1_up", add=dh1)
    du1, acc = _ln_bwd(a["h0"], a["mix"], p["ln_mix_g"], dh1, ALPHA, tag + "ln_mix")
    gs["ln_mix_g"], gs["ln_mix_b"] = acc[0], acc[1]
    gw["w_mix_out"] = _dw(a["merged"], du1, tag + "dw_mix")
    dya, dyb, dproj = _merge_bwd(a["gl"], a["ya"], a["yb"], du1, p["w_mix_out"], tag + "merge",
                                 (None, CAT_OFF["gl"], MAIN_DIM))
    gw["w_ssd_out"] = _dw(a["yn"], dya, tag + "dw_ssd")
    gw["w_att_out"] = _dw(a["att"], dyb, tag + "dw_att")
    return {"du1": du1, "dya": dya, "dyb": dyb, "dproj": dproj}, gw, gs


def _bwd_mixers(a, p, carry, l, dep=None):
    tag = f"l{l}_b_"
    gs = {}
    du1, dproj = carry["du1"], carry["dproj"]

    def win(pc):
        return (dproj, CAT_OFF[pc], MAIN_DIM)

    dyn = _mm(carry["dya"], p["w_ssd_out"], "nt", tag + "dyn", dep=dep)
    datt = _mm(carry["dyb"], p["w_att_out"], "nt", tag + "datt", out_dtype=BF16, dep=dep)
    dproj, dkv, acc = _att_bwd(a["q"], a["kv"], p["sinks8"], a["att32"], datt, tag + "att", win("q"))
    gs["att_sinks"] = acc[0, :ATT_HEADS]
    dy, dproj, acc = _gnorm_bwd(a["y"], a["z"], p["norm_w"], dyn, tag + "gnorm", win("z"))
    gs["ssd_norm_w"] = acc[0]
    dxs, dbm, dcm, dproj, acc = _ssd_bwd(a["xsc"], a["bc"], a["cc"], a["dt"], p["ssd_par"], a["hs"], dy,
                                         tag + "ssd", win("dt"))
    gs["dt_bias"], gs["a_log"], gs["d_skip"] = (_from_group_major(acc[i]) for i in range(3))
    dconv_w, dconv_b = [], []
    for s, dout in (("xs", dxs), ("b", dbm), ("c", dcm)):
        dc, acc = _conv_bwd_pre(a[s], p["conv_w8"][s], p["conv_b8"][s], dout, tag + "conv_pre_" + s)
        dconv_w.append(acc[:CONV_TAPS])
        dconv_b.append(acc[CONV_TAPS])
        dproj = _conv_bwd_in(dc, p["conv_w8"][s], tag + "conv_in_" + s, win(s))
    gconv = jnp.concatenate(dconv_w, axis=1)
    gs["conv_b"] = jnp.concatenate(dconv_b)
    w_main, w_kv = p["w_cat"][:, :MAIN_DIM], p["w_cat"][:, MAIN_DIM:]
    dw_main, dw_kv = _dw(a["h0"], dproj, tag + "dw_in"), _dw(a["h0"], dkv, tag + "dw_in_kv")

    def grad_h0(dep=None):
        dh0 = _mm(dproj, w_main, "nt", tag + "dh0", add=du1, add_scale=ALPHA, dep=dep)
        return _mm(dkv, w_kv, "nt", tag + "dh0_kv", add=dh0)

    return grad_h0, _dw_in_chunks(dw_main, dw_kv), gconv, gs


def _step(x, target, w, m, v):
    x2 = x[0]
    t2 = target[0]
    tok = jnp.zeros(TOKEN.shape, TOKEN.dtype)

    ex = _exchange_start(_gather_items(w, IN_PART, 0), tok, "gather_l0_in_start")
    h = _ln_fwd(x2, None, w["ln_in_g"], w["ln_in_b"], 1.0, "ln_in")
    lands, tok = _exchange_wait(ex, h, "gather_l0_in_wait")
    p0 = _params_in(0, dict(zip(IN_PART, lands)), w)
    ex = _exchange_start(_gather_items(w, OUT_PART, 0) + _gather_items(w, IN_PART, 1), tok,
                         "gather_l0_out_l1_in_start")
    a0 = _fwd_mixers(h, p0, 0, dep=ex["token"])
    lands, tok = _exchange_wait(ex, a0["att"], "gather_l0_out_l1_in_wait")
    p0.update(_params_out(dict(zip(OUT_PART, lands))))
    p1 = _params_in(1, dict(zip(IN_PART, lands[len(OUT_PART):])), w)
    ex = _exchange_start(_gather_items(w, OUT_PART, 1), tok, "gather_l1_out_start")
    a0 = _fwd_out(a0, p0, 0, dep=ex["token"])
    lands, tok = _exchange_wait(ex, a0["h2"], "gather_l1_out_wait")
    p1.update(_params_out(dict(zip(OUT_PART, lands))))
    a1 = _fwd_out(_fwd_mixers(a0["h2"], p1, 1), p1, 1)

    sse, dh = _loss_fwd_bwd(a1["h2"], t2, "loss")
    loss = lax.psum(0.5 / D_MODEL * sse[0, 0], ("x", "y", "c"))

    carry, gw1, gs1 = _bwd_out(a1, p1, dh, 1)
    grad_h0, gw1["w_in"], gw1["conv_w"], gs = _bwd_mixers(a1, p1, carry, 1)
    dh = grad_h0()
    gs1.update(gs)
    ex1 = _exchange_start(_scatter_items(gw1, GATHERED), tok, "scatter_l1_start")
    carry, gw0, gs0 = _bwd_out(a0, p0, dh, 0, dep=ex1["token"])
    lands, tok = _exchange_wait(ex1, carry["dyb"], "scatter_l1_wait")
    land1 = dict(zip(GATHERED, lands))
    ex0 = _exchange_start(_scatter_items(gw0, OUT_PART), tok, "scatter_l0_out_start")
    grad_h0, gw0["w_in"], gw0["conv_w"], gs = _bwd_mixers(a0, p0, carry, 0, dep=ex0["token"])
    gs0.update(gs)
    lands, tok = _exchange_wait(ex0, gw0["w_in"], "scatter_l0_out_wait")
    land0 = dict(zip(OUT_PART, lands))
    ex0 = _exchange_start(_scatter_items(gw0, IN_PART), tok, "scatter_l0_in_start")
    dh = grad_h0(dep=ex0["token"])
    grad_x2, acc = _ln_bwd(x2, None, w["ln_in_g"], dh, 1.0, "ln_in_b")

    outs = [{} for _ in range(4)]

    def update(names):
        res = None
        for n in names:
            if n in TRANSPOSED:
                res = _adamw([land0[n], land1[n]], *(jnp.swapaxes(t, 1, 2) for t in (w[n], m[n], v[n])),
                             "adamw_" + n)
                res = tuple(jnp.swapaxes(t, 1, 2) for t in res)
            else:
                res = _adamw([land0[n], land1[n]], w[n], m[n], v[n], "adamw_" + n)
            for o, t in zip(outs, res):
                o[n] = t
        return res[1]

    update(OUT_PART)
    gsm = {"ln_in_g": acc[0], "ln_in_b": acc[1]}
    for n in SMALL[2:]:
        gsm[n] = jnp.stack([gs0[n], gs1[n]])
    small_shapes = {n: w[n].shape for n in SMALL}
    land_s = _all_gather_small(_pack_small(gsm), "small_grads_all_gather")
    res = _adamw([land_s], _pack_small(w)[None], _pack_small(m)[None], _pack_small(v)[None], "adamw_small")
    for o, t in zip(outs, res):
        o.update(_unpack_small(t[0], small_shapes))
    lands, _ = _exchange_wait(ex0, res[1], "scatter_l0_in_wait")
    land0.update(zip(IN_PART, lands))
    update(IN_PART)
    return loss, grad_x2[None], outs


WEIGHT_NAMES = ("ln_in_g", "ln_in_b", "w_in", "conv_w", "conv_b", "dt_bias", "a_log", "d_skip", "ssd_norm_w",
                "att_sinks", "w_ssd_out", "w_att_out", "w_mix_out", "ln_mix_g", "ln_mix_b", "w_ffn_gate",
                "w_ffn_up", "w_ffn_down", "ln_ffn_g", "ln_ffn_b")


def kernel(x, ln_in_g, ln_in_b, w_in, conv_w, conv_b, dt_bias, a_log, d_skip, ssd_norm_w, att_sinks, w_ssd_out, w_att_out, w_mix_out, ln_mix_g, ln_mix_b, w_ffn_gate, w_ffn_up, w_ffn_down, ln_ffn_g, ln_ffn_b, loss_target, m_ln_in_g, m_ln_in_b, m_w_in, m_conv_w, m_conv_b, m_dt_bias, m_a_log, m_d_skip, m_ssd_norm_w, m_att_sinks, m_w_ssd_out, m_w_att_out, m_w_mix_out, m_ln_mix_g, m_ln_mix_b, m_w_ffn_gate, m_w_ffn_up, m_w_ffn_down, m_ln_ffn_g, m_ln_ffn_b, v_ln_in_g, v_ln_in_b, v_w_in, v_conv_w, v_conv_b, v_dt_bias, v_a_log, v_d_skip, v_ssd_norm_w, v_att_sinks, v_w_ssd_out, v_w_att_out, v_w_mix_out, v_ln_mix_g, v_ln_mix_b, v_w_ffn_gate, v_w_ffn_up, v_w_ffn_down, v_ln_ffn_g, v_ln_ffn_b):
    w = dict(zip(WEIGHT_NAMES, (ln_in_g, ln_in_b, w_in, conv_w, conv_b, dt_bias, a_log, d_skip, ssd_norm_w,
                                att_sinks, w_ssd_out, w_att_out, w_mix_out, ln_mix_g, ln_mix_b, w_ffn_gate,
                                w_ffn_up, w_ffn_down, ln_ffn_g, ln_ffn_b)))
    m = dict(zip(WEIGHT_NAMES, (m_ln_in_g, m_ln_in_b, m_w_in, m_conv_w, m_conv_b, m_dt_bias, m_a_log, m_d_skip,
                                m_ssd_norm_w, m_att_sinks, m_w_ssd_out, m_w_att_out, m_w_mix_out, m_ln_mix_g,
                                m_ln_mix_b, m_w_ffn_gate, m_w_ffn_up, m_w_ffn_down, m_ln_ffn_g, m_ln_ffn_b)))
    v = dict(zip(WEIGHT_NAMES, (v_ln_in_g, v_ln_in_b, v_w_in, v_conv_w, v_conv_b, v_dt_bias, v_a_log, v_d_skip,
                                v_ssd_norm_w, v_att_sinks, v_w_ssd_out, v_w_att_out, v_w_mix_out, v_ln_mix_g,
                                v_ln_mix_b, v_w_ffn_gate, v_w_ffn_up, v_w_ffn_down, v_ln_ffn_g, v_ln_ffn_b)))
    loss, grad_x, outs = _step(x, loss_target, w, m, v)
    result = [loss, grad_x]
    for o in outs:
        result.extend(o[n] for n in WEIGHT_NAMES)
    return tuple(result)
```

```python
import math

import jax
import jax.numpy as jnp
from jax import lax
from jax.experimental import pallas as pl
from jax.experimental.pallas import tpu as pltpu

F32 = jnp.float32
BF16 = jnp.bfloat16

D_MODEL = 1024
DEPTH = 2
N_DEV = 8
ATT_HEADS = 16
ATT_KV_HEADS = 2
ATT_HEAD_DIM = 64
ATT_BLOCK = 128
SSD_D_INNER = 2048
SSD_HEADS = 32
SSD_GROUPS = 4
SSD_STATE = 128
SSD_CHUNK = 128
FFN_HIDDEN = 2816
LN_EPS = 1e-5
RMS_EPS = 1e-5
ALPHA = (2 * DEPTH) ** 0.25
Q_DIM = 1024
BC_DIM = 512
DT_PAD = 512

ADAM_LR = 0.001
ADAM_B1 = 0.9
ADAM_B2 = 0.999
ADAM_EPS = 1e-08
ADAM_WD = 0.01
ADAM_STEP = 10

LANE = 128
VMEM_LIMIT = 48 * 1024 * 1024
NEG = -1e30

_NN = (((1,), (0,)), ((), ()))
_NT = (((1,), (1,)), ((), ()))
_TN = (((0,), (0,)), ((), ()))
MESH_ID = pl.DeviceIdType.MESH


def _dot(a, b, dims=_NN):
    return lax.dot_general(a, b, dims, preferred_element_type=F32)


def _sig(x):
    return 1.0 / (1.0 + jnp.exp(-x))


def _softplus(x):
    return jnp.maximum(x, 0.0) + jnp.log(1.0 + jnp.exp(-jnp.abs(x)))


def _cparams(*sem):
    return pltpu.CompilerParams(dimension_semantics=sem, vmem_limit_bytes=VMEM_LIMIT)


def _pick(n, cap):
    if n <= cap:
        return n
    best = None
    for t in range(LANE, cap + 1, LANE):
        if n % t == 0:
            best = t
    assert best is not None, (n, cap)
    return best


def _tile(n):
    if n <= 1024 or n % 1024 == 0:
        return min(n, 1024)
    return _pick(n, 1408)


def _rows(n):
    return min(512, n)


def _window(x):
    return x if isinstance(x, tuple) else (x, 0, x.shape[1])


def _into(into, n_in, out_idx):
    buf, col0, width = into
    if buf is None:
        return [], [], {}, col0, width
    return [buf], [pl.BlockSpec(memory_space=pl.ANY)], {n_in: out_idx}, col0, width


def _mm(a, b, mode, name, add=None, add_scale=1.0, out_dtype=F32, dep=None):
    if mode == "nn":
        m, k = a.shape
        n = b.shape[1]
    elif mode == "nt":
        m, k = a.shape
        n = b.shape[0]
    else:
        k, m = a.shape
        n = b.shape[1]
    tm = _tile(m)
    tn = _pick(n, 2176) if mode == "tn" and n > 1024 else _tile(n)
    tk = _pick(k, 2176) if mode == "nt" and a.dtype == BF16 and k > 2816 else _tile(k)
    nk = k // tk
    has_add = add is not None
    dims = {"nn": _NN, "nt": _NT, "tn": _TN}[mode]

    def body(*refs):
        if dep is not None:
            refs = refs[:-3] + refs[-2:]
        if has_add:
            a_ref, b_ref, add_ref, o_ref, acc_ref = refs
        else:
            a_ref, b_ref, o_ref, acc_ref = refs
        kk = pl.program_id(2)

        @pl.when(kk == 0)
        def _():
            if has_add:
                acc_ref[...] = add_scale * add_ref[...].astype(F32)
            else:
                acc_ref[...] = jnp.zeros_like(acc_ref)

        acc_ref[...] += _dot(a_ref[...].astype(BF16), b_ref[...].astype(BF16), dims)

        @pl.when(kk == nk - 1)
        def _():
            o_ref[...] = acc_ref[...].astype(o_ref.dtype)

    if mode == "nn":
        a_spec = pl.BlockSpec((tm, tk), lambda i, j, kk: (i, kk))
        b_spec = pl.BlockSpec((tk, tn), lambda i, j, kk: (kk, j))
    elif mode == "nt":
        a_spec = pl.BlockSpec((tm, tk), lambda i, j, kk: (i, kk))
        b_spec = pl.BlockSpec((tn, tk), lambda i, j, kk: (j, kk))
    else:
        a_spec = pl.BlockSpec((tk, tm), lambda i, j, kk: (kk, i))
        b_spec = pl.BlockSpec((tk, tn), lambda i, j, kk: (kk, j))
    o_spec = pl.BlockSpec((tm, tn), lambda i, j, kk: (i, j))
    in_specs = [a_spec, b_spec] + ([o_spec] if has_add else [])
    args = (a, b) + ((add,) if has_add else ())
    if dep is not None:
        in_specs.append(pl.BlockSpec((8, LANE), lambda i, j, kk: (0, 0)))
        args += (dep,)
    return pl.pallas_call(
        body, name=name, grid=(m // tm, n // tn, nk),
        in_specs=in_specs, out_specs=o_spec,
        out_shape=jax.ShapeDtypeStruct((m, n), out_dtype),
        scratch_shapes=[pltpu.VMEM((tm, tn), F32)],
        compiler_params=_cparams("parallel", "parallel", "arbitrary"),
    )(*args)


def _vec_spec(width):
    return pl.BlockSpec((1, width), lambda i: (0, 0))


def _ln_fwd(a, b, gamma, beta, alpha, name):
    n_rows, dm = a.shape
    has_b = b is not None

    def body(*refs):
        if has_b:
            a_ref, b_ref, g_ref, be_ref, o_ref = refs
            u = alpha * a_ref[...] + b_ref[...]
        else:
            a_ref, g_ref, be_ref, o_ref = refs
            u = a_ref[...]
        mu = jnp.mean(u, axis=-1, keepdims=True)
        d = u - mu
        var = jnp.mean(d * d, axis=-1, keepdims=True)
        o_ref[...] = d * lax.rsqrt(var + LN_EPS) * g_ref[...] + be_ref[...]

    row = pl.BlockSpec((_rows(n_rows),dm), lambda i: (i, 0))
    in_specs = [row] + ([row] if has_b else []) + [_vec_spec(dm), _vec_spec(dm)]
    args = (a,) + ((b,) if has_b else ()) + (gamma.reshape(1, dm), beta.reshape(1, dm))
    return pl.pallas_call(
        body, name=name, grid=(n_rows // _rows(n_rows),), in_specs=in_specs, out_specs=row,
        out_shape=jax.ShapeDtypeStruct((n_rows, dm), F32),
        compiler_params=_cparams("parallel"),
    )(*args)


def _mm_ln(x, w, res, gamma, beta, alpha, name):
    n_rows, k = x.shape
    dm = w.shape[1]
    tm = _rows(n_rows)

    def body(x_ref, w_ref, r_ref, g_ref, be_ref, y_ref, o_ref):
        y = _dot(x_ref[...], w_ref[...])
        y_ref[...] = y
        u = alpha * r_ref[...] + y
        mu = jnp.mean(u, axis=-1, keepdims=True)
        d = u - mu
        var = jnp.mean(d * d, axis=-1, keepdims=True)
        o_ref[...] = d * lax.rsqrt(var + LN_EPS) * g_ref[...] + be_ref[...]

    row = pl.BlockSpec((tm, dm), lambda i: (i, 0))
    return pl.pallas_call(
        body, name=name, grid=(n_rows // tm,),
        in_specs=[pl.BlockSpec((tm, k), lambda i: (i, 0)), pl.BlockSpec((k, dm), lambda i: (0, 0)), row,
                  _vec_spec(dm), _vec_spec(dm)],
        out_specs=(row, row),
        out_shape=(jax.ShapeDtypeStruct((n_rows, dm), F32), jax.ShapeDtypeStruct((n_rows, dm), F32)),
        compiler_params=_cparams("parallel"),
    )(x, w, res, gamma.reshape(1, dm), beta.reshape(1, dm))


def _ln_bwd(a, b, gamma, dy, alpha, name):
    n_rows, dm = a.shape
    has_b = b is not None

    def body(*refs):
        if has_b:
            a_ref, b_ref, g_ref, dy_ref, du_ref, acc_ref = refs
            u = alpha * a_ref[...] + b_ref[...]
        else:
            a_ref, g_ref, dy_ref, du_ref, acc_ref = refs
            u = a_ref[...]

        @pl.when(pl.program_id(0) == 0)
        def _():
            acc_ref[...] = jnp.zeros_like(acc_ref)

        mu = jnp.mean(u, axis=-1, keepdims=True)
        d = u - mu
        var = jnp.mean(d * d, axis=-1, keepdims=True)
        rstd = lax.rsqrt(var + LN_EPS)
        xhat = d * rstd
        dyv = dy_ref[...]
        acc_ref[0:1, :] += jnp.sum(dyv * xhat, axis=0, keepdims=True)
        acc_ref[1:2, :] += jnp.sum(dyv, axis=0, keepdims=True)
        dxh = dyv * g_ref[...]
        m1 = jnp.mean(dxh, axis=-1, keepdims=True)
        m2 = jnp.mean(dxh * xhat, axis=-1, keepdims=True)
        du_ref[...] = rstd * (dxh - m1 - xhat * m2)

    row = pl.BlockSpec((_rows(n_rows),dm), lambda i: (i, 0))
    in_specs = [row] + ([row] if has_b else []) + [_vec_spec(dm), row]
    args = (a,) + ((b,) if has_b else ()) + (gamma.reshape(1, dm), dy)
    return pl.pallas_call(
        body, name=name, grid=(n_rows // _rows(n_rows),), in_specs=in_specs,
        out_specs=(row, pl.BlockSpec((8, dm), lambda i: (0, 0))),
        out_shape=(jax.ShapeDtypeStruct((n_rows, dm), F32), jax.ShapeDtypeStruct((8, dm), F32)),
        compiler_params=_cparams("arbitrary"),
    )(*args)


def _loss_fwd_bwd(y, target, name):
    n_rows, dm = y.shape

    def body(y_ref, t_ref, acc_ref, dy_ref):
        @pl.when(pl.program_id(0) == 0)
        def _():
            acc_ref[...] = jnp.zeros_like(acc_ref)

        d = y_ref[...] - t_ref[...]
        acc_ref[...] += jnp.sum(d * d)
        dy_ref[...] = d * (1.0 / dm)

    row = pl.BlockSpec((_rows(n_rows),dm), lambda i: (i, 0))
    return pl.pallas_call(
        body, name=name, grid=(n_rows // _rows(n_rows),), in_specs=[row, row],
        out_specs=(pl.BlockSpec((8, LANE), lambda i: (0, 0)), row),
        out_shape=(jax.ShapeDtypeStruct((8, LANE), F32), jax.ShapeDtypeStruct((n_rows, dm), F32)),
        compiler_params=_cparams("arbitrary"),
    )(y, target)


FFN_ROWS = 512


def _ffn_in(h, wg, wu, name, dep=None):
    m, k = h.shape
    n = wg.shape[0]
    tm, tn = min(FFN_ROWS, m), _tile(n)

    def body(*refs):
        h_ref, wg_ref, wu_ref = refs[:3]
        g_ref, u_ref, act_ref = refs[-3:]
        hb = h_ref[...].astype(BF16)
        g = _dot(hb, wg_ref[...], _NT)
        u = _dot(hb, wu_ref[...], _NT)
        g_ref[...] = g
        u_ref[...] = u
        act_ref[...] = (g * _sig(g) * u).astype(BF16)

    rows = pl.BlockSpec((tm, k), lambda j, i: (i, 0))
    wrow = pl.BlockSpec((tn, k), lambda j, i: (j, 0))
    out = pl.BlockSpec((tm, tn), lambda j, i: (i, j))
    in_specs, args = [rows, wrow, wrow], (h, wg, wu)
    if dep is not None:
        in_specs.append(pl.BlockSpec((8, LANE), lambda j, i: (0, 0)))
        args += (dep,)
    return pl.pallas_call(
        body, name=name, grid=(n // tn, m // tm), in_specs=in_specs, out_specs=(out, out, out),
        out_shape=(jax.ShapeDtypeStruct((m, n), F32), jax.ShapeDtypeStruct((m, n), F32),
                   jax.ShapeDtypeStruct((m, n), BF16)),
        compiler_params=_cparams("parallel", "parallel"),
    )(*args)


def _ffn_dact(dy, wd, g, u, name, dep=None):
    m, k = dy.shape
    n = wd.shape[0]
    tm, tn = min(FFN_ROWS, m), _tile(n)

    def body(*refs):
        dy_ref, wd_ref, g_ref, u_ref = refs[:4]
        dg_ref, du_ref = refs[-2:]
        da = _dot(dy_ref[...].astype(BF16), wd_ref[...], _NT)
        gv = g_ref[...]
        s = _sig(gv)
        dg_ref[...] = (da * u_ref[...] * (s * (1.0 + gv * (1.0 - s)))).astype(BF16)
        du_ref[...] = (da * gv * s).astype(BF16)

    rows = pl.BlockSpec((tm, k), lambda j, i: (i, 0))
    wrow = pl.BlockSpec((tn, k), lambda j, i: (j, 0))
    out = pl.BlockSpec((tm, tn), lambda j, i: (i, j))
    in_specs, args = [rows, wrow, out, out], (dy, wd, g, u)
    if dep is not None:
        in_specs.append(pl.BlockSpec((8, LANE), lambda j, i: (0, 0)))
        args += (dep,)
    return pl.pallas_call(
        body, name=name, grid=(n // tn, m // tm), in_specs=in_specs, out_specs=(out, out),
        out_shape=(jax.ShapeDtypeStruct((m, n), BF16), jax.ShapeDtypeStruct((m, n), BF16)),
        compiler_params=_cparams("parallel", "parallel"),
    )(*args)


def _gate_specs(gl, n_rows, dm):
    arr, g0, _ = _window(gl)
    return arr, [pl.BlockSpec((_rows(n_rows), dm), lambda i, k=k: (i, g0 // dm + k)) for k in range(2)]


def _branch_out(yn, att, w_ssd, w_att, gl, name, dep=None):
    n_rows, dm = yn.shape[0], w_ssd.shape[1]
    gl_arr, gspecs = _gate_specs(gl, n_rows, dm)

    def body(*refs):
        ga_ref, gb_ref, yn_ref, att_ref, ws_ref, wa_ref = refs[:6]
        ya_ref, yb_ref, o_ref = refs[-3:]
        ya = _dot(yn_ref[...], ws_ref[...])
        yb = _dot(att_ref[...], wa_ref[...])
        ya_ref[...] = ya
        yb_ref[...] = yb
        o_ref[...] = (_sig(ga_ref[...]) * ya + _sig(gb_ref[...]) * yb).astype(BF16)

    tm = _rows(n_rows)
    row = pl.BlockSpec((tm, dm), lambda i: (i, 0))
    in_specs = gspecs + [pl.BlockSpec((tm, yn.shape[1]), lambda i: (i, 0)), row,
                         pl.BlockSpec(w_ssd.shape, lambda i: (0, 0)), pl.BlockSpec(w_att.shape, lambda i: (0, 0))]
    args = (gl_arr, gl_arr, yn, att, w_ssd, w_att)
    if dep is not None:
        in_specs.append(pl.BlockSpec((8, LANE), lambda i: (0, 0)))
        args += (dep,)
    return pl.pallas_call(
        body, name=name, grid=(n_rows // tm,), in_specs=in_specs, out_specs=(row, row, row),
        out_shape=(jax.ShapeDtypeStruct((n_rows, dm), F32), jax.ShapeDtypeStruct((n_rows, dm), F32),
                   jax.ShapeDtypeStruct((n_rows, dm), BF16)),
        compiler_params=_cparams("parallel"),
    )(*args)


def _merge_bwd(gl, ya, yb, dmix, w_mix, name, into):
    n_rows, dm = ya.shape
    gl_arr, gspecs = _gate_specs(gl, n_rows, dm)
    extra, extra_specs, aliases, col0, width = _into(into, 6, 2)

    def body(*refs):
        ga_ref, gb_ref, ya_ref, yb_ref, dx_ref, w_ref = refs[:6]
        dya_ref, dyb_ref, dgl_ref = refs[-3:]
        ga = _sig(ga_ref[...])
        gb = _sig(gb_ref[...])
        dmv = _dot(dx_ref[...].astype(BF16), w_ref[...], _NT)
        dya_ref[...] = (dmv * ga).astype(BF16)
        dyb_ref[...] = (dmv * gb).astype(BF16)
        dgl_ref[:, :dm] = (dmv * ya_ref[...] * ga * (1.0 - ga)).astype(BF16)
        dgl_ref[:, dm:] = (dmv * yb_ref[...] * gb * (1.0 - gb)).astype(BF16)

    row = pl.BlockSpec((_rows(n_rows),dm), lambda i: (i, 0))
    row2 = pl.BlockSpec((_rows(n_rows),2 * dm), lambda i: (i, col0 // (2 * dm)))
    return pl.pallas_call(
        body, name=name, grid=(n_rows // _rows(n_rows),),
        in_specs=gspecs + [row, row, row, pl.BlockSpec(w_mix.shape, lambda i: (0, 0))] + extra_specs,
        out_specs=(row, row, row2),
        out_shape=(jax.ShapeDtypeStruct((n_rows, dm), BF16), jax.ShapeDtypeStruct((n_rows, dm), BF16),
                   jax.ShapeDtypeStruct((n_rows, width), BF16)),
        input_output_aliases=aliases,
        compiler_params=_cparams("parallel"),
    )(gl_arr, gl_arr, ya, yb, dmix, w_mix, *extra)


CONV_TAPS = 4
CONV_COLS = 512
HALO = 8


def _shift_down(cur, prev8, s, row8):
    r = pltpu.roll(cur, s, axis=0)
    top = jnp.where(row8 < s, pltpu.roll(prev8, s, axis=0), r[0:HALO])
    return jnp.concatenate([top, r[HALO:]], axis=0)


def _shift_up(cur, next8, s, row8):
    n = cur.shape[0]
    r = pltpu.roll(cur, n - s, axis=0)
    bot = jnp.where(row8 >= HALO - s, pltpu.roll(next8, HALO - s, axis=0), r[n - HALO:])
    return jnp.concatenate([r[:n - HALO], bot], axis=0)


def _conv_pre(u_ref, prev_ref, w_ref, b_ref, li):
    cur = u_ref[...]
    prev8 = jnp.where(li == 0, 0.0, prev_ref[...])
    row8 = lax.broadcasted_iota(jnp.int32, prev8.shape, 0)
    shifted = [cur] + [_shift_down(cur, prev8, s, row8) for s in range(1, CONV_TAPS)]
    acc = b_ref[...] + shifted[0] * w_ref[CONV_TAPS - 1:CONV_TAPS, :]
    for s in range(1, CONV_TAPS):
        acc = acc + shifted[s] * w_ref[CONV_TAPS - 1 - s:CONV_TAPS - s, :]
    return acc, shifted


def _conv_specs(n_rows, tl, col0=0):
    off = col0 // CONV_COLS
    cur = pl.BlockSpec((tl, CONV_COLS), lambda cj, li: (li, cj + off))
    prev = pl.BlockSpec((HALO, CONV_COLS), lambda cj, li: (jnp.maximum(li * (tl // HALO) - 1, 0), cj + off))
    nxt = pl.BlockSpec((HALO, CONV_COLS),
                       lambda cj, li: (jnp.minimum((li + 1) * (tl // HALO), n_rows // HALO - 1), cj + off))
    par = pl.BlockSpec((8, CONV_COLS), lambda cj, li: (0, cj + off))
    return cur, prev, nxt, par


def _conv_fwd(u, w8, b8, name):
    u, u0, c = _window(u)
    n_rows = u.shape[0]
    tl = _rows(n_rows)
    cur, _, _, par = _conv_specs(n_rows, tl)
    ucur, prev, _, _ = _conv_specs(n_rows, tl, u0)

    def body(u_ref, prev_ref, w_ref, b_ref, o_ref):
        acc, _ = _conv_pre(u_ref, prev_ref, w_ref, b_ref[0:1, :], pl.program_id(1))
        o_ref[...] = acc * _sig(acc)

    return pl.pallas_call(
        body, name=name, grid=(c // CONV_COLS, n_rows // tl), in_specs=[ucur, prev, par, par], out_specs=cur,
        out_shape=jax.ShapeDtypeStruct((n_rows, c), F32),
        compiler_params=_cparams("parallel", "parallel"),
    )(u, u, w8, b8)


def _conv_bwd_pre(u, w8, b8, dout, name):
    u, u0, c = _window(u)
    n_rows = u.shape[0]
    tl = _rows(n_rows)
    cur, _, _, par = _conv_specs(n_rows, tl)
    ucur, prev, _, _ = _conv_specs(n_rows, tl, u0)

    def body(u_ref, prev_ref, w_ref, b_ref, do_ref, dc_ref, acc_ref):
        @pl.when(pl.program_id(1) == 0)
        def _():
            acc_ref[...] = jnp.zeros_like(acc_ref)

        acc, shifted = _conv_pre(u_ref, prev_ref, w_ref, b_ref[0:1, :], pl.program_id(1))
        sg = _sig(acc)
        dc = do_ref[...] * (sg * (1.0 + acc * (1.0 - sg)))
        dc_ref[...] = dc
        for k in range(CONV_TAPS):
            acc_ref[k:k + 1, :] += jnp.sum(dc * shifted[CONV_TAPS - 1 - k], axis=0, keepdims=True)
        acc_ref[CONV_TAPS:CONV_TAPS + 1, :] += jnp.sum(dc, axis=0, keepdims=True)

    return pl.pallas_call(
        body, name=name, grid=(c // CONV_COLS, n_rows // tl), in_specs=[ucur, prev, par, par, cur],
        out_specs=(cur, par),
        out_shape=(jax.ShapeDtypeStruct((n_rows, c), F32), jax.ShapeDtypeStruct((8, c), F32)),
        compiler_params=_cparams("parallel", "arbitrary"),
    )(u, u, w8, b8, dout)


def _conv_bwd_in(dc, w8, name, into):
    n_rows, c = dc.shape
    tl = _rows(n_rows)
    cur, _, nxt, par = _conv_specs(n_rows, tl)
    n_l = n_rows // tl
    extra, extra_specs, aliases, col0, width = _into(into, 3, 0)
    out_spec = _conv_specs(n_rows, tl, col0)[0]

    def body(*refs):
        dc_ref, next_ref, w_ref = refs[:3]
        o_ref = refs[-1]
        cur_v = dc_ref[...]
        next8 = jnp.where(pl.program_id(1) == n_l - 1, 0.0, next_ref[...])
        row8 = lax.broadcasted_iota(jnp.int32, next8.shape, 0)
        acc = cur_v * w_ref[CONV_TAPS - 1:CONV_TAPS, :]
        for s in range(1, CONV_TAPS):
            acc = acc + _shift_up(cur_v, next8, s, row8) * w_ref[CONV_TAPS - 1 - s:CONV_TAPS - s, :]
        o_ref[...] = acc.astype(BF16)

    return pl.pallas_call(
        body, name=name, grid=(c // CONV_COLS, n_l), in_specs=[cur, nxt, par] + extra_specs, out_specs=out_spec,
        out_shape=jax.ShapeDtypeStruct((n_rows, width), BF16), input_output_aliases=aliases,
        compiler_params=_cparams("parallel", "parallel"),
    )(dc, dc, w8, *extra)


NORM_GROUP = SSD_D_INNER // SSD_GROUPS


def _gnorm_fwd(y, z, w, name):
    n_rows, c = y.shape
    z, z0, _ = _window(z)
    zoff = z0 // NORM_GROUP

    def body(y_ref, z_ref, w_ref, o_ref):
        zv = z_ref[...]
        yg = y_ref[...] * (zv * _sig(zv))
        r = lax.rsqrt(jnp.mean(yg * yg, axis=-1, keepdims=True) + RMS_EPS)
        o_ref[...] = (yg * r * w_ref[...]).astype(BF16)

    blk = pl.BlockSpec((_rows(n_rows),NORM_GROUP), lambda i, j: (i, j))
    zblk = pl.BlockSpec((_rows(n_rows),NORM_GROUP), lambda i, j: (i, j + zoff))
    wspec = pl.BlockSpec((1, NORM_GROUP), lambda i, j: (0, j))
    return pl.pallas_call(
        body, name=name, grid=(n_rows // _rows(n_rows), c // NORM_GROUP), in_specs=[blk, zblk, wspec], out_specs=blk,
        out_shape=jax.ShapeDtypeStruct((n_rows, c), BF16),
        compiler_params=_cparams("parallel", "parallel"),
    )(y, z, w.reshape(1, c))


def _gnorm_bwd(y, z, w, dyn, name, into):
    n_rows, c = y.shape
    z, z0, _ = _window(z)
    zoff = z0 // NORM_GROUP
    extra, extra_specs, aliases, col0, width = _into(into, 4, 1)
    doff = col0 // NORM_GROUP

    def body(*refs):
        y_ref, z_ref, w_ref, dn_ref = refs[:4]
        dy_ref, dz_ref, acc_ref = refs[-3:]
        @pl.when(pl.program_id(1) == 0)
        def _():
            acc_ref[...] = jnp.zeros_like(acc_ref)

        zv = z_ref[...]
        yv = y_ref[...]
        sz = _sig(zv)
        silu = zv * sz
        yg = yv * silu
        r = lax.rsqrt(jnp.mean(yg * yg, axis=-1, keepdims=True) + RMS_EPS)
        nrm = yg * r
        dn = dn_ref[...]
        acc_ref[0:1, :] += jnp.sum(dn * nrm, axis=0, keepdims=True)
        dnw = dn * w_ref[...]
        dyg = r * (dnw - nrm * jnp.mean(dnw * nrm, axis=-1, keepdims=True))
        dy_ref[...] = dyg * silu
        dz_ref[...] = (dyg * yv * (sz * (1.0 + zv * (1.0 - sz)))).astype(BF16)

    blk = pl.BlockSpec((_rows(n_rows),NORM_GROUP), lambda j, i: (i, j))
    zblk = pl.BlockSpec((_rows(n_rows),NORM_GROUP), lambda j, i: (i, j + zoff))
    wspec = pl.BlockSpec((1, NORM_GROUP), lambda j, i: (0, j))
    aspec = pl.BlockSpec((8, NORM_GROUP), lambda j, i: (0, j))
    return pl.pallas_call(
        body, name=name, grid=(c // NORM_GROUP, n_rows // _rows(n_rows)),
        in_specs=[blk, zblk, wspec, blk] + extra_specs,
        out_specs=(blk, pl.BlockSpec((_rows(n_rows), NORM_GROUP), lambda j, i: (i, j + doff)), aspec),
        out_shape=(jax.ShapeDtypeStruct((n_rows, c), F32), jax.ShapeDtypeStruct((n_rows, width), BF16),
                   jax.ShapeDtypeStruct((8, c), F32)),
        input_output_aliases=aliases,
        compiler_params=_cparams("parallel", "arbitrary"),
    )(y, z, w.reshape(1, c), dyn, *extra)


ATT_SCALE = ATT_HEAD_DIM ** -0.5
ATT_SLOPES = [2.0 ** (-8.0 * (h + 1) / ATT_HEADS) for h in range(ATT_HEADS)]
Q_PER_KV = ATT_HEADS // ATT_KV_HEADS


def _dup_half(t, g, lo):
    tr = pltpu.roll(t, ATT_HEAD_DIM, axis=1)
    return jnp.where(lo, t, tr) if g == 0 else jnp.where(lo, tr, t)


def _att_band(kv_ref, kvp_ref, n):
    cur = kv_ref[...]
    prev = jnp.where(n == 0, 0.0, kvp_ref[...])
    lo = lax.broadcasted_iota(jnp.int32, (ATT_BLOCK, LANE), 1) < ATT_HEAD_DIM
    bands = []
    for g in range(ATT_KV_HEADS):
        kb = jnp.concatenate([_dup_half(prev[:, :LANE], g, lo), _dup_half(cur[:, :LANE], g, lo)], axis=0)
        vb = jnp.concatenate([_dup_half(prev[:, LANE:], g, lo), _dup_half(cur[:, LANE:], g, lo)], axis=0)
        bands.append((kb.astype(BF16), vb.astype(BF16)))
    return bands


def _att_tile(n):
    shape = (2 * ATT_BLOCK, ATT_BLOCK)
    row = lax.broadcasted_iota(jnp.int32, shape, 0)
    i = row & (ATT_BLOCK - 1)
    s = lax.broadcasted_iota(jnp.int32, shape, 1)
    upper = s > i
    dist = ((i - s) & (ATT_BLOCK - 1)).astype(F32)
    dead = upper & (n == 0)
    return upper, dist, dead, row[:, 0:1] < ATT_BLOCK


def _stack_pair(t, lo):
    return jnp.concatenate([jnp.where(lo, t, 0.0), jnp.where(lo, 0.0, t)], axis=0).astype(BF16)


def _att_exp(qs, kb, s_ref, j, tile):
    upper, dist, dead, first = tile
    s2 = _dot(qs, kb, _NT)
    slope = jnp.where(first, ATT_SLOPES[2 * j], ATT_SLOPES[2 * j + 1])
    sink = jnp.where(first, s_ref[0:1, 2 * j:2 * j + 1], s_ref[0:1, 2 * j + 1:2 * j + 2])
    s = jnp.where(upper, s2[:, :ATT_BLOCK], s2[:, ATT_BLOCK:]) - slope * dist
    s = jnp.where(dead, NEG, s)
    m = jnp.maximum(jnp.max(s, axis=-1, keepdims=True), sink)
    return jnp.exp(s - m), jnp.exp(sink - m)


def _band_split(t, upper):
    return jnp.concatenate([jnp.where(upper, t, 0.0), jnp.where(upper, 0.0, t)], axis=1)


def _att_fwd(q, kv, sinks8, name):
    q, q0, _ = _window(q)
    kv, kv0, _ = _window(kv)
    qoff, kvoff = q0 // Q_DIM, kv0 // (2 * LANE)
    n_rows = q.shape[0]
    nb = n_rows // ATT_BLOCK

    def body(q_ref, kv_ref, kvp_ref, s_ref, o_ref, o32_ref):
        n = pl.program_id(0)
        bands = _att_band(kv_ref, kvp_ref, n)
        lo = lax.broadcasted_iota(jnp.int32, (ATT_BLOCK, LANE), 1) < ATT_HEAD_DIM
        tile = _att_tile(n)
        ones_b = jnp.ones((2 * ATT_BLOCK, LANE), BF16)
        for j in range(ATT_HEADS // 2):
            kb, vb = bands[2 * j // Q_PER_KV]
            qs = _stack_pair(q_ref[:, j * LANE:(j + 1) * LANE] * ATT_SCALE, lo)
            p, es = _att_exp(qs, kb, s_ref, j, tile)
            pv = _dot(_band_split(p, tile[0]).astype(BF16), jnp.concatenate([vb, ones_b], axis=1))
            out = pv[:, :LANE] / (pv[:, LANE:] + es)
            out = jnp.where(lo, out[:ATT_BLOCK], out[ATT_BLOCK:])
            o_ref[:, j * LANE:(j + 1) * LANE] = out.astype(BF16)
            o32_ref[:, j * LANE:(j + 1) * LANE] = out

    return pl.pallas_call(
        body, name=name, grid=(nb,),
        in_specs=[pl.BlockSpec((ATT_BLOCK, Q_DIM), lambda n: (n, qoff)),
                  pl.BlockSpec((ATT_BLOCK, 2 * LANE), lambda n: (n, kvoff)),
                  pl.BlockSpec((ATT_BLOCK, 2 * LANE), lambda n: (jnp.maximum(n - 1, 0), kvoff)),
                  pl.BlockSpec((8, LANE), lambda n: (0, 0))],
        out_specs=(pl.BlockSpec((ATT_BLOCK, Q_DIM), lambda n: (n, 0)),) * 2,
        out_shape=(jax.ShapeDtypeStruct((n_rows, Q_DIM), BF16), jax.ShapeDtypeStruct((n_rows, Q_DIM), F32)),
        compiler_params=_cparams("parallel"),
    )(q, kv, kv, sinks8)


def _att_bwd(q, kv, sinks8, out32, dout, name, into):
    q, q0, _ = _window(q)
    kv, kv0, _ = _window(kv)
    qoff, kvoff = q0 // Q_DIM, kv0 // (2 * LANE)
    n_rows = q.shape[0]
    nb = n_rows // ATT_BLOCK

    extra, extra_specs, aliases, col0, width = _into(into, 6, 0)
    dqoff = col0 // Q_DIM

    def body(*refs):
        q_ref, kv_ref, kvp_ref, s_ref, o_ref, do_ref = refs[:6]
        dq_ref, dkv_ref, acc_ref, carry_ref = refs[-4:]
        n = pl.program_id(0)

        @pl.when(n == 0)
        def _():
            acc_ref[...] = jnp.zeros_like(acc_ref)
            carry_ref[...] = jnp.zeros_like(carry_ref)

        @pl.when(n == nb)
        def _():
            dkv_ref[...] = carry_ref[...].astype(BF16)

        @pl.when(n < nb)
        def _():
            bands = _att_band(kv_ref, kvp_ref, n)
            lo = lax.broadcasted_iota(jnp.int32, (ATT_BLOCK, LANE), 1) < ATT_HEAD_DIM
            lane1 = lax.broadcasted_iota(jnp.int32, (1, LANE), 1)
            tile = _att_tile(n)
            upper, first = tile[0], tile[3]
            ones_b = jnp.ones((ATT_BLOCK, LANE), BF16)
            ones2_b = jnp.ones((2 * LANE, LANE), BF16)
            dk_acc = [jnp.zeros((2 * ATT_BLOCK, LANE), F32) for _ in range(ATT_KV_HEADS)]
            dv_acc = [jnp.zeros((2 * ATT_BLOCK, LANE), F32) for _ in range(ATT_KV_HEADS)]
            dsink = jnp.zeros((1, LANE), F32)
            for j in range(ATT_HEADS // 2):
                g = 2 * j // Q_PER_KV
                kb, vb = bands[g]
                qs = _stack_pair(q_ref[:, j * LANE:(j + 1) * LANE] * ATT_SCALE, lo)
                dop = do_ref[:, j * LANE:(j + 1) * LANE].astype(F32)
                dos = _stack_pair(dop, lo)
                pu, es = _att_exp(qs, kb, s_ref, j, tile)
                inv = 1.0 / (_dot(pu.astype(BF16), ones_b) + es)
                p = pu * inv
                od = dos.astype(F32) * jnp.concatenate([o_ref[:, j * LANE:(j + 1) * LANE]] * 2, axis=0)
                od_hi = od.astype(BF16)
                delta = _dot(jnp.concatenate([od_hi, (od - od_hi.astype(F32)).astype(BF16)], axis=1), ones2_b)
                dp2 = _dot(dos, vb, _NT)
                dp = jnp.where(upper, dp2[:, :ATT_BLOCK], dp2[:, ATT_BLOCK:])
                ds2 = _band_split(p * (dp - delta), upper)
                psd = jnp.sum(es * inv * delta, axis=0, keepdims=True)
                psd0 = jnp.sum(jnp.where(first, es * inv * delta, 0.0), axis=0, keepdims=True)
                dsink = jnp.where(lane1 == 2 * j, -psd0, jnp.where(lane1 == 2 * j + 1, psd0 - psd, dsink))
                ds2_b = ds2.astype(BF16)
                dq = _dot(ds2_b, kb) * ATT_SCALE
                dq_ref[:, j * LANE:(j + 1) * LANE] = jnp.where(lo, dq[:ATT_BLOCK], dq[ATT_BLOCK:]).astype(BF16)
                dk_acc[g] = dk_acc[g] + _dot(ds2_b, qs, _TN)
                dv_acc[g] = dv_acc[g] + _dot(_band_split(p, upper).astype(BF16), dos, _TN)
            acc_ref[0:1, :] += dsink
            lo2 = lax.broadcasted_iota(jnp.int32, (2 * ATT_BLOCK, LANE), 1) < ATT_HEAD_DIM
            folded = []
            for acc in (dk_acc, dv_acc):
                t0 = acc[0] + pltpu.roll(acc[0], ATT_HEAD_DIM, axis=1)
                t1 = acc[1] + pltpu.roll(acc[1], ATT_HEAD_DIM, axis=1)
                folded.append(jnp.where(lo2, t0, t1))
            band = jnp.concatenate(folded, axis=1)
            dkv_ref[...] = (carry_ref[...] + band[:ATT_BLOCK]).astype(BF16)
            carry_ref[...] = band[ATT_BLOCK:]

    def qmap(n):
        return (jnp.minimum(n, nb - 1), 0)

    return pl.pallas_call(
        body, name=name, grid=(nb + 1,),
        in_specs=[pl.BlockSpec((ATT_BLOCK, Q_DIM), lambda n: (jnp.minimum(n, nb - 1), qoff)),
                  pl.BlockSpec((ATT_BLOCK, 2 * LANE), lambda n: (jnp.minimum(n, nb - 1), kvoff)),
                  pl.BlockSpec((ATT_BLOCK, 2 * LANE),
                               lambda n: (jnp.maximum(jnp.minimum(n, nb - 1) - 1, 0), kvoff)),
                  pl.BlockSpec((8, LANE), lambda n: (0, 0)),
                  pl.BlockSpec((ATT_BLOCK, Q_DIM), qmap),
                  pl.BlockSpec((ATT_BLOCK, Q_DIM), qmap)] + extra_specs,
        out_specs=(pl.BlockSpec((ATT_BLOCK, Q_DIM), lambda n: (jnp.minimum(n, nb - 1), dqoff)),
                   pl.BlockSpec((ATT_BLOCK, 2 * LANE), lambda n: (jnp.maximum(n - 1, 0), 0)),
                   pl.BlockSpec((8, LANE), lambda n: (0, 0))),
        out_shape=(jax.ShapeDtypeStruct((n_rows, width), BF16), jax.ShapeDtypeStruct((n_rows, 2 * LANE), BF16),
                   jax.ShapeDtypeStruct((8, LANE), F32)),
        input_output_aliases=aliases,
        scratch_shapes=[pltpu.VMEM((ATT_BLOCK, 2 * LANE), F32)],
        compiler_params=_cparams("arbitrary"),
    )(q, kv, kv, sinks8, out32, dout, *extra)


HEADS_PER_GROUP = SSD_HEADS // SSD_GROUPS
PAIRS_PER_GROUP = HEADS_PER_GROUP // 2
T = SSD_CHUNK


def _cumsum_mm(mat, x):
    hi = x.astype(BF16)
    r = x - hi.astype(F32)
    mid = r.astype(BF16)
    lo = (r - mid.astype(F32)).astype(BF16)
    w = x.shape[1]
    out = _dot(mat, jnp.concatenate([hi, mid, lo], axis=1))
    return out[:, :w] + out[:, w:2 * w] + out[:, 2 * w:]


def _ssd_prep(dtr_ref, par_ref):
    dt = _softplus(dtr_ref[...] + par_ref[0:1, :])
    a = -jnp.exp(par_ref[1:2, :])
    ri = lax.broadcasted_iota(jnp.int32, (T, T), 0)
    ci = lax.broadcasted_iota(jnp.int32, (T, T), 1)
    cs = _cumsum_mm((ri >= ci).astype(BF16), dt * a)
    lo = lax.broadcasted_iota(jnp.int32, (T, LANE), 1) < SSD_CHUNK // 2

    def expand(arr):
        rows = arr.shape[0]
        return jnp.concatenate([jnp.where(lo[:rows], arr[:, 2 * j:2 * j + 1], arr[:, 2 * j + 1:2 * j + 2])
                                for j in range(PAIRS_PER_GROUP)], axis=1)

    tot = cs[T - 1:T, :]
    return {"dt": dt, "a": a, "cs": cs, "cst": cs.T, "lo": lo, "ri": ri, "ci": ci, "expand": expand,
            "dt_x": expand(dt), "ecs_x": expand(jnp.exp(cs)), "dec_x": expand(jnp.exp(tot - cs)),
            "et_x": expand(jnp.exp(tot)), "etot": jnp.exp(tot), "dsk_x": expand(par_ref[2:3, :])}


def _wide_masks():
    r = lax.broadcasted_iota(jnp.int32, (T, 2 * T), 0)
    l = lax.broadcasted_iota(jnp.int32, (T, 2 * T), 1)
    s = l & (T - 1)
    return r >= s, s >= r, l < T


def _wide_cs(q, k0, even):
    cs, cst = q["cs"], q["cst"]
    col = jnp.where(even, cs[:, k0:k0 + 1], cs[:, k0 + 1:k0 + 2])
    row = jnp.concatenate([cst[k0:k0 + 1, :], cst[k0 + 1:k0 + 2, :]], axis=1)
    return col, row


def _ssd_fwd(xs, bm, cm, dtr, par, name):
    dtr, dt0, _ = _window(dtr)
    dtoff = dt0 // LANE
    n_rows = xs.shape[0]
    nc = n_rows // T
    gw = PAIRS_PER_GROUP * LANE

    def body(x_ref, b_ref, c_ref, dtr_ref, par_ref, y_ref, hs_ref, h_ref):
        @pl.when(pl.program_id(1) == 0)
        def _():
            h_ref[...] = jnp.zeros_like(h_ref)

        q = _ssd_prep(dtr_ref, par_ref)
        lo = q["lo"]
        tri_w, _, even = _wide_masks()
        bg_b = b_ref[...].astype(BF16)
        cg_b = c_ref[...].astype(BF16)
        xv = x_ref[...]
        xdt = xv * q["dt_x"]
        h = h_ref[...]
        hs_ref[0, 0] = h
        yo = q["ecs_x"] * _dot(cg_b, h.astype(BF16))
        h_ref[...] = h * q["et_x"] + _dot(bg_b, (xdt * q["dec_x"]).astype(BF16), _TN)
        cb = _dot(cg_b, bg_b, _NT)
        cb_w = jnp.concatenate([cb, cb], axis=1)
        for j in range(PAIRS_PER_GROUP):
            col, row = _wide_cs(q, 2 * j, even)
            m_w = (jnp.exp(jnp.where(tri_w, col - row, NEG)) * cb_w).astype(BF16)
            sl = slice(j * LANE, (j + 1) * LANE)
            y_ref[:, sl] = (_dot(m_w, _stack_pair(xdt[:, sl], lo)) + yo[:, sl] + q["dsk_x"][:, sl] * xv[:, sl])

    return pl.pallas_call(
        body, name=name, grid=(SSD_GROUPS, nc),
        in_specs=[pl.BlockSpec((T, gw), lambda g, c: (c, g)),
                  pl.BlockSpec((T, SSD_STATE), lambda g, c: (c, g)),
                  pl.BlockSpec((T, SSD_STATE), lambda g, c: (c, g)),
                  pl.BlockSpec((T, LANE), lambda g, c: (c, g + dtoff)),
                  pl.BlockSpec((8, LANE), lambda g, c: (0, g))],
        out_specs=(pl.BlockSpec((T, gw), lambda g, c: (c, g)),
                   pl.BlockSpec((1, 1, SSD_STATE, gw), lambda g, c: (g, c, 0, 0))),
        out_shape=(jax.ShapeDtypeStruct((n_rows, SSD_D_INNER), F32),
                   jax.ShapeDtypeStruct((SSD_GROUPS, nc, SSD_STATE, gw), F32)),
        scratch_shapes=[pltpu.VMEM((SSD_STATE, gw), F32)],
        compiler_params=_cparams("parallel", "arbitrary"),
    )(xs, bm, cm, dtr, par)


def _ssd_bwd(xs, bm, cm, dtr, par, hs, dy, name, into):
    dtr, dt0, _ = _window(dtr)
    dtoff = dt0 // LANE
    n_rows = xs.shape[0]
    nc = n_rows // T
    gw = PAIRS_PER_GROUP * LANE
    extra, extra_specs, aliases, col0, width = _into(into, 7, 3)
    ddoff = col0 // LANE

    def body(*refs):
        x_ref, b_ref, c_ref, dtr_ref, par_ref, hs_ref, dy_ref = refs[:7]
        dx_ref, db_ref, dc_ref, ddtr_ref, acc_ref, dh_ref = refs[-6:]

        @pl.when(pl.program_id(1) == 0)
        def _():
            dh_ref[...] = jnp.zeros_like(dh_ref)
            acc_ref[...] = jnp.zeros_like(acc_ref)

        q = _ssd_prep(dtr_ref, par_ref)
        lo, dt, a = q["lo"], q["dt"], q["a"]
        tri_w, trit_w, even = _wide_masks()
        lane = lax.broadcasted_iota(jnp.int32, (T, LANE), 1)
        lane1 = lane[0:1, :]
        last_row = lax.broadcasted_iota(jnp.int32, (T, 1), 0) == T - 1
        bg_b = b_ref[...].astype(BF16)
        cg_b = c_ref[...].astype(BF16)
        xv = x_ref[...]
        dyv = dy_ref[...]
        xdt = xv * q["dt_x"]
        h = hs_ref[0, 0]
        dhn = dh_ref[...]
        h_b, dhn_b = h.astype(BF16), dhn.astype(BF16)
        yo = q["ecs_x"] * _dot(cg_b, h_b)
        bdh = q["dec_x"] * _dot(bg_b, dhn_b)
        dye = (dyv * q["ecs_x"]).astype(BF16)
        xd = (xdt * q["dec_x"]).astype(BF16)
        dcg = _dot(dye, h_b, _NT)
        dbg = _dot(xd, dhn_b, _NT)
        dh_ref[...] = dhn * q["et_x"] + _dot(cg_b, dye, _TN)
        e4_all = xdt * bdh
        f_all = dyv * yo - e4_all
        tot_row = jnp.sum(e4_all, axis=0, keepdims=True) + q["et_x"] * jnp.sum(h * dhn, axis=0, keepdims=True)
        dsk_row = jnp.sum(dyv * xv, axis=0, keepdims=True)
        cb = _dot(cg_b, bg_b, _NT)
        cbt = _dot(bg_b, cg_b, _NT)
        cb_w = jnp.concatenate([cb, cb], axis=1)
        cbt_w = jnp.concatenate([cbt, cbt], axis=1)
        dcb = jnp.zeros((T, T), F32)
        dcbt = jnp.zeros((T, T), F32)
        dcs_acc = jnp.zeros((T, LANE), F32)
        ddt_acc = jnp.zeros((T, LANE), F32)
        dsk_acc = jnp.zeros((1, LANE), F32)
        tot_acc = jnp.zeros((1, LANE), F32)
        ind_r = lax.broadcasted_iota(jnp.int32, (2 * T, LANE), 0)
        ind_l = lax.broadcasted_iota(jnp.int32, (2 * T, LANE), 1)

        def halves(t):
            return (jnp.sum(jnp.where(lo[0:1], t, 0.0), axis=-1, keepdims=True),
                    jnp.sum(jnp.where(lo[0:1], 0.0, t), axis=-1, keepdims=True))

        def split2(t):
            hi = t.astype(BF16)
            return jnp.concatenate([hi, (t - hi.astype(F32)).astype(BF16)], axis=1)

        for j in range(PAIRS_PER_GROUP):
            k0, k1 = 2 * j, 2 * j + 1
            sl = slice(j * LANE, (j + 1) * LANE)
            col, row = _wide_cs(q, k0, even)
            lm_w = jnp.exp(jnp.where(tri_w, col - row, NEG))
            lmt_w = jnp.exp(jnp.where(trit_w, row - col, NEG))
            dyp, xp = dyv[:, sl], xdt[:, sl]
            dym, xm = _stack_pair(dyp, lo), _stack_pair(xp, lo)
            dm_w = _dot(dyp.astype(BF16), xm, _NT)
            dmt_w = _dot(xp.astype(BF16), dym, _NT)
            mm_w = lm_w * cb_w
            mmt_w = lmt_w * cbt_w
            dxdt = _dot(mmt_w.astype(BF16), dym) + bdh[:, sl]
            g1 = dm_w * lm_w
            g2 = dmt_w * lmt_w
            dcb = dcb + g1[:, :T] + g1[:, T:]
            dcbt = dcbt + g2[:, :T] + g2[:, T:]
            ind_w = jnp.where(ind_l == jnp.where(ind_r < T, k0, k1), 1.0, 0.0).astype(BF16)
            ind_p = jnp.where(ind_l[:T] == jnp.where(ind_r[:T] < SSD_CHUNK // 2, k0, k1), 1.0, 0.0).astype(BF16)
            dcs_acc = dcs_acc + _dot(
                jnp.concatenate([split2(dm_w * mm_w - dmt_w * mmt_w), split2(f_all[:, sl])], axis=1),
                jnp.concatenate([ind_w, ind_w, ind_p, ind_p], axis=0))
            ddt_acc = ddt_acc + _dot(split2(dxdt * xv[:, sl]), jnp.concatenate([ind_p, ind_p], axis=0))
            tot2 = halves(tot_row[:, sl])
            tot_acc = jnp.where(lane1 == k0, tot2[0], jnp.where(lane1 == k1, tot2[1], tot_acc))
            dsk2 = halves(dsk_row[:, sl])
            dsk_acc = jnp.where(lane1 == k0, dsk2[0], jnp.where(lane1 == k1, dsk2[1], dsk_acc))
            dx_ref[:, sl] = dxdt * q["dt_x"][:, sl] + q["dsk_x"][:, sl] * dyp
        dcs_acc = dcs_acc + jnp.where(last_row, tot_acc, 0.0)
        dc_ref[...] = dcg + _dot(dcb.astype(BF16), bg_b)
        db_ref[...] = dbg + _dot(dcbt.astype(BF16), cg_b)
        dda = _cumsum_mm((q["ci"] >= q["ri"]).astype(BF16), dcs_acc)
        ddt = ddt_acc + dda * a
        ddtr = ddt * _sig(dtr_ref[...] + par_ref[0:1, :])
        ddtr_ref[...] = ddtr.astype(BF16)
        acc_ref[0:1, :] += jnp.sum(ddtr, axis=0, keepdims=True)
        acc_ref[1:2, :] += jnp.sum(dda * dt, axis=0, keepdims=True) * a
        acc_ref[2:3, :] += dsk_acc

    def rev(g, c):
        return (nc - 1 - c, g)

    return pl.pallas_call(
        body, name=name, grid=(SSD_GROUPS, nc),
        in_specs=[pl.BlockSpec((T, gw), rev),
                  pl.BlockSpec((T, SSD_STATE), rev),
                  pl.BlockSpec((T, SSD_STATE), rev),
                  pl.BlockSpec((T, LANE), lambda g, c: (nc - 1 - c, g + dtoff)),
                  pl.BlockSpec((8, LANE), lambda g, c: (0, g)),
                  pl.BlockSpec((1, 1, SSD_STATE, gw), lambda g, c: (g, nc - 1 - c, 0, 0)),
                  pl.BlockSpec((T, gw), rev)] + extra_specs,
        out_specs=(pl.BlockSpec((T, gw), rev),
                   pl.BlockSpec((T, SSD_STATE), rev),
                   pl.BlockSpec((T, SSD_STATE), rev),
                   pl.BlockSpec((T, LANE), lambda g, c: (nc - 1 - c, g + ddoff)),
                   pl.BlockSpec((8, LANE), lambda g, c: (0, g))),
        out_shape=(jax.ShapeDtypeStruct((n_rows, SSD_D_INNER), F32),
                   jax.ShapeDtypeStruct((n_rows, BC_DIM), F32),
                   jax.ShapeDtypeStruct((n_rows, BC_DIM), F32),
                   jax.ShapeDtypeStruct((n_rows, width), BF16),
                   jax.ShapeDtypeStruct((8, DT_PAD), F32)),
        input_output_aliases=aliases,
        scratch_shapes=[pltpu.VMEM((SSD_STATE, gw), F32)],
        compiler_params=_cparams("parallel", "arbitrary"),
    )(xs, bm, cm, dtr, par, hs, dy, *extra)


ADAM_ROWS = 256


def _adamw(lands, w, m, v, name):
    na = len(lands)
    n_slots, r, wd = lands[0].shape
    tr = r if r <= 2 * ADAM_ROWS else ADAM_ROWS
    nj = r // tr
    bc1 = 1.0 - ADAM_B1 ** ADAM_STEP
    bc2 = 1.0 - ADAM_B2 ** ADAM_STEP

    def body(*refs):
        l_refs = refs[:na]
        w_ref, m_ref, v_ref, g_ref, d_ref, nm_ref, nv_ref = refs[na:]
        for a in range(na):
            @pl.when(pl.program_id(0) == a)
            def _(l_ref=l_refs[a]):
                g = l_ref[0].astype(F32)
                for s in range(1, n_slots):
                    g = g + l_ref[s].astype(F32)
                mn = ADAM_B1 * m_ref[0] + (1.0 - ADAM_B1) * g
                vn = ADAM_B2 * v_ref[0] + (1.0 - ADAM_B2) * (g * g)
                mh = mn / bc1
                vh = vn / bc2
                g_ref[0] = g
                nm_ref[0] = mn
                nv_ref[0] = vn
                d_ref[0] = -ADAM_LR * (mh / (jnp.sqrt(vh) + ADAM_EPS) + ADAM_WD * w_ref[0])

    def land_spec(a):
        return pl.BlockSpec((n_slots, tr, wd),
                            lambda i, j: (0, jnp.where(i == a, j, jnp.where(i < a, 0, nj - 1)), 0))

    blk = pl.BlockSpec((1, tr, wd), lambda i, j: (i, j, 0))
    shp = jax.ShapeDtypeStruct((na, r, wd), F32)
    return pl.pallas_call(
        body, name=name, grid=(na, nj), in_specs=[land_spec(a) for a in range(na)] + [blk, blk, blk],
        out_specs=(blk, blk, blk, blk), out_shape=(shp, shp, shp, shp),
        compiler_params=_cparams("arbitrary", "arbitrary"),
    )(*lands, w, m, v)


def _mesh_pos():
    return lax.axis_index("x"), lax.axis_index("y"), lax.axis_index("c")


def _peer(pos, k):
    x, y, c = pos
    px = 1 - x if (k >> 2) & 1 else x
    py = 1 - y if (k >> 1) & 1 else y
    pc = 1 - c if k & 1 else c
    return px, py, pc


def _flat(pos):
    return 4 * pos[0] + 2 * pos[1] + pos[2]


HBM_SPEC = pl.BlockSpec(memory_space=pl.ANY)


ROW_SHARDED = ("w_ssd_out", "w_att_out", "w_mix_out", "w_ffn_down")
COL_SHARDED = ("w_in", "w_ffn_gate", "w_ffn_up")
GATHERED = ROW_SHARDED + COL_SHARDED + ("conv_w",)


SEM_SPEC = pl.BlockSpec(memory_space=pltpu.SEMAPHORE)
TOKEN = jax.ShapeDtypeStruct((8, LANE), F32)
SPLIT_EFFECT = pltpu.SideEffectType.DATAFLOW_SIDE_EFFECTING
GATHER_ROWS = "gather_rows"
GATHER_SLOT = "gather_slot"
SCATTER_ROWS = "scatter_rows"
SCATTER_SLOT = "scatter_slot"


def _land_shape(kind, src):
    if kind == GATHER_ROWS:
        return (N_DEV * src.shape[0],) + src.shape[1:]
    if kind == GATHER_SLOT:
        return (N_DEV,) + src.shape
    if kind == SCATTER_ROWS:
        return (N_DEV, src.shape[0] // N_DEV) + src.shape[1:]
    return src.shape


def _views(kind, src_ref, land_ref, pos, k):
    me = _flat(pos)
    if kind == GATHER_ROWS:
        r = src_ref.shape[0]
        return src_ref, land_ref.at[pl.ds(pl.multiple_of(me * r, 16), r), :]
    if kind == GATHER_SLOT:
        return src_ref, land_ref.at[me]
    dev = _flat(_peer(pos, k))
    if kind == SCATTER_ROWS:
        r = land_ref.shape[1]
        return src_ref.at[pl.ds(pl.multiple_of(dev * r, 16), r), :], land_ref.at[k]
    return src_ref.at[dev], land_ref.at[k]


def _hbm(x):
    return pltpu.with_memory_space_constraint(x, pltpu.HBM)


def _exchange_start(items, after, name):
    kinds = [k for k, _ in items]
    srcs = [_hbm(s) for _, s in items]
    lands = [_hbm(lax.empty(_land_shape(k, s), s.dtype)) for k, s in items]
    n = len(items)
    n_copy = n * (N_DEV - 1)

    def body(*refs):
        src_refs, land_refs = refs[:n], refs[n:2 * n]
        send_sems, recv_sems = refs[2 * n + 1], refs[2 * n + 2]
        token_ref = refs[4 * n + 3]
        pos = _mesh_pos()
        for i, kind in enumerate(kinds):
            for k in range(1, N_DEV):
                s, d = _views(kind, src_refs[i], land_refs[i], pos, k)
                j = i * (N_DEV - 1) + k - 1
                pltpu.make_async_remote_copy(src_ref=s, dst_ref=d, send_sem=send_sems.at[j], recv_sem=recv_sems.at[j],
                                             device_id=_peer(pos, k), device_id_type=MESH_ID).start()
        token_ref[...] = jnp.zeros_like(token_ref)

    arrs = srcs + lands
    outs = pl.pallas_call(
        body, name=name,
        in_specs=[HBM_SPEC] * (2 * n + 1),
        out_specs=[SEM_SPEC, SEM_SPEC] + [HBM_SPEC] * (2 * n) + [pl.BlockSpec(memory_space=pltpu.VMEM)],
        out_shape=[pltpu.SemaphoreType.DMA((n_copy,)), pltpu.SemaphoreType.DMA((n_copy,))]
        + [pltpu.HBM(a.shape, a.dtype) for a in arrs] + [TOKEN],
        input_output_aliases={i: 2 + i for i in range(2 * n)},
        compiler_params=pltpu.CompilerParams(has_side_effects=SPLIT_EFFECT),
    )(*arrs, after)
    return {"kinds": kinds, "send": outs[0], "recv": outs[1], "arrs": outs[2:2 + 2 * n], "token": outs[-1]}


def _exchange_wait(ex, after, name):
    kinds = ex["kinds"]
    n = len(kinds)

    def body(*refs):
        src_refs, land_refs = refs[:n], refs[n:2 * n]
        send_sems, recv_sems = refs[2 * n], refs[2 * n + 1]
        token_ref = refs[-1]
        pos = _mesh_pos()
        for i, kind in enumerate(kinds):
            for k in range(1, N_DEV):
                s, d = _views(kind, src_refs[i], land_refs[i], pos, k)
                j = i * (N_DEV - 1) + k - 1
                cp = pltpu.make_async_remote_copy(src_ref=s, dst_ref=d, send_sem=send_sems.at[j],
                                                  recv_sem=recv_sems.at[j], device_id=_peer(pos, k),
                                                  device_id_type=MESH_ID)
                cp.wait_send()
                cp.wait_recv()
        token_ref[...] = jnp.zeros_like(token_ref)

    outs = pl.pallas_call(
        body, name=name,
        in_specs=[HBM_SPEC] * (2 * n) + [SEM_SPEC, SEM_SPEC, HBM_SPEC],
        out_specs=[HBM_SPEC] * (2 * n) + [pl.BlockSpec(memory_space=pltpu.VMEM)],
        out_shape=[pltpu.HBM(a.shape, a.dtype) for a in ex["arrs"]] + [TOKEN],
        input_output_aliases={i: i for i in range(2 * n)},
        compiler_params=pltpu.CompilerParams(has_side_effects=SPLIT_EFFECT),
    )(*ex["arrs"], ex["send"], ex["recv"], after)
    lands = [_place_own(k, s, d) for k, s, d in zip(kinds, outs[:n], outs[n:2 * n])]
    return lands, outs[-1]


def _place_own(kind, src, land):
    me = _flat(_mesh_pos())
    zeros = (0,) * (src.ndim - 1)
    if kind == GATHER_ROWS:
        return lax.dynamic_update_slice(land, src, (me * src.shape[0],) + zeros)
    if kind == GATHER_SLOT:
        return lax.dynamic_update_slice(land, src[None], (me,) + (0,) * src.ndim)
    if kind == SCATTER_ROWS:
        r = land.shape[1]
        own = lax.dynamic_slice(src, (me * r,) + zeros, (r,) + src.shape[1:])
    else:
        own = lax.dynamic_index_in_dim(src, me, 0, keepdims=False)
    return lax.dynamic_update_slice(land, own[None], (0,) * land.ndim)


def _all_gather_small(x, name):
    r, w = x.shape

    def body(x_ref, out_ref, send_sems, recv_sems):
        pos = _mesh_pos()
        me = _flat(pos)
        copies = []
        for k in range(1, N_DEV):
            cp = pltpu.make_async_remote_copy(
                src_ref=x_ref, dst_ref=out_ref.at[me], send_sem=send_sems.at[k - 1], recv_sem=recv_sems.at[k - 1],
                device_id=_peer(pos, k), device_id_type=MESH_ID)
            cp.start()
            copies.append(cp)
        out_ref[me] = x_ref[...]
        for cp in copies:
            cp.wait()

    vmem = pl.BlockSpec(memory_space=pltpu.VMEM)
    return pl.pallas_call(
        body, name=name, in_specs=[vmem], out_specs=vmem,
        out_shape=jax.ShapeDtypeStruct((N_DEV, r, w), x.dtype),
        scratch_shapes=[pltpu.SemaphoreType.DMA((N_DEV - 1,)), pltpu.SemaphoreType.DMA((N_DEV - 1,))],
        compiler_params=pltpu.CompilerParams(has_side_effects=True),
    )(x)


def _cols(g, lo, hi):
    c = g.shape[-1]
    parts = []
    for d in range(N_DEV):
        a, b = max(lo, d * c), min(hi, (d + 1) * c)
        if a < b:
            parts.append(g[d, :, a - d * c:b - d * c])
    return parts[0] if len(parts) == 1 else jnp.concatenate(parts, axis=1)


def _col_chunks(g):
    c = g.shape[-1] // N_DEV
    return jnp.stack([g[:, d * c:(d + 1) * c] for d in range(N_DEV)])


IN_PART = ("w_in", "conv_w")
OUT_PART = ROW_SHARDED + ("w_ffn_gate", "w_ffn_up")
TRANSPOSED = ("w_ffn_gate", "w_ffn_up")


def _gather_items(w, names, l):
    items = []
    for n in names:
        blk = w[n][l] if n == "conv_w" else w[n][l].astype(BF16)
        if n in TRANSPOSED:
            blk = blk.T
        items.append((GATHER_ROWS if n in ROW_SHARDED + TRANSPOSED else GATHER_SLOT, blk))
    return items


def _scatter_items(grads, names):
    def chunked(g):
        return g if g.ndim == 3 else _col_chunks(g)

    return [(SCATTER_ROWS, grads[n]) if n in ROW_SHARDED + TRANSPOSED else (SCATTER_SLOT, chunked(grads[n]))
            for n in names]


SMALL = ("ln_in_g", "ln_in_b", "conv_b", "dt_bias", "a_log", "d_skip", "ssd_norm_w", "att_sinks",
         "ln_mix_g", "ln_mix_b", "ln_ffn_g", "ln_ffn_b")


def _pack_small(vals):
    flat = jnp.concatenate([vals[n].reshape(-1) for n in SMALL])
    n = flat.shape[0]
    rows = -(-n // LANE)
    rows = -(-rows // 8) * 8
    return jnp.pad(flat, (0, rows * LANE - n)).reshape(rows, LANE)


def _unpack_small(buf, shapes):
    flat = buf.reshape(-1)
    off = 0
    out = {}
    for n in SMALL:
        cnt = math.prod(shapes[n])
        out[n] = flat[off:off + cnt].reshape(shapes[n])
        off += cnt
    return out


def _to_group_major(v):
    lead = v.shape[:-1]
    t = v.reshape(lead + (SSD_GROUPS, HEADS_PER_GROUP))
    t = jnp.pad(t, [(0, 0)] * len(lead) + [(0, 0), (0, LANE - HEADS_PER_GROUP)])
    return t.reshape(lead + (DT_PAD,))


def _from_group_major(v):
    lead = v.shape[:-1]
    return v.reshape(lead + (SSD_GROUPS, LANE))[..., :HEADS_PER_GROUP].reshape(lead + (SSD_HEADS,))


def _rows8(v):
    return jnp.pad(v, ((0, 8 - v.shape[0]), (0, 0)))


IN_OFFS = {"q": (0, 1024), "kv": (1024, 1280), "z": (1280, 3328), "xs": (3328, 5376), "b": (5376, 5888),
           "c": (5888, 6400), "dt": (6400, 6432), "gl": (6432, 8480)}
PIECES = ("q", "kv", "z", "xs", "b", "c", "dt", "gl")


CAT = ("z", "xs", "gl", "q", "b", "c", "dt", "kv")
CAT_WIDTH = {"q": 1024, "z": 2048, "xs": 2048, "gl": 2048, "b": 512, "c": 512, "kv": 256, "dt": DT_PAD}
CAT_OFF = {p: sum(CAT_WIDTH[q] for q in CAT[:i]) for i, p in enumerate(CAT)}
CAT_DIM = sum(CAT_WIDTH.values())
MAIN_DIM = CAT_OFF["kv"]


def _cat_w_in(g):
    pieces = {p: _cols(g, lo, hi) for p, (lo, hi) in IN_OFFS.items()}
    pieces["dt"] = _to_group_major(pieces["dt"])
    return jnp.concatenate([pieces[p] for p in CAT], axis=1)


def _dw_in_chunks(dw_main, dw_kv):
    dt = _from_group_major(dw_main[:, CAT_OFF["dt"]:CAT_OFF["dt"] + DT_PAD])
    shard = IN_OFFS[PIECES[-1]][1] // N_DEV

    def piece(pc, a, b):
        if pc == "dt":
            return dt[:, a:b]
        if pc == "kv":
            return dw_kv[:, a:b]
        return dw_main[:, CAT_OFF[pc] + a:CAT_OFF[pc] + b]

    chunks = []
    for d in range(N_DEV):
        parts = []
        for pc in PIECES:
            lo, hi = IN_OFFS[pc]
            a, b = max(lo, d * shard), min(hi, (d + 1) * shard)
            if a < b:
                parts.append(piece(pc, a - lo, b - lo))
        chunks.append(parts[0] if len(parts) == 1 else jnp.concatenate(parts, axis=1))
    return jnp.stack(chunks)


def _params_out(W):
    return {n: W[n] for n in OUT_PART}


def _params_in(l, W, sm):
    p = {"w_cat": _cat_w_in(W["w_in"])}
    cw = _cols(W["conv_w"], 0, SSD_D_INNER + 2 * BC_DIM)
    cb = sm["conv_b"][l]
    segs = {"xs": (0, 2048), "b": (2048, 2560), "c": (2560, 3072)}
    p["conv_w8"] = {s: _rows8(cw[:, lo:hi]) for s, (lo, hi) in segs.items()}
    p["conv_b8"] = {s: _rows8(cb[None, lo:hi]) for s, (lo, hi) in segs.items()}
    p["ssd_par"] = _rows8(jnp.stack([_to_group_major(sm["dt_bias"][l]), _to_group_major(sm["a_log"][l]),
                                     _to_group_major(sm["d_skip"][l])]))
    p["norm_w"] = sm["ssd_norm_w"][l]
    p["sinks8"] = _rows8(jnp.pad(sm["att_sinks"][l], (0, LANE - ATT_HEADS))[None])
    for n in ("ln_mix_g", "ln_mix_b", "ln_ffn_g", "ln_ffn_b"):
        p[n] = sm[n][l]
    return p


def _fwd_mixers(h0, p, l, dep=None):
    tag = f"l{l}_"
    a = {"h0": h0}
    proj = _mm(h0, p["w_cat"], "nn", tag + "proj", dep=dep)
    for pc in CAT:
        a[pc] = (proj, CAT_OFF[pc], CAT_WIDTH[pc])
    for s in ("xs", "b", "c"):
        a[s + "c"] = _conv_fwd(a[s], p["conv_w8"][s], p["conv_b8"][s], tag + "conv_" + s)
    a["y"], a["hs"] = _ssd_fwd(a["xsc"], a["bc"], a["cc"], a["dt"], p["ssd_par"], tag + "ssd_fwd")
    a["yn"] = _gnorm_fwd(a["y"], a["z"], p["norm_w"], tag + "gnorm")
    a["att"], a["att32"] = _att_fwd(a["q"], a["kv"], p["sinks8"], tag + "att_fwd")
    return a


def _fwd_out(a, p, l, dep=None):
    tag = f"l{l}_"
    h0 = a["h0"]
    a["ya"], a["yb"], a["merged"] = _branch_out(a["yn"], a["att"], p["w_ssd_out"], p["w_att_out"], a["gl"],
                                                tag + "branch_out", dep=dep)
    a["mix"], a["h1"] = _mm_ln(a["merged"], p["w_mix_out"], h0, p["ln_mix_g"], p["ln_mix_b"], ALPHA,
                               tag + "mix_out_ln")
    a["fg"], a["fu"], a["act"] = _ffn_in(a["h1"], p["w_ffn_gate"], p["w_ffn_up"], tag + "ffn_in")
    a["ffn"], a["h2"] = _mm_ln(a["act"], p["w_ffn_down"], a["h1"], p["ln_ffn_g"], p["ln_ffn_b"], ALPHA,
                               tag + "ffn_down_ln")
    return a


def _dw(x, dy, name, dep=None):
    return _mm(x, dy, "tn", name, out_dtype=BF16, dep=dep)


def _bwd_out(a, p, dh2, l, dep=None):
    tag = f"l{l}_b_"
    gw, gs = {}, {}
    du2, acc = _ln_bwd(a["h1"], a["ffn"], p["ln_ffn_g"], dh2, ALPHA, tag + "ln_ffn")
    gs["ln_ffn_g"], gs["ln_ffn_b"] = acc[0], acc[1]
    gw["w_ffn_down"] = _dw(a["act"], du2, tag + "dw_down", dep=dep)
    dfg, dfu = _ffn_dact(du2, p["w_ffn_down"], a["fg"], a["fu"], tag + "ffn_dact", dep=dep)
    gw["w_ffn_gate"] = _dw(dfg, a["h1"], tag + "dw_gate")
    gw["w_ffn_up"] = _dw(dfu, a["h1"], tag + "dw_up")
    dh1 = _mm(dfg, p["w_ffn_gate"], "nn", tag + "dh1_gate", add=du2, add_scale=ALPHA)
    dh1 = _mm(dfu, p["w_ffn_up"], "nn", tag + "dh1_up", add=dh1)
    du1, acc = _ln_bwd(a["h0"], a["mix"], p["ln_mix_g"], dh1, ALPHA, tag + "ln_mix")
    gs["ln_mix_g"], gs["ln_mix_b"] = acc[0], acc[1]
    gw["w_mix_out"] = _dw(a["merged"], du1, tag + "dw_mix")
    dya, dyb, dproj = _merge_bwd(a["gl"], a["ya"], a["yb"], du1, p["w_mix_out"], tag + "merge",
                                 (None, CAT_OFF["gl"], MAIN_DIM))
    gw["w_ssd_out"] = _dw(a["yn"], dya, tag + "dw_ssd")
    gw["w_att_out"] = _dw(a["att"], dyb, tag + "dw_att")
    return {"du1": du1, "dya": dya, "dyb": dyb, "dproj": dproj}, gw, gs


def _bwd_mixers(a, p, carry, l, dep=None):
    tag = f"l{l}_b_"
    gs = {}
    du1, dproj = carry["du1"], carry["dproj"]

    def win(pc):
        return (dproj, CAT_OFF[pc], MAIN_DIM)

    dyn = _mm(carry["dya"], p["w_ssd_out"], "nt", tag + "dyn", dep=dep)
    datt = _mm(carry["dyb"], p["w_att_out"], "nt", tag + "datt", out_dtype=BF16, dep=dep)
    dproj, dkv, acc = _att_bwd(a["q"], a["kv"], p["sinks8"], a["att32"], datt, tag + "att", win("q"))
    gs["att_sinks"] = acc[0, :ATT_HEADS]
    dy, dproj, acc = _gnorm_bwd(a["y"], a["z"], p["norm_w"], dyn, tag + "gnorm", win("z"))
    gs["ssd_norm_w"] = acc[0]
    dxs, dbm, dcm, dproj, acc = _ssd_bwd(a["xsc"], a["bc"], a["cc"], a["dt"], p["ssd_par"], a["hs"], dy,
                                         tag + "ssd", win("dt"))
    gs["dt_bias"], gs["a_log"], gs["d_skip"] = (_from_group_major(acc[i]) for i in range(3))
    dconv_w, dconv_b = [], []
    for s, dout in (("xs", dxs), ("b", dbm), ("c", dcm)):
        dc, acc = _conv_bwd_pre(a[s], p["conv_w8"][s], p["conv_b8"][s], dout, tag + "conv_pre_" + s)
        dconv_w.append(acc[:CONV_TAPS])
        dconv_b.append(acc[CONV_TAPS])
        dproj = _conv_bwd_in(dc, p["conv_w8"][s], tag + "conv_in_" + s, win(s))
    gconv = jnp.concatenate(dconv_w, axis=1)
    gs["conv_b"] = jnp.concatenate(dconv_b)
    w_main, w_kv = p["w_cat"][:, :MAIN_DIM], p["w_cat"][:, MAIN_DIM:]
    dw_main, dw_kv = _dw(a["h0"], dproj, tag + "dw_in"), _dw(a["h0"], dkv, tag + "dw_in_kv")

    def grad_h0(dep=None):
        dh0 = _mm(dproj, w_main, "nt", tag + "dh0", add=du1, add_scale=ALPHA, dep=dep)
        return _mm(dkv, w_kv, "nt", tag + "dh0_kv", add=dh0)

    return grad_h0, _dw_in_chunks(dw_main, dw_kv), gconv, gs


def _step(x, target, w, m, v):
    x2 = x[0]
    t2 = target[0]
    tok = jnp.zeros(TOKEN.shape, TOKEN.dtype)

    ex = _exchange_start(_gather_items(w, IN_PART, 0), tok, "gather_l0_in_start")
    h = _ln_fwd(x2, None, w["ln_in_g"], w["ln_in_b"], 1.0, "ln_in")
    lands, tok = _exchange_wait(ex, h, "gather_l0_in_wait")
    p0 = _params_in(0, dict(zip(IN_PART, lands)), w)
    ex = _exchange_start(_gather_items(w, OUT_PART, 0) + _gather_items(w, IN_PART, 1), tok,
                         "gather_l0_out_l1_in_start")
    a0 = _fwd_mixers(h, p0, 0, dep=ex["token"])
    lands, tok = _exchange_wait(ex, a0["att"], "gather_l0_out_l1_in_wait")
    p0.update(_params_out(dict(zip(OUT_PART, lands))))
    p1 = _params_in(1, dict(zip(IN_PART, lands[len(OUT_PART):])), w)
    ex = _exchange_start(_gather_items(w, OUT_PART, 1), tok, "gather_l1_out_start")
    a0 = _fwd_out(a0, p0, 0, dep=ex["token"])
    lands, tok = _exchange_wait(ex, a0["h2"], "gather_l1_out_wait")
    p1.update(_params_out(dict(zip(OUT_PART, lands))))
    a1 = _fwd_out(_fwd_mixers(a0["h2"], p1, 1), p1, 1)

    sse, dh = _loss_fwd_bwd(a1["h2"], t2, "loss")
    loss = lax.psum(0.5 / D_MODEL * sse[0, 0], ("x", "y", "c"))

    carry, gw1, gs1 = _bwd_out(a1, p1, dh, 1)
    grad_h0, gw1["w_in"], gw1["conv_w"], gs = _bwd_mixers(a1, p1, carry, 1)
    dh = grad_h0()
    gs1.update(gs)
    ex1 = _exchange_start(_scatter_items(gw1, GATHERED), tok, "scatter_l1_start")
    carry, gw0, gs0 = _bwd_out(a0, p0, dh, 0, dep=ex1["token"])
    lands, tok = _exchange_wait(ex1, carry["dyb"], "scatter_l1_wait")
    land1 = dict(zip(GATHERED, lands))
    ex0 = _exchange_start(_scatter_items(gw0, OUT_PART), tok, "scatter_l0_out_start")
    grad_h0, gw0["w_in"], gw0["conv_w"], gs = _bwd_mixers(a0, p0, carry, 0, dep=ex0["token"])
    gs0.update(gs)
    lands, tok = _exchange_wait(ex0, gw0["w_in"], "scatter_l0_out_wait")
    land0 = dict(zip(OUT_PART, lands))
    ex0 = _exchange_start(_scatter_items(gw0, IN_PART), tok, "scatter_l0_in_start")
    dh = grad_h0(dep=ex0["token"])
    grad_x2, acc = _ln_bwd(x2, None, w["ln_in_g"], dh, 1.0, "ln_in_b")

    outs = [{} for _ in range(4)]

    def update(names):
        res = None
        for n in names:
            if n in TRANSPOSED:
                res = _adamw([land0[n], land1[n]], *(jnp.swapaxes(t, 1, 2) for t in (w[n], m[n], v[n])),
                             "adamw_" + n)
                res = tuple(jnp.swapaxes(t, 1, 2) for t in res)
            else:
                res = _adamw([land0[n], land1[n]], w[n], m[n], v[n], "adamw_" + n)
            for o, t in zip(outs, res):
                o[n] = t
        return res[1]

    update(OUT_PART)
    gsm = {"ln_in_g": acc[0], "ln_in_b": acc[1]}
    for n in SMALL[2:]:
        gsm[n] = jnp.stack([gs0[n], gs1[n]])
    small_shapes = {n: w[n].shape for n in SMALL}
    land_s = _all_gather_small(_pack_small(gsm), "small_grads_all_gather")
    res = _adamw([land_s], _pack_small(w)[None], _pack_small(m)[None], _pack_small(v)[None], "adamw_small")
    for o, t in zip(outs, res):
        o.update(_unpack_small(t[0], small_shapes))
    lands, _ = _exchange_wait(ex0, res[1], "scatter_l0_in_wait")
    land0.update(zip(IN_PART, lands))
    update(IN_PART)
    return loss, grad_x2[None], outs


WEIGHT_NAMES = ("ln_in_g", "ln_in_b", "w_in", "conv_w", "conv_b", "dt_bias", "a_log", "d_skip", "ssd_norm_w",
                "att_sinks", "w_ssd_out", "w_att_out", "w_mix_out", "ln_mix_g", "ln_mix_b", "w_ffn_gate",
                "w_ffn_up", "w_ffn_down", "ln_ffn_g", "ln_ffn_b")


def kernel(x, ln_in_g, ln_in_b, w_in, conv_w, conv_b, dt_bias, a_log, d_skip, ssd_norm_w, att_sinks, w_ssd_out, w_att_out, w_mix_out, ln_mix_g, ln_mix_b, w_ffn_gate, w_ffn_up, w_ffn_down, ln_ffn_g, ln_ffn_b, loss_target, m_ln_in_g, m_ln_in_b, m_w_in, m_conv_w, m_conv_b, m_dt_bias, m_a_log, m_d_skip, m_ssd_norm_w, m_att_sinks, m_w_ssd_out, m_w_att_out, m_w_mix_out, m_ln_mix_g, m_ln_mix_b, m_w_ffn_gate, m_w_ffn_up, m_w_ffn_down, m_ln_ffn_g, m_ln_ffn_b, v_ln_in_g, v_ln_in_b, v_w_in, v_conv_w, v_conv_b, v_dt_bias, v_a_log, v_d_skip, v_ssd_norm_w, v_att_sinks, v_w_ssd_out, v_w_att_out, v_w_mix_out, v_ln_mix_g, v_ln_mix_b, v_w_ffn_gate, v_w_ffn_up, v_w_ffn_down, v_ln_ffn_g, v_ln_ffn_b):
    w = dict(zip(WEIGHT_NAMES, (ln_in_g, ln_in_b, w_in, conv_w, conv_b, dt_bias, a_log, d_skip, ssd_norm_w,
                                att_sinks, w_ssd_out, w_att_out, w_mix_out, ln_mix_g, ln_mix_b, w_ffn_gate,
                                w_ffn_up, w_ffn_down, ln_ffn_g, ln_ffn_b)))
    m = dict(zip(WEIGHT_NAMES, (m_ln_in_g, m_ln_in_b, m_w_in, m_conv_w, m_conv_b, m_dt_bias, m_a_log, m_d_skip,
                                m_ssd_norm_w, m_att_sinks, m_w_ssd_out, m_w_att_out, m_w_mix_out, m_ln_mix_g,
                                m_ln_mix_b, m_w_ffn_gate, m_w_ffn_up, m_w_ffn_down, m_ln_ffn_g, m_ln_ffn_b)))
    v = dict(zip(WEIGHT_NAMES, (v_ln_in_g, v_ln_in_b, v_w_in, v_conv_w, v_conv_b, v_dt_bias, v_a_log, v_d_skip,
                                v_ssd_norm_w, v_att_sinks, v_w_ssd_out, v_w_att_out, v_w_mix_out, v_ln_mix_g,
                                v_ln_mix_b, v_w_ffn_gate, v_w_ffn_up, v_w_ffn_down, v_ln_ffn_g, v_ln_ffn_b)))
    loss, grad_x, outs = _step(x, loss_target, w, m, v)
    result = [loss, grad_x]
    for o in outs:
        result.extend(o[n] for n in WEIGHT_NAMES)
    return tuple(result)
```

```python
import math

import jax
import jax.numpy as jnp
from jax import lax
from jax.experimental import pallas as pl
from jax.experimental.pallas import tpu as pltpu

F32 = jnp.float32
BF16 = jnp.bfloat16

D_MODEL = 1024
DEPTH = 2
N_DEV = 8
ATT_HEADS = 16
ATT_KV_HEADS = 2
ATT_HEAD_DIM = 64
ATT_BLOCK = 128
SSD_D_INNER = 2048
SSD_HEADS = 32
SSD_GROUPS = 4
SSD_STATE = 128
SSD_CHUNK = 128
FFN_HIDDEN = 2816
LN_EPS = 1e-5
RMS_EPS = 1e-5
ALPHA = (2 * DEPTH) ** 0.25
Q_DIM = 1024
BC_DIM = 512
DT_PAD = 512

ADAM_LR = 0.001
ADAM_B1 = 0.9
ADAM_B2 = 0.999
ADAM_EPS = 1e-08
ADAM_WD = 0.01
ADAM_STEP = 10

LANE = 128
VMEM_LIMIT = 48 * 1024 * 1024
NEG = -1e30

_NN = (((1,), (0,)), ((), ()))
_NT = (((1,), (1,)), ((), ()))
_TN = (((0,), (0,)), ((), ()))
MESH_ID = pl.DeviceIdType.MESH


def _dot(a, b, dims=_NN):
    return lax.dot_general(a, b, dims, preferred_element_type=F32)


def _sig(x):
    return 1.0 / (1.0 + jnp.exp(-x))


def _softplus(x):
    return jnp.maximum(x, 0.0) + jnp.log(1.0 + jnp.exp(-jnp.abs(x)))


def _cparams(*sem):
    return pltpu.CompilerParams(dimension_semantics=sem, vmem_limit_bytes=VMEM_LIMIT)


def _pick(n, cap):
    if n <= cap:
        return n
    best = None
    for t in range(LANE, cap + 1, LANE):
        if n % t == 0:
            best = t
    assert best is not None, (n, cap)
    return best


def _tile(n):
    if n <= 1024 or n % 1024 == 0:
        return min(n, 1024)
    return _pick(n, 1408)


def _rows(n):
    return min(512, n)


def _window(x):
    return x if isinstance(x, tuple) else (x, 0, x.shape[1])


def _into(into, n_in, out_idx):
    buf, col0, width = into
    if buf is None:
        return [], [], {}, col0, width
    return [buf], [pl.BlockSpec(memory_space=pl.ANY)], {n_in: out_idx}, col0, width


def _mm(a, b, mode, name, add=None, add_scale=1.0, out_dtype=F32, dep=None):
    if mode == "nn":
        m, k = a.shape
        n = b.shape[1]
    elif mode == "nt":
        m, k = a.shape
        n = b.shape[0]
    else:
        k, m = a.shape
        n = b.shape[1]
    tm = _tile(m)
    tn = _pick(n, 2176) if mode == "tn" and n > 1024 else _tile(n)
    tk = _pick(k, 2176) if mode == "nt" and a.dtype == BF16 and k > 2816 else _tile(k)
    nk = k // tk
    has_add = add is not None
    dims = {"nn": _NN, "nt": _NT, "tn": _TN}[mode]

    def body(*refs):
        if dep is not None:
            refs = refs[:-3] + refs[-2:]
        if has_add:
            a_ref, b_ref, add_ref, o_ref, acc_ref = refs
        else:
            a_ref, b_ref, o_ref, acc_ref = refs
        kk = pl.program_id(2)

        @pl.when(kk == 0)
        def _():
            if has_add:
                acc_ref[...] = add_scale * add_ref[...].astype(F32)
            else:
                acc_ref[...] = jnp.zeros_like(acc_ref)

        acc_ref[...] += _dot(a_ref[...].astype(BF16), b_ref[...].astype(BF16), dims)

        @pl.when(kk == nk - 1)
        def _():
            o_ref[...] = acc_ref[...].astype(o_ref.dtype)

    if mode == "nn":
        a_spec = pl.BlockSpec((tm, tk), lambda i, j, kk: (i, kk))
        b_spec = pl.BlockSpec((tk, tn), lambda i, j, kk: (kk, j))
    elif mode == "nt":
        a_spec = pl.BlockSpec((tm, tk), lambda i, j, kk: (i, kk))
        b_spec = pl.BlockSpec((tn, tk), lambda i, j, kk: (j, kk))
    else:
        a_spec = pl.BlockSpec((tk, tm), lambda i, j, kk: (kk, i))
        b_spec = pl.BlockSpec((tk, tn), lambda i, j, kk: (kk, j))
    o_spec = pl.BlockSpec((tm, tn), lambda i, j, kk: (i, j))
    in_specs = [a_spec, b_spec] + ([o_spec] if has_add else [])
    args = (a, b) + ((add,) if has_add else ())
    if dep is not None:
        in_specs.append(pl.BlockSpec((8, LANE), lambda i, j, kk: (0, 0)))
        args += (dep,)
    return pl.pallas_call(
        body, name=name, grid=(m // tm, n // tn, nk),
        in_specs=in_specs, out_specs=o_spec,
        out_shape=jax.ShapeDtypeStruct((m, n), out_dtype),
        scratch_shapes=[pltpu.VMEM((tm, tn), F32)],
        compiler_params=_cparams("parallel", "parallel", "arbitrary"),
    )(*args)


def _vec_spec(width):
    return pl.BlockSpec((1, width), lambda i: (0, 0))


def _ln_fwd(a, b, gamma, beta, alpha, name):
    n_rows, dm = a.shape
    has_b = b is not None

    def body(*refs):
        if has_b:
            a_ref, b_ref, g_ref, be_ref, o_ref = refs
            u = alpha * a_ref[...] + b_ref[...]
        else:
            a_ref, g_ref, be_ref, o_ref = refs
            u = a_ref[...]
        mu = jnp.mean(u, axis=-1, keepdims=True)
        d = u - mu
        var = jnp.mean(d * d, axis=-1, keepdims=True)
        o_ref[...] = d * lax.rsqrt(var + LN_EPS) * g_ref[...] + be_ref[...]

    row = pl.BlockSpec((_rows(n_rows),dm), lambda i: (i, 0))
    in_specs = [row] + ([row] if has_b else []) + [_vec_spec(dm), _vec_spec(dm)]
    args = (a,) + ((b,) if has_b else ()) + (gamma.reshape(1, dm), beta.reshape(1, dm))
    return pl.pallas_call(
        body, name=name, grid=(n_rows // _rows(n_rows),), in_specs=in_specs, out_specs=row,
        out_shape=jax.ShapeDtypeStruct((n_rows, dm), F32),
        compiler_params=_cparams("parallel"),
    )(*args)


def _mm_ln(x, w, res, gamma, beta, alpha, name):
    n_rows, k = x.shape
    dm = w.shape[1]
    tm = _rows(n_rows)

    def body(x_ref, w_ref, r_ref, g_ref, be_ref, y_ref, o_ref):
        y = _dot(x_ref[...], w_ref[...])
        y_ref[...] = y
        u = alpha * r_ref[...] + y
        mu = jnp.mean(u, axis=-1, keepdims=True)
        d = u - mu
        var = jnp.mean(d * d, axis=-1, keepdims=True)
        o_ref[...] = d * lax.rsqrt(var + LN_EPS) * g_ref[...] + be_ref[...]

    row = pl.BlockSpec((tm, dm), lambda i: (i, 0))
    return pl.pallas_call(
        body, name=name, grid=(n_rows // tm,),
        in_specs=[pl.BlockSpec((tm, k), lambda i: (i, 0)), pl.BlockSpec((k, dm), lambda i: (0, 0)), row,
                  _vec_spec(dm), _vec_spec(dm)],
        out_specs=(row, row),
        out_shape=(jax.ShapeDtypeStruct((n_rows, dm), F32), jax.ShapeDtypeStruct((n_rows, dm), F32)),
        compiler_params=_cparams("parallel"),
    )(x, w, res, gamma.reshape(1, dm), beta.reshape(1, dm))


def _ln_bwd(a, b, gamma, dy, alpha, name):
    n_rows, dm = a.shape
    has_b = b is not None

    def body(*refs):
        if has_b:
            a_ref, b_ref, g_ref, dy_ref, du_ref, acc_ref = refs
            u = alpha * a_ref[...] + b_ref[...]
        else:
            a_ref, g_ref, dy_ref, du_ref, acc_ref = refs
            u = a_ref[...]

        @pl.when(pl.program_id(0) == 0)
        def _():
            acc_ref[...] = jnp.zeros_like(acc_ref)

        mu = jnp.mean(u, axis=-1, keepdims=True)
        d = u - mu
        var = jnp.mean(d * d, axis=-1, keepdims=True)
        rstd = lax.rsqrt(var + LN_EPS)
        xhat = d * rstd
        dyv = dy_ref[...]
        acc_ref[0:1, :] += jnp.sum(dyv * xhat, axis=0, keepdims=True)
        acc_ref[1:2, :] += jnp.sum(dyv, axis=0, keepdims=True)
        dxh = dyv * g_ref[...]
        m1 = jnp.mean(dxh, axis=-1, keepdims=True)
        m2 = jnp.mean(dxh * xhat, axis=-1, keepdims=True)
        du_ref[...] = rstd * (dxh - m1 - xhat * m2)

    row = pl.BlockSpec((_rows(n_rows),dm), lambda i: (i, 0))
    in_specs = [row] + ([row] if has_b else []) + [_vec_spec(dm), row]
    args = (a,) + ((b,) if has_b else ()) + (gamma.reshape(1, dm), dy)
    return pl.pallas_call(
        body, name=name, grid=(n_rows // _rows(n_rows),), in_specs=in_specs,
        out_specs=(row, pl.BlockSpec((8, dm), lambda i: (0, 0))),
        out_shape=(jax.ShapeDtypeStruct((n_rows, dm), F32), jax.ShapeDtypeStruct((8, dm), F32)),
        compiler_params=_cparams("arbitrary"),
    )(*args)


def _loss_fwd_bwd(y, target, name):
    n_rows, dm = y.shape

    def body(y_ref, t_ref, acc_ref, dy_ref):
        @pl.when(pl.program_id(0) == 0)
        def _():
            acc_ref[...] = jnp.zeros_like(acc_ref)

        d = y_ref[...] - t_ref[...]
        acc_ref[...] += jnp.sum(d * d)
        dy_ref[...] = d * (1.0 / dm)

    row = pl.BlockSpec((_rows(n_rows),dm), lambda i: (i, 0))
    return pl.pallas_call(
        body, name=name, grid=(n_rows // _rows(n_rows),), in_specs=[row, row],
        out_specs=(pl.BlockSpec((8, LANE), lambda i: (0, 0)), row),
        out_shape=(jax.ShapeDtypeStruct((8, LANE), F32), jax.ShapeDtypeStruct((n_rows, dm), F32)),
        compiler_params=_cparams("arbitrary"),
    )(y, target)


FFN_ROWS = 512


def _ffn_in(h, wg, wu, name, dep=None):
    m, k = h.shape
    n = wg.shape[0]
    tm, tn = min(FFN_ROWS, m), _tile(n)

    def body(*refs):
        h_ref, wg_ref, wu_ref = refs[:3]
        g_ref, u_ref, act_ref = refs[-3:]
        hb = h_ref[...].astype(BF16)
        g = _dot(hb, wg_ref[...], _NT)
        u = _dot(hb, wu_ref[...], _NT)
        g_ref[...] = g
        u_ref[...] = u
        act_ref[...] = (g * _sig(g) * u).astype(BF16)

    rows = pl.BlockSpec((tm, k), lambda j, i: (i, 0))
    wrow = pl.BlockSpec((tn, k), lambda j, i: (j, 0))
    out = pl.BlockSpec((tm, tn), lambda j, i: (i, j))
    in_specs, args = [rows, wrow, wrow], (h, wg, wu)
    if dep is not None:
        in_specs.append(pl.BlockSpec((8, LANE), lambda j, i: (0, 0)))
        args += (dep,)
    return pl.pallas_call(
        body, name=name, grid=(n // tn, m // tm), in_specs=in_specs, out_specs=(out, out, out),
        out_shape=(jax.ShapeDtypeStruct((m, n), F32), jax.ShapeDtypeStruct((m, n), F32),
                   jax.ShapeDtypeStruct((m, n), BF16)),
        compiler_params=_cparams("parallel", "parallel"),
    )(*args)


def _ffn_dh(dg, du, wg, wu, res, alpha, name):
    m, f = dg.shape
    dm = wg.shape[1]
    tm = min(FFN_ROWS // 2, m)

    def body(dg_ref, du_ref, wg_ref, wu_ref, r_ref, o_ref):
        o_ref[...] = alpha * r_ref[...] + _dot(dg_ref[...], wg_ref[...]) + _dot(du_ref[...], wu_ref[...])

    rows_f = pl.BlockSpec((tm, f), lambda i: (i, 0))
    rows_d = pl.BlockSpec((tm, dm), lambda i: (i, 0))
    whole = pl.BlockSpec((f, dm), lambda i: (0, 0))
    return pl.pallas_call(
        body, name=name, grid=(m // tm,), in_specs=[rows_f, rows_f, whole, whole, rows_d], out_specs=rows_d,
        out_shape=jax.ShapeDtypeStruct((m, dm), F32),
        compiler_params=_cparams("parallel"),
    )(dg, du, wg, wu, res)


def _ffn_dact(dy, wd, g, u, name, dep=None):
    m, k = dy.shape
    n = wd.shape[0]
    tm, tn = min(FFN_ROWS, m), _tile(n)

    def body(*refs):
        dy_ref, wd_ref, g_ref, u_ref = refs[:4]
        dg_ref, du_ref = refs[-2:]
        da = _dot(dy_ref[...].astype(BF16), wd_ref[...], _NT)
        gv = g_ref[...]
        s = _sig(gv)
        dg_ref[...] = (da * u_ref[...] * (s * (1.0 + gv * (1.0 - s)))).astype(BF16)
        du_ref[...] = (da * gv * s).astype(BF16)

    rows = pl.BlockSpec((tm, k), lambda j, i: (i, 0))
    wrow = pl.BlockSpec((tn, k), lambda j, i: (j, 0))
    out = pl.BlockSpec((tm, tn), lambda j, i: (i, j))
    in_specs, args = [rows, wrow, out, out], (dy, wd, g, u)
    if dep is not None:
        in_specs.append(pl.BlockSpec((8, LANE), lambda j, i: (0, 0)))
        args += (dep,)
    return pl.pallas_call(
        body, name=name, grid=(n // tn, m // tm), in_specs=in_specs, out_specs=(out, out),
        out_shape=(jax.ShapeDtypeStruct((m, n), BF16), jax.ShapeDtypeStruct((m, n), BF16)),
        compiler_params=_cparams("parallel", "parallel"),
    )(*args)


def _gate_specs(gl, n_rows, dm):
    arr, g0, _ = _window(gl)
    return arr, [pl.BlockSpec((_rows(n_rows), dm), lambda i, k=k: (i, g0 // dm + k)) for k in range(2)]


def _branch_out(yn, att, w_ssd, w_att, gl, name, dep=None):
    n_rows, dm = yn.shape[0], w_ssd.shape[1]
    gl_arr, gspecs = _gate_specs(gl, n_rows, dm)

    def body(*refs):
        ga_ref, gb_ref, yn_ref, att_ref, ws_ref, wa_ref = refs[:6]
        ya_ref, yb_ref, o_ref = refs[-3:]
        ya = _dot(yn_ref[...], ws_ref[...])
        yb = _dot(att_ref[...], wa_ref[...])
        ya_ref[...] = ya
        yb_ref[...] = yb
        o_ref[...] = (_sig(ga_ref[...]) * ya + _sig(gb_ref[...]) * yb).astype(BF16)

    tm = _rows(n_rows)
    row = pl.BlockSpec((tm, dm), lambda i: (i, 0))
    in_specs = gspecs + [pl.BlockSpec((tm, yn.shape[1]), lambda i: (i, 0)), row,
                         pl.BlockSpec(w_ssd.shape, lambda i: (0, 0)), pl.BlockSpec(w_att.shape, lambda i: (0, 0))]
    args = (gl_arr, gl_arr, yn, att, w_ssd, w_att)
    if dep is not None:
        in_specs.append(pl.BlockSpec((8, LANE), lambda i: (0, 0)))
        args += (dep,)
    return pl.pallas_call(
        body, name=name, grid=(n_rows // tm,), in_specs=in_specs, out_specs=(row, row, row),
        out_shape=(jax.ShapeDtypeStruct((n_rows, dm), F32), jax.ShapeDtypeStruct((n_rows, dm), F32),
                   jax.ShapeDtypeStruct((n_rows, dm), BF16)),
        compiler_params=_cparams("parallel"),
    )(*args)


def _merge_bwd(gl, ya, yb, dmix, w_mix, name, into):
    n_rows, dm = ya.shape
    gl_arr, gspecs = _gate_specs(gl, n_rows, dm)
    extra, extra_specs, aliases, col0, width = _into(into, 6, 2)

    def body(*refs):
        ga_ref, gb_ref, ya_ref, yb_ref, dx_ref, w_ref = refs[:6]
        dya_ref, dyb_ref, dgl_ref = refs[-3:]
        ga = _sig(ga_ref[...])
        gb = _sig(gb_ref[...])
        dmv = _dot(dx_ref[...].astype(BF16), w_ref[...], _NT)
        dya_ref[...] = (dmv * ga).astype(BF16)
        dyb_ref[...] = (dmv * gb).astype(BF16)
        dgl_ref[:, :dm] = (dmv * ya_ref[...] * ga * (1.0 - ga)).astype(BF16)
        dgl_ref[:, dm:] = (dmv * yb_ref[...] * gb * (1.0 - gb)).astype(BF16)

    row = pl.BlockSpec((_rows(n_rows),dm), lambda i: (i, 0))
    row2 = pl.BlockSpec((_rows(n_rows),2 * dm), lambda i: (i, col0 // (2 * dm)))
    return pl.pallas_call(
        body, name=name, grid=(n_rows // _rows(n_rows),),
        in_specs=gspecs + [row, row, row, pl.BlockSpec(w_mix.shape, lambda i: (0, 0))] + extra_specs,
        out_specs=(row, row, row2),
        out_shape=(jax.ShapeDtypeStruct((n_rows, dm), BF16), jax.ShapeDtypeStruct((n_rows, dm), BF16),
                   jax.ShapeDtypeStruct((n_rows, width), BF16)),
        input_output_aliases=aliases,
        compiler_params=_cparams("parallel"),
    )(gl_arr, gl_arr, ya, yb, dmix, w_mix, *extra)


CONV_TAPS = 4
CONV_COLS = 512
HALO = 8


def _shift_down(cur, prev8, s, row8):
    r = pltpu.roll(cur, s, axis=0)
    top = jnp.where(row8 < s, pltpu.roll(prev8, s, axis=0), r[0:HALO])
    return jnp.concatenate([top, r[HALO:]], axis=0)


def _shift_up(cur, next8, s, row8):
    n = cur.shape[0]
    r = pltpu.roll(cur, n - s, axis=0)
    bot = jnp.where(row8 >= HALO - s, pltpu.roll(next8, HALO - s, axis=0), r[n - HALO:])
    return jnp.concatenate([r[:n - HALO], bot], axis=0)


def _conv_pre(u_ref, prev_ref, w_ref, b_ref, li):
    cur = u_ref[...]
    prev8 = jnp.where(li == 0, 0.0, prev_ref[...])
    row8 = lax.broadcasted_iota(jnp.int32, prev8.shape, 0)
    shifted = [cur] + [_shift_down(cur, prev8, s, row8) for s in range(1, CONV_TAPS)]
    acc = b_ref[...] + shifted[0] * w_ref[CONV_TAPS - 1:CONV_TAPS, :]
    for s in range(1, CONV_TAPS):
        acc = acc + shifted[s] * w_ref[CONV_TAPS - 1 - s:CONV_TAPS - s, :]
    return acc, shifted


def _conv_specs(n_rows, tl, col0=0):
    off = col0 // CONV_COLS
    cur = pl.BlockSpec((tl, CONV_COLS), lambda cj, li: (li, cj + off))
    prev = pl.BlockSpec((HALO, CONV_COLS), lambda cj, li: (jnp.maximum(li * (tl // HALO) - 1, 0), cj + off))
    nxt = pl.BlockSpec((HALO, CONV_COLS),
                       lambda cj, li: (jnp.minimum((li + 1) * (tl // HALO), n_rows // HALO - 1), cj + off))
    par = pl.BlockSpec((8, CONV_COLS), lambda cj, li: (0, cj + off))
    return cur, prev, nxt, par


def _conv_fwd(u, w8, b8, name):
    u, u0, c = _window(u)
    n_rows = u.shape[0]
    tl = _rows(n_rows)
    cur, _, _, par = _conv_specs(n_rows, tl)
    ucur, prev, _, _ = _conv_specs(n_rows, tl, u0)

    def body(u_ref, prev_ref, w_ref, b_ref, o_ref):
        acc, _ = _conv_pre(u_ref, prev_ref, w_ref, b_ref[0:1, :], pl.program_id(1))
        o_ref[...] = acc * _sig(acc)

    return pl.pallas_call(
        body, name=name, grid=(c // CONV_COLS, n_rows // tl), in_specs=[ucur, prev, par, par], out_specs=cur,
        out_shape=jax.ShapeDtypeStruct((n_rows, c), F32),
        compiler_params=_cparams("parallel", "parallel"),
    )(u, u, w8, b8)


def _conv_bwd_pre(u, w8, b8, dout, name):
    u, u0, c = _window(u)
    n_rows = u.shape[0]
    tl = _rows(n_rows)
    cur, _, _, par = _conv_specs(n_rows, tl)
    ucur, prev, _, _ = _conv_specs(n_rows, tl, u0)

    def body(u_ref, prev_ref, w_ref, b_ref, do_ref, dc_ref, acc_ref):
        @pl.when(pl.program_id(1) == 0)
        def _():
            acc_ref[...] = jnp.zeros_like(acc_ref)

        acc, shifted = _conv_pre(u_ref, prev_ref, w_ref, b_ref[0:1, :], pl.program_id(1))
        sg = _sig(acc)
        dc = do_ref[...] * (sg * (1.0 + acc * (1.0 - sg)))
        dc_ref[...] = dc
        for k in range(CONV_TAPS):
            acc_ref[k:k + 1, :] += jnp.sum(dc * shifted[CONV_TAPS - 1 - k], axis=0, keepdims=True)
        acc_ref[CONV_TAPS:CONV_TAPS + 1, :] += jnp.sum(dc, axis=0, keepdims=True)

    return pl.pallas_call(
        body, name=name, grid=(c // CONV_COLS, n_rows // tl), in_specs=[ucur, prev, par, par, cur],
        out_specs=(cur, par),
        out_shape=(jax.ShapeDtypeStruct((n_rows, c), F32), jax.ShapeDtypeStruct((8, c), F32)),
        compiler_params=_cparams("parallel", "arbitrary"),
    )(u, u, w8, b8, dout)


def _conv_bwd_in(dc, w8, name, into):
    n_rows, c = dc.shape
    tl = _rows(n_rows)
    cur, _, nxt, par = _conv_specs(n_rows, tl)
    n_l = n_rows // tl
    extra, extra_specs, aliases, col0, width = _into(into, 3, 0)
    out_spec = _conv_specs(n_rows, tl, col0)[0]

    def body(*refs):
        dc_ref, next_ref, w_ref = refs[:3]
        o_ref = refs[-1]
        cur_v = dc_ref[...]
        next8 = jnp.where(pl.program_id(1) == n_l - 1, 0.0, next_ref[...])
        row8 = lax.broadcasted_iota(jnp.int32, next8.shape, 0)
        acc = cur_v * w_ref[CONV_TAPS - 1:CONV_TAPS, :]
        for s in range(1, CONV_TAPS):
            acc = acc + _shift_up(cur_v, next8, s, row8) * w_ref[CONV_TAPS - 1 - s:CONV_TAPS - s, :]
        o_ref[...] = acc.astype(BF16)

    return pl.pallas_call(
        body, name=name, grid=(c // CONV_COLS, n_l), in_specs=[cur, nxt, par] + extra_specs, out_specs=out_spec,
        out_shape=jax.ShapeDtypeStruct((n_rows, width), BF16), input_output_aliases=aliases,
        compiler_params=_cparams("parallel", "parallel"),
    )(dc, dc, w8, *extra)


NORM_GROUP = SSD_D_INNER // SSD_GROUPS


def _gnorm_fwd(y, z, w, name):
    n_rows, c = y.shape
    z, z0, _ = _window(z)
    zoff = z0 // NORM_GROUP

    def body(y_ref, z_ref, w_ref, o_ref):
        zv = z_ref[...]
        yg = y_ref[...] * (zv * _sig(zv))
        r = lax.rsqrt(jnp.mean(yg * yg, axis=-1, keepdims=True) + RMS_EPS)
        o_ref[...] = (yg * r * w_ref[...]).astype(BF16)

    blk = pl.BlockSpec((_rows(n_rows),NORM_GROUP), lambda i, j: (i, j))
    zblk = pl.BlockSpec((_rows(n_rows),NORM_GROUP), lambda i, j: (i, j + zoff))
    wspec = pl.BlockSpec((1, NORM_GROUP), lambda i, j: (0, j))
    return pl.pallas_call(
        body, name=name, grid=(n_rows // _rows(n_rows), c // NORM_GROUP), in_specs=[blk, zblk, wspec], out_specs=blk,
        out_shape=jax.ShapeDtypeStruct((n_rows, c), BF16),
        compiler_params=_cparams("parallel", "parallel"),
    )(y, z, w.reshape(1, c))


def _gnorm_bwd(y, z, w, dyn, name, into):
    n_rows, c = y.shape
    z, z0, _ = _window(z)
    zoff = z0 // NORM_GROUP
    extra, extra_specs, aliases, col0, width = _into(into, 4, 1)
    doff = col0 // NORM_GROUP

    def body(*refs):
        y_ref, z_ref, w_ref, dn_ref = refs[:4]
        dy_ref, dz_ref, acc_ref = refs[-3:]
        @pl.when(pl.program_id(1) == 0)
        def _():
            acc_ref[...] = jnp.zeros_like(acc_ref)

        zv = z_ref[...]
        yv = y_ref[...]
        sz = _sig(zv)
        silu = zv * sz
        yg = yv * silu
        r = lax.rsqrt(jnp.mean(yg * yg, axis=-1, keepdims=True) + RMS_EPS)
        nrm = yg * r
        dn = dn_ref[...]
        acc_ref[0:1, :] += jnp.sum(dn * nrm, axis=0, keepdims=True)
        dnw = dn * w_ref[...]
        dyg = r * (dnw - nrm * jnp.mean(dnw * nrm, axis=-1, keepdims=True))
        dy_ref[...] = dyg * silu
        dz_ref[...] = (dyg * yv * (sz * (1.0 + zv * (1.0 - sz)))).astype(BF16)

    blk = pl.BlockSpec((_rows(n_rows),NORM_GROUP), lambda j, i: (i, j))
    zblk = pl.BlockSpec((_rows(n_rows),NORM_GROUP), lambda j, i: (i, j + zoff))
    wspec = pl.BlockSpec((1, NORM_GROUP), lambda j, i: (0, j))
    aspec = pl.BlockSpec((8, NORM_GROUP), lambda j, i: (0, j))
    return pl.pallas_call(
        body, name=name, grid=(c // NORM_GROUP, n_rows // _rows(n_rows)),
        in_specs=[blk, zblk, wspec, blk] + extra_specs,
        out_specs=(blk, pl.BlockSpec((_rows(n_rows), NORM_GROUP), lambda j, i: (i, j + doff)), aspec),
        out_shape=(jax.ShapeDtypeStruct((n_rows, c), F32), jax.ShapeDtypeStruct((n_rows, width), BF16),
                   jax.ShapeDtypeStruct((8, c), F32)),
        input_output_aliases=aliases,
        compiler_params=_cparams("parallel", "arbitrary"),
    )(y, z, w.reshape(1, c), dyn, *extra)


ATT_SCALE = ATT_HEAD_DIM ** -0.5
ATT_SLOPES = [2.0 ** (-8.0 * (h + 1) / ATT_HEADS) for h in range(ATT_HEADS)]
Q_PER_KV = ATT_HEADS // ATT_KV_HEADS


def _dup_half(t, g, lo):
    tr = pltpu.roll(t, ATT_HEAD_DIM, axis=1)
    return jnp.where(lo, t, tr) if g == 0 else jnp.where(lo, tr, t)


def _att_band(kv_ref, kvp_ref, n):
    cur = kv_ref[...]
    prev = jnp.where(n == 0, 0.0, kvp_ref[...])
    lo = lax.broadcasted_iota(jnp.int32, (ATT_BLOCK, LANE), 1) < ATT_HEAD_DIM
    bands = []
    for g in range(ATT_KV_HEADS):
        kb = jnp.concatenate([_dup_half(prev[:, :LANE], g, lo), _dup_half(cur[:, :LANE], g, lo)], axis=0)
        vb = jnp.concatenate([_dup_half(prev[:, LANE:], g, lo), _dup_half(cur[:, LANE:], g, lo)], axis=0)
        bands.append((kb.astype(BF16), vb.astype(BF16)))
    return bands


def _att_tile(n):
    shape = (2 * ATT_BLOCK, ATT_BLOCK)
    row = lax.broadcasted_iota(jnp.int32, shape, 0)
    i = row & (ATT_BLOCK - 1)
    s = lax.broadcasted_iota(jnp.int32, shape, 1)
    upper = s > i
    dist = ((i - s) & (ATT_BLOCK - 1)).astype(F32)
    dead = upper & (n == 0)
    return upper, dist, dead, row[:, 0:1] < ATT_BLOCK


def _stack_pair(t, lo):
    return jnp.concatenate([jnp.where(lo, t, 0.0), jnp.where(lo, 0.0, t)], axis=0).astype(BF16)


def _att_exp(qs, kb, s_ref, j, tile):
    upper, dist, dead, first = tile
    s2 = _dot(qs, kb, _NT)
    slope = jnp.where(first, ATT_SLOPES[2 * j], ATT_SLOPES[2 * j + 1])
    sink = jnp.where(first, s_ref[0:1, 2 * j:2 * j + 1], s_ref[0:1, 2 * j + 1:2 * j + 2])
    s = jnp.where(upper, s2[:, :ATT_BLOCK], s2[:, ATT_BLOCK:]) - slope * dist
    s = jnp.where(dead, NEG, s)
    m = jnp.maximum(jnp.max(s, axis=-1, keepdims=True), sink)
    return jnp.exp(s - m), jnp.exp(sink - m)


def _band_split(t, upper):
    return jnp.concatenate([jnp.where(upper, t, 0.0), jnp.where(upper, 0.0, t)], axis=1)


def _att_fwd(q, kv, sinks8, name):
    q, q0, _ = _window(q)
    kv, kv0, _ = _window(kv)
    qoff, kvoff = q0 // Q_DIM, kv0 // (2 * LANE)
    n_rows = q.shape[0]
    nb = n_rows // ATT_BLOCK

    def body(q_ref, kv_ref, kvp_ref, s_ref, o_ref, o32_ref):
        n = pl.program_id(0)
        bands = _att_band(kv_ref, kvp_ref, n)
        lo = lax.broadcasted_iota(jnp.int32, (ATT_BLOCK, LANE), 1) < ATT_HEAD_DIM
        tile = _att_tile(n)
        ones_b = jnp.ones((2 * ATT_BLOCK, LANE), BF16)
        for j in range(ATT_HEADS // 2):
            kb, vb = bands[2 * j // Q_PER_KV]
            qs = _stack_pair(q_ref[:, j * LANE:(j + 1) * LANE] * ATT_SCALE, lo)
            p, es = _att_exp(qs, kb, s_ref, j, tile)
            pv = _dot(_band_split(p, tile[0]).astype(BF16), jnp.concatenate([vb, ones_b], axis=1))
            out = pv[:, :LANE] / (pv[:, LANE:] + es)
            out = jnp.where(lo, out[:ATT_BLOCK], out[ATT_BLOCK:])
            o_ref[:, j * LANE:(j + 1) * LANE] = out.astype(BF16)
            o32_ref[:, j * LANE:(j + 1) * LANE] = out

    return pl.pallas_call(
        body, name=name, grid=(nb,),
        in_specs=[pl.BlockSpec((ATT_BLOCK, Q_DIM), lambda n: (n, qoff)),
                  pl.BlockSpec((ATT_BLOCK, 2 * LANE), lambda n: (n, kvoff)),
                  pl.BlockSpec((ATT_BLOCK, 2 * LANE), lambda n: (jnp.maximum(n - 1, 0), kvoff)),
                  pl.BlockSpec((8, LANE), lambda n: (0, 0))],
        out_specs=(pl.BlockSpec((ATT_BLOCK, Q_DIM), lambda n: (n, 0)),) * 2,
        out_shape=(jax.ShapeDtypeStruct((n_rows, Q_DIM), BF16), jax.ShapeDtypeStruct((n_rows, Q_DIM), F32)),
        compiler_params=_cparams("parallel"),
    )(q, kv, kv, sinks8)


def _att_bwd(q, kv, sinks8, out32, dout, name, into):
    q, q0, _ = _window(q)
    kv, kv0, _ = _window(kv)
    qoff, kvoff = q0 // Q_DIM, kv0 // (2 * LANE)
    n_rows = q.shape[0]
    nb = n_rows // ATT_BLOCK

    extra, extra_specs, aliases, col0, width = _into(into, 6, 0)
    dqoff = col0 // Q_DIM

    def body(*refs):
        q_ref, kv_ref, kvp_ref, s_ref, o_ref, do_ref = refs[:6]
        dq_ref, dkv_ref, acc_ref, carry_ref = refs[-4:]
        n = pl.program_id(0)

        @pl.when(n == 0)
        def _():
            acc_ref[...] = jnp.zeros_like(acc_ref)
            carry_ref[...] = jnp.zeros_like(carry_ref)

        @pl.when(n == nb)
        def _():
            dkv_ref[...] = carry_ref[...].astype(BF16)

        @pl.when(n < nb)
        def _():
            bands = _att_band(kv_ref, kvp_ref, n)
            lo = lax.broadcasted_iota(jnp.int32, (ATT_BLOCK, LANE), 1) < ATT_HEAD_DIM
            lane1 = lax.broadcasted_iota(jnp.int32, (1, LANE), 1)
            tile = _att_tile(n)
            upper, first = tile[0], tile[3]
            ones_b = jnp.ones((ATT_BLOCK, LANE), BF16)
            ones2_b = jnp.ones((2 * LANE, LANE), BF16)
            dk_acc = [jnp.zeros((2 * ATT_BLOCK, LANE), F32) for _ in range(ATT_KV_HEADS)]
            dv_acc = [jnp.zeros((2 * ATT_BLOCK, LANE), F32) for _ in range(ATT_KV_HEADS)]
            dsink = jnp.zeros((1, LANE), F32)
            for j in range(ATT_HEADS // 2):
                g = 2 * j // Q_PER_KV
                kb, vb = bands[g]
                qs = _stack_pair(q_ref[:, j * LANE:(j + 1) * LANE] * ATT_SCALE, lo)
                dop = do_ref[:, j * LANE:(j + 1) * LANE].astype(F32)
                dos = _stack_pair(dop, lo)
                pu, es = _att_exp(qs, kb, s_ref, j, tile)
                inv = 1.0 / (_dot(pu.astype(BF16), ones_b) + es)
                p = pu * inv
                od = dos.astype(F32) * jnp.concatenate([o_ref[:, j * LANE:(j + 1) * LANE]] * 2, axis=0)
                od_hi = od.astype(BF16)
                delta = _dot(jnp.concatenate([od_hi, (od - od_hi.astype(F32)).astype(BF16)], axis=1), ones2_b)
                dp2 = _dot(dos, vb, _NT)
                dp = jnp.where(upper, dp2[:, :ATT_BLOCK], dp2[:, ATT_BLOCK:])
                ds2 = _band_split(p * (dp - delta), upper)
                psd = jnp.sum(es * inv * delta, axis=0, keepdims=True)
                psd0 = jnp.sum(jnp.where(first, es * inv * delta, 0.0), axis=0, keepdims=True)
                dsink = jnp.where(lane1 == 2 * j, -psd0, jnp.where(lane1 == 2 * j + 1, psd0 - psd, dsink))
                ds2_b = ds2.astype(BF16)
                dq = _dot(ds2_b, kb) * ATT_SCALE
                dq_ref[:, j * LANE:(j + 1) * LANE] = jnp.where(lo, dq[:ATT_BLOCK], dq[ATT_BLOCK:]).astype(BF16)
                dk_acc[g] = dk_acc[g] + _dot(ds2_b, qs, _TN)
                dv_acc[g] = dv_acc[g] + _dot(_band_split(p, upper).astype(BF16), dos, _TN)
            acc_ref[0:1, :] += dsink
            lo2 = lax.broadcasted_iota(jnp.int32, (2 * ATT_BLOCK, LANE), 1) < ATT_HEAD_DIM
            folded = []
            for acc in (dk_acc, dv_acc):
                t0 = acc[0] + pltpu.roll(acc[0], ATT_HEAD_DIM, axis=1)
                t1 = acc[1] + pltpu.roll(acc[1], ATT_HEAD_DIM, axis=1)
                folded.append(jnp.where(lo2, t0, t1))
            band = jnp.concatenate(folded, axis=1)
            dkv_ref[...] = (carry_ref[...] + band[:ATT_BLOCK]).astype(BF16)
            carry_ref[...] = band[ATT_BLOCK:]

    def qmap(n):
        return (jnp.minimum(n, nb - 1), 0)

    return pl.pallas_call(
        body, name=name, grid=(nb + 1,),
        in_specs=[pl.BlockSpec((ATT_BLOCK, Q_DIM), lambda n: (jnp.minimum(n, nb - 1), qoff)),
                  pl.BlockSpec((ATT_BLOCK, 2 * LANE), lambda n: (jnp.minimum(n, nb - 1), kvoff)),
                  pl.BlockSpec((ATT_BLOCK, 2 * LANE),
                               lambda n: (jnp.maximum(jnp.minimum(n, nb - 1) - 1, 0), kvoff)),
                  pl.BlockSpec((8, LANE), lambda n: (0, 0)),
                  pl.BlockSpec((ATT_BLOCK, Q_DIM), qmap),
                  pl.BlockSpec((ATT_BLOCK, Q_DIM), qmap)] + extra_specs,
        out_specs=(pl.BlockSpec((ATT_BLOCK, Q_DIM), lambda n: (jnp.minimum(n, nb - 1), dqoff)),
                   pl.BlockSpec((ATT_BLOCK, 2 * LANE), lambda n: (jnp.maximum(n - 1, 0), 0)),
                   pl.BlockSpec((8, LANE), lambda n: (0, 0))),
        out_shape=(jax.ShapeDtypeStruct((n_rows, width), BF16), jax.ShapeDtypeStruct((n_rows, 2 * LANE), BF16),
                   jax.ShapeDtypeStruct((8, LANE), F32)),
        input_output_aliases=aliases,
        scratch_shapes=[pltpu.VMEM((ATT_BLOCK, 2 * LANE), F32)],
        compiler_params=_cparams("arbitrary"),
    )(q, kv, kv, sinks8, out32, dout, *extra)


HEADS_PER_GROUP = SSD_HEADS // SSD_GROUPS
PAIRS_PER_GROUP = HEADS_PER_GROUP // 2
T = SSD_CHUNK


def _cumsum_mm(mat, x):
    hi = x.astype(BF16)
    r = x - hi.astype(F32)
    mid = r.astype(BF16)
    lo = (r - mid.astype(F32)).astype(BF16)
    w = x.shape[1]
    out = _dot(mat, jnp.concatenate([hi, mid, lo], axis=1))
    return out[:, :w] + out[:, w:2 * w] + out[:, 2 * w:]


def _ssd_prep(dtr_ref, par_ref):
    dt = _softplus(dtr_ref[...] + par_ref[0:1, :])
    a = -jnp.exp(par_ref[1:2, :])
    ri = lax.broadcasted_iota(jnp.int32, (T, T), 0)
    ci = lax.broadcasted_iota(jnp.int32, (T, T), 1)
    cs = _cumsum_mm((ri >= ci).astype(BF16), dt * a)
    lo = lax.broadcasted_iota(jnp.int32, (T, LANE), 1) < SSD_CHUNK // 2

    def expand(arr):
        rows = arr.shape[0]
        return jnp.concatenate([jnp.where(lo[:rows], arr[:, 2 * j:2 * j + 1], arr[:, 2 * j + 1:2 * j + 2])
                                for j in range(PAIRS_PER_GROUP)], axis=1)

    tot = cs[T - 1:T, :]
    return {"dt": dt, "a": a, "cs": cs, "cst": cs.T, "lo": lo, "ri": ri, "ci": ci, "expand": expand,
            "dt_x": expand(dt), "ecs_x": expand(jnp.exp(cs)), "dec_x": expand(jnp.exp(tot - cs)),
            "et_x": expand(jnp.exp(tot)), "etot": jnp.exp(tot), "dsk_x": expand(par_ref[2:3, :])}


def _wide_masks():
    r = lax.broadcasted_iota(jnp.int32, (T, 2 * T), 0)
    l = lax.broadcasted_iota(jnp.int32, (T, 2 * T), 1)
    s = l & (T - 1)
    return r >= s, s >= r, l < T


def _wide_cs(q, k0, even):
    cs, cst = q["cs"], q["cst"]
    col = jnp.where(even, cs[:, k0:k0 + 1], cs[:, k0 + 1:k0 + 2])
    row = jnp.concatenate([cst[k0:k0 + 1, :], cst[k0 + 1:k0 + 2, :]], axis=1)
    return col, row


def _ssd_fwd(xs, bm, cm, dtr, par, name):
    dtr, dt0, _ = _window(dtr)
    dtoff = dt0 // LANE
    n_rows = xs.shape[0]
    nc = n_rows // T
    gw = PAIRS_PER_GROUP * LANE

    def body(x_ref, b_ref, c_ref, dtr_ref, par_ref, y_ref, hs_ref, h_ref):
        @pl.when(pl.program_id(1) == 0)
        def _():
            h_ref[...] = jnp.zeros_like(h_ref)

        q = _ssd_prep(dtr_ref, par_ref)
        lo = q["lo"]
        tri_w, _, even = _wide_masks()
        bg_b = b_ref[...].astype(BF16)
        cg_b = c_ref[...].astype(BF16)
        xv = x_ref[...]
        xdt = xv * q["dt_x"]
        h = h_ref[...]
        hs_ref[0, 0] = h
        yo = q["ecs_x"] * _dot(cg_b, h.astype(BF16))
        h_ref[...] = h * q["et_x"] + _dot(bg_b, (xdt * q["dec_x"]).astype(BF16), _TN)
        cb = _dot(cg_b, bg_b, _NT)
        cb_w = jnp.concatenate([cb, cb], axis=1)
        for j in range(PAIRS_PER_GROUP):
            col, row = _wide_cs(q, 2 * j, even)
            m_w = (jnp.exp(jnp.where(tri_w, col - row, NEG)) * cb_w).astype(BF16)
            sl = slice(j * LANE, (j + 1) * LANE)
            y_ref[:, sl] = (_dot(m_w, _stack_pair(xdt[:, sl], lo)) + yo[:, sl] + q["dsk_x"][:, sl] * xv[:, sl])

    return pl.pallas_call(
        body, name=name, grid=(SSD_GROUPS, nc),
        in_specs=[pl.BlockSpec((T, gw), lambda g, c: (c, g)),
                  pl.BlockSpec((T, SSD_STATE), lambda g, c: (c, g)),
                  pl.BlockSpec((T, SSD_STATE), lambda g, c: (c, g)),
                  pl.BlockSpec((T, LANE), lambda g, c: (c, g + dtoff)),
                  pl.BlockSpec((8, LANE), lambda g, c: (0, g))],
        out_specs=(pl.BlockSpec((T, gw), lambda g, c: (c, g)),
                   pl.BlockSpec((1, 1, SSD_STATE, gw), lambda g, c: (g, c, 0, 0))),
        out_shape=(jax.ShapeDtypeStruct((n_rows, SSD_D_INNER), F32),
                   jax.ShapeDtypeStruct((SSD_GROUPS, nc, SSD_STATE, gw), F32)),
        scratch_shapes=[pltpu.VMEM((SSD_STATE, gw), F32)],
        compiler_params=_cparams("parallel", "arbitrary"),
    )(xs, bm, cm, dtr, par)


def _ssd_bwd(xs, bm, cm, dtr, par, hs, dy, name, into):
    dtr, dt0, _ = _window(dtr)
    dtoff = dt0 // LANE
    n_rows = xs.shape[0]
    nc = n_rows // T
    gw = PAIRS_PER_GROUP * LANE
    extra, extra_specs, aliases, col0, width = _into(into, 7, 3)
    ddoff = col0 // LANE

    def body(*refs):
        x_ref, b_ref, c_ref, dtr_ref, par_ref, hs_ref, dy_ref = refs[:7]
        dx_ref, db_ref, dc_ref, ddtr_ref, acc_ref, dh_ref = refs[-6:]

        @pl.when(pl.program_id(1) == 0)
        def _():
            dh_ref[...] = jnp.zeros_like(dh_ref)
            acc_ref[...] = jnp.zeros_like(acc_ref)

        q = _ssd_prep(dtr_ref, par_ref)
        lo, dt, a = q["lo"], q["dt"], q["a"]
        tri_w, trit_w, even = _wide_masks()
        lane = lax.broadcasted_iota(jnp.int32, (T, LANE), 1)
        lane1 = lane[0:1, :]
        last_row = lax.broadcasted_iota(jnp.int32, (T, 1), 0) == T - 1
        bg_b = b_ref[...].astype(BF16)
        cg_b = c_ref[...].astype(BF16)
        xv = x_ref[...]
        dyv = dy_ref[...]
        xdt = xv * q["dt_x"]
        h = hs_ref[0, 0]
        dhn = dh_ref[...]
        h_b, dhn_b = h.astype(BF16), dhn.astype(BF16)
        yo = q["ecs_x"] * _dot(cg_b, h_b)
        bdh = q["dec_x"] * _dot(bg_b, dhn_b)
        dye = (dyv * q["ecs_x"]).astype(BF16)
        xd = (xdt * q["dec_x"]).astype(BF16)
        dcg = _dot(dye, h_b, _NT)
        dbg = _dot(xd, dhn_b, _NT)
        dh_ref[...] = dhn * q["et_x"] + _dot(cg_b, dye, _TN)
        e4_all = xdt * bdh
        f_all = dyv * yo - e4_all
        tot_row = jnp.sum(e4_all, axis=0, keepdims=True) + q["et_x"] * jnp.sum(h * dhn, axis=0, keepdims=True)
        dsk_row = jnp.sum(dyv * xv, axis=0, keepdims=True)
        cb = _dot(cg_b, bg_b, _NT)
        cbt = _dot(bg_b, cg_b, _NT)
        cb_w = jnp.concatenate([cb, cb], axis=1)
        cbt_w = jnp.concatenate([cbt, cbt], axis=1)
        dcb = jnp.zeros((T, T), F32)
        dcbt = jnp.zeros((T, T), F32)
        dcs_acc = jnp.zeros((T, LANE), F32)
        ddt_acc = jnp.zeros((T, LANE), F32)
        dsk_acc = jnp.zeros((1, LANE), F32)
        tot_acc = jnp.zeros((1, LANE), F32)
        ind_r = lax.broadcasted_iota(jnp.int32, (2 * T, LANE), 0)
        ind_l = lax.broadcasted_iota(jnp.int32, (2 * T, LANE), 1)

        def halves(t):
            return (jnp.sum(jnp.where(lo[0:1], t, 0.0), axis=-1, keepdims=True),
                    jnp.sum(jnp.where(lo[0:1], 0.0, t), axis=-1, keepdims=True))

        def split2(t):
            hi = t.astype(BF16)
            return jnp.concatenate([hi, (t - hi.astype(F32)).astype(BF16)], axis=1)

        for j in range(PAIRS_PER_GROUP):
            k0, k1 = 2 * j, 2 * j + 1
            sl = slice(j * LANE, (j + 1) * LANE)
            col, row = _wide_cs(q, k0, even)
            lm_w = jnp.exp(jnp.where(tri_w, col - row, NEG))
            lmt_w = jnp.exp(jnp.where(trit_w, row - col, NEG))
            dyp, xp = dyv[:, sl], xdt[:, sl]
            dym, xm = _stack_pair(dyp, lo), _stack_pair(xp, lo)
            dm_w = _dot(dyp.astype(BF16), xm, _NT)
            dmt_w = _dot(xp.astype(BF16), dym, _NT)
            mm_w = lm_w * cb_w
            mmt_w = lmt_w * cbt_w
            dxdt = _dot(mmt_w.astype(BF16), dym) + bdh[:, sl]
            g1 = dm_w * lm_w
            g2 = dmt_w * lmt_w
            dcb = dcb + g1[:, :T] + g1[:, T:]
            dcbt = dcbt + g2[:, :T] + g2[:, T:]
            ind_w = jnp.where(ind_l == jnp.where(ind_r < T, k0, k1), 1.0, 0.0).astype(BF16)
            ind_p = jnp.where(ind_l[:T] == jnp.where(ind_r[:T] < SSD_CHUNK // 2, k0, k1), 1.0, 0.0).astype(BF16)
            dcs_acc = dcs_acc + _dot(
                jnp.concatenate([split2(dm_w * mm_w - dmt_w * mmt_w), split2(f_all[:, sl])], axis=1),
                jnp.concatenate([ind_w, ind_w, ind_p, ind_p], axis=0))
            ddt_acc = ddt_acc + _dot(split2(dxdt * xv[:, sl]), jnp.concatenate([ind_p, ind_p], axis=0))
            tot2 = halves(tot_row[:, sl])
            tot_acc = jnp.where(lane1 == k0, tot2[0], jnp.where(lane1 == k1, tot2[1], tot_acc))
            dsk2 = halves(dsk_row[:, sl])
            dsk_acc = jnp.where(lane1 == k0, dsk2[0], jnp.where(lane1 == k1, dsk2[1], dsk_acc))
            dx_ref[:, sl] = dxdt * q["dt_x"][:, sl] + q["dsk_x"][:, sl] * dyp
        dcs_acc = dcs_acc + jnp.where(last_row, tot_acc, 0.0)
        dc_ref[...] = dcg + _dot(dcb.astype(BF16), bg_b)
        db_ref[...] = dbg + _dot(dcbt.astype(BF16), cg_b)
        dda = _cumsum_mm((q["ci"] >= q["ri"]).astype(BF16), dcs_acc)
        ddt = ddt_acc + dda * a
        ddtr = ddt * _sig(dtr_ref[...] + par_ref[0:1, :])
        ddtr_ref[...] = ddtr.astype(BF16)
        acc_ref[0:1, :] += jnp.sum(ddtr, axis=0, keepdims=True)
        acc_ref[1:2, :] += jnp.sum(dda * dt, axis=0, keepdims=True) * a
        acc_ref[2:3, :] += dsk_acc

    def rev(g, c):
        return (nc - 1 - c, g)

    return pl.pallas_call(
        body, name=name, grid=(SSD_GROUPS, nc),
        in_specs=[pl.BlockSpec((T, gw), rev),
                  pl.BlockSpec((T, SSD_STATE), rev),
                  pl.BlockSpec((T, SSD_STATE), rev),
                  pl.BlockSpec((T, LANE), lambda g, c: (nc - 1 - c, g + dtoff)),
                  pl.BlockSpec((8, LANE), lambda g, c: (0, g)),
                  pl.BlockSpec((1, 1, SSD_STATE, gw), lambda g, c: (g, nc - 1 - c, 0, 0)),
                  pl.BlockSpec((T, gw), rev)] + extra_specs,
        out_specs=(pl.BlockSpec((T, gw), rev),
                   pl.BlockSpec((T, SSD_STATE), rev),
                   pl.BlockSpec((T, SSD_STATE), rev),
                   pl.BlockSpec((T, LANE), lambda g, c: (nc - 1 - c, g + ddoff)),
                   pl.BlockSpec((8, LANE), lambda g, c: (0, g))),
        out_shape=(jax.ShapeDtypeStruct((n_rows, SSD_D_INNER), F32),
                   jax.ShapeDtypeStruct((n_rows, BC_DIM), F32),
                   jax.ShapeDtypeStruct((n_rows, BC_DIM), F32),
                   jax.ShapeDtypeStruct((n_rows, width), BF16),
                   jax.ShapeDtypeStruct((8, DT_PAD), F32)),
        input_output_aliases=aliases,
        scratch_shapes=[pltpu.VMEM((SSD_STATE, gw), F32)],
        compiler_params=_cparams("parallel", "arbitrary"),
    )(xs, bm, cm, dtr, par, hs, dy, *extra)


ADAM_ROWS = 256


def _adamw(lands, w, m, v, name):
    na = len(lands)
    n_slots, r, wd = lands[0].shape
    tr = r if r <= 2 * ADAM_ROWS else ADAM_ROWS
    nj = r // tr
    bc1 = 1.0 - ADAM_B1 ** ADAM_STEP
    bc2 = 1.0 - ADAM_B2 ** ADAM_STEP

    def body(*refs):
        l_refs = refs[:na]
        w_ref, m_ref, v_ref, g_ref, d_ref, nm_ref, nv_ref = refs[na:]
        for a in range(na):
            @pl.when(pl.program_id(0) == a)
            def _(l_ref=l_refs[a]):
                g = l_ref[0].astype(F32)
                for s in range(1, n_slots):
                    g = g + l_ref[s].astype(F32)
                mn = ADAM_B1 * m_ref[0] + (1.0 - ADAM_B1) * g
                vn = ADAM_B2 * v_ref[0] + (1.0 - ADAM_B2) * (g * g)
                mh = mn / bc1
                vh = vn / bc2
                g_ref[0] = g
                nm_ref[0] = mn
                nv_ref[0] = vn
                d_ref[0] = -ADAM_LR * (mh / (jnp.sqrt(vh) + ADAM_EPS) + ADAM_WD * w_ref[0])

    def land_spec(a):
        return pl.BlockSpec((n_slots, tr, wd),
                            lambda i, j: (0, jnp.where(i == a, j, jnp.where(i < a, 0, nj - 1)), 0))

    blk = pl.BlockSpec((1, tr, wd), lambda i, j: (i, j, 0))
    shp = jax.ShapeDtypeStruct((na, r, wd), F32)
    return pl.pallas_call(
        body, name=name, grid=(na, nj), in_specs=[land_spec(a) for a in range(na)] + [blk, blk, blk],
        out_specs=(blk, blk, blk, blk), out_shape=(shp, shp, shp, shp),
        compiler_params=_cparams("arbitrary", "arbitrary"),
    )(*lands, w, m, v)


def _mesh_pos():
    return lax.axis_index("x"), lax.axis_index("y"), lax.axis_index("c")


def _peer(pos, k):
    x, y, c = pos
    px = 1 - x if (k >> 2) & 1 else x
    py = 1 - y if (k >> 1) & 1 else y
    pc = 1 - c if k & 1 else c
    return px, py, pc


def _flat(pos):
    return 4 * pos[0] + 2 * pos[1] + pos[2]


HBM_SPEC = pl.BlockSpec(memory_space=pl.ANY)


ROW_SHARDED = ("w_ssd_out", "w_att_out", "w_mix_out", "w_ffn_down")
COL_SHARDED = ("w_in", "w_ffn_gate", "w_ffn_up")
GATHERED = ROW_SHARDED + COL_SHARDED + ("conv_w",)


SEM_SPEC = pl.BlockSpec(memory_space=pltpu.SEMAPHORE)
TOKEN = jax.ShapeDtypeStruct((8, LANE), F32)
SPLIT_EFFECT = pltpu.SideEffectType.DATAFLOW_SIDE_EFFECTING
GATHER_ROWS = "gather_rows"
GATHER_SLOT = "gather_slot"
SCATTER_ROWS = "scatter_rows"
SCATTER_SLOT = "scatter_slot"


def _land_shape(kind, src):
    if kind == GATHER_ROWS:
        return (N_DEV * src.shape[0],) + src.shape[1:]
    if kind == GATHER_SLOT:
        return (N_DEV,) + src.shape
    if kind == SCATTER_ROWS:
        return (N_DEV, src.shape[0] // N_DEV) + src.shape[1:]
    return src.shape


def _views(kind, src_ref, land_ref, pos, k):
    me = _flat(pos)
    if kind == GATHER_ROWS:
        r = src_ref.shape[0]
        return src_ref, land_ref.at[pl.ds(pl.multiple_of(me * r, 16), r), :]
    if kind == GATHER_SLOT:
        return src_ref, land_ref.at[me]
    dev = _flat(_peer(pos, k))
    if kind == SCATTER_ROWS:
        r = land_ref.shape[1]
        return src_ref.at[pl.ds(pl.multiple_of(dev * r, 16), r), :], land_ref.at[k]
    return src_ref.at[dev], land_ref.at[k]


def _hbm(x):
    return pltpu.with_memory_space_constraint(x, pltpu.HBM)


def _exchange_start(items, after, name):
    kinds = [k for k, _ in items]
    srcs = [_hbm(s) for _, s in items]
    lands = [_hbm(lax.empty(_land_shape(k, s), s.dtype)) for k, s in items]
    n = len(items)
    n_copy = n * (N_DEV - 1)

    def body(*refs):
        src_refs, land_refs = refs[:n], refs[n:2 * n]
        send_sems, recv_sems = refs[2 * n + 1], refs[2 * n + 2]
        token_ref = refs[4 * n + 3]
        pos = _mesh_pos()
        for i, kind in enumerate(kinds):
            for k in range(1, N_DEV):
                s, d = _views(kind, src_refs[i], land_refs[i], pos, k)
                j = i * (N_DEV - 1) + k - 1
                pltpu.make_async_remote_copy(src_ref=s, dst_ref=d, send_sem=send_sems.at[j], recv_sem=recv_sems.at[j],
                                             device_id=_peer(pos, k), device_id_type=MESH_ID).start()
        token_ref[...] = jnp.zeros_like(token_ref)

    arrs = srcs + lands
    outs = pl.pallas_call(
        body, name=name,
        in_specs=[HBM_SPEC] * (2 * n + 1),
        out_specs=[SEM_SPEC, SEM_SPEC] + [HBM_SPEC] * (2 * n) + [pl.BlockSpec(memory_space=pltpu.VMEM)],
        out_shape=[pltpu.SemaphoreType.DMA((n_copy,)), pltpu.SemaphoreType.DMA((n_copy,))]
        + [pltpu.HBM(a.shape, a.dtype) for a in arrs] + [TOKEN],
        input_output_aliases={i: 2 + i for i in range(2 * n)},
        compiler_params=pltpu.CompilerParams(has_side_effects=SPLIT_EFFECT),
    )(*arrs, after)
    return {"kinds": kinds, "send": outs[0], "recv": outs[1], "arrs": outs[2:2 + 2 * n], "token": outs[-1]}


def _exchange_wait(ex, after, name):
    kinds = ex["kinds"]
    n = len(kinds)

    def body(*refs):
        src_refs, land_refs = refs[:n], refs[n:2 * n]
        send_sems, recv_sems = refs[2 * n], refs[2 * n + 1]
        token_ref = refs[-1]
        pos = _mesh_pos()
        for i, kind in enumerate(kinds):
            for k in range(1, N_DEV):
                s, d = _views(kind, src_refs[i], land_refs[i], pos, k)
                j = i * (N_DEV - 1) + k - 1
                cp = pltpu.make_async_remote_copy(src_ref=s, dst_ref=d, send_sem=send_sems.at[j],
                                                  recv_sem=recv_sems.at[j], device_id=_peer(pos, k),
                                                  device_id_type=MESH_ID)
                cp.wait_send()
                cp.wait_recv()
        token_ref[...] = jnp.zeros_like(token_ref)

    outs = pl.pallas_call(
        body, name=name,
        in_specs=[HBM_SPEC] * (2 * n) + [SEM_SPEC, SEM_SPEC, HBM_SPEC],
        out_specs=[HBM_SPEC] * (2 * n) + [pl.BlockSpec(memory_space=pltpu.VMEM)],
        out_shape=[pltpu.HBM(a.shape, a.dtype) for a in ex["arrs"]] + [TOKEN],
        input_output_aliases={i: i for i in range(2 * n)},
        compiler_params=pltpu.CompilerParams(has_side_effects=SPLIT_EFFECT),
    )(*ex["arrs"], ex["send"], ex["recv"], after)
    lands = [_place_own(k, s, d) for k, s, d in zip(kinds, outs[:n], outs[n:2 * n])]
    return lands, outs[-1]


def _place_own(kind, src, land):
    me = _flat(_mesh_pos())
    zeros = (0,) * (src.ndim - 1)
    if kind == GATHER_ROWS:
        return lax.dynamic_update_slice(land, src, (me * src.shape[0],) + zeros)
    if kind == GATHER_SLOT:
        return lax.dynamic_update_slice(land, src[None], (me,) + (0,) * src.ndim)
    if kind == SCATTER_ROWS:
        r = land.shape[1]
        own = lax.dynamic_slice(src, (me * r,) + zeros, (r,) + src.shape[1:])
    else:
        own = lax.dynamic_index_in_dim(src, me, 0, keepdims=False)
    return lax.dynamic_update_slice(land, own[None], (0,) * land.ndim)


def _all_gather_small(x, name):
    r, w = x.shape

    def body(x_ref, out_ref, send_sems, recv_sems):
        pos = _mesh_pos()
        me = _flat(pos)
        copies = []
        for k in range(1, N_DEV):
            cp = pltpu.make_async_remote_copy(
                src_ref=x_ref, dst_ref=out_ref.at[me], send_sem=send_sems.at[k - 1], recv_sem=recv_sems.at[k - 1],
                device_id=_peer(pos, k), device_id_type=MESH_ID)
            cp.start()
            copies.append(cp)
        out_ref[me] = x_ref[...]
        for cp in copies:
            cp.wait()

    vmem = pl.BlockSpec(memory_space=pltpu.VMEM)
    return pl.pallas_call(
        body, name=name, in_specs=[vmem], out_specs=vmem,
        out_shape=jax.ShapeDtypeStruct((N_DEV, r, w), x.dtype),
        scratch_shapes=[pltpu.SemaphoreType.DMA((N_DEV - 1,)), pltpu.SemaphoreType.DMA((N_DEV - 1,))],
        compiler_params=pltpu.CompilerParams(has_side_effects=True),
    )(x)


def _cols(g, lo, hi):
    c = g.shape[-1]
    parts = []
    for d in range(N_DEV):
        a, b = max(lo, d * c), min(hi, (d + 1) * c)
        if a < b:
            parts.append(g[d, :, a - d * c:b - d * c])
    return parts[0] if len(parts) == 1 else jnp.concatenate(parts, axis=1)


def _col_chunks(g):
    c = g.shape[-1] // N_DEV
    return jnp.stack([g[:, d * c:(d + 1) * c] for d in range(N_DEV)])


IN_PART = ("w_in", "conv_w")
OUT_PART = ROW_SHARDED + ("w_ffn_gate", "w_ffn_up")
TRANSPOSED = ("w_ffn_gate", "w_ffn_up")


def _gather_items(w, names, l):
    items = []
    for n in names:
        blk = w[n][l] if n == "conv_w" else w[n][l].astype(BF16)
        if n in TRANSPOSED:
            blk = blk.T
        items.append((GATHER_ROWS if n in ROW_SHARDED + TRANSPOSED else GATHER_SLOT, blk))
    return items


def _scatter_items(grads, names):
    def chunked(g):
        return g if g.ndim == 3 else _col_chunks(g)

    return [(SCATTER_ROWS, grads[n]) if n in ROW_SHARDED + TRANSPOSED else (SCATTER_SLOT, chunked(grads[n]))
            for n in names]


SMALL = ("ln_in_g", "ln_in_b", "conv_b", "dt_bias", "a_log", "d_skip", "ssd_norm_w", "att_sinks",
         "ln_mix_g", "ln_mix_b", "ln_ffn_g", "ln_ffn_b")


def _pack_small(vals):
    flat = jnp.concatenate([vals[n].reshape(-1) for n in SMALL])
    n = flat.shape[0]
    rows = -(-n // LANE)
    rows = -(-rows // 8) * 8
    return jnp.pad(flat, (0, rows * LANE - n)).reshape(rows, LANE)


def _unpack_small(buf, shapes):
    flat = buf.reshape(-1)
    off = 0
    out = {}
    for n in SMALL:
        cnt = math.prod(shapes[n])
        out[n] = flat[off:off + cnt].reshape(shapes[n])
        off += cnt
    return out


def _to_group_major(v):
    lead = v.shape[:-1]
    t = v.reshape(lead + (SSD_GROUPS, HEADS_PER_GROUP))
    t = jnp.pad(t, [(0, 0)] * len(lead) + [(0, 0), (0, LANE - HEADS_PER_GROUP)])
    return t.reshape(lead + (DT_PAD,))


def _from_group_major(v):
    lead = v.shape[:-1]
    return v.reshape(lead + (SSD_GROUPS, LANE))[..., :HEADS_PER_GROUP].reshape(lead + (SSD_HEADS,))


def _rows8(v):
    return jnp.pad(v, ((0, 8 - v.shape[0]), (0, 0)))


IN_OFFS = {"q": (0, 1024), "kv": (1024, 1280), "z": (1280, 3328), "xs": (3328, 5376), "b": (5376, 5888),
           "c": (5888, 6400), "dt": (6400, 6432), "gl": (6432, 8480)}
PIECES = ("q", "kv", "z", "xs", "b", "c", "dt", "gl")


CAT = ("z", "xs", "gl", "q", "b", "c", "dt", "kv")
CAT_WIDTH = {"q": 1024, "z": 2048, "xs": 2048, "gl": 2048, "b": 512, "c": 512, "kv": 256, "dt": DT_PAD}
CAT_OFF = {p: sum(CAT_WIDTH[q] for q in CAT[:i]) for i, p in enumerate(CAT)}
CAT_DIM = sum(CAT_WIDTH.values())
MAIN_DIM = CAT_OFF["kv"]


def _cat_w_in(g):
    pieces = {p: _cols(g, lo, hi) for p, (lo, hi) in IN_OFFS.items()}
    pieces["dt"] = _to_group_major(pieces["dt"])
    return jnp.concatenate([pieces[p] for p in CAT], axis=1)


def _dw_in_chunks(dw_main, dw_kv):
    dt = _from_group_major(dw_main[:, CAT_OFF["dt"]:CAT_OFF["dt"] + DT_PAD])
    shard = IN_OFFS[PIECES[-1]][1] // N_DEV

    def piece(pc, a, b):
        if pc == "dt":
            return dt[:, a:b]
        if pc == "kv":
            return dw_kv[:, a:b]
        return dw_main[:, CAT_OFF[pc] + a:CAT_OFF[pc] + b]

    chunks = []
    for d in range(N_DEV):
        parts = []
        for pc in PIECES:
            lo, hi = IN_OFFS[pc]
            a, b = max(lo, d * shard), min(hi, (d + 1) * shard)
            if a < b:
                parts.append(piece(pc, a - lo, b - lo))
        chunks.append(parts[0] if len(parts) == 1 else jnp.concatenate(parts, axis=1))
    return jnp.stack(chunks)


def _params_out(W):
    return {n: W[n] for n in OUT_PART}


def _params_in(l, W, sm):
    p = {"w_cat": _cat_w_in(W["w_in"])}
    cw = _cols(W["conv_w"], 0, SSD_D_INNER + 2 * BC_DIM)
    cb = sm["conv_b"][l]
    segs = {"xs": (0, 2048), "b": (2048, 2560), "c": (2560, 3072)}
    p["conv_w8"] = {s: _rows8(cw[:, lo:hi]) for s, (lo, hi) in segs.items()}
    p["conv_b8"] = {s: _rows8(cb[None, lo:hi]) for s, (lo, hi) in segs.items()}
    p["ssd_par"] = _rows8(jnp.stack([_to_group_major(sm["dt_bias"][l]), _to_group_major(sm["a_log"][l]),
                                     _to_group_major(sm["d_skip"][l])]))
    p["norm_w"] = sm["ssd_norm_w"][l]
    p["sinks8"] = _rows8(jnp.pad(sm["att_sinks"][l], (0, LANE - ATT_HEADS))[None])
    for n in ("ln_mix_g", "ln_mix_b", "ln_ffn_g", "ln_ffn_b"):
        p[n] = sm[n][l]
    return p


def _fwd_mixers(h0, p, l, dep=None):
    tag = f"l{l}_"
    a = {"h0": h0}
    proj = _mm(h0, p["w_cat"], "nn", tag + "proj", dep=dep)
    for pc in CAT:
        a[pc] = (proj, CAT_OFF[pc], CAT_WIDTH[pc])
    for s in ("xs", "b", "c"):
        a[s + "c"] = _conv_fwd(a[s], p["conv_w8"][s], p["conv_b8"][s], tag + "conv_" + s)
    a["y"], a["hs"] = _ssd_fwd(a["xsc"], a["bc"], a["cc"], a["dt"], p["ssd_par"], tag + "ssd_fwd")
    a["yn"] = _gnorm_fwd(a["y"], a["z"], p["norm_w"], tag + "gnorm")
    a["att"], a["att32"] = _att_fwd(a["q"], a["kv"], p["sinks8"], tag + "att_fwd")
    return a


def _fwd_out(a, p, l, dep=None):
    tag = f"l{l}_"
    h0 = a["h0"]
    a["ya"], a["yb"], a["merged"] = _branch_out(a["yn"], a["att"], p["w_ssd_out"], p["w_att_out"], a["gl"],
                                                tag + "branch_out", dep=dep)
    a["mix"], a["h1"] = _mm_ln(a["merged"], p["w_mix_out"], h0, p["ln_mix_g"], p["ln_mix_b"], ALPHA,
                               tag + "mix_out_ln")
    a["fg"], a["fu"], a["act"] = _ffn_in(a["h1"], p["w_ffn_gate"], p["w_ffn_up"], tag + "ffn_in")
    a["ffn"], a["h2"] = _mm_ln(a["act"], p["w_ffn_down"], a["h1"], p["ln_ffn_g"], p["ln_ffn_b"], ALPHA,
                               tag + "ffn_down_ln")
    return a


def _dw(x, dy, name, dep=None):
    return _mm(x, dy, "tn", name, out_dtype=BF16, dep=dep)


def _bwd_out(a, p, dh2, l, dep=None):
    tag = f"l{l}_b_"
    gw, gs = {}, {}
    du2, acc = _ln_bwd(a["h1"], a["ffn"], p["ln_ffn_g"], dh2, ALPHA, tag + "ln_ffn")
    gs["ln_ffn_g"], gs["ln_ffn_b"] = acc[0], acc[1]
    gw["w_ffn_down"] = _dw(a["act"], du2, tag + "dw_down", dep=dep)
    dfg, dfu = _ffn_dact(du2, p["w_ffn_down"], a["fg"], a["fu"], tag + "ffn_dact", dep=dep)
    gw["w_ffn_gate"] = _dw(dfg, a["h1"], tag + "dw_gate")
    gw["w_ffn_up"] = _dw(dfu, a["h1"], tag + "dw_up")
    dh1 = _ffn_dh(dfg, dfu, p["w_ffn_gate"], p["w_ffn_up"], du2, ALPHA, tag + "dh1")
    du1, acc = _ln_bwd(a["h0"], a["mix"], p["ln_mix_g"], dh1, ALPHA, tag + "ln_mix")
    gs["ln_mix_g"], gs["ln_mix_b"] = acc[0], acc[1]
    gw["w_mix_out"] = _dw(a["merged"], du1, tag + "dw_mix")
    dya, dyb, dproj = _merge_bwd(a["gl"], a["ya"], a["yb"], du1, p["w_mix_out"], tag + "merge",
                                 (None, CAT_OFF["gl"], MAIN_DIM))
    gw["w_ssd_out"] = _dw(a["yn"], dya, tag + "dw_ssd")
    gw["w_att_out"] = _dw(a["att"], dyb, tag + "dw_att")
    return {"du1": du1, "dya": dya, "dyb": dyb, "dproj": dproj}, gw, gs


def _bwd_mixers(a, p, carry, l, dep=None):
    tag = f"l{l}_b_"
    gs = {}
    du1, dproj = carry["du1"], carry["dproj"]

    def win(pc):
        return (dproj, CAT_OFF[pc], MAIN_DIM)

    dyn = _mm(carry["dya"], p["w_ssd_out"], "nt", tag + "dyn", dep=dep)
    datt = _mm(carry["dyb"], p["w_att_out"], "nt", tag + "datt", out_dtype=BF16, dep=dep)
    dproj, dkv, acc = _att_bwd(a["q"], a["kv"], p["sinks8"], a["att32"], datt, tag + "att", win("q"))
    gs["att_sinks"] = acc[0, :ATT_HEADS]
    dy, dproj, acc = _gnorm_bwd(a["y"], a["z"], p["norm_w"], dyn, tag + "gnorm", win("z"))
    gs["ssd_norm_w"] = acc[0]
    dxs, dbm, dcm, dproj, acc = _ssd_bwd(a["xsc"], a["bc"], a["cc"], a["dt"], p["ssd_par"], a["hs"], dy,
                                         tag + "ssd", win("dt"))
    gs["dt_bias"], gs["a_log"], gs["d_skip"] = (_from_group_major(acc[i]) for i in range(3))
    dconv_w, dconv_b = [], []
    for s, dout in (("xs", dxs), ("b", dbm), ("c", dcm)):
        dc, acc = _conv_bwd_pre(a[s], p["conv_w8"][s], p["conv_b8"][s], dout, tag + "conv_pre_" + s)
        dconv_w.append(acc[:CONV_TAPS])
        dconv_b.append(acc[CONV_TAPS])
        dproj = _conv_bwd_in(dc, p["conv_w8"][s], tag + "conv_in_" + s, win(s))
    gconv = jnp.concatenate(dconv_w, axis=1)
    gs["conv_b"] = jnp.concatenate(dconv_b)
    w_main, w_kv = p["w_cat"][:, :MAIN_DIM], p["w_cat"][:, MAIN_DIM:]
    dw_main, dw_kv = _dw(a["h0"], dproj, tag + "dw_in"), _dw(a["h0"], dkv, tag + "dw_in_kv")

    def grad_h0(dep=None):
        dh0 = _mm(dproj, w_main, "nt", tag + "dh0", add=du1, add_scale=ALPHA, dep=dep)
        return _mm(dkv, w_kv, "nt", tag + "dh0_kv", add=dh0)

    return grad_h0, _dw_in_chunks(dw_main, dw_kv), gconv, gs


def _step(x, target, w, m, v):
    x2 = x[0]
    t2 = target[0]
    tok = jnp.zeros(TOKEN.shape, TOKEN.dtype)

    ex = _exchange_start(_gather_items(w, IN_PART, 0), tok, "gather_l0_in_start")
    h = _ln_fwd(x2, None, w["ln_in_g"], w["ln_in_b"], 1.0, "ln_in")
    lands, tok = _exchange_wait(ex, h, "gather_l0_in_wait")
    p0 = _params_in(0, dict(zip(IN_PART, lands)), w)
    ex = _exchange_start(_gather_items(w, OUT_PART, 0) + _gather_items(w, IN_PART, 1), tok,
                         "gather_l0_out_l1_in_start")
    a0 = _fwd_mixers(h, p0, 0, dep=ex["token"])
    lands, tok = _exchange_wait(ex, a0["att"], "gather_l0_out_l1_in_wait")
    p0.update(_params_out(dict(zip(OUT_PART, lands))))
    p1 = _params_in(1, dict(zip(IN_PART, lands[len(OUT_PART):])), w)
    ex = _exchange_start(_gather_items(w, OUT_PART, 1), tok, "gather_l1_out_start")
    a0 = _fwd_out(a0, p0, 0, dep=ex["token"])
    lands, tok = _exchange_wait(ex, a0["h2"], "gather_l1_out_wait")
    p1.update(_params_out(dict(zip(OUT_PART, lands))))
    a1 = _fwd_out(_fwd_mixers(a0["h2"], p1, 1), p1, 1)

    sse, dh = _loss_fwd_bwd(a1["h2"], t2, "loss")
    loss = lax.psum(0.5 / D_MODEL * sse[0, 0], ("x", "y", "c"))

    carry, gw1, gs1 = _bwd_out(a1, p1, dh, 1)
    grad_h0, gw1["w_in"], gw1["conv_w"], gs = _bwd_mixers(a1, p1, carry, 1)
    dh = grad_h0()
    gs1.update(gs)
    ex1 = _exchange_start(_scatter_items(gw1, GATHERED), tok, "scatter_l1_start")
    carry, gw0, gs0 = _bwd_out(a0, p0, dh, 0, dep=ex1["token"])
    lands, tok = _exchange_wait(ex1, carry["dyb"], "scatter_l1_wait")
    land1 = dict(zip(GATHERED, lands))
    ex0 = _exchange_start(_scatter_items(gw0, OUT_PART), tok, "scatter_l0_out_start")
    grad_h0, gw0["w_in"], gw0["conv_w"], gs = _bwd_mixers(a0, p0, carry, 0, dep=ex0["token"])
    gs0.update(gs)
    lands, tok = _exchange_wait(ex0, gw0["w_in"], "scatter_l0_out_wait")
    land0 = dict(zip(OUT_PART, lands))
    ex0 = _exchange_start(_scatter_items(gw0, IN_PART), tok, "scatter_l0_in_start")
    dh = grad_h0(dep=ex0["token"])
    grad_x2, acc = _ln_bwd(x2, None, w["ln_in_g"], dh, 1.0, "ln_in_b")

    outs = [{} for _ in range(4)]

    def update(names):
        res = None
        for n in names:
            if n in TRANSPOSED:
                res = _adamw([land0[n], land1[n]], *(jnp.swapaxes(t, 1, 2) for t in (w[n], m[n], v[n])),
                             "adamw_" + n)
                res = tuple(jnp.swapaxes(t, 1, 2) for t in res)
            else:
                res = _adamw([land0[n], land1[n]], w[n], m[n], v[n], "adamw_" + n)
            for o, t in zip(outs, res):
                o[n] = t
        return res[1]

    update(OUT_PART)
    gsm = {"ln_in_g": acc[0], "ln_in_b": acc[1]}
    for n in SMALL[2:]:
        gsm[n] = jnp.stack([gs0[n], gs1[n]])
    small_shapes = {n: w[n].shape for n in SMALL}
    land_s = _all_gather_small(_pack_small(gsm), "small_grads_all_gather")
    res = _adamw([land_s], _pack_small(w)[None], _pack_small(m)[None], _pack_small(v)[None], "adamw_small")
    for o, t in zip(outs, res):
        o.update(_unpack_small(t[0], small_shapes))
    lands, _ = _exchange_wait(ex0, res[1], "scatter_l0_in_wait")
    land0.update(zip(IN_PART, lands))
    update(IN_PART)
    return loss, grad_x2[None], outs


WEIGHT_NAMES = ("ln_in_g", "ln_in_b", "w_in", "conv_w", "conv_b", "dt_bias", "a_log", "d_skip", "ssd_norm_w",
                "att_sinks", "w_ssd_out", "w_att_out", "w_mix_out", "ln_mix_g", "ln_mix_b", "w_ffn_gate",
                "w_ffn_up", "w_ffn_down", "ln_ffn_g", "ln_ffn_b")


def kernel(x, ln_in_g, ln_in_b, w_in, conv_w, conv_b, dt_bias, a_log, d_skip, ssd_norm_w, att_sinks, w_ssd_out, w_att_out, w_mix_out, ln_mix_g, ln_mix_b, w_ffn_gate, w_ffn_up, w_ffn_down, ln_ffn_g, ln_ffn_b, loss_target, m_ln_in_g, m_ln_in_b, m_w_in, m_conv_w, m_conv_b, m_dt_bias, m_a_log, m_d_skip, m_ssd_norm_w, m_att_sinks, m_w_ssd_out, m_w_att_out, m_w_mix_out, m_ln_mix_g, m_ln_mix_b, m_w_ffn_gate, m_w_ffn_up, m_w_ffn_down, m_ln_ffn_g, m_ln_ffn_b, v_ln_in_g, v_ln_in_b, v_w_in, v_conv_w, v_conv_b, v_dt_bias, v_a_log, v_d_skip, v_ssd_norm_w, v_att_sinks, v_w_ssd_out, v_w_att_out, v_w_mix_out, v_ln_mix_g, v_ln_mix_b, v_w_ffn_gate, v_w_ffn_up, v_w_ffn_down, v_ln_ffn_g, v_ln_ffn_b):
    w = dict(zip(WEIGHT_NAMES, (ln_in_g, ln_in_b, w_in, conv_w, conv_b, dt_bias, a_log, d_skip, ssd_norm_w,
                                att_sinks, w_ssd_out, w_att_out, w_mix_out, ln_mix_g, ln_mix_b, w_ffn_gate,
                                w_ffn_up, w_ffn_down, ln_ffn_g, ln_ffn_b)))
    m = dict(zip(WEIGHT_NAMES, (m_ln_in_g, m_ln_in_b, m_w_in, m_conv_w, m_conv_b, m_dt_bias, m_a_log, m_d_skip,
                                m_ssd_norm_w, m_att_sinks, m_w_ssd_out, m_w_att_out, m_w_mix_out, m_ln_mix_g,
                                m_ln_mix_b, m_w_ffn_gate, m_w_ffn_up, m_w_ffn_down, m_ln_ffn_g, m_ln_ffn_b)))
    v = dict(zip(WEIGHT_NAMES, (v_ln_in_g, v_ln_in_b, v_w_in, v_conv_w, v_conv_b, v_dt_bias, v_a_log, v_d_skip,
                                v_ssd_norm_w, v_att_sinks, v_w_ssd_out, v_w_att_out, v_w_mix_out, v_ln_mix_g,
                                v_ln_mix_b, v_w_ffn_gate, v_w_ffn_up, v_w_ffn_down, v_ln_ffn_g, v_ln_ffn_b)))
    loss, grad_x, outs = _step(x, loss_target, w, m, v)
    result = [loss, grad_x]
    for o in outs:
        result.extend(o[n] for n in WEIGHT_NAMES)
    return tuple(result)
```

```python
import math

import jax
import jax.numpy as jnp
from jax import lax
from jax.experimental import pallas as pl
from jax.experimental.pallas import tpu as pltpu

F32 = jnp.float32
BF16 = jnp.bfloat16

D_MODEL = 1024
DEPTH = 2
N_DEV = 8
ATT_HEADS = 16
ATT_KV_HEADS = 2
ATT_HEAD_DIM = 64
ATT_BLOCK = 128
SSD_D_INNER = 2048
SSD_HEADS = 32
SSD_GROUPS = 4
SSD_STATE = 128
SSD_CHUNK = 128
FFN_HIDDEN = 2816
LN_EPS = 1e-5
RMS_EPS = 1e-5
ALPHA = (2 * DEPTH) ** 0.25
Q_DIM = 1024
BC_DIM = 512
DT_PAD = 512

ADAM_LR = 0.001
ADAM_B1 = 0.9
ADAM_B2 = 0.999
ADAM_EPS = 1e-08
ADAM_WD = 0.01
ADAM_STEP = 10

LANE = 128
VMEM_LIMIT = 48 * 1024 * 1024
NEG = -1e30

_NN = (((1,), (0,)), ((), ()))
_NT = (((1,), (1,)), ((), ()))
_TN = (((0,), (0,)), ((), ()))
MESH_ID = pl.DeviceIdType.MESH


def _dot(a, b, dims=_NN):
    return lax.dot_general(a, b, dims, preferred_element_type=F32)


def _sig(x):
    return 1.0 / (1.0 + jnp.exp(-x))


def _softplus(x):
    return jnp.maximum(x, 0.0) + jnp.log(1.0 + jnp.exp(-jnp.abs(x)))


def _cparams(*sem):
    return pltpu.CompilerParams(dimension_semantics=sem, vmem_limit_bytes=VMEM_LIMIT)


def _pick(n, cap):
    if n <= cap:
        return n
    best = None
    for t in range(LANE, cap + 1, LANE):
        if n % t == 0:
            best = t
    assert best is not None, (n, cap)
    return best


def _tile(n):
    if n <= 1024 or n % 1024 == 0:
        return min(n, 1024)
    return _pick(n, 1408)


def _rows(n):
    return min(512, n)


def _window(x):
    return x if isinstance(x, tuple) else (x, 0, x.shape[1])


def _into(into, n_in, out_idx):
    buf, col0, width = into
    if buf is None:
        return [], [], {}, col0, width
    return [buf], [pl.BlockSpec(memory_space=pl.ANY)], {n_in: out_idx}, col0, width


def _mm(a, b, mode, name, add=None, add_scale=1.0, out_dtype=F32, dep=None):
    if mode == "nn":
        m, k = a.shape
        n = b.shape[1]
    elif mode == "nt":
        m, k = a.shape
        n = b.shape[0]
    else:
        k, m = a.shape
        n = b.shape[1]
    tm = _tile(m)
    tn = _pick(n, 2176) if mode == "tn" and n > 1024 else _tile(n)
    tk = _pick(k, 2176) if mode == "nt" and a.dtype == BF16 and k > 2816 else _tile(k)
    nk = k // tk
    has_add = add is not None
    dims = {"nn": _NN, "nt": _NT, "tn": _TN}[mode]

    def body(*refs):
        if dep is not None:
            refs = refs[:-3] + refs[-2:]
        if has_add:
            a_ref, b_ref, add_ref, o_ref, acc_ref = refs
        else:
            a_ref, b_ref, o_ref, acc_ref = refs
        kk = pl.program_id(2)

        @pl.when(kk == 0)
        def _():
            if has_add:
                acc_ref[...] = add_scale * add_ref[...].astype(F32)
            else:
                acc_ref[...] = jnp.zeros_like(acc_ref)

        acc_ref[...] += _dot(a_ref[...].astype(BF16), b_ref[...].astype(BF16), dims)

        @pl.when(kk == nk - 1)
        def _():
            o_ref[...] = acc_ref[...].astype(o_ref.dtype)

    if mode == "nn":
        a_spec = pl.BlockSpec((tm, tk), lambda i, j, kk: (i, kk))
        b_spec = pl.BlockSpec((tk, tn), lambda i, j, kk: (kk, j))
    elif mode == "nt":
        a_spec = pl.BlockSpec((tm, tk), lambda i, j, kk: (i, kk))
        b_spec = pl.BlockSpec((tn, tk), lambda i, j, kk: (j, kk))
    else:
        a_spec = pl.BlockSpec((tk, tm), lambda i, j, kk: (kk, i))
        b_spec = pl.BlockSpec((tk, tn), lambda i, j, kk: (kk, j))
    o_spec = pl.BlockSpec((tm, tn), lambda i, j, kk: (i, j))
    in_specs = [a_spec, b_spec] + ([o_spec] if has_add else [])
    args = (a, b) + ((add,) if has_add else ())
    if dep is not None:
        in_specs.append(pl.BlockSpec((8, LANE), lambda i, j, kk: (0, 0)))
        args += (dep,)
    return pl.pallas_call(
        body, name=name, grid=(m // tm, n // tn, nk),
        in_specs=in_specs, out_specs=o_spec,
        out_shape=jax.ShapeDtypeStruct((m, n), out_dtype),
        scratch_shapes=[pltpu.VMEM((tm, tn), F32)],
        compiler_params=_cparams("parallel", "parallel", "arbitrary"),
    )(*args)


def _vec_spec(width):
    return pl.BlockSpec((1, width), lambda i: (0, 0))


def _ln_fwd(a, b, gamma, beta, alpha, name):
    n_rows, dm = a.shape
    has_b = b is not None

    def body(*refs):
        if has_b:
            a_ref, b_ref, g_ref, be_ref, o_ref = refs
            u = alpha * a_ref[...] + b_ref[...]
        else:
            a_ref, g_ref, be_ref, o_ref = refs
            u = a_ref[...]
        mu = jnp.mean(u, axis=-1, keepdims=True)
        d = u - mu
        var = jnp.mean(d * d, axis=-1, keepdims=True)
        o_ref[...] = d * lax.rsqrt(var + LN_EPS) * g_ref[...] + be_ref[...]

    row = pl.BlockSpec((_rows(n_rows),dm), lambda i: (i, 0))
    in_specs = [row] + ([row] if has_b else []) + [_vec_spec(dm), _vec_spec(dm)]
    args = (a,) + ((b,) if has_b else ()) + (gamma.reshape(1, dm), beta.reshape(1, dm))
    return pl.pallas_call(
        body, name=name, grid=(n_rows // _rows(n_rows),), in_specs=in_specs, out_specs=row,
        out_shape=jax.ShapeDtypeStruct((n_rows, dm), F32),
        compiler_params=_cparams("parallel"),
    )(*args)


def _mm_ln(x, w, res, gamma, beta, alpha, name):
    n_rows, k = x.shape
    dm = w.shape[1]
    tm = _rows(n_rows)

    def body(x_ref, w_ref, r_ref, g_ref, be_ref, y_ref, o_ref):
        y = _dot(x_ref[...], w_ref[...])
        y_ref[...] = y
        u = alpha * r_ref[...] + y
        mu = jnp.mean(u, axis=-1, keepdims=True)
        d = u - mu
        var = jnp.mean(d * d, axis=-1, keepdims=True)
        o_ref[...] = d * lax.rsqrt(var + LN_EPS) * g_ref[...] + be_ref[...]

    row = pl.BlockSpec((tm, dm), lambda i: (i, 0))
    return pl.pallas_call(
        body, name=name, grid=(n_rows // tm,),
        in_specs=[pl.BlockSpec((tm, k), lambda i: (i, 0)), pl.BlockSpec((k, dm), lambda i: (0, 0)), row,
                  _vec_spec(dm), _vec_spec(dm)],
        out_specs=(row, row),
        out_shape=(jax.ShapeDtypeStruct((n_rows, dm), F32), jax.ShapeDtypeStruct((n_rows, dm), F32)),
        compiler_params=_cparams("parallel"),
    )(x, w, res, gamma.reshape(1, dm), beta.reshape(1, dm))


def _ln_bwd(a, b, gamma, dy, alpha, name):
    n_rows, dm = a.shape
    has_b = b is not None

    def body(*refs):
        if has_b:
            a_ref, b_ref, g_ref, dy_ref, du_ref, acc_ref = refs
            u = alpha * a_ref[...] + b_ref[...]
        else:
            a_ref, g_ref, dy_ref, du_ref, acc_ref = refs
            u = a_ref[...]

        @pl.when(pl.program_id(0) == 0)
        def _():
            acc_ref[...] = jnp.zeros_like(acc_ref)

        mu = jnp.mean(u, axis=-1, keepdims=True)
        d = u - mu
        var = jnp.mean(d * d, axis=-1, keepdims=True)
        rstd = lax.rsqrt(var + LN_EPS)
        xhat = d * rstd
        dyv = dy_ref[...]
        acc_ref[0:1, :] += jnp.sum(dyv * xhat, axis=0, keepdims=True)
        acc_ref[1:2, :] += jnp.sum(dyv, axis=0, keepdims=True)
        dxh = dyv * g_ref[...]
        m1 = jnp.mean(dxh, axis=-1, keepdims=True)
        m2 = jnp.mean(dxh * xhat, axis=-1, keepdims=True)
        du_ref[...] = rstd * (dxh - m1 - xhat * m2)

    row = pl.BlockSpec((_rows(n_rows),dm), lambda i: (i, 0))
    in_specs = [row] + ([row] if has_b else []) + [_vec_spec(dm), row]
    args = (a,) + ((b,) if has_b else ()) + (gamma.reshape(1, dm), dy)
    return pl.pallas_call(
        body, name=name, grid=(n_rows // _rows(n_rows),), in_specs=in_specs,
        out_specs=(row, pl.BlockSpec((8, dm), lambda i: (0, 0))),
        out_shape=(jax.ShapeDtypeStruct((n_rows, dm), F32), jax.ShapeDtypeStruct((8, dm), F32)),
        compiler_params=_cparams("arbitrary"),
    )(*args)


def _loss_ln_bwd(a, b, gamma, beta, target, alpha, name):
    n_rows, dm = a.shape

    def body(a_ref, b_ref, g_ref, be_ref, t_ref, sse_ref, du_ref, acc_ref):
        @pl.when(pl.program_id(0) == 0)
        def _():
            sse_ref[...] = jnp.zeros_like(sse_ref)
            acc_ref[...] = jnp.zeros_like(acc_ref)

        u = alpha * a_ref[...] + b_ref[...]
        mu = jnp.mean(u, axis=-1, keepdims=True)
        d = u - mu
        var = jnp.mean(d * d, axis=-1, keepdims=True)
        rstd = lax.rsqrt(var + LN_EPS)
        xhat = d * rstd
        err = xhat * g_ref[...] + be_ref[...] - t_ref[...]
        sse_ref[...] += jnp.sum(err * err)
        dyv = err * (1.0 / dm)
        acc_ref[0:1, :] += jnp.sum(dyv * xhat, axis=0, keepdims=True)
        acc_ref[1:2, :] += jnp.sum(dyv, axis=0, keepdims=True)
        dxh = dyv * g_ref[...]
        m1 = jnp.mean(dxh, axis=-1, keepdims=True)
        m2 = jnp.mean(dxh * xhat, axis=-1, keepdims=True)
        du_ref[...] = rstd * (dxh - m1 - xhat * m2)

    row = pl.BlockSpec((_rows(n_rows), dm), lambda i: (i, 0))
    return pl.pallas_call(
        body, name=name, grid=(n_rows // _rows(n_rows),),
        in_specs=[row, row, _vec_spec(dm), _vec_spec(dm), row],
        out_specs=(pl.BlockSpec((8, LANE), lambda i: (0, 0)), row, pl.BlockSpec((8, dm), lambda i: (0, 0))),
        out_shape=(jax.ShapeDtypeStruct((8, LANE), F32), jax.ShapeDtypeStruct((n_rows, dm), F32),
                   jax.ShapeDtypeStruct((8, dm), F32)),
        compiler_params=_cparams("arbitrary"),
    )(a, b, gamma.reshape(1, dm), beta.reshape(1, dm), target)


FFN_ROWS = 512


def _ffn_in(h, wg, wu, name, dep=None):
    m, k = h.shape
    n = wg.shape[0]
    tm, tn = min(FFN_ROWS, m), _tile(n)

    def body(*refs):
        h_ref, wg_ref, wu_ref = refs[:3]
        g_ref, u_ref, act_ref = refs[-3:]
        hb = h_ref[...].astype(BF16)
        g = _dot(hb, wg_ref[...], _NT)
        u = _dot(hb, wu_ref[...], _NT)
        g_ref[...] = g
        u_ref[...] = u
        act_ref[...] = (g * _sig(g) * u).astype(BF16)

    rows = pl.BlockSpec((tm, k), lambda j, i: (i, 0))
    wrow = pl.BlockSpec((tn, k), lambda j, i: (j, 0))
    out = pl.BlockSpec((tm, tn), lambda j, i: (i, j))
    in_specs, args = [rows, wrow, wrow], (h, wg, wu)
    if dep is not None:
        in_specs.append(pl.BlockSpec((8, LANE), lambda j, i: (0, 0)))
        args += (dep,)
    return pl.pallas_call(
        body, name=name, grid=(n // tn, m // tm), in_specs=in_specs, out_specs=(out, out, out),
        out_shape=(jax.ShapeDtypeStruct((m, n), F32), jax.ShapeDtypeStruct((m, n), F32),
                   jax.ShapeDtypeStruct((m, n), BF16)),
        compiler_params=_cparams("parallel", "parallel"),
    )(*args)


def _ffn_dh(dg, du, wg, wu, res, alpha, name):
    m, f = dg.shape
    dm = wg.shape[1]
    tm = min(FFN_ROWS // 2, m)

    def body(dg_ref, du_ref, wg_ref, wu_ref, r_ref, o_ref):
        o_ref[...] = alpha * r_ref[...] + _dot(dg_ref[...], wg_ref[...]) + _dot(du_ref[...], wu_ref[...])

    rows_f = pl.BlockSpec((tm, f), lambda i: (i, 0))
    rows_d = pl.BlockSpec((tm, dm), lambda i: (i, 0))
    whole = pl.BlockSpec((f, dm), lambda i: (0, 0))
    return pl.pallas_call(
        body, name=name, grid=(m // tm,), in_specs=[rows_f, rows_f, whole, whole, rows_d], out_specs=rows_d,
        out_shape=jax.ShapeDtypeStruct((m, dm), F32),
        compiler_params=_cparams("parallel"),
    )(dg, du, wg, wu, res)


def _ffn_dact(dy, wd, g, u, name, dep=None):
    m, k = dy.shape
    n = wd.shape[0]
    tm, tn = min(FFN_ROWS, m), _tile(n)

    def body(*refs):
        dy_ref, wd_ref, g_ref, u_ref = refs[:4]
        dg_ref, du_ref = refs[-2:]
        da = _dot(dy_ref[...].astype(BF16), wd_ref[...], _NT)
        gv = g_ref[...]
        s = _sig(gv)
        dg_ref[...] = (da * u_ref[...] * (s * (1.0 + gv * (1.0 - s)))).astype(BF16)
        du_ref[...] = (da * gv * s).astype(BF16)

    rows = pl.BlockSpec((tm, k), lambda j, i: (i, 0))
    wrow = pl.BlockSpec((tn, k), lambda j, i: (j, 0))
    out = pl.BlockSpec((tm, tn), lambda j, i: (i, j))
    in_specs, args = [rows, wrow, out, out], (dy, wd, g, u)
    if dep is not None:
        in_specs.append(pl.BlockSpec((8, LANE), lambda j, i: (0, 0)))
        args += (dep,)
    return pl.pallas_call(
        body, name=name, grid=(n // tn, m // tm), in_specs=in_specs, out_specs=(out, out),
        out_shape=(jax.ShapeDtypeStruct((m, n), BF16), jax.ShapeDtypeStruct((m, n), BF16)),
        compiler_params=_cparams("parallel", "parallel"),
    )(*args)


def _gate_specs(gl, n_rows, dm):
    arr, g0, _ = _window(gl)
    return arr, [pl.BlockSpec((_rows(n_rows), dm), lambda i, k=k: (i, g0 // dm + k)) for k in range(2)]


def _branch_out(yn, att, w_ssd, w_att, gl, name, dep=None):
    n_rows, dm = yn.shape[0], w_ssd.shape[1]
    gl_arr, gspecs = _gate_specs(gl, n_rows, dm)

    def body(*refs):
        ga_ref, gb_ref, yn_ref, att_ref, ws_ref, wa_ref = refs[:6]
        ya_ref, yb_ref, o_ref = refs[-3:]
        ya = _dot(yn_ref[...], ws_ref[...])
        yb = _dot(att_ref[...], wa_ref[...])
        ya_ref[...] = ya
        yb_ref[...] = yb
        o_ref[...] = (_sig(ga_ref[...]) * ya + _sig(gb_ref[...]) * yb).astype(BF16)

    tm = _rows(n_rows)
    row = pl.BlockSpec((tm, dm), lambda i: (i, 0))
    in_specs = gspecs + [pl.BlockSpec((tm, yn.shape[1]), lambda i: (i, 0)), row,
                         pl.BlockSpec(w_ssd.shape, lambda i: (0, 0)), pl.BlockSpec(w_att.shape, lambda i: (0, 0))]
    args = (gl_arr, gl_arr, yn, att, w_ssd, w_att)
    if dep is not None:
        in_specs.append(pl.BlockSpec((8, LANE), lambda i: (0, 0)))
        args += (dep,)
    return pl.pallas_call(
        body, name=name, grid=(n_rows // tm,), in_specs=in_specs, out_specs=(row, row, row),
        out_shape=(jax.ShapeDtypeStruct((n_rows, dm), F32), jax.ShapeDtypeStruct((n_rows, dm), F32),
                   jax.ShapeDtypeStruct((n_rows, dm), BF16)),
        compiler_params=_cparams("parallel"),
    )(*args)


def _merge_bwd(gl, ya, yb, dmix, w_mix, name, into):
    n_rows, dm = ya.shape
    gl_arr, gspecs = _gate_specs(gl, n_rows, dm)
    extra, extra_specs, aliases, col0, width = _into(into, 6, 2)

    def body(*refs):
        ga_ref, gb_ref, ya_ref, yb_ref, dx_ref, w_ref = refs[:6]
        dya_ref, dyb_ref, dgl_ref = refs[-3:]
        ga = _sig(ga_ref[...])
        gb = _sig(gb_ref[...])
        dmv = _dot(dx_ref[...].astype(BF16), w_ref[...], _NT)
        dya_ref[...] = (dmv * ga).astype(BF16)
        dyb_ref[...] = (dmv * gb).astype(BF16)
        dgl_ref[:, :dm] = (dmv * ya_ref[...] * ga * (1.0 - ga)).astype(BF16)
        dgl_ref[:, dm:] = (dmv * yb_ref[...] * gb * (1.0 - gb)).astype(BF16)

    row = pl.BlockSpec((_rows(n_rows),dm), lambda i: (i, 0))
    row2 = pl.BlockSpec((_rows(n_rows),2 * dm), lambda i: (i, col0 // (2 * dm)))
    return pl.pallas_call(
        body, name=name, grid=(n_rows // _rows(n_rows),),
        in_specs=gspecs + [row, row, row, pl.BlockSpec(w_mix.shape, lambda i: (0, 0))] + extra_specs,
        out_specs=(row, row, row2),
        out_shape=(jax.ShapeDtypeStruct((n_rows, dm), BF16), jax.ShapeDtypeStruct((n_rows, dm), BF16),
                   jax.ShapeDtypeStruct((n_rows, width), BF16)),
        input_output_aliases=aliases,
        compiler_params=_cparams("parallel"),
    )(gl_arr, gl_arr, ya, yb, dmix, w_mix, *extra)


CONV_TAPS = 4
CONV_COLS = 512
HALO = 8


def _shift_down(cur, prev8, s, row8):
    r = pltpu.roll(cur, s, axis=0)
    top = jnp.where(row8 < s, pltpu.roll(prev8, s, axis=0), r[0:HALO])
    return jnp.concatenate([top, r[HALO:]], axis=0)


def _shift_up(cur, next8, s, row8):
    n = cur.shape[0]
    r = pltpu.roll(cur, n - s, axis=0)
    bot = jnp.where(row8 >= HALO - s, pltpu.roll(next8, HALO - s, axis=0), r[n - HALO:])
    return jnp.concatenate([r[:n - HALO], bot], axis=0)


def _conv_pre(u_ref, prev_ref, w_ref, b_ref, li):
    cur = u_ref[...]
    prev8 = jnp.where(li == 0, 0.0, prev_ref[...])
    row8 = lax.broadcasted_iota(jnp.int32, prev8.shape, 0)
    shifted = [cur] + [_shift_down(cur, prev8, s, row8) for s in range(1, CONV_TAPS)]
    acc = b_ref[...] + shifted[0] * w_ref[CONV_TAPS - 1:CONV_TAPS, :]
    for s in range(1, CONV_TAPS):
        acc = acc + shifted[s] * w_ref[CONV_TAPS - 1 - s:CONV_TAPS - s, :]
    return acc, shifted


def _conv_specs(n_rows, tl, col0=0):
    off = col0 // CONV_COLS
    cur = pl.BlockSpec((tl, CONV_COLS), lambda cj, li: (li, cj + off))
    prev = pl.BlockSpec((HALO, CONV_COLS), lambda cj, li: (jnp.maximum(li * (tl // HALO) - 1, 0), cj + off))
    nxt = pl.BlockSpec((HALO, CONV_COLS),
                       lambda cj, li: (jnp.minimum((li + 1) * (tl // HALO), n_rows // HALO - 1), cj + off))
    par = pl.BlockSpec((8, CONV_COLS), lambda cj, li: (0, cj + off))
    return cur, prev, nxt, par


def _conv_fwd(u, w8, b8, name):
    u, u0, c = _window(u)
    n_rows = u.shape[0]
    tl = _rows(n_rows)
    cur, _, _, par = _conv_specs(n_rows, tl)
    ucur, prev, _, _ = _conv_specs(n_rows, tl, u0)

    def body(u_ref, prev_ref, w_ref, b_ref, o_ref):
        acc, _ = _conv_pre(u_ref, prev_ref, w_ref, b_ref[0:1, :], pl.program_id(1))
        o_ref[...] = acc * _sig(acc)

    return pl.pallas_call(
        body, name=name, grid=(c // CONV_COLS, n_rows // tl), in_specs=[ucur, prev, par, par], out_specs=cur,
        out_shape=jax.ShapeDtypeStruct((n_rows, c), F32),
        compiler_params=_cparams("parallel", "parallel"),
    )(u, u, w8, b8)


def _conv_bwd_pre(u, w8, b8, dout, name):
    u, u0, c = _window(u)
    n_rows = u.shape[0]
    tl = _rows(n_rows)
    cur, _, _, par = _conv_specs(n_rows, tl)
    ucur, prev, _, _ = _conv_specs(n_rows, tl, u0)

    def body(u_ref, prev_ref, w_ref, b_ref, do_ref, dc_ref, acc_ref):
        @pl.when(pl.program_id(1) == 0)
        def _():
            acc_ref[...] = jnp.zeros_like(acc_ref)

        acc, shifted = _conv_pre(u_ref, prev_ref, w_ref, b_ref[0:1, :], pl.program_id(1))
        sg = _sig(acc)
        dc = do_ref[...] * (sg * (1.0 + acc * (1.0 - sg)))
        dc_ref[...] = dc
        for k in range(CONV_TAPS):
            acc_ref[k:k + 1, :] += jnp.sum(dc * shifted[CONV_TAPS - 1 - k], axis=0, keepdims=True)
        acc_ref[CONV_TAPS:CONV_TAPS + 1, :] += jnp.sum(dc, axis=0, keepdims=True)

    return pl.pallas_call(
        body, name=name, grid=(c // CONV_COLS, n_rows // tl), in_specs=[ucur, prev, par, par, cur],
        out_specs=(cur, par),
        out_shape=(jax.ShapeDtypeStruct((n_rows, c), F32), jax.ShapeDtypeStruct((8, c), F32)),
        compiler_params=_cparams("parallel", "arbitrary"),
    )(u, u, w8, b8, dout)


def _conv_bwd_in(dc, w8, name, into):
    n_rows, c = dc.shape
    tl = _rows(n_rows)
    cur, _, nxt, par = _conv_specs(n_rows, tl)
    n_l = n_rows // tl
    extra, extra_specs, aliases, col0, width = _into(into, 3, 0)
    out_spec = _conv_specs(n_rows, tl, col0)[0]

    def body(*refs):
        dc_ref, next_ref, w_ref = refs[:3]
        o_ref = refs[-1]
        cur_v = dc_ref[...]
        next8 = jnp.where(pl.program_id(1) == n_l - 1, 0.0, next_ref[...])
        row8 = lax.broadcasted_iota(jnp.int32, next8.shape, 0)
        acc = cur_v * w_ref[CONV_TAPS - 1:CONV_TAPS, :]
        for s in range(1, CONV_TAPS):
            acc = acc + _shift_up(cur_v, next8, s, row8) * w_ref[CONV_TAPS - 1 - s:CONV_TAPS - s, :]
        o_ref[...] = acc.astype(BF16)

    return pl.pallas_call(
        body, name=name, grid=(c // CONV_COLS, n_l), in_specs=[cur, nxt, par] + extra_specs, out_specs=out_spec,
        out_shape=jax.ShapeDtypeStruct((n_rows, width), BF16), input_output_aliases=aliases,
        compiler_params=_cparams("parallel", "parallel"),
    )(dc, dc, w8, *extra)


NORM_GROUP = SSD_D_INNER // SSD_GROUPS


def _gnorm_fwd(y, z, w, name):
    n_rows, c = y.shape
    z, z0, _ = _window(z)
    zoff = z0 // NORM_GROUP

    def body(y_ref, z_ref, w_ref, o_ref):
        zv = z_ref[...]
        yg = y_ref[...] * (zv * _sig(zv))
        r = lax.rsqrt(jnp.mean(yg * yg, axis=-1, keepdims=True) + RMS_EPS)
        o_ref[...] = (yg * r * w_ref[...]).astype(BF16)

    blk = pl.BlockSpec((_rows(n_rows),NORM_GROUP), lambda i, j: (i, j))
    zblk = pl.BlockSpec((_rows(n_rows),NORM_GROUP), lambda i, j: (i, j + zoff))
    wspec = pl.BlockSpec((1, NORM_GROUP), lambda i, j: (0, j))
    return pl.pallas_call(
        body, name=name, grid=(n_rows // _rows(n_rows), c // NORM_GROUP), in_specs=[blk, zblk, wspec], out_specs=blk,
        out_shape=jax.ShapeDtypeStruct((n_rows, c), BF16),
        compiler_params=_cparams("parallel", "parallel"),
    )(y, z, w.reshape(1, c))


def _gnorm_bwd(y, z, w, dyn, name, into):
    n_rows, c = y.shape
    z, z0, _ = _window(z)
    zoff = z0 // NORM_GROUP
    extra, extra_specs, aliases, col0, width = _into(into, 4, 1)
    doff = col0 // NORM_GROUP

    def body(*refs):
        y_ref, z_ref, w_ref, dn_ref = refs[:4]
        dy_ref, dz_ref, acc_ref = refs[-3:]
        @pl.when(pl.program_id(1) == 0)
        def _():
            acc_ref[...] = jnp.zeros_like(acc_ref)

        zv = z_ref[...]
        yv = y_ref[...]
        sz = _sig(zv)
        silu = zv * sz
        yg = yv * silu
        r = lax.rsqrt(jnp.mean(yg * yg, axis=-1, keepdims=True) + RMS_EPS)
        nrm = yg * r
        dn = dn_ref[...]
        acc_ref[0:1, :] += jnp.sum(dn * nrm, axis=0, keepdims=True)
        dnw = dn * w_ref[...]
        dyg = r * (dnw - nrm * jnp.mean(dnw * nrm, axis=-1, keepdims=True))
        dy_ref[...] = dyg * silu
        dz_ref[...] = (dyg * yv * (sz * (1.0 + zv * (1.0 - sz)))).astype(BF16)

    blk = pl.BlockSpec((_rows(n_rows),NORM_GROUP), lambda j, i: (i, j))
    zblk = pl.BlockSpec((_rows(n_rows),NORM_GROUP), lambda j, i: (i, j + zoff))
    wspec = pl.BlockSpec((1, NORM_GROUP), lambda j, i: (0, j))
    aspec = pl.BlockSpec((8, NORM_GROUP), lambda j, i: (0, j))
    return pl.pallas_call(
        body, name=name, grid=(c // NORM_GROUP, n_rows // _rows(n_rows)),
        in_specs=[blk, zblk, wspec, blk] + extra_specs,
        out_specs=(blk, pl.BlockSpec((_rows(n_rows), NORM_GROUP), lambda j, i: (i, j + doff)), aspec),
        out_shape=(jax.ShapeDtypeStruct((n_rows, c), F32), jax.ShapeDtypeStruct((n_rows, width), BF16),
                   jax.ShapeDtypeStruct((8, c), F32)),
        input_output_aliases=aliases,
        compiler_params=_cparams("parallel", "arbitrary"),
    )(y, z, w.reshape(1, c), dyn, *extra)


ATT_SCALE = ATT_HEAD_DIM ** -0.5
ATT_SLOPES = [2.0 ** (-8.0 * (h + 1) / ATT_HEADS) for h in range(ATT_HEADS)]
Q_PER_KV = ATT_HEADS // ATT_KV_HEADS


def _dup_half(t, g, lo):
    tr = pltpu.roll(t, ATT_HEAD_DIM, axis=1)
    return jnp.where(lo, t, tr) if g == 0 else jnp.where(lo, tr, t)


def _att_band(kv_ref, kvp_ref, n):
    cur = kv_ref[...]
    prev = jnp.where(n == 0, 0.0, kvp_ref[...])
    lo = lax.broadcasted_iota(jnp.int32, (ATT_BLOCK, LANE), 1) < ATT_HEAD_DIM
    bands = []
    for g in range(ATT_KV_HEADS):
        kb = jnp.concatenate([_dup_half(prev[:, :LANE], g, lo), _dup_half(cur[:, :LANE], g, lo)], axis=0)
        vb = jnp.concatenate([_dup_half(prev[:, LANE:], g, lo), _dup_half(cur[:, LANE:], g, lo)], axis=0)
        bands.append((kb.astype(BF16), vb.astype(BF16)))
    return bands


def _att_tile(n):
    shape = (2 * ATT_BLOCK, ATT_BLOCK)
    row = lax.broadcasted_iota(jnp.int32, shape, 0)
    i = row & (ATT_BLOCK - 1)
    s = lax.broadcasted_iota(jnp.int32, shape, 1)
    upper = s > i
    dist = ((i - s) & (ATT_BLOCK - 1)).astype(F32)
    dead = upper & (n == 0)
    return upper, dist, dead, row[:, 0:1] < ATT_BLOCK


def _stack_pair(t, lo):
    return jnp.concatenate([jnp.where(lo, t, 0.0), jnp.where(lo, 0.0, t)], axis=0).astype(BF16)


def _att_exp(qs, kb, s_ref, j, tile):
    upper, dist, dead, first = tile
    s2 = _dot(qs, kb, _NT)
    slope = jnp.where(first, ATT_SLOPES[2 * j], ATT_SLOPES[2 * j + 1])
    sink = jnp.where(first, s_ref[0:1, 2 * j:2 * j + 1], s_ref[0:1, 2 * j + 1:2 * j + 2])
    s = jnp.where(upper, s2[:, :ATT_BLOCK], s2[:, ATT_BLOCK:]) - slope * dist
    s = jnp.where(dead, NEG, s)
    m = jnp.maximum(jnp.max(s, axis=-1, keepdims=True), sink)
    return jnp.exp(s - m), jnp.exp(sink - m)


def _band_split(t, upper):
    return jnp.concatenate([jnp.where(upper, t, 0.0), jnp.where(upper, 0.0, t)], axis=1)


def _att_fwd(q, kv, sinks8, name):
    q, q0, _ = _window(q)
    kv, kv0, _ = _window(kv)
    qoff, kvoff = q0 // Q_DIM, kv0 // (2 * LANE)
    n_rows = q.shape[0]
    nb = n_rows // ATT_BLOCK

    def body(q_ref, kv_ref, kvp_ref, s_ref, o_ref, o32_ref):
        n = pl.program_id(0)
        bands = _att_band(kv_ref, kvp_ref, n)
        lo = lax.broadcasted_iota(jnp.int32, (ATT_BLOCK, LANE), 1) < ATT_HEAD_DIM
        tile = _att_tile(n)
        ones_b = jnp.ones((2 * ATT_BLOCK, LANE), BF16)
        for j in range(ATT_HEADS // 2):
            kb, vb = bands[2 * j // Q_PER_KV]
            qs = _stack_pair(q_ref[:, j * LANE:(j + 1) * LANE] * ATT_SCALE, lo)
            p, es = _att_exp(qs, kb, s_ref, j, tile)
            pv = _dot(_band_split(p, tile[0]).astype(BF16), jnp.concatenate([vb, ones_b], axis=1))
            out = pv[:, :LANE] / (pv[:, LANE:] + es)
            out = jnp.where(lo, out[:ATT_BLOCK], out[ATT_BLOCK:])
            o_ref[:, j * LANE:(j + 1) * LANE] = out.astype(BF16)
            o32_ref[:, j * LANE:(j + 1) * LANE] = out

    return pl.pallas_call(
        body, name=name, grid=(nb,),
        in_specs=[pl.BlockSpec((ATT_BLOCK, Q_DIM), lambda n: (n, qoff)),
                  pl.BlockSpec((ATT_BLOCK, 2 * LANE), lambda n: (n, kvoff)),
                  pl.BlockSpec((ATT_BLOCK, 2 * LANE), lambda n: (jnp.maximum(n - 1, 0), kvoff)),
                  pl.BlockSpec((8, LANE), lambda n: (0, 0))],
        out_specs=(pl.BlockSpec((ATT_BLOCK, Q_DIM), lambda n: (n, 0)),) * 2,
        out_shape=(jax.ShapeDtypeStruct((n_rows, Q_DIM), BF16), jax.ShapeDtypeStruct((n_rows, Q_DIM), F32)),
        compiler_params=_cparams("parallel"),
    )(q, kv, kv, sinks8)


def _att_bwd(q, kv, sinks8, out32, dout, name, into):
    q, q0, _ = _window(q)
    kv, kv0, _ = _window(kv)
    qoff, kvoff = q0 // Q_DIM, kv0 // (2 * LANE)
    n_rows = q.shape[0]
    nb = n_rows // ATT_BLOCK

    extra, extra_specs, aliases, col0, width = _into(into, 6, 0)
    dqoff = col0 // Q_DIM

    def body(*refs):
        q_ref, kv_ref, kvp_ref, s_ref, o_ref, do_ref = refs[:6]
        dq_ref, dkv_ref, acc_ref, carry_ref = refs[-4:]
        n = pl.program_id(0)

        @pl.when(n == 0)
        def _():
            acc_ref[...] = jnp.zeros_like(acc_ref)
            carry_ref[...] = jnp.zeros_like(carry_ref)

        @pl.when(n == nb)
        def _():
            dkv_ref[...] = carry_ref[...].astype(BF16)

        @pl.when(n < nb)
        def _():
            bands = _att_band(kv_ref, kvp_ref, n)
            lo = lax.broadcasted_iota(jnp.int32, (ATT_BLOCK, LANE), 1) < ATT_HEAD_DIM
            lane1 = lax.broadcasted_iota(jnp.int32, (1, LANE), 1)
            tile = _att_tile(n)
            upper, first = tile[0], tile[3]
            ones_b = jnp.ones((ATT_BLOCK, LANE), BF16)
            ones2_b = jnp.ones((2 * LANE, LANE), BF16)
            dk_acc = [jnp.zeros((2 * ATT_BLOCK, LANE), F32) for _ in range(ATT_KV_HEADS)]
            dv_acc = [jnp.zeros((2 * ATT_BLOCK, LANE), F32) for _ in range(ATT_KV_HEADS)]
            dsink = jnp.zeros((1, LANE), F32)
            for j in range(ATT_HEADS // 2):
                g = 2 * j // Q_PER_KV
                kb, vb = bands[g]
                qs = _stack_pair(q_ref[:, j * LANE:(j + 1) * LANE] * ATT_SCALE, lo)
                dop = do_ref[:, j * LANE:(j + 1) * LANE].astype(F32)
                dos = _stack_pair(dop, lo)
                pu, es = _att_exp(qs, kb, s_ref, j, tile)
                inv = 1.0 / (_dot(pu.astype(BF16), ones_b) + es)
                p = pu * inv
                od = dos.astype(F32) * jnp.concatenate([o_ref[:, j * LANE:(j + 1) * LANE]] * 2, axis=0)
                od_hi = od.astype(BF16)
                delta = _dot(jnp.concatenate([od_hi, (od - od_hi.astype(F32)).astype(BF16)], axis=1), ones2_b)
                dp2 = _dot(dos, vb, _NT)
                dp = jnp.where(upper, dp2[:, :ATT_BLOCK], dp2[:, ATT_BLOCK:])
                ds2 = _band_split(p * (dp - delta), upper)
                psd = jnp.sum(es * inv * delta, axis=0, keepdims=True)
                psd0 = jnp.sum(jnp.where(first, es * inv * delta, 0.0), axis=0, keepdims=True)
                dsink = jnp.where(lane1 == 2 * j, -psd0, jnp.where(lane1 == 2 * j + 1, psd0 - psd, dsink))
                ds2_b = ds2.astype(BF16)
                dq = _dot(ds2_b, kb) * ATT_SCALE
                dq_ref[:, j * LANE:(j + 1) * LANE] = jnp.where(lo, dq[:ATT_BLOCK], dq[ATT_BLOCK:]).astype(BF16)
                dk_acc[g] = dk_acc[g] + _dot(ds2_b, qs, _TN)
                dv_acc[g] = dv_acc[g] + _dot(_band_split(p, upper).astype(BF16), dos, _TN)
            acc_ref[0:1, :] += dsink
            lo2 = lax.broadcasted_iota(jnp.int32, (2 * ATT_BLOCK, LANE), 1) < ATT_HEAD_DIM
            folded = []
            for acc in (dk_acc, dv_acc):
                t0 = acc[0] + pltpu.roll(acc[0], ATT_HEAD_DIM, axis=1)
                t1 = acc[1] + pltpu.roll(acc[1], ATT_HEAD_DIM, axis=1)
                folded.append(jnp.where(lo2, t0, t1))
            band = jnp.concatenate(folded, axis=1)
            dkv_ref[...] = (carry_ref[...] + band[:ATT_BLOCK]).astype(BF16)
            carry_ref[...] = band[ATT_BLOCK:]

    def qmap(n):
        return (jnp.minimum(n, nb - 1), 0)

    return pl.pallas_call(
        body, name=name, grid=(nb + 1,),
        in_specs=[pl.BlockSpec((ATT_BLOCK, Q_DIM), lambda n: (jnp.minimum(n, nb - 1), qoff)),
                  pl.BlockSpec((ATT_BLOCK, 2 * LANE), lambda n: (jnp.minimum(n, nb - 1), kvoff)),
                  pl.BlockSpec((ATT_BLOCK, 2 * LANE),
                               lambda n: (jnp.maximum(jnp.minimum(n, nb - 1) - 1, 0), kvoff)),
                  pl.BlockSpec((8, LANE), lambda n: (0, 0)),
                  pl.BlockSpec((ATT_BLOCK, Q_DIM), qmap),
                  pl.BlockSpec((ATT_BLOCK, Q_DIM), qmap)] + extra_specs,
        out_specs=(pl.BlockSpec((ATT_BLOCK, Q_DIM), lambda n: (jnp.minimum(n, nb - 1), dqoff)),
                   pl.BlockSpec((ATT_BLOCK, 2 * LANE), lambda n: (jnp.maximum(n - 1, 0), 0)),
                   pl.BlockSpec((8, LANE), lambda n: (0, 0))),
        out_shape=(jax.ShapeDtypeStruct((n_rows, width), BF16), jax.ShapeDtypeStruct((n_rows, 2 * LANE), BF16),
                   jax.ShapeDtypeStruct((8, LANE), F32)),
        input_output_aliases=aliases,
        scratch_shapes=[pltpu.VMEM((ATT_BLOCK, 2 * LANE), F32)],
        compiler_params=_cparams("arbitrary"),
    )(q, kv, kv, sinks8, out32, dout, *extra)


HEADS_PER_GROUP = SSD_HEADS // SSD_GROUPS
PAIRS_PER_GROUP = HEADS_PER_GROUP // 2
T = SSD_CHUNK


def _cumsum_mm(mat, x):
    hi = x.astype(BF16)
    r = x - hi.astype(F32)
    mid = r.astype(BF16)
    lo = (r - mid.astype(F32)).astype(BF16)
    w = x.shape[1]
    out = _dot(mat, jnp.concatenate([hi, mid, lo], axis=1))
    return out[:, :w] + out[:, w:2 * w] + out[:, 2 * w:]


def _ssd_prep(dtr_ref, par_ref):
    dt = _softplus(dtr_ref[...] + par_ref[0:1, :])
    a = -jnp.exp(par_ref[1:2, :])
    ri = lax.broadcasted_iota(jnp.int32, (T, T), 0)
    ci = lax.broadcasted_iota(jnp.int32, (T, T), 1)
    cs = _cumsum_mm((ri >= ci).astype(BF16), dt * a)
    lo = lax.broadcasted_iota(jnp.int32, (T, LANE), 1) < SSD_CHUNK // 2

    def expand(arr):
        rows = arr.shape[0]
        return jnp.concatenate([jnp.where(lo[:rows], arr[:, 2 * j:2 * j + 1], arr[:, 2 * j + 1:2 * j + 2])
                                for j in range(PAIRS_PER_GROUP)], axis=1)

    tot = cs[T - 1:T, :]
    return {"dt": dt, "a": a, "cs": cs, "cst": cs.T, "lo": lo, "ri": ri, "ci": ci, "expand": expand,
            "dt_x": expand(dt), "ecs_x": expand(jnp.exp(cs)), "dec_x": expand(jnp.exp(tot - cs)),
            "et_x": expand(jnp.exp(tot)), "etot": jnp.exp(tot), "dsk_x": expand(par_ref[2:3, :])}


def _wide_masks():
    r = lax.broadcasted_iota(jnp.int32, (T, 2 * T), 0)
    l = lax.broadcasted_iota(jnp.int32, (T, 2 * T), 1)
    s = l & (T - 1)
    return r >= s, s >= r, l < T


def _wide_cs(q, k0, even):
    cs, cst = q["cs"], q["cst"]
    col = jnp.where(even, cs[:, k0:k0 + 1], cs[:, k0 + 1:k0 + 2])
    row = jnp.concatenate([cst[k0:k0 + 1, :], cst[k0 + 1:k0 + 2, :]], axis=1)
    return col, row


def _ssd_fwd(xs, bm, cm, dtr, par, name):
    dtr, dt0, _ = _window(dtr)
    dtoff = dt0 // LANE
    n_rows = xs.shape[0]
    nc = n_rows // T
    gw = PAIRS_PER_GROUP * LANE

    def body(x_ref, b_ref, c_ref, dtr_ref, par_ref, y_ref, hs_ref, h_ref):
        @pl.when(pl.program_id(1) == 0)
        def _():
            h_ref[...] = jnp.zeros_like(h_ref)

        q = _ssd_prep(dtr_ref, par_ref)
        lo = q["lo"]
        tri_w, _, even = _wide_masks()
        bg_b = b_ref[...].astype(BF16)
        cg_b = c_ref[...].astype(BF16)
        xv = x_ref[...]
        xdt = xv * q["dt_x"]
        h = h_ref[...]
        hs_ref[0, 0] = h
        yo = q["ecs_x"] * _dot(cg_b, h.astype(BF16))
        h_ref[...] = h * q["et_x"] + _dot(bg_b, (xdt * q["dec_x"]).astype(BF16), _TN)
        cb = _dot(cg_b, bg_b, _NT)
        cb_w = jnp.concatenate([cb, cb], axis=1)
        for j in range(PAIRS_PER_GROUP):
            col, row = _wide_cs(q, 2 * j, even)
            m_w = (jnp.exp(jnp.where(tri_w, col - row, NEG)) * cb_w).astype(BF16)
            sl = slice(j * LANE, (j + 1) * LANE)
            y_ref[:, sl] = (_dot(m_w, _stack_pair(xdt[:, sl], lo)) + yo[:, sl] + q["dsk_x"][:, sl] * xv[:, sl])

    return pl.pallas_call(
        body, name=name, grid=(SSD_GROUPS, nc),
        in_specs=[pl.BlockSpec((T, gw), lambda g, c: (c, g)),
                  pl.BlockSpec((T, SSD_STATE), lambda g, c: (c, g)),
                  pl.BlockSpec((T, SSD_STATE), lambda g, c: (c, g)),
                  pl.BlockSpec((T, LANE), lambda g, c: (c, g + dtoff)),
                  pl.BlockSpec((8, LANE), lambda g, c: (0, g))],
        out_specs=(pl.BlockSpec((T, gw), lambda g, c: (c, g)),
                   pl.BlockSpec((1, 1, SSD_STATE, gw), lambda g, c: (g, c, 0, 0))),
        out_shape=(jax.ShapeDtypeStruct((n_rows, SSD_D_INNER), F32),
                   jax.ShapeDtypeStruct((SSD_GROUPS, nc, SSD_STATE, gw), F32)),
        scratch_shapes=[pltpu.VMEM((SSD_STATE, gw), F32)],
        compiler_params=_cparams("parallel", "arbitrary"),
    )(xs, bm, cm, dtr, par)


def _ssd_bwd(xs, bm, cm, dtr, par, hs, dy, name, into):
    dtr, dt0, _ = _window(dtr)
    dtoff = dt0 // LANE
    n_rows = xs.shape[0]
    nc = n_rows // T
    gw = PAIRS_PER_GROUP * LANE
    extra, extra_specs, aliases, col0, width = _into(into, 7, 3)
    ddoff = col0 // LANE

    def body(*refs):
        x_ref, b_ref, c_ref, dtr_ref, par_ref, hs_ref, dy_ref = refs[:7]
        dx_ref, db_ref, dc_ref, ddtr_ref, acc_ref, dh_ref = refs[-6:]

        @pl.when(pl.program_id(1) == 0)
        def _():
            dh_ref[...] = jnp.zeros_like(dh_ref)
            acc_ref[...] = jnp.zeros_like(acc_ref)

        q = _ssd_prep(dtr_ref, par_ref)
        lo, dt, a = q["lo"], q["dt"], q["a"]
        tri_w, trit_w, even = _wide_masks()
        lane = lax.broadcasted_iota(jnp.int32, (T, LANE), 1)
        lane1 = lane[0:1, :]
        last_row = lax.broadcasted_iota(jnp.int32, (T, 1), 0) == T - 1
        bg_b = b_ref[...].astype(BF16)
        cg_b = c_ref[...].astype(BF16)
        xv = x_ref[...]
        dyv = dy_ref[...]
        xdt = xv * q["dt_x"]
        h = hs_ref[0, 0]
        dhn = dh_ref[...]
        h_b, dhn_b = h.astype(BF16), dhn.astype(BF16)
        yo = q["ecs_x"] * _dot(cg_b, h_b)
        bdh = q["dec_x"] * _dot(bg_b, dhn_b)
        dye = (dyv * q["ecs_x"]).astype(BF16)
        xd = (xdt * q["dec_x"]).astype(BF16)
        dcg = _dot(dye, h_b, _NT)
        dbg = _dot(xd, dhn_b, _NT)
        dh_ref[...] = dhn * q["et_x"] + _dot(cg_b, dye, _TN)
        e4_all = xdt * bdh
        f_all = dyv * yo - e4_all
        tot_row = jnp.sum(e4_all, axis=0, keepdims=True) + q["et_x"] * jnp.sum(h * dhn, axis=0, keepdims=True)
        dsk_row = jnp.sum(dyv * xv, axis=0, keepdims=True)
        cb = _dot(cg_b, bg_b, _NT)
        cbt = _dot(bg_b, cg_b, _NT)
        cb_w = jnp.concatenate([cb, cb], axis=1)
        cbt_w = jnp.concatenate([cbt, cbt], axis=1)
        dcb = jnp.zeros((T, T), F32)
        dcbt = jnp.zeros((T, T), F32)
        dcs_acc = jnp.zeros((T, LANE), F32)
        ddt_acc = jnp.zeros((T, LANE), F32)
        dsk_acc = jnp.zeros((1, LANE), F32)
        tot_acc = jnp.zeros((1, LANE), F32)
        ind_r = lax.broadcasted_iota(jnp.int32, (2 * T, LANE), 0)
        ind_l = lax.broadcasted_iota(jnp.int32, (2 * T, LANE), 1)

        def halves(t):
            return (jnp.sum(jnp.where(lo[0:1], t, 0.0), axis=-1, keepdims=True),
                    jnp.sum(jnp.where(lo[0:1], 0.0, t), axis=-1, keepdims=True))

        def split2(t):
            hi = t.astype(BF16)
            return jnp.concatenate([hi, (t - hi.astype(F32)).astype(BF16)], axis=1)

        for j in range(PAIRS_PER_GROUP):
            k0, k1 = 2 * j, 2 * j + 1
            sl = slice(j * LANE, (j + 1) * LANE)
            col, row = _wide_cs(q, k0, even)
            lm_w = jnp.exp(jnp.where(tri_w, col - row, NEG))
            lmt_w = jnp.exp(jnp.where(trit_w, row - col, NEG))
            dyp, xp = dyv[:, sl], xdt[:, sl]
            dym, xm = _stack_pair(dyp, lo), _stack_pair(xp, lo)
            dm_w = _dot(dyp.astype(BF16), xm, _NT)
            dmt_w = _dot(xp.astype(BF16), dym, _NT)
            mm_w = lm_w * cb_w
            mmt_w = lmt_w * cbt_w
            dxdt = _dot(mmt_w.astype(BF16), dym) + bdh[:, sl]
            g1 = dm_w * lm_w
            g2 = dmt_w * lmt_w
            dcb = dcb + g1[:, :T] + g1[:, T:]
            dcbt = dcbt + g2[:, :T] + g2[:, T:]
            ind_w = jnp.where(ind_l == jnp.where(ind_r < T, k0, k1), 1.0, 0.0).astype(BF16)
            ind_p = jnp.where(ind_l[:T] == jnp.where(ind_r[:T] < SSD_CHUNK // 2, k0, k1), 1.0, 0.0).astype(BF16)
            dcs_acc = dcs_acc + _dot(
                jnp.concatenate([split2(dm_w * mm_w - dmt_w * mmt_w), split2(f_all[:, sl])], axis=1),
                jnp.concatenate([ind_w, ind_w, ind_p, ind_p], axis=0))
            ddt_acc = ddt_acc + _dot(split2(dxdt * xv[:, sl]), jnp.concatenate([ind_p, ind_p], axis=0))
            tot2 = halves(tot_row[:, sl])
            tot_acc = jnp.where(lane1 == k0, tot2[0], jnp.where(lane1 == k1, tot2[1], tot_acc))
            dsk2 = halves(dsk_row[:, sl])
            dsk_acc = jnp.where(lane1 == k0, dsk2[0], jnp.where(lane1 == k1, dsk2[1], dsk_acc))
            dx_ref[:, sl] = dxdt * q["dt_x"][:, sl] + q["dsk_x"][:, sl] * dyp
        dcs_acc = dcs_acc + jnp.where(last_row, tot_acc, 0.0)
        dc_ref[...] = dcg + _dot(dcb.astype(BF16), bg_b)
        db_ref[...] = dbg + _dot(dcbt.astype(BF16), cg_b)
        dda = _cumsum_mm((q["ci"] >= q["ri"]).astype(BF16), dcs_acc)
        ddt = ddt_acc + dda * a
        ddtr = ddt * _sig(dtr_ref[...] + par_ref[0:1, :])
        ddtr_ref[...] = ddtr.astype(BF16)
        acc_ref[0:1, :] += jnp.sum(ddtr, axis=0, keepdims=True)
        acc_ref[1:2, :] += jnp.sum(dda * dt, axis=0, keepdims=True) * a
        acc_ref[2:3, :] += dsk_acc

    def rev(g, c):
        return (nc - 1 - c, g)

    return pl.pallas_call(
        body, name=name, grid=(SSD_GROUPS, nc),
        in_specs=[pl.BlockSpec((T, gw), rev),
                  pl.BlockSpec((T, SSD_STATE), rev),
                  pl.BlockSpec((T, SSD_STATE), rev),
                  pl.BlockSpec((T, LANE), lambda g, c: (nc - 1 - c, g + dtoff)),
                  pl.BlockSpec((8, LANE), lambda g, c: (0, g)),
                  pl.BlockSpec((1, 1, SSD_STATE, gw), lambda g, c: (g, nc - 1 - c, 0, 0)),
                  pl.BlockSpec((T, gw), rev)] + extra_specs,
        out_specs=(pl.BlockSpec((T, gw), rev),
                   pl.BlockSpec((T, SSD_STATE), rev),
                   pl.BlockSpec((T, SSD_STATE), rev),
                   pl.BlockSpec((T, LANE), lambda g, c: (nc - 1 - c, g + ddoff)),
                   pl.BlockSpec((8, LANE), lambda g, c: (0, g))),
        out_shape=(jax.ShapeDtypeStruct((n_rows, SSD_D_INNER), F32),
                   jax.ShapeDtypeStruct((n_rows, BC_DIM), F32),
                   jax.ShapeDtypeStruct((n_rows, BC_DIM), F32),
                   jax.ShapeDtypeStruct((n_rows, width), BF16),
                   jax.ShapeDtypeStruct((8, DT_PAD), F32)),
        input_output_aliases=aliases,
        scratch_shapes=[pltpu.VMEM((SSD_STATE, gw), F32)],
        compiler_params=_cparams("parallel", "arbitrary"),
    )(xs, bm, cm, dtr, par, hs, dy, *extra)


ADAM_ROWS = 256


def _adamw(lands, w, m, v, name):
    na = len(lands)
    n_slots, r, wd = lands[0].shape
    tr = r if r <= 2 * ADAM_ROWS else ADAM_ROWS
    nj = r // tr
    bc1 = 1.0 - ADAM_B1 ** ADAM_STEP
    bc2 = 1.0 - ADAM_B2 ** ADAM_STEP

    def body(*refs):
        l_refs = refs[:na]
        w_ref, m_ref, v_ref, g_ref, d_ref, nm_ref, nv_ref = refs[na:]
        for a in range(na):
            @pl.when(pl.program_id(0) == a)
            def _(l_ref=l_refs[a]):
                g = l_ref[0].astype(F32)
                for s in range(1, n_slots):
                    g = g + l_ref[s].astype(F32)
                mn = ADAM_B1 * m_ref[0] + (1.0 - ADAM_B1) * g
                vn = ADAM_B2 * v_ref[0] + (1.0 - ADAM_B2) * (g * g)
                mh = mn / bc1
                vh = vn / bc2
                g_ref[0] = g
                nm_ref[0] = mn
                nv_ref[0] = vn
                d_ref[0] = -ADAM_LR * (mh / (jnp.sqrt(vh) + ADAM_EPS) + ADAM_WD * w_ref[0])

    def land_spec(a):
        return pl.BlockSpec((n_slots, tr, wd),
                            lambda i, j: (0, jnp.where(i == a, j, jnp.where(i < a, 0, nj - 1)), 0))

    blk = pl.BlockSpec((1, tr, wd), lambda i, j: (i, j, 0))
    shp = jax.ShapeDtypeStruct((na, r, wd), F32)
    return pl.pallas_call(
        body, name=name, grid=(na, nj), in_specs=[land_spec(a) for a in range(na)] + [blk, blk, blk],
        out_specs=(blk, blk, blk, blk), out_shape=(shp, shp, shp, shp),
        compiler_params=_cparams("arbitrary", "arbitrary"),
    )(*lands, w, m, v)


def _mesh_pos():
    return lax.axis_index("x"), lax.axis_index("y"), lax.axis_index("c")


def _peer(pos, k):
    x, y, c = pos
    px = 1 - x if (k >> 2) & 1 else x
    py = 1 - y if (k >> 1) & 1 else y
    pc = 1 - c if k & 1 else c
    return px, py, pc


def _flat(pos):
    return 4 * pos[0] + 2 * pos[1] + pos[2]


HBM_SPEC = pl.BlockSpec(memory_space=pl.ANY)


ROW_SHARDED = ("w_ssd_out", "w_att_out", "w_mix_out", "w_ffn_down")
COL_SHARDED = ("w_in", "w_ffn_gate", "w_ffn_up")
GATHERED = ROW_SHARDED + COL_SHARDED + ("conv_w",)


SEM_SPEC = pl.BlockSpec(memory_space=pltpu.SEMAPHORE)
TOKEN = jax.ShapeDtypeStruct((8, LANE), F32)
SPLIT_EFFECT = pltpu.SideEffectType.DATAFLOW_SIDE_EFFECTING
GATHER_ROWS = "gather_rows"
GATHER_SLOT = "gather_slot"
SCATTER_ROWS = "scatter_rows"
SCATTER_SLOT = "scatter_slot"


def _land_shape(kind, src):
    if kind == GATHER_ROWS:
        return (N_DEV * src.shape[0],) + src.shape[1:]
    if kind == GATHER_SLOT:
        return (N_DEV,) + src.shape
    if kind == SCATTER_ROWS:
        return (N_DEV, src.shape[0] // N_DEV) + src.shape[1:]
    return src.shape


def _views(kind, src_ref, land_ref, pos, k):
    me = _flat(pos)
    if kind == GATHER_ROWS:
        r = src_ref.shape[0]
        return src_ref, land_ref.at[pl.ds(pl.multiple_of(me * r, 16), r), :]
    if kind == GATHER_SLOT:
        return src_ref, land_ref.at[me]
    dev = _flat(_peer(pos, k))
    if kind == SCATTER_ROWS:
        r = land_ref.shape[1]
        return src_ref.at[pl.ds(pl.multiple_of(dev * r, 16), r), :], land_ref.at[k]
    return src_ref.at[dev], land_ref.at[k]


def _hbm(x):
    return pltpu.with_memory_space_constraint(x, pltpu.HBM)


def _exchange_start(items, after, name):
    kinds = [k for k, _ in items]
    srcs = [_hbm(s) for _, s in items]
    lands = [_hbm(lax.empty(_land_shape(k, s), s.dtype)) for k, s in items]
    n = len(items)
    n_copy = n * (N_DEV - 1)

    def body(*refs):
        src_refs, land_refs = refs[:n], refs[n:2 * n]
        send_sems, recv_sems = refs[2 * n + 1], refs[2 * n + 2]
        token_ref = refs[4 * n + 3]
        pos = _mesh_pos()
        for i, kind in enumerate(kinds):
            for k in range(1, N_DEV):
                s, d = _views(kind, src_refs[i], land_refs[i], pos, k)
                j = i * (N_DEV - 1) + k - 1
                pltpu.make_async_remote_copy(src_ref=s, dst_ref=d, send_sem=send_sems.at[j], recv_sem=recv_sems.at[j],
                                             device_id=_peer(pos, k), device_id_type=MESH_ID).start()
        token_ref[...] = jnp.zeros_like(token_ref)

    arrs = srcs + lands
    outs = pl.pallas_call(
        body, name=name,
        in_specs=[HBM_SPEC] * (2 * n + 1),
        out_specs=[SEM_SPEC, SEM_SPEC] + [HBM_SPEC] * (2 * n) + [pl.BlockSpec(memory_space=pltpu.VMEM)],
        out_shape=[pltpu.SemaphoreType.DMA((n_copy,)), pltpu.SemaphoreType.DMA((n_copy,))]
        + [pltpu.HBM(a.shape, a.dtype) for a in arrs] + [TOKEN],
        input_output_aliases={i: 2 + i for i in range(2 * n)},
        compiler_params=pltpu.CompilerParams(has_side_effects=SPLIT_EFFECT),
    )(*arrs, after)
    return {"kinds": kinds, "send": outs[0], "recv": outs[1], "arrs": outs[2:2 + 2 * n], "token": outs[-1]}


def _exchange_wait(ex, after, name):
    kinds = ex["kinds"]
    n = len(kinds)

    def body(*refs):
        src_refs, land_refs = refs[:n], refs[n:2 * n]
        send_sems, recv_sems = refs[2 * n], refs[2 * n + 1]
        token_ref = refs[-1]
        pos = _mesh_pos()
        for i, kind in enumerate(kinds):
            for k in range(1, N_DEV):
                s, d = _views(kind, src_refs[i], land_refs[i], pos, k)
                j = i * (N_DEV - 1) + k - 1
                cp = pltpu.make_async_remote_copy(src_ref=s, dst_ref=d, send_sem=send_sems.at[j],
                                                  recv_sem=recv_sems.at[j], device_id=_peer(pos, k),
                                                  device_id_type=MESH_ID)
                cp.wait_send()
                cp.wait_recv()
        token_ref[...] = jnp.zeros_like(token_ref)

    outs = pl.pallas_call(
        body, name=name,
        in_specs=[HBM_SPEC] * (2 * n) + [SEM_SPEC, SEM_SPEC, HBM_SPEC],
        out_specs=[HBM_SPEC] * (2 * n) + [pl.BlockSpec(memory_space=pltpu.VMEM)],
        out_shape=[pltpu.HBM(a.shape, a.dtype) for a in ex["arrs"]] + [TOKEN],
        input_output_aliases={i: i for i in range(2 * n)},
        compiler_params=pltpu.CompilerParams(has_side_effects=SPLIT_EFFECT),
    )(*ex["arrs"], ex["send"], ex["recv"], after)
    lands = [_place_own(k, s, d) for k, s, d in zip(kinds, outs[:n], outs[n:2 * n])]
    return lands, outs[-1]


def _place_own(kind, src, land):
    me = _flat(_mesh_pos())
    zeros = (0,) * (src.ndim - 1)
    if kind == GATHER_ROWS:
        return lax.dynamic_update_slice(land, src, (me * src.shape[0],) + zeros)
    if kind == GATHER_SLOT:
        return lax.dynamic_update_slice(land, src[None], (me,) + (0,) * src.ndim)
    if kind == SCATTER_ROWS:
        r = land.shape[1]
        own = lax.dynamic_slice(src, (me * r,) + zeros, (r,) + src.shape[1:])
    else:
        own = lax.dynamic_index_in_dim(src, me, 0, keepdims=False)
    return lax.dynamic_update_slice(land, own[None], (0,) * land.ndim)


def _all_gather_small(x, name):
    r, w = x.shape

    def body(x_ref, out_ref, send_sems, recv_sems):
        pos = _mesh_pos()
        me = _flat(pos)
        copies = []
        for k in range(1, N_DEV):
            cp = pltpu.make_async_remote_copy(
                src_ref=x_ref, dst_ref=out_ref.at[me], send_sem=send_sems.at[k - 1], recv_sem=recv_sems.at[k - 1],
                device_id=_peer(pos, k), device_id_type=MESH_ID)
            cp.start()
            copies.append(cp)
        out_ref[me] = x_ref[...]
        for cp in copies:
            cp.wait()

    vmem = pl.BlockSpec(memory_space=pltpu.VMEM)
    return pl.pallas_call(
        body, name=name, in_specs=[vmem], out_specs=vmem,
        out_shape=jax.ShapeDtypeStruct((N_DEV, r, w), x.dtype),
        scratch_shapes=[pltpu.SemaphoreType.DMA((N_DEV - 1,)), pltpu.SemaphoreType.DMA((N_DEV - 1,))],
        compiler_params=pltpu.CompilerParams(has_side_effects=True),
    )(x)


def _cols(g, lo, hi):
    c = g.shape[-1]
    parts = []
    for d in range(N_DEV):
        a, b = max(lo, d * c), min(hi, (d + 1) * c)
        if a < b:
            parts.append(g[d, :, a - d * c:b - d * c])
    return parts[0] if len(parts) == 1 else jnp.concatenate(parts, axis=1)


def _col_chunks(g):
    c = g.shape[-1] // N_DEV
    return jnp.stack([g[:, d * c:(d + 1) * c] for d in range(N_DEV)])


IN_PART = ("w_in", "conv_w")
OUT_PART = ROW_SHARDED + ("w_ffn_gate", "w_ffn_up")
TRANSPOSED = ("w_ffn_gate", "w_ffn_up")


def _gather_items(w, names, l):
    items = []
    for n in names:
        blk = w[n][l] if n == "conv_w" else w[n][l].astype(BF16)
        if n in TRANSPOSED:
            blk = blk.T
        items.append((GATHER_ROWS if n in ROW_SHARDED + TRANSPOSED else GATHER_SLOT, blk))
    return items


def _scatter_items(grads, names):
    def chunked(g):
        return g if g.ndim == 3 else _col_chunks(g)

    return [(SCATTER_ROWS, grads[n]) if n in ROW_SHARDED + TRANSPOSED else (SCATTER_SLOT, chunked(grads[n]))
            for n in names]


SMALL = ("ln_in_g", "ln_in_b", "conv_b", "dt_bias", "a_log", "d_skip", "ssd_norm_w", "att_sinks",
         "ln_mix_g", "ln_mix_b", "ln_ffn_g", "ln_ffn_b")


def _pack_small(vals):
    flat = jnp.concatenate([vals[n].reshape(-1) for n in SMALL])
    n = flat.shape[0]
    rows = -(-n // LANE)
    rows = -(-rows // 8) * 8
    return jnp.pad(flat, (0, rows * LANE - n)).reshape(rows, LANE)


def _unpack_small(buf, shapes):
    flat = buf.reshape(-1)
    off = 0
    out = {}
    for n in SMALL:
        cnt = math.prod(shapes[n])
        out[n] = flat[off:off + cnt].reshape(shapes[n])
        off += cnt
    return out


def _to_group_major(v):
    lead = v.shape[:-1]
    t = v.reshape(lead + (SSD_GROUPS, HEADS_PER_GROUP))
    t = jnp.pad(t, [(0, 0)] * len(lead) + [(0, 0), (0, LANE - HEADS_PER_GROUP)])
    return t.reshape(lead + (DT_PAD,))


def _from_group_major(v):
    lead = v.shape[:-1]
    return v.reshape(lead + (SSD_GROUPS, LANE))[..., :HEADS_PER_GROUP].reshape(lead + (SSD_HEADS,))


def _rows8(v):
    return jnp.pad(v, ((0, 8 - v.shape[0]), (0, 0)))


IN_OFFS = {"q": (0, 1024), "kv": (1024, 1280), "z": (1280, 3328), "xs": (3328, 5376), "b": (5376, 5888),
           "c": (5888, 6400), "dt": (6400, 6432), "gl": (6432, 8480)}
PIECES = ("q", "kv", "z", "xs", "b", "c", "dt", "gl")


CAT = ("z", "xs", "gl", "q", "b", "c", "dt", "kv")
CAT_WIDTH = {"q": 1024, "z": 2048, "xs": 2048, "gl": 2048, "b": 512, "c": 512, "kv": 256, "dt": DT_PAD}
CAT_OFF = {p: sum(CAT_WIDTH[q] for q in CAT[:i]) for i, p in enumerate(CAT)}
CAT_DIM = sum(CAT_WIDTH.values())
MAIN_DIM = CAT_OFF["kv"]


def _cat_w_in(g):
    pieces = {p: _cols(g, lo, hi) for p, (lo, hi) in IN_OFFS.items()}
    pieces["dt"] = _to_group_major(pieces["dt"])
    return jnp.concatenate([pieces[p] for p in CAT], axis=1)


def _dw_in_chunks(dw_main, dw_kv):
    dt = _from_group_major(dw_main[:, CAT_OFF["dt"]:CAT_OFF["dt"] + DT_PAD])
    shard = IN_OFFS[PIECES[-1]][1] // N_DEV

    def piece(pc, a, b):
        if pc == "dt":
            return dt[:, a:b]
        if pc == "kv":
            return dw_kv[:, a:b]
        return dw_main[:, CAT_OFF[pc] + a:CAT_OFF[pc] + b]

    chunks = []
    for d in range(N_DEV):
        parts = []
        for pc in PIECES:
            lo, hi = IN_OFFS[pc]
            a, b = max(lo, d * shard), min(hi, (d + 1) * shard)
            if a < b:
                parts.append(piece(pc, a - lo, b - lo))
        chunks.append(parts[0] if len(parts) == 1 else jnp.concatenate(parts, axis=1))
    return jnp.stack(chunks)


def _params_out(W):
    return {n: W[n] for n in OUT_PART}


def _params_in(l, W, sm):
    p = {"w_cat": _cat_w_in(W["w_in"])}
    cw = _cols(W["conv_w"], 0, SSD_D_INNER + 2 * BC_DIM)
    cb = sm["conv_b"][l]
    segs = {"xs": (0, 2048), "b": (2048, 2560), "c": (2560, 3072)}
    p["conv_w8"] = {s: _rows8(cw[:, lo:hi]) for s, (lo, hi) in segs.items()}
    p["conv_b8"] = {s: _rows8(cb[None, lo:hi]) for s, (lo, hi) in segs.items()}
    p["ssd_par"] = _rows8(jnp.stack([_to_group_major(sm["dt_bias"][l]), _to_group_major(sm["a_log"][l]),
                                     _to_group_major(sm["d_skip"][l])]))
    p["norm_w"] = sm["ssd_norm_w"][l]
    p["sinks8"] = _rows8(jnp.pad(sm["att_sinks"][l], (0, LANE - ATT_HEADS))[None])
    for n in ("ln_mix_g", "ln_mix_b", "ln_ffn_g", "ln_ffn_b"):
        p[n] = sm[n][l]
    return p


def _fwd_mixers(h0, p, l, dep=None):
    tag = f"l{l}_"
    a = {"h0": h0}
    proj = _mm(h0, p["w_cat"], "nn", tag + "proj", dep=dep)
    for pc in CAT:
        a[pc] = (proj, CAT_OFF[pc], CAT_WIDTH[pc])
    for s in ("xs", "b", "c"):
        a[s + "c"] = _conv_fwd(a[s], p["conv_w8"][s], p["conv_b8"][s], tag + "conv_" + s)
    a["y"], a["hs"] = _ssd_fwd(a["xsc"], a["bc"], a["cc"], a["dt"], p["ssd_par"], tag + "ssd_fwd")
    a["yn"] = _gnorm_fwd(a["y"], a["z"], p["norm_w"], tag + "gnorm")
    a["att"], a["att32"] = _att_fwd(a["q"], a["kv"], p["sinks8"], tag + "att_fwd")
    return a


def _fwd_out(a, p, l, dep=None):
    tag = f"l{l}_"
    h0 = a["h0"]
    a["ya"], a["yb"], a["merged"] = _branch_out(a["yn"], a["att"], p["w_ssd_out"], p["w_att_out"], a["gl"],
                                                tag + "branch_out", dep=dep)
    a["mix"], a["h1"] = _mm_ln(a["merged"], p["w_mix_out"], h0, p["ln_mix_g"], p["ln_mix_b"], ALPHA,
                               tag + "mix_out_ln")
    a["fg"], a["fu"], a["act"] = _ffn_in(a["h1"], p["w_ffn_gate"], p["w_ffn_up"], tag + "ffn_in")
    a["ffn"], a["h2"] = _mm_ln(a["act"], p["w_ffn_down"], a["h1"], p["ln_ffn_g"], p["ln_ffn_b"], ALPHA,
                               tag + "ffn_down_ln")
    return a


def _dw(x, dy, name, dep=None):
    return _mm(x, dy, "tn", name, out_dtype=BF16, dep=dep)


def _bwd_out(a, p, dh2, l, dep=None, target=None):
    tag = f"l{l}_b_"
    gw, gs = {}, {}
    if target is not None:
        gs["sse"], du2, acc = _loss_ln_bwd(a["h1"], a["ffn"], p["ln_ffn_g"], p["ln_ffn_b"], target, ALPHA,
                                           tag + "loss_ln_ffn")
    else:
        du2, acc = _ln_bwd(a["h1"], a["ffn"], p["ln_ffn_g"], dh2, ALPHA, tag + "ln_ffn")
    gs["ln_ffn_g"], gs["ln_ffn_b"] = acc[0], acc[1]
    gw["w_ffn_down"] = _dw(a["act"], du2, tag + "dw_down", dep=dep)
    dfg, dfu = _ffn_dact(du2, p["w_ffn_down"], a["fg"], a["fu"], tag + "ffn_dact", dep=dep)
    gw["w_ffn_gate"] = _dw(dfg, a["h1"], tag + "dw_gate")
    gw["w_ffn_up"] = _dw(dfu, a["h1"], tag + "dw_up")
    dh1 = _ffn_dh(dfg, dfu, p["w_ffn_gate"], p["w_ffn_up"], du2, ALPHA, tag + "dh1")
    du1, acc = _ln_bwd(a["h0"], a["mix"], p["ln_mix_g"], dh1, ALPHA, tag + "ln_mix")
    gs["ln_mix_g"], gs["ln_mix_b"] = acc[0], acc[1]
    gw["w_mix_out"] = _dw(a["merged"], du1, tag + "dw_mix")
    dya, dyb, dproj = _merge_bwd(a["gl"], a["ya"], a["yb"], du1, p["w_mix_out"], tag + "merge",
                                 (None, CAT_OFF["gl"], MAIN_DIM))
    gw["w_ssd_out"] = _dw(a["yn"], dya, tag + "dw_ssd")
    gw["w_att_out"] = _dw(a["att"], dyb, tag + "dw_att")
    return {"du1": du1, "dya": dya, "dyb": dyb, "dproj": dproj}, gw, gs


def _bwd_mixers(a, p, carry, l, dep=None):
    tag = f"l{l}_b_"
    gs = {}
    du1, dproj = carry["du1"], carry["dproj"]

    def win(pc):
        return (dproj, CAT_OFF[pc], MAIN_DIM)

    dyn = _mm(carry["dya"], p["w_ssd_out"], "nt", tag + "dyn", dep=dep)
    datt = _mm(carry["dyb"], p["w_att_out"], "nt", tag + "datt", out_dtype=BF16, dep=dep)
    dproj, dkv, acc = _att_bwd(a["q"], a["kv"], p["sinks8"], a["att32"], datt, tag + "att", win("q"))
    gs["att_sinks"] = acc[0, :ATT_HEADS]
    dy, dproj, acc = _gnorm_bwd(a["y"], a["z"], p["norm_w"], dyn, tag + "gnorm", win("z"))
    gs["ssd_norm_w"] = acc[0]
    dxs, dbm, dcm, dproj, acc = _ssd_bwd(a["xsc"], a["bc"], a["cc"], a["dt"], p["ssd_par"], a["hs"], dy,
                                         tag + "ssd", win("dt"))
    gs["dt_bias"], gs["a_log"], gs["d_skip"] = (_from_group_major(acc[i]) for i in range(3))
    dconv_w, dconv_b = [], []
    for s, dout in (("xs", dxs), ("b", dbm), ("c", dcm)):
        dc, acc = _conv_bwd_pre(a[s], p["conv_w8"][s], p["conv_b8"][s], dout, tag + "conv_pre_" + s)
        dconv_w.append(acc[:CONV_TAPS])
        dconv_b.append(acc[CONV_TAPS])
        dproj = _conv_bwd_in(dc, p["conv_w8"][s], tag + "conv_in_" + s, win(s))
    gconv = jnp.concatenate(dconv_w, axis=1)
    gs["conv_b"] = jnp.concatenate(dconv_b)
    w_main, w_kv = p["w_cat"][:, :MAIN_DIM], p["w_cat"][:, MAIN_DIM:]
    dw_main, dw_kv = _dw(a["h0"], dproj, tag + "dw_in"), _dw(a["h0"], dkv, tag + "dw_in_kv")

    def grad_h0(dep=None):
        dh0 = _mm(dproj, w_main, "nt", tag + "dh0", add=du1, add_scale=ALPHA, dep=dep)
        return _mm(dkv, w_kv, "nt", tag + "dh0_kv", add=dh0)

    return grad_h0, _dw_in_chunks(dw_main, dw_kv), gconv, gs


def _step(x, target, w, m, v):
    x2 = x[0]
    t2 = target[0]
    tok = jnp.zeros(TOKEN.shape, TOKEN.dtype)

    ex = _exchange_start(_gather_items(w, IN_PART, 0), tok, "gather_l0_in_start")
    h = _ln_fwd(x2, None, w["ln_in_g"], w["ln_in_b"], 1.0, "ln_in")
    lands, tok = _exchange_wait(ex, h, "gather_l0_in_wait")
    p0 = _params_in(0, dict(zip(IN_PART, lands)), w)
    ex = _exchange_start(_gather_items(w, OUT_PART, 0) + _gather_items(w, IN_PART, 1), tok,
                         "gather_l0_out_l1_in_start")
    a0 = _fwd_mixers(h, p0, 0, dep=ex["token"])
    lands, tok = _exchange_wait(ex, a0["att"], "gather_l0_out_l1_in_wait")
    p0.update(_params_out(dict(zip(OUT_PART, lands))))
    p1 = _params_in(1, dict(zip(IN_PART, lands[len(OUT_PART):])), w)
    ex = _exchange_start(_gather_items(w, OUT_PART, 1), tok, "gather_l1_out_start")
    a0 = _fwd_out(a0, p0, 0, dep=ex["token"])
    lands, tok = _exchange_wait(ex, a0["h2"], "gather_l1_out_wait")
    p1.update(_params_out(dict(zip(OUT_PART, lands))))
    a1 = _fwd_out(_fwd_mixers(a0["h2"], p1, 1), p1, 1)

    carry, gw1, gs1 = _bwd_out(a1, p1, None, 1, target=t2)
    loss = lax.psum(0.5 / D_MODEL * gs1.pop("sse")[0, 0], ("x", "y", "c"))
    grad_h0, gw1["w_in"], gw1["conv_w"], gs = _bwd_mixers(a1, p1, carry, 1)
    dh = grad_h0()
    gs1.update(gs)
    ex1 = _exchange_start(_scatter_items(gw1, GATHERED), tok, "scatter_l1_start")
    carry, gw0, gs0 = _bwd_out(a0, p0, dh, 0, dep=ex1["token"])
    lands, tok = _exchange_wait(ex1, carry["dyb"], "scatter_l1_wait")
    land1 = dict(zip(GATHERED, lands))
    ex0 = _exchange_start(_scatter_items(gw0, OUT_PART), tok, "scatter_l0_out_start")
    grad_h0, gw0["w_in"], gw0["conv_w"], gs = _bwd_mixers(a0, p0, carry, 0, dep=ex0["token"])
    gs0.update(gs)
    lands, tok = _exchange_wait(ex0, gw0["w_in"], "scatter_l0_out_wait")
    land0 = dict(zip(OUT_PART, lands))
    ex0 = _exchange_start(_scatter_items(gw0, IN_PART), tok, "scatter_l0_in_start")
    dh = grad_h0(dep=ex0["token"])
    grad_x2, acc = _ln_bwd(x2, None, w["ln_in_g"], dh, 1.0, "ln_in_b")

    outs = [{} for _ in range(4)]

    def update(names):
        res = None
        for n in names:
            if n in TRANSPOSED:
                res = _adamw([land0[n], land1[n]], *(jnp.swapaxes(t, 1, 2) for t in (w[n], m[n], v[n])),
                             "adamw_" + n)
                res = tuple(jnp.swapaxes(t, 1, 2) for t in res)
            else:
                res = _adamw([land0[n], land1[n]], w[n], m[n], v[n], "adamw_" + n)
            for o, t in zip(outs, res):
                o[n] = t
        return res[1]

    update(OUT_PART)
    gsm = {"ln_in_g": acc[0], "ln_in_b": acc[1]}
    for n in SMALL[2:]:
        gsm[n] = jnp.stack([gs0[n], gs1[n]])
    small_shapes = {n: w[n].shape for n in SMALL}
    land_s = _all_gather_small(_pack_small(gsm), "small_grads_all_gather")
    res = _adamw([land_s], _pack_small(w)[None], _pack_small(m)[None], _pack_small(v)[None], "adamw_small")
    for o, t in zip(outs, res):
        o.update(_unpack_small(t[0], small_shapes))
    lands, _ = _exchange_wait(ex0, res[1], "scatter_l0_in_wait")
    land0.update(zip(IN_PART, lands))
    update(IN_PART)
    return loss, grad_x2[None], outs


WEIGHT_NAMES = ("ln_in_g", "ln_in_b", "w_in", "conv_w", "conv_b", "dt_bias", "a_log", "d_skip", "ssd_norm_w",
                "att_sinks", "w_ssd_out", "w_att_out", "w_mix_out", "ln_mix_g", "ln_mix_b", "w_ffn_gate",
                "w_ffn_up", "w_ffn_down", "ln_ffn_g", "ln_ffn_b")


def kernel(x, ln_in_g, ln_in_b, w_in, conv_w, conv_b, dt_bias, a_log, d_skip, ssd_norm_w, att_sinks, w_ssd_out, w_att_out, w_mix_out, ln_mix_g, ln_mix_b, w_ffn_gate, w_ffn_up, w_ffn_down, ln_ffn_g, ln_ffn_b, loss_target, m_ln_in_g, m_ln_in_b, m_w_in, m_conv_w, m_conv_b, m_dt_bias, m_a_log, m_d_skip, m_ssd_norm_w, m_att_sinks, m_w_ssd_out, m_w_att_out, m_w_mix_out, m_ln_mix_g, m_ln_mix_b, m_w_ffn_gate, m_w_ffn_up, m_w_ffn_down, m_ln_ffn_g, m_ln_ffn_b, v_ln_in_g, v_ln_in_b, v_w_in, v_conv_w, v_conv_b, v_dt_bias, v_a_log, v_d_skip, v_ssd_norm_w, v_att_sinks, v_w_ssd_out, v_w_att_out, v_w_mix_out, v_ln_mix_g, v_ln_mix_b, v_w_ffn_gate, v_w_ffn_up, v_w_ffn_down, v_ln_ffn_g, v_ln_ffn_b):
    w = dict(zip(WEIGHT_NAMES, (ln_in_g, ln_in_b, w_in, conv_w, conv_b, dt_bias, a_log, d_skip, ssd_norm_w,
                                att_sinks, w_ssd_out, w_att_out, w_mix_out, ln_mix_g, ln_mix_b, w_ffn_gate,
                                w_ffn_up, w_ffn_down, ln_ffn_g, ln_ffn_b)))
    m = dict(zip(WEIGHT_NAMES, (m_ln_in_g, m_ln_in_b, m_w_in, m_conv_w, m_conv_b, m_dt_bias, m_a_log, m_d_skip,
                                m_ssd_norm_w, m_att_sinks, m_w_ssd_out, m_w_att_out, m_w_mix_out, m_ln_mix_g,
                                m_ln_mix_b, m_w_ffn_gate, m_w_ffn_up, m_w_ffn_down, m_ln_ffn_g, m_ln_ffn_b)))
    v = dict(zip(WEIGHT_NAMES, (v_ln_in_g, v_ln_in_b, v_w_in, v_conv_w, v_conv_b, v_dt_bias, v_a_log, v_d_skip,
                                v_ssd_norm_w, v_att_sinks, v_w_ssd_out, v_w_att_out, v_w_mix_out, v_ln_mix_g,
                                v_ln_mix_b, v_w_ffn_gate, v_w_ffn_up, v_w_ffn_down, v_ln_ffn_g, v_ln_ffn_b)))
    loss, grad_x, outs = _step(x, loss_target, w, m, v)
    result = [loss, grad_x]
    for o in outs:
        result.extend(o[n] for n in WEIGHT_NAMES)
    return tuple(result)
```
